```python
import math
import jax, jax.numpy as jnp
from jax import lax
import numpy as np

D_MODEL = 1024
BATCH = 16
SEQ = 2048
DEPTH = 1

HEAD_DIM = 64
RET_HEADS = 8
ATTN_HEADS = 8
ATTN_KV_HEADS = 2
ATTN_GROUP = ATTN_HEADS // ATTN_KV_HEADS
RET_WIDTH = RET_HEADS * HEAD_DIM
ATTN_WIDTH = ATTN_HEADS * HEAD_DIM
KV_WIDTH = ATTN_KV_HEADS * HEAD_DIM
MIX_WIDTH = RET_WIDTH + ATTN_WIDTH
IN_PROJ_WIDTH = 4 * RET_WIDTH + ATTN_WIDTH + 2 * KV_WIDTH
CHUNK = 128
WINDOW = 128
BLOCK = 128
PLE_DIM = 256
FFN_HIDDEN = ((8 * D_MODEL + 767) // 768) * 256
ALPHA = (2.0 * DEPTH) ** 0.25
BETA = (8.0 * DEPTH) ** -0.25
LN_EPS = 1e-5
GN_EPS = 1e-5
NEG_INF = -1e30

kernel_name = "hybrid_retention_swa_deepnorm_encoder"


def _layer_norm(x, gain, bias):
    xf = x.astype(jnp.float32)
    mu = jnp.mean(xf, axis=-1, keepdims=True)
    var = jnp.mean(jnp.square(xf - mu), axis=-1, keepdims=True)
    return (xf - mu) * lax.rsqrt(var + LN_EPS) * gain.astype(jnp.float32) + bias.astype(jnp.float32)


def _retention_one_direction(q, k, v, log_gamma):
    b, h, s, d = q.shape
    n = s // CHUNK
    q = q.reshape(b, h, n, CHUNK, d)
    k = k.reshape(b, h, n, CHUNK, d)
    v = v.reshape(b, h, n, CHUNK, d)
    idx = jnp.arange(CHUNK, dtype=jnp.float32)
    diff = idx[:, None] - idx[None, :]
    lg = log_gamma[:, None, None]
    decay_in = jnp.where(diff >= 0, jnp.exp(lg * jnp.maximum(diff, 0.0)), 0.0)
    k_decay = jnp.exp(log_gamma[:, None] * (CHUNK - 1.0 - idx)[None, :])
    q_decay = jnp.exp(log_gamma[:, None] * (idx + 1.0)[None, :])
    chunk_decay = jnp.exp(log_gamma * CHUNK)
    scores = jnp.einsum('bhncd,bhnsd->bhncs', q, k) * decay_in[None, :, None]
    y_inner = jnp.einsum('bhncs,bhnse->bhnce', scores, v)
    kv = jnp.einsum('bhnsd,bhnse->bhnde', k * k_decay[None, :, None, :, None], v)

    def step(state, kv_n):
        return state * chunk_decay[None, :, None, None] + kv_n, state

    _, r_prev = lax.scan(step, jnp.zeros((b, h, d, d), q.dtype), jnp.moveaxis(kv, 2, 0))
    r_prev = jnp.moveaxis(r_prev, 0, 2)
    y_cross = jnp.einsum('bhncd,bhnde->bhnce', q * q_decay[None, :, None, :, None], r_prev)
    return (y_inner + y_cross).reshape(b, h, s, d)


def _bidirectional_retention(q, k, v, log_gamma_fwd, log_gamma_bwd, gn_gain, gate):
    b, s, _ = q.shape
    to_heads = lambda t: jnp.transpose(t.astype(jnp.float32).reshape(b, s, RET_HEADS, HEAD_DIM), (0, 2, 1, 3))
    qh, kh, vh = to_heads(q), to_heads(k) * (HEAD_DIM ** -0.5), to_heads(v)
    y_f = _retention_one_direction(qh, kh, vh, log_gamma_fwd)
    flip = lambda t: jnp.flip(t, axis=2)
    y_b = flip(_retention_one_direction(flip(qh), flip(kh), flip(vh), log_gamma_bwd))
    y = y_f + y_b
    mu = jnp.mean(y, axis=-1, keepdims=True)
    var = jnp.mean(jnp.square(y - mu), axis=-1, keepdims=True)
    y = (y - mu) * lax.rsqrt(var + GN_EPS)
    y = jnp.transpose(y, (0, 2, 1, 3)).reshape(b, s, RET_WIDTH) * gn_gain.astype(jnp.float32)
    return y * jax.nn.silu(gate.astype(jnp.float32))


def _alibi_slopes(n_heads):
    return 2.0 ** (-8.0 * (jnp.arange(n_heads, dtype=jnp.float32) + 1.0) / n_heads)


def _windowed_gqa(q, k, v, sink):
    b, s, _ = q.shape
    n = s // BLOCK
    qb = q.astype(jnp.float32).reshape(b, n, BLOCK, ATTN_KV_HEADS, ATTN_GROUP, HEAD_DIM)

    def neighbour_blocks(t):
        tp = jnp.pad(t.astype(jnp.float32).reshape(b, s, ATTN_KV_HEADS, HEAD_DIM),
                     ((0, 0), (BLOCK, BLOCK), (0, 0), (0, 0)))
        tp = tp.reshape(b, n + 2, BLOCK, ATTN_KV_HEADS, HEAD_DIM)
        return jnp.concatenate([tp[:, :-2], tp[:, 1:-1], tp[:, 2:]], axis=2)

    kb, vb = neighbour_blocks(k), neighbour_blocks(v)
    scores = jnp.einsum('bnqhgd,bnshd->bnhgqs', qb, kb) * (HEAD_DIM ** -0.5)
    qi = jnp.arange(BLOCK)
    kj = jnp.arange(3 * BLOCK)
    dist = jnp.abs(kj[None, :] - BLOCK - qi[:, None])
    key_pos = jnp.arange(n)[:, None] * BLOCK - BLOCK + kj[None, :]
    valid = (dist <= WINDOW)[None] & ((key_pos >= 0) & (key_pos < s))[:, None, :]
    slopes = _alibi_slopes(ATTN_HEADS).reshape(ATTN_KV_HEADS, ATTN_GROUP)
    alibi = -slopes[:, :, None, None] * dist.astype(jnp.float32)[None, None]
    scores = jnp.where(valid[None, :, None, None], scores + alibi[None, None], NEG_INF)
    sink_l = sink.astype(jnp.float32).reshape(ATTN_KV_HEADS, ATTN_GROUP)[None, None, :, :, None, None]
    m = jnp.maximum(jnp.max(scores, axis=-1, keepdims=True), sink_l)
    e = jnp.exp(scores - m)
    denom = jnp.sum(e, axis=-1, keepdims=True) + jnp.exp(sink_l - m)
    probs = e / denom
    out = jnp.einsum('bnhgqs,bnshd->bnqhgd', probs, vb)
    return out.reshape(b, s, ATTN_WIDTH)


def _fwd_setup_inputs(seed: int = 0) -> dict:
    key = jax.random.key(seed)
    ks = jax.random.split(key, 18)
    nrm = lambda k, shape, scale: jax.random.normal(k, shape, jnp.float32) * scale
    base_log2 = -5.0 - jnp.arange(RET_HEADS, dtype=jnp.float32)
    return {
        "x": nrm(ks[0], (BATCH, SEQ, D_MODEL), 1.0),
        "p": nrm(ks[1], (DEPTH, BATCH, SEQ, PLE_DIM), 1.0),
        "w_in": nrm(ks[2], (DEPTH, D_MODEL, IN_PROJ_WIDTH), D_MODEL ** -0.5),
        "ret_decay_fwd": base_log2[None] + nrm(ks[3], (DEPTH, RET_HEADS), 0.1),
        "ret_decay_bwd": base_log2[None] + nrm(ks[4], (DEPTH, RET_HEADS), 0.1),
        "ret_gn_gain": 1.0 + nrm(ks[5], (DEPTH, RET_WIDTH), 0.02),
        "attn_sink": nrm(ks[6], (DEPTH, ATTN_HEADS), 0.5),
        "w_out": nrm(ks[7], (DEPTH, MIX_WIDTH, D_MODEL), BETA * MIX_WIDTH ** -0.5),
        "ln1_gain": 1.0 + nrm(ks[8], (DEPTH, D_MODEL), 0.02),
        "ln1_bias": nrm(ks[9], (DEPTH, D_MODEL), 0.02),
        "w_ffn_gate": nrm(ks[10], (DEPTH, D_MODEL, FFN_HIDDEN), D_MODEL ** -0.5),
        "w_ffn_up": nrm(ks[11], (DEPTH, D_MODEL, FFN_HIDDEN), D_MODEL ** -0.5),
        "w_ffn_down": nrm(ks[12], (DEPTH, FFN_HIDDEN, D_MODEL), BETA * FFN_HIDDEN ** -0.5),
        "w_ple_proj": nrm(ks[13], (DEPTH, PLE_DIM, D_MODEL), BETA * PLE_DIM ** -0.5),
        "w_ple_gate": nrm(ks[14], (DEPTH, D_MODEL, D_MODEL), D_MODEL ** -0.5),
        "ln2_gain": 1.0 + nrm(ks[15], (DEPTH, D_MODEL), 0.02),
        "ln2_bias": nrm(ks[16], (DEPTH, D_MODEL), 0.02),
    }


def _fwd_reference(x, p, w_in, ret_decay_fwd, ret_decay_bwd, ret_gn_gain, attn_sink, w_out,
              ln1_gain, ln1_bias, w_ffn_gate, w_ffn_up, w_ffn_down, w_ple_proj, w_ple_gate,
              ln2_gain, ln2_bias):
    split_points = [RET_WIDTH, 2 * RET_WIDTH, 3 * RET_WIDTH, 4 * RET_WIDTH,
                    4 * RET_WIDTH + ATTN_WIDTH, 4 * RET_WIDTH + ATTN_WIDTH + KV_WIDTH]
    h = x.astype(jnp.float32)
    for i in range(DEPTH):
        u = jnp.einsum('bsd,de->bse', h, w_in[i].astype(jnp.float32))
        rq, rk, rv, rg, aq, ak, av = jnp.split(u, split_points, axis=-1)
        lg_f = jnp.log1p(-jnp.exp2(ret_decay_fwd[i].astype(jnp.float32)))
        lg_b = jnp.log1p(-jnp.exp2(ret_decay_bwd[i].astype(jnp.float32)))
        y_ret = _bidirectional_retention(rq, rk, rv, lg_f, lg_b, ret_gn_gain[i], rg)
        y_att = _windowed_gqa(aq, ak, av, attn_sink[i])
        mix = jnp.einsum('bse,ed->bsd', jnp.concatenate([y_ret, y_att], axis=-1),
                         w_out[i].astype(jnp.float32))
        h = _layer_norm(ALPHA * h + mix, ln1_gain[i], ln1_bias[i])
        g = jnp.einsum('bsd,df->bsf', h, w_ffn_gate[i].astype(jnp.float32))
        up = jnp.einsum('bsd,df->bsf', h, w_ffn_up[i].astype(jnp.float32))
        ffn = jnp.einsum('bsf,fd->bsd', jax.nn.silu(g) * up, w_ffn_down[i].astype(jnp.float32))
        ple = jnp.einsum('bsr,rd->bsd', p[i].astype(jnp.float32), w_ple_proj[i].astype(jnp.float32))
        ple_gate = jax.nn.sigmoid(jnp.einsum('bsd,de->bse', h, w_ple_gate[i].astype(jnp.float32)))
        h = _layer_norm(ALPHA * h + ffn + ple_gate * ple, ln2_gain[i], ln2_bias[i])
    return h.astype(x.dtype)


import jax as _jax
import jax.numpy as _jnp

TWIN_FORMAT = 'train_step'
FWD_PARAMS = ['x', 'p', 'w_in', 'ret_decay_fwd', 'ret_decay_bwd', 'ret_gn_gain', 'attn_sink', 'w_out', 'ln1_gain', 'ln1_bias', 'w_ffn_gate', 'w_ffn_up', 'w_ffn_down', 'w_ple_proj', 'w_ple_gate', 'ln2_gain', 'ln2_bias']
TWIN_WEIGHTS = ['w_in', 'ret_decay_fwd', 'ret_decay_bwd', 'ret_gn_gain', 'attn_sink', 'w_out', 'ln1_gain', 'ln1_bias', 'w_ffn_gate', 'w_ffn_up', 'w_ffn_down', 'w_ple_proj', 'w_ple_gate', 'ln2_gain', 'ln2_bias']
TWIN_DIFF_INPUT = 'x'
TWIN_INPUTS = ['x', 'p', 'w_in', 'ret_decay_fwd', 'ret_decay_bwd', 'ret_gn_gain', 'attn_sink', 'w_out', 'ln1_gain', 'ln1_bias', 'w_ffn_gate', 'w_ffn_up', 'w_ffn_down', 'w_ple_proj', 'w_ple_gate', 'ln2_gain', 'ln2_bias', 'loss_target', 'm_w_in', 'm_ret_decay_fwd', 'm_ret_decay_bwd', 'm_ret_gn_gain', 'm_attn_sink', 'm_w_out', 'm_ln1_gain', 'm_ln1_bias', 'm_w_ffn_gate', 'm_w_ffn_up', 'm_w_ffn_down', 'm_w_ple_proj', 'm_w_ple_gate', 'm_ln2_gain', 'm_ln2_bias', 'v_w_in', 'v_ret_decay_fwd', 'v_ret_decay_bwd', 'v_ret_gn_gain', 'v_attn_sink', 'v_w_out', 'v_ln1_gain', 'v_ln1_bias', 'v_w_ffn_gate', 'v_w_ffn_up', 'v_w_ffn_down', 'v_w_ple_proj', 'v_w_ple_gate', 'v_ln2_gain', 'v_ln2_bias']
TWIN_OUTPUTS = ['loss', 'grad_x', 'grad_w_in', 'grad_ret_decay_fwd', 'grad_ret_decay_bwd', 'grad_ret_gn_gain', 'grad_attn_sink', 'grad_w_out', 'grad_ln1_gain', 'grad_ln1_bias', 'grad_w_ffn_gate', 'grad_w_ffn_up', 'grad_w_ffn_down', 'grad_w_ple_proj', 'grad_w_ple_gate', 'grad_ln2_gain', 'grad_ln2_bias', 'delta_w_in', 'delta_ret_decay_fwd', 'delta_ret_decay_bwd', 'delta_ret_gn_gain', 'delta_attn_sink', 'delta_w_out', 'delta_ln1_gain', 'delta_ln1_bias', 'delta_w_ffn_gate', 'delta_w_ffn_up', 'delta_w_ffn_down', 'delta_w_ple_proj', 'delta_w_ple_gate', 'delta_ln2_gain', 'delta_ln2_bias', 'new_m_w_in', 'new_m_ret_decay_fwd', 'new_m_ret_decay_bwd', 'new_m_ret_gn_gain', 'new_m_attn_sink', 'new_m_w_out', 'new_m_ln1_gain', 'new_m_ln1_bias', 'new_m_w_ffn_gate', 'new_m_w_ffn_up', 'new_m_w_ffn_down', 'new_m_w_ple_proj', 'new_m_w_ple_gate', 'new_m_ln2_gain', 'new_m_ln2_bias', 'new_v_w_in', 'new_v_ret_decay_fwd', 'new_v_ret_decay_bwd', 'new_v_ret_gn_gain', 'new_v_attn_sink', 'new_v_w_out', 'new_v_ln1_gain', 'new_v_ln1_bias', 'new_v_w_ffn_gate', 'new_v_w_ffn_up', 'new_v_w_ffn_down', 'new_v_w_ple_proj', 'new_v_w_ple_gate', 'new_v_ln2_gain', 'new_v_ln2_bias']
TWIN_LEAF_KINDS = {'loss': 'loss', 'grad_x': 'grad_x', 'grad_w_in': 'grad_w', 'grad_ret_decay_fwd': 'grad_w', 'grad_ret_decay_bwd': 'grad_w', 'grad_ret_gn_gain': 'grad_w', 'grad_attn_sink': 'grad_w', 'grad_w_out': 'grad_w', 'grad_ln1_gain': 'grad_w', 'grad_ln1_bias': 'grad_w', 'grad_w_ffn_gate': 'grad_w', 'grad_w_ffn_up': 'grad_w', 'grad_w_ffn_down': 'grad_w', 'grad_w_ple_proj': 'grad_w', 'grad_w_ple_gate': 'grad_w', 'grad_ln2_gain': 'grad_w', 'grad_ln2_bias': 'grad_w', 'delta_w_in': 'delta_w', 'delta_ret_decay_fwd': 'delta_w', 'delta_ret_decay_bwd': 'delta_w', 'delta_ret_gn_gain': 'delta_w', 'delta_attn_sink': 'delta_w', 'delta_w_out': 'delta_w', 'delta_ln1_gain': 'delta_w', 'delta_ln1_bias': 'delta_w', 'delta_w_ffn_gate': 'delta_w', 'delta_w_ffn_up': 'delta_w', 'delta_w_ffn_down': 'delta_w', 'delta_w_ple_proj': 'delta_w', 'delta_w_ple_gate': 'delta_w', 'delta_ln2_gain': 'delta_w', 'delta_ln2_bias': 'delta_w', 'new_m_w_in': 'new_m', 'new_m_ret_decay_fwd': 'new_m', 'new_m_ret_decay_bwd': 'new_m', 'new_m_ret_gn_gain': 'new_m', 'new_m_attn_sink': 'new_m', 'new_m_w_out': 'new_m', 'new_m_ln1_gain': 'new_m', 'new_m_ln1_bias': 'new_m', 'new_m_w_ffn_gate': 'new_m', 'new_m_w_ffn_up': 'new_m', 'new_m_w_ffn_down': 'new_m', 'new_m_w_ple_proj': 'new_m', 'new_m_w_ple_gate': 'new_m', 'new_m_ln2_gain': 'new_m', 'new_m_ln2_bias': 'new_m', 'new_v_w_in': 'new_v', 'new_v_ret_decay_fwd': 'new_v', 'new_v_ret_decay_bwd': 'new_v', 'new_v_ret_gn_gain': 'new_v', 'new_v_attn_sink': 'new_v', 'new_v_w_out': 'new_v', 'new_v_ln1_gain': 'new_v', 'new_v_ln1_bias': 'new_v', 'new_v_w_ffn_gate': 'new_v', 'new_v_w_ffn_up': 'new_v', 'new_v_w_ffn_down': 'new_v', 'new_v_w_ple_proj': 'new_v', 'new_v_w_ple_gate': 'new_v', 'new_v_ln2_gain': 'new_v', 'new_v_ln2_bias': 'new_v'}


def _forward(args):
    return _fwd_reference(*[args[k] for k in FWD_PARAMS])


def _output_shape():
    out = _jax.eval_shape(lambda: _forward(_fwd_setup_inputs(0)))
    return out.shape, out.dtype

N_MICROBATCH = 1
ADAM_LR = 0.001
ADAM_B1 = 0.9
ADAM_B2 = 0.999
ADAM_EPS = 1e-08
ADAM_WD = 0.01
ADAM_STEP = 10
PER_EXAMPLE_BATCH_AXIS = {'x': 0, 'p': 1, 'loss_target': 0}
SHARED_INPUTS = []
_WEIGHT_DTYPES = {'w_in': _jnp.float32, 'ret_decay_fwd': _jnp.float32, 'ret_decay_bwd': _jnp.float32, 'ret_gn_gain': _jnp.float32, 'attn_sink': _jnp.float32, 'w_out': _jnp.float32, 'ln1_gain': _jnp.float32, 'ln1_bias': _jnp.float32, 'w_ffn_gate': _jnp.float32, 'w_ffn_up': _jnp.float32, 'w_ffn_down': _jnp.float32, 'w_ple_proj': _jnp.float32, 'w_ple_gate': _jnp.float32, 'ln2_gain': _jnp.float32, 'ln2_bias': _jnp.float32}
MOMENT_SCALE = {'w_in': 4.717129e-02, 'ret_decay_fwd': 1.110379e-01, 'ret_decay_bwd': 5.574479e-02, 'ret_gn_gain': 5.171904e-02, 'attn_sink': 2.815120e-02, 'w_out': 6.663984e-02, 'ln1_gain': 8.678218e-01, 'ln1_bias': 3.956444e-01, 'w_ffn_gate': 3.055913e-02, 'w_ffn_up': 2.950797e-02, 'w_ffn_down': 8.243783e-02, 'w_ple_proj': 7.499717e-02, 'w_ple_gate': 1.740530e-02, 'ln2_gain': 3.197777e+01, 'ln2_bias': 6.438237e-01}


def _to_microbatches(a, axis):
    t = _jnp.moveaxis(a, axis, 0)
    t = t.reshape((N_MICROBATCH, t.shape[0] // N_MICROBATCH) + t.shape[1:])
    return _jnp.moveaxis(t, 1, axis + 1)


def setup_inputs(seed: int = 0) -> dict:
    inp = _fwd_setup_inputs(seed)
    key = _jax.random.fold_in(_jax.random.key(seed), 7919)
    shape, _ = _output_shape()
    out = dict(inp)
    out["loss_target"] = _jax.random.normal(_jax.random.fold_in(key, 0), shape, _jnp.float32)
    for i, name in enumerate(TWIN_WEIGHTS):
        w = inp[name].astype(_jnp.float32)
        if MOMENT_SCALE is None:
            s = _jnp.sqrt(_jnp.mean(_jnp.square(w)) + 1e-30)
        else:
            s = MOMENT_SCALE[name]
        km, kv = _jax.random.split(_jax.random.fold_in(key, i + 1))
        out[name] = w
        out["m_" + name] = s * _jax.random.normal(km, w.shape, _jnp.float32)
        out["v_" + name] = (s * s) * _jax.random.uniform(kv, w.shape, _jnp.float32, 0.5, 1.5)
    if N_MICROBATCH > 1:
        for name, axis in PER_EXAMPLE_BATCH_AXIS.items():
            out[name] = _to_microbatches(out[name], axis)
    return {'x': out['x'], 'p': out['p'], 'w_in': out['w_in'], 'ret_decay_fwd': out['ret_decay_fwd'], 'ret_decay_bwd': out['ret_decay_bwd'], 'ret_gn_gain': out['ret_gn_gain'], 'attn_sink': out['attn_sink'], 'w_out': out['w_out'], 'ln1_gain': out['ln1_gain'], 'ln1_bias': out['ln1_bias'], 'w_ffn_gate': out['w_ffn_gate'], 'w_ffn_up': out['w_ffn_up'], 'w_ffn_down': out['w_ffn_down'], 'w_ple_proj': out['w_ple_proj'], 'w_ple_gate': out['w_ple_gate'], 'ln2_gain': out['ln2_gain'], 'ln2_bias': out['ln2_bias'], 'loss_target': out['loss_target'], 'm_w_in': out['m_w_in'], 'm_ret_decay_fwd': out['m_ret_decay_fwd'], 'm_ret_decay_bwd': out['m_ret_decay_bwd'], 'm_ret_gn_gain': out['m_ret_gn_gain'], 'm_attn_sink': out['m_attn_sink'], 'm_w_out': out['m_w_out'], 'm_ln1_gain': out['m_ln1_gain'], 'm_ln1_bias': out['m_ln1_bias'], 'm_w_ffn_gate': out['m_w_ffn_gate'], 'm_w_ffn_up': out['m_w_ffn_up'], 'm_w_ffn_down': out['m_w_ffn_down'], 'm_w_ple_proj': out['m_w_ple_proj'], 'm_w_ple_gate': out['m_w_ple_gate'], 'm_ln2_gain': out['m_ln2_gain'], 'm_ln2_bias': out['m_ln2_bias'], 'v_w_in': out['v_w_in'], 'v_ret_decay_fwd': out['v_ret_decay_fwd'], 'v_ret_decay_bwd': out['v_ret_decay_bwd'], 'v_ret_gn_gain': out['v_ret_gn_gain'], 'v_attn_sink': out['v_attn_sink'], 'v_w_out': out['v_w_out'], 'v_ln1_gain': out['v_ln1_gain'], 'v_ln1_bias': out['v_ln1_bias'], 'v_w_ffn_gate': out['v_w_ffn_gate'], 'v_w_ffn_up': out['v_w_ffn_up'], 'v_w_ffn_down': out['v_w_ffn_down'], 'v_w_ple_proj': out['v_w_ple_proj'], 'v_w_ple_gate': out['v_w_ple_gate'], 'v_ln2_gain': out['v_ln2_gain'], 'v_ln2_bias': out['v_ln2_bias']}


def _loss(weights, diff, rest, loss_target):
    with _jax.named_scope("forward"):
        args = {**rest, TWIN_DIFF_INPUT: diff, **{k: w.astype(_WEIGHT_DTYPES[k]) for k, w in weights.items()}}
        y = _forward(args)
    with _jax.named_scope("loss_head"):
        err = _jnp.square(y.astype(_jnp.float32) - loss_target)
        return 0.5 * _jnp.sum(_jnp.mean(err, axis=-1)) if err.ndim else 0.5 * err


def _adamw(w, g, m, v):
    m = ADAM_B1 * m + (1.0 - ADAM_B1) * g
    v = ADAM_B2 * v + (1.0 - ADAM_B2) * _jnp.square(g)
    m_hat = m / (1.0 - ADAM_B1 ** ADAM_STEP)
    v_hat = v / (1.0 - ADAM_B2 ** ADAM_STEP)
    delta = -ADAM_LR * (m_hat / (_jnp.sqrt(v_hat) + ADAM_EPS) + ADAM_WD * w)
    return delta, m, v


def reference(x, p, w_in, ret_decay_fwd, ret_decay_bwd, ret_gn_gain, attn_sink, w_out, ln1_gain, ln1_bias, w_ffn_gate, w_ffn_up, w_ffn_down, w_ple_proj, w_ple_gate, ln2_gain, ln2_bias, loss_target, m_w_in, m_ret_decay_fwd, m_ret_decay_bwd, m_ret_gn_gain, m_attn_sink, m_w_out, m_ln1_gain, m_ln1_bias, m_w_ffn_gate, m_w_ffn_up, m_w_ffn_down, m_w_ple_proj, m_w_ple_gate, m_ln2_gain, m_ln2_bias, v_w_in, v_ret_decay_fwd, v_ret_decay_bwd, v_ret_gn_gain, v_attn_sink, v_w_out, v_ln1_gain, v_ln1_bias, v_w_ffn_gate, v_w_ffn_up, v_w_ffn_down, v_w_ple_proj, v_w_ple_gate, v_ln2_gain, v_ln2_bias):
    given = dict(x=x, p=p, w_in=w_in, ret_decay_fwd=ret_decay_fwd, ret_decay_bwd=ret_decay_bwd, ret_gn_gain=ret_gn_gain, attn_sink=attn_sink, w_out=w_out, ln1_gain=ln1_gain, ln1_bias=ln1_bias, w_ffn_gate=w_ffn_gate, w_ffn_up=w_ffn_up, w_ffn_down=w_ffn_down, w_ple_proj=w_ple_proj, w_ple_gate=w_ple_gate, ln2_gain=ln2_gain, ln2_bias=ln2_bias, loss_target=loss_target, m_w_in=m_w_in, m_ret_decay_fwd=m_ret_decay_fwd, m_ret_decay_bwd=m_ret_decay_bwd, m_ret_gn_gain=m_ret_gn_gain, m_attn_sink=m_attn_sink, m_w_out=m_w_out, m_ln1_gain=m_ln1_gain, m_ln1_bias=m_ln1_bias, m_w_ffn_gate=m_w_ffn_gate, m_w_ffn_up=m_w_ffn_up, m_w_ffn_down=m_w_ffn_down, m_w_ple_proj=m_w_ple_proj, m_w_ple_gate=m_w_ple_gate, m_ln2_gain=m_ln2_gain, m_ln2_bias=m_ln2_bias, v_w_in=v_w_in, v_ret_decay_fwd=v_ret_decay_fwd, v_ret_decay_bwd=v_ret_decay_bwd, v_ret_gn_gain=v_ret_gn_gain, v_attn_sink=v_attn_sink, v_w_out=v_w_out, v_ln1_gain=v_ln1_gain, v_ln1_bias=v_ln1_bias, v_w_ffn_gate=v_w_ffn_gate, v_w_ffn_up=v_w_ffn_up, v_w_ffn_down=v_w_ffn_down, v_w_ple_proj=v_w_ple_proj, v_w_ple_gate=v_w_ple_gate, v_ln2_gain=v_ln2_gain, v_ln2_bias=v_ln2_bias)
    weights = {n: given[n] for n in TWIN_WEIGHTS}
    shared = {n: given[n] for n in SHARED_INPUTS}
    per_example = {n: given[n] for n in ['x', 'p']}
    grad_fn = _jax.value_and_grad(_loss, argnums=(0, 1))

    def one_microbatch(ex, loss_target):
        ex = dict(ex)
        diff = ex.pop(TWIN_DIFF_INPUT)
        return grad_fn(weights, diff, {**shared, **ex}, loss_target)

    if N_MICROBATCH == 1:
        loss, (grad_w, grad_x) = one_microbatch(per_example, given["loss_target"])
    else:
        def body(carry, xs):
            loss_sum, grad_sum = carry
            l_k, (gw_k, gx_k) = one_microbatch(xs[0], xs[1])
            with _jax.named_scope("update"):
                return (loss_sum + l_k, _jax.tree.map(_jnp.add, grad_sum, gw_k)), gx_k

        init = (_jnp.zeros((), _jnp.float32), _jax.tree.map(_jnp.zeros_like, weights))
        (loss, grad_w), grad_x = _jax.lax.scan(body, init, (per_example, given["loss_target"]))
    with _jax.named_scope("update"):
        delta_w, new_m, new_v = {}, {}, {}
        for n in TWIN_WEIGHTS:
            delta_w[n], new_m[n], new_v[n] = _adamw(weights[n], grad_w[n], given["m_" + n], given["v_" + n])
    return (loss, grad_x, *[grad_w[n] for n in TWIN_WEIGHTS], *[delta_w[n] for n in TWIN_WEIGHTS],
            *[new_m[n] for n in TWIN_WEIGHTS], *[new_v[n] for n in TWIN_WEIGHTS])
```

```python
import functools

import jax
import jax.numpy as jnp
from jax import lax
from jax.experimental import pallas as pl
from jax.experimental.pallas import tpu as pltpu

F32, BF16 = jnp.float32, jnp.bfloat16
SDS = jax.ShapeDtypeStruct
MESH = pl.DeviceIdType.MESH

N_DEV = 8
HEAD_DIM = 64
RET_HEADS = 8
ATTN_HEADS = 8
KV_HEADS = 2
GROUP = ATTN_HEADS // KV_HEADS
RET_W = RET_HEADS * HEAD_DIM
ATT_W = ATTN_HEADS * HEAD_DIM
KV_W = KV_HEADS * HEAD_DIM
LANES = 128
CHUNK = 128
BLOCK = 128
Q_SCALE = HEAD_DIM ** -0.5
ALPHA = 2.0 ** 0.25
LN_EPS = 1e-5
GN_EPS = 1e-5
NEG_INF = -1e30
C_RQ, C_RK, C_RV, C_RG = 0, RET_W, 2 * RET_W, 3 * RET_W
C_AQ = 4 * RET_W
C_AK = C_AQ + ATT_W
C_AV = C_AK + KV_W
IN_W = C_AV + KV_W

ADAM_LR = 0.001
ADAM_B1 = 0.9
ADAM_B2 = 0.999
ADAM_EPS = 1e-08
ADAM_WD = 0.01
ADAM_STEP = 10

VMEM_LIMIT = 48 * 1024 * 1024
SMALL_ROWS = 16
ROW_LN1G, ROW_LN1B, ROW_LN2G, ROW_LN2B, ROW_LOSS, ROW_GN, ROW_MISC = 0, 1, 2, 3, 4, 5, 6
MISC_DF, MISC_DB, MISC_SINK = 0, 8, 16


def _dot_nn(a, b):
    return lax.dot_general(a, b, (((1,), (0,)), ((), ())), preferred_element_type=F32)


def _dot_nt(a, b):
    return lax.dot_general(a, b, (((1,), (1,)), ((), ())), preferred_element_type=F32)


def _dot_tn(a, b):
    return lax.dot_general(a, b, (((0,), (0,)), ((), ())), preferred_element_type=F32)


def _params(sem=None, vmem=VMEM_LIMIT):
    kw = {"vmem_limit_bytes": vmem}
    if sem is not None:
        kw["dimension_semantics"] = sem
    return pltpu.CompilerParams(**kw)


def _row_tile(t, want=512):
    tm = want
    while t % tm:
        tm //= 2
    return tm


def _sigmoid(x):
    return 1.0 / (1.0 + jnp.exp(-x))


def _layer_norm_stats(z):
    mu = jnp.mean(z, axis=1, keepdims=True)
    d = z - mu
    var = jnp.mean(d * d, axis=1, keepdims=True)
    rstd = lax.rsqrt(var + LN_EPS)
    return d * rstd, rstd


def _layer_norm_bwd(dxh, xhat, rstd):
    m1 = jnp.mean(dxh, axis=1, keepdims=True)
    m2 = jnp.mean(dxh * xhat, axis=1, keepdims=True)
    return rstd * (dxh - m1 - xhat * m2)


def _prep_shards(w_in, w_out, w_gate, w_up, w_down, w_pe, w_pg):
    def body(win, wo, wg, wu, wd, wpe, wpg, o_in, o_o, o_g, o_u, o_d, o_pe, o_pg):
        o_in[...] = win[...].T.astype(BF16)
        o_g[...] = wg[...].T.astype(BF16)
        o_u[...] = wu[...].T.astype(BF16)
        o_pe[...] = wpe[...].T.astype(BF16)
        o_o[...] = wo[...].astype(BF16)
        o_d[...] = wd[...].astype(BF16)
        o_pg[...] = wpg[...].astype(BF16)

    t = lambda a: SDS((a.shape[1], a.shape[0]), BF16)
    s = lambda a: SDS(a.shape, BF16)
    return pl.pallas_call(
        body, name="prep_shards",
        out_shape=[t(w_in), s(w_out), t(w_gate), t(w_up), s(w_down), t(w_pe), s(w_pg)],
        compiler_params=_params(),
    )(w_in, w_out, w_gate, w_up, w_down, w_pe, w_pg)


def _mesh_pos():
    return lax.axis_index("x"), lax.axis_index("y"), lax.axis_index("c")


def _all_gather(shards):
    n = len(shards)

    def body(*refs):
        ins, outs = refs[:n], refs[n:2 * n]
        send_sems, recv_sems, local_sems = refs[2 * n:]
        x, y, c = _mesh_pos()
        me, sibling = (x, y, c), (x, y, 1 - c)
        chips = [(1 - x, y), (x, 1 - y), (1 - x, 1 - y)]

        def rows(i, pos):
            r = ins[i].shape[0]
            px, py, pc = pos
            return outs[i].at[pl.ds(pl.multiple_of((4 * px + 2 * py + pc) * r, 8), r), :]

        def copy(i, k, block, to, src=None):
            return pltpu.make_async_remote_copy(
                src_ref=rows(i, block) if src is None else src, dst_ref=rows(i, block),
                send_sem=send_sems.at[i, k], recv_sem=recv_sems.at[i, k], device_id=to, device_id_type=MESH)

        mine = [pltpu.make_async_copy(ins[i], rows(i, me), local_sems.at[i]) for i in range(n)]
        for cp in mine:
            cp.start()
        first = []
        for i in range(n):
            first.append(copy(i, 0, me, sibling, src=ins[i]))
            first += [copy(i, 1 + j, me, (*chip, c), src=ins[i]) for j, chip in enumerate(chips)]
        for cp in first:
            cp.start()
        passed = []
        for j, chip in enumerate(chips):
            for i in range(n):
                copy(i, 1 + j, (*chip, c), me).wait_recv()
                fwd = copy(i, 4 + j, (*chip, c), sibling)
                fwd.start()
                passed.append(fwd)
        for i in range(n):
            copy(i, 0, sibling, me).wait_recv()
            for j, chip in enumerate(chips):
                copy(i, 4 + j, (*chip, 1 - c), me).wait_recv()
        for cp in first + passed:
            cp.wait_send()
        for cp in mine:
            cp.wait()

    any_spec = pl.BlockSpec(memory_space=pl.ANY)
    return pl.pallas_call(
        body, name="all_gather_weights",
        out_shape=[SDS((N_DEV * s.shape[0], s.shape[1]), s.dtype) for s in shards],
        in_specs=[any_spec] * n, out_specs=[any_spec] * n,
        scratch_shapes=[pltpu.SemaphoreType.DMA((n, 7)), pltpu.SemaphoreType.DMA((n, 7)),
                        pltpu.SemaphoreType.DMA((n,))],
    )(*shards)


def _peer(m):
    x, y, c = _mesh_pos()
    bx, by, bc = (m >> 2) & 1, (m >> 1) & 1, m & 1
    return (x ^ bx if bx else x, y ^ by if by else y, c ^ bc if bc else c)


def _peer_index(m):
    x, y, c = _mesh_pos()
    return (4 * x + 2 * y + c) ^ m


def _scatter_rows(partials):
    n = len(partials)

    def body(*refs):
        ins, outs = refs[:n], refs[n:2 * n]
        send_sems, recv_sems = refs[2 * n:]

        def copy(i, m):
            r = ins[i].shape[0] // N_DEV
            src = ins[i].at[pl.ds(pl.multiple_of(_peer_index(m) * r, 8), r), :]
            return pltpu.make_async_remote_copy(
                src_ref=src, dst_ref=outs[i].at[m], send_sem=send_sems.at[i, m - 1],
                recv_sem=recv_sems.at[i, m - 1], device_id=_peer(m), device_id_type=MESH)

        cps = [copy(i, m) for i in range(n) for m in range(1, N_DEV)]
        for cp in cps:
            cp.start()
        for cp in cps:
            cp.wait_recv()
        for cp in cps:
            cp.wait_send()

    any_spec = pl.BlockSpec(memory_space=pl.ANY)
    return pl.pallas_call(
        body, name="scatter_weight_grads",
        out_shape=[SDS((N_DEV, p.shape[0] // N_DEV, p.shape[1]), p.dtype) for p in partials],
        in_specs=[any_spec] * n, out_specs=[any_spec] * n,
        scratch_shapes=[pltpu.SemaphoreType.DMA((n, 7)), pltpu.SemaphoreType.DMA((n, 7))],
    )(*partials)


def _small_all_reduce_adamw(part, w, m, v, decay):
    d = part.shape[1]

    def body(part_ref, w_ref, m_ref, v_ref, decay_ref, g_ref, dl_ref, nm_ref, nv_ref, buf, send_sems, recv_sems):
        buf[0] = part_ref[...]
        cps = [pltpu.make_async_remote_copy(
            src_ref=part_ref, dst_ref=buf.at[k], send_sem=send_sems.at[k - 1], recv_sem=recv_sems.at[k - 1],
            device_id=_peer(k), device_id_type=MESH) for k in range(1, N_DEV)]
        for cp in cps:
            cp.start()
        for cp in cps:
            cp.wait_recv()
        for cp in cps:
            cp.wait_send()
        me = _peer_index(0)
        tot = buf[me]
        for dev in range(1, N_DEV):
            tot = tot + buf[dev ^ me]
        row = lax.broadcasted_iota(jnp.int32, tot.shape, 0)
        lane = lax.broadcasted_iota(jnp.int32, tot.shape, 1)
        p2 = jnp.exp2(decay_ref[...])
        chain = jnp.where((row == ROW_MISC) & (lane < MISC_SINK), -p2 * jnp.log(2.0) / (1.0 - p2), 1.0)
        loss = 0.5 / d * jnp.sum(jnp.where(row == ROW_LOSS, tot, 0.0))
        g = jnp.where(row == ROW_LOSS, loss, tot * chain)
        g_ref[...] = g
        _adamw_store(g, w_ref[...], m_ref[...], v_ref[...], dl_ref, nm_ref, nv_ref)

    vm = pl.BlockSpec(memory_space=pltpu.VMEM)
    out = SDS(part.shape, F32)
    return pl.pallas_call(
        body, name="small_all_reduce_adamw",
        out_shape=[out] * 4, in_specs=[vm] * 5, out_specs=[vm] * 4,
        scratch_shapes=[pltpu.VMEM((N_DEV,) + part.shape, F32), pltpu.SemaphoreType.DMA((7,)),
                        pltpu.SemaphoreType.DMA((7,))],
    )(part, w, m, v, decay)


def _adamw_store(g, w, m, v, dl_ref, nm_ref, nv_ref):
    m = ADAM_B1 * m + (1.0 - ADAM_B1) * g
    v = ADAM_B2 * v + (1.0 - ADAM_B2) * (g * g)
    m_hat = m / (1.0 - ADAM_B1 ** ADAM_STEP)
    v_hat = v / (1.0 - ADAM_B2 ** ADAM_STEP)
    dl_ref[...] = -ADAM_LR * (m_hat / (jnp.sqrt(v_hat) + ADAM_EPS) + ADAM_WD * w)
    nm_ref[...] = m
    nv_ref[...] = v


def _reduce_adamw(own, recv, w, m, v, transposed):
    def body(own_ref, recv_ref, w_ref, m_ref, v_ref, g_ref, dl_ref, nm_ref, nv_ref):
        g = own_ref[...]
        for k in range(1, N_DEV):
            g = g + recv_ref[k].astype(F32)
        if transposed:
            g = g.T
        g_ref[...] = g
        _adamw_store(g, w_ref[...], m_ref[...], v_ref[...], dl_ref, nm_ref, nv_ref)

    out = SDS(w.shape, F32)
    return pl.pallas_call(body, name="reduce_adamw", out_shape=[out] * 4, compiler_params=_params())(own, recv, w, m, v)


def _row_spec(tm, width):
    return pl.BlockSpec((tm, width), lambda i: (i, 0))


def _full_spec(shape):
    return pl.BlockSpec(shape, lambda i: (0,) * len(shape))


def _in_proj(x2, wt_in):
    t, d = x2.shape
    u_w = wt_in.shape[0]
    tm = _row_tile(t)

    def body(x_ref, w_ref, u_ref, xb_ref):
        xb = x_ref[...].astype(BF16)
        xb_ref[...] = xb
        u_ref[...] = _dot_nt(xb, w_ref[...]).astype(BF16)

    return pl.pallas_call(
        body, name="in_proj", grid=(t // tm,),
        in_specs=[_row_spec(tm, d), _full_spec(wt_in.shape)],
        out_specs=[_row_spec(tm, u_w), _row_spec(tm, d)],
        out_shape=[SDS((t, u_w), BF16), SDS((t, d), BF16)],
        compiler_params=_params(("parallel",)),
    )(x2, wt_in)


def _out_proj_ln1(r, a, x2, w_out, g1, b1):
    t, d = x2.shape
    tm = _row_tile(t)

    def body(r_ref, a_ref, x_ref, wo_ref, g_ref, b_ref, z_ref, hb_ref):
        mix = _dot_nn(r_ref[...], wo_ref[0:RET_W, :]) + _dot_nn(a_ref[...], wo_ref[RET_W:RET_W + ATT_W, :])
        z = ALPHA * x_ref[...] + mix
        xhat, _ = _layer_norm_stats(z)
        z_ref[...] = z
        hb_ref[...] = (xhat * g_ref[...] + b_ref[...]).astype(BF16)

    return pl.pallas_call(
        body, name="out_proj_ln1", grid=(t // tm,),
        in_specs=[_row_spec(tm, RET_W), _row_spec(tm, ATT_W), _row_spec(tm, d), _full_spec(w_out.shape),
                  _full_spec(g1.shape), _full_spec(b1.shape)],
        out_specs=[_row_spec(tm, d), _row_spec(tm, d)],
        out_shape=[SDS((t, d), F32), SDS((t, d), BF16)],
        compiler_params=_params(("parallel",)),
    )(r, a, x2, w_out, g1, b1)


def _col_halves(f):
    n = f // LANES
    k = (n + 1) // 2 * LANES
    return [(0, k), (k, f)] if k < f else [(0, f)]


def _ffn_up(h1b, wt_gate, wt_up):
    t, d = h1b.shape
    f = wt_gate.shape[0]
    tm = _row_tile(t, 256)

    def body(h_ref, wg_ref, wu_ref, g_ref, u_ref, act_ref):
        h = h_ref[...]
        for lo, hi in _col_halves(f):
            g = _dot_nt(h, wg_ref[lo:hi, :])
            u = _dot_nt(h, wu_ref[lo:hi, :])
            g_ref[:, lo:hi] = g.astype(BF16)
            u_ref[:, lo:hi] = u.astype(BF16)
            act_ref[:, lo:hi] = (g * _sigmoid(g) * u).astype(BF16)

    return pl.pallas_call(
        body, name="ffn_up", grid=(t // tm,),
        in_specs=[_row_spec(tm, d), _full_spec(wt_gate.shape), _full_spec(wt_up.shape)],
        out_specs=[_row_spec(tm, f)] * 3,
        out_shape=[SDS((t, f), BF16)] * 3,
        compiler_params=_params(("parallel",)),
    )(h1b, wt_gate, wt_up)


def _ffn_down_ln2_loss(act, h1b, p2, z1, target, w_down, w_pg, wt_pe, g1, b1, g2, b2):
    t, d = z1.shape
    f = act.shape[1]
    pdim = p2.shape[1]
    tm = _row_tile(t, 256)

    def body(act_ref, hb_ref, p_ref, z1_ref, tgt_ref, wd_ref, wpg_ref, wpe_ref, g1_ref, b1_ref, g2_ref, b2_ref,
             dz_ref, dzb_ref, ds_ref, dple_ref, acc_ref):
        @pl.when(pl.program_id(0) == 0)
        def _():
            acc_ref[...] = jnp.zeros_like(acc_ref)

        xhat1, _ = _layer_norm_stats(z1_ref[...])
        h1 = xhat1 * g1_ref[...] + b1_ref[...]
        ffn = _dot_nn(act_ref[...], wd_ref[...])
        pg = _sigmoid(_dot_nn(hb_ref[...], wpg_ref[...]))
        ple = _dot_nt(p_ref[...].astype(BF16), wpe_ref[...])
        z2 = ALPHA * h1 + ffn + pg * ple
        xhat2, rstd2 = _layer_norm_stats(z2)
        err = xhat2 * g2_ref[...] + b2_ref[...] - tgt_ref[...]
        dy = err * (1.0 / d)
        dz = _layer_norm_bwd(dy * g2_ref[...], xhat2, rstd2)
        dz_ref[...] = dz
        dzb_ref[...] = dz.astype(BF16)
        ds_ref[...] = (dz * ple * pg * (1.0 - pg)).astype(BF16)
        dple_ref[...] = (dz * pg).astype(BF16)
        acc_ref[0:1, :] += jnp.sum(err * err, axis=0, keepdims=True)
        acc_ref[1:2, :] += jnp.sum(dy * xhat2, axis=0, keepdims=True)
        acc_ref[2:3, :] += jnp.sum(dy, axis=0, keepdims=True)

    vec = _full_spec(g1.shape)
    return pl.pallas_call(
        body, name="ffn_down_ln2_loss", grid=(t // tm,),
        in_specs=[_row_spec(tm, f), _row_spec(tm, d), _row_spec(tm, pdim), _row_spec(tm, d), _row_spec(tm, d),
                  _full_spec(w_down.shape), _full_spec(w_pg.shape), _full_spec(wt_pe.shape), vec, vec, vec, vec],
        out_specs=[_row_spec(tm, d)] * 4 + [_full_spec((8, d))],
        out_shape=[SDS((t, d), F32), SDS((t, d), BF16), SDS((t, d), BF16), SDS((t, d), BF16), SDS((8, d), F32)],
        compiler_params=_params(("arbitrary",)),
    )(act, h1b, p2, z1, target, w_down, w_pg, wt_pe, g1, b1, g2, b2)


def _ffn_bwd_act(dzb, g, up, w_down):
    t, d = dzb.shape
    f = g.shape[1]
    tm = _row_tile(t, 256)

    def body(dz_ref, g_ref, u_ref, wd_ref, dg_ref, du_ref):
        dz = dz_ref[...]
        for lo, hi in _col_halves(f):
            da = _dot_nt(dz, wd_ref[lo:hi, :])
            gv = g_ref[:, lo:hi].astype(F32)
            sg = _sigmoid(gv)
            dg_ref[:, lo:hi] = (da * u_ref[:, lo:hi].astype(F32) * (sg * (1.0 + gv * (1.0 - sg)))).astype(BF16)
            du_ref[:, lo:hi] = (da * gv * sg).astype(BF16)

    return pl.pallas_call(
        body, name="ffn_bwd_act", grid=(t // tm,),
        in_specs=[_row_spec(tm, d), _row_spec(tm, f), _row_spec(tm, f), _full_spec(w_down.shape)],
        out_specs=[_row_spec(tm, f)] * 2,
        out_shape=[SDS((t, f), BF16)] * 2,
        compiler_params=_params(("parallel",)),
    )(dzb, g, up, w_down)


def _dh1_ln1_bwd(dz2, dg, dup, dsb, z1, wt_gate, wt_up, w_pg, g1):
    t, d = dz2.shape
    f = dg.shape[1]
    tm = _row_tile(t, 256)

    def body(dz_ref, dg_ref, du_ref, ds_ref, z1_ref, wg_ref, wu_ref, wpg_ref, g1_ref, dz1_ref, dz1b_ref, acc_ref):
        @pl.when(pl.program_id(0) == 0)
        def _():
            acc_ref[...] = jnp.zeros_like(acc_ref)

        dh = (ALPHA * dz_ref[...] + _dot_nn(dg_ref[...], wg_ref[...]) + _dot_nn(du_ref[...], wu_ref[...])
              + _dot_nt(ds_ref[...], wpg_ref[...]))
        xhat, rstd = _layer_norm_stats(z1_ref[...])
        dz1 = _layer_norm_bwd(dh * g1_ref[...], xhat, rstd)
        dz1_ref[...] = dz1
        dz1b_ref[...] = dz1.astype(BF16)
        acc_ref[0:1, :] += jnp.sum(dh * xhat, axis=0, keepdims=True)
        acc_ref[1:2, :] += jnp.sum(dh, axis=0, keepdims=True)

    return pl.pallas_call(
        body, name="dh1_ln1_bwd", grid=(t // tm,),
        in_specs=[_row_spec(tm, d), _row_spec(tm, f), _row_spec(tm, f), _row_spec(tm, d), _row_spec(tm, d),
                  _full_spec(wt_gate.shape), _full_spec(wt_up.shape), _full_spec(w_pg.shape), _full_spec(g1.shape)],
        out_specs=[_row_spec(tm, d), _row_spec(tm, d), _full_spec((8, d))],
        out_shape=[SDS((t, d), F32), SDS((t, d), BF16), SDS((8, d), F32)],
        compiler_params=_params(("arbitrary",)),
    )(dz2, dg, dup, dsb, z1, wt_gate, wt_up, w_pg, g1)


def _out_proj_bwd(dz1b, w_out):
    t, d = dz1b.shape
    tm = _row_tile(t)

    def body(dz_ref, wo_ref, dr_ref, da_ref):
        dz = dz_ref[...]
        dr_ref[...] = _dot_nt(dz, wo_ref[0:RET_W, :]).astype(BF16)
        da_ref[...] = _dot_nt(dz, wo_ref[RET_W:RET_W + ATT_W, :]).astype(BF16)

    return pl.pallas_call(
        body, name="out_proj_bwd", grid=(t // tm,),
        in_specs=[_row_spec(tm, d), _full_spec(w_out.shape)],
        out_specs=[_row_spec(tm, RET_W), _row_spec(tm, ATT_W)],
        out_shape=[SDS((t, RET_W), BF16), SDS((t, ATT_W), BF16)],
        compiler_params=_params(("parallel",)),
    )(dz1b, w_out)


def _in_proj_bwd(dz1, parts, wt_in):
    t, d = dz1.shape
    tm = _row_tile(t)
    widths = [p.shape[1] for p in parts]

    def body(*refs):
        dz_ref, part_refs, w_ref, dx_ref = refs[0], refs[1:1 + len(parts)], refs[-2], refs[-1]
        acc = ALPHA * dz_ref[...]
        lo = 0
        for p_ref, w in zip(part_refs, widths):
            acc = acc + _dot_nn(p_ref[...], w_ref[lo:lo + w, :])
            lo += w
        dx_ref[...] = acc

    return pl.pallas_call(
        body, name="in_proj_bwd", grid=(t // tm,),
        in_specs=[_row_spec(tm, d)] + [_row_spec(tm, w) for w in widths] + [_full_spec(wt_in.shape)],
        out_specs=_row_spec(tm, d), out_shape=SDS((t, d), F32),
        compiler_params=_params(("parallel",)),
    )(dz1, *parts, wt_in)


def _weight_grad(name, me, parts, rhs):
    t, n = rhs.shape
    widths = [p.shape[1] for p in parts]
    rows = sum(widths)
    own_rows = rows // N_DEV
    tk = _row_tile(t)
    step = 256

    def body(*refs):
        me_ref, part_refs, rhs_ref = refs[0], refs[1:1 + len(parts)], refs[1 + len(parts)]
        full_ref, own_ref, acc = refs[-3], refs[-2], refs[-1]
        i = pl.program_id(0)

        @pl.when(i == 0)
        def _():
            acc[...] = jnp.zeros_like(acc)

        b = rhs_ref[...].astype(BF16)
        lo = 0
        for p_ref, w in zip(part_refs, widths):
            for c0 in range(0, w, step):
                c1 = min(c0 + step, w)
                acc[lo + c0:lo + c1, :] += _dot_tn(p_ref[:, c0:c1].astype(BF16), b)
            lo += w

        @pl.when(i == pl.num_programs(0) - 1)
        def _():
            full_ref[...] = acc[...].astype(BF16)
            own_ref[...] = acc[pl.ds(pl.multiple_of(me_ref[0] * own_rows, 8), own_rows), :]

    return pl.pallas_call(
        body, name=name, grid=(t // tk,),
        in_specs=[_smem_spec()] + [_row_spec(tk, w) for w in widths] + [_row_spec(tk, n)],
        out_specs=[_full_spec((rows, n)), _full_spec((own_rows, n))],
        out_shape=[SDS((rows, n), BF16), SDS((own_rows, n), F32)],
        scratch_shapes=[pltpu.VMEM((rows, n), F32)],
        compiler_params=_params(("arbitrary",)),
    )(me, *parts, rhs)


def _log_decay(decay_f, decay_b):
    def body(f_ref, b_ref, lf_ref, lb_ref):
        lf_ref[...] = jnp.log1p(-jnp.exp2(f_ref[...]))
        lb_ref[...] = jnp.log1p(-jnp.exp2(b_ref[...]))

    return pl.pallas_call(body, name="log_decay", out_shape=[SDS(decay_f.shape, F32)] * 2)(decay_f, decay_b)


def _pair_tables(lgf_ref, lgb_ref, pair):
    lane = lax.broadcasted_iota(jnp.int32, (1, LANES), 1)
    is_a = lane < HEAD_DIM
    lgf = jnp.where(is_a, lgf_ref[2 * pair], lgf_ref[2 * pair + 1])
    lgb = jnp.where(is_a, lgb_ref[2 * pair], lgb_ref[2 * pair + 1])
    row = lax.broadcasted_iota(jnp.int32, (CHUNK, 1), 0).astype(F32)
    tab = dict(
        is_a=is_a, row=row,
        kdec_f=jnp.exp(lgf * (CHUNK - 1.0 - row)), qdec_f=jnp.exp(lgf * (row + 1.0)),
        kdec_b=jnp.exp(lgb * row), qdec_b=jnp.exp(lgb * (CHUNK - row)),
        lam_f=jnp.exp(lgf * CHUNK), lam_b=jnp.exp(lgb * CHUNK),
    )
    r = lax.broadcasted_iota(jnp.int32, (LANES, LANES), 0)
    c = lax.broadcasted_iota(jnp.int32, (LANES, LANES), 1)
    tab["diag"] = (r < HEAD_DIM) == (c < HEAD_DIM)
    i = lax.broadcasted_iota(jnp.int32, (CHUNK, CHUNK), 0)
    j = lax.broadcasted_iota(jnp.int32, (CHUNK, CHUNK), 1)
    diff = (i - j).astype(F32)
    up, dn = jnp.maximum(diff, 0.0), jnp.maximum(-diff, 0.0)
    for h, name in ((0, "a"), (1, "b")):
        ef = jnp.where(diff >= 0, jnp.exp(lgf_ref[2 * pair + h] * up), 0.0)
        eb = jnp.where(diff <= 0, jnp.exp(lgb_ref[2 * pair + h] * dn), 0.0)
        tab["d_" + name] = ef + eb
        tab["df_" + name] = ef * up
        tab["db_" + name] = eb * dn
    return tab


def _chunk(ref, n):
    return ref[pl.ds(pl.multiple_of(n * CHUNK, CHUNK), CHUNK), :]


def _pair_select(is_a, a, b):
    return jnp.where(is_a, a, b)


def _group_sum(is_a, v):
    sa = jnp.sum(jnp.where(is_a, v, 0.0), axis=1, keepdims=True)
    sb = jnp.sum(jnp.where(is_a, 0.0, v), axis=1, keepdims=True)
    return jnp.where(is_a, sa, sb)


def _retention_states(tab, k_ref, v_ref, n_chunks, rf, rb, extra=None):
    diag = tab["diag"]
    zero = jnp.zeros((LANES, LANES), F32)

    def kv(n, kdec):
        k = _chunk(k_ref, n).astype(F32) * Q_SCALE
        return jnp.where(diag, _dot_tn((k * kdec).astype(BF16), _chunk(v_ref, n)), 0.0)

    def up_body(n, carry):
        r, e = carry
        rf[n] = r
        r = r * tab["lam_f"] + kv(n, tab["kdec_f"])
        if extra is not None:
            ef, eb, gfun = extra
            eb[n] = e
            e = e * tab["lam_b"] + gfun(n, tab["qdec_b"])
        return r, e

    def down_body(s, carry):
        n = n_chunks - 1 - s
        r, e = carry
        rb[n] = r
        r = r * tab["lam_b"] + kv(n, tab["kdec_b"])
        if extra is not None:
            ef, eb, gfun = extra
            ef[n] = e
            e = e * tab["lam_f"] + gfun(n, tab["qdec_f"])
        return r, e

    lax.fori_loop(0, n_chunks, up_body, (zero, zero))
    lax.fori_loop(0, n_chunks, down_body, (zero, zero))


def _seq_spec(s, col_block):
    return pl.BlockSpec((s, LANES), lambda b, h: (b, col_block + h))


def _smem_spec():
    return pl.BlockSpec(memory_space=pltpu.SMEM)


def _retention_fwd(u, lgf, lgb, gn_gain, b_loc):
    t = u.shape[0]
    s = t // b_loc
    n_chunks = s // CHUNK
    pairs = RET_HEADS // 2

    def body(lgf_ref, lgb_ref, q_ref, k_ref, v_ref, g_ref, gain_ref, r_ref, y_ref, rf, rb):
        tab = _pair_tables(lgf_ref, lgb_ref, pl.program_id(1))
        is_a = tab["is_a"]
        _retention_states(tab, k_ref, v_ref, n_chunks, rf, rb)

        def out_body(n, _):
            q = _chunk(q_ref, n)
            k8 = (_chunk(k_ref, n).astype(F32) * Q_SCALE).astype(BF16)
            v = _chunk(v_ref, n)
            zero = jnp.zeros_like(q)
            sa = _dot_nt(jnp.where(is_a, q, zero), k8) * tab["d_a"]
            sb = _dot_nt(jnp.where(is_a, zero, q), k8) * tab["d_b"]
            y = _pair_select(is_a, _dot_nn(sa.astype(BF16), v), _dot_nn(sb.astype(BF16), v))
            qf = q.astype(F32)
            y = y + _dot_nn((qf * tab["qdec_f"]).astype(BF16), rf[n].astype(BF16))
            y = y + _dot_nn((qf * tab["qdec_b"]).astype(BF16), rb[n].astype(BF16))
            rows = pl.ds(pl.multiple_of(n * CHUNK, CHUNK), CHUNK)
            y_ref[rows, :] = y
            mu = _group_sum(is_a, y) * (1.0 / HEAD_DIM)
            dlt = y - mu
            var = _group_sum(is_a, dlt * dlt) * (1.0 / HEAD_DIM)
            xhat = dlt * lax.rsqrt(var + GN_EPS)
            gate = _chunk(g_ref, n).astype(F32)
            r_ref[rows, :] = (xhat * gain_ref[...] * gate * _sigmoid(gate)).astype(BF16)
            return 0

        lax.fori_loop(0, n_chunks, out_body, 0)

    lane_blk = lambda c0: _seq_spec(s, c0 // LANES)
    return pl.pallas_call(
        body, name="retention_fwd", grid=(b_loc, pairs),
        in_specs=[_smem_spec(), _smem_spec(), lane_blk(C_RQ), lane_blk(C_RK), lane_blk(C_RV), lane_blk(C_RG),
                  pl.BlockSpec((1, LANES), lambda b, h: (0, h))],
        out_specs=[_seq_spec(s, 0), _seq_spec(s, 0)],
        out_shape=[SDS((t, RET_W), BF16), SDS((t, RET_W), F32)],
        scratch_shapes=[pltpu.VMEM((n_chunks, LANES, LANES), F32)] * 2,
        compiler_params=_params(("parallel", "parallel")),
    )(lgf, lgb, u, u, u, u, gn_gain)


ST_GAIN, ST_XF, ST_XB, ST_IFA, ST_IFB, ST_IBA, ST_IBB, ST_LF, ST_LB = 0, 1, 2, 3, 4, 5, 6, 8, 9
ST_ROWS = 16


def _retention_bwd(u, y_pre, dr, lgf, lgb, gn_gain, b_loc):
    t = u.shape[0]
    s = t // b_loc
    n_chunks = s // CHUNK
    pairs = RET_HEADS // 2

    def body(lgf_ref, lgb_ref, q_ref, k_ref, v_ref, g_ref, y_ref, dr_ref, gain_ref,
             dq_ref, dk_ref, dv_ref, dg_ref, st_ref, rf, rb, ef, eb, dy_s):
        tab = _pair_tables(lgf_ref, lgb_ref, pl.program_id(1))
        is_a, row, diag = tab["is_a"], tab["row"], tab["diag"]
        gain = gain_ref[...]

        def norm_body(n, dgain):
            rows = pl.ds(pl.multiple_of(n * CHUNK, CHUNK), CHUNK)
            y = y_ref[rows, :]
            mu = _group_sum(is_a, y) * (1.0 / HEAD_DIM)
            dlt = y - mu
            var = _group_sum(is_a, dlt * dlt) * (1.0 / HEAD_DIM)
            rstd = lax.rsqrt(var + GN_EPS)
            xhat = dlt * rstd
            gate = g_ref[rows, :].astype(F32)
            sg = _sigmoid(gate)
            silu = gate * sg
            d_out = dr_ref[rows, :].astype(F32)
            dg_ref[rows, :] = (d_out * xhat * gain * (sg * (1.0 + gate * (1.0 - sg)))).astype(BF16)
            dxh = d_out * gain * silu
            m1 = _group_sum(is_a, dxh) * (1.0 / HEAD_DIM)
            m2 = _group_sum(is_a, dxh * xhat) * (1.0 / HEAD_DIM)
            dy_s[rows, :] = (rstd * (dxh - m1 - xhat * m2)).astype(BF16)
            return dgain + jnp.sum(d_out * xhat * silu, axis=0, keepdims=True)

        dgain = lax.fori_loop(0, n_chunks, norm_body, jnp.zeros((1, LANES), F32))

        def state_grad(n, qdec):
            qd = (_chunk(q_ref, n).astype(F32) * qdec).astype(BF16)
            return jnp.where(diag, _dot_tn(qd, _chunk(dy_s, n)), 0.0)

        _retention_states(tab, k_ref, v_ref, n_chunks, rf, rb, extra=(ef, eb, state_grad))

        def grad_body(n, carry):
            xf, xb, ifa, ifb, iba, ibb, lf, lb = carry
            rows = pl.ds(pl.multiple_of(n * CHUNK, CHUNK), CHUNK)
            q = q_ref[rows, :]
            qf = q.astype(F32)
            k8f = k_ref[rows, :].astype(F32) * Q_SCALE
            k8 = k8f.astype(BF16)
            v = v_ref[rows, :]
            dy = dy_s[rows, :]
            zero = jnp.zeros_like(q)
            dq = jnp.zeros((CHUNK, LANES), F32)
            dk = jnp.zeros((CHUNK, LANES), F32)
            dv = jnp.zeros((CHUNK, LANES), F32)
            intra = []
            for first, name in ((True, "a"), (False, "b")):
                sel = is_a if first else jnp.logical_not(is_a)
                sc = _dot_nt(jnp.where(sel, q, zero), k8)
                dp = _dot_nt(jnp.where(sel, dy, zero), v)
                a_mat = (sc * tab["d_" + name]).astype(BF16)
                ds_mat = (dp * tab["d_" + name]).astype(BF16)
                dq = dq + jnp.where(sel, _dot_nn(ds_mat, k8), 0.0)
                dk = dk + jnp.where(sel, _dot_tn(ds_mat, q), 0.0)
                dv = dv + jnp.where(sel, _dot_tn(a_mat, dy), 0.0)
                prod = sc * dp
                intra.append(jnp.sum(prod * tab["df_" + name], axis=0, keepdims=True))
                intra.append(jnp.sum(prod * tab["db_" + name], axis=0, keepdims=True))
            r_f, r_b = rf[n], rb[n]
            e_f, e_b = ef[n], eb[n]
            dqc_f = _dot_nt(dy, r_f.astype(BF16)) * tab["qdec_f"]
            dqc_b = _dot_nt(dy, r_b.astype(BF16)) * tab["qdec_b"]
            dkc_f = _dot_nt(v, e_f.astype(BF16)) * tab["kdec_f"]
            dkc_b = _dot_nt(v, e_b.astype(BF16)) * tab["kdec_b"]
            dv = dv + _dot_nn((k8f * tab["kdec_f"]).astype(BF16), e_f.astype(BF16))
            dv = dv + _dot_nn((k8f * tab["kdec_b"]).astype(BF16), e_b.astype(BF16))
            dq_ref[rows, :] = (dq + dqc_f + dqc_b).astype(BF16)
            dk_ref[rows, :] = ((dk + dkc_f + dkc_b) * Q_SCALE).astype(BF16)
            dv_ref[rows, :] = dv.astype(BF16)
            xf = xf + jnp.sum((row + 1.0) * qf * dqc_f + (CHUNK - 1.0 - row) * k8f * dkc_f, axis=0, keepdims=True)
            xb = xb + jnp.sum((CHUNK - row) * qf * dqc_b + row * k8f * dkc_b, axis=0, keepdims=True)
            lf = lf + jnp.sum(e_f * r_f, axis=0, keepdims=True)
            lb = lb + jnp.sum(e_b * r_b, axis=0, keepdims=True)
            return (xf, xb, ifa + intra[0], ifb + intra[2], iba + intra[1], ibb + intra[3], lf, lb)

        z = jnp.zeros((1, LANES), F32)
        xf, xb, ifa, ifb, iba, ibb, lf, lb = lax.fori_loop(0, n_chunks, grad_body, (z,) * 8)
        st_ref[...] = jnp.zeros_like(st_ref)
        st_ref[ST_GAIN:ST_GAIN + 1, :] = dgain
        st_ref[ST_XF:ST_XF + 1, :] = xf
        st_ref[ST_XB:ST_XB + 1, :] = xb
        st_ref[ST_IFA:ST_IFA + 1, :] = ifa
        st_ref[ST_IFB:ST_IFB + 1, :] = ifb
        st_ref[ST_IBA:ST_IBA + 1, :] = iba
        st_ref[ST_IBB:ST_IBB + 1, :] = ibb
        st_ref[ST_LF:ST_LF + 1, :] = lf * (CHUNK * tab["lam_f"])
        st_ref[ST_LB:ST_LB + 1, :] = lb * (CHUNK * tab["lam_b"])

    lane_blk = lambda c0: _seq_spec(s, c0 // LANES)
    seq0 = _seq_spec(s, 0)
    state = pltpu.VMEM((n_chunks, LANES, LANES), F32)
    return pl.pallas_call(
        body, name="retention_bwd", grid=(b_loc, pairs),
        in_specs=[_smem_spec(), _smem_spec(), lane_blk(C_RQ), lane_blk(C_RK), lane_blk(C_RV), lane_blk(C_RG),
                  seq0, seq0, pl.BlockSpec((1, LANES), lambda b, h: (0, h))],
        out_specs=[seq0] * 4 + [pl.BlockSpec((ST_ROWS, LANES), lambda b, h: (b, h))],
        out_shape=[SDS((t, RET_W), BF16)] * 4 + [SDS((b_loc * ST_ROWS, RET_W), F32)],
        scratch_shapes=[state] * 4 + [pltpu.VMEM((s, LANES), BF16)],
        compiler_params=_params(("parallel", "parallel")),
    )(lgf, lgb, u, u, u, u, y_pre, dr, gn_gain)


GW = GROUP * HEAD_DIM
KEYS = 3 * BLOCK


def _attn_tables(g, bias_ref):
    r = lax.broadcasted_iota(jnp.int32, (GROUP * BLOCK, KEYS), 0)
    kj = lax.broadcasted_iota(jnp.int32, (GROUP * BLOCK, KEYS), 1)
    qi = r & (BLOCK - 1)
    hh = lax.shift_right_logical(r, 7)
    dist = jnp.abs(kj - BLOCK - qi)
    slope = jnp.exp2(-(GROUP * g + hh + 1).astype(F32) * (8.0 / ATTN_HEADS))
    bias_ref[...] = jnp.where(dist <= BLOCK, -slope * dist.astype(F32), NEG_INF)


def _tile_keys(x_ref, g, scale, pad_ref, s):
    r = lax.broadcasted_iota(jnp.int32, (LANES, GW), 0)
    c = lax.broadcasted_iota(jnp.int32, (LANES, GW), 1)
    place = jnp.where(r == g * HEAD_DIM + (c & (HEAD_DIM - 1)), 1.0, 0.0).astype(BF16)
    pad_ref[0:BLOCK, :] = jnp.zeros((BLOCK, GW), BF16)
    pad_ref[BLOCK + s:2 * BLOCK + s, :] = jnp.zeros((BLOCK, GW), BF16)
    pad_ref[BLOCK:BLOCK + s, :] = (_dot_nn(x_ref[...], place) * scale).astype(BF16)


def _stack_heads(x):
    lane_h = lax.shift_right_logical(lax.broadcasted_iota(jnp.int32, (1, GW), 1), 6)
    zero = jnp.zeros_like(x)
    return jnp.concatenate([jnp.where(lane_h == h, x, zero) for h in range(GROUP)], axis=0)


def _unstack_heads(x4):
    lane_h = lax.shift_right_logical(lax.broadcasted_iota(jnp.int32, (1, GW), 1), 6)
    out = jnp.zeros((BLOCK, GW), F32)
    for h in range(GROUP):
        out = out + jnp.where(lane_h == h, x4[h * BLOCK:(h + 1) * BLOCK, :], 0.0)
    return out


def _sink_column(sink_ref, g):
    rh = lax.shift_right_logical(lax.broadcasted_iota(jnp.int32, (GROUP * BLOCK, 1), 0), 7)
    col = jnp.zeros((GROUP * BLOCK, 1), F32)
    for h in range(GROUP):
        col = jnp.where(rh == h, sink_ref[GROUP * g + h], col)
    return col


def _attn_probs(qm, k3, bias_ref, sink_col, n, s):
    logits = _dot_nt(qm, k3) + bias_ref[...]
    kpos = n * BLOCK - BLOCK + lax.broadcasted_iota(jnp.int32, (1, KEYS), 1)
    logits = jnp.where((kpos >= 0) & (kpos < s), logits, NEG_INF)
    m = jnp.maximum(jnp.max(logits, axis=1, keepdims=True), sink_col)
    e = jnp.exp(logits - m)
    e_sink = jnp.exp(sink_col - m)
    inv = 1.0 / (jnp.sum(e, axis=1, keepdims=True) + e_sink)
    return e * inv, e_sink * inv


def _attn_specs(s, n_blocks):
    q_spec = pl.BlockSpec((BLOCK, GW), lambda b, g, n: (b * n_blocks + n, C_AQ // GW + g))
    k_spec = pl.BlockSpec((s, LANES), lambda b, g, n: (b, C_AK // LANES))
    v_spec = pl.BlockSpec((s, LANES), lambda b, g, n: (b, C_AV // LANES))
    o_spec = pl.BlockSpec((BLOCK, GW), lambda b, g, n: (b * n_blocks + n, g))
    return q_spec, k_spec, v_spec, o_spec


def _attention_fwd(u, sink, b_loc):
    t = u.shape[0]
    s = t // b_loc
    n_blocks = s // BLOCK

    def body(sink_ref, q_ref, k_ref, v_ref, o_ref, kpad, vpad, bias):
        g, n = pl.program_id(1), pl.program_id(2)

        @pl.when(n == 0)
        def _():
            _attn_tables(g, bias)
            _tile_keys(k_ref, g, Q_SCALE, kpad, s)
            _tile_keys(v_ref, g, 1.0, vpad, s)

        keys = pl.ds(pl.multiple_of(n * BLOCK, BLOCK), KEYS)
        p, _ = _attn_probs(_stack_heads(q_ref[...]), kpad[keys, :], bias, _sink_column(sink_ref, g), n, s)
        o_ref[...] = _unstack_heads(_dot_nn(p.astype(BF16), vpad[keys, :])).astype(BF16)

    q_spec, k_spec, v_spec, o_spec = _attn_specs(s, n_blocks)
    pad = pltpu.VMEM((s + 2 * BLOCK, GW), BF16)
    return pl.pallas_call(
        body, name="attention_fwd", grid=(b_loc, KV_HEADS, n_blocks),
        in_specs=[_smem_spec(), q_spec, k_spec, v_spec], out_specs=o_spec,
        out_shape=SDS((t, ATT_W), BF16),
        scratch_shapes=[pad, pad, pltpu.VMEM((GROUP * BLOCK, KEYS), F32)],
        compiler_params=_params(("parallel", "arbitrary", "arbitrary")),
    )(sink, u, u, u)


def _fold_groups(x, g):
    x = x + pltpu.roll(x, 2 * HEAD_DIM, 1)
    x = x + pltpu.roll(x, HEAD_DIM, 1)
    lane_g = lax.shift_right_logical(lax.broadcasted_iota(jnp.int32, (1, LANES), 1), 6)
    return jnp.where(lane_g == g, x[:, 0:LANES], 0.0)


def _attention_bwd(u, da, sink, b_loc):
    t = u.shape[0]
    s = t // b_loc
    n_blocks = s // BLOCK

    def body(sink_ref, q_ref, k_ref, v_ref, do_ref, dq_ref, dk_ref, dv_ref, dsink_ref, kpad, vpad, bias, dk_acc, dv_acc):
        g, n = pl.program_id(1), pl.program_id(2)

        @pl.when(n == 0)
        def _():
            _attn_tables(g, bias)
            _tile_keys(k_ref, g, Q_SCALE, kpad, s)
            _tile_keys(v_ref, g, 1.0, vpad, s)
            dsink_ref[...] = jnp.zeros_like(dsink_ref)

        @pl.when((n == 0) & (g == 0))
        def _():
            dk_acc[...] = jnp.zeros_like(dk_acc)
            dv_acc[...] = jnp.zeros_like(dv_acc)

        keys = pl.ds(pl.multiple_of(n * BLOCK, BLOCK), KEYS)
        qm = _stack_heads(q_ref[...])
        k3, v3 = kpad[keys, :], vpad[keys, :]
        p, p_sink = _attn_probs(qm, k3, bias, _sink_column(sink_ref, g), n, s)
        dom = _stack_heads(do_ref[...])
        dp = _dot_nt(dom, v3)
        delta = jnp.sum(p * dp, axis=1, keepdims=True)
        ds_mat = (p * (dp - delta)).astype(BF16)
        dq_ref[...] = _unstack_heads(_dot_nn(ds_mat, k3)).astype(BF16)
        dk_acc[keys, :] += _fold_groups(_dot_tn(ds_mat, qm), g) * Q_SCALE
        dv_acc[keys, :] += _fold_groups(_dot_tn(p.astype(BF16), dom), g)
        w = p_sink * delta
        rows = lax.broadcasted_iota(jnp.int32, dsink_ref.shape, 0)
        upd = jnp.zeros(dsink_ref.shape, F32)
        for h in range(GROUP):
            upd = jnp.where(rows == h, -jnp.sum(w[h * BLOCK:(h + 1) * BLOCK, :]), upd)
        dsink_ref[...] += upd

        @pl.when((n == n_blocks - 1) & (g == KV_HEADS - 1))
        def _():
            dk_ref[...] = dk_acc[BLOCK:BLOCK + s, :].astype(BF16)
            dv_ref[...] = dv_acc[BLOCK:BLOCK + s, :].astype(BF16)

    q_spec, k_spec, v_spec, o_spec = _attn_specs(s, n_blocks)
    kv_out = pl.BlockSpec((s, LANES), lambda b, g, n: (b, 0))
    pad = pltpu.VMEM((s + 2 * BLOCK, GW), BF16)
    acc = pltpu.VMEM((s + 2 * BLOCK, LANES), F32)
    return pl.pallas_call(
        body, name="attention_bwd", grid=(b_loc, KV_HEADS, n_blocks),
        in_specs=[_smem_spec(), q_spec, k_spec, v_spec, o_spec],
        out_specs=[o_spec, kv_out, kv_out, pl.BlockSpec((8, LANES), lambda b, g, n: (b * KV_HEADS + g, 0))],
        out_shape=[SDS((t, ATT_W), BF16), SDS((t, KV_W), BF16), SDS((t, KV_W), BF16),
                   SDS((b_loc * KV_HEADS * 8, LANES), F32)],
        scratch_shapes=[pad, pad, pltpu.VMEM((GROUP * BLOCK, KEYS), F32), acc, acc],
        compiler_params=_params(("arbitrary", "arbitrary", "arbitrary")),
    )(sink, u, u, u, da)


def _pack_small(acc2, acc1, ret_stats, dsink, b_loc, d):
    pairs = RET_HEADS // 2

    def body(acc2_ref, acc1_ref, st_ref, dsink_ref, out_ref):
        out_ref[...] = jnp.zeros_like(out_ref)
        out_ref[ROW_LN1G:ROW_LN1G + 1, :] = acc1_ref[0:1, :]
        out_ref[ROW_LN1B:ROW_LN1B + 1, :] = acc1_ref[1:2, :]
        out_ref[ROW_LN2G:ROW_LN2G + 1, :] = acc2_ref[1:2, :]
        out_ref[ROW_LN2B:ROW_LN2B + 1, :] = acc2_ref[2:3, :]
        out_ref[ROW_LOSS:ROW_LOSS + 1, :] = acc2_ref[0:1, :]
        st = st_ref[0:ST_ROWS, :]
        for b in range(1, b_loc):
            st = st + st_ref[b * ST_ROWS:(b + 1) * ST_ROWS, :]
        out_ref[ROW_GN:ROW_GN + 1, 0:RET_W] = st[ST_GAIN:ST_GAIN + 1, :]
        lane = lax.broadcasted_iota(jnp.int32, (1, d), 1)
        misc = jnp.zeros((1, d), F32)
        for pr in range(pairs):
            blk = st[:, pr * LANES:(pr + 1) * LANES]
            half = lax.broadcasted_iota(jnp.int32, (1, LANES), 1) < HEAD_DIM
            for h in range(2):
                sel = half if h == 0 else jnp.logical_not(half)
                cross_f = jnp.sum(jnp.where(sel, blk[ST_XF:ST_XF + 1, :] + blk[ST_LF:ST_LF + 1, :], 0.0))
                cross_b = jnp.sum(jnp.where(sel, blk[ST_XB:ST_XB + 1, :] + blk[ST_LB:ST_LB + 1, :], 0.0))
                intra_f = jnp.sum(blk[ST_IFA + h:ST_IFA + h + 1, :])
                intra_b = jnp.sum(blk[ST_IBA + h:ST_IBA + h + 1, :])
                head = 2 * pr + h
                misc = jnp.where(lane == MISC_DF + head, cross_f + intra_f, misc)
                misc = jnp.where(lane == MISC_DB + head, cross_b + intra_b, misc)
        for g in range(KV_HEADS):
            tot = dsink_ref[g * 8:(g + 1) * 8, :]
            for b in range(1, b_loc):
                tot = tot + dsink_ref[(b * KV_HEADS + g) * 8:(b * KV_HEADS + g + 1) * 8, :]
            for h in range(GROUP):
                misc = jnp.where(lane == MISC_SINK + GROUP * g + h, jnp.sum(tot[h:h + 1, 0:1]), misc)
        out_ref[ROW_MISC:ROW_MISC + 1, :] = misc

    return pl.pallas_call(body, name="pack_small", out_shape=SDS((SMALL_ROWS, d), F32))(acc2, acc1, ret_stats, dsink)


def _pack_rows(d, ln1g, ln1b, ln2g, ln2b, gn, df, db, sink):
    pad = lambda a: jnp.pad(a, ((0, 0), (0, d - a.shape[1])))
    misc = jnp.concatenate([df, db, sink], axis=1)
    rows = [ln1g, ln1b, ln2g, ln2b, jnp.zeros((1, d), F32), pad(gn), pad(misc)]
    return jnp.concatenate(rows + [jnp.zeros((SMALL_ROWS - len(rows), d), F32)], axis=0)


def _unpack_rows(a):
    misc = a[ROW_MISC:ROW_MISC + 1]
    return dict(
        ln1_gain=a[ROW_LN1G:ROW_LN1G + 1], ln1_bias=a[ROW_LN1B:ROW_LN1B + 1], ln2_gain=a[ROW_LN2G:ROW_LN2G + 1],
        ln2_bias=a[ROW_LN2B:ROW_LN2B + 1], ret_gn_gain=a[ROW_GN:ROW_GN + 1, 0:RET_W],
        ret_decay_fwd=misc[:, MISC_DF:MISC_DF + RET_HEADS], ret_decay_bwd=misc[:, MISC_DB:MISC_DB + RET_HEADS],
        attn_sink=misc[:, MISC_SINK:MISC_SINK + ATTN_HEADS])


BIG = ("w_in", "w_out", "w_ffn_gate", "w_ffn_up", "w_ffn_down", "w_ple_proj", "w_ple_gate")
TRANSPOSED = ("w_in", "w_ffn_gate", "w_ffn_up", "w_ple_proj")
SMALL = ("ret_decay_fwd", "ret_decay_bwd", "ret_gn_gain", "attn_sink", "ln1_gain", "ln1_bias", "ln2_gain", "ln2_bias")
ORDER = ("w_in", "ret_decay_fwd", "ret_decay_bwd", "ret_gn_gain", "attn_sink", "w_out", "ln1_gain", "ln1_bias",
         "w_ffn_gate", "w_ffn_up", "w_ffn_down", "w_ple_proj", "w_ple_gate", "ln2_gain", "ln2_bias")


def _local_step(x2, p2, target2, full, small, b_loc, me):
    d = x2.shape[1]
    lgf, lgb = _log_decay(small["ret_decay_fwd"], small["ret_decay_bwd"])
    lgf1, lgb1, sink1 = lgf.reshape(-1), lgb.reshape(-1), small["attn_sink"].reshape(-1)
    u, xb = _in_proj(x2, full["w_in"])
    r, y_pre = _retention_fwd(u, lgf1, lgb1, small["ret_gn_gain"], b_loc)
    a = _attention_fwd(u, sink1, b_loc)
    z1, h1b = _out_proj_ln1(r, a, x2, full["w_out"], small["ln1_gain"], small["ln1_bias"])
    g, up, act = _ffn_up(h1b, full["w_ffn_gate"], full["w_ffn_up"])
    dz2, dz2b, dsb, dpleb, acc2 = _ffn_down_ln2_loss(
        act, h1b, p2, z1, target2, full["w_ffn_down"], full["w_ple_gate"], full["w_ple_proj"],
        small["ln1_gain"], small["ln1_bias"], small["ln2_gain"], small["ln2_bias"])
    grads = {}
    grads["w_ffn_down"] = _weight_grad("grad_w_ffn_down", me, [act], dz2b)
    grads["w_ple_proj"] = _weight_grad("grad_w_ple_proj", me, [dpleb], p2)
    grads["w_ple_gate"] = _weight_grad("grad_w_ple_gate", me, [h1b], dsb)
    dg, dup = _ffn_bwd_act(dz2b, g, up, full["w_ffn_down"])
    grads["w_ffn_gate"] = _weight_grad("grad_w_ffn_gate", me, [dg], h1b)
    grads["w_ffn_up"] = _weight_grad("grad_w_ffn_up", me, [dup], h1b)
    dz1, dz1b, acc1 = _dh1_ln1_bwd(dz2, dg, dup, dsb, z1, full["w_ffn_gate"], full["w_ffn_up"], full["w_ple_gate"],
                                   small["ln1_gain"])
    grads["w_out"] = _weight_grad("grad_w_out", me, [r, a], dz1b)
    dr, da = _out_proj_bwd(dz1b, full["w_out"])
    dq, dk, dv, dgate, ret_stats = _retention_bwd(u, y_pre, dr, lgf1, lgb1, small["ret_gn_gain"], b_loc)
    daq, dak, dav, dsink = _attention_bwd(u, da, sink1, b_loc)
    parts = [dq, dk, dv, dgate, daq, dak, dav]
    grads["w_in"] = _weight_grad("grad_w_in", me, parts, xb)
    grad_x = _in_proj_bwd(dz1, parts, full["w_in"])
    small_part = _pack_small(acc2, acc1, ret_stats, dsink, b_loc, d)
    return grad_x, grads, small_part


def kernel(x, p, w_in, ret_decay_fwd, ret_decay_bwd, ret_gn_gain, attn_sink, w_out, ln1_gain, ln1_bias, w_ffn_gate, w_ffn_up, w_ffn_down, w_ple_proj, w_ple_gate, ln2_gain, ln2_bias, loss_target, m_w_in, m_ret_decay_fwd, m_ret_decay_bwd, m_ret_gn_gain, m_attn_sink, m_w_out, m_ln1_gain, m_ln1_bias, m_w_ffn_gate, m_w_ffn_up, m_w_ffn_down, m_w_ple_proj, m_w_ple_gate, m_ln2_gain, m_ln2_bias, v_w_in, v_ret_decay_fwd, v_ret_decay_bwd, v_ret_gn_gain, v_attn_sink, v_w_out, v_ln1_gain, v_ln1_bias, v_w_ffn_gate, v_w_ffn_up, v_w_ffn_down, v_w_ple_proj, v_w_ple_gate, v_ln2_gain, v_ln2_bias):
    given = dict(locals())
    strip = lambda n, a: a[0] if n in BIG else a
    w = {n: strip(n, given[n]) for n in ORDER}
    m = {n: strip(n, given["m_" + n]) for n in ORDER}
    v = {n: strip(n, given["v_" + n]) for n in ORDER}
    b_loc, s, d = x.shape
    x2 = x.reshape(b_loc * s, d)
    p2 = p[0].reshape(b_loc * s, p.shape[-1])
    target2 = loss_target.reshape(b_loc * s, d)

    shards = _prep_shards(*[w[n] for n in BIG])
    full = dict(zip(BIG, _all_gather(shards)))
    small = {n: w[n] for n in SMALL}
    me = (4 * lax.axis_index("x") + 2 * lax.axis_index("y") + lax.axis_index("c")).astype(jnp.int32).reshape(1)
    grad_x, grads, small_part = _local_step(x2, p2, target2, full, small, b_loc, me)

    received = _scatter_rows([grads[n][0] for n in BIG])
    out_g, out_d, out_m, out_v = {}, {}, {}, {}
    for n, recv in zip(BIG, received):
        out_g[n], out_d[n], out_m[n], out_v[n] = _reduce_adamw(grads[n][1], recv, w[n], m[n], v[n], n in TRANSPOSED)

    pack = lambda src: _pack_rows(d, src["ln1_gain"], src["ln1_bias"], src["ln2_gain"], src["ln2_bias"],
                                  src["ret_gn_gain"], src["ret_decay_fwd"], src["ret_decay_bwd"], src["attn_sink"])
    sg, sd, sm, sv = _small_all_reduce_adamw(small_part, pack(w), pack(m), pack(v), pack(w))
    loss = sg[ROW_LOSS, 0]
    for dst, src in ((out_g, sg), (out_d, sd), (out_m, sm), (out_v, sv)):
        dst.update(_unpack_rows(src))

    outs = [loss, grad_x.reshape(x.shape)]
    for group in (out_g, out_d, out_m, out_v):
        outs += [group[n][None] if n in BIG else group[n] for n in ORDER]
    return tuple(outs)
```

```python
import functools

import jax
import jax.numpy as jnp
from jax import lax
from jax.experimental import pallas as pl
from jax.experimental.pallas import tpu as pltpu

F32, BF16 = jnp.float32, jnp.bfloat16
SDS = jax.ShapeDtypeStruct
MESH = pl.DeviceIdType.MESH

N_DEV = 8
HEAD_DIM = 64
RET_HEADS = 8
ATTN_HEADS = 8
KV_HEADS = 2
GROUP = ATTN_HEADS // KV_HEADS
RET_W = RET_HEADS * HEAD_DIM
ATT_W = ATTN_HEADS * HEAD_DIM
KV_W = KV_HEADS * HEAD_DIM
LANES = 128
CHUNK = 128
BLOCK = 128
Q_SCALE = HEAD_DIM ** -0.5
ALPHA = 2.0 ** 0.25
LN_EPS = 1e-5
GN_EPS = 1e-5
NEG_INF = -1e30
C_RQ, C_RK, C_RV, C_RG = 0, RET_W, 2 * RET_W, 3 * RET_W
C_AQ = 4 * RET_W
C_AK = C_AQ + ATT_W
C_AV = C_AK + KV_W
IN_W = C_AV + KV_W

ADAM_LR = 0.001
ADAM_B1 = 0.9
ADAM_B2 = 0.999
ADAM_EPS = 1e-08
ADAM_WD = 0.01
ADAM_STEP = 10

VMEM_LIMIT = 48 * 1024 * 1024
SMALL_ROWS = 16
ROW_LN1G, ROW_LN1B, ROW_LN2G, ROW_LN2B, ROW_LOSS, ROW_GN, ROW_MISC = 0, 1, 2, 3, 4, 5, 6
MISC_DF, MISC_DB, MISC_SINK = 0, 8, 16


def _dot_nn(a, b):
    return lax.dot_general(a, b, (((1,), (0,)), ((), ())), preferred_element_type=F32)


def _dot_nt(a, b):
    return lax.dot_general(a, b, (((1,), (1,)), ((), ())), preferred_element_type=F32)


def _dot_tn(a, b):
    return lax.dot_general(a, b, (((0,), (0,)), ((), ())), preferred_element_type=F32)


def _params(sem=None, vmem=VMEM_LIMIT):
    kw = {"vmem_limit_bytes": vmem}
    if sem is not None:
        kw["dimension_semantics"] = sem
    return pltpu.CompilerParams(**kw)


def _row_tile(t, want=512):
    tm = want
    while t % tm:
        tm //= 2
    return tm


def _sigmoid(x):
    return 1.0 / (1.0 + jnp.exp(-x))


def _layer_norm_stats(z):
    mu = jnp.mean(z, axis=1, keepdims=True)
    d = z - mu
    var = jnp.mean(d * d, axis=1, keepdims=True)
    rstd = lax.rsqrt(var + LN_EPS)
    return d * rstd, rstd


def _layer_norm_bwd(dxh, xhat, rstd):
    m1 = jnp.mean(dxh, axis=1, keepdims=True)
    m2 = jnp.mean(dxh * xhat, axis=1, keepdims=True)
    return rstd * (dxh - m1 - xhat * m2)


def _prep_shards(shards):
    names = list(shards)

    def body(*refs):
        for name, src, dst in zip(names, refs[:len(names)], refs[len(names):]):
            val = src[...]
            dst[...] = (val.T if name in TRANSPOSED_HERE else val).astype(BF16)

    shape = lambda n, a: SDS(a.shape[::-1] if n in TRANSPOSED_HERE else a.shape, BF16)
    out = pl.pallas_call(
        body, name="prep_shards", out_shape=[shape(n, shards[n]) for n in names], compiler_params=_params(),
    )(*[shards[n] for n in names])
    return dict(zip(names, out))


def _mesh_pos():
    return lax.axis_index("x"), lax.axis_index("y"), lax.axis_index("c")


def _all_gather(shards):
    n = len(shards)

    def body(*refs):
        ins, outs = refs[:n], refs[n:2 * n]
        send_sems, recv_sems, local_sems = refs[2 * n:]
        x, y, c = _mesh_pos()
        me, sibling = (x, y, c), (x, y, 1 - c)
        chips = [(1 - x, y), (x, 1 - y), (1 - x, 1 - y)]

        def rows(i, pos):
            r = ins[i].shape[0]
            px, py, pc = pos
            return outs[i].at[pl.ds(pl.multiple_of((4 * px + 2 * py + pc) * r, 8), r), :]

        def copy(i, k, block, to, src=None):
            return pltpu.make_async_remote_copy(
                src_ref=rows(i, block) if src is None else src, dst_ref=rows(i, block),
                send_sem=send_sems.at[i, k], recv_sem=recv_sems.at[i, k], device_id=to, device_id_type=MESH)

        mine = [pltpu.make_async_copy(ins[i], rows(i, me), local_sems.at[i]) for i in range(n)]
        for cp in mine:
            cp.start()
        first = []
        for i in range(n):
            first.append(copy(i, 0, me, sibling, src=ins[i]))
            first += [copy(i, 1 + j, me, (*chip, c), src=ins[i]) for j, chip in enumerate(chips)]
        for cp in first:
            cp.start()
        passed = []
        for j, chip in enumerate(chips):
            for i in range(n):
                copy(i, 1 + j, (*chip, c), me).wait_recv()
                fwd = copy(i, 4 + j, (*chip, c), sibling)
                fwd.start()
                passed.append(fwd)
        for i in range(n):
            copy(i, 0, sibling, me).wait_recv()
            for j, chip in enumerate(chips):
                copy(i, 4 + j, (*chip, 1 - c), me).wait_recv()
        for cp in first + passed:
            cp.wait_send()
        for cp in mine:
            cp.wait()

    any_spec = pl.BlockSpec(memory_space=pl.ANY)
    return pl.pallas_call(
        body, name="all_gather_weights",
        out_shape=[SDS((N_DEV * s.shape[0], s.shape[1]), s.dtype) for s in shards],
        in_specs=[any_spec] * n, out_specs=[any_spec] * n,
        scratch_shapes=[pltpu.SemaphoreType.DMA((n, 7)), pltpu.SemaphoreType.DMA((n, 7)),
                        pltpu.SemaphoreType.DMA((n,))],
    )(*shards)


def _peer(m):
    x, y, c = _mesh_pos()
    bx, by, bc = (m >> 2) & 1, (m >> 1) & 1, m & 1
    return (x ^ bx if bx else x, y ^ by if by else y, c ^ bc if bc else c)


def _peer_index(m):
    x, y, c = _mesh_pos()
    return (4 * x + 2 * y + c) ^ m


def _scatter_rows(partials):
    n = len(partials)

    def body(*refs):
        ins, outs = refs[:n], refs[n:2 * n]
        send_sems, recv_sems = refs[2 * n:]

        def copy(i, m):
            r = ins[i].shape[0] // N_DEV
            src = ins[i].at[pl.ds(pl.multiple_of(_peer_index(m) * r, 8), r), :]
            return pltpu.make_async_remote_copy(
                src_ref=src, dst_ref=outs[i].at[m], send_sem=send_sems.at[i, m - 1],
                recv_sem=recv_sems.at[i, m - 1], device_id=_peer(m), device_id_type=MESH)

        cps = [copy(i, m) for i in range(n) for m in range(1, N_DEV)]
        for cp in cps:
            cp.start()
        for cp in cps:
            cp.wait_recv()
        for cp in cps:
            cp.wait_send()

    any_spec = pl.BlockSpec(memory_space=pl.ANY)
    return pl.pallas_call(
        body, name="scatter_weight_grads",
        out_shape=[SDS((N_DEV, p.shape[0] // N_DEV, p.shape[1]), p.dtype) for p in partials],
        in_specs=[any_spec] * n, out_specs=[any_spec] * n,
        scratch_shapes=[pltpu.SemaphoreType.DMA((n, 7)), pltpu.SemaphoreType.DMA((n, 7))],
    )(*partials)


SMALL_PLACE = {
    "ln1_gain": (ROW_LN1G, 0), "ln1_bias": (ROW_LN1B, 0), "ln2_gain": (ROW_LN2G, 0), "ln2_bias": (ROW_LN2B, 0),
    "ret_gn_gain": (ROW_GN, 0), "ret_decay_fwd": (ROW_MISC, MISC_DF), "ret_decay_bwd": (ROW_MISC, MISC_DB),
    "attn_sink": (ROW_MISC, MISC_SINK)}


def _small_all_reduce_adamw(part, w, m, v):
    d = part.shape[1]
    names = list(SMALL_PLACE)
    k = len(names)

    def body(*refs):
        part_ref = refs[0]
        w_refs, m_refs, v_refs = refs[1:1 + k], refs[1 + k:1 + 2 * k], refs[1 + 2 * k:1 + 3 * k]
        outs = refs[1 + 3 * k:2 + 7 * k]
        loss_ref, g_refs, dl_refs = outs[0], outs[1:1 + k], outs[1 + k:1 + 2 * k]
        nm_refs, nv_refs = outs[1 + 2 * k:1 + 3 * k], outs[1 + 3 * k:1 + 4 * k]
        buf, send_sems, recv_sems = refs[2 + 7 * k:]
        buf[0] = part_ref[...]
        cps = [pltpu.make_async_remote_copy(
            src_ref=part_ref, dst_ref=buf.at[j], send_sem=send_sems.at[j - 1], recv_sem=recv_sems.at[j - 1],
            device_id=_peer(j), device_id_type=MESH) for j in range(1, N_DEV)]
        for cp in cps:
            cp.start()
        for cp in cps:
            cp.wait_recv()
        for cp in cps:
            cp.wait_send()
        me = _peer_index(0)
        tot = buf[me]
        for dev in range(1, N_DEV):
            tot = tot + buf[dev ^ me]
        loss_ref[...] = (0.5 / d) * jnp.sum(tot[ROW_LOSS:ROW_LOSS + 1, :], axis=1, keepdims=True)
        for i, name in enumerate(names):
            row, lo = SMALL_PLACE[name]
            wv = w_refs[i][...]
            g = tot[row:row + 1, lo:lo + wv.shape[1]]
            if name.startswith("ret_decay"):
                p2 = jnp.exp2(wv)
                g = g * (-p2 * jnp.log(2.0) / (1.0 - p2))
            g_refs[i][...] = g
            _adamw_store(g, wv, m_refs[i][...], v_refs[i][...], dl_refs[i], nm_refs[i], nv_refs[i])

    vm = pl.BlockSpec(memory_space=pltpu.VMEM)
    shapes = [SDS(w[n].shape, F32) for n in names]
    res = pl.pallas_call(
        body, name="small_all_reduce_adamw",
        out_shape=[SDS((1, 1), F32)] + shapes * 4, in_specs=[vm] * (1 + 3 * k), out_specs=[vm] * (1 + 4 * k),
        scratch_shapes=[pltpu.VMEM((N_DEV,) + part.shape, F32), pltpu.SemaphoreType.DMA((7,)),
                        pltpu.SemaphoreType.DMA((7,))],
    )(part, *[w[n] for n in names], *[m[n] for n in names], *[v[n] for n in names])
    groups = [dict(zip(names, res[1 + j * k:1 + (j + 1) * k])) for j in range(4)]
    return (res[0], *groups)


def _adamw_store(g, w, m, v, dl_ref, nm_ref, nv_ref):
    m = ADAM_B1 * m + (1.0 - ADAM_B1) * g
    v = ADAM_B2 * v + (1.0 - ADAM_B2) * (g * g)
    m_hat = m / (1.0 - ADAM_B1 ** ADAM_STEP)
    v_hat = v / (1.0 - ADAM_B2 ** ADAM_STEP)
    dl_ref[...] = -ADAM_LR * (m_hat / (jnp.sqrt(v_hat) + ADAM_EPS) + ADAM_WD * w)
    nm_ref[...] = m
    nv_ref[...] = v


def _reduce_adamw(own, recv, w, m, v, transposed):
    def body(own_ref, recv_ref, w_ref, m_ref, v_ref, g_ref, dl_ref, nm_ref, nv_ref):
        g = own_ref[...]
        for k in range(1, N_DEV):
            g = g + recv_ref[k].astype(F32)
        if transposed:
            g = g.T
        g_ref[...] = g
        _adamw_store(g, w_ref[...], m_ref[...], v_ref[...], dl_ref, nm_ref, nv_ref)

    out = SDS(w.shape, F32)
    return pl.pallas_call(body, name="reduce_adamw", out_shape=[out] * 4, compiler_params=_params())(own, recv, w, m, v)


def _row_spec(tm, width):
    return pl.BlockSpec((tm, width), lambda i: (i, 0))


def _full_spec(shape):
    return pl.BlockSpec(shape, lambda i: (0,) * len(shape))


def _in_proj(x2, wt_in):
    t, d = x2.shape
    u_w = wt_in.shape[0]
    tm = _row_tile(t)

    def body(x_ref, w_ref, u_ref, xb_ref):
        xb = x_ref[...].astype(BF16)
        xb_ref[...] = xb
        u_ref[...] = _dot_nt(xb, w_ref[...]).astype(BF16)

    return pl.pallas_call(
        body, name="in_proj", grid=(t // tm,),
        in_specs=[_row_spec(tm, d), _full_spec(wt_in.shape)],
        out_specs=[_row_spec(tm, u_w), _row_spec(tm, d)],
        out_shape=[SDS((t, u_w), BF16), SDS((t, d), BF16)],
        compiler_params=_params(("parallel",)),
    )(x2, wt_in)


def _out_proj_ln1(r, a, x2, w_out, g1, b1):
    t, d = x2.shape
    tm = _row_tile(t)

    def body(r_ref, a_ref, x_ref, wo_ref, g_ref, b_ref, z_ref, hb_ref):
        mix = _dot_nn(r_ref[...], wo_ref[0:RET_W, :]) + _dot_nn(a_ref[...], wo_ref[RET_W:RET_W + ATT_W, :])
        z = ALPHA * x_ref[...] + mix
        xhat, _ = _layer_norm_stats(z)
        z_ref[...] = z
        hb_ref[...] = (xhat * g_ref[...] + b_ref[...]).astype(BF16)

    return pl.pallas_call(
        body, name="out_proj_ln1", grid=(t // tm,),
        in_specs=[_row_spec(tm, RET_W), _row_spec(tm, ATT_W), _row_spec(tm, d), _full_spec(w_out.shape),
                  _full_spec(g1.shape), _full_spec(b1.shape)],
        out_specs=[_row_spec(tm, d), _row_spec(tm, d)],
        out_shape=[SDS((t, d), F32), SDS((t, d), BF16)],
        compiler_params=_params(("parallel",)),
    )(r, a, x2, w_out, g1, b1)


def _col_halves(f):
    n = f // LANES
    k = (n + 1) // 2 * LANES
    return [(0, k), (k, f)] if k < f else [(0, f)]


def _ffn_up(h1b, wt_gate, wt_up):
    t, d = h1b.shape
    f = wt_gate.shape[0]
    tm = _row_tile(t, 256)

    def body(h_ref, wg_ref, wu_ref, g_ref, u_ref, act_ref):
        h = h_ref[...]
        for lo, hi in _col_halves(f):
            g = _dot_nt(h, wg_ref[lo:hi, :])
            u = _dot_nt(h, wu_ref[lo:hi, :])
            g_ref[:, lo:hi] = g.astype(BF16)
            u_ref[:, lo:hi] = u.astype(BF16)
            act_ref[:, lo:hi] = (g * _sigmoid(g) * u).astype(BF16)

    return pl.pallas_call(
        body, name="ffn_up", grid=(t // tm,),
        in_specs=[_row_spec(tm, d), _full_spec(wt_gate.shape), _full_spec(wt_up.shape)],
        out_specs=[_row_spec(tm, f)] * 3,
        out_shape=[SDS((t, f), BF16)] * 3,
        compiler_params=_params(("parallel",)),
    )(h1b, wt_gate, wt_up)


def _ffn_down_ln2_loss(act, h1b, p2, z1, target, w_down, w_pg, wt_pe, g1, b1, g2, b2):
    t, d = z1.shape
    f = act.shape[1]
    pdim = p2.shape[1]
    tm = _row_tile(t, 256)

    def body(act_ref, hb_ref, p_ref, z1_ref, tgt_ref, wd_ref, wpg_ref, wpe_ref, g1_ref, b1_ref, g2_ref, b2_ref,
             dz_ref, dzb_ref, ds_ref, dple_ref, acc_ref):
        @pl.when(pl.program_id(0) == 0)
        def _():
            acc_ref[...] = jnp.zeros_like(acc_ref)

        xhat1, _ = _layer_norm_stats(z1_ref[...])
        h1 = xhat1 * g1_ref[...] + b1_ref[...]
        ffn = _dot_nn(act_ref[...], wd_ref[...])
        pg = _sigmoid(_dot_nn(hb_ref[...], wpg_ref[...]))
        ple = _dot_nt(p_ref[...].astype(BF16), wpe_ref[...])
        z2 = ALPHA * h1 + ffn + pg * ple
        xhat2, rstd2 = _layer_norm_stats(z2)
        err = xhat2 * g2_ref[...] + b2_ref[...] - tgt_ref[...]
        dy = err * (1.0 / d)
        dz = _layer_norm_bwd(dy * g2_ref[...], xhat2, rstd2)
        dz_ref[...] = dz
        dzb_ref[...] = dz.astype(BF16)
        ds_ref[...] = (dz * ple * pg * (1.0 - pg)).astype(BF16)
        dple_ref[...] = (dz * pg).astype(BF16)
        acc_ref[0:1, :] += jnp.sum(err * err, axis=0, keepdims=True)
        acc_ref[1:2, :] += jnp.sum(dy * xhat2, axis=0, keepdims=True)
        acc_ref[2:3, :] += jnp.sum(dy, axis=0, keepdims=True)

    vec = _full_spec(g1.shape)
    return pl.pallas_call(
        body, name="ffn_down_ln2_loss", grid=(t // tm,),
        in_specs=[_row_spec(tm, f), _row_spec(tm, d), _row_spec(tm, pdim), _row_spec(tm, d), _row_spec(tm, d),
                  _full_spec(w_down.shape), _full_spec(w_pg.shape), _full_spec(wt_pe.shape), vec, vec, vec, vec],
        out_specs=[_row_spec(tm, d)] * 4 + [_full_spec((8, d))],
        out_shape=[SDS((t, d), F32), SDS((t, d), BF16), SDS((t, d), BF16), SDS((t, d), BF16), SDS((8, d), F32)],
        compiler_params=_params(("arbitrary",)),
    )(act, h1b, p2, z1, target, w_down, w_pg, wt_pe, g1, b1, g2, b2)


def _ffn_bwd_act(dzb, g, up, w_down):
    t, d = dzb.shape
    f = g.shape[1]
    tm = _row_tile(t, 256)

    def body(dz_ref, g_ref, u_ref, wd_ref, dg_ref, du_ref):
        dz = dz_ref[...]
        for lo, hi in _col_halves(f):
            da = _dot_nt(dz, wd_ref[lo:hi, :])
            gv = g_ref[:, lo:hi].astype(F32)
            sg = _sigmoid(gv)
            dg_ref[:, lo:hi] = (da * u_ref[:, lo:hi].astype(F32) * (sg * (1.0 + gv * (1.0 - sg)))).astype(BF16)
            du_ref[:, lo:hi] = (da * gv * sg).astype(BF16)

    return pl.pallas_call(
        body, name="ffn_bwd_act", grid=(t // tm,),
        in_specs=[_row_spec(tm, d), _row_spec(tm, f), _row_spec(tm, f), _full_spec(w_down.shape)],
        out_specs=[_row_spec(tm, f)] * 2,
        out_shape=[SDS((t, f), BF16)] * 2,
        compiler_params=_params(("parallel",)),
    )(dzb, g, up, w_down)


def _dh1_ln1_bwd(dz2, dg, dup, dsb, z1, wt_gate, wt_up, w_pg, g1):
    t, d = dz2.shape
    f = dg.shape[1]
    tm = _row_tile(t, 256)

    def body(dz_ref, dg_ref, du_ref, ds_ref, z1_ref, wg_ref, wu_ref, wpg_ref, g1_ref, dz1_ref, dz1b_ref, acc_ref):
        @pl.when(pl.program_id(0) == 0)
        def _():
            acc_ref[...] = jnp.zeros_like(acc_ref)

        dh = (ALPHA * dz_ref[...] + _dot_nn(dg_ref[...], wg_ref[...]) + _dot_nn(du_ref[...], wu_ref[...])
              + _dot_nt(ds_ref[...], wpg_ref[...]))
        xhat, rstd = _layer_norm_stats(z1_ref[...])
        dz1 = _layer_norm_bwd(dh * g1_ref[...], xhat, rstd)
        dz1_ref[...] = dz1
        dz1b_ref[...] = dz1.astype(BF16)
        acc_ref[0:1, :] += jnp.sum(dh * xhat, axis=0, keepdims=True)
        acc_ref[1:2, :] += jnp.sum(dh, axis=0, keepdims=True)

    return pl.pallas_call(
        body, name="dh1_ln1_bwd", grid=(t // tm,),
        in_specs=[_row_spec(tm, d), _row_spec(tm, f), _row_spec(tm, f), _row_spec(tm, d), _row_spec(tm, d),
                  _full_spec(wt_gate.shape), _full_spec(wt_up.shape), _full_spec(w_pg.shape), _full_spec(g1.shape)],
        out_specs=[_row_spec(tm, d), _row_spec(tm, d), _full_spec((8, d))],
        out_shape=[SDS((t, d), F32), SDS((t, d), BF16), SDS((8, d), F32)],
        compiler_params=_params(("arbitrary",)),
    )(dz2, dg, dup, dsb, z1, wt_gate, wt_up, w_pg, g1)


def _out_proj_bwd(dz1b, w_out):
    t, d = dz1b.shape
    tm = _row_tile(t)

    def body(dz_ref, wo_ref, dr_ref, da_ref):
        dz = dz_ref[...]
        dr_ref[...] = _dot_nt(dz, wo_ref[0:RET_W, :]).astype(BF16)
        da_ref[...] = _dot_nt(dz, wo_ref[RET_W:RET_W + ATT_W, :]).astype(BF16)

    return pl.pallas_call(
        body, name="out_proj_bwd", grid=(t // tm,),
        in_specs=[_row_spec(tm, d), _full_spec(w_out.shape)],
        out_specs=[_row_spec(tm, RET_W), _row_spec(tm, ATT_W)],
        out_shape=[SDS((t, RET_W), BF16), SDS((t, ATT_W), BF16)],
        compiler_params=_params(("parallel",)),
    )(dz1b, w_out)


def _in_proj_bwd(dz1, parts, wt_in):
    t, d = dz1.shape
    tm = _row_tile(t)
    widths = [p.shape[1] for p in parts]

    def body(*refs):
        dz_ref, part_refs, w_ref, dx_ref = refs[0], refs[1:1 + len(parts)], refs[-2], refs[-1]
        acc = ALPHA * dz_ref[...]
        lo = 0
        for p_ref, w in zip(part_refs, widths):
            acc = acc + _dot_nn(p_ref[...], w_ref[lo:lo + w, :])
            lo += w
        dx_ref[...] = acc

    return pl.pallas_call(
        body, name="in_proj_bwd", grid=(t // tm,),
        in_specs=[_row_spec(tm, d)] + [_row_spec(tm, w) for w in widths] + [_full_spec(wt_in.shape)],
        out_specs=_row_spec(tm, d), out_shape=SDS((t, d), F32),
        compiler_params=_params(("parallel",)),
    )(dz1, *parts, wt_in)


def _weight_grad(name, me, parts, rhs):
    t, n = rhs.shape
    widths = [p.shape[1] for p in parts]
    rows = sum(widths)
    own_rows = rows // N_DEV
    tk = _row_tile(t)
    step = 256

    def body(*refs):
        me_ref, part_refs, rhs_ref = refs[0], refs[1:1 + len(parts)], refs[1 + len(parts)]
        full_ref, own_ref, acc = refs[-3], refs[-2], refs[-1]
        i = pl.program_id(0)

        @pl.when(i == 0)
        def _():
            acc[...] = jnp.zeros_like(acc)

        b = rhs_ref[...].astype(BF16)
        lo = 0
        for p_ref, w in zip(part_refs, widths):
            for c0 in range(0, w, step):
                c1 = min(c0 + step, w)
                acc[lo + c0:lo + c1, :] += _dot_tn(p_ref[:, c0:c1].astype(BF16), b)
            lo += w

        @pl.when(i == pl.num_programs(0) - 1)
        def _():
            full_ref[...] = acc[...].astype(BF16)
            own_ref[...] = acc[pl.ds(pl.multiple_of(me_ref[0] * own_rows, 8), own_rows), :]

    return pl.pallas_call(
        body, name=name, grid=(t // tk,),
        in_specs=[_smem_spec()] + [_row_spec(tk, w) for w in widths] + [_row_spec(tk, n)],
        out_specs=[_full_spec((rows, n)), _full_spec((own_rows, n))],
        out_shape=[SDS((rows, n), BF16), SDS((own_rows, n), F32)],
        scratch_shapes=[pltpu.VMEM((rows, n), F32)],
        compiler_params=_params(("arbitrary",)),
    )(me, *parts, rhs)


def _log_decay(decay_f, decay_b):
    def body(f_ref, b_ref, lf_ref, lb_ref):
        lf_ref[...] = jnp.log1p(-jnp.exp2(f_ref[...]))
        lb_ref[...] = jnp.log1p(-jnp.exp2(b_ref[...]))

    return pl.pallas_call(body, name="log_decay", out_shape=[SDS(decay_f.shape, F32)] * 2)(decay_f, decay_b)


def _pair_tables(lgf_ref, lgb_ref, pair):
    lane = lax.broadcasted_iota(jnp.int32, (1, LANES), 1)
    is_a = lane < HEAD_DIM
    lgf = jnp.where(is_a, lgf_ref[2 * pair], lgf_ref[2 * pair + 1])
    lgb = jnp.where(is_a, lgb_ref[2 * pair], lgb_ref[2 * pair + 1])
    row = lax.broadcasted_iota(jnp.int32, (CHUNK, 1), 0).astype(F32)
    tab = dict(
        is_a=is_a, row=row,
        kdec_f=jnp.exp(lgf * (CHUNK - 1.0 - row)), qdec_f=jnp.exp(lgf * (row + 1.0)),
        kdec_b=jnp.exp(lgb * row), qdec_b=jnp.exp(lgb * (CHUNK - row)),
        lam_f=jnp.exp(lgf * CHUNK), lam_b=jnp.exp(lgb * CHUNK),
    )
    r = lax.broadcasted_iota(jnp.int32, (LANES, LANES), 0)
    c = lax.broadcasted_iota(jnp.int32, (LANES, LANES), 1)
    tab["diag"] = (r < HEAD_DIM) == (c < HEAD_DIM)
    i = lax.broadcasted_iota(jnp.int32, (CHUNK, CHUNK), 0)
    j = lax.broadcasted_iota(jnp.int32, (CHUNK, CHUNK), 1)
    diff = (i - j).astype(F32)
    up, dn = jnp.maximum(diff, 0.0), jnp.maximum(-diff, 0.0)
    for h, name in ((0, "a"), (1, "b")):
        ef = jnp.where(diff >= 0, jnp.exp(lgf_ref[2 * pair + h] * up), 0.0)
        eb = jnp.where(diff <= 0, jnp.exp(lgb_ref[2 * pair + h] * dn), 0.0)
        tab["d_" + name] = ef + eb
        tab["df_" + name] = ef * up
        tab["db_" + name] = eb * dn
    return tab


def _chunk(ref, n):
    return ref[pl.ds(pl.multiple_of(n * CHUNK, CHUNK), CHUNK), :]


def _pair_select(is_a, a, b):
    return jnp.where(is_a, a, b)


def _group_sum(is_a, v):
    sa = jnp.sum(jnp.where(is_a, v, 0.0), axis=1, keepdims=True)
    sb = jnp.sum(jnp.where(is_a, 0.0, v), axis=1, keepdims=True)
    return jnp.where(is_a, sa, sb)


def _retention_states(tab, k_ref, v_ref, n_chunks, rf, rb, extra=None):
    diag = tab["diag"]
    zero = jnp.zeros((LANES, LANES), F32)

    def kv(n, kdec):
        k = _chunk(k_ref, n).astype(F32) * Q_SCALE
        return jnp.where(diag, _dot_tn((k * kdec).astype(BF16), _chunk(v_ref, n)), 0.0)

    def up_body(n, carry):
        r, e = carry
        rf[n] = r
        r = r * tab["lam_f"] + kv(n, tab["kdec_f"])
        if extra is not None:
            ef, eb, gfun = extra
            eb[n] = e
            e = e * tab["lam_b"] + gfun(n, tab["qdec_b"])
        return r, e

    def down_body(s, carry):
        n = n_chunks - 1 - s
        r, e = carry
        rb[n] = r
        r = r * tab["lam_b"] + kv(n, tab["kdec_b"])
        if extra is not None:
            ef, eb, gfun = extra
            ef[n] = e
            e = e * tab["lam_f"] + gfun(n, tab["qdec_f"])
        return r, e

    lax.fori_loop(0, n_chunks, up_body, (zero, zero))
    lax.fori_loop(0, n_chunks, down_body, (zero, zero))


def _seq_spec(s, col_block):
    return pl.BlockSpec((s, LANES), lambda b, h: (b, col_block + h))


def _smem_spec():
    return pl.BlockSpec(memory_space=pltpu.SMEM)


def _retention_fwd(u, lgf, lgb, gn_gain, b_loc):
    t = u.shape[0]
    s = t // b_loc
    n_chunks = s // CHUNK
    pairs = RET_HEADS // 2

    def body(lgf_ref, lgb_ref, q_ref, k_ref, v_ref, g_ref, gain_ref, r_ref, y_ref, rf, rb):
        tab = _pair_tables(lgf_ref, lgb_ref, pl.program_id(1))
        is_a = tab["is_a"]
        _retention_states(tab, k_ref, v_ref, n_chunks, rf, rb)

        def out_body(n, _):
            q = _chunk(q_ref, n)
            k8 = (_chunk(k_ref, n).astype(F32) * Q_SCALE).astype(BF16)
            v = _chunk(v_ref, n)
            zero = jnp.zeros_like(q)
            sa = _dot_nt(jnp.where(is_a, q, zero), k8) * tab["d_a"]
            sb = _dot_nt(jnp.where(is_a, zero, q), k8) * tab["d_b"]
            y = _pair_select(is_a, _dot_nn(sa.astype(BF16), v), _dot_nn(sb.astype(BF16), v))
            qf = q.astype(F32)
            y = y + _dot_nn((qf * tab["qdec_f"]).astype(BF16), rf[n].astype(BF16))
            y = y + _dot_nn((qf * tab["qdec_b"]).astype(BF16), rb[n].astype(BF16))
            rows = pl.ds(pl.multiple_of(n * CHUNK, CHUNK), CHUNK)
            y_ref[rows, :] = y
            mu = _group_sum(is_a, y) * (1.0 / HEAD_DIM)
            dlt = y - mu
            var = _group_sum(is_a, dlt * dlt) * (1.0 / HEAD_DIM)
            xhat = dlt * lax.rsqrt(var + GN_EPS)
            gate = _chunk(g_ref, n).astype(F32)
            r_ref[rows, :] = (xhat * gain_ref[...] * gate * _sigmoid(gate)).astype(BF16)
            return 0

        lax.fori_loop(0, n_chunks, out_body, 0)

    lane_blk = lambda c0: _seq_spec(s, c0 // LANES)
    return pl.pallas_call(
        body, name="retention_fwd", grid=(b_loc, pairs),
        in_specs=[_smem_spec(), _smem_spec(), lane_blk(C_RQ), lane_blk(C_RK), lane_blk(C_RV), lane_blk(C_RG),
                  pl.BlockSpec((1, LANES), lambda b, h: (0, h))],
        out_specs=[_seq_spec(s, 0), _seq_spec(s, 0)],
        out_shape=[SDS((t, RET_W), BF16), SDS((t, RET_W), F32)],
        scratch_shapes=[pltpu.VMEM((n_chunks, LANES, LANES), F32)] * 2,
        compiler_params=_params(("parallel", "parallel")),
    )(lgf, lgb, u, u, u, u, gn_gain)


ST_GAIN, ST_XF, ST_XB, ST_IFA, ST_IFB, ST_IBA, ST_IBB, ST_LF, ST_LB = 0, 1, 2, 3, 4, 5, 6, 8, 9
ST_ROWS = 16


def _retention_bwd(u, y_pre, dr, lgf, lgb, gn_gain, b_loc):
    t = u.shape[0]
    s = t // b_loc
    n_chunks = s // CHUNK
    pairs = RET_HEADS // 2

    def body(lgf_ref, lgb_ref, q_ref, k_ref, v_ref, g_ref, y_ref, dr_ref, gain_ref,
             dq_ref, dk_ref, dv_ref, dg_ref, st_ref, rf, rb, ef, eb, dy_s):
        tab = _pair_tables(lgf_ref, lgb_ref, pl.program_id(1))
        is_a, row, diag = tab["is_a"], tab["row"], tab["diag"]
        gain = gain_ref[...]

        def norm_body(n, dgain):
            rows = pl.ds(pl.multiple_of(n * CHUNK, CHUNK), CHUNK)
            y = y_ref[rows, :]
            mu = _group_sum(is_a, y) * (1.0 / HEAD_DIM)
            dlt = y - mu
            var = _group_sum(is_a, dlt * dlt) * (1.0 / HEAD_DIM)
            rstd = lax.rsqrt(var + GN_EPS)
            xhat = dlt * rstd
            gate = g_ref[rows, :].astype(F32)
            sg = _sigmoid(gate)
            silu = gate * sg
            d_out = dr_ref[rows, :].astype(F32)
            dg_ref[rows, :] = (d_out * xhat * gain * (sg * (1.0 + gate * (1.0 - sg)))).astype(BF16)
            dxh = d_out * gain * silu
            m1 = _group_sum(is_a, dxh) * (1.0 / HEAD_DIM)
            m2 = _group_sum(is_a, dxh * xhat) * (1.0 / HEAD_DIM)
            dy_s[rows, :] = (rstd * (dxh - m1 - xhat * m2)).astype(BF16)
            return dgain + jnp.sum(d_out * xhat * silu, axis=0, keepdims=True)

        dgain = lax.fori_loop(0, n_chunks, norm_body, jnp.zeros((1, LANES), F32))

        def state_grad(n, qdec):
            qd = (_chunk(q_ref, n).astype(F32) * qdec).astype(BF16)
            return jnp.where(diag, _dot_tn(qd, _chunk(dy_s, n)), 0.0)

        _retention_states(tab, k_ref, v_ref, n_chunks, rf, rb, extra=(ef, eb, state_grad))

        def grad_body(n, carry):
            xf, xb, ifa, ifb, iba, ibb, lf, lb = carry
            rows = pl.ds(pl.multiple_of(n * CHUNK, CHUNK), CHUNK)
            q = q_ref[rows, :]
            qf = q.astype(F32)
            k8f = k_ref[rows, :].astype(F32) * Q_SCALE
            k8 = k8f.astype(BF16)
            v = v_ref[rows, :]
            dy = dy_s[rows, :]
            zero = jnp.zeros_like(q)
            dq = jnp.zeros((CHUNK, LANES), F32)
            dk = jnp.zeros((CHUNK, LANES), F32)
            dv = jnp.zeros((CHUNK, LANES), F32)
            intra = []
            for first, name in ((True, "a"), (False, "b")):
                sel = is_a if first else jnp.logical_not(is_a)
                sc = _dot_nt(jnp.where(sel, q, zero), k8)
                dp = _dot_nt(jnp.where(sel, dy, zero), v)
                a_mat = (sc * tab["d_" + name]).astype(BF16)
                ds_mat = (dp * tab["d_" + name]).astype(BF16)
                dq = dq + jnp.where(sel, _dot_nn(ds_mat, k8), 0.0)
                dk = dk + jnp.where(sel, _dot_tn(ds_mat, q), 0.0)
                dv = dv + jnp.where(sel, _dot_tn(a_mat, dy), 0.0)
                prod = sc * dp
                intra.append(jnp.sum(prod * tab["df_" + name], axis=0, keepdims=True))
                intra.append(jnp.sum(prod * tab["db_" + name], axis=0, keepdims=True))
            r_f, r_b = rf[n], rb[n]
            e_f, e_b = ef[n], eb[n]
            dqc_f = _dot_nt(dy, r_f.astype(BF16)) * tab["qdec_f"]
            dqc_b = _dot_nt(dy, r_b.astype(BF16)) * tab["qdec_b"]
            dkc_f = _dot_nt(v, e_f.astype(BF16)) * tab["kdec_f"]
            dkc_b = _dot_nt(v, e_b.astype(BF16)) * tab["kdec_b"]
            dv = dv + _dot_nn((k8f * tab["kdec_f"]).astype(BF16), e_f.astype(BF16))
            dv = dv + _dot_nn((k8f * tab["kdec_b"]).astype(BF16), e_b.astype(BF16))
            dq_ref[rows, :] = (dq + dqc_f + dqc_b).astype(BF16)
            dk_ref[rows, :] = ((dk + dkc_f + dkc_b) * Q_SCALE).astype(BF16)
            dv_ref[rows, :] = dv.astype(BF16)
            xf = xf + jnp.sum((row + 1.0) * qf * dqc_f + (CHUNK - 1.0 - row) * k8f * dkc_f, axis=0, keepdims=True)
            xb = xb + jnp.sum((CHUNK - row) * qf * dqc_b + row * k8f * dkc_b, axis=0, keepdims=True)
            lf = lf + jnp.sum(e_f * r_f, axis=0, keepdims=True)
            lb = lb + jnp.sum(e_b * r_b, axis=0, keepdims=True)
            return (xf, xb, ifa + intra[0], ifb + intra[2], iba + intra[1], ibb + intra[3], lf, lb)

        z = jnp.zeros((1, LANES), F32)
        xf, xb, ifa, ifb, iba, ibb, lf, lb = lax.fori_loop(0, n_chunks, grad_body, (z,) * 8)
        st_ref[...] = jnp.zeros_like(st_ref)
        st_ref[ST_GAIN:ST_GAIN + 1, :] = dgain
        st_ref[ST_XF:ST_XF + 1, :] = xf
        st_ref[ST_XB:ST_XB + 1, :] = xb
        st_ref[ST_IFA:ST_IFA + 1, :] = ifa
        st_ref[ST_IFB:ST_IFB + 1, :] = ifb
        st_ref[ST_IBA:ST_IBA + 1, :] = iba
        st_ref[ST_IBB:ST_IBB + 1, :] = ibb
        st_ref[ST_LF:ST_LF + 1, :] = lf * (CHUNK * tab["lam_f"])
        st_ref[ST_LB:ST_LB + 1, :] = lb * (CHUNK * tab["lam_b"])

    lane_blk = lambda c0: _seq_spec(s, c0 // LANES)
    seq0 = _seq_spec(s, 0)
    state = pltpu.VMEM((n_chunks, LANES, LANES), F32)
    return pl.pallas_call(
        body, name="retention_bwd", grid=(b_loc, pairs),
        in_specs=[_smem_spec(), _smem_spec(), lane_blk(C_RQ), lane_blk(C_RK), lane_blk(C_RV), lane_blk(C_RG),
                  seq0, seq0, pl.BlockSpec((1, LANES), lambda b, h: (0, h))],
        out_specs=[seq0] * 4 + [pl.BlockSpec((ST_ROWS, LANES), lambda b, h: (b, h))],
        out_shape=[SDS((t, RET_W), BF16)] * 4 + [SDS((b_loc * ST_ROWS, RET_W), F32)],
        scratch_shapes=[state] * 4 + [pltpu.VMEM((s, LANES), BF16)],
        compiler_params=_params(("parallel", "parallel")),
    )(lgf, lgb, u, u, u, u, y_pre, dr, gn_gain)


GW = GROUP * HEAD_DIM
KEYS = 3 * BLOCK


def _attn_tables(g, bias_ref):
    r = lax.broadcasted_iota(jnp.int32, (GROUP * BLOCK, KEYS), 0)
    kj = lax.broadcasted_iota(jnp.int32, (GROUP * BLOCK, KEYS), 1)
    qi = r & (BLOCK - 1)
    hh = lax.shift_right_logical(r, 7)
    dist = jnp.abs(kj - BLOCK - qi)
    slope = jnp.exp2(-(GROUP * g + hh + 1).astype(F32) * (8.0 / ATTN_HEADS))
    bias_ref[...] = jnp.where(dist <= BLOCK, -slope * dist.astype(F32), NEG_INF)


def _tile_keys(x_ref, g, scale, pad_ref, s):
    r = lax.broadcasted_iota(jnp.int32, (LANES, GW), 0)
    c = lax.broadcasted_iota(jnp.int32, (LANES, GW), 1)
    place = jnp.where(r == g * HEAD_DIM + (c & (HEAD_DIM - 1)), 1.0, 0.0).astype(BF16)
    pad_ref[0:BLOCK, :] = jnp.zeros((BLOCK, GW), BF16)
    pad_ref[BLOCK + s:2 * BLOCK + s, :] = jnp.zeros((BLOCK, GW), BF16)
    pad_ref[BLOCK:BLOCK + s, :] = (_dot_nn(x_ref[...], place) * scale).astype(BF16)


def _stack_heads(x):
    lane_h = lax.shift_right_logical(lax.broadcasted_iota(jnp.int32, (1, GW), 1), 6)
    zero = jnp.zeros_like(x)
    return jnp.concatenate([jnp.where(lane_h == h, x, zero) for h in range(GROUP)], axis=0)


def _unstack_heads(x4):
    lane_h = lax.shift_right_logical(lax.broadcasted_iota(jnp.int32, (1, GW), 1), 6)
    out = jnp.zeros((BLOCK, GW), F32)
    for h in range(GROUP):
        out = out + jnp.where(lane_h == h, x4[h * BLOCK:(h + 1) * BLOCK, :], 0.0)
    return out


def _sink_column(sink_ref, g):
    rh = lax.shift_right_logical(lax.broadcasted_iota(jnp.int32, (GROUP * BLOCK, 1), 0), 7)
    col = jnp.zeros((GROUP * BLOCK, 1), F32)
    for h in range(GROUP):
        col = jnp.where(rh == h, sink_ref[GROUP * g + h], col)
    return col


def _attn_probs(qm, k3, bias_ref, sink_col, n, s):
    logits = _dot_nt(qm, k3) + bias_ref[...]
    kpos = n * BLOCK - BLOCK + lax.broadcasted_iota(jnp.int32, (1, KEYS), 1)
    logits = jnp.where((kpos >= 0) & (kpos < s), logits, NEG_INF)
    m = jnp.maximum(jnp.max(logits, axis=1, keepdims=True), sink_col)
    e = jnp.exp(logits - m)
    e_sink = jnp.exp(sink_col - m)
    inv = 1.0 / (jnp.sum(e, axis=1, keepdims=True) + e_sink)
    return e * inv, e_sink * inv


def _attn_specs(s, n_blocks):
    q_spec = pl.BlockSpec((BLOCK, GW), lambda b, g, n: (b * n_blocks + n, C_AQ // GW + g))
    k_spec = pl.BlockSpec((s, LANES), lambda b, g, n: (b, C_AK // LANES))
    v_spec = pl.BlockSpec((s, LANES), lambda b, g, n: (b, C_AV // LANES))
    o_spec = pl.BlockSpec((BLOCK, GW), lambda b, g, n: (b * n_blocks + n, g))
    return q_spec, k_spec, v_spec, o_spec


def _attention_fwd(u, sink, b_loc):
    t = u.shape[0]
    s = t // b_loc
    n_blocks = s // BLOCK

    def body(sink_ref, q_ref, k_ref, v_ref, o_ref, kpad, vpad, bias):
        g, n = pl.program_id(1), pl.program_id(2)

        @pl.when(n == 0)
        def _():
            _attn_tables(g, bias)
            _tile_keys(k_ref, g, Q_SCALE, kpad, s)
            _tile_keys(v_ref, g, 1.0, vpad, s)

        keys = pl.ds(pl.multiple_of(n * BLOCK, BLOCK), KEYS)
        p, _ = _attn_probs(_stack_heads(q_ref[...]), kpad[keys, :], bias, _sink_column(sink_ref, g), n, s)
        o_ref[...] = _unstack_heads(_dot_nn(p.astype(BF16), vpad[keys, :])).astype(BF16)

    q_spec, k_spec, v_spec, o_spec = _attn_specs(s, n_blocks)
    pad = pltpu.VMEM((s + 2 * BLOCK, GW), BF16)
    return pl.pallas_call(
        body, name="attention_fwd", grid=(b_loc, KV_HEADS, n_blocks),
        in_specs=[_smem_spec(), q_spec, k_spec, v_spec], out_specs=o_spec,
        out_shape=SDS((t, ATT_W), BF16),
        scratch_shapes=[pad, pad, pltpu.VMEM((GROUP * BLOCK, KEYS), F32)],
        compiler_params=_params(("parallel", "arbitrary", "arbitrary")),
    )(sink, u, u, u)


def _fold_groups(x, g):
    x = x + pltpu.roll(x, 2 * HEAD_DIM, 1)
    x = x + pltpu.roll(x, HEAD_DIM, 1)
    lane_g = lax.shift_right_logical(lax.broadcasted_iota(jnp.int32, (1, LANES), 1), 6)
    return jnp.where(lane_g == g, x[:, 0:LANES], 0.0)


def _attention_bwd(u, da, sink, b_loc):
    t = u.shape[0]
    s = t // b_loc
    n_blocks = s // BLOCK

    def body(sink_ref, q_ref, k_ref, v_ref, do_ref, dq_ref, dk_ref, dv_ref, dsink_ref, kpad, vpad, bias, dk_acc, dv_acc):
        g, n = pl.program_id(1), pl.program_id(2)

        @pl.when(n == 0)
        def _():
            _attn_tables(g, bias)
            _tile_keys(k_ref, g, Q_SCALE, kpad, s)
            _tile_keys(v_ref, g, 1.0, vpad, s)
            dsink_ref[...] = jnp.zeros_like(dsink_ref)

        @pl.when((n == 0) & (g == 0))
        def _():
            dk_acc[...] = jnp.zeros_like(dk_acc)
            dv_acc[...] = jnp.zeros_like(dv_acc)

        keys = pl.ds(pl.multiple_of(n * BLOCK, BLOCK), KEYS)
        qm = _stack_heads(q_ref[...])
        k3, v3 = kpad[keys, :], vpad[keys, :]
        p, p_sink = _attn_probs(qm, k3, bias, _sink_column(sink_ref, g), n, s)
        dom = _stack_heads(do_ref[...])
        dp = _dot_nt(dom, v3)
        delta = jnp.sum(p * dp, axis=1, keepdims=True)
        ds_mat = (p * (dp - delta)).astype(BF16)
        dq_ref[...] = _unstack_heads(_dot_nn(ds_mat, k3)).astype(BF16)
        dk_acc[keys, :] += _fold_groups(_dot_tn(ds_mat, qm), g) * Q_SCALE
        dv_acc[keys, :] += _fold_groups(_dot_tn(p.astype(BF16), dom), g)
        w = p_sink * delta
        rows = lax.broadcasted_iota(jnp.int32, dsink_ref.shape, 0)
        upd = jnp.zeros(dsink_ref.shape, F32)
        for h in range(GROUP):
            upd = jnp.where(rows == h, -jnp.sum(w[h * BLOCK:(h + 1) * BLOCK, :]), upd)
        dsink_ref[...] += upd

        @pl.when((n == n_blocks - 1) & (g == KV_HEADS - 1))
        def _():
            dk_ref[...] = dk_acc[BLOCK:BLOCK + s, :].astype(BF16)
            dv_ref[...] = dv_acc[BLOCK:BLOCK + s, :].astype(BF16)

    q_spec, k_spec, v_spec, o_spec = _attn_specs(s, n_blocks)
    kv_out = pl.BlockSpec((s, LANES), lambda b, g, n: (b, 0))
    pad = pltpu.VMEM((s + 2 * BLOCK, GW), BF16)
    acc = pltpu.VMEM((s + 2 * BLOCK, LANES), F32)
    return pl.pallas_call(
        body, name="attention_bwd", grid=(b_loc, KV_HEADS, n_blocks),
        in_specs=[_smem_spec(), q_spec, k_spec, v_spec, o_spec],
        out_specs=[o_spec, kv_out, kv_out, pl.BlockSpec((8, LANES), lambda b, g, n: (b * KV_HEADS + g, 0))],
        out_shape=[SDS((t, ATT_W), BF16), SDS((t, KV_W), BF16), SDS((t, KV_W), BF16),
                   SDS((b_loc * KV_HEADS * 8, LANES), F32)],
        scratch_shapes=[pad, pad, pltpu.VMEM((GROUP * BLOCK, KEYS), F32), acc, acc],
        compiler_params=_params(("arbitrary", "arbitrary", "arbitrary")),
    )(sink, u, u, u, da)


def _pack_small(acc2, acc1, ret_stats, dsink, b_loc, d):
    pairs = RET_HEADS // 2

    def body(acc2_ref, acc1_ref, st_ref, dsink_ref, out_ref):
        out_ref[...] = jnp.zeros_like(out_ref)
        out_ref[ROW_LN1G:ROW_LN1G + 1, :] = acc1_ref[0:1, :]
        out_ref[ROW_LN1B:ROW_LN1B + 1, :] = acc1_ref[1:2, :]
        out_ref[ROW_LN2G:ROW_LN2G + 1, :] = acc2_ref[1:2, :]
        out_ref[ROW_LN2B:ROW_LN2B + 1, :] = acc2_ref[2:3, :]
        out_ref[ROW_LOSS:ROW_LOSS + 1, :] = acc2_ref[0:1, :]
        st = st_ref[0:ST_ROWS, :]
        for b in range(1, b_loc):
            st = st + st_ref[b * ST_ROWS:(b + 1) * ST_ROWS, :]
        out_ref[ROW_GN:ROW_GN + 1, 0:RET_W] = st[ST_GAIN:ST_GAIN + 1, :]
        lane = lax.broadcasted_iota(jnp.int32, (1, d), 1)
        misc = jnp.zeros((1, d), F32)
        for pr in range(pairs):
            blk = st[:, pr * LANES:(pr + 1) * LANES]
            half = lax.broadcasted_iota(jnp.int32, (1, LANES), 1) < HEAD_DIM
            for h in range(2):
                sel = half if h == 0 else jnp.logical_not(half)
                cross_f = jnp.sum(jnp.where(sel, blk[ST_XF:ST_XF + 1, :] + blk[ST_LF:ST_LF + 1, :], 0.0))
                cross_b = jnp.sum(jnp.where(sel, blk[ST_XB:ST_XB + 1, :] + blk[ST_LB:ST_LB + 1, :], 0.0))
                intra_f = jnp.sum(blk[ST_IFA + h:ST_IFA + h + 1, :])
                intra_b = jnp.sum(blk[ST_IBA + h:ST_IBA + h + 1, :])
                head = 2 * pr + h
                misc = jnp.where(lane == MISC_DF + head, cross_f + intra_f, misc)
                misc = jnp.where(lane == MISC_DB + head, cross_b + intra_b, misc)
        for g in range(KV_HEADS):
            tot = dsink_ref[g * 8:(g + 1) * 8, :]
            for b in range(1, b_loc):
                tot = tot + dsink_ref[(b * KV_HEADS + g) * 8:(b * KV_HEADS + g + 1) * 8, :]
            for h in range(GROUP):
                misc = jnp.where(lane == MISC_SINK + GROUP * g + h, jnp.sum(tot[h:h + 1, 0:1]), misc)
        out_ref[ROW_MISC:ROW_MISC + 1, :] = misc

    return pl.pallas_call(body, name="pack_small", out_shape=SDS((SMALL_ROWS, d), F32))(acc2, acc1, ret_stats, dsink)


BIG = ("w_in", "w_out", "w_ffn_gate", "w_ffn_up", "w_ffn_down", "w_ple_proj", "w_ple_gate")
TRANSPOSED_OUTSIDE = ("w_in", "w_ffn_gate", "w_ffn_up")
TRANSPOSED_HERE = ("w_ple_proj",)
SMALL = ("ret_decay_fwd", "ret_decay_bwd", "ret_gn_gain", "attn_sink", "ln1_gain", "ln1_bias", "ln2_gain", "ln2_bias")
ORDER = ("w_in", "ret_decay_fwd", "ret_decay_bwd", "ret_gn_gain", "attn_sink", "w_out", "ln1_gain", "ln1_bias",
         "w_ffn_gate", "w_ffn_up", "w_ffn_down", "w_ple_proj", "w_ple_gate", "ln2_gain", "ln2_bias")


def _local_step(x2, p2, target2, full, small, b_loc, me):
    d = x2.shape[1]
    lgf, lgb = _log_decay(small["ret_decay_fwd"], small["ret_decay_bwd"])
    lgf1, lgb1, sink1 = lgf.reshape(-1), lgb.reshape(-1), small["attn_sink"].reshape(-1)
    u, xb = _in_proj(x2, full["w_in"])
    r, y_pre = _retention_fwd(u, lgf1, lgb1, small["ret_gn_gain"], b_loc)
    a = _attention_fwd(u, sink1, b_loc)
    z1, h1b = _out_proj_ln1(r, a, x2, full["w_out"], small["ln1_gain"], small["ln1_bias"])
    g, up, act = _ffn_up(h1b, full["w_ffn_gate"], full["w_ffn_up"])
    dz2, dz2b, dsb, dpleb, acc2 = _ffn_down_ln2_loss(
        act, h1b, p2, z1, target2, full["w_ffn_down"], full["w_ple_gate"], full["w_ple_proj"],
        small["ln1_gain"], small["ln1_bias"], small["ln2_gain"], small["ln2_bias"])
    grads = {}
    grads["w_ffn_down"] = _weight_grad("grad_w_ffn_down", me, [act], dz2b)
    grads["w_ple_proj"] = _weight_grad("grad_w_ple_proj", me, [dpleb], p2)
    grads["w_ple_gate"] = _weight_grad("grad_w_ple_gate", me, [h1b], dsb)
    dg, dup = _ffn_bwd_act(dz2b, g, up, full["w_ffn_down"])
    grads["w_ffn_gate"] = _weight_grad("grad_w_ffn_gate", me, [dg], h1b)
    grads["w_ffn_up"] = _weight_grad("grad_w_ffn_up", me, [dup], h1b)
    dz1, dz1b, acc1 = _dh1_ln1_bwd(dz2, dg, dup, dsb, z1, full["w_ffn_gate"], full["w_ffn_up"], full["w_ple_gate"],
                                   small["ln1_gain"])
    grads["w_out"] = _weight_grad("grad_w_out", me, [r, a], dz1b)
    dr, da = _out_proj_bwd(dz1b, full["w_out"])
    dq, dk, dv, dgate, ret_stats = _retention_bwd(u, y_pre, dr, lgf1, lgb1, small["ret_gn_gain"], b_loc)
    daq, dak, dav, dsink = _attention_bwd(u, da, sink1, b_loc)
    parts = [dq, dk, dv, dgate, daq, dak, dav]
    grads["w_in"] = _weight_grad("grad_w_in", me, parts, xb)
    grad_x = _in_proj_bwd(dz1, parts, full["w_in"])
    small_part = _pack_small(acc2, acc1, ret_stats, dsink, b_loc, d)
    return grad_x, grads, small_part


def kernel(x, p, w_in, ret_decay_fwd, ret_decay_bwd, ret_gn_gain, attn_sink, w_out, ln1_gain, ln1_bias, w_ffn_gate, w_ffn_up, w_ffn_down, w_ple_proj, w_ple_gate, ln2_gain, ln2_bias, loss_target, m_w_in, m_ret_decay_fwd, m_ret_decay_bwd, m_ret_gn_gain, m_attn_sink, m_w_out, m_ln1_gain, m_ln1_bias, m_w_ffn_gate, m_w_ffn_up, m_w_ffn_down, m_w_ple_proj, m_w_ple_gate, m_ln2_gain, m_ln2_bias, v_w_in, v_ret_decay_fwd, v_ret_decay_bwd, v_ret_gn_gain, v_attn_sink, v_w_out, v_ln1_gain, v_ln1_bias, v_w_ffn_gate, v_w_ffn_up, v_w_ffn_down, v_w_ple_proj, v_w_ple_gate, v_ln2_gain, v_ln2_bias):
    given = dict(locals())

    def strip(n, a):
        if n not in BIG:
            return a
        return a[0].T if n in TRANSPOSED_OUTSIDE else a[0]

    def restore(n, a):
        if n not in BIG:
            return a
        return (a.T if n in TRANSPOSED_OUTSIDE else a)[None]

    w = {n: strip(n, given[n]) for n in ORDER}
    m = {n: strip(n, given["m_" + n]) for n in ORDER}
    v = {n: strip(n, given["v_" + n]) for n in ORDER}
    b_loc, s, d = x.shape
    x2 = x.reshape(b_loc * s, d)
    p2 = p[0].reshape(b_loc * s, p.shape[-1])
    target2 = loss_target.reshape(b_loc * s, d)

    shards = _prep_shards({n: w[n] for n in BIG})
    full = dict(zip(BIG, _all_gather([shards[n] for n in BIG])))
    small = {n: w[n] for n in SMALL}
    me = (4 * lax.axis_index("x") + 2 * lax.axis_index("y") + lax.axis_index("c")).astype(jnp.int32).reshape(1)
    grad_x, grads, small_part = _local_step(x2, p2, target2, full, small, b_loc, me)

    received = _scatter_rows([grads[n][0] for n in BIG])
    out_g, out_d, out_m, out_v = {}, {}, {}, {}
    for n, recv in zip(BIG, received):
        out_g[n], out_d[n], out_m[n], out_v[n] = _reduce_adamw(
            grads[n][1], recv, w[n], m[n], v[n], n in TRANSPOSED_HERE)

    loss, sg, sd, sm, sv = _small_all_reduce_adamw(small_part, small, {n: m[n] for n in SMALL}, {n: v[n] for n in SMALL})
    for dst, src in ((out_g, sg), (out_d, sd), (out_m, sm), (out_v, sv)):
        dst.update(src)

    outs = [loss[0, 0], grad_x.reshape(x.shape)]
    for group in (out_g, out_d, out_m, out_v):
        outs += [restore(n, group[n]) for n in ORDER]
    return tuple(outs)
```

```python
import functools

import jax
import jax.numpy as jnp
from jax import lax
from jax.experimental import pallas as pl
from jax.experimental.pallas import tpu as pltpu

F32, BF16 = jnp.float32, jnp.bfloat16
SDS = jax.ShapeDtypeStruct
MESH = pl.DeviceIdType.MESH

N_DEV = 8
HEAD_DIM = 64
RET_HEADS = 8
ATTN_HEADS = 8
KV_HEADS = 2
GROUP = ATTN_HEADS // KV_HEADS
RET_W = RET_HEADS * HEAD_DIM
ATT_W = ATTN_HEADS * HEAD_DIM
KV_W = KV_HEADS * HEAD_DIM
LANES = 128
CHUNK = 128
BLOCK = 128
Q_SCALE = HEAD_DIM ** -0.5
ALPHA = 2.0 ** 0.25
LN_EPS = 1e-5
GN_EPS = 1e-5
NEG_INF = -1e30
C_RQ, C_RK, C_RV, C_RG = 0, RET_W, 2 * RET_W, 3 * RET_W
C_AQ = 4 * RET_W
C_AK = C_AQ + ATT_W
C_AV = C_AK + KV_W
IN_W = C_AV + KV_W

ADAM_LR = 0.001
ADAM_B1 = 0.9
ADAM_B2 = 0.999
ADAM_EPS = 1e-08
ADAM_WD = 0.01
ADAM_STEP = 10

VMEM_LIMIT = 48 * 1024 * 1024
SMALL_ROWS = 16
ROW_LN1G, ROW_LN1B, ROW_LN2G, ROW_LN2B, ROW_LOSS, ROW_GN, ROW_MISC = 0, 1, 2, 3, 4, 5, 6
MISC_DF, MISC_DB, MISC_SINK = 0, 8, 16


def _dot_nn(a, b):
    return lax.dot_general(a, b, (((1,), (0,)), ((), ())), preferred_element_type=F32)


def _dot_nt(a, b):
    return lax.dot_general(a, b, (((1,), (1,)), ((), ())), preferred_element_type=F32)


def _dot_tn(a, b):
    return lax.dot_general(a, b, (((0,), (0,)), ((), ())), preferred_element_type=F32)


def _params(sem=None, vmem=VMEM_LIMIT):
    kw = {"vmem_limit_bytes": vmem}
    if sem is not None:
        kw["dimension_semantics"] = sem
    return pltpu.CompilerParams(**kw)


def _row_tile(t, want=512):
    tm = want
    while t % tm:
        tm //= 2
    return tm


def _sigmoid(x):
    return 1.0 / (1.0 + jnp.exp(-x))


def _layer_norm_stats(z):
    mu = jnp.mean(z, axis=1, keepdims=True)
    d = z - mu
    var = jnp.mean(d * d, axis=1, keepdims=True)
    rstd = lax.rsqrt(var + LN_EPS)
    return d * rstd, rstd


def _layer_norm_bwd(dxh, xhat, rstd):
    m1 = jnp.mean(dxh, axis=1, keepdims=True)
    m2 = jnp.mean(dxh * xhat, axis=1, keepdims=True)
    return rstd * (dxh - m1 - xhat * m2)


def _prep_shards(me, shards):
    names = list(shards)

    def body(me_ref, *refs):
        for name, src, dst in zip(names, refs[:len(names)], refs[len(names):]):
            val = src[...]
            dst[...] = (val.T if name in TRANSPOSED_HERE else val).astype(BF16)

    shape = lambda n, a: a.shape[::-1] if n in TRANSPOSED_HERE else a.shape
    shapes = [shape(n, shards[n]) for n in names]
    out = pl.pallas_call(
        body, name="prep_shards",
        grid_spec=pltpu.PrefetchScalarGridSpec(
            num_scalar_prefetch=1, grid=(1,),
            in_specs=[pl.BlockSpec(shards[n].shape, lambda i, me_ref: (0, 0)) for n in names],
            out_specs=[pl.BlockSpec(s, lambda i, me_ref: (me_ref[0], 0)) for s in shapes]),
        out_shape=[SDS((N_DEV * s[0], s[1]), BF16) for s in shapes], compiler_params=_params(("arbitrary",)),
    )(me, *[shards[n] for n in names])
    return dict(zip(names, out))


def _mesh_pos():
    return lax.axis_index("x"), lax.axis_index("y"), lax.axis_index("c")


HBM_SPEC = pl.BlockSpec(memory_space=pltpu.HBM)
SEM_SPEC = pl.BlockSpec(memory_space=pltpu.SEMAPHORE)
ANY_SPEC = pl.BlockSpec(memory_space=pl.ANY)
SIDE_EFFECT = pltpu.SideEffectType.DATAFLOW_SIDE_EFFECTING
PEER_SEMS = pltpu.SemaphoreType.DMA((N_DEV - 1,))


def _in_hbm(a):
    return pltpu.with_memory_space_constraint(a, pltpu.HBM)


def _split_copy_start(name, items, copy_of):
    n = len(items)
    flat = [a for it in items for a in it]
    k = len(flat)

    def body(*refs):
        arr, sems = list(refs[:k]), refs[k:k + 2 * n]
        for i, it in enumerate(items):
            mine = [arr.pop(0) for _ in it]
            for m in range(1, N_DEV):
                copy_of(m, mine, sems[i].at[m - 1], sems[n + i].at[m - 1]).start()
        token = refs[-1]
        token[...] = jnp.zeros_like(token)

    res = pl.pallas_call(
        body, name=name,
        out_shape=[PEER_SEMS] * (2 * n) + [pltpu.HBM(a.shape, a.dtype) for a in flat] + [SDS((8, LANES), F32)],
        in_specs=[HBM_SPEC] * k,
        out_specs=[SEM_SPEC] * (2 * n) + [HBM_SPEC] * k + [pl.BlockSpec(memory_space=pltpu.VMEM)],
        input_output_aliases={j: 2 * n + j for j in range(k)},
        compiler_params=pltpu.CompilerParams(has_side_effects=SIDE_EFFECT),
    )(*[_in_hbm(a) for a in flat])
    thru, out_items = list(res[2 * n:2 * n + k]), []
    for it in items:
        out_items.append(tuple(thru.pop(0) for _ in it))
    return dict(send=res[:n], recv=res[n:2 * n], items=out_items, token=res[-1])


def _split_copy_wait(name, started, which, copy_of, after):
    items = [started["items"][i] for i in which]
    n = len(items)
    flat = [a for it in items for a in it]
    k = len(flat)

    def body(*refs):
        arr, sems = list(refs[:k]), refs[k:k + 2 * n]
        for i, it in enumerate(items):
            mine = [arr.pop(0) for _ in it]
            for m in range(1, N_DEV):
                cp = copy_of(m, mine, sems[i].at[m - 1], sems[n + i].at[m - 1])
                cp.wait_send()
                cp.wait_recv()

    res = pl.pallas_call(
        body, name=name,
        out_shape=[pltpu.HBM(a.shape, a.dtype) for a in flat],
        in_specs=[HBM_SPEC] * k + [SEM_SPEC] * (2 * n) + [ANY_SPEC] * len(after),
        out_specs=[HBM_SPEC] * k,
        input_output_aliases={j: j for j in range(k)},
        compiler_params=pltpu.CompilerParams(has_side_effects=SIDE_EFFECT),
    )(*flat, *[started["send"][i] for i in which], *[started["recv"][i] for i in which], *[_in_hbm(a) for a in after])
    thru, out_items = list(res), []
    for it in items:
        out_items.append(tuple(thru.pop(0) for _ in it))
    return out_items


def _gather_copy(m, refs, send_sem, recv_sem):
    (land_ref,) = refs
    r = land_ref.shape[0] // N_DEV
    mine = land_ref.at[pl.ds(pl.multiple_of(_peer_index(0) * r, 8), r), :]
    return pltpu.make_async_remote_copy(src_ref=mine, dst_ref=mine, send_sem=send_sem, recv_sem=recv_sem,
                                        device_id=_peer(m), device_id_type=MESH)


def _scatter_copy(m, refs, send_sem, recv_sem):
    buf_ref, land_ref = refs
    r = buf_ref.shape[0] // N_DEV
    src = buf_ref.at[pl.ds(pl.multiple_of(_peer_index(m) * r, 8), r), :]
    return pltpu.make_async_remote_copy(src_ref=src, dst_ref=land_ref.at[m], send_sem=send_sem, recv_sem=recv_sem,
                                        device_id=_peer(m), device_id_type=MESH)


def _peer(m):
    x, y, c = _mesh_pos()
    bx, by, bc = (m >> 2) & 1, (m >> 1) & 1, m & 1
    return (x ^ bx if bx else x, y ^ by if by else y, c ^ bc if bc else c)


def _peer_index(m):
    x, y, c = _mesh_pos()
    return (4 * x + 2 * y + c) ^ m


SMALL_PLACE = {
    "ln1_gain": (ROW_LN1G, 0), "ln1_bias": (ROW_LN1B, 0), "ln2_gain": (ROW_LN2G, 0), "ln2_bias": (ROW_LN2B, 0),
    "ret_gn_gain": (ROW_GN, 0), "ret_decay_fwd": (ROW_MISC, MISC_DF), "ret_decay_bwd": (ROW_MISC, MISC_DB),
    "attn_sink": (ROW_MISC, MISC_SINK)}


def _small_all_reduce_adamw(part, w, m, v):
    d = part.shape[1]
    names = list(SMALL_PLACE)
    k = len(names)

    def body(*refs):
        part_ref = refs[0]
        w_refs, m_refs, v_refs = refs[1:1 + k], refs[1 + k:1 + 2 * k], refs[1 + 2 * k:1 + 3 * k]
        outs = refs[1 + 3 * k:2 + 7 * k]
        loss_ref, g_refs, dl_refs = outs[0], outs[1:1 + k], outs[1 + k:1 + 2 * k]
        nm_refs, nv_refs = outs[1 + 2 * k:1 + 3 * k], outs[1 + 3 * k:1 + 4 * k]
        buf, send_sems, recv_sems = refs[2 + 7 * k:]
        buf[0] = part_ref[...]
        cps = [pltpu.make_async_remote_copy(
            src_ref=part_ref, dst_ref=buf.at[j], send_sem=send_sems.at[j - 1], recv_sem=recv_sems.at[j - 1],
            device_id=_peer(j), device_id_type=MESH) for j in range(1, N_DEV)]
        for cp in cps:
            cp.start()
        for cp in cps:
            cp.wait_recv()
        for cp in cps:
            cp.wait_send()
        me = _peer_index(0)
        tot = buf[me]
        for dev in range(1, N_DEV):
            tot = tot + buf[dev ^ me]
        loss_ref[...] = (0.5 / d) * jnp.sum(tot[ROW_LOSS:ROW_LOSS + 1, :], axis=1, keepdims=True)
        for i, name in enumerate(names):
            row, lo = SMALL_PLACE[name]
            wv = w_refs[i][...]
            g = tot[row:row + 1, lo:lo + wv.shape[1]]
            if name.startswith("ret_decay"):
                p2 = jnp.exp2(wv)
                g = g * (-p2 * jnp.log(2.0) / (1.0 - p2))
            g_refs[i][...] = g
            _adamw_store(g, wv, m_refs[i][...], v_refs[i][...], dl_refs[i], nm_refs[i], nv_refs[i])

    vm = pl.BlockSpec(memory_space=pltpu.VMEM)
    shapes = [SDS(w[n].shape, F32) for n in names]
    res = pl.pallas_call(
        body, name="small_all_reduce_adamw",
        out_shape=[SDS((1, 1), F32)] + shapes * 4, in_specs=[vm] * (1 + 3 * k), out_specs=[vm] * (1 + 4 * k),
        scratch_shapes=[pltpu.VMEM((N_DEV,) + part.shape, F32), pltpu.SemaphoreType.DMA((7,)),
                        pltpu.SemaphoreType.DMA((7,))],
    )(part, *[w[n] for n in names], *[m[n] for n in names], *[v[n] for n in names])
    groups = [dict(zip(names, res[1 + j * k:1 + (j + 1) * k])) for j in range(4)]
    return (res[0], *groups)


def _adamw_store(g, w, m, v, dl_ref, nm_ref, nv_ref):
    m = ADAM_B1 * m + (1.0 - ADAM_B1) * g
    v = ADAM_B2 * v + (1.0 - ADAM_B2) * (g * g)
    m_hat = m / (1.0 - ADAM_B1 ** ADAM_STEP)
    v_hat = v / (1.0 - ADAM_B2 ** ADAM_STEP)
    dl_ref[...] = -ADAM_LR * (m_hat / (jnp.sqrt(v_hat) + ADAM_EPS) + ADAM_WD * w)
    nm_ref[...] = m
    nv_ref[...] = v


def _reduce_adamw(name, own, recv, w, m, v, transposed):
    def body(own_ref, recv_ref, w_ref, m_ref, v_ref, g_ref, dl_ref, nm_ref, nv_ref):
        g = own_ref[...]
        for k in range(1, N_DEV):
            g = g + recv_ref[k].astype(F32)
        if transposed:
            g = g.T
        g_ref[...] = g
        _adamw_store(g, w_ref[...], m_ref[...], v_ref[...], dl_ref, nm_ref, nv_ref)

    out = SDS(w.shape, F32)
    return pl.pallas_call(body, name="adamw_" + name, out_shape=[out] * 4, compiler_params=_params())(own, recv, w, m, v)


def _row_spec(tm, width):
    return pl.BlockSpec((tm, width), lambda i: (i, 0))


def _full_spec(shape):
    return pl.BlockSpec(shape, lambda i: (0,) * len(shape))


def _in_proj(x2, wt_in):
    t, d = x2.shape
    u_w = wt_in.shape[0]
    tm = _row_tile(t)

    def body(x_ref, w_ref, u_ref, xb_ref):
        xb = x_ref[...].astype(BF16)
        xb_ref[...] = xb
        u_ref[...] = _dot_nt(xb, w_ref[...]).astype(BF16)

    return pl.pallas_call(
        body, name="in_proj", grid=(t // tm,),
        in_specs=[_row_spec(tm, d), _full_spec(wt_in.shape)],
        out_specs=[_row_spec(tm, u_w), _row_spec(tm, d)],
        out_shape=[SDS((t, u_w), BF16), SDS((t, d), BF16)],
        compiler_params=_params(("parallel",)),
    )(x2, wt_in)


def _out_proj_ln1(r, a, x2, w_out, g1, b1):
    t, d = x2.shape
    tm = _row_tile(t)

    def body(r_ref, a_ref, x_ref, wo_ref, g_ref, b_ref, z_ref, hb_ref):
        mix = _dot_nn(r_ref[...], wo_ref[0:RET_W, :]) + _dot_nn(a_ref[...], wo_ref[RET_W:RET_W + ATT_W, :])
        z = ALPHA * x_ref[...] + mix
        xhat, _ = _layer_norm_stats(z)
        z_ref[...] = z
        hb_ref[...] = (xhat * g_ref[...] + b_ref[...]).astype(BF16)

    return pl.pallas_call(
        body, name="out_proj_ln1", grid=(t // tm,),
        in_specs=[_row_spec(tm, RET_W), _row_spec(tm, ATT_W), _row_spec(tm, d), _full_spec(w_out.shape),
                  _full_spec(g1.shape), _full_spec(b1.shape)],
        out_specs=[_row_spec(tm, d), _row_spec(tm, d)],
        out_shape=[SDS((t, d), F32), SDS((t, d), BF16)],
        compiler_params=_params(("parallel",)),
    )(r, a, x2, w_out, g1, b1)


def _col_halves(f):
    n = f // LANES
    k = (n + 1) // 2 * LANES
    return [(0, k), (k, f)] if k < f else [(0, f)]


def _ffn_up(h1b, wt_gate, wt_up):
    t, d = h1b.shape
    f = wt_gate.shape[0]
    tm = _row_tile(t, 256)

    def body(h_ref, wg_ref, wu_ref, g_ref, u_ref, act_ref):
        h = h_ref[...]
        for lo, hi in _col_halves(f):
            g = _dot_nt(h, wg_ref[lo:hi, :])
            u = _dot_nt(h, wu_ref[lo:hi, :])
            g_ref[:, lo:hi] = g.astype(BF16)
            u_ref[:, lo:hi] = u.astype(BF16)
            act_ref[:, lo:hi] = (g * _sigmoid(g) * u).astype(BF16)

    return pl.pallas_call(
        body, name="ffn_up", grid=(t // tm,),
        in_specs=[_row_spec(tm, d), _full_spec(wt_gate.shape), _full_spec(wt_up.shape)],
        out_specs=[_row_spec(tm, f)] * 3,
        out_shape=[SDS((t, f), BF16)] * 3,
        compiler_params=_params(("parallel",)),
    )(h1b, wt_gate, wt_up)


def _ffn_down_ln2_loss(act, h1b, p2, z1, target, w_down, w_pg, wt_pe, g1, b1, g2, b2):
    t, d = z1.shape
    f = act.shape[1]
    pdim = p2.shape[1]
    tm = _row_tile(t, 256)

    def body(act_ref, hb_ref, p_ref, z1_ref, tgt_ref, wd_ref, wpg_ref, wpe_ref, g1_ref, b1_ref, g2_ref, b2_ref,
             dz_ref, dzb_ref, ds_ref, dple_ref, acc_ref):
        @pl.when(pl.program_id(0) == 0)
        def _():
            acc_ref[...] = jnp.zeros_like(acc_ref)

        xhat1, _ = _layer_norm_stats(z1_ref[...])
        h1 = xhat1 * g1_ref[...] + b1_ref[...]
        ffn = _dot_nn(act_ref[...], wd_ref[...])
        pg = _sigmoid(_dot_nn(hb_ref[...], wpg_ref[...]))
        ple = _dot_nt(p_ref[...].astype(BF16), wpe_ref[...])
        z2 = ALPHA * h1 + ffn + pg * ple
        xhat2, rstd2 = _layer_norm_stats(z2)
        err = xhat2 * g2_ref[...] + b2_ref[...] - tgt_ref[...]
        dy = err * (1.0 / d)
        dz = _layer_norm_bwd(dy * g2_ref[...], xhat2, rstd2)
        dz_ref[...] = dz
        dzb_ref[...] = dz.astype(BF16)
        ds_ref[...] = (dz * ple * pg * (1.0 - pg)).astype(BF16)
        dple_ref[...] = (dz * pg).astype(BF16)
        acc_ref[0:1, :] += jnp.sum(err * err, axis=0, keepdims=True)
        acc_ref[1:2, :] += jnp.sum(dy * xhat2, axis=0, keepdims=True)
        acc_ref[2:3, :] += jnp.sum(dy, axis=0, keepdims=True)

    vec = _full_spec(g1.shape)
    return pl.pallas_call(
        body, name="ffn_down_ln2_loss", grid=(t // tm,),
        in_specs=[_row_spec(tm, f), _row_spec(tm, d), _row_spec(tm, pdim), _row_spec(tm, d), _row_spec(tm, d),
                  _full_spec(w_down.shape), _full_spec(w_pg.shape), _full_spec(wt_pe.shape), vec, vec, vec, vec],
        out_specs=[_row_spec(tm, d)] * 4 + [_full_spec((8, d))],
        out_shape=[SDS((t, d), F32), SDS((t, d), BF16), SDS((t, d), BF16), SDS((t, d), BF16), SDS((8, d), F32)],
        compiler_params=_params(("arbitrary",)),
    )(act, h1b, p2, z1, target, w_down, w_pg, wt_pe, g1, b1, g2, b2)


def _after(after, body):
    k = len(after)
    return (lambda *refs: body(*refs[k:])), [ANY_SPEC] * k


def _ffn_bwd_act(dzb, g, up, w_down, after=()):
    t, d = dzb.shape
    f = g.shape[1]
    tm = _row_tile(t, 256)

    def body(dz_ref, g_ref, u_ref, wd_ref, dg_ref, du_ref):
        dz = dz_ref[...]
        for lo, hi in _col_halves(f):
            da = _dot_nt(dz, wd_ref[lo:hi, :])
            gv = g_ref[:, lo:hi].astype(F32)
            sg = _sigmoid(gv)
            dg_ref[:, lo:hi] = (da * u_ref[:, lo:hi].astype(F32) * (sg * (1.0 + gv * (1.0 - sg)))).astype(BF16)
            du_ref[:, lo:hi] = (da * gv * sg).astype(BF16)

    body, lead = _after(after, body)
    return pl.pallas_call(
        body, name="ffn_bwd_act", grid=(t // tm,),
        in_specs=lead + [_row_spec(tm, d), _row_spec(tm, f), _row_spec(tm, f), _full_spec(w_down.shape)],
        out_specs=[_row_spec(tm, f)] * 2,
        out_shape=[SDS((t, f), BF16)] * 2,
        compiler_params=_params(("parallel",)),
    )(*after, dzb, g, up, w_down)


def _dh1_ln1_bwd(dz2, dg, dup, dsb, z1, wt_gate, wt_up, w_pg, g1, after=()):
    t, d = dz2.shape
    f = dg.shape[1]
    tm = _row_tile(t, 256)

    def body(dz_ref, dg_ref, du_ref, ds_ref, z1_ref, wg_ref, wu_ref, wpg_ref, g1_ref, dz1_ref, dz1b_ref, acc_ref):
        @pl.when(pl.program_id(0) == 0)
        def _():
            acc_ref[...] = jnp.zeros_like(acc_ref)

        dh = (ALPHA * dz_ref[...] + _dot_nn(dg_ref[...], wg_ref[...]) + _dot_nn(du_ref[...], wu_ref[...])
              + _dot_nt(ds_ref[...], wpg_ref[...]))
        xhat, rstd = _layer_norm_stats(z1_ref[...])
        dz1 = _layer_norm_bwd(dh * g1_ref[...], xhat, rstd)
        dz1_ref[...] = dz1
        dz1b_ref[...] = dz1.astype(BF16)
        acc_ref[0:1, :] += jnp.sum(dh * xhat, axis=0, keepdims=True)
        acc_ref[1:2, :] += jnp.sum(dh, axis=0, keepdims=True)

    body, lead = _after(after, body)
    return pl.pallas_call(
        body, name="dh1_ln1_bwd", grid=(t // tm,),
        in_specs=lead + [_row_spec(tm, d), _row_spec(tm, f), _row_spec(tm, f), _row_spec(tm, d), _row_spec(tm, d),
                         _full_spec(wt_gate.shape), _full_spec(wt_up.shape), _full_spec(w_pg.shape),
                         _full_spec(g1.shape)],
        out_specs=[_row_spec(tm, d), _row_spec(tm, d), _full_spec((8, d))],
        out_shape=[SDS((t, d), F32), SDS((t, d), BF16), SDS((8, d), F32)],
        compiler_params=_params(("arbitrary",)),
    )(*after, dz2, dg, dup, dsb, z1, wt_gate, wt_up, w_pg, g1)


def _out_proj_bwd(dz1b, w_out, after=()):
    t, d = dz1b.shape
    tm = _row_tile(t)

    def body(dz_ref, wo_ref, dr_ref, da_ref):
        dz = dz_ref[...]
        dr_ref[...] = _dot_nt(dz, wo_ref[0:RET_W, :]).astype(BF16)
        da_ref[...] = _dot_nt(dz, wo_ref[RET_W:RET_W + ATT_W, :]).astype(BF16)

    body, lead = _after(after, body)
    return pl.pallas_call(
        body, name="out_proj_bwd", grid=(t // tm,),
        in_specs=lead + [_row_spec(tm, d), _full_spec(w_out.shape)],
        out_specs=[_row_spec(tm, RET_W), _row_spec(tm, ATT_W)],
        out_shape=[SDS((t, RET_W), BF16), SDS((t, ATT_W), BF16)],
        compiler_params=_params(("parallel",)),
    )(*after, dz1b, w_out)


def _in_proj_bwd(dz1, parts, wt_in, after=()):
    t, d = dz1.shape
    tm = _row_tile(t)
    widths = [p.shape[1] for p in parts]

    def body(*refs):
        dz_ref, part_refs, w_ref, dx_ref = refs[0], refs[1:1 + len(parts)], refs[-2], refs[-1]
        acc = ALPHA * dz_ref[...]
        lo = 0
        for p_ref, w in zip(part_refs, widths):
            acc = acc + _dot_nn(p_ref[...], w_ref[lo:lo + w, :])
            lo += w
        dx_ref[...] = acc

    body, lead = _after(after, body)
    return pl.pallas_call(
        body, name="in_proj_bwd", grid=(t // tm,),
        in_specs=lead + [_row_spec(tm, d)] + [_row_spec(tm, w) for w in widths] + [_full_spec(wt_in.shape)],
        out_specs=_row_spec(tm, d), out_shape=SDS((t, d), F32),
        compiler_params=_params(("parallel",)),
    )(*after, dz1, *parts, wt_in)


def _weight_grad(name, me, parts, rhs):
    t, n = rhs.shape
    widths = [p.shape[1] for p in parts]
    rows = sum(widths)
    own_rows = rows // N_DEV
    tk = _row_tile(t)
    step = 256

    def body(*refs):
        me_ref, part_refs, rhs_ref = refs[0], refs[1:1 + len(parts)], refs[1 + len(parts)]
        full_ref, own_ref, acc = refs[-3], refs[-2], refs[-1]
        i = pl.program_id(0)

        @pl.when(i == 0)
        def _():
            acc[...] = jnp.zeros_like(acc)

        b = rhs_ref[...].astype(BF16)
        lo = 0
        for p_ref, w in zip(part_refs, widths):
            for c0 in range(0, w, step):
                c1 = min(c0 + step, w)
                acc[lo + c0:lo + c1, :] += _dot_tn(p_ref[:, c0:c1].astype(BF16), b)
            lo += w

        @pl.when(i == pl.num_programs(0) - 1)
        def _():
            full_ref[...] = acc[...].astype(BF16)
            own_ref[...] = acc[pl.ds(pl.multiple_of(me_ref[0] * own_rows, 8), own_rows), :]

    return pl.pallas_call(
        body, name=name, grid=(t // tk,),
        in_specs=[_smem_spec()] + [_row_spec(tk, w) for w in widths] + [_row_spec(tk, n)],
        out_specs=[_full_spec((rows, n)), _full_spec((own_rows, n))],
        out_shape=[SDS((rows, n), BF16), SDS((own_rows, n), F32)],
        scratch_shapes=[pltpu.VMEM((rows, n), F32)],
        compiler_params=_params(("arbitrary",)),
    )(me, *parts, rhs)


def _log_decay(decay_f, decay_b):
    def body(f_ref, b_ref, lf_ref, lb_ref):
        lf_ref[...] = jnp.log1p(-jnp.exp2(f_ref[...]))
        lb_ref[...] = jnp.log1p(-jnp.exp2(b_ref[...]))

    return pl.pallas_call(body, name="log_decay", out_shape=[SDS(decay_f.shape, F32)] * 2)(decay_f, decay_b)


def _pair_tables(lgf_ref, lgb_ref, pair):
    lane = lax.broadcasted_iota(jnp.int32, (1, LANES), 1)
    is_a = lane < HEAD_DIM
    lgf = jnp.where(is_a, lgf_ref[2 * pair], lgf_ref[2 * pair + 1])
    lgb = jnp.where(is_a, lgb_ref[2 * pair], lgb_ref[2 * pair + 1])
    row = lax.broadcasted_iota(jnp.int32, (CHUNK, 1), 0).astype(F32)
    tab = dict(
        is_a=is_a, row=row,
        kdec_f=jnp.exp(lgf * (CHUNK - 1.0 - row)), qdec_f=jnp.exp(lgf * (row + 1.0)),
        kdec_b=jnp.exp(lgb * row), qdec_b=jnp.exp(lgb * (CHUNK - row)),
        lam_f=jnp.exp(lgf * CHUNK), lam_b=jnp.exp(lgb * CHUNK),
    )
    r = lax.broadcasted_iota(jnp.int32, (LANES, LANES), 0)
    c = lax.broadcasted_iota(jnp.int32, (LANES, LANES), 1)
    tab["diag"] = (r < HEAD_DIM) == (c < HEAD_DIM)
    i = lax.broadcasted_iota(jnp.int32, (CHUNK, CHUNK), 0)
    j = lax.broadcasted_iota(jnp.int32, (CHUNK, CHUNK), 1)
    diff = (i - j).astype(F32)
    up, dn = jnp.maximum(diff, 0.0), jnp.maximum(-diff, 0.0)
    for h, name in ((0, "a"), (1, "b")):
        ef = jnp.where(diff >= 0, jnp.exp(lgf_ref[2 * pair + h] * up), 0.0)
        eb = jnp.where(diff <= 0, jnp.exp(lgb_ref[2 * pair + h] * dn), 0.0)
        tab["d_" + name] = ef + eb
        tab["df_" + name] = ef * up
        tab["db_" + name] = eb * dn
    return tab


def _chunk(ref, n):
    return ref[pl.ds(pl.multiple_of(n * CHUNK, CHUNK), CHUNK), :]


def _pair_select(is_a, a, b):
    return jnp.where(is_a, a, b)


def _group_sum(is_a, v):
    sa = jnp.sum(jnp.where(is_a, v, 0.0), axis=1, keepdims=True)
    sb = jnp.sum(jnp.where(is_a, 0.0, v), axis=1, keepdims=True)
    return jnp.where(is_a, sa, sb)


def _retention_states(tab, k_ref, v_ref, n_chunks, rf, rb, extra=None):
    diag = tab["diag"]
    zero = jnp.zeros((LANES, LANES), F32)

    def kv(n, kdec):
        k = _chunk(k_ref, n).astype(F32) * Q_SCALE
        return jnp.where(diag, _dot_tn((k * kdec).astype(BF16), _chunk(v_ref, n)), 0.0)

    def up_body(n, carry):
        r, e = carry
        rf[n] = r
        r = r * tab["lam_f"] + kv(n, tab["kdec_f"])
        if extra is not None:
            ef, eb, gfun = extra
            eb[n] = e
            e = e * tab["lam_b"] + gfun(n, tab["qdec_b"])
        return r, e

    def down_body(s, carry):
        n = n_chunks - 1 - s
        r, e = carry
        rb[n] = r
        r = r * tab["lam_b"] + kv(n, tab["kdec_b"])
        if extra is not None:
            ef, eb, gfun = extra
            ef[n] = e
            e = e * tab["lam_f"] + gfun(n, tab["qdec_f"])
        return r, e

    lax.fori_loop(0, n_chunks, up_body, (zero, zero))
    lax.fori_loop(0, n_chunks, down_body, (zero, zero))


def _seq_spec(s, col_block):
    return pl.BlockSpec((s, LANES), lambda b, h: (b, col_block + h))


def _smem_spec():
    return pl.BlockSpec(memory_space=pltpu.SMEM)


def _retention_fwd(u, lgf, lgb, gn_gain, b_loc):
    t = u.shape[0]
    s = t // b_loc
    n_chunks = s // CHUNK
    pairs = RET_HEADS // 2

    def body(lgf_ref, lgb_ref, q_ref, k_ref, v_ref, g_ref, gain_ref, r_ref, y_ref, rf, rb):
        tab = _pair_tables(lgf_ref, lgb_ref, pl.program_id(1))
        is_a = tab["is_a"]
        _retention_states(tab, k_ref, v_ref, n_chunks, rf, rb)

        def out_body(n, _):
            q = _chunk(q_ref, n)
            k8 = (_chunk(k_ref, n).astype(F32) * Q_SCALE).astype(BF16)
            v = _chunk(v_ref, n)
            zero = jnp.zeros_like(q)
            sa = _dot_nt(jnp.where(is_a, q, zero), k8) * tab["d_a"]
            sb = _dot_nt(jnp.where(is_a, zero, q), k8) * tab["d_b"]
            y = _pair_select(is_a, _dot_nn(sa.astype(BF16), v), _dot_nn(sb.astype(BF16), v))
            qf = q.astype(F32)
            y = y + _dot_nn((qf * tab["qdec_f"]).astype(BF16), rf[n].astype(BF16))
            y = y + _dot_nn((qf * tab["qdec_b"]).astype(BF16), rb[n].astype(BF16))
            rows = pl.ds(pl.multiple_of(n * CHUNK, CHUNK), CHUNK)
            y_ref[rows, :] = y
            mu = _group_sum(is_a, y) * (1.0 / HEAD_DIM)
            dlt = y - mu
            var = _group_sum(is_a, dlt * dlt) * (1.0 / HEAD_DIM)
            xhat = dlt * lax.rsqrt(var + GN_EPS)
            gate = _chunk(g_ref, n).astype(F32)
            r_ref[rows, :] = (xhat * gain_ref[...] * gate * _sigmoid(gate)).astype(BF16)
            return 0

        lax.fori_loop(0, n_chunks, out_body, 0)

    lane_blk = lambda c0: _seq_spec(s, c0 // LANES)
    return pl.pallas_call(
        body, name="retention_fwd", grid=(b_loc, pairs),
        in_specs=[_smem_spec(), _smem_spec(), lane_blk(C_RQ), lane_blk(C_RK), lane_blk(C_RV), lane_blk(C_RG),
                  pl.BlockSpec((1, LANES), lambda b, h: (0, h))],
        out_specs=[_seq_spec(s, 0), _seq_spec(s, 0)],
        out_shape=[SDS((t, RET_W), BF16), SDS((t, RET_W), F32)],
        scratch_shapes=[pltpu.VMEM((n_chunks, LANES, LANES), F32)] * 2,
        compiler_params=_params(("parallel", "parallel")),
    )(lgf, lgb, u, u, u, u, gn_gain)


ST_GAIN, ST_XF, ST_XB, ST_IFA, ST_IFB, ST_IBA, ST_IBB, ST_LF, ST_LB = 0, 1, 2, 3, 4, 5, 6, 8, 9
ST_ROWS = 16


def _retention_bwd(u, y_pre, dr, lgf, lgb, gn_gain, b_loc):
    t = u.shape[0]
    s = t // b_loc
    n_chunks = s // CHUNK
    pairs = RET_HEADS // 2

    def body(lgf_ref, lgb_ref, q_ref, k_ref, v_ref, g_ref, y_ref, dr_ref, gain_ref,
             dq_ref, dk_ref, dv_ref, dg_ref, st_ref, rf, rb, ef, eb, dy_s):
        tab = _pair_tables(lgf_ref, lgb_ref, pl.program_id(1))
        is_a, row, diag = tab["is_a"], tab["row"], tab["diag"]
        gain = gain_ref[...]

        def norm_body(n, dgain):
            rows = pl.ds(pl.multiple_of(n * CHUNK, CHUNK), CHUNK)
            y = y_ref[rows, :]
            mu = _group_sum(is_a, y) * (1.0 / HEAD_DIM)
            dlt = y - mu
            var = _group_sum(is_a, dlt * dlt) * (1.0 / HEAD_DIM)
            rstd = lax.rsqrt(var + GN_EPS)
            xhat = dlt * rstd
            gate = g_ref[rows, :].astype(F32)
            sg = _sigmoid(gate)
            silu = gate * sg
            d_out = dr_ref[rows, :].astype(F32)
            dg_ref[rows, :] = (d_out * xhat * gain * (sg * (1.0 + gate * (1.0 - sg)))).astype(BF16)
            dxh = d_out * gain * silu
            m1 = _group_sum(is_a, dxh) * (1.0 / HEAD_DIM)
            m2 = _group_sum(is_a, dxh * xhat) * (1.0 / HEAD_DIM)
            dy_s[rows, :] = (rstd * (dxh - m1 - xhat * m2)).astype(BF16)
            return dgain + jnp.sum(d_out * xhat * silu, axis=0, keepdims=True)

        dgain = lax.fori_loop(0, n_chunks, norm_body, jnp.zeros((1, LANES), F32))

        def state_grad(n, qdec):
            qd = (_chunk(q_ref, n).astype(F32) * qdec).astype(BF16)
            return jnp.where(diag, _dot_tn(qd, _chunk(dy_s, n)), 0.0)

        _retention_states(tab, k_ref, v_ref, n_chunks, rf, rb, extra=(ef, eb, state_grad))

        def grad_body(n, carry):
            xf, xb, ifa, ifb, iba, ibb, lf, lb = carry
            rows = pl.ds(pl.multiple_of(n * CHUNK, CHUNK), CHUNK)
            q = q_ref[rows, :]
            qf = q.astype(F32)
            k8f = k_ref[rows, :].astype(F32) * Q_SCALE
            k8 = k8f.astype(BF16)
            v = v_ref[rows, :]
            dy = dy_s[rows, :]
            zero = jnp.zeros_like(q)
            dq = jnp.zeros((CHUNK, LANES), F32)
            dk = jnp.zeros((CHUNK, LANES), F32)
            dv = jnp.zeros((CHUNK, LANES), F32)
            intra = []
            for first, name in ((True, "a"), (False, "b")):
                sel = is_a if first else jnp.logical_not(is_a)
                sc = _dot_nt(jnp.where(sel, q, zero), k8)
                dp = _dot_nt(jnp.where(sel, dy, zero), v)
                a_mat = (sc * tab["d_" + name]).astype(BF16)
                ds_mat = (dp * tab["d_" + name]).astype(BF16)
                dq = dq + jnp.where(sel, _dot_nn(ds_mat, k8), 0.0)
                dk = dk + jnp.where(sel, _dot_tn(ds_mat, q), 0.0)
                dv = dv + jnp.where(sel, _dot_tn(a_mat, dy), 0.0)
                prod = sc * dp
                intra.append(jnp.sum(prod * tab["df_" + name], axis=0, keepdims=True))
                intra.append(jnp.sum(prod * tab["db_" + name], axis=0, keepdims=True))
            r_f, r_b = rf[n], rb[n]
            e_f, e_b = ef[n], eb[n]
            dqc_f = _dot_nt(dy, r_f.astype(BF16)) * tab["qdec_f"]
            dqc_b = _dot_nt(dy, r_b.astype(BF16)) * tab["qdec_b"]
            dkc_f = _dot_nt(v, e_f.astype(BF16)) * tab["kdec_f"]
            dkc_b = _dot_nt(v, e_b.astype(BF16)) * tab["kdec_b"]
            dv = dv + _dot_nn((k8f * tab["kdec_f"]).astype(BF16), e_f.astype(BF16))
            dv = dv + _dot_nn((k8f * tab["kdec_b"]).astype(BF16), e_b.astype(BF16))
            dq_ref[rows, :] = (dq + dqc_f + dqc_b).astype(BF16)
            dk_ref[rows, :] = ((dk + dkc_f + dkc_b) * Q_SCALE).astype(BF16)
            dv_ref[rows, :] = dv.astype(BF16)
            xf = xf + jnp.sum((row + 1.0) * qf * dqc_f + (CHUNK - 1.0 - row) * k8f * dkc_f, axis=0, keepdims=True)
            xb = xb + jnp.sum((CHUNK - row) * qf * dqc_b + row * k8f * dkc_b, axis=0, keepdims=True)
            lf = lf + jnp.sum(e_f * r_f, axis=0, keepdims=True)
            lb = lb + jnp.sum(e_b * r_b, axis=0, keepdims=True)
            return (xf, xb, ifa + intra[0], ifb + intra[2], iba + intra[1], ibb + intra[3], lf, lb)

        z = jnp.zeros((1, LANES), F32)
        xf, xb, ifa, ifb, iba, ibb, lf, lb = lax.fori_loop(0, n_chunks, grad_body, (z,) * 8)
        st_ref[...] = jnp.zeros_like(st_ref)
        st_ref[ST_GAIN:ST_GAIN + 1, :] = dgain
        st_ref[ST_XF:ST_XF + 1, :] = xf
        st_ref[ST_XB:ST_XB + 1, :] = xb
        st_ref[ST_IFA:ST_IFA + 1, :] = ifa
        st_ref[ST_IFB:ST_IFB + 1, :] = ifb
        st_ref[ST_IBA:ST_IBA + 1, :] = iba
        st_ref[ST_IBB:ST_IBB + 1, :] = ibb
        st_ref[ST_LF:ST_LF + 1, :] = lf * (CHUNK * tab["lam_f"])
        st_ref[ST_LB:ST_LB + 1, :] = lb * (CHUNK * tab["lam_b"])

    lane_blk = lambda c0: _seq_spec(s, c0 // LANES)
    seq0 = _seq_spec(s, 0)
    state = pltpu.VMEM((n_chunks, LANES, LANES), F32)
    return pl.pallas_call(
        body, name="retention_bwd", grid=(b_loc, pairs),
        in_specs=[_smem_spec(), _smem_spec(), lane_blk(C_RQ), lane_blk(C_RK), lane_blk(C_RV), lane_blk(C_RG),
                  seq0, seq0, pl.BlockSpec((1, LANES), lambda b, h: (0, h))],
        out_specs=[seq0] * 4 + [pl.BlockSpec((ST_ROWS, LANES), lambda b, h: (b, h))],
        out_shape=[SDS((t, RET_W), BF16)] * 4 + [SDS((b_loc * ST_ROWS, RET_W), F32)],
        scratch_shapes=[state] * 4 + [pltpu.VMEM((s, LANES), BF16)],
        compiler_params=_params(("parallel", "parallel")),
    )(lgf, lgb, u, u, u, u, y_pre, dr, gn_gain)


GW = GROUP * HEAD_DIM
KEYS = 3 * BLOCK


def _attn_tables(g, bias_ref):
    r = lax.broadcasted_iota(jnp.int32, (GROUP * BLOCK, KEYS), 0)
    kj = lax.broadcasted_iota(jnp.int32, (GROUP * BLOCK, KEYS), 1)
    qi = r & (BLOCK - 1)
    hh = lax.shift_right_logical(r, 7)
    dist = jnp.abs(kj - BLOCK - qi)
    slope = jnp.exp2(-(GROUP * g + hh + 1).astype(F32) * (8.0 / ATTN_HEADS))
    bias_ref[...] = jnp.where(dist <= BLOCK, -slope * dist.astype(F32), NEG_INF)


def _tile_keys(x_ref, g, scale, pad_ref, s):
    r = lax.broadcasted_iota(jnp.int32, (LANES, GW), 0)
    c = lax.broadcasted_iota(jnp.int32, (LANES, GW), 1)
    place = jnp.where(r == g * HEAD_DIM + (c & (HEAD_DIM - 1)), 1.0, 0.0).astype(BF16)
    pad_ref[0:BLOCK, :] = jnp.zeros((BLOCK, GW), BF16)
    pad_ref[BLOCK + s:2 * BLOCK + s, :] = jnp.zeros((BLOCK, GW), BF16)
    pad_ref[BLOCK:BLOCK + s, :] = (_dot_nn(x_ref[...], place) * scale).astype(BF16)


def _stack_heads(x):
    lane_h = lax.shift_right_logical(lax.broadcasted_iota(jnp.int32, (1, GW), 1), 6)
    zero = jnp.zeros_like(x)
    return jnp.concatenate([jnp.where(lane_h == h, x, zero) for h in range(GROUP)], axis=0)


def _unstack_heads(x4):
    lane_h = lax.shift_right_logical(lax.broadcasted_iota(jnp.int32, (1, GW), 1), 6)
    out = jnp.zeros((BLOCK, GW), F32)
    for h in range(GROUP):
        out = out + jnp.where(lane_h == h, x4[h * BLOCK:(h + 1) * BLOCK, :], 0.0)
    return out


def _sink_column(sink_ref, g):
    rh = lax.shift_right_logical(lax.broadcasted_iota(jnp.int32, (GROUP * BLOCK, 1), 0), 7)
    col = jnp.zeros((GROUP * BLOCK, 1), F32)
    for h in range(GROUP):
        col = jnp.where(rh == h, sink_ref[GROUP * g + h], col)
    return col


def _attn_probs(qm, k3, bias_ref, sink_col, n, s):
    logits = _dot_nt(qm, k3) + bias_ref[...]
    kpos = n * BLOCK - BLOCK + lax.broadcasted_iota(jnp.int32, (1, KEYS), 1)
    logits = jnp.where((kpos >= 0) & (kpos < s), logits, NEG_INF)
    m = jnp.maximum(jnp.max(logits, axis=1, keepdims=True), sink_col)
    e = jnp.exp(logits - m)
    e_sink = jnp.exp(sink_col - m)
    inv = 1.0 / (jnp.sum(e, axis=1, keepdims=True) + e_sink)
    return e * inv, e_sink * inv


def _attn_specs(s, n_blocks):
    q_spec = pl.BlockSpec((BLOCK, GW), lambda b, g, n: (b * n_blocks + n, C_AQ // GW + g))
    k_spec = pl.BlockSpec((s, LANES), lambda b, g, n: (b, C_AK // LANES))
    v_spec = pl.BlockSpec((s, LANES), lambda b, g, n: (b, C_AV // LANES))
    o_spec = pl.BlockSpec((BLOCK, GW), lambda b, g, n: (b * n_blocks + n, g))
    return q_spec, k_spec, v_spec, o_spec


def _attention_fwd(u, sink, b_loc):
    t = u.shape[0]
    s = t // b_loc
    n_blocks = s // BLOCK

    def body(sink_ref, q_ref, k_ref, v_ref, o_ref, kpad, vpad, bias):
        g, n = pl.program_id(1), pl.program_id(2)

        @pl.when(n == 0)
        def _():
            _attn_tables(g, bias)
            _tile_keys(k_ref, g, Q_SCALE, kpad, s)
            _tile_keys(v_ref, g, 1.0, vpad, s)

        keys = pl.ds(pl.multiple_of(n * BLOCK, BLOCK), KEYS)
        p, _ = _attn_probs(_stack_heads(q_ref[...]), kpad[keys, :], bias, _sink_column(sink_ref, g), n, s)
        o_ref[...] = _unstack_heads(_dot_nn(p.astype(BF16), vpad[keys, :])).astype(BF16)

    q_spec, k_spec, v_spec, o_spec = _attn_specs(s, n_blocks)
    pad = pltpu.VMEM((s + 2 * BLOCK, GW), BF16)
    return pl.pallas_call(
        body, name="attention_fwd", grid=(b_loc, KV_HEADS, n_blocks),
        in_specs=[_smem_spec(), q_spec, k_spec, v_spec], out_specs=o_spec,
        out_shape=SDS((t, ATT_W), BF16),
        scratch_shapes=[pad, pad, pltpu.VMEM((GROUP * BLOCK, KEYS), F32)],
        compiler_params=_params(("parallel", "arbitrary", "arbitrary")),
    )(sink, u, u, u)


def _fold_groups(x, g):
    x = x + pltpu.roll(x, 2 * HEAD_DIM, 1)
    x = x + pltpu.roll(x, HEAD_DIM, 1)
    lane_g = lax.shift_right_logical(lax.broadcasted_iota(jnp.int32, (1, LANES), 1), 6)
    return jnp.where(lane_g == g, x[:, 0:LANES], 0.0)


def _attention_bwd(u, da, sink, b_loc):
    t = u.shape[0]
    s = t // b_loc
    n_blocks = s // BLOCK

    def body(sink_ref, q_ref, k_ref, v_ref, do_ref, dq_ref, dk_ref, dv_ref, dsink_ref, kpad, vpad, bias, dk_acc, dv_acc):
        g, n = pl.program_id(1), pl.program_id(2)

        @pl.when(n == 0)
        def _():
            _attn_tables(g, bias)
            _tile_keys(k_ref, g, Q_SCALE, kpad, s)
            _tile_keys(v_ref, g, 1.0, vpad, s)
            dsink_ref[...] = jnp.zeros_like(dsink_ref)

        @pl.when((n == 0) & (g == 0))
        def _():
            dk_acc[...] = jnp.zeros_like(dk_acc)
            dv_acc[...] = jnp.zeros_like(dv_acc)

        keys = pl.ds(pl.multiple_of(n * BLOCK, BLOCK), KEYS)
        qm = _stack_heads(q_ref[...])
        k3, v3 = kpad[keys, :], vpad[keys, :]
        p, p_sink = _attn_probs(qm, k3, bias, _sink_column(sink_ref, g), n, s)
        dom = _stack_heads(do_ref[...])
        dp = _dot_nt(dom, v3)
        delta = jnp.sum(p * dp, axis=1, keepdims=True)
        ds_mat = (p * (dp - delta)).astype(BF16)
        dq_ref[...] = _unstack_heads(_dot_nn(ds_mat, k3)).astype(BF16)
        dk_acc[keys, :] += _fold_groups(_dot_tn(ds_mat, qm), g) * Q_SCALE
        dv_acc[keys, :] += _fold_groups(_dot_tn(p.astype(BF16), dom), g)
        w = p_sink * delta
        rows = lax.broadcasted_iota(jnp.int32, dsink_ref.shape, 0)
        upd = jnp.zeros(dsink_ref.shape, F32)
        for h in range(GROUP):
            upd = jnp.where(rows == h, -jnp.sum(w[h * BLOCK:(h + 1) * BLOCK, :]), upd)
        dsink_ref[...] += upd

        @pl.when((n == n_blocks - 1) & (g == KV_HEADS - 1))
        def _():
            dk_ref[...] = dk_acc[BLOCK:BLOCK + s, :].astype(BF16)
            dv_ref[...] = dv_acc[BLOCK:BLOCK + s, :].astype(BF16)

    q_spec, k_spec, v_spec, o_spec = _attn_specs(s, n_blocks)
    kv_out = pl.BlockSpec((s, LANES), lambda b, g, n: (b, 0))
    pad = pltpu.VMEM((s + 2 * BLOCK, GW), BF16)
    acc = pltpu.VMEM((s + 2 * BLOCK, LANES), F32)
    return pl.pallas_call(
        body, name="attention_bwd", grid=(b_loc, KV_HEADS, n_blocks),
        in_specs=[_smem_spec(), q_spec, k_spec, v_spec, o_spec],
        out_specs=[o_spec, kv_out, kv_out, pl.BlockSpec((8, LANES), lambda b, g, n: (b * KV_HEADS + g, 0))],
        out_shape=[SDS((t, ATT_W), BF16), SDS((t, KV_W), BF16), SDS((t, KV_W), BF16),
                   SDS((b_loc * KV_HEADS * 8, LANES), F32)],
        scratch_shapes=[pad, pad, pltpu.VMEM((GROUP * BLOCK, KEYS), F32), acc, acc],
        compiler_params=_params(("arbitrary", "arbitrary", "arbitrary")),
    )(sink, u, u, u, da)


def _pack_small(acc2, acc1, ret_stats, dsink, b_loc, d):
    pairs = RET_HEADS // 2

    def body(acc2_ref, acc1_ref, st_ref, dsink_ref, out_ref):
        out_ref[...] = jnp.zeros_like(out_ref)
        out_ref[ROW_LN1G:ROW_LN1G + 1, :] = acc1_ref[0:1, :]
        out_ref[ROW_LN1B:ROW_LN1B + 1, :] = acc1_ref[1:2, :]
        out_ref[ROW_LN2G:ROW_LN2G + 1, :] = acc2_ref[1:2, :]
        out_ref[ROW_LN2B:ROW_LN2B + 1, :] = acc2_ref[2:3, :]
        out_ref[ROW_LOSS:ROW_LOSS + 1, :] = acc2_ref[0:1, :]
        st = st_ref[0:ST_ROWS, :]
        for b in range(1, b_loc):
            st = st + st_ref[b * ST_ROWS:(b + 1) * ST_ROWS, :]
        out_ref[ROW_GN:ROW_GN + 1, 0:RET_W] = st[ST_GAIN:ST_GAIN + 1, :]
        lane = lax.broadcasted_iota(jnp.int32, (1, d), 1)
        misc = jnp.zeros((1, d), F32)
        for pr in range(pairs):
            blk = st[:, pr * LANES:(pr + 1) * LANES]
            half = lax.broadcasted_iota(jnp.int32, (1, LANES), 1) < HEAD_DIM
            for h in range(2):
                sel = half if h == 0 else jnp.logical_not(half)
                cross_f = jnp.sum(jnp.where(sel, blk[ST_XF:ST_XF + 1, :] + blk[ST_LF:ST_LF + 1, :], 0.0))
                cross_b = jnp.sum(jnp.where(sel, blk[ST_XB:ST_XB + 1, :] + blk[ST_LB:ST_LB + 1, :], 0.0))
                intra_f = jnp.sum(blk[ST_IFA + h:ST_IFA + h + 1, :])
                intra_b = jnp.sum(blk[ST_IBA + h:ST_IBA + h + 1, :])
                head = 2 * pr + h
                misc = jnp.where(lane == MISC_DF + head, cross_f + intra_f, misc)
                misc = jnp.where(lane == MISC_DB + head, cross_b + intra_b, misc)
        for g in range(KV_HEADS):
            tot = dsink_ref[g * 8:(g + 1) * 8, :]
            for b in range(1, b_loc):
                tot = tot + dsink_ref[(b * KV_HEADS + g) * 8:(b * KV_HEADS + g + 1) * 8, :]
            for h in range(GROUP):
                misc = jnp.where(lane == MISC_SINK + GROUP * g + h, jnp.sum(tot[h:h + 1, 0:1]), misc)
        out_ref[ROW_MISC:ROW_MISC + 1, :] = misc

    return pl.pallas_call(body, name="pack_small", out_shape=SDS((SMALL_ROWS, d), F32))(acc2, acc1, ret_stats, dsink)


BIG = ("w_in", "w_out", "w_ffn_gate", "w_ffn_up", "w_ffn_down", "w_ple_proj", "w_ple_gate")
TRANSPOSED_OUTSIDE = ("w_in", "w_ffn_gate", "w_ffn_up")
TRANSPOSED_HERE = ("w_ple_proj",)
SMALL = ("ret_decay_fwd", "ret_decay_bwd", "ret_gn_gain", "attn_sink", "ln1_gain", "ln1_bias", "ln2_gain", "ln2_bias")
ORDER = ("w_in", "ret_decay_fwd", "ret_decay_bwd", "ret_gn_gain", "attn_sink", "w_out", "ln1_gain", "ln1_bias",
         "w_ffn_gate", "w_ffn_up", "w_ffn_down", "w_ple_proj", "w_ple_gate", "ln2_gain", "ln2_bias")


GATHER_ORDER = ("w_in", "w_out", "w_ffn_gate", "w_ffn_up", "w_ple_gate", "w_ple_proj", "w_ffn_down")


def _local_step(x2, p2, target2, fetch, publish, small, b_loc, me):
    d = x2.shape[1]
    lgf, lgb = _log_decay(small["ret_decay_fwd"], small["ret_decay_bwd"])
    lgf1, lgb1, sink1 = lgf.reshape(-1), lgb.reshape(-1), small["attn_sink"].reshape(-1)
    (w_in,) = fetch(("w_in",), ())
    u, xb = _in_proj(x2, w_in)
    r, y_pre = _retention_fwd(u, lgf1, lgb1, small["ret_gn_gain"], b_loc)
    a = _attention_fwd(u, sink1, b_loc)
    w_out, w_gate, w_up = fetch(("w_out", "w_ffn_gate", "w_ffn_up"), (r, a))
    z1, h1b = _out_proj_ln1(r, a, x2, w_out, small["ln1_gain"], small["ln1_bias"])
    g, up, act = _ffn_up(h1b, w_gate, w_up)
    w_pg, w_pe, w_down = fetch(("w_ple_gate", "w_ple_proj", "w_ffn_down"), (act,))
    dz2, dz2b, dsb, dpleb, acc2 = _ffn_down_ln2_loss(
        act, h1b, p2, z1, target2, w_down, w_pg, w_pe,
        small["ln1_gain"], small["ln1_bias"], small["ln2_gain"], small["ln2_bias"])
    own = {}

    def grad(name, parts, rhs):
        whole, own[name] = _weight_grad("grad_" + name, me, parts, rhs)
        return whole

    t1 = publish("ffn_down", dict(w_ffn_down=grad("w_ffn_down", [act], dz2b),
                                  w_ple_proj=grad("w_ple_proj", [dpleb], p2),
                                  w_ple_gate=grad("w_ple_gate", [h1b], dsb)))
    dg, dup = _ffn_bwd_act(dz2b, g, up, w_down, t1)
    t2 = publish("ffn_up", dict(w_ffn_gate=grad("w_ffn_gate", [dg], h1b), w_ffn_up=grad("w_ffn_up", [dup], h1b)))
    dz1, dz1b, acc1 = _dh1_ln1_bwd(dz2, dg, dup, dsb, z1, w_gate, w_up, w_pg, small["ln1_gain"], t2)
    t3 = publish("out", dict(w_out=grad("w_out", [r, a], dz1b)))
    dr, da = _out_proj_bwd(dz1b, w_out, t3)
    dq, dk, dv, dgate, ret_stats = _retention_bwd(u, y_pre, dr, lgf1, lgb1, small["ret_gn_gain"], b_loc)
    daq, dak, dav, dsink = _attention_bwd(u, da, sink1, b_loc)
    parts = [dq, dk, dv, dgate, daq, dak, dav]
    t4 = publish("in", dict(w_in=grad("w_in", parts, xb)))
    grad_x = _in_proj_bwd(dz1, parts, w_in, t4)
    small_part = _pack_small(acc2, acc1, ret_stats, dsink, b_loc, d)
    return grad_x, own, small_part


def kernel(x, p, w_in, ret_decay_fwd, ret_decay_bwd, ret_gn_gain, attn_sink, w_out, ln1_gain, ln1_bias, w_ffn_gate, w_ffn_up, w_ffn_down, w_ple_proj, w_ple_gate, ln2_gain, ln2_bias, loss_target, m_w_in, m_ret_decay_fwd, m_ret_decay_bwd, m_ret_gn_gain, m_attn_sink, m_w_out, m_ln1_gain, m_ln1_bias, m_w_ffn_gate, m_w_ffn_up, m_w_ffn_down, m_w_ple_proj, m_w_ple_gate, m_ln2_gain, m_ln2_bias, v_w_in, v_ret_decay_fwd, v_ret_decay_bwd, v_ret_gn_gain, v_attn_sink, v_w_out, v_ln1_gain, v_ln1_bias, v_w_ffn_gate, v_w_ffn_up, v_w_ffn_down, v_w_ple_proj, v_w_ple_gate, v_ln2_gain, v_ln2_bias):
    given = dict(locals())

    def strip(n, a):
        if n not in BIG:
            return a
        return a[0].T if n in TRANSPOSED_OUTSIDE else a[0]

    def restore(n, a):
        if n not in BIG:
            return a
        return (a.T if n in TRANSPOSED_OUTSIDE else a)[None]

    w = {n: strip(n, given[n]) for n in ORDER}
    m = {n: strip(n, given["m_" + n]) for n in ORDER}
    v = {n: strip(n, given["v_" + n]) for n in ORDER}
    b_loc, s, d = x.shape
    x2 = x.reshape(b_loc * s, d)
    p2 = p[0].reshape(b_loc * s, p.shape[-1])
    target2 = loss_target.reshape(b_loc * s, d)

    small = {n: w[n] for n in SMALL}
    me = (4 * lax.axis_index("x") + 2 * lax.axis_index("y") + lax.axis_index("c")).astype(jnp.int32).reshape(1)

    gathered = _prep_shards(me, {n: w[n] for n in BIG})
    gather = _split_copy_start("gather_start", [(gathered[n],) for n in GATHER_ORDER], _gather_copy)

    def fetch(names, after):
        which = [GATHER_ORDER.index(n) for n in names]
        got = _split_copy_wait("gather_wait_" + names[0], gather, which, _gather_copy, list(after))
        return [item[0] for item in got]

    scatters = []

    def publish(tag, products):
        names = list(products)
        items = [(products[n], lax.empty((N_DEV, products[n].shape[0] // N_DEV, products[n].shape[1]), BF16))
                 for n in names]
        started = _split_copy_start("scatter_start_" + tag, items, _scatter_copy)
        scatters.append((tag, names, started))
        return (started["token"],)

    grad_x, own, small_part = _local_step(x2, p2, target2, fetch, publish, small, b_loc, me)

    out_g, out_d, out_m, out_v = {}, {}, {}, {}
    after = [grad_x]
    for tag, names, started in scatters:
        landed = _split_copy_wait("scatter_wait_" + tag, started, list(range(len(names))), _scatter_copy, after)
        for n, (_, recv) in zip(names, landed):
            out_g[n], out_d[n], out_m[n], out_v[n] = _reduce_adamw(
                n, own[n], recv, w[n], m[n], v[n], n in TRANSPOSED_HERE)
        after = [out_v[names[-1]]]
        if tag == "out":
            loss, sg, sd, sm, sv = _small_all_reduce_adamw(
                small_part, small, {n: m[n] for n in SMALL}, {n: v[n] for n in SMALL})
            for dst, src in ((out_g, sg), (out_d, sd), (out_m, sm), (out_v, sv)):
                dst.update(src)
            after.append(sv["ln2_bias"])

    outs = [loss[0, 0], grad_x.reshape(x.shape)]
    for group in (out_g, out_d, out_m, out_v):
        outs += [restore(n, group[n]) for n in ORDER]
    return tuple(outs)
```

```python
import functools

import jax
import jax.numpy as jnp
from jax import lax
from jax.experimental import pallas as pl
from jax.experimental.pallas import tpu as pltpu

F32, BF16 = jnp.float32, jnp.bfloat16
SDS = jax.ShapeDtypeStruct
MESH = pl.DeviceIdType.MESH

N_DEV = 8
HEAD_DIM = 64
RET_HEADS = 8
ATTN_HEADS = 8
KV_HEADS = 2
GROUP = ATTN_HEADS // KV_HEADS
RET_W = RET_HEADS * HEAD_DIM
ATT_W = ATTN_HEADS * HEAD_DIM
KV_W = KV_HEADS * HEAD_DIM
LANES = 128
CHUNK = 128
BLOCK = 128
Q_SCALE = HEAD_DIM ** -0.5
ALPHA = 2.0 ** 0.25
LN_EPS = 1e-5
GN_EPS = 1e-5
NEG_INF = -1e30
C_RQ, C_RK, C_RV, C_RG = 0, RET_W, 2 * RET_W, 3 * RET_W
C_AQ = 4 * RET_W
C_AK = C_AQ + ATT_W
C_AV = C_AK + KV_W
IN_W = C_AV + KV_W

ADAM_LR = 0.001
ADAM_B1 = 0.9
ADAM_B2 = 0.999
ADAM_EPS = 1e-08
ADAM_WD = 0.01
ADAM_STEP = 10

VMEM_LIMIT = 48 * 1024 * 1024
SMALL_ROWS = 16
ROW_LN1G, ROW_LN1B, ROW_LN2G, ROW_LN2B, ROW_LOSS, ROW_GN, ROW_MISC = 0, 1, 2, 3, 4, 5, 6
MISC_DF, MISC_DB, MISC_SINK = 0, 8, 16


def _dot_nn(a, b):
    return lax.dot_general(a, b, (((1,), (0,)), ((), ())), preferred_element_type=F32)


def _dot_nt(a, b):
    return lax.dot_general(a, b, (((1,), (1,)), ((), ())), preferred_element_type=F32)


def _dot_tn(a, b):
    return lax.dot_general(a, b, (((0,), (0,)), ((), ())), preferred_element_type=F32)


def _params(sem=None, vmem=VMEM_LIMIT):
    kw = {"vmem_limit_bytes": vmem}
    if sem is not None:
        kw["dimension_semantics"] = sem
    return pltpu.CompilerParams(**kw)


def _row_tile(t, want=512):
    tm = want
    while t % tm:
        tm //= 2
    return tm


def _sigmoid(x):
    return 1.0 / (1.0 + jnp.exp(-x))


def _layer_norm_stats(z):
    mu = jnp.mean(z, axis=1, keepdims=True)
    d = z - mu
    var = jnp.mean(d * d, axis=1, keepdims=True)
    rstd = lax.rsqrt(var + LN_EPS)
    return d * rstd, rstd


def _layer_norm_bwd(dxh, xhat, rstd):
    m1 = jnp.mean(dxh, axis=1, keepdims=True)
    m2 = jnp.mean(dxh * xhat, axis=1, keepdims=True)
    return rstd * (dxh - m1 - xhat * m2)


def _prep_shards(me, shards):
    names = list(shards)

    def body(me_ref, *refs):
        for name, src, dst in zip(names, refs[:len(names)], refs[len(names):]):
            val = src[...]
            dst[...] = (val.T if name in TRANSPOSED_HERE else val).astype(BF16)

    shape = lambda n, a: a.shape[::-1] if n in TRANSPOSED_HERE else a.shape
    shapes = [shape(n, shards[n]) for n in names]
    out = pl.pallas_call(
        body, name="prep_shards",
        grid_spec=pltpu.PrefetchScalarGridSpec(
            num_scalar_prefetch=1, grid=(1,),
            in_specs=[pl.BlockSpec(shards[n].shape, lambda i, me_ref: (0, 0)) for n in names],
            out_specs=[pl.BlockSpec(s, lambda i, me_ref: (me_ref[0], 0)) for s in shapes]),
        out_shape=[SDS((N_DEV * s[0], s[1]), BF16) for s in shapes], compiler_params=_params(("arbitrary",)),
    )(me, *[shards[n] for n in names])
    return dict(zip(names, out))


def _mesh_pos():
    return lax.axis_index("x"), lax.axis_index("y"), lax.axis_index("c")


HBM_SPEC = pl.BlockSpec(memory_space=pltpu.HBM)
SEM_SPEC = pl.BlockSpec(memory_space=pltpu.SEMAPHORE)
ANY_SPEC = pl.BlockSpec(memory_space=pl.ANY)
SIDE_EFFECT = pltpu.SideEffectType.DATAFLOW_SIDE_EFFECTING
PEER_SEMS = pltpu.SemaphoreType.DMA((N_DEV - 1,))


def _in_hbm(a):
    return pltpu.with_memory_space_constraint(a, pltpu.HBM)


def _split_copy_start(name, items, copy_of):
    n = len(items)
    flat = [a for it in items for a in it]
    k = len(flat)

    def body(*refs):
        arr, sems = list(refs[:k]), refs[k:k + 2 * n]
        for i, it in enumerate(items):
            mine = [arr.pop(0) for _ in it]
            for m in range(1, N_DEV):
                copy_of(m, mine, sems[i].at[m - 1], sems[n + i].at[m - 1]).start()
        token = refs[-1]
        token[...] = jnp.zeros_like(token)

    res = pl.pallas_call(
        body, name=name,
        out_shape=[PEER_SEMS] * (2 * n) + [pltpu.HBM(a.shape, a.dtype) for a in flat] + [SDS((8, LANES), F32)],
        in_specs=[HBM_SPEC] * k,
        out_specs=[SEM_SPEC] * (2 * n) + [HBM_SPEC] * k + [pl.BlockSpec(memory_space=pltpu.VMEM)],
        input_output_aliases={j: 2 * n + j for j in range(k)},
        compiler_params=pltpu.CompilerParams(has_side_effects=SIDE_EFFECT),
    )(*[_in_hbm(a) for a in flat])
    thru, out_items = list(res[2 * n:2 * n + k]), []
    for it in items:
        out_items.append(tuple(thru.pop(0) for _ in it))
    return dict(send=res[:n], recv=res[n:2 * n], items=out_items, token=res[-1])


def _split_copy_wait(name, started, which, copy_of, after):
    items = [started["items"][i] for i in which]
    n = len(items)
    flat = [a for it in items for a in it]
    k = len(flat)

    def body(*refs):
        arr, sems = list(refs[:k]), refs[k:k + 2 * n]
        for i, it in enumerate(items):
            mine = [arr.pop(0) for _ in it]
            for m in range(1, N_DEV):
                cp = copy_of(m, mine, sems[i].at[m - 1], sems[n + i].at[m - 1])
                cp.wait_send()
                cp.wait_recv()

    res = pl.pallas_call(
        body, name=name,
        out_shape=[pltpu.HBM(a.shape, a.dtype) for a in flat],
        in_specs=[HBM_SPEC] * k + [SEM_SPEC] * (2 * n) + [ANY_SPEC] * len(after),
        out_specs=[HBM_SPEC] * k,
        input_output_aliases={j: j for j in range(k)},
        compiler_params=pltpu.CompilerParams(has_side_effects=SIDE_EFFECT),
    )(*flat, *[started["send"][i] for i in which], *[started["recv"][i] for i in which], *[_in_hbm(a) for a in after])
    thru, out_items = list(res), []
    for it in items:
        out_items.append(tuple(thru.pop(0) for _ in it))
    return out_items


def _gather_copy(m, refs, send_sem, recv_sem):
    (land_ref,) = refs
    r = land_ref.shape[0] // N_DEV
    mine = land_ref.at[pl.ds(pl.multiple_of(_peer_index(0) * r, 8), r), :]
    return pltpu.make_async_remote_copy(src_ref=mine, dst_ref=mine, send_sem=send_sem, recv_sem=recv_sem,
                                        device_id=_peer(m), device_id_type=MESH)


def _scatter_copy(m, refs, send_sem, recv_sem):
    buf_ref, land_ref = refs
    r = buf_ref.shape[0] // N_DEV
    src = buf_ref.at[pl.ds(pl.multiple_of(_peer_index(m) * r, 8), r), :]
    return pltpu.make_async_remote_copy(src_ref=src, dst_ref=land_ref.at[m], send_sem=send_sem, recv_sem=recv_sem,
                                        device_id=_peer(m), device_id_type=MESH)


def _peer(m):
    x, y, c = _mesh_pos()
    bx, by, bc = (m >> 2) & 1, (m >> 1) & 1, m & 1
    return (x ^ bx if bx else x, y ^ by if by else y, c ^ bc if bc else c)


def _peer_index(m):
    x, y, c = _mesh_pos()
    return (4 * x + 2 * y + c) ^ m


SMALL_PLACE = {
    "ln1_gain": (ROW_LN1G, 0), "ln1_bias": (ROW_LN1B, 0), "ln2_gain": (ROW_LN2G, 0), "ln2_bias": (ROW_LN2B, 0),
    "ret_gn_gain": (ROW_GN, 0), "ret_decay_fwd": (ROW_MISC, MISC_DF), "ret_decay_bwd": (ROW_MISC, MISC_DB),
    "attn_sink": (ROW_MISC, MISC_SINK)}


def _small_all_reduce_adamw(part, w, m, v):
    d = part.shape[1]
    names = list(SMALL_PLACE)
    k = len(names)

    def body(*refs):
        part_ref = refs[0]
        w_refs, m_refs, v_refs = refs[1:1 + k], refs[1 + k:1 + 2 * k], refs[1 + 2 * k:1 + 3 * k]
        outs = refs[1 + 3 * k:2 + 7 * k]
        loss_ref, g_refs, dl_refs = outs[0], outs[1:1 + k], outs[1 + k:1 + 2 * k]
        nm_refs, nv_refs = outs[1 + 2 * k:1 + 3 * k], outs[1 + 3 * k:1 + 4 * k]
        buf, send_sems, recv_sems = refs[2 + 7 * k:]
        buf[0] = part_ref[...]
        cps = [pltpu.make_async_remote_copy(
            src_ref=part_ref, dst_ref=buf.at[j], send_sem=send_sems.at[j - 1], recv_sem=recv_sems.at[j - 1],
            device_id=_peer(j), device_id_type=MESH) for j in range(1, N_DEV)]
        for cp in cps:
            cp.start()
        for cp in cps:
            cp.wait_recv()
        for cp in cps:
            cp.wait_send()
        me = _peer_index(0)
        tot = buf[me]
        for dev in range(1, N_DEV):
            tot = tot + buf[dev ^ me]
        loss_ref[...] = (0.5 / d) * jnp.sum(tot[ROW_LOSS:ROW_LOSS + 1, :], axis=1, keepdims=True)
        for i, name in enumerate(names):
            row, lo = SMALL_PLACE[name]
            wv = w_refs[i][...]
            g = tot[row:row + 1, lo:lo + wv.shape[1]]
            if name.startswith("ret_decay"):
                p2 = jnp.exp2(wv)
                g = g * (-p2 * jnp.log(2.0) / (1.0 - p2))
            g_refs[i][...] = g
            _adamw_store(g, wv, m_refs[i][...], v_refs[i][...], dl_refs[i], nm_refs[i], nv_refs[i])

    vm = pl.BlockSpec(memory_space=pltpu.VMEM)
    shapes = [SDS(w[n].shape, F32) for n in names]
    res = pl.pallas_call(
        body, name="small_all_reduce_adamw",
        out_shape=[SDS((1, 1), F32)] + shapes * 4, in_specs=[vm] * (1 + 3 * k), out_specs=[vm] * (1 + 4 * k),
        scratch_shapes=[pltpu.VMEM((N_DEV,) + part.shape, F32), pltpu.SemaphoreType.DMA((7,)),
                        pltpu.SemaphoreType.DMA((7,))],
    )(part, *[w[n] for n in names], *[m[n] for n in names], *[v[n] for n in names])
    groups = [dict(zip(names, res[1 + j * k:1 + (j + 1) * k])) for j in range(4)]
    return (res[0], *groups)


def _adamw_store(g, w, m, v, dl_ref, nm_ref, nv_ref):
    m = ADAM_B1 * m + (1.0 - ADAM_B1) * g
    v = ADAM_B2 * v + (1.0 - ADAM_B2) * (g * g)
    m_hat = m / (1.0 - ADAM_B1 ** ADAM_STEP)
    v_hat = v / (1.0 - ADAM_B2 ** ADAM_STEP)
    dl_ref[...] = -ADAM_LR * (m_hat / (jnp.sqrt(v_hat) + ADAM_EPS) + ADAM_WD * w)
    nm_ref[...] = m
    nv_ref[...] = v


def _reduce_adamw(name, own, recv, w, m, v, transposed):
    def body(own_ref, recv_ref, w_ref, m_ref, v_ref, g_ref, dl_ref, nm_ref, nv_ref):
        g = own_ref[...]
        for k in range(1, N_DEV):
            g = g + recv_ref[k].astype(F32)
        if transposed:
            g = g.T
        g_ref[...] = g
        _adamw_store(g, w_ref[...], m_ref[...], v_ref[...], dl_ref, nm_ref, nv_ref)

    out = SDS(w.shape, F32)
    return pl.pallas_call(body, name="adamw_" + name, out_shape=[out] * 4, compiler_params=_params())(own, recv, w, m, v)


def _row_spec(tm, width):
    return pl.BlockSpec((tm, width), lambda i: (i, 0))


def _full_spec(shape):
    return pl.BlockSpec(shape, lambda i: (0,) * len(shape))


def _in_proj(x2, wt_in):
    t, d = x2.shape
    u_w = wt_in.shape[0]
    tm = _row_tile(t)

    def body(x_ref, w_ref, u_ref, xb_ref):
        xb = x_ref[...].astype(BF16)
        xb_ref[...] = xb
        u_ref[...] = _dot_nt(xb, w_ref[...]).astype(BF16)

    return pl.pallas_call(
        body, name="in_proj", grid=(t // tm,),
        in_specs=[_row_spec(tm, d), _full_spec(wt_in.shape)],
        out_specs=[_row_spec(tm, u_w), _row_spec(tm, d)],
        out_shape=[SDS((t, u_w), BF16), SDS((t, d), BF16)],
        compiler_params=_params(("parallel",)),
    )(x2, wt_in)


def _out_proj_ln1(r, a, x2, w_out, g1, b1):
    t, d = x2.shape
    tm = _row_tile(t)

    def body(r_ref, a_ref, x_ref, wo_ref, g_ref, b_ref, z_ref, hb_ref):
        mix = _dot_nn(r_ref[...], wo_ref[0:RET_W, :]) + _dot_nn(a_ref[...], wo_ref[RET_W:RET_W + ATT_W, :])
        z = ALPHA * x_ref[...] + mix
        xhat, _ = _layer_norm_stats(z)
        z_ref[...] = z
        hb_ref[...] = (xhat * g_ref[...] + b_ref[...]).astype(BF16)

    return pl.pallas_call(
        body, name="out_proj_ln1", grid=(t // tm,),
        in_specs=[_row_spec(tm, RET_W), _row_spec(tm, ATT_W), _row_spec(tm, d), _full_spec(w_out.shape),
                  _full_spec(g1.shape), _full_spec(b1.shape)],
        out_specs=[_row_spec(tm, d), _row_spec(tm, d)],
        out_shape=[SDS((t, d), F32), SDS((t, d), BF16)],
        compiler_params=_params(("parallel",)),
    )(r, a, x2, w_out, g1, b1)


def _col_halves(f):
    n = f // LANES
    k = (n + 1) // 2 * LANES
    return [(0, k), (k, f)] if k < f else [(0, f)]


def _ffn_up(h1b, wt_gate, wt_up):
    t, d = h1b.shape
    f = wt_gate.shape[0]
    tm = _row_tile(t, 256)

    def body(h_ref, wg_ref, wu_ref, g_ref, u_ref, act_ref):
        h = h_ref[...]
        for lo, hi in _col_halves(f):
            g = _dot_nt(h, wg_ref[lo:hi, :])
            u = _dot_nt(h, wu_ref[lo:hi, :])
            g_ref[:, lo:hi] = g.astype(BF16)
            u_ref[:, lo:hi] = u.astype(BF16)
            act_ref[:, lo:hi] = (g * _sigmoid(g) * u).astype(BF16)

    return pl.pallas_call(
        body, name="ffn_up", grid=(t // tm,),
        in_specs=[_row_spec(tm, d), _full_spec(wt_gate.shape), _full_spec(wt_up.shape)],
        out_specs=[_row_spec(tm, f)] * 3,
        out_shape=[SDS((t, f), BF16)] * 3,
        compiler_params=_params(("parallel",)),
    )(h1b, wt_gate, wt_up)


def _ffn_down_ln2_loss(act, h1b, p2, z1, target, w_down, w_pg, wt_pe, g1, b1, g2, b2):
    t, d = z1.shape
    f = act.shape[1]
    pdim = p2.shape[1]
    tm = _row_tile(t, 256)

    def body(act_ref, hb_ref, p_ref, z1_ref, tgt_ref, wd_ref, wpg_ref, wpe_ref, g1_ref, b1_ref, g2_ref, b2_ref,
             dz_ref, dzb_ref, ds_ref, dple_ref, acc_ref):
        @pl.when(pl.program_id(0) == 0)
        def _():
            acc_ref[...] = jnp.zeros_like(acc_ref)

        xhat1, _ = _layer_norm_stats(z1_ref[...])
        h1 = xhat1 * g1_ref[...] + b1_ref[...]
        ffn = _dot_nn(act_ref[...], wd_ref[...])
        pg = _sigmoid(_dot_nn(hb_ref[...], wpg_ref[...]))
        ple = _dot_nt(p_ref[...].astype(BF16), wpe_ref[...])
        z2 = ALPHA * h1 + ffn + pg * ple
        xhat2, rstd2 = _layer_norm_stats(z2)
        err = xhat2 * g2_ref[...] + b2_ref[...] - tgt_ref[...]
        dy = err * (1.0 / d)
        dz = _layer_norm_bwd(dy * g2_ref[...], xhat2, rstd2)
        dz_ref[...] = dz
        dzb_ref[...] = dz.astype(BF16)
        ds_ref[...] = (dz * ple * pg * (1.0 - pg)).astype(BF16)
        dple_ref[...] = (dz * pg).astype(BF16)
        acc_ref[0:1, :] += jnp.sum(err * err, axis=0, keepdims=True)
        acc_ref[1:2, :] += jnp.sum(dy * xhat2, axis=0, keepdims=True)
        acc_ref[2:3, :] += jnp.sum(dy, axis=0, keepdims=True)

    vec = _full_spec(g1.shape)
    return pl.pallas_call(
        body, name="ffn_down_ln2_loss", grid=(t // tm,),
        in_specs=[_row_spec(tm, f), _row_spec(tm, d), _row_spec(tm, pdim), _row_spec(tm, d), _row_spec(tm, d),
                  _full_spec(w_down.shape), _full_spec(w_pg.shape), _full_spec(wt_pe.shape), vec, vec, vec, vec],
        out_specs=[_row_spec(tm, d)] * 4 + [_full_spec((8, d))],
        out_shape=[SDS((t, d), F32), SDS((t, d), BF16), SDS((t, d), BF16), SDS((t, d), BF16), SDS((8, d), F32)],
        compiler_params=_params(("arbitrary",)),
    )(act, h1b, p2, z1, target, w_down, w_pg, wt_pe, g1, b1, g2, b2)


def _after(after, body):
    k = len(after)
    return (lambda *refs: body(*refs[k:])), [ANY_SPEC] * k


def _ffn_bwd_act(dzb, g, up, w_down, after=()):
    t, d = dzb.shape
    f = g.shape[1]
    tm = _row_tile(t, 256)

    def body(dz_ref, g_ref, u_ref, wd_ref, dg_ref, du_ref):
        dz = dz_ref[...]
        for lo, hi in _col_halves(f):
            da = _dot_nt(dz, wd_ref[lo:hi, :])
            gv = g_ref[:, lo:hi].astype(F32)
            sg = _sigmoid(gv)
            dg_ref[:, lo:hi] = (da * u_ref[:, lo:hi].astype(F32) * (sg * (1.0 + gv * (1.0 - sg)))).astype(BF16)
            du_ref[:, lo:hi] = (da * gv * sg).astype(BF16)

    body, lead = _after(after, body)
    return pl.pallas_call(
        body, name="ffn_bwd_act", grid=(t // tm,),
        in_specs=lead + [_row_spec(tm, d), _row_spec(tm, f), _row_spec(tm, f), _full_spec(w_down.shape)],
        out_specs=[_row_spec(tm, f)] * 2,
        out_shape=[SDS((t, f), BF16)] * 2,
        compiler_params=_params(("parallel",)),
    )(*after, dzb, g, up, w_down)


def _dh1_ln1_bwd(dz2, dg, dup, dsb, z1, wt_gate, wt_up, w_pg, g1, after=()):
    t, d = dz2.shape
    f = dg.shape[1]
    tm = _row_tile(t, 256)

    def body(dz_ref, dg_ref, du_ref, ds_ref, z1_ref, wg_ref, wu_ref, wpg_ref, g1_ref, dz1_ref, dz1b_ref, acc_ref):
        @pl.when(pl.program_id(0) == 0)
        def _():
            acc_ref[...] = jnp.zeros_like(acc_ref)

        dh = (ALPHA * dz_ref[...] + _dot_nn(dg_ref[...], wg_ref[...]) + _dot_nn(du_ref[...], wu_ref[...])
              + _dot_nt(ds_ref[...], wpg_ref[...]))
        xhat, rstd = _layer_norm_stats(z1_ref[...])
        dz1 = _layer_norm_bwd(dh * g1_ref[...], xhat, rstd)
        dz1_ref[...] = dz1
        dz1b_ref[...] = dz1.astype(BF16)
        acc_ref[0:1, :] += jnp.sum(dh * xhat, axis=0, keepdims=True)
        acc_ref[1:2, :] += jnp.sum(dh, axis=0, keepdims=True)

    body, lead = _after(after, body)
    return pl.pallas_call(
        body, name="dh1_ln1_bwd", grid=(t // tm,),
        in_specs=lead + [_row_spec(tm, d), _row_spec(tm, f), _row_spec(tm, f), _row_spec(tm, d), _row_spec(tm, d),
                         _full_spec(wt_gate.shape), _full_spec(wt_up.shape), _full_spec(w_pg.shape),
                         _full_spec(g1.shape)],
        out_specs=[_row_spec(tm, d), _row_spec(tm, d), _full_spec((8, d))],
        out_shape=[SDS((t, d), F32), SDS((t, d), BF16), SDS((8, d), F32)],
        compiler_params=_params(("arbitrary",)),
    )(*after, dz2, dg, dup, dsb, z1, wt_gate, wt_up, w_pg, g1)


def _out_proj_bwd(dz1b, w_out, after=()):
    t, d = dz1b.shape
    tm = _row_tile(t)

    def body(dz_ref, wo_ref, dr_ref, da_ref):
        dz = dz_ref[...]
        dr_ref[...] = _dot_nt(dz, wo_ref[0:RET_W, :]).astype(BF16)
        da_ref[...] = _dot_nt(dz, wo_ref[RET_W:RET_W + ATT_W, :]).astype(BF16)

    body, lead = _after(after, body)
    return pl.pallas_call(
        body, name="out_proj_bwd", grid=(t // tm,),
        in_specs=lead + [_row_spec(tm, d), _full_spec(w_out.shape)],
        out_specs=[_row_spec(tm, RET_W), _row_spec(tm, ATT_W)],
        out_shape=[SDS((t, RET_W), BF16), SDS((t, ATT_W), BF16)],
        compiler_params=_params(("parallel",)),
    )(*after, dz1b, w_out)


def _in_proj_bwd(dz1, parts, wt_in, after=()):
    t, d = dz1.shape
    tm = _row_tile(t)
    widths = [p.shape[1] for p in parts]

    def body(*refs):
        dz_ref, part_refs, w_ref, dx_ref = refs[0], refs[1:1 + len(parts)], refs[-2], refs[-1]
        acc = ALPHA * dz_ref[...]
        lo = 0
        for p_ref, w in zip(part_refs, widths):
            acc = acc + _dot_nn(p_ref[...], w_ref[lo:lo + w, :])
            lo += w
        dx_ref[...] = acc

    body, lead = _after(after, body)
    return pl.pallas_call(
        body, name="in_proj_bwd", grid=(t // tm,),
        in_specs=lead + [_row_spec(tm, d)] + [_row_spec(tm, w) for w in widths] + [_full_spec(wt_in.shape)],
        out_specs=_row_spec(tm, d), out_shape=SDS((t, d), F32),
        compiler_params=_params(("parallel",)),
    )(*after, dz1, *parts, wt_in)


def _weight_grad(name, me, parts, rhs):
    t, n = rhs.shape
    widths = [p.shape[1] for p in parts]
    rows = sum(widths)
    own_rows = rows // N_DEV
    tk = _row_tile(t)
    step = 256

    def body(*refs):
        me_ref, part_refs, rhs_ref = refs[0], refs[1:1 + len(parts)], refs[1 + len(parts)]
        full_ref, own_ref, acc = refs[-3], refs[-2], refs[-1]
        i = pl.program_id(0)

        @pl.when(i == 0)
        def _():
            acc[...] = jnp.zeros_like(acc)

        b = rhs_ref[...].astype(BF16)
        lo = 0
        for p_ref, w in zip(part_refs, widths):
            for c0 in range(0, w, step):
                c1 = min(c0 + step, w)
                acc[lo + c0:lo + c1, :] += _dot_tn(p_ref[:, c0:c1].astype(BF16), b)
            lo += w

        @pl.when(i == pl.num_programs(0) - 1)
        def _():
            full_ref[...] = acc[...].astype(BF16)
            own_ref[...] = acc[pl.ds(pl.multiple_of(me_ref[0] * own_rows, 8), own_rows), :]

    return pl.pallas_call(
        body, name=name, grid=(t // tk,),
        in_specs=[_smem_spec()] + [_row_spec(tk, w) for w in widths] + [_row_spec(tk, n)],
        out_specs=[_full_spec((rows, n)), _full_spec((own_rows, n))],
        out_shape=[SDS((rows, n), BF16), SDS((own_rows, n), F32)],
        scratch_shapes=[pltpu.VMEM((rows, n), F32)],
        compiler_params=_params(("arbitrary",)),
    )(me, *parts, rhs)


def _log_decay(decay_f, decay_b):
    def body(f_ref, b_ref, lf_ref, lb_ref):
        lf_ref[...] = jnp.log1p(-jnp.exp2(f_ref[...]))
        lb_ref[...] = jnp.log1p(-jnp.exp2(b_ref[...]))

    return pl.pallas_call(body, name="log_decay", out_shape=[SDS(decay_f.shape, F32)] * 2)(decay_f, decay_b)


def _chunk(ref, n):
    return ref[pl.ds(pl.multiple_of(n * CHUNK, CHUNK), CHUNK), :]


def _group_sum(is_a, v):
    sa = jnp.sum(jnp.where(is_a, v, 0.0), axis=1, keepdims=True)
    sb = jnp.sum(jnp.where(is_a, 0.0, v), axis=1, keepdims=True)
    return jnp.where(is_a, sa, sb)


def _seq_spec(s, col_block):
    return pl.BlockSpec((s, LANES), lambda b, h: (b, col_block + h))


def _smem_spec():
    return pl.BlockSpec(memory_space=pltpu.SMEM)


RET_UNROLL = 2


def _stacked_tables(lgf_ref, lgb_ref, pair):
    lane = lax.broadcasted_iota(jnp.int32, (1, LANES), 1)
    is_a = lane < HEAD_DIM
    lgf = jnp.where(is_a, lgf_ref[2 * pair], lgf_ref[2 * pair + 1])
    lgb = jnp.where(is_a, lgb_ref[2 * pair], lgb_ref[2 * pair + 1])
    row = lax.broadcasted_iota(jnp.int32, (CHUNK, 1), 0).astype(F32)
    kdec_f, qdec_f = jnp.exp(lgf * (CHUNK - 1.0 - row)), jnp.exp(lgf * (row + 1.0))
    kdec_b, qdec_b = jnp.exp(lgb * row), jnp.exp(lgb * (CHUNK - row))
    tab = dict(
        is_a=is_a, row=row, lam_f=jnp.exp(lgf * CHUNK), lam_b=jnp.exp(lgb * CHUNK),
        kdec=jnp.concatenate([kdec_f, kdec_b], axis=1), qdec=jnp.concatenate([qdec_f, qdec_b], axis=1),
        qexp=jnp.concatenate([jnp.broadcast_to(row + 1.0, (CHUNK, LANES)),
                              jnp.broadcast_to(CHUNK - row, (CHUNK, LANES))], axis=1),
        kexp=jnp.concatenate([jnp.broadcast_to(CHUNK - 1.0 - row, (CHUNK, LANES)),
                              jnp.broadcast_to(row, (CHUNK, LANES))], axis=1),
    )
    r = lax.broadcasted_iota(jnp.int32, (2 * LANES, LANES), 0)
    c = lax.broadcasted_iota(jnp.int32, (2 * LANES, LANES), 1)
    tab["diag2"] = ((r & (LANES - 1)) < HEAD_DIM) == (c < HEAD_DIM)
    i2 = lax.broadcasted_iota(jnp.int32, (2 * CHUNK, CHUNK), 0)
    j = lax.broadcasted_iota(jnp.int32, (2 * CHUNK, CHUNK), 1)
    head_b = i2 >= CHUNK
    diff = ((i2 & (CHUNK - 1)) - j).astype(F32)
    up, dn = jnp.maximum(diff, 0.0), jnp.maximum(-diff, 0.0)
    lgf2 = jnp.where(head_b, lgf_ref[2 * pair + 1], lgf_ref[2 * pair])
    lgb2 = jnp.where(head_b, lgb_ref[2 * pair + 1], lgb_ref[2 * pair])
    ef = jnp.where(diff >= 0, jnp.exp(lgf2 * up), 0.0)
    eb = jnp.where(diff <= 0, jnp.exp(lgb2 * dn), 0.0)
    tab["d2"] = ef + eb
    tab["df2"] = ef * up
    tab["db2"] = eb * dn
    return tab


def _stack_pair(is_a, x):
    zero = jnp.zeros_like(x)
    return jnp.concatenate([jnp.where(is_a, x, zero), jnp.where(is_a, zero, x)], axis=0)


def _unstack_pair(is_a, x2):
    return jnp.where(is_a, x2[0:CHUNK, :], x2[CHUNK:2 * CHUNK, :])


def _both_ways(x, dec):
    return (jnp.concatenate([x, x], axis=1) * dec).astype(BF16)


def _scan_states(n_chunks, st, up_rows, up_lam, down_rows, down_lam):
    zero = jnp.zeros((LANES, LANES), F32)

    def up(n, r):
        new = st[n, up_rows, :]
        st[n, up_rows, :] = r
        return r * up_lam + new

    def down(s, r):
        n = n_chunks - 1 - s
        new = st[n, down_rows, :]
        st[n, down_rows, :] = r
        return r * down_lam + new

    lax.fori_loop(0, n_chunks, up, zero)
    lax.fori_loop(0, n_chunks, down, zero)


FWD_ROWS, BWD_ROWS = pl.ds(0, LANES), pl.ds(LANES, LANES)


def _retention_fwd(u, lgf, lgb, gn_gain, b_loc):
    t = u.shape[0]
    s = t // b_loc
    n_chunks = s // CHUNK
    pairs = RET_HEADS // 2

    def body(lgf_ref, lgb_ref, q_ref, k_ref, v_ref, g_ref, gain_ref, r_ref, y_ref, st):
        tab = _stacked_tables(lgf_ref, lgb_ref, pl.program_id(1))
        is_a = tab["is_a"]

        def kv_body(n, _):
            k8 = _chunk(k_ref, n).astype(F32) * Q_SCALE
            st[n] = jnp.where(tab["diag2"], _dot_tn(_both_ways(k8, tab["kdec"]), _chunk(v_ref, n)), 0.0)
            return 0

        lax.fori_loop(0, n_chunks, kv_body, 0, unroll=RET_UNROLL)
        _scan_states(n_chunks, st, FWD_ROWS, tab["lam_f"], BWD_ROWS, tab["lam_b"])

        def out_body(n, _):
            q = _chunk(q_ref, n)
            k8 = (_chunk(k_ref, n).astype(F32) * Q_SCALE).astype(BF16)
            v = _chunk(v_ref, n)
            p2 = (_dot_nt(_stack_pair(is_a, q), k8) * tab["d2"]).astype(BF16)
            y = _unstack_pair(is_a, _dot_nn(p2, v))
            y = y + _dot_nn(_both_ways(q.astype(F32), tab["qdec"]), st[n].astype(BF16))
            rows = pl.ds(pl.multiple_of(n * CHUNK, CHUNK), CHUNK)
            y_ref[rows, :] = y
            mu = _group_sum(is_a, y) * (1.0 / HEAD_DIM)
            dlt = y - mu
            var = _group_sum(is_a, dlt * dlt) * (1.0 / HEAD_DIM)
            xhat = dlt * lax.rsqrt(var + GN_EPS)
            gate = _chunk(g_ref, n).astype(F32)
            r_ref[rows, :] = (xhat * gain_ref[...] * gate * _sigmoid(gate)).astype(BF16)
            return 0

        lax.fori_loop(0, n_chunks, out_body, 0, unroll=RET_UNROLL)

    lane_blk = lambda c0: _seq_spec(s, c0 // LANES)
    return pl.pallas_call(
        body, name="retention_fwd", grid=(b_loc, pairs),
        in_specs=[_smem_spec(), _smem_spec(), lane_blk(C_RQ), lane_blk(C_RK), lane_blk(C_RV), lane_blk(C_RG),
                  pl.BlockSpec((1, LANES), lambda b, h: (0, h))],
        out_specs=[_seq_spec(s, 0), _seq_spec(s, 0)],
        out_shape=[SDS((t, RET_W), BF16), SDS((t, RET_W), F32)],
        scratch_shapes=[pltpu.VMEM((n_chunks, 2 * LANES, LANES), F32)],
        compiler_params=_params(("parallel", "parallel")),
    )(lgf, lgb, u, u, u, u, gn_gain)


ST_GAIN, ST_XF, ST_XB, ST_IFA, ST_IFB, ST_IBA, ST_IBB, ST_LF, ST_LB = 0, 1, 2, 3, 4, 5, 6, 8, 9
ST_ROWS = 16


def _retention_bwd(u, y_pre, dr, lgf, lgb, gn_gain, b_loc):
    t = u.shape[0]
    s = t // b_loc
    n_chunks = s // CHUNK
    pairs = RET_HEADS // 2

    def body(lgf_ref, lgb_ref, q_ref, k_ref, v_ref, g_ref, y_ref, dr_ref, gain_ref,
             dq_ref, dk_ref, dv_ref, dg_ref, st_ref, st, gr, dy_s):
        tab = _stacked_tables(lgf_ref, lgb_ref, pl.program_id(1))
        is_a, row = tab["is_a"], tab["row"]
        gain = gain_ref[...]

        def norm_body(n, dgain):
            rows = pl.ds(pl.multiple_of(n * CHUNK, CHUNK), CHUNK)
            y = y_ref[rows, :]
            mu = _group_sum(is_a, y) * (1.0 / HEAD_DIM)
            dlt = y - mu
            var = _group_sum(is_a, dlt * dlt) * (1.0 / HEAD_DIM)
            rstd = lax.rsqrt(var + GN_EPS)
            xhat = dlt * rstd
            gate = g_ref[rows, :].astype(F32)
            sg = _sigmoid(gate)
            silu = gate * sg
            d_out = dr_ref[rows, :].astype(F32)
            dg_ref[rows, :] = (d_out * xhat * gain * (sg * (1.0 + gate * (1.0 - sg)))).astype(BF16)
            dxh = d_out * gain * silu
            m1 = _group_sum(is_a, dxh) * (1.0 / HEAD_DIM)
            m2 = _group_sum(is_a, dxh * xhat) * (1.0 / HEAD_DIM)
            dy = (rstd * (dxh - m1 - xhat * m2)).astype(BF16)
            dy_s[rows, :] = dy
            k8 = k_ref[rows, :].astype(F32) * Q_SCALE
            st[n] = jnp.where(tab["diag2"], _dot_tn(_both_ways(k8, tab["kdec"]), v_ref[rows, :]), 0.0)
            qf = q_ref[rows, :].astype(F32)
            gr[n] = jnp.where(tab["diag2"], _dot_tn(_both_ways(qf, tab["qdec"]), dy), 0.0)
            return dgain + jnp.sum(d_out * xhat * silu, axis=0, keepdims=True)

        dgain = lax.fori_loop(0, n_chunks, norm_body, jnp.zeros((1, LANES), F32), unroll=RET_UNROLL)
        _scan_states(n_chunks, st, FWD_ROWS, tab["lam_f"], BWD_ROWS, tab["lam_b"])
        _scan_states(n_chunks, gr, BWD_ROWS, tab["lam_b"], FWD_ROWS, tab["lam_f"])
        colsum = lambda x: jnp.sum(x, axis=0, keepdims=True)

        def grad_body(n, carry):
            xfb, ifa, ifb, iba, ibb, lf, lb = carry
            rows = pl.ds(pl.multiple_of(n * CHUNK, CHUNK), CHUNK)
            q = q_ref[rows, :]
            qf = q.astype(F32)
            k8f = k_ref[rows, :].astype(F32) * Q_SCALE
            k8 = k8f.astype(BF16)
            v = v_ref[rows, :]
            dy = dy_s[rows, :]
            q2, dy2 = _stack_pair(is_a, q), _stack_pair(is_a, dy)
            sc = _dot_nt(q2, k8)
            dp = _dot_nt(dy2, v)
            a2 = (sc * tab["d2"]).astype(BF16)
            ds2 = (dp * tab["d2"]).astype(BF16)
            dq = _unstack_pair(is_a, _dot_nn(ds2, k8))
            dk = _dot_tn(ds2, q2)
            dv = _dot_tn(a2, dy2)
            prod = sc * dp
            pf, pb = prod * tab["df2"], prod * tab["db2"]
            ifa, ifb = ifa + colsum(pf[0:CHUNK, :]), ifb + colsum(pf[CHUNK:2 * CHUNK, :])
            iba, ibb = iba + colsum(pb[0:CHUNK, :]), ibb + colsum(pb[CHUNK:2 * CHUNK, :])
            states, sgrads = st[n], gr[n]
            sb, gb = states.astype(BF16), sgrads.astype(BF16)
            dqc = _dot_nt(dy, sb) * tab["qdec"]
            dkc = _dot_nt(v, gb) * tab["kdec"]
            dv = dv + _dot_nn(_both_ways(k8f, tab["kdec"]), gb)
            dq_ref[rows, :] = (dq + dqc[:, 0:LANES] + dqc[:, LANES:2 * LANES]).astype(BF16)
            dk_ref[rows, :] = ((dk + dkc[:, 0:LANES] + dkc[:, LANES:2 * LANES]) * Q_SCALE).astype(BF16)
            dv_ref[rows, :] = dv.astype(BF16)
            q2w, k2w = jnp.concatenate([qf, qf], axis=1), jnp.concatenate([k8f, k8f], axis=1)
            xfb = xfb + colsum(tab["qexp"] * q2w * dqc + tab["kexp"] * k2w * dkc)
            prod_s = sgrads * states
            lf, lb = lf + colsum(prod_s[0:LANES, :]), lb + colsum(prod_s[LANES:2 * LANES, :])
            return xfb, ifa, ifb, iba, ibb, lf, lb

        z = jnp.zeros((1, LANES), F32)
        init = (jnp.zeros((1, 2 * LANES), F32), z, z, z, z, z, z)
        xfb, ifa, ifb, iba, ibb, lf, lb = lax.fori_loop(0, n_chunks, grad_body, init, unroll=RET_UNROLL)
        st_ref[...] = jnp.zeros_like(st_ref)
        st_ref[ST_GAIN:ST_GAIN + 1, :] = dgain
        st_ref[ST_XF:ST_XF + 1, :] = xfb[:, 0:LANES]
        st_ref[ST_XB:ST_XB + 1, :] = xfb[:, LANES:2 * LANES]
        st_ref[ST_IFA:ST_IFA + 1, :] = ifa
        st_ref[ST_IFB:ST_IFB + 1, :] = ifb
        st_ref[ST_IBA:ST_IBA + 1, :] = iba
        st_ref[ST_IBB:ST_IBB + 1, :] = ibb
        st_ref[ST_LF:ST_LF + 1, :] = lf * (CHUNK * tab["lam_f"])
        st_ref[ST_LB:ST_LB + 1, :] = lb * (CHUNK * tab["lam_b"])

    lane_blk = lambda c0: _seq_spec(s, c0 // LANES)
    seq0 = _seq_spec(s, 0)
    state = pltpu.VMEM((n_chunks, 2 * LANES, LANES), F32)
    return pl.pallas_call(
        body, name="retention_bwd", grid=(b_loc, pairs),
        in_specs=[_smem_spec(), _smem_spec(), lane_blk(C_RQ), lane_blk(C_RK), lane_blk(C_RV), lane_blk(C_RG),
                  seq0, seq0, pl.BlockSpec((1, LANES), lambda b, h: (0, h))],
        out_specs=[seq0] * 4 + [pl.BlockSpec((ST_ROWS, LANES), lambda b, h: (b, h))],
        out_shape=[SDS((t, RET_W), BF16)] * 4 + [SDS((b_loc * ST_ROWS, RET_W), F32)],
        scratch_shapes=[state, state, pltpu.VMEM((s, LANES), BF16)],
        compiler_params=_params(("parallel", "parallel")),
    )(lgf, lgb, u, u, u, u, y_pre, dr, gn_gain)


GW = GROUP * HEAD_DIM
KEYS = 3 * BLOCK


def _attn_tables(g, bias_ref):
    r = lax.broadcasted_iota(jnp.int32, (GROUP * BLOCK, KEYS), 0)
    kj = lax.broadcasted_iota(jnp.int32, (GROUP * BLOCK, KEYS), 1)
    qi = r & (BLOCK - 1)
    hh = lax.shift_right_logical(r, 7)
    dist = jnp.abs(kj - BLOCK - qi)
    slope = jnp.exp2(-(GROUP * g + hh + 1).astype(F32) * (8.0 / ATTN_HEADS))
    bias_ref[...] = jnp.where(dist <= BLOCK, -slope * dist.astype(F32), NEG_INF)


def _tile_keys(x_ref, g, scale, pad_ref, s):
    r = lax.broadcasted_iota(jnp.int32, (LANES, GW), 0)
    c = lax.broadcasted_iota(jnp.int32, (LANES, GW), 1)
    place = jnp.where(r == g * HEAD_DIM + (c & (HEAD_DIM - 1)), 1.0, 0.0).astype(BF16)
    pad_ref[0:BLOCK, :] = jnp.zeros((BLOCK, GW), BF16)
    pad_ref[BLOCK + s:2 * BLOCK + s, :] = jnp.zeros((BLOCK, GW), BF16)
    pad_ref[BLOCK:BLOCK + s, :] = (_dot_nn(x_ref[...], place) * scale).astype(BF16)


def _stack_heads(x):
    lane_h = lax.shift_right_logical(lax.broadcasted_iota(jnp.int32, (1, GW), 1), 6)
    zero = jnp.zeros_like(x)
    return jnp.concatenate([jnp.where(lane_h == h, x, zero) for h in range(GROUP)], axis=0)


def _unstack_heads(x4):
    lane_h = lax.shift_right_logical(lax.broadcasted_iota(jnp.int32, (1, GW), 1), 6)
    out = jnp.zeros((BLOCK, GW), F32)
    for h in range(GROUP):
        out = out + jnp.where(lane_h == h, x4[h * BLOCK:(h + 1) * BLOCK, :], 0.0)
    return out


def _sink_column(sink_ref, g):
    rh = lax.shift_right_logical(lax.broadcasted_iota(jnp.int32, (GROUP * BLOCK, 1), 0), 7)
    col = jnp.zeros((GROUP * BLOCK, 1), F32)
    for h in range(GROUP):
        col = jnp.where(rh == h, sink_ref[GROUP * g + h], col)
    return col


def _attn_probs(qm, k3, bias_ref, sink_col, n, s):
    logits = _dot_nt(qm, k3) + bias_ref[...]
    kpos = n * BLOCK - BLOCK + lax.broadcasted_iota(jnp.int32, (1, KEYS), 1)
    logits = jnp.where((kpos >= 0) & (kpos < s), logits, NEG_INF)
    m = jnp.maximum(jnp.max(logits, axis=1, keepdims=True), sink_col)
    e = jnp.exp(logits - m)
    e_sink = jnp.exp(sink_col - m)
    inv = 1.0 / (jnp.sum(e, axis=1, keepdims=True) + e_sink)
    return e * inv, e_sink * inv


def _attn_specs(s, n_blocks):
    q_spec = pl.BlockSpec((BLOCK, GW), lambda b, g, n: (b * n_blocks + n, C_AQ // GW + g))
    k_spec = pl.BlockSpec((s, LANES), lambda b, g, n: (b, C_AK // LANES))
    v_spec = pl.BlockSpec((s, LANES), lambda b, g, n: (b, C_AV // LANES))
    o_spec = pl.BlockSpec((BLOCK, GW), lambda b, g, n: (b * n_blocks + n, g))
    return q_spec, k_spec, v_spec, o_spec


def _attention_fwd(u, sink, b_loc):
    t = u.shape[0]
    s = t // b_loc
    n_blocks = s // BLOCK

    def body(sink_ref, q_ref, k_ref, v_ref, o_ref, kpad, vpad, bias):
        g, n = pl.program_id(1), pl.program_id(2)

        @pl.when(n == 0)
        def _():
            _attn_tables(g, bias)
            _tile_keys(k_ref, g, Q_SCALE, kpad, s)
            _tile_keys(v_ref, g, 1.0, vpad, s)

        keys = pl.ds(pl.multiple_of(n * BLOCK, BLOCK), KEYS)
        p, _ = _attn_probs(_stack_heads(q_ref[...]), kpad[keys, :], bias, _sink_column(sink_ref, g), n, s)
        o_ref[...] = _unstack_heads(_dot_nn(p.astype(BF16), vpad[keys, :])).astype(BF16)

    q_spec, k_spec, v_spec, o_spec = _attn_specs(s, n_blocks)
    pad = pltpu.VMEM((s + 2 * BLOCK, GW), BF16)
    return pl.pallas_call(
        body, name="attention_fwd", grid=(b_loc, KV_HEADS, n_blocks),
        in_specs=[_smem_spec(), q_spec, k_spec, v_spec], out_specs=o_spec,
        out_shape=SDS((t, ATT_W), BF16),
        scratch_shapes=[pad, pad, pltpu.VMEM((GROUP * BLOCK, KEYS), F32)],
        compiler_params=_params(("parallel", "arbitrary", "arbitrary")),
    )(sink, u, u, u)


def _fold_groups(x, g):
    x = x + pltpu.roll(x, 2 * HEAD_DIM, 1)
    x = x + pltpu.roll(x, HEAD_DIM, 1)
    lane_g = lax.shift_right_logical(lax.broadcasted_iota(jnp.int32, (1, LANES), 1), 6)
    return jnp.where(lane_g == g, x[:, 0:LANES], 0.0)


def _attention_bwd(u, da, sink, b_loc):
    t = u.shape[0]
    s = t // b_loc
    n_blocks = s // BLOCK

    def body(sink_ref, q_ref, k_ref, v_ref, do_ref, dq_ref, dk_ref, dv_ref, dsink_ref, kpad, vpad, bias, dk_acc, dv_acc):
        g, n = pl.program_id(1), pl.program_id(2)

        @pl.when(n == 0)
        def _():
            _attn_tables(g, bias)
            _tile_keys(k_ref, g, Q_SCALE, kpad, s)
            _tile_keys(v_ref, g, 1.0, vpad, s)
            dsink_ref[...] = jnp.zeros_like(dsink_ref)

        @pl.when((n == 0) & (g == 0))
        def _():
            dk_acc[...] = jnp.zeros_like(dk_acc)
            dv_acc[...] = jnp.zeros_like(dv_acc)

        keys = pl.ds(pl.multiple_of(n * BLOCK, BLOCK), KEYS)
        qm = _stack_heads(q_ref[...])
        k3, v3 = kpad[keys, :], vpad[keys, :]
        p, p_sink = _attn_probs(qm, k3, bias, _sink_column(sink_ref, g), n, s)
        dom = _stack_heads(do_ref[...])
        dp = _dot_nt(dom, v3)
        delta = jnp.sum(p * dp, axis=1, keepdims=True)
        ds_mat = (p * (dp - delta)).astype(BF16)
        dq_ref[...] = _unstack_heads(_dot_nn(ds_mat, k3)).astype(BF16)
        dk_acc[keys, :] += _fold_groups(_dot_tn(ds_mat, qm), g) * Q_SCALE
        dv_acc[keys, :] += _fold_groups(_dot_tn(p.astype(BF16), dom), g)
        w = p_sink * delta
        rows = lax.broadcasted_iota(jnp.int32, dsink_ref.shape, 0)
        upd = jnp.zeros(dsink_ref.shape, F32)
        for h in range(GROUP):
            upd = jnp.where(rows == h, -jnp.sum(w[h * BLOCK:(h + 1) * BLOCK, :]), upd)
        dsink_ref[...] += upd

        @pl.when((n == n_blocks - 1) & (g == KV_HEADS - 1))
        def _():
            dk_ref[...] = dk_acc[BLOCK:BLOCK + s, :].astype(BF16)
            dv_ref[...] = dv_acc[BLOCK:BLOCK + s, :].astype(BF16)

    q_spec, k_spec, v_spec, o_spec = _attn_specs(s, n_blocks)
    kv_out = pl.BlockSpec((s, LANES), lambda b, g, n: (b, 0))
    pad = pltpu.VMEM((s + 2 * BLOCK, GW), BF16)
    acc = pltpu.VMEM((s + 2 * BLOCK, LANES), F32)
    return pl.pallas_call(
        body, name="attention_bwd", grid=(b_loc, KV_HEADS, n_blocks),
        in_specs=[_smem_spec(), q_spec, k_spec, v_spec, o_spec],
        out_specs=[o_spec, kv_out, kv_out, pl.BlockSpec((8, LANES), lambda b, g, n: (b * KV_HEADS + g, 0))],
        out_shape=[SDS((t, ATT_W), BF16), SDS((t, KV_W), BF16), SDS((t, KV_W), BF16),
                   SDS((b_loc * KV_HEADS * 8, LANES), F32)],
        scratch_shapes=[pad, pad, pltpu.VMEM((GROUP * BLOCK, KEYS), F32), acc, acc],
        compiler_params=_params(("arbitrary", "arbitrary", "arbitrary")),
    )(sink, u, u, u, da)


def _pack_small(acc2, acc1, ret_stats, dsink, b_loc, d):
    pairs = RET_HEADS // 2

    def body(acc2_ref, acc1_ref, st_ref, dsink_ref, out_ref):
        out_ref[...] = jnp.zeros_like(out_ref)
        out_ref[ROW_LN1G:ROW_LN1G + 1, :] = acc1_ref[0:1, :]
        out_ref[ROW_LN1B:ROW_LN1B + 1, :] = acc1_ref[1:2, :]
        out_ref[ROW_LN2G:ROW_LN2G + 1, :] = acc2_ref[1:2, :]
        out_ref[ROW_LN2B:ROW_LN2B + 1, :] = acc2_ref[2:3, :]
        out_ref[ROW_LOSS:ROW_LOSS + 1, :] = acc2_ref[0:1, :]
        st = st_ref[0:ST_ROWS, :]
        for b in range(1, b_loc):
            st = st + st_ref[b * ST_ROWS:(b + 1) * ST_ROWS, :]
        out_ref[ROW_GN:ROW_GN + 1, 0:RET_W] = st[ST_GAIN:ST_GAIN + 1, :]
        lane = lax.broadcasted_iota(jnp.int32, (1, d), 1)
        misc = jnp.zeros((1, d), F32)
        for pr in range(pairs):
            blk = st[:, pr * LANES:(pr + 1) * LANES]
            half = lax.broadcasted_iota(jnp.int32, (1, LANES), 1) < HEAD_DIM
            for h in range(2):
                sel = half if h == 0 else jnp.logical_not(half)
                cross_f = jnp.sum(jnp.where(sel, blk[ST_XF:ST_XF + 1, :] + blk[ST_LF:ST_LF + 1, :], 0.0))
                cross_b = jnp.sum(jnp.where(sel, blk[ST_XB:ST_XB + 1, :] + blk[ST_LB:ST_LB + 1, :], 0.0))
                intra_f = jnp.sum(blk[ST_IFA + h:ST_IFA + h + 1, :])
                intra_b = jnp.sum(blk[ST_IBA + h:ST_IBA + h + 1, :])
                head = 2 * pr + h
                misc = jnp.where(lane == MISC_DF + head, cross_f + intra_f, misc)
                misc = jnp.where(lane == MISC_DB + head, cross_b + intra_b, misc)
        for g in range(KV_HEADS):
            tot = dsink_ref[g * 8:(g + 1) * 8, :]
            for b in range(1, b_loc):
                tot = tot + dsink_ref[(b * KV_HEADS + g) * 8:(b * KV_HEADS + g + 1) * 8, :]
            for h in range(GROUP):
                misc = jnp.where(lane == MISC_SINK + GROUP * g + h, jnp.sum(tot[h:h + 1, 0:1]), misc)
        out_ref[ROW_MISC:ROW_MISC + 1, :] = misc

    return pl.pallas_call(body, name="pack_small", out_shape=SDS((SMALL_ROWS, d), F32))(acc2, acc1, ret_stats, dsink)


BIG = ("w_in", "w_out", "w_ffn_gate", "w_ffn_up", "w_ffn_down", "w_ple_proj", "w_ple_gate")
TRANSPOSED_OUTSIDE = ("w_in", "w_ffn_gate", "w_ffn_up")
TRANSPOSED_HERE = ("w_ple_proj",)
SMALL = ("ret_decay_fwd", "ret_decay_bwd", "ret_gn_gain", "attn_sink", "ln1_gain", "ln1_bias", "ln2_gain", "ln2_bias")
ORDER = ("w_in", "ret_decay_fwd", "ret_decay_bwd", "ret_gn_gain", "attn_sink", "w_out", "ln1_gain", "ln1_bias",
         "w_ffn_gate", "w_ffn_up", "w_ffn_down", "w_ple_proj", "w_ple_gate", "ln2_gain", "ln2_bias")


GATHER_ORDER = ("w_in", "w_out", "w_ffn_gate", "w_ffn_up", "w_ple_gate", "w_ple_proj", "w_ffn_down")


def _local_step(x2, p2, target2, fetch, publish, small, b_loc, me):
    d = x2.shape[1]
    lgf, lgb = _log_decay(small["ret_decay_fwd"], small["ret_decay_bwd"])
    lgf1, lgb1, sink1 = lgf.reshape(-1), lgb.reshape(-1), small["attn_sink"].reshape(-1)
    (w_in,) = fetch(("w_in",), ())
    u, xb = _in_proj(x2, w_in)
    r, y_pre = _retention_fwd(u, lgf1, lgb1, small["ret_gn_gain"], b_loc)
    a = _attention_fwd(u, sink1, b_loc)
    w_out, w_gate, w_up = fetch(("w_out", "w_ffn_gate", "w_ffn_up"), (r, a))
    z1, h1b = _out_proj_ln1(r, a, x2, w_out, small["ln1_gain"], small["ln1_bias"])
    g, up, act = _ffn_up(h1b, w_gate, w_up)
    w_pg, w_pe, w_down = fetch(("w_ple_gate", "w_ple_proj", "w_ffn_down"), (act,))
    dz2, dz2b, dsb, dpleb, acc2 = _ffn_down_ln2_loss(
        act, h1b, p2, z1, target2, w_down, w_pg, w_pe,
        small["ln1_gain"], small["ln1_bias"], small["ln2_gain"], small["ln2_bias"])
    own = {}

    def grad(name, parts, rhs):
        whole, own[name] = _weight_grad("grad_" + name, me, parts, rhs)
        return whole

    t1 = publish("ffn_down", dict(w_ffn_down=grad("w_ffn_down", [act], dz2b),
                                  w_ple_proj=grad("w_ple_proj", [dpleb], p2),
                                  w_ple_gate=grad("w_ple_gate", [h1b], dsb)))
    dg, dup = _ffn_bwd_act(dz2b, g, up, w_down, t1)
    t2 = publish("ffn_up", dict(w_ffn_gate=grad("w_ffn_gate", [dg], h1b), w_ffn_up=grad("w_ffn_up", [dup], h1b)))
    dz1, dz1b, acc1 = _dh1_ln1_bwd(dz2, dg, dup, dsb, z1, w_gate, w_up, w_pg, small["ln1_gain"], t2)
    t3 = publish("out", dict(w_out=grad("w_out", [r, a], dz1b)))
    dr, da = _out_proj_bwd(dz1b, w_out, t3)
    dq, dk, dv, dgate, ret_stats = _retention_bwd(u, y_pre, dr, lgf1, lgb1, small["ret_gn_gain"], b_loc)
    daq, dak, dav, dsink = _attention_bwd(u, da, sink1, b_loc)
    parts = [dq, dk, dv, dgate, daq, dak, dav]
    t4 = publish("in", dict(w_in=grad("w_in", parts, xb)))
    grad_x = _in_proj_bwd(dz1, parts, w_in, t4)
    small_part = _pack_small(acc2, acc1, ret_stats, dsink, b_loc, d)
    return grad_x, own, small_part


def kernel(x, p, w_in, ret_decay_fwd, ret_decay_bwd, ret_gn_gain, attn_sink, w_out, ln1_gain, ln1_bias, w_ffn_gate, w_ffn_up, w_ffn_down, w_ple_proj, w_ple_gate, ln2_gain, ln2_bias, loss_target, m_w_in, m_ret_decay_fwd, m_ret_decay_bwd, m_ret_gn_gain, m_attn_sink, m_w_out, m_ln1_gain, m_ln1_bias, m_w_ffn_gate, m_w_ffn_up, m_w_ffn_down, m_w_ple_proj, m_w_ple_gate, m_ln2_gain, m_ln2_bias, v_w_in, v_ret_decay_fwd, v_ret_decay_bwd, v_ret_gn_gain, v_attn_sink, v_w_out, v_ln1_gain, v_ln1_bias, v_w_ffn_gate, v_w_ffn_up, v_w_ffn_down, v_w_ple_proj, v_w_ple_gate, v_ln2_gain, v_ln2_bias):
    given = dict(locals())

    def strip(n, a):
        if n not in BIG:
            return a
        return a[0].T if n in TRANSPOSED_OUTSIDE else a[0]

    def restore(n, a):
        if n not in BIG:
            return a
        return (a.T if n in TRANSPOSED_OUTSIDE else a)[None]

    w = {n: strip(n, given[n]) for n in ORDER}
    m = {n: strip(n, given["m_" + n]) for n in ORDER}
    v = {n: strip(n, given["v_" + n]) for n in ORDER}
    b_loc, s, d = x.shape
    x2 = x.reshape(b_loc * s, d)
    p2 = p[0].reshape(b_loc * s, p.shape[-1])
    target2 = loss_target.reshape(b_loc * s, d)

    small = {n: w[n] for n in SMALL}
    me = (4 * lax.axis_index("x") + 2 * lax.axis_index("y") + lax.axis_index("c")).astype(jnp.int32).reshape(1)

    gathered = _prep_shards(me, {n: w[n] for n in BIG})
    gather = _split_copy_start("gather_start", [(gathered[n],) for n in GATHER_ORDER], _gather_copy)

    def fetch(names, after):
        which = [GATHER_ORDER.index(n) for n in names]
        got = _split_copy_wait("gather_wait_" + names[0], gather, which, _gather_copy, list(after))
        return [item[0] for item in got]

    scatters = []

    def publish(tag, products):
        names = list(products)
        items = [(products[n], lax.empty((N_DEV, products[n].shape[0] // N_DEV, products[n].shape[1]), BF16))
                 for n in names]
        started = _split_copy_start("scatter_start_" + tag, items, _scatter_copy)
        scatters.append((tag, names, started))
        return (started["token"],)

    grad_x, own, small_part = _local_step(x2, p2, target2, fetch, publish, small, b_loc, me)

    out_g, out_d, out_m, out_v = {}, {}, {}, {}
    after = [grad_x]
    for tag, names, started in scatters:
        landed = _split_copy_wait("scatter_wait_" + tag, started, list(range(len(names))), _scatter_copy, after)
        for n, (_, recv) in zip(names, landed):
            out_g[n], out_d[n], out_m[n], out_v[n] = _reduce_adamw(
                n, own[n], recv, w[n], m[n], v[n], n in TRANSPOSED_HERE)
        after = [out_v[names[-1]]]
        if tag == "out":
            loss, sg, sd, sm, sv = _small_all_reduce_adamw(
                small_part, small, {n: m[n] for n in SMALL}, {n: v[n] for n in SMALL})
            for dst, src in ((out_g, sg), (out_d, sd), (out_m, sm), (out_v, sv)):
                dst.update(src)
            after.append(sv["ln2_bias"])

    outs = [loss[0, 0], grad_x.reshape(x.shape)]
    for group in (out_g, out_d, out_m, out_v):
        outs += [restore(n, group[n]) for n in ORDER]
    return tuple(outs)
```

```python
import functools

import jax
import jax.numpy as jnp
from jax import lax
from jax.experimental import pallas as pl
from jax.experimental.pallas import tpu as pltpu

F32, BF16 = jnp.float32, jnp.bfloat16
SDS = jax.ShapeDtypeStruct
MESH = pl.DeviceIdType.MESH

N_DEV = 8
HEAD_DIM = 64
RET_HEADS = 8
ATTN_HEADS = 8
KV_HEADS = 2
GROUP = ATTN_HEADS // KV_HEADS
RET_W = RET_HEADS * HEAD_DIM
ATT_W = ATTN_HEADS * HEAD_DIM
KV_W = KV_HEADS * HEAD_DIM
LANES = 128
CHUNK = 128
BLOCK = 128
Q_SCALE = HEAD_DIM ** -0.5
ALPHA = 2.0 ** 0.25
LN_EPS = 1e-5
GN_EPS = 1e-5
NEG_INF = -1e30
C_RQ, C_RK, C_RV, C_RG = 0, RET_W, 2 * RET_W, 3 * RET_W
C_AQ = 4 * RET_W
C_AK = C_AQ + ATT_W
C_AV = C_AK + KV_W
IN_W = C_AV + KV_W

ADAM_LR = 0.001
ADAM_B1 = 0.9
ADAM_B2 = 0.999
ADAM_EPS = 1e-08
ADAM_WD = 0.01
ADAM_STEP = 10

VMEM_LIMIT = 56 * 1024 * 1024
MATMUL_ROWS = 1024
EPILOGUE_ROWS = 512
SUB_ROWS = 256
SMALL_ROWS = 16
ROW_LN1G, ROW_LN1B, ROW_LN2G, ROW_LN2B, ROW_LOSS, ROW_GN, ROW_MISC = 0, 1, 2, 3, 4, 5, 6
MISC_DF, MISC_DB, MISC_SINK = 0, 8, 16


def _dot_nn(a, b):
    return lax.dot_general(a, b, (((1,), (0,)), ((), ())), preferred_element_type=F32)


def _dot_nt(a, b):
    return lax.dot_general(a, b, (((1,), (1,)), ((), ())), preferred_element_type=F32)


def _dot_tn(a, b):
    return lax.dot_general(a, b, (((0,), (0,)), ((), ())), preferred_element_type=F32)


def _params(sem=None, vmem=VMEM_LIMIT):
    kw = {"vmem_limit_bytes": vmem}
    if sem is not None:
        kw["dimension_semantics"] = sem
    return pltpu.CompilerParams(**kw)


def _row_tile(t, want=512):
    tm = want
    while t % tm:
        tm //= 2
    return tm


def _sigmoid(x):
    return 1.0 / (1.0 + jnp.exp(-x))


def _layer_norm_stats(z):
    mu = jnp.mean(z, axis=1, keepdims=True)
    d = z - mu
    var = jnp.mean(d * d, axis=1, keepdims=True)
    rstd = lax.rsqrt(var + LN_EPS)
    return d * rstd, rstd


def _layer_norm_bwd(dxh, xhat, rstd):
    m1 = jnp.mean(dxh, axis=1, keepdims=True)
    m2 = jnp.mean(dxh * xhat, axis=1, keepdims=True)
    return rstd * (dxh - m1 - xhat * m2)


def _prep_shards(me, shards):
    names = list(shards)

    def body(me_ref, *refs):
        for name, src, dst in zip(names, refs[:len(names)], refs[len(names):]):
            val = src[...]
            dst[...] = (val.T if name in TRANSPOSED_HERE else val).astype(BF16)

    shape = lambda n, a: a.shape[::-1] if n in TRANSPOSED_HERE else a.shape
    shapes = [shape(n, shards[n]) for n in names]
    out = pl.pallas_call(
        body, name="prep_shards",
        grid_spec=pltpu.PrefetchScalarGridSpec(
            num_scalar_prefetch=1, grid=(1,),
            in_specs=[pl.BlockSpec(shards[n].shape, lambda i, me_ref: (0, 0)) for n in names],
            out_specs=[pl.BlockSpec(s, lambda i, me_ref: (me_ref[0], 0)) for s in shapes]),
        out_shape=[SDS((N_DEV * s[0], s[1]), BF16) for s in shapes], compiler_params=_params(("arbitrary",)),
    )(me, *[shards[n] for n in names])
    return dict(zip(names, out))


def _mesh_pos():
    return lax.axis_index("x"), lax.axis_index("y"), lax.axis_index("c")


HBM_SPEC = pl.BlockSpec(memory_space=pltpu.HBM)
SEM_SPEC = pl.BlockSpec(memory_space=pltpu.SEMAPHORE)
ANY_SPEC = pl.BlockSpec(memory_space=pl.ANY)
SIDE_EFFECT = pltpu.SideEffectType.DATAFLOW_SIDE_EFFECTING
PEER_SEMS = pltpu.SemaphoreType.DMA((N_DEV - 1,))


def _in_hbm(a):
    return pltpu.with_memory_space_constraint(a, pltpu.HBM)


def _split_copy_start(name, items, copy_of):
    n = len(items)
    flat = [a for it in items for a in it]
    k = len(flat)

    def body(*refs):
        arr, sems = list(refs[:k]), refs[k:k + 2 * n]
        for i, it in enumerate(items):
            mine = [arr.pop(0) for _ in it]
            for m in range(1, N_DEV):
                copy_of(m, mine, sems[i].at[m - 1], sems[n + i].at[m - 1]).start()
        token = refs[-1]
        token[...] = jnp.zeros_like(token)

    res = pl.pallas_call(
        body, name=name,
        out_shape=[PEER_SEMS] * (2 * n) + [pltpu.HBM(a.shape, a.dtype) for a in flat] + [SDS((8, LANES), F32)],
        in_specs=[HBM_SPEC] * k,
        out_specs=[SEM_SPEC] * (2 * n) + [HBM_SPEC] * k + [pl.BlockSpec(memory_space=pltpu.VMEM)],
        input_output_aliases={j: 2 * n + j for j in range(k)},
        compiler_params=pltpu.CompilerParams(has_side_effects=SIDE_EFFECT),
    )(*[_in_hbm(a) for a in flat])
    thru, out_items = list(res[2 * n:2 * n + k]), []
    for it in items:
        out_items.append(tuple(thru.pop(0) for _ in it))
    return dict(send=res[:n], recv=res[n:2 * n], items=out_items, token=res[-1])


def _split_copy_wait(name, started, which, copy_of, after):
    items = [started["items"][i] for i in which]
    n = len(items)
    flat = [a for it in items for a in it]
    k = len(flat)

    def body(*refs):
        arr, sems = list(refs[:k]), refs[k:k + 2 * n]
        for i, it in enumerate(items):
            mine = [arr.pop(0) for _ in it]
            for m in range(1, N_DEV):
                cp = copy_of(m, mine, sems[i].at[m - 1], sems[n + i].at[m - 1])
                cp.wait_send()
                cp.wait_recv()

    res = pl.pallas_call(
        body, name=name,
        out_shape=[pltpu.HBM(a.shape, a.dtype) for a in flat],
        in_specs=[HBM_SPEC] * k + [SEM_SPEC] * (2 * n) + [ANY_SPEC] * len(after),
        out_specs=[HBM_SPEC] * k,
        input_output_aliases={j: j for j in range(k)},
        compiler_params=pltpu.CompilerParams(has_side_effects=SIDE_EFFECT),
    )(*flat, *[started["send"][i] for i in which], *[started["recv"][i] for i in which], *[_in_hbm(a) for a in after])
    thru, out_items = list(res), []
    for it in items:
        out_items.append(tuple(thru.pop(0) for _ in it))
    return out_items


def _gather_copy(m, refs, send_sem, recv_sem):
    (land_ref,) = refs
    r = land_ref.shape[0] // N_DEV
    mine = land_ref.at[pl.ds(pl.multiple_of(_peer_index(0) * r, 8), r), :]
    return pltpu.make_async_remote_copy(src_ref=mine, dst_ref=mine, send_sem=send_sem, recv_sem=recv_sem,
                                        device_id=_peer(m), device_id_type=MESH)


def _scatter_copy(m, refs, send_sem, recv_sem):
    buf_ref, land_ref = refs
    r = buf_ref.shape[0] // N_DEV
    src = buf_ref.at[pl.ds(pl.multiple_of(_peer_index(m) * r, 8), r), :]
    return pltpu.make_async_remote_copy(src_ref=src, dst_ref=land_ref.at[m], send_sem=send_sem, recv_sem=recv_sem,
                                        device_id=_peer(m), device_id_type=MESH)


def _peer(m):
    x, y, c = _mesh_pos()
    bx, by, bc = (m >> 2) & 1, (m >> 1) & 1, m & 1
    return (x ^ bx if bx else x, y ^ by if by else y, c ^ bc if bc else c)


def _peer_index(m):
    x, y, c = _mesh_pos()
    return (4 * x + 2 * y + c) ^ m


SMALL_PLACE = {
    "ln1_gain": (ROW_LN1G, 0), "ln1_bias": (ROW_LN1B, 0), "ln2_gain": (ROW_LN2G, 0), "ln2_bias": (ROW_LN2B, 0),
    "ret_gn_gain": (ROW_GN, 0), "ret_decay_fwd": (ROW_MISC, MISC_DF), "ret_decay_bwd": (ROW_MISC, MISC_DB),
    "attn_sink": (ROW_MISC, MISC_SINK)}


def _small_all_reduce_adamw(part, w, m, v):
    d = part.shape[1]
    names = list(SMALL_PLACE)
    k = len(names)

    def body(*refs):
        part_ref = refs[0]
        w_refs, m_refs, v_refs = refs[1:1 + k], refs[1 + k:1 + 2 * k], refs[1 + 2 * k:1 + 3 * k]
        outs = refs[1 + 3 * k:2 + 7 * k]
        loss_ref, g_refs, dl_refs = outs[0], outs[1:1 + k], outs[1 + k:1 + 2 * k]
        nm_refs, nv_refs = outs[1 + 2 * k:1 + 3 * k], outs[1 + 3 * k:1 + 4 * k]
        buf, send_sems, recv_sems = refs[2 + 7 * k:]
        buf[0] = part_ref[...]
        cps = [pltpu.make_async_remote_copy(
            src_ref=part_ref, dst_ref=buf.at[j], send_sem=send_sems.at[j - 1], recv_sem=recv_sems.at[j - 1],
            device_id=_peer(j), device_id_type=MESH) for j in range(1, N_DEV)]
        for cp in cps:
            cp.start()
        for cp in cps:
            cp.wait_recv()
        for cp in cps:
            cp.wait_send()
        me = _peer_index(0)
        tot = buf[me]
        for dev in range(1, N_DEV):
            tot = tot + buf[dev ^ me]
        loss_ref[...] = (0.5 / d) * jnp.sum(tot[ROW_LOSS:ROW_LOSS + 1, :], axis=1, keepdims=True)
        for i, name in enumerate(names):
            row, lo = SMALL_PLACE[name]
            wv = w_refs[i][...]
            g = tot[row:row + 1, lo:lo + wv.shape[1]]
            if name.startswith("ret_decay"):
                p2 = jnp.exp2(wv)
                g = g * (-p2 * jnp.log(2.0) / (1.0 - p2))
            g_refs[i][...] = g
            _adamw_store(g, wv, m_refs[i][...], v_refs[i][...], dl_refs[i], nm_refs[i], nv_refs[i])

    vm = pl.BlockSpec(memory_space=pltpu.VMEM)
    shapes = [SDS(w[n].shape, F32) for n in names]
    res = pl.pallas_call(
        body, name="small_all_reduce_adamw",
        out_shape=[SDS((1, 1), F32)] + shapes * 4, in_specs=[vm] * (1 + 3 * k), out_specs=[vm] * (1 + 4 * k),
        scratch_shapes=[pltpu.VMEM((N_DEV,) + part.shape, F32), pltpu.SemaphoreType.DMA((7,)),
                        pltpu.SemaphoreType.DMA((7,))],
    )(part, *[w[n] for n in names], *[m[n] for n in names], *[v[n] for n in names])
    groups = [dict(zip(names, res[1 + j * k:1 + (j + 1) * k])) for j in range(4)]
    return (res[0], *groups)


def _adamw_store(g, w, m, v, dl_ref, nm_ref, nv_ref):
    m = ADAM_B1 * m + (1.0 - ADAM_B1) * g
    v = ADAM_B2 * v + (1.0 - ADAM_B2) * (g * g)
    m_hat = m / (1.0 - ADAM_B1 ** ADAM_STEP)
    v_hat = v / (1.0 - ADAM_B2 ** ADAM_STEP)
    dl_ref[...] = -ADAM_LR * (m_hat / (jnp.sqrt(v_hat) + ADAM_EPS) + ADAM_WD * w)
    nm_ref[...] = m
    nv_ref[...] = v


def _reduce_adamw(name, own, recv, w, m, v, transposed):
    def body(own_ref, recv_ref, w_ref, m_ref, v_ref, g_ref, dl_ref, nm_ref, nv_ref):
        g = own_ref[...]
        for k in range(1, N_DEV):
            g = g + recv_ref[k].astype(F32)
        if transposed:
            g = g.T
        g_ref[...] = g
        _adamw_store(g, w_ref[...], m_ref[...], v_ref[...], dl_ref, nm_ref, nv_ref)

    out = SDS(w.shape, F32)
    return pl.pallas_call(body, name="adamw_" + name, out_shape=[out] * 4, compiler_params=_params())(own, recv, w, m, v)


def _row_spec(tm, width):
    return pl.BlockSpec((tm, width), lambda i: (i, 0))


def _full_spec(shape):
    return pl.BlockSpec(shape, lambda i: (0,) * len(shape), pipeline_mode=pl.Buffered(1))


def _acc_spec(shape):
    return pl.BlockSpec(shape, lambda i: (0,) * len(shape))


def _sub_rows(tm):
    step = min(SUB_ROWS, tm)
    return [(lo, lo + step) for lo in range(0, tm, step)]


def _in_proj(x2, wt_in):
    t, d = x2.shape
    u_w = wt_in.shape[0]
    tm = _row_tile(t, MATMUL_ROWS)

    def body(x_ref, w_ref, u_ref, xb_ref):
        xb = x_ref[...].astype(BF16)
        xb_ref[...] = xb
        u_ref[...] = _dot_nt(xb, w_ref[...]).astype(BF16)

    return pl.pallas_call(
        body, name="in_proj", grid=(t // tm,),
        in_specs=[_row_spec(tm, d), _full_spec(wt_in.shape)],
        out_specs=[_row_spec(tm, u_w), _row_spec(tm, d)],
        out_shape=[SDS((t, u_w), BF16), SDS((t, d), BF16)],
        compiler_params=_params(("parallel",)),
    )(x2, wt_in)


def _out_proj_ln1(r, a, x2, w_out, g1, b1):
    t, d = x2.shape
    tm = _row_tile(t, EPILOGUE_ROWS)

    def body(r_ref, a_ref, x_ref, wo_ref, g_ref, b_ref, z_ref, hb_ref):
        for lo, hi in _sub_rows(tm):
            mix = (_dot_nn(r_ref[lo:hi, :], wo_ref[0:RET_W, :])
                   + _dot_nn(a_ref[lo:hi, :], wo_ref[RET_W:RET_W + ATT_W, :]))
            z = ALPHA * x_ref[lo:hi, :] + mix
            xhat, _ = _layer_norm_stats(z)
            z_ref[lo:hi, :] = z
            hb_ref[lo:hi, :] = (xhat * g_ref[...] + b_ref[...]).astype(BF16)

    return pl.pallas_call(
        body, name="out_proj_ln1", grid=(t // tm,),
        in_specs=[_row_spec(tm, RET_W), _row_spec(tm, ATT_W), _row_spec(tm, d), _full_spec(w_out.shape),
                  _full_spec(g1.shape), _full_spec(b1.shape)],
        out_specs=[_row_spec(tm, d), _row_spec(tm, d)],
        out_shape=[SDS((t, d), F32), SDS((t, d), BF16)],
        compiler_params=_params(("parallel",)),
    )(r, a, x2, w_out, g1, b1)


def _col_halves(f):
    n = f // LANES
    k = (n + 1) // 2 * LANES
    return [(0, k), (k, f)] if k < f else [(0, f)]


def _ffn_up(h1b, wt_gate, wt_up):
    t, d = h1b.shape
    f = wt_gate.shape[0]
    tm = _row_tile(t, EPILOGUE_ROWS)

    def body(h_ref, wg_ref, wu_ref, dg_ref, du_ref, act_ref):
        h = h_ref[...]
        for lo, hi in _col_halves(f):
            g = _dot_nt(h, wg_ref[lo:hi, :])
            u = _dot_nt(h, wu_ref[lo:hi, :])
            sg = _sigmoid(g)
            silu = g * sg
            dg_ref[:, lo:hi] = (u * (sg * (1.0 + g * (1.0 - sg)))).astype(BF16)
            du_ref[:, lo:hi] = silu.astype(BF16)
            act_ref[:, lo:hi] = (silu * u).astype(BF16)

    return pl.pallas_call(
        body, name="ffn_up", grid=(t // tm,),
        in_specs=[_row_spec(tm, d), _full_spec(wt_gate.shape), _full_spec(wt_up.shape)],
        out_specs=[_row_spec(tm, f)] * 3,
        out_shape=[SDS((t, f), BF16)] * 3,
        compiler_params=_params(("parallel",)),
    )(h1b, wt_gate, wt_up)


def _ffn_down_ln2_loss(act, h1b, p2, z1, target, w_down, w_pg, wt_pe, g1, b1, g2, b2):
    t, d = z1.shape
    f = act.shape[1]
    pdim = p2.shape[1]
    tm = _row_tile(t, EPILOGUE_ROWS)

    def body(act_ref, hb_ref, p_ref, z1_ref, tgt_ref, wd_ref, wpg_ref, wpe_ref, g1_ref, b1_ref, g2_ref, b2_ref,
             dz_ref, dzb_ref, ds_ref, dple_ref, acc_ref):
        @pl.when(pl.program_id(0) == 0)
        def _():
            acc_ref[...] = jnp.zeros_like(acc_ref)

        for lo, hi in _sub_rows(tm):
            xhat1, _ = _layer_norm_stats(z1_ref[lo:hi, :])
            h1 = xhat1 * g1_ref[...] + b1_ref[...]
            ffn = _dot_nn(act_ref[lo:hi, :], wd_ref[...])
            pg = _sigmoid(_dot_nn(hb_ref[lo:hi, :], wpg_ref[...]))
            ple = _dot_nt(p_ref[lo:hi, :].astype(BF16), wpe_ref[...])
            z2 = ALPHA * h1 + ffn + pg * ple
            xhat2, rstd2 = _layer_norm_stats(z2)
            err = xhat2 * g2_ref[...] + b2_ref[...] - tgt_ref[lo:hi, :]
            dy = err * (1.0 / d)
            dz = _layer_norm_bwd(dy * g2_ref[...], xhat2, rstd2)
            dz_ref[lo:hi, :] = dz
            dzb_ref[lo:hi, :] = dz.astype(BF16)
            ds_ref[lo:hi, :] = (dz * ple * pg * (1.0 - pg)).astype(BF16)
            dple_ref[lo:hi, :] = (dz * pg).astype(BF16)
            acc_ref[0:1, :] += jnp.sum(err * err, axis=0, keepdims=True)
            acc_ref[1:2, :] += jnp.sum(dy * xhat2, axis=0, keepdims=True)
            acc_ref[2:3, :] += jnp.sum(dy, axis=0, keepdims=True)

    vec = _full_spec(g1.shape)
    return pl.pallas_call(
        body, name="ffn_down_ln2_loss", grid=(t // tm,),
        in_specs=[_row_spec(tm, f), _row_spec(tm, d), _row_spec(tm, pdim), _row_spec(tm, d), _row_spec(tm, d),
                  _full_spec(w_down.shape), _full_spec(w_pg.shape), _full_spec(wt_pe.shape), vec, vec, vec, vec],
        out_specs=[_row_spec(tm, d)] * 4 + [_acc_spec((8, d))],
        out_shape=[SDS((t, d), F32), SDS((t, d), BF16), SDS((t, d), BF16), SDS((t, d), BF16), SDS((8, d), F32)],
        compiler_params=_params(("arbitrary",)),
    )(act, h1b, p2, z1, target, w_down, w_pg, wt_pe, g1, b1, g2, b2)


def _after(after, body):
    k = len(after)
    return (lambda *refs: body(*refs[k:])), [ANY_SPEC] * k


def _ffn_bwd_act(dzb, dact_dg, dact_du, w_down, after=()):
    t, d = dzb.shape
    f = dact_dg.shape[1]
    tm = _row_tile(t, EPILOGUE_ROWS)

    def body(dz_ref, fg_ref, fu_ref, wd_ref, dg_ref, du_ref):
        dz = dz_ref[...]
        for lo, hi in _col_halves(f):
            da = _dot_nt(dz, wd_ref[lo:hi, :])
            dg_ref[:, lo:hi] = (da * fg_ref[:, lo:hi].astype(F32)).astype(BF16)
            du_ref[:, lo:hi] = (da * fu_ref[:, lo:hi].astype(F32)).astype(BF16)

    body, lead = _after(after, body)
    return pl.pallas_call(
        body, name="ffn_bwd_act", grid=(t // tm,),
        in_specs=lead + [_row_spec(tm, d), _row_spec(tm, f), _row_spec(tm, f), _full_spec(w_down.shape)],
        out_specs=[_row_spec(tm, f)] * 2,
        out_shape=[SDS((t, f), BF16)] * 2,
        compiler_params=_params(("parallel",)),
    )(*after, dzb, dact_dg, dact_du, w_down)


def _dh1_ln1_bwd(dz2, dg, dup, dsb, z1, wt_gate, wt_up, w_pg, g1, after=()):
    t, d = dz2.shape
    f = dg.shape[1]
    tm = _row_tile(t, EPILOGUE_ROWS)

    def body(dz_ref, dg_ref, du_ref, ds_ref, z1_ref, wg_ref, wu_ref, wpg_ref, g1_ref, dz1_ref, dz1b_ref, acc_ref):
        @pl.when(pl.program_id(0) == 0)
        def _():
            acc_ref[...] = jnp.zeros_like(acc_ref)

        for lo, hi in _sub_rows(tm):
            dh = (ALPHA * dz_ref[lo:hi, :] + _dot_nn(dg_ref[lo:hi, :], wg_ref[...])
                  + _dot_nn(du_ref[lo:hi, :], wu_ref[...]) + _dot_nt(ds_ref[lo:hi, :], wpg_ref[...]))
            xhat, rstd = _layer_norm_stats(z1_ref[lo:hi, :])
            dz1 = _layer_norm_bwd(dh * g1_ref[...], xhat, rstd)
            dz1_ref[lo:hi, :] = dz1
            dz1b_ref[lo:hi, :] = dz1.astype(BF16)
            acc_ref[0:1, :] += jnp.sum(dh * xhat, axis=0, keepdims=True)
            acc_ref[1:2, :] += jnp.sum(dh, axis=0, keepdims=True)

    body, lead = _after(after, body)
    return pl.pallas_call(
        body, name="dh1_ln1_bwd", grid=(t // tm,),
        in_specs=lead + [_row_spec(tm, d), _row_spec(tm, f), _row_spec(tm, f), _row_spec(tm, d), _row_spec(tm, d),
                         _full_spec(wt_gate.shape), _full_spec(wt_up.shape), _full_spec(w_pg.shape),
                         _full_spec(g1.shape)],
        out_specs=[_row_spec(tm, d), _row_spec(tm, d), _acc_spec((8, d))],
        out_shape=[SDS((t, d), F32), SDS((t, d), BF16), SDS((8, d), F32)],
        compiler_params=_params(("arbitrary",)),
    )(*after, dz2, dg, dup, dsb, z1, wt_gate, wt_up, w_pg, g1)


def _out_proj_bwd(dz1b, w_out, after=()):
    t, d = dz1b.shape
    tm = _row_tile(t, MATMUL_ROWS)

    def body(dz_ref, wo_ref, dr_ref, da_ref):
        dz = dz_ref[...]
        dr_ref[...] = _dot_nt(dz, wo_ref[0:RET_W, :]).astype(BF16)
        da_ref[...] = _dot_nt(dz, wo_ref[RET_W:RET_W + ATT_W, :]).astype(BF16)

    body, lead = _after(after, body)
    return pl.pallas_call(
        body, name="out_proj_bwd", grid=(t // tm,),
        in_specs=lead + [_row_spec(tm, d), _full_spec(w_out.shape)],
        out_specs=[_row_spec(tm, RET_W), _row_spec(tm, ATT_W)],
        out_shape=[SDS((t, RET_W), BF16), SDS((t, ATT_W), BF16)],
        compiler_params=_params(("parallel",)),
    )(*after, dz1b, w_out)


def _in_proj_bwd(dz1, parts, wt_in, after=()):
    t, d = dz1.shape
    tm = _row_tile(t, MATMUL_ROWS)
    widths = [p.shape[1] for p in parts]

    def body(*refs):
        dz_ref, part_refs, w_ref, dx_ref = refs[0], refs[1:1 + len(parts)], refs[-2], refs[-1]
        acc = ALPHA * dz_ref[...]
        lo = 0
        for p_ref, w in zip(part_refs, widths):
            acc = acc + _dot_nn(p_ref[...], w_ref[lo:lo + w, :])
            lo += w
        dx_ref[...] = acc

    body, lead = _after(after, body)
    return pl.pallas_call(
        body, name="in_proj_bwd", grid=(t // tm,),
        in_specs=lead + [_row_spec(tm, d)] + [_row_spec(tm, w) for w in widths] + [_full_spec(wt_in.shape)],
        out_specs=_row_spec(tm, d), out_shape=SDS((t, d), F32),
        compiler_params=_params(("parallel",)),
    )(*after, dz1, *parts, wt_in)


def _weight_grad(name, me, parts, rhs):
    t, n = rhs.shape
    widths = [p.shape[1] for p in parts]
    rows = sum(widths)
    own_rows = rows // N_DEV
    tk = _row_tile(t, MATMUL_ROWS)
    step = 256

    def body(*refs):
        me_ref, part_refs, rhs_ref = refs[0], refs[1:1 + len(parts)], refs[1 + len(parts)]
        full_ref, own_ref, acc = refs[-3], refs[-2], refs[-1]
        i = pl.program_id(0)

        @pl.when(i == 0)
        def _():
            acc[...] = jnp.zeros_like(acc)

        b = rhs_ref[...].astype(BF16)
        lo = 0
        for p_ref, w in zip(part_refs, widths):
            for c0 in range(0, w, step):
                c1 = min(c0 + step, w)
                acc[lo + c0:lo + c1, :] += _dot_tn(p_ref[:, c0:c1].astype(BF16), b)
            lo += w

        @pl.when(i == pl.num_programs(0) - 1)
        def _():
            full_ref[...] = acc[...].astype(BF16)
            own_ref[...] = acc[pl.ds(pl.multiple_of(me_ref[0] * own_rows, 8), own_rows), :]

    return pl.pallas_call(
        body, name=name, grid=(t // tk,),
        in_specs=[_smem_spec()] + [_row_spec(tk, w) for w in widths] + [_row_spec(tk, n)],
        out_specs=[_full_spec((rows, n)), _full_spec((own_rows, n))],
        out_shape=[SDS((rows, n), BF16), SDS((own_rows, n), F32)],
        scratch_shapes=[pltpu.VMEM((rows, n), F32)],
        compiler_params=_params(("arbitrary",)),
    )(me, *parts, rhs)


def _log_decay(decay_f, decay_b):
    def body(f_ref, b_ref, lf_ref, lb_ref):
        lf_ref[...] = jnp.log1p(-jnp.exp2(f_ref[...]))
        lb_ref[...] = jnp.log1p(-jnp.exp2(b_ref[...]))

    return pl.pallas_call(body, name="log_decay", out_shape=[SDS(decay_f.shape, F32)] * 2)(decay_f, decay_b)


def _chunk(ref, n):
    return ref[pl.ds(pl.multiple_of(n * CHUNK, CHUNK), CHUNK), :]


def _group_sum(is_a, v):
    sa = jnp.sum(jnp.where(is_a, v, 0.0), axis=1, keepdims=True)
    sb = jnp.sum(jnp.where(is_a, 0.0, v), axis=1, keepdims=True)
    return jnp.where(is_a, sa, sb)


def _seq_spec(s, col_block):
    return pl.BlockSpec((s, LANES), lambda b, h: (b, col_block + h))


def _smem_spec():
    return pl.BlockSpec(memory_space=pltpu.SMEM)


RET_UNROLL = 2


def _stacked_tables(lgf_ref, lgb_ref, pair):
    lane = lax.broadcasted_iota(jnp.int32, (1, LANES), 1)
    is_a = lane < HEAD_DIM
    lgf = jnp.where(is_a, lgf_ref[2 * pair], lgf_ref[2 * pair + 1])
    lgb = jnp.where(is_a, lgb_ref[2 * pair], lgb_ref[2 * pair + 1])
    row = lax.broadcasted_iota(jnp.int32, (CHUNK, 1), 0).astype(F32)
    kdec_f, qdec_f = jnp.exp(lgf * (CHUNK - 1.0 - row)), jnp.exp(lgf * (row + 1.0))
    kdec_b, qdec_b = jnp.exp(lgb * row), jnp.exp(lgb * (CHUNK - row))
    tab = dict(
        is_a=is_a, row=row, lam_f=jnp.exp(lgf * CHUNK), lam_b=jnp.exp(lgb * CHUNK),
        kdec=jnp.concatenate([kdec_f, kdec_b], axis=1), qdec=jnp.concatenate([qdec_f, qdec_b], axis=1),
        qexp=jnp.concatenate([jnp.broadcast_to(row + 1.0, (CHUNK, LANES)),
                              jnp.broadcast_to(CHUNK - row, (CHUNK, LANES))], axis=1),
        kexp=jnp.concatenate([jnp.broadcast_to(CHUNK - 1.0 - row, (CHUNK, LANES)),
                              jnp.broadcast_to(row, (CHUNK, LANES))], axis=1),
    )
    r = lax.broadcasted_iota(jnp.int32, (2 * LANES, LANES), 0)
    c = lax.broadcasted_iota(jnp.int32, (2 * LANES, LANES), 1)
    tab["diag2"] = ((r & (LANES - 1)) < HEAD_DIM) == (c < HEAD_DIM)
    i2 = lax.broadcasted_iota(jnp.int32, (2 * CHUNK, CHUNK), 0)
    j = lax.broadcasted_iota(jnp.int32, (2 * CHUNK, CHUNK), 1)
    head_b = i2 >= CHUNK
    diff = ((i2 & (CHUNK - 1)) - j).astype(F32)
    up, dn = jnp.maximum(diff, 0.0), jnp.maximum(-diff, 0.0)
    lgf2 = jnp.where(head_b, lgf_ref[2 * pair + 1], lgf_ref[2 * pair])
    lgb2 = jnp.where(head_b, lgb_ref[2 * pair + 1], lgb_ref[2 * pair])
    ef = jnp.where(diff >= 0, jnp.exp(lgf2 * up), 0.0)
    eb = jnp.where(diff <= 0, jnp.exp(lgb2 * dn), 0.0)
    tab["d2"] = ef + eb
    tab["df2"] = ef * up
    tab["db2"] = eb * dn
    return tab


def _stack_pair(is_a, x):
    zero = jnp.zeros_like(x)
    return jnp.concatenate([jnp.where(is_a, x, zero), jnp.where(is_a, zero, x)], axis=0)


def _unstack_pair(is_a, x2):
    return jnp.where(is_a, x2[0:CHUNK, :], x2[CHUNK:2 * CHUNK, :])


def _both_ways(x, dec):
    return (jnp.concatenate([x, x], axis=1) * dec).astype(BF16)


def _scan_states(n_chunks, st, up_rows, up_lam, down_rows, down_lam):
    zero = jnp.zeros((LANES, LANES), F32)

    def up(n, r):
        new = st[n, up_rows, :]
        st[n, up_rows, :] = r
        return r * up_lam + new

    def down(s, r):
        n = n_chunks - 1 - s
        new = st[n, down_rows, :]
        st[n, down_rows, :] = r
        return r * down_lam + new

    lax.fori_loop(0, n_chunks, up, zero)
    lax.fori_loop(0, n_chunks, down, zero)


FWD_ROWS, BWD_ROWS = pl.ds(0, LANES), pl.ds(LANES, LANES)


def _retention_fwd(u, lgf, lgb, gn_gain, b_loc):
    t = u.shape[0]
    s = t // b_loc
    n_chunks = s // CHUNK
    pairs = RET_HEADS // 2

    def body(lgf_ref, lgb_ref, q_ref, k_ref, v_ref, g_ref, gain_ref, r_ref, y_ref, st):
        tab = _stacked_tables(lgf_ref, lgb_ref, pl.program_id(1))
        is_a = tab["is_a"]

        def kv_body(n, _):
            k8 = _chunk(k_ref, n).astype(F32) * Q_SCALE
            st[n] = jnp.where(tab["diag2"], _dot_tn(_both_ways(k8, tab["kdec"]), _chunk(v_ref, n)), 0.0)
            return 0

        lax.fori_loop(0, n_chunks, kv_body, 0, unroll=RET_UNROLL)
        _scan_states(n_chunks, st, FWD_ROWS, tab["lam_f"], BWD_ROWS, tab["lam_b"])

        def out_body(n, _):
            q = _chunk(q_ref, n)
            k8 = (_chunk(k_ref, n).astype(F32) * Q_SCALE).astype(BF16)
            v = _chunk(v_ref, n)
            p2 = (_dot_nt(_stack_pair(is_a, q), k8) * tab["d2"]).astype(BF16)
            y = _unstack_pair(is_a, _dot_nn(p2, v))
            y = y + _dot_nn(_both_ways(q.astype(F32), tab["qdec"]), st[n].astype(BF16))
            rows = pl.ds(pl.multiple_of(n * CHUNK, CHUNK), CHUNK)
            y_ref[rows, :] = y
            mu = _group_sum(is_a, y) * (1.0 / HEAD_DIM)
            dlt = y - mu
            var = _group_sum(is_a, dlt * dlt) * (1.0 / HEAD_DIM)
            xhat = dlt * lax.rsqrt(var + GN_EPS)
            gate = _chunk(g_ref, n).astype(F32)
            r_ref[rows, :] = (xhat * gain_ref[...] * gate * _sigmoid(gate)).astype(BF16)
            return 0

        lax.fori_loop(0, n_chunks, out_body, 0, unroll=RET_UNROLL)

    lane_blk = lambda c0: _seq_spec(s, c0 // LANES)
    return pl.pallas_call(
        body, name="retention_fwd", grid=(b_loc, pairs),
        in_specs=[_smem_spec(), _smem_spec(), lane_blk(C_RQ), lane_blk(C_RK), lane_blk(C_RV), lane_blk(C_RG),
                  pl.BlockSpec((1, LANES), lambda b, h: (0, h))],
        out_specs=[_seq_spec(s, 0), _seq_spec(s, 0)],
        out_shape=[SDS((t, RET_W), BF16), SDS((t, RET_W), F32)],
        scratch_shapes=[pltpu.VMEM((n_chunks, 2 * LANES, LANES), F32)],
        compiler_params=_params(("parallel", "parallel")),
    )(lgf, lgb, u, u, u, u, gn_gain)


ST_GAIN, ST_XF, ST_XB, ST_IFA, ST_IFB, ST_IBA, ST_IBB, ST_LF, ST_LB = 0, 1, 2, 3, 4, 5, 6, 8, 9
ST_ROWS = 16


def _retention_bwd(u, y_pre, dr, lgf, lgb, gn_gain, b_loc):
    t = u.shape[0]
    s = t // b_loc
    n_chunks = s // CHUNK
    pairs = RET_HEADS // 2

    def body(lgf_ref, lgb_ref, q_ref, k_ref, v_ref, g_ref, y_ref, dr_ref, gain_ref,
             dq_ref, dk_ref, dv_ref, dg_ref, st_ref, st, gr, dy_s):
        tab = _stacked_tables(lgf_ref, lgb_ref, pl.program_id(1))
        is_a, row = tab["is_a"], tab["row"]
        gain = gain_ref[...]

        def norm_body(n, dgain):
            rows = pl.ds(pl.multiple_of(n * CHUNK, CHUNK), CHUNK)
            y = y_ref[rows, :]
            mu = _group_sum(is_a, y) * (1.0 / HEAD_DIM)
            dlt = y - mu
            var = _group_sum(is_a, dlt * dlt) * (1.0 / HEAD_DIM)
            rstd = lax.rsqrt(var + GN_EPS)
            xhat = dlt * rstd
            gate = g_ref[rows, :].astype(F32)
            sg = _sigmoid(gate)
            silu = gate * sg
            d_out = dr_ref[rows, :].astype(F32)
            dg_ref[rows, :] = (d_out * xhat * gain * (sg * (1.0 + gate * (1.0 - sg)))).astype(BF16)
            dxh = d_out * gain * silu
            m1 = _group_sum(is_a, dxh) * (1.0 / HEAD_DIM)
            m2 = _group_sum(is_a, dxh * xhat) * (1.0 / HEAD_DIM)
            dy = (rstd * (dxh - m1 - xhat * m2)).astype(BF16)
            dy_s[rows, :] = dy
            k8 = k_ref[rows, :].astype(F32) * Q_SCALE
            st[n] = jnp.where(tab["diag2"], _dot_tn(_both_ways(k8, tab["kdec"]), v_ref[rows, :]), 0.0)
            qf = q_ref[rows, :].astype(F32)
            gr[n] = jnp.where(tab["diag2"], _dot_tn(_both_ways(qf, tab["qdec"]), dy), 0.0)
            return dgain + jnp.sum(d_out * xhat * silu, axis=0, keepdims=True)

        dgain = lax.fori_loop(0, n_chunks, norm_body, jnp.zeros((1, LANES), F32), unroll=RET_UNROLL)
        _scan_states(n_chunks, st, FWD_ROWS, tab["lam_f"], BWD_ROWS, tab["lam_b"])
        _scan_states(n_chunks, gr, BWD_ROWS, tab["lam_b"], FWD_ROWS, tab["lam_f"])
        colsum = lambda x: jnp.sum(x, axis=0, keepdims=True)

        def grad_body(n, carry):
            xfb, ifa, ifb, iba, ibb, lf, lb = carry
            rows = pl.ds(pl.multiple_of(n * CHUNK, CHUNK), CHUNK)
            q = q_ref[rows, :]
            qf = q.astype(F32)
            k8f = k_ref[rows, :].astype(F32) * Q_SCALE
            k8 = k8f.astype(BF16)
            v = v_ref[rows, :]
            dy = dy_s[rows, :]
            q2, dy2 = _stack_pair(is_a, q), _stack_pair(is_a, dy)
            sc = _dot_nt(q2, k8)
            dp = _dot_nt(dy2, v)
            a2 = (sc * tab["d2"]).astype(BF16)
            ds2 = (dp * tab["d2"]).astype(BF16)
            dq = _unstack_pair(is_a, _dot_nn(ds2, k8))
            dk = _dot_tn(ds2, q2)
            dv = _dot_tn(a2, dy2)
            prod = sc * dp
            pf, pb = prod * tab["df2"], prod * tab["db2"]
            ifa, ifb = ifa + colsum(pf[0:CHUNK, :]), ifb + colsum(pf[CHUNK:2 * CHUNK, :])
            iba, ibb = iba + colsum(pb[0:CHUNK, :]), ibb + colsum(pb[CHUNK:2 * CHUNK, :])
            states, sgrads = st[n], gr[n]
            sb, gb = states.astype(BF16), sgrads.astype(BF16)
            dqc = _dot_nt(dy, sb) * tab["qdec"]
            dkc = _dot_nt(v, gb) * tab["kdec"]
            dv = dv + _dot_nn(_both_ways(k8f, tab["kdec"]), gb)
            dq_ref[rows, :] = (dq + dqc[:, 0:LANES] + dqc[:, LANES:2 * LANES]).astype(BF16)
            dk_ref[rows, :] = ((dk + dkc[:, 0:LANES] + dkc[:, LANES:2 * LANES]) * Q_SCALE).astype(BF16)
            dv_ref[rows, :] = dv.astype(BF16)
            q2w, k2w = jnp.concatenate([qf, qf], axis=1), jnp.concatenate([k8f, k8f], axis=1)
            xfb = xfb + colsum(tab["qexp"] * q2w * dqc + tab["kexp"] * k2w * dkc)
            prod_s = sgrads * states
            lf, lb = lf + colsum(prod_s[0:LANES, :]), lb + colsum(prod_s[LANES:2 * LANES, :])
            return xfb, ifa, ifb, iba, ibb, lf, lb

        z = jnp.zeros((1, LANES), F32)
        init = (jnp.zeros((1, 2 * LANES), F32), z, z, z, z, z, z)
        xfb, ifa, ifb, iba, ibb, lf, lb = lax.fori_loop(0, n_chunks, grad_body, init, unroll=RET_UNROLL)
        st_ref[...] = jnp.zeros_like(st_ref)
        st_ref[ST_GAIN:ST_GAIN + 1, :] = dgain
        st_ref[ST_XF:ST_XF + 1, :] = xfb[:, 0:LANES]
        st_ref[ST_XB:ST_XB + 1, :] = xfb[:, LANES:2 * LANES]
        st_ref[ST_IFA:ST_IFA + 1, :] = ifa
        st_ref[ST_IFB:ST_IFB + 1, :] = ifb
        st_ref[ST_IBA:ST_IBA + 1, :] = iba
        st_ref[ST_IBB:ST_IBB + 1, :] = ibb
        st_ref[ST_LF:ST_LF + 1, :] = lf * (CHUNK * tab["lam_f"])
        st_ref[ST_LB:ST_LB + 1, :] = lb * (CHUNK * tab["lam_b"])

    lane_blk = lambda c0: _seq_spec(s, c0 // LANES)
    seq0 = _seq_spec(s, 0)
    state = pltpu.VMEM((n_chunks, 2 * LANES, LANES), F32)
    return pl.pallas_call(
        body, name="retention_bwd", grid=(b_loc, pairs),
        in_specs=[_smem_spec(), _smem_spec(), lane_blk(C_RQ), lane_blk(C_RK), lane_blk(C_RV), lane_blk(C_RG),
                  seq0, seq0, pl.BlockSpec((1, LANES), lambda b, h: (0, h))],
        out_specs=[seq0] * 4 + [pl.BlockSpec((ST_ROWS, LANES), lambda b, h: (b, h))],
        out_shape=[SDS((t, RET_W), BF16)] * 4 + [SDS((b_loc * ST_ROWS, RET_W), F32)],
        scratch_shapes=[state, state, pltpu.VMEM((s, LANES), BF16)],
        compiler_params=_params(("parallel", "parallel")),
    )(lgf, lgb, u, u, u, u, y_pre, dr, gn_gain)


GW = GROUP * HEAD_DIM
KEYS = 3 * BLOCK


def _attn_tables(g, bias_ref):
    r = lax.broadcasted_iota(jnp.int32, (GROUP * BLOCK, KEYS), 0)
    kj = lax.broadcasted_iota(jnp.int32, (GROUP * BLOCK, KEYS), 1)
    qi = r & (BLOCK - 1)
    hh = lax.shift_right_logical(r, 7)
    dist = jnp.abs(kj - BLOCK - qi)
    slope = jnp.exp2(-(GROUP * g + hh + 1).astype(F32) * (8.0 / ATTN_HEADS))
    bias_ref[...] = jnp.where(dist <= BLOCK, -slope * dist.astype(F32), NEG_INF)


def _tile_keys(x_ref, g, scale, pad_ref, s):
    r = lax.broadcasted_iota(jnp.int32, (LANES, GW), 0)
    c = lax.broadcasted_iota(jnp.int32, (LANES, GW), 1)
    place = jnp.where(r == g * HEAD_DIM + (c & (HEAD_DIM - 1)), 1.0, 0.0).astype(BF16)
    pad_ref[0:BLOCK, :] = jnp.zeros((BLOCK, GW), BF16)
    pad_ref[BLOCK + s:2 * BLOCK + s, :] = jnp.zeros((BLOCK, GW), BF16)
    pad_ref[BLOCK:BLOCK + s, :] = (_dot_nn(x_ref[...], place) * scale).astype(BF16)


def _stack_heads(x):
    lane_h = lax.shift_right_logical(lax.broadcasted_iota(jnp.int32, (1, GW), 1), 6)
    zero = jnp.zeros_like(x)
    return jnp.concatenate([jnp.where(lane_h == h, x, zero) for h in range(GROUP)], axis=0)


def _unstack_heads(x4):
    lane_h = lax.shift_right_logical(lax.broadcasted_iota(jnp.int32, (1, GW), 1), 6)
    out = jnp.zeros((BLOCK, GW), F32)
    for h in range(GROUP):
        out = out + jnp.where(lane_h == h, x4[h * BLOCK:(h + 1) * BLOCK, :], 0.0)
    return out


def _sink_column(sink_ref, g):
    rh = lax.shift_right_logical(lax.broadcasted_iota(jnp.int32, (GROUP * BLOCK, 1), 0), 7)
    col = jnp.zeros((GROUP * BLOCK, 1), F32)
    for h in range(GROUP):
        col = jnp.where(rh == h, sink_ref[GROUP * g + h], col)
    return col


def _attn_probs(qm, k3, bias_ref, sink_col, n, s):
    logits = _dot_nt(qm, k3) + bias_ref[...]
    kpos = n * BLOCK - BLOCK + lax.broadcasted_iota(jnp.int32, (1, KEYS), 1)
    logits = jnp.where((kpos >= 0) & (kpos < s), logits, NEG_INF)
    m = jnp.maximum(jnp.max(logits, axis=1, keepdims=True), sink_col)
    e = jnp.exp(logits - m)
    e_sink = jnp.exp(sink_col - m)
    inv = 1.0 / (jnp.sum(e, axis=1, keepdims=True) + e_sink)
    return e * inv, e_sink * inv


def _attn_specs(s, n_blocks):
    q_spec = pl.BlockSpec((BLOCK, GW), lambda b, g, n: (b * n_blocks + n, C_AQ // GW + g))
    k_spec = pl.BlockSpec((s, LANES), lambda b, g, n: (b, C_AK // LANES))
    v_spec = pl.BlockSpec((s, LANES), lambda b, g, n: (b, C_AV // LANES))
    o_spec = pl.BlockSpec((BLOCK, GW), lambda b, g, n: (b * n_blocks + n, g))
    return q_spec, k_spec, v_spec, o_spec


def _attention_fwd(u, sink, b_loc):
    t = u.shape[0]
    s = t // b_loc
    n_blocks = s // BLOCK

    def body(sink_ref, q_ref, k_ref, v_ref, o_ref, kpad, vpad, bias):
        g, n = pl.program_id(1), pl.program_id(2)

        @pl.when(n == 0)
        def _():
            _attn_tables(g, bias)
            _tile_keys(k_ref, g, Q_SCALE, kpad, s)
            _tile_keys(v_ref, g, 1.0, vpad, s)

        keys = pl.ds(pl.multiple_of(n * BLOCK, BLOCK), KEYS)
        p, _ = _attn_probs(_stack_heads(q_ref[...]), kpad[keys, :], bias, _sink_column(sink_ref, g), n, s)
        o_ref[...] = _unstack_heads(_dot_nn(p.astype(BF16), vpad[keys, :])).astype(BF16)

    q_spec, k_spec, v_spec, o_spec = _attn_specs(s, n_blocks)
    pad = pltpu.VMEM((s + 2 * BLOCK, GW), BF16)
    return pl.pallas_call(
        body, name="attention_fwd", grid=(b_loc, KV_HEADS, n_blocks),
        in_specs=[_smem_spec(), q_spec, k_spec, v_spec], out_specs=o_spec,
        out_shape=SDS((t, ATT_W), BF16),
        scratch_shapes=[pad, pad, pltpu.VMEM((GROUP * BLOCK, KEYS), F32)],
        compiler_params=_params(("parallel", "arbitrary", "arbitrary")),
    )(sink, u, u, u)


def _fold_groups(x, g):
    x = x + pltpu.roll(x, 2 * HEAD_DIM, 1)
    x = x + pltpu.roll(x, HEAD_DIM, 1)
    lane_g = lax.shift_right_logical(lax.broadcasted_iota(jnp.int32, (1, LANES), 1), 6)
    return jnp.where(lane_g == g, x[:, 0:LANES], 0.0)


def _attention_bwd(u, da, sink, b_loc):
    t = u.shape[0]
    s = t // b_loc
    n_blocks = s // BLOCK

    def body(sink_ref, q_ref, k_ref, v_ref, do_ref, dq_ref, dk_ref, dv_ref, dsink_ref, kpad, vpad, bias, dk_acc, dv_acc):
        g, n = pl.program_id(1), pl.program_id(2)

        @pl.when(n == 0)
        def _():
            _attn_tables(g, bias)
            _tile_keys(k_ref, g, Q_SCALE, kpad, s)
            _tile_keys(v_ref, g, 1.0, vpad, s)
            dsink_ref[...] = jnp.zeros_like(dsink_ref)

        @pl.when((n == 0) & (g == 0))
        def _():
            dk_acc[...] = jnp.zeros_like(dk_acc)
            dv_acc[...] = jnp.zeros_like(dv_acc)

        keys = pl.ds(pl.multiple_of(n * BLOCK, BLOCK), KEYS)
        qm = _stack_heads(q_ref[...])
        k3, v3 = kpad[keys, :], vpad[keys, :]
        p, p_sink = _attn_probs(qm, k3, bias, _sink_column(sink_ref, g), n, s)
        dom = _stack_heads(do_ref[...])
        dp = _dot_nt(dom, v3)
        delta = jnp.sum(p * dp, axis=1, keepdims=True)
        ds_mat = (p * (dp - delta)).astype(BF16)
        dq_ref[...] = _unstack_heads(_dot_nn(ds_mat, k3)).astype(BF16)
        dk_acc[keys, :] += _fold_groups(_dot_tn(ds_mat, qm), g) * Q_SCALE
        dv_acc[keys, :] += _fold_groups(_dot_tn(p.astype(BF16), dom), g)
        w = p_sink * delta
        rows = lax.broadcasted_iota(jnp.int32, dsink_ref.shape, 0)
        upd = jnp.zeros(dsink_ref.shape, F32)
        for h in range(GROUP):
            upd = jnp.where(rows == h, -jnp.sum(w[h * BLOCK:(h + 1) * BLOCK, :]), upd)
        dsink_ref[...] += upd

        @pl.when((n == n_blocks - 1) & (g == KV_HEADS - 1))
        def _():
            dk_ref[...] = dk_acc[BLOCK:BLOCK + s, :].astype(BF16)
            dv_ref[...] = dv_acc[BLOCK:BLOCK + s, :].astype(BF16)

    q_spec, k_spec, v_spec, o_spec = _attn_specs(s, n_blocks)
    kv_out = pl.BlockSpec((s, LANES), lambda b, g, n: (b, 0))
    pad = pltpu.VMEM((s + 2 * BLOCK, GW), BF16)
    acc = pltpu.VMEM((s + 2 * BLOCK, LANES), F32)
    return pl.pallas_call(
        body, name="attention_bwd", grid=(b_loc, KV_HEADS, n_blocks),
        in_specs=[_smem_spec(), q_spec, k_spec, v_spec, o_spec],
        out_specs=[o_spec, kv_out, kv_out, pl.BlockSpec((8, LANES), lambda b, g, n: (b * KV_HEADS + g, 0))],
        out_shape=[SDS((t, ATT_W), BF16), SDS((t, KV_W), BF16), SDS((t, KV_W), BF16),
                   SDS((b_loc * KV_HEADS * 8, LANES), F32)],
        scratch_shapes=[pad, pad, pltpu.VMEM((GROUP * BLOCK, KEYS), F32), acc, acc],
        compiler_params=_params(("arbitrary", "arbitrary", "arbitrary")),
    )(sink, u, u, u, da)


def _pack_small(acc2, acc1, ret_stats, dsink, b_loc, d):
    pairs = RET_HEADS // 2

    def body(acc2_ref, acc1_ref, st_ref, dsink_ref, out_ref):
        out_ref[...] = jnp.zeros_like(out_ref)
        out_ref[ROW_LN1G:ROW_LN1G + 1, :] = acc1_ref[0:1, :]
        out_ref[ROW_LN1B:ROW_LN1B + 1, :] = acc1_ref[1:2, :]
        out_ref[ROW_LN2G:ROW_LN2G + 1, :] = acc2_ref[1:2, :]
        out_ref[ROW_LN2B:ROW_LN2B + 1, :] = acc2_ref[2:3, :]
        out_ref[ROW_LOSS:ROW_LOSS + 1, :] = acc2_ref[0:1, :]
        st = st_ref[0:ST_ROWS, :]
        for b in range(1, b_loc):
            st = st + st_ref[b * ST_ROWS:(b + 1) * ST_ROWS, :]
        out_ref[ROW_GN:ROW_GN + 1, 0:RET_W] = st[ST_GAIN:ST_GAIN + 1, :]
        lane = lax.broadcasted_iota(jnp.int32, (1, d), 1)
        misc = jnp.zeros((1, d), F32)
        for pr in range(pairs):
            blk = st[:, pr * LANES:(pr + 1) * LANES]
            half = lax.broadcasted_iota(jnp.int32, (1, LANES), 1) < HEAD_DIM
            for h in range(2):
                sel = half if h == 0 else jnp.logical_not(half)
                cross_f = jnp.sum(jnp.where(sel, blk[ST_XF:ST_XF + 1, :] + blk[ST_LF:ST_LF + 1, :], 0.0))
                cross_b = jnp.sum(jnp.where(sel, blk[ST_XB:ST_XB + 1, :] + blk[ST_LB:ST_LB + 1, :], 0.0))
                intra_f = jnp.sum(blk[ST_IFA + h:ST_IFA + h + 1, :])
                intra_b = jnp.sum(blk[ST_IBA + h:ST_IBA + h + 1, :])
                head = 2 * pr + h
                misc = jnp.where(lane == MISC_DF + head, cross_f + intra_f, misc)
                misc = jnp.where(lane == MISC_DB + head, cross_b + intra_b, misc)
        for g in range(KV_HEADS):
            tot = dsink_ref[g * 8:(g + 1) * 8, :]
            for b in range(1, b_loc):
                tot = tot + dsink_ref[(b * KV_HEADS + g) * 8:(b * KV_HEADS + g + 1) * 8, :]
            for h in range(GROUP):
                misc = jnp.where(lane == MISC_SINK + GROUP * g + h, jnp.sum(tot[h:h + 1, 0:1]), misc)
        out_ref[ROW_MISC:ROW_MISC + 1, :] = misc

    return pl.pallas_call(body, name="pack_small", out_shape=SDS((SMALL_ROWS, d), F32))(acc2, acc1, ret_stats, dsink)


BIG = ("w_in", "w_out", "w_ffn_gate", "w_ffn_up", "w_ffn_down", "w_ple_proj", "w_ple_gate")
TRANSPOSED_OUTSIDE = ("w_in", "w_ffn_gate", "w_ffn_up")
TRANSPOSED_HERE = ("w_ple_proj",)
SMALL = ("ret_decay_fwd", "ret_decay_bwd", "ret_gn_gain", "attn_sink", "ln1_gain", "ln1_bias", "ln2_gain", "ln2_bias")
ORDER = ("w_in", "ret_decay_fwd", "ret_decay_bwd", "ret_gn_gain", "attn_sink", "w_out", "ln1_gain", "ln1_bias",
         "w_ffn_gate", "w_ffn_up", "w_ffn_down", "w_ple_proj", "w_ple_gate", "ln2_gain", "ln2_bias")


GATHER_ORDER = ("w_in", "w_out", "w_ffn_gate", "w_ffn_up", "w_ple_gate", "w_ple_proj", "w_ffn_down")


def _local_step(x2, p2, target2, fetch, publish, small, b_loc, me):
    d = x2.shape[1]
    lgf, lgb = _log_decay(small["ret_decay_fwd"], small["ret_decay_bwd"])
    lgf1, lgb1, sink1 = lgf.reshape(-1), lgb.reshape(-1), small["attn_sink"].reshape(-1)
    (w_in,) = fetch(("w_in",), ())
    u, xb = _in_proj(x2, w_in)
    r, y_pre = _retention_fwd(u, lgf1, lgb1, small["ret_gn_gain"], b_loc)
    a = _attention_fwd(u, sink1, b_loc)
    w_out, w_gate, w_up = fetch(("w_out", "w_ffn_gate", "w_ffn_up"), (r, a))
    z1, h1b = _out_proj_ln1(r, a, x2, w_out, small["ln1_gain"], small["ln1_bias"])
    dact_dg, dact_du, act = _ffn_up(h1b, w_gate, w_up)
    w_pg, w_pe, w_down = fetch(("w_ple_gate", "w_ple_proj", "w_ffn_down"), (act,))
    dz2, dz2b, dsb, dpleb, acc2 = _ffn_down_ln2_loss(
        act, h1b, p2, z1, target2, w_down, w_pg, w_pe,
        small["ln1_gain"], small["ln1_bias"], small["ln2_gain"], small["ln2_bias"])
    own = {}

    def grad(name, parts, rhs):
        whole, own[name] = _weight_grad("grad_" + name, me, parts, rhs)
        return whole

    t1 = publish("ffn_down", dict(w_ffn_down=grad("w_ffn_down", [act], dz2b),
                                  w_ple_proj=grad("w_ple_proj", [dpleb], p2),
                                  w_ple_gate=grad("w_ple_gate", [h1b], dsb)))
    dg, dup = _ffn_bwd_act(dz2b, dact_dg, dact_du, w_down, t1)
    t2 = publish("ffn_up", dict(w_ffn_gate=grad("w_ffn_gate", [dg], h1b), w_ffn_up=grad("w_ffn_up", [dup], h1b)))
    dz1, dz1b, acc1 = _dh1_ln1_bwd(dz2, dg, dup, dsb, z1, w_gate, w_up, w_pg, small["ln1_gain"], t2)
    t3 = publish("out", dict(w_out=grad("w_out", [r, a], dz1b)))
    dr, da = _out_proj_bwd(dz1b, w_out, t3)
    dq, dk, dv, dgate, ret_stats = _retention_bwd(u, y_pre, dr, lgf1, lgb1, small["ret_gn_gain"], b_loc)
    daq, dak, dav, dsink = _attention_bwd(u, da, sink1, b_loc)
    parts = [dq, dk, dv, dgate, daq, dak, dav]
    t4 = publish("in", dict(w_in=grad("w_in", parts, xb)))
    grad_x = _in_proj_bwd(dz1, parts, w_in, t4)
    small_part = _pack_small(acc2, acc1, ret_stats, dsink, b_loc, d)
    return grad_x, own, small_part


def kernel(x, p, w_in, ret_decay_fwd, ret_decay_bwd, ret_gn_gain, attn_sink, w_out, ln1_gain, ln1_bias, w_ffn_gate, w_ffn_up, w_ffn_down, w_ple_proj, w_ple_gate, ln2_gain, ln2_bias, loss_target, m_w_in, m_ret_decay_fwd, m_ret_decay_bwd, m_ret_gn_gain, m_attn_sink, m_w_out, m_ln1_gain, m_ln1_bias, m_w_ffn_gate, m_w_ffn_up, m_w_ffn_down, m_w_ple_proj, m_w_ple_gate, m_ln2_gain, m_ln2_bias, v_w_in, v_ret_decay_fwd, v_ret_decay_bwd, v_ret_gn_gain, v_attn_sink, v_w_out, v_ln1_gain, v_ln1_bias, v_w_ffn_gate, v_w_ffn_up, v_w_ffn_down, v_w_ple_proj, v_w_ple_gate, v_ln2_gain, v_ln2_bias):
    given = dict(locals())

    def strip(n, a):
        if n not in BIG:
            return a
        return a[0].T if n in TRANSPOSED_OUTSIDE else a[0]

    def restore(n, a):
        if n not in BIG:
            return a
        return (a.T if n in TRANSPOSED_OUTSIDE else a)[None]

    w = {n: strip(n, given[n]) for n in ORDER}
    m = {n: strip(n, given["m_" + n]) for n in ORDER}
    v = {n: strip(n, given["v_" + n]) for n in ORDER}
    b_loc, s, d = x.shape
    x2 = x.reshape(b_loc * s, d)
    p2 = p[0].reshape(b_loc * s, p.shape[-1])
    target2 = loss_target.reshape(b_loc * s, d)

    small = {n: w[n] for n in SMALL}
    me = (4 * lax.axis_index("x") + 2 * lax.axis_index("y") + lax.axis_index("c")).astype(jnp.int32).reshape(1)

    gathered = _prep_shards(me, {n: w[n] for n in BIG})
    gather = _split_copy_start("gather_start", [(gathered[n],) for n in GATHER_ORDER], _gather_copy)

    def fetch(names, after):
        which = [GATHER_ORDER.index(n) for n in names]
        got = _split_copy_wait("gather_wait_" + names[0], gather, which, _gather_copy, list(after))
        return [item[0] for item in got]

    scatters = []

    def publish(tag, products):
        names = list(products)
        items = [(products[n], lax.empty((N_DEV, products[n].shape[0] // N_DEV, products[n].shape[1]), BF16))
                 for n in names]
        started = _split_copy_start("scatter_start_" + tag, items, _scatter_copy)
        scatters.append((tag, names, started))
        return (started["token"],)

    grad_x, own, small_part = _local_step(x2, p2, target2, fetch, publish, small, b_loc, me)

    out_g, out_d, out_m, out_v = {}, {}, {}, {}
    after = [grad_x]
    for tag, names, started in scatters:
        landed = _split_copy_wait("scatter_wait_" + tag, started, list(range(len(names))), _scatter_copy, after)
        for n, (_, recv) in zip(names, landed):
            out_g[n], out_d[n], out_m[n], out_v[n] = _reduce_adamw(
                n, own[n], recv, w[n], m[n], v[n], n in TRANSPOSED_HERE)
        after = [out_v[names[-1]]]
        if tag == "out":
            loss, sg, sd, sm, sv = _small_all_reduce_adamw(
                small_part, small, {n: m[n] for n in SMALL}, {n: v[n] for n in SMALL})
            for dst, src in ((out_g, sg), (out_d, sd), (out_m, sm), (out_v, sv)):
                dst.update(src)
            after.append(sv["ln2_bias"])

    outs = [loss[0, 0], grad_x.reshape(x.shape)]
    for group in (out_g, out_d, out_m, out_v):
        outs += [restore(n, group[n]) for n in ORDER]
    return tuple(outs)
```

```python
import functools

import jax
import jax.numpy as jnp
from jax import lax
from jax.experimental import pallas as pl
from jax.experimental.pallas import tpu as pltpu

F32, BF16 = jnp.float32, jnp.bfloat16
SDS = jax.ShapeDtypeStruct
MESH = pl.DeviceIdType.MESH

N_DEV = 8
HEAD_DIM = 64
RET_HEADS = 8
ATTN_HEADS = 8
KV_HEADS = 2
GROUP = ATTN_HEADS // KV_HEADS
RET_W = RET_HEADS * HEAD_DIM
ATT_W = ATTN_HEADS * HEAD_DIM
KV_W = KV_HEADS * HEAD_DIM
LANES = 128
CHUNK = 128
BLOCK = 128
Q_SCALE = HEAD_DIM ** -0.5
ALPHA = 2.0 ** 0.25
LN_EPS = 1e-5
GN_EPS = 1e-5
NEG_INF = -1e30
C_RQ, C_RK, C_RV, C_RG = 0, RET_W, 2 * RET_W, 3 * RET_W
C_AQ = 4 * RET_W
C_AK = C_AQ + ATT_W
C_AV = C_AK + KV_W
IN_W = C_AV + KV_W

ADAM_LR = 0.001
ADAM_B1 = 0.9
ADAM_B2 = 0.999
ADAM_EPS = 1e-08
ADAM_WD = 0.01
ADAM_STEP = 10

VMEM_LIMIT = 56 * 1024 * 1024
MATMUL_ROWS = 512
EPILOGUE_ROWS = 256
SUB_ROWS = 256
SMALL_ROWS = 16
ROW_LN1G, ROW_LN1B, ROW_LN2G, ROW_LN2B, ROW_LOSS, ROW_GN, ROW_MISC = 0, 1, 2, 3, 4, 5, 6
MISC_DF, MISC_DB, MISC_SINK = 0, 8, 16


def _dot_nn(a, b):
    return lax.dot_general(a, b, (((1,), (0,)), ((), ())), preferred_element_type=F32)


def _dot_nt(a, b):
    return lax.dot_general(a, b, (((1,), (1,)), ((), ())), preferred_element_type=F32)


def _dot_tn(a, b):
    return lax.dot_general(a, b, (((0,), (0,)), ((), ())), preferred_element_type=F32)


def _params(sem=None, vmem=VMEM_LIMIT):
    kw = {"vmem_limit_bytes": vmem}
    if sem is not None:
        kw["dimension_semantics"] = sem
    return pltpu.CompilerParams(**kw)


def _row_tile(t, want=512):
    tm = want
    while t % tm:
        tm //= 2
    return tm


def _sigmoid(x):
    return 1.0 / (1.0 + jnp.exp(-x))


def _layer_norm_stats(z):
    mu = jnp.mean(z, axis=1, keepdims=True)
    d = z - mu
    var = jnp.mean(d * d, axis=1, keepdims=True)
    rstd = lax.rsqrt(var + LN_EPS)
    return d * rstd, rstd


def _layer_norm_bwd(dxh, xhat, rstd):
    m1 = jnp.mean(dxh, axis=1, keepdims=True)
    m2 = jnp.mean(dxh * xhat, axis=1, keepdims=True)
    return rstd * (dxh - m1 - xhat * m2)


def _prep_shards(me, shards):
    names = list(shards)

    def body(me_ref, *refs):
        for name, src, dst in zip(names, refs[:len(names)], refs[len(names):]):
            val = src[...]
            dst[...] = (val.T if name in TRANSPOSED_HERE else val).astype(BF16)

    shape = lambda n, a: a.shape[::-1] if n in TRANSPOSED_HERE else a.shape
    shapes = [shape(n, shards[n]) for n in names]
    out = pl.pallas_call(
        body, name="prep_shards",
        grid_spec=pltpu.PrefetchScalarGridSpec(
            num_scalar_prefetch=1, grid=(1,),
            in_specs=[pl.BlockSpec(shards[n].shape, lambda i, me_ref: (0, 0)) for n in names],
            out_specs=[pl.BlockSpec(s, lambda i, me_ref: (me_ref[0], 0)) for s in shapes]),
        out_shape=[SDS((N_DEV * s[0], s[1]), BF16) for s in shapes], compiler_params=_params(("arbitrary",)),
    )(me, *[shards[n] for n in names])
    return dict(zip(names, out))


def _mesh_pos():
    return lax.axis_index("x"), lax.axis_index("y"), lax.axis_index("c")


HBM_SPEC = pl.BlockSpec(memory_space=pltpu.HBM)
SEM_SPEC = pl.BlockSpec(memory_space=pltpu.SEMAPHORE)
ANY_SPEC = pl.BlockSpec(memory_space=pl.ANY)
SIDE_EFFECT = pltpu.SideEffectType.DATAFLOW_SIDE_EFFECTING
PEER_SEMS = pltpu.SemaphoreType.DMA((N_DEV - 1,))


def _in_hbm(a):
    return pltpu.with_memory_space_constraint(a, pltpu.HBM)


def _split_copy_start(name, items, copy_of):
    n = len(items)
    flat = [a for it in items for a in it]
    k = len(flat)

    def body(*refs):
        arr, sems = list(refs[:k]), refs[k:k + 2 * n]
        for i, it in enumerate(items):
            mine = [arr.pop(0) for _ in it]
            for m in range(1, N_DEV):
                copy_of(m, mine, sems[i].at[m - 1], sems[n + i].at[m - 1]).start()
        token = refs[-1]
        token[...] = jnp.zeros_like(token)

    res = pl.pallas_call(
        body, name=name,
        out_shape=[PEER_SEMS] * (2 * n) + [pltpu.HBM(a.shape, a.dtype) for a in flat] + [SDS((8, LANES), F32)],
        in_specs=[HBM_SPEC] * k,
        out_specs=[SEM_SPEC] * (2 * n) + [HBM_SPEC] * k + [pl.BlockSpec(memory_space=pltpu.VMEM)],
        input_output_aliases={j: 2 * n + j for j in range(k)},
        compiler_params=pltpu.CompilerParams(has_side_effects=SIDE_EFFECT),
    )(*[_in_hbm(a) for a in flat])
    thru, out_items = list(res[2 * n:2 * n + k]), []
    for it in items:
        out_items.append(tuple(thru.pop(0) for _ in it))
    return dict(send=res[:n], recv=res[n:2 * n], items=out_items, token=res[-1])


def _split_copy_wait(name, started, which, copy_of, after):
    items = [started["items"][i] for i in which]
    n = len(items)
    flat = [a for it in items for a in it]
    k = len(flat)

    def body(*refs):
        arr, sems = list(refs[:k]), refs[k:k + 2 * n]
        for i, it in enumerate(items):
            mine = [arr.pop(0) for _ in it]
            for m in range(1, N_DEV):
                cp = copy_of(m, mine, sems[i].at[m - 1], sems[n + i].at[m - 1])
                cp.wait_send()
                cp.wait_recv()

    res = pl.pallas_call(
        body, name=name,
        out_shape=[pltpu.HBM(a.shape, a.dtype) for a in flat],
        in_specs=[HBM_SPEC] * k + [SEM_SPEC] * (2 * n) + [ANY_SPEC] * len(after),
        out_specs=[HBM_SPEC] * k,
        input_output_aliases={j: j for j in range(k)},
        compiler_params=pltpu.CompilerParams(has_side_effects=SIDE_EFFECT),
    )(*flat, *[started["send"][i] for i in which], *[started["recv"][i] for i in which], *[_in_hbm(a) for a in after])
    thru, out_items = list(res), []
    for it in items:
        out_items.append(tuple(thru.pop(0) for _ in it))
    return out_items


def _gather_copy(m, refs, send_sem, recv_sem):
    (land_ref,) = refs
    r = land_ref.shape[0] // N_DEV
    mine = land_ref.at[pl.ds(pl.multiple_of(_peer_index(0) * r, 8), r), :]
    return pltpu.make_async_remote_copy(src_ref=mine, dst_ref=mine, send_sem=send_sem, recv_sem=recv_sem,
                                        device_id=_peer(m), device_id_type=MESH)


def _scatter_copy(m, refs, send_sem, recv_sem):
    buf_ref, land_ref = refs
    r = buf_ref.shape[0] // N_DEV
    src = buf_ref.at[pl.ds(pl.multiple_of(_peer_index(m) * r, 8), r), :]
    return pltpu.make_async_remote_copy(src_ref=src, dst_ref=land_ref.at[m], send_sem=send_sem, recv_sem=recv_sem,
                                        device_id=_peer(m), device_id_type=MESH)


def _peer(m):
    x, y, c = _mesh_pos()
    bx, by, bc = (m >> 2) & 1, (m >> 1) & 1, m & 1
    return (x ^ bx if bx else x, y ^ by if by else y, c ^ bc if bc else c)


def _peer_index(m):
    x, y, c = _mesh_pos()
    return (4 * x + 2 * y + c) ^ m


SMALL_PLACE = {
    "ln1_gain": (ROW_LN1G, 0), "ln1_bias": (ROW_LN1B, 0), "ln2_gain": (ROW_LN2G, 0), "ln2_bias": (ROW_LN2B, 0),
    "ret_gn_gain": (ROW_GN, 0), "ret_decay_fwd": (ROW_MISC, MISC_DF), "ret_decay_bwd": (ROW_MISC, MISC_DB),
    "attn_sink": (ROW_MISC, MISC_SINK)}


def _small_all_reduce_adamw(part, w, m, v):
    d = part.shape[1]
    names = list(SMALL_PLACE)
    k = len(names)

    def body(*refs):
        part_ref = refs[0]
        w_refs, m_refs, v_refs = refs[1:1 + k], refs[1 + k:1 + 2 * k], refs[1 + 2 * k:1 + 3 * k]
        outs = refs[1 + 3 * k:2 + 7 * k]
        loss_ref, g_refs, dl_refs = outs[0], outs[1:1 + k], outs[1 + k:1 + 2 * k]
        nm_refs, nv_refs = outs[1 + 2 * k:1 + 3 * k], outs[1 + 3 * k:1 + 4 * k]
        buf, send_sems, recv_sems = refs[2 + 7 * k:]
        buf[0] = part_ref[...]
        cps = [pltpu.make_async_remote_copy(
            src_ref=part_ref, dst_ref=buf.at[j], send_sem=send_sems.at[j - 1], recv_sem=recv_sems.at[j - 1],
            device_id=_peer(j), device_id_type=MESH) for j in range(1, N_DEV)]
        for cp in cps:
            cp.start()
        for cp in cps:
            cp.wait_recv()
        for cp in cps:
            cp.wait_send()
        me = _peer_index(0)
        tot = buf[me]
        for dev in range(1, N_DEV):
            tot = tot + buf[dev ^ me]
        loss_ref[...] = (0.5 / d) * jnp.sum(tot[ROW_LOSS:ROW_LOSS + 1, :], axis=1, keepdims=True)
        for i, name in enumerate(names):
            row, lo = SMALL_PLACE[name]
            wv = w_refs[i][...]
            g = tot[row:row + 1, lo:lo + wv.shape[1]]
            if name.startswith("ret_decay"):
                p2 = jnp.exp2(wv)
                g = g * (-p2 * jnp.log(2.0) / (1.0 - p2))
            g_refs[i][...] = g
            _adamw_store(g, wv, m_refs[i][...], v_refs[i][...], dl_refs[i], nm_refs[i], nv_refs[i])

    vm = pl.BlockSpec(memory_space=pltpu.VMEM)
    shapes = [SDS(w[n].shape, F32) for n in names]
    res = pl.pallas_call(
        body, name="small_all_reduce_adamw",
        out_shape=[SDS((1, 1), F32)] + shapes * 4, in_specs=[vm] * (1 + 3 * k), out_specs=[vm] * (1 + 4 * k),
        scratch_shapes=[pltpu.VMEM((N_DEV,) + part.shape, F32), pltpu.SemaphoreType.DMA((7,)),
                        pltpu.SemaphoreType.DMA((7,))],
    )(part, *[w[n] for n in names], *[m[n] for n in names], *[v[n] for n in names])
    groups = [dict(zip(names, res[1 + j * k:1 + (j + 1) * k])) for j in range(4)]
    return (res[0], *groups)


def _adamw_store(g, w, m, v, dl_ref, nm_ref, nv_ref):
    m = ADAM_B1 * m + (1.0 - ADAM_B1) * g
    v = ADAM_B2 * v + (1.0 - ADAM_B2) * (g * g)
    m_hat = m / (1.0 - ADAM_B1 ** ADAM_STEP)
    v_hat = v / (1.0 - ADAM_B2 ** ADAM_STEP)
    dl_ref[...] = -ADAM_LR * (m_hat / (jnp.sqrt(v_hat) + ADAM_EPS) + ADAM_WD * w)
    nm_ref[...] = m
    nv_ref[...] = v


def _reduce_adamw(name, own, recv, w, m, v, transposed):
    def body(own_ref, recv_ref, w_ref, m_ref, v_ref, g_ref, dl_ref, nm_ref, nv_ref):
        g = own_ref[...]
        for k in range(1, N_DEV):
            g = g + recv_ref[k].astype(F32)
        if transposed:
            g = g.T
        g_ref[...] = g
        _adamw_store(g, w_ref[...], m_ref[...], v_ref[...], dl_ref, nm_ref, nv_ref)

    out = SDS(w.shape, F32)
    return pl.pallas_call(body, name="adamw_" + name, out_shape=[out] * 4, compiler_params=_params())(own, recv, w, m, v)


def _row_spec(tm, width):
    return pl.BlockSpec((tm, width), lambda i: (i, 0))


def _full_spec(shape):
    return pl.BlockSpec(shape, lambda i: (0,) * len(shape), pipeline_mode=pl.Buffered(1))


def _acc_spec(shape):
    return pl.BlockSpec(shape, lambda i: (0,) * len(shape))


def _sub_rows(tm):
    step = min(SUB_ROWS, tm)
    return [(lo, lo + step) for lo in range(0, tm, step)]


def _in_proj(x2, wt_in):
    t, d = x2.shape
    u_w = wt_in.shape[0]
    tm = _row_tile(t, MATMUL_ROWS)

    def body(x_ref, w_ref, u_ref, xb_ref):
        xb = x_ref[...].astype(BF16)
        xb_ref[...] = xb
        u_ref[...] = _dot_nt(xb, w_ref[...]).astype(BF16)

    return pl.pallas_call(
        body, name="in_proj", grid=(t // tm,),
        in_specs=[_row_spec(tm, d), _full_spec(wt_in.shape)],
        out_specs=[_row_spec(tm, u_w), _row_spec(tm, d)],
        out_shape=[SDS((t, u_w), BF16), SDS((t, d), BF16)],
        compiler_params=_params(("parallel",)),
    )(x2, wt_in)


def _out_proj_ln1(r, a, x2, w_out, g1, b1):
    t, d = x2.shape
    tm = _row_tile(t, MATMUL_ROWS)

    def body(r_ref, a_ref, x_ref, wo_ref, g_ref, b_ref, z_ref, hb_ref):
        for lo, hi in _sub_rows(tm):
            mix = (_dot_nn(r_ref[lo:hi, :], wo_ref[0:RET_W, :])
                   + _dot_nn(a_ref[lo:hi, :], wo_ref[RET_W:RET_W + ATT_W, :]))
            z = ALPHA * x_ref[lo:hi, :] + mix
            xhat, _ = _layer_norm_stats(z)
            z_ref[lo:hi, :] = z
            hb_ref[lo:hi, :] = (xhat * g_ref[...] + b_ref[...]).astype(BF16)

    return pl.pallas_call(
        body, name="out_proj_ln1", grid=(t // tm,),
        in_specs=[_row_spec(tm, RET_W), _row_spec(tm, ATT_W), _row_spec(tm, d), _full_spec(w_out.shape),
                  _full_spec(g1.shape), _full_spec(b1.shape)],
        out_specs=[_row_spec(tm, d), _row_spec(tm, d)],
        out_shape=[SDS((t, d), F32), SDS((t, d), BF16)],
        compiler_params=_params(("parallel",)),
    )(r, a, x2, w_out, g1, b1)


def _col_halves(f):
    n = f // LANES
    k = (n + 1) // 2 * LANES
    return [(0, k), (k, f)] if k < f else [(0, f)]


def _ffn_up(h1b, wt_gate, wt_up):
    t, d = h1b.shape
    f = wt_gate.shape[0]
    tm = _row_tile(t, EPILOGUE_ROWS)

    def body(h_ref, wg_ref, wu_ref, dg_ref, du_ref, act_ref):
        h = h_ref[...]
        for lo, hi in _col_halves(f):
            g = _dot_nt(h, wg_ref[lo:hi, :])
            u = _dot_nt(h, wu_ref[lo:hi, :])
            sg = _sigmoid(g)
            silu = g * sg
            dg_ref[:, lo:hi] = (u * (sg * (1.0 + g * (1.0 - sg)))).astype(BF16)
            du_ref[:, lo:hi] = silu.astype(BF16)
            act_ref[:, lo:hi] = (silu * u).astype(BF16)

    return pl.pallas_call(
        body, name="ffn_up", grid=(t // tm,),
        in_specs=[_row_spec(tm, d), _full_spec(wt_gate.shape), _full_spec(wt_up.shape)],
        out_specs=[_row_spec(tm, f)] * 3,
        out_shape=[SDS((t, f), BF16)] * 3,
        compiler_params=_params(("parallel",)),
    )(h1b, wt_gate, wt_up)


def _ffn_down_ln2_loss(act, h1b, p2, z1, target, w_down, w_pg, wt_pe, g1, b1, g2, b2):
    t, d = z1.shape
    f = act.shape[1]
    pdim = p2.shape[1]
    tm = _row_tile(t, EPILOGUE_ROWS)

    def body(act_ref, hb_ref, p_ref, z1_ref, tgt_ref, wd_ref, wpg_ref, wpe_ref, g1_ref, b1_ref, g2_ref, b2_ref,
             dz_ref, dzb_ref, ds_ref, dple_ref, acc_ref):
        @pl.when(pl.program_id(0) == 0)
        def _():
            acc_ref[...] = jnp.zeros_like(acc_ref)

        for lo, hi in _sub_rows(tm):
            xhat1, _ = _layer_norm_stats(z1_ref[lo:hi, :])
            h1 = xhat1 * g1_ref[...] + b1_ref[...]
            ffn = _dot_nn(act_ref[lo:hi, :], wd_ref[...])
            pg = _sigmoid(_dot_nn(hb_ref[lo:hi, :], wpg_ref[...]))
            ple = _dot_nt(p_ref[lo:hi, :].astype(BF16), wpe_ref[...])
            z2 = ALPHA * h1 + ffn + pg * ple
            xhat2, rstd2 = _layer_norm_stats(z2)
            err = xhat2 * g2_ref[...] + b2_ref[...] - tgt_ref[lo:hi, :]
            dy = err * (1.0 / d)
            dz = _layer_norm_bwd(dy * g2_ref[...], xhat2, rstd2)
            dz_ref[lo:hi, :] = dz
            dzb_ref[lo:hi, :] = dz.astype(BF16)
            ds_ref[lo:hi, :] = (dz * ple * pg * (1.0 - pg)).astype(BF16)
            dple_ref[lo:hi, :] = (dz * pg).astype(BF16)
            acc_ref[0:1, :] += jnp.sum(err * err, axis=0, keepdims=True)
            acc_ref[1:2, :] += jnp.sum(dy * xhat2, axis=0, keepdims=True)
            acc_ref[2:3, :] += jnp.sum(dy, axis=0, keepdims=True)

    vec = _full_spec(g1.shape)
    return pl.pallas_call(
        body, name="ffn_down_ln2_loss", grid=(t // tm,),
        in_specs=[_row_spec(tm, f), _row_spec(tm, d), _row_spec(tm, pdim), _row_spec(tm, d), _row_spec(tm, d),
                  _full_spec(w_down.shape), _full_spec(w_pg.shape), _full_spec(wt_pe.shape), vec, vec, vec, vec],
        out_specs=[_row_spec(tm, d)] * 4 + [_acc_spec((8, d))],
        out_shape=[SDS((t, d), F32), SDS((t, d), BF16), SDS((t, d), BF16), SDS((t, d), BF16), SDS((8, d), F32)],
        compiler_params=_params(("arbitrary",)),
    )(act, h1b, p2, z1, target, w_down, w_pg, wt_pe, g1, b1, g2, b2)


def _after(after, body):
    k = len(after)
    return (lambda *refs: body(*refs[k:])), [ANY_SPEC] * k


def _ffn_bwd_act(dzb, dact_dg, dact_du, w_down, after=()):
    t, d = dzb.shape
    f = dact_dg.shape[1]
    tm = _row_tile(t, EPILOGUE_ROWS)

    def body(dz_ref, fg_ref, fu_ref, wd_ref, dg_ref, du_ref):
        dz = dz_ref[...]
        for lo, hi in _col_halves(f):
            da = _dot_nt(dz, wd_ref[lo:hi, :])
            dg_ref[:, lo:hi] = (da * fg_ref[:, lo:hi].astype(F32)).astype(BF16)
            du_ref[:, lo:hi] = (da * fu_ref[:, lo:hi].astype(F32)).astype(BF16)

    body, lead = _after(after, body)
    return pl.pallas_call(
        body, name="ffn_bwd_act", grid=(t // tm,),
        in_specs=lead + [_row_spec(tm, d), _row_spec(tm, f), _row_spec(tm, f), _full_spec(w_down.shape)],
        out_specs=[_row_spec(tm, f)] * 2,
        out_shape=[SDS((t, f), BF16)] * 2,
        compiler_params=_params(("parallel",)),
    )(*after, dzb, dact_dg, dact_du, w_down)


def _dh1_ln1_bwd(dz2, dg, dup, dsb, z1, wt_gate, wt_up, w_pg, g1, after=()):
    t, d = dz2.shape
    f = dg.shape[1]
    tm = _row_tile(t, EPILOGUE_ROWS)

    def body(dz_ref, dg_ref, du_ref, ds_ref, z1_ref, wg_ref, wu_ref, wpg_ref, g1_ref, dz1_ref, dz1b_ref, acc_ref):
        @pl.when(pl.program_id(0) == 0)
        def _():
            acc_ref[...] = jnp.zeros_like(acc_ref)

        for lo, hi in _sub_rows(tm):
            dh = (ALPHA * dz_ref[lo:hi, :] + _dot_nn(dg_ref[lo:hi, :], wg_ref[...])
                  + _dot_nn(du_ref[lo:hi, :], wu_ref[...]) + _dot_nt(ds_ref[lo:hi, :], wpg_ref[...]))
            xhat, rstd = _layer_norm_stats(z1_ref[lo:hi, :])
            dz1 = _layer_norm_bwd(dh * g1_ref[...], xhat, rstd)
            dz1_ref[lo:hi, :] = dz1
            dz1b_ref[lo:hi, :] = dz1.astype(BF16)
            acc_ref[0:1, :] += jnp.sum(dh * xhat, axis=0, keepdims=True)
            acc_ref[1:2, :] += jnp.sum(dh, axis=0, keepdims=True)

    body, lead = _after(after, body)
    return pl.pallas_call(
        body, name="dh1_ln1_bwd", grid=(t // tm,),
        in_specs=lead + [_row_spec(tm, d), _row_spec(tm, f), _row_spec(tm, f), _row_spec(tm, d), _row_spec(tm, d),
                         _full_spec(wt_gate.shape), _full_spec(wt_up.shape), _full_spec(w_pg.shape),
                         _full_spec(g1.shape)],
        out_specs=[_row_spec(tm, d), _row_spec(tm, d), _acc_spec((8, d))],
        out_shape=[SDS((t, d), F32), SDS((t, d), BF16), SDS((8, d), F32)],
        compiler_params=_params(("arbitrary",)),
    )(*after, dz2, dg, dup, dsb, z1, wt_gate, wt_up, w_pg, g1)


def _out_proj_bwd(dz1b, w_out, after=()):
    t, d = dz1b.shape
    tm = _row_tile(t, MATMUL_ROWS)

    def body(dz_ref, wo_ref, dr_ref, da_ref):
        dz = dz_ref[...]
        dr_ref[...] = _dot_nt(dz, wo_ref[0:RET_W, :]).astype(BF16)
        da_ref[...] = _dot_nt(dz, wo_ref[RET_W:RET_W + ATT_W, :]).astype(BF16)

    body, lead = _after(after, body)
    return pl.pallas_call(
        body, name="out_proj_bwd", grid=(t // tm,),
        in_specs=lead + [_row_spec(tm, d), _full_spec(w_out.shape)],
        out_specs=[_row_spec(tm, RET_W), _row_spec(tm, ATT_W)],
        out_shape=[SDS((t, RET_W), BF16), SDS((t, ATT_W), BF16)],
        compiler_params=_params(("parallel",)),
    )(*after, dz1b, w_out)


def _in_proj_bwd(dz1, parts, wt_in, after=()):
    t, d = dz1.shape
    tm = _row_tile(t, MATMUL_ROWS)
    widths = [p.shape[1] for p in parts]

    def body(*refs):
        dz_ref, part_refs, w_ref, dx_ref = refs[0], refs[1:1 + len(parts)], refs[-2], refs[-1]
        acc = ALPHA * dz_ref[...]
        lo = 0
        for p_ref, w in zip(part_refs, widths):
            acc = acc + _dot_nn(p_ref[...], w_ref[lo:lo + w, :])
            lo += w
        dx_ref[...] = acc

    body, lead = _after(after, body)
    return pl.pallas_call(
        body, name="in_proj_bwd", grid=(t // tm,),
        in_specs=lead + [_row_spec(tm, d)] + [_row_spec(tm, w) for w in widths] + [_full_spec(wt_in.shape)],
        out_specs=_row_spec(tm, d), out_shape=SDS((t, d), F32),
        compiler_params=_params(("parallel",)),
    )(*after, dz1, *parts, wt_in)


def _weight_grad(name, me, parts, rhs):
    t, n = rhs.shape
    widths = [p.shape[1] for p in parts]
    rows = sum(widths)
    own_rows = rows // N_DEV
    tk = _row_tile(t, MATMUL_ROWS)
    step = 256

    def body(*refs):
        me_ref, part_refs, rhs_ref = refs[0], refs[1:1 + len(parts)], refs[1 + len(parts)]
        full_ref, own_ref, acc = refs[-3], refs[-2], refs[-1]
        i = pl.program_id(0)

        @pl.when(i == 0)
        def _():
            acc[...] = jnp.zeros_like(acc)

        b = rhs_ref[...].astype(BF16)
        lo = 0
        for p_ref, w in zip(part_refs, widths):
            for c0 in range(0, w, step):
                c1 = min(c0 + step, w)
                acc[lo + c0:lo + c1, :] += _dot_tn(p_ref[:, c0:c1].astype(BF16), b)
            lo += w

        @pl.when(i == pl.num_programs(0) - 1)
        def _():
            full_ref[...] = acc[...].astype(BF16)
            own_ref[...] = acc[pl.ds(pl.multiple_of(me_ref[0] * own_rows, 8), own_rows), :]

    return pl.pallas_call(
        body, name=name, grid=(t // tk,),
        in_specs=[_smem_spec()] + [_row_spec(tk, w) for w in widths] + [_row_spec(tk, n)],
        out_specs=[_full_spec((rows, n)), _full_spec((own_rows, n))],
        out_shape=[SDS((rows, n), BF16), SDS((own_rows, n), F32)],
        scratch_shapes=[pltpu.VMEM((rows, n), F32)],
        compiler_params=_params(("arbitrary",)),
    )(me, *parts, rhs)


def _log_decay(decay_f, decay_b):
    def body(f_ref, b_ref, lf_ref, lb_ref):
        lf_ref[...] = jnp.log1p(-jnp.exp2(f_ref[...]))
        lb_ref[...] = jnp.log1p(-jnp.exp2(b_ref[...]))

    return pl.pallas_call(body, name="log_decay", out_shape=[SDS(decay_f.shape, F32)] * 2)(decay_f, decay_b)


def _chunk(ref, n):
    return ref[pl.ds(pl.multiple_of(n * CHUNK, CHUNK), CHUNK), :]


def _group_sum(is_a, v):
    sa = jnp.sum(jnp.where(is_a, v, 0.0), axis=1, keepdims=True)
    sb = jnp.sum(jnp.where(is_a, 0.0, v), axis=1, keepdims=True)
    return jnp.where(is_a, sa, sb)


def _seq_spec(s, col_block):
    return pl.BlockSpec((s, LANES), lambda b, h: (b, col_block + h))


def _smem_spec():
    return pl.BlockSpec(memory_space=pltpu.SMEM)


RET_UNROLL = 4


def _chunk_loop(n_chunks, body, init):
    u = RET_UNROLL if n_chunks % RET_UNROLL == 0 else 1

    def trip(i, carry):
        for j in range(u):
            carry = body(i * u + j, carry)
        return carry

    return lax.fori_loop(0, n_chunks // u, trip, init)


def _stacked_tables(lgf_ref, lgb_ref, pair):
    lane = lax.broadcasted_iota(jnp.int32, (1, LANES), 1)
    is_a = lane < HEAD_DIM
    lgf = jnp.where(is_a, lgf_ref[2 * pair], lgf_ref[2 * pair + 1])
    lgb = jnp.where(is_a, lgb_ref[2 * pair], lgb_ref[2 * pair + 1])
    row = lax.broadcasted_iota(jnp.int32, (CHUNK, 1), 0).astype(F32)
    kdec_f, qdec_f = jnp.exp(lgf * (CHUNK - 1.0 - row)), jnp.exp(lgf * (row + 1.0))
    kdec_b, qdec_b = jnp.exp(lgb * row), jnp.exp(lgb * (CHUNK - row))
    tab = dict(
        is_a=is_a, row=row, lam_f=jnp.exp(lgf * CHUNK), lam_b=jnp.exp(lgb * CHUNK),
        kdec=jnp.concatenate([kdec_f, kdec_b], axis=1), qdec=jnp.concatenate([qdec_f, qdec_b], axis=1),
        qexp=jnp.concatenate([jnp.broadcast_to(row + 1.0, (CHUNK, LANES)),
                              jnp.broadcast_to(CHUNK - row, (CHUNK, LANES))], axis=1),
        kexp=jnp.concatenate([jnp.broadcast_to(CHUNK - 1.0 - row, (CHUNK, LANES)),
                              jnp.broadcast_to(row, (CHUNK, LANES))], axis=1),
    )
    r = lax.broadcasted_iota(jnp.int32, (2 * LANES, LANES), 0)
    c = lax.broadcasted_iota(jnp.int32, (2 * LANES, LANES), 1)
    tab["diag2"] = ((r & (LANES - 1)) < HEAD_DIM) == (c < HEAD_DIM)
    i2 = lax.broadcasted_iota(jnp.int32, (2 * CHUNK, CHUNK), 0)
    j = lax.broadcasted_iota(jnp.int32, (2 * CHUNK, CHUNK), 1)
    head_b = i2 >= CHUNK
    diff = ((i2 & (CHUNK - 1)) - j).astype(F32)
    up, dn = jnp.maximum(diff, 0.0), jnp.maximum(-diff, 0.0)
    lgf2 = jnp.where(head_b, lgf_ref[2 * pair + 1], lgf_ref[2 * pair])
    lgb2 = jnp.where(head_b, lgb_ref[2 * pair + 1], lgb_ref[2 * pair])
    ef = jnp.where(diff >= 0, jnp.exp(lgf2 * up), 0.0)
    eb = jnp.where(diff <= 0, jnp.exp(lgb2 * dn), 0.0)
    tab["d2"] = ef + eb
    tab["df2"] = ef * up
    tab["db2"] = eb * dn
    return tab


def _stack_pair(is_a, x):
    zero = jnp.zeros_like(x)
    return jnp.concatenate([jnp.where(is_a, x, zero), jnp.where(is_a, zero, x)], axis=0)


def _unstack_pair(is_a, x2):
    return jnp.where(is_a, x2[0:CHUNK, :], x2[CHUNK:2 * CHUNK, :])


def _both_ways(x, dec):
    return (jnp.concatenate([x, x], axis=1) * dec).astype(BF16)


def _scan_states(n_chunks, st, up_rows, up_lam, down_rows, down_lam):
    zero = jnp.zeros((LANES, LANES), F32)

    def up(n, r):
        new = st[n, up_rows, :]
        st[n, up_rows, :] = r
        return r * up_lam + new

    def down(s, r):
        n = n_chunks - 1 - s
        new = st[n, down_rows, :]
        st[n, down_rows, :] = r
        return r * down_lam + new

    lax.fori_loop(0, n_chunks, up, zero)
    lax.fori_loop(0, n_chunks, down, zero)


FWD_ROWS, BWD_ROWS = pl.ds(0, LANES), pl.ds(LANES, LANES)


def _retention_fwd(u, lgf, lgb, gn_gain, b_loc):
    t = u.shape[0]
    s = t // b_loc
    n_chunks = s // CHUNK
    pairs = RET_HEADS // 2

    def body(lgf_ref, lgb_ref, q_ref, k_ref, v_ref, g_ref, gain_ref, r_ref, y_ref, st):
        tab = _stacked_tables(lgf_ref, lgb_ref, pl.program_id(1))
        is_a = tab["is_a"]

        def kv_body(n, _):
            k8 = _chunk(k_ref, n).astype(F32) * Q_SCALE
            st[n] = jnp.where(tab["diag2"], _dot_tn(_both_ways(k8, tab["kdec"]), _chunk(v_ref, n)), 0.0)
            return 0

        _chunk_loop(n_chunks, kv_body, 0)
        _scan_states(n_chunks, st, FWD_ROWS, tab["lam_f"], BWD_ROWS, tab["lam_b"])

        def out_body(n, _):
            q = _chunk(q_ref, n)
            k8 = (_chunk(k_ref, n).astype(F32) * Q_SCALE).astype(BF16)
            v = _chunk(v_ref, n)
            p2 = (_dot_nt(_stack_pair(is_a, q), k8) * tab["d2"]).astype(BF16)
            y = _unstack_pair(is_a, _dot_nn(p2, v))
            y = y + _dot_nn(_both_ways(q.astype(F32), tab["qdec"]), st[n].astype(BF16))
            rows = pl.ds(pl.multiple_of(n * CHUNK, CHUNK), CHUNK)
            y_ref[rows, :] = y
            mu = _group_sum(is_a, y) * (1.0 / HEAD_DIM)
            dlt = y - mu
            var = _group_sum(is_a, dlt * dlt) * (1.0 / HEAD_DIM)
            xhat = dlt * lax.rsqrt(var + GN_EPS)
            gate = _chunk(g_ref, n).astype(F32)
            r_ref[rows, :] = (xhat * gain_ref[...] * gate * _sigmoid(gate)).astype(BF16)
            return 0

        _chunk_loop(n_chunks, out_body, 0)

    lane_blk = lambda c0: _seq_spec(s, c0 // LANES)
    return pl.pallas_call(
        body, name="retention_fwd", grid=(b_loc, pairs),
        in_specs=[_smem_spec(), _smem_spec(), lane_blk(C_RQ), lane_blk(C_RK), lane_blk(C_RV), lane_blk(C_RG),
                  pl.BlockSpec((1, LANES), lambda b, h: (0, h))],
        out_specs=[_seq_spec(s, 0), _seq_spec(s, 0)],
        out_shape=[SDS((t, RET_W), BF16), SDS((t, RET_W), F32)],
        scratch_shapes=[pltpu.VMEM((n_chunks, 2 * LANES, LANES), F32)],
        compiler_params=_params(("parallel", "parallel")),
    )(lgf, lgb, u, u, u, u, gn_gain)


ST_GAIN, ST_XF, ST_XB, ST_IFA, ST_IFB, ST_IBA, ST_IBB, ST_LF, ST_LB = 0, 1, 2, 3, 4, 5, 6, 8, 9
ST_ROWS = 16


def _retention_bwd(u, y_pre, dr, lgf, lgb, gn_gain, b_loc):
    t = u.shape[0]
    s = t // b_loc
    n_chunks = s // CHUNK
    pairs = RET_HEADS // 2

    def body(lgf_ref, lgb_ref, q_ref, k_ref, v_ref, g_ref, y_ref, dr_ref, gain_ref,
             dq_ref, dk_ref, dv_ref, dg_ref, st_ref, st, gr, dy_s):
        tab = _stacked_tables(lgf_ref, lgb_ref, pl.program_id(1))
        is_a, row = tab["is_a"], tab["row"]
        gain = gain_ref[...]

        def norm_body(n, dgain):
            rows = pl.ds(pl.multiple_of(n * CHUNK, CHUNK), CHUNK)
            y = y_ref[rows, :]
            mu = _group_sum(is_a, y) * (1.0 / HEAD_DIM)
            dlt = y - mu
            var = _group_sum(is_a, dlt * dlt) * (1.0 / HEAD_DIM)
            rstd = lax.rsqrt(var + GN_EPS)
            xhat = dlt * rstd
            gate = g_ref[rows, :].astype(F32)
            sg = _sigmoid(gate)
            silu = gate * sg
            d_out = dr_ref[rows, :].astype(F32)
            dg_ref[rows, :] = (d_out * xhat * gain * (sg * (1.0 + gate * (1.0 - sg)))).astype(BF16)
            dxh = d_out * gain * silu
            m1 = _group_sum(is_a, dxh) * (1.0 / HEAD_DIM)
            m2 = _group_sum(is_a, dxh * xhat) * (1.0 / HEAD_DIM)
            dy = (rstd * (dxh - m1 - xhat * m2)).astype(BF16)
            dy_s[rows, :] = dy
            k8 = k_ref[rows, :].astype(F32) * Q_SCALE
            st[n] = jnp.where(tab["diag2"], _dot_tn(_both_ways(k8, tab["kdec"]), v_ref[rows, :]), 0.0)
            qf = q_ref[rows, :].astype(F32)
            gr[n] = jnp.where(tab["diag2"], _dot_tn(_both_ways(qf, tab["qdec"]), dy), 0.0)
            return dgain + jnp.sum(d_out * xhat * silu, axis=0, keepdims=True)

        dgain = _chunk_loop(n_chunks, norm_body, jnp.zeros((1, LANES), F32))
        _scan_states(n_chunks, st, FWD_ROWS, tab["lam_f"], BWD_ROWS, tab["lam_b"])
        _scan_states(n_chunks, gr, BWD_ROWS, tab["lam_b"], FWD_ROWS, tab["lam_f"])
        colsum = lambda x: jnp.sum(x, axis=0, keepdims=True)

        def grad_body(n, carry):
            xfb, ifa, ifb, iba, ibb, lf, lb = carry
            rows = pl.ds(pl.multiple_of(n * CHUNK, CHUNK), CHUNK)
            q = q_ref[rows, :]
            qf = q.astype(F32)
            k8f = k_ref[rows, :].astype(F32) * Q_SCALE
            k8 = k8f.astype(BF16)
            v = v_ref[rows, :]
            dy = dy_s[rows, :]
            q2, dy2 = _stack_pair(is_a, q), _stack_pair(is_a, dy)
            sc = _dot_nt(q2, k8)
            dp = _dot_nt(dy2, v)
            a2 = (sc * tab["d2"]).astype(BF16)
            ds2 = (dp * tab["d2"]).astype(BF16)
            dq = _unstack_pair(is_a, _dot_nn(ds2, k8))
            dk = _dot_tn(ds2, q2)
            dv = _dot_tn(a2, dy2)
            prod = sc * dp
            pf, pb = prod * tab["df2"], prod * tab["db2"]
            ifa, ifb = ifa + colsum(pf[0:CHUNK, :]), ifb + colsum(pf[CHUNK:2 * CHUNK, :])
            iba, ibb = iba + colsum(pb[0:CHUNK, :]), ibb + colsum(pb[CHUNK:2 * CHUNK, :])
            states, sgrads = st[n], gr[n]
            sb, gb = states.astype(BF16), sgrads.astype(BF16)
            dqc = _dot_nt(dy, sb) * tab["qdec"]
            dkc = _dot_nt(v, gb) * tab["kdec"]
            dv = dv + _dot_nn(_both_ways(k8f, tab["kdec"]), gb)
            dq_ref[rows, :] = (dq + dqc[:, 0:LANES] + dqc[:, LANES:2 * LANES]).astype(BF16)
            dk_ref[rows, :] = ((dk + dkc[:, 0:LANES] + dkc[:, LANES:2 * LANES]) * Q_SCALE).astype(BF16)
            dv_ref[rows, :] = dv.astype(BF16)
            q2w, k2w = jnp.concatenate([qf, qf], axis=1), jnp.concatenate([k8f, k8f], axis=1)
            xfb = xfb + colsum(tab["qexp"] * q2w * dqc + tab["kexp"] * k2w * dkc)
            prod_s = sgrads * states
            lf, lb = lf + colsum(prod_s[0:LANES, :]), lb + colsum(prod_s[LANES:2 * LANES, :])
            return xfb, ifa, ifb, iba, ibb, lf, lb

        z = jnp.zeros((1, LANES), F32)
        init = (jnp.zeros((1, 2 * LANES), F32), z, z, z, z, z, z)
        xfb, ifa, ifb, iba, ibb, lf, lb = _chunk_loop(n_chunks, grad_body, init)
        st_ref[...] = jnp.zeros_like(st_ref)
        st_ref[ST_GAIN:ST_GAIN + 1, :] = dgain
        st_ref[ST_XF:ST_XF + 1, :] = xfb[:, 0:LANES]
        st_ref[ST_XB:ST_XB + 1, :] = xfb[:, LANES:2 * LANES]
        st_ref[ST_IFA:ST_IFA + 1, :] = ifa
        st_ref[ST_IFB:ST_IFB + 1, :] = ifb
        st_ref[ST_IBA:ST_IBA + 1, :] = iba
        st_ref[ST_IBB:ST_IBB + 1, :] = ibb
        st_ref[ST_LF:ST_LF + 1, :] = lf * (CHUNK * tab["lam_f"])
        st_ref[ST_LB:ST_LB + 1, :] = lb * (CHUNK * tab["lam_b"])

    lane_blk = lambda c0: _seq_spec(s, c0 // LANES)
    seq0 = _seq_spec(s, 0)
    state = pltpu.VMEM((n_chunks, 2 * LANES, LANES), F32)
    return pl.pallas_call(
        body, name="retention_bwd", grid=(b_loc, pairs),
        in_specs=[_smem_spec(), _smem_spec(), lane_blk(C_RQ), lane_blk(C_RK), lane_blk(C_RV), lane_blk(C_RG),
                  seq0, seq0, pl.BlockSpec((1, LANES), lambda b, h: (0, h))],
        out_specs=[seq0] * 4 + [pl.BlockSpec((ST_ROWS, LANES), lambda b, h: (b, h))],
        out_shape=[SDS((t, RET_W), BF16)] * 4 + [SDS((b_loc * ST_ROWS, RET_W), F32)],
        scratch_shapes=[state, state, pltpu.VMEM((s, LANES), BF16)],
        compiler_params=_params(("parallel", "parallel")),
    )(lgf, lgb, u, u, u, u, y_pre, dr, gn_gain)


GW = GROUP * HEAD_DIM
KEYS = 3 * BLOCK


def _attn_tables(g, bias_ref):
    r = lax.broadcasted_iota(jnp.int32, (GROUP * BLOCK, KEYS), 0)
    kj = lax.broadcasted_iota(jnp.int32, (GROUP * BLOCK, KEYS), 1)
    qi = r & (BLOCK - 1)
    hh = lax.shift_right_logical(r, 7)
    dist = jnp.abs(kj - BLOCK - qi)
    slope = jnp.exp2(-(GROUP * g + hh + 1).astype(F32) * (8.0 / ATTN_HEADS))
    bias_ref[...] = jnp.where(dist <= BLOCK, -slope * dist.astype(F32), NEG_INF)


def _tile_keys(x_ref, g, scale, pad_ref, s):
    r = lax.broadcasted_iota(jnp.int32, (LANES, GW), 0)
    c = lax.broadcasted_iota(jnp.int32, (LANES, GW), 1)
    place = jnp.where(r == g * HEAD_DIM + (c & (HEAD_DIM - 1)), 1.0, 0.0).astype(BF16)
    pad_ref[0:BLOCK, :] = jnp.zeros((BLOCK, GW), BF16)
    pad_ref[BLOCK + s:2 * BLOCK + s, :] = jnp.zeros((BLOCK, GW), BF16)
    pad_ref[BLOCK:BLOCK + s, :] = (_dot_nn(x_ref[...], place) * scale).astype(BF16)


def _stack_heads(x):
    lane_h = lax.shift_right_logical(lax.broadcasted_iota(jnp.int32, (1, GW), 1), 6)
    zero = jnp.zeros_like(x)
    return jnp.concatenate([jnp.where(lane_h == h, x, zero) for h in range(GROUP)], axis=0)


def _unstack_heads(x4):
    lane_h = lax.shift_right_logical(lax.broadcasted_iota(jnp.int32, (1, GW), 1), 6)
    out = jnp.zeros((BLOCK, GW), F32)
    for h in range(GROUP):
        out = out + jnp.where(lane_h == h, x4[h * BLOCK:(h + 1) * BLOCK, :], 0.0)
    return out


def _sink_column(sink_ref, g):
    rh = lax.shift_right_logical(lax.broadcasted_iota(jnp.int32, (GROUP * BLOCK, 1), 0), 7)
    col = jnp.zeros((GROUP * BLOCK, 1), F32)
    for h in range(GROUP):
        col = jnp.where(rh == h, sink_ref[GROUP * g + h], col)
    return col


def _attn_probs(qm, k3, bias_ref, sink_col, n, s):
    logits = _dot_nt(qm, k3) + bias_ref[...]
    kpos = n * BLOCK - BLOCK + lax.broadcasted_iota(jnp.int32, (1, KEYS), 1)
    logits = jnp.where((kpos >= 0) & (kpos < s), logits, NEG_INF)
    m = jnp.maximum(jnp.max(logits, axis=1, keepdims=True), sink_col)
    e = jnp.exp(logits - m)
    e_sink = jnp.exp(sink_col - m)
    inv = 1.0 / (jnp.sum(e, axis=1, keepdims=True) + e_sink)
    return e * inv, e_sink * inv


ATT_SUB = 2


def _attn_specs(s, n_steps):
    rows = ATT_SUB * BLOCK
    q_spec = pl.BlockSpec((rows, GW), lambda b, g, n: (b * n_steps + n, C_AQ // GW + g))
    k_spec = pl.BlockSpec((s, LANES), lambda b, g, n: (b, C_AK // LANES))
    v_spec = pl.BlockSpec((s, LANES), lambda b, g, n: (b, C_AV // LANES))
    o_spec = pl.BlockSpec((rows, GW), lambda b, g, n: (b * n_steps + n, g))
    return q_spec, k_spec, v_spec, o_spec


def _attention_fwd(u, sink, b_loc):
    t = u.shape[0]
    s = t // b_loc
    n_steps = s // (ATT_SUB * BLOCK)

    def body(sink_ref, q_ref, k_ref, v_ref, o_ref, kpad, vpad, bias):
        g, step = pl.program_id(1), pl.program_id(2)

        @pl.when(step == 0)
        def _():
            _attn_tables(g, bias)
            _tile_keys(k_ref, g, Q_SCALE, kpad, s)
            _tile_keys(v_ref, g, 1.0, vpad, s)

        sink_col = _sink_column(sink_ref, g)
        for j in range(ATT_SUB):
            n = step * ATT_SUB + j
            rows = pl.ds(j * BLOCK, BLOCK)
            keys = pl.ds(pl.multiple_of(n * BLOCK, BLOCK), KEYS)
            p, _ = _attn_probs(_stack_heads(q_ref[rows, :]), kpad[keys, :], bias, sink_col, n, s)
            o_ref[rows, :] = _unstack_heads(_dot_nn(p.astype(BF16), vpad[keys, :])).astype(BF16)

    q_spec, k_spec, v_spec, o_spec = _attn_specs(s, n_steps)
    pad = pltpu.VMEM((s + 2 * BLOCK, GW), BF16)
    return pl.pallas_call(
        body, name="attention_fwd", grid=(b_loc, KV_HEADS, n_steps),
        in_specs=[_smem_spec(), q_spec, k_spec, v_spec], out_specs=o_spec,
        out_shape=SDS((t, ATT_W), BF16),
        scratch_shapes=[pad, pad, pltpu.VMEM((GROUP * BLOCK, KEYS), F32)],
        compiler_params=_params(("parallel", "arbitrary", "arbitrary")),
    )(sink, u, u, u)


def _fold_groups(x, g):
    x = x + pltpu.roll(x, 2 * HEAD_DIM, 1)
    x = x + pltpu.roll(x, HEAD_DIM, 1)
    lane_g = lax.shift_right_logical(lax.broadcasted_iota(jnp.int32, (1, LANES), 1), 6)
    return jnp.where(lane_g == g, x[:, 0:LANES], 0.0)


def _attention_bwd(u, da, sink, b_loc):
    t = u.shape[0]
    s = t // b_loc
    n_steps = s // (ATT_SUB * BLOCK)

    def body(sink_ref, q_ref, k_ref, v_ref, do_ref, dq_ref, dk_ref, dv_ref, dsink_ref, kpad, vpad, bias, dk_acc, dv_acc):
        g, step = pl.program_id(1), pl.program_id(2)

        @pl.when(step == 0)
        def _():
            _attn_tables(g, bias)
            _tile_keys(k_ref, g, Q_SCALE, kpad, s)
            _tile_keys(v_ref, g, 1.0, vpad, s)
            dsink_ref[...] = jnp.zeros_like(dsink_ref)

        @pl.when((step == 0) & (g == 0))
        def _():
            dk_acc[...] = jnp.zeros_like(dk_acc)
            dv_acc[...] = jnp.zeros_like(dv_acc)

        sink_col = _sink_column(sink_ref, g)
        head_row = lax.broadcasted_iota(jnp.int32, dsink_ref.shape, 0)
        upd = jnp.zeros(dsink_ref.shape, F32)
        for j in range(ATT_SUB):
            n = step * ATT_SUB + j
            rows = pl.ds(j * BLOCK, BLOCK)
            keys = pl.ds(pl.multiple_of(n * BLOCK, BLOCK), KEYS)
            qm = _stack_heads(q_ref[rows, :])
            k3, v3 = kpad[keys, :], vpad[keys, :]
            p, p_sink = _attn_probs(qm, k3, bias, sink_col, n, s)
            dom = _stack_heads(do_ref[rows, :])
            dp = _dot_nt(dom, v3)
            delta = jnp.sum(p * dp, axis=1, keepdims=True)
            ds_mat = (p * (dp - delta)).astype(BF16)
            dq_ref[rows, :] = _unstack_heads(_dot_nn(ds_mat, k3)).astype(BF16)
            dk_acc[keys, :] += _fold_groups(_dot_tn(ds_mat, qm), g) * Q_SCALE
            dv_acc[keys, :] += _fold_groups(_dot_tn(p.astype(BF16), dom), g)
            w = p_sink * delta
            for h in range(GROUP):
                upd = upd + jnp.where(head_row == h, -jnp.sum(w[h * BLOCK:(h + 1) * BLOCK, :]), 0.0)
        dsink_ref[...] += upd

        @pl.when((step == n_steps - 1) & (g == KV_HEADS - 1))
        def _():
            dk_ref[...] = dk_acc[BLOCK:BLOCK + s, :].astype(BF16)
            dv_ref[...] = dv_acc[BLOCK:BLOCK + s, :].astype(BF16)

    q_spec, k_spec, v_spec, o_spec = _attn_specs(s, n_steps)
    kv_out = pl.BlockSpec((s, LANES), lambda b, g, n: (b, 0))
    pad = pltpu.VMEM((s + 2 * BLOCK, GW), BF16)
    acc = pltpu.VMEM((s + 2 * BLOCK, LANES), F32)
    return pl.pallas_call(
        body, name="attention_bwd", grid=(b_loc, KV_HEADS, n_steps),
        in_specs=[_smem_spec(), q_spec, k_spec, v_spec, o_spec],
        out_specs=[o_spec, kv_out, kv_out, pl.BlockSpec((8, LANES), lambda b, g, n: (b * KV_HEADS + g, 0))],
        out_shape=[SDS((t, ATT_W), BF16), SDS((t, KV_W), BF16), SDS((t, KV_W), BF16),
                   SDS((b_loc * KV_HEADS * 8, LANES), F32)],
        scratch_shapes=[pad, pad, pltpu.VMEM((GROUP * BLOCK, KEYS), F32), acc, acc],
        compiler_params=_params(("arbitrary", "arbitrary", "arbitrary")),
    )(sink, u, u, u, da)


def _pack_small(acc2, acc1, ret_stats, dsink, b_loc, d):
    pairs = RET_HEADS // 2

    def body(acc2_ref, acc1_ref, st_ref, dsink_ref, out_ref):
        out_ref[...] = jnp.zeros_like(out_ref)
        out_ref[ROW_LN1G:ROW_LN1G + 1, :] = acc1_ref[0:1, :]
        out_ref[ROW_LN1B:ROW_LN1B + 1, :] = acc1_ref[1:2, :]
        out_ref[ROW_LN2G:ROW_LN2G + 1, :] = acc2_ref[1:2, :]
        out_ref[ROW_LN2B:ROW_LN2B + 1, :] = acc2_ref[2:3, :]
        out_ref[ROW_LOSS:ROW_LOSS + 1, :] = acc2_ref[0:1, :]
        st = st_ref[0:ST_ROWS, :]
        for b in range(1, b_loc):
            st = st + st_ref[b * ST_ROWS:(b + 1) * ST_ROWS, :]
        out_ref[ROW_GN:ROW_GN + 1, 0:RET_W] = st[ST_GAIN:ST_GAIN + 1, :]
        lane = lax.broadcasted_iota(jnp.int32, (1, d), 1)
        misc = jnp.zeros((1, d), F32)
        for pr in range(pairs):
            blk = st[:, pr * LANES:(pr + 1) * LANES]
            half = lax.broadcasted_iota(jnp.int32, (1, LANES), 1) < HEAD_DIM
            for h in range(2):
                sel = half if h == 0 else jnp.logical_not(half)
                cross_f = jnp.sum(jnp.where(sel, blk[ST_XF:ST_XF + 1, :] + blk[ST_LF:ST_LF + 1, :], 0.0))
                cross_b = jnp.sum(jnp.where(sel, blk[ST_XB:ST_XB + 1, :] + blk[ST_LB:ST_LB + 1, :], 0.0))
                intra_f = jnp.sum(blk[ST_IFA + h:ST_IFA + h + 1, :])
                intra_b = jnp.sum(blk[ST_IBA + h:ST_IBA + h + 1, :])
                head = 2 * pr + h
                misc = jnp.where(lane == MISC_DF + head, cross_f + intra_f, misc)
                misc = jnp.where(lane == MISC_DB + head, cross_b + intra_b, misc)
        for g in range(KV_HEADS):
            tot = dsink_ref[g * 8:(g + 1) * 8, :]
            for b in range(1, b_loc):
                tot = tot + dsink_ref[(b * KV_HEADS + g) * 8:(b * KV_HEADS + g + 1) * 8, :]
            for h in range(GROUP):
                misc = jnp.where(lane == MISC_SINK + GROUP * g + h, jnp.sum(tot[h:h + 1, 0:1]), misc)
        out_ref[ROW_MISC:ROW_MISC + 1, :] = misc

    return pl.pallas_call(body, name="pack_small", out_shape=SDS((SMALL_ROWS, d), F32))(acc2, acc1, ret_stats, dsink)


BIG = ("w_in", "w_out", "w_ffn_gate", "w_ffn_up", "w_ffn_down", "w_ple_proj", "w_ple_gate")
TRANSPOSED_OUTSIDE = ("w_in", "w_ffn_gate", "w_ffn_up")
TRANSPOSED_HERE = ("w_ple_proj",)
SMALL = ("ret_decay_fwd", "ret_decay_bwd", "ret_gn_gain", "attn_sink", "ln1_gain", "ln1_bias", "ln2_gain", "ln2_bias")
ORDER = ("w_in", "ret_decay_fwd", "ret_decay_bwd", "ret_gn_gain", "attn_sink", "w_out", "ln1_gain", "ln1_bias",
         "w_ffn_gate", "w_ffn_up", "w_ffn_down", "w_ple_proj", "w_ple_gate", "ln2_gain", "ln2_bias")


GATHER_ORDER = ("w_in", "w_out", "w_ffn_gate", "w_ffn_up", "w_ple_gate", "w_ple_proj", "w_ffn_down")


def _local_step(x2, p2, target2, fetch, publish, small, b_loc, me):
    d = x2.shape[1]
    lgf, lgb = _log_decay(small["ret_decay_fwd"], small["ret_decay_bwd"])
    lgf1, lgb1, sink1 = lgf.reshape(-1), lgb.reshape(-1), small["attn_sink"].reshape(-1)
    (w_in,) = fetch(("w_in",), ())
    u, xb = _in_proj(x2, w_in)
    r, y_pre = _retention_fwd(u, lgf1, lgb1, small["ret_gn_gain"], b_loc)
    a = _attention_fwd(u, sink1, b_loc)
    w_out, w_gate, w_up = fetch(("w_out", "w_ffn_gate", "w_ffn_up"), (r, a))
    z1, h1b = _out_proj_ln1(r, a, x2, w_out, small["ln1_gain"], small["ln1_bias"])
    dact_dg, dact_du, act = _ffn_up(h1b, w_gate, w_up)
    w_pg, w_pe, w_down = fetch(("w_ple_gate", "w_ple_proj", "w_ffn_down"), (act,))
    dz2, dz2b, dsb, dpleb, acc2 = _ffn_down_ln2_loss(
        act, h1b, p2, z1, target2, w_down, w_pg, w_pe,
        small["ln1_gain"], small["ln1_bias"], small["ln2_gain"], small["ln2_bias"])
    own = {}

    def grad(name, parts, rhs):
        whole, own[name] = _weight_grad("grad_" + name, me, parts, rhs)
        return whole

    t1 = publish("ffn_down", dict(w_ffn_down=grad("w_ffn_down", [act], dz2b),
                                  w_ple_proj=grad("w_ple_proj", [dpleb], p2),
                                  w_ple_gate=grad("w_ple_gate", [h1b], dsb)))
    dg, dup = _ffn_bwd_act(dz2b, dact_dg, dact_du, w_down, t1)
    t2 = publish("ffn_up", dict(w_ffn_gate=grad("w_ffn_gate", [dg], h1b), w_ffn_up=grad("w_ffn_up", [dup], h1b)))
    dz1, dz1b, acc1 = _dh1_ln1_bwd(dz2, dg, dup, dsb, z1, w_gate, w_up, w_pg, small["ln1_gain"], t2)
    t3 = publish("out", dict(w_out=grad("w_out", [r, a], dz1b)))
    dr, da = _out_proj_bwd(dz1b, w_out, t3)
    dq, dk, dv, dgate, ret_stats = _retention_bwd(u, y_pre, dr, lgf1, lgb1, small["ret_gn_gain"], b_loc)
    daq, dak, dav, dsink = _attention_bwd(u, da, sink1, b_loc)
    parts = [dq, dk, dv, dgate, daq, dak, dav]
    t4 = publish("in", dict(w_in=grad("w_in", parts, xb)))
    grad_x = _in_proj_bwd(dz1, parts, w_in, t4)
    small_part = _pack_small(acc2, acc1, ret_stats, dsink, b_loc, d)
    return grad_x, own, small_part


def kernel(x, p, w_in, ret_decay_fwd, ret_decay_bwd, ret_gn_gain, attn_sink, w_out, ln1_gain, ln1_bias, w_ffn_gate, w_ffn_up, w_ffn_down, w_ple_proj, w_ple_gate, ln2_gain, ln2_bias, loss_target, m_w_in, m_ret_decay_fwd, m_ret_decay_bwd, m_ret_gn_gain, m_attn_sink, m_w_out, m_ln1_gain, m_ln1_bias, m_w_ffn_gate, m_w_ffn_up, m_w_ffn_down, m_w_ple_proj, m_w_ple_gate, m_ln2_gain, m_ln2_bias, v_w_in, v_ret_decay_fwd, v_ret_decay_bwd, v_ret_gn_gain, v_attn_sink, v_w_out, v_ln1_gain, v_ln1_bias, v_w_ffn_gate, v_w_ffn_up, v_w_ffn_down, v_w_ple_proj, v_w_ple_gate, v_ln2_gain, v_ln2_bias):
    given = dict(locals())

    def strip(n, a):
        if n not in BIG:
            return a
        return a[0].T if n in TRANSPOSED_OUTSIDE else a[0]

    def restore(n, a):
        if n not in BIG:
            return a
        return (a.T if n in TRANSPOSED_OUTSIDE else a)[None]

    w = {n: strip(n, given[n]) for n in ORDER}
    m = {n: strip(n, given["m_" + n]) for n in ORDER}
    v = {n: strip(n, given["v_" + n]) for n in ORDER}
    b_loc, s, d = x.shape
    x2 = x.reshape(b_loc * s, d)
    p2 = p[0].reshape(b_loc * s, p.shape[-1])
    target2 = loss_target.reshape(b_loc * s, d)

    small = {n: w[n] for n in SMALL}
    me = (4 * lax.axis_index("x") + 2 * lax.axis_index("y") + lax.axis_index("c")).astype(jnp.int32).reshape(1)

    gathered = _prep_shards(me, {n: w[n] for n in BIG})
    gather = _split_copy_start("gather_start", [(gathered[n],) for n in GATHER_ORDER], _gather_copy)

    def fetch(names, after):
        which = [GATHER_ORDER.index(n) for n in names]
        got = _split_copy_wait("gather_wait_" + names[0], gather, which, _gather_copy, list(after))
        return [item[0] for item in got]

    scatters = []

    def publish(tag, products):
        names = list(products)
        items = [(products[n], lax.empty((N_DEV, products[n].shape[0] // N_DEV, products[n].shape[1]), BF16))
                 for n in names]
        started = _split_copy_start("scatter_start_" + tag, items, _scatter_copy)
        scatters.append((tag, names, started))
        return (started["token"],)

    grad_x, own, small_part = _local_step(x2, p2, target2, fetch, publish, small, b_loc, me)

    out_g, out_d, out_m, out_v = {}, {}, {}, {}
    after = [grad_x]
    for tag, names, started in scatters:
        landed = _split_copy_wait("scatter_wait_" + tag, started, list(range(len(names))), _scatter_copy, after)
        for n, (_, recv) in zip(names, landed):
            out_g[n], out_d[n], out_m[n], out_v[n] = _reduce_adamw(
                n, own[n], recv, w[n], m[n], v[n], n in TRANSPOSED_HERE)
        after = [out_v[names[-1]]]
        if tag == "out":
            loss, sg, sd, sm, sv = _small_all_reduce_adamw(
                small_part, small, {n: m[n] for n in SMALL}, {n: v[n] for n in SMALL})
            for dst, src in ((out_g, sg), (out_d, sd), (out_m, sm), (out_v, sv)):
                dst.update(src)
            after.append(sv["ln2_bias"])

    outs = [loss[0, 0], grad_x.reshape(x.shape)]
    for group in (out_g, out_d, out_m, out_v):
        outs += [restore(n, group[n]) for n in ORDER]
    return tuple(outs)
```

```python
import functools

import jax
import jax.numpy as jnp
from jax import lax
from jax.experimental import pallas as pl
from jax.experimental.pallas import tpu as pltpu

F32, BF16 = jnp.float32, jnp.bfloat16
SDS = jax.ShapeDtypeStruct
MESH = pl.DeviceIdType.MESH

N_DEV = 8
HEAD_DIM = 64
RET_HEADS = 8
ATTN_HEADS = 8
KV_HEADS = 2
GROUP = ATTN_HEADS // KV_HEADS
RET_W = RET_HEADS * HEAD_DIM
ATT_W = ATTN_HEADS * HEAD_DIM
KV_W = KV_HEADS * HEAD_DIM
LANES = 128
CHUNK = 128
BLOCK = 128
Q_SCALE = HEAD_DIM ** -0.5
ALPHA = 2.0 ** 0.25
LN_EPS = 1e-5
GN_EPS = 1e-5
NEG_INF = -1e30
C_RQ, C_RK, C_RV, C_RG = 0, RET_W, 2 * RET_W, 3 * RET_W
C_AQ = 4 * RET_W
C_AK = C_AQ + ATT_W
C_AV = C_AK + KV_W
IN_W = C_AV + KV_W

ADAM_LR = 0.001
ADAM_B1 = 0.9
ADAM_B2 = 0.999
ADAM_EPS = 1e-08
ADAM_WD = 0.01
ADAM_STEP = 10

VMEM_LIMIT = 48 * 1024 * 1024
MATMUL_ROWS = 512
EPILOGUE_ROWS = 256
SUB_ROWS = 256
SMALL_ROWS = 16
ROW_LN1G, ROW_LN1B, ROW_LN2G, ROW_LN2B, ROW_LOSS, ROW_GN, ROW_MISC = 0, 1, 2, 3, 4, 5, 6
MISC_DF, MISC_DB, MISC_SINK = 0, 8, 16


def _dot_nn(a, b):
    return lax.dot_general(a, b, (((1,), (0,)), ((), ())), preferred_element_type=F32)


def _dot_nt(a, b):
    return lax.dot_general(a, b, (((1,), (1,)), ((), ())), preferred_element_type=F32)


def _dot_tn(a, b):
    return lax.dot_general(a, b, (((0,), (0,)), ((), ())), preferred_element_type=F32)


def _params(sem=None, vmem=VMEM_LIMIT):
    kw = {"vmem_limit_bytes": vmem}
    if sem is not None:
        kw["dimension_semantics"] = sem
    return pltpu.CompilerParams(**kw)


def _row_tile(t, want=512):
    tm = want
    while t % tm:
        tm //= 2
    return tm


def _sigmoid(x):
    return 1.0 / (1.0 + jnp.exp(-x))


def _layer_norm_stats(z):
    mu = jnp.mean(z, axis=1, keepdims=True)
    d = z - mu
    var = jnp.mean(d * d, axis=1, keepdims=True)
    rstd = lax.rsqrt(var + LN_EPS)
    return d * rstd, rstd


def _layer_norm_bwd(dxh, xhat, rstd):
    m1 = jnp.mean(dxh, axis=1, keepdims=True)
    m2 = jnp.mean(dxh * xhat, axis=1, keepdims=True)
    return rstd * (dxh - m1 - xhat * m2)


def _prep_shards(me, shards):
    names = list(shards)

    def body(me_ref, *refs):
        for name, src, dst in zip(names, refs[:len(names)], refs[len(names):]):
            val = src[...]
            dst[...] = (val.T if name in TRANSPOSED_HERE else val).astype(BF16)

    shape = lambda n, a: a.shape[::-1] if n in TRANSPOSED_HERE else a.shape
    shapes = [shape(n, shards[n]) for n in names]
    out = pl.pallas_call(
        body, name="prep_shards",
        grid_spec=pltpu.PrefetchScalarGridSpec(
            num_scalar_prefetch=1, grid=(1,),
            in_specs=[pl.BlockSpec(shards[n].shape, lambda i, me_ref: (0, 0)) for n in names],
            out_specs=[pl.BlockSpec(s, lambda i, me_ref: (me_ref[0], 0)) for s in shapes]),
        out_shape=[SDS((N_DEV * s[0], s[1]), BF16) for s in shapes], compiler_params=_params(("arbitrary",)),
    )(me, *[shards[n] for n in names])
    return dict(zip(names, out))


def _mesh_pos():
    return lax.axis_index("x"), lax.axis_index("y"), lax.axis_index("c")


HBM_SPEC = pl.BlockSpec(memory_space=pltpu.HBM)
SEM_SPEC = pl.BlockSpec(memory_space=pltpu.SEMAPHORE)
ANY_SPEC = pl.BlockSpec(memory_space=pl.ANY)
SIDE_EFFECT = pltpu.SideEffectType.DATAFLOW_SIDE_EFFECTING
PEER_SEMS = pltpu.SemaphoreType.DMA((N_DEV - 1,))


def _in_hbm(a):
    return pltpu.with_memory_space_constraint(a, pltpu.HBM)


def _split_copy_start(name, items, copy_of):
    n = len(items)
    flat = [a for it in items for a in it]
    k = len(flat)

    def body(*refs):
        arr, sems = list(refs[:k]), refs[k:k + 2 * n]
        for i, it in enumerate(items):
            mine = [arr.pop(0) for _ in it]
            for m in range(1, N_DEV):
                copy_of(m, mine, sems[i].at[m - 1], sems[n + i].at[m - 1]).start()
        token = refs[-1]
        token[...] = jnp.zeros_like(token)

    res = pl.pallas_call(
        body, name=name,
        out_shape=[PEER_SEMS] * (2 * n) + [pltpu.HBM(a.shape, a.dtype) for a in flat] + [SDS((8, LANES), F32)],
        in_specs=[HBM_SPEC] * k,
        out_specs=[SEM_SPEC] * (2 * n) + [HBM_SPEC] * k + [pl.BlockSpec(memory_space=pltpu.VMEM)],
        input_output_aliases={j: 2 * n + j for j in range(k)},
        compiler_params=pltpu.CompilerParams(has_side_effects=SIDE_EFFECT),
    )(*[_in_hbm(a) for a in flat])
    thru, out_items = list(res[2 * n:2 * n + k]), []
    for it in items:
        out_items.append(tuple(thru.pop(0) for _ in it))
    return dict(send=res[:n], recv=res[n:2 * n], items=out_items, token=res[-1])


def _split_copy_wait(name, started, which, copy_of, after):
    items = [started["items"][i] for i in which]
    n = len(items)
    flat = [a for it in items for a in it]
    k = len(flat)

    def body(*refs):
        arr, sems = list(refs[:k]), refs[k:k + 2 * n]
        for i, it in enumerate(items):
            mine = [arr.pop(0) for _ in it]
            for m in range(1, N_DEV):
                cp = copy_of(m, mine, sems[i].at[m - 1], sems[n + i].at[m - 1])
                cp.wait_send()
                cp.wait_recv()

    res = pl.pallas_call(
        body, name=name,
        out_shape=[pltpu.HBM(a.shape, a.dtype) for a in flat],
        in_specs=[HBM_SPEC] * k + [SEM_SPEC] * (2 * n) + [ANY_SPEC] * len(after),
        out_specs=[HBM_SPEC] * k,
        input_output_aliases={j: j for j in range(k)},
        compiler_params=pltpu.CompilerParams(has_side_effects=SIDE_EFFECT),
    )(*flat, *[started["send"][i] for i in which], *[started["recv"][i] for i in which], *[_in_hbm(a) for a in after])
    thru, out_items = list(res), []
    for it in items:
        out_items.append(tuple(thru.pop(0) for _ in it))
    return out_items


def _gather_copy(m, refs, send_sem, recv_sem):
    (land_ref,) = refs
    r = land_ref.shape[0] // N_DEV
    mine = land_ref.at[pl.ds(pl.multiple_of(_peer_index(0) * r, 8), r), :]
    return pltpu.make_async_remote_copy(src_ref=mine, dst_ref=mine, send_sem=send_sem, recv_sem=recv_sem,
                                        device_id=_peer(m), device_id_type=MESH)


def _scatter_copy(m, refs, send_sem, recv_sem):
    buf_ref, land_ref = refs
    r = buf_ref.shape[0] // N_DEV
    src = buf_ref.at[pl.ds(pl.multiple_of(_peer_index(m) * r, 8), r), :]
    return pltpu.make_async_remote_copy(src_ref=src, dst_ref=land_ref.at[m], send_sem=send_sem, recv_sem=recv_sem,
                                        device_id=_peer(m), device_id_type=MESH)


def _peer(m):
    x, y, c = _mesh_pos()
    bx, by, bc = (m >> 2) & 1, (m >> 1) & 1, m & 1
    return (x ^ bx if bx else x, y ^ by if by else y, c ^ bc if bc else c)


def _peer_index(m):
    x, y, c = _mesh_pos()
    return (4 * x + 2 * y + c) ^ m


SMALL_PLACE = {
    "ln1_gain": (ROW_LN1G, 0), "ln1_bias": (ROW_LN1B, 0), "ln2_gain": (ROW_LN2G, 0), "ln2_bias": (ROW_LN2B, 0),
    "ret_gn_gain": (ROW_GN, 0), "ret_decay_fwd": (ROW_MISC, MISC_DF), "ret_decay_bwd": (ROW_MISC, MISC_DB),
    "attn_sink": (ROW_MISC, MISC_SINK)}


def _small_all_reduce_adamw(part, w, m, v):
    d = part.shape[1]
    names = list(SMALL_PLACE)
    k = len(names)

    def body(*refs):
        part_ref = refs[0]
        w_refs, m_refs, v_refs = refs[1:1 + k], refs[1 + k:1 + 2 * k], refs[1 + 2 * k:1 + 3 * k]
        outs = refs[1 + 3 * k:2 + 7 * k]
        loss_ref, g_refs, dl_refs = outs[0], outs[1:1 + k], outs[1 + k:1 + 2 * k]
        nm_refs, nv_refs = outs[1 + 2 * k:1 + 3 * k], outs[1 + 3 * k:1 + 4 * k]
        buf, send_sems, recv_sems = refs[2 + 7 * k:]
        buf[0] = part_ref[...]
        cps = [pltpu.make_async_remote_copy(
            src_ref=part_ref, dst_ref=buf.at[j], send_sem=send_sems.at[j - 1], recv_sem=recv_sems.at[j - 1],
            device_id=_peer(j), device_id_type=MESH) for j in range(1, N_DEV)]
        for cp in cps:
            cp.start()
        for cp in cps:
            cp.wait_recv()
        for cp in cps:
            cp.wait_send()
        me = _peer_index(0)
        tot = buf[me]
        for dev in range(1, N_DEV):
            tot = tot + buf[dev ^ me]
        loss_ref[...] = (0.5 / d) * jnp.sum(tot[ROW_LOSS:ROW_LOSS + 1, :], axis=1, keepdims=True)
        for i, name in enumerate(names):
            row, lo = SMALL_PLACE[name]
            wv = w_refs[i][...]
            g = tot[row:row + 1, lo:lo + wv.shape[1]]
            if name.startswith("ret_decay"):
                p2 = jnp.exp2(wv)
                g = g * (-p2 * jnp.log(2.0) / (1.0 - p2))
            g_refs[i][...] = g
            _adamw_store(g, wv, m_refs[i][...], v_refs[i][...], dl_refs[i], nm_refs[i], nv_refs[i])

    vm = pl.BlockSpec(memory_space=pltpu.VMEM)
    shapes = [SDS(w[n].shape, F32) for n in names]
    res = pl.pallas_call(
        body, name="small_all_reduce_adamw",
        out_shape=[SDS((1, 1), F32)] + shapes * 4, in_specs=[vm] * (1 + 3 * k), out_specs=[vm] * (1 + 4 * k),
        scratch_shapes=[pltpu.VMEM((N_DEV,) + part.shape, F32), pltpu.SemaphoreType.DMA((7,)),
                        pltpu.SemaphoreType.DMA((7,))],
    )(part, *[w[n] for n in names], *[m[n] for n in names], *[v[n] for n in names])
    groups = [dict(zip(names, res[1 + j * k:1 + (j + 1) * k])) for j in range(4)]
    return (res[0], *groups)


def _adamw_store(g, w, m, v, dl_ref, nm_ref, nv_ref):
    m = ADAM_B1 * m + (1.0 - ADAM_B1) * g
    v = ADAM_B2 * v + (1.0 - ADAM_B2) * (g * g)
    m_hat = m / (1.0 - ADAM_B1 ** ADAM_STEP)
    v_hat = v / (1.0 - ADAM_B2 ** ADAM_STEP)
    dl_ref[...] = -ADAM_LR * (m_hat / (jnp.sqrt(v_hat) + ADAM_EPS) + ADAM_WD * w)
    nm_ref[...] = m
    nv_ref[...] = v


def _reduce_adamw(name, own, recv, w, m, v, transposed):
    def body(own_ref, recv_ref, w_ref, m_ref, v_ref, g_ref, dl_ref, nm_ref, nv_ref):
        g = own_ref[...]
        for k in range(1, N_DEV):
            g = g + recv_ref[k].astype(F32)
        if transposed:
            g = g.T
        g_ref[...] = g
        _adamw_store(g, w_ref[...], m_ref[...], v_ref[...], dl_ref, nm_ref, nv_ref)

    out = SDS(w.shape, F32)
    return pl.pallas_call(body, name="adamw_" + name, out_shape=[out] * 4, compiler_params=_params())(own, recv, w, m, v)


def _row_spec(tm, width):
    return pl.BlockSpec((tm, width), lambda i: (i, 0))


def _full_spec(shape):
    return pl.BlockSpec(shape, lambda i: (0,) * len(shape))


_acc_spec = _full_spec


def _sub_rows(tm):
    step = min(SUB_ROWS, tm)
    return [(lo, lo + step) for lo in range(0, tm, step)]


def _in_proj(x2, wt_in):
    t, d = x2.shape
    u_w = wt_in.shape[0]
    tm = _row_tile(t, MATMUL_ROWS)

    def body(x_ref, w_ref, u_ref, xb_ref):
        xb = x_ref[...].astype(BF16)
        xb_ref[...] = xb
        u_ref[...] = _dot_nt(xb, w_ref[...]).astype(BF16)

    return pl.pallas_call(
        body, name="in_proj", grid=(t // tm,),
        in_specs=[_row_spec(tm, d), _full_spec(wt_in.shape)],
        out_specs=[_row_spec(tm, u_w), _row_spec(tm, d)],
        out_shape=[SDS((t, u_w), BF16), SDS((t, d), BF16)],
        compiler_params=_params(("parallel",)),
    )(x2, wt_in)


def _col_halves(f):
    n = f // LANES
    k = (n + 1) // 2 * LANES
    return [(0, k), (k, f)] if k < f else [(0, f)]


def _mix_ln1_ffn_up(r, a, x2, w_out, wt_gate, wt_up, g1, b1):
    t, d = x2.shape
    f = wt_gate.shape[0]
    tm = _row_tile(t, EPILOGUE_ROWS)

    def body(r_ref, a_ref, x_ref, wo_ref, wg_ref, wu_ref, g_ref, b_ref, z_ref, hb_ref, dg_ref, du_ref, act_ref):
        mix = _dot_nn(r_ref[...], wo_ref[0:RET_W, :]) + _dot_nn(a_ref[...], wo_ref[RET_W:RET_W + ATT_W, :])
        z = ALPHA * x_ref[...] + mix
        xhat, _ = _layer_norm_stats(z)
        z_ref[...] = z
        h = (xhat * g_ref[...] + b_ref[...]).astype(BF16)
        hb_ref[...] = h
        for lo, hi in _col_halves(f):
            g = _dot_nt(h, wg_ref[lo:hi, :])
            u = _dot_nt(h, wu_ref[lo:hi, :])
            sg = _sigmoid(g)
            silu = g * sg
            dg_ref[:, lo:hi] = (u * (sg * (1.0 + g * (1.0 - sg)))).astype(BF16)
            du_ref[:, lo:hi] = silu.astype(BF16)
            act_ref[:, lo:hi] = (silu * u).astype(BF16)

    wide, narrow = _row_spec(tm, f), _row_spec(tm, d)
    return pl.pallas_call(
        body, name="mix_ln1_ffn_up", grid=(t // tm,),
        in_specs=[_row_spec(tm, RET_W), _row_spec(tm, ATT_W), narrow, _resident_spec(w_out.shape),
                  _resident_spec(wt_gate.shape), _resident_spec(wt_up.shape), _full_spec(g1.shape),
                  _full_spec(b1.shape)],
        out_specs=[narrow, narrow, wide, wide, wide],
        out_shape=[SDS((t, d), F32), SDS((t, d), BF16)] + [SDS((t, f), BF16)] * 3,
        compiler_params=_params(("parallel",)),
    )(r, a, x2, w_out, wt_gate, wt_up, g1, b1)


def _ffn_down_ln2_loss(act, dact_dg, dact_du, h1b, p2, z1, target, w_down, w_pg, wt_pe, g1, b1, g2, b2):
    t, d = z1.shape
    f = act.shape[1]
    pdim = p2.shape[1]
    tm = _row_tile(t, EPILOGUE_ROWS)

    def body(act_ref, fg_ref, fu_ref, hb_ref, p_ref, z1_ref, tgt_ref, wd_ref, wpg_ref, wpe_ref, g1_ref, b1_ref,
             g2_ref, b2_ref, dz_ref, dzb_ref, ds_ref, dple_ref, dg_ref, du_ref, acc_ref):
        @pl.when(pl.program_id(0) == 0)
        def _():
            acc_ref[...] = jnp.zeros_like(acc_ref)

        for lo, hi in _sub_rows(tm):
            xhat1, _ = _layer_norm_stats(z1_ref[lo:hi, :])
            h1 = xhat1 * g1_ref[...] + b1_ref[...]
            ffn = _dot_nn(act_ref[lo:hi, :], wd_ref[...])
            pg = _sigmoid(_dot_nn(hb_ref[lo:hi, :], wpg_ref[...]))
            ple = _dot_nt(p_ref[lo:hi, :].astype(BF16), wpe_ref[...])
            z2 = ALPHA * h1 + ffn + pg * ple
            xhat2, rstd2 = _layer_norm_stats(z2)
            err = xhat2 * g2_ref[...] + b2_ref[...] - tgt_ref[lo:hi, :]
            dy = err * (1.0 / d)
            dz = _layer_norm_bwd(dy * g2_ref[...], xhat2, rstd2)
            dzb = dz.astype(BF16)
            dz_ref[lo:hi, :] = dz
            dzb_ref[lo:hi, :] = dzb
            ds_ref[lo:hi, :] = (dz * ple * pg * (1.0 - pg)).astype(BF16)
            dple_ref[lo:hi, :] = (dz * pg).astype(BF16)
            acc_ref[0:1, :] += jnp.sum(err * err, axis=0, keepdims=True)
            acc_ref[1:2, :] += jnp.sum(dy * xhat2, axis=0, keepdims=True)
            acc_ref[2:3, :] += jnp.sum(dy, axis=0, keepdims=True)
            for c0, c1 in _col_halves(f):
                da = _dot_nt(dzb, wd_ref[c0:c1, :])
                dg_ref[lo:hi, c0:c1] = (da * fg_ref[lo:hi, c0:c1].astype(F32)).astype(BF16)
                du_ref[lo:hi, c0:c1] = (da * fu_ref[lo:hi, c0:c1].astype(F32)).astype(BF16)

    vec = _full_spec(g1.shape)
    wide, narrow = _row_spec(tm, f), _row_spec(tm, d)
    return pl.pallas_call(
        body, name="ffn_down_ln2_loss", grid=(t // tm,),
        in_specs=[wide, wide, wide, narrow, _row_spec(tm, pdim), narrow, narrow,
                  _full_spec(w_down.shape), _full_spec(w_pg.shape), _full_spec(wt_pe.shape), vec, vec, vec, vec],
        out_specs=[narrow] * 4 + [wide, wide, _acc_spec((8, d))],
        out_shape=[SDS((t, d), F32), SDS((t, d), BF16), SDS((t, d), BF16), SDS((t, d), BF16),
                   SDS((t, f), BF16), SDS((t, f), BF16), SDS((8, d), F32)],
        compiler_params=_params(("arbitrary",), 56 * 1024 * 1024),
    )(act, dact_dg, dact_du, h1b, p2, z1, target, w_down, w_pg, wt_pe, g1, b1, g2, b2)


def _after(after, body):
    k = len(after)
    return (lambda *refs: body(*refs[k:])), [ANY_SPEC] * k


def _resident_spec(shape):
    return pl.BlockSpec(shape, lambda i: (0,) * len(shape), pipeline_mode=pl.Buffered(1))


def _dh1_ln1_bwd(dz2, dg, dup, dsb, z1, wt_gate, wt_up, w_pg, w_out, g1, after=()):
    t, d = dz2.shape
    f = dg.shape[1]
    tm = _row_tile(t, EPILOGUE_ROWS)

    def body(dz_ref, dg_ref, du_ref, ds_ref, z1_ref, wg_ref, wu_ref, wpg_ref, wo_ref, g1_ref,
             dz1_ref, dz1b_ref, dr_ref, da_ref, acc_ref):
        @pl.when(pl.program_id(0) == 0)
        def _():
            acc_ref[...] = jnp.zeros_like(acc_ref)

        for lo, hi in _sub_rows(tm):
            dh = (ALPHA * dz_ref[lo:hi, :] + _dot_nn(dg_ref[lo:hi, :], wg_ref[...])
                  + _dot_nn(du_ref[lo:hi, :], wu_ref[...]) + _dot_nt(ds_ref[lo:hi, :], wpg_ref[...]))
            xhat, rstd = _layer_norm_stats(z1_ref[lo:hi, :])
            dz1 = _layer_norm_bwd(dh * g1_ref[...], xhat, rstd)
            dz1b = dz1.astype(BF16)
            dz1_ref[lo:hi, :] = dz1
            dz1b_ref[lo:hi, :] = dz1b
            acc_ref[0:1, :] += jnp.sum(dh * xhat, axis=0, keepdims=True)
            acc_ref[1:2, :] += jnp.sum(dh, axis=0, keepdims=True)
            dr_ref[lo:hi, :] = _dot_nt(dz1b, wo_ref[0:RET_W, :]).astype(BF16)
            da_ref[lo:hi, :] = _dot_nt(dz1b, wo_ref[RET_W:RET_W + ATT_W, :]).astype(BF16)

    body, lead = _after(after, body)
    return pl.pallas_call(
        body, name="dh1_ln1_bwd", grid=(t // tm,),
        in_specs=lead + [_row_spec(tm, d), _row_spec(tm, f), _row_spec(tm, f), _row_spec(tm, d), _row_spec(tm, d),
                         _resident_spec(wt_gate.shape), _resident_spec(wt_up.shape), _resident_spec(w_pg.shape),
                         _resident_spec(w_out.shape), _full_spec(g1.shape)],
        out_specs=[_row_spec(tm, d), _row_spec(tm, d), _row_spec(tm, RET_W), _row_spec(tm, ATT_W), _acc_spec((8, d))],
        out_shape=[SDS((t, d), F32), SDS((t, d), BF16), SDS((t, RET_W), BF16), SDS((t, ATT_W), BF16),
                   SDS((8, d), F32)],
        compiler_params=_params(("arbitrary",)),
    )(*after, dz2, dg, dup, dsb, z1, wt_gate, wt_up, w_pg, w_out, g1)


def _in_proj_bwd(dz1, parts, wt_in, after=()):
    t, d = dz1.shape
    tm = _row_tile(t, MATMUL_ROWS)
    widths = [p.shape[1] for p in parts]

    def body(*refs):
        dz_ref, part_refs, w_ref, dx_ref = refs[0], refs[1:1 + len(parts)], refs[-2], refs[-1]
        acc = ALPHA * dz_ref[...]
        lo = 0
        for p_ref, w in zip(part_refs, widths):
            acc = acc + _dot_nn(p_ref[...], w_ref[lo:lo + w, :])
            lo += w
        dx_ref[...] = acc

    body, lead = _after(after, body)
    return pl.pallas_call(
        body, name="in_proj_bwd", grid=(t // tm,),
        in_specs=lead + [_row_spec(tm, d)] + [_row_spec(tm, w) for w in widths] + [_full_spec(wt_in.shape)],
        out_specs=_row_spec(tm, d), out_shape=SDS((t, d), F32),
        compiler_params=_params(("parallel",)),
    )(*after, dz1, *parts, wt_in)


def _weight_grad(name, me, parts, rhs, after=()):
    t, n = rhs.shape
    widths = [p.shape[1] for p in parts]
    rows = sum(widths)
    own_rows = rows // N_DEV
    tk = _row_tile(t, MATMUL_ROWS)
    step = 256

    def body(*refs):
        me_ref, part_refs, rhs_ref = refs[0], refs[1:1 + len(parts)], refs[1 + len(parts)]
        full_ref, own_ref, acc = refs[-3], refs[-2], refs[-1]
        i = pl.program_id(0)

        @pl.when(i == 0)
        def _():
            acc[...] = jnp.zeros_like(acc)

        b = rhs_ref[...].astype(BF16)
        lo = 0
        for p_ref, w in zip(part_refs, widths):
            for c0 in range(0, w, step):
                c1 = min(c0 + step, w)
                acc[lo + c0:lo + c1, :] += _dot_tn(p_ref[:, c0:c1].astype(BF16), b)
            lo += w

        @pl.when(i == pl.num_programs(0) - 1)
        def _():
            full_ref[...] = acc[...].astype(BF16)
            own_ref[...] = acc[pl.ds(pl.multiple_of(me_ref[0] * own_rows, 8), own_rows), :]

    body, lead = _after(after, body)
    return pl.pallas_call(
        body, name=name, grid=(t // tk,),
        in_specs=lead + [_smem_spec()] + [_row_spec(tk, w) for w in widths] + [_row_spec(tk, n)],
        out_specs=[_full_spec((rows, n)), _full_spec((own_rows, n))],
        out_shape=[SDS((rows, n), BF16), SDS((own_rows, n), F32)],
        scratch_shapes=[pltpu.VMEM((rows, n), F32)],
        compiler_params=_params(("arbitrary",)),
    )(*after, me, *parts, rhs)


def _log_decay(decay_f, decay_b):
    def body(f_ref, b_ref, lf_ref, lb_ref):
        lf_ref[...] = jnp.log1p(-jnp.exp2(f_ref[...]))
        lb_ref[...] = jnp.log1p(-jnp.exp2(b_ref[...]))

    return pl.pallas_call(body, name="log_decay", out_shape=[SDS(decay_f.shape, F32)] * 2)(decay_f, decay_b)


def _chunk(ref, n):
    return ref[pl.ds(pl.multiple_of(n * CHUNK, CHUNK), CHUNK), :]


def _group_sum(is_a, v):
    sa = jnp.sum(jnp.where(is_a, v, 0.0), axis=1, keepdims=True)
    sb = jnp.sum(jnp.where(is_a, 0.0, v), axis=1, keepdims=True)
    return jnp.where(is_a, sa, sb)


def _seq_spec(s, col_block):
    return pl.BlockSpec((s, LANES), lambda b, h: (b, col_block + h))


def _smem_spec():
    return pl.BlockSpec(memory_space=pltpu.SMEM)


RET_UNROLL = 4


def _chunk_loop(n_chunks, body, init):
    u = RET_UNROLL if n_chunks % RET_UNROLL == 0 else 1

    def trip(i, carry):
        for j in range(u):
            carry = body(i * u + j, carry)
        return carry

    return lax.fori_loop(0, n_chunks // u, trip, init)


def _stacked_tables(lgf_ref, lgb_ref, pair):
    lane = lax.broadcasted_iota(jnp.int32, (1, LANES), 1)
    is_a = lane < HEAD_DIM
    lgf = jnp.where(is_a, lgf_ref[2 * pair], lgf_ref[2 * pair + 1])
    lgb = jnp.where(is_a, lgb_ref[2 * pair], lgb_ref[2 * pair + 1])
    row = lax.broadcasted_iota(jnp.int32, (CHUNK, 1), 0).astype(F32)
    kdec_f, qdec_f = jnp.exp(lgf * (CHUNK - 1.0 - row)), jnp.exp(lgf * (row + 1.0))
    kdec_b, qdec_b = jnp.exp(lgb * row), jnp.exp(lgb * (CHUNK - row))
    tab = dict(
        is_a=is_a, row=row, lam_f=jnp.exp(lgf * CHUNK), lam_b=jnp.exp(lgb * CHUNK),
        kdec=jnp.concatenate([kdec_f, kdec_b], axis=1), qdec=jnp.concatenate([qdec_f, qdec_b], axis=1),
        qexp=jnp.concatenate([jnp.broadcast_to(row + 1.0, (CHUNK, LANES)),
                              jnp.broadcast_to(CHUNK - row, (CHUNK, LANES))], axis=1),
        kexp=jnp.concatenate([jnp.broadcast_to(CHUNK - 1.0 - row, (CHUNK, LANES)),
                              jnp.broadcast_to(row, (CHUNK, LANES))], axis=1),
    )
    r = lax.broadcasted_iota(jnp.int32, (2 * LANES, LANES), 0)
    c = lax.broadcasted_iota(jnp.int32, (2 * LANES, LANES), 1)
    tab["diag2"] = ((r & (LANES - 1)) < HEAD_DIM) == (c < HEAD_DIM)
    i2 = lax.broadcasted_iota(jnp.int32, (2 * CHUNK, CHUNK), 0)
    j = lax.broadcasted_iota(jnp.int32, (2 * CHUNK, CHUNK), 1)
    head_b = i2 >= CHUNK
    diff = ((i2 & (CHUNK - 1)) - j).astype(F32)
    up, dn = jnp.maximum(diff, 0.0), jnp.maximum(-diff, 0.0)
    lgf2 = jnp.where(head_b, lgf_ref[2 * pair + 1], lgf_ref[2 * pair])
    lgb2 = jnp.where(head_b, lgb_ref[2 * pair + 1], lgb_ref[2 * pair])
    ef = jnp.where(diff >= 0, jnp.exp(lgf2 * up), 0.0)
    eb = jnp.where(diff <= 0, jnp.exp(lgb2 * dn), 0.0)
    tab["d2"] = ef + eb
    tab["df2"] = ef * up
    tab["db2"] = eb * dn
    return tab


def _stack_pair(is_a, x):
    zero = jnp.zeros_like(x)
    return jnp.concatenate([jnp.where(is_a, x, zero), jnp.where(is_a, zero, x)], axis=0)


def _unstack_pair(is_a, x2):
    return jnp.where(is_a, x2[0:CHUNK, :], x2[CHUNK:2 * CHUNK, :])


def _both_ways(x, dec):
    return (jnp.concatenate([x, x], axis=1) * dec).astype(BF16)


def _scan_states(n_chunks, st, up_rows, up_lam, down_rows, down_lam):
    zero = jnp.zeros((LANES, LANES), F32)

    def up(n, r):
        new = st[n, up_rows, :]
        st[n, up_rows, :] = r
        return r * up_lam + new

    def down(s, r):
        n = n_chunks - 1 - s
        new = st[n, down_rows, :]
        st[n, down_rows, :] = r
        return r * down_lam + new

    lax.fori_loop(0, n_chunks, up, zero)
    lax.fori_loop(0, n_chunks, down, zero)


FWD_ROWS, BWD_ROWS = pl.ds(0, LANES), pl.ds(LANES, LANES)


def _retention_fwd(u, lgf, lgb, gn_gain, b_loc):
    t = u.shape[0]
    s = t // b_loc
    n_chunks = s // CHUNK
    pairs = RET_HEADS // 2

    def body(lgf_ref, lgb_ref, q_ref, k_ref, v_ref, g_ref, gain_ref, r_ref, y_ref, st):
        tab = _stacked_tables(lgf_ref, lgb_ref, pl.program_id(1))
        is_a = tab["is_a"]

        def kv_body(n, _):
            k8 = _chunk(k_ref, n).astype(F32) * Q_SCALE
            st[n] = jnp.where(tab["diag2"], _dot_tn(_both_ways(k8, tab["kdec"]), _chunk(v_ref, n)), 0.0)
            return 0

        _chunk_loop(n_chunks, kv_body, 0)
        _scan_states(n_chunks, st, FWD_ROWS, tab["lam_f"], BWD_ROWS, tab["lam_b"])

        def out_body(n, _):
            q = _chunk(q_ref, n)
            k8 = (_chunk(k_ref, n).astype(F32) * Q_SCALE).astype(BF16)
            v = _chunk(v_ref, n)
            p2 = (_dot_nt(_stack_pair(is_a, q), k8) * tab["d2"]).astype(BF16)
            y = _unstack_pair(is_a, _dot_nn(p2, v))
            y = y + _dot_nn(_both_ways(q.astype(F32), tab["qdec"]), st[n].astype(BF16))
            rows = pl.ds(pl.multiple_of(n * CHUNK, CHUNK), CHUNK)
            y_ref[rows, :] = y
            mu = _group_sum(is_a, y) * (1.0 / HEAD_DIM)
            dlt = y - mu
            var = _group_sum(is_a, dlt * dlt) * (1.0 / HEAD_DIM)
            xhat = dlt * lax.rsqrt(var + GN_EPS)
            gate = _chunk(g_ref, n).astype(F32)
            r_ref[rows, :] = (xhat * gain_ref[...] * gate * _sigmoid(gate)).astype(BF16)
            return 0

        _chunk_loop(n_chunks, out_body, 0)

    lane_blk = lambda c0: _seq_spec(s, c0 // LANES)
    return pl.pallas_call(
        body, name="retention_fwd", grid=(b_loc, pairs),
        in_specs=[_smem_spec(), _smem_spec(), lane_blk(C_RQ), lane_blk(C_RK), lane_blk(C_RV), lane_blk(C_RG),
                  pl.BlockSpec((1, LANES), lambda b, h: (0, h))],
        out_specs=[_seq_spec(s, 0), _seq_spec(s, 0)],
        out_shape=[SDS((t, RET_W), BF16), SDS((t, RET_W), F32)],
        scratch_shapes=[pltpu.VMEM((n_chunks, 2 * LANES, LANES), F32)],
        compiler_params=_params(("parallel", "parallel")),
    )(lgf, lgb, u, u, u, u, gn_gain)


ST_GAIN, ST_XF, ST_XB, ST_IFA, ST_IFB, ST_IBA, ST_IBB, ST_LF, ST_LB = 0, 1, 2, 3, 4, 5, 6, 8, 9
ST_ROWS = 16


def _retention_bwd(u, y_pre, dr, lgf, lgb, gn_gain, b_loc, after=()):
    t = u.shape[0]
    s = t // b_loc
    n_chunks = s // CHUNK
    pairs = RET_HEADS // 2

    def body(lgf_ref, lgb_ref, q_ref, k_ref, v_ref, g_ref, y_ref, dr_ref, gain_ref,
             dq_ref, dk_ref, dv_ref, dg_ref, st_ref, st, gr, dy_s):
        tab = _stacked_tables(lgf_ref, lgb_ref, pl.program_id(1))
        is_a, row = tab["is_a"], tab["row"]
        gain = gain_ref[...]

        def norm_body(n, dgain):
            rows = pl.ds(pl.multiple_of(n * CHUNK, CHUNK), CHUNK)
            y = y_ref[rows, :]
            mu = _group_sum(is_a, y) * (1.0 / HEAD_DIM)
            dlt = y - mu
            var = _group_sum(is_a, dlt * dlt) * (1.0 / HEAD_DIM)
            rstd = lax.rsqrt(var + GN_EPS)
            xhat = dlt * rstd
            gate = g_ref[rows, :].astype(F32)
            sg = _sigmoid(gate)
            silu = gate * sg
            d_out = dr_ref[rows, :].astype(F32)
            dg_ref[rows, :] = (d_out * xhat * gain * (sg * (1.0 + gate * (1.0 - sg)))).astype(BF16)
            dxh = d_out * gain * silu
            m1 = _group_sum(is_a, dxh) * (1.0 / HEAD_DIM)
            m2 = _group_sum(is_a, dxh * xhat) * (1.0 / HEAD_DIM)
            dy = (rstd * (dxh - m1 - xhat * m2)).astype(BF16)
            dy_s[rows, :] = dy
            k8 = k_ref[rows, :].astype(F32) * Q_SCALE
            st[n] = jnp.where(tab["diag2"], _dot_tn(_both_ways(k8, tab["kdec"]), v_ref[rows, :]), 0.0)
            qf = q_ref[rows, :].astype(F32)
            gr[n] = jnp.where(tab["diag2"], _dot_tn(_both_ways(qf, tab["qdec"]), dy), 0.0)
            return dgain + jnp.sum(d_out * xhat * silu, axis=0, keepdims=True)

        dgain = _chunk_loop(n_chunks, norm_body, jnp.zeros((1, LANES), F32))
        _scan_states(n_chunks, st, FWD_ROWS, tab["lam_f"], BWD_ROWS, tab["lam_b"])
        _scan_states(n_chunks, gr, BWD_ROWS, tab["lam_b"], FWD_ROWS, tab["lam_f"])
        colsum = lambda x: jnp.sum(x, axis=0, keepdims=True)

        def grad_body(n, carry):
            xfb, ifa, ifb, iba, ibb, lf, lb = carry
            rows = pl.ds(pl.multiple_of(n * CHUNK, CHUNK), CHUNK)
            q = q_ref[rows, :]
            qf = q.astype(F32)
            k8f = k_ref[rows, :].astype(F32) * Q_SCALE
            k8 = k8f.astype(BF16)
            v = v_ref[rows, :]
            dy = dy_s[rows, :]
            q2, dy2 = _stack_pair(is_a, q), _stack_pair(is_a, dy)
            sc = _dot_nt(q2, k8)
            dp = _dot_nt(dy2, v)
            a2 = (sc * tab["d2"]).astype(BF16)
            ds2 = (dp * tab["d2"]).astype(BF16)
            dq = _unstack_pair(is_a, _dot_nn(ds2, k8))
            dk = _dot_tn(ds2, q2)
            dv = _dot_tn(a2, dy2)
            prod = sc * dp
            pf, pb = prod * tab["df2"], prod * tab["db2"]
            ifa, ifb = ifa + colsum(pf[0:CHUNK, :]), ifb + colsum(pf[CHUNK:2 * CHUNK, :])
            iba, ibb = iba + colsum(pb[0:CHUNK, :]), ibb + colsum(pb[CHUNK:2 * CHUNK, :])
            states, sgrads = st[n], gr[n]
            sb, gb = states.astype(BF16), sgrads.astype(BF16)
            dqc = _dot_nt(dy, sb) * tab["qdec"]
            dkc = _dot_nt(v, gb) * tab["kdec"]
            dv = dv + _dot_nn(_both_ways(k8f, tab["kdec"]), gb)
            dq_ref[rows, :] = (dq + dqc[:, 0:LANES] + dqc[:, LANES:2 * LANES]).astype(BF16)
            dk_ref[rows, :] = ((dk + dkc[:, 0:LANES] + dkc[:, LANES:2 * LANES]) * Q_SCALE).astype(BF16)
            dv_ref[rows, :] = dv.astype(BF16)
            q2w, k2w = jnp.concatenate([qf, qf], axis=1), jnp.concatenate([k8f, k8f], axis=1)
            xfb = xfb + colsum(tab["qexp"] * q2w * dqc + tab["kexp"] * k2w * dkc)
            prod_s = sgrads * states
            lf, lb = lf + colsum(prod_s[0:LANES, :]), lb + colsum(prod_s[LANES:2 * LANES, :])
            return xfb, ifa, ifb, iba, ibb, lf, lb

        z = jnp.zeros((1, LANES), F32)
        init = (jnp.zeros((1, 2 * LANES), F32), z, z, z, z, z, z)
        xfb, ifa, ifb, iba, ibb, lf, lb = _chunk_loop(n_chunks, grad_body, init)
        st_ref[...] = jnp.zeros_like(st_ref)
        st_ref[ST_GAIN:ST_GAIN + 1, :] = dgain
        st_ref[ST_XF:ST_XF + 1, :] = xfb[:, 0:LANES]
        st_ref[ST_XB:ST_XB + 1, :] = xfb[:, LANES:2 * LANES]
        st_ref[ST_IFA:ST_IFA + 1, :] = ifa
        st_ref[ST_IFB:ST_IFB + 1, :] = ifb
        st_ref[ST_IBA:ST_IBA + 1, :] = iba
        st_ref[ST_IBB:ST_IBB + 1, :] = ibb
        st_ref[ST_LF:ST_LF + 1, :] = lf * (CHUNK * tab["lam_f"])
        st_ref[ST_LB:ST_LB + 1, :] = lb * (CHUNK * tab["lam_b"])

    lane_blk = lambda c0: _seq_spec(s, c0 // LANES)
    seq0 = _seq_spec(s, 0)
    state = pltpu.VMEM((n_chunks, 2 * LANES, LANES), F32)
    body, lead = _after(after, body)
    return pl.pallas_call(
        body, name="retention_bwd", grid=(b_loc, pairs),
        in_specs=lead + [_smem_spec(), _smem_spec(), lane_blk(C_RQ), lane_blk(C_RK), lane_blk(C_RV), lane_blk(C_RG),
                         seq0, seq0, pl.BlockSpec((1, LANES), lambda b, h: (0, h))],
        out_specs=[seq0] * 4 + [pl.BlockSpec((ST_ROWS, LANES), lambda b, h: (b, h))],
        out_shape=[SDS((t, RET_W), BF16)] * 4 + [SDS((b_loc * ST_ROWS, RET_W), F32)],
        scratch_shapes=[state, state, pltpu.VMEM((s, LANES), BF16)],
        compiler_params=_params(("parallel", "parallel")),
    )(*after, lgf, lgb, u, u, u, u, y_pre, dr, gn_gain)


GW = GROUP * HEAD_DIM
KEYS = 3 * BLOCK


def _attn_tables(g, bias_ref):
    r = lax.broadcasted_iota(jnp.int32, (GROUP * BLOCK, KEYS), 0)
    kj = lax.broadcasted_iota(jnp.int32, (GROUP * BLOCK, KEYS), 1)
    qi = r & (BLOCK - 1)
    hh = lax.shift_right_logical(r, 7)
    dist = jnp.abs(kj - BLOCK - qi)
    slope = jnp.exp2(-(GROUP * g + hh + 1).astype(F32) * (8.0 / ATTN_HEADS))
    bias_ref[...] = jnp.where(dist <= BLOCK, -slope * dist.astype(F32), NEG_INF)


def _tile_keys(x_ref, g, scale, pad_ref, s):
    r = lax.broadcasted_iota(jnp.int32, (LANES, GW), 0)
    c = lax.broadcasted_iota(jnp.int32, (LANES, GW), 1)
    place = jnp.where(r == g * HEAD_DIM + (c & (HEAD_DIM - 1)), 1.0, 0.0).astype(BF16)
    pad_ref[0:BLOCK, :] = jnp.zeros((BLOCK, GW), BF16)
    pad_ref[BLOCK + s:2 * BLOCK + s, :] = jnp.zeros((BLOCK, GW), BF16)
    pad_ref[BLOCK:BLOCK + s, :] = (_dot_nn(x_ref[...], place) * scale).astype(BF16)


def _stack_heads(x):
    lane_h = lax.shift_right_logical(lax.broadcasted_iota(jnp.int32, (1, GW), 1), 6)
    zero = jnp.zeros_like(x)
    return jnp.concatenate([jnp.where(lane_h == h, x, zero) for h in range(GROUP)], axis=0)


def _unstack_heads(x4):
    lane_h = lax.shift_right_logical(lax.broadcasted_iota(jnp.int32, (1, GW), 1), 6)
    out = jnp.zeros((BLOCK, GW), F32)
    for h in range(GROUP):
        out = out + jnp.where(lane_h == h, x4[h * BLOCK:(h + 1) * BLOCK, :], 0.0)
    return out


def _sink_column(sink_ref, g):
    rh = lax.shift_right_logical(lax.broadcasted_iota(jnp.int32, (GROUP * BLOCK, 1), 0), 7)
    col = jnp.zeros((GROUP * BLOCK, 1), F32)
    for h in range(GROUP):
        col = jnp.where(rh == h, sink_ref[GROUP * g + h], col)
    return col


def _attn_probs(qm, k3, bias_ref, sink_col, n, s):
    logits = _dot_nt(qm, k3) + bias_ref[...]
    kpos = n * BLOCK - BLOCK + lax.broadcasted_iota(jnp.int32, (1, KEYS), 1)
    logits = jnp.where((kpos >= 0) & (kpos < s), logits, NEG_INF)
    m = jnp.maximum(jnp.max(logits, axis=1, keepdims=True), sink_col)
    e = jnp.exp(logits - m)
    e_sink = jnp.exp(sink_col - m)
    inv = 1.0 / (jnp.sum(e, axis=1, keepdims=True) + e_sink)
    return e * inv, e_sink * inv


ATT_SUB = 2


def _attn_specs(s, n_steps):
    rows = ATT_SUB * BLOCK
    q_spec = pl.BlockSpec((rows, GW), lambda b, g, n: (b * n_steps + n, C_AQ // GW + g))
    k_spec = pl.BlockSpec((s, LANES), lambda b, g, n: (b, C_AK // LANES))
    v_spec = pl.BlockSpec((s, LANES), lambda b, g, n: (b, C_AV // LANES))
    o_spec = pl.BlockSpec((rows, GW), lambda b, g, n: (b * n_steps + n, g))
    return q_spec, k_spec, v_spec, o_spec


def _attention_fwd(u, sink, b_loc):
    t = u.shape[0]
    s = t // b_loc
    n_steps = s // (ATT_SUB * BLOCK)

    def body(sink_ref, q_ref, k_ref, v_ref, o_ref, kpad, vpad, bias):
        g, step = pl.program_id(1), pl.program_id(2)

        @pl.when(step == 0)
        def _():
            _attn_tables(g, bias)
            _tile_keys(k_ref, g, Q_SCALE, kpad, s)
            _tile_keys(v_ref, g, 1.0, vpad, s)

        sink_col = _sink_column(sink_ref, g)
        for j in range(ATT_SUB):
            n = step * ATT_SUB + j
            rows = pl.ds(j * BLOCK, BLOCK)
            keys = pl.ds(pl.multiple_of(n * BLOCK, BLOCK), KEYS)
            p, _ = _attn_probs(_stack_heads(q_ref[rows, :]), kpad[keys, :], bias, sink_col, n, s)
            o_ref[rows, :] = _unstack_heads(_dot_nn(p.astype(BF16), vpad[keys, :])).astype(BF16)

    q_spec, k_spec, v_spec, o_spec = _attn_specs(s, n_steps)
    pad = pltpu.VMEM((s + 2 * BLOCK, GW), BF16)
    return pl.pallas_call(
        body, name="attention_fwd", grid=(b_loc, KV_HEADS, n_steps),
        in_specs=[_smem_spec(), q_spec, k_spec, v_spec], out_specs=o_spec,
        out_shape=SDS((t, ATT_W), BF16),
        scratch_shapes=[pad, pad, pltpu.VMEM((GROUP * BLOCK, KEYS), F32)],
        compiler_params=_params(("parallel", "arbitrary", "arbitrary")),
    )(sink, u, u, u)


def _fold_groups(x, g):
    x = x + pltpu.roll(x, 2 * HEAD_DIM, 1)
    x = x + pltpu.roll(x, HEAD_DIM, 1)
    lane_g = lax.shift_right_logical(lax.broadcasted_iota(jnp.int32, (1, LANES), 1), 6)
    return jnp.where(lane_g == g, x[:, 0:LANES], 0.0)


def _attention_bwd(u, da, sink, b_loc):
    t = u.shape[0]
    s = t // b_loc
    n_steps = s // (ATT_SUB * BLOCK)

    def body(sink_ref, q_ref, k_ref, v_ref, do_ref, dq_ref, dk_ref, dv_ref, dsink_ref, kpad, vpad, bias, dk_acc, dv_acc):
        g, step = pl.program_id(1), pl.program_id(2)

        @pl.when(step == 0)
        def _():
            _attn_tables(g, bias)
            _tile_keys(k_ref, g, Q_SCALE, kpad, s)
            _tile_keys(v_ref, g, 1.0, vpad, s)
            dsink_ref[...] = jnp.zeros_like(dsink_ref)

        @pl.when((step == 0) & (g == 0))
        def _():
            dk_acc[...] = jnp.zeros_like(dk_acc)
            dv_acc[...] = jnp.zeros_like(dv_acc)

        sink_col = _sink_column(sink_ref, g)
        head_row = lax.broadcasted_iota(jnp.int32, dsink_ref.shape, 0)
        upd = jnp.zeros(dsink_ref.shape, F32)
        for j in range(ATT_SUB):
            n = step * ATT_SUB + j
            rows = pl.ds(j * BLOCK, BLOCK)
            keys = pl.ds(pl.multiple_of(n * BLOCK, BLOCK), KEYS)
            qm = _stack_heads(q_ref[rows, :])
            k3, v3 = kpad[keys, :], vpad[keys, :]
            p, p_sink = _attn_probs(qm, k3, bias, sink_col, n, s)
            dom = _stack_heads(do_ref[rows, :])
            dp = _dot_nt(dom, v3)
            delta = jnp.sum(p * dp, axis=1, keepdims=True)
            ds_mat = (p * (dp - delta)).astype(BF16)
            dq_ref[rows, :] = _unstack_heads(_dot_nn(ds_mat, k3)).astype(BF16)
            dk_acc[keys, :] += _fold_groups(_dot_tn(ds_mat, qm), g) * Q_SCALE
            dv_acc[keys, :] += _fold_groups(_dot_tn(p.astype(BF16), dom), g)
            w = p_sink * delta
            for h in range(GROUP):
                upd = upd + jnp.where(head_row == h, -jnp.sum(w[h * BLOCK:(h + 1) * BLOCK, :]), 0.0)
        dsink_ref[...] += upd

        @pl.when((step == n_steps - 1) & (g == KV_HEADS - 1))
        def _():
            dk_ref[...] = dk_acc[BLOCK:BLOCK + s, :].astype(BF16)
            dv_ref[...] = dv_acc[BLOCK:BLOCK + s, :].astype(BF16)

    q_spec, k_spec, v_spec, o_spec = _attn_specs(s, n_steps)
    kv_out = pl.BlockSpec((s, LANES), lambda b, g, n: (b, 0))
    pad = pltpu.VMEM((s + 2 * BLOCK, GW), BF16)
    acc = pltpu.VMEM((s + 2 * BLOCK, LANES), F32)
    return pl.pallas_call(
        body, name="attention_bwd", grid=(b_loc, KV_HEADS, n_steps),
        in_specs=[_smem_spec(), q_spec, k_spec, v_spec, o_spec],
        out_specs=[o_spec, kv_out, kv_out, pl.BlockSpec((8, LANES), lambda b, g, n: (b * KV_HEADS + g, 0))],
        out_shape=[SDS((t, ATT_W), BF16), SDS((t, KV_W), BF16), SDS((t, KV_W), BF16),
                   SDS((b_loc * KV_HEADS * 8, LANES), F32)],
        scratch_shapes=[pad, pad, pltpu.VMEM((GROUP * BLOCK, KEYS), F32), acc, acc],
        compiler_params=_params(("arbitrary", "arbitrary", "arbitrary")),
    )(sink, u, u, u, da)


def _pack_small(acc2, acc1, ret_stats, dsink, b_loc, d):
    pairs = RET_HEADS // 2

    def body(acc2_ref, acc1_ref, st_ref, dsink_ref, out_ref):
        out_ref[...] = jnp.zeros_like(out_ref)
        out_ref[ROW_LN1G:ROW_LN1G + 1, :] = acc1_ref[0:1, :]
        out_ref[ROW_LN1B:ROW_LN1B + 1, :] = acc1_ref[1:2, :]
        out_ref[ROW_LN2G:ROW_LN2G + 1, :] = acc2_ref[1:2, :]
        out_ref[ROW_LN2B:ROW_LN2B + 1, :] = acc2_ref[2:3, :]
        out_ref[ROW_LOSS:ROW_LOSS + 1, :] = acc2_ref[0:1, :]
        st = st_ref[0:ST_ROWS, :]
        for b in range(1, b_loc):
            st = st + st_ref[b * ST_ROWS:(b + 1) * ST_ROWS, :]
        out_ref[ROW_GN:ROW_GN + 1, 0:RET_W] = st[ST_GAIN:ST_GAIN + 1, :]
        lane = lax.broadcasted_iota(jnp.int32, (1, d), 1)
        misc = jnp.zeros((1, d), F32)
        for pr in range(pairs):
            blk = st[:, pr * LANES:(pr + 1) * LANES]
            half = lax.broadcasted_iota(jnp.int32, (1, LANES), 1) < HEAD_DIM
            for h in range(2):
                sel = half if h == 0 else jnp.logical_not(half)
                cross_f = jnp.sum(jnp.where(sel, blk[ST_XF:ST_XF + 1, :] + blk[ST_LF:ST_LF + 1, :], 0.0))
                cross_b = jnp.sum(jnp.where(sel, blk[ST_XB:ST_XB + 1, :] + blk[ST_LB:ST_LB + 1, :], 0.0))
                intra_f = jnp.sum(blk[ST_IFA + h:ST_IFA + h + 1, :])
                intra_b = jnp.sum(blk[ST_IBA + h:ST_IBA + h + 1, :])
                head = 2 * pr + h
                misc = jnp.where(lane == MISC_DF + head, cross_f + intra_f, misc)
                misc = jnp.where(lane == MISC_DB + head, cross_b + intra_b, misc)
        for g in range(KV_HEADS):
            tot = dsink_ref[g * 8:(g + 1) * 8, :]
            for b in range(1, b_loc):
                tot = tot + dsink_ref[(b * KV_HEADS + g) * 8:(b * KV_HEADS + g + 1) * 8, :]
            for h in range(GROUP):
                misc = jnp.where(lane == MISC_SINK + GROUP * g + h, jnp.sum(tot[h:h + 1, 0:1]), misc)
        out_ref[ROW_MISC:ROW_MISC + 1, :] = misc

    return pl.pallas_call(body, name="pack_small", out_shape=SDS((SMALL_ROWS, d), F32))(acc2, acc1, ret_stats, dsink)


BIG = ("w_in", "w_out", "w_ffn_gate", "w_ffn_up", "w_ffn_down", "w_ple_proj", "w_ple_gate")
TRANSPOSED_OUTSIDE = ("w_in", "w_ffn_gate", "w_ffn_up")
TRANSPOSED_HERE = ("w_ple_proj",)
SMALL = ("ret_decay_fwd", "ret_decay_bwd", "ret_gn_gain", "attn_sink", "ln1_gain", "ln1_bias", "ln2_gain", "ln2_bias")
ORDER = ("w_in", "ret_decay_fwd", "ret_decay_bwd", "ret_gn_gain", "attn_sink", "w_out", "ln1_gain", "ln1_bias",
         "w_ffn_gate", "w_ffn_up", "w_ffn_down", "w_ple_proj", "w_ple_gate", "ln2_gain", "ln2_bias")


GATHER_ORDER = ("w_in", "w_out", "w_ffn_gate", "w_ffn_up", "w_ple_gate", "w_ple_proj", "w_ffn_down")


def _local_step(x2, p2, target2, fetch, publish, small, b_loc, me):
    d = x2.shape[1]
    lgf, lgb = _log_decay(small["ret_decay_fwd"], small["ret_decay_bwd"])
    lgf1, lgb1, sink1 = lgf.reshape(-1), lgb.reshape(-1), small["attn_sink"].reshape(-1)
    (w_in,) = fetch(("w_in",), ())
    u, xb = _in_proj(x2, w_in)
    r, y_pre = _retention_fwd(u, lgf1, lgb1, small["ret_gn_gain"], b_loc)
    a = _attention_fwd(u, sink1, b_loc)
    w_out, w_gate, w_up = fetch(("w_out", "w_ffn_gate", "w_ffn_up"), (r, a))
    z1, h1b, dact_dg, dact_du, act = _mix_ln1_ffn_up(
        r, a, x2, w_out, w_gate, w_up, small["ln1_gain"], small["ln1_bias"])
    w_pg, w_pe, w_down = fetch(("w_ple_gate", "w_ple_proj", "w_ffn_down"), (act,))
    dz2, dz2b, dsb, dpleb, dg, dup, acc2 = _ffn_down_ln2_loss(
        act, dact_dg, dact_du, h1b, p2, z1, target2, w_down, w_pg, w_pe,
        small["ln1_gain"], small["ln1_bias"], small["ln2_gain"], small["ln2_bias"])
    own = {}

    def grad(name, parts, rhs, after=()):
        whole, own[name] = _weight_grad("grad_" + name, me, parts, rhs, after)
        return whole

    t1 = publish("ffn_down", dict(w_ffn_down=grad("w_ffn_down", [act], dz2b),
                                  w_ple_proj=grad("w_ple_proj", [dpleb], p2),
                                  w_ple_gate=grad("w_ple_gate", [h1b], dsb)))
    t2 = publish("ffn_up", dict(w_ffn_gate=grad("w_ffn_gate", [dg], h1b, t1), w_ffn_up=grad("w_ffn_up", [dup], h1b)))
    dz1, dz1b, dr, da, acc1 = _dh1_ln1_bwd(dz2, dg, dup, dsb, z1, w_gate, w_up, w_pg, w_out, small["ln1_gain"], t2)
    t3 = publish("out", dict(w_out=grad("w_out", [r, a], dz1b)))
    dq, dk, dv, dgate, ret_stats = _retention_bwd(u, y_pre, dr, lgf1, lgb1, small["ret_gn_gain"], b_loc, t3)
    daq, dak, dav, dsink = _attention_bwd(u, da, sink1, b_loc)
    parts = [dq, dk, dv, dgate, daq, dak, dav]
    t4 = publish("in", dict(w_in=grad("w_in", parts, xb)))
    grad_x = _in_proj_bwd(dz1, parts, w_in, t4)
    small_part = _pack_small(acc2, acc1, ret_stats, dsink, b_loc, d)
    return grad_x, own, small_part


def kernel(x, p, w_in, ret_decay_fwd, ret_decay_bwd, ret_gn_gain, attn_sink, w_out, ln1_gain, ln1_bias, w_ffn_gate, w_ffn_up, w_ffn_down, w_ple_proj, w_ple_gate, ln2_gain, ln2_bias, loss_target, m_w_in, m_ret_decay_fwd, m_ret_decay_bwd, m_ret_gn_gain, m_attn_sink, m_w_out, m_ln1_gain, m_ln1_bias, m_w_ffn_gate, m_w_ffn_up, m_w_ffn_down, m_w_ple_proj, m_w_ple_gate, m_ln2_gain, m_ln2_bias, v_w_in, v_ret_decay_fwd, v_ret_decay_bwd, v_ret_gn_gain, v_attn_sink, v_w_out, v_ln1_gain, v_ln1_bias, v_w_ffn_gate, v_w_ffn_up, v_w_ffn_down, v_w_ple_proj, v_w_ple_gate, v_ln2_gain, v_ln2_bias):
    given = dict(locals())

    def strip(n, a):
        if n not in BIG:
            return a
        return a[0].T if n in TRANSPOSED_OUTSIDE else a[0]

    def restore(n, a):
        if n not in BIG:
            return a
        return (a.T if n in TRANSPOSED_OUTSIDE else a)[None]

    w = {n: strip(n, given[n]) for n in ORDER}
    m = {n: strip(n, given["m_" + n]) for n in ORDER}
    v = {n: strip(n, given["v_" + n]) for n in ORDER}
    b_loc, s, d = x.shape
    x2 = x.reshape(b_loc * s, d)
    p2 = p[0].reshape(b_loc * s, p.shape[-1])
    target2 = loss_target.reshape(b_loc * s, d)

    small = {n: w[n] for n in SMALL}
    me = (4 * lax.axis_index("x") + 2 * lax.axis_index("y") + lax.axis_index("c")).astype(jnp.int32).reshape(1)

    gathered = _prep_shards(me, {n: w[n] for n in BIG})
    gather = _split_copy_start("gather_start", [(gathered[n],) for n in GATHER_ORDER], _gather_copy)

    def fetch(names, after):
        which = [GATHER_ORDER.index(n) for n in names]
        got = _split_copy_wait("gather_wait_" + names[0], gather, which, _gather_copy, list(after))
        return [item[0] for item in got]

    scatters = []

    def publish(tag, products):
        names = list(products)
        items = [(products[n], lax.empty((N_DEV, products[n].shape[0] // N_DEV, products[n].shape[1]), BF16))
                 for n in names]
        started = _split_copy_start("scatter_start_" + tag, items, _scatter_copy)
        scatters.append((tag, names, started))
        return (started["token"],)

    grad_x, own, small_part = _local_step(x2, p2, target2, fetch, publish, small, b_loc, me)

    out_g, out_d, out_m, out_v = {}, {}, {}, {}
    after = [grad_x]
    for tag, names, started in scatters:
        landed = _split_copy_wait("scatter_wait_" + tag, started, list(range(len(names))), _scatter_copy, after)
        for n, (_, recv) in zip(names, landed):
            out_g[n], out_d[n], out_m[n], out_v[n] = _reduce_adamw(
                n, own[n], recv, w[n], m[n], v[n], n in TRANSPOSED_HERE)
        after = [out_v[names[-1]]]
        if tag == "out":
            loss, sg, sd, sm, sv = _small_all_reduce_adamw(
                small_part, small, {n: m[n] for n in SMALL}, {n: v[n] for n in SMALL})
            for dst, src in ((out_g, sg), (out_d, sd), (out_m, sm), (out_v, sv)):
                dst.update(src)
            after.append(sv["ln2_bias"])

    outs = [loss[0, 0], grad_x.reshape(x.shape)]
    for group in (out_g, out_d, out_m, out_v):
        outs += [restore(n, group[n]) for n in ORDER]
    return tuple(outs)
```

```python
import functools

import jax
import jax.numpy as jnp
from jax import lax
from jax.experimental import pallas as pl
from jax.experimental.pallas import tpu as pltpu

F32, BF16 = jnp.float32, jnp.bfloat16
SDS = jax.ShapeDtypeStruct
MESH = pl.DeviceIdType.MESH

N_DEV = 8
HEAD_DIM = 64
RET_HEADS = 8
ATTN_HEADS = 8
KV_HEADS = 2
GROUP = ATTN_HEADS // KV_HEADS
RET_W = RET_HEADS * HEAD_DIM
ATT_W = ATTN_HEADS * HEAD_DIM
KV_W = KV_HEADS * HEAD_DIM
LANES = 128
CHUNK = 128
BLOCK = 128
Q_SCALE = HEAD_DIM ** -0.5
ALPHA = 2.0 ** 0.25
LN_EPS = 1e-5
GN_EPS = 1e-5
NEG_INF = -1e30
C_RQ, C_RK, C_RV, C_RG = 0, RET_W, 2 * RET_W, 3 * RET_W
C_AQ = 4 * RET_W
C_AK = C_AQ + ATT_W
C_AV = C_AK + KV_W
IN_W = C_AV + KV_W

ADAM_LR = 0.001
ADAM_B1 = 0.9
ADAM_B2 = 0.999
ADAM_EPS = 1e-08
ADAM_WD = 0.01
ADAM_STEP = 10

VMEM_LIMIT = 48 * 1024 * 1024
MATMUL_ROWS = 512
EPILOGUE_ROWS = 256
SUB_ROWS = 256
SMALL_ROWS = 16
ROW_LN1G, ROW_LN1B, ROW_LN2G, ROW_LN2B, ROW_LOSS, ROW_GN, ROW_MISC = 0, 1, 2, 3, 4, 5, 6
MISC_DF, MISC_DB, MISC_SINK = 0, 8, 16


def _dot_nn(a, b):
    return lax.dot_general(a, b, (((1,), (0,)), ((), ())), preferred_element_type=F32)


def _dot_nt(a, b):
    return lax.dot_general(a, b, (((1,), (1,)), ((), ())), preferred_element_type=F32)


def _dot_tn(a, b):
    return lax.dot_general(a, b, (((0,), (0,)), ((), ())), preferred_element_type=F32)


def _params(sem=None, vmem=VMEM_LIMIT):
    kw = {"vmem_limit_bytes": vmem}
    if sem is not None:
        kw["dimension_semantics"] = sem
    return pltpu.CompilerParams(**kw)


def _row_tile(t, want=512):
    tm = want
    while t % tm:
        tm //= 2
    return tm


def _sigmoid(x):
    return 1.0 / (1.0 + jnp.exp(-x))


def _layer_norm_stats(z):
    mu = jnp.mean(z, axis=1, keepdims=True)
    d = z - mu
    var = jnp.mean(d * d, axis=1, keepdims=True)
    rstd = lax.rsqrt(var + LN_EPS)
    return d * rstd, rstd


def _layer_norm_bwd(dxh, xhat, rstd):
    m1 = jnp.mean(dxh, axis=1, keepdims=True)
    m2 = jnp.mean(dxh * xhat, axis=1, keepdims=True)
    return rstd * (dxh - m1 - xhat * m2)


def _prep_shards(me, shards):
    names = list(shards)

    def body(me_ref, *refs):
        for name, src, dst in zip(names, refs[:len(names)], refs[len(names):]):
            val = src[...]
            dst[...] = (val.T if name in TRANSPOSED_HERE else val).astype(BF16)

    shape = lambda n, a: a.shape[::-1] if n in TRANSPOSED_HERE else a.shape
    shapes = [shape(n, shards[n]) for n in names]
    out = pl.pallas_call(
        body, name="prep_shards",
        grid_spec=pltpu.PrefetchScalarGridSpec(
            num_scalar_prefetch=1, grid=(1,),
            in_specs=[pl.BlockSpec(shards[n].shape, lambda i, me_ref: (0, 0)) for n in names],
            out_specs=[pl.BlockSpec(s, lambda i, me_ref: (me_ref[0], 0)) for s in shapes]),
        out_shape=[SDS((N_DEV * s[0], s[1]), BF16) for s in shapes], compiler_params=_params(("arbitrary",)),
    )(me, *[shards[n] for n in names])
    return dict(zip(names, out))


def _mesh_pos():
    return lax.axis_index("x"), lax.axis_index("y"), lax.axis_index("c")


HBM_SPEC = pl.BlockSpec(memory_space=pltpu.HBM)
SEM_SPEC = pl.BlockSpec(memory_space=pltpu.SEMAPHORE)
ANY_SPEC = pl.BlockSpec(memory_space=pl.ANY)
SIDE_EFFECT = pltpu.SideEffectType.DATAFLOW_SIDE_EFFECTING
PEER_SEMS = pltpu.SemaphoreType.DMA((N_DEV - 1,))


def _in_hbm(a):
    return pltpu.with_memory_space_constraint(a, pltpu.HBM)


def _split_copy_start(name, items, copy_of):
    n = len(items)
    flat = [a for it in items for a in it]
    k = len(flat)

    def body(*refs):
        arr, sems = list(refs[:k]), refs[k:k + 2 * n]
        for i, it in enumerate(items):
            mine = [arr.pop(0) for _ in it]
            for m in range(1, N_DEV):
                copy_of(m, mine, sems[i].at[m - 1], sems[n + i].at[m - 1]).start()
        token = refs[-1]
        token[...] = jnp.zeros_like(token)

    res = pl.pallas_call(
        body, name=name,
        out_shape=[PEER_SEMS] * (2 * n) + [pltpu.HBM(a.shape, a.dtype) for a in flat] + [SDS((8, LANES), F32)],
        in_specs=[HBM_SPEC] * k,
        out_specs=[SEM_SPEC] * (2 * n) + [HBM_SPEC] * k + [pl.BlockSpec(memory_space=pltpu.VMEM)],
        input_output_aliases={j: 2 * n + j for j in range(k)},
        compiler_params=pltpu.CompilerParams(has_side_effects=SIDE_EFFECT),
    )(*[_in_hbm(a) for a in flat])
    thru, out_items = list(res[2 * n:2 * n + k]), []
    for it in items:
        out_items.append(tuple(thru.pop(0) for _ in it))
    return dict(send=res[:n], recv=res[n:2 * n], items=out_items, token=res[-1])


def _split_copy_wait(name, started, which, copy_of, after):
    items = [started["items"][i] for i in which]
    n = len(items)
    flat = [a for it in items for a in it]
    k = len(flat)

    def body(*refs):
        arr, sems = list(refs[:k]), refs[k:k + 2 * n]
        for i, it in enumerate(items):
            mine = [arr.pop(0) for _ in it]
            for m in range(1, N_DEV):
                cp = copy_of(m, mine, sems[i].at[m - 1], sems[n + i].at[m - 1])
                cp.wait_send()
                cp.wait_recv()

    res = pl.pallas_call(
        body, name=name,
        out_shape=[pltpu.HBM(a.shape, a.dtype) for a in flat],
        in_specs=[HBM_SPEC] * k + [SEM_SPEC] * (2 * n) + [ANY_SPEC] * len(after),
        out_specs=[HBM_SPEC] * k,
        input_output_aliases={j: j for j in range(k)},
        compiler_params=pltpu.CompilerParams(has_side_effects=SIDE_EFFECT),
    )(*flat, *[started["send"][i] for i in which], *[started["recv"][i] for i in which], *[_in_hbm(a) for a in after])
    thru, out_items = list(res), []
    for it in items:
        out_items.append(tuple(thru.pop(0) for _ in it))
    return out_items


def _gather_copy(m, refs, send_sem, recv_sem):
    (land_ref,) = refs
    r = land_ref.shape[0] // N_DEV
    mine = land_ref.at[pl.ds(pl.multiple_of(_peer_index(0) * r, 8), r), :]
    return pltpu.make_async_remote_copy(src_ref=mine, dst_ref=mine, send_sem=send_sem, recv_sem=recv_sem,
                                        device_id=_peer(m), device_id_type=MESH)


def _scatter_copy(m, refs, send_sem, recv_sem):
    buf_ref, land_ref = refs
    r = buf_ref.shape[0] // N_DEV
    src = buf_ref.at[pl.ds(pl.multiple_of(_peer_index(m) * r, 8), r), :]
    return pltpu.make_async_remote_copy(src_ref=src, dst_ref=land_ref.at[m - 1], send_sem=send_sem,
                                        recv_sem=recv_sem, device_id=_peer(m), device_id_type=MESH)


def _peer(m):
    x, y, c = _mesh_pos()
    bx, by, bc = (m >> 2) & 1, (m >> 1) & 1, m & 1
    return (x ^ bx if bx else x, y ^ by if by else y, c ^ bc if bc else c)


def _peer_index(m):
    x, y, c = _mesh_pos()
    return (4 * x + 2 * y + c) ^ m


SMALL_PLACE = {
    "ln1_gain": (ROW_LN1G, 0), "ln1_bias": (ROW_LN1B, 0), "ln2_gain": (ROW_LN2G, 0), "ln2_bias": (ROW_LN2B, 0),
    "ret_gn_gain": (ROW_GN, 0), "ret_decay_fwd": (ROW_MISC, MISC_DF), "ret_decay_bwd": (ROW_MISC, MISC_DB),
    "attn_sink": (ROW_MISC, MISC_SINK)}


def _small_all_reduce(part):
    def body(part_ref, tot_ref, buf, send_sems, recv_sems):
        buf[0] = part_ref[...]
        cps = [pltpu.make_async_remote_copy(
            src_ref=part_ref, dst_ref=buf.at[j], send_sem=send_sems.at[j - 1], recv_sem=recv_sems.at[j - 1],
            device_id=_peer(j), device_id_type=MESH) for j in range(1, N_DEV)]
        for cp in cps:
            cp.start()
        for cp in cps:
            cp.wait_recv()
        for cp in cps:
            cp.wait_send()
        me = _peer_index(0)
        tot = buf[me]
        for dev in range(1, N_DEV):
            tot = tot + buf[dev ^ me]
        tot_ref[...] = tot

    vm = pl.BlockSpec(memory_space=pltpu.VMEM)
    return pl.pallas_call(
        body, name="small_all_reduce", out_shape=SDS(part.shape, F32), in_specs=[vm], out_specs=vm,
        scratch_shapes=[pltpu.VMEM((N_DEV,) + part.shape, F32), pltpu.SemaphoreType.DMA((7,)),
                        pltpu.SemaphoreType.DMA((7,))],
    )(part)


def _small_adamw(tot, w, m, v):
    d = tot.shape[1]
    names = list(SMALL_PLACE)
    k = len(names)

    def body(*refs):
        tot_ref = refs[0]
        w_refs, m_refs, v_refs = refs[1:1 + k], refs[1 + k:1 + 2 * k], refs[1 + 2 * k:1 + 3 * k]
        outs = refs[1 + 3 * k:]
        loss_ref, g_refs, dl_refs = outs[0], outs[1:1 + k], outs[1 + k:1 + 2 * k]
        nm_refs, nv_refs = outs[1 + 2 * k:1 + 3 * k], outs[1 + 3 * k:1 + 4 * k]
        loss_ref[...] = (0.5 / d) * jnp.sum(tot_ref[ROW_LOSS:ROW_LOSS + 1, :], axis=1, keepdims=True)
        for i, name in enumerate(names):
            row, lo = SMALL_PLACE[name]
            wv = w_refs[i][...]
            g = tot_ref[row:row + 1, lo:lo + wv.shape[1]]
            if name.startswith("ret_decay"):
                p2 = jnp.exp2(wv)
                g = g * (-p2 * jnp.log(2.0) / (1.0 - p2))
            g_refs[i][...] = g
            _adamw_store(g, wv, m_refs[i][...], v_refs[i][...], dl_refs[i], nm_refs[i], nv_refs[i])

    shapes = [SDS(w[n].shape, F32) for n in names]
    res = pl.pallas_call(body, name="small_adamw", out_shape=[SDS((1, 1), F32)] + shapes * 4)(
        tot, *[w[n] for n in names], *[m[n] for n in names], *[v[n] for n in names])
    groups = [dict(zip(names, res[1 + j * k:1 + (j + 1) * k])) for j in range(4)]
    return (res[0], *groups)


def _adamw_store(g, w, m, v, dl_ref, nm_ref, nv_ref):
    m = ADAM_B1 * m + (1.0 - ADAM_B1) * g
    v = ADAM_B2 * v + (1.0 - ADAM_B2) * (g * g)
    m_hat = m / (1.0 - ADAM_B1 ** ADAM_STEP)
    v_hat = v / (1.0 - ADAM_B2 ** ADAM_STEP)
    dl_ref[...] = -ADAM_LR * (m_hat / (jnp.sqrt(v_hat) + ADAM_EPS) + ADAM_WD * w)
    nm_ref[...] = m
    nv_ref[...] = v


def _reduce_adamw(name, own, recv, w, m, v, transposed):
    rows, n = own.shape
    steps = 1 if transposed or rows % 32 else 4
    rb = rows // steps

    def body(own_ref, recv_ref, w_ref, m_ref, v_ref, g_ref, dl_ref, nm_ref, nv_ref):
        g = own_ref[...]
        for k in range(N_DEV - 1):
            g = g + recv_ref[k].astype(F32)
        if transposed:
            g = g.T
        g_ref[...] = g
        _adamw_store(g, w_ref[...], m_ref[...], v_ref[...], dl_ref, nm_ref, nv_ref)

    blk = pl.BlockSpec(w.shape if transposed else (rb, n), lambda i: (i, 0))
    out = SDS(w.shape, F32)
    return pl.pallas_call(
        body, name="adamw_" + name, grid=(steps,),
        in_specs=[pl.BlockSpec((rb, n), lambda i: (i, 0)), pl.BlockSpec((N_DEV - 1, rb, n), lambda i: (0, i, 0)),
                  blk, blk, blk],
        out_specs=[blk] * 4, out_shape=[out] * 4, compiler_params=_params(("parallel",)),
    )(own, recv, w, m, v)


def _row_spec(tm, width):
    return pl.BlockSpec((tm, width), lambda i: (i, 0))


def _full_spec(shape):
    return pl.BlockSpec(shape, lambda i: (0,) * len(shape))


_acc_spec = _full_spec


def _sub_rows(tm):
    step = min(SUB_ROWS, tm)
    return [(lo, lo + step) for lo in range(0, tm, step)]


def _in_proj(x2, wt_in):
    t, d = x2.shape
    u_w = wt_in.shape[0]
    tm = _row_tile(t, MATMUL_ROWS)

    def body(x_ref, w_ref, u_ref, xb_ref):
        xb = x_ref[...].astype(BF16)
        xb_ref[...] = xb
        u_ref[...] = _dot_nt(xb, w_ref[...]).astype(BF16)

    return pl.pallas_call(
        body, name="in_proj", grid=(t // tm,),
        in_specs=[_row_spec(tm, d), _full_spec(wt_in.shape)],
        out_specs=[_row_spec(tm, u_w), _row_spec(tm, d)],
        out_shape=[SDS((t, u_w), BF16), SDS((t, d), BF16)],
        compiler_params=_params(("parallel",)),
    )(x2, wt_in)


def _col_halves(f):
    n = f // LANES
    k = (n + 1) // 2 * LANES
    return [(0, k), (k, f)] if k < f else [(0, f)]


def _mix_ln1_ffn_up(r, a, x2, w_out, wt_gate, wt_up, g1, b1):
    t, d = x2.shape
    f = wt_gate.shape[0]
    tm = _row_tile(t, EPILOGUE_ROWS)

    def body(r_ref, a_ref, x_ref, wo_ref, wg_ref, wu_ref, g_ref, b_ref, z_ref, hb_ref, dg_ref, du_ref, act_ref):
        mix = _dot_nn(r_ref[...], wo_ref[0:RET_W, :]) + _dot_nn(a_ref[...], wo_ref[RET_W:RET_W + ATT_W, :])
        z = ALPHA * x_ref[...] + mix
        xhat, _ = _layer_norm_stats(z)
        z_ref[...] = z
        h = (xhat * g_ref[...] + b_ref[...]).astype(BF16)
        hb_ref[...] = h
        for lo, hi in _col_halves(f):
            g = _dot_nt(h, wg_ref[lo:hi, :])
            u = _dot_nt(h, wu_ref[lo:hi, :])
            sg = _sigmoid(g)
            silu = g * sg
            dg_ref[:, lo:hi] = (u * (sg * (1.0 + g * (1.0 - sg)))).astype(BF16)
            du_ref[:, lo:hi] = silu.astype(BF16)
            act_ref[:, lo:hi] = (silu * u).astype(BF16)

    wide, narrow = _row_spec(tm, f), _row_spec(tm, d)
    return pl.pallas_call(
        body, name="mix_ln1_ffn_up", grid=(t // tm,),
        in_specs=[_row_spec(tm, RET_W), _row_spec(tm, ATT_W), narrow, _resident_spec(w_out.shape),
                  _resident_spec(wt_gate.shape), _resident_spec(wt_up.shape), _full_spec(g1.shape),
                  _full_spec(b1.shape)],
        out_specs=[narrow, narrow, wide, wide, wide],
        out_shape=[SDS((t, d), F32), SDS((t, d), BF16)] + [SDS((t, f), BF16)] * 3,
        compiler_params=_params(("parallel",)),
    )(r, a, x2, w_out, wt_gate, wt_up, g1, b1)


def _ffn_down_ln2_loss(act, dact_dg, dact_du, h1b, p2, z1, target, w_down, w_pg, wt_pe, g1, b1, g2, b2):
    t, d = z1.shape
    f = act.shape[1]
    pdim = p2.shape[1]
    tm = _row_tile(t, EPILOGUE_ROWS)

    def body(act_ref, fg_ref, fu_ref, hb_ref, p_ref, z1_ref, tgt_ref, wd_ref, wpg_ref, wpe_ref, g1_ref, b1_ref,
             g2_ref, b2_ref, dz_ref, dzb_ref, ds_ref, dple_ref, dg_ref, du_ref, acc_ref):
        @pl.when(pl.program_id(0) == 0)
        def _():
            acc_ref[...] = jnp.zeros_like(acc_ref)

        for lo, hi in _sub_rows(tm):
            xhat1, _ = _layer_norm_stats(z1_ref[lo:hi, :])
            h1 = xhat1 * g1_ref[...] + b1_ref[...]
            ffn = _dot_nn(act_ref[lo:hi, :], wd_ref[...])
            pg = _sigmoid(_dot_nn(hb_ref[lo:hi, :], wpg_ref[...]))
            ple = _dot_nt(p_ref[lo:hi, :].astype(BF16), wpe_ref[...])
            z2 = ALPHA * h1 + ffn + pg * ple
            xhat2, rstd2 = _layer_norm_stats(z2)
            err = xhat2 * g2_ref[...] + b2_ref[...] - tgt_ref[lo:hi, :]
            dy = err * (1.0 / d)
            dz = _layer_norm_bwd(dy * g2_ref[...], xhat2, rstd2)
            dzb = dz.astype(BF16)
            dz_ref[lo:hi, :] = dz
            dzb_ref[lo:hi, :] = dzb
            ds_ref[lo:hi, :] = (dz * ple * pg * (1.0 - pg)).astype(BF16)
            dple_ref[lo:hi, :] = (dz * pg).astype(BF16)
            acc_ref[0:1, :] += jnp.sum(err * err, axis=0, keepdims=True)
            acc_ref[1:2, :] += jnp.sum(dy * xhat2, axis=0, keepdims=True)
            acc_ref[2:3, :] += jnp.sum(dy, axis=0, keepdims=True)
            for c0, c1 in _col_halves(f):
                da = _dot_nt(dzb, wd_ref[c0:c1, :])
                dg_ref[lo:hi, c0:c1] = (da * fg_ref[lo:hi, c0:c1].astype(F32)).astype(BF16)
                du_ref[lo:hi, c0:c1] = (da * fu_ref[lo:hi, c0:c1].astype(F32)).astype(BF16)

    vec = _full_spec(g1.shape)
    wide, narrow = _row_spec(tm, f), _row_spec(tm, d)
    return pl.pallas_call(
        body, name="ffn_down_ln2_loss", grid=(t // tm,),
        in_specs=[wide, wide, wide, narrow, _row_spec(tm, pdim), narrow, narrow,
                  _full_spec(w_down.shape), _full_spec(w_pg.shape), _full_spec(wt_pe.shape), vec, vec, vec, vec],
        out_specs=[narrow] * 4 + [wide, wide, _acc_spec((8, d))],
        out_shape=[SDS((t, d), F32), SDS((t, d), BF16), SDS((t, d), BF16), SDS((t, d), BF16),
                   SDS((t, f), BF16), SDS((t, f), BF16), SDS((8, d), F32)],
        compiler_params=_params(("arbitrary",), 56 * 1024 * 1024),
    )(act, dact_dg, dact_du, h1b, p2, z1, target, w_down, w_pg, wt_pe, g1, b1, g2, b2)


def _after(after, body):
    k = len(after)
    return (lambda *refs: body(*refs[k:])), [ANY_SPEC] * k


def _resident_spec(shape):
    return pl.BlockSpec(shape, lambda i: (0,) * len(shape), pipeline_mode=pl.Buffered(1))


def _dh1_ln1_bwd(dz2, dg, dup, dsb, z1, wt_gate, wt_up, w_pg, w_out, g1, after=()):
    t, d = dz2.shape
    f = dg.shape[1]
    tm = _row_tile(t, EPILOGUE_ROWS)

    def body(dz_ref, dg_ref, du_ref, ds_ref, z1_ref, wg_ref, wu_ref, wpg_ref, wo_ref, g1_ref,
             dz1_ref, dz1b_ref, dr_ref, da_ref, acc_ref):
        @pl.when(pl.program_id(0) == 0)
        def _():
            acc_ref[...] = jnp.zeros_like(acc_ref)

        for lo, hi in _sub_rows(tm):
            dh = (ALPHA * dz_ref[lo:hi, :] + _dot_nn(dg_ref[lo:hi, :], wg_ref[...])
                  + _dot_nn(du_ref[lo:hi, :], wu_ref[...]) + _dot_nt(ds_ref[lo:hi, :], wpg_ref[...]))
            xhat, rstd = _layer_norm_stats(z1_ref[lo:hi, :])
            dz1 = _layer_norm_bwd(dh * g1_ref[...], xhat, rstd)
            dz1b = dz1.astype(BF16)
            dz1_ref[lo:hi, :] = dz1
            dz1b_ref[lo:hi, :] = dz1b
            acc_ref[0:1, :] += jnp.sum(dh * xhat, axis=0, keepdims=True)
            acc_ref[1:2, :] += jnp.sum(dh, axis=0, keepdims=True)
            dr_ref[lo:hi, :] = _dot_nt(dz1b, wo_ref[0:RET_W, :]).astype(BF16)
            da_ref[lo:hi, :] = _dot_nt(dz1b, wo_ref[RET_W:RET_W + ATT_W, :]).astype(BF16)

    body, lead = _after(after, body)
    return pl.pallas_call(
        body, name="dh1_ln1_bwd", grid=(t // tm,),
        in_specs=lead + [_row_spec(tm, d), _row_spec(tm, f), _row_spec(tm, f), _row_spec(tm, d), _row_spec(tm, d),
                         _resident_spec(wt_gate.shape), _resident_spec(wt_up.shape), _resident_spec(w_pg.shape),
                         _resident_spec(w_out.shape), _full_spec(g1.shape)],
        out_specs=[_row_spec(tm, d), _row_spec(tm, d), _row_spec(tm, RET_W), _row_spec(tm, ATT_W), _acc_spec((8, d))],
        out_shape=[SDS((t, d), F32), SDS((t, d), BF16), SDS((t, RET_W), BF16), SDS((t, ATT_W), BF16),
                   SDS((8, d), F32)],
        compiler_params=_params(("arbitrary",)),
    )(*after, dz2, dg, dup, dsb, z1, wt_gate, wt_up, w_pg, w_out, g1)


def _in_proj_bwd(dz1, parts, wt_in, after=()):
    t, d = dz1.shape
    tm = _row_tile(t, MATMUL_ROWS)
    widths = [p.shape[1] for p in parts]

    def body(*refs):
        dz_ref, part_refs, w_ref, dx_ref = refs[0], refs[1:1 + len(parts)], refs[-2], refs[-1]
        acc = ALPHA * dz_ref[...]
        lo = 0
        for p_ref, w in zip(part_refs, widths):
            acc = acc + _dot_nn(p_ref[...], w_ref[lo:lo + w, :])
            lo += w
        dx_ref[...] = acc

    body, lead = _after(after, body)
    return pl.pallas_call(
        body, name="in_proj_bwd", grid=(t // tm,),
        in_specs=lead + [_row_spec(tm, d)] + [_row_spec(tm, w) for w in widths] + [_full_spec(wt_in.shape)],
        out_specs=_row_spec(tm, d), out_shape=SDS((t, d), F32),
        compiler_params=_params(("parallel",)),
    )(*after, dz1, *parts, wt_in)


def _weight_grad(name, me, parts, rhs, after=()):
    t, n = rhs.shape
    widths = [p.shape[1] for p in parts]
    rows = sum(widths)
    own_rows = rows // N_DEV
    tk = _row_tile(t, MATMUL_ROWS)
    step = 256

    def body(*refs):
        me_ref, part_refs, rhs_ref = refs[0], refs[1:1 + len(parts)], refs[1 + len(parts)]
        full_ref, own_ref, acc = refs[-3], refs[-2], refs[-1]
        i = pl.program_id(0)

        @pl.when(i == 0)
        def _():
            acc[...] = jnp.zeros_like(acc)

        b = rhs_ref[...].astype(BF16)
        lo = 0
        for p_ref, w in zip(part_refs, widths):
            for c0 in range(0, w, step):
                c1 = min(c0 + step, w)
                acc[lo + c0:lo + c1, :] += _dot_tn(p_ref[:, c0:c1].astype(BF16), b)
            lo += w

        @pl.when(i == pl.num_programs(0) - 1)
        def _():
            full_ref[...] = acc[...].astype(BF16)
            own_ref[...] = acc[pl.ds(pl.multiple_of(me_ref[0] * own_rows, 8), own_rows), :]

    body, lead = _after(after, body)
    return pl.pallas_call(
        body, name=name, grid=(t // tk,),
        in_specs=lead + [_smem_spec()] + [_row_spec(tk, w) for w in widths] + [_row_spec(tk, n)],
        out_specs=[_full_spec((rows, n)), _full_spec((own_rows, n))],
        out_shape=[SDS((rows, n), BF16), SDS((own_rows, n), F32)],
        scratch_shapes=[pltpu.VMEM((rows, n), F32)],
        compiler_params=_params(("arbitrary",)),
    )(*after, me, *parts, rhs)


def _log_decay(decay_f, decay_b):
    def body(f_ref, b_ref, lf_ref, lb_ref):
        lf_ref[...] = jnp.log1p(-jnp.exp2(f_ref[...]))
        lb_ref[...] = jnp.log1p(-jnp.exp2(b_ref[...]))

    return pl.pallas_call(body, name="log_decay", out_shape=[SDS(decay_f.shape, F32)] * 2)(decay_f, decay_b)


def _chunk(ref, n):
    return ref[pl.ds(pl.multiple_of(n * CHUNK, CHUNK), CHUNK), :]


def _group_sum(is_a, v):
    sa = jnp.sum(jnp.where(is_a, v, 0.0), axis=1, keepdims=True)
    sb = jnp.sum(jnp.where(is_a, 0.0, v), axis=1, keepdims=True)
    return jnp.where(is_a, sa, sb)


def _seq_spec(s, col_block):
    return pl.BlockSpec((s, LANES), lambda b, h: (b, col_block + h))


def _smem_spec():
    return pl.BlockSpec(memory_space=pltpu.SMEM)


RET_UNROLL = 4


def _chunk_loop(n_chunks, body, init):
    u = RET_UNROLL if n_chunks % RET_UNROLL == 0 else 1

    def trip(i, carry):
        for j in range(u):
            carry = body(i * u + j, carry)
        return carry

    return lax.fori_loop(0, n_chunks // u, trip, init)


def _stacked_tables(lgf_ref, lgb_ref, pair):
    lane = lax.broadcasted_iota(jnp.int32, (1, LANES), 1)
    is_a = lane < HEAD_DIM
    lgf = jnp.where(is_a, lgf_ref[2 * pair], lgf_ref[2 * pair + 1])
    lgb = jnp.where(is_a, lgb_ref[2 * pair], lgb_ref[2 * pair + 1])
    row = lax.broadcasted_iota(jnp.int32, (CHUNK, 1), 0).astype(F32)
    kdec_f, qdec_f = jnp.exp(lgf * (CHUNK - 1.0 - row)), jnp.exp(lgf * (row + 1.0))
    kdec_b, qdec_b = jnp.exp(lgb * row), jnp.exp(lgb * (CHUNK - row))
    tab = dict(
        is_a=is_a, row=row, lam_f=jnp.exp(lgf * CHUNK), lam_b=jnp.exp(lgb * CHUNK),
        kdec=jnp.concatenate([kdec_f, kdec_b], axis=1), qdec=jnp.concatenate([qdec_f, qdec_b], axis=1),
        qexp=jnp.concatenate([jnp.broadcast_to(row + 1.0, (CHUNK, LANES)),
                              jnp.broadcast_to(CHUNK - row, (CHUNK, LANES))], axis=1),
        kexp=jnp.concatenate([jnp.broadcast_to(CHUNK - 1.0 - row, (CHUNK, LANES)),
                              jnp.broadcast_to(row, (CHUNK, LANES))], axis=1),
    )
    r = lax.broadcasted_iota(jnp.int32, (2 * LANES, LANES), 0)
    c = lax.broadcasted_iota(jnp.int32, (2 * LANES, LANES), 1)
    tab["diag2"] = ((r & (LANES - 1)) < HEAD_DIM) == (c < HEAD_DIM)
    i2 = lax.broadcasted_iota(jnp.int32, (2 * CHUNK, CHUNK), 0)
    j = lax.broadcasted_iota(jnp.int32, (2 * CHUNK, CHUNK), 1)
    head_b = i2 >= CHUNK
    diff = ((i2 & (CHUNK - 1)) - j).astype(F32)
    up, dn = jnp.maximum(diff, 0.0), jnp.maximum(-diff, 0.0)
    lgf2 = jnp.where(head_b, lgf_ref[2 * pair + 1], lgf_ref[2 * pair])
    lgb2 = jnp.where(head_b, lgb_ref[2 * pair + 1], lgb_ref[2 * pair])
    ef = jnp.where(diff >= 0, jnp.exp(lgf2 * up), 0.0)
    eb = jnp.where(diff <= 0, jnp.exp(lgb2 * dn), 0.0)
    tab["d2"] = ef + eb
    tab["df2"] = ef * up
    tab["db2"] = eb * dn
    return tab


def _stack_pair(is_a, x):
    zero = jnp.zeros_like(x)
    return jnp.concatenate([jnp.where(is_a, x, zero), jnp.where(is_a, zero, x)], axis=0)


def _unstack_pair(is_a, x2):
    return jnp.where(is_a, x2[0:CHUNK, :], x2[CHUNK:2 * CHUNK, :])


def _both_ways(x, dec):
    return (jnp.concatenate([x, x], axis=1) * dec).astype(BF16)


def _scan_states(n_chunks, st, up_rows, up_lam, down_rows, down_lam):
    zero = jnp.zeros((LANES, LANES), F32)

    def up(n, r):
        new = st[n, up_rows, :]
        st[n, up_rows, :] = r
        return r * up_lam + new

    def down(s, r):
        n = n_chunks - 1 - s
        new = st[n, down_rows, :]
        st[n, down_rows, :] = r
        return r * down_lam + new

    lax.fori_loop(0, n_chunks, up, zero)
    lax.fori_loop(0, n_chunks, down, zero)


FWD_ROWS, BWD_ROWS = pl.ds(0, LANES), pl.ds(LANES, LANES)


def _retention_fwd(u, lgf, lgb, gn_gain, b_loc):
    t = u.shape[0]
    s = t // b_loc
    n_chunks = s // CHUNK
    pairs = RET_HEADS // 2

    def body(lgf_ref, lgb_ref, q_ref, k_ref, v_ref, g_ref, gain_ref, r_ref, y_ref, st):
        tab = _stacked_tables(lgf_ref, lgb_ref, pl.program_id(1))
        is_a = tab["is_a"]

        def kv_body(n, _):
            k8 = _chunk(k_ref, n).astype(F32) * Q_SCALE
            st[n] = jnp.where(tab["diag2"], _dot_tn(_both_ways(k8, tab["kdec"]), _chunk(v_ref, n)), 0.0)
            return 0

        _chunk_loop(n_chunks, kv_body, 0)
        _scan_states(n_chunks, st, FWD_ROWS, tab["lam_f"], BWD_ROWS, tab["lam_b"])

        def out_body(n, _):
            q = _chunk(q_ref, n)
            k8 = (_chunk(k_ref, n).astype(F32) * Q_SCALE).astype(BF16)
            v = _chunk(v_ref, n)
            p2 = (_dot_nt(_stack_pair(is_a, q), k8) * tab["d2"]).astype(BF16)
            y = _unstack_pair(is_a, _dot_nn(p2, v))
            y = y + _dot_nn(_both_ways(q.astype(F32), tab["qdec"]), st[n].astype(BF16))
            rows = pl.ds(pl.multiple_of(n * CHUNK, CHUNK), CHUNK)
            y_ref[rows, :] = y
            mu = _group_sum(is_a, y) * (1.0 / HEAD_DIM)
            dlt = y - mu
            var = _group_sum(is_a, dlt * dlt) * (1.0 / HEAD_DIM)
            xhat = dlt * lax.rsqrt(var + GN_EPS)
            gate = _chunk(g_ref, n).astype(F32)
            r_ref[rows, :] = (xhat * gain_ref[...] * gate * _sigmoid(gate)).astype(BF16)
            return 0

        _chunk_loop(n_chunks, out_body, 0)

    lane_blk = lambda c0: _seq_spec(s, c0 // LANES)
    return pl.pallas_call(
        body, name="retention_fwd", grid=(b_loc, pairs),
        in_specs=[_smem_spec(), _smem_spec(), lane_blk(C_RQ), lane_blk(C_RK), lane_blk(C_RV), lane_blk(C_RG),
                  pl.BlockSpec((1, LANES), lambda b, h: (0, h))],
        out_specs=[_seq_spec(s, 0), _seq_spec(s, 0)],
        out_shape=[SDS((t, RET_W), BF16), SDS((t, RET_W), F32)],
        scratch_shapes=[pltpu.VMEM((n_chunks, 2 * LANES, LANES), F32)],
        compiler_params=_params(("parallel", "parallel")),
    )(lgf, lgb, u, u, u, u, gn_gain)


ST_GAIN, ST_XF, ST_XB, ST_IFA, ST_IFB, ST_IBA, ST_IBB, ST_LF, ST_LB = 0, 1, 2, 3, 4, 5, 6, 8, 9
ST_ROWS = 16


def _retention_bwd(u, y_pre, dr, lgf, lgb, gn_gain, b_loc, after=()):
    t = u.shape[0]
    s = t // b_loc
    n_chunks = s // CHUNK
    pairs = RET_HEADS // 2

    def body(lgf_ref, lgb_ref, q_ref, k_ref, v_ref, g_ref, y_ref, dr_ref, gain_ref,
             dq_ref, dk_ref, dv_ref, dg_ref, st_ref, st, gr, dy_s):
        tab = _stacked_tables(lgf_ref, lgb_ref, pl.program_id(1))
        is_a, row = tab["is_a"], tab["row"]
        gain = gain_ref[...]

        def norm_body(n, dgain):
            rows = pl.ds(pl.multiple_of(n * CHUNK, CHUNK), CHUNK)
            y = y_ref[rows, :]
            mu = _group_sum(is_a, y) * (1.0 / HEAD_DIM)
            dlt = y - mu
            var = _group_sum(is_a, dlt * dlt) * (1.0 / HEAD_DIM)
            rstd = lax.rsqrt(var + GN_EPS)
            xhat = dlt * rstd
            gate = g_ref[rows, :].astype(F32)
            sg = _sigmoid(gate)
            silu = gate * sg
            d_out = dr_ref[rows, :].astype(F32)
            dg_ref[rows, :] = (d_out * xhat * gain * (sg * (1.0 + gate * (1.0 - sg)))).astype(BF16)
            dxh = d_out * gain * silu
            m1 = _group_sum(is_a, dxh) * (1.0 / HEAD_DIM)
            m2 = _group_sum(is_a, dxh * xhat) * (1.0 / HEAD_DIM)
            dy = (rstd * (dxh - m1 - xhat * m2)).astype(BF16)
            dy_s[rows, :] = dy
            k8 = k_ref[rows, :].astype(F32) * Q_SCALE
            st[n] = jnp.where(tab["diag2"], _dot_tn(_both_ways(k8, tab["kdec"]), v_ref[rows, :]), 0.0)
            qf = q_ref[rows, :].astype(F32)
            gr[n] = jnp.where(tab["diag2"], _dot_tn(_both_ways(qf, tab["qdec"]), dy), 0.0)
            return dgain + jnp.sum(d_out * xhat * silu, axis=0, keepdims=True)

        dgain = _chunk_loop(n_chunks, norm_body, jnp.zeros((1, LANES), F32))
        _scan_states(n_chunks, st, FWD_ROWS, tab["lam_f"], BWD_ROWS, tab["lam_b"])
        _scan_states(n_chunks, gr, BWD_ROWS, tab["lam_b"], FWD_ROWS, tab["lam_f"])
        colsum = lambda x: jnp.sum(x, axis=0, keepdims=True)

        def grad_body(n, carry):
            xfb, ifa, ifb, iba, ibb, lf, lb = carry
            rows = pl.ds(pl.multiple_of(n * CHUNK, CHUNK), CHUNK)
            q = q_ref[rows, :]
            qf = q.astype(F32)
            k8f = k_ref[rows, :].astype(F32) * Q_SCALE
            k8 = k8f.astype(BF16)
            v = v_ref[rows, :]
            dy = dy_s[rows, :]
            q2, dy2 = _stack_pair(is_a, q), _stack_pair(is_a, dy)
            sc = _dot_nt(q2, k8)
            dp = _dot_nt(dy2, v)
            a2 = (sc * tab["d2"]).astype(BF16)
            ds2 = (dp * tab["d2"]).astype(BF16)
            dq = _unstack_pair(is_a, _dot_nn(ds2, k8))
            dk = _dot_tn(ds2, q2)
            dv = _dot_tn(a2, dy2)
            prod = sc * dp
            pf, pb = prod * tab["df2"], prod * tab["db2"]
            ifa, ifb = ifa + colsum(pf[0:CHUNK, :]), ifb + colsum(pf[CHUNK:2 * CHUNK, :])
            iba, ibb = iba + colsum(pb[0:CHUNK, :]), ibb + colsum(pb[CHUNK:2 * CHUNK, :])
            states, sgrads = st[n], gr[n]
            sb, gb = states.astype(BF16), sgrads.astype(BF16)
            dqc = _dot_nt(dy, sb) * tab["qdec"]
            dkc = _dot_nt(v, gb) * tab["kdec"]
            dv = dv + _dot_nn(_both_ways(k8f, tab["kdec"]), gb)
            dq_ref[rows, :] = (dq + dqc[:, 0:LANES] + dqc[:, LANES:2 * LANES]).astype(BF16)
            dk_ref[rows, :] = ((dk + dkc[:, 0:LANES] + dkc[:, LANES:2 * LANES]) * Q_SCALE).astype(BF16)
            dv_ref[rows, :] = dv.astype(BF16)
            q2w, k2w = jnp.concatenate([qf, qf], axis=1), jnp.concatenate([k8f, k8f], axis=1)
            xfb = xfb + colsum(tab["qexp"] * q2w * dqc + tab["kexp"] * k2w * dkc)
            prod_s = sgrads * states
            lf, lb = lf + colsum(prod_s[0:LANES, :]), lb + colsum(prod_s[LANES:2 * LANES, :])
            return xfb, ifa, ifb, iba, ibb, lf, lb

        z = jnp.zeros((1, LANES), F32)
        init = (jnp.zeros((1, 2 * LANES), F32), z, z, z, z, z, z)
        xfb, ifa, ifb, iba, ibb, lf, lb = _chunk_loop(n_chunks, grad_body, init)
        st_ref[...] = jnp.zeros_like(st_ref)
        st_ref[ST_GAIN:ST_GAIN + 1, :] = dgain
        st_ref[ST_XF:ST_XF + 1, :] = xfb[:, 0:LANES]
        st_ref[ST_XB:ST_XB + 1, :] = xfb[:, LANES:2 * LANES]
        st_ref[ST_IFA:ST_IFA + 1, :] = ifa
        st_ref[ST_IFB:ST_IFB + 1, :] = ifb
        st_ref[ST_IBA:ST_IBA + 1, :] = iba
        st_ref[ST_IBB:ST_IBB + 1, :] = ibb
        st_ref[ST_LF:ST_LF + 1, :] = lf * (CHUNK * tab["lam_f"])
        st_ref[ST_LB:ST_LB + 1, :] = lb * (CHUNK * tab["lam_b"])

    lane_blk = lambda c0: _seq_spec(s, c0 // LANES)
    seq0 = _seq_spec(s, 0)
    state = pltpu.VMEM((n_chunks, 2 * LANES, LANES), F32)
    body, lead = _after(after, body)
    return pl.pallas_call(
        body, name="retention_bwd", grid=(b_loc, pairs),
        in_specs=lead + [_smem_spec(), _smem_spec(), lane_blk(C_RQ), lane_blk(C_RK), lane_blk(C_RV), lane_blk(C_RG),
                         seq0, seq0, pl.BlockSpec((1, LANES), lambda b, h: (0, h))],
        out_specs=[seq0] * 4 + [pl.BlockSpec((ST_ROWS, LANES), lambda b, h: (b, h))],
        out_shape=[SDS((t, RET_W), BF16)] * 4 + [SDS((b_loc * ST_ROWS, RET_W), F32)],
        scratch_shapes=[state, state, pltpu.VMEM((s, LANES), BF16)],
        compiler_params=_params(("parallel", "parallel")),
    )(*after, lgf, lgb, u, u, u, u, y_pre, dr, gn_gain)


GW = GROUP * HEAD_DIM
KEYS = 3 * BLOCK


def _attn_tables(g, bias_ref):
    r = lax.broadcasted_iota(jnp.int32, (GROUP * BLOCK, KEYS), 0)
    kj = lax.broadcasted_iota(jnp.int32, (GROUP * BLOCK, KEYS), 1)
    qi = r & (BLOCK - 1)
    hh = lax.shift_right_logical(r, 7)
    dist = jnp.abs(kj - BLOCK - qi)
    slope = jnp.exp2(-(GROUP * g + hh + 1).astype(F32) * (8.0 / ATTN_HEADS))
    bias_ref[...] = jnp.where(dist <= BLOCK, -slope * dist.astype(F32), NEG_INF)


def _tile_keys(x_ref, g, scale, pad_ref, s):
    r = lax.broadcasted_iota(jnp.int32, (LANES, GW), 0)
    c = lax.broadcasted_iota(jnp.int32, (LANES, GW), 1)
    place = jnp.where(r == g * HEAD_DIM + (c & (HEAD_DIM - 1)), 1.0, 0.0).astype(BF16)
    pad_ref[0:BLOCK, :] = jnp.zeros((BLOCK, GW), BF16)
    pad_ref[BLOCK + s:2 * BLOCK + s, :] = jnp.zeros((BLOCK, GW), BF16)
    pad_ref[BLOCK:BLOCK + s, :] = (_dot_nn(x_ref[...], place) * scale).astype(BF16)


def _stack_heads(x):
    lane_h = lax.shift_right_logical(lax.broadcasted_iota(jnp.int32, (1, GW), 1), 6)
    zero = jnp.zeros_like(x)
    return jnp.concatenate([jnp.where(lane_h == h, x, zero) for h in range(GROUP)], axis=0)


def _unstack_heads(x4):
    lane_h = lax.shift_right_logical(lax.broadcasted_iota(jnp.int32, (1, GW), 1), 6)
    out = jnp.zeros((BLOCK, GW), F32)
    for h in range(GROUP):
        out = out + jnp.where(lane_h == h, x4[h * BLOCK:(h + 1) * BLOCK, :], 0.0)
    return out


def _sink_column(sink_ref, g):
    rh = lax.shift_right_logical(lax.broadcasted_iota(jnp.int32, (GROUP * BLOCK, 1), 0), 7)
    col = jnp.zeros((GROUP * BLOCK, 1), F32)
    for h in range(GROUP):
        col = jnp.where(rh == h, sink_ref[GROUP * g + h], col)
    return col


def _attn_probs(qm, k3, bias_ref, sink_col, n, s):
    logits = _dot_nt(qm, k3) + bias_ref[...]
    kpos = n * BLOCK - BLOCK + lax.broadcasted_iota(jnp.int32, (1, KEYS), 1)
    logits = jnp.where((kpos >= 0) & (kpos < s), logits, NEG_INF)
    m = jnp.maximum(jnp.max(logits, axis=1, keepdims=True), sink_col)
    e = jnp.exp(logits - m)
    e_sink = jnp.exp(sink_col - m)
    inv = 1.0 / (jnp.sum(e, axis=1, keepdims=True) + e_sink)
    return e * inv, e_sink * inv


ATT_SUB = 2


def _attn_specs(s, n_steps):
    rows = ATT_SUB * BLOCK
    q_spec = pl.BlockSpec((rows, GW), lambda b, g, n: (b * n_steps + n, C_AQ // GW + g))
    k_spec = pl.BlockSpec((s, LANES), lambda b, g, n: (b, C_AK // LANES))
    v_spec = pl.BlockSpec((s, LANES), lambda b, g, n: (b, C_AV // LANES))
    o_spec = pl.BlockSpec((rows, GW), lambda b, g, n: (b * n_steps + n, g))
    return q_spec, k_spec, v_spec, o_spec


def _attention_fwd(u, sink, b_loc):
    t = u.shape[0]
    s = t // b_loc
    n_steps = s // (ATT_SUB * BLOCK)

    def body(sink_ref, q_ref, k_ref, v_ref, o_ref, kpad, vpad, bias):
        g, step = pl.program_id(1), pl.program_id(2)

        @pl.when(step == 0)
        def _():
            _attn_tables(g, bias)
            _tile_keys(k_ref, g, Q_SCALE, kpad, s)
            _tile_keys(v_ref, g, 1.0, vpad, s)

        sink_col = _sink_column(sink_ref, g)
        for j in range(ATT_SUB):
            n = step * ATT_SUB + j
            rows = pl.ds(j * BLOCK, BLOCK)
            keys = pl.ds(pl.multiple_of(n * BLOCK, BLOCK), KEYS)
            p, _ = _attn_probs(_stack_heads(q_ref[rows, :]), kpad[keys, :], bias, sink_col, n, s)
            o_ref[rows, :] = _unstack_heads(_dot_nn(p.astype(BF16), vpad[keys, :])).astype(BF16)

    q_spec, k_spec, v_spec, o_spec = _attn_specs(s, n_steps)
    pad = pltpu.VMEM((s + 2 * BLOCK, GW), BF16)
    return pl.pallas_call(
        body, name="attention_fwd", grid=(b_loc, KV_HEADS, n_steps),
        in_specs=[_smem_spec(), q_spec, k_spec, v_spec], out_specs=o_spec,
        out_shape=SDS((t, ATT_W), BF16),
        scratch_shapes=[pad, pad, pltpu.VMEM((GROUP * BLOCK, KEYS), F32)],
        compiler_params=_params(("parallel", "arbitrary", "arbitrary")),
    )(sink, u, u, u)


def _fold_groups(x, g):
    x = x + pltpu.roll(x, 2 * HEAD_DIM, 1)
    x = x + pltpu.roll(x, HEAD_DIM, 1)
    lane_g = lax.shift_right_logical(lax.broadcasted_iota(jnp.int32, (1, LANES), 1), 6)
    return jnp.where(lane_g == g, x[:, 0:LANES], 0.0)


def _attention_bwd(u, da, sink, b_loc):
    t = u.shape[0]
    s = t // b_loc
    n_steps = s // (ATT_SUB * BLOCK)

    def body(sink_ref, q_ref, k_ref, v_ref, do_ref, dq_ref, dk_ref, dv_ref, dsink_ref, kpad, vpad, bias, dk_acc, dv_acc):
        g, step = pl.program_id(1), pl.program_id(2)

        @pl.when(step == 0)
        def _():
            _attn_tables(g, bias)
            _tile_keys(k_ref, g, Q_SCALE, kpad, s)
            _tile_keys(v_ref, g, 1.0, vpad, s)
            dsink_ref[...] = jnp.zeros_like(dsink_ref)

        @pl.when((step == 0) & (g == 0))
        def _():
            dk_acc[...] = jnp.zeros_like(dk_acc)
            dv_acc[...] = jnp.zeros_like(dv_acc)

        sink_col = _sink_column(sink_ref, g)
        head_row = lax.broadcasted_iota(jnp.int32, dsink_ref.shape, 0)
        upd = jnp.zeros(dsink_ref.shape, F32)
        for j in range(ATT_SUB):
            n = step * ATT_SUB + j
            rows = pl.ds(j * BLOCK, BLOCK)
            keys = pl.ds(pl.multiple_of(n * BLOCK, BLOCK), KEYS)
            qm = _stack_heads(q_ref[rows, :])
            k3, v3 = kpad[keys, :], vpad[keys, :]
            p, p_sink = _attn_probs(qm, k3, bias, sink_col, n, s)
            dom = _stack_heads(do_ref[rows, :])
            dp = _dot_nt(dom, v3)
            delta = jnp.sum(p * dp, axis=1, keepdims=True)
            ds_mat = (p * (dp - delta)).astype(BF16)
            dq_ref[rows, :] = _unstack_heads(_dot_nn(ds_mat, k3)).astype(BF16)
            dk_acc[keys, :] += _fold_groups(_dot_tn(ds_mat, qm), g) * Q_SCALE
            dv_acc[keys, :] += _fold_groups(_dot_tn(p.astype(BF16), dom), g)
            w = p_sink * delta
            for h in range(GROUP):
                upd = upd + jnp.where(head_row == h, -jnp.sum(w[h * BLOCK:(h + 1) * BLOCK, :]), 0.0)
        dsink_ref[...] += upd

        @pl.when((step == n_steps - 1) & (g == KV_HEADS - 1))
        def _():
            dk_ref[...] = dk_acc[BLOCK:BLOCK + s, :].astype(BF16)
            dv_ref[...] = dv_acc[BLOCK:BLOCK + s, :].astype(BF16)

    q_spec, k_spec, v_spec, o_spec = _attn_specs(s, n_steps)
    kv_out = pl.BlockSpec((s, LANES), lambda b, g, n: (b, 0))
    pad = pltpu.VMEM((s + 2 * BLOCK, GW), BF16)
    acc = pltpu.VMEM((s + 2 * BLOCK, LANES), F32)
    return pl.pallas_call(
        body, name="attention_bwd", grid=(b_loc, KV_HEADS, n_steps),
        in_specs=[_smem_spec(), q_spec, k_spec, v_spec, o_spec],
        out_specs=[o_spec, kv_out, kv_out, pl.BlockSpec((8, LANES), lambda b, g, n: (b * KV_HEADS + g, 0))],
        out_shape=[SDS((t, ATT_W), BF16), SDS((t, KV_W), BF16), SDS((t, KV_W), BF16),
                   SDS((b_loc * KV_HEADS * 8, LANES), F32)],
        scratch_shapes=[pad, pad, pltpu.VMEM((GROUP * BLOCK, KEYS), F32), acc, acc],
        compiler_params=_params(("arbitrary", "arbitrary", "arbitrary")),
    )(sink, u, u, u, da)


def _pack_small(acc2, acc1, ret_stats, dsink, b_loc, d):
    pairs = RET_HEADS // 2

    def body(acc2_ref, acc1_ref, st_ref, dsink_ref, out_ref):
        out_ref[...] = jnp.zeros_like(out_ref)
        out_ref[ROW_LN1G:ROW_LN1G + 1, :] = acc1_ref[0:1, :]
        out_ref[ROW_LN1B:ROW_LN1B + 1, :] = acc1_ref[1:2, :]
        out_ref[ROW_LN2G:ROW_LN2G + 1, :] = acc2_ref[1:2, :]
        out_ref[ROW_LN2B:ROW_LN2B + 1, :] = acc2_ref[2:3, :]
        out_ref[ROW_LOSS:ROW_LOSS + 1, :] = acc2_ref[0:1, :]
        st = st_ref[0:ST_ROWS, :]
        for b in range(1, b_loc):
            st = st + st_ref[b * ST_ROWS:(b + 1) * ST_ROWS, :]
        out_ref[ROW_GN:ROW_GN + 1, 0:RET_W] = st[ST_GAIN:ST_GAIN + 1, :]
        lane = lax.broadcasted_iota(jnp.int32, (1, d), 1)
        misc = jnp.zeros((1, d), F32)
        for pr in range(pairs):
            blk = st[:, pr * LANES:(pr + 1) * LANES]
            half = lax.broadcasted_iota(jnp.int32, (1, LANES), 1) < HEAD_DIM
            for h in range(2):
                sel = half if h == 0 else jnp.logical_not(half)
                cross_f = jnp.sum(jnp.where(sel, blk[ST_XF:ST_XF + 1, :] + blk[ST_LF:ST_LF + 1, :], 0.0))
                cross_b = jnp.sum(jnp.where(sel, blk[ST_XB:ST_XB + 1, :] + blk[ST_LB:ST_LB + 1, :], 0.0))
                intra_f = jnp.sum(blk[ST_IFA + h:ST_IFA + h + 1, :])
                intra_b = jnp.sum(blk[ST_IBA + h:ST_IBA + h + 1, :])
                head = 2 * pr + h
                misc = jnp.where(lane == MISC_DF + head, cross_f + intra_f, misc)
                misc = jnp.where(lane == MISC_DB + head, cross_b + intra_b, misc)
        for g in range(KV_HEADS):
            tot = dsink_ref[g * 8:(g + 1) * 8, :]
            for b in range(1, b_loc):
                tot = tot + dsink_ref[(b * KV_HEADS + g) * 8:(b * KV_HEADS + g + 1) * 8, :]
            for h in range(GROUP):
                misc = jnp.where(lane == MISC_SINK + GROUP * g + h, jnp.sum(tot[h:h + 1, 0:1]), misc)
        out_ref[ROW_MISC:ROW_MISC + 1, :] = misc

    return pl.pallas_call(body, name="pack_small", out_shape=SDS((SMALL_ROWS, d), F32))(acc2, acc1, ret_stats, dsink)


BIG = ("w_in", "w_out", "w_ffn_gate", "w_ffn_up", "w_ffn_down", "w_ple_proj", "w_ple_gate")
TRANSPOSED_OUTSIDE = ("w_in", "w_ffn_gate", "w_ffn_up")
TRANSPOSED_HERE = ("w_ple_proj",)
SMALL = ("ret_decay_fwd", "ret_decay_bwd", "ret_gn_gain", "attn_sink", "ln1_gain", "ln1_bias", "ln2_gain", "ln2_bias")
ORDER = ("w_in", "ret_decay_fwd", "ret_decay_bwd", "ret_gn_gain", "attn_sink", "w_out", "ln1_gain", "ln1_bias",
         "w_ffn_gate", "w_ffn_up", "w_ffn_down", "w_ple_proj", "w_ple_gate", "ln2_gain", "ln2_bias")


GATHER_ORDER = ("w_in", "w_out", "w_ffn_gate", "w_ffn_up", "w_ple_gate", "w_ple_proj", "w_ffn_down")


def _local_step(x2, p2, target2, fetch, publish, small, b_loc, me):
    d = x2.shape[1]
    lgf, lgb = _log_decay(small["ret_decay_fwd"], small["ret_decay_bwd"])
    lgf1, lgb1, sink1 = lgf.reshape(-1), lgb.reshape(-1), small["attn_sink"].reshape(-1)
    (w_in,) = fetch(("w_in",), ())
    u, xb = _in_proj(x2, w_in)
    r, y_pre = _retention_fwd(u, lgf1, lgb1, small["ret_gn_gain"], b_loc)
    a = _attention_fwd(u, sink1, b_loc)
    w_out, w_gate, w_up = fetch(("w_out", "w_ffn_gate", "w_ffn_up"), (r, a))
    z1, h1b, dact_dg, dact_du, act = _mix_ln1_ffn_up(
        r, a, x2, w_out, w_gate, w_up, small["ln1_gain"], small["ln1_bias"])
    w_pg, w_pe, w_down = fetch(("w_ple_gate", "w_ple_proj", "w_ffn_down"), (act,))
    dz2, dz2b, dsb, dpleb, dg, dup, acc2 = _ffn_down_ln2_loss(
        act, dact_dg, dact_du, h1b, p2, z1, target2, w_down, w_pg, w_pe,
        small["ln1_gain"], small["ln1_bias"], small["ln2_gain"], small["ln2_bias"])
    own = {}

    def grad(name, parts, rhs, after=()):
        whole, own[name] = _weight_grad("grad_" + name, me, parts, rhs, after)
        return whole

    t1 = publish("ffn_down", dict(w_ffn_down=grad("w_ffn_down", [act], dz2b),
                                  w_ple_proj=grad("w_ple_proj", [dpleb], p2),
                                  w_ple_gate=grad("w_ple_gate", [h1b], dsb)))
    t2 = publish("ffn_up", dict(w_ffn_gate=grad("w_ffn_gate", [dg], h1b, t1), w_ffn_up=grad("w_ffn_up", [dup], h1b)))
    dz1, dz1b, dr, da, acc1 = _dh1_ln1_bwd(dz2, dg, dup, dsb, z1, w_gate, w_up, w_pg, w_out, small["ln1_gain"], t2)
    t3 = publish("out", dict(w_out=grad("w_out", [r, a], dz1b)))
    dq, dk, dv, dgate, ret_stats = _retention_bwd(u, y_pre, dr, lgf1, lgb1, small["ret_gn_gain"], b_loc, t3)
    daq, dak, dav, dsink = _attention_bwd(u, da, sink1, b_loc)
    parts = [dq, dk, dv, dgate, daq, dak, dav]
    t4 = publish("in", dict(w_in=grad("w_in", parts, xb)))
    grad_x = _in_proj_bwd(dz1, parts, w_in, t4)
    small_part = _pack_small(acc2, acc1, ret_stats, dsink, b_loc, d)
    return grad_x, own, small_part


def kernel(x, p, w_in, ret_decay_fwd, ret_decay_bwd, ret_gn_gain, attn_sink, w_out, ln1_gain, ln1_bias, w_ffn_gate, w_ffn_up, w_ffn_down, w_ple_proj, w_ple_gate, ln2_gain, ln2_bias, loss_target, m_w_in, m_ret_decay_fwd, m_ret_decay_bwd, m_ret_gn_gain, m_attn_sink, m_w_out, m_ln1_gain, m_ln1_bias, m_w_ffn_gate, m_w_ffn_up, m_w_ffn_down, m_w_ple_proj, m_w_ple_gate, m_ln2_gain, m_ln2_bias, v_w_in, v_ret_decay_fwd, v_ret_decay_bwd, v_ret_gn_gain, v_attn_sink, v_w_out, v_ln1_gain, v_ln1_bias, v_w_ffn_gate, v_w_ffn_up, v_w_ffn_down, v_w_ple_proj, v_w_ple_gate, v_ln2_gain, v_ln2_bias):
    given = dict(locals())

    def strip(n, a):
        if n not in BIG:
            return a
        return a[0].T if n in TRANSPOSED_OUTSIDE else a[0]

    def restore(n, a):
        if n not in BIG:
            return a
        return (a.T if n in TRANSPOSED_OUTSIDE else a)[None]

    w = {n: strip(n, given[n]) for n in ORDER}
    m = {n: strip(n, given["m_" + n]) for n in ORDER}
    v = {n: strip(n, given["v_" + n]) for n in ORDER}
    b_loc, s, d = x.shape
    x2 = x.reshape(b_loc * s, d)
    p2 = p[0].reshape(b_loc * s, p.shape[-1])
    target2 = loss_target.reshape(b_loc * s, d)

    small = {n: w[n] for n in SMALL}
    me = (4 * lax.axis_index("x") + 2 * lax.axis_index("y") + lax.axis_index("c")).astype(jnp.int32).reshape(1)

    gathered = _prep_shards(me, {n: w[n] for n in BIG})
    gather = _split_copy_start("gather_start", [(gathered[n],) for n in GATHER_ORDER], _gather_copy)

    def fetch(names, after):
        which = [GATHER_ORDER.index(n) for n in names]
        got = _split_copy_wait("gather_wait_" + names[0], gather, which, _gather_copy, list(after))
        return [item[0] for item in got]

    scatters = []

    def publish(tag, products):
        names = list(products)
        items = [(products[n], lax.empty((N_DEV - 1, products[n].shape[0] // N_DEV, products[n].shape[1]), BF16))
                 for n in names]
        started = _split_copy_start("scatter_start_" + tag, items, _scatter_copy)
        scatters.append((tag, names, started))
        return (started["token"],)

    grad_x, own, small_part = _local_step(x2, p2, target2, fetch, publish, small, b_loc, me)

    out_g, out_d, out_m, out_v = {}, {}, {}, {}
    after = [grad_x]
    for tag, names, started in scatters:
        landed = _split_copy_wait("scatter_wait_" + tag, started, list(range(len(names))), _scatter_copy, after)
        for n, (_, recv) in zip(names, landed):
            out_g[n], out_d[n], out_m[n], out_v[n] = _reduce_adamw(
                n, own[n], recv, w[n], m[n], v[n], n in TRANSPOSED_HERE)
        after = [out_v[names[-1]]]
        if tag == "out":
            loss, sg, sd, sm, sv = _small_adamw(
                _small_all_reduce(small_part), small, {n: m[n] for n in SMALL}, {n: v[n] for n in SMALL})
            for dst, src in ((out_g, sg), (out_d, sd), (out_m, sm), (out_v, sv)):
                dst.update(src)
            after.append(sv["ln2_bias"])

    outs = [loss[0, 0], grad_x.reshape(x.shape)]
    for group in (out_g, out_d, out_m, out_v):
        outs += [restore(n, group[n]) for n in ORDER]
    return tuple(outs)
```

```python
import functools

import jax
import jax.numpy as jnp
from jax import lax
from jax.experimental import pallas as pl
from jax.experimental.pallas import tpu as pltpu

F32, BF16 = jnp.float32, jnp.bfloat16
SDS = jax.ShapeDtypeStruct
MESH = pl.DeviceIdType.MESH

N_DEV = 8
HEAD_DIM = 64
RET_HEADS = 8
ATTN_HEADS = 8
KV_HEADS = 2
GROUP = ATTN_HEADS // KV_HEADS
RET_W = RET_HEADS * HEAD_DIM
ATT_W = ATTN_HEADS * HEAD_DIM
KV_W = KV_HEADS * HEAD_DIM
LANES = 128
CHUNK = 128
BLOCK = 128
Q_SCALE = HEAD_DIM ** -0.5
ALPHA = 2.0 ** 0.25
LN_EPS = 1e-5
GN_EPS = 1e-5
NEG_INF = -1e30
C_RQ, C_RK, C_RV, C_RG = 0, RET_W, 2 * RET_W, 3 * RET_W
C_AQ = 4 * RET_W
C_AK = C_AQ + ATT_W
C_AV = C_AK + KV_W
IN_W = C_AV + KV_W

ADAM_LR = 0.001
ADAM_B1 = 0.9
ADAM_B2 = 0.999
ADAM_EPS = 1e-08
ADAM_WD = 0.01
ADAM_STEP = 10

VMEM_LIMIT = 48 * 1024 * 1024
MATMUL_ROWS = 512
EPILOGUE_ROWS = 256
SUB_ROWS = 256
SMALL_ROWS = 16
ROW_LN1G, ROW_LN1B, ROW_LN2G, ROW_LN2B, ROW_LOSS, ROW_GN, ROW_MISC = 0, 1, 2, 3, 4, 5, 6
MISC_DF, MISC_DB, MISC_SINK = 0, 8, 16


def _dot_nn(a, b):
    return lax.dot_general(a, b, (((1,), (0,)), ((), ())), preferred_element_type=F32)


def _dot_nt(a, b):
    return lax.dot_general(a, b, (((1,), (1,)), ((), ())), preferred_element_type=F32)


def _dot_tn(a, b):
    return lax.dot_general(a, b, (((0,), (0,)), ((), ())), preferred_element_type=F32)


def _params(sem=None, vmem=VMEM_LIMIT):
    kw = {"vmem_limit_bytes": vmem}
    if sem is not None:
        kw["dimension_semantics"] = sem
    return pltpu.CompilerParams(**kw)


def _row_tile(t, want=512):
    tm = want
    while t % tm:
        tm //= 2
    return tm


def _sigmoid(x):
    return 1.0 / (1.0 + jnp.exp(-x))


def _layer_norm_stats(z):
    mu = jnp.mean(z, axis=1, keepdims=True)
    d = z - mu
    var = jnp.mean(d * d, axis=1, keepdims=True)
    rstd = lax.rsqrt(var + LN_EPS)
    return d * rstd, rstd


def _layer_norm_bwd(dxh, xhat, rstd):
    m1 = jnp.mean(dxh, axis=1, keepdims=True)
    m2 = jnp.mean(dxh * xhat, axis=1, keepdims=True)
    return rstd * (dxh - m1 - xhat * m2)


def _prep_shards(me, shards):
    names = list(shards)

    def body(me_ref, *refs):
        for name, src, dst in zip(names, refs[:len(names)], refs[len(names):]):
            val = src[...]
            dst[...] = (val.T if name in TRANSPOSED_HERE else val).astype(BF16)

    shape = lambda n, a: a.shape[::-1] if n in TRANSPOSED_HERE else a.shape
    shapes = [shape(n, shards[n]) for n in names]
    out = pl.pallas_call(
        body, name="prep_shards",
        grid_spec=pltpu.PrefetchScalarGridSpec(
            num_scalar_prefetch=1, grid=(1,),
            in_specs=[pl.BlockSpec(shards[n].shape, lambda i, me_ref: (0, 0)) for n in names],
            out_specs=[pl.BlockSpec(s, lambda i, me_ref: (me_ref[0], 0)) for s in shapes]),
        out_shape=[SDS((N_DEV * s[0], s[1]), BF16) for s in shapes], compiler_params=_params(("arbitrary",)),
    )(me, *[shards[n] for n in names])
    return dict(zip(names, out))


def _mesh_pos():
    return lax.axis_index("x"), lax.axis_index("y"), lax.axis_index("c")


HBM_SPEC = pl.BlockSpec(memory_space=pltpu.HBM)
SEM_SPEC = pl.BlockSpec(memory_space=pltpu.SEMAPHORE)
ANY_SPEC = pl.BlockSpec(memory_space=pl.ANY)
SIDE_EFFECT = pltpu.SideEffectType.DATAFLOW_SIDE_EFFECTING
PEER_SEMS = pltpu.SemaphoreType.DMA((N_DEV - 1,))


def _in_hbm(a):
    return pltpu.with_memory_space_constraint(a, pltpu.HBM)


def _split_copy_start(name, items, copies):
    n = len(items)
    flat = [a for it in items for a in it]
    k = len(flat)

    def body(*refs):
        arr, sems = list(refs[:k]), refs[k:k + 2 * n]
        for i, it in enumerate(items):
            mine = [arr.pop(0) for _ in it]
            for m in range(1, N_DEV):
                cp = copies[i](m, mine, sems[i].at[m - 1], sems[n + i].at[m - 1])
                if cp is not None:
                    cp.start()
        token = refs[-1]
        token[...] = jnp.zeros_like(token)

    res = pl.pallas_call(
        body, name=name,
        out_shape=[PEER_SEMS] * (2 * n) + [pltpu.HBM(a.shape, a.dtype) for a in flat] + [SDS((8, LANES), F32)],
        in_specs=[HBM_SPEC] * k,
        out_specs=[SEM_SPEC] * (2 * n) + [HBM_SPEC] * k + [pl.BlockSpec(memory_space=pltpu.VMEM)],
        input_output_aliases={j: 2 * n + j for j in range(k)},
        compiler_params=pltpu.CompilerParams(has_side_effects=SIDE_EFFECT),
    )(*[_in_hbm(a) for a in flat])
    thru, out_items = list(res[2 * n:2 * n + k]), []
    for it in items:
        out_items.append(tuple(thru.pop(0) for _ in it))
    return dict(send=res[:n], recv=res[n:2 * n], items=out_items, token=res[-1], copies=copies)


def _split_copy_wait(name, started, which, after):
    items = [started["items"][i] for i in which]
    copies = [started["copies"][i] for i in which]
    n = len(items)
    flat = [a for it in items for a in it]
    k = len(flat)

    def body(*refs):
        arr, sems = list(refs[:k]), refs[k:k + 2 * n]
        for i, it in enumerate(items):
            mine = [arr.pop(0) for _ in it]
            for m in range(1, N_DEV):
                cp = copies[i](m, mine, sems[i].at[m - 1], sems[n + i].at[m - 1])
                if cp is not None:
                    cp.wait_send()
                    cp.wait_recv()

    res = pl.pallas_call(
        body, name=name,
        out_shape=[pltpu.HBM(a.shape, a.dtype) for a in flat],
        in_specs=[HBM_SPEC] * k + [SEM_SPEC] * (2 * n) + [ANY_SPEC] * len(after),
        out_specs=[HBM_SPEC] * k,
        input_output_aliases={j: j for j in range(k)},
        compiler_params=pltpu.CompilerParams(has_side_effects=SIDE_EFFECT),
    )(*flat, *[started["send"][i] for i in which], *[started["recv"][i] for i in which], *[_in_hbm(a) for a in after])
    thru, out_items = list(res), []
    for it in items:
        out_items.append(tuple(thru.pop(0) for _ in it))
    return out_items


def _gather_copy(m, refs, send_sem, recv_sem):
    (land_ref,) = refs
    r = land_ref.shape[0] // N_DEV
    mine = land_ref.at[pl.ds(pl.multiple_of(_peer_index(0) * r, 8), r), :]
    return pltpu.make_async_remote_copy(src_ref=mine, dst_ref=mine, send_sem=send_sem, recv_sem=recv_sem,
                                        device_id=_peer(m), device_id_type=MESH)


def _gather_copy_near(m, refs, send_sem, recv_sem):
    return _gather_copy(m, refs, send_sem, recv_sem) if m == 1 or m % 2 == 0 else None


def _gather_copy_pass(m, refs, send_sem, recv_sem):
    if m == 1 or m % 2 == 0:
        return None
    (land_ref,) = refs
    r = land_ref.shape[0] // N_DEV
    block = land_ref.at[pl.ds(pl.multiple_of(_peer_index(m ^ 1) * r, 8), r), :]
    return pltpu.make_async_remote_copy(src_ref=block, dst_ref=block, send_sem=send_sem, recv_sem=recv_sem,
                                        device_id=_peer(1), device_id_type=MESH)


def _small_copy(m, refs, send_sem, recv_sem):
    part_ref, land_ref = refs
    return pltpu.make_async_remote_copy(src_ref=part_ref, dst_ref=land_ref.at[m - 1], send_sem=send_sem,
                                        recv_sem=recv_sem, device_id=_peer(m), device_id_type=MESH)


def _scatter_copy(m, refs, send_sem, recv_sem):
    buf_ref, land_ref = refs
    r = buf_ref.shape[0] // N_DEV
    src = buf_ref.at[pl.ds(pl.multiple_of(_peer_index(m) * r, 8), r), :]
    return pltpu.make_async_remote_copy(src_ref=src, dst_ref=land_ref.at[m - 1], send_sem=send_sem,
                                        recv_sem=recv_sem, device_id=_peer(m), device_id_type=MESH)


def _peer(m):
    x, y, c = _mesh_pos()
    bx, by, bc = (m >> 2) & 1, (m >> 1) & 1, m & 1
    return (x ^ bx if bx else x, y ^ by if by else y, c ^ bc if bc else c)


def _peer_index(m):
    x, y, c = _mesh_pos()
    return (4 * x + 2 * y + c) ^ m


SMALL_PLACE = {
    "ln1_gain": (ROW_LN1G, 0), "ln1_bias": (ROW_LN1B, 0), "ln2_gain": (ROW_LN2G, 0), "ln2_bias": (ROW_LN2B, 0),
    "ret_gn_gain": (ROW_GN, 0), "ret_decay_fwd": (ROW_MISC, MISC_DF), "ret_decay_bwd": (ROW_MISC, MISC_DB),
    "attn_sink": (ROW_MISC, MISC_SINK)}


def _small_adamw(me, part, landed, w, m, v):
    d = part.shape[1]
    names = list(SMALL_PLACE)
    k = len(names)

    def body(*refs):
        me_ref, part_ref, land_ref = refs[:3]
        refs = refs[2:]
        w_refs, m_refs, v_refs = refs[1:1 + k], refs[1 + k:1 + 2 * k], refs[1 + 2 * k:1 + 3 * k]
        outs = refs[1 + 3 * k:1 + 7 * k + 1]
        tot_ref = refs[-1]
        loss_ref, g_refs, dl_refs = outs[0], outs[1:1 + k], outs[1 + k:1 + 2 * k]
        nm_refs, nv_refs = outs[1 + 2 * k:1 + 3 * k], outs[1 + 3 * k:1 + 4 * k]
        tot = jnp.zeros(part_ref.shape, F32)
        for dev in range(N_DEV):
            j = dev ^ me_ref[0]
            tot = tot + jnp.where(j == 0, part_ref[...], land_ref[jnp.maximum(j, 1) - 1])
        tot_ref[...] = tot
        loss_ref[...] = (0.5 / d) * jnp.sum(tot_ref[ROW_LOSS:ROW_LOSS + 1, :], axis=1, keepdims=True)
        for i, name in enumerate(names):
            row, lo = SMALL_PLACE[name]
            wv = w_refs[i][...]
            g = tot_ref[row:row + 1, lo:lo + wv.shape[1]]
            if name.startswith("ret_decay"):
                p2 = jnp.exp2(wv)
                g = g * (-p2 * jnp.log(2.0) / (1.0 - p2))
            g_refs[i][...] = g
            _adamw_store(g, wv, m_refs[i][...], v_refs[i][...], dl_refs[i], nm_refs[i], nv_refs[i])

    shapes = [SDS(w[n].shape, F32) for n in names]
    vm = pl.BlockSpec(memory_space=pltpu.VMEM)
    res = pl.pallas_call(
        body, name="small_adamw", out_shape=[SDS((1, 1), F32)] + shapes * 4,
        in_specs=[_smem_spec()] + [vm] * (2 + 3 * k), out_specs=[vm] * (1 + 4 * k),
        scratch_shapes=[pltpu.VMEM(part.shape, F32)],
    )(me, part, landed, *[w[n] for n in names], *[m[n] for n in names], *[v[n] for n in names])
    groups = [dict(zip(names, res[1 + j * k:1 + (j + 1) * k])) for j in range(4)]
    return (res[0], *groups)


def _adamw_store(g, w, m, v, dl_ref, nm_ref, nv_ref):
    m = ADAM_B1 * m + (1.0 - ADAM_B1) * g
    v = ADAM_B2 * v + (1.0 - ADAM_B2) * (g * g)
    m_hat = m / (1.0 - ADAM_B1 ** ADAM_STEP)
    v_hat = v / (1.0 - ADAM_B2 ** ADAM_STEP)
    dl_ref[...] = -ADAM_LR * (m_hat / (jnp.sqrt(v_hat) + ADAM_EPS) + ADAM_WD * w)
    nm_ref[...] = m
    nv_ref[...] = v


def _reduce_adamw(name, own, recv, w, m, v, transposed):
    rows, n = own.shape
    steps = 1 if transposed or rows % 32 else 4
    rb = rows // steps

    def body(own_ref, recv_ref, w_ref, m_ref, v_ref, g_ref, dl_ref, nm_ref, nv_ref):
        g = own_ref[...]
        for k in range(N_DEV - 1):
            g = g + recv_ref[k].astype(F32)
        if transposed:
            g = g.T
        g_ref[...] = g
        _adamw_store(g, w_ref[...], m_ref[...], v_ref[...], dl_ref, nm_ref, nv_ref)

    blk = pl.BlockSpec(w.shape if transposed else (rb, n), lambda i: (i, 0))
    out = SDS(w.shape, F32)
    return pl.pallas_call(
        body, name="adamw_" + name, grid=(steps,),
        in_specs=[pl.BlockSpec((rb, n), lambda i: (i, 0)), pl.BlockSpec((N_DEV - 1, rb, n), lambda i: (0, i, 0)),
                  blk, blk, blk],
        out_specs=[blk] * 4, out_shape=[out] * 4, compiler_params=_params(("parallel",)),
    )(own, recv, w, m, v)


def _row_spec(tm, width):
    return pl.BlockSpec((tm, width), lambda i: (i, 0))


def _full_spec(shape):
    return pl.BlockSpec(shape, lambda i: (0,) * len(shape))


_acc_spec = _full_spec


def _sub_rows(tm):
    step = min(SUB_ROWS, tm)
    return [(lo, lo + step) for lo in range(0, tm, step)]


def _in_proj(x2, wt_in):
    t, d = x2.shape
    u_w = wt_in.shape[0]
    tm = _row_tile(t, MATMUL_ROWS)

    def body(x_ref, w_ref, u_ref, xb_ref):
        xb = x_ref[...].astype(BF16)
        xb_ref[...] = xb
        u_ref[...] = _dot_nt(xb, w_ref[...]).astype(BF16)

    return pl.pallas_call(
        body, name="in_proj", grid=(t // tm,),
        in_specs=[_row_spec(tm, d), _full_spec(wt_in.shape)],
        out_specs=[_row_spec(tm, u_w), _row_spec(tm, d)],
        out_shape=[SDS((t, u_w), BF16), SDS((t, d), BF16)],
        compiler_params=_params(("parallel",)),
    )(x2, wt_in)


def _col_halves(f):
    n = f // LANES
    k = (n + 1) // 2 * LANES
    return [(0, k), (k, f)] if k < f else [(0, f)]


def _mix_ln1_ffn_up(r, a, x2, w_out, wt_gate, wt_up, g1, b1):
    t, d = x2.shape
    f = wt_gate.shape[0]
    tm = _row_tile(t, EPILOGUE_ROWS)

    def body(r_ref, a_ref, x_ref, wo_ref, wg_ref, wu_ref, g_ref, b_ref, z_ref, hb_ref, dg_ref, du_ref, act_ref):
        mix = _dot_nn(r_ref[...], wo_ref[0:RET_W, :]) + _dot_nn(a_ref[...], wo_ref[RET_W:RET_W + ATT_W, :])
        z = ALPHA * x_ref[...] + mix
        xhat, _ = _layer_norm_stats(z)
        z_ref[...] = z
        h = (xhat * g_ref[...] + b_ref[...]).astype(BF16)
        hb_ref[...] = h
        for lo, hi in _col_halves(f):
            g = _dot_nt(h, wg_ref[lo:hi, :])
            u = _dot_nt(h, wu_ref[lo:hi, :])
            sg = _sigmoid(g)
            silu = g * sg
            dg_ref[:, lo:hi] = (u * (sg * (1.0 + g * (1.0 - sg)))).astype(BF16)
            du_ref[:, lo:hi] = silu.astype(BF16)
            act_ref[:, lo:hi] = (silu * u).astype(BF16)

    wide, narrow = _row_spec(tm, f), _row_spec(tm, d)
    return pl.pallas_call(
        body, name="mix_ln1_ffn_up", grid=(t // tm,),
        in_specs=[_row_spec(tm, RET_W), _row_spec(tm, ATT_W), narrow, _resident_spec(w_out.shape),
                  _resident_spec(wt_gate.shape), _resident_spec(wt_up.shape), _full_spec(g1.shape),
                  _full_spec(b1.shape)],
        out_specs=[narrow, narrow, wide, wide, wide],
        out_shape=[SDS((t, d), F32), SDS((t, d), BF16)] + [SDS((t, f), BF16)] * 3,
        compiler_params=_params(("parallel",)),
    )(r, a, x2, w_out, wt_gate, wt_up, g1, b1)


def _ffn_down_ln2_loss(act, dact_dg, dact_du, h1b, p2, z1, target, w_down, w_pg, wt_pe, g1, b1, g2, b2):
    t, d = z1.shape
    f = act.shape[1]
    pdim = p2.shape[1]
    tm = _row_tile(t, EPILOGUE_ROWS)

    def body(act_ref, fg_ref, fu_ref, hb_ref, p_ref, z1_ref, tgt_ref, wd_ref, wpg_ref, wpe_ref, g1_ref, b1_ref,
             g2_ref, b2_ref, dz_ref, dzb_ref, ds_ref, dple_ref, dg_ref, du_ref, acc_ref):
        @pl.when(pl.program_id(0) == 0)
        def _():
            acc_ref[...] = jnp.zeros_like(acc_ref)

        for lo, hi in _sub_rows(tm):
            xhat1, _ = _layer_norm_stats(z1_ref[lo:hi, :])
            h1 = xhat1 * g1_ref[...] + b1_ref[...]
            ffn = _dot_nn(act_ref[lo:hi, :], wd_ref[...])
            pg = _sigmoid(_dot_nn(hb_ref[lo:hi, :], wpg_ref[...]))
            ple = _dot_nt(p_ref[lo:hi, :].astype(BF16), wpe_ref[...])
            z2 = ALPHA * h1 + ffn + pg * ple
            xhat2, rstd2 = _layer_norm_stats(z2)
            err = xhat2 * g2_ref[...] + b2_ref[...] - tgt_ref[lo:hi, :]
            dy = err * (1.0 / d)
            dz = _layer_norm_bwd(dy * g2_ref[...], xhat2, rstd2)
            dzb = dz.astype(BF16)
            dz_ref[lo:hi, :] = dz
            dzb_ref[lo:hi, :] = dzb
            ds_ref[lo:hi, :] = (dz * ple * pg * (1.0 - pg)).astype(BF16)
            dple_ref[lo:hi, :] = (dz * pg).astype(BF16)
            acc_ref[0:1, :] += jnp.sum(err * err, axis=0, keepdims=True)
            acc_ref[1:2, :] += jnp.sum(dy * xhat2, axis=0, keepdims=True)
            acc_ref[2:3, :] += jnp.sum(dy, axis=0, keepdims=True)
            for c0, c1 in _col_halves(f):
                da = _dot_nt(dzb, wd_ref[c0:c1, :])
                dg_ref[lo:hi, c0:c1] = (da * fg_ref[lo:hi, c0:c1].astype(F32)).astype(BF16)
                du_ref[lo:hi, c0:c1] = (da * fu_ref[lo:hi, c0:c1].astype(F32)).astype(BF16)

    vec = _full_spec(g1.shape)
    wide, narrow = _row_spec(tm, f), _row_spec(tm, d)
    return pl.pallas_call(
        body, name="ffn_down_ln2_loss", grid=(t // tm,),
        in_specs=[wide, wide, wide, narrow, _row_spec(tm, pdim), narrow, narrow,
                  _full_spec(w_down.shape), _full_spec(w_pg.shape), _full_spec(wt_pe.shape), vec, vec, vec, vec],
        out_specs=[narrow] * 4 + [wide, wide, _acc_spec((8, d))],
        out_shape=[SDS((t, d), F32), SDS((t, d), BF16), SDS((t, d), BF16), SDS((t, d), BF16),
                   SDS((t, f), BF16), SDS((t, f), BF16), SDS((8, d), F32)],
        compiler_params=_params(("arbitrary",), 56 * 1024 * 1024),
    )(act, dact_dg, dact_du, h1b, p2, z1, target, w_down, w_pg, wt_pe, g1, b1, g2, b2)


def _after(after, body):
    k = len(after)
    return (lambda *refs: body(*refs[k:])), [ANY_SPEC] * k


def _resident_spec(shape):
    return pl.BlockSpec(shape, lambda i: (0,) * len(shape), pipeline_mode=pl.Buffered(1))


def _dh1_ln1_bwd(dz2, dg, dup, dsb, z1, wt_gate, wt_up, w_pg, w_out, g1, after=()):
    t, d = dz2.shape
    f = dg.shape[1]
    tm = _row_tile(t, EPILOGUE_ROWS)

    def body(dz_ref, dg_ref, du_ref, ds_ref, z1_ref, wg_ref, wu_ref, wpg_ref, wo_ref, g1_ref,
             dz1_ref, dz1b_ref, dr_ref, da_ref, acc_ref):
        @pl.when(pl.program_id(0) == 0)
        def _():
            acc_ref[...] = jnp.zeros_like(acc_ref)

        for lo, hi in _sub_rows(tm):
            dh = (ALPHA * dz_ref[lo:hi, :] + _dot_nn(dg_ref[lo:hi, :], wg_ref[...])
                  + _dot_nn(du_ref[lo:hi, :], wu_ref[...]) + _dot_nt(ds_ref[lo:hi, :], wpg_ref[...]))
            xhat, rstd = _layer_norm_stats(z1_ref[lo:hi, :])
            dz1 = _layer_norm_bwd(dh * g1_ref[...], xhat, rstd)
            dz1b = dz1.astype(BF16)
            dz1_ref[lo:hi, :] = dz1
            dz1b_ref[lo:hi, :] = dz1b
            acc_ref[0:1, :] += jnp.sum(dh * xhat, axis=0, keepdims=True)
            acc_ref[1:2, :] += jnp.sum(dh, axis=0, keepdims=True)
            dr_ref[lo:hi, :] = _dot_nt(dz1b, wo_ref[0:RET_W, :]).astype(BF16)
            da_ref[lo:hi, :] = _dot_nt(dz1b, wo_ref[RET_W:RET_W + ATT_W, :]).astype(BF16)

    body, lead = _after(after, body)
    return pl.pallas_call(
        body, name="dh1_ln1_bwd", grid=(t // tm,),
        in_specs=lead + [_row_spec(tm, d), _row_spec(tm, f), _row_spec(tm, f), _row_spec(tm, d), _row_spec(tm, d),
                         _resident_spec(wt_gate.shape), _resident_spec(wt_up.shape), _resident_spec(w_pg.shape),
                         _resident_spec(w_out.shape), _full_spec(g1.shape)],
        out_specs=[_row_spec(tm, d), _row_spec(tm, d), _row_spec(tm, RET_W), _row_spec(tm, ATT_W), _acc_spec((8, d))],
        out_shape=[SDS((t, d), F32), SDS((t, d), BF16), SDS((t, RET_W), BF16), SDS((t, ATT_W), BF16),
                   SDS((8, d), F32)],
        compiler_params=_params(("arbitrary",)),
    )(*after, dz2, dg, dup, dsb, z1, wt_gate, wt_up, w_pg, w_out, g1)


def _in_proj_bwd(dz1, parts, wt_in, after=()):
    t, d = dz1.shape
    tm = _row_tile(t, MATMUL_ROWS)
    widths = [p.shape[1] for p in parts]

    def body(*refs):
        dz_ref, part_refs, w_ref, dx_ref = refs[0], refs[1:1 + len(parts)], refs[-2], refs[-1]
        acc = ALPHA * dz_ref[...]
        lo = 0
        for p_ref, w in zip(part_refs, widths):
            acc = acc + _dot_nn(p_ref[...], w_ref[lo:lo + w, :])
            lo += w
        dx_ref[...] = acc

    body, lead = _after(after, body)
    return pl.pallas_call(
        body, name="in_proj_bwd", grid=(t // tm,),
        in_specs=lead + [_row_spec(tm, d)] + [_row_spec(tm, w) for w in widths] + [_full_spec(wt_in.shape)],
        out_specs=_row_spec(tm, d), out_shape=SDS((t, d), F32),
        compiler_params=_params(("parallel",)),
    )(*after, dz1, *parts, wt_in)


def _weight_grad(name, me, parts, rhs, after=()):
    t, n = rhs.shape
    widths = [p.shape[1] for p in parts]
    rows = sum(widths)
    own_rows = rows // N_DEV
    tk = _row_tile(t, MATMUL_ROWS)
    step = 256

    def body(*refs):
        me_ref, part_refs, rhs_ref = refs[0], refs[1:1 + len(parts)], refs[1 + len(parts)]
        full_ref, own_ref, acc = refs[-3], refs[-2], refs[-1]
        i = pl.program_id(0)

        @pl.when(i == 0)
        def _():
            acc[...] = jnp.zeros_like(acc)

        b = rhs_ref[...].astype(BF16)
        lo = 0
        for p_ref, w in zip(part_refs, widths):
            for c0 in range(0, w, step):
                c1 = min(c0 + step, w)
                acc[lo + c0:lo + c1, :] += _dot_tn(p_ref[:, c0:c1].astype(BF16), b)
            lo += w

        @pl.when(i == pl.num_programs(0) - 1)
        def _():
            full_ref[...] = acc[...].astype(BF16)
            own_ref[...] = acc[pl.ds(pl.multiple_of(me_ref[0] * own_rows, 8), own_rows), :]

    body, lead = _after(after, body)
    return pl.pallas_call(
        body, name=name, grid=(t // tk,),
        in_specs=lead + [_smem_spec()] + [_row_spec(tk, w) for w in widths] + [_row_spec(tk, n)],
        out_specs=[_full_spec((rows, n)), _full_spec((own_rows, n))],
        out_shape=[SDS((rows, n), BF16), SDS((own_rows, n), F32)],
        scratch_shapes=[pltpu.VMEM((rows, n), F32)],
        compiler_params=_params(("arbitrary",)),
    )(*after, me, *parts, rhs)


def _log_decay(decay_f, decay_b):
    def body(f_ref, b_ref, lf_ref, lb_ref):
        lf_ref[...] = jnp.log1p(-jnp.exp2(f_ref[...]))
        lb_ref[...] = jnp.log1p(-jnp.exp2(b_ref[...]))

    return pl.pallas_call(body, name="log_decay", out_shape=[SDS(decay_f.shape, F32)] * 2)(decay_f, decay_b)


def _chunk(ref, n):
    return ref[pl.ds(pl.multiple_of(n * CHUNK, CHUNK), CHUNK), :]


def _group_sum(is_a, v):
    sa = jnp.sum(jnp.where(is_a, v, 0.0), axis=1, keepdims=True)
    sb = jnp.sum(jnp.where(is_a, 0.0, v), axis=1, keepdims=True)
    return jnp.where(is_a, sa, sb)


def _seq_spec(s, col_block):
    return pl.BlockSpec((s, LANES), lambda b, h: (b, col_block + h))


def _smem_spec():
    return pl.BlockSpec(memory_space=pltpu.SMEM)


RET_UNROLL = 4


def _chunk_loop(n_chunks, body, init):
    u = RET_UNROLL if n_chunks % RET_UNROLL == 0 else 1

    def trip(i, carry):
        for j in range(u):
            carry = body(i * u + j, carry)
        return carry

    return lax.fori_loop(0, n_chunks // u, trip, init)


def _stacked_tables(lgf_ref, lgb_ref, pair):
    lane = lax.broadcasted_iota(jnp.int32, (1, LANES), 1)
    is_a = lane < HEAD_DIM
    lgf = jnp.where(is_a, lgf_ref[2 * pair], lgf_ref[2 * pair + 1])
    lgb = jnp.where(is_a, lgb_ref[2 * pair], lgb_ref[2 * pair + 1])
    row = lax.broadcasted_iota(jnp.int32, (CHUNK, 1), 0).astype(F32)
    kdec_f, qdec_f = jnp.exp(lgf * (CHUNK - 1.0 - row)), jnp.exp(lgf * (row + 1.0))
    kdec_b, qdec_b = jnp.exp(lgb * row), jnp.exp(lgb * (CHUNK - row))
    tab = dict(
        is_a=is_a, row=row, lam_f=jnp.exp(lgf * CHUNK), lam_b=jnp.exp(lgb * CHUNK),
        kdec=jnp.concatenate([kdec_f, kdec_b], axis=1), qdec=jnp.concatenate([qdec_f, qdec_b], axis=1),
        qexp=jnp.concatenate([jnp.broadcast_to(row + 1.0, (CHUNK, LANES)),
                              jnp.broadcast_to(CHUNK - row, (CHUNK, LANES))], axis=1),
        kexp=jnp.concatenate([jnp.broadcast_to(CHUNK - 1.0 - row, (CHUNK, LANES)),
                              jnp.broadcast_to(row, (CHUNK, LANES))], axis=1),
    )
    r = lax.broadcasted_iota(jnp.int32, (2 * LANES, LANES), 0)
    c = lax.broadcasted_iota(jnp.int32, (2 * LANES, LANES), 1)
    tab["diag2"] = ((r & (LANES - 1)) < HEAD_DIM) == (c < HEAD_DIM)
    i2 = lax.broadcasted_iota(jnp.int32, (2 * CHUNK, CHUNK), 0)
    j = lax.broadcasted_iota(jnp.int32, (2 * CHUNK, CHUNK), 1)
    head_b = i2 >= CHUNK
    diff = ((i2 & (CHUNK - 1)) - j).astype(F32)
    up, dn = jnp.maximum(diff, 0.0), jnp.maximum(-diff, 0.0)
    lgf2 = jnp.where(head_b, lgf_ref[2 * pair + 1], lgf_ref[2 * pair])
    lgb2 = jnp.where(head_b, lgb_ref[2 * pair + 1], lgb_ref[2 * pair])
    ef = jnp.where(diff >= 0, jnp.exp(lgf2 * up), 0.0)
    eb = jnp.where(diff <= 0, jnp.exp(lgb2 * dn), 0.0)
    tab["d2"] = ef + eb
    tab["df2"] = ef * up
    tab["db2"] = eb * dn
    return tab


def _stack_pair(is_a, x):
    zero = jnp.zeros_like(x)
    return jnp.concatenate([jnp.where(is_a, x, zero), jnp.where(is_a, zero, x)], axis=0)


def _unstack_pair(is_a, x2):
    return jnp.where(is_a, x2[0:CHUNK, :], x2[CHUNK:2 * CHUNK, :])


def _both_ways(x, dec):
    return (jnp.concatenate([x, x], axis=1) * dec).astype(BF16)


def _scan_states(n_chunks, st, up_rows, up_lam, down_rows, down_lam):
    zero = jnp.zeros((LANES, LANES), F32)

    def up(n, r):
        new = st[n, up_rows, :]
        st[n, up_rows, :] = r
        return r * up_lam + new

    def down(s, r):
        n = n_chunks - 1 - s
        new = st[n, down_rows, :]
        st[n, down_rows, :] = r
        return r * down_lam + new

    lax.fori_loop(0, n_chunks, up, zero)
    lax.fori_loop(0, n_chunks, down, zero)


FWD_ROWS, BWD_ROWS = pl.ds(0, LANES), pl.ds(LANES, LANES)


def _retention_fwd(u, lgf, lgb, gn_gain, b_loc):
    t = u.shape[0]
    s = t // b_loc
    n_chunks = s // CHUNK
    pairs = RET_HEADS // 2

    def body(lgf_ref, lgb_ref, q_ref, k_ref, v_ref, g_ref, gain_ref, r_ref, y_ref, st):
        tab = _stacked_tables(lgf_ref, lgb_ref, pl.program_id(1))
        is_a = tab["is_a"]

        def kv_body(n, _):
            k8 = _chunk(k_ref, n).astype(F32) * Q_SCALE
            st[n] = jnp.where(tab["diag2"], _dot_tn(_both_ways(k8, tab["kdec"]), _chunk(v_ref, n)), 0.0)
            return 0

        _chunk_loop(n_chunks, kv_body, 0)
        _scan_states(n_chunks, st, FWD_ROWS, tab["lam_f"], BWD_ROWS, tab["lam_b"])

        def out_body(n, _):
            q = _chunk(q_ref, n)
            k8 = (_chunk(k_ref, n).astype(F32) * Q_SCALE).astype(BF16)
            v = _chunk(v_ref, n)
            p2 = (_dot_nt(_stack_pair(is_a, q), k8) * tab["d2"]).astype(BF16)
            y = _unstack_pair(is_a, _dot_nn(p2, v))
            y = y + _dot_nn(_both_ways(q.astype(F32), tab["qdec"]), st[n].astype(BF16))
            rows = pl.ds(pl.multiple_of(n * CHUNK, CHUNK), CHUNK)
            y_ref[rows, :] = y
            mu = _group_sum(is_a, y) * (1.0 / HEAD_DIM)
            dlt = y - mu
            var = _group_sum(is_a, dlt * dlt) * (1.0 / HEAD_DIM)
            xhat = dlt * lax.rsqrt(var + GN_EPS)
            gate = _chunk(g_ref, n).astype(F32)
            r_ref[rows, :] = (xhat * gain_ref[...] * gate * _sigmoid(gate)).astype(BF16)
            return 0

        _chunk_loop(n_chunks, out_body, 0)

    lane_blk = lambda c0: _seq_spec(s, c0 // LANES)
    return pl.pallas_call(
        body, name="retention_fwd", grid=(b_loc, pairs),
        in_specs=[_smem_spec(), _smem_spec(), lane_blk(C_RQ), lane_blk(C_RK), lane_blk(C_RV), lane_blk(C_RG),
                  pl.BlockSpec((1, LANES), lambda b, h: (0, h))],
        out_specs=[_seq_spec(s, 0), _seq_spec(s, 0)],
        out_shape=[SDS((t, RET_W), BF16), SDS((t, RET_W), F32)],
        scratch_shapes=[pltpu.VMEM((n_chunks, 2 * LANES, LANES), F32)],
        compiler_params=_params(("parallel", "parallel")),
    )(lgf, lgb, u, u, u, u, gn_gain)


ST_GAIN, ST_XF, ST_XB, ST_IFA, ST_IFB, ST_IBA, ST_IBB, ST_LF, ST_LB = 0, 1, 2, 3, 4, 5, 6, 8, 9
ST_ROWS = 16


def _retention_bwd(u, y_pre, dr, lgf, lgb, gn_gain, b_loc, after=()):
    t = u.shape[0]
    s = t // b_loc
    n_chunks = s // CHUNK
    pairs = RET_HEADS // 2

    def body(lgf_ref, lgb_ref, q_ref, k_ref, v_ref, g_ref, y_ref, dr_ref, gain_ref,
             dq_ref, dk_ref, dv_ref, dg_ref, st_ref, st, gr, dy_s):
        tab = _stacked_tables(lgf_ref, lgb_ref, pl.program_id(1))
        is_a, row = tab["is_a"], tab["row"]
        gain = gain_ref[...]

        def norm_body(n, dgain):
            rows = pl.ds(pl.multiple_of(n * CHUNK, CHUNK), CHUNK)
            y = y_ref[rows, :]
            mu = _group_sum(is_a, y) * (1.0 / HEAD_DIM)
            dlt = y - mu
            var = _group_sum(is_a, dlt * dlt) * (1.0 / HEAD_DIM)
            rstd = lax.rsqrt(var + GN_EPS)
            xhat = dlt * rstd
            gate = g_ref[rows, :].astype(F32)
            sg = _sigmoid(gate)
            silu = gate * sg
            d_out = dr_ref[rows, :].astype(F32)
            dg_ref[rows, :] = (d_out * xhat * gain * (sg * (1.0 + gate * (1.0 - sg)))).astype(BF16)
            dxh = d_out * gain * silu
            m1 = _group_sum(is_a, dxh) * (1.0 / HEAD_DIM)
            m2 = _group_sum(is_a, dxh * xhat) * (1.0 / HEAD_DIM)
            dy = (rstd * (dxh - m1 - xhat * m2)).astype(BF16)
            dy_s[rows, :] = dy
            k8 = k_ref[rows, :].astype(F32) * Q_SCALE
            st[n] = jnp.where(tab["diag2"], _dot_tn(_both_ways(k8, tab["kdec"]), v_ref[rows, :]), 0.0)
            qf = q_ref[rows, :].astype(F32)
            gr[n] = jnp.where(tab["diag2"], _dot_tn(_both_ways(qf, tab["qdec"]), dy), 0.0)
            return dgain + jnp.sum(d_out * xhat * silu, axis=0, keepdims=True)

        dgain = _chunk_loop(n_chunks, norm_body, jnp.zeros((1, LANES), F32))
        _scan_states(n_chunks, st, FWD_ROWS, tab["lam_f"], BWD_ROWS, tab["lam_b"])
        _scan_states(n_chunks, gr, BWD_ROWS, tab["lam_b"], FWD_ROWS, tab["lam_f"])
        colsum = lambda x: jnp.sum(x, axis=0, keepdims=True)

        def grad_body(n, carry):
            xfb, ifa, ifb, iba, ibb, lf, lb = carry
            rows = pl.ds(pl.multiple_of(n * CHUNK, CHUNK), CHUNK)
            q = q_ref[rows, :]
            qf = q.astype(F32)
            k8f = k_ref[rows, :].astype(F32) * Q_SCALE
            k8 = k8f.astype(BF16)
            v = v_ref[rows, :]
            dy = dy_s[rows, :]
            q2, dy2 = _stack_pair(is_a, q), _stack_pair(is_a, dy)
            sc = _dot_nt(q2, k8)
            dp = _dot_nt(dy2, v)
            a2 = (sc * tab["d2"]).astype(BF16)
            ds2 = (dp * tab["d2"]).astype(BF16)
            dq = _unstack_pair(is_a, _dot_nn(ds2, k8))
            dk = _dot_tn(ds2, q2)
            dv = _dot_tn(a2, dy2)
            prod = sc * dp
            pf, pb = prod * tab["df2"], prod * tab["db2"]
            ifa, ifb = ifa + colsum(pf[0:CHUNK, :]), ifb + colsum(pf[CHUNK:2 * CHUNK, :])
            iba, ibb = iba + colsum(pb[0:CHUNK, :]), ibb + colsum(pb[CHUNK:2 * CHUNK, :])
            states, sgrads = st[n], gr[n]
            sb, gb = states.astype(BF16), sgrads.astype(BF16)
            dqc = _dot_nt(dy, sb) * tab["qdec"]
            dkc = _dot_nt(v, gb) * tab["kdec"]
            dv = dv + _dot_nn(_both_ways(k8f, tab["kdec"]), gb)
            dq_ref[rows, :] = (dq + dqc[:, 0:LANES] + dqc[:, LANES:2 * LANES]).astype(BF16)
            dk_ref[rows, :] = ((dk + dkc[:, 0:LANES] + dkc[:, LANES:2 * LANES]) * Q_SCALE).astype(BF16)
            dv_ref[rows, :] = dv.astype(BF16)
            q2w, k2w = jnp.concatenate([qf, qf], axis=1), jnp.concatenate([k8f, k8f], axis=1)
            xfb = xfb + colsum(tab["qexp"] * q2w * dqc + tab["kexp"] * k2w * dkc)
            prod_s = sgrads * states
            lf, lb = lf + colsum(prod_s[0:LANES, :]), lb + colsum(prod_s[LANES:2 * LANES, :])
            return xfb, ifa, ifb, iba, ibb, lf, lb

        z = jnp.zeros((1, LANES), F32)
        init = (jnp.zeros((1, 2 * LANES), F32), z, z, z, z, z, z)
        xfb, ifa, ifb, iba, ibb, lf, lb = _chunk_loop(n_chunks, grad_body, init)
        st_ref[...] = jnp.zeros_like(st_ref)
        st_ref[ST_GAIN:ST_GAIN + 1, :] = dgain
        st_ref[ST_XF:ST_XF + 1, :] = xfb[:, 0:LANES]
        st_ref[ST_XB:ST_XB + 1, :] = xfb[:, LANES:2 * LANES]
        st_ref[ST_IFA:ST_IFA + 1, :] = ifa
        st_ref[ST_IFB:ST_IFB + 1, :] = ifb
        st_ref[ST_IBA:ST_IBA + 1, :] = iba
        st_ref[ST_IBB:ST_IBB + 1, :] = ibb
        st_ref[ST_LF:ST_LF + 1, :] = lf * (CHUNK * tab["lam_f"])
        st_ref[ST_LB:ST_LB + 1, :] = lb * (CHUNK * tab["lam_b"])

    lane_blk = lambda c0: _seq_spec(s, c0 // LANES)
    seq0 = _seq_spec(s, 0)
    state = pltpu.VMEM((n_chunks, 2 * LANES, LANES), F32)
    body, lead = _after(after, body)
    return pl.pallas_call(
        body, name="retention_bwd", grid=(b_loc, pairs),
        in_specs=lead + [_smem_spec(), _smem_spec(), lane_blk(C_RQ), lane_blk(C_RK), lane_blk(C_RV), lane_blk(C_RG),
                         seq0, seq0, pl.BlockSpec((1, LANES), lambda b, h: (0, h))],
        out_specs=[seq0] * 4 + [pl.BlockSpec((ST_ROWS, LANES), lambda b, h: (b, h))],
        out_shape=[SDS((t, RET_W), BF16)] * 4 + [SDS((b_loc * ST_ROWS, RET_W), F32)],
        scratch_shapes=[state, state, pltpu.VMEM((s, LANES), BF16)],
        compiler_params=_params(("parallel", "parallel")),
    )(*after, lgf, lgb, u, u, u, u, y_pre, dr, gn_gain)


GW = GROUP * HEAD_DIM
KEYS = 3 * BLOCK


def _attn_tables(g, bias_ref):
    r = lax.broadcasted_iota(jnp.int32, (GROUP * BLOCK, KEYS), 0)
    kj = lax.broadcasted_iota(jnp.int32, (GROUP * BLOCK, KEYS), 1)
    qi = r & (BLOCK - 1)
    hh = lax.shift_right_logical(r, 7)
    dist = jnp.abs(kj - BLOCK - qi)
    slope = jnp.exp2(-(GROUP * g + hh + 1).astype(F32) * (8.0 / ATTN_HEADS))
    bias_ref[...] = jnp.where(dist <= BLOCK, -slope * dist.astype(F32), NEG_INF)


def _tile_keys(x_ref, g, scale, pad_ref, s):
    r = lax.broadcasted_iota(jnp.int32, (LANES, GW), 0)
    c = lax.broadcasted_iota(jnp.int32, (LANES, GW), 1)
    place = jnp.where(r == g * HEAD_DIM + (c & (HEAD_DIM - 1)), 1.0, 0.0).astype(BF16)
    pad_ref[0:BLOCK, :] = jnp.zeros((BLOCK, GW), BF16)
    pad_ref[BLOCK + s:2 * BLOCK + s, :] = jnp.zeros((BLOCK, GW), BF16)
    pad_ref[BLOCK:BLOCK + s, :] = (_dot_nn(x_ref[...], place) * scale).astype(BF16)


def _stack_heads(x):
    lane_h = lax.shift_right_logical(lax.broadcasted_iota(jnp.int32, (1, GW), 1), 6)
    zero = jnp.zeros_like(x)
    return jnp.concatenate([jnp.where(lane_h == h, x, zero) for h in range(GROUP)], axis=0)


def _unstack_heads(x4):
    lane_h = lax.shift_right_logical(lax.broadcasted_iota(jnp.int32, (1, GW), 1), 6)
    out = jnp.zeros((BLOCK, GW), F32)
    for h in range(GROUP):
        out = out + jnp.where(lane_h == h, x4[h * BLOCK:(h + 1) * BLOCK, :], 0.0)
    return out


def _sink_column(sink_ref, g):
    rh = lax.shift_right_logical(lax.broadcasted_iota(jnp.int32, (GROUP * BLOCK, 1), 0), 7)
    col = jnp.zeros((GROUP * BLOCK, 1), F32)
    for h in range(GROUP):
        col = jnp.where(rh == h, sink_ref[GROUP * g + h], col)
    return col


def _attn_probs(qm, k3, bias_ref, sink_col, n, s):
    logits = _dot_nt(qm, k3) + bias_ref[...]
    kpos = n * BLOCK - BLOCK + lax.broadcasted_iota(jnp.int32, (1, KEYS), 1)
    logits = jnp.where((kpos >= 0) & (kpos < s), logits, NEG_INF)
    m = jnp.maximum(jnp.max(logits, axis=1, keepdims=True), sink_col)
    e = jnp.exp(logits - m)
    e_sink = jnp.exp(sink_col - m)
    inv = 1.0 / (jnp.sum(e, axis=1, keepdims=True) + e_sink)
    return e * inv, e_sink * inv


ATT_SUB = 2


def _attn_specs(s, n_steps):
    rows = ATT_SUB * BLOCK
    q_spec = pl.BlockSpec((rows, GW), lambda b, g, n: (b * n_steps + n, C_AQ // GW + g))
    k_spec = pl.BlockSpec((s, LANES), lambda b, g, n: (b, C_AK // LANES))
    v_spec = pl.BlockSpec((s, LANES), lambda b, g, n: (b, C_AV // LANES))
    o_spec = pl.BlockSpec((rows, GW), lambda b, g, n: (b * n_steps + n, g))
    return q_spec, k_spec, v_spec, o_spec


def _attention_fwd(u, sink, b_loc):
    t = u.shape[0]
    s = t // b_loc
    n_steps = s // (ATT_SUB * BLOCK)

    def body(sink_ref, q_ref, k_ref, v_ref, o_ref, kpad, vpad, bias):
        g, step = pl.program_id(1), pl.program_id(2)

        @pl.when(step == 0)
        def _():
            _attn_tables(g, bias)
            _tile_keys(k_ref, g, Q_SCALE, kpad, s)
            _tile_keys(v_ref, g, 1.0, vpad, s)

        sink_col = _sink_column(sink_ref, g)
        for j in range(ATT_SUB):
            n = step * ATT_SUB + j
            rows = pl.ds(j * BLOCK, BLOCK)
            keys = pl.ds(pl.multiple_of(n * BLOCK, BLOCK), KEYS)
            p, _ = _attn_probs(_stack_heads(q_ref[rows, :]), kpad[keys, :], bias, sink_col, n, s)
            o_ref[rows, :] = _unstack_heads(_dot_nn(p.astype(BF16), vpad[keys, :])).astype(BF16)

    q_spec, k_spec, v_spec, o_spec = _attn_specs(s, n_steps)
    pad = pltpu.VMEM((s + 2 * BLOCK, GW), BF16)
    return pl.pallas_call(
        body, name="attention_fwd", grid=(b_loc, KV_HEADS, n_steps),
        in_specs=[_smem_spec(), q_spec, k_spec, v_spec], out_specs=o_spec,
        out_shape=SDS((t, ATT_W), BF16),
        scratch_shapes=[pad, pad, pltpu.VMEM((GROUP * BLOCK, KEYS), F32)],
        compiler_params=_params(("parallel", "arbitrary", "arbitrary")),
    )(sink, u, u, u)


def _fold_groups(x, g):
    x = x + pltpu.roll(x, 2 * HEAD_DIM, 1)
    x = x + pltpu.roll(x, HEAD_DIM, 1)
    lane_g = lax.shift_right_logical(lax.broadcasted_iota(jnp.int32, (1, LANES), 1), 6)
    return jnp.where(lane_g == g, x[:, 0:LANES], 0.0)


def _attention_bwd(u, da, sink, b_loc):
    t = u.shape[0]
    s = t // b_loc
    n_steps = s // (ATT_SUB * BLOCK)

    def body(sink_ref, q_ref, k_ref, v_ref, do_ref, dq_ref, dk_ref, dv_ref, dsink_ref, kpad, vpad, bias, dk_acc, dv_acc):
        g, step = pl.program_id(1), pl.program_id(2)

        @pl.when(step == 0)
        def _():
            _attn_tables(g, bias)
            _tile_keys(k_ref, g, Q_SCALE, kpad, s)
            _tile_keys(v_ref, g, 1.0, vpad, s)
            dsink_ref[...] = jnp.zeros_like(dsink_ref)

        @pl.when((step == 0) & (g == 0))
        def _():
            dk_acc[...] = jnp.zeros_like(dk_acc)
            dv_acc[...] = jnp.zeros_like(dv_acc)

        sink_col = _sink_column(sink_ref, g)
        head_row = lax.broadcasted_iota(jnp.int32, dsink_ref.shape, 0)
        upd = jnp.zeros(dsink_ref.shape, F32)
        for j in range(ATT_SUB):
            n = step * ATT_SUB + j
            rows = pl.ds(j * BLOCK, BLOCK)
            keys = pl.ds(pl.multiple_of(n * BLOCK, BLOCK), KEYS)
            qm = _stack_heads(q_ref[rows, :])
            k3, v3 = kpad[keys, :], vpad[keys, :]
            p, p_sink = _attn_probs(qm, k3, bias, sink_col, n, s)
            dom = _stack_heads(do_ref[rows, :])
            dp = _dot_nt(dom, v3)
            delta = jnp.sum(p * dp, axis=1, keepdims=True)
            ds_mat = (p * (dp - delta)).astype(BF16)
            dq_ref[rows, :] = _unstack_heads(_dot_nn(ds_mat, k3)).astype(BF16)
            dk_acc[keys, :] += _fold_groups(_dot_tn(ds_mat, qm), g) * Q_SCALE
            dv_acc[keys, :] += _fold_groups(_dot_tn(p.astype(BF16), dom), g)
            w = p_sink * delta
            for h in range(GROUP):
                upd = upd + jnp.where(head_row == h, -jnp.sum(w[h * BLOCK:(h + 1) * BLOCK, :]), 0.0)
        dsink_ref[...] += upd

        @pl.when((step == n_steps - 1) & (g == KV_HEADS - 1))
        def _():
            dk_ref[...] = dk_acc[BLOCK:BLOCK + s, :].astype(BF16)
            dv_ref[...] = dv_acc[BLOCK:BLOCK + s, :].astype(BF16)

    q_spec, k_spec, v_spec, o_spec = _attn_specs(s, n_steps)
    kv_out = pl.BlockSpec((s, LANES), lambda b, g, n: (b, 0))
    pad = pltpu.VMEM((s + 2 * BLOCK, GW), BF16)
    acc = pltpu.VMEM((s + 2 * BLOCK, LANES), F32)
    return pl.pallas_call(
        body, name="attention_bwd", grid=(b_loc, KV_HEADS, n_steps),
        in_specs=[_smem_spec(), q_spec, k_spec, v_spec, o_spec],
        out_specs=[o_spec, kv_out, kv_out, pl.BlockSpec((8, LANES), lambda b, g, n: (b * KV_HEADS + g, 0))],
        out_shape=[SDS((t, ATT_W), BF16), SDS((t, KV_W), BF16), SDS((t, KV_W), BF16),
                   SDS((b_loc * KV_HEADS * 8, LANES), F32)],
        scratch_shapes=[pad, pad, pltpu.VMEM((GROUP * BLOCK, KEYS), F32), acc, acc],
        compiler_params=_params(("arbitrary", "arbitrary", "arbitrary")),
    )(sink, u, u, u, da)


def _pack_small(acc2, acc1, ret_stats, dsink, b_loc, d):
    pairs = RET_HEADS // 2

    def body(acc2_ref, acc1_ref, st_ref, dsink_ref, out_ref):
        out_ref[...] = jnp.zeros_like(out_ref)
        out_ref[ROW_LN1G:ROW_LN1G + 1, :] = acc1_ref[0:1, :]
        out_ref[ROW_LN1B:ROW_LN1B + 1, :] = acc1_ref[1:2, :]
        out_ref[ROW_LN2G:ROW_LN2G + 1, :] = acc2_ref[1:2, :]
        out_ref[ROW_LN2B:ROW_LN2B + 1, :] = acc2_ref[2:3, :]
        out_ref[ROW_LOSS:ROW_LOSS + 1, :] = acc2_ref[0:1, :]
        st = st_ref[0:ST_ROWS, :]
        for b in range(1, b_loc):
            st = st + st_ref[b * ST_ROWS:(b + 1) * ST_ROWS, :]
        out_ref[ROW_GN:ROW_GN + 1, 0:RET_W] = st[ST_GAIN:ST_GAIN + 1, :]
        lane = lax.broadcasted_iota(jnp.int32, (1, d), 1)
        misc = jnp.zeros((1, d), F32)
        for pr in range(pairs):
            blk = st[:, pr * LANES:(pr + 1) * LANES]
            half = lax.broadcasted_iota(jnp.int32, (1, LANES), 1) < HEAD_DIM
            for h in range(2):
                sel = half if h == 0 else jnp.logical_not(half)
                cross_f = jnp.sum(jnp.where(sel, blk[ST_XF:ST_XF + 1, :] + blk[ST_LF:ST_LF + 1, :], 0.0))
                cross_b = jnp.sum(jnp.where(sel, blk[ST_XB:ST_XB + 1, :] + blk[ST_LB:ST_LB + 1, :], 0.0))
                intra_f = jnp.sum(blk[ST_IFA + h:ST_IFA + h + 1, :])
                intra_b = jnp.sum(blk[ST_IBA + h:ST_IBA + h + 1, :])
                head = 2 * pr + h
                misc = jnp.where(lane == MISC_DF + head, cross_f + intra_f, misc)
                misc = jnp.where(lane == MISC_DB + head, cross_b + intra_b, misc)
        for g in range(KV_HEADS):
            tot = dsink_ref[g * 8:(g + 1) * 8, :]
            for b in range(1, b_loc):
                tot = tot + dsink_ref[(b * KV_HEADS + g) * 8:(b * KV_HEADS + g + 1) * 8, :]
            for h in range(GROUP):
                misc = jnp.where(lane == MISC_SINK + GROUP * g + h, jnp.sum(tot[h:h + 1, 0:1]), misc)
        out_ref[ROW_MISC:ROW_MISC + 1, :] = misc

    return pl.pallas_call(body, name="pack_small", out_shape=SDS((SMALL_ROWS, d), F32))(acc2, acc1, ret_stats, dsink)


BIG = ("w_in", "w_out", "w_ffn_gate", "w_ffn_up", "w_ffn_down", "w_ple_proj", "w_ple_gate")
TRANSPOSED_OUTSIDE = ("w_in", "w_ffn_gate", "w_ffn_up")
TRANSPOSED_HERE = ("w_ple_proj",)
SMALL = ("ret_decay_fwd", "ret_decay_bwd", "ret_gn_gain", "attn_sink", "ln1_gain", "ln1_bias", "ln2_gain", "ln2_bias")
ORDER = ("w_in", "ret_decay_fwd", "ret_decay_bwd", "ret_gn_gain", "attn_sink", "w_out", "ln1_gain", "ln1_bias",
         "w_ffn_gate", "w_ffn_up", "w_ffn_down", "w_ple_proj", "w_ple_gate", "ln2_gain", "ln2_bias")


GATHER_ORDER = ("w_in", "w_out", "w_ffn_gate", "w_ffn_up", "w_ple_gate", "w_ple_proj", "w_ffn_down")


def _local_step(x2, p2, target2, fetch, publish, small, b_loc, me):
    d = x2.shape[1]
    lgf, lgb = _log_decay(small["ret_decay_fwd"], small["ret_decay_bwd"])
    lgf1, lgb1, sink1 = lgf.reshape(-1), lgb.reshape(-1), small["attn_sink"].reshape(-1)
    (w_in,) = fetch(("w_in",), ())
    u, xb = _in_proj(x2, w_in)
    r, y_pre = _retention_fwd(u, lgf1, lgb1, small["ret_gn_gain"], b_loc)
    a = _attention_fwd(u, sink1, b_loc)
    w_out, w_gate, w_up = fetch(("w_out", "w_ffn_gate", "w_ffn_up"), (r, a))
    z1, h1b, dact_dg, dact_du, act = _mix_ln1_ffn_up(
        r, a, x2, w_out, w_gate, w_up, small["ln1_gain"], small["ln1_bias"])
    w_pg, w_pe, w_down = fetch(("w_ple_gate", "w_ple_proj", "w_ffn_down"), (act,))
    dz2, dz2b, dsb, dpleb, dg, dup, acc2 = _ffn_down_ln2_loss(
        act, dact_dg, dact_du, h1b, p2, z1, target2, w_down, w_pg, w_pe,
        small["ln1_gain"], small["ln1_bias"], small["ln2_gain"], small["ln2_bias"])
    own = {}

    def grad(name, parts, rhs, after=()):
        whole, own[name] = _weight_grad("grad_" + name, me, parts, rhs, after)
        return whole

    t1 = publish("ffn_down", dict(w_ffn_down=grad("w_ffn_down", [act], dz2b),
                                  w_ple_proj=grad("w_ple_proj", [dpleb], p2),
                                  w_ple_gate=grad("w_ple_gate", [h1b], dsb)))
    t2 = publish("ffn_up", dict(w_ffn_gate=grad("w_ffn_gate", [dg], h1b, t1), w_ffn_up=grad("w_ffn_up", [dup], h1b)))
    dz1, dz1b, dr, da, acc1 = _dh1_ln1_bwd(dz2, dg, dup, dsb, z1, w_gate, w_up, w_pg, w_out, small["ln1_gain"], t2)
    t3 = publish("out", dict(w_out=grad("w_out", [r, a], dz1b)))
    dq, dk, dv, dgate, ret_stats = _retention_bwd(u, y_pre, dr, lgf1, lgb1, small["ret_gn_gain"], b_loc, t3)
    daq, dak, dav, dsink = _attention_bwd(u, da, sink1, b_loc)
    parts = [dq, dk, dv, dgate, daq, dak, dav]
    small_part = _pack_small(acc2, acc1, ret_stats, dsink, b_loc, d)
    ts = publish("small", small_part)
    t4 = publish("in", dict(w_in=grad("w_in", parts, xb, ts)))
    grad_x = _in_proj_bwd(dz1, parts, w_in, t4)
    return grad_x, own, small_part


def kernel(x, p, w_in, ret_decay_fwd, ret_decay_bwd, ret_gn_gain, attn_sink, w_out, ln1_gain, ln1_bias, w_ffn_gate, w_ffn_up, w_ffn_down, w_ple_proj, w_ple_gate, ln2_gain, ln2_bias, loss_target, m_w_in, m_ret_decay_fwd, m_ret_decay_bwd, m_ret_gn_gain, m_attn_sink, m_w_out, m_ln1_gain, m_ln1_bias, m_w_ffn_gate, m_w_ffn_up, m_w_ffn_down, m_w_ple_proj, m_w_ple_gate, m_ln2_gain, m_ln2_bias, v_w_in, v_ret_decay_fwd, v_ret_decay_bwd, v_ret_gn_gain, v_attn_sink, v_w_out, v_ln1_gain, v_ln1_bias, v_w_ffn_gate, v_w_ffn_up, v_w_ffn_down, v_w_ple_proj, v_w_ple_gate, v_ln2_gain, v_ln2_bias):
    given = dict(locals())

    def strip(n, a):
        if n not in BIG:
            return a
        return a[0].T if n in TRANSPOSED_OUTSIDE else a[0]

    def restore(n, a):
        if n not in BIG:
            return a
        return (a.T if n in TRANSPOSED_OUTSIDE else a)[None]

    w = {n: strip(n, given[n]) for n in ORDER}
    m = {n: strip(n, given["m_" + n]) for n in ORDER}
    v = {n: strip(n, given["v_" + n]) for n in ORDER}
    b_loc, s, d = x.shape
    x2 = x.reshape(b_loc * s, d)
    p2 = p[0].reshape(b_loc * s, p.shape[-1])
    target2 = loss_target.reshape(b_loc * s, d)

    small = {n: w[n] for n in SMALL}
    me = (4 * lax.axis_index("x") + 2 * lax.axis_index("y") + lax.axis_index("c")).astype(jnp.int32).reshape(1)

    gathered = _prep_shards(me, {n: w[n] for n in BIG})
    gather = _split_copy_start("gather_start", [(gathered[n],) for n in GATHER_ORDER],
                               [_gather_copy_near] + [_gather_copy] * (len(GATHER_ORDER) - 1))

    def fetch(names, after):
        if names == ("w_in",):
            near = _split_copy_wait("gather_wait_w_in_near", gather, [0], list(after))
            passed = _split_copy_start("gather_pass_w_in", near, [_gather_copy_pass])
            return [_split_copy_wait("gather_wait_w_in", passed, [0], [])[0][0]]
        which = [GATHER_ORDER.index(n) for n in names]
        got = _split_copy_wait("gather_wait_" + names[0], gather, which, list(after))
        return [item[0] for item in got]

    scatters = []

    def publish(tag, products):
        if tag == "small":
            items, copies = [(products, lax.empty((N_DEV - 1,) + products.shape, F32))], [_small_copy]
        else:
            items = [(products[n], lax.empty((N_DEV - 1, products[n].shape[0] // N_DEV, products[n].shape[1]), BF16))
                     for n in products]
            copies = [_scatter_copy] * len(items)
        started = _split_copy_start("scatter_start_" + tag, items, copies)
        scatters.append((tag, list(products) if tag != "small" else [], started))
        return (started["token"],)

    grad_x, own, small_part = _local_step(x2, p2, target2, fetch, publish, small, b_loc, me)

    out_g, out_d, out_m, out_v = {}, {}, {}, {}
    after = [grad_x]
    for tag, names, started in scatters:
        landed = _split_copy_wait("scatter_wait_" + tag, started, list(range(len(started["items"]))), after)
        if tag == "small":
            loss, sg, sd, sm, sv = _small_adamw(
                me, small_part, landed[0][1], small, {n: m[n] for n in SMALL}, {n: v[n] for n in SMALL})
            for dst, src in ((out_g, sg), (out_d, sd), (out_m, sm), (out_v, sv)):
                dst.update(src)
            after = [sv["ln2_bias"]]
            continue
        for n, (_, recv) in zip(names, landed):
            out_g[n], out_d[n], out_m[n], out_v[n] = _reduce_adamw(
                n, own[n], recv, w[n], m[n], v[n], n in TRANSPOSED_HERE)
        after = [out_v[names[-1]]]

    outs = [loss[0, 0], grad_x.reshape(x.shape)]
    for group in (out_g, out_d, out_m, out_v):
        outs += [restore(n, group[n]) for n in ORDER]
    return tuple(outs)
```

```python
import functools

import jax
import jax.numpy as jnp
from jax import lax
from jax.experimental import pallas as pl
from jax.experimental.pallas import tpu as pltpu

F32, BF16 = jnp.float32, jnp.bfloat16
SDS = jax.ShapeDtypeStruct
MESH = pl.DeviceIdType.MESH

N_DEV = 8
HEAD_DIM = 64
RET_HEADS = 8
ATTN_HEADS = 8
KV_HEADS = 2
GROUP = ATTN_HEADS // KV_HEADS
RET_W = RET_HEADS * HEAD_DIM
ATT_W = ATTN_HEADS * HEAD_DIM
KV_W = KV_HEADS * HEAD_DIM
LANES = 128
CHUNK = 128
BLOCK = 128
Q_SCALE = HEAD_DIM ** -0.5
ALPHA = 2.0 ** 0.25
LN_EPS = 1e-5
GN_EPS = 1e-5
NEG_INF = -1e30
C_RQ, C_RK, C_RV, C_RG = 0, RET_W, 2 * RET_W, 3 * RET_W
C_AQ = 4 * RET_W
C_AK = C_AQ + ATT_W
C_AV = C_AK + KV_W
IN_W = C_AV + KV_W

ADAM_LR = 0.001
ADAM_B1 = 0.9
ADAM_B2 = 0.999
ADAM_EPS = 1e-08
ADAM_WD = 0.01
ADAM_STEP = 10

VMEM_LIMIT = 56 * 1024 * 1024
MATMUL_ROWS = 512
EPILOGUE_ROWS = 256
SUB_ROWS = 256
SMALL_ROWS = 16
ROW_LN1G, ROW_LN1B, ROW_LN2G, ROW_LN2B, ROW_LOSS, ROW_GN, ROW_MISC = 0, 1, 2, 3, 4, 5, 6
MISC_DF, MISC_DB, MISC_SINK = 0, 8, 16


def _dot_nn(a, b):
    return lax.dot_general(a, b, (((1,), (0,)), ((), ())), preferred_element_type=F32)


def _dot_nt(a, b):
    return lax.dot_general(a, b, (((1,), (1,)), ((), ())), preferred_element_type=F32)


def _dot_tn(a, b):
    return lax.dot_general(a, b, (((0,), (0,)), ((), ())), preferred_element_type=F32)


def _params(sem=None, vmem=VMEM_LIMIT):
    kw = {"vmem_limit_bytes": vmem}
    if sem is not None:
        kw["dimension_semantics"] = sem
    return pltpu.CompilerParams(**kw)


def _row_tile(t, want=512):
    tm = want
    while t % tm:
        tm //= 2
    return tm


def _sigmoid(x):
    return 1.0 / (1.0 + jnp.exp(-x))


def _layer_norm_stats(z):
    mu = jnp.mean(z, axis=1, keepdims=True)
    d = z - mu
    var = jnp.mean(d * d, axis=1, keepdims=True)
    rstd = lax.rsqrt(var + LN_EPS)
    return d * rstd, rstd


def _layer_norm_bwd(dxh, xhat, rstd):
    m1 = jnp.mean(dxh, axis=1, keepdims=True)
    m2 = jnp.mean(dxh * xhat, axis=1, keepdims=True)
    return rstd * (dxh - m1 - xhat * m2)


def _prep_shards(me, shards):
    names = list(shards)

    def body(me_ref, *refs):
        for name, src, dst in zip(names, refs[:len(names)], refs[len(names):]):
            val = src[...]
            dst[...] = (val.T if name in TRANSPOSED_HERE else val).astype(BF16)

    shape = lambda n, a: a.shape[::-1] if n in TRANSPOSED_HERE else a.shape
    shapes = [shape(n, shards[n]) for n in names]
    out = pl.pallas_call(
        body, name="prep_shards",
        grid_spec=pltpu.PrefetchScalarGridSpec(
            num_scalar_prefetch=1, grid=(1,),
            in_specs=[pl.BlockSpec(shards[n].shape, lambda i, me_ref: (0, 0)) for n in names],
            out_specs=[pl.BlockSpec(s, lambda i, me_ref: (me_ref[0], 0)) for s in shapes]),
        out_shape=[SDS((N_DEV * s[0], s[1]), BF16) for s in shapes], compiler_params=_params(("arbitrary",)),
    )(me, *[shards[n] for n in names])
    return dict(zip(names, out))


def _mesh_pos():
    return lax.axis_index("x"), lax.axis_index("y"), lax.axis_index("c")


HBM_SPEC = pl.BlockSpec(memory_space=pltpu.HBM)
SEM_SPEC = pl.BlockSpec(memory_space=pltpu.SEMAPHORE)
ANY_SPEC = pl.BlockSpec(memory_space=pl.ANY)
SIDE_EFFECT = pltpu.SideEffectType.DATAFLOW_SIDE_EFFECTING
PEER_SEMS = pltpu.SemaphoreType.DMA((N_DEV - 1,))


def _in_hbm(a):
    return pltpu.with_memory_space_constraint(a, pltpu.HBM)


def _split_copy_start(name, items, copies):
    n = len(items)
    flat = [a for it in items for a in it]
    k = len(flat)

    def body(*refs):
        arr, sems = list(refs[:k]), refs[k:k + 2 * n]
        for i, it in enumerate(items):
            mine = [arr.pop(0) for _ in it]
            for m in range(1, N_DEV):
                cp = copies[i](m, mine, sems[i].at[m - 1], sems[n + i].at[m - 1])
                if cp is not None:
                    cp.start()
        token = refs[-1]
        token[...] = jnp.zeros_like(token)

    res = pl.pallas_call(
        body, name=name,
        out_shape=[PEER_SEMS] * (2 * n) + [pltpu.HBM(a.shape, a.dtype) for a in flat] + [SDS((8, LANES), F32)],
        in_specs=[HBM_SPEC] * k,
        out_specs=[SEM_SPEC] * (2 * n) + [HBM_SPEC] * k + [pl.BlockSpec(memory_space=pltpu.VMEM)],
        input_output_aliases={j: 2 * n + j for j in range(k)},
        compiler_params=pltpu.CompilerParams(has_side_effects=SIDE_EFFECT),
    )(*[_in_hbm(a) for a in flat])
    thru, out_items = list(res[2 * n:2 * n + k]), []
    for it in items:
        out_items.append(tuple(thru.pop(0) for _ in it))
    return dict(send=res[:n], recv=res[n:2 * n], items=out_items, token=res[-1], copies=copies)


def _split_copy_wait(name, started, which, after):
    items = [started["items"][i] for i in which]
    copies = [started["copies"][i] for i in which]
    n = len(items)
    flat = [a for it in items for a in it]
    k = len(flat)

    def body(*refs):
        arr, sems = list(refs[:k]), refs[k:k + 2 * n]
        for i, it in enumerate(items):
            mine = [arr.pop(0) for _ in it]
            for m in range(1, N_DEV):
                cp = copies[i](m, mine, sems[i].at[m - 1], sems[n + i].at[m - 1])
                if cp is not None:
                    cp.wait_send()
                    cp.wait_recv()

    res = pl.pallas_call(
        body, name=name,
        out_shape=[pltpu.HBM(a.shape, a.dtype) for a in flat],
        in_specs=[HBM_SPEC] * k + [SEM_SPEC] * (2 * n) + [ANY_SPEC] * len(after),
        out_specs=[HBM_SPEC] * k,
        input_output_aliases={j: j for j in range(k)},
        compiler_params=pltpu.CompilerParams(has_side_effects=SIDE_EFFECT),
    )(*flat, *[started["send"][i] for i in which], *[started["recv"][i] for i in which], *[_in_hbm(a) for a in after])
    thru, out_items = list(res), []
    for it in items:
        out_items.append(tuple(thru.pop(0) for _ in it))
    return out_items


def _gather_copy(m, refs, send_sem, recv_sem):
    (land_ref,) = refs
    r = land_ref.shape[0] // N_DEV
    mine = land_ref.at[pl.ds(pl.multiple_of(_peer_index(0) * r, 8), r), :]
    return pltpu.make_async_remote_copy(src_ref=mine, dst_ref=mine, send_sem=send_sem, recv_sem=recv_sem,
                                        device_id=_peer(m), device_id_type=MESH)


def _gather_copy_near(m, refs, send_sem, recv_sem):
    return _gather_copy(m, refs, send_sem, recv_sem) if m == 1 or m % 2 == 0 else None


def _gather_copy_pass(m, refs, send_sem, recv_sem):
    if m == 1 or m % 2 == 0:
        return None
    (land_ref,) = refs
    r = land_ref.shape[0] // N_DEV
    block = land_ref.at[pl.ds(pl.multiple_of(_peer_index(m ^ 1) * r, 8), r), :]
    return pltpu.make_async_remote_copy(src_ref=block, dst_ref=block, send_sem=send_sem, recv_sem=recv_sem,
                                        device_id=_peer(1), device_id_type=MESH)


def _small_copy(m, refs, send_sem, recv_sem):
    part_ref, land_ref = refs
    return pltpu.make_async_remote_copy(src_ref=part_ref, dst_ref=land_ref.at[m - 1], send_sem=send_sem,
                                        recv_sem=recv_sem, device_id=_peer(m), device_id_type=MESH)


def _scatter_copy(m, refs, send_sem, recv_sem):
    buf_ref, land_ref = refs
    r = buf_ref.shape[0] // N_DEV
    src = buf_ref.at[pl.ds(pl.multiple_of(_peer_index(m) * r, 8), r), :]
    return pltpu.make_async_remote_copy(src_ref=src, dst_ref=land_ref.at[m - 1], send_sem=send_sem,
                                        recv_sem=recv_sem, device_id=_peer(m), device_id_type=MESH)


def _peer(m):
    x, y, c = _mesh_pos()
    bx, by, bc = (m >> 2) & 1, (m >> 1) & 1, m & 1
    return (x ^ bx if bx else x, y ^ by if by else y, c ^ bc if bc else c)


def _peer_index(m):
    x, y, c = _mesh_pos()
    return (4 * x + 2 * y + c) ^ m


SMALL_PLACE = {
    "ln1_gain": (ROW_LN1G, 0), "ln1_bias": (ROW_LN1B, 0), "ln2_gain": (ROW_LN2G, 0), "ln2_bias": (ROW_LN2B, 0),
    "ret_gn_gain": (ROW_GN, 0), "ret_decay_fwd": (ROW_MISC, MISC_DF), "ret_decay_bwd": (ROW_MISC, MISC_DB),
    "attn_sink": (ROW_MISC, MISC_SINK)}


def _small_adamw(me, part, landed, w, m, v):
    d = part.shape[1]
    names = list(SMALL_PLACE)
    k = len(names)

    def body(*refs):
        me_ref, part_ref, land_ref = refs[:3]
        refs = refs[2:]
        w_refs, m_refs, v_refs = refs[1:1 + k], refs[1 + k:1 + 2 * k], refs[1 + 2 * k:1 + 3 * k]
        outs = refs[1 + 3 * k:1 + 7 * k + 1]
        tot_ref = refs[-1]
        loss_ref, g_refs, dl_refs = outs[0], outs[1:1 + k], outs[1 + k:1 + 2 * k]
        nm_refs, nv_refs = outs[1 + 2 * k:1 + 3 * k], outs[1 + 3 * k:1 + 4 * k]
        tot = jnp.zeros(part_ref.shape, F32)
        for dev in range(N_DEV):
            j = dev ^ me_ref[0]
            tot = tot + jnp.where(j == 0, part_ref[...], land_ref[jnp.maximum(j, 1) - 1])
        tot_ref[...] = tot
        loss_ref[...] = (0.5 / d) * jnp.sum(tot_ref[ROW_LOSS:ROW_LOSS + 1, :], axis=1, keepdims=True)
        for i, name in enumerate(names):
            row, lo = SMALL_PLACE[name]
            wv = w_refs[i][...]
            g = tot_ref[row:row + 1, lo:lo + wv.shape[1]]
            if name.startswith("ret_decay"):
                p2 = jnp.exp2(wv)
                g = g * (-p2 * jnp.log(2.0) / (1.0 - p2))
            g_refs[i][...] = g
            _adamw_store(g, wv, m_refs[i][...], v_refs[i][...], dl_refs[i], nm_refs[i], nv_refs[i])

    shapes = [SDS(w[n].shape, F32) for n in names]
    vm = pl.BlockSpec(memory_space=pltpu.VMEM)
    res = pl.pallas_call(
        body, name="small_adamw", out_shape=[SDS((1, 1), F32)] + shapes * 4,
        in_specs=[_smem_spec()] + [vm] * (2 + 3 * k), out_specs=[vm] * (1 + 4 * k),
        scratch_shapes=[pltpu.VMEM(part.shape, F32)],
    )(me, part, landed, *[w[n] for n in names], *[m[n] for n in names], *[v[n] for n in names])
    groups = [dict(zip(names, res[1 + j * k:1 + (j + 1) * k])) for j in range(4)]
    return (res[0], *groups)


def _adamw_store(g, w, m, v, dl_ref, nm_ref, nv_ref):
    m = ADAM_B1 * m + (1.0 - ADAM_B1) * g
    v = ADAM_B2 * v + (1.0 - ADAM_B2) * (g * g)
    m_hat = m / (1.0 - ADAM_B1 ** ADAM_STEP)
    v_hat = v / (1.0 - ADAM_B2 ** ADAM_STEP)
    dl_ref[...] = -ADAM_LR * (m_hat / (jnp.sqrt(v_hat) + ADAM_EPS) + ADAM_WD * w)
    nm_ref[...] = m
    nv_ref[...] = v


def _reduce_adamw(name, own, recv, w, m, v, transposed):
    rows, n = own.shape
    steps = 1 if transposed or rows % 32 else 4
    rb = rows // steps

    def body(own_ref, recv_ref, w_ref, m_ref, v_ref, g_ref, dl_ref, nm_ref, nv_ref):
        g = own_ref[...]
        for k in range(N_DEV - 1):
            g = g + recv_ref[k].astype(F32)
        if transposed:
            g = g.T
        g_ref[...] = g
        _adamw_store(g, w_ref[...], m_ref[...], v_ref[...], dl_ref, nm_ref, nv_ref)

    blk = pl.BlockSpec(w.shape if transposed else (rb, n), lambda i: (i, 0))
    out = SDS(w.shape, F32)
    return pl.pallas_call(
        body, name="adamw_" + name, grid=(steps,),
        in_specs=[pl.BlockSpec((rb, n), lambda i: (i, 0)), pl.BlockSpec((N_DEV - 1, rb, n), lambda i: (0, i, 0)),
                  blk, blk, blk],
        out_specs=[blk] * 4, out_shape=[out] * 4, compiler_params=_params(("parallel",)),
    )(own, recv, w, m, v)


def _row_spec(tm, width):
    return pl.BlockSpec((tm, width), lambda i: (i, 0))


def _full_spec(shape):
    return pl.BlockSpec(shape, lambda i: (0,) * len(shape))


_acc_spec = _full_spec


def _sub_rows(tm):
    step = min(SUB_ROWS, tm)
    return [(lo, lo + step) for lo in range(0, tm, step)]


def _in_proj(x2, wt_in):
    t, d = x2.shape
    u_w = wt_in.shape[0]
    tm = _row_tile(t, MATMUL_ROWS)

    def body(x_ref, w_ref, u_ref, xb_ref):
        xb = x_ref[...].astype(BF16)
        xb_ref[...] = xb
        u_ref[...] = _dot_nt(xb, w_ref[...]).astype(BF16)

    return pl.pallas_call(
        body, name="in_proj", grid=(t // tm,),
        in_specs=[_row_spec(tm, d), _full_spec(wt_in.shape)],
        out_specs=[_row_spec(tm, u_w), _row_spec(tm, d)],
        out_shape=[SDS((t, u_w), BF16), SDS((t, d), BF16)],
        compiler_params=_params(("parallel",)),
    )(x2, wt_in)


def _col_halves(f):
    n = f // LANES
    k = (n + 1) // 2 * LANES
    return [(0, k), (k, f)] if k < f else [(0, f)]


def _mix_ln1_ffn_up(r, a, x2, w_out, wt_gate, wt_up, g1, b1):
    t, d = x2.shape
    f = wt_gate.shape[0]
    tm = _row_tile(t, EPILOGUE_ROWS)

    def body(r_ref, a_ref, x_ref, wo_ref, wg_ref, wu_ref, g_ref, b_ref, z_ref, hb_ref, dg_ref, du_ref, act_ref):
        mix = _dot_nn(r_ref[...], wo_ref[0:RET_W, :]) + _dot_nn(a_ref[...], wo_ref[RET_W:RET_W + ATT_W, :])
        z = ALPHA * x_ref[...] + mix
        xhat, _ = _layer_norm_stats(z)
        z_ref[...] = z
        h = (xhat * g_ref[...] + b_ref[...]).astype(BF16)
        hb_ref[...] = h
        for lo, hi in _col_halves(f):
            g = _dot_nt(h, wg_ref[lo:hi, :])
            u = _dot_nt(h, wu_ref[lo:hi, :])
            sg = _sigmoid(g)
            silu = g * sg
            dg_ref[:, lo:hi] = (u * (sg * (1.0 + g * (1.0 - sg)))).astype(BF16)
            du_ref[:, lo:hi] = silu.astype(BF16)
            act_ref[:, lo:hi] = (silu * u).astype(BF16)

    wide, narrow = _row_spec(tm, f), _row_spec(tm, d)
    return pl.pallas_call(
        body, name="mix_ln1_ffn_up", grid=(t // tm,),
        in_specs=[_row_spec(tm, RET_W), _row_spec(tm, ATT_W), narrow, _resident_spec(w_out.shape),
                  _resident_spec(wt_gate.shape), _resident_spec(wt_up.shape), _full_spec(g1.shape),
                  _full_spec(b1.shape)],
        out_specs=[narrow, narrow, wide, wide, wide],
        out_shape=[SDS((t, d), F32), SDS((t, d), BF16)] + [SDS((t, f), BF16)] * 3,
        compiler_params=_params(("parallel",)),
    )(r, a, x2, w_out, wt_gate, wt_up, g1, b1)


def _ffn_down_ln2_loss(act, dact_dg, dact_du, h1b, p2, z1, target, w_down, w_pg, wt_pe, g1, b1, g2, b2):
    t, d = z1.shape
    f = act.shape[1]
    pdim = p2.shape[1]
    tm = _row_tile(t, EPILOGUE_ROWS)

    def body(act_ref, fg_ref, fu_ref, hb_ref, p_ref, z1_ref, tgt_ref, wd_ref, wpg_ref, wpe_ref, g1_ref, b1_ref,
             g2_ref, b2_ref, dz_ref, dzb_ref, ds_ref, dple_ref, dg_ref, du_ref, acc_ref):
        @pl.when(pl.program_id(0) == 0)
        def _():
            acc_ref[...] = jnp.zeros_like(acc_ref)

        for lo, hi in _sub_rows(tm):
            xhat1, _ = _layer_norm_stats(z1_ref[lo:hi, :])
            h1 = xhat1 * g1_ref[...] + b1_ref[...]
            ffn = _dot_nn(act_ref[lo:hi, :], wd_ref[...])
            pg = _sigmoid(_dot_nn(hb_ref[lo:hi, :], wpg_ref[...]))
            ple = _dot_nt(p_ref[lo:hi, :].astype(BF16), wpe_ref[...])
            z2 = ALPHA * h1 + ffn + pg * ple
            xhat2, rstd2 = _layer_norm_stats(z2)
            err = xhat2 * g2_ref[...] + b2_ref[...] - tgt_ref[lo:hi, :]
            dy = err * (1.0 / d)
            dz = _layer_norm_bwd(dy * g2_ref[...], xhat2, rstd2)
            dzb = dz.astype(BF16)
            dz_ref[lo:hi, :] = dz
            dzb_ref[lo:hi, :] = dzb
            ds_ref[lo:hi, :] = (dz * ple * pg * (1.0 - pg)).astype(BF16)
            dple_ref[lo:hi, :] = (dz * pg).astype(BF16)
            acc_ref[0:1, :] += jnp.sum(err * err, axis=0, keepdims=True)
            acc_ref[1:2, :] += jnp.sum(dy * xhat2, axis=0, keepdims=True)
            acc_ref[2:3, :] += jnp.sum(dy, axis=0, keepdims=True)
            for c0, c1 in _col_halves(f):
                da = _dot_nt(dzb, wd_ref[c0:c1, :])
                dg_ref[lo:hi, c0:c1] = (da * fg_ref[lo:hi, c0:c1].astype(F32)).astype(BF16)
                du_ref[lo:hi, c0:c1] = (da * fu_ref[lo:hi, c0:c1].astype(F32)).astype(BF16)

    vec = _full_spec(g1.shape)
    wide, narrow = _row_spec(tm, f), _row_spec(tm, d)
    return pl.pallas_call(
        body, name="ffn_down_ln2_loss", grid=(t // tm,),
        in_specs=[wide, wide, wide, narrow, _row_spec(tm, pdim), narrow, narrow,
                  _full_spec(w_down.shape), _full_spec(w_pg.shape), _full_spec(wt_pe.shape), vec, vec, vec, vec],
        out_specs=[narrow] * 4 + [wide, wide, _acc_spec((8, d))],
        out_shape=[SDS((t, d), F32), SDS((t, d), BF16), SDS((t, d), BF16), SDS((t, d), BF16),
                   SDS((t, f), BF16), SDS((t, f), BF16), SDS((8, d), F32)],
        compiler_params=_params(("arbitrary",)),
    )(act, dact_dg, dact_du, h1b, p2, z1, target, w_down, w_pg, wt_pe, g1, b1, g2, b2)


def _after(after, body):
    k = len(after)
    return (lambda *refs: body(*refs[k:])), [ANY_SPEC] * k


def _resident_spec(shape):
    return pl.BlockSpec(shape, lambda i: (0,) * len(shape), pipeline_mode=pl.Buffered(1))


def _dh1_ln1_bwd(dz2, dg, dup, dsb, z1, wt_gate, wt_up, w_pg, w_out, g1, after=()):
    t, d = dz2.shape
    f = dg.shape[1]
    tm = _row_tile(t, EPILOGUE_ROWS)

    def body(dz_ref, dg_ref, du_ref, ds_ref, z1_ref, wg_ref, wu_ref, wpg_ref, wo_ref, g1_ref,
             dz1_ref, dz1b_ref, dr_ref, da_ref, acc_ref):
        @pl.when(pl.program_id(0) == 0)
        def _():
            acc_ref[...] = jnp.zeros_like(acc_ref)

        for lo, hi in _sub_rows(tm):
            dh = (ALPHA * dz_ref[lo:hi, :] + _dot_nn(dg_ref[lo:hi, :], wg_ref[...])
                  + _dot_nn(du_ref[lo:hi, :], wu_ref[...]) + _dot_nt(ds_ref[lo:hi, :], wpg_ref[...]))
            xhat, rstd = _layer_norm_stats(z1_ref[lo:hi, :])
            dz1 = _layer_norm_bwd(dh * g1_ref[...], xhat, rstd)
            dz1b = dz1.astype(BF16)
            dz1_ref[lo:hi, :] = dz1
            dz1b_ref[lo:hi, :] = dz1b
            acc_ref[0:1, :] += jnp.sum(dh * xhat, axis=0, keepdims=True)
            acc_ref[1:2, :] += jnp.sum(dh, axis=0, keepdims=True)
            dr_ref[lo:hi, :] = _dot_nt(dz1b, wo_ref[0:RET_W, :]).astype(BF16)
            da_ref[lo:hi, :] = _dot_nt(dz1b, wo_ref[RET_W:RET_W + ATT_W, :]).astype(BF16)

    body, lead = _after(after, body)
    return pl.pallas_call(
        body, name="dh1_ln1_bwd", grid=(t // tm,),
        in_specs=lead + [_row_spec(tm, d), _row_spec(tm, f), _row_spec(tm, f), _row_spec(tm, d), _row_spec(tm, d),
                         _resident_spec(wt_gate.shape), _resident_spec(wt_up.shape), _resident_spec(w_pg.shape),
                         _resident_spec(w_out.shape), _full_spec(g1.shape)],
        out_specs=[_row_spec(tm, d), _row_spec(tm, d), _row_spec(tm, RET_W), _row_spec(tm, ATT_W), _acc_spec((8, d))],
        out_shape=[SDS((t, d), F32), SDS((t, d), BF16), SDS((t, RET_W), BF16), SDS((t, ATT_W), BF16),
                   SDS((8, d), F32)],
        compiler_params=_params(("arbitrary",)),
    )(*after, dz2, dg, dup, dsb, z1, wt_gate, wt_up, w_pg, w_out, g1)


def _in_proj_bwd(dz1, parts, wt_in, after=()):
    t, d = dz1.shape
    tm = _row_tile(t, MATMUL_ROWS)
    widths = [p.shape[1] for p in parts]

    def body(*refs):
        dz_ref, part_refs, w_ref, dx_ref = refs[0], refs[1:1 + len(parts)], refs[-2], refs[-1]
        acc = ALPHA * dz_ref[...]
        lo = 0
        for p_ref, w in zip(part_refs, widths):
            acc = acc + _dot_nn(p_ref[...], w_ref[lo:lo + w, :])
            lo += w
        dx_ref[...] = acc

    body, lead = _after(after, body)
    return pl.pallas_call(
        body, name="in_proj_bwd", grid=(t // tm,),
        in_specs=lead + [_row_spec(tm, d)] + [_row_spec(tm, w) for w in widths] + [_full_spec(wt_in.shape)],
        out_specs=_row_spec(tm, d), out_shape=SDS((t, d), F32),
        compiler_params=_params(("parallel",)),
    )(*after, dz1, *parts, wt_in)


def _weight_grad(name, me, parts, rhs, after=()):
    t, n = rhs.shape
    widths = [p.shape[1] for p in parts]
    rows = sum(widths)
    own_rows = rows // N_DEV
    tk = _row_tile(t, MATMUL_ROWS)
    step = 256

    def body(*refs):
        me_ref, part_refs, rhs_ref = refs[0], refs[1:1 + len(parts)], refs[1 + len(parts)]
        full_ref, own_ref, acc = refs[-3], refs[-2], refs[-1]
        i = pl.program_id(0)

        @pl.when(i == 0)
        def _():
            acc[...] = jnp.zeros_like(acc)

        b = rhs_ref[...].astype(BF16)
        lo = 0
        for p_ref, w in zip(part_refs, widths):
            for c0 in range(0, w, step):
                c1 = min(c0 + step, w)
                acc[lo + c0:lo + c1, :] += _dot_tn(p_ref[:, c0:c1].astype(BF16), b)
            lo += w

        @pl.when(i == pl.num_programs(0) - 1)
        def _():
            full_ref[...] = acc[...].astype(BF16)
            own_ref[...] = acc[pl.ds(pl.multiple_of(me_ref[0] * own_rows, 8), own_rows), :]

    body, lead = _after(after, body)
    return pl.pallas_call(
        body, name=name, grid=(t // tk,),
        in_specs=lead + [_smem_spec()] + [_row_spec(tk, w) for w in widths] + [_row_spec(tk, n)],
        out_specs=[_full_spec((rows, n)), _full_spec((own_rows, n))],
        out_shape=[SDS((rows, n), BF16), SDS((own_rows, n), F32)],
        scratch_shapes=[pltpu.VMEM((rows, n), F32)],
        compiler_params=_params(("arbitrary",)),
    )(*after, me, *parts, rhs)


def _log_decay(decay_f, decay_b):
    def body(f_ref, b_ref, lf_ref, lb_ref):
        lf_ref[...] = jnp.log1p(-jnp.exp2(f_ref[...]))
        lb_ref[...] = jnp.log1p(-jnp.exp2(b_ref[...]))

    return pl.pallas_call(body, name="log_decay", out_shape=[SDS(decay_f.shape, F32)] * 2)(decay_f, decay_b)


def _chunk(ref, n):
    return ref[pl.ds(pl.multiple_of(n * CHUNK, CHUNK), CHUNK), :]


def _group_sum(is_a, v):
    sa = jnp.sum(jnp.where(is_a, v, 0.0), axis=1, keepdims=True)
    sb = jnp.sum(jnp.where(is_a, 0.0, v), axis=1, keepdims=True)
    return jnp.where(is_a, sa, sb)


def _seq_spec(s, col_block):
    return pl.BlockSpec((s, LANES), lambda b, h: (b, col_block + h))


def _smem_spec():
    return pl.BlockSpec(memory_space=pltpu.SMEM)


RET_UNROLL = 4


def _chunk_loop(n_chunks, body, init):
    u = RET_UNROLL if n_chunks % RET_UNROLL == 0 else 1

    def trip(i, carry):
        for j in range(u):
            carry = body(i * u + j, carry)
        return carry

    return lax.fori_loop(0, n_chunks // u, trip, init)


def _stacked_tables(lgf_ref, lgb_ref, pair):
    lane = lax.broadcasted_iota(jnp.int32, (1, LANES), 1)
    is_a = lane < HEAD_DIM
    lgf = jnp.where(is_a, lgf_ref[2 * pair], lgf_ref[2 * pair + 1])
    lgb = jnp.where(is_a, lgb_ref[2 * pair], lgb_ref[2 * pair + 1])
    row = lax.broadcasted_iota(jnp.int32, (CHUNK, 1), 0).astype(F32)
    kdec_f, qdec_f = jnp.exp(lgf * (CHUNK - 1.0 - row)), jnp.exp(lgf * (row + 1.0))
    kdec_b, qdec_b = jnp.exp(lgb * row), jnp.exp(lgb * (CHUNK - row))
    tab = dict(
        is_a=is_a, row=row, lam_f=jnp.exp(lgf * CHUNK), lam_b=jnp.exp(lgb * CHUNK),
        kdec=jnp.concatenate([kdec_f, kdec_b], axis=1), qdec=jnp.concatenate([qdec_f, qdec_b], axis=1),
        qexp=jnp.concatenate([jnp.broadcast_to(row + 1.0, (CHUNK, LANES)),
                              jnp.broadcast_to(CHUNK - row, (CHUNK, LANES))], axis=1),
        kexp=jnp.concatenate([jnp.broadcast_to(CHUNK - 1.0 - row, (CHUNK, LANES)),
                              jnp.broadcast_to(row, (CHUNK, LANES))], axis=1),
    )
    r = lax.broadcasted_iota(jnp.int32, (2 * LANES, LANES), 0)
    c = lax.broadcasted_iota(jnp.int32, (2 * LANES, LANES), 1)
    tab["diag2"] = ((r & (LANES - 1)) < HEAD_DIM) == (c < HEAD_DIM)
    i2 = lax.broadcasted_iota(jnp.int32, (2 * CHUNK, CHUNK), 0)
    j = lax.broadcasted_iota(jnp.int32, (2 * CHUNK, CHUNK), 1)
    head_b = i2 >= CHUNK
    diff = ((i2 & (CHUNK - 1)) - j).astype(F32)
    up, dn = jnp.maximum(diff, 0.0), jnp.maximum(-diff, 0.0)
    lgf2 = jnp.where(head_b, lgf_ref[2 * pair + 1], lgf_ref[2 * pair])
    lgb2 = jnp.where(head_b, lgb_ref[2 * pair + 1], lgb_ref[2 * pair])
    ef = jnp.where(diff >= 0, jnp.exp(lgf2 * up), 0.0)
    eb = jnp.where(diff <= 0, jnp.exp(lgb2 * dn), 0.0)
    tab["d2"] = ef + eb
    tab["df2"] = ef * up
    tab["db2"] = eb * dn
    return tab


def _stack_pair(is_a, x):
    zero = jnp.zeros_like(x)
    return jnp.concatenate([jnp.where(is_a, x, zero), jnp.where(is_a, zero, x)], axis=0)


def _unstack_pair(is_a, x2):
    return jnp.where(is_a, x2[0:CHUNK, :], x2[CHUNK:2 * CHUNK, :])


def _both_ways(x, dec):
    return (jnp.concatenate([x, x], axis=1) * dec).astype(BF16)


def _scan_states(n_chunks, st, up_rows, up_lam, down_rows, down_lam):
    zero = jnp.zeros((LANES, LANES), F32)

    def up(n, r):
        new = st[n, up_rows, :]
        st[n, up_rows, :] = r
        return r * up_lam + new

    def down(s, r):
        n = n_chunks - 1 - s
        new = st[n, down_rows, :]
        st[n, down_rows, :] = r
        return r * down_lam + new

    lax.fori_loop(0, n_chunks, up, zero)
    lax.fori_loop(0, n_chunks, down, zero)


FWD_ROWS, BWD_ROWS = pl.ds(0, LANES), pl.ds(LANES, LANES)


def _retention_fwd(u, lgf, lgb, gn_gain, b_loc):
    t = u.shape[0]
    s = t // b_loc
    n_chunks = s // CHUNK
    pairs = RET_HEADS // 2

    def body(lgf_ref, lgb_ref, q_ref, k_ref, v_ref, g_ref, gain_ref, r_ref, y_ref, st):
        tab = _stacked_tables(lgf_ref, lgb_ref, pl.program_id(1))
        is_a = tab["is_a"]

        def kv_body(n, _):
            k8 = _chunk(k_ref, n).astype(F32) * Q_SCALE
            st[n] = jnp.where(tab["diag2"], _dot_tn(_both_ways(k8, tab["kdec"]), _chunk(v_ref, n)), 0.0)
            return 0

        _chunk_loop(n_chunks, kv_body, 0)
        _scan_states(n_chunks, st, FWD_ROWS, tab["lam_f"], BWD_ROWS, tab["lam_b"])

        def out_body(n, _):
            q = _chunk(q_ref, n)
            k8 = (_chunk(k_ref, n).astype(F32) * Q_SCALE).astype(BF16)
            v = _chunk(v_ref, n)
            p2 = (_dot_nt(_stack_pair(is_a, q), k8) * tab["d2"]).astype(BF16)
            y = _unstack_pair(is_a, _dot_nn(p2, v))
            y = y + _dot_nn(_both_ways(q.astype(F32), tab["qdec"]), st[n].astype(BF16))
            rows = pl.ds(pl.multiple_of(n * CHUNK, CHUNK), CHUNK)
            y_ref[rows, :] = y
            mu = _group_sum(is_a, y) * (1.0 / HEAD_DIM)
            dlt = y - mu
            var = _group_sum(is_a, dlt * dlt) * (1.0 / HEAD_DIM)
            xhat = dlt * lax.rsqrt(var + GN_EPS)
            gate = _chunk(g_ref, n).astype(F32)
            r_ref[rows, :] = (xhat * gain_ref[...] * gate * _sigmoid(gate)).astype(BF16)
            return 0

        _chunk_loop(n_chunks, out_body, 0)

    lane_blk = lambda c0: _seq_spec(s, c0 // LANES)
    return pl.pallas_call(
        body, name="retention_fwd", grid=(b_loc, pairs),
        in_specs=[_smem_spec(), _smem_spec(), lane_blk(C_RQ), lane_blk(C_RK), lane_blk(C_RV), lane_blk(C_RG),
                  pl.BlockSpec((1, LANES), lambda b, h: (0, h))],
        out_specs=[_seq_spec(s, 0), _seq_spec(s, 0)],
        out_shape=[SDS((t, RET_W), BF16), SDS((t, RET_W), F32)],
        scratch_shapes=[pltpu.VMEM((n_chunks, 2 * LANES, LANES), F32)],
        compiler_params=_params(("parallel", "parallel")),
    )(lgf, lgb, u, u, u, u, gn_gain)


ST_GAIN, ST_XF, ST_XB, ST_IFA, ST_IFB, ST_IBA, ST_IBB, ST_LF, ST_LB = 0, 1, 2, 3, 4, 5, 6, 8, 9
ST_ROWS = 16


def _retention_bwd(u, y_pre, dr, lgf, lgb, gn_gain, b_loc, after=()):
    t = u.shape[0]
    s = t // b_loc
    n_chunks = s // CHUNK
    pairs = RET_HEADS // 2

    def body(lgf_ref, lgb_ref, q_ref, k_ref, v_ref, g_ref, y_ref, dr_ref, gain_ref,
             dq_ref, dk_ref, dv_ref, dg_ref, st_ref, st, gr, dy_s):
        tab = _stacked_tables(lgf_ref, lgb_ref, pl.program_id(1))
        is_a, row = tab["is_a"], tab["row"]
        gain = gain_ref[...]

        def norm_body(n, dgain):
            rows = pl.ds(pl.multiple_of(n * CHUNK, CHUNK), CHUNK)
            y = y_ref[rows, :]
            mu = _group_sum(is_a, y) * (1.0 / HEAD_DIM)
            dlt = y - mu
            var = _group_sum(is_a, dlt * dlt) * (1.0 / HEAD_DIM)
            rstd = lax.rsqrt(var + GN_EPS)
            xhat = dlt * rstd
            gate = g_ref[rows, :].astype(F32)
            sg = _sigmoid(gate)
            silu = gate * sg
            d_out = dr_ref[rows, :].astype(F32)
            dg_ref[rows, :] = (d_out * xhat * gain * (sg * (1.0 + gate * (1.0 - sg)))).astype(BF16)
            dxh = d_out * gain * silu
            m1 = _group_sum(is_a, dxh) * (1.0 / HEAD_DIM)
            m2 = _group_sum(is_a, dxh * xhat) * (1.0 / HEAD_DIM)
            dy = (rstd * (dxh - m1 - xhat * m2)).astype(BF16)
            dy_s[rows, :] = dy
            k8 = k_ref[rows, :].astype(F32) * Q_SCALE
            st[n] = jnp.where(tab["diag2"], _dot_tn(_both_ways(k8, tab["kdec"]), v_ref[rows, :]), 0.0)
            qf = q_ref[rows, :].astype(F32)
            gr[n] = jnp.where(tab["diag2"], _dot_tn(_both_ways(qf, tab["qdec"]), dy), 0.0)
            return dgain + jnp.sum(d_out * xhat * silu, axis=0, keepdims=True)

        dgain = _chunk_loop(n_chunks, norm_body, jnp.zeros((1, LANES), F32))
        _scan_states(n_chunks, st, FWD_ROWS, tab["lam_f"], BWD_ROWS, tab["lam_b"])
        _scan_states(n_chunks, gr, BWD_ROWS, tab["lam_b"], FWD_ROWS, tab["lam_f"])
        colsum = lambda x: jnp.sum(x, axis=0, keepdims=True)

        def grad_body(n, carry):
            xfb, ifa, ifb, iba, ibb, lf, lb = carry
            rows = pl.ds(pl.multiple_of(n * CHUNK, CHUNK), CHUNK)
            q = q_ref[rows, :]
            qf = q.astype(F32)
            k8f = k_ref[rows, :].astype(F32) * Q_SCALE
            k8 = k8f.astype(BF16)
            v = v_ref[rows, :]
            dy = dy_s[rows, :]
            q2, dy2 = _stack_pair(is_a, q), _stack_pair(is_a, dy)
            sc = _dot_nt(q2, k8)
            dp = _dot_nt(dy2, v)
            a2 = (sc * tab["d2"]).astype(BF16)
            ds2 = (dp * tab["d2"]).astype(BF16)
            dq = _unstack_pair(is_a, _dot_nn(ds2, k8))
            dk = _dot_tn(ds2, q2)
            dv = _dot_tn(a2, dy2)
            prod = sc * dp
            pf, pb = prod * tab["df2"], prod * tab["db2"]
            ifa, ifb = ifa + colsum(pf[0:CHUNK, :]), ifb + colsum(pf[CHUNK:2 * CHUNK, :])
            iba, ibb = iba + colsum(pb[0:CHUNK, :]), ibb + colsum(pb[CHUNK:2 * CHUNK, :])
            states, sgrads = st[n], gr[n]
            sb, gb = states.astype(BF16), sgrads.astype(BF16)
            dqc = _dot_nt(dy, sb) * tab["qdec"]
            dkc = _dot_nt(v, gb) * tab["kdec"]
            dv = dv + _dot_nn(_both_ways(k8f, tab["kdec"]), gb)
            dq_ref[rows, :] = (dq + dqc[:, 0:LANES] + dqc[:, LANES:2 * LANES]).astype(BF16)
            dk_ref[rows, :] = ((dk + dkc[:, 0:LANES] + dkc[:, LANES:2 * LANES]) * Q_SCALE).astype(BF16)
            dv_ref[rows, :] = dv.astype(BF16)
            q2w, k2w = jnp.concatenate([qf, qf], axis=1), jnp.concatenate([k8f, k8f], axis=1)
            xfb = xfb + colsum(tab["qexp"] * q2w * dqc + tab["kexp"] * k2w * dkc)
            prod_s = sgrads * states
            lf, lb = lf + colsum(prod_s[0:LANES, :]), lb + colsum(prod_s[LANES:2 * LANES, :])
            return xfb, ifa, ifb, iba, ibb, lf, lb

        z = jnp.zeros((1, LANES), F32)
        init = (jnp.zeros((1, 2 * LANES), F32), z, z, z, z, z, z)
        xfb, ifa, ifb, iba, ibb, lf, lb = _chunk_loop(n_chunks, grad_body, init)
        st_ref[...] = jnp.zeros_like(st_ref)
        st_ref[ST_GAIN:ST_GAIN + 1, :] = dgain
        st_ref[ST_XF:ST_XF + 1, :] = xfb[:, 0:LANES]
        st_ref[ST_XB:ST_XB + 1, :] = xfb[:, LANES:2 * LANES]
        st_ref[ST_IFA:ST_IFA + 1, :] = ifa
        st_ref[ST_IFB:ST_IFB + 1, :] = ifb
        st_ref[ST_IBA:ST_IBA + 1, :] = iba
        st_ref[ST_IBB:ST_IBB + 1, :] = ibb
        st_ref[ST_LF:ST_LF + 1, :] = lf * (CHUNK * tab["lam_f"])
        st_ref[ST_LB:ST_LB + 1, :] = lb * (CHUNK * tab["lam_b"])

    lane_blk = lambda c0: _seq_spec(s, c0 // LANES)
    seq0 = _seq_spec(s, 0)
    state = pltpu.VMEM((n_chunks, 2 * LANES, LANES), F32)
    body, lead = _after(after, body)
    return pl.pallas_call(
        body, name="retention_bwd", grid=(b_loc, pairs),
        in_specs=lead + [_smem_spec(), _smem_spec(), lane_blk(C_RQ), lane_blk(C_RK), lane_blk(C_RV), lane_blk(C_RG),
                         seq0, seq0, pl.BlockSpec((1, LANES), lambda b, h: (0, h))],
        out_specs=[seq0] * 4 + [pl.BlockSpec((ST_ROWS, LANES), lambda b, h: (b, h))],
        out_shape=[SDS((t, RET_W), BF16)] * 4 + [SDS((b_loc * ST_ROWS, RET_W), F32)],
        scratch_shapes=[state, state, pltpu.VMEM((s, LANES), BF16)],
        compiler_params=_params(("parallel", "parallel")),
    )(*after, lgf, lgb, u, u, u, u, y_pre, dr, gn_gain)


GW = GROUP * HEAD_DIM
KEYS = 3 * BLOCK


def _attn_tables(g, bias_ref):
    r = lax.broadcasted_iota(jnp.int32, (GROUP * BLOCK, KEYS), 0)
    kj = lax.broadcasted_iota(jnp.int32, (GROUP * BLOCK, KEYS), 1)
    qi = r & (BLOCK - 1)
    hh = lax.shift_right_logical(r, 7)
    dist = jnp.abs(kj - BLOCK - qi)
    slope = jnp.exp2(-(GROUP * g + hh + 1).astype(F32) * (8.0 / ATTN_HEADS))
    bias_ref[...] = jnp.where(dist <= BLOCK, -slope * dist.astype(F32), NEG_INF)


def _tile_keys(x_ref, g, scale, pad_ref, s):
    r = lax.broadcasted_iota(jnp.int32, (LANES, GW), 0)
    c = lax.broadcasted_iota(jnp.int32, (LANES, GW), 1)
    place = jnp.where(r == g * HEAD_DIM + (c & (HEAD_DIM - 1)), 1.0, 0.0).astype(BF16)
    pad_ref[0:BLOCK, :] = jnp.zeros((BLOCK, GW), BF16)
    pad_ref[BLOCK + s:2 * BLOCK + s, :] = jnp.zeros((BLOCK, GW), BF16)
    pad_ref[BLOCK:BLOCK + s, :] = (_dot_nn(x_ref[...], place) * scale).astype(BF16)


def _stack_heads(x):
    lane_h = lax.shift_right_logical(lax.broadcasted_iota(jnp.int32, (1, GW), 1), 6)
    zero = jnp.zeros_like(x)
    return jnp.concatenate([jnp.where(lane_h == h, x, zero) for h in range(GROUP)], axis=0)


def _unstack_heads(x4):
    lane_h = lax.shift_right_logical(lax.broadcasted_iota(jnp.int32, (1, GW), 1), 6)
    out = jnp.zeros((BLOCK, GW), F32)
    for h in range(GROUP):
        out = out + jnp.where(lane_h == h, x4[h * BLOCK:(h + 1) * BLOCK, :], 0.0)
    return out


def _sink_column(sink_ref, g):
    rh = lax.shift_right_logical(lax.broadcasted_iota(jnp.int32, (GROUP * BLOCK, 1), 0), 7)
    col = jnp.zeros((GROUP * BLOCK, 1), F32)
    for h in range(GROUP):
        col = jnp.where(rh == h, sink_ref[GROUP * g + h], col)
    return col


def _attn_probs(qm, k3, bias_ref, sink_col, n, s):
    logits = _dot_nt(qm, k3) + bias_ref[...]
    kpos = n * BLOCK - BLOCK + lax.broadcasted_iota(jnp.int32, (1, KEYS), 1)
    logits = jnp.where((kpos >= 0) & (kpos < s), logits, NEG_INF)
    m = jnp.maximum(jnp.max(logits, axis=1, keepdims=True), sink_col)
    e = jnp.exp(logits - m)
    e_sink = jnp.exp(sink_col - m)
    inv = 1.0 / (jnp.sum(e, axis=1, keepdims=True) + e_sink)
    return e * inv, e_sink * inv


ATT_SUB = 2


def _attn_specs(s, n_steps):
    rows = ATT_SUB * BLOCK
    q_spec = pl.BlockSpec((rows, GW), lambda b, g, n: (b * n_steps + n, C_AQ // GW + g))
    k_spec = pl.BlockSpec((s, LANES), lambda b, g, n: (b, C_AK // LANES))
    v_spec = pl.BlockSpec((s, LANES), lambda b, g, n: (b, C_AV // LANES))
    o_spec = pl.BlockSpec((rows, GW), lambda b, g, n: (b * n_steps + n, g))
    return q_spec, k_spec, v_spec, o_spec


def _attention_fwd(u, sink, b_loc):
    t = u.shape[0]
    s = t // b_loc
    n_steps = s // (ATT_SUB * BLOCK)

    def body(sink_ref, q_ref, k_ref, v_ref, o_ref, kpad, vpad, bias):
        g, step = pl.program_id(1), pl.program_id(2)

        @pl.when(step == 0)
        def _():
            _attn_tables(g, bias)
            _tile_keys(k_ref, g, Q_SCALE, kpad, s)
            _tile_keys(v_ref, g, 1.0, vpad, s)

        sink_col = _sink_column(sink_ref, g)
        for j in range(ATT_SUB):
            n = step * ATT_SUB + j
            rows = pl.ds(j * BLOCK, BLOCK)
            keys = pl.ds(pl.multiple_of(n * BLOCK, BLOCK), KEYS)
            p, _ = _attn_probs(_stack_heads(q_ref[rows, :]), kpad[keys, :], bias, sink_col, n, s)
            o_ref[rows, :] = _unstack_heads(_dot_nn(p.astype(BF16), vpad[keys, :])).astype(BF16)

    q_spec, k_spec, v_spec, o_spec = _attn_specs(s, n_steps)
    pad = pltpu.VMEM((s + 2 * BLOCK, GW), BF16)
    return pl.pallas_call(
        body, name="attention_fwd", grid=(b_loc, KV_HEADS, n_steps),
        in_specs=[_smem_spec(), q_spec, k_spec, v_spec], out_specs=o_spec,
        out_shape=SDS((t, ATT_W), BF16),
        scratch_shapes=[pad, pad, pltpu.VMEM((GROUP * BLOCK, KEYS), F32)],
        compiler_params=_params(("parallel", "arbitrary", "arbitrary")),
    )(sink, u, u, u)


def _fold_groups(x, g):
    x = x + pltpu.roll(x, 2 * HEAD_DIM, 1)
    x = x + pltpu.roll(x, HEAD_DIM, 1)
    lane_g = lax.shift_right_logical(lax.broadcasted_iota(jnp.int32, (1, LANES), 1), 6)
    return jnp.where(lane_g == g, x[:, 0:LANES], 0.0)


def _attention_bwd(u, da, sink, b_loc):
    t = u.shape[0]
    s = t // b_loc
    n_steps = s // (ATT_SUB * BLOCK)

    def body(sink_ref, q_ref, k_ref, v_ref, do_ref, dq_ref, dk_ref, dv_ref, dsink_ref, kpad, vpad, bias, dk_acc, dv_acc):
        g, step = pl.program_id(1), pl.program_id(2)

        @pl.when(step == 0)
        def _():
            _attn_tables(g, bias)
            _tile_keys(k_ref, g, Q_SCALE, kpad, s)
            _tile_keys(v_ref, g, 1.0, vpad, s)
            dsink_ref[...] = jnp.zeros_like(dsink_ref)

        @pl.when((step == 0) & (g == 0))
        def _():
            dk_acc[...] = jnp.zeros_like(dk_acc)
            dv_acc[...] = jnp.zeros_like(dv_acc)

        sink_col = _sink_column(sink_ref, g)
        head_row = lax.broadcasted_iota(jnp.int32, dsink_ref.shape, 0)
        upd = jnp.zeros(dsink_ref.shape, F32)
        for j in range(ATT_SUB):
            n = step * ATT_SUB + j
            rows = pl.ds(j * BLOCK, BLOCK)
            keys = pl.ds(pl.multiple_of(n * BLOCK, BLOCK), KEYS)
            qm = _stack_heads(q_ref[rows, :])
            k3, v3 = kpad[keys, :], vpad[keys, :]
            p, p_sink = _attn_probs(qm, k3, bias, sink_col, n, s)
            dom = _stack_heads(do_ref[rows, :])
            dp = _dot_nt(dom, v3)
            delta = jnp.sum(p * dp, axis=1, keepdims=True)
            ds_mat = (p * (dp - delta)).astype(BF16)
            dq_ref[rows, :] = _unstack_heads(_dot_nn(ds_mat, k3)).astype(BF16)
            dk_acc[keys, :] += _fold_groups(_dot_tn(ds_mat, qm), g) * Q_SCALE
            dv_acc[keys, :] += _fold_groups(_dot_tn(p.astype(BF16), dom), g)
            w = p_sink * delta
            for h in range(GROUP):
                upd = upd + jnp.where(head_row == h, -jnp.sum(w[h * BLOCK:(h + 1) * BLOCK, :]), 0.0)
        dsink_ref[...] += upd

        @pl.when((step == n_steps - 1) & (g == KV_HEADS - 1))
        def _():
            dk_ref[...] = dk_acc[BLOCK:BLOCK + s, :].astype(BF16)
            dv_ref[...] = dv_acc[BLOCK:BLOCK + s, :].astype(BF16)

    q_spec, k_spec, v_spec, o_spec = _attn_specs(s, n_steps)
    kv_out = pl.BlockSpec((s, LANES), lambda b, g, n: (b, 0))
    pad = pltpu.VMEM((s + 2 * BLOCK, GW), BF16)
    acc = pltpu.VMEM((s + 2 * BLOCK, LANES), F32)
    return pl.pallas_call(
        body, name="attention_bwd", grid=(b_loc, KV_HEADS, n_steps),
        in_specs=[_smem_spec(), q_spec, k_spec, v_spec, o_spec],
        out_specs=[o_spec, kv_out, kv_out, pl.BlockSpec((8, LANES), lambda b, g, n: (b * KV_HEADS + g, 0))],
        out_shape=[SDS((t, ATT_W), BF16), SDS((t, KV_W), BF16), SDS((t, KV_W), BF16),
                   SDS((b_loc * KV_HEADS * 8, LANES), F32)],
        scratch_shapes=[pad, pad, pltpu.VMEM((GROUP * BLOCK, KEYS), F32), acc, acc],
        compiler_params=_params(("arbitrary", "arbitrary", "arbitrary")),
    )(sink, u, u, u, da)


def _pack_small(acc2, acc1, ret_stats, dsink, b_loc, d):
    pairs = RET_HEADS // 2

    def body(acc2_ref, acc1_ref, st_ref, dsink_ref, out_ref):
        out_ref[...] = jnp.zeros_like(out_ref)
        out_ref[ROW_LN1G:ROW_LN1G + 1, :] = acc1_ref[0:1, :]
        out_ref[ROW_LN1B:ROW_LN1B + 1, :] = acc1_ref[1:2, :]
        out_ref[ROW_LN2G:ROW_LN2G + 1, :] = acc2_ref[1:2, :]
        out_ref[ROW_LN2B:ROW_LN2B + 1, :] = acc2_ref[2:3, :]
        out_ref[ROW_LOSS:ROW_LOSS + 1, :] = acc2_ref[0:1, :]
        st = st_ref[0:ST_ROWS, :]
        for b in range(1, b_loc):
            st = st + st_ref[b * ST_ROWS:(b + 1) * ST_ROWS, :]
        out_ref[ROW_GN:ROW_GN + 1, 0:RET_W] = st[ST_GAIN:ST_GAIN + 1, :]
        lane = lax.broadcasted_iota(jnp.int32, (1, d), 1)
        misc = jnp.zeros((1, d), F32)
        for pr in range(pairs):
            blk = st[:, pr * LANES:(pr + 1) * LANES]
            half = lax.broadcasted_iota(jnp.int32, (1, LANES), 1) < HEAD_DIM
            for h in range(2):
                sel = half if h == 0 else jnp.logical_not(half)
                cross_f = jnp.sum(jnp.where(sel, blk[ST_XF:ST_XF + 1, :] + blk[ST_LF:ST_LF + 1, :], 0.0))
                cross_b = jnp.sum(jnp.where(sel, blk[ST_XB:ST_XB + 1, :] + blk[ST_LB:ST_LB + 1, :], 0.0))
                intra_f = jnp.sum(blk[ST_IFA + h:ST_IFA + h + 1, :])
                intra_b = jnp.sum(blk[ST_IBA + h:ST_IBA + h + 1, :])
                head = 2 * pr + h
                misc = jnp.where(lane == MISC_DF + head, cross_f + intra_f, misc)
                misc = jnp.where(lane == MISC_DB + head, cross_b + intra_b, misc)
        for g in range(KV_HEADS):
            tot = dsink_ref[g * 8:(g + 1) * 8, :]
            for b in range(1, b_loc):
                tot = tot + dsink_ref[(b * KV_HEADS + g) * 8:(b * KV_HEADS + g + 1) * 8, :]
            for h in range(GROUP):
                misc = jnp.where(lane == MISC_SINK + GROUP * g + h, jnp.sum(tot[h:h + 1, 0:1]), misc)
        out_ref[ROW_MISC:ROW_MISC + 1, :] = misc

    return pl.pallas_call(body, name="pack_small", out_shape=SDS((SMALL_ROWS, d), F32))(acc2, acc1, ret_stats, dsink)


BIG = ("w_in", "w_out", "w_ffn_gate", "w_ffn_up", "w_ffn_down", "w_ple_proj", "w_ple_gate")
TRANSPOSED_OUTSIDE = ("w_in", "w_ffn_gate", "w_ffn_up")
TRANSPOSED_HERE = ("w_ple_proj",)
SMALL = ("ret_decay_fwd", "ret_decay_bwd", "ret_gn_gain", "attn_sink", "ln1_gain", "ln1_bias", "ln2_gain", "ln2_bias")
ORDER = ("w_in", "ret_decay_fwd", "ret_decay_bwd", "ret_gn_gain", "attn_sink", "w_out", "ln1_gain", "ln1_bias",
         "w_ffn_gate", "w_ffn_up", "w_ffn_down", "w_ple_proj", "w_ple_gate", "ln2_gain", "ln2_bias")


GATHER_ORDER = ("w_in", "w_out", "w_ffn_gate", "w_ffn_up", "w_ple_gate", "w_ple_proj", "w_ffn_down")


def _local_step(x2, p2, target2, fetch, publish, small, b_loc, me):
    d = x2.shape[1]
    lgf, lgb = _log_decay(small["ret_decay_fwd"], small["ret_decay_bwd"])
    lgf1, lgb1, sink1 = lgf.reshape(-1), lgb.reshape(-1), small["attn_sink"].reshape(-1)
    (w_in,) = fetch(("w_in",), ())
    u, xb = _in_proj(x2, w_in)
    r, y_pre = _retention_fwd(u, lgf1, lgb1, small["ret_gn_gain"], b_loc)
    a = _attention_fwd(u, sink1, b_loc)
    w_out, w_gate, w_up = fetch(("w_out", "w_ffn_gate", "w_ffn_up"), (r, a))
    z1, h1b, dact_dg, dact_du, act = _mix_ln1_ffn_up(
        r, a, x2, w_out, w_gate, w_up, small["ln1_gain"], small["ln1_bias"])
    w_pg, w_pe, w_down = fetch(("w_ple_gate", "w_ple_proj", "w_ffn_down"), (act,))
    dz2, dz2b, dsb, dpleb, dg, dup, acc2 = _ffn_down_ln2_loss(
        act, dact_dg, dact_du, h1b, p2, z1, target2, w_down, w_pg, w_pe,
        small["ln1_gain"], small["ln1_bias"], small["ln2_gain"], small["ln2_bias"])
    own = {}

    def grad(name, parts, rhs, after=()):
        whole, own[name] = _weight_grad("grad_" + name, me, parts, rhs, after)
        return whole

    t2 = publish("ffn", dict(w_ffn_down=grad("w_ffn_down", [act], dz2b),
                             w_ple_proj=grad("w_ple_proj", [dpleb], p2),
                             w_ple_gate=grad("w_ple_gate", [h1b], dsb),
                             w_ffn_gate=grad("w_ffn_gate", [dg], h1b),
                             w_ffn_up=grad("w_ffn_up", [dup], h1b)))
    dz1, dz1b, dr, da, acc1 = _dh1_ln1_bwd(dz2, dg, dup, dsb, z1, w_gate, w_up, w_pg, w_out, small["ln1_gain"], t2)
    t3 = publish("out", dict(w_out=grad("w_out", [r, a], dz1b)))
    dq, dk, dv, dgate, ret_stats = _retention_bwd(u, y_pre, dr, lgf1, lgb1, small["ret_gn_gain"], b_loc, t3)
    daq, dak, dav, dsink = _attention_bwd(u, da, sink1, b_loc)
    parts = [dq, dk, dv, dgate, daq, dak, dav]
    small_part = _pack_small(acc2, acc1, ret_stats, dsink, b_loc, d)
    t4 = publish("in", dict(w_in=grad("w_in", parts, xb)), small_part)
    grad_x = _in_proj_bwd(dz1, parts, w_in, t4)
    return grad_x, own, small_part


def kernel(x, p, w_in, ret_decay_fwd, ret_decay_bwd, ret_gn_gain, attn_sink, w_out, ln1_gain, ln1_bias, w_ffn_gate, w_ffn_up, w_ffn_down, w_ple_proj, w_ple_gate, ln2_gain, ln2_bias, loss_target, m_w_in, m_ret_decay_fwd, m_ret_decay_bwd, m_ret_gn_gain, m_attn_sink, m_w_out, m_ln1_gain, m_ln1_bias, m_w_ffn_gate, m_w_ffn_up, m_w_ffn_down, m_w_ple_proj, m_w_ple_gate, m_ln2_gain, m_ln2_bias, v_w_in, v_ret_decay_fwd, v_ret_decay_bwd, v_ret_gn_gain, v_attn_sink, v_w_out, v_ln1_gain, v_ln1_bias, v_w_ffn_gate, v_w_ffn_up, v_w_ffn_down, v_w_ple_proj, v_w_ple_gate, v_ln2_gain, v_ln2_bias):
    given = dict(locals())

    def strip(n, a):
        if n not in BIG:
            return a
        return a[0].T if n in TRANSPOSED_OUTSIDE else a[0]

    def restore(n, a):
        if n not in BIG:
            return a
        return (a.T if n in TRANSPOSED_OUTSIDE else a)[None]

    w = {n: strip(n, given[n]) for n in ORDER}
    m = {n: strip(n, given["m_" + n]) for n in ORDER}
    v = {n: strip(n, given["v_" + n]) for n in ORDER}
    b_loc, s, d = x.shape
    x2 = x.reshape(b_loc * s, d)
    p2 = p[0].reshape(b_loc * s, p.shape[-1])
    target2 = loss_target.reshape(b_loc * s, d)

    small = {n: w[n] for n in SMALL}
    me = (4 * lax.axis_index("x") + 2 * lax.axis_index("y") + lax.axis_index("c")).astype(jnp.int32).reshape(1)

    gathered = _prep_shards(me, {n: w[n] for n in BIG})
    gather = _split_copy_start("gather_start", [(gathered[n],) for n in GATHER_ORDER],
                               [_gather_copy_near] + [_gather_copy] * (len(GATHER_ORDER) - 1))

    def fetch(names, after):
        if names == ("w_in",):
            near = _split_copy_wait("gather_wait_w_in_near", gather, [0], list(after))
            passed = _split_copy_start("gather_pass_w_in", near, [_gather_copy_pass])
            return [_split_copy_wait("gather_wait_w_in", passed, [0], [])[0][0]]
        which = [GATHER_ORDER.index(n) for n in names]
        got = _split_copy_wait("gather_wait_" + names[0], gather, which, list(after))
        return [item[0] for item in got]

    scatters = []

    def publish(tag, products, small_sums=None):
        items = [(products[n], lax.empty((N_DEV - 1, products[n].shape[0] // N_DEV, products[n].shape[1]), BF16))
                 for n in products]
        copies = [_scatter_copy] * len(items)
        if small_sums is not None:
            items.append((small_sums, lax.empty((N_DEV - 1,) + small_sums.shape, F32)))
            copies.append(_small_copy)
        started = _split_copy_start("scatter_start_" + tag, items, copies)
        scatters.append((list(products), small_sums is not None, started))
        return (started["token"],)

    grad_x, own, small_part = _local_step(x2, p2, target2, fetch, publish, small, b_loc, me)

    out_g, out_d, out_m, out_v = {}, {}, {}, {}
    after = [grad_x]
    for names, with_small, started in scatters:
        landed = _split_copy_wait("scatter_wait_" + names[0], started, list(range(len(started["items"]))), after)
        if with_small:
            loss, sg, sd, sm, sv = _small_adamw(
                me, small_part, landed[-1][1], small, {n: m[n] for n in SMALL}, {n: v[n] for n in SMALL})
            for dst, src in ((out_g, sg), (out_d, sd), (out_m, sm), (out_v, sv)):
                dst.update(src)
        for n, (_, recv) in zip(names, landed):
            out_g[n], out_d[n], out_m[n], out_v[n] = _reduce_adamw(
                n, own[n], recv, w[n], m[n], v[n], n in TRANSPOSED_HERE)
        after = [out_v[names[-1]]]

    outs = [loss[0, 0], grad_x.reshape(x.shape)]
    for group in (out_g, out_d, out_m, out_v):
        outs += [restore(n, group[n]) for n in ORDER]
    return tuple(outs)
```

```python
import functools

import jax
import jax.numpy as jnp
from jax import lax
from jax.experimental import pallas as pl
from jax.experimental.pallas import tpu as pltpu

F32, BF16 = jnp.float32, jnp.bfloat16
SDS = jax.ShapeDtypeStruct
MESH = pl.DeviceIdType.MESH

N_DEV = 8
HEAD_DIM = 64
RET_HEADS = 8
ATTN_HEADS = 8
KV_HEADS = 2
GROUP = ATTN_HEADS // KV_HEADS
RET_W = RET_HEADS * HEAD_DIM
ATT_W = ATTN_HEADS * HEAD_DIM
KV_W = KV_HEADS * HEAD_DIM
LANES = 128
CHUNK = 128
BLOCK = 128
Q_SCALE = HEAD_DIM ** -0.5
ALPHA = 2.0 ** 0.25
LN_EPS = 1e-5
GN_EPS = 1e-5
NEG_INF = -1e30
C_RQ, C_RK, C_RV, C_RG = 0, RET_W, 2 * RET_W, 3 * RET_W
C_AQ = 4 * RET_W
C_AK = C_AQ + ATT_W
C_AV = C_AK + KV_W
IN_W = C_AV + KV_W

ADAM_LR = 0.001
ADAM_B1 = 0.9
ADAM_B2 = 0.999
ADAM_EPS = 1e-08
ADAM_WD = 0.01
ADAM_STEP = 10

VMEM_LIMIT = 56 * 1024 * 1024
MATMUL_ROWS = 512
EPILOGUE_ROWS = 256
SUB_ROWS = 256
SMALL_ROWS = 16
ROW_LN1G, ROW_LN1B, ROW_LN2G, ROW_LN2B, ROW_LOSS, ROW_GN, ROW_MISC = 0, 1, 2, 3, 4, 5, 6
MISC_DF, MISC_DB, MISC_SINK = 0, 8, 16


def _dot_nn(a, b):
    return lax.dot_general(a, b, (((1,), (0,)), ((), ())), preferred_element_type=F32)


def _dot_nt(a, b):
    return lax.dot_general(a, b, (((1,), (1,)), ((), ())), preferred_element_type=F32)


def _dot_tn(a, b):
    return lax.dot_general(a, b, (((0,), (0,)), ((), ())), preferred_element_type=F32)


def _params(sem=None, vmem=VMEM_LIMIT):
    kw = {"vmem_limit_bytes": vmem}
    if sem is not None:
        kw["dimension_semantics"] = sem
    return pltpu.CompilerParams(**kw)


def _row_tile(t, want=512):
    tm = want
    while t % tm:
        tm //= 2
    return tm


def _sigmoid(x):
    return 1.0 / (1.0 + jnp.exp(-x))


def _layer_norm_stats(z):
    mu = jnp.mean(z, axis=1, keepdims=True)
    d = z - mu
    var = jnp.mean(d * d, axis=1, keepdims=True)
    rstd = lax.rsqrt(var + LN_EPS)
    return d * rstd, rstd


def _layer_norm_bwd(dxh, xhat, rstd):
    m1 = jnp.mean(dxh, axis=1, keepdims=True)
    m2 = jnp.mean(dxh * xhat, axis=1, keepdims=True)
    return rstd * (dxh - m1 - xhat * m2)


def _prep_shards(me, shards):
    names = list(shards)

    def body(me_ref, *refs):
        for name, src, dst in zip(names, refs[:len(names)], refs[len(names):]):
            val = src[...]
            dst[...] = (val.T if name in TRANSPOSED_HERE else val).astype(BF16)

    shape = lambda n, a: a.shape[::-1] if n in TRANSPOSED_HERE else a.shape
    shapes = [shape(n, shards[n]) for n in names]
    out = pl.pallas_call(
        body, name="prep_shards",
        grid_spec=pltpu.PrefetchScalarGridSpec(
            num_scalar_prefetch=1, grid=(1,),
            in_specs=[pl.BlockSpec(shards[n].shape, lambda i, me_ref: (0, 0)) for n in names],
            out_specs=[pl.BlockSpec(s, lambda i, me_ref: (me_ref[0], 0)) for s in shapes]),
        out_shape=[SDS((N_DEV * s[0], s[1]), BF16) for s in shapes], compiler_params=_params(("arbitrary",)),
    )(me, *[shards[n] for n in names])
    return dict(zip(names, out))


def _mesh_pos():
    return lax.axis_index("x"), lax.axis_index("y"), lax.axis_index("c")


HBM_SPEC = pl.BlockSpec(memory_space=pltpu.HBM)
SEM_SPEC = pl.BlockSpec(memory_space=pltpu.SEMAPHORE)
ANY_SPEC = pl.BlockSpec(memory_space=pl.ANY)
SIDE_EFFECT = pltpu.SideEffectType.DATAFLOW_SIDE_EFFECTING
PEER_SEMS = pltpu.SemaphoreType.DMA((N_DEV - 1,))


def _in_hbm(a):
    return pltpu.with_memory_space_constraint(a, pltpu.HBM)


def _split_copy_start(name, items, copies):
    n = len(items)
    flat = [a for it in items for a in it]
    k = len(flat)

    def body(*refs):
        arr, sems = list(refs[:k]), refs[k:k + 2 * n]
        for i, it in enumerate(items):
            mine = [arr.pop(0) for _ in it]
            for m in range(1, N_DEV):
                cp = copies[i](m, mine, sems[i].at[m - 1], sems[n + i].at[m - 1])
                if cp is not None:
                    cp.start()
        token = refs[-1]
        token[...] = jnp.zeros_like(token)

    res = pl.pallas_call(
        body, name=name,
        out_shape=[PEER_SEMS] * (2 * n) + [pltpu.HBM(a.shape, a.dtype) for a in flat] + [SDS((8, LANES), F32)],
        in_specs=[HBM_SPEC] * k,
        out_specs=[SEM_SPEC] * (2 * n) + [HBM_SPEC] * k + [pl.BlockSpec(memory_space=pltpu.VMEM)],
        input_output_aliases={j: 2 * n + j for j in range(k)},
        compiler_params=pltpu.CompilerParams(has_side_effects=SIDE_EFFECT),
    )(*[_in_hbm(a) for a in flat])
    thru, out_items = list(res[2 * n:2 * n + k]), []
    for it in items:
        out_items.append(tuple(thru.pop(0) for _ in it))
    return dict(send=res[:n], recv=res[n:2 * n], items=out_items, token=res[-1], copies=copies)


def _split_copy_wait(name, started, which, after):
    items = [started["items"][i] for i in which]
    copies = [started["copies"][i] for i in which]
    n = len(items)
    flat = [a for it in items for a in it]
    k = len(flat)

    def body(*refs):
        arr, sems = list(refs[:k]), refs[k:k + 2 * n]
        for i, it in enumerate(items):
            mine = [arr.pop(0) for _ in it]
            for m in range(1, N_DEV):
                cp = copies[i](m, mine, sems[i].at[m - 1], sems[n + i].at[m - 1])
                if cp is not None:
                    cp.wait_send()
                    cp.wait_recv()

    res = pl.pallas_call(
        body, name=name,
        out_shape=[pltpu.HBM(a.shape, a.dtype) for a in flat],
        in_specs=[HBM_SPEC] * k + [SEM_SPEC] * (2 * n) + [ANY_SPEC] * len(after),
        out_specs=[HBM_SPEC] * k,
        input_output_aliases={j: j for j in range(k)},
        compiler_params=pltpu.CompilerParams(has_side_effects=SIDE_EFFECT),
    )(*flat, *[started["send"][i] for i in which], *[started["recv"][i] for i in which], *[_in_hbm(a) for a in after])
    thru, out_items = list(res), []
    for it in items:
        out_items.append(tuple(thru.pop(0) for _ in it))
    return out_items


def _gather_copy(m, refs, send_sem, recv_sem):
    (land_ref,) = refs
    r = land_ref.shape[0] // N_DEV
    mine = land_ref.at[pl.ds(pl.multiple_of(_peer_index(0) * r, 8), r), :]
    return pltpu.make_async_remote_copy(src_ref=mine, dst_ref=mine, send_sem=send_sem, recv_sem=recv_sem,
                                        device_id=_peer(m), device_id_type=MESH)


def _gather_copy_near(m, refs, send_sem, recv_sem):
    return _gather_copy(m, refs, send_sem, recv_sem) if m == 1 or m % 2 == 0 else None


def _gather_copy_pass(m, refs, send_sem, recv_sem):
    if m == 1 or m % 2 == 0:
        return None
    (land_ref,) = refs
    r = land_ref.shape[0] // N_DEV
    block = land_ref.at[pl.ds(pl.multiple_of(_peer_index(m ^ 1) * r, 8), r), :]
    return pltpu.make_async_remote_copy(src_ref=block, dst_ref=block, send_sem=send_sem, recv_sem=recv_sem,
                                        device_id=_peer(1), device_id_type=MESH)


def _small_copy(m, refs, send_sem, recv_sem):
    part_ref, land_ref = refs
    return pltpu.make_async_remote_copy(src_ref=part_ref, dst_ref=land_ref.at[m - 1], send_sem=send_sem,
                                        recv_sem=recv_sem, device_id=_peer(m), device_id_type=MESH)


def _scatter_copy(m, refs, send_sem, recv_sem):
    buf_ref, land_ref = refs
    r = buf_ref.shape[0] // N_DEV
    src = buf_ref.at[pl.ds(pl.multiple_of(_peer_index(m) * r, 8), r), :]
    return pltpu.make_async_remote_copy(src_ref=src, dst_ref=land_ref.at[m - 1], send_sem=send_sem,
                                        recv_sem=recv_sem, device_id=_peer(m), device_id_type=MESH)


def _peer(m):
    x, y, c = _mesh_pos()
    bx, by, bc = (m >> 2) & 1, (m >> 1) & 1, m & 1
    return (x ^ bx if bx else x, y ^ by if by else y, c ^ bc if bc else c)


def _peer_index(m):
    x, y, c = _mesh_pos()
    return (4 * x + 2 * y + c) ^ m


SMALL_PLACE = {
    "ln1_gain": (ROW_LN1G, 0), "ln1_bias": (ROW_LN1B, 0), "ln2_gain": (ROW_LN2G, 0), "ln2_bias": (ROW_LN2B, 0),
    "ret_gn_gain": (ROW_GN, 0), "ret_decay_fwd": (ROW_MISC, MISC_DF), "ret_decay_bwd": (ROW_MISC, MISC_DB),
    "attn_sink": (ROW_MISC, MISC_SINK)}


def _small_adamw(me, part, landed, w, m, v):
    d = part.shape[1]
    names = list(SMALL_PLACE)
    k = len(names)

    def body(*refs):
        me_ref, part_ref, land_ref = refs[:3]
        refs = refs[2:]
        w_refs, m_refs, v_refs = refs[1:1 + k], refs[1 + k:1 + 2 * k], refs[1 + 2 * k:1 + 3 * k]
        outs = refs[1 + 3 * k:1 + 7 * k + 1]
        tot_ref = refs[-1]
        loss_ref, g_refs, dl_refs = outs[0], outs[1:1 + k], outs[1 + k:1 + 2 * k]
        nm_refs, nv_refs = outs[1 + 2 * k:1 + 3 * k], outs[1 + 3 * k:1 + 4 * k]
        tot = jnp.zeros(part_ref.shape, F32)
        for dev in range(N_DEV):
            j = dev ^ me_ref[0]
            tot = tot + jnp.where(j == 0, part_ref[...], land_ref[jnp.maximum(j, 1) - 1])
        tot_ref[...] = tot
        loss_ref[...] = (0.5 / d) * jnp.sum(tot_ref[ROW_LOSS:ROW_LOSS + 1, :], axis=1, keepdims=True)
        for i, name in enumerate(names):
            row, lo = SMALL_PLACE[name]
            wv = w_refs[i][...]
            g = tot_ref[row:row + 1, lo:lo + wv.shape[1]]
            if name.startswith("ret_decay"):
                p2 = jnp.exp2(wv)
                g = g * (-p2 * jnp.log(2.0) / (1.0 - p2))
            g_refs[i][...] = g
            _adamw_store(g, wv, m_refs[i][...], v_refs[i][...], dl_refs[i], nm_refs[i], nv_refs[i])

    shapes = [SDS(w[n].shape, F32) for n in names]
    vm = pl.BlockSpec(memory_space=pltpu.VMEM)
    res = pl.pallas_call(
        body, name="small_adamw", out_shape=[SDS((1, 1), F32)] + shapes * 4,
        in_specs=[_smem_spec()] + [vm] * (2 + 3 * k), out_specs=[vm] * (1 + 4 * k),
        scratch_shapes=[pltpu.VMEM(part.shape, F32)],
    )(me, part, landed, *[w[n] for n in names], *[m[n] for n in names], *[v[n] for n in names])
    groups = [dict(zip(names, res[1 + j * k:1 + (j + 1) * k])) for j in range(4)]
    return (res[0], *groups)


def _adamw_store(g, w, m, v, dl_ref, nm_ref, nv_ref):
    m = ADAM_B1 * m + (1.0 - ADAM_B1) * g
    v = ADAM_B2 * v + (1.0 - ADAM_B2) * (g * g)
    m_hat = m / (1.0 - ADAM_B1 ** ADAM_STEP)
    v_hat = v / (1.0 - ADAM_B2 ** ADAM_STEP)
    dl_ref[...] = -ADAM_LR * (m_hat / (jnp.sqrt(v_hat) + ADAM_EPS) + ADAM_WD * w)
    nm_ref[...] = m
    nv_ref[...] = v


def _reduce_adamw(name, own, recv, w, m, v, transposed):
    rows, n = own.shape
    steps = 1 if transposed or rows % 32 else 4
    rb = rows // steps

    def body(own_ref, recv_ref, w_ref, m_ref, v_ref, g_ref, dl_ref, nm_ref, nv_ref):
        g = own_ref[...]
        for k in range(N_DEV - 1):
            g = g + recv_ref[k].astype(F32)
        if transposed:
            g = g.T
        g_ref[...] = g
        _adamw_store(g, w_ref[...], m_ref[...], v_ref[...], dl_ref, nm_ref, nv_ref)

    blk = pl.BlockSpec(w.shape if transposed else (rb, n), lambda i: (i, 0))
    out = SDS(w.shape, F32)
    return pl.pallas_call(
        body, name="adamw_" + name, grid=(steps,),
        in_specs=[pl.BlockSpec((rb, n), lambda i: (i, 0)), pl.BlockSpec((N_DEV - 1, rb, n), lambda i: (0, i, 0)),
                  blk, blk, blk],
        out_specs=[blk] * 4, out_shape=[out] * 4, compiler_params=_params(("parallel",)),
    )(own, recv, w, m, v)


def _row_spec(tm, width):
    return pl.BlockSpec((tm, width), lambda i: (i, 0))


def _full_spec(shape):
    return pl.BlockSpec(shape, lambda i: (0,) * len(shape))


_acc_spec = _full_spec


def _sub_rows(tm):
    step = min(SUB_ROWS, tm)
    return [(lo, lo + step) for lo in range(0, tm, step)]


def _in_proj(x2, wt_in):
    t, d = x2.shape
    u_w = wt_in.shape[0]
    tm = _row_tile(t, MATMUL_ROWS)

    def body(x_ref, w_ref, u_ref, xb_ref):
        xb = x_ref[...].astype(BF16)
        xb_ref[...] = xb
        u_ref[...] = _dot_nt(xb, w_ref[...]).astype(BF16)

    return pl.pallas_call(
        body, name="in_proj", grid=(t // tm,),
        in_specs=[_row_spec(tm, d), _full_spec(wt_in.shape)],
        out_specs=[_row_spec(tm, u_w), _row_spec(tm, d)],
        out_shape=[SDS((t, u_w), BF16), SDS((t, d), BF16)],
        compiler_params=_params(("parallel",)),
    )(x2, wt_in)


def _col_halves(f):
    n = f // LANES
    k = (n + 1) // 2 * LANES
    return [(0, k), (k, f)] if k < f else [(0, f)]


def _mix_ln1_ffn_up(r, a, x2, w_out, wt_gate, wt_up, g1, b1):
    t, d = x2.shape
    f = wt_gate.shape[0]
    tm = _row_tile(t, EPILOGUE_ROWS)

    def body(r_ref, a_ref, x_ref, wo_ref, wg_ref, wu_ref, g_ref, b_ref, z_ref, hb_ref, dg_ref, du_ref, act_ref):
        mix = _dot_nn(r_ref[...], wo_ref[0:RET_W, :]) + _dot_nn(a_ref[...], wo_ref[RET_W:RET_W + ATT_W, :])
        z = ALPHA * x_ref[...] + mix
        xhat, _ = _layer_norm_stats(z)
        z_ref[...] = z
        h = (xhat * g_ref[...] + b_ref[...]).astype(BF16)
        hb_ref[...] = h
        for lo, hi in _col_halves(f):
            g = _dot_nt(h, wg_ref[lo:hi, :])
            u = _dot_nt(h, wu_ref[lo:hi, :])
            sg = _sigmoid(g)
            silu = g * sg
            dg_ref[:, lo:hi] = (u * (sg * (1.0 + g * (1.0 - sg)))).astype(BF16)
            du_ref[:, lo:hi] = silu.astype(BF16)
            act_ref[:, lo:hi] = (silu * u).astype(BF16)

    wide, narrow = _row_spec(tm, f), _row_spec(tm, d)
    return pl.pallas_call(
        body, name="mix_ln1_ffn_up", grid=(t // tm,),
        in_specs=[_row_spec(tm, RET_W), _row_spec(tm, ATT_W), narrow, _resident_spec(w_out.shape),
                  _resident_spec(wt_gate.shape), _resident_spec(wt_up.shape), _full_spec(g1.shape),
                  _full_spec(b1.shape)],
        out_specs=[narrow, narrow, wide, wide, wide],
        out_shape=[SDS((t, d), F32), SDS((t, d), BF16)] + [SDS((t, f), BF16)] * 3,
        compiler_params=_params(("parallel",)),
    )(r, a, x2, w_out, wt_gate, wt_up, g1, b1)


def _ffn_down_ln2_loss(act, dact_dg, dact_du, h1b, p2, z1, target, w_down, w_pg, wt_pe, g1, b1, g2, b2):
    t, d = z1.shape
    f = act.shape[1]
    pdim = p2.shape[1]
    tm = _row_tile(t, EPILOGUE_ROWS)

    def body(act_ref, fg_ref, fu_ref, hb_ref, p_ref, z1_ref, tgt_ref, wd_ref, wpg_ref, wpe_ref, g1_ref, b1_ref,
             g2_ref, b2_ref, dz_ref, dzb_ref, ds_ref, dple_ref, dg_ref, du_ref, acc_ref):
        @pl.when(pl.program_id(0) == 0)
        def _():
            acc_ref[...] = jnp.zeros_like(acc_ref)

        for lo, hi in _sub_rows(tm):
            xhat1, _ = _layer_norm_stats(z1_ref[lo:hi, :])
            h1 = xhat1 * g1_ref[...] + b1_ref[...]
            ffn = _dot_nn(act_ref[lo:hi, :], wd_ref[...])
            pg = _sigmoid(_dot_nn(hb_ref[lo:hi, :], wpg_ref[...]))
            ple = _dot_nt(p_ref[lo:hi, :].astype(BF16), wpe_ref[...])
            z2 = ALPHA * h1 + ffn + pg * ple
            xhat2, rstd2 = _layer_norm_stats(z2)
            err = xhat2 * g2_ref[...] + b2_ref[...] - tgt_ref[lo:hi, :]
            dy = err * (1.0 / d)
            dz = _layer_norm_bwd(dy * g2_ref[...], xhat2, rstd2)
            dzb = dz.astype(BF16)
            dz_ref[lo:hi, :] = dz
            dzb_ref[lo:hi, :] = dzb
            ds_ref[lo:hi, :] = (dz * ple * pg * (1.0 - pg)).astype(BF16)
            dple_ref[lo:hi, :] = (dz * pg).astype(BF16)
            acc_ref[0:1, :] += jnp.sum(err * err, axis=0, keepdims=True)
            acc_ref[1:2, :] += jnp.sum(dy * xhat2, axis=0, keepdims=True)
            acc_ref[2:3, :] += jnp.sum(dy, axis=0, keepdims=True)
            for c0, c1 in _col_halves(f):
                da = _dot_nt(dzb, wd_ref[c0:c1, :])
                dg_ref[lo:hi, c0:c1] = (da * fg_ref[lo:hi, c0:c1].astype(F32)).astype(BF16)
                du_ref[lo:hi, c0:c1] = (da * fu_ref[lo:hi, c0:c1].astype(F32)).astype(BF16)

    vec = _full_spec(g1.shape)
    wide, narrow = _row_spec(tm, f), _row_spec(tm, d)
    return pl.pallas_call(
        body, name="ffn_down_ln2_loss", grid=(t // tm,),
        in_specs=[wide, wide, wide, narrow, _row_spec(tm, pdim), narrow, narrow,
                  _full_spec(w_down.shape), _full_spec(w_pg.shape), _full_spec(wt_pe.shape), vec, vec, vec, vec],
        out_specs=[narrow] * 4 + [wide, wide, _acc_spec((8, d))],
        out_shape=[SDS((t, d), F32), SDS((t, d), BF16), SDS((t, d), BF16), SDS((t, d), BF16),
                   SDS((t, f), BF16), SDS((t, f), BF16), SDS((8, d), F32)],
        compiler_params=_params(("arbitrary",)),
    )(act, dact_dg, dact_du, h1b, p2, z1, target, w_down, w_pg, wt_pe, g1, b1, g2, b2)


def _after(after, body):
    k = len(after)
    return (lambda *refs: body(*refs[k:])), [ANY_SPEC] * k


def _resident_spec(shape):
    return pl.BlockSpec(shape, lambda i: (0,) * len(shape), pipeline_mode=pl.Buffered(1))


def _dh1_ln1_bwd(dz2, dg, dup, dsb, z1, wt_gate, wt_up, w_pg, w_out, g1, after=()):
    t, d = dz2.shape
    f = dg.shape[1]
    tm = _row_tile(t, EPILOGUE_ROWS)

    def body(dz_ref, dg_ref, du_ref, ds_ref, z1_ref, wg_ref, wu_ref, wpg_ref, wo_ref, g1_ref,
             dz1_ref, dz1b_ref, dr_ref, da_ref, acc_ref):
        @pl.when(pl.program_id(0) == 0)
        def _():
            acc_ref[...] = jnp.zeros_like(acc_ref)

        for lo, hi in _sub_rows(tm):
            dh = (ALPHA * dz_ref[lo:hi, :] + _dot_nn(dg_ref[lo:hi, :], wg_ref[...])
                  + _dot_nn(du_ref[lo:hi, :], wu_ref[...]) + _dot_nt(ds_ref[lo:hi, :], wpg_ref[...]))
            xhat, rstd = _layer_norm_stats(z1_ref[lo:hi, :])
            dz1 = _layer_norm_bwd(dh * g1_ref[...], xhat, rstd)
            dz1b = dz1.astype(BF16)
            dz1_ref[lo:hi, :] = dz1
            dz1b_ref[lo:hi, :] = dz1b
            acc_ref[0:1, :] += jnp.sum(dh * xhat, axis=0, keepdims=True)
            acc_ref[1:2, :] += jnp.sum(dh, axis=0, keepdims=True)
            dr_ref[lo:hi, :] = _dot_nt(dz1b, wo_ref[0:RET_W, :]).astype(BF16)
            da_ref[lo:hi, :] = _dot_nt(dz1b, wo_ref[RET_W:RET_W + ATT_W, :]).astype(BF16)

    body, lead = _after(after, body)
    return pl.pallas_call(
        body, name="dh1_ln1_bwd", grid=(t // tm,),
        in_specs=lead + [_row_spec(tm, d), _row_spec(tm, f), _row_spec(tm, f), _row_spec(tm, d), _row_spec(tm, d),
                         _resident_spec(wt_gate.shape), _resident_spec(wt_up.shape), _resident_spec(w_pg.shape),
                         _resident_spec(w_out.shape), _full_spec(g1.shape)],
        out_specs=[_row_spec(tm, d), _row_spec(tm, d), _row_spec(tm, RET_W), _row_spec(tm, ATT_W), _acc_spec((8, d))],
        out_shape=[SDS((t, d), F32), SDS((t, d), BF16), SDS((t, RET_W), BF16), SDS((t, ATT_W), BF16),
                   SDS((8, d), F32)],
        compiler_params=_params(("arbitrary",)),
    )(*after, dz2, dg, dup, dsb, z1, wt_gate, wt_up, w_pg, w_out, g1)


def _in_proj_bwd(dz1, parts, wt_in, after=()):
    t, d = dz1.shape
    tm = _row_tile(t, MATMUL_ROWS)
    widths = [p.shape[1] for p in parts]

    def body(*refs):
        dz_ref, part_refs, w_ref, dx_ref = refs[0], refs[1:1 + len(parts)], refs[-2], refs[-1]
        acc = ALPHA * dz_ref[...]
        lo = 0
        for p_ref, w in zip(part_refs, widths):
            acc = acc + _dot_nn(p_ref[...], w_ref[lo:lo + w, :])
            lo += w
        dx_ref[...] = acc

    body, lead = _after(after, body)
    return pl.pallas_call(
        body, name="in_proj_bwd", grid=(t // tm,),
        in_specs=lead + [_row_spec(tm, d)] + [_row_spec(tm, w) for w in widths] + [_full_spec(wt_in.shape)],
        out_specs=_row_spec(tm, d), out_shape=SDS((t, d), F32),
        compiler_params=_params(("parallel",)),
    )(*after, dz1, *parts, wt_in)


def _weight_grad(name, me, parts, rhs, after=()):
    t, n = rhs.shape
    widths = [p.shape[1] for p in parts]
    rows = sum(widths)
    own_rows = rows // N_DEV
    tk = _row_tile(t, MATMUL_ROWS)
    step = 256

    def body(*refs):
        me_ref, part_refs, rhs_ref = refs[0], refs[1:1 + len(parts)], refs[1 + len(parts)]
        full_ref, own_ref, acc = refs[-3], refs[-2], refs[-1]
        i = pl.program_id(0)

        @pl.when(i == 0)
        def _():
            acc[...] = jnp.zeros_like(acc)

        b = rhs_ref[...].astype(BF16)
        lo = 0
        for p_ref, w in zip(part_refs, widths):
            for c0 in range(0, w, step):
                c1 = min(c0 + step, w)
                acc[lo + c0:lo + c1, :] += _dot_tn(p_ref[:, c0:c1].astype(BF16), b)
            lo += w

        @pl.when(i == pl.num_programs(0) - 1)
        def _():
            full_ref[...] = acc[...].astype(BF16)
            own_ref[...] = acc[pl.ds(pl.multiple_of(me_ref[0] * own_rows, 8), own_rows), :]

    body, lead = _after(after, body)
    return pl.pallas_call(
        body, name=name, grid=(t // tk,),
        in_specs=lead + [_smem_spec()] + [_row_spec(tk, w) for w in widths] + [_row_spec(tk, n)],
        out_specs=[_full_spec((rows, n)), _full_spec((own_rows, n))],
        out_shape=[SDS((rows, n), BF16), SDS((own_rows, n), F32)],
        scratch_shapes=[pltpu.VMEM((rows, n), F32)],
        compiler_params=_params(("arbitrary",)),
    )(*after, me, *parts, rhs)


def _log_decay(decay_f, decay_b):
    def body(f_ref, b_ref, lf_ref, lb_ref):
        lf_ref[...] = jnp.log1p(-jnp.exp2(f_ref[...]))
        lb_ref[...] = jnp.log1p(-jnp.exp2(b_ref[...]))

    return pl.pallas_call(body, name="log_decay", out_shape=[SDS(decay_f.shape, F32)] * 2)(decay_f, decay_b)


def _chunk(ref, n):
    return ref[pl.ds(pl.multiple_of(n * CHUNK, CHUNK), CHUNK), :]


def _group_sum(is_a, v):
    sa = jnp.sum(jnp.where(is_a, v, 0.0), axis=1, keepdims=True)
    sb = jnp.sum(jnp.where(is_a, 0.0, v), axis=1, keepdims=True)
    return jnp.where(is_a, sa, sb)


def _seq_spec(s, col_block):
    return pl.BlockSpec((s, LANES), lambda b, h: (b, col_block + h))


def _smem_spec():
    return pl.BlockSpec(memory_space=pltpu.SMEM)


RET_UNROLL = 4


def _chunk_loop(n_chunks, body, init):
    u = RET_UNROLL if n_chunks % RET_UNROLL == 0 else 1

    def trip(i, carry):
        for j in range(u):
            carry = body(i * u + j, carry)
        return carry

    return lax.fori_loop(0, n_chunks // u, trip, init)


def _stacked_tables(lgf_ref, lgb_ref, pair):
    lane = lax.broadcasted_iota(jnp.int32, (1, LANES), 1)
    is_a = lane < HEAD_DIM
    lgf = jnp.where(is_a, lgf_ref[2 * pair], lgf_ref[2 * pair + 1])
    lgb = jnp.where(is_a, lgb_ref[2 * pair], lgb_ref[2 * pair + 1])
    row = lax.broadcasted_iota(jnp.int32, (CHUNK, 1), 0).astype(F32)
    kdec_f, qdec_f = jnp.exp(lgf * (CHUNK - 1.0 - row)), jnp.exp(lgf * (row + 1.0))
    kdec_b, qdec_b = jnp.exp(lgb * row), jnp.exp(lgb * (CHUNK - row))
    tab = dict(
        is_a=is_a, row=row, lam_f=jnp.exp(lgf * CHUNK), lam_b=jnp.exp(lgb * CHUNK),
        kdec=jnp.concatenate([kdec_f, kdec_b], axis=1), qdec=jnp.concatenate([qdec_f, qdec_b], axis=1),
        qexp=jnp.concatenate([jnp.broadcast_to(row + 1.0, (CHUNK, LANES)),
                              jnp.broadcast_to(CHUNK - row, (CHUNK, LANES))], axis=1),
        kexp=jnp.concatenate([jnp.broadcast_to(CHUNK - 1.0 - row, (CHUNK, LANES)),
                              jnp.broadcast_to(row, (CHUNK, LANES))], axis=1),
    )
    r = lax.broadcasted_iota(jnp.int32, (2 * LANES, LANES), 0)
    c = lax.broadcasted_iota(jnp.int32, (2 * LANES, LANES), 1)
    tab["diag2"] = ((r & (LANES - 1)) < HEAD_DIM) == (c < HEAD_DIM)
    i2 = lax.broadcasted_iota(jnp.int32, (2 * CHUNK, CHUNK), 0)
    j = lax.broadcasted_iota(jnp.int32, (2 * CHUNK, CHUNK), 1)
    head_b = i2 >= CHUNK
    diff = ((i2 & (CHUNK - 1)) - j).astype(F32)
    up, dn = jnp.maximum(diff, 0.0), jnp.maximum(-diff, 0.0)
    lgf2 = jnp.where(head_b, lgf_ref[2 * pair + 1], lgf_ref[2 * pair])
    lgb2 = jnp.where(head_b, lgb_ref[2 * pair + 1], lgb_ref[2 * pair])
    ef = jnp.where(diff >= 0, jnp.exp(lgf2 * up), 0.0)
    eb = jnp.where(diff <= 0, jnp.exp(lgb2 * dn), 0.0)
    tab["d2"] = ef + eb
    tab["df2"] = ef * up
    tab["db2"] = eb * dn
    return tab


def _stack_pair(is_a, x):
    zero = jnp.zeros_like(x)
    return jnp.concatenate([jnp.where(is_a, x, zero), jnp.where(is_a, zero, x)], axis=0)


def _unstack_pair(is_a, x2):
    return jnp.where(is_a, x2[0:CHUNK, :], x2[CHUNK:2 * CHUNK, :])


def _both_ways(x, dec):
    return (jnp.concatenate([x, x], axis=1) * dec).astype(BF16)


def _scan_states(n_chunks, st, up_rows, up_lam, down_rows, down_lam):
    zero = jnp.zeros((LANES, LANES), F32)

    def up(n, r):
        new = st[n, up_rows, :]
        st[n, up_rows, :] = r
        return r * up_lam + new

    def down(s, r):
        n = n_chunks - 1 - s
        new = st[n, down_rows, :]
        st[n, down_rows, :] = r
        return r * down_lam + new

    lax.fori_loop(0, n_chunks, up, zero)
    lax.fori_loop(0, n_chunks, down, zero)


FWD_ROWS, BWD_ROWS = pl.ds(0, LANES), pl.ds(LANES, LANES)


def _retention_fwd(u, lgf, lgb, gn_gain, b_loc):
    t = u.shape[0]
    s = t // b_loc
    n_chunks = s // CHUNK
    pairs = RET_HEADS // 2

    def body(lgf_ref, lgb_ref, q_ref, k_ref, v_ref, g_ref, gain_ref, r_ref, y_ref, st):
        tab = _stacked_tables(lgf_ref, lgb_ref, pl.program_id(1))
        is_a = tab["is_a"]

        def kv_body(n, _):
            k8 = _chunk(k_ref, n).astype(F32) * Q_SCALE
            st[n] = jnp.where(tab["diag2"], _dot_tn(_both_ways(k8, tab["kdec"]), _chunk(v_ref, n)), 0.0)
            return 0

        _chunk_loop(n_chunks, kv_body, 0)
        _scan_states(n_chunks, st, FWD_ROWS, tab["lam_f"], BWD_ROWS, tab["lam_b"])

        def out_body(n, _):
            q = _chunk(q_ref, n)
            k8 = (_chunk(k_ref, n).astype(F32) * Q_SCALE).astype(BF16)
            v = _chunk(v_ref, n)
            p2 = (_dot_nt(_stack_pair(is_a, q), k8) * tab["d2"]).astype(BF16)
            y = _unstack_pair(is_a, _dot_nn(p2, v))
            y = y + _dot_nn(_both_ways(q.astype(F32), tab["qdec"]), st[n].astype(BF16))
            rows = pl.ds(pl.multiple_of(n * CHUNK, CHUNK), CHUNK)
            y_ref[rows, :] = y
            mu = _group_sum(is_a, y) * (1.0 / HEAD_DIM)
            dlt = y - mu
            var = _group_sum(is_a, dlt * dlt) * (1.0 / HEAD_DIM)
            xhat = dlt * lax.rsqrt(var + GN_EPS)
            gate = _chunk(g_ref, n).astype(F32)
            r_ref[rows, :] = (xhat * gain_ref[...] * gate * _sigmoid(gate)).astype(BF16)
            return 0

        _chunk_loop(n_chunks, out_body, 0)

    lane_blk = lambda c0: _seq_spec(s, c0 // LANES)
    return pl.pallas_call(
        body, name="retention_fwd", grid=(b_loc, pairs),
        in_specs=[_smem_spec(), _smem_spec(), lane_blk(C_RQ), lane_blk(C_RK), lane_blk(C_RV), lane_blk(C_RG),
                  pl.BlockSpec((1, LANES), lambda b, h: (0, h))],
        out_specs=[_seq_spec(s, 0), _seq_spec(s, 0)],
        out_shape=[SDS((t, RET_W), BF16), SDS((t, RET_W), F32)],
        scratch_shapes=[pltpu.VMEM((n_chunks, 2 * LANES, LANES), F32)],
        compiler_params=_params(("parallel", "parallel")),
    )(lgf, lgb, u, u, u, u, gn_gain)


ST_GAIN, ST_XF, ST_XB, ST_IFA, ST_IFB, ST_IBA, ST_IBB, ST_LF, ST_LB = 0, 1, 2, 3, 4, 5, 6, 8, 9
ST_ROWS = 16


def _retention_bwd(u, y_pre, dr, lgf, lgb, gn_gain, b_loc, after=()):
    t = u.shape[0]
    s = t // b_loc
    n_chunks = s // CHUNK
    pairs = RET_HEADS // 2

    def body(lgf_ref, lgb_ref, q_ref, k_ref, v_ref, g_ref, y_ref, dr_ref, gain_ref,
             dq_ref, dk_ref, dv_ref, dg_ref, st_ref, st, gr, dy_s):
        tab = _stacked_tables(lgf_ref, lgb_ref, pl.program_id(1))
        is_a, row = tab["is_a"], tab["row"]
        gain = gain_ref[...]

        def norm_body(n, dgain):
            rows = pl.ds(pl.multiple_of(n * CHUNK, CHUNK), CHUNK)
            y = y_ref[rows, :]
            mu = _group_sum(is_a, y) * (1.0 / HEAD_DIM)
            dlt = y - mu
            var = _group_sum(is_a, dlt * dlt) * (1.0 / HEAD_DIM)
            rstd = lax.rsqrt(var + GN_EPS)
            xhat = dlt * rstd
            gate = g_ref[rows, :].astype(F32)
            sg = _sigmoid(gate)
            silu = gate * sg
            d_out = dr_ref[rows, :].astype(F32)
            dg_ref[rows, :] = (d_out * xhat * gain * (sg * (1.0 + gate * (1.0 - sg)))).astype(BF16)
            dxh = d_out * gain * silu
            m1 = _group_sum(is_a, dxh) * (1.0 / HEAD_DIM)
            m2 = _group_sum(is_a, dxh * xhat) * (1.0 / HEAD_DIM)
            dy = (rstd * (dxh - m1 - xhat * m2)).astype(BF16)
            dy_s[rows, :] = dy
            k8 = k_ref[rows, :].astype(F32) * Q_SCALE
            st[n] = jnp.where(tab["diag2"], _dot_tn(_both_ways(k8, tab["kdec"]), v_ref[rows, :]), 0.0)
            qf = q_ref[rows, :].astype(F32)
            gr[n] = jnp.where(tab["diag2"], _dot_tn(_both_ways(qf, tab["qdec"]), dy), 0.0)
            return dgain + jnp.sum(d_out * xhat * silu, axis=0, keepdims=True)

        dgain = _chunk_loop(n_chunks, norm_body, jnp.zeros((1, LANES), F32))
        _scan_states(n_chunks, st, FWD_ROWS, tab["lam_f"], BWD_ROWS, tab["lam_b"])
        _scan_states(n_chunks, gr, BWD_ROWS, tab["lam_b"], FWD_ROWS, tab["lam_f"])
        colsum = lambda x: jnp.sum(x, axis=0, keepdims=True)

        def grad_body(n, carry):
            xfb, ifa, ifb, iba, ibb, lf, lb = carry
            rows = pl.ds(pl.multiple_of(n * CHUNK, CHUNK), CHUNK)
            q = q_ref[rows, :]
            qf = q.astype(F32)
            k8f = k_ref[rows, :].astype(F32) * Q_SCALE
            k8 = k8f.astype(BF16)
            v = v_ref[rows, :]
            dy = dy_s[rows, :]
            q2, dy2 = _stack_pair(is_a, q), _stack_pair(is_a, dy)
            sc = _dot_nt(q2, k8)
            dp = _dot_nt(dy2, v)
            a2 = (sc * tab["d2"]).astype(BF16)
            ds2 = (dp * tab["d2"]).astype(BF16)
            dq = _unstack_pair(is_a, _dot_nn(ds2, k8))
            dk = _dot_tn(ds2, q2)
            dv = _dot_tn(a2, dy2)
            prod = sc * dp
            pf, pb = prod * tab["df2"], prod * tab["db2"]
            ifa, ifb = ifa + colsum(pf[0:CHUNK, :]), ifb + colsum(pf[CHUNK:2 * CHUNK, :])
            iba, ibb = iba + colsum(pb[0:CHUNK, :]), ibb + colsum(pb[CHUNK:2 * CHUNK, :])
            states, sgrads = st[n], gr[n]
            sb, gb = states.astype(BF16), sgrads.astype(BF16)
            dqc = _dot_nt(dy, sb) * tab["qdec"]
            dkc = _dot_nt(v, gb) * tab["kdec"]
            dv = dv + _dot_nn(_both_ways(k8f, tab["kdec"]), gb)
            dq_ref[rows, :] = (dq + dqc[:, 0:LANES] + dqc[:, LANES:2 * LANES]).astype(BF16)
            dk_ref[rows, :] = ((dk + dkc[:, 0:LANES] + dkc[:, LANES:2 * LANES]) * Q_SCALE).astype(BF16)
            dv_ref[rows, :] = dv.astype(BF16)
            q2w, k2w = jnp.concatenate([qf, qf], axis=1), jnp.concatenate([k8f, k8f], axis=1)
            xfb = xfb + colsum(tab["qexp"] * q2w * dqc + tab["kexp"] * k2w * dkc)
            prod_s = sgrads * states
            lf, lb = lf + colsum(prod_s[0:LANES, :]), lb + colsum(prod_s[LANES:2 * LANES, :])
            return xfb, ifa, ifb, iba, ibb, lf, lb

        z = jnp.zeros((1, LANES), F32)
        init = (jnp.zeros((1, 2 * LANES), F32), z, z, z, z, z, z)
        xfb, ifa, ifb, iba, ibb, lf, lb = _chunk_loop(n_chunks, grad_body, init)
        st_ref[...] = jnp.zeros_like(st_ref)
        st_ref[ST_GAIN:ST_GAIN + 1, :] = dgain
        st_ref[ST_XF:ST_XF + 1, :] = xfb[:, 0:LANES]
        st_ref[ST_XB:ST_XB + 1, :] = xfb[:, LANES:2 * LANES]
        st_ref[ST_IFA:ST_IFA + 1, :] = ifa
        st_ref[ST_IFB:ST_IFB + 1, :] = ifb
        st_ref[ST_IBA:ST_IBA + 1, :] = iba
        st_ref[ST_IBB:ST_IBB + 1, :] = ibb
        st_ref[ST_LF:ST_LF + 1, :] = lf * (CHUNK * tab["lam_f"])
        st_ref[ST_LB:ST_LB + 1, :] = lb * (CHUNK * tab["lam_b"])

    lane_blk = lambda c0: _seq_spec(s, c0 // LANES)
    seq0 = _seq_spec(s, 0)
    state = pltpu.VMEM((n_chunks, 2 * LANES, LANES), F32)
    body, lead = _after(after, body)
    return pl.pallas_call(
        body, name="retention_bwd", grid=(b_loc, pairs),
        in_specs=lead + [_smem_spec(), _smem_spec(), lane_blk(C_RQ), lane_blk(C_RK), lane_blk(C_RV), lane_blk(C_RG),
                         seq0, seq0, pl.BlockSpec((1, LANES), lambda b, h: (0, h))],
        out_specs=[seq0] * 4 + [pl.BlockSpec((ST_ROWS, LANES), lambda b, h: (b, h))],
        out_shape=[SDS((t, RET_W), BF16)] * 4 + [SDS((b_loc * ST_ROWS, RET_W), F32)],
        scratch_shapes=[state, state, pltpu.VMEM((s, LANES), BF16)],
        compiler_params=_params(("parallel", "parallel")),
    )(*after, lgf, lgb, u, u, u, u, y_pre, dr, gn_gain)


GW = GROUP * HEAD_DIM
KEYS = 3 * BLOCK


def _attn_tables(g, bias_ref):
    r = lax.broadcasted_iota(jnp.int32, (GROUP * BLOCK, KEYS), 0)
    kj = lax.broadcasted_iota(jnp.int32, (GROUP * BLOCK, KEYS), 1)
    qi = r & (BLOCK - 1)
    hh = lax.shift_right_logical(r, 7)
    dist = jnp.abs(kj - BLOCK - qi)
    slope = jnp.exp2(-(GROUP * g + hh + 1).astype(F32) * (8.0 / ATTN_HEADS))
    bias_ref[...] = jnp.where(dist <= BLOCK, -slope * dist.astype(F32), NEG_INF)


def _tile_keys(x_ref, g, scale, pad_ref, s):
    r = lax.broadcasted_iota(jnp.int32, (LANES, GW), 0)
    c = lax.broadcasted_iota(jnp.int32, (LANES, GW), 1)
    place = jnp.where(r == g * HEAD_DIM + (c & (HEAD_DIM - 1)), 1.0, 0.0).astype(BF16)
    pad_ref[0:BLOCK, :] = jnp.zeros((BLOCK, GW), BF16)
    pad_ref[BLOCK + s:2 * BLOCK + s, :] = jnp.zeros((BLOCK, GW), BF16)
    pad_ref[BLOCK:BLOCK + s, :] = (_dot_nn(x_ref[...], place) * scale).astype(BF16)


def _stack_heads(x):
    lane_h = lax.shift_right_logical(lax.broadcasted_iota(jnp.int32, (1, GW), 1), 6)
    zero = jnp.zeros_like(x)
    return jnp.concatenate([jnp.where(lane_h == h, x, zero) for h in range(GROUP)], axis=0)


def _unstack_heads(x4):
    lane_h = lax.shift_right_logical(lax.broadcasted_iota(jnp.int32, (1, GW), 1), 6)
    out = jnp.zeros((BLOCK, GW), F32)
    for h in range(GROUP):
        out = out + jnp.where(lane_h == h, x4[h * BLOCK:(h + 1) * BLOCK, :], 0.0)
    return out


def _sink_column(sink_ref, g):
    rh = lax.shift_right_logical(lax.broadcasted_iota(jnp.int32, (GROUP * BLOCK, 1), 0), 7)
    col = jnp.zeros((GROUP * BLOCK, 1), F32)
    for h in range(GROUP):
        col = jnp.where(rh == h, sink_ref[GROUP * g + h], col)
    return col


def _attn_probs(qm, k3, bias_ref, sink_col, n, s):
    logits = _dot_nt(qm, k3) + bias_ref[...]
    kpos = n * BLOCK - BLOCK + lax.broadcasted_iota(jnp.int32, (1, KEYS), 1)
    logits = jnp.where((kpos >= 0) & (kpos < s), logits, NEG_INF)
    m = jnp.maximum(jnp.max(logits, axis=1, keepdims=True), sink_col)
    e = jnp.exp(logits - m)
    e_sink = jnp.exp(sink_col - m)
    inv = 1.0 / (jnp.sum(e, axis=1, keepdims=True) + e_sink)
    return e * inv, e_sink * inv


ATT_SUB = 2


def _attn_specs(s, n_steps):
    rows = ATT_SUB * BLOCK
    q_spec = pl.BlockSpec((rows, GW), lambda b, g, n: (b * n_steps + n, C_AQ // GW + g))
    k_spec = pl.BlockSpec((s, LANES), lambda b, g, n: (b, C_AK // LANES))
    v_spec = pl.BlockSpec((s, LANES), lambda b, g, n: (b, C_AV // LANES))
    o_spec = pl.BlockSpec((rows, GW), lambda b, g, n: (b * n_steps + n, g))
    return q_spec, k_spec, v_spec, o_spec


def _attention_fwd(u, sink, b_loc):
    t = u.shape[0]
    s = t // b_loc
    n_steps = s // (ATT_SUB * BLOCK)

    def body(sink_ref, q_ref, k_ref, v_ref, o_ref, kpad, vpad, bias):
        g, step = pl.program_id(1), pl.program_id(2)

        @pl.when(step == 0)
        def _():
            _attn_tables(g, bias)
            _tile_keys(k_ref, g, Q_SCALE, kpad, s)
            _tile_keys(v_ref, g, 1.0, vpad, s)

        sink_col = _sink_column(sink_ref, g)
        for j in range(ATT_SUB):
            n = step * ATT_SUB + j
            rows = pl.ds(j * BLOCK, BLOCK)
            keys = pl.ds(pl.multiple_of(n * BLOCK, BLOCK), KEYS)
            p, _ = _attn_probs(_stack_heads(q_ref[rows, :]), kpad[keys, :], bias, sink_col, n, s)
            o_ref[rows, :] = _unstack_heads(_dot_nn(p.astype(BF16), vpad[keys, :])).astype(BF16)

    q_spec, k_spec, v_spec, o_spec = _attn_specs(s, n_steps)
    pad = pltpu.VMEM((s + 2 * BLOCK, GW), BF16)
    return pl.pallas_call(
        body, name="attention_fwd", grid=(b_loc, KV_HEADS, n_steps),
        in_specs=[_smem_spec(), q_spec, k_spec, v_spec], out_specs=o_spec,
        out_shape=SDS((t, ATT_W), BF16),
        scratch_shapes=[pad, pad, pltpu.VMEM((GROUP * BLOCK, KEYS), F32)],
        compiler_params=_params(("parallel", "arbitrary", "arbitrary")),
    )(sink, u, u, u)


PAIRS_PER_KV = (RET_HEADS // 2) // KV_HEADS


def _mixers_fwd(u, lgf, lgb, gn_gain, sink, b_loc):
    t = u.shape[0]
    s = t // b_loc
    n_chunks = s // CHUNK
    pairs = RET_HEADS // 2
    trips = n_chunks // RET_UNROLL
    blocks_half = (s // BLOCK) // PAIRS_PER_KV
    per_trip = blocks_half // trips
    assert n_chunks % RET_UNROLL == 0 and blocks_half % trips == 0 and PAIRS_PER_KV == 2

    def body(lgf_ref, lgb_ref, sink_ref, q_ref, k_ref, v_ref, g_ref, gain_ref, aq_ref, ak_ref, av_ref,
             r_ref, y_ref, a_ref, st, kpad, vpad, bias):
        pair = pl.program_id(1)
        g, half = lax.shift_right_logical(pair, 1), pair & 1
        tab = _stacked_tables(lgf_ref, lgb_ref, pair)
        is_a = tab["is_a"]

        @pl.when(half == 0)
        def _():
            _attn_tables(g, bias)
            _tile_keys(ak_ref, g, Q_SCALE, kpad, s)
            _tile_keys(av_ref, g, 1.0, vpad, s)

        def kv_body(n, _):
            k8 = _chunk(k_ref, n).astype(F32) * Q_SCALE
            st[n] = jnp.where(tab["diag2"], _dot_tn(_both_ways(k8, tab["kdec"]), _chunk(v_ref, n)), 0.0)
            return 0

        _chunk_loop(n_chunks, kv_body, 0)
        _scan_states(n_chunks, st, FWD_ROWS, tab["lam_f"], BWD_ROWS, tab["lam_b"])
        sink_col = _sink_column(sink_ref, g)

        def retention_chunk(n):
            q = _chunk(q_ref, n)
            k8 = (_chunk(k_ref, n).astype(F32) * Q_SCALE).astype(BF16)
            v = _chunk(v_ref, n)
            p2 = (_dot_nt(_stack_pair(is_a, q), k8) * tab["d2"]).astype(BF16)
            y = _unstack_pair(is_a, _dot_nn(p2, v))
            y = y + _dot_nn(_both_ways(q.astype(F32), tab["qdec"]), st[n].astype(BF16))
            rows = pl.ds(pl.multiple_of(n * CHUNK, CHUNK), CHUNK)
            y_ref[rows, :] = y
            mu = _group_sum(is_a, y) * (1.0 / HEAD_DIM)
            dlt = y - mu
            var = _group_sum(is_a, dlt * dlt) * (1.0 / HEAD_DIM)
            xhat = dlt * lax.rsqrt(var + GN_EPS)
            gate = _chunk(g_ref, n).astype(F32)
            r_ref[rows, :] = (xhat * gain_ref[...] * gate * _sigmoid(gate)).astype(BF16)

        def attention_block(blk):
            n = half * blocks_half + blk
            rows = pl.ds(pl.multiple_of(blk * BLOCK, BLOCK), BLOCK)
            keys = pl.ds(pl.multiple_of(n * BLOCK, BLOCK), KEYS)
            p, _ = _attn_probs(_stack_heads(aq_ref[rows, :]), kpad[keys, :], bias, sink_col, n, s)
            a_ref[rows, :] = _unstack_heads(_dot_nn(p.astype(BF16), vpad[keys, :])).astype(BF16)

        def trip(i, _):
            for j in range(max(RET_UNROLL, per_trip)):
                if j < RET_UNROLL:
                    retention_chunk(i * RET_UNROLL + j)
                if j < per_trip:
                    attention_block(i * per_trip + j)
            return 0

        lax.fori_loop(0, trips, trip, 0)

    lane_blk = lambda c0: _seq_spec(s, c0 // LANES)
    half_rows = blocks_half * BLOCK
    aq_spec = pl.BlockSpec((half_rows, GW), lambda b, h: (b * PAIRS_PER_KV + (h & 1), C_AQ // GW + h // 2))
    a_spec = pl.BlockSpec((half_rows, GW), lambda b, h: (b * PAIRS_PER_KV + (h & 1), h // 2))
    kv_spec = lambda c0: pl.BlockSpec((s, LANES), lambda b, h: (b, c0 // LANES))
    pad = pltpu.VMEM((s + 2 * BLOCK, GW), BF16)
    return pl.pallas_call(
        body, name="mixers_fwd", grid=(b_loc, pairs),
        in_specs=[_smem_spec(), _smem_spec(), _smem_spec(), lane_blk(C_RQ), lane_blk(C_RK), lane_blk(C_RV),
                  lane_blk(C_RG), pl.BlockSpec((1, LANES), lambda b, h: (0, h)), aq_spec, kv_spec(C_AK), kv_spec(C_AV)],
        out_specs=[_seq_spec(s, 0), _seq_spec(s, 0), a_spec],
        out_shape=[SDS((t, RET_W), BF16), SDS((t, RET_W), F32), SDS((t, ATT_W), BF16)],
        scratch_shapes=[pltpu.VMEM((n_chunks, 2 * LANES, LANES), F32), pad, pad,
                        pltpu.VMEM((GROUP * BLOCK, KEYS), F32)],
        compiler_params=_params(("arbitrary", "arbitrary")),
    )(lgf, lgb, sink, u, u, u, u, gn_gain, u, u, u)


def _fold_groups(x, g):
    x = x + pltpu.roll(x, 2 * HEAD_DIM, 1)
    x = x + pltpu.roll(x, HEAD_DIM, 1)
    lane_g = lax.shift_right_logical(lax.broadcasted_iota(jnp.int32, (1, LANES), 1), 6)
    return jnp.where(lane_g == g, x[:, 0:LANES], 0.0)


def _mixers_bwd(u, y_pre, dr, da, lgf, lgb, gn_gain, sink, b_loc, after=()):
    t = u.shape[0]
    s = t // b_loc
    n_chunks = s // CHUNK
    pairs = RET_HEADS // 2
    trips = n_chunks // RET_UNROLL
    blocks_half = (s // BLOCK) // PAIRS_PER_KV
    per_trip = blocks_half // trips
    assert n_chunks % RET_UNROLL == 0 and blocks_half % trips == 0 and PAIRS_PER_KV == 2

    def body(lgf_ref, lgb_ref, sink_ref, q_ref, k_ref, v_ref, g_ref, y_ref, dr_ref, gain_ref,
             aq_ref, ak_ref, av_ref, do_ref,
             dq_ref, dk_ref, dv_ref, dg_ref, st_ref, daq_ref, dak_ref, dav_ref, dsink_ref,
             st, gr, dy_s, kpad, vpad, bias, dk_acc, dv_acc):
        pair = pl.program_id(1)
        g, half = lax.shift_right_logical(pair, 1), pair & 1
        tab = _stacked_tables(lgf_ref, lgb_ref, pair)
        is_a = tab["is_a"]
        gain = gain_ref[...]

        @pl.when(half == 0)
        def _():
            _attn_tables(g, bias)
            _tile_keys(ak_ref, g, Q_SCALE, kpad, s)
            _tile_keys(av_ref, g, 1.0, vpad, s)
            dsink_ref[...] = jnp.zeros_like(dsink_ref)

        @pl.when(pair == 0)
        def _():
            dk_acc[...] = jnp.zeros_like(dk_acc)
            dv_acc[...] = jnp.zeros_like(dv_acc)

        def norm_body(n, dgain):
            rows = pl.ds(pl.multiple_of(n * CHUNK, CHUNK), CHUNK)
            y = y_ref[rows, :]
            mu = _group_sum(is_a, y) * (1.0 / HEAD_DIM)
            dlt = y - mu
            var = _group_sum(is_a, dlt * dlt) * (1.0 / HEAD_DIM)
            rstd = lax.rsqrt(var + GN_EPS)
            xhat = dlt * rstd
            gate = g_ref[rows, :].astype(F32)
            sg = _sigmoid(gate)
            silu = gate * sg
            d_out = dr_ref[rows, :].astype(F32)
            dg_ref[rows, :] = (d_out * xhat * gain * (sg * (1.0 + gate * (1.0 - sg)))).astype(BF16)
            dxh = d_out * gain * silu
            m1 = _group_sum(is_a, dxh) * (1.0 / HEAD_DIM)
            m2 = _group_sum(is_a, dxh * xhat) * (1.0 / HEAD_DIM)
            dy = (rstd * (dxh - m1 - xhat * m2)).astype(BF16)
            dy_s[rows, :] = dy
            k8 = k_ref[rows, :].astype(F32) * Q_SCALE
            st[n] = jnp.where(tab["diag2"], _dot_tn(_both_ways(k8, tab["kdec"]), v_ref[rows, :]), 0.0)
            qf = q_ref[rows, :].astype(F32)
            gr[n] = jnp.where(tab["diag2"], _dot_tn(_both_ways(qf, tab["qdec"]), dy), 0.0)
            return dgain + jnp.sum(d_out * xhat * silu, axis=0, keepdims=True)

        dgain = _chunk_loop(n_chunks, norm_body, jnp.zeros((1, LANES), F32))
        _scan_states(n_chunks, st, FWD_ROWS, tab["lam_f"], BWD_ROWS, tab["lam_b"])
        _scan_states(n_chunks, gr, BWD_ROWS, tab["lam_b"], FWD_ROWS, tab["lam_f"])
        colsum = lambda x: jnp.sum(x, axis=0, keepdims=True)

        def grad_body(n, carry):
            xfb, ifa, ifb, iba, ibb, lf, lb = carry
            rows = pl.ds(pl.multiple_of(n * CHUNK, CHUNK), CHUNK)
            q = q_ref[rows, :]
            qf = q.astype(F32)
            k8f = k_ref[rows, :].astype(F32) * Q_SCALE
            k8 = k8f.astype(BF16)
            v = v_ref[rows, :]
            dy = dy_s[rows, :]
            q2, dy2 = _stack_pair(is_a, q), _stack_pair(is_a, dy)
            sc = _dot_nt(q2, k8)
            dp = _dot_nt(dy2, v)
            a2 = (sc * tab["d2"]).astype(BF16)
            ds2 = (dp * tab["d2"]).astype(BF16)
            dq = _unstack_pair(is_a, _dot_nn(ds2, k8))
            dk = _dot_tn(ds2, q2)
            dv = _dot_tn(a2, dy2)
            prod = sc * dp
            pf, pb = prod * tab["df2"], prod * tab["db2"]
            ifa, ifb = ifa + colsum(pf[0:CHUNK, :]), ifb + colsum(pf[CHUNK:2 * CHUNK, :])
            iba, ibb = iba + colsum(pb[0:CHUNK, :]), ibb + colsum(pb[CHUNK:2 * CHUNK, :])
            states, sgrads = st[n], gr[n]
            sb, gb = states.astype(BF16), sgrads.astype(BF16)
            dqc = _dot_nt(dy, sb) * tab["qdec"]
            dkc = _dot_nt(v, gb) * tab["kdec"]
            dv = dv + _dot_nn(_both_ways(k8f, tab["kdec"]), gb)
            dq_ref[rows, :] = (dq + dqc[:, 0:LANES] + dqc[:, LANES:2 * LANES]).astype(BF16)
            dk_ref[rows, :] = ((dk + dkc[:, 0:LANES] + dkc[:, LANES:2 * LANES]) * Q_SCALE).astype(BF16)
            dv_ref[rows, :] = dv.astype(BF16)
            q2w, k2w = jnp.concatenate([qf, qf], axis=1), jnp.concatenate([k8f, k8f], axis=1)
            xfb = xfb + colsum(tab["qexp"] * q2w * dqc + tab["kexp"] * k2w * dkc)
            prod_s = sgrads * states
            lf, lb = lf + colsum(prod_s[0:LANES, :]), lb + colsum(prod_s[LANES:2 * LANES, :])
            return xfb, ifa, ifb, iba, ibb, lf, lb

        sink_col = _sink_column(sink_ref, g)
        head_row = lax.broadcasted_iota(jnp.int32, dsink_ref.shape, 0)

        def attention_block(blk):
            n = half * blocks_half + blk
            rows = pl.ds(pl.multiple_of(blk * BLOCK, BLOCK), BLOCK)
            keys = pl.ds(pl.multiple_of(n * BLOCK, BLOCK), KEYS)
            qm = _stack_heads(aq_ref[rows, :])
            k3, v3 = kpad[keys, :], vpad[keys, :]
            p, p_sink = _attn_probs(qm, k3, bias, sink_col, n, s)
            dom = _stack_heads(do_ref[rows, :])
            dp = _dot_nt(dom, v3)
            delta = jnp.sum(p * dp, axis=1, keepdims=True)
            ds_mat = (p * (dp - delta)).astype(BF16)
            daq_ref[rows, :] = _unstack_heads(_dot_nn(ds_mat, k3)).astype(BF16)
            dk_acc[keys, :] += _fold_groups(_dot_tn(ds_mat, qm), g) * Q_SCALE
            dv_acc[keys, :] += _fold_groups(_dot_tn(p.astype(BF16), dom), g)
            w = p_sink * delta
            upd = jnp.zeros(dsink_ref.shape, F32)
            for h in range(GROUP):
                upd = upd + jnp.where(head_row == h, -jnp.sum(w[h * BLOCK:(h + 1) * BLOCK, :]), 0.0)
            dsink_ref[...] += upd

        def trip(i, carry):
            for j in range(max(RET_UNROLL, per_trip)):
                if j < RET_UNROLL:
                    carry = grad_body(i * RET_UNROLL + j, carry)
                if j < per_trip:
                    attention_block(i * per_trip + j)
            return carry

        z = jnp.zeros((1, LANES), F32)
        init = (jnp.zeros((1, 2 * LANES), F32), z, z, z, z, z, z)
        xfb, ifa, ifb, iba, ibb, lf, lb = lax.fori_loop(0, trips, trip, init)
        st_ref[...] = jnp.zeros_like(st_ref)
        st_ref[ST_GAIN:ST_GAIN + 1, :] = dgain
        st_ref[ST_XF:ST_XF + 1, :] = xfb[:, 0:LANES]
        st_ref[ST_XB:ST_XB + 1, :] = xfb[:, LANES:2 * LANES]
        st_ref[ST_IFA:ST_IFA + 1, :] = ifa
        st_ref[ST_IFB:ST_IFB + 1, :] = ifb
        st_ref[ST_IBA:ST_IBA + 1, :] = iba
        st_ref[ST_IBB:ST_IBB + 1, :] = ibb
        st_ref[ST_LF:ST_LF + 1, :] = lf * (CHUNK * tab["lam_f"])
        st_ref[ST_LB:ST_LB + 1, :] = lb * (CHUNK * tab["lam_b"])

        @pl.when(pair == pairs - 1)
        def _():
            dak_ref[...] = dk_acc[BLOCK:BLOCK + s, :].astype(BF16)
            dav_ref[...] = dv_acc[BLOCK:BLOCK + s, :].astype(BF16)

    lane_blk = lambda c0: _seq_spec(s, c0 // LANES)
    seq0 = _seq_spec(s, 0)
    half_rows = blocks_half * BLOCK
    aq_spec = pl.BlockSpec((half_rows, GW), lambda b, h: (b * PAIRS_PER_KV + (h & 1), C_AQ // GW + h // 2))
    a_spec = pl.BlockSpec((half_rows, GW), lambda b, h: (b * PAIRS_PER_KV + (h & 1), h // 2))
    kv_spec = lambda c0: pl.BlockSpec((s, LANES), lambda b, h: (b, c0 // LANES))
    kv_out = pl.BlockSpec((s, LANES), lambda b, h: (b, 0))
    state = pltpu.VMEM((n_chunks, 2 * LANES, LANES), F32)
    pad = pltpu.VMEM((s + 2 * BLOCK, GW), BF16)
    acc = pltpu.VMEM((s + 2 * BLOCK, LANES), F32)
    body, lead = _after(after, body)
    return pl.pallas_call(
        body, name="mixers_bwd", grid=(b_loc, pairs),
        in_specs=lead + [_smem_spec(), _smem_spec(), _smem_spec(), lane_blk(C_RQ), lane_blk(C_RK), lane_blk(C_RV),
                         lane_blk(C_RG), seq0, seq0, pl.BlockSpec((1, LANES), lambda b, h: (0, h)),
                         aq_spec, kv_spec(C_AK), kv_spec(C_AV), a_spec],
        out_specs=[seq0] * 4 + [pl.BlockSpec((ST_ROWS, LANES), lambda b, h: (b, h)), a_spec, kv_out, kv_out,
                                pl.BlockSpec((8, LANES), lambda b, h: (b * KV_HEADS + h // 2, 0))],
        out_shape=[SDS((t, RET_W), BF16)] * 4 + [SDS((b_loc * ST_ROWS, RET_W), F32), SDS((t, ATT_W), BF16),
                                                   SDS((t, KV_W), BF16), SDS((t, KV_W), BF16),
                                                   SDS((b_loc * KV_HEADS * 8, LANES), F32)],
        scratch_shapes=[state, state, pltpu.VMEM((s, LANES), BF16), pad, pad,
                        pltpu.VMEM((GROUP * BLOCK, KEYS), F32), acc, acc],
        compiler_params=_params(("arbitrary", "arbitrary")),
    )(*after, lgf, lgb, sink, u, u, u, u, y_pre, dr, gn_gain, u, u, u, da)


def _attention_bwd(u, da, sink, b_loc):
    t = u.shape[0]
    s = t // b_loc
    n_steps = s // (ATT_SUB * BLOCK)

    def body(sink_ref, q_ref, k_ref, v_ref, do_ref, dq_ref, dk_ref, dv_ref, dsink_ref, kpad, vpad, bias, dk_acc, dv_acc):
        g, step = pl.program_id(1), pl.program_id(2)

        @pl.when(step == 0)
        def _():
            _attn_tables(g, bias)
            _tile_keys(k_ref, g, Q_SCALE, kpad, s)
            _tile_keys(v_ref, g, 1.0, vpad, s)
            dsink_ref[...] = jnp.zeros_like(dsink_ref)

        @pl.when((step == 0) & (g == 0))
        def _():
            dk_acc[...] = jnp.zeros_like(dk_acc)
            dv_acc[...] = jnp.zeros_like(dv_acc)

        sink_col = _sink_column(sink_ref, g)
        head_row = lax.broadcasted_iota(jnp.int32, dsink_ref.shape, 0)
        upd = jnp.zeros(dsink_ref.shape, F32)
        for j in range(ATT_SUB):
            n = step * ATT_SUB + j
            rows = pl.ds(j * BLOCK, BLOCK)
            keys = pl.ds(pl.multiple_of(n * BLOCK, BLOCK), KEYS)
            qm = _stack_heads(q_ref[rows, :])
            k3, v3 = kpad[keys, :], vpad[keys, :]
            p, p_sink = _attn_probs(qm, k3, bias, sink_col, n, s)
            dom = _stack_heads(do_ref[rows, :])
            dp = _dot_nt(dom, v3)
            delta = jnp.sum(p * dp, axis=1, keepdims=True)
            ds_mat = (p * (dp - delta)).astype(BF16)
            dq_ref[rows, :] = _unstack_heads(_dot_nn(ds_mat, k3)).astype(BF16)
            dk_acc[keys, :] += _fold_groups(_dot_tn(ds_mat, qm), g) * Q_SCALE
            dv_acc[keys, :] += _fold_groups(_dot_tn(p.astype(BF16), dom), g)
            w = p_sink * delta
            for h in range(GROUP):
                upd = upd + jnp.where(head_row == h, -jnp.sum(w[h * BLOCK:(h + 1) * BLOCK, :]), 0.0)
        dsink_ref[...] += upd

        @pl.when((step == n_steps - 1) & (g == KV_HEADS - 1))
        def _():
            dk_ref[...] = dk_acc[BLOCK:BLOCK + s, :].astype(BF16)
            dv_ref[...] = dv_acc[BLOCK:BLOCK + s, :].astype(BF16)

    q_spec, k_spec, v_spec, o_spec = _attn_specs(s, n_steps)
    kv_out = pl.BlockSpec((s, LANES), lambda b, g, n: (b, 0))
    pad = pltpu.VMEM((s + 2 * BLOCK, GW), BF16)
    acc = pltpu.VMEM((s + 2 * BLOCK, LANES), F32)
    return pl.pallas_call(
        body, name="attention_bwd", grid=(b_loc, KV_HEADS, n_steps),
        in_specs=[_smem_spec(), q_spec, k_spec, v_spec, o_spec],
        out_specs=[o_spec, kv_out, kv_out, pl.BlockSpec((8, LANES), lambda b, g, n: (b * KV_HEADS + g, 0))],
        out_shape=[SDS((t, ATT_W), BF16), SDS((t, KV_W), BF16), SDS((t, KV_W), BF16),
                   SDS((b_loc * KV_HEADS * 8, LANES), F32)],
        scratch_shapes=[pad, pad, pltpu.VMEM((GROUP * BLOCK, KEYS), F32), acc, acc],
        compiler_params=_params(("arbitrary", "arbitrary", "arbitrary")),
    )(sink, u, u, u, da)


def _pack_small(acc2, acc1, ret_stats, dsink, b_loc, d):
    pairs = RET_HEADS // 2

    def body(acc2_ref, acc1_ref, st_ref, dsink_ref, out_ref):
        out_ref[...] = jnp.zeros_like(out_ref)
        out_ref[ROW_LN1G:ROW_LN1G + 1, :] = acc1_ref[0:1, :]
        out_ref[ROW_LN1B:ROW_LN1B + 1, :] = acc1_ref[1:2, :]
        out_ref[ROW_LN2G:ROW_LN2G + 1, :] = acc2_ref[1:2, :]
        out_ref[ROW_LN2B:ROW_LN2B + 1, :] = acc2_ref[2:3, :]
        out_ref[ROW_LOSS:ROW_LOSS + 1, :] = acc2_ref[0:1, :]
        st = st_ref[0:ST_ROWS, :]
        for b in range(1, b_loc):
            st = st + st_ref[b * ST_ROWS:(b + 1) * ST_ROWS, :]
        out_ref[ROW_GN:ROW_GN + 1, 0:RET_W] = st[ST_GAIN:ST_GAIN + 1, :]
        lane = lax.broadcasted_iota(jnp.int32, (1, d), 1)
        misc = jnp.zeros((1, d), F32)
        for pr in range(pairs):
            blk = st[:, pr * LANES:(pr + 1) * LANES]
            half = lax.broadcasted_iota(jnp.int32, (1, LANES), 1) < HEAD_DIM
            for h in range(2):
                sel = half if h == 0 else jnp.logical_not(half)
                cross_f = jnp.sum(jnp.where(sel, blk[ST_XF:ST_XF + 1, :] + blk[ST_LF:ST_LF + 1, :], 0.0))
                cross_b = jnp.sum(jnp.where(sel, blk[ST_XB:ST_XB + 1, :] + blk[ST_LB:ST_LB + 1, :], 0.0))
                intra_f = jnp.sum(blk[ST_IFA + h:ST_IFA + h + 1, :])
                intra_b = jnp.sum(blk[ST_IBA + h:ST_IBA + h + 1, :])
                head = 2 * pr + h
                misc = jnp.where(lane == MISC_DF + head, cross_f + intra_f, misc)
                misc = jnp.where(lane == MISC_DB + head, cross_b + intra_b, misc)
        for g in range(KV_HEADS):
            tot = dsink_ref[g * 8:(g + 1) * 8, :]
            for b in range(1, b_loc):
                tot = tot + dsink_ref[(b * KV_HEADS + g) * 8:(b * KV_HEADS + g + 1) * 8, :]
            for h in range(GROUP):
                misc = jnp.where(lane == MISC_SINK + GROUP * g + h, jnp.sum(tot[h:h + 1, 0:1]), misc)
        out_ref[ROW_MISC:ROW_MISC + 1, :] = misc

    return pl.pallas_call(body, name="pack_small", out_shape=SDS((SMALL_ROWS, d), F32))(acc2, acc1, ret_stats, dsink)


BIG = ("w_in", "w_out", "w_ffn_gate", "w_ffn_up", "w_ffn_down", "w_ple_proj", "w_ple_gate")
TRANSPOSED_OUTSIDE = ("w_in", "w_ffn_gate", "w_ffn_up")
TRANSPOSED_HERE = ("w_ple_proj",)
SMALL = ("ret_decay_fwd", "ret_decay_bwd", "ret_gn_gain", "attn_sink", "ln1_gain", "ln1_bias", "ln2_gain", "ln2_bias")
ORDER = ("w_in", "ret_decay_fwd", "ret_decay_bwd", "ret_gn_gain", "attn_sink", "w_out", "ln1_gain", "ln1_bias",
         "w_ffn_gate", "w_ffn_up", "w_ffn_down", "w_ple_proj", "w_ple_gate", "ln2_gain", "ln2_bias")


GATHER_ORDER = ("w_in", "w_out", "w_ffn_gate", "w_ffn_up", "w_ple_gate", "w_ple_proj", "w_ffn_down")


def _local_step(x2, p2, target2, fetch, publish, small, b_loc, me):
    d = x2.shape[1]
    lgf, lgb = _log_decay(small["ret_decay_fwd"], small["ret_decay_bwd"])
    lgf1, lgb1, sink1 = lgf.reshape(-1), lgb.reshape(-1), small["attn_sink"].reshape(-1)
    (w_in,) = fetch(("w_in",), ())
    u, xb = _in_proj(x2, w_in)
    r, y_pre, a = _mixers_fwd(u, lgf1, lgb1, small["ret_gn_gain"], sink1, b_loc)
    w_out, w_gate, w_up = fetch(("w_out", "w_ffn_gate", "w_ffn_up"), (r, a))
    z1, h1b, dact_dg, dact_du, act = _mix_ln1_ffn_up(
        r, a, x2, w_out, w_gate, w_up, small["ln1_gain"], small["ln1_bias"])
    w_pg, w_pe, w_down = fetch(("w_ple_gate", "w_ple_proj", "w_ffn_down"), (act,))
    dz2, dz2b, dsb, dpleb, dg, dup, acc2 = _ffn_down_ln2_loss(
        act, dact_dg, dact_du, h1b, p2, z1, target2, w_down, w_pg, w_pe,
        small["ln1_gain"], small["ln1_bias"], small["ln2_gain"], small["ln2_bias"])
    own = {}

    def grad(name, parts, rhs, after=()):
        whole, own[name] = _weight_grad("grad_" + name, me, parts, rhs, after)
        return whole

    t2 = publish("ffn", dict(w_ffn_down=grad("w_ffn_down", [act], dz2b),
                             w_ple_proj=grad("w_ple_proj", [dpleb], p2),
                             w_ple_gate=grad("w_ple_gate", [h1b], dsb),
                             w_ffn_gate=grad("w_ffn_gate", [dg], h1b),
                             w_ffn_up=grad("w_ffn_up", [dup], h1b)))
    dz1, dz1b, dr, da, acc1 = _dh1_ln1_bwd(dz2, dg, dup, dsb, z1, w_gate, w_up, w_pg, w_out, small["ln1_gain"], t2)
    t3 = publish("out", dict(w_out=grad("w_out", [r, a], dz1b)))
    dq, dk, dv, dgate, ret_stats, daq, dak, dav, dsink = _mixers_bwd(
        u, y_pre, dr, da, lgf1, lgb1, small["ret_gn_gain"], sink1, b_loc, t3)
    parts = [dq, dk, dv, dgate, daq, dak, dav]
    small_part = _pack_small(acc2, acc1, ret_stats, dsink, b_loc, d)
    t4 = publish("in", dict(w_in=grad("w_in", parts, xb)), small_part)
    grad_x = _in_proj_bwd(dz1, parts, w_in, t4)
    return grad_x, own, small_part


def kernel(x, p, w_in, ret_decay_fwd, ret_decay_bwd, ret_gn_gain, attn_sink, w_out, ln1_gain, ln1_bias, w_ffn_gate, w_ffn_up, w_ffn_down, w_ple_proj, w_ple_gate, ln2_gain, ln2_bias, loss_target, m_w_in, m_ret_decay_fwd, m_ret_decay_bwd, m_ret_gn_gain, m_attn_sink, m_w_out, m_ln1_gain, m_ln1_bias, m_w_ffn_gate, m_w_ffn_up, m_w_ffn_down, m_w_ple_proj, m_w_ple_gate, m_ln2_gain, m_ln2_bias, v_w_in, v_ret_decay_fwd, v_ret_decay_bwd, v_ret_gn_gain, v_attn_sink, v_w_out, v_ln1_gain, v_ln1_bias, v_w_ffn_gate, v_w_ffn_up, v_w_ffn_down, v_w_ple_proj, v_w_ple_gate, v_ln2_gain, v_ln2_bias):
    given = dict(locals())

    def strip(n, a):
        if n not in BIG:
            return a
        return a[0].T if n in TRANSPOSED_OUTSIDE else a[0]

    def restore(n, a):
        if n not in BIG:
            return a
        return (a.T if n in TRANSPOSED_OUTSIDE else a)[None]

    w = {n: strip(n, given[n]) for n in ORDER}
    m = {n: strip(n, given["m_" + n]) for n in ORDER}
    v = {n: strip(n, given["v_" + n]) for n in ORDER}
    b_loc, s, d = x.shape
    x2 = x.reshape(b_loc * s, d)
    p2 = p[0].reshape(b_loc * s, p.shape[-1])
    target2 = loss_target.reshape(b_loc * s, d)

    small = {n: w[n] for n in SMALL}
    me = (4 * lax.axis_index("x") + 2 * lax.axis_index("y") + lax.axis_index("c")).astype(jnp.int32).reshape(1)

    gathered = _prep_shards(me, {n: w[n] for n in BIG})
    gather = _split_copy_start("gather_start", [(gathered[n],) for n in GATHER_ORDER],
                               [_gather_copy_near] + [_gather_copy] * (len(GATHER_ORDER) - 1))

    def fetch(names, after):
        if names == ("w_in",):
            near = _split_copy_wait("gather_wait_w_in_near", gather, [0], list(after))
            passed = _split_copy_start("gather_pass_w_in", near, [_gather_copy_pass])
            return [_split_copy_wait("gather_wait_w_in", passed, [0], [])[0][0]]
        which = [GATHER_ORDER.index(n) for n in names]
        got = _split_copy_wait("gather_wait_" + names[0], gather, which, list(after))
        return [item[0] for item in got]

    scatters = []

    def publish(tag, products, small_sums=None):
        items = [(products[n], lax.empty((N_DEV - 1, products[n].shape[0] // N_DEV, products[n].shape[1]), BF16))
                 for n in products]
        copies = [_scatter_copy] * len(items)
        if small_sums is not None:
            items.append((small_sums, lax.empty((N_DEV - 1,) + small_sums.shape, F32)))
            copies.append(_small_copy)
        started = _split_copy_start("scatter_start_" + tag, items, copies)
        scatters.append((list(products), small_sums is not None, started))
        return (started["token"],)

    grad_x, own, small_part = _local_step(x2, p2, target2, fetch, publish, small, b_loc, me)

    out_g, out_d, out_m, out_v = {}, {}, {}, {}
    after = [grad_x]
    for names, with_small, started in scatters:
        landed = _split_copy_wait("scatter_wait_" + names[0], started, list(range(len(started["items"]))), after)
        if with_small:
            loss, sg, sd, sm, sv = _small_adamw(
                me, small_part, landed[-1][1], small, {n: m[n] for n in SMALL}, {n: v[n] for n in SMALL})
            for dst, src in ((out_g, sg), (out_d, sd), (out_m, sm), (out_v, sv)):
                dst.update(src)
        for n, (_, recv) in zip(names, landed):
            out_g[n], out_d[n], out_m[n], out_v[n] = _reduce_adamw(
                n, own[n], recv, w[n], m[n], v[n], n in TRANSPOSED_HERE)
        after = [out_v[names[-1]]]

    outs = [loss[0, 0], grad_x.reshape(x.shape)]
    for group in (out_g, out_d, out_m, out_v):
        outs += [restore(n, group[n]) for n in ORDER]
    return tuple(outs)
```

```python
import functools

import jax
import jax.numpy as jnp
from jax import lax
from jax.experimental import pallas as pl
from jax.experimental.pallas import tpu as pltpu

F32, BF16 = jnp.float32, jnp.bfloat16
SDS = jax.ShapeDtypeStruct
MESH = pl.DeviceIdType.MESH

N_DEV = 8
HEAD_DIM = 64
RET_HEADS = 8
ATTN_HEADS = 8
KV_HEADS = 2
GROUP = ATTN_HEADS // KV_HEADS
RET_W = RET_HEADS * HEAD_DIM
ATT_W = ATTN_HEADS * HEAD_DIM
KV_W = KV_HEADS * HEAD_DIM
LANES = 128
CHUNK = 128
BLOCK = 128
Q_SCALE = HEAD_DIM ** -0.5
ALPHA = 2.0 ** 0.25
LN_EPS = 1e-5
GN_EPS = 1e-5
NEG_INF = -1e30
C_RQ, C_RK, C_RV, C_RG = 0, RET_W, 2 * RET_W, 3 * RET_W
C_AQ = 4 * RET_W
C_AK = C_AQ + ATT_W
C_AV = C_AK + KV_W
IN_W = C_AV + KV_W

ADAM_LR = 0.001
ADAM_B1 = 0.9
ADAM_B2 = 0.999
ADAM_EPS = 1e-08
ADAM_WD = 0.01
ADAM_STEP = 10

VMEM_LIMIT = 56 * 1024 * 1024
MATMUL_ROWS = 512
EPILOGUE_ROWS = 256
SUB_ROWS = 256
SMALL_ROWS = 16
ROW_LN1G, ROW_LN1B, ROW_LN2G, ROW_LN2B, ROW_LOSS, ROW_GN, ROW_MISC = 0, 1, 2, 3, 4, 5, 6
MISC_DF, MISC_DB, MISC_SINK = 0, 8, 16


def _dot_nn(a, b):
    return lax.dot_general(a, b, (((1,), (0,)), ((), ())), preferred_element_type=F32)


def _dot_nt(a, b):
    return lax.dot_general(a, b, (((1,), (1,)), ((), ())), preferred_element_type=F32)


def _dot_tn(a, b):
    return lax.dot_general(a, b, (((0,), (0,)), ((), ())), preferred_element_type=F32)


def _params(sem=None, vmem=VMEM_LIMIT):
    kw = {"vmem_limit_bytes": vmem}
    if sem is not None:
        kw["dimension_semantics"] = sem
    return pltpu.CompilerParams(**kw)


def _row_tile(t, want=512):
    tm = want
    while t % tm:
        tm //= 2
    return tm


def _sigmoid(x):
    return 1.0 / (1.0 + jnp.exp(-x))


def _layer_norm_stats(z):
    mu = jnp.mean(z, axis=1, keepdims=True)
    d = z - mu
    var = jnp.mean(d * d, axis=1, keepdims=True)
    rstd = lax.rsqrt(var + LN_EPS)
    return d * rstd, rstd


def _layer_norm_bwd(dxh, xhat, rstd):
    m1 = jnp.mean(dxh, axis=1, keepdims=True)
    m2 = jnp.mean(dxh * xhat, axis=1, keepdims=True)
    return rstd * (dxh - m1 - xhat * m2)


def _prep_shards(me, shards):
    names = list(shards)

    def body(me_ref, *refs):
        for name, src, dst in zip(names, refs[:len(names)], refs[len(names):]):
            val = src[...]
            dst[...] = (val.T if name in TRANSPOSED_HERE else val).astype(BF16)

    shape = lambda n, a: a.shape[::-1] if n in TRANSPOSED_HERE else a.shape
    shapes = [shape(n, shards[n]) for n in names]
    out = pl.pallas_call(
        body, name="prep_shards",
        grid_spec=pltpu.PrefetchScalarGridSpec(
            num_scalar_prefetch=1, grid=(1,),
            in_specs=[pl.BlockSpec(shards[n].shape, lambda i, me_ref: (0, 0)) for n in names],
            out_specs=[pl.BlockSpec(s, lambda i, me_ref: (me_ref[0], 0)) for s in shapes]),
        out_shape=[SDS((N_DEV * s[0], s[1]), BF16) for s in shapes], compiler_params=_params(("arbitrary",)),
    )(me, *[shards[n] for n in names])
    return dict(zip(names, out))


def _mesh_pos():
    return lax.axis_index("x"), lax.axis_index("y"), lax.axis_index("c")


HBM_SPEC = pl.BlockSpec(memory_space=pltpu.HBM)
SEM_SPEC = pl.BlockSpec(memory_space=pltpu.SEMAPHORE)
ANY_SPEC = pl.BlockSpec(memory_space=pl.ANY)
SIDE_EFFECT = pltpu.SideEffectType.DATAFLOW_SIDE_EFFECTING
PEER_SEMS = pltpu.SemaphoreType.DMA((N_DEV - 1,))


def _in_hbm(a):
    return pltpu.with_memory_space_constraint(a, pltpu.HBM)


def _split_copy_start(name, items, copies):
    n = len(items)
    flat = [a for it in items for a in it]
    k = len(flat)

    def body(*refs):
        arr, sems = list(refs[:k]), refs[k:k + 2 * n]
        for i, it in enumerate(items):
            mine = [arr.pop(0) for _ in it]
            for m in range(1, N_DEV):
                cp = copies[i](m, mine, sems[i].at[m - 1], sems[n + i].at[m - 1])
                if cp is not None:
                    cp.start()
        token = refs[-1]
        token[...] = jnp.zeros_like(token)

    res = pl.pallas_call(
        body, name=name,
        out_shape=[PEER_SEMS] * (2 * n) + [pltpu.HBM(a.shape, a.dtype) for a in flat] + [SDS((8, LANES), F32)],
        in_specs=[HBM_SPEC] * k,
        out_specs=[SEM_SPEC] * (2 * n) + [HBM_SPEC] * k + [pl.BlockSpec(memory_space=pltpu.VMEM)],
        input_output_aliases={j: 2 * n + j for j in range(k)},
        compiler_params=pltpu.CompilerParams(has_side_effects=SIDE_EFFECT),
    )(*[_in_hbm(a) for a in flat])
    thru, out_items = list(res[2 * n:2 * n + k]), []
    for it in items:
        out_items.append(tuple(thru.pop(0) for _ in it))
    return dict(send=res[:n], recv=res[n:2 * n], items=out_items, token=res[-1], copies=copies)


def _split_copy_wait(name, started, which, after):
    items = [started["items"][i] for i in which]
    copies = [started["copies"][i] for i in which]
    n = len(items)
    flat = [a for it in items for a in it]
    k = len(flat)

    def body(*refs):
        arr, sems = list(refs[:k]), refs[k:k + 2 * n]
        for i, it in enumerate(items):
            mine = [arr.pop(0) for _ in it]
            for m in range(1, N_DEV):
                cp = copies[i](m, mine, sems[i].at[m - 1], sems[n + i].at[m - 1])
                if cp is not None:
                    cp.wait_send()
                    cp.wait_recv()

    res = pl.pallas_call(
        body, name=name,
        out_shape=[pltpu.HBM(a.shape, a.dtype) for a in flat],
        in_specs=[HBM_SPEC] * k + [SEM_SPEC] * (2 * n) + [ANY_SPEC] * len(after),
        out_specs=[HBM_SPEC] * k,
        input_output_aliases={j: j for j in range(k)},
        compiler_params=pltpu.CompilerParams(has_side_effects=SIDE_EFFECT),
    )(*flat, *[started["send"][i] for i in which], *[started["recv"][i] for i in which], *[_in_hbm(a) for a in after])
    thru, out_items = list(res), []
    for it in items:
        out_items.append(tuple(thru.pop(0) for _ in it))
    return out_items


def _gather_copy(m, refs, send_sem, recv_sem):
    (land_ref,) = refs
    r = land_ref.shape[0] // N_DEV
    mine = land_ref.at[pl.ds(pl.multiple_of(_peer_index(0) * r, 8), r), :]
    return pltpu.make_async_remote_copy(src_ref=mine, dst_ref=mine, send_sem=send_sem, recv_sem=recv_sem,
                                        device_id=_peer(m), device_id_type=MESH)


def _gather_copy_near(m, refs, send_sem, recv_sem):
    return _gather_copy(m, refs, send_sem, recv_sem) if m == 1 or m % 2 == 0 else None


def _gather_copy_pass(m, refs, send_sem, recv_sem):
    if m == 1 or m % 2 == 0:
        return None
    (land_ref,) = refs
    r = land_ref.shape[0] // N_DEV
    block = land_ref.at[pl.ds(pl.multiple_of(_peer_index(m ^ 1) * r, 8), r), :]
    return pltpu.make_async_remote_copy(src_ref=block, dst_ref=block, send_sem=send_sem, recv_sem=recv_sem,
                                        device_id=_peer(1), device_id_type=MESH)


def _small_copy(m, refs, send_sem, recv_sem):
    part_ref, land_ref = refs
    return pltpu.make_async_remote_copy(src_ref=part_ref, dst_ref=land_ref.at[m - 1], send_sem=send_sem,
                                        recv_sem=recv_sem, device_id=_peer(m), device_id_type=MESH)


def _scatter_copy(m, refs, send_sem, recv_sem):
    buf_ref, land_ref = refs
    r = buf_ref.shape[0] // N_DEV
    src = buf_ref.at[pl.ds(pl.multiple_of(_peer_index(m) * r, 8), r), :]
    return pltpu.make_async_remote_copy(src_ref=src, dst_ref=land_ref.at[m - 1], send_sem=send_sem,
                                        recv_sem=recv_sem, device_id=_peer(m), device_id_type=MESH)


def _peer(m):
    x, y, c = _mesh_pos()
    bx, by, bc = (m >> 2) & 1, (m >> 1) & 1, m & 1
    return (x ^ bx if bx else x, y ^ by if by else y, c ^ bc if bc else c)


def _peer_index(m):
    x, y, c = _mesh_pos()
    return (4 * x + 2 * y + c) ^ m


SMALL_PLACE = {
    "ln1_gain": (ROW_LN1G, 0), "ln1_bias": (ROW_LN1B, 0), "ln2_gain": (ROW_LN2G, 0), "ln2_bias": (ROW_LN2B, 0),
    "ret_gn_gain": (ROW_GN, 0), "ret_decay_fwd": (ROW_MISC, MISC_DF), "ret_decay_bwd": (ROW_MISC, MISC_DB),
    "attn_sink": (ROW_MISC, MISC_SINK)}


def _small_adamw(me, part, landed, w, m, v):
    d = part.shape[1]
    names = list(SMALL_PLACE)
    k = len(names)

    def body(*refs):
        me_ref, part_ref, land_ref = refs[:3]
        refs = refs[2:]
        w_refs, m_refs, v_refs = refs[1:1 + k], refs[1 + k:1 + 2 * k], refs[1 + 2 * k:1 + 3 * k]
        outs = refs[1 + 3 * k:1 + 7 * k + 1]
        tot_ref = refs[-1]
        loss_ref, g_refs, dl_refs = outs[0], outs[1:1 + k], outs[1 + k:1 + 2 * k]
        nm_refs, nv_refs = outs[1 + 2 * k:1 + 3 * k], outs[1 + 3 * k:1 + 4 * k]
        tot = jnp.zeros(part_ref.shape, F32)
        for dev in range(N_DEV):
            j = dev ^ me_ref[0]
            tot = tot + jnp.where(j == 0, part_ref[...], land_ref[jnp.maximum(j, 1) - 1])
        tot_ref[...] = tot
        loss_ref[...] = (0.5 / d) * jnp.sum(tot_ref[ROW_LOSS:ROW_LOSS + 1, :], axis=1, keepdims=True)
        for i, name in enumerate(names):
            row, lo = SMALL_PLACE[name]
            wv = w_refs[i][...]
            g = tot_ref[row:row + 1, lo:lo + wv.shape[1]]
            if name.startswith("ret_decay"):
                p2 = jnp.exp2(wv)
                g = g * (-p2 * jnp.log(2.0) / (1.0 - p2))
            g_refs[i][...] = g
            _adamw_store(g, wv, m_refs[i][...], v_refs[i][...], dl_refs[i], nm_refs[i], nv_refs[i])

    shapes = [SDS(w[n].shape, F32) for n in names]
    vm = pl.BlockSpec(memory_space=pltpu.VMEM)
    res = pl.pallas_call(
        body, name="small_adamw", out_shape=[SDS((1, 1), F32)] + shapes * 4,
        in_specs=[_smem_spec()] + [vm] * (2 + 3 * k), out_specs=[vm] * (1 + 4 * k),
        scratch_shapes=[pltpu.VMEM(part.shape, F32)],
    )(me, part, landed, *[w[n] for n in names], *[m[n] for n in names], *[v[n] for n in names])
    groups = [dict(zip(names, res[1 + j * k:1 + (j + 1) * k])) for j in range(4)]
    return (res[0], *groups)


def _adamw_store(g, w, m, v, dl_ref, nm_ref, nv_ref):
    m = ADAM_B1 * m + (1.0 - ADAM_B1) * g
    v = ADAM_B2 * v + (1.0 - ADAM_B2) * (g * g)
    m_hat = m / (1.0 - ADAM_B1 ** ADAM_STEP)
    v_hat = v / (1.0 - ADAM_B2 ** ADAM_STEP)
    dl_ref[...] = -ADAM_LR * (m_hat / (jnp.sqrt(v_hat) + ADAM_EPS) + ADAM_WD * w)
    nm_ref[...] = m
    nv_ref[...] = v


def _reduce_adamw(name, own, recv, w, m, v, transposed):
    rows, n = own.shape
    steps = 1 if transposed or rows % 32 else 4
    rb = rows // steps

    def body(own_ref, recv_ref, w_ref, m_ref, v_ref, g_ref, dl_ref, nm_ref, nv_ref):
        g = own_ref[...]
        for k in range(N_DEV - 1):
            g = g + recv_ref[k].astype(F32)
        if transposed:
            g = g.T
        g_ref[...] = g
        _adamw_store(g, w_ref[...], m_ref[...], v_ref[...], dl_ref, nm_ref, nv_ref)

    blk = pl.BlockSpec(w.shape if transposed else (rb, n), lambda i: (i, 0))
    out = SDS(w.shape, F32)
    return pl.pallas_call(
        body, name="adamw_" + name, grid=(steps,),
        in_specs=[pl.BlockSpec((rb, n), lambda i: (i, 0)), pl.BlockSpec((N_DEV - 1, rb, n), lambda i: (0, i, 0)),
                  blk, blk, blk],
        out_specs=[blk] * 4, out_shape=[out] * 4, compiler_params=_params(("parallel",)),
    )(own, recv, w, m, v)


def _row_spec(tm, width):
    return pl.BlockSpec((tm, width), lambda i: (i, 0))


def _full_spec(shape):
    return pl.BlockSpec(shape, lambda i: (0,) * len(shape))


_acc_spec = _full_spec


def _sub_rows(tm):
    step = min(SUB_ROWS, tm)
    return [(lo, lo + step) for lo in range(0, tm, step)]


def _in_proj(x2, wt_in):
    t, d = x2.shape
    u_w = wt_in.shape[0]
    tm = _row_tile(t, MATMUL_ROWS)

    def body(x_ref, w_ref, u_ref, xb_ref):
        xb = x_ref[...].astype(BF16)
        xb_ref[...] = xb
        u_ref[...] = _dot_nt(xb, w_ref[...]).astype(BF16)

    return pl.pallas_call(
        body, name="in_proj", grid=(t // tm,),
        in_specs=[_row_spec(tm, d), _full_spec(wt_in.shape)],
        out_specs=[_row_spec(tm, u_w), _row_spec(tm, d)],
        out_shape=[SDS((t, u_w), BF16), SDS((t, d), BF16)],
        compiler_params=_params(("parallel",)),
    )(x2, wt_in)


def _col_halves(f):
    n = f // LANES
    k = (n + 1) // 2 * LANES
    return [(0, k), (k, f)] if k < f else [(0, f)]


def _mix_ln1_ffn_up(r, a, x2, w_out, wt_gate, wt_up, g1, b1):
    t, d = x2.shape
    f = wt_gate.shape[0]
    tm = _row_tile(t, EPILOGUE_ROWS)

    def body(r_ref, a_ref, x_ref, wo_ref, wg_ref, wu_ref, g_ref, b_ref, z_ref, hb_ref, dg_ref, du_ref, act_ref):
        mix = _dot_nn(r_ref[...], wo_ref[0:RET_W, :]) + _dot_nn(a_ref[...], wo_ref[RET_W:RET_W + ATT_W, :])
        z = ALPHA * x_ref[...] + mix
        xhat, _ = _layer_norm_stats(z)
        z_ref[...] = z
        h = (xhat * g_ref[...] + b_ref[...]).astype(BF16)
        hb_ref[...] = h
        for lo, hi in _col_halves(f):
            g = _dot_nt(h, wg_ref[lo:hi, :])
            u = _dot_nt(h, wu_ref[lo:hi, :])
            sg = _sigmoid(g)
            silu = g * sg
            dg_ref[:, lo:hi] = (u * (sg * (1.0 + g * (1.0 - sg)))).astype(BF16)
            du_ref[:, lo:hi] = silu.astype(BF16)
            act_ref[:, lo:hi] = (silu * u).astype(BF16)

    wide, narrow = _row_spec(tm, f), _row_spec(tm, d)
    return pl.pallas_call(
        body, name="mix_ln1_ffn_up", grid=(t // tm,),
        in_specs=[_row_spec(tm, RET_W), _row_spec(tm, ATT_W), narrow, _resident_spec(w_out.shape),
                  _resident_spec(wt_gate.shape), _resident_spec(wt_up.shape), _full_spec(g1.shape),
                  _full_spec(b1.shape)],
        out_specs=[narrow, narrow, wide, wide, wide],
        out_shape=[SDS((t, d), F32), SDS((t, d), BF16)] + [SDS((t, f), BF16)] * 3,
        compiler_params=_params(("parallel",)),
    )(r, a, x2, w_out, wt_gate, wt_up, g1, b1)


def _ffn_down_ln2_loss(act, dact_dg, dact_du, h1b, p2, z1, target, w_down, w_pg, wt_pe, g1, b1, g2, b2):
    t, d = z1.shape
    f = act.shape[1]
    pdim = p2.shape[1]
    tm = _row_tile(t, EPILOGUE_ROWS)

    def body(act_ref, fg_ref, fu_ref, hb_ref, p_ref, z1_ref, tgt_ref, wd_ref, wpg_ref, wpe_ref, g1_ref, b1_ref,
             g2_ref, b2_ref, dz_ref, dzb_ref, ds_ref, dple_ref, dg_ref, du_ref, acc_ref):
        @pl.when(pl.program_id(0) == 0)
        def _():
            acc_ref[...] = jnp.zeros_like(acc_ref)

        for lo, hi in _sub_rows(tm):
            xhat1, _ = _layer_norm_stats(z1_ref[lo:hi, :])
            h1 = xhat1 * g1_ref[...] + b1_ref[...]
            ffn = _dot_nn(act_ref[lo:hi, :], wd_ref[...])
            pg = _sigmoid(_dot_nn(hb_ref[lo:hi, :], wpg_ref[...]))
            ple = _dot_nt(p_ref[lo:hi, :].astype(BF16), wpe_ref[...])
            z2 = ALPHA * h1 + ffn + pg * ple
            xhat2, rstd2 = _layer_norm_stats(z2)
            err = xhat2 * g2_ref[...] + b2_ref[...] - tgt_ref[lo:hi, :]
            dy = err * (1.0 / d)
            dz = _layer_norm_bwd(dy * g2_ref[...], xhat2, rstd2)
            dzb = dz.astype(BF16)
            dz_ref[lo:hi, :] = dz
            dzb_ref[lo:hi, :] = dzb
            ds_ref[lo:hi, :] = (dz * ple * pg * (1.0 - pg)).astype(BF16)
            dple_ref[lo:hi, :] = (dz * pg).astype(BF16)
            acc_ref[0:1, :] += jnp.sum(err * err, axis=0, keepdims=True)
            acc_ref[1:2, :] += jnp.sum(dy * xhat2, axis=0, keepdims=True)
            acc_ref[2:3, :] += jnp.sum(dy, axis=0, keepdims=True)
            for c0, c1 in _col_halves(f):
                da = _dot_nt(dzb, wd_ref[c0:c1, :])
                dg_ref[lo:hi, c0:c1] = (da * fg_ref[lo:hi, c0:c1].astype(F32)).astype(BF16)
                du_ref[lo:hi, c0:c1] = (da * fu_ref[lo:hi, c0:c1].astype(F32)).astype(BF16)

    vec = _full_spec(g1.shape)
    wide, narrow = _row_spec(tm, f), _row_spec(tm, d)
    return pl.pallas_call(
        body, name="ffn_down_ln2_loss", grid=(t // tm,),
        in_specs=[wide, wide, wide, narrow, _row_spec(tm, pdim), narrow, narrow,
                  _full_spec(w_down.shape), _full_spec(w_pg.shape), _full_spec(wt_pe.shape), vec, vec, vec, vec],
        out_specs=[narrow] * 4 + [wide, wide, _acc_spec((8, d))],
        out_shape=[SDS((t, d), F32), SDS((t, d), BF16), SDS((t, d), BF16), SDS((t, d), BF16),
                   SDS((t, f), BF16), SDS((t, f), BF16), SDS((8, d), F32)],
        compiler_params=_params(("arbitrary",)),
    )(act, dact_dg, dact_du, h1b, p2, z1, target, w_down, w_pg, wt_pe, g1, b1, g2, b2)


def _after(after, body):
    k = len(after)
    return (lambda *refs: body(*refs[k:])), [ANY_SPEC] * k


def _resident_spec(shape):
    return pl.BlockSpec(shape, lambda i: (0,) * len(shape), pipeline_mode=pl.Buffered(1))


def _dh1_ln1_bwd(dz2, dg, dup, dsb, z1, wt_gate, wt_up, w_pg, w_out, g1, after=()):
    t, d = dz2.shape
    f = dg.shape[1]
    tm = _row_tile(t, EPILOGUE_ROWS)

    def body(dz_ref, dg_ref, du_ref, ds_ref, z1_ref, wg_ref, wu_ref, wpg_ref, wo_ref, g1_ref,
             dz1_ref, dz1b_ref, dr_ref, da_ref, acc_ref):
        @pl.when(pl.program_id(0) == 0)
        def _():
            acc_ref[...] = jnp.zeros_like(acc_ref)

        for lo, hi in _sub_rows(tm):
            dh = (ALPHA * dz_ref[lo:hi, :] + _dot_nn(dg_ref[lo:hi, :], wg_ref[...])
                  + _dot_nn(du_ref[lo:hi, :], wu_ref[...]) + _dot_nt(ds_ref[lo:hi, :], wpg_ref[...]))
            xhat, rstd = _layer_norm_stats(z1_ref[lo:hi, :])
            dz1 = _layer_norm_bwd(dh * g1_ref[...], xhat, rstd)
            dz1b = dz1.astype(BF16)
            dz1_ref[lo:hi, :] = dz1
            dz1b_ref[lo:hi, :] = dz1b
            acc_ref[0:1, :] += jnp.sum(dh * xhat, axis=0, keepdims=True)
            acc_ref[1:2, :] += jnp.sum(dh, axis=0, keepdims=True)
            dr_ref[lo:hi, :] = _dot_nt(dz1b, wo_ref[0:RET_W, :]).astype(BF16)
            da_ref[lo:hi, :] = _dot_nt(dz1b, wo_ref[RET_W:RET_W + ATT_W, :]).astype(BF16)

    body, lead = _after(after, body)
    return pl.pallas_call(
        body, name="dh1_ln1_bwd", grid=(t // tm,),
        in_specs=lead + [_row_spec(tm, d), _row_spec(tm, f), _row_spec(tm, f), _row_spec(tm, d), _row_spec(tm, d),
                         _resident_spec(wt_gate.shape), _resident_spec(wt_up.shape), _resident_spec(w_pg.shape),
                         _resident_spec(w_out.shape), _full_spec(g1.shape)],
        out_specs=[_row_spec(tm, d), _row_spec(tm, d), _row_spec(tm, RET_W), _row_spec(tm, ATT_W), _acc_spec((8, d))],
        out_shape=[SDS((t, d), F32), SDS((t, d), BF16), SDS((t, RET_W), BF16), SDS((t, ATT_W), BF16),
                   SDS((8, d), F32)],
        compiler_params=_params(("arbitrary",)),
    )(*after, dz2, dg, dup, dsb, z1, wt_gate, wt_up, w_pg, w_out, g1)


def _in_proj_bwd(dz1, parts, wt_in, after=()):
    t, d = dz1.shape
    tm = _row_tile(t, MATMUL_ROWS)
    widths = [p.shape[1] for p in parts]

    def body(*refs):
        dz_ref, part_refs, w_ref, dx_ref = refs[0], refs[1:1 + len(parts)], refs[-2], refs[-1]
        acc = ALPHA * dz_ref[...]
        lo = 0
        for p_ref, w in zip(part_refs, widths):
            acc = acc + _dot_nn(p_ref[...], w_ref[lo:lo + w, :])
            lo += w
        dx_ref[...] = acc

    body, lead = _after(after, body)
    return pl.pallas_call(
        body, name="in_proj_bwd", grid=(t // tm,),
        in_specs=lead + [_row_spec(tm, d)] + [_row_spec(tm, w) for w in widths] + [_full_spec(wt_in.shape)],
        out_specs=_row_spec(tm, d), out_shape=SDS((t, d), F32),
        compiler_params=_params(("parallel",)),
    )(*after, dz1, *parts, wt_in)


def _weight_grad(name, me, parts, rhs, after=()):
    t, n = rhs.shape
    widths = [p.shape[1] for p in parts]
    rows = sum(widths)
    own_rows = rows // N_DEV
    tk = _row_tile(t, MATMUL_ROWS)
    step = 256

    def body(*refs):
        me_ref, part_refs, rhs_ref = refs[0], refs[1:1 + len(parts)], refs[1 + len(parts)]
        full_ref, own_ref, acc = refs[-3], refs[-2], refs[-1]
        i = pl.program_id(0)

        @pl.when(i == 0)
        def _():
            acc[...] = jnp.zeros_like(acc)

        b = rhs_ref[...].astype(BF16)
        lo = 0
        for p_ref, w in zip(part_refs, widths):
            for c0 in range(0, w, step):
                c1 = min(c0 + step, w)
                acc[lo + c0:lo + c1, :] += _dot_tn(p_ref[:, c0:c1].astype(BF16), b)
            lo += w

        @pl.when(i == pl.num_programs(0) - 1)
        def _():
            full_ref[...] = acc[...].astype(BF16)
            own_ref[...] = acc[pl.ds(pl.multiple_of(me_ref[0] * own_rows, 8), own_rows), :]

    body, lead = _after(after, body)
    return pl.pallas_call(
        body, name=name, grid=(t // tk,),
        in_specs=lead + [_smem_spec()] + [_row_spec(tk, w) for w in widths] + [_row_spec(tk, n)],
        out_specs=[_full_spec((rows, n)), _full_spec((own_rows, n))],
        out_shape=[SDS((rows, n), BF16), SDS((own_rows, n), F32)],
        scratch_shapes=[pltpu.VMEM((rows, n), F32)],
        compiler_params=_params(("arbitrary",)),
    )(*after, me, *parts, rhs)


def _log_decay(decay_f, decay_b):
    def body(f_ref, b_ref, lf_ref, lb_ref):
        lf_ref[...] = jnp.log1p(-jnp.exp2(f_ref[...]))
        lb_ref[...] = jnp.log1p(-jnp.exp2(b_ref[...]))

    return pl.pallas_call(body, name="log_decay", out_shape=[SDS(decay_f.shape, F32)] * 2)(decay_f, decay_b)


def _chunk(ref, n):
    return ref[pl.ds(pl.multiple_of(n * CHUNK, CHUNK), CHUNK), :]


def _group_sum(is_a, v):
    sa = jnp.sum(jnp.where(is_a, v, 0.0), axis=1, keepdims=True)
    sb = jnp.sum(jnp.where(is_a, 0.0, v), axis=1, keepdims=True)
    return jnp.where(is_a, sa, sb)


def _seq_spec(s, col_block):
    return pl.BlockSpec((s, LANES), lambda b, h: (b, col_block + h))


def _smem_spec():
    return pl.BlockSpec(memory_space=pltpu.SMEM)


RET_UNROLL = 4


def _chunk_loop(n_chunks, body, init):
    u = RET_UNROLL if n_chunks % RET_UNROLL == 0 else 1

    def trip(i, carry):
        for j in range(u):
            carry = body(i * u + j, carry)
        return carry

    return lax.fori_loop(0, n_chunks // u, trip, init)


def _stacked_tables(lgf_ref, lgb_ref, pair):
    lane = lax.broadcasted_iota(jnp.int32, (1, LANES), 1)
    is_a = lane < HEAD_DIM
    lgf = jnp.where(is_a, lgf_ref[2 * pair], lgf_ref[2 * pair + 1])
    lgb = jnp.where(is_a, lgb_ref[2 * pair], lgb_ref[2 * pair + 1])
    row = lax.broadcasted_iota(jnp.int32, (CHUNK, 1), 0).astype(F32)
    kdec_f, qdec_f = jnp.exp(lgf * (CHUNK - 1.0 - row)), jnp.exp(lgf * (row + 1.0))
    kdec_b, qdec_b = jnp.exp(lgb * row), jnp.exp(lgb * (CHUNK - row))
    tab = dict(
        is_a=is_a, row=row, lam_f=jnp.exp(lgf * CHUNK), lam_b=jnp.exp(lgb * CHUNK),
        kdec=jnp.concatenate([kdec_f, kdec_b], axis=1), qdec=jnp.concatenate([qdec_f, qdec_b], axis=1),
        qexp=jnp.concatenate([jnp.broadcast_to(row + 1.0, (CHUNK, LANES)),
                              jnp.broadcast_to(CHUNK - row, (CHUNK, LANES))], axis=1),
        kexp=jnp.concatenate([jnp.broadcast_to(CHUNK - 1.0 - row, (CHUNK, LANES)),
                              jnp.broadcast_to(row, (CHUNK, LANES))], axis=1),
    )
    r = lax.broadcasted_iota(jnp.int32, (2 * LANES, LANES), 0)
    c = lax.broadcasted_iota(jnp.int32, (2 * LANES, LANES), 1)
    tab["diag2"] = ((r & (LANES - 1)) < HEAD_DIM) == (c < HEAD_DIM)
    i2 = lax.broadcasted_iota(jnp.int32, (2 * CHUNK, CHUNK), 0)
    j = lax.broadcasted_iota(jnp.int32, (2 * CHUNK, CHUNK), 1)
    head_b = i2 >= CHUNK
    diff = ((i2 & (CHUNK - 1)) - j).astype(F32)
    up, dn = jnp.maximum(diff, 0.0), jnp.maximum(-diff, 0.0)
    lgf2 = jnp.where(head_b, lgf_ref[2 * pair + 1], lgf_ref[2 * pair])
    lgb2 = jnp.where(head_b, lgb_ref[2 * pair + 1], lgb_ref[2 * pair])
    ef = jnp.where(diff >= 0, jnp.exp(lgf2 * up), 0.0)
    eb = jnp.where(diff <= 0, jnp.exp(lgb2 * dn), 0.0)
    tab["d2"] = ef + eb
    tab["df2"] = ef * up
    tab["db2"] = eb * dn
    return tab


def _stack_pair(is_a, x):
    zero = jnp.zeros_like(x)
    return jnp.concatenate([jnp.where(is_a, x, zero), jnp.where(is_a, zero, x)], axis=0)


def _unstack_pair(is_a, x2):
    return jnp.where(is_a, x2[0:CHUNK, :], x2[CHUNK:2 * CHUNK, :])


def _both_ways(x, dec):
    return (jnp.concatenate([x, x], axis=1) * dec).astype(BF16)


def _scan_states(n_chunks, st, up_rows, up_lam, down_rows, down_lam):
    zero = jnp.zeros((LANES, LANES), F32)

    def up(n, r):
        new = st[n, up_rows, :]
        st[n, up_rows, :] = r
        return r * up_lam + new

    def down(s, r):
        n = n_chunks - 1 - s
        new = st[n, down_rows, :]
        st[n, down_rows, :] = r
        return r * down_lam + new

    lax.fori_loop(0, n_chunks, up, zero)
    lax.fori_loop(0, n_chunks, down, zero)


FWD_ROWS, BWD_ROWS = pl.ds(0, LANES), pl.ds(LANES, LANES)


ST_GAIN, ST_XF, ST_XB, ST_IFA, ST_IFB, ST_IBA, ST_IBB, ST_LF, ST_LB = 0, 1, 2, 3, 4, 5, 6, 8, 9
ST_ROWS = 16


GW = GROUP * HEAD_DIM
KEYS = 3 * BLOCK


def _attn_tables(g, bias_ref):
    r = lax.broadcasted_iota(jnp.int32, (GROUP * BLOCK, KEYS), 0)
    kj = lax.broadcasted_iota(jnp.int32, (GROUP * BLOCK, KEYS), 1)
    qi = r & (BLOCK - 1)
    hh = lax.shift_right_logical(r, 7)
    dist = jnp.abs(kj - BLOCK - qi)
    slope = jnp.exp2(-(GROUP * g + hh + 1).astype(F32) * (8.0 / ATTN_HEADS))
    bias_ref[...] = jnp.where(dist <= BLOCK, -slope * dist.astype(F32), NEG_INF)


def _own_lanes(g):
    return lax.shift_right_logical(lax.broadcasted_iota(jnp.int32, (1, LANES), 1), 6) == g


def _mask_keys(x_ref, g, scale, pad_ref, s):
    pad_ref[0:BLOCK, :] = jnp.zeros((BLOCK, LANES), BF16)
    pad_ref[BLOCK + s:2 * BLOCK + s, :] = jnp.zeros((BLOCK, LANES), BF16)
    pad_ref[BLOCK:BLOCK + s, :] = jnp.where(_own_lanes(g), x_ref[...].astype(F32) * scale, 0.0).astype(BF16)


def _lane_block(x, j):
    return x[:, j * LANES:(j + 1) * LANES]


def _stack_heads(x, g):
    assert GROUP == 4 and GW == 2 * LANES
    x1 = pltpu.roll(x, HEAD_DIM, 1)
    keep = _own_lanes(g)
    zero = jnp.zeros((BLOCK, LANES), x.dtype)
    rows = []
    for h in range(GROUP):
        for_g0 = _lane_block(x, h // 2) if h % 2 == 0 else _lane_block(x1, ((h + 1) // 2) % 2)
        for_g1 = _lane_block(x, h // 2) if h % 2 == 1 else _lane_block(x1, h // 2)
        rows.append(jnp.where(keep, jnp.where(g == 0, for_g0, for_g1), zero))
    return jnp.concatenate(rows, axis=0)


def _unstack_heads(x4, g):
    p = [x4[h * BLOCK:(h + 1) * BLOCK, :] for h in range(GROUP)]
    cat = lambda a, b: jnp.concatenate([a, b], axis=1)
    in_place = jnp.where(g == 0, cat(p[0], p[2]), cat(p[1], p[3]))
    one_left = jnp.where(g == 0, cat(p[1], p[3]), cat(p[2], p[0]))
    return in_place + pltpu.roll(one_left, HEAD_DIM, 1)


def _sink_column(sink_ref, g):
    rh = lax.shift_right_logical(lax.broadcasted_iota(jnp.int32, (GROUP * BLOCK, 1), 0), 7)
    col = jnp.zeros((GROUP * BLOCK, 1), F32)
    for h in range(GROUP):
        col = jnp.where(rh == h, sink_ref[GROUP * g + h], col)
    return col


def _attn_probs(qm, k3, bias_ref, sink_col, n, s):
    logits = _dot_nt(qm, k3) + bias_ref[...]
    kpos = n * BLOCK - BLOCK + lax.broadcasted_iota(jnp.int32, (1, KEYS), 1)
    logits = jnp.where((kpos >= 0) & (kpos < s), logits, NEG_INF)
    m = jnp.maximum(jnp.max(logits, axis=1, keepdims=True), sink_col)
    e = jnp.exp(logits - m)
    e_sink = jnp.exp(sink_col - m)
    inv = 1.0 / (jnp.sum(e, axis=1, keepdims=True) + e_sink)
    return e * inv, e_sink * inv


PAIRS_PER_KV = (RET_HEADS // 2) // KV_HEADS


def _mixers_fwd(u, lgf, lgb, gn_gain, sink, b_loc):
    t = u.shape[0]
    s = t // b_loc
    n_chunks = s // CHUNK
    pairs = RET_HEADS // 2
    trips = n_chunks // RET_UNROLL
    blocks_half = (s // BLOCK) // PAIRS_PER_KV
    per_trip = blocks_half // trips
    assert n_chunks % RET_UNROLL == 0 and blocks_half % trips == 0 and PAIRS_PER_KV == 2

    def body(lgf_ref, lgb_ref, sink_ref, q_ref, k_ref, v_ref, g_ref, gain_ref, aq_ref, ak_ref, av_ref,
             r_ref, y_ref, a_ref, st, kpad, vpad, bias):
        pair = pl.program_id(1)
        g, half = lax.shift_right_logical(pair, 1), pair & 1
        tab = _stacked_tables(lgf_ref, lgb_ref, pair)
        is_a = tab["is_a"]

        @pl.when(half == 0)
        def _():
            _attn_tables(g, bias)
            _mask_keys(ak_ref, g, Q_SCALE, kpad, s)
            _mask_keys(av_ref, g, 1.0, vpad, s)

        def kv_body(n, _):
            k8 = _chunk(k_ref, n).astype(F32) * Q_SCALE
            st[n] = jnp.where(tab["diag2"], _dot_tn(_both_ways(k8, tab["kdec"]), _chunk(v_ref, n)), 0.0)
            return 0

        _chunk_loop(n_chunks, kv_body, 0)
        _scan_states(n_chunks, st, FWD_ROWS, tab["lam_f"], BWD_ROWS, tab["lam_b"])
        sink_col = _sink_column(sink_ref, g)

        def retention_chunk(n):
            q = _chunk(q_ref, n)
            k8 = (_chunk(k_ref, n).astype(F32) * Q_SCALE).astype(BF16)
            v = _chunk(v_ref, n)
            p2 = (_dot_nt(_stack_pair(is_a, q), k8) * tab["d2"]).astype(BF16)
            y = _unstack_pair(is_a, _dot_nn(p2, v))
            y = y + _dot_nn(_both_ways(q.astype(F32), tab["qdec"]), st[n].astype(BF16))
            rows = pl.ds(pl.multiple_of(n * CHUNK, CHUNK), CHUNK)
            y_ref[rows, :] = y
            mu = _group_sum(is_a, y) * (1.0 / HEAD_DIM)
            dlt = y - mu
            var = _group_sum(is_a, dlt * dlt) * (1.0 / HEAD_DIM)
            xhat = dlt * lax.rsqrt(var + GN_EPS)
            gate = _chunk(g_ref, n).astype(F32)
            r_ref[rows, :] = (xhat * gain_ref[...] * gate * _sigmoid(gate)).astype(BF16)

        def attention_block(blk):
            n = half * blocks_half + blk
            rows = pl.ds(pl.multiple_of(blk * BLOCK, BLOCK), BLOCK)
            keys = pl.ds(pl.multiple_of(n * BLOCK, BLOCK), KEYS)
            p, _ = _attn_probs(_stack_heads(aq_ref[rows, :], g), kpad[keys, :], bias, sink_col, n, s)
            a_ref[rows, :] = _unstack_heads(_dot_nn(p.astype(BF16), vpad[keys, :]), g).astype(BF16)

        def trip(i, _):
            for j in range(max(RET_UNROLL, per_trip)):
                if j < RET_UNROLL:
                    retention_chunk(i * RET_UNROLL + j)
                if j < per_trip:
                    attention_block(i * per_trip + j)
            return 0

        lax.fori_loop(0, trips, trip, 0)

    lane_blk = lambda c0: _seq_spec(s, c0 // LANES)
    half_rows = blocks_half * BLOCK
    aq_spec = pl.BlockSpec((half_rows, GW), lambda b, h: (b * PAIRS_PER_KV + (h & 1), C_AQ // GW + h // 2))
    a_spec = pl.BlockSpec((half_rows, GW), lambda b, h: (b * PAIRS_PER_KV + (h & 1), h // 2))
    kv_spec = lambda c0: pl.BlockSpec((s, LANES), lambda b, h: (b, c0 // LANES))
    pad = pltpu.VMEM((s + 2 * BLOCK, LANES), BF16)
    return pl.pallas_call(
        body, name="mixers_fwd", grid=(b_loc, pairs),
        in_specs=[_smem_spec(), _smem_spec(), _smem_spec(), lane_blk(C_RQ), lane_blk(C_RK), lane_blk(C_RV),
                  lane_blk(C_RG), pl.BlockSpec((1, LANES), lambda b, h: (0, h)), aq_spec, kv_spec(C_AK), kv_spec(C_AV)],
        out_specs=[_seq_spec(s, 0), _seq_spec(s, 0), a_spec],
        out_shape=[SDS((t, RET_W), BF16), SDS((t, RET_W), F32), SDS((t, ATT_W), BF16)],
        scratch_shapes=[pltpu.VMEM((n_chunks, 2 * LANES, LANES), F32), pad, pad,
                        pltpu.VMEM((GROUP * BLOCK, KEYS), F32)],
        compiler_params=_params(("arbitrary", "arbitrary")),
    )(lgf, lgb, sink, u, u, u, u, gn_gain, u, u, u)


def _mixers_bwd(u, y_pre, dr, da, lgf, lgb, gn_gain, sink, b_loc, after=()):
    t = u.shape[0]
    s = t // b_loc
    n_chunks = s // CHUNK
    pairs = RET_HEADS // 2
    trips = n_chunks // RET_UNROLL
    blocks_half = (s // BLOCK) // PAIRS_PER_KV
    per_trip = blocks_half // trips
    assert n_chunks % RET_UNROLL == 0 and blocks_half % trips == 0 and PAIRS_PER_KV == 2

    def body(lgf_ref, lgb_ref, sink_ref, q_ref, k_ref, v_ref, g_ref, y_ref, dr_ref, gain_ref,
             aq_ref, ak_ref, av_ref, do_ref,
             dq_ref, dk_ref, dv_ref, dg_ref, st_ref, daq_ref, dak_ref, dav_ref, dsink_ref,
             st, gr, dy_s, kpad, vpad, bias, dk_acc, dv_acc):
        pair = pl.program_id(1)
        g, half = lax.shift_right_logical(pair, 1), pair & 1
        tab = _stacked_tables(lgf_ref, lgb_ref, pair)
        is_a = tab["is_a"]
        gain = gain_ref[...]

        @pl.when(half == 0)
        def _():
            _attn_tables(g, bias)
            _mask_keys(ak_ref, g, Q_SCALE, kpad, s)
            _mask_keys(av_ref, g, 1.0, vpad, s)
            dsink_ref[...] = jnp.zeros_like(dsink_ref)

        @pl.when(pair == 0)
        def _():
            dk_acc[...] = jnp.zeros_like(dk_acc)
            dv_acc[...] = jnp.zeros_like(dv_acc)

        def norm_body(n, dgain):
            rows = pl.ds(pl.multiple_of(n * CHUNK, CHUNK), CHUNK)
            y = y_ref[rows, :]
            mu = _group_sum(is_a, y) * (1.0 / HEAD_DIM)
            dlt = y - mu
            var = _group_sum(is_a, dlt * dlt) * (1.0 / HEAD_DIM)
            rstd = lax.rsqrt(var + GN_EPS)
            xhat = dlt * rstd
            gate = g_ref[rows, :].astype(F32)
            sg = _sigmoid(gate)
            silu = gate * sg
            d_out = dr_ref[rows, :].astype(F32)
            dg_ref[rows, :] = (d_out * xhat * gain * (sg * (1.0 + gate * (1.0 - sg)))).astype(BF16)
            dxh = d_out * gain * silu
            m1 = _group_sum(is_a, dxh) * (1.0 / HEAD_DIM)
            m2 = _group_sum(is_a, dxh * xhat) * (1.0 / HEAD_DIM)
            dy = (rstd * (dxh - m1 - xhat * m2)).astype(BF16)
            dy_s[rows, :] = dy
            k8 = k_ref[rows, :].astype(F32) * Q_SCALE
            st[n] = jnp.where(tab["diag2"], _dot_tn(_both_ways(k8, tab["kdec"]), v_ref[rows, :]), 0.0)
            qf = q_ref[rows, :].astype(F32)
            gr[n] = jnp.where(tab["diag2"], _dot_tn(_both_ways(qf, tab["qdec"]), dy), 0.0)
            return dgain + jnp.sum(d_out * xhat * silu, axis=0, keepdims=True)

        dgain = _chunk_loop(n_chunks, norm_body, jnp.zeros((1, LANES), F32))
        _scan_states(n_chunks, st, FWD_ROWS, tab["lam_f"], BWD_ROWS, tab["lam_b"])
        _scan_states(n_chunks, gr, BWD_ROWS, tab["lam_b"], FWD_ROWS, tab["lam_f"])
        colsum = lambda x: jnp.sum(x, axis=0, keepdims=True)

        def grad_body(n, carry):
            xfb, ifa, ifb, iba, ibb, lf, lb = carry
            rows = pl.ds(pl.multiple_of(n * CHUNK, CHUNK), CHUNK)
            q = q_ref[rows, :]
            qf = q.astype(F32)
            k8f = k_ref[rows, :].astype(F32) * Q_SCALE
            k8 = k8f.astype(BF16)
            v = v_ref[rows, :]
            dy = dy_s[rows, :]
            q2, dy2 = _stack_pair(is_a, q), _stack_pair(is_a, dy)
            sc = _dot_nt(q2, k8)
            dp = _dot_nt(dy2, v)
            a2 = (sc * tab["d2"]).astype(BF16)
            ds2 = (dp * tab["d2"]).astype(BF16)
            dq = _unstack_pair(is_a, _dot_nn(ds2, k8))
            dk = _dot_tn(ds2, q2)
            dv = _dot_tn(a2, dy2)
            prod = sc * dp
            pf, pb = prod * tab["df2"], prod * tab["db2"]
            ifa, ifb = ifa + colsum(pf[0:CHUNK, :]), ifb + colsum(pf[CHUNK:2 * CHUNK, :])
            iba, ibb = iba + colsum(pb[0:CHUNK, :]), ibb + colsum(pb[CHUNK:2 * CHUNK, :])
            states, sgrads = st[n], gr[n]
            sb, gb = states.astype(BF16), sgrads.astype(BF16)
            dqc = _dot_nt(dy, sb) * tab["qdec"]
            dkc = _dot_nt(v, gb) * tab["kdec"]
            dv = dv + _dot_nn(_both_ways(k8f, tab["kdec"]), gb)
            dq_ref[rows, :] = (dq + dqc[:, 0:LANES] + dqc[:, LANES:2 * LANES]).astype(BF16)
            dk_ref[rows, :] = ((dk + dkc[:, 0:LANES] + dkc[:, LANES:2 * LANES]) * Q_SCALE).astype(BF16)
            dv_ref[rows, :] = dv.astype(BF16)
            q2w, k2w = jnp.concatenate([qf, qf], axis=1), jnp.concatenate([k8f, k8f], axis=1)
            xfb = xfb + colsum(tab["qexp"] * q2w * dqc + tab["kexp"] * k2w * dkc)
            prod_s = sgrads * states
            lf, lb = lf + colsum(prod_s[0:LANES, :]), lb + colsum(prod_s[LANES:2 * LANES, :])
            return xfb, ifa, ifb, iba, ibb, lf, lb

        sink_col = _sink_column(sink_ref, g)
        head_row = lax.broadcasted_iota(jnp.int32, dsink_ref.shape, 0)

        def attention_block(blk):
            n = half * blocks_half + blk
            rows = pl.ds(pl.multiple_of(blk * BLOCK, BLOCK), BLOCK)
            keys = pl.ds(pl.multiple_of(n * BLOCK, BLOCK), KEYS)
            qm = _stack_heads(aq_ref[rows, :], g)
            k3, v3 = kpad[keys, :], vpad[keys, :]
            p, p_sink = _attn_probs(qm, k3, bias, sink_col, n, s)
            dom = _stack_heads(do_ref[rows, :], g)
            dp = _dot_nt(dom, v3)
            delta = jnp.sum(p * dp, axis=1, keepdims=True)
            ds_mat = (p * (dp - delta)).astype(BF16)
            daq_ref[rows, :] = _unstack_heads(_dot_nn(ds_mat, k3), g).astype(BF16)
            dk_acc[keys, :] += _dot_tn(ds_mat, qm) * Q_SCALE
            dv_acc[keys, :] += _dot_tn(p.astype(BF16), dom)
            w = p_sink * delta
            upd = jnp.zeros(dsink_ref.shape, F32)
            for h in range(GROUP):
                upd = upd + jnp.where(head_row == h, -jnp.sum(w[h * BLOCK:(h + 1) * BLOCK, :]), 0.0)
            dsink_ref[...] += upd

        def trip(i, carry):
            for j in range(max(RET_UNROLL, per_trip)):
                if j < RET_UNROLL:
                    carry = grad_body(i * RET_UNROLL + j, carry)
                if j < per_trip:
                    attention_block(i * per_trip + j)
            return carry

        z = jnp.zeros((1, LANES), F32)
        init = (jnp.zeros((1, 2 * LANES), F32), z, z, z, z, z, z)
        xfb, ifa, ifb, iba, ibb, lf, lb = lax.fori_loop(0, trips, trip, init)
        st_ref[...] = jnp.zeros_like(st_ref)
        st_ref[ST_GAIN:ST_GAIN + 1, :] = dgain
        st_ref[ST_XF:ST_XF + 1, :] = xfb[:, 0:LANES]
        st_ref[ST_XB:ST_XB + 1, :] = xfb[:, LANES:2 * LANES]
        st_ref[ST_IFA:ST_IFA + 1, :] = ifa
        st_ref[ST_IFB:ST_IFB + 1, :] = ifb
        st_ref[ST_IBA:ST_IBA + 1, :] = iba
        st_ref[ST_IBB:ST_IBB + 1, :] = ibb
        st_ref[ST_LF:ST_LF + 1, :] = lf * (CHUNK * tab["lam_f"])
        st_ref[ST_LB:ST_LB + 1, :] = lb * (CHUNK * tab["lam_b"])

        @pl.when(pair == pairs - 1)
        def _():
            dak_ref[...] = dk_acc[BLOCK:BLOCK + s, :].astype(BF16)
            dav_ref[...] = dv_acc[BLOCK:BLOCK + s, :].astype(BF16)

    lane_blk = lambda c0: _seq_spec(s, c0 // LANES)
    seq0 = _seq_spec(s, 0)
    half_rows = blocks_half * BLOCK
    aq_spec = pl.BlockSpec((half_rows, GW), lambda b, h: (b * PAIRS_PER_KV + (h & 1), C_AQ // GW + h // 2))
    a_spec = pl.BlockSpec((half_rows, GW), lambda b, h: (b * PAIRS_PER_KV + (h & 1), h // 2))
    kv_spec = lambda c0: pl.BlockSpec((s, LANES), lambda b, h: (b, c0 // LANES))
    kv_out = pl.BlockSpec((s, LANES), lambda b, h: (b, 0))
    state = pltpu.VMEM((n_chunks, 2 * LANES, LANES), F32)
    pad = pltpu.VMEM((s + 2 * BLOCK, LANES), BF16)
    acc = pltpu.VMEM((s + 2 * BLOCK, LANES), F32)
    body, lead = _after(after, body)
    return pl.pallas_call(
        body, name="mixers_bwd", grid=(b_loc, pairs),
        in_specs=lead + [_smem_spec(), _smem_spec(), _smem_spec(), lane_blk(C_RQ), lane_blk(C_RK), lane_blk(C_RV),
                         lane_blk(C_RG), seq0, seq0, pl.BlockSpec((1, LANES), lambda b, h: (0, h)),
                         aq_spec, kv_spec(C_AK), kv_spec(C_AV), a_spec],
        out_specs=[seq0] * 4 + [pl.BlockSpec((ST_ROWS, LANES), lambda b, h: (b, h)), a_spec, kv_out, kv_out,
                                pl.BlockSpec((8, LANES), lambda b, h: (b * KV_HEADS + h // 2, 0))],
        out_shape=[SDS((t, RET_W), BF16)] * 4 + [SDS((b_loc * ST_ROWS, RET_W), F32), SDS((t, ATT_W), BF16),
                                                   SDS((t, KV_W), BF16), SDS((t, KV_W), BF16),
                                                   SDS((b_loc * KV_HEADS * 8, LANES), F32)],
        scratch_shapes=[state, state, pltpu.VMEM((s, LANES), BF16), pad, pad,
                        pltpu.VMEM((GROUP * BLOCK, KEYS), F32), acc, acc],
        compiler_params=_params(("arbitrary", "arbitrary")),
    )(*after, lgf, lgb, sink, u, u, u, u, y_pre, dr, gn_gain, u, u, u, da)


def _pack_small(acc2, acc1, ret_stats, dsink, b_loc, d):
    pairs = RET_HEADS // 2

    def body(acc2_ref, acc1_ref, st_ref, dsink_ref, out_ref):
        out_ref[...] = jnp.zeros_like(out_ref)
        out_ref[ROW_LN1G:ROW_LN1G + 1, :] = acc1_ref[0:1, :]
        out_ref[ROW_LN1B:ROW_LN1B + 1, :] = acc1_ref[1:2, :]
        out_ref[ROW_LN2G:ROW_LN2G + 1, :] = acc2_ref[1:2, :]
        out_ref[ROW_LN2B:ROW_LN2B + 1, :] = acc2_ref[2:3, :]
        out_ref[ROW_LOSS:ROW_LOSS + 1, :] = acc2_ref[0:1, :]
        st = st_ref[0:ST_ROWS, :]
        for b in range(1, b_loc):
            st = st + st_ref[b * ST_ROWS:(b + 1) * ST_ROWS, :]
        out_ref[ROW_GN:ROW_GN + 1, 0:RET_W] = st[ST_GAIN:ST_GAIN + 1, :]
        lane = lax.broadcasted_iota(jnp.int32, (1, d), 1)
        misc = jnp.zeros((1, d), F32)
        for pr in range(pairs):
            blk = st[:, pr * LANES:(pr + 1) * LANES]
            half = lax.broadcasted_iota(jnp.int32, (1, LANES), 1) < HEAD_DIM
            for h in range(2):
                sel = half if h == 0 else jnp.logical_not(half)
                cross_f = jnp.sum(jnp.where(sel, blk[ST_XF:ST_XF + 1, :] + blk[ST_LF:ST_LF + 1, :], 0.0))
                cross_b = jnp.sum(jnp.where(sel, blk[ST_XB:ST_XB + 1, :] + blk[ST_LB:ST_LB + 1, :], 0.0))
                intra_f = jnp.sum(blk[ST_IFA + h:ST_IFA + h + 1, :])
                intra_b = jnp.sum(blk[ST_IBA + h:ST_IBA + h + 1, :])
                head = 2 * pr + h
                misc = jnp.where(lane == MISC_DF + head, cross_f + intra_f, misc)
                misc = jnp.where(lane == MISC_DB + head, cross_b + intra_b, misc)
        for g in range(KV_HEADS):
            tot = dsink_ref[g * 8:(g + 1) * 8, :]
            for b in range(1, b_loc):
                tot = tot + dsink_ref[(b * KV_HEADS + g) * 8:(b * KV_HEADS + g + 1) * 8, :]
            for h in range(GROUP):
                misc = jnp.where(lane == MISC_SINK + GROUP * g + h, jnp.sum(tot[h:h + 1, 0:1]), misc)
        out_ref[ROW_MISC:ROW_MISC + 1, :] = misc

    return pl.pallas_call(body, name="pack_small", out_shape=SDS((SMALL_ROWS, d), F32))(acc2, acc1, ret_stats, dsink)


BIG = ("w_in", "w_out", "w_ffn_gate", "w_ffn_up", "w_ffn_down", "w_ple_proj", "w_ple_gate")
TRANSPOSED_OUTSIDE = ("w_in", "w_ffn_gate", "w_ffn_up")
TRANSPOSED_HERE = ("w_ple_proj",)
SMALL = ("ret_decay_fwd", "ret_decay_bwd", "ret_gn_gain", "attn_sink", "ln1_gain", "ln1_bias", "ln2_gain", "ln2_bias")
ORDER = ("w_in", "ret_decay_fwd", "ret_decay_bwd", "ret_gn_gain", "attn_sink", "w_out", "ln1_gain", "ln1_bias",
         "w_ffn_gate", "w_ffn_up", "w_ffn_down", "w_ple_proj", "w_ple_gate", "ln2_gain", "ln2_bias")


GATHER_ORDER = ("w_in", "w_out", "w_ffn_gate", "w_ffn_up", "w_ple_gate", "w_ple_proj", "w_ffn_down")


def _local_step(x2, p2, target2, fetch, publish, small, b_loc, me):
    d = x2.shape[1]
    lgf, lgb = _log_decay(small["ret_decay_fwd"], small["ret_decay_bwd"])
    lgf1, lgb1, sink1 = lgf.reshape(-1), lgb.reshape(-1), small["attn_sink"].reshape(-1)
    (w_in,) = fetch(("w_in",), ())
    u, xb = _in_proj(x2, w_in)
    r, y_pre, a = _mixers_fwd(u, lgf1, lgb1, small["ret_gn_gain"], sink1, b_loc)
    w_out, w_gate, w_up = fetch(("w_out", "w_ffn_gate", "w_ffn_up"), (r, a))
    z1, h1b, dact_dg, dact_du, act = _mix_ln1_ffn_up(
        r, a, x2, w_out, w_gate, w_up, small["ln1_gain"], small["ln1_bias"])
    w_pg, w_pe, w_down = fetch(("w_ple_gate", "w_ple_proj", "w_ffn_down"), (act,))
    dz2, dz2b, dsb, dpleb, dg, dup, acc2 = _ffn_down_ln2_loss(
        act, dact_dg, dact_du, h1b, p2, z1, target2, w_down, w_pg, w_pe,
        small["ln1_gain"], small["ln1_bias"], small["ln2_gain"], small["ln2_bias"])
    own = {}

    def grad(name, parts, rhs, after=()):
        whole, own[name] = _weight_grad("grad_" + name, me, parts, rhs, after)
        return whole

    t2 = publish("ffn", dict(w_ffn_down=grad("w_ffn_down", [act], dz2b),
                             w_ple_proj=grad("w_ple_proj", [dpleb], p2),
                             w_ple_gate=grad("w_ple_gate", [h1b], dsb),
                             w_ffn_gate=grad("w_ffn_gate", [dg], h1b),
                             w_ffn_up=grad("w_ffn_up", [dup], h1b)))
    dz1, dz1b, dr, da, acc1 = _dh1_ln1_bwd(dz2, dg, dup, dsb, z1, w_gate, w_up, w_pg, w_out, small["ln1_gain"], t2)
    t3 = publish("out", dict(w_out=grad("w_out", [r, a], dz1b)))
    dq, dk, dv, dgate, ret_stats, daq, dak, dav, dsink = _mixers_bwd(
        u, y_pre, dr, da, lgf1, lgb1, small["ret_gn_gain"], sink1, b_loc, t3)
    parts = [dq, dk, dv, dgate, daq, dak, dav]
    small_part = _pack_small(acc2, acc1, ret_stats, dsink, b_loc, d)
    t4 = publish("in", dict(w_in=grad("w_in", parts, xb)), small_part)
    grad_x = _in_proj_bwd(dz1, parts, w_in, t4)
    return grad_x, own, small_part


def kernel(x, p, w_in, ret_decay_fwd, ret_decay_bwd, ret_gn_gain, attn_sink, w_out, ln1_gain, ln1_bias, w_ffn_gate, w_ffn_up, w_ffn_down, w_ple_proj, w_ple_gate, ln2_gain, ln2_bias, loss_target, m_w_in, m_ret_decay_fwd, m_ret_decay_bwd, m_ret_gn_gain, m_attn_sink, m_w_out, m_ln1_gain, m_ln1_bias, m_w_ffn_gate, m_w_ffn_up, m_w_ffn_down, m_w_ple_proj, m_w_ple_gate, m_ln2_gain, m_ln2_bias, v_w_in, v_ret_decay_fwd, v_ret_decay_bwd, v_ret_gn_gain, v_attn_sink, v_w_out, v_ln1_gain, v_ln1_bias, v_w_ffn_gate, v_w_ffn_up, v_w_ffn_down, v_w_ple_proj, v_w_ple_gate, v_ln2_gain, v_ln2_bias):
    given = dict(locals())

    def strip(n, a):
        if n not in BIG:
            return a
        return a[0].T if n in TRANSPOSED_OUTSIDE else a[0]

    def restore(n, a):
        if n not in BIG:
            return a
        return (a.T if n in TRANSPOSED_OUTSIDE else a)[None]

    w = {n: strip(n, given[n]) for n in ORDER}
    m = {n: strip(n, given["m_" + n]) for n in ORDER}
    v = {n: strip(n, given["v_" + n]) for n in ORDER}
    b_loc, s, d = x.shape
    x2 = x.reshape(b_loc * s, d)
    p2 = p[0].reshape(b_loc * s, p.shape[-1])
    target2 = loss_target.reshape(b_loc * s, d)

    small = {n: w[n] for n in SMALL}
    me = (4 * lax.axis_index("x") + 2 * lax.axis_index("y") + lax.axis_index("c")).astype(jnp.int32).reshape(1)

    gathered = _prep_shards(me, {n: w[n] for n in BIG})
    gather = _split_copy_start("gather_start", [(gathered[n],) for n in GATHER_ORDER],
                               [_gather_copy_near] + [_gather_copy] * (len(GATHER_ORDER) - 1))

    def fetch(names, after):
        if names == ("w_in",):
            near = _split_copy_wait("gather_wait_w_in_near", gather, [0], list(after))
            passed = _split_copy_start("gather_pass_w_in", near, [_gather_copy_pass])
            return [_split_copy_wait("gather_wait_w_in", passed, [0], [])[0][0]]
        which = [GATHER_ORDER.index(n) for n in names]
        got = _split_copy_wait("gather_wait_" + names[0], gather, which, list(after))
        return [item[0] for item in got]

    scatters = []

    def publish(tag, products, small_sums=None):
        items = [(products[n], lax.empty((N_DEV - 1, products[n].shape[0] // N_DEV, products[n].shape[1]), BF16))
                 for n in products]
        copies = [_scatter_copy] * len(items)
        if small_sums is not None:
            items.append((small_sums, lax.empty((N_DEV - 1,) + small_sums.shape, F32)))
            copies.append(_small_copy)
        started = _split_copy_start("scatter_start_" + tag, items, copies)
        scatters.append((list(products), small_sums is not None, started))
        return (started["token"],)

    grad_x, own, small_part = _local_step(x2, p2, target2, fetch, publish, small, b_loc, me)

    out_g, out_d, out_m, out_v = {}, {}, {}, {}
    after = [grad_x]
    for names, with_small, started in scatters:
        landed = _split_copy_wait("scatter_wait_" + names[0], started, list(range(len(started["items"]))), after)
        if with_small:
            loss, sg, sd, sm, sv = _small_adamw(
                me, small_part, landed[-1][1], small, {n: m[n] for n in SMALL}, {n: v[n] for n in SMALL})
            for dst, src in ((out_g, sg), (out_d, sd), (out_m, sm), (out_v, sv)):
                dst.update(src)
        for n, (_, recv) in zip(names, landed):
            out_g[n], out_d[n], out_m[n], out_v[n] = _reduce_adamw(
                n, own[n], recv, w[n], m[n], v[n], n in TRANSPOSED_HERE)
        after = [out_v[names[-1]]]

    outs = [loss[0, 0], grad_x.reshape(x.shape)]
    for group in (out_g, out_d, out_m, out_v):
        outs += [restore(n, group[n]) for n in ORDER]
    return tuple(outs)
```

```python
import functools

import jax
import jax.numpy as jnp
from jax import lax
from jax.experimental import pallas as pl
from jax.experimental.pallas import tpu as pltpu

F32, BF16 = jnp.float32, jnp.bfloat16
SDS = jax.ShapeDtypeStruct
MESH = pl.DeviceIdType.MESH

N_DEV = 8
HEAD_DIM = 64
RET_HEADS = 8
ATTN_HEADS = 8
KV_HEADS = 2
GROUP = ATTN_HEADS // KV_HEADS
RET_W = RET_HEADS * HEAD_DIM
ATT_W = ATTN_HEADS * HEAD_DIM
KV_W = KV_HEADS * HEAD_DIM
LANES = 128
CHUNK = 128
BLOCK = 128
Q_SCALE = HEAD_DIM ** -0.5
ALPHA = 2.0 ** 0.25
LN_EPS = 1e-5
GN_EPS = 1e-5
NEG_INF = -1e30
C_RQ, C_RK, C_RV, C_RG = 0, RET_W, 2 * RET_W, 3 * RET_W
C_AQ = 4 * RET_W
C_AK = C_AQ + ATT_W
C_AV = C_AK + KV_W
IN_W = C_AV + KV_W

ADAM_LR = 0.001
ADAM_B1 = 0.9
ADAM_B2 = 0.999
ADAM_EPS = 1e-08
ADAM_WD = 0.01
ADAM_STEP = 10

VMEM_LIMIT = 56 * 1024 * 1024
MATMUL_ROWS = 512
EPILOGUE_ROWS = 256
SUB_ROWS = 256
SMALL_ROWS = 16
ROW_LN1G, ROW_LN1B, ROW_LN2G, ROW_LN2B, ROW_LOSS, ROW_GN, ROW_MISC = 0, 1, 2, 3, 4, 5, 6
MISC_DF, MISC_DB, MISC_SINK = 0, 8, 16


def _dot_nn(a, b):
    return lax.dot_general(a, b, (((1,), (0,)), ((), ())), preferred_element_type=F32)


def _dot_nt(a, b):
    return lax.dot_general(a, b, (((1,), (1,)), ((), ())), preferred_element_type=F32)


def _dot_tn(a, b):
    return lax.dot_general(a, b, (((0,), (0,)), ((), ())), preferred_element_type=F32)


def _params(sem=None, vmem=VMEM_LIMIT):
    kw = {"vmem_limit_bytes": vmem}
    if sem is not None:
        kw["dimension_semantics"] = sem
    return pltpu.CompilerParams(**kw)


def _row_tile(t, want=512):
    tm = want
    while t % tm:
        tm //= 2
    return tm


def _sigmoid(x):
    return jax.nn.sigmoid(x)


def _layer_norm_stats(z):
    mu = jnp.mean(z, axis=1, keepdims=True)
    d = z - mu
    var = jnp.mean(d * d, axis=1, keepdims=True)
    rstd = lax.rsqrt(var + LN_EPS)
    return d * rstd, rstd


def _layer_norm_bwd(dxh, xhat, rstd):
    m1 = jnp.mean(dxh, axis=1, keepdims=True)
    m2 = jnp.mean(dxh * xhat, axis=1, keepdims=True)
    return rstd * (dxh - m1 - xhat * m2)


def _prep_shards(me, shards):
    names = list(shards)

    def body(me_ref, *refs):
        for name, src, dst in zip(names, refs[:len(names)], refs[len(names):]):
            val = src[...]
            dst[...] = (val.T if name in TRANSPOSED_HERE else val).astype(BF16)

    shape = lambda n, a: a.shape[::-1] if n in TRANSPOSED_HERE else a.shape
    shapes = [shape(n, shards[n]) for n in names]
    out = pl.pallas_call(
        body, name="prep_shards",
        grid_spec=pltpu.PrefetchScalarGridSpec(
            num_scalar_prefetch=1, grid=(1,),
            in_specs=[pl.BlockSpec(shards[n].shape, lambda i, me_ref: (0, 0)) for n in names],
            out_specs=[pl.BlockSpec(s, lambda i, me_ref: (me_ref[0], 0)) for s in shapes]),
        out_shape=[SDS((N_DEV * s[0], s[1]), BF16) for s in shapes], compiler_params=_params(("arbitrary",)),
    )(me, *[shards[n] for n in names])
    return dict(zip(names, out))


def _mesh_pos():
    return lax.axis_index("x"), lax.axis_index("y"), lax.axis_index("c")


HBM_SPEC = pl.BlockSpec(memory_space=pltpu.HBM)
SEM_SPEC = pl.BlockSpec(memory_space=pltpu.SEMAPHORE)
ANY_SPEC = pl.BlockSpec(memory_space=pl.ANY)
SIDE_EFFECT = pltpu.SideEffectType.DATAFLOW_SIDE_EFFECTING
PEER_SEMS = pltpu.SemaphoreType.DMA((N_DEV - 1,))


def _in_hbm(a):
    return pltpu.with_memory_space_constraint(a, pltpu.HBM)


def _split_copy_start(name, items, copies):
    n = len(items)
    flat = [a for it in items for a in it]
    k = len(flat)

    def body(*refs):
        arr, sems = list(refs[:k]), refs[k:k + 2 * n]
        for i, it in enumerate(items):
            mine = [arr.pop(0) for _ in it]
            for m in range(1, N_DEV):
                cp = copies[i](m, mine, sems[i].at[m - 1], sems[n + i].at[m - 1])
                if cp is not None:
                    cp.start()
        token = refs[-1]
        token[...] = jnp.zeros_like(token)

    res = pl.pallas_call(
        body, name=name,
        out_shape=[PEER_SEMS] * (2 * n) + [pltpu.HBM(a.shape, a.dtype) for a in flat] + [SDS((8, LANES), F32)],
        in_specs=[HBM_SPEC] * k,
        out_specs=[SEM_SPEC] * (2 * n) + [HBM_SPEC] * k + [pl.BlockSpec(memory_space=pltpu.VMEM)],
        input_output_aliases={j: 2 * n + j for j in range(k)},
        compiler_params=pltpu.CompilerParams(has_side_effects=SIDE_EFFECT),
    )(*[_in_hbm(a) for a in flat])
    thru, out_items = list(res[2 * n:2 * n + k]), []
    for it in items:
        out_items.append(tuple(thru.pop(0) for _ in it))
    return dict(send=res[:n], recv=res[n:2 * n], items=out_items, token=res[-1], copies=copies)


def _split_copy_wait(name, started, which, after):
    items = [started["items"][i] for i in which]
    copies = [started["copies"][i] for i in which]
    n = len(items)
    flat = [a for it in items for a in it]
    k = len(flat)

    def body(*refs):
        arr, sems = list(refs[:k]), refs[k:k + 2 * n]
        for i, it in enumerate(items):
            mine = [arr.pop(0) for _ in it]
            for m in range(1, N_DEV):
                cp = copies[i](m, mine, sems[i].at[m - 1], sems[n + i].at[m - 1])
                if cp is not None:
                    cp.wait_send()
                    cp.wait_recv()

    res = pl.pallas_call(
        body, name=name,
        out_shape=[pltpu.HBM(a.shape, a.dtype) for a in flat],
        in_specs=[HBM_SPEC] * k + [SEM_SPEC] * (2 * n) + [ANY_SPEC] * len(after),
        out_specs=[HBM_SPEC] * k,
        input_output_aliases={j: j for j in range(k)},
        compiler_params=pltpu.CompilerParams(has_side_effects=SIDE_EFFECT),
    )(*flat, *[started["send"][i] for i in which], *[started["recv"][i] for i in which], *[_in_hbm(a) for a in after])
    thru, out_items = list(res), []
    for it in items:
        out_items.append(tuple(thru.pop(0) for _ in it))
    return out_items


def _gather_copy(m, refs, send_sem, recv_sem):
    (land_ref,) = refs
    r = land_ref.shape[0] // N_DEV
    mine = land_ref.at[pl.ds(pl.multiple_of(_peer_index(0) * r, 8), r), :]
    return pltpu.make_async_remote_copy(src_ref=mine, dst_ref=mine, send_sem=send_sem, recv_sem=recv_sem,
                                        device_id=_peer(m), device_id_type=MESH)


def _gather_copy_near(m, refs, send_sem, recv_sem):
    return _gather_copy(m, refs, send_sem, recv_sem) if m == 1 or m % 2 == 0 else None


def _gather_copy_pass(m, refs, send_sem, recv_sem):
    if m == 1 or m % 2 == 0:
        return None
    (land_ref,) = refs
    r = land_ref.shape[0] // N_DEV
    block = land_ref.at[pl.ds(pl.multiple_of(_peer_index(m ^ 1) * r, 8), r), :]
    return pltpu.make_async_remote_copy(src_ref=block, dst_ref=block, send_sem=send_sem, recv_sem=recv_sem,
                                        device_id=_peer(1), device_id_type=MESH)


def _small_copy(m, refs, send_sem, recv_sem):
    part_ref, land_ref = refs
    return pltpu.make_async_remote_copy(src_ref=part_ref, dst_ref=land_ref.at[m - 1], send_sem=send_sem,
                                        recv_sem=recv_sem, device_id=_peer(m), device_id_type=MESH)


def _scatter_copy(m, refs, send_sem, recv_sem):
    buf_ref, land_ref = refs
    r = buf_ref.shape[0] // N_DEV
    src = buf_ref.at[pl.ds(pl.multiple_of(_peer_index(m) * r, 8), r), :]
    return pltpu.make_async_remote_copy(src_ref=src, dst_ref=land_ref.at[m - 1], send_sem=send_sem,
                                        recv_sem=recv_sem, device_id=_peer(m), device_id_type=MESH)


def _peer(m):
    x, y, c = _mesh_pos()
    bx, by, bc = (m >> 2) & 1, (m >> 1) & 1, m & 1
    return (x ^ bx if bx else x, y ^ by if by else y, c ^ bc if bc else c)


def _peer_index(m):
    x, y, c = _mesh_pos()
    return (4 * x + 2 * y + c) ^ m


SMALL_PLACE = {
    "ln1_gain": (ROW_LN1G, 0), "ln1_bias": (ROW_LN1B, 0), "ln2_gain": (ROW_LN2G, 0), "ln2_bias": (ROW_LN2B, 0),
    "ret_gn_gain": (ROW_GN, 0), "ret_decay_fwd": (ROW_MISC, MISC_DF), "ret_decay_bwd": (ROW_MISC, MISC_DB),
    "attn_sink": (ROW_MISC, MISC_SINK)}


def _small_adamw(me, part, landed, w, m, v):
    d = part.shape[1]
    names = list(SMALL_PLACE)
    k = len(names)

    def body(*refs):
        me_ref, part_ref, land_ref = refs[:3]
        refs = refs[2:]
        w_refs, m_refs, v_refs = refs[1:1 + k], refs[1 + k:1 + 2 * k], refs[1 + 2 * k:1 + 3 * k]
        outs = refs[1 + 3 * k:1 + 7 * k + 1]
        tot_ref = refs[-1]
        loss_ref, g_refs, dl_refs = outs[0], outs[1:1 + k], outs[1 + k:1 + 2 * k]
        nm_refs, nv_refs = outs[1 + 2 * k:1 + 3 * k], outs[1 + 3 * k:1 + 4 * k]
        tot = jnp.zeros(part_ref.shape, F32)
        for dev in range(N_DEV):
            j = dev ^ me_ref[0]
            tot = tot + jnp.where(j == 0, part_ref[...], land_ref[jnp.maximum(j, 1) - 1])
        tot_ref[...] = tot
        loss_ref[...] = (0.5 / d) * jnp.sum(tot_ref[ROW_LOSS:ROW_LOSS + 1, :], axis=1, keepdims=True)
        for i, name in enumerate(names):
            row, lo = SMALL_PLACE[name]
            wv = w_refs[i][...]
            g = tot_ref[row:row + 1, lo:lo + wv.shape[1]]
            if name.startswith("ret_decay"):
                p2 = jnp.exp2(wv)
                g = g * (-p2 * jnp.log(2.0) / (1.0 - p2))
            g_refs[i][...] = g
            _adamw_store(g, wv, m_refs[i][...], v_refs[i][...], dl_refs[i], nm_refs[i], nv_refs[i])

    shapes = [SDS(w[n].shape, F32) for n in names]
    vm = pl.BlockSpec(memory_space=pltpu.VMEM)
    res = pl.pallas_call(
        body, name="small_adamw", out_shape=[SDS((1, 1), F32)] + shapes * 4,
        in_specs=[_smem_spec()] + [vm] * (2 + 3 * k), out_specs=[vm] * (1 + 4 * k),
        scratch_shapes=[pltpu.VMEM(part.shape, F32)],
    )(me, part, landed, *[w[n] for n in names], *[m[n] for n in names], *[v[n] for n in names])
    groups = [dict(zip(names, res[1 + j * k:1 + (j + 1) * k])) for j in range(4)]
    return (res[0], *groups)


def _adamw_store(g, w, m, v, dl_ref, nm_ref, nv_ref):
    m = ADAM_B1 * m + (1.0 - ADAM_B1) * g
    v = ADAM_B2 * v + (1.0 - ADAM_B2) * (g * g)
    m_hat = m / (1.0 - ADAM_B1 ** ADAM_STEP)
    v_hat = v / (1.0 - ADAM_B2 ** ADAM_STEP)
    dl_ref[...] = -ADAM_LR * (m_hat / (jnp.sqrt(v_hat) + ADAM_EPS) + ADAM_WD * w)
    nm_ref[...] = m
    nv_ref[...] = v


def _reduce_adamw(name, own, recv, w, m, v, transposed):
    rows, n = own.shape
    steps = 1 if transposed or rows % 32 else 4
    rb = rows // steps

    def body(own_ref, recv_ref, w_ref, m_ref, v_ref, g_ref, dl_ref, nm_ref, nv_ref):
        g = own_ref[...]
        for k in range(N_DEV - 1):
            g = g + recv_ref[k].astype(F32)
        if transposed:
            g = g.T
        g_ref[...] = g
        _adamw_store(g, w_ref[...], m_ref[...], v_ref[...], dl_ref, nm_ref, nv_ref)

    blk = pl.BlockSpec(w.shape if transposed else (rb, n), lambda i: (i, 0))
    out = SDS(w.shape, F32)
    return pl.pallas_call(
        body, name="adamw_" + name, grid=(steps,),
        in_specs=[pl.BlockSpec((rb, n), lambda i: (i, 0)), pl.BlockSpec((N_DEV - 1, rb, n), lambda i: (0, i, 0)),
                  blk, blk, blk],
        out_specs=[blk] * 4, out_shape=[out] * 4, compiler_params=_params(("parallel",)),
    )(own, recv, w, m, v)


def _row_spec(tm, width):
    return pl.BlockSpec((tm, width), lambda i: (i, 0))


def _full_spec(shape):
    return pl.BlockSpec(shape, lambda i: (0,) * len(shape))


_acc_spec = _full_spec


def _sub_rows(tm):
    step = min(SUB_ROWS, tm)
    return [(lo, lo + step) for lo in range(0, tm, step)]


def _in_proj(x2, wt_in):
    t, d = x2.shape
    u_w = wt_in.shape[0]
    tm = _row_tile(t, MATMUL_ROWS)

    def body(x_ref, w_ref, u_ref, xb_ref):
        xb = x_ref[...].astype(BF16)
        xb_ref[...] = xb
        u_ref[...] = _dot_nt(xb, w_ref[...]).astype(BF16)

    return pl.pallas_call(
        body, name="in_proj", grid=(t // tm,),
        in_specs=[_row_spec(tm, d), _full_spec(wt_in.shape)],
        out_specs=[_row_spec(tm, u_w), _row_spec(tm, d)],
        out_shape=[SDS((t, u_w), BF16), SDS((t, d), BF16)],
        compiler_params=_params(("parallel",)),
    )(x2, wt_in)


def _col_halves(f):
    n = f // LANES
    k = (n + 1) // 2 * LANES
    return [(0, k), (k, f)] if k < f else [(0, f)]


def _mix_ln1_ffn_up(r, a, x2, w_out, wt_gate, wt_up, g1, b1):
    t, d = x2.shape
    f = wt_gate.shape[0]
    tm = _row_tile(t, EPILOGUE_ROWS)

    def body(r_ref, a_ref, x_ref, wo_ref, wg_ref, wu_ref, g_ref, b_ref, z_ref, hb_ref, dg_ref, du_ref, act_ref):
        mix = _dot_nn(r_ref[...], wo_ref[0:RET_W, :]) + _dot_nn(a_ref[...], wo_ref[RET_W:RET_W + ATT_W, :])
        z = ALPHA * x_ref[...] + mix
        xhat, _ = _layer_norm_stats(z)
        z_ref[...] = z
        h = (xhat * g_ref[...] + b_ref[...]).astype(BF16)
        hb_ref[...] = h
        for lo, hi in _col_halves(f):
            g = _dot_nt(h, wg_ref[lo:hi, :])
            u = _dot_nt(h, wu_ref[lo:hi, :])
            sg = _sigmoid(g)
            silu = g * sg
            dg_ref[:, lo:hi] = (u * (sg * (1.0 + g * (1.0 - sg)))).astype(BF16)
            du_ref[:, lo:hi] = silu.astype(BF16)
            act_ref[:, lo:hi] = (silu * u).astype(BF16)

    wide, narrow = _row_spec(tm, f), _row_spec(tm, d)
    return pl.pallas_call(
        body, name="mix_ln1_ffn_up", grid=(t // tm,),
        in_specs=[_row_spec(tm, RET_W), _row_spec(tm, ATT_W), narrow, _resident_spec(w_out.shape),
                  _resident_spec(wt_gate.shape), _resident_spec(wt_up.shape), _full_spec(g1.shape),
                  _full_spec(b1.shape)],
        out_specs=[narrow, narrow, wide, wide, wide],
        out_shape=[SDS((t, d), F32), SDS((t, d), BF16)] + [SDS((t, f), BF16)] * 3,
        compiler_params=_params(("parallel",)),
    )(r, a, x2, w_out, wt_gate, wt_up, g1, b1)


def _ffn_down_ln2_loss(act, dact_dg, dact_du, h1b, p2, z1, target, w_down, w_pg, wt_pe, g1, b1, g2, b2):
    t, d = z1.shape
    f = act.shape[1]
    pdim = p2.shape[1]
    tm = _row_tile(t, EPILOGUE_ROWS)

    def body(act_ref, fg_ref, fu_ref, hb_ref, p_ref, z1_ref, tgt_ref, wd_ref, wpg_ref, wpe_ref, g1_ref, b1_ref,
             g2_ref, b2_ref, dz_ref, dzb_ref, ds_ref, dple_ref, dg_ref, du_ref, acc_ref):
        @pl.when(pl.program_id(0) == 0)
        def _():
            acc_ref[...] = jnp.zeros_like(acc_ref)

        for lo, hi in _sub_rows(tm):
            xhat1, _ = _layer_norm_stats(z1_ref[lo:hi, :])
            h1 = xhat1 * g1_ref[...] + b1_ref[...]
            ffn = _dot_nn(act_ref[lo:hi, :], wd_ref[...])
            pg = _sigmoid(_dot_nn(hb_ref[lo:hi, :], wpg_ref[...]))
            ple = _dot_nt(p_ref[lo:hi, :].astype(BF16), wpe_ref[...])
            z2 = ALPHA * h1 + ffn + pg * ple
            xhat2, rstd2 = _layer_norm_stats(z2)
            err = xhat2 * g2_ref[...] + b2_ref[...] - tgt_ref[lo:hi, :]
            dy = err * (1.0 / d)
            dz = _layer_norm_bwd(dy * g2_ref[...], xhat2, rstd2)
            dzb = dz.astype(BF16)
            dz_ref[lo:hi, :] = dz
            dzb_ref[lo:hi, :] = dzb
            ds_ref[lo:hi, :] = (dz * ple * pg * (1.0 - pg)).astype(BF16)
            dple_ref[lo:hi, :] = (dz * pg).astype(BF16)
            acc_ref[0:1, :] += jnp.sum(err * err, axis=0, keepdims=True)
            acc_ref[1:2, :] += jnp.sum(dy * xhat2, axis=0, keepdims=True)
            acc_ref[2:3, :] += jnp.sum(dy, axis=0, keepdims=True)
            for c0, c1 in _col_halves(f):
                da = _dot_nt(dzb, wd_ref[c0:c1, :])
                dg_ref[lo:hi, c0:c1] = (da * fg_ref[lo:hi, c0:c1].astype(F32)).astype(BF16)
                du_ref[lo:hi, c0:c1] = (da * fu_ref[lo:hi, c0:c1].astype(F32)).astype(BF16)

    vec = _full_spec(g1.shape)
    wide, narrow = _row_spec(tm, f), _row_spec(tm, d)
    return pl.pallas_call(
        body, name="ffn_down_ln2_loss", grid=(t // tm,),
        in_specs=[wide, wide, wide, narrow, _row_spec(tm, pdim), narrow, narrow,
                  _full_spec(w_down.shape), _full_spec(w_pg.shape), _full_spec(wt_pe.shape), vec, vec, vec, vec],
        out_specs=[narrow] * 4 + [wide, wide, _acc_spec((8, d))],
        out_shape=[SDS((t, d), F32), SDS((t, d), BF16), SDS((t, d), BF16), SDS((t, d), BF16),
                   SDS((t, f), BF16), SDS((t, f), BF16), SDS((8, d), F32)],
        compiler_params=_params(("arbitrary",)),
    )(act, dact_dg, dact_du, h1b, p2, z1, target, w_down, w_pg, wt_pe, g1, b1, g2, b2)


def _after(after, body):
    k = len(after)
    return (lambda *refs: body(*refs[k:])), [ANY_SPEC] * k


def _resident_spec(shape):
    return pl.BlockSpec(shape, lambda i: (0,) * len(shape), pipeline_mode=pl.Buffered(1))


def _dh1_ln1_bwd(dz2, dg, dup, dsb, z1, wt_gate, wt_up, w_pg, w_out, g1, after=()):
    t, d = dz2.shape
    f = dg.shape[1]
    tm = _row_tile(t, EPILOGUE_ROWS)

    def body(dz_ref, dg_ref, du_ref, ds_ref, z1_ref, wg_ref, wu_ref, wpg_ref, wo_ref, g1_ref,
             dz1_ref, dz1b_ref, dr_ref, da_ref, acc_ref):
        @pl.when(pl.program_id(0) == 0)
        def _():
            acc_ref[...] = jnp.zeros_like(acc_ref)

        for lo, hi in _sub_rows(tm):
            dh = (ALPHA * dz_ref[lo:hi, :] + _dot_nn(dg_ref[lo:hi, :], wg_ref[...])
                  + _dot_nn(du_ref[lo:hi, :], wu_ref[...]) + _dot_nt(ds_ref[lo:hi, :], wpg_ref[...]))
            xhat, rstd = _layer_norm_stats(z1_ref[lo:hi, :])
            dz1 = _layer_norm_bwd(dh * g1_ref[...], xhat, rstd)
            dz1b = dz1.astype(BF16)
            dz1_ref[lo:hi, :] = dz1
            dz1b_ref[lo:hi, :] = dz1b
            acc_ref[0:1, :] += jnp.sum(dh * xhat, axis=0, keepdims=True)
            acc_ref[1:2, :] += jnp.sum(dh, axis=0, keepdims=True)
            dr_ref[lo:hi, :] = _dot_nt(dz1b, wo_ref[0:RET_W, :]).astype(BF16)
            da_ref[lo:hi, :] = _dot_nt(dz1b, wo_ref[RET_W:RET_W + ATT_W, :]).astype(BF16)

    body, lead = _after(after, body)
    return pl.pallas_call(
        body, name="dh1_ln1_bwd", grid=(t // tm,),
        in_specs=lead + [_row_spec(tm, d), _row_spec(tm, f), _row_spec(tm, f), _row_spec(tm, d), _row_spec(tm, d),
                         _resident_spec(wt_gate.shape), _resident_spec(wt_up.shape), _resident_spec(w_pg.shape),
                         _resident_spec(w_out.shape), _full_spec(g1.shape)],
        out_specs=[_row_spec(tm, d), _row_spec(tm, d), _row_spec(tm, RET_W), _row_spec(tm, ATT_W), _acc_spec((8, d))],
        out_shape=[SDS((t, d), F32), SDS((t, d), BF16), SDS((t, RET_W), BF16), SDS((t, ATT_W), BF16),
                   SDS((8, d), F32)],
        compiler_params=_params(("arbitrary",)),
    )(*after, dz2, dg, dup, dsb, z1, wt_gate, wt_up, w_pg, w_out, g1)


def _in_proj_bwd(dz1, parts, wt_in, after=()):
    t, d = dz1.shape
    tm = _row_tile(t, MATMUL_ROWS)
    widths = [p.shape[1] for p in parts]

    def body(*refs):
        dz_ref, part_refs, w_ref, dx_ref = refs[0], refs[1:1 + len(parts)], refs[-2], refs[-1]
        acc = ALPHA * dz_ref[...]
        lo = 0
        for p_ref, w in zip(part_refs, widths):
            acc = acc + _dot_nn(p_ref[...], w_ref[lo:lo + w, :])
            lo += w
        dx_ref[...] = acc

    body, lead = _after(after, body)
    return pl.pallas_call(
        body, name="in_proj_bwd", grid=(t // tm,),
        in_specs=lead + [_row_spec(tm, d)] + [_row_spec(tm, w) for w in widths] + [_full_spec(wt_in.shape)],
        out_specs=_row_spec(tm, d), out_shape=SDS((t, d), F32),
        compiler_params=_params(("parallel",)),
    )(*after, dz1, *parts, wt_in)


def _weight_grad(name, me, parts, rhs, after=()):
    t, n = rhs.shape
    widths = [p.shape[1] for p in parts]
    rows = sum(widths)
    own_rows = rows // N_DEV
    tk = _row_tile(t, MATMUL_ROWS)
    step = 256

    def body(*refs):
        me_ref, part_refs, rhs_ref = refs[0], refs[1:1 + len(parts)], refs[1 + len(parts)]
        full_ref, own_ref, acc = refs[-3], refs[-2], refs[-1]
        i = pl.program_id(0)

        @pl.when(i == 0)
        def _():
            acc[...] = jnp.zeros_like(acc)

        b = rhs_ref[...].astype(BF16)
        lo = 0
        for p_ref, w in zip(part_refs, widths):
            for c0 in range(0, w, step):
                c1 = min(c0 + step, w)
                acc[lo + c0:lo + c1, :] += _dot_tn(p_ref[:, c0:c1].astype(BF16), b)
            lo += w

        @pl.when(i == pl.num_programs(0) - 1)
        def _():
            full_ref[...] = acc[...].astype(BF16)
            own_ref[...] = acc[pl.ds(pl.multiple_of(me_ref[0] * own_rows, 8), own_rows), :]

    body, lead = _after(after, body)
    return pl.pallas_call(
        body, name=name, grid=(t // tk,),
        in_specs=lead + [_smem_spec()] + [_row_spec(tk, w) for w in widths] + [_row_spec(tk, n)],
        out_specs=[_full_spec((rows, n)), _full_spec((own_rows, n))],
        out_shape=[SDS((rows, n), BF16), SDS((own_rows, n), F32)],
        scratch_shapes=[pltpu.VMEM((rows, n), F32)],
        compiler_params=_params(("arbitrary",)),
    )(*after, me, *parts, rhs)


def _log_decay(decay_f, decay_b):
    def body(f_ref, b_ref, lf_ref, lb_ref):
        lf_ref[...] = jnp.log1p(-jnp.exp2(f_ref[...]))
        lb_ref[...] = jnp.log1p(-jnp.exp2(b_ref[...]))

    return pl.pallas_call(body, name="log_decay", out_shape=[SDS(decay_f.shape, F32)] * 2)(decay_f, decay_b)


def _chunk(ref, n):
    return ref[pl.ds(pl.multiple_of(n * CHUNK, CHUNK), CHUNK), :]


def _group_sum(is_a, v):
    sa = jnp.sum(jnp.where(is_a, v, 0.0), axis=1, keepdims=True)
    sb = jnp.sum(jnp.where(is_a, 0.0, v), axis=1, keepdims=True)
    return jnp.where(is_a, sa, sb)


def _seq_spec(s, col_block):
    return pl.BlockSpec((s, LANES), lambda b, h: (b, col_block + h))


def _smem_spec():
    return pl.BlockSpec(memory_space=pltpu.SMEM)


RET_UNROLL = 4


def _chunk_loop(n_chunks, body, init):
    u = RET_UNROLL if n_chunks % RET_UNROLL == 0 else 1

    def trip(i, carry):
        for j in range(u):
            carry = body(i * u + j, carry)
        return carry

    return lax.fori_loop(0, n_chunks // u, trip, init)


def _stacked_tables(lgf_ref, lgb_ref, pair):
    lane = lax.broadcasted_iota(jnp.int32, (1, LANES), 1)
    is_a = lane < HEAD_DIM
    lgf = jnp.where(is_a, lgf_ref[2 * pair], lgf_ref[2 * pair + 1])
    lgb = jnp.where(is_a, lgb_ref[2 * pair], lgb_ref[2 * pair + 1])
    row = lax.broadcasted_iota(jnp.int32, (CHUNK, 1), 0).astype(F32)
    kdec_f, qdec_f = jnp.exp(lgf * (CHUNK - 1.0 - row)), jnp.exp(lgf * (row + 1.0))
    kdec_b, qdec_b = jnp.exp(lgb * row), jnp.exp(lgb * (CHUNK - row))
    tab = dict(
        is_a=is_a, row=row, lam_f=jnp.exp(lgf * CHUNK), lam_b=jnp.exp(lgb * CHUNK),
        kdec=jnp.concatenate([kdec_f, kdec_b], axis=1), qdec=jnp.concatenate([qdec_f, qdec_b], axis=1),
        qexp=jnp.concatenate([jnp.broadcast_to(row + 1.0, (CHUNK, LANES)),
                              jnp.broadcast_to(CHUNK - row, (CHUNK, LANES))], axis=1),
        kexp=jnp.concatenate([jnp.broadcast_to(CHUNK - 1.0 - row, (CHUNK, LANES)),
                              jnp.broadcast_to(row, (CHUNK, LANES))], axis=1),
    )
    r = lax.broadcasted_iota(jnp.int32, (2 * LANES, LANES), 0)
    c = lax.broadcasted_iota(jnp.int32, (2 * LANES, LANES), 1)
    tab["diag2"] = ((r & (LANES - 1)) < HEAD_DIM) == (c < HEAD_DIM)
    i2 = lax.broadcasted_iota(jnp.int32, (2 * CHUNK, CHUNK), 0)
    j = lax.broadcasted_iota(jnp.int32, (2 * CHUNK, CHUNK), 1)
    head_b = i2 >= CHUNK
    diff = ((i2 & (CHUNK - 1)) - j).astype(F32)
    up, dn = jnp.maximum(diff, 0.0), jnp.maximum(-diff, 0.0)
    lgf2 = jnp.where(head_b, lgf_ref[2 * pair + 1], lgf_ref[2 * pair])
    lgb2 = jnp.where(head_b, lgb_ref[2 * pair + 1], lgb_ref[2 * pair])
    ef = jnp.where(diff >= 0, jnp.exp(lgf2 * up), 0.0)
    eb = jnp.where(diff <= 0, jnp.exp(lgb2 * dn), 0.0)
    tab["d2"] = ef + eb
    tab["df2"] = ef * up
    tab["db2"] = eb * dn
    return tab


def _stack_pair(is_a, x):
    zero = jnp.zeros_like(x)
    return jnp.concatenate([jnp.where(is_a, x, zero), jnp.where(is_a, zero, x)], axis=0)


def _unstack_pair(is_a, x2):
    return jnp.where(is_a, x2[0:CHUNK, :], x2[CHUNK:2 * CHUNK, :])


def _both_ways(x, dec):
    return (jnp.concatenate([x, x], axis=1) * dec).astype(BF16)


def _scan_states(n_chunks, st, up_rows, up_lam, down_rows, down_lam):
    zero = jnp.zeros((LANES, LANES), F32)

    def up(n, r):
        new = st[n, up_rows, :]
        st[n, up_rows, :] = r
        return r * up_lam + new

    def down(s, r):
        n = n_chunks - 1 - s
        new = st[n, down_rows, :]
        st[n, down_rows, :] = r
        return r * down_lam + new

    lax.fori_loop(0, n_chunks, up, zero)
    lax.fori_loop(0, n_chunks, down, zero)


FWD_ROWS, BWD_ROWS = pl.ds(0, LANES), pl.ds(LANES, LANES)


ST_GAIN, ST_XF, ST_XB, ST_IFA, ST_IFB, ST_IBA, ST_IBB, ST_LF, ST_LB = 0, 1, 2, 3, 4, 5, 6, 8, 9
ST_ROWS = 16


GW = GROUP * HEAD_DIM
KEYS = 3 * BLOCK


def _attn_tables(g, bias_ref):
    r = lax.broadcasted_iota(jnp.int32, (GROUP * BLOCK, KEYS), 0)
    kj = lax.broadcasted_iota(jnp.int32, (GROUP * BLOCK, KEYS), 1)
    qi = r & (BLOCK - 1)
    hh = lax.shift_right_logical(r, 7)
    dist = jnp.abs(kj - BLOCK - qi)
    slope = jnp.exp2(-(GROUP * g + hh + 1).astype(F32) * (8.0 / ATTN_HEADS))
    bias_ref[...] = jnp.where(dist <= BLOCK, -slope * dist.astype(F32), NEG_INF)


def _own_lanes(g):
    return lax.shift_right_logical(lax.broadcasted_iota(jnp.int32, (1, LANES), 1), 6) == g


def _mask_keys(x_ref, g, scale, pad_ref, s):
    pad_ref[0:BLOCK, :] = jnp.zeros((BLOCK, LANES), BF16)
    pad_ref[BLOCK + s:2 * BLOCK + s, :] = jnp.zeros((BLOCK, LANES), BF16)
    pad_ref[BLOCK:BLOCK + s, :] = jnp.where(_own_lanes(g), x_ref[...].astype(F32) * scale, 0.0).astype(BF16)


def _lane_block(x, j):
    return x[:, j * LANES:(j + 1) * LANES]


def _stack_heads(x, g):
    assert GROUP == 4 and GW == 2 * LANES
    x1 = pltpu.roll(x, HEAD_DIM, 1)
    keep = _own_lanes(g)
    zero = jnp.zeros((BLOCK, LANES), x.dtype)
    rows = []
    for h in range(GROUP):
        for_g0 = _lane_block(x, h // 2) if h % 2 == 0 else _lane_block(x1, ((h + 1) // 2) % 2)
        for_g1 = _lane_block(x, h // 2) if h % 2 == 1 else _lane_block(x1, h // 2)
        rows.append(jnp.where(keep, jnp.where(g == 0, for_g0, for_g1), zero))
    return jnp.concatenate(rows, axis=0)


def _unstack_heads(x4, g):
    p = [x4[h * BLOCK:(h + 1) * BLOCK, :] for h in range(GROUP)]
    cat = lambda a, b: jnp.concatenate([a, b], axis=1)
    in_place = jnp.where(g == 0, cat(p[0], p[2]), cat(p[1], p[3]))
    one_left = jnp.where(g == 0, cat(p[1], p[3]), cat(p[2], p[0]))
    return in_place + pltpu.roll(one_left, HEAD_DIM, 1)


def _sink_column(sink_ref, g):
    rh = lax.shift_right_logical(lax.broadcasted_iota(jnp.int32, (GROUP * BLOCK, 1), 0), 7)
    col = jnp.zeros((GROUP * BLOCK, 1), F32)
    for h in range(GROUP):
        col = jnp.where(rh == h, sink_ref[GROUP * g + h], col)
    return col


def _attn_probs(qm, k3, bias_ref, sink_col, n, s):
    logits = _dot_nt(qm, k3) + bias_ref[...]
    kpos = n * BLOCK - BLOCK + lax.broadcasted_iota(jnp.int32, (1, KEYS), 1)
    logits = jnp.where((kpos >= 0) & (kpos < s), logits, NEG_INF)
    m = jnp.maximum(jnp.max(logits, axis=1, keepdims=True), sink_col)
    e = jnp.exp(logits - m)
    e_sink = jnp.exp(sink_col - m)
    inv = 1.0 / (jnp.sum(e, axis=1, keepdims=True) + e_sink)
    return e * inv, e_sink * inv


PAIRS_PER_KV = (RET_HEADS // 2) // KV_HEADS


def _mixers_fwd(u, lgf, lgb, gn_gain, sink, b_loc):
    t = u.shape[0]
    s = t // b_loc
    n_chunks = s // CHUNK
    pairs = RET_HEADS // 2
    trips = n_chunks // RET_UNROLL
    blocks_half = (s // BLOCK) // PAIRS_PER_KV
    per_trip = blocks_half // trips
    assert n_chunks % RET_UNROLL == 0 and blocks_half % trips == 0 and PAIRS_PER_KV == 2

    def body(lgf_ref, lgb_ref, sink_ref, q_ref, k_ref, v_ref, g_ref, gain_ref, aq_ref, ak_ref, av_ref,
             r_ref, xhat_ref, rstd_ref, a_ref, st, kpad, vpad, bias):
        pair = pl.program_id(1)
        g, half = lax.shift_right_logical(pair, 1), pair & 1
        tab = _stacked_tables(lgf_ref, lgb_ref, pair)
        is_a = tab["is_a"]

        @pl.when(half == 0)
        def _():
            _attn_tables(g, bias)
            _mask_keys(ak_ref, g, Q_SCALE, kpad, s)
            _mask_keys(av_ref, g, 1.0, vpad, s)

        def kv_body(n, _):
            k8 = _chunk(k_ref, n).astype(F32) * Q_SCALE
            st[n] = jnp.where(tab["diag2"], _dot_tn(_both_ways(k8, tab["kdec"]), _chunk(v_ref, n)), 0.0)
            return 0

        _chunk_loop(n_chunks, kv_body, 0)
        _scan_states(n_chunks, st, FWD_ROWS, tab["lam_f"], BWD_ROWS, tab["lam_b"])
        sink_col = _sink_column(sink_ref, g)

        def retention_chunk(n):
            q = _chunk(q_ref, n)
            k8 = (_chunk(k_ref, n).astype(F32) * Q_SCALE).astype(BF16)
            v = _chunk(v_ref, n)
            p2 = (_dot_nt(_stack_pair(is_a, q), k8) * tab["d2"]).astype(BF16)
            y = _unstack_pair(is_a, _dot_nn(p2, v))
            y = y + _dot_nn(_both_ways(q.astype(F32), tab["qdec"]), st[n].astype(BF16))
            rows = pl.ds(pl.multiple_of(n * CHUNK, CHUNK), CHUNK)
            mu = _group_sum(is_a, y) * (1.0 / HEAD_DIM)
            dlt = y - mu
            var = _group_sum(is_a, dlt * dlt) * (1.0 / HEAD_DIM)
            rstd = lax.rsqrt(var + GN_EPS)
            xhat = dlt * rstd
            xhat_ref[rows, :] = xhat
            rstd_ref[rows, :] = rstd
            gate = _chunk(g_ref, n).astype(F32)
            r_ref[rows, :] = (xhat * gain_ref[...] * gate * _sigmoid(gate)).astype(BF16)

        def attention_block(blk):
            n = half * blocks_half + blk
            rows = pl.ds(pl.multiple_of(blk * BLOCK, BLOCK), BLOCK)
            keys = pl.ds(pl.multiple_of(n * BLOCK, BLOCK), KEYS)
            p, _ = _attn_probs(_stack_heads(aq_ref[rows, :], g), kpad[keys, :], bias, sink_col, n, s)
            a_ref[rows, :] = _unstack_heads(_dot_nn(p.astype(BF16), vpad[keys, :]), g).astype(BF16)

        def trip(i, _):
            for j in range(max(RET_UNROLL, per_trip)):
                if j < RET_UNROLL:
                    retention_chunk(i * RET_UNROLL + j)
                if j < per_trip:
                    attention_block(i * per_trip + j)
            return 0

        lax.fori_loop(0, trips, trip, 0)

    lane_blk = lambda c0: _seq_spec(s, c0 // LANES)
    half_rows = blocks_half * BLOCK
    aq_spec = pl.BlockSpec((half_rows, GW), lambda b, h: (b * PAIRS_PER_KV + (h & 1), C_AQ // GW + h // 2))
    a_spec = pl.BlockSpec((half_rows, GW), lambda b, h: (b * PAIRS_PER_KV + (h & 1), h // 2))
    kv_spec = lambda c0: pl.BlockSpec((s, LANES), lambda b, h: (b, c0 // LANES))
    pad = pltpu.VMEM((s + 2 * BLOCK, LANES), BF16)
    return pl.pallas_call(
        body, name="mixers_fwd", grid=(b_loc, pairs),
        in_specs=[_smem_spec(), _smem_spec(), _smem_spec(), lane_blk(C_RQ), lane_blk(C_RK), lane_blk(C_RV),
                  lane_blk(C_RG), pl.BlockSpec((1, LANES), lambda b, h: (0, h)), aq_spec, kv_spec(C_AK), kv_spec(C_AV)],
        out_specs=[_seq_spec(s, 0), _seq_spec(s, 0), _seq_spec(s, 0), a_spec],
        out_shape=[SDS((t, RET_W), BF16), SDS((t, RET_W), F32), SDS((t, RET_W), F32), SDS((t, ATT_W), BF16)],
        scratch_shapes=[pltpu.VMEM((n_chunks, 2 * LANES, LANES), F32), pad, pad,
                        pltpu.VMEM((GROUP * BLOCK, KEYS), F32)],
        compiler_params=_params(("arbitrary", "arbitrary")),
    )(lgf, lgb, sink, u, u, u, u, gn_gain, u, u, u)


def _mixers_bwd(u, xhat, rstd, dr, da, lgf, lgb, gn_gain, sink, b_loc, after=()):
    t = u.shape[0]
    s = t // b_loc
    n_chunks = s // CHUNK
    pairs = RET_HEADS // 2
    trips = n_chunks // RET_UNROLL
    blocks_half = (s // BLOCK) // PAIRS_PER_KV
    per_trip = blocks_half // trips
    assert n_chunks % RET_UNROLL == 0 and blocks_half % trips == 0 and PAIRS_PER_KV == 2

    def body(lgf_ref, lgb_ref, sink_ref, q_ref, k_ref, v_ref, g_ref, xhat_ref, rstd_ref, dr_ref, gain_ref,
             aq_ref, ak_ref, av_ref, do_ref,
             dq_ref, dk_ref, dv_ref, dg_ref, st_ref, daq_ref, dak_ref, dav_ref, dsink_ref,
             st, gr, dy_s, kpad, vpad, bias, dk_acc, dv_acc):
        pair = pl.program_id(1)
        g, half = lax.shift_right_logical(pair, 1), pair & 1
        tab = _stacked_tables(lgf_ref, lgb_ref, pair)
        is_a = tab["is_a"]
        gain = gain_ref[...]

        @pl.when(half == 0)
        def _():
            _attn_tables(g, bias)
            _mask_keys(ak_ref, g, Q_SCALE, kpad, s)
            _mask_keys(av_ref, g, 1.0, vpad, s)
            dsink_ref[...] = jnp.zeros_like(dsink_ref)

        @pl.when(pair == 0)
        def _():
            dk_acc[...] = jnp.zeros_like(dk_acc)
            dv_acc[...] = jnp.zeros_like(dv_acc)

        def norm_body(n, dgain):
            rows = pl.ds(pl.multiple_of(n * CHUNK, CHUNK), CHUNK)
            xhat, rstd = xhat_ref[rows, :], rstd_ref[rows, :]
            gate = g_ref[rows, :].astype(F32)
            sg = _sigmoid(gate)
            silu = gate * sg
            d_out = dr_ref[rows, :].astype(F32)
            dg_ref[rows, :] = (d_out * xhat * gain * (sg * (1.0 + gate * (1.0 - sg)))).astype(BF16)
            dxh = d_out * gain * silu
            m1 = _group_sum(is_a, dxh) * (1.0 / HEAD_DIM)
            m2 = _group_sum(is_a, dxh * xhat) * (1.0 / HEAD_DIM)
            dy = (rstd * (dxh - m1 - xhat * m2)).astype(BF16)
            dy_s[rows, :] = dy
            k8 = k_ref[rows, :].astype(F32) * Q_SCALE
            st[n] = jnp.where(tab["diag2"], _dot_tn(_both_ways(k8, tab["kdec"]), v_ref[rows, :]), 0.0)
            qf = q_ref[rows, :].astype(F32)
            gr[n] = jnp.where(tab["diag2"], _dot_tn(_both_ways(qf, tab["qdec"]), dy), 0.0)
            return dgain + jnp.sum(d_out * xhat * silu, axis=0, keepdims=True)

        colsum = lambda x: jnp.sum(x, axis=0, keepdims=True)

        def grad_body(n, carry):
            xfb, ifa, ifb, iba, ibb, lf, lb = carry
            rows = pl.ds(pl.multiple_of(n * CHUNK, CHUNK), CHUNK)
            q = q_ref[rows, :]
            qf = q.astype(F32)
            k8f = k_ref[rows, :].astype(F32) * Q_SCALE
            k8 = k8f.astype(BF16)
            v = v_ref[rows, :]
            dy = dy_s[rows, :]
            q2, dy2 = _stack_pair(is_a, q), _stack_pair(is_a, dy)
            sc = _dot_nt(q2, k8)
            dp = _dot_nt(dy2, v)
            a2 = (sc * tab["d2"]).astype(BF16)
            ds2 = (dp * tab["d2"]).astype(BF16)
            dq = _unstack_pair(is_a, _dot_nn(ds2, k8))
            dk = _dot_tn(ds2, q2)
            dv = _dot_tn(a2, dy2)
            prod = sc * dp
            pf, pb = prod * tab["df2"], prod * tab["db2"]
            ifa, ifb = ifa + colsum(pf[0:CHUNK, :]), ifb + colsum(pf[CHUNK:2 * CHUNK, :])
            iba, ibb = iba + colsum(pb[0:CHUNK, :]), ibb + colsum(pb[CHUNK:2 * CHUNK, :])
            states, sgrads = st[n], gr[n]
            sb, gb = states.astype(BF16), sgrads.astype(BF16)
            dqc = _dot_nt(dy, sb) * tab["qdec"]
            dkc = _dot_nt(v, gb) * tab["kdec"]
            dv = dv + _dot_nn(_both_ways(k8f, tab["kdec"]), gb)
            dq_ref[rows, :] = (dq + dqc[:, 0:LANES] + dqc[:, LANES:2 * LANES]).astype(BF16)
            dk_ref[rows, :] = ((dk + dkc[:, 0:LANES] + dkc[:, LANES:2 * LANES]) * Q_SCALE).astype(BF16)
            dv_ref[rows, :] = dv.astype(BF16)
            q2w, k2w = jnp.concatenate([qf, qf], axis=1), jnp.concatenate([k8f, k8f], axis=1)
            xfb = xfb + colsum(tab["qexp"] * q2w * dqc + tab["kexp"] * k2w * dkc)
            prod_s = sgrads * states
            lf, lb = lf + colsum(prod_s[0:LANES, :]), lb + colsum(prod_s[LANES:2 * LANES, :])
            return xfb, ifa, ifb, iba, ibb, lf, lb

        sink_col = _sink_column(sink_ref, g)
        head_row = lax.broadcasted_iota(jnp.int32, dsink_ref.shape, 0)

        def attention_block(blk):
            n = half * blocks_half + blk
            rows = pl.ds(pl.multiple_of(blk * BLOCK, BLOCK), BLOCK)
            keys = pl.ds(pl.multiple_of(n * BLOCK, BLOCK), KEYS)
            qm = _stack_heads(aq_ref[rows, :], g)
            k3, v3 = kpad[keys, :], vpad[keys, :]
            p, p_sink = _attn_probs(qm, k3, bias, sink_col, n, s)
            dom = _stack_heads(do_ref[rows, :], g)
            dp = _dot_nt(dom, v3)
            delta = jnp.sum(p * dp, axis=1, keepdims=True)
            ds_mat = (p * (dp - delta)).astype(BF16)
            daq_ref[rows, :] = _unstack_heads(_dot_nn(ds_mat, k3), g).astype(BF16)
            dk_acc[keys, :] += _dot_tn(ds_mat, qm) * Q_SCALE
            dv_acc[keys, :] += _dot_tn(p.astype(BF16), dom)
            w = p_sink * delta
            upd = jnp.zeros(dsink_ref.shape, F32)
            for h in range(GROUP):
                upd = upd + jnp.where(head_row == h, -jnp.sum(w[h * BLOCK:(h + 1) * BLOCK, :]), 0.0)
            dsink_ref[...] += upd

        dgain = _chunk_loop(n_chunks, norm_body, jnp.zeros((1, LANES), F32))
        _scan_states(n_chunks, st, FWD_ROWS, tab["lam_f"], BWD_ROWS, tab["lam_b"])
        _scan_states(n_chunks, gr, BWD_ROWS, tab["lam_b"], FWD_ROWS, tab["lam_f"])

        def trip(i, carry):
            for j in range(max(RET_UNROLL, per_trip)):
                if j < RET_UNROLL:
                    carry = grad_body(i * RET_UNROLL + j, carry)
                if j < per_trip:
                    attention_block(i * per_trip + j)
            return carry

        z = jnp.zeros((1, LANES), F32)
        init = (jnp.zeros((1, 2 * LANES), F32), z, z, z, z, z, z)
        xfb, ifa, ifb, iba, ibb, lf, lb = lax.fori_loop(0, trips, trip, init)
        st_ref[...] = jnp.zeros_like(st_ref)
        st_ref[ST_GAIN:ST_GAIN + 1, :] = dgain
        st_ref[ST_XF:ST_XF + 1, :] = xfb[:, 0:LANES]
        st_ref[ST_XB:ST_XB + 1, :] = xfb[:, LANES:2 * LANES]
        st_ref[ST_IFA:ST_IFA + 1, :] = ifa
        st_ref[ST_IFB:ST_IFB + 1, :] = ifb
        st_ref[ST_IBA:ST_IBA + 1, :] = iba
        st_ref[ST_IBB:ST_IBB + 1, :] = ibb
        st_ref[ST_LF:ST_LF + 1, :] = lf * (CHUNK * tab["lam_f"])
        st_ref[ST_LB:ST_LB + 1, :] = lb * (CHUNK * tab["lam_b"])

        @pl.when(pair == pairs - 1)
        def _():
            dak_ref[...] = dk_acc[BLOCK:BLOCK + s, :].astype(BF16)
            dav_ref[...] = dv_acc[BLOCK:BLOCK + s, :].astype(BF16)

    lane_blk = lambda c0: _seq_spec(s, c0 // LANES)
    seq0 = _seq_spec(s, 0)
    half_rows = blocks_half * BLOCK
    aq_spec = pl.BlockSpec((half_rows, GW), lambda b, h: (b * PAIRS_PER_KV + (h & 1), C_AQ // GW + h // 2))
    a_spec = pl.BlockSpec((half_rows, GW), lambda b, h: (b * PAIRS_PER_KV + (h & 1), h // 2))
    kv_spec = lambda c0: pl.BlockSpec((s, LANES), lambda b, h: (b, c0 // LANES))
    kv_out = pl.BlockSpec((s, LANES), lambda b, h: (b, 0))
    state = pltpu.VMEM((n_chunks, 2 * LANES, LANES), F32)
    pad = pltpu.VMEM((s + 2 * BLOCK, LANES), BF16)
    acc = pltpu.VMEM((s + 2 * BLOCK, LANES), F32)
    body, lead = _after(after, body)
    return pl.pallas_call(
        body, name="mixers_bwd", grid=(b_loc, pairs),
        in_specs=lead + [_smem_spec(), _smem_spec(), _smem_spec(), lane_blk(C_RQ), lane_blk(C_RK), lane_blk(C_RV),
                         lane_blk(C_RG), seq0, seq0, seq0, pl.BlockSpec((1, LANES), lambda b, h: (0, h)),
                         aq_spec, kv_spec(C_AK), kv_spec(C_AV), a_spec],
        out_specs=[seq0] * 4 + [pl.BlockSpec((ST_ROWS, LANES), lambda b, h: (b, h)), a_spec, kv_out, kv_out,
                                pl.BlockSpec((8, LANES), lambda b, h: (b * KV_HEADS + h // 2, 0))],
        out_shape=[SDS((t, RET_W), BF16)] * 4 + [SDS((b_loc * ST_ROWS, RET_W), F32), SDS((t, ATT_W), BF16),
                                                   SDS((t, KV_W), BF16), SDS((t, KV_W), BF16),
                                                   SDS((b_loc * KV_HEADS * 8, LANES), F32)],
        scratch_shapes=[state, state, pltpu.VMEM((s, LANES), BF16), pad, pad,
                        pltpu.VMEM((GROUP * BLOCK, KEYS), F32), acc, acc],
        compiler_params=_params(("arbitrary", "arbitrary")),
    )(*after, lgf, lgb, sink, u, u, u, u, xhat, rstd, dr, gn_gain, u, u, u, da)


def _pack_small(acc2, acc1, ret_stats, dsink, b_loc, d):
    pairs = RET_HEADS // 2

    def body(acc2_ref, acc1_ref, st_ref, dsink_ref, out_ref):
        out_ref[...] = jnp.zeros_like(out_ref)
        out_ref[ROW_LN1G:ROW_LN1G + 1, :] = acc1_ref[0:1, :]
        out_ref[ROW_LN1B:ROW_LN1B + 1, :] = acc1_ref[1:2, :]
        out_ref[ROW_LN2G:ROW_LN2G + 1, :] = acc2_ref[1:2, :]
        out_ref[ROW_LN2B:ROW_LN2B + 1, :] = acc2_ref[2:3, :]
        out_ref[ROW_LOSS:ROW_LOSS + 1, :] = acc2_ref[0:1, :]
        st = st_ref[0:ST_ROWS, :]
        for b in range(1, b_loc):
            st = st + st_ref[b * ST_ROWS:(b + 1) * ST_ROWS, :]
        out_ref[ROW_GN:ROW_GN + 1, 0:RET_W] = st[ST_GAIN:ST_GAIN + 1, :]
        lane = lax.broadcasted_iota(jnp.int32, (1, d), 1)
        misc = jnp.zeros((1, d), F32)
        for pr in range(pairs):
            blk = st[:, pr * LANES:(pr + 1) * LANES]
            half = lax.broadcasted_iota(jnp.int32, (1, LANES), 1) < HEAD_DIM
            for h in range(2):
                sel = half if h == 0 else jnp.logical_not(half)
                cross_f = jnp.sum(jnp.where(sel, blk[ST_XF:ST_XF + 1, :] + blk[ST_LF:ST_LF + 1, :], 0.0))
                cross_b = jnp.sum(jnp.where(sel, blk[ST_XB:ST_XB + 1, :] + blk[ST_LB:ST_LB + 1, :], 0.0))
                intra_f = jnp.sum(blk[ST_IFA + h:ST_IFA + h + 1, :])
                intra_b = jnp.sum(blk[ST_IBA + h:ST_IBA + h + 1, :])
                head = 2 * pr + h
                misc = jnp.where(lane == MISC_DF + head, cross_f + intra_f, misc)
                misc = jnp.where(lane == MISC_DB + head, cross_b + intra_b, misc)
        for g in range(KV_HEADS):
            tot = dsink_ref[g * 8:(g + 1) * 8, :]
            for b in range(1, b_loc):
                tot = tot + dsink_ref[(b * KV_HEADS + g) * 8:(b * KV_HEADS + g + 1) * 8, :]
            for h in range(GROUP):
                misc = jnp.where(lane == MISC_SINK + GROUP * g + h, jnp.sum(tot[h:h + 1, 0:1]), misc)
        out_ref[ROW_MISC:ROW_MISC + 1, :] = misc

    return pl.pallas_call(body, name="pack_small", out_shape=SDS((SMALL_ROWS, d), F32))(acc2, acc1, ret_stats, dsink)


BIG = ("w_in", "w_out", "w_ffn_gate", "w_ffn_up", "w_ffn_down", "w_ple_proj", "w_ple_gate")
TRANSPOSED_OUTSIDE = ("w_in", "w_ffn_gate", "w_ffn_up")
TRANSPOSED_HERE = ("w_ple_proj",)
SMALL = ("ret_decay_fwd", "ret_decay_bwd", "ret_gn_gain", "attn_sink", "ln1_gain", "ln1_bias", "ln2_gain", "ln2_bias")
ORDER = ("w_in", "ret_decay_fwd", "ret_decay_bwd", "ret_gn_gain", "attn_sink", "w_out", "ln1_gain", "ln1_bias",
         "w_ffn_gate", "w_ffn_up", "w_ffn_down", "w_ple_proj", "w_ple_gate", "ln2_gain", "ln2_bias")


GATHER_ORDER = ("w_in", "w_out", "w_ffn_gate", "w_ffn_up", "w_ple_gate", "w_ple_proj", "w_ffn_down")


def _local_step(x2, p2, target2, fetch, publish, small, b_loc, me):
    d = x2.shape[1]
    lgf, lgb = _log_decay(small["ret_decay_fwd"], small["ret_decay_bwd"])
    lgf1, lgb1, sink1 = lgf.reshape(-1), lgb.reshape(-1), small["attn_sink"].reshape(-1)
    (w_in,) = fetch(("w_in",), ())
    u, xb = _in_proj(x2, w_in)
    r, ret_xhat, ret_rstd, a = _mixers_fwd(u, lgf1, lgb1, small["ret_gn_gain"], sink1, b_loc)
    w_out, w_gate, w_up = fetch(("w_out", "w_ffn_gate", "w_ffn_up"), (r, a))
    z1, h1b, dact_dg, dact_du, act = _mix_ln1_ffn_up(
        r, a, x2, w_out, w_gate, w_up, small["ln1_gain"], small["ln1_bias"])
    w_pg, w_pe, w_down = fetch(("w_ple_gate", "w_ple_proj", "w_ffn_down"), (act,))
    dz2, dz2b, dsb, dpleb, dg, dup, acc2 = _ffn_down_ln2_loss(
        act, dact_dg, dact_du, h1b, p2, z1, target2, w_down, w_pg, w_pe,
        small["ln1_gain"], small["ln1_bias"], small["ln2_gain"], small["ln2_bias"])
    own = {}

    def grad(name, parts, rhs, after=()):
        whole, own[name] = _weight_grad("grad_" + name, me, parts, rhs, after)
        return whole

    t2 = publish("ffn", dict(w_ffn_down=grad("w_ffn_down", [act], dz2b),
                             w_ple_proj=grad("w_ple_proj", [dpleb], p2),
                             w_ple_gate=grad("w_ple_gate", [h1b], dsb),
                             w_ffn_gate=grad("w_ffn_gate", [dg], h1b),
                             w_ffn_up=grad("w_ffn_up", [dup], h1b)))
    dz1, dz1b, dr, da, acc1 = _dh1_ln1_bwd(dz2, dg, dup, dsb, z1, w_gate, w_up, w_pg, w_out, small["ln1_gain"], t2)
    t3 = publish("out", dict(w_out=grad("w_out", [r, a], dz1b)))
    dq, dk, dv, dgate, ret_stats, daq, dak, dav, dsink = _mixers_bwd(
        u, ret_xhat, ret_rstd, dr, da, lgf1, lgb1, small["ret_gn_gain"], sink1, b_loc, t3)
    parts = [dq, dk, dv, dgate, daq, dak, dav]
    small_part = _pack_small(acc2, acc1, ret_stats, dsink, b_loc, d)
    t4 = publish("in", dict(w_in=grad("w_in", parts, xb)), small_part)
    grad_x = _in_proj_bwd(dz1, parts, w_in, t4)
    return grad_x, own, small_part


def kernel(x, p, w_in, ret_decay_fwd, ret_decay_bwd, ret_gn_gain, attn_sink, w_out, ln1_gain, ln1_bias, w_ffn_gate, w_ffn_up, w_ffn_down, w_ple_proj, w_ple_gate, ln2_gain, ln2_bias, loss_target, m_w_in, m_ret_decay_fwd, m_ret_decay_bwd, m_ret_gn_gain, m_attn_sink, m_w_out, m_ln1_gain, m_ln1_bias, m_w_ffn_gate, m_w_ffn_up, m_w_ffn_down, m_w_ple_proj, m_w_ple_gate, m_ln2_gain, m_ln2_bias, v_w_in, v_ret_decay_fwd, v_ret_decay_bwd, v_ret_gn_gain, v_attn_sink, v_w_out, v_ln1_gain, v_ln1_bias, v_w_ffn_gate, v_w_ffn_up, v_w_ffn_down, v_w_ple_proj, v_w_ple_gate, v_ln2_gain, v_ln2_bias):
    given = dict(locals())

    def strip(n, a):
        if n not in BIG:
            return a
        return a[0].T if n in TRANSPOSED_OUTSIDE else a[0]

    def restore(n, a):
        if n not in BIG:
            return a
        return (a.T if n in TRANSPOSED_OUTSIDE else a)[None]

    w = {n: strip(n, given[n]) for n in ORDER}
    m = {n: strip(n, given["m_" + n]) for n in ORDER}
    v = {n: strip(n, given["v_" + n]) for n in ORDER}
    b_loc, s, d = x.shape
    x2 = x.reshape(b_loc * s, d)
    p2 = p[0].reshape(b_loc * s, p.shape[-1])
    target2 = loss_target.reshape(b_loc * s, d)

    small = {n: w[n] for n in SMALL}
    me = (4 * lax.axis_index("x") + 2 * lax.axis_index("y") + lax.axis_index("c")).astype(jnp.int32).reshape(1)

    gathered = _prep_shards(me, {n: w[n] for n in BIG})
    gather = _split_copy_start("gather_start", [(gathered[n],) for n in GATHER_ORDER],
                               [_gather_copy_near] + [_gather_copy] * (len(GATHER_ORDER) - 1))

    def fetch(names, after):
        if names == ("w_in",):
            near = _split_copy_wait("gather_wait_w_in_near", gather, [0], list(after))
            passed = _split_copy_start("gather_pass_w_in", near, [_gather_copy_pass])
            return [_split_copy_wait("gather_wait_w_in", passed, [0], [])[0][0]]
        which = [GATHER_ORDER.index(n) for n in names]
        got = _split_copy_wait("gather_wait_" + names[0], gather, which, list(after))
        return [item[0] for item in got]

    scatters = []

    def publish(tag, products, small_sums=None):
        items = [(products[n], lax.empty((N_DEV - 1, products[n].shape[0] // N_DEV, products[n].shape[1]), BF16))
                 for n in products]
        copies = [_scatter_copy] * len(items)
        if small_sums is not None:
            items.append((small_sums, lax.empty((N_DEV - 1,) + small_sums.shape, F32)))
            copies.append(_small_copy)
        started = _split_copy_start("scatter_start_" + tag, items, copies)
        scatters.append((list(products), small_sums is not None, started))
        return (started["token"],)

    grad_x, own, small_part = _local_step(x2, p2, target2, fetch, publish, small, b_loc, me)

    out_g, out_d, out_m, out_v = {}, {}, {}, {}
    after = [grad_x]
    for names, with_small, started in scatters:
        landed = _split_copy_wait("scatter_wait_" + names[0], started, list(range(len(started["items"]))), after)
        if with_small:
            loss, sg, sd, sm, sv = _small_adamw(
                me, small_part, landed[-1][1], small, {n: m[n] for n in SMALL}, {n: v[n] for n in SMALL})
            for dst, src in ((out_g, sg), (out_d, sd), (out_m, sm), (out_v, sv)):
                dst.update(src)
        for n, (_, recv) in zip(names, landed):
            out_g[n], out_d[n], out_m[n], out_v[n] = _reduce_adamw(
                n, own[n], recv, w[n], m[n], v[n], n in TRANSPOSED_HERE)
        after = [out_v[names[-1]]]

    outs = [loss[0, 0], grad_x.reshape(x.shape)]
    for group in (out_g, out_d, out_m, out_v):
        outs += [restore(n, group[n]) for n in ORDER]
    return tuple(outs)
```

```python
import functools

import jax
import jax.numpy as jnp
from jax import lax
from jax.experimental import pallas as pl
from jax.experimental.pallas import tpu as pltpu

F32, BF16 = jnp.float32, jnp.bfloat16
SDS = jax.ShapeDtypeStruct
MESH = pl.DeviceIdType.MESH

N_DEV = 8
HEAD_DIM = 64
RET_HEADS = 8
ATTN_HEADS = 8
KV_HEADS = 2
GROUP = ATTN_HEADS // KV_HEADS
RET_W = RET_HEADS * HEAD_DIM
ATT_W = ATTN_HEADS * HEAD_DIM
KV_W = KV_HEADS * HEAD_DIM
LANES = 128
CHUNK = 128
BLOCK = 128
Q_SCALE = HEAD_DIM ** -0.5
ALPHA = 2.0 ** 0.25
LN_EPS = 1e-5
GN_EPS = 1e-5
NEG_INF = -1e30
C_RQ, C_RK, C_RV, C_RG = 0, RET_W, 2 * RET_W, 3 * RET_W
C_AQ = 4 * RET_W
C_AK = C_AQ + ATT_W
C_AV = C_AK + KV_W
IN_W = C_AV + KV_W

ADAM_LR = 0.001
ADAM_B1 = 0.9
ADAM_B2 = 0.999
ADAM_EPS = 1e-08
ADAM_WD = 0.01
ADAM_STEP = 10

VMEM_LIMIT = 56 * 1024 * 1024
MATMUL_ROWS = 512
EPILOGUE_ROWS = 256
SUB_ROWS = 256
SMALL_ROWS = 16
ROW_LN1G, ROW_LN1B, ROW_LN2G, ROW_LN2B, ROW_LOSS, ROW_GN, ROW_MISC = 0, 1, 2, 3, 4, 5, 6
MISC_DF, MISC_DB, MISC_SINK = 0, 8, 16


def _dot_nn(a, b):
    return lax.dot_general(a, b, (((1,), (0,)), ((), ())), preferred_element_type=F32)


def _dot_nt(a, b):
    return lax.dot_general(a, b, (((1,), (1,)), ((), ())), preferred_element_type=F32)


def _dot_tn(a, b):
    return lax.dot_general(a, b, (((0,), (0,)), ((), ())), preferred_element_type=F32)


def _params(sem=None, vmem=VMEM_LIMIT):
    kw = {"vmem_limit_bytes": vmem}
    if sem is not None:
        kw["dimension_semantics"] = sem
    return pltpu.CompilerParams(**kw)


def _row_tile(t, want=512):
    tm = want
    while t % tm:
        tm //= 2
    return tm


def _sigmoid(x):
    return jax.nn.sigmoid(x)


def _layer_norm_stats(z):
    mu = jnp.mean(z, axis=1, keepdims=True)
    d = z - mu
    var = jnp.mean(d * d, axis=1, keepdims=True)
    rstd = lax.rsqrt(var + LN_EPS)
    return d * rstd, rstd


def _layer_norm_bwd(dxh, xhat, rstd):
    m1 = jnp.mean(dxh, axis=1, keepdims=True)
    m2 = jnp.mean(dxh * xhat, axis=1, keepdims=True)
    return rstd * (dxh - m1 - xhat * m2)


def _prep_shards(me, shards):
    names = list(shards)

    def body(me_ref, *refs):
        for name, src, dst in zip(names, refs[:len(names)], refs[len(names):]):
            val = src[...]
            dst[...] = (val.T if name in TRANSPOSED_HERE else val).astype(BF16)

    shape = lambda n, a: a.shape[::-1] if n in TRANSPOSED_HERE else a.shape
    shapes = [shape(n, shards[n]) for n in names]
    out = pl.pallas_call(
        body, name="prep_shards",
        grid_spec=pltpu.PrefetchScalarGridSpec(
            num_scalar_prefetch=1, grid=(1,),
            in_specs=[pl.BlockSpec(shards[n].shape, lambda i, me_ref: (0, 0)) for n in names],
            out_specs=[pl.BlockSpec(s, lambda i, me_ref: (me_ref[0], 0)) for s in shapes]),
        out_shape=[SDS((N_DEV * s[0], s[1]), BF16) for s in shapes], compiler_params=_params(("arbitrary",)),
    )(me, *[shards[n] for n in names])
    return dict(zip(names, out))


def _mesh_pos():
    return lax.axis_index("x"), lax.axis_index("y"), lax.axis_index("c")


HBM_SPEC = pl.BlockSpec(memory_space=pltpu.HBM)
SEM_SPEC = pl.BlockSpec(memory_space=pltpu.SEMAPHORE)
ANY_SPEC = pl.BlockSpec(memory_space=pl.ANY)
SIDE_EFFECT = pltpu.SideEffectType.DATAFLOW_SIDE_EFFECTING
PEER_SEMS = pltpu.SemaphoreType.DMA((N_DEV - 1,))


def _in_hbm(a):
    return pltpu.with_memory_space_constraint(a, pltpu.HBM)


def _split_copy_start(name, items, copies):
    n = len(items)
    flat = [a for it in items for a in it]
    k = len(flat)

    def body(*refs):
        arr, sems = list(refs[:k]), refs[k:k + 2 * n]
        for i, it in enumerate(items):
            mine = [arr.pop(0) for _ in it]
            for m in range(1, N_DEV):
                cp = copies[i](m, mine, sems[i].at[m - 1], sems[n + i].at[m - 1])
                if cp is not None:
                    cp.start()
        token = refs[-1]
        token[...] = jnp.zeros_like(token)

    res = pl.pallas_call(
        body, name=name,
        out_shape=[PEER_SEMS] * (2 * n) + [pltpu.HBM(a.shape, a.dtype) for a in flat] + [SDS((8, LANES), F32)],
        in_specs=[HBM_SPEC] * k,
        out_specs=[SEM_SPEC] * (2 * n) + [HBM_SPEC] * k + [pl.BlockSpec(memory_space=pltpu.VMEM)],
        input_output_aliases={j: 2 * n + j for j in range(k)},
        compiler_params=pltpu.CompilerParams(has_side_effects=SIDE_EFFECT),
    )(*[_in_hbm(a) for a in flat])
    thru, out_items = list(res[2 * n:2 * n + k]), []
    for it in items:
        out_items.append(tuple(thru.pop(0) for _ in it))
    return dict(send=res[:n], recv=res[n:2 * n], items=out_items, token=res[-1], copies=copies)


def _split_copy_wait(name, started, which, after):
    items = [started["items"][i] for i in which]
    copies = [started["copies"][i] for i in which]
    n = len(items)
    flat = [a for it in items for a in it]
    k = len(flat)

    def body(*refs):
        arr, sems = list(refs[:k]), refs[k:k + 2 * n]
        for i, it in enumerate(items):
            mine = [arr.pop(0) for _ in it]
            for m in range(1, N_DEV):
                cp = copies[i](m, mine, sems[i].at[m - 1], sems[n + i].at[m - 1])
                if cp is not None:
                    cp.wait_send()
                    cp.wait_recv()

    res = pl.pallas_call(
        body, name=name,
        out_shape=[pltpu.HBM(a.shape, a.dtype) for a in flat],
        in_specs=[HBM_SPEC] * k + [SEM_SPEC] * (2 * n) + [ANY_SPEC] * len(after),
        out_specs=[HBM_SPEC] * k,
        input_output_aliases={j: j for j in range(k)},
        compiler_params=pltpu.CompilerParams(has_side_effects=SIDE_EFFECT),
    )(*flat, *[started["send"][i] for i in which], *[started["recv"][i] for i in which], *[_in_hbm(a) for a in after])
    thru, out_items = list(res), []
    for it in items:
        out_items.append(tuple(thru.pop(0) for _ in it))
    return out_items


def _gather_copy(m, refs, send_sem, recv_sem):
    (land_ref,) = refs
    r = land_ref.shape[0] // N_DEV
    mine = land_ref.at[pl.ds(pl.multiple_of(_peer_index(0) * r, 8), r), :]
    return pltpu.make_async_remote_copy(src_ref=mine, dst_ref=mine, send_sem=send_sem, recv_sem=recv_sem,
                                        device_id=_peer(m), device_id_type=MESH)


def _gather_copy_near(m, refs, send_sem, recv_sem):
    return _gather_copy(m, refs, send_sem, recv_sem) if m == 1 or m % 2 == 0 else None


def _gather_copy_pass(m, refs, send_sem, recv_sem):
    if m == 1 or m % 2 == 0:
        return None
    (land_ref,) = refs
    r = land_ref.shape[0] // N_DEV
    block = land_ref.at[pl.ds(pl.multiple_of(_peer_index(m ^ 1) * r, 8), r), :]
    return pltpu.make_async_remote_copy(src_ref=block, dst_ref=block, send_sem=send_sem, recv_sem=recv_sem,
                                        device_id=_peer(1), device_id_type=MESH)


def _small_copy(m, refs, send_sem, recv_sem):
    part_ref, land_ref = refs
    return pltpu.make_async_remote_copy(src_ref=part_ref, dst_ref=land_ref.at[m - 1], send_sem=send_sem,
                                        recv_sem=recv_sem, device_id=_peer(m), device_id_type=MESH)


def _scatter_copy(m, refs, send_sem, recv_sem):
    buf_ref, land_ref = refs
    r = buf_ref.shape[0] // N_DEV
    src = buf_ref.at[pl.ds(pl.multiple_of(_peer_index(m) * r, 8), r), :]
    return pltpu.make_async_remote_copy(src_ref=src, dst_ref=land_ref.at[m - 1], send_sem=send_sem,
                                        recv_sem=recv_sem, device_id=_peer(m), device_id_type=MESH)


def _peer(m):
    x, y, c = _mesh_pos()
    bx, by, bc = (m >> 2) & 1, (m >> 1) & 1, m & 1
    return (x ^ bx if bx else x, y ^ by if by else y, c ^ bc if bc else c)


def _peer_index(m):
    x, y, c = _mesh_pos()
    return (4 * x + 2 * y + c) ^ m


SMALL_PLACE = {
    "ln1_gain": (ROW_LN1G, 0), "ln1_bias": (ROW_LN1B, 0), "ln2_gain": (ROW_LN2G, 0), "ln2_bias": (ROW_LN2B, 0),
    "ret_gn_gain": (ROW_GN, 0), "ret_decay_fwd": (ROW_MISC, MISC_DF), "ret_decay_bwd": (ROW_MISC, MISC_DB),
    "attn_sink": (ROW_MISC, MISC_SINK)}


def _small_adamw(me, part, landed, w, m, v):
    d = part.shape[1]
    names = list(SMALL_PLACE)
    k = len(names)

    def body(*refs):
        me_ref, part_ref, land_ref = refs[:3]
        refs = refs[2:]
        w_refs, m_refs, v_refs = refs[1:1 + k], refs[1 + k:1 + 2 * k], refs[1 + 2 * k:1 + 3 * k]
        outs = refs[1 + 3 * k:1 + 7 * k + 1]
        tot_ref = refs[-1]
        loss_ref, g_refs, dl_refs = outs[0], outs[1:1 + k], outs[1 + k:1 + 2 * k]
        nm_refs, nv_refs = outs[1 + 2 * k:1 + 3 * k], outs[1 + 3 * k:1 + 4 * k]
        tot = jnp.zeros(part_ref.shape, F32)
        for dev in range(N_DEV):
            j = dev ^ me_ref[0]
            tot = tot + jnp.where(j == 0, part_ref[...], land_ref[jnp.maximum(j, 1) - 1])
        tot_ref[...] = tot
        loss_ref[...] = (0.5 / d) * jnp.sum(tot_ref[ROW_LOSS:ROW_LOSS + 1, :], axis=1, keepdims=True)
        for i, name in enumerate(names):
            row, lo = SMALL_PLACE[name]
            wv = w_refs[i][...]
            g = tot_ref[row:row + 1, lo:lo + wv.shape[1]]
            if name.startswith("ret_decay"):
                p2 = jnp.exp2(wv)
                g = g * (-p2 * jnp.log(2.0) / (1.0 - p2))
            g_refs[i][...] = g
            _adamw_store(g, wv, m_refs[i][...], v_refs[i][...], dl_refs[i], nm_refs[i], nv_refs[i])

    shapes = [SDS(w[n].shape, F32) for n in names]
    vm = pl.BlockSpec(memory_space=pltpu.VMEM)
    res = pl.pallas_call(
        body, name="small_adamw", out_shape=[SDS((1, 1), F32)] + shapes * 4,
        in_specs=[_smem_spec()] + [vm] * (2 + 3 * k), out_specs=[vm] * (1 + 4 * k),
        scratch_shapes=[pltpu.VMEM(part.shape, F32)],
    )(me, part, landed, *[w[n] for n in names], *[m[n] for n in names], *[v[n] for n in names])
    groups = [dict(zip(names, res[1 + j * k:1 + (j + 1) * k])) for j in range(4)]
    return (res[0], *groups)


def _adamw_store(g, w, m, v, dl_ref, nm_ref, nv_ref):
    m = ADAM_B1 * m + (1.0 - ADAM_B1) * g
    v = ADAM_B2 * v + (1.0 - ADAM_B2) * (g * g)
    m_hat = m / (1.0 - ADAM_B1 ** ADAM_STEP)
    v_hat = v / (1.0 - ADAM_B2 ** ADAM_STEP)
    dl_ref[...] = -ADAM_LR * (m_hat / (jnp.sqrt(v_hat) + ADAM_EPS) + ADAM_WD * w)
    nm_ref[...] = m
    nv_ref[...] = v


def _reduce_adamw(name, own, recv, w, m, v, transposed):
    rows, n = own.shape
    steps = 1 if transposed or rows % 32 else 4
    rb = rows // steps

    def body(own_ref, recv_ref, w_ref, m_ref, v_ref, g_ref, dl_ref, nm_ref, nv_ref):
        g = own_ref[...]
        for k in range(N_DEV - 1):
            g = g + recv_ref[k].astype(F32)
        if transposed:
            g = g.T
        g_ref[...] = g
        _adamw_store(g, w_ref[...], m_ref[...], v_ref[...], dl_ref, nm_ref, nv_ref)

    blk = pl.BlockSpec(w.shape if transposed else (rb, n), lambda i: (i, 0))
    out = SDS(w.shape, F32)
    return pl.pallas_call(
        body, name="adamw_" + name, grid=(steps,),
        in_specs=[pl.BlockSpec((rb, n), lambda i: (i, 0)), pl.BlockSpec((N_DEV - 1, rb, n), lambda i: (0, i, 0)),
                  blk, blk, blk],
        out_specs=[blk] * 4, out_shape=[out] * 4, compiler_params=_params(("parallel",)),
    )(own, recv, w, m, v)


def _row_spec(tm, width):
    return pl.BlockSpec((tm, width), lambda i: (i, 0))


def _full_spec(shape):
    return pl.BlockSpec(shape, lambda i: (0,) * len(shape))


_acc_spec = _full_spec


def _sub_rows(tm):
    step = min(SUB_ROWS, tm)
    return [(lo, lo + step) for lo in range(0, tm, step)]


def _in_proj(x2, wt_in):
    t, d = x2.shape
    u_w = wt_in.shape[0]
    tm = _row_tile(t, MATMUL_ROWS)

    def body(x_ref, w_ref, u_ref, xb_ref):
        xb = x_ref[...].astype(BF16)
        xb_ref[...] = xb
        u_ref[...] = _dot_nt(xb, w_ref[...]).astype(BF16)

    return pl.pallas_call(
        body, name="in_proj", grid=(t // tm,),
        in_specs=[_row_spec(tm, d), _full_spec(wt_in.shape)],
        out_specs=[_row_spec(tm, u_w), _row_spec(tm, d)],
        out_shape=[SDS((t, u_w), BF16), SDS((t, d), BF16)],
        compiler_params=_params(("parallel",)),
    )(x2, wt_in)


def _col_halves(f):
    n = f // LANES
    k = (n + 1) // 2 * LANES
    return [(0, k), (k, f)] if k < f else [(0, f)]


def _mix_ln1_ffn_up(r, a, x2, w_out, wt_gate, wt_up, g1, b1):
    t, d = x2.shape
    f = wt_gate.shape[0]
    tm = _row_tile(t, EPILOGUE_ROWS)

    def body(r_ref, a_ref, x_ref, wo_ref, wg_ref, wu_ref, g_ref, b_ref, z_ref, hb_ref, dg_ref, du_ref, act_ref):
        mix = _dot_nn(r_ref[...], wo_ref[0:RET_W, :]) + _dot_nn(a_ref[...], wo_ref[RET_W:RET_W + ATT_W, :])
        z = ALPHA * x_ref[...] + mix
        xhat, _ = _layer_norm_stats(z)
        z_ref[...] = z
        h = (xhat * g_ref[...] + b_ref[...]).astype(BF16)
        hb_ref[...] = h
        for lo, hi in _col_halves(f):
            g = _dot_nt(h, wg_ref[lo:hi, :])
            u = _dot_nt(h, wu_ref[lo:hi, :])
            sg = _sigmoid(g)
            silu = g * sg
            dg_ref[:, lo:hi] = (u * (sg * (1.0 + g * (1.0 - sg)))).astype(BF16)
            du_ref[:, lo:hi] = silu.astype(BF16)
            act_ref[:, lo:hi] = (silu * u).astype(BF16)

    wide, narrow = _row_spec(tm, f), _row_spec(tm, d)
    return pl.pallas_call(
        body, name="mix_ln1_ffn_up", grid=(t // tm,),
        in_specs=[_row_spec(tm, RET_W), _row_spec(tm, ATT_W), narrow, _resident_spec(w_out.shape),
                  _resident_spec(wt_gate.shape), _resident_spec(wt_up.shape), _full_spec(g1.shape),
                  _full_spec(b1.shape)],
        out_specs=[narrow, narrow, wide, wide, wide],
        out_shape=[SDS((t, d), F32), SDS((t, d), BF16)] + [SDS((t, f), BF16)] * 3,
        compiler_params=_params(("parallel",)),
    )(r, a, x2, w_out, wt_gate, wt_up, g1, b1)


def _ffn_down_ln2_loss(act, dact_dg, dact_du, h1b, p2, z1, target, w_down, w_pg, wt_pe, g1, b1, g2, b2):
    t, d = z1.shape
    f = act.shape[1]
    pdim = p2.shape[1]
    tm = _row_tile(t, EPILOGUE_ROWS)

    def body(act_ref, fg_ref, fu_ref, hb_ref, p_ref, z1_ref, tgt_ref, wd_ref, wpg_ref, wpe_ref, g1_ref, b1_ref,
             g2_ref, b2_ref, dz_ref, dzb_ref, ds_ref, dple_ref, dg_ref, du_ref, acc_ref):
        @pl.when(pl.program_id(0) == 0)
        def _():
            acc_ref[...] = jnp.zeros_like(acc_ref)

        for lo, hi in _sub_rows(tm):
            xhat1, _ = _layer_norm_stats(z1_ref[lo:hi, :])
            h1 = xhat1 * g1_ref[...] + b1_ref[...]
            ffn = _dot_nn(act_ref[lo:hi, :], wd_ref[...])
            pg = _sigmoid(_dot_nn(hb_ref[lo:hi, :], wpg_ref[...]))
            ple = _dot_nt(p_ref[lo:hi, :].astype(BF16), wpe_ref[...])
            z2 = ALPHA * h1 + ffn + pg * ple
            xhat2, rstd2 = _layer_norm_stats(z2)
            err = xhat2 * g2_ref[...] + b2_ref[...] - tgt_ref[lo:hi, :]
            dy = err * (1.0 / d)
            dz = _layer_norm_bwd(dy * g2_ref[...], xhat2, rstd2)
            dzb = dz.astype(BF16)
            dz_ref[lo:hi, :] = dz
            dzb_ref[lo:hi, :] = dzb
            ds_ref[lo:hi, :] = (dz * ple * pg * (1.0 - pg)).astype(BF16)
            dple_ref[lo:hi, :] = (dz * pg).astype(BF16)
            acc_ref[0:1, :] += jnp.sum(err * err, axis=0, keepdims=True)
            acc_ref[1:2, :] += jnp.sum(dy * xhat2, axis=0, keepdims=True)
            acc_ref[2:3, :] += jnp.sum(dy, axis=0, keepdims=True)
            for c0, c1 in _col_halves(f):
                da = _dot_nt(dzb, wd_ref[c0:c1, :])
                dg_ref[lo:hi, c0:c1] = (da * fg_ref[lo:hi, c0:c1].astype(F32)).astype(BF16)
                du_ref[lo:hi, c0:c1] = (da * fu_ref[lo:hi, c0:c1].astype(F32)).astype(BF16)

    vec = _full_spec(g1.shape)
    wide, narrow = _row_spec(tm, f), _row_spec(tm, d)
    return pl.pallas_call(
        body, name="ffn_down_ln2_loss", grid=(t // tm,),
        in_specs=[wide, wide, wide, narrow, _row_spec(tm, pdim), narrow, narrow,
                  _full_spec(w_down.shape), _full_spec(w_pg.shape), _full_spec(wt_pe.shape), vec, vec, vec, vec],
        out_specs=[narrow] * 4 + [wide, wide, _acc_spec((8, d))],
        out_shape=[SDS((t, d), F32), SDS((t, d), BF16), SDS((t, d), BF16), SDS((t, d), BF16),
                   SDS((t, f), BF16), SDS((t, f), BF16), SDS((8, d), F32)],
        compiler_params=_params(("arbitrary",)),
    )(act, dact_dg, dact_du, h1b, p2, z1, target, w_down, w_pg, wt_pe, g1, b1, g2, b2)


def _after(after, body):
    k = len(after)
    return (lambda *refs: body(*refs[k:])), [ANY_SPEC] * k


def _resident_spec(shape):
    return pl.BlockSpec(shape, lambda i: (0,) * len(shape), pipeline_mode=pl.Buffered(1))


def _dh1_ln1_bwd(dz2, dg, dup, dsb, z1, wt_gate, wt_up, w_pg, w_out, g1, after=()):
    t, d = dz2.shape
    f = dg.shape[1]
    tm = _row_tile(t, EPILOGUE_ROWS)

    def body(dz_ref, dg_ref, du_ref, ds_ref, z1_ref, wg_ref, wu_ref, wpg_ref, wo_ref, g1_ref,
             dz1_ref, dz1b_ref, dr_ref, da_ref, acc_ref):
        @pl.when(pl.program_id(0) == 0)
        def _():
            acc_ref[...] = jnp.zeros_like(acc_ref)

        for lo, hi in _sub_rows(tm):
            dh = (ALPHA * dz_ref[lo:hi, :] + _dot_nn(dg_ref[lo:hi, :], wg_ref[...])
                  + _dot_nn(du_ref[lo:hi, :], wu_ref[...]) + _dot_nt(ds_ref[lo:hi, :], wpg_ref[...]))
            xhat, rstd = _layer_norm_stats(z1_ref[lo:hi, :])
            dz1 = _layer_norm_bwd(dh * g1_ref[...], xhat, rstd)
            dz1b = dz1.astype(BF16)
            dz1_ref[lo:hi, :] = dz1
            dz1b_ref[lo:hi, :] = dz1b
            acc_ref[0:1, :] += jnp.sum(dh * xhat, axis=0, keepdims=True)
            acc_ref[1:2, :] += jnp.sum(dh, axis=0, keepdims=True)
            dr_ref[lo:hi, :] = _dot_nt(dz1b, wo_ref[0:RET_W, :]).astype(BF16)
            da_ref[lo:hi, :] = _dot_nt(dz1b, wo_ref[RET_W:RET_W + ATT_W, :]).astype(BF16)

    body, lead = _after(after, body)
    return pl.pallas_call(
        body, name="dh1_ln1_bwd", grid=(t // tm,),
        in_specs=lead + [_row_spec(tm, d), _row_spec(tm, f), _row_spec(tm, f), _row_spec(tm, d), _row_spec(tm, d),
                         _resident_spec(wt_gate.shape), _resident_spec(wt_up.shape), _resident_spec(w_pg.shape),
                         _resident_spec(w_out.shape), _full_spec(g1.shape)],
        out_specs=[_row_spec(tm, d), _row_spec(tm, d), _row_spec(tm, RET_W), _row_spec(tm, ATT_W), _acc_spec((8, d))],
        out_shape=[SDS((t, d), F32), SDS((t, d), BF16), SDS((t, RET_W), BF16), SDS((t, ATT_W), BF16),
                   SDS((8, d), F32)],
        compiler_params=_params(("arbitrary",)),
    )(*after, dz2, dg, dup, dsb, z1, wt_gate, wt_up, w_pg, w_out, g1)


def _in_proj_bwd(dz1, parts, wt_in, after=()):
    t, d = dz1.shape
    tm = _row_tile(t, MATMUL_ROWS)
    widths = [p.shape[1] for p in parts]

    def body(*refs):
        dz_ref, part_refs, w_ref, dx_ref = refs[0], refs[1:1 + len(parts)], refs[-2], refs[-1]
        acc = ALPHA * dz_ref[...]
        lo = 0
        for p_ref, w in zip(part_refs, widths):
            acc = acc + _dot_nn(p_ref[...], w_ref[lo:lo + w, :])
            lo += w
        dx_ref[...] = acc

    body, lead = _after(after, body)
    return pl.pallas_call(
        body, name="in_proj_bwd", grid=(t // tm,),
        in_specs=lead + [_row_spec(tm, d)] + [_row_spec(tm, w) for w in widths] + [_full_spec(wt_in.shape)],
        out_specs=_row_spec(tm, d), out_shape=SDS((t, d), F32),
        compiler_params=_params(("parallel",)),
    )(*after, dz1, *parts, wt_in)


def _weight_grad(name, me, parts, rhs, after=()):
    t, n = rhs.shape
    widths = [p.shape[1] for p in parts]
    rows = sum(widths)
    own_rows = rows // N_DEV
    tk = _row_tile(t, MATMUL_ROWS)
    step = 256

    def body(*refs):
        me_ref, part_refs, rhs_ref = refs[0], refs[1:1 + len(parts)], refs[1 + len(parts)]
        full_ref, own_ref, acc = refs[-3], refs[-2], refs[-1]
        i = pl.program_id(0)

        @pl.when(i == 0)
        def _():
            acc[...] = jnp.zeros_like(acc)

        b = rhs_ref[...].astype(BF16)
        lo = 0
        for p_ref, w in zip(part_refs, widths):
            for c0 in range(0, w, step):
                c1 = min(c0 + step, w)
                acc[lo + c0:lo + c1, :] += _dot_tn(p_ref[:, c0:c1].astype(BF16), b)
            lo += w

        @pl.when(i == pl.num_programs(0) - 1)
        def _():
            full_ref[...] = acc[...].astype(BF16)
            own_ref[...] = acc[pl.ds(pl.multiple_of(me_ref[0] * own_rows, 8), own_rows), :]

    body, lead = _after(after, body)
    return pl.pallas_call(
        body, name=name, grid=(t // tk,),
        in_specs=lead + [_smem_spec()] + [_row_spec(tk, w) for w in widths] + [_row_spec(tk, n)],
        out_specs=[_full_spec((rows, n)), _full_spec((own_rows, n))],
        out_shape=[SDS((rows, n), BF16), SDS((own_rows, n), F32)],
        scratch_shapes=[pltpu.VMEM((rows, n), F32)],
        compiler_params=_params(("arbitrary",)),
    )(*after, me, *parts, rhs)


def _log_decay(decay_f, decay_b):
    def body(f_ref, b_ref, lf_ref, lb_ref):
        lf_ref[...] = jnp.log1p(-jnp.exp2(f_ref[...]))
        lb_ref[...] = jnp.log1p(-jnp.exp2(b_ref[...]))

    return pl.pallas_call(body, name="log_decay", out_shape=[SDS(decay_f.shape, F32)] * 2)(decay_f, decay_b)


def _chunk(ref, n):
    return ref[pl.ds(pl.multiple_of(n * CHUNK, CHUNK), CHUNK), :]


def _group_sum(is_a, v):
    sa = jnp.sum(jnp.where(is_a, v, 0.0), axis=1, keepdims=True)
    sb = jnp.sum(jnp.where(is_a, 0.0, v), axis=1, keepdims=True)
    return jnp.where(is_a, sa, sb)


def _seq_spec(s, col_block):
    return pl.BlockSpec((s, LANES), lambda b, h: (b, col_block + h))


def _smem_spec():
    return pl.BlockSpec(memory_space=pltpu.SMEM)


RET_UNROLL = 4


def _chunk_loop(n_chunks, body, init):
    u = RET_UNROLL if n_chunks % RET_UNROLL == 0 else 1

    def trip(i, carry):
        for j in range(u):
            carry = body(i * u + j, carry)
        return carry

    return lax.fori_loop(0, n_chunks // u, trip, init)


def _stacked_tables(lgf_ref, lgb_ref, pair):
    lane = lax.broadcasted_iota(jnp.int32, (1, LANES), 1)
    is_a = lane < HEAD_DIM
    lgf = jnp.where(is_a, lgf_ref[2 * pair], lgf_ref[2 * pair + 1])
    lgb = jnp.where(is_a, lgb_ref[2 * pair], lgb_ref[2 * pair + 1])
    row = lax.broadcasted_iota(jnp.int32, (CHUNK, 1), 0).astype(F32)
    kdec_f, qdec_f = jnp.exp(lgf * (CHUNK - 1.0 - row)), jnp.exp(lgf * (row + 1.0))
    kdec_b, qdec_b = jnp.exp(lgb * row), jnp.exp(lgb * (CHUNK - row))
    tab = dict(
        is_a=is_a, row=row, lam_f=jnp.exp(lgf * CHUNK), lam_b=jnp.exp(lgb * CHUNK),
        kdec=jnp.concatenate([kdec_f, kdec_b], axis=1), qdec=jnp.concatenate([qdec_f, qdec_b], axis=1),
        qexp=jnp.concatenate([jnp.broadcast_to(row + 1.0, (CHUNK, LANES)),
                              jnp.broadcast_to(CHUNK - row, (CHUNK, LANES))], axis=1),
        kexp=jnp.concatenate([jnp.broadcast_to(CHUNK - 1.0 - row, (CHUNK, LANES)),
                              jnp.broadcast_to(row, (CHUNK, LANES))], axis=1),
    )
    r = lax.broadcasted_iota(jnp.int32, (2 * LANES, LANES), 0)
    c = lax.broadcasted_iota(jnp.int32, (2 * LANES, LANES), 1)
    tab["diag2"] = ((r & (LANES - 1)) < HEAD_DIM) == (c < HEAD_DIM)
    i2 = lax.broadcasted_iota(jnp.int32, (2 * CHUNK, CHUNK), 0)
    j = lax.broadcasted_iota(jnp.int32, (2 * CHUNK, CHUNK), 1)
    head_b = i2 >= CHUNK
    diff = ((i2 & (CHUNK - 1)) - j).astype(F32)
    up, dn = jnp.maximum(diff, 0.0), jnp.maximum(-diff, 0.0)
    lgf2 = jnp.where(head_b, lgf_ref[2 * pair + 1], lgf_ref[2 * pair])
    lgb2 = jnp.where(head_b, lgb_ref[2 * pair + 1], lgb_ref[2 * pair])
    ef = jnp.where(diff >= 0, jnp.exp(lgf2 * up), 0.0)
    eb = jnp.where(diff <= 0, jnp.exp(lgb2 * dn), 0.0)
    tab["d2"] = ef + eb
    tab["df2"] = ef * up
    tab["db2"] = eb * dn
    return tab


def _stack_pair(is_a, x):
    zero = jnp.zeros_like(x)
    return jnp.concatenate([jnp.where(is_a, x, zero), jnp.where(is_a, zero, x)], axis=0)


def _unstack_pair(is_a, x2):
    return jnp.where(is_a, x2[0:CHUNK, :], x2[CHUNK:2 * CHUNK, :])


def _both_ways(x, dec):
    return (jnp.concatenate([x, x], axis=1) * dec).astype(BF16)


def _scan_states(n_chunks, st, up_rows, up_lam, down_rows, down_lam):
    zero = jnp.zeros((LANES, LANES), F32)

    def up(n, r):
        new = st[n, up_rows, :]
        st[n, up_rows, :] = r
        return r * up_lam + new

    def down(s, r):
        n = n_chunks - 1 - s
        new = st[n, down_rows, :]
        st[n, down_rows, :] = r
        return r * down_lam + new

    lax.fori_loop(0, n_chunks, up, zero)
    lax.fori_loop(0, n_chunks, down, zero)


FWD_ROWS, BWD_ROWS = pl.ds(0, LANES), pl.ds(LANES, LANES)


ST_GAIN, ST_XF, ST_XB, ST_IFA, ST_IFB, ST_IBA, ST_IBB, ST_LF, ST_LB = 0, 1, 2, 3, 4, 5, 6, 8, 9
ST_ROWS = 16


GW = GROUP * HEAD_DIM
KEYS = 3 * BLOCK


def _attn_tables(g, bias_ref):
    r = lax.broadcasted_iota(jnp.int32, (GROUP * BLOCK, KEYS), 0)
    kj = lax.broadcasted_iota(jnp.int32, (GROUP * BLOCK, KEYS), 1)
    qi = r & (BLOCK - 1)
    hh = lax.shift_right_logical(r, 7)
    dist = jnp.abs(kj - BLOCK - qi)
    slope = jnp.exp2(-(GROUP * g + hh + 1).astype(F32) * (8.0 / ATTN_HEADS))
    bias_ref[...] = jnp.where(dist <= BLOCK, -slope * dist.astype(F32), NEG_INF)


def _own_lanes(g):
    return lax.shift_right_logical(lax.broadcasted_iota(jnp.int32, (1, LANES), 1), 6) == g


def _mask_keys(x_ref, g, scale, pad_ref, s):
    pad_ref[0:BLOCK, :] = jnp.zeros((BLOCK, LANES), BF16)
    pad_ref[BLOCK + s:2 * BLOCK + s, :] = jnp.zeros((BLOCK, LANES), BF16)
    pad_ref[BLOCK:BLOCK + s, :] = jnp.where(_own_lanes(g), x_ref[...].astype(F32) * scale, 0.0).astype(BF16)


def _lane_block(x, j):
    return x[:, j * LANES:(j + 1) * LANES]


def _stack_heads(x, g):
    assert GROUP == 4 and GW == 2 * LANES
    x1 = pltpu.roll(x, HEAD_DIM, 1)
    keep = _own_lanes(g)
    zero = jnp.zeros((BLOCK, LANES), x.dtype)
    rows = []
    for h in range(GROUP):
        for_g0 = _lane_block(x, h // 2) if h % 2 == 0 else _lane_block(x1, ((h + 1) // 2) % 2)
        for_g1 = _lane_block(x, h // 2) if h % 2 == 1 else _lane_block(x1, h // 2)
        rows.append(jnp.where(keep, jnp.where(g == 0, for_g0, for_g1), zero))
    return jnp.concatenate(rows, axis=0)


def _unstack_heads(x4, g):
    p = [x4[h * BLOCK:(h + 1) * BLOCK, :] for h in range(GROUP)]
    cat = lambda a, b: jnp.concatenate([a, b], axis=1)
    in_place = jnp.where(g == 0, cat(p[0], p[2]), cat(p[1], p[3]))
    one_left = jnp.where(g == 0, cat(p[1], p[3]), cat(p[2], p[0]))
    return in_place + pltpu.roll(one_left, HEAD_DIM, 1)


def _sink_column(sink_ref, g):
    rh = lax.shift_right_logical(lax.broadcasted_iota(jnp.int32, (GROUP * BLOCK, 1), 0), 7)
    col = jnp.zeros((GROUP * BLOCK, 1), F32)
    for h in range(GROUP):
        col = jnp.where(rh == h, sink_ref[GROUP * g + h], col)
    return col


def _attn_probs(qm, k3, bias_ref, sink_col, n, s):
    logits = _dot_nt(qm, k3) + bias_ref[...]
    kpos = n * BLOCK - BLOCK + lax.broadcasted_iota(jnp.int32, (1, KEYS), 1)
    logits = jnp.where((kpos >= 0) & (kpos < s), logits, NEG_INF)
    m = jnp.maximum(jnp.max(logits, axis=1, keepdims=True), sink_col)
    e = jnp.exp(logits - m)
    e_sink = jnp.exp(sink_col - m)
    inv = 1.0 / (jnp.sum(e, axis=1, keepdims=True) + e_sink)
    return e * inv, e_sink * inv


PAIRS_PER_KV = (RET_HEADS // 2) // KV_HEADS


def _mixers_fwd(u, lgf, lgb, gn_gain, sink, b_loc):
    t = u.shape[0]
    s = t // b_loc
    n_chunks = s // CHUNK
    pairs = RET_HEADS // 2
    trips = n_chunks // RET_UNROLL
    blocks_half = (s // BLOCK) // PAIRS_PER_KV
    per_trip = blocks_half // trips
    assert n_chunks % RET_UNROLL == 0 and blocks_half % trips == 0 and PAIRS_PER_KV == 2

    def body(lgf_ref, lgb_ref, sink_ref, q_ref, k_ref, v_ref, g_ref, gain_ref, aq_ref, ak_ref, av_ref,
             r_ref, xhat_ref, rstd_ref, a_ref, st, kpad, vpad, bias):
        pair = pl.program_id(1)
        g, half = lax.shift_right_logical(pair, 1), pair & 1
        tab = _stacked_tables(lgf_ref, lgb_ref, pair)
        is_a = tab["is_a"]

        @pl.when(half == 0)
        def _():
            _attn_tables(g, bias)
            _mask_keys(ak_ref, g, Q_SCALE, kpad, s)
            _mask_keys(av_ref, g, 1.0, vpad, s)

        def kv_body(n, _):
            k8 = _chunk(k_ref, n).astype(F32) * Q_SCALE
            st[n] = jnp.where(tab["diag2"], _dot_tn(_both_ways(k8, tab["kdec"]), _chunk(v_ref, n)), 0.0)
            return 0

        _chunk_loop(n_chunks, kv_body, 0)
        _scan_states(n_chunks, st, FWD_ROWS, tab["lam_f"], BWD_ROWS, tab["lam_b"])
        sink_col = _sink_column(sink_ref, g)

        def retention_chunk(n):
            q = _chunk(q_ref, n)
            k8 = (_chunk(k_ref, n).astype(F32) * Q_SCALE).astype(BF16)
            v = _chunk(v_ref, n)
            p2 = (_dot_nt(_stack_pair(is_a, q), k8) * tab["d2"]).astype(BF16)
            y = _unstack_pair(is_a, _dot_nn(p2, v))
            y = y + _dot_nn(_both_ways(q.astype(F32), tab["qdec"]), st[n].astype(BF16))
            rows = pl.ds(pl.multiple_of(n * CHUNK, CHUNK), CHUNK)
            mu = _group_sum(is_a, y) * (1.0 / HEAD_DIM)
            dlt = y - mu
            var = _group_sum(is_a, dlt * dlt) * (1.0 / HEAD_DIM)
            rstd = lax.rsqrt(var + GN_EPS)
            xhat = dlt * rstd
            xhat_ref[rows, :] = xhat
            rstd_ref[rows, :] = rstd
            gate = _chunk(g_ref, n).astype(F32)
            r_ref[rows, :] = (xhat * gain_ref[...] * gate * _sigmoid(gate)).astype(BF16)

        def attention_block(blk):
            n = half * blocks_half + blk
            rows = pl.ds(pl.multiple_of(blk * BLOCK, BLOCK), BLOCK)
            keys = pl.ds(pl.multiple_of(n * BLOCK, BLOCK), KEYS)
            p, _ = _attn_probs(_stack_heads(aq_ref[rows, :], g), kpad[keys, :], bias, sink_col, n, s)
            a_ref[rows, :] = _unstack_heads(_dot_nn(p.astype(BF16), vpad[keys, :]), g).astype(BF16)

        def trip(i, _):
            for j in range(max(RET_UNROLL, per_trip)):
                if j < RET_UNROLL:
                    retention_chunk(i * RET_UNROLL + j)
                if j < per_trip:
                    attention_block(i * per_trip + j)
            return 0

        lax.fori_loop(0, trips, trip, 0)

    lane_blk = lambda c0: _seq_spec(s, c0 // LANES)
    half_rows = blocks_half * BLOCK
    aq_spec = pl.BlockSpec((half_rows, GW), lambda b, h: (b * PAIRS_PER_KV + (h & 1), C_AQ // GW + h // 2))
    a_spec = pl.BlockSpec((half_rows, GW), lambda b, h: (b * PAIRS_PER_KV + (h & 1), h // 2))
    kv_spec = lambda c0: pl.BlockSpec((s, LANES), lambda b, h: (b, c0 // LANES))
    pad = pltpu.VMEM((s + 2 * BLOCK, LANES), BF16)
    return pl.pallas_call(
        body, name="mixers_fwd", grid=(b_loc, pairs),
        in_specs=[_smem_spec(), _smem_spec(), _smem_spec(), lane_blk(C_RQ), lane_blk(C_RK), lane_blk(C_RV),
                  lane_blk(C_RG), pl.BlockSpec((1, LANES), lambda b, h: (0, h)), aq_spec, kv_spec(C_AK), kv_spec(C_AV)],
        out_specs=[_seq_spec(s, 0), _seq_spec(s, 0), _seq_spec(s, 0), a_spec],
        out_shape=[SDS((t, RET_W), BF16), SDS((t, RET_W), F32), SDS((t, RET_W), F32), SDS((t, ATT_W), BF16)],
        scratch_shapes=[pltpu.VMEM((n_chunks, 2 * LANES, LANES), F32), pad, pad,
                        pltpu.VMEM((GROUP * BLOCK, KEYS), F32)],
        compiler_params=_params(("arbitrary", "arbitrary")),
    )(lgf, lgb, sink, u, u, u, u, gn_gain, u, u, u)


def _mixers_bwd(u, xhat, rstd, dr, da, lgf, lgb, gn_gain, sink, b_loc, after=()):
    t = u.shape[0]
    s = t // b_loc
    n_chunks = s // CHUNK
    pairs = RET_HEADS // 2
    trips = n_chunks // RET_UNROLL
    blocks_half = (s // BLOCK) // PAIRS_PER_KV
    per_trip = blocks_half // trips
    assert n_chunks % RET_UNROLL == 0 and blocks_half % trips == 0 and PAIRS_PER_KV == 2

    def body(lgf_ref, lgb_ref, sink_ref, q_ref, k_ref, v_ref, g_ref, xhat_ref, rstd_ref, dr_ref, gain_ref,
             aq_ref, ak_ref, av_ref, do_ref,
             dq_ref, dk_ref, dv_ref, dg_ref, st_ref, daq_ref, dak_ref, dav_ref, dsink_ref,
             st, gr, dy_s, kpad, vpad, bias, dk_acc, dv_acc):
        pair = pl.program_id(1)
        g, half = lax.shift_right_logical(pair, 1), pair & 1
        tab = _stacked_tables(lgf_ref, lgb_ref, pair)
        is_a = tab["is_a"]
        gain = gain_ref[...]

        @pl.when(half == 0)
        def _():
            _attn_tables(g, bias)
            _mask_keys(ak_ref, g, Q_SCALE, kpad, s)
            _mask_keys(av_ref, g, 1.0, vpad, s)
            dsink_ref[...] = jnp.zeros_like(dsink_ref)

        @pl.when(pair == 0)
        def _():
            dk_acc[...] = jnp.zeros_like(dk_acc)
            dv_acc[...] = jnp.zeros_like(dv_acc)

        def norm_body(n, dgain):
            rows = pl.ds(pl.multiple_of(n * CHUNK, CHUNK), CHUNK)
            xhat, rstd = xhat_ref[rows, :], rstd_ref[rows, :]
            gate = g_ref[rows, :].astype(F32)
            sg = _sigmoid(gate)
            silu = gate * sg
            d_out = dr_ref[rows, :].astype(F32)
            dg_ref[rows, :] = (d_out * xhat * gain * (sg * (1.0 + gate * (1.0 - sg)))).astype(BF16)
            dxh = d_out * gain * silu
            m1 = _group_sum(is_a, dxh) * (1.0 / HEAD_DIM)
            m2 = _group_sum(is_a, dxh * xhat) * (1.0 / HEAD_DIM)
            dy = (rstd * (dxh - m1 - xhat * m2)).astype(BF16)
            dy_s[rows, :] = dy
            k8 = k_ref[rows, :].astype(F32) * Q_SCALE
            st[n] = jnp.where(tab["diag2"], _dot_tn(_both_ways(k8, tab["kdec"]), v_ref[rows, :]), 0.0)
            qf = q_ref[rows, :].astype(F32)
            gr[n] = jnp.where(tab["diag2"], _dot_tn(_both_ways(qf, tab["qdec"]), dy), 0.0)
            return dgain + jnp.sum(d_out * xhat * silu, axis=0, keepdims=True)

        colsum = lambda x: jnp.sum(x, axis=0, keepdims=True)

        def grad_body(n, carry):
            xfb, ifa, ifb, iba, ibb, lf, lb = carry
            rows = pl.ds(pl.multiple_of(n * CHUNK, CHUNK), CHUNK)
            q = q_ref[rows, :]
            qf = q.astype(F32)
            k8f = k_ref[rows, :].astype(F32) * Q_SCALE
            k8 = k8f.astype(BF16)
            v = v_ref[rows, :]
            dy = dy_s[rows, :]
            q2, dy2 = _stack_pair(is_a, q), _stack_pair(is_a, dy)
            sc = _dot_nt(q2, k8)
            dp = _dot_nt(dy2, v)
            a2 = (sc * tab["d2"]).astype(BF16)
            ds2 = (dp * tab["d2"]).astype(BF16)
            dq = _unstack_pair(is_a, _dot_nn(ds2, k8))
            dk = _dot_tn(ds2, q2)
            dv = _dot_tn(a2, dy2)
            prod = sc * dp
            pf, pb = prod * tab["df2"], prod * tab["db2"]
            ifa, ifb = ifa + colsum(pf[0:CHUNK, :]), ifb + colsum(pf[CHUNK:2 * CHUNK, :])
            iba, ibb = iba + colsum(pb[0:CHUNK, :]), ibb + colsum(pb[CHUNK:2 * CHUNK, :])
            states, sgrads = st[n], gr[n]
            sb, gb = states.astype(BF16), sgrads.astype(BF16)
            dqc = _dot_nt(dy, sb) * tab["qdec"]
            dkc = _dot_nt(v, gb) * tab["kdec"]
            dv = dv + _dot_nn(_both_ways(k8f, tab["kdec"]), gb)
            dq_ref[rows, :] = (dq + dqc[:, 0:LANES] + dqc[:, LANES:2 * LANES]).astype(BF16)
            dk_ref[rows, :] = ((dk + dkc[:, 0:LANES] + dkc[:, LANES:2 * LANES]) * Q_SCALE).astype(BF16)
            dv_ref[rows, :] = dv.astype(BF16)
            q2w, k2w = jnp.concatenate([qf, qf], axis=1), jnp.concatenate([k8f, k8f], axis=1)
            xfb = xfb + colsum(tab["qexp"] * q2w * dqc + tab["kexp"] * k2w * dkc)
            prod_s = sgrads * states
            lf, lb = lf + colsum(prod_s[0:LANES, :]), lb + colsum(prod_s[LANES:2 * LANES, :])
            return xfb, ifa, ifb, iba, ibb, lf, lb

        sink_col = _sink_column(sink_ref, g)
        head_row = lax.broadcasted_iota(jnp.int32, dsink_ref.shape, 0)

        def attention_block(blk):
            n = half * blocks_half + blk
            rows = pl.ds(pl.multiple_of(blk * BLOCK, BLOCK), BLOCK)
            keys = pl.ds(pl.multiple_of(n * BLOCK, BLOCK), KEYS)
            qm = _stack_heads(aq_ref[rows, :], g)
            k3, v3 = kpad[keys, :], vpad[keys, :]
            p, p_sink = _attn_probs(qm, k3, bias, sink_col, n, s)
            dom = _stack_heads(do_ref[rows, :], g)
            dp = _dot_nt(dom, v3)
            delta = jnp.sum(p * dp, axis=1, keepdims=True)
            ds_mat = (p * (dp - delta)).astype(BF16)
            daq_ref[rows, :] = _unstack_heads(_dot_nn(ds_mat, k3), g).astype(BF16)
            dk_acc[keys, :] += _dot_tn(ds_mat, qm) * Q_SCALE
            dv_acc[keys, :] += _dot_tn(p.astype(BF16), dom)
            w = p_sink * delta
            upd = jnp.zeros(dsink_ref.shape, F32)
            for h in range(GROUP):
                upd = upd + jnp.where(head_row == h, -jnp.sum(w[h * BLOCK:(h + 1) * BLOCK, :]), 0.0)
            dsink_ref[...] += upd

        dgain = _chunk_loop(n_chunks, norm_body, jnp.zeros((1, LANES), F32))
        _scan_states(n_chunks, st, FWD_ROWS, tab["lam_f"], BWD_ROWS, tab["lam_b"])
        _scan_states(n_chunks, gr, BWD_ROWS, tab["lam_b"], FWD_ROWS, tab["lam_f"])

        def trip(i, carry):
            for j in range(max(RET_UNROLL, per_trip)):
                if j < RET_UNROLL:
                    carry = grad_body(i * RET_UNROLL + j, carry)
                if j < per_trip:
                    attention_block(i * per_trip + j)
            return carry

        z = jnp.zeros((1, LANES), F32)
        init = (jnp.zeros((1, 2 * LANES), F32), z, z, z, z, z, z)
        xfb, ifa, ifb, iba, ibb, lf, lb = lax.fori_loop(0, trips, trip, init)
        st_ref[...] = jnp.zeros_like(st_ref)
        st_ref[ST_GAIN:ST_GAIN + 1, :] = dgain
        st_ref[ST_XF:ST_XF + 1, :] = xfb[:, 0:LANES]
        st_ref[ST_XB:ST_XB + 1, :] = xfb[:, LANES:2 * LANES]
        st_ref[ST_IFA:ST_IFA + 1, :] = ifa
        st_ref[ST_IFB:ST_IFB + 1, :] = ifb
        st_ref[ST_IBA:ST_IBA + 1, :] = iba
        st_ref[ST_IBB:ST_IBB + 1, :] = ibb
        st_ref[ST_LF:ST_LF + 1, :] = lf * (CHUNK * tab["lam_f"])
        st_ref[ST_LB:ST_LB + 1, :] = lb * (CHUNK * tab["lam_b"])

        @pl.when(pair == pairs - 1)
        def _():
            dak_ref[...] = dk_acc[BLOCK:BLOCK + s, :].astype(BF16)
            dav_ref[...] = dv_acc[BLOCK:BLOCK + s, :].astype(BF16)

    lane_blk = lambda c0: _seq_spec(s, c0 // LANES)
    seq0 = _seq_spec(s, 0)
    half_rows = blocks_half * BLOCK
    aq_spec = pl.BlockSpec((half_rows, GW), lambda b, h: (b * PAIRS_PER_KV + (h & 1), C_AQ // GW + h // 2))
    a_spec = pl.BlockSpec((half_rows, GW), lambda b, h: (b * PAIRS_PER_KV + (h & 1), h // 2))
    kv_spec = lambda c0: pl.BlockSpec((s, LANES), lambda b, h: (b, c0 // LANES))
    kv_out = pl.BlockSpec((s, LANES), lambda b, h: (b, 0))
    state = pltpu.VMEM((n_chunks, 2 * LANES, LANES), F32)
    pad = pltpu.VMEM((s + 2 * BLOCK, LANES), BF16)
    acc = pltpu.VMEM((s + 2 * BLOCK, LANES), F32)
    body, lead = _after(after, body)
    return pl.pallas_call(
        body, name="mixers_bwd", grid=(b_loc, pairs),
        in_specs=lead + [_smem_spec(), _smem_spec(), _smem_spec(), lane_blk(C_RQ), lane_blk(C_RK), lane_blk(C_RV),
                         lane_blk(C_RG), seq0, seq0, seq0, pl.BlockSpec((1, LANES), lambda b, h: (0, h)),
                         aq_spec, kv_spec(C_AK), kv_spec(C_AV), a_spec],
        out_specs=[seq0] * 4 + [pl.BlockSpec((ST_ROWS, LANES), lambda b, h: (b, h)), a_spec, kv_out, kv_out,
                                pl.BlockSpec((8, LANES), lambda b, h: (b * KV_HEADS + h // 2, 0))],
        out_shape=[SDS((t, RET_W), BF16)] * 4 + [SDS((b_loc * ST_ROWS, RET_W), F32), SDS((t, ATT_W), BF16),
                                                   SDS((t, KV_W), BF16), SDS((t, KV_W), BF16),
                                                   SDS((b_loc * KV_HEADS * 8, LANES), F32)],
        scratch_shapes=[state, state, pltpu.VMEM((s, LANES), BF16), pad, pad,
                        pltpu.VMEM((GROUP * BLOCK, KEYS), F32), acc, acc],
        compiler_params=_params(("arbitrary", "arbitrary")),
    )(*after, lgf, lgb, sink, u, u, u, u, xhat, rstd, dr, gn_gain, u, u, u, da)


def _pack_small(acc2, acc1, ret_stats, dsink, b_loc, d):
    pairs = RET_HEADS // 2

    def body(acc2_ref, acc1_ref, st_ref, dsink_ref, out_ref):
        out_ref[...] = jnp.zeros_like(out_ref)
        out_ref[ROW_LN1G:ROW_LN1G + 1, :] = acc1_ref[0:1, :]
        out_ref[ROW_LN1B:ROW_LN1B + 1, :] = acc1_ref[1:2, :]
        out_ref[ROW_LN2G:ROW_LN2G + 1, :] = acc2_ref[1:2, :]
        out_ref[ROW_LN2B:ROW_LN2B + 1, :] = acc2_ref[2:3, :]
        out_ref[ROW_LOSS:ROW_LOSS + 1, :] = acc2_ref[0:1, :]
        st = st_ref[0:ST_ROWS, :]
        for b in range(1, b_loc):
            st = st + st_ref[b * ST_ROWS:(b + 1) * ST_ROWS, :]
        out_ref[ROW_GN:ROW_GN + 1, 0:RET_W] = st[ST_GAIN:ST_GAIN + 1, :]
        lane = lax.broadcasted_iota(jnp.int32, (1, d), 1)
        misc = jnp.zeros((1, d), F32)
        for pr in range(pairs):
            blk = st[:, pr * LANES:(pr + 1) * LANES]
            half = lax.broadcasted_iota(jnp.int32, (1, LANES), 1) < HEAD_DIM
            for h in range(2):
                sel = half if h == 0 else jnp.logical_not(half)
                cross_f = jnp.sum(jnp.where(sel, blk[ST_XF:ST_XF + 1, :] + blk[ST_LF:ST_LF + 1, :], 0.0))
                cross_b = jnp.sum(jnp.where(sel, blk[ST_XB:ST_XB + 1, :] + blk[ST_LB:ST_LB + 1, :], 0.0))
                intra_f = jnp.sum(blk[ST_IFA + h:ST_IFA + h + 1, :])
                intra_b = jnp.sum(blk[ST_IBA + h:ST_IBA + h + 1, :])
                head = 2 * pr + h
                misc = jnp.where(lane == MISC_DF + head, cross_f + intra_f, misc)
                misc = jnp.where(lane == MISC_DB + head, cross_b + intra_b, misc)
        for g in range(KV_HEADS):
            tot = dsink_ref[g * 8:(g + 1) * 8, :]
            for b in range(1, b_loc):
                tot = tot + dsink_ref[(b * KV_HEADS + g) * 8:(b * KV_HEADS + g + 1) * 8, :]
            for h in range(GROUP):
                misc = jnp.where(lane == MISC_SINK + GROUP * g + h, jnp.sum(tot[h:h + 1, 0:1]), misc)
        out_ref[ROW_MISC:ROW_MISC + 1, :] = misc

    return pl.pallas_call(body, name="pack_small", out_shape=SDS((SMALL_ROWS, d), F32))(acc2, acc1, ret_stats, dsink)


BIG = ("w_in", "w_out", "w_ffn_gate", "w_ffn_up", "w_ffn_down", "w_ple_proj", "w_ple_gate")
TRANSPOSED_OUTSIDE = ("w_in", "w_ffn_gate", "w_ffn_up")
TRANSPOSED_HERE = ("w_ple_proj",)
SMALL = ("ret_decay_fwd", "ret_decay_bwd", "ret_gn_gain", "attn_sink", "ln1_gain", "ln1_bias", "ln2_gain", "ln2_bias")
ORDER = ("w_in", "ret_decay_fwd", "ret_decay_bwd", "ret_gn_gain", "attn_sink", "w_out", "ln1_gain", "ln1_bias",
         "w_ffn_gate", "w_ffn_up", "w_ffn_down", "w_ple_proj", "w_ple_gate", "ln2_gain", "ln2_bias")


GATHER_ORDER = ("w_in", "w_out", "w_ffn_gate", "w_ffn_up", "w_ple_gate", "w_ple_proj", "w_ffn_down")
GATHER_TWO_LEVEL = ("w_in", "w_ffn_up")


def _local_step(x2, p2, target2, fetch, publish, small, b_loc, me):
    d = x2.shape[1]
    lgf, lgb = _log_decay(small["ret_decay_fwd"], small["ret_decay_bwd"])
    lgf1, lgb1, sink1 = lgf.reshape(-1), lgb.reshape(-1), small["attn_sink"].reshape(-1)
    (w_in,) = fetch(("w_in",), ())
    u, xb = _in_proj(x2, w_in)
    r, ret_xhat, ret_rstd, a = _mixers_fwd(u, lgf1, lgb1, small["ret_gn_gain"], sink1, b_loc)
    w_out, w_gate, w_up = fetch(("w_out", "w_ffn_gate", "w_ffn_up"), (r, a))
    z1, h1b, dact_dg, dact_du, act = _mix_ln1_ffn_up(
        r, a, x2, w_out, w_gate, w_up, small["ln1_gain"], small["ln1_bias"])
    w_pg, w_pe, w_down = fetch(("w_ple_gate", "w_ple_proj", "w_ffn_down"), (act,))
    dz2, dz2b, dsb, dpleb, dg, dup, acc2 = _ffn_down_ln2_loss(
        act, dact_dg, dact_du, h1b, p2, z1, target2, w_down, w_pg, w_pe,
        small["ln1_gain"], small["ln1_bias"], small["ln2_gain"], small["ln2_bias"])
    own = {}

    def grad(name, parts, rhs, after=()):
        whole, own[name] = _weight_grad("grad_" + name, me, parts, rhs, after)
        return whole

    t2 = publish("ffn", dict(w_ffn_down=grad("w_ffn_down", [act], dz2b),
                             w_ple_proj=grad("w_ple_proj", [dpleb], p2),
                             w_ple_gate=grad("w_ple_gate", [h1b], dsb),
                             w_ffn_gate=grad("w_ffn_gate", [dg], h1b),
                             w_ffn_up=grad("w_ffn_up", [dup], h1b)))
    dz1, dz1b, dr, da, acc1 = _dh1_ln1_bwd(dz2, dg, dup, dsb, z1, w_gate, w_up, w_pg, w_out, small["ln1_gain"], t2)
    t3 = publish("out", dict(w_out=grad("w_out", [r, a], dz1b)))
    dq, dk, dv, dgate, ret_stats, daq, dak, dav, dsink = _mixers_bwd(
        u, ret_xhat, ret_rstd, dr, da, lgf1, lgb1, small["ret_gn_gain"], sink1, b_loc, t3)
    parts = [dq, dk, dv, dgate, daq, dak, dav]
    small_part = _pack_small(acc2, acc1, ret_stats, dsink, b_loc, d)
    t4 = publish("in", dict(w_in=grad("w_in", parts, xb)), small_part)
    grad_x = _in_proj_bwd(dz1, parts, w_in, t4)
    return grad_x, own, small_part


def kernel(x, p, w_in, ret_decay_fwd, ret_decay_bwd, ret_gn_gain, attn_sink, w_out, ln1_gain, ln1_bias, w_ffn_gate, w_ffn_up, w_ffn_down, w_ple_proj, w_ple_gate, ln2_gain, ln2_bias, loss_target, m_w_in, m_ret_decay_fwd, m_ret_decay_bwd, m_ret_gn_gain, m_attn_sink, m_w_out, m_ln1_gain, m_ln1_bias, m_w_ffn_gate, m_w_ffn_up, m_w_ffn_down, m_w_ple_proj, m_w_ple_gate, m_ln2_gain, m_ln2_bias, v_w_in, v_ret_decay_fwd, v_ret_decay_bwd, v_ret_gn_gain, v_attn_sink, v_w_out, v_ln1_gain, v_ln1_bias, v_w_ffn_gate, v_w_ffn_up, v_w_ffn_down, v_w_ple_proj, v_w_ple_gate, v_ln2_gain, v_ln2_bias):
    given = dict(locals())

    def strip(n, a):
        if n not in BIG:
            return a
        return a[0].T if n in TRANSPOSED_OUTSIDE else a[0]

    def restore(n, a):
        if n not in BIG:
            return a
        return (a.T if n in TRANSPOSED_OUTSIDE else a)[None]

    w = {n: strip(n, given[n]) for n in ORDER}
    m = {n: strip(n, given["m_" + n]) for n in ORDER}
    v = {n: strip(n, given["v_" + n]) for n in ORDER}
    b_loc, s, d = x.shape
    x2 = x.reshape(b_loc * s, d)
    p2 = p[0].reshape(b_loc * s, p.shape[-1])
    target2 = loss_target.reshape(b_loc * s, d)

    small = {n: w[n] for n in SMALL}
    me = (4 * lax.axis_index("x") + 2 * lax.axis_index("y") + lax.axis_index("c")).astype(jnp.int32).reshape(1)

    gathered = _prep_shards(me, {n: w[n] for n in BIG})
    gather = _split_copy_start(
        "gather_start", [(gathered[n],) for n in GATHER_ORDER],
        [_gather_copy_near if n in GATHER_TWO_LEVEL else _gather_copy for n in GATHER_ORDER])

    def fetch(names, after):
        out = {}
        for n in [n for n in names if n in GATHER_TWO_LEVEL]:
            near = _split_copy_wait("gather_wait_" + n + "_near", gather, [GATHER_ORDER.index(n)], list(after))
            passed = _split_copy_start("gather_pass_" + n, near, [_gather_copy_pass])
            out[n] = _split_copy_wait("gather_wait_" + n, passed, [0], [])[0][0]
        direct = [n for n in names if n not in GATHER_TWO_LEVEL]
        if direct:
            got = _split_copy_wait("gather_wait_" + direct[0], gather, [GATHER_ORDER.index(n) for n in direct],
                                   list(after))
            out.update({n: item[0] for n, item in zip(direct, got)})
        return [out[n] for n in names]

    scatters = []

    def publish(tag, products, small_sums=None):
        items = [(products[n], lax.empty((N_DEV - 1, products[n].shape[0] // N_DEV, products[n].shape[1]), BF16))
                 for n in products]
        copies = [_scatter_copy] * len(items)
        if small_sums is not None:
            items.append((small_sums, lax.empty((N_DEV - 1,) + small_sums.shape, F32)))
            copies.append(_small_copy)
        started = _split_copy_start("scatter_start_" + tag, items, copies)
        scatters.append((list(products), small_sums is not None, started))
        return (started["token"],)

    grad_x, own, small_part = _local_step(x2, p2, target2, fetch, publish, small, b_loc, me)

    out_g, out_d, out_m, out_v = {}, {}, {}, {}
    after = [grad_x]
    for names, with_small, started in scatters:
        landed = _split_copy_wait("scatter_wait_" + names[0], started, list(range(len(started["items"]))), after)
        if with_small:
            loss, sg, sd, sm, sv = _small_adamw(
                me, small_part, landed[-1][1], small, {n: m[n] for n in SMALL}, {n: v[n] for n in SMALL})
            for dst, src in ((out_g, sg), (out_d, sd), (out_m, sm), (out_v, sv)):
                dst.update(src)
        for n, (_, recv) in zip(names, landed):
            out_g[n], out_d[n], out_m[n], out_v[n] = _reduce_adamw(
                n, own[n], recv, w[n], m[n], v[n], n in TRANSPOSED_HERE)
        after = [out_v[names[-1]]]

    outs = [loss[0, 0], grad_x.reshape(x.shape)]
    for group in (out_g, out_d, out_m, out_v):
        outs += [restore(n, group[n]) for n in ORDER]
    return tuple(outs)
```

```python
import functools

import jax
import jax.numpy as jnp
from jax import lax
from jax.experimental import pallas as pl
from jax.experimental.pallas import tpu as pltpu

F32, BF16 = jnp.float32, jnp.bfloat16
SDS = jax.ShapeDtypeStruct
MESH = pl.DeviceIdType.MESH

N_DEV = 8
HEAD_DIM = 64
RET_HEADS = 8
ATTN_HEADS = 8
KV_HEADS = 2
GROUP = ATTN_HEADS // KV_HEADS
RET_W = RET_HEADS * HEAD_DIM
ATT_W = ATTN_HEADS * HEAD_DIM
KV_W = KV_HEADS * HEAD_DIM
LANES = 128
CHUNK = 128
BLOCK = 128
Q_SCALE = HEAD_DIM ** -0.5
ALPHA = 2.0 ** 0.25
LN_EPS = 1e-5
GN_EPS = 1e-5
NEG_INF = -1e30
C_RQ, C_RK, C_RV, C_RG = 0, RET_W, 2 * RET_W, 3 * RET_W
C_AQ = 4 * RET_W
C_AK = C_AQ + ATT_W
C_AV = C_AK + KV_W
IN_W = C_AV + KV_W

ADAM_LR = 0.001
ADAM_B1 = 0.9
ADAM_B2 = 0.999
ADAM_EPS = 1e-08
ADAM_WD = 0.01
ADAM_STEP = 10

VMEM_LIMIT = 56 * 1024 * 1024
MATMUL_ROWS = 512
EPILOGUE_ROWS = 256
SUB_ROWS = 256
SMALL_ROWS = 16
ROW_LN1G, ROW_LN1B, ROW_LN2G, ROW_LN2B, ROW_LOSS, ROW_GN, ROW_MISC = 0, 1, 2, 3, 4, 5, 6
MISC_DF, MISC_DB, MISC_SINK = 0, 8, 16


def _dot_nn(a, b):
    return lax.dot_general(a, b, (((1,), (0,)), ((), ())), preferred_element_type=F32)


def _dot_nt(a, b):
    return lax.dot_general(a, b, (((1,), (1,)), ((), ())), preferred_element_type=F32)


def _dot_tn(a, b):
    return lax.dot_general(a, b, (((0,), (0,)), ((), ())), preferred_element_type=F32)


def _params(sem=None, vmem=VMEM_LIMIT):
    kw = {"vmem_limit_bytes": vmem}
    if sem is not None:
        kw["dimension_semantics"] = sem
    return pltpu.CompilerParams(**kw)


def _row_tile(t, want=512):
    tm = want
    while t % tm:
        tm //= 2
    return tm


def _sigmoid(x):
    return jax.nn.sigmoid(x)


def _layer_norm_stats(z):
    mu = jnp.mean(z, axis=1, keepdims=True)
    d = z - mu
    var = jnp.mean(d * d, axis=1, keepdims=True)
    rstd = lax.rsqrt(var + LN_EPS)
    return d * rstd, rstd


def _layer_norm_bwd(dxh, xhat, rstd):
    m1 = jnp.mean(dxh, axis=1, keepdims=True)
    m2 = jnp.mean(dxh * xhat, axis=1, keepdims=True)
    return rstd * (dxh - m1 - xhat * m2)


def _prep_shards(me, shards):
    names = list(shards)

    def body(me_ref, *refs):
        for name, src, dst in zip(names, refs[:len(names)], refs[len(names):]):
            val = src[...]
            dst[...] = (val.T if name in TRANSPOSED_HERE else val).astype(BF16)

    shape = lambda n, a: a.shape[::-1] if n in TRANSPOSED_HERE else a.shape
    shapes = [shape(n, shards[n]) for n in names]
    out = pl.pallas_call(
        body, name="prep_shards",
        grid_spec=pltpu.PrefetchScalarGridSpec(
            num_scalar_prefetch=1, grid=(1,),
            in_specs=[pl.BlockSpec(shards[n].shape, lambda i, me_ref: (0, 0)) for n in names],
            out_specs=[pl.BlockSpec(s, lambda i, me_ref: (me_ref[0], 0)) for s in shapes]),
        out_shape=[SDS((N_DEV * s[0], s[1]), BF16) for s in shapes], compiler_params=_params(("arbitrary",)),
    )(me, *[shards[n] for n in names])
    return dict(zip(names, out))


def _mesh_pos():
    return lax.axis_index("x"), lax.axis_index("y"), lax.axis_index("c")


HBM_SPEC = pl.BlockSpec(memory_space=pltpu.HBM)
SEM_SPEC = pl.BlockSpec(memory_space=pltpu.SEMAPHORE)
ANY_SPEC = pl.BlockSpec(memory_space=pl.ANY)
SIDE_EFFECT = pltpu.SideEffectType.DATAFLOW_SIDE_EFFECTING
PEER_SEMS = pltpu.SemaphoreType.DMA((N_DEV - 1,))


def _in_hbm(a):
    return pltpu.with_memory_space_constraint(a, pltpu.HBM)


def _split_copy_start(name, items, copies):
    n = len(items)
    flat = [a for it in items for a in it]
    k = len(flat)

    def body(*refs):
        arr, sems = list(refs[:k]), refs[k:k + 2 * n]
        for i, it in enumerate(items):
            mine = [arr.pop(0) for _ in it]
            for m in range(1, N_DEV):
                cp = copies[i](m, mine, sems[i].at[m - 1], sems[n + i].at[m - 1])
                if cp is not None:
                    cp.start()
        token = refs[-1]
        token[...] = jnp.zeros_like(token)

    res = pl.pallas_call(
        body, name=name,
        out_shape=[PEER_SEMS] * (2 * n) + [pltpu.HBM(a.shape, a.dtype) for a in flat] + [SDS((8, LANES), F32)],
        in_specs=[HBM_SPEC] * k,
        out_specs=[SEM_SPEC] * (2 * n) + [HBM_SPEC] * k + [pl.BlockSpec(memory_space=pltpu.VMEM)],
        input_output_aliases={j: 2 * n + j for j in range(k)},
        compiler_params=pltpu.CompilerParams(has_side_effects=SIDE_EFFECT),
    )(*[_in_hbm(a) for a in flat])
    thru, out_items = list(res[2 * n:2 * n + k]), []
    for it in items:
        out_items.append(tuple(thru.pop(0) for _ in it))
    return dict(send=res[:n], recv=res[n:2 * n], items=out_items, token=res[-1], copies=copies)


def _split_copy_wait(name, started, which, after):
    items = [started["items"][i] for i in which]
    copies = [started["copies"][i] for i in which]
    n = len(items)
    flat = [a for it in items for a in it]
    k = len(flat)

    def body(*refs):
        arr, sems = list(refs[:k]), refs[k:k + 2 * n]
        for i, it in enumerate(items):
            mine = [arr.pop(0) for _ in it]
            for m in range(1, N_DEV):
                cp = copies[i](m, mine, sems[i].at[m - 1], sems[n + i].at[m - 1])
                if cp is not None:
                    cp.wait_send()
                    cp.wait_recv()

    res = pl.pallas_call(
        body, name=name,
        out_shape=[pltpu.HBM(a.shape, a.dtype) for a in flat],
        in_specs=[HBM_SPEC] * k + [SEM_SPEC] * (2 * n) + [ANY_SPEC] * len(after),
        out_specs=[HBM_SPEC] * k,
        input_output_aliases={j: j for j in range(k)},
        compiler_params=pltpu.CompilerParams(has_side_effects=SIDE_EFFECT),
    )(*flat, *[started["send"][i] for i in which], *[started["recv"][i] for i in which], *[_in_hbm(a) for a in after])
    thru, out_items = list(res), []
    for it in items:
        out_items.append(tuple(thru.pop(0) for _ in it))
    return out_items


def _gather_copy(m, refs, send_sem, recv_sem):
    (land_ref,) = refs
    r = land_ref.shape[0] // N_DEV
    mine = land_ref.at[pl.ds(pl.multiple_of(_peer_index(0) * r, 8), r), :]
    return pltpu.make_async_remote_copy(src_ref=mine, dst_ref=mine, send_sem=send_sem, recv_sem=recv_sem,
                                        device_id=_peer(m), device_id_type=MESH)


def _gather_copy_near(m, refs, send_sem, recv_sem):
    return _gather_copy(m, refs, send_sem, recv_sem) if m == 1 or m % 2 == 0 else None


def _gather_copy_pass(m, refs, send_sem, recv_sem):
    if m == 1 or m % 2 == 0:
        return None
    (land_ref,) = refs
    r = land_ref.shape[0] // N_DEV
    block = land_ref.at[pl.ds(pl.multiple_of(_peer_index(m ^ 1) * r, 8), r), :]
    return pltpu.make_async_remote_copy(src_ref=block, dst_ref=block, send_sem=send_sem, recv_sem=recv_sem,
                                        device_id=_peer(1), device_id_type=MESH)


def _small_copy(m, refs, send_sem, recv_sem):
    part_ref, land_ref = refs
    return pltpu.make_async_remote_copy(src_ref=part_ref, dst_ref=land_ref.at[m - 1], send_sem=send_sem,
                                        recv_sem=recv_sem, device_id=_peer(m), device_id_type=MESH)


def _scatter_copy(m, refs, send_sem, recv_sem):
    buf_ref, land_ref = refs
    r = buf_ref.shape[0] // N_DEV
    src = buf_ref.at[pl.ds(pl.multiple_of(_peer_index(m) * r, 8), r), :]
    return pltpu.make_async_remote_copy(src_ref=src, dst_ref=land_ref.at[m - 1], send_sem=send_sem,
                                        recv_sem=recv_sem, device_id=_peer(m), device_id_type=MESH)


def _peer(m):
    x, y, c = _mesh_pos()
    bx, by, bc = (m >> 2) & 1, (m >> 1) & 1, m & 1
    return (x ^ bx if bx else x, y ^ by if by else y, c ^ bc if bc else c)


def _peer_index(m):
    x, y, c = _mesh_pos()
    return (4 * x + 2 * y + c) ^ m


SMALL_PLACE = {
    "ln1_gain": (ROW_LN1G, 0), "ln1_bias": (ROW_LN1B, 0), "ln2_gain": (ROW_LN2G, 0), "ln2_bias": (ROW_LN2B, 0),
    "ret_gn_gain": (ROW_GN, 0), "ret_decay_fwd": (ROW_MISC, MISC_DF), "ret_decay_bwd": (ROW_MISC, MISC_DB),
    "attn_sink": (ROW_MISC, MISC_SINK)}


def _small_adamw(me, part, landed, w, m, v):
    d = part.shape[1]
    names = list(SMALL_PLACE)
    k = len(names)

    def body(*refs):
        me_ref, part_ref, land_ref = refs[:3]
        refs = refs[2:]
        w_refs, m_refs, v_refs = refs[1:1 + k], refs[1 + k:1 + 2 * k], refs[1 + 2 * k:1 + 3 * k]
        outs = refs[1 + 3 * k:1 + 7 * k + 1]
        tot_ref = refs[-1]
        loss_ref, g_refs, dl_refs = outs[0], outs[1:1 + k], outs[1 + k:1 + 2 * k]
        nm_refs, nv_refs = outs[1 + 2 * k:1 + 3 * k], outs[1 + 3 * k:1 + 4 * k]
        tot = jnp.zeros(part_ref.shape, F32)
        for dev in range(N_DEV):
            j = dev ^ me_ref[0]
            tot = tot + jnp.where(j == 0, part_ref[...], land_ref[jnp.maximum(j, 1) - 1])
        tot_ref[...] = tot
        loss_ref[...] = (0.5 / d) * jnp.sum(tot_ref[ROW_LOSS:ROW_LOSS + 1, :], axis=1, keepdims=True)
        for i, name in enumerate(names):
            row, lo = SMALL_PLACE[name]
            wv = w_refs[i][...]
            g = tot_ref[row:row + 1, lo:lo + wv.shape[1]]
            if name.startswith("ret_decay"):
                p2 = jnp.exp2(wv)
                g = g * (-p2 * jnp.log(2.0) / (1.0 - p2))
            g_refs[i][...] = g
            _adamw_store(g, wv, m_refs[i][...], v_refs[i][...], dl_refs[i], nm_refs[i], nv_refs[i])

    shapes = [SDS(w[n].shape, F32) for n in names]
    vm = pl.BlockSpec(memory_space=pltpu.VMEM)
    res = pl.pallas_call(
        body, name="small_adamw", out_shape=[SDS((1, 1), F32)] + shapes * 4,
        in_specs=[_smem_spec()] + [vm] * (2 + 3 * k), out_specs=[vm] * (1 + 4 * k),
        scratch_shapes=[pltpu.VMEM(part.shape, F32)],
    )(me, part, landed, *[w[n] for n in names], *[m[n] for n in names], *[v[n] for n in names])
    groups = [dict(zip(names, res[1 + j * k:1 + (j + 1) * k])) for j in range(4)]
    return (res[0], *groups)


def _adamw_store(g, w, m, v, dl_ref, nm_ref, nv_ref):
    m = ADAM_B1 * m + (1.0 - ADAM_B1) * g
    v = ADAM_B2 * v + (1.0 - ADAM_B2) * (g * g)
    m_hat = m / (1.0 - ADAM_B1 ** ADAM_STEP)
    v_hat = v / (1.0 - ADAM_B2 ** ADAM_STEP)
    dl_ref[...] = -ADAM_LR * (m_hat / (jnp.sqrt(v_hat) + ADAM_EPS) + ADAM_WD * w)
    nm_ref[...] = m
    nv_ref[...] = v


def _reduce_adamw(name, own, recv, w, m, v, transposed):
    rows, n = own.shape
    steps = 1 if transposed or rows % 32 else 4
    rb = rows // steps

    def body(own_ref, recv_ref, w_ref, m_ref, v_ref, g_ref, dl_ref, nm_ref, nv_ref):
        g = own_ref[...]
        for k in range(N_DEV - 1):
            g = g + recv_ref[k].astype(F32)
        if transposed:
            g = g.T
        g_ref[...] = g
        _adamw_store(g, w_ref[...], m_ref[...], v_ref[...], dl_ref, nm_ref, nv_ref)

    blk = pl.BlockSpec(w.shape if transposed else (rb, n), lambda i: (i, 0))
    out = SDS(w.shape, F32)
    return pl.pallas_call(
        body, name="adamw_" + name, grid=(steps,),
        in_specs=[pl.BlockSpec((rb, n), lambda i: (i, 0)), pl.BlockSpec((N_DEV - 1, rb, n), lambda i: (0, i, 0)),
                  blk, blk, blk],
        out_specs=[blk] * 4, out_shape=[out] * 4, compiler_params=_params(("parallel",)),
    )(own, recv, w, m, v)


def _row_spec(tm, width):
    return pl.BlockSpec((tm, width), lambda i: (i, 0))


def _full_spec(shape):
    return pl.BlockSpec(shape, lambda i: (0,) * len(shape))


_acc_spec = _full_spec


def _sub_rows(tm):
    step = min(SUB_ROWS, tm)
    return [(lo, lo + step) for lo in range(0, tm, step)]


def _in_proj(x2, wt_in):
    t, d = x2.shape
    u_w = wt_in.shape[0]
    tm = _row_tile(t, MATMUL_ROWS)

    def body(x_ref, w_ref, u_ref, xb_ref):
        xb = x_ref[...].astype(BF16)
        xb_ref[...] = xb
        u_ref[...] = _dot_nt(xb, w_ref[...]).astype(BF16)

    return pl.pallas_call(
        body, name="in_proj", grid=(t // tm,),
        in_specs=[_row_spec(tm, d), _full_spec(wt_in.shape)],
        out_specs=[_row_spec(tm, u_w), _row_spec(tm, d)],
        out_shape=[SDS((t, u_w), BF16), SDS((t, d), BF16)],
        compiler_params=_params(("parallel",)),
    )(x2, wt_in)


def _col_halves(f):
    n = f // LANES
    k = (n + 1) // 2 * LANES
    return [(0, k), (k, f)] if k < f else [(0, f)]


def _mix_ln1_ffn_up(r, a, x2, w_out, wt_gate, wt_up, g1, b1):
    t, d = x2.shape
    f = wt_gate.shape[0]
    tm = _row_tile(t, EPILOGUE_ROWS)

    def body(r_ref, a_ref, x_ref, wo_ref, wg_ref, wu_ref, g_ref, b_ref, z_ref, hb_ref, dg_ref, du_ref, act_ref):
        mix = _dot_nn(r_ref[...], wo_ref[0:RET_W, :]) + _dot_nn(a_ref[...], wo_ref[RET_W:RET_W + ATT_W, :])
        z = ALPHA * x_ref[...] + mix
        xhat, _ = _layer_norm_stats(z)
        z_ref[...] = z
        h = (xhat * g_ref[...] + b_ref[...]).astype(BF16)
        hb_ref[...] = h
        for lo, hi in _col_halves(f):
            g = _dot_nt(h, wg_ref[lo:hi, :])
            u = _dot_nt(h, wu_ref[lo:hi, :])
            sg = _sigmoid(g)
            silu = g * sg
            dg_ref[:, lo:hi] = (u * (sg * (1.0 + g * (1.0 - sg)))).astype(BF16)
            du_ref[:, lo:hi] = silu.astype(BF16)
            act_ref[:, lo:hi] = (silu * u).astype(BF16)

    wide, narrow = _row_spec(tm, f), _row_spec(tm, d)
    return pl.pallas_call(
        body, name="mix_ln1_ffn_up", grid=(t // tm,),
        in_specs=[_row_spec(tm, RET_W), _row_spec(tm, ATT_W), narrow, _resident_spec(w_out.shape),
                  _resident_spec(wt_gate.shape), _resident_spec(wt_up.shape), _full_spec(g1.shape),
                  _full_spec(b1.shape)],
        out_specs=[narrow, narrow, wide, wide, wide],
        out_shape=[SDS((t, d), F32), SDS((t, d), BF16)] + [SDS((t, f), BF16)] * 3,
        compiler_params=_params(("parallel",)),
    )(r, a, x2, w_out, wt_gate, wt_up, g1, b1)


def _ffn_down_ln2_loss(act, dact_dg, dact_du, h1b, p2, z1, target, w_down, w_pg, wt_pe, g1, b1, g2, b2):
    t, d = z1.shape
    f = act.shape[1]
    pdim = p2.shape[1]
    tm = _row_tile(t, EPILOGUE_ROWS)

    def body(act_ref, fg_ref, fu_ref, hb_ref, p_ref, z1_ref, tgt_ref, wd_ref, wpg_ref, wpe_ref, g1_ref, b1_ref,
             g2_ref, b2_ref, dz_ref, dzb_ref, ds_ref, dple_ref, dg_ref, du_ref, acc_ref):
        @pl.when(pl.program_id(0) == 0)
        def _():
            acc_ref[...] = jnp.zeros_like(acc_ref)

        for lo, hi in _sub_rows(tm):
            xhat1, _ = _layer_norm_stats(z1_ref[lo:hi, :])
            h1 = xhat1 * g1_ref[...] + b1_ref[...]
            ffn = _dot_nn(act_ref[lo:hi, :], wd_ref[...])
            pg = _sigmoid(_dot_nn(hb_ref[lo:hi, :], wpg_ref[...]))
            ple = _dot_nt(p_ref[lo:hi, :].astype(BF16), wpe_ref[...])
            z2 = ALPHA * h1 + ffn + pg * ple
            xhat2, rstd2 = _layer_norm_stats(z2)
            err = xhat2 * g2_ref[...] + b2_ref[...] - tgt_ref[lo:hi, :]
            dy = err * (1.0 / d)
            dz = _layer_norm_bwd(dy * g2_ref[...], xhat2, rstd2)
            dzb = dz.astype(BF16)
            dz_ref[lo:hi, :] = dz
            dzb_ref[lo:hi, :] = dzb
            ds_ref[lo:hi, :] = (dz * ple * pg * (1.0 - pg)).astype(BF16)
            dple_ref[lo:hi, :] = (dz * pg).astype(BF16)
            acc_ref[0:1, :] += jnp.sum(err * err, axis=0, keepdims=True)
            acc_ref[1:2, :] += jnp.sum(dy * xhat2, axis=0, keepdims=True)
            acc_ref[2:3, :] += jnp.sum(dy, axis=0, keepdims=True)
            for c0, c1 in _col_halves(f):
                da = _dot_nt(dzb, wd_ref[c0:c1, :])
                dg_ref[lo:hi, c0:c1] = (da * fg_ref[lo:hi, c0:c1].astype(F32)).astype(BF16)
                du_ref[lo:hi, c0:c1] = (da * fu_ref[lo:hi, c0:c1].astype(F32)).astype(BF16)

    vec = _full_spec(g1.shape)
    wide, narrow = _row_spec(tm, f), _row_spec(tm, d)
    return pl.pallas_call(
        body, name="ffn_down_ln2_loss", grid=(t // tm,),
        in_specs=[wide, wide, wide, narrow, _row_spec(tm, pdim), narrow, narrow,
                  _full_spec(w_down.shape), _full_spec(w_pg.shape), _full_spec(wt_pe.shape), vec, vec, vec, vec],
        out_specs=[narrow] * 4 + [wide, wide, _acc_spec((8, d))],
        out_shape=[SDS((t, d), F32), SDS((t, d), BF16), SDS((t, d), BF16), SDS((t, d), BF16),
                   SDS((t, f), BF16), SDS((t, f), BF16), SDS((8, d), F32)],
        compiler_params=_params(("arbitrary",)),
    )(act, dact_dg, dact_du, h1b, p2, z1, target, w_down, w_pg, wt_pe, g1, b1, g2, b2)


def _after(after, body):
    k = len(after)
    return (lambda *refs: body(*refs[k:])), [ANY_SPEC] * k


def _resident_spec(shape):
    return pl.BlockSpec(shape, lambda i: (0,) * len(shape), pipeline_mode=pl.Buffered(1))


def _dh1_ln1_bwd(dz2, dg, dup, dsb, z1, wt_gate, wt_up, w_pg, w_out, g1, after=()):
    t, d = dz2.shape
    f = dg.shape[1]
    tm = _row_tile(t, EPILOGUE_ROWS)

    def body(dz_ref, dg_ref, du_ref, ds_ref, z1_ref, wg_ref, wu_ref, wpg_ref, wo_ref, g1_ref,
             dz1_ref, dz1b_ref, dr_ref, da_ref, acc_ref):
        @pl.when(pl.program_id(0) == 0)
        def _():
            acc_ref[...] = jnp.zeros_like(acc_ref)

        for lo, hi in _sub_rows(tm):
            dh = (ALPHA * dz_ref[lo:hi, :] + _dot_nn(dg_ref[lo:hi, :], wg_ref[...])
                  + _dot_nn(du_ref[lo:hi, :], wu_ref[...]) + _dot_nt(ds_ref[lo:hi, :], wpg_ref[...]))
            xhat, rstd = _layer_norm_stats(z1_ref[lo:hi, :])
            dz1 = _layer_norm_bwd(dh * g1_ref[...], xhat, rstd)
            dz1b = dz1.astype(BF16)
            dz1_ref[lo:hi, :] = dz1
            dz1b_ref[lo:hi, :] = dz1b
            acc_ref[0:1, :] += jnp.sum(dh * xhat, axis=0, keepdims=True)
            acc_ref[1:2, :] += jnp.sum(dh, axis=0, keepdims=True)
            dr_ref[lo:hi, :] = _dot_nt(dz1b, wo_ref[0:RET_W, :]).astype(BF16)
            da_ref[lo:hi, :] = _dot_nt(dz1b, wo_ref[RET_W:RET_W + ATT_W, :]).astype(BF16)

    body, lead = _after(after, body)
    return pl.pallas_call(
        body, name="dh1_ln1_bwd", grid=(t // tm,),
        in_specs=lead + [_row_spec(tm, d), _row_spec(tm, f), _row_spec(tm, f), _row_spec(tm, d), _row_spec(tm, d),
                         _resident_spec(wt_gate.shape), _resident_spec(wt_up.shape), _resident_spec(w_pg.shape),
                         _resident_spec(w_out.shape), _full_spec(g1.shape)],
        out_specs=[_row_spec(tm, d), _row_spec(tm, d), _row_spec(tm, RET_W), _row_spec(tm, ATT_W), _acc_spec((8, d))],
        out_shape=[SDS((t, d), F32), SDS((t, d), BF16), SDS((t, RET_W), BF16), SDS((t, ATT_W), BF16),
                   SDS((8, d), F32)],
        compiler_params=_params(("arbitrary",)),
    )(*after, dz2, dg, dup, dsb, z1, wt_gate, wt_up, w_pg, w_out, g1)


def _in_proj_bwd(dz1, parts, wt_in, after=()):
    t, d = dz1.shape
    tm = _row_tile(t, MATMUL_ROWS)
    widths = [p.shape[1] for p in parts]

    def body(*refs):
        dz_ref, part_refs, w_ref, dx_ref = refs[0], refs[1:1 + len(parts)], refs[-2], refs[-1]
        acc = ALPHA * dz_ref[...]
        lo = 0
        for p_ref, w in zip(part_refs, widths):
            acc = acc + _dot_nn(p_ref[...], w_ref[lo:lo + w, :])
            lo += w
        dx_ref[...] = acc

    body, lead = _after(after, body)
    return pl.pallas_call(
        body, name="in_proj_bwd", grid=(t // tm,),
        in_specs=lead + [_row_spec(tm, d)] + [_row_spec(tm, w) for w in widths] + [_full_spec(wt_in.shape)],
        out_specs=_row_spec(tm, d), out_shape=SDS((t, d), F32),
        compiler_params=_params(("parallel",)),
    )(*after, dz1, *parts, wt_in)


def _weight_grad(name, me, parts, rhs, after=()):
    t, n = rhs.shape
    widths = [p.shape[1] for p in parts]
    rows = sum(widths)
    own_rows = rows // N_DEV
    tk = _row_tile(t, MATMUL_ROWS)
    step = 256

    def body(*refs):
        me_ref, part_refs, rhs_ref = refs[0], refs[1:1 + len(parts)], refs[1 + len(parts)]
        full_ref, own_ref, acc = refs[-3], refs[-2], refs[-1]
        i = pl.program_id(0)

        @pl.when(i == 0)
        def _():
            acc[...] = jnp.zeros_like(acc)

        b = rhs_ref[...].astype(BF16)
        lo = 0
        for p_ref, w in zip(part_refs, widths):
            for c0 in range(0, w, step):
                c1 = min(c0 + step, w)
                acc[lo + c0:lo + c1, :] += _dot_tn(p_ref[:, c0:c1].astype(BF16), b)
            lo += w

        @pl.when(i == pl.num_programs(0) - 1)
        def _():
            full_ref[...] = acc[...].astype(BF16)
            own_ref[...] = acc[pl.ds(pl.multiple_of(me_ref[0] * own_rows, 8), own_rows), :]

    body, lead = _after(after, body)
    return pl.pallas_call(
        body, name=name, grid=(t // tk,),
        in_specs=lead + [_smem_spec()] + [_row_spec(tk, w) for w in widths] + [_row_spec(tk, n)],
        out_specs=[_full_spec((rows, n)), _full_spec((own_rows, n))],
        out_shape=[SDS((rows, n), BF16), SDS((own_rows, n), F32)],
        scratch_shapes=[pltpu.VMEM((rows, n), F32)],
        compiler_params=_params(("arbitrary",)),
    )(*after, me, *parts, rhs)


def _log_decay(decay_f, decay_b):
    def body(f_ref, b_ref, lf_ref, lb_ref):
        lf_ref[...] = jnp.log1p(-jnp.exp2(f_ref[...]))
        lb_ref[...] = jnp.log1p(-jnp.exp2(b_ref[...]))

    return pl.pallas_call(body, name="log_decay", out_shape=[SDS(decay_f.shape, F32)] * 2)(decay_f, decay_b)


def _chunk(ref, n):
    return ref[pl.ds(pl.multiple_of(n * CHUNK, CHUNK), CHUNK), :]


def _group_sum(is_a, v):
    sa = jnp.sum(jnp.where(is_a, v, 0.0), axis=1, keepdims=True)
    sb = jnp.sum(jnp.where(is_a, 0.0, v), axis=1, keepdims=True)
    return jnp.where(is_a, sa, sb)


def _seq_spec(s, col_block):
    return pl.BlockSpec((s, LANES), lambda b, h: (b, col_block + h))


def _smem_spec():
    return pl.BlockSpec(memory_space=pltpu.SMEM)


RET_UNROLL = 4


def _chunk_loop(n_chunks, body, init):
    u = RET_UNROLL if n_chunks % RET_UNROLL == 0 else 1

    def trip(i, carry):
        for j in range(u):
            carry = body(i * u + j, carry)
        return carry

    return lax.fori_loop(0, n_chunks // u, trip, init)


def _stacked_tables(lgf_ref, lgb_ref, pair):
    lane = lax.broadcasted_iota(jnp.int32, (1, LANES), 1)
    is_a = lane < HEAD_DIM
    lgf = jnp.where(is_a, lgf_ref[2 * pair], lgf_ref[2 * pair + 1])
    lgb = jnp.where(is_a, lgb_ref[2 * pair], lgb_ref[2 * pair + 1])
    row = lax.broadcasted_iota(jnp.int32, (CHUNK, 1), 0).astype(F32)
    kdec_f, qdec_f = jnp.exp(lgf * (CHUNK - 1.0 - row)), jnp.exp(lgf * (row + 1.0))
    kdec_b, qdec_b = jnp.exp(lgb * row), jnp.exp(lgb * (CHUNK - row))
    tab = dict(
        is_a=is_a, row=row, lam_f=jnp.exp(lgf * CHUNK), lam_b=jnp.exp(lgb * CHUNK),
        kdec=jnp.concatenate([kdec_f, kdec_b], axis=1), qdec=jnp.concatenate([qdec_f, qdec_b], axis=1),
        qexp=jnp.concatenate([jnp.broadcast_to(row + 1.0, (CHUNK, LANES)),
                              jnp.broadcast_to(CHUNK - row, (CHUNK, LANES))], axis=1),
        kexp=jnp.concatenate([jnp.broadcast_to(CHUNK - 1.0 - row, (CHUNK, LANES)),
                              jnp.broadcast_to(row, (CHUNK, LANES))], axis=1),
    )
    r = lax.broadcasted_iota(jnp.int32, (2 * LANES, LANES), 0)
    c = lax.broadcasted_iota(jnp.int32, (2 * LANES, LANES), 1)
    tab["diag2"] = ((r & (LANES - 1)) < HEAD_DIM) == (c < HEAD_DIM)
    i2 = lax.broadcasted_iota(jnp.int32, (2 * CHUNK, CHUNK), 0)
    j = lax.broadcasted_iota(jnp.int32, (2 * CHUNK, CHUNK), 1)
    head_b = i2 >= CHUNK
    diff = ((i2 & (CHUNK - 1)) - j).astype(F32)
    up, dn = jnp.maximum(diff, 0.0), jnp.maximum(-diff, 0.0)
    lgf2 = jnp.where(head_b, lgf_ref[2 * pair + 1], lgf_ref[2 * pair])
    lgb2 = jnp.where(head_b, lgb_ref[2 * pair + 1], lgb_ref[2 * pair])
    ef = jnp.where(diff >= 0, jnp.exp(lgf2 * up), 0.0)
    eb = jnp.where(diff <= 0, jnp.exp(lgb2 * dn), 0.0)
    tab["d2"] = ef + eb
    tab["df2"] = ef * up
    tab["db2"] = eb * dn
    return tab


def _stack_pair(is_a, x):
    zero = jnp.zeros_like(x)
    return jnp.concatenate([jnp.where(is_a, x, zero), jnp.where(is_a, zero, x)], axis=0)


def _unstack_pair(is_a, x2):
    return jnp.where(is_a, x2[0:CHUNK, :], x2[CHUNK:2 * CHUNK, :])


def _both_ways(x, dec):
    return (jnp.concatenate([x, x], axis=1) * dec).astype(BF16)


def _scan_states(n_chunks, st, up_rows, up_lam, down_rows, down_lam):
    zero = jnp.zeros((LANES, LANES), F32)

    def up(n, r):
        new = st[n, up_rows, :]
        st[n, up_rows, :] = r
        return r * up_lam + new

    def down(s, r):
        n = n_chunks - 1 - s
        new = st[n, down_rows, :]
        st[n, down_rows, :] = r
        return r * down_lam + new

    lax.fori_loop(0, n_chunks, up, zero)
    lax.fori_loop(0, n_chunks, down, zero)


FWD_ROWS, BWD_ROWS = pl.ds(0, LANES), pl.ds(LANES, LANES)


ST_GAIN, ST_XF, ST_XB, ST_IFA, ST_IFB, ST_IBA, ST_IBB, ST_LF, ST_LB = 0, 1, 2, 3, 4, 5, 6, 8, 9
ST_ROWS = 16


GW = GROUP * HEAD_DIM
KEYS = 3 * BLOCK


def _attn_tables(g, bias_ref):
    r = lax.broadcasted_iota(jnp.int32, (GROUP * BLOCK, KEYS), 0)
    kj = lax.broadcasted_iota(jnp.int32, (GROUP * BLOCK, KEYS), 1)
    qi = r & (BLOCK - 1)
    hh = lax.shift_right_logical(r, 7)
    dist = jnp.abs(kj - BLOCK - qi)
    slope = jnp.exp2(-(GROUP * g + hh + 1).astype(F32) * (8.0 / ATTN_HEADS))
    bias_ref[...] = jnp.where(dist <= BLOCK, -slope * dist.astype(F32), NEG_INF)


def _own_lanes(g):
    return lax.shift_right_logical(lax.broadcasted_iota(jnp.int32, (1, LANES), 1), 6) == g


def _mask_keys(x_ref, g, scale, pad_ref, s):
    pad_ref[0:BLOCK, :] = jnp.zeros((BLOCK, LANES), BF16)
    pad_ref[BLOCK + s:2 * BLOCK + s, :] = jnp.zeros((BLOCK, LANES), BF16)
    pad_ref[BLOCK:BLOCK + s, :] = jnp.where(_own_lanes(g), x_ref[...].astype(F32) * scale, 0.0).astype(BF16)


def _lane_block(x, j):
    return x[:, j * LANES:(j + 1) * LANES]


def _stack_heads(x, g):
    assert GROUP == 4 and GW == 2 * LANES
    x1 = pltpu.roll(x, HEAD_DIM, 1)
    keep = _own_lanes(g)
    zero = jnp.zeros((BLOCK, LANES), x.dtype)
    rows = []
    for h in range(GROUP):
        for_g0 = _lane_block(x, h // 2) if h % 2 == 0 else _lane_block(x1, ((h + 1) // 2) % 2)
        for_g1 = _lane_block(x, h // 2) if h % 2 == 1 else _lane_block(x1, h // 2)
        rows.append(jnp.where(keep, jnp.where(g == 0, for_g0, for_g1), zero))
    return jnp.concatenate(rows, axis=0)


def _unstack_heads(x4, g):
    p = [x4[h * BLOCK:(h + 1) * BLOCK, :] for h in range(GROUP)]
    cat = lambda a, b: jnp.concatenate([a, b], axis=1)
    in_place = jnp.where(g == 0, cat(p[0], p[2]), cat(p[1], p[3]))
    one_left = jnp.where(g == 0, cat(p[1], p[3]), cat(p[2], p[0]))
    return in_place + pltpu.roll(one_left, HEAD_DIM, 1)


def _sink_column(sink_ref, g):
    rh = lax.shift_right_logical(lax.broadcasted_iota(jnp.int32, (GROUP * BLOCK, 1), 0), 7)
    col = jnp.zeros((GROUP * BLOCK, 1), F32)
    for h in range(GROUP):
        col = jnp.where(rh == h, sink_ref[GROUP * g + h], col)
    return col


def _attn_probs(qm, k3, bias_ref, sink_col, n, s):
    logits = _dot_nt(qm, k3) + bias_ref[...]
    kpos = n * BLOCK - BLOCK + lax.broadcasted_iota(jnp.int32, (1, KEYS), 1)
    logits = jnp.where((kpos >= 0) & (kpos < s), logits, NEG_INF)
    m = jnp.maximum(jnp.max(logits, axis=1, keepdims=True), sink_col)
    e = jnp.exp(logits - m)
    e_sink = jnp.exp(sink_col - m)
    inv = 1.0 / (jnp.sum(e, axis=1, keepdims=True) + e_sink)
    return e * inv, e_sink * inv


PAIRS_PER_KV = (RET_HEADS // 2) // KV_HEADS


def _mixers_fwd(u, lgf, lgb, gn_gain, sink, b_loc, after=()):
    t = u.shape[0]
    s = t // b_loc
    n_chunks = s // CHUNK
    pairs = RET_HEADS // 2
    trips = n_chunks // RET_UNROLL
    blocks_half = (s // BLOCK) // PAIRS_PER_KV
    per_trip = blocks_half // trips
    assert n_chunks % RET_UNROLL == 0 and blocks_half % trips == 0 and PAIRS_PER_KV == 2

    def body(lgf_ref, lgb_ref, sink_ref, q_ref, k_ref, v_ref, g_ref, gain_ref, aq_ref, ak_ref, av_ref,
             r_ref, xhat_ref, rstd_ref, a_ref, st, kpad, vpad, bias):
        pair = pl.program_id(1)
        g, half = lax.shift_right_logical(pair, 1), pair & 1
        tab = _stacked_tables(lgf_ref, lgb_ref, pair)
        is_a = tab["is_a"]

        @pl.when(half == 0)
        def _():
            _attn_tables(g, bias)
            _mask_keys(ak_ref, g, Q_SCALE, kpad, s)
            _mask_keys(av_ref, g, 1.0, vpad, s)

        def kv_body(n, _):
            k8 = _chunk(k_ref, n).astype(F32) * Q_SCALE
            st[n] = jnp.where(tab["diag2"], _dot_tn(_both_ways(k8, tab["kdec"]), _chunk(v_ref, n)), 0.0)
            return 0

        _chunk_loop(n_chunks, kv_body, 0)
        _scan_states(n_chunks, st, FWD_ROWS, tab["lam_f"], BWD_ROWS, tab["lam_b"])
        sink_col = _sink_column(sink_ref, g)

        def retention_chunk(n):
            q = _chunk(q_ref, n)
            k8 = (_chunk(k_ref, n).astype(F32) * Q_SCALE).astype(BF16)
            v = _chunk(v_ref, n)
            p2 = (_dot_nt(_stack_pair(is_a, q), k8) * tab["d2"]).astype(BF16)
            y = _unstack_pair(is_a, _dot_nn(p2, v))
            y = y + _dot_nn(_both_ways(q.astype(F32), tab["qdec"]), st[n].astype(BF16))
            rows = pl.ds(pl.multiple_of(n * CHUNK, CHUNK), CHUNK)
            mu = _group_sum(is_a, y) * (1.0 / HEAD_DIM)
            dlt = y - mu
            var = _group_sum(is_a, dlt * dlt) * (1.0 / HEAD_DIM)
            rstd = lax.rsqrt(var + GN_EPS)
            xhat = dlt * rstd
            xhat_ref[rows, :] = xhat
            rstd_ref[rows, :] = rstd
            gate = _chunk(g_ref, n).astype(F32)
            r_ref[rows, :] = (xhat * gain_ref[...] * gate * _sigmoid(gate)).astype(BF16)

        def attention_block(blk):
            n = half * blocks_half + blk
            rows = pl.ds(pl.multiple_of(blk * BLOCK, BLOCK), BLOCK)
            keys = pl.ds(pl.multiple_of(n * BLOCK, BLOCK), KEYS)
            p, _ = _attn_probs(_stack_heads(aq_ref[rows, :], g), kpad[keys, :], bias, sink_col, n, s)
            a_ref[rows, :] = _unstack_heads(_dot_nn(p.astype(BF16), vpad[keys, :]), g).astype(BF16)

        def trip(i, _):
            for j in range(max(RET_UNROLL, per_trip)):
                if j < RET_UNROLL:
                    retention_chunk(i * RET_UNROLL + j)
                if j < per_trip:
                    attention_block(i * per_trip + j)
            return 0

        lax.fori_loop(0, trips, trip, 0)

    lane_blk = lambda c0: _seq_spec(s, c0 // LANES)
    half_rows = blocks_half * BLOCK
    aq_spec = pl.BlockSpec((half_rows, GW), lambda b, h: (b * PAIRS_PER_KV + (h & 1), C_AQ // GW + h // 2))
    a_spec = pl.BlockSpec((half_rows, GW), lambda b, h: (b * PAIRS_PER_KV + (h & 1), h // 2))
    kv_spec = lambda c0: pl.BlockSpec((s, LANES), lambda b, h: (b, c0 // LANES))
    pad = pltpu.VMEM((s + 2 * BLOCK, LANES), BF16)
    body, lead = _after(after, body)
    return pl.pallas_call(
        body, name="mixers_fwd", grid=(b_loc, pairs),
        in_specs=lead + [_smem_spec(), _smem_spec(), _smem_spec(), lane_blk(C_RQ), lane_blk(C_RK), lane_blk(C_RV),
                         lane_blk(C_RG), pl.BlockSpec((1, LANES), lambda b, h: (0, h)), aq_spec, kv_spec(C_AK),
                         kv_spec(C_AV)],
        out_specs=[_seq_spec(s, 0), _seq_spec(s, 0), _seq_spec(s, 0), a_spec],
        out_shape=[SDS((t, RET_W), BF16), SDS((t, RET_W), F32), SDS((t, RET_W), F32), SDS((t, ATT_W), BF16)],
        scratch_shapes=[pltpu.VMEM((n_chunks, 2 * LANES, LANES), F32), pad, pad,
                        pltpu.VMEM((GROUP * BLOCK, KEYS), F32)],
        compiler_params=_params(("arbitrary", "arbitrary")),
    )(*after, lgf, lgb, sink, u, u, u, u, gn_gain, u, u, u)


def _mixers_bwd(u, xhat, rstd, dr, da, lgf, lgb, gn_gain, sink, b_loc, after=()):
    t = u.shape[0]
    s = t // b_loc
    n_chunks = s // CHUNK
    pairs = RET_HEADS // 2
    trips = n_chunks // RET_UNROLL
    blocks_half = (s // BLOCK) // PAIRS_PER_KV
    per_trip = blocks_half // trips
    assert n_chunks % RET_UNROLL == 0 and blocks_half % trips == 0 and PAIRS_PER_KV == 2

    def body(lgf_ref, lgb_ref, sink_ref, q_ref, k_ref, v_ref, g_ref, xhat_ref, rstd_ref, dr_ref, gain_ref,
             aq_ref, ak_ref, av_ref, do_ref,
             dq_ref, dk_ref, dv_ref, dg_ref, st_ref, daq_ref, dak_ref, dav_ref, dsink_ref,
             st, gr, dy_s, kpad, vpad, bias, dk_acc, dv_acc):
        pair = pl.program_id(1)
        g, half = lax.shift_right_logical(pair, 1), pair & 1
        tab = _stacked_tables(lgf_ref, lgb_ref, pair)
        is_a = tab["is_a"]
        gain = gain_ref[...]

        @pl.when(half == 0)
        def _():
            _attn_tables(g, bias)
            _mask_keys(ak_ref, g, Q_SCALE, kpad, s)
            _mask_keys(av_ref, g, 1.0, vpad, s)
            dsink_ref[...] = jnp.zeros_like(dsink_ref)

        @pl.when(pair == 0)
        def _():
            dk_acc[...] = jnp.zeros_like(dk_acc)
            dv_acc[...] = jnp.zeros_like(dv_acc)

        def norm_body(n, dgain):
            rows = pl.ds(pl.multiple_of(n * CHUNK, CHUNK), CHUNK)
            xhat, rstd = xhat_ref[rows, :], rstd_ref[rows, :]
            gate = g_ref[rows, :].astype(F32)
            sg = _sigmoid(gate)
            silu = gate * sg
            d_out = dr_ref[rows, :].astype(F32)
            dg_ref[rows, :] = (d_out * xhat * gain * (sg * (1.0 + gate * (1.0 - sg)))).astype(BF16)
            dxh = d_out * gain * silu
            m1 = _group_sum(is_a, dxh) * (1.0 / HEAD_DIM)
            m2 = _group_sum(is_a, dxh * xhat) * (1.0 / HEAD_DIM)
            dy = (rstd * (dxh - m1 - xhat * m2)).astype(BF16)
            dy_s[rows, :] = dy
            k8 = k_ref[rows, :].astype(F32) * Q_SCALE
            st[n] = jnp.where(tab["diag2"], _dot_tn(_both_ways(k8, tab["kdec"]), v_ref[rows, :]), 0.0)
            qf = q_ref[rows, :].astype(F32)
            gr[n] = jnp.where(tab["diag2"], _dot_tn(_both_ways(qf, tab["qdec"]), dy), 0.0)
            return dgain + jnp.sum(d_out * xhat * silu, axis=0, keepdims=True)

        colsum = lambda x: jnp.sum(x, axis=0, keepdims=True)

        def grad_body(n, carry):
            xfb, ifa, ifb, iba, ibb, lf, lb = carry
            rows = pl.ds(pl.multiple_of(n * CHUNK, CHUNK), CHUNK)
            q = q_ref[rows, :]
            qf = q.astype(F32)
            k8f = k_ref[rows, :].astype(F32) * Q_SCALE
            k8 = k8f.astype(BF16)
            v = v_ref[rows, :]
            dy = dy_s[rows, :]
            q2, dy2 = _stack_pair(is_a, q), _stack_pair(is_a, dy)
            sc = _dot_nt(q2, k8)
            dp = _dot_nt(dy2, v)
            a2 = (sc * tab["d2"]).astype(BF16)
            ds2 = (dp * tab["d2"]).astype(BF16)
            dq = _unstack_pair(is_a, _dot_nn(ds2, k8))
            dk = _dot_tn(ds2, q2)
            dv = _dot_tn(a2, dy2)
            prod = sc * dp
            pf, pb = prod * tab["df2"], prod * tab["db2"]
            ifa, ifb = ifa + colsum(pf[0:CHUNK, :]), ifb + colsum(pf[CHUNK:2 * CHUNK, :])
            iba, ibb = iba + colsum(pb[0:CHUNK, :]), ibb + colsum(pb[CHUNK:2 * CHUNK, :])
            states, sgrads = st[n], gr[n]
            sb, gb = states.astype(BF16), sgrads.astype(BF16)
            dqc = _dot_nt(dy, sb) * tab["qdec"]
            dkc = _dot_nt(v, gb) * tab["kdec"]
            dv = dv + _dot_nn(_both_ways(k8f, tab["kdec"]), gb)
            dq_ref[rows, :] = (dq + dqc[:, 0:LANES] + dqc[:, LANES:2 * LANES]).astype(BF16)
            dk_ref[rows, :] = ((dk + dkc[:, 0:LANES] + dkc[:, LANES:2 * LANES]) * Q_SCALE).astype(BF16)
            dv_ref[rows, :] = dv.astype(BF16)
            q2w, k2w = jnp.concatenate([qf, qf], axis=1), jnp.concatenate([k8f, k8f], axis=1)
            xfb = xfb + colsum(tab["qexp"] * q2w * dqc + tab["kexp"] * k2w * dkc)
            prod_s = sgrads * states
            lf, lb = lf + colsum(prod_s[0:LANES, :]), lb + colsum(prod_s[LANES:2 * LANES, :])
            return xfb, ifa, ifb, iba, ibb, lf, lb

        sink_col = _sink_column(sink_ref, g)
        head_row = lax.broadcasted_iota(jnp.int32, dsink_ref.shape, 0)

        def attention_block(blk):
            n = half * blocks_half + blk
            rows = pl.ds(pl.multiple_of(blk * BLOCK, BLOCK), BLOCK)
            keys = pl.ds(pl.multiple_of(n * BLOCK, BLOCK), KEYS)
            qm = _stack_heads(aq_ref[rows, :], g)
            k3, v3 = kpad[keys, :], vpad[keys, :]
            p, p_sink = _attn_probs(qm, k3, bias, sink_col, n, s)
            dom = _stack_heads(do_ref[rows, :], g)
            dp = _dot_nt(dom, v3)
            delta = jnp.sum(p * dp, axis=1, keepdims=True)
            ds_mat = (p * (dp - delta)).astype(BF16)
            daq_ref[rows, :] = _unstack_heads(_dot_nn(ds_mat, k3), g).astype(BF16)
            dk_acc[keys, :] += _dot_tn(ds_mat, qm) * Q_SCALE
            dv_acc[keys, :] += _dot_tn(p.astype(BF16), dom)
            w = p_sink * delta
            upd = jnp.zeros(dsink_ref.shape, F32)
            for h in range(GROUP):
                upd = upd + jnp.where(head_row == h, -jnp.sum(w[h * BLOCK:(h + 1) * BLOCK, :]), 0.0)
            dsink_ref[...] += upd

        dgain = _chunk_loop(n_chunks, norm_body, jnp.zeros((1, LANES), F32))
        _scan_states(n_chunks, st, FWD_ROWS, tab["lam_f"], BWD_ROWS, tab["lam_b"])
        _scan_states(n_chunks, gr, BWD_ROWS, tab["lam_b"], FWD_ROWS, tab["lam_f"])

        def trip(i, carry):
            for j in range(max(RET_UNROLL, per_trip)):
                if j < RET_UNROLL:
                    carry = grad_body(i * RET_UNROLL + j, carry)
                if j < per_trip:
                    attention_block(i * per_trip + j)
            return carry

        z = jnp.zeros((1, LANES), F32)
        init = (jnp.zeros((1, 2 * LANES), F32), z, z, z, z, z, z)
        xfb, ifa, ifb, iba, ibb, lf, lb = lax.fori_loop(0, trips, trip, init)
        st_ref[...] = jnp.zeros_like(st_ref)
        st_ref[ST_GAIN:ST_GAIN + 1, :] = dgain
        st_ref[ST_XF:ST_XF + 1, :] = xfb[:, 0:LANES]
        st_ref[ST_XB:ST_XB + 1, :] = xfb[:, LANES:2 * LANES]
        st_ref[ST_IFA:ST_IFA + 1, :] = ifa
        st_ref[ST_IFB:ST_IFB + 1, :] = ifb
        st_ref[ST_IBA:ST_IBA + 1, :] = iba
        st_ref[ST_IBB:ST_IBB + 1, :] = ibb
        st_ref[ST_LF:ST_LF + 1, :] = lf * (CHUNK * tab["lam_f"])
        st_ref[ST_LB:ST_LB + 1, :] = lb * (CHUNK * tab["lam_b"])

        @pl.when(pair == pairs - 1)
        def _():
            dak_ref[...] = dk_acc[BLOCK:BLOCK + s, :].astype(BF16)
            dav_ref[...] = dv_acc[BLOCK:BLOCK + s, :].astype(BF16)

    lane_blk = lambda c0: _seq_spec(s, c0 // LANES)
    seq0 = _seq_spec(s, 0)
    half_rows = blocks_half * BLOCK
    aq_spec = pl.BlockSpec((half_rows, GW), lambda b, h: (b * PAIRS_PER_KV + (h & 1), C_AQ // GW + h // 2))
    a_spec = pl.BlockSpec((half_rows, GW), lambda b, h: (b * PAIRS_PER_KV + (h & 1), h // 2))
    kv_spec = lambda c0: pl.BlockSpec((s, LANES), lambda b, h: (b, c0 // LANES))
    kv_out = pl.BlockSpec((s, LANES), lambda b, h: (b, 0))
    state = pltpu.VMEM((n_chunks, 2 * LANES, LANES), F32)
    pad = pltpu.VMEM((s + 2 * BLOCK, LANES), BF16)
    acc = pltpu.VMEM((s + 2 * BLOCK, LANES), F32)
    body, lead = _after(after, body)
    return pl.pallas_call(
        body, name="mixers_bwd", grid=(b_loc, pairs),
        in_specs=lead + [_smem_spec(), _smem_spec(), _smem_spec(), lane_blk(C_RQ), lane_blk(C_RK), lane_blk(C_RV),
                         lane_blk(C_RG), seq0, seq0, seq0, pl.BlockSpec((1, LANES), lambda b, h: (0, h)),
                         aq_spec, kv_spec(C_AK), kv_spec(C_AV), a_spec],
        out_specs=[seq0] * 4 + [pl.BlockSpec((ST_ROWS, LANES), lambda b, h: (b, h)), a_spec, kv_out, kv_out,
                                pl.BlockSpec((8, LANES), lambda b, h: (b * KV_HEADS + h // 2, 0))],
        out_shape=[SDS((t, RET_W), BF16)] * 4 + [SDS((b_loc * ST_ROWS, RET_W), F32), SDS((t, ATT_W), BF16),
                                                   SDS((t, KV_W), BF16), SDS((t, KV_W), BF16),
                                                   SDS((b_loc * KV_HEADS * 8, LANES), F32)],
        scratch_shapes=[state, state, pltpu.VMEM((s, LANES), BF16), pad, pad,
                        pltpu.VMEM((GROUP * BLOCK, KEYS), F32), acc, acc],
        compiler_params=_params(("arbitrary", "arbitrary")),
    )(*after, lgf, lgb, sink, u, u, u, u, xhat, rstd, dr, gn_gain, u, u, u, da)


def _pack_small(acc2, acc1, ret_stats, dsink, b_loc, d):
    pairs = RET_HEADS // 2

    def body(acc2_ref, acc1_ref, st_ref, dsink_ref, out_ref):
        out_ref[...] = jnp.zeros_like(out_ref)
        out_ref[ROW_LN1G:ROW_LN1G + 1, :] = acc1_ref[0:1, :]
        out_ref[ROW_LN1B:ROW_LN1B + 1, :] = acc1_ref[1:2, :]
        out_ref[ROW_LN2G:ROW_LN2G + 1, :] = acc2_ref[1:2, :]
        out_ref[ROW_LN2B:ROW_LN2B + 1, :] = acc2_ref[2:3, :]
        out_ref[ROW_LOSS:ROW_LOSS + 1, :] = acc2_ref[0:1, :]
        st = st_ref[0:ST_ROWS, :]
        for b in range(1, b_loc):
            st = st + st_ref[b * ST_ROWS:(b + 1) * ST_ROWS, :]
        out_ref[ROW_GN:ROW_GN + 1, 0:RET_W] = st[ST_GAIN:ST_GAIN + 1, :]
        lane = lax.broadcasted_iota(jnp.int32, (1, d), 1)
        misc = jnp.zeros((1, d), F32)
        for pr in range(pairs):
            blk = st[:, pr * LANES:(pr + 1) * LANES]
            half = lax.broadcasted_iota(jnp.int32, (1, LANES), 1) < HEAD_DIM
            for h in range(2):
                sel = half if h == 0 else jnp.logical_not(half)
                cross_f = jnp.sum(jnp.where(sel, blk[ST_XF:ST_XF + 1, :] + blk[ST_LF:ST_LF + 1, :], 0.0))
                cross_b = jnp.sum(jnp.where(sel, blk[ST_XB:ST_XB + 1, :] + blk[ST_LB:ST_LB + 1, :], 0.0))
                intra_f = jnp.sum(blk[ST_IFA + h:ST_IFA + h + 1, :])
                intra_b = jnp.sum(blk[ST_IBA + h:ST_IBA + h + 1, :])
                head = 2 * pr + h
                misc = jnp.where(lane == MISC_DF + head, cross_f + intra_f, misc)
                misc = jnp.where(lane == MISC_DB + head, cross_b + intra_b, misc)
        for g in range(KV_HEADS):
            tot = dsink_ref[g * 8:(g + 1) * 8, :]
            for b in range(1, b_loc):
                tot = tot + dsink_ref[(b * KV_HEADS + g) * 8:(b * KV_HEADS + g + 1) * 8, :]
            for h in range(GROUP):
                misc = jnp.where(lane == MISC_SINK + GROUP * g + h, jnp.sum(tot[h:h + 1, 0:1]), misc)
        out_ref[ROW_MISC:ROW_MISC + 1, :] = misc

    return pl.pallas_call(body, name="pack_small", out_shape=SDS((SMALL_ROWS, d), F32))(acc2, acc1, ret_stats, dsink)


BIG = ("w_in", "w_out", "w_ffn_gate", "w_ffn_up", "w_ffn_down", "w_ple_proj", "w_ple_gate")
TRANSPOSED_OUTSIDE = ("w_in", "w_ffn_gate", "w_ffn_up")
TRANSPOSED_HERE = ("w_ple_proj",)
SMALL = ("ret_decay_fwd", "ret_decay_bwd", "ret_gn_gain", "attn_sink", "ln1_gain", "ln1_bias", "ln2_gain", "ln2_bias")
ORDER = ("w_in", "ret_decay_fwd", "ret_decay_bwd", "ret_gn_gain", "attn_sink", "w_out", "ln1_gain", "ln1_bias",
         "w_ffn_gate", "w_ffn_up", "w_ffn_down", "w_ple_proj", "w_ple_gate", "ln2_gain", "ln2_bias")


GATHER_ORDER = ("w_in", "w_ffn_up", "w_out", "w_ffn_gate", "w_ple_gate", "w_ple_proj", "w_ffn_down")
GATHER_TWO_LEVEL = ("w_in", "w_ffn_up")


def _local_step(x2, p2, target2, fetch, publish, small, b_loc, me):
    d = x2.shape[1]
    lgf, lgb = _log_decay(small["ret_decay_fwd"], small["ret_decay_bwd"])
    lgf1, lgb1, sink1 = lgf.reshape(-1), lgb.reshape(-1), small["attn_sink"].reshape(-1)
    (w_in,) = fetch(("w_in",), ())
    u, xb = _in_proj(x2, w_in)
    passed = fetch.pass_on("w_ffn_up", (xb,))
    r, ret_xhat, ret_rstd, a = _mixers_fwd(u, lgf1, lgb1, small["ret_gn_gain"], sink1, b_loc, passed)
    w_out, w_gate, w_up = fetch(("w_out", "w_ffn_gate", "w_ffn_up"), (r, a))
    z1, h1b, dact_dg, dact_du, act = _mix_ln1_ffn_up(
        r, a, x2, w_out, w_gate, w_up, small["ln1_gain"], small["ln1_bias"])
    w_pg, w_pe, w_down = fetch(("w_ple_gate", "w_ple_proj", "w_ffn_down"), (act,))
    dz2, dz2b, dsb, dpleb, dg, dup, acc2 = _ffn_down_ln2_loss(
        act, dact_dg, dact_du, h1b, p2, z1, target2, w_down, w_pg, w_pe,
        small["ln1_gain"], small["ln1_bias"], small["ln2_gain"], small["ln2_bias"])
    own = {}

    def grad(name, parts, rhs, after=()):
        whole, own[name] = _weight_grad("grad_" + name, me, parts, rhs, after)
        return whole

    t2 = publish("ffn", dict(w_ffn_down=grad("w_ffn_down", [act], dz2b),
                             w_ple_proj=grad("w_ple_proj", [dpleb], p2),
                             w_ple_gate=grad("w_ple_gate", [h1b], dsb),
                             w_ffn_gate=grad("w_ffn_gate", [dg], h1b),
                             w_ffn_up=grad("w_ffn_up", [dup], h1b)))
    dz1, dz1b, dr, da, acc1 = _dh1_ln1_bwd(dz2, dg, dup, dsb, z1, w_gate, w_up, w_pg, w_out, small["ln1_gain"], t2)
    t3 = publish("out", dict(w_out=grad("w_out", [r, a], dz1b)))
    dq, dk, dv, dgate, ret_stats, daq, dak, dav, dsink = _mixers_bwd(
        u, ret_xhat, ret_rstd, dr, da, lgf1, lgb1, small["ret_gn_gain"], sink1, b_loc, t3)
    parts = [dq, dk, dv, dgate, daq, dak, dav]
    small_part = _pack_small(acc2, acc1, ret_stats, dsink, b_loc, d)
    t4 = publish("in", dict(w_in=grad("w_in", parts, xb)), small_part)
    grad_x = _in_proj_bwd(dz1, parts, w_in, t4)
    return grad_x, own, small_part


def kernel(x, p, w_in, ret_decay_fwd, ret_decay_bwd, ret_gn_gain, attn_sink, w_out, ln1_gain, ln1_bias, w_ffn_gate, w_ffn_up, w_ffn_down, w_ple_proj, w_ple_gate, ln2_gain, ln2_bias, loss_target, m_w_in, m_ret_decay_fwd, m_ret_decay_bwd, m_ret_gn_gain, m_attn_sink, m_w_out, m_ln1_gain, m_ln1_bias, m_w_ffn_gate, m_w_ffn_up, m_w_ffn_down, m_w_ple_proj, m_w_ple_gate, m_ln2_gain, m_ln2_bias, v_w_in, v_ret_decay_fwd, v_ret_decay_bwd, v_ret_gn_gain, v_attn_sink, v_w_out, v_ln1_gain, v_ln1_bias, v_w_ffn_gate, v_w_ffn_up, v_w_ffn_down, v_w_ple_proj, v_w_ple_gate, v_ln2_gain, v_ln2_bias):
    given = dict(locals())

    def strip(n, a):
        if n not in BIG:
            return a
        return a[0].T if n in TRANSPOSED_OUTSIDE else a[0]

    def restore(n, a):
        if n not in BIG:
            return a
        return (a.T if n in TRANSPOSED_OUTSIDE else a)[None]

    w = {n: strip(n, given[n]) for n in ORDER}
    m = {n: strip(n, given["m_" + n]) for n in ORDER}
    v = {n: strip(n, given["v_" + n]) for n in ORDER}
    b_loc, s, d = x.shape
    x2 = x.reshape(b_loc * s, d)
    p2 = p[0].reshape(b_loc * s, p.shape[-1])
    target2 = loss_target.reshape(b_loc * s, d)

    small = {n: w[n] for n in SMALL}
    me = (4 * lax.axis_index("x") + 2 * lax.axis_index("y") + lax.axis_index("c")).astype(jnp.int32).reshape(1)

    gathered = _prep_shards(me, {n: w[n] for n in BIG})
    gather = _split_copy_start(
        "gather_start", [(gathered[n],) for n in GATHER_ORDER],
        [_gather_copy_near if n in GATHER_TWO_LEVEL else _gather_copy for n in GATHER_ORDER])

    passing = {}

    def pass_on(n, after):
        near = _split_copy_wait("gather_wait_" + n + "_near", gather, [GATHER_ORDER.index(n)], list(after))
        passing[n] = _split_copy_start("gather_pass_" + n, near, [_gather_copy_pass])
        return (passing[n]["token"],)

    def fetch(names, after):
        out = {}
        for n in [n for n in names if n in GATHER_TWO_LEVEL]:
            if n not in passing:
                pass_on(n, after)
            out[n] = _split_copy_wait("gather_wait_" + n, passing[n], [0], list(after))[0][0]
        direct = [n for n in names if n not in GATHER_TWO_LEVEL]
        if direct:
            got = _split_copy_wait("gather_wait_" + direct[0], gather, [GATHER_ORDER.index(n) for n in direct],
                                   list(after))
            out.update({n: item[0] for n, item in zip(direct, got)})
        return [out[n] for n in names]

    scatters = []

    def publish(tag, products, small_sums=None):
        items = [(products[n], lax.empty((N_DEV - 1, products[n].shape[0] // N_DEV, products[n].shape[1]), BF16))
                 for n in products]
        copies = [_scatter_copy] * len(items)
        if small_sums is not None:
            items.append((small_sums, lax.empty((N_DEV - 1,) + small_sums.shape, F32)))
            copies.append(_small_copy)
        started = _split_copy_start("scatter_start_" + tag, items, copies)
        scatters.append((list(products), small_sums is not None, started))
        return (started["token"],)

    fetch.pass_on = pass_on
    grad_x, own, small_part = _local_step(x2, p2, target2, fetch, publish, small, b_loc, me)

    out_g, out_d, out_m, out_v = {}, {}, {}, {}
    after = [grad_x]
    for names, with_small, started in scatters:
        landed = _split_copy_wait("scatter_wait_" + names[0], started, list(range(len(started["items"]))), after)
        if with_small:
            loss, sg, sd, sm, sv = _small_adamw(
                me, small_part, landed[-1][1], small, {n: m[n] for n in SMALL}, {n: v[n] for n in SMALL})
            for dst, src in ((out_g, sg), (out_d, sd), (out_m, sm), (out_v, sv)):
                dst.update(src)
        for n, (_, recv) in zip(names, landed):
            out_g[n], out_d[n], out_m[n], out_v[n] = _reduce_adamw(
                n, own[n], recv, w[n], m[n], v[n], n in TRANSPOSED_HERE)
        after = [out_v[names[-1]]]

    outs = [loss[0, 0], grad_x.reshape(x.shape)]
    for group in (out_g, out_d, out_m, out_v):
        outs += [restore(n, group[n]) for n in ORDER]
    return tuple(outs)
```

```python
import functools

import jax
import jax.numpy as jnp
from jax import lax
from jax.experimental import pallas as pl
from jax.experimental.pallas import tpu as pltpu

F32, BF16 = jnp.float32, jnp.bfloat16
SDS = jax.ShapeDtypeStruct
MESH = pl.DeviceIdType.MESH

N_DEV = 8
HEAD_DIM = 64
RET_HEADS = 8
ATTN_HEADS = 8
KV_HEADS = 2
GROUP = ATTN_HEADS // KV_HEADS
RET_W = RET_HEADS * HEAD_DIM
ATT_W = ATTN_HEADS * HEAD_DIM
KV_W = KV_HEADS * HEAD_DIM
LANES = 128
CHUNK = 128
BLOCK = 128
Q_SCALE = HEAD_DIM ** -0.5
ALPHA = 2.0 ** 0.25
LN_EPS = 1e-5
GN_EPS = 1e-5
NEG_INF = -1e30
C_RQ, C_RK, C_RV, C_RG = 0, RET_W, 2 * RET_W, 3 * RET_W
C_AQ = 4 * RET_W
C_AK = C_AQ + ATT_W
C_AV = C_AK + KV_W
IN_W = C_AV + KV_W

ADAM_LR = 0.001
ADAM_B1 = 0.9
ADAM_B2 = 0.999
ADAM_EPS = 1e-08
ADAM_WD = 0.01
ADAM_STEP = 10

VMEM_LIMIT = 56 * 1024 * 1024
MATMUL_ROWS = 512
EPILOGUE_ROWS = 256
SUB_ROWS = 256
SMALL_ROWS = 16
ROW_LN1G, ROW_LN1B, ROW_LN2G, ROW_LN2B, ROW_LOSS, ROW_GN, ROW_MISC = 0, 1, 2, 3, 4, 5, 6
MISC_DF, MISC_DB, MISC_SINK = 0, 8, 16


def _dot_nn(a, b):
    return lax.dot_general(a, b, (((1,), (0,)), ((), ())), preferred_element_type=F32)


def _dot_nt(a, b):
    return lax.dot_general(a, b, (((1,), (1,)), ((), ())), preferred_element_type=F32)


def _dot_tn(a, b):
    return lax.dot_general(a, b, (((0,), (0,)), ((), ())), preferred_element_type=F32)


def _params(sem=None, vmem=VMEM_LIMIT):
    kw = {"vmem_limit_bytes": vmem}
    if sem is not None:
        kw["dimension_semantics"] = sem
    return pltpu.CompilerParams(**kw)


def _row_tile(t, want=512):
    tm = want
    while t % tm:
        tm //= 2
    return tm


def _sigmoid(x):
    return jax.nn.sigmoid(x)


def _layer_norm_stats(z):
    mu = jnp.mean(z, axis=1, keepdims=True)
    d = z - mu
    var = jnp.mean(d * d, axis=1, keepdims=True)
    rstd = lax.rsqrt(var + LN_EPS)
    return d * rstd, rstd


def _layer_norm_bwd(dxh, xhat, rstd):
    m1 = jnp.mean(dxh, axis=1, keepdims=True)
    m2 = jnp.mean(dxh * xhat, axis=1, keepdims=True)
    return rstd * (dxh - m1 - xhat * m2)


def _prep_shards(name, me, shards):
    names = list(shards)

    def body(me_ref, *refs):
        for name, src, dst in zip(names, refs[:len(names)], refs[len(names):]):
            val = src[...]
            dst[...] = (val.T if name in TRANSPOSED_HERE else val).astype(BF16)

    shape = lambda n, a: a.shape[::-1] if n in TRANSPOSED_HERE else a.shape
    shapes = [shape(n, shards[n]) for n in names]
    out = pl.pallas_call(
        body, name=name,
        grid_spec=pltpu.PrefetchScalarGridSpec(
            num_scalar_prefetch=1, grid=(1,),
            in_specs=[pl.BlockSpec(shards[n].shape, lambda i, me_ref: (0, 0)) for n in names],
            out_specs=[pl.BlockSpec(s, lambda i, me_ref: (me_ref[0], 0)) for s in shapes]),
        out_shape=[SDS((N_DEV * s[0], s[1]), BF16) for s in shapes], compiler_params=_params(("arbitrary",)),
    )(me, *[shards[n] for n in names])
    return dict(zip(names, out))


def _mesh_pos():
    return lax.axis_index("x"), lax.axis_index("y"), lax.axis_index("c")


HBM_SPEC = pl.BlockSpec(memory_space=pltpu.HBM)
SEM_SPEC = pl.BlockSpec(memory_space=pltpu.SEMAPHORE)
ANY_SPEC = pl.BlockSpec(memory_space=pl.ANY)
SIDE_EFFECT = pltpu.SideEffectType.DATAFLOW_SIDE_EFFECTING
PEER_SEMS = pltpu.SemaphoreType.DMA((N_DEV - 1,))


def _in_hbm(a):
    return pltpu.with_memory_space_constraint(a, pltpu.HBM)


def _split_copy_start(name, items, copies):
    n = len(items)
    flat = [a for it in items for a in it]
    k = len(flat)

    def body(*refs):
        arr, sems = list(refs[:k]), refs[k:k + 2 * n]
        for i, it in enumerate(items):
            mine = [arr.pop(0) for _ in it]
            for m in range(1, N_DEV):
                cp = copies[i](m, mine, sems[i].at[m - 1], sems[n + i].at[m - 1])
                if cp is not None:
                    cp.start()
        token = refs[-1]
        token[...] = jnp.zeros_like(token)

    res = pl.pallas_call(
        body, name=name,
        out_shape=[PEER_SEMS] * (2 * n) + [pltpu.HBM(a.shape, a.dtype) for a in flat] + [SDS((8, LANES), F32)],
        in_specs=[HBM_SPEC] * k,
        out_specs=[SEM_SPEC] * (2 * n) + [HBM_SPEC] * k + [pl.BlockSpec(memory_space=pltpu.VMEM)],
        input_output_aliases={j: 2 * n + j for j in range(k)},
        compiler_params=pltpu.CompilerParams(has_side_effects=SIDE_EFFECT),
    )(*[_in_hbm(a) for a in flat])
    thru, out_items = list(res[2 * n:2 * n + k]), []
    for it in items:
        out_items.append(tuple(thru.pop(0) for _ in it))
    return dict(send=res[:n], recv=res[n:2 * n], items=out_items, token=res[-1], copies=copies)


def _split_copy_wait(name, started, which, after):
    items = [started["items"][i] for i in which]
    copies = [started["copies"][i] for i in which]
    n = len(items)
    flat = [a for it in items for a in it]
    k = len(flat)

    def body(*refs):
        arr, sems = list(refs[:k]), refs[k:k + 2 * n]
        for i, it in enumerate(items):
            mine = [arr.pop(0) for _ in it]
            for m in range(1, N_DEV):
                cp = copies[i](m, mine, sems[i].at[m - 1], sems[n + i].at[m - 1])
                if cp is not None:
                    cp.wait_send()
                    cp.wait_recv()

    res = pl.pallas_call(
        body, name=name,
        out_shape=[pltpu.HBM(a.shape, a.dtype) for a in flat],
        in_specs=[HBM_SPEC] * k + [SEM_SPEC] * (2 * n) + [ANY_SPEC] * len(after),
        out_specs=[HBM_SPEC] * k,
        input_output_aliases={j: j for j in range(k)},
        compiler_params=pltpu.CompilerParams(has_side_effects=SIDE_EFFECT),
    )(*flat, *[started["send"][i] for i in which], *[started["recv"][i] for i in which], *[_in_hbm(a) for a in after])
    thru, out_items = list(res), []
    for it in items:
        out_items.append(tuple(thru.pop(0) for _ in it))
    return out_items


def _gather_copy(m, refs, send_sem, recv_sem):
    (land_ref,) = refs
    r = land_ref.shape[0] // N_DEV
    mine = land_ref.at[pl.ds(pl.multiple_of(_peer_index(0) * r, 8), r), :]
    return pltpu.make_async_remote_copy(src_ref=mine, dst_ref=mine, send_sem=send_sem, recv_sem=recv_sem,
                                        device_id=_peer(m), device_id_type=MESH)


def _gather_copy_near(m, refs, send_sem, recv_sem):
    return _gather_copy(m, refs, send_sem, recv_sem) if m == 1 or m % 2 == 0 else None


def _gather_copy_pass(m, refs, send_sem, recv_sem):
    if m == 1 or m % 2 == 0:
        return None
    (land_ref,) = refs
    r = land_ref.shape[0] // N_DEV
    block = land_ref.at[pl.ds(pl.multiple_of(_peer_index(m ^ 1) * r, 8), r), :]
    return pltpu.make_async_remote_copy(src_ref=block, dst_ref=block, send_sem=send_sem, recv_sem=recv_sem,
                                        device_id=_peer(1), device_id_type=MESH)


def _small_copy(m, refs, send_sem, recv_sem):
    part_ref, land_ref = refs
    return pltpu.make_async_remote_copy(src_ref=part_ref, dst_ref=land_ref.at[m - 1], send_sem=send_sem,
                                        recv_sem=recv_sem, device_id=_peer(m), device_id_type=MESH)


def _scatter_copy(m, refs, send_sem, recv_sem):
    buf_ref, land_ref = refs
    r = buf_ref.shape[0] // N_DEV
    src = buf_ref.at[pl.ds(pl.multiple_of(_peer_index(m) * r, 8), r), :]
    return pltpu.make_async_remote_copy(src_ref=src, dst_ref=land_ref.at[m - 1], send_sem=send_sem,
                                        recv_sem=recv_sem, device_id=_peer(m), device_id_type=MESH)


def _peer(m):
    x, y, c = _mesh_pos()
    bx, by, bc = (m >> 2) & 1, (m >> 1) & 1, m & 1
    return (x ^ bx if bx else x, y ^ by if by else y, c ^ bc if bc else c)


def _peer_index(m):
    x, y, c = _mesh_pos()
    return (4 * x + 2 * y + c) ^ m


SMALL_PLACE = {
    "ln1_gain": (ROW_LN1G, 0), "ln1_bias": (ROW_LN1B, 0), "ln2_gain": (ROW_LN2G, 0), "ln2_bias": (ROW_LN2B, 0),
    "ret_gn_gain": (ROW_GN, 0), "ret_decay_fwd": (ROW_MISC, MISC_DF), "ret_decay_bwd": (ROW_MISC, MISC_DB),
    "attn_sink": (ROW_MISC, MISC_SINK)}


def _small_adamw(me, part, landed, w, m, v):
    d = part.shape[1]
    names = list(SMALL_PLACE)
    k = len(names)

    def body(*refs):
        me_ref, part_ref, land_ref = refs[:3]
        refs = refs[2:]
        w_refs, m_refs, v_refs = refs[1:1 + k], refs[1 + k:1 + 2 * k], refs[1 + 2 * k:1 + 3 * k]
        outs = refs[1 + 3 * k:1 + 7 * k + 1]
        tot_ref = refs[-1]
        loss_ref, g_refs, dl_refs = outs[0], outs[1:1 + k], outs[1 + k:1 + 2 * k]
        nm_refs, nv_refs = outs[1 + 2 * k:1 + 3 * k], outs[1 + 3 * k:1 + 4 * k]
        tot = jnp.zeros(part_ref.shape, F32)
        for dev in range(N_DEV):
            j = dev ^ me_ref[0]
            tot = tot + jnp.where(j == 0, part_ref[...], land_ref[jnp.maximum(j, 1) - 1])
        tot_ref[...] = tot
        loss_ref[...] = (0.5 / d) * jnp.sum(tot_ref[ROW_LOSS:ROW_LOSS + 1, :], axis=1, keepdims=True)
        for i, name in enumerate(names):
            row, lo = SMALL_PLACE[name]
            wv = w_refs[i][...]
            g = tot_ref[row:row + 1, lo:lo + wv.shape[1]]
            if name.startswith("ret_decay"):
                p2 = jnp.exp2(wv)
                g = g * (-p2 * jnp.log(2.0) / (1.0 - p2))
            g_refs[i][...] = g
            _adamw_store(g, wv, m_refs[i][...], v_refs[i][...], dl_refs[i], nm_refs[i], nv_refs[i])

    shapes = [SDS(w[n].shape, F32) for n in names]
    vm = pl.BlockSpec(memory_space=pltpu.VMEM)
    res = pl.pallas_call(
        body, name="small_adamw", out_shape=[SDS((1, 1), F32)] + shapes * 4,
        in_specs=[_smem_spec()] + [vm] * (2 + 3 * k), out_specs=[vm] * (1 + 4 * k),
        scratch_shapes=[pltpu.VMEM(part.shape, F32)],
    )(me, part, landed, *[w[n] for n in names], *[m[n] for n in names], *[v[n] for n in names])
    groups = [dict(zip(names, res[1 + j * k:1 + (j + 1) * k])) for j in range(4)]
    return (res[0], *groups)


def _adamw_store(g, w, m, v, dl_ref, nm_ref, nv_ref):
    m = ADAM_B1 * m + (1.0 - ADAM_B1) * g
    v = ADAM_B2 * v + (1.0 - ADAM_B2) * (g * g)
    m_hat = m / (1.0 - ADAM_B1 ** ADAM_STEP)
    v_hat = v / (1.0 - ADAM_B2 ** ADAM_STEP)
    dl_ref[...] = -ADAM_LR * (m_hat / (jnp.sqrt(v_hat) + ADAM_EPS) + ADAM_WD * w)
    nm_ref[...] = m
    nv_ref[...] = v


def _reduce_adamw(name, own, recv, w, m, v, transposed):
    rows, n = own.shape
    steps = 1 if transposed or rows % 32 else 4
    rb = rows // steps

    def body(own_ref, recv_ref, w_ref, m_ref, v_ref, g_ref, dl_ref, nm_ref, nv_ref):
        g = own_ref[...]
        for k in range(N_DEV - 1):
            g = g + recv_ref[k].astype(F32)
        if transposed:
            g = g.T
        g_ref[...] = g
        _adamw_store(g, w_ref[...], m_ref[...], v_ref[...], dl_ref, nm_ref, nv_ref)

    blk = pl.BlockSpec(w.shape if transposed else (rb, n), lambda i: (i, 0))
    out = SDS(w.shape, F32)
    return pl.pallas_call(
        body, name="adamw_" + name, grid=(steps,),
        in_specs=[pl.BlockSpec((rb, n), lambda i: (i, 0)), pl.BlockSpec((N_DEV - 1, rb, n), lambda i: (0, i, 0)),
                  blk, blk, blk],
        out_specs=[blk] * 4, out_shape=[out] * 4, compiler_params=_params(("parallel",)),
    )(own, recv, w, m, v)


def _row_spec(tm, width):
    return pl.BlockSpec((tm, width), lambda i: (i, 0))


def _full_spec(shape):
    return pl.BlockSpec(shape, lambda i: (0,) * len(shape))


_acc_spec = _full_spec


def _sub_rows(tm):
    step = min(SUB_ROWS, tm)
    return [(lo, lo + step) for lo in range(0, tm, step)]


def _in_proj(x2, wt_in):
    t, d = x2.shape
    u_w = wt_in.shape[0]
    tm = _row_tile(t, MATMUL_ROWS)

    def body(x_ref, w_ref, u_ref, xb_ref):
        xb = x_ref[...].astype(BF16)
        xb_ref[...] = xb
        u_ref[...] = _dot_nt(xb, w_ref[...]).astype(BF16)

    return pl.pallas_call(
        body, name="in_proj", grid=(t // tm,),
        in_specs=[_row_spec(tm, d), _full_spec(wt_in.shape)],
        out_specs=[_row_spec(tm, u_w), _row_spec(tm, d)],
        out_shape=[SDS((t, u_w), BF16), SDS((t, d), BF16)],
        compiler_params=_params(("parallel",)),
    )(x2, wt_in)


def _col_halves(f):
    n = f // LANES
    k = (n + 1) // 2 * LANES
    return [(0, k), (k, f)] if k < f else [(0, f)]


def _mix_ln1_ffn_up(r, a, x2, w_out, wt_gate, wt_up, g1, b1):
    t, d = x2.shape
    f = wt_gate.shape[0]
    tm = _row_tile(t, EPILOGUE_ROWS)

    def body(r_ref, a_ref, x_ref, wo_ref, wg_ref, wu_ref, g_ref, b_ref, z_ref, hb_ref, dg_ref, du_ref, act_ref):
        mix = _dot_nn(r_ref[...], wo_ref[0:RET_W, :]) + _dot_nn(a_ref[...], wo_ref[RET_W:RET_W + ATT_W, :])
        z = ALPHA * x_ref[...] + mix
        xhat, _ = _layer_norm_stats(z)
        z_ref[...] = z
        h = (xhat * g_ref[...] + b_ref[...]).astype(BF16)
        hb_ref[...] = h
        for lo, hi in _col_halves(f):
            g = _dot_nt(h, wg_ref[lo:hi, :])
            u = _dot_nt(h, wu_ref[lo:hi, :])
            sg = _sigmoid(g)
            silu = g * sg
            dg_ref[:, lo:hi] = (u * (sg * (1.0 + g * (1.0 - sg)))).astype(BF16)
            du_ref[:, lo:hi] = silu.astype(BF16)
            act_ref[:, lo:hi] = (silu * u).astype(BF16)

    wide, narrow = _row_spec(tm, f), _row_spec(tm, d)
    return pl.pallas_call(
        body, name="mix_ln1_ffn_up", grid=(t // tm,),
        in_specs=[_row_spec(tm, RET_W), _row_spec(tm, ATT_W), narrow, _resident_spec(w_out.shape),
                  _resident_spec(wt_gate.shape), _resident_spec(wt_up.shape), _full_spec(g1.shape),
                  _full_spec(b1.shape)],
        out_specs=[narrow, narrow, wide, wide, wide],
        out_shape=[SDS((t, d), F32), SDS((t, d), BF16)] + [SDS((t, f), BF16)] * 3,
        compiler_params=_params(("parallel",)),
    )(r, a, x2, w_out, wt_gate, wt_up, g1, b1)


def _ffn_down_ln2_loss(act, dact_dg, dact_du, h1b, p2, z1, target, w_down, w_pg, wt_pe, g1, b1, g2, b2):
    t, d = z1.shape
    f = act.shape[1]
    pdim = p2.shape[1]
    tm = _row_tile(t, EPILOGUE_ROWS)

    def body(act_ref, fg_ref, fu_ref, hb_ref, p_ref, z1_ref, tgt_ref, wd_ref, wpg_ref, wpe_ref, g1_ref, b1_ref,
             g2_ref, b2_ref, dz_ref, dzb_ref, ds_ref, dple_ref, dg_ref, du_ref, acc_ref):
        @pl.when(pl.program_id(0) == 0)
        def _():
            acc_ref[...] = jnp.zeros_like(acc_ref)

        for lo, hi in _sub_rows(tm):
            xhat1, _ = _layer_norm_stats(z1_ref[lo:hi, :])
            h1 = xhat1 * g1_ref[...] + b1_ref[...]
            ffn = _dot_nn(act_ref[lo:hi, :], wd_ref[...])
            pg = _sigmoid(_dot_nn(hb_ref[lo:hi, :], wpg_ref[...]))
            ple = _dot_nt(p_ref[lo:hi, :].astype(BF16), wpe_ref[...])
            z2 = ALPHA * h1 + ffn + pg * ple
            xhat2, rstd2 = _layer_norm_stats(z2)
            err = xhat2 * g2_ref[...] + b2_ref[...] - tgt_ref[lo:hi, :]
            dy = err * (1.0 / d)
            dz = _layer_norm_bwd(dy * g2_ref[...], xhat2, rstd2)
            dzb = dz.astype(BF16)
            dz_ref[lo:hi, :] = dz
            dzb_ref[lo:hi, :] = dzb
            ds_ref[lo:hi, :] = (dz * ple * pg * (1.0 - pg)).astype(BF16)
            dple_ref[lo:hi, :] = (dz * pg).astype(BF16)
            acc_ref[0:1, :] += jnp.sum(err * err, axis=0, keepdims=True)
            acc_ref[1:2, :] += jnp.sum(dy * xhat2, axis=0, keepdims=True)
            acc_ref[2:3, :] += jnp.sum(dy, axis=0, keepdims=True)
            for c0, c1 in _col_halves(f):
                da = _dot_nt(dzb, wd_ref[c0:c1, :])
                dg_ref[lo:hi, c0:c1] = (da * fg_ref[lo:hi, c0:c1].astype(F32)).astype(BF16)
                du_ref[lo:hi, c0:c1] = (da * fu_ref[lo:hi, c0:c1].astype(F32)).astype(BF16)

    vec = _full_spec(g1.shape)
    wide, narrow = _row_spec(tm, f), _row_spec(tm, d)
    return pl.pallas_call(
        body, name="ffn_down_ln2_loss", grid=(t // tm,),
        in_specs=[wide, wide, wide, narrow, _row_spec(tm, pdim), narrow, narrow,
                  _full_spec(w_down.shape), _full_spec(w_pg.shape), _full_spec(wt_pe.shape), vec, vec, vec, vec],
        out_specs=[narrow] * 4 + [wide, wide, _acc_spec((8, d))],
        out_shape=[SDS((t, d), F32), SDS((t, d), BF16), SDS((t, d), BF16), SDS((t, d), BF16),
                   SDS((t, f), BF16), SDS((t, f), BF16), SDS((8, d), F32)],
        compiler_params=_params(("arbitrary",)),
    )(act, dact_dg, dact_du, h1b, p2, z1, target, w_down, w_pg, wt_pe, g1, b1, g2, b2)


def _after(after, body):
    k = len(after)
    return (lambda *refs: body(*refs[k:])), [ANY_SPEC] * k


def _resident_spec(shape):
    return pl.BlockSpec(shape, lambda i: (0,) * len(shape), pipeline_mode=pl.Buffered(1))


def _dh1_ln1_bwd(dz2, dg, dup, dsb, z1, wt_gate, wt_up, w_pg, w_out, g1, after=()):
    t, d = dz2.shape
    f = dg.shape[1]
    tm = _row_tile(t, EPILOGUE_ROWS)

    def body(dz_ref, dg_ref, du_ref, ds_ref, z1_ref, wg_ref, wu_ref, wpg_ref, wo_ref, g1_ref,
             dz1_ref, dz1b_ref, dr_ref, da_ref, acc_ref):
        @pl.when(pl.program_id(0) == 0)
        def _():
            acc_ref[...] = jnp.zeros_like(acc_ref)

        for lo, hi in _sub_rows(tm):
            dh = (ALPHA * dz_ref[lo:hi, :] + _dot_nn(dg_ref[lo:hi, :], wg_ref[...])
                  + _dot_nn(du_ref[lo:hi, :], wu_ref[...]) + _dot_nt(ds_ref[lo:hi, :], wpg_ref[...]))
            xhat, rstd = _layer_norm_stats(z1_ref[lo:hi, :])
            dz1 = _layer_norm_bwd(dh * g1_ref[...], xhat, rstd)
            dz1b = dz1.astype(BF16)
            dz1_ref[lo:hi, :] = dz1
            dz1b_ref[lo:hi, :] = dz1b
            acc_ref[0:1, :] += jnp.sum(dh * xhat, axis=0, keepdims=True)
            acc_ref[1:2, :] += jnp.sum(dh, axis=0, keepdims=True)
            dr_ref[lo:hi, :] = _dot_nt(dz1b, wo_ref[0:RET_W, :]).astype(BF16)
            da_ref[lo:hi, :] = _dot_nt(dz1b, wo_ref[RET_W:RET_W + ATT_W, :]).astype(BF16)

    body, lead = _after(after, body)
    return pl.pallas_call(
        body, name="dh1_ln1_bwd", grid=(t // tm,),
        in_specs=lead + [_row_spec(tm, d), _row_spec(tm, f), _row_spec(tm, f), _row_spec(tm, d), _row_spec(tm, d),
                         _resident_spec(wt_gate.shape), _resident_spec(wt_up.shape), _resident_spec(w_pg.shape),
                         _resident_spec(w_out.shape), _full_spec(g1.shape)],
        out_specs=[_row_spec(tm, d), _row_spec(tm, d), _row_spec(tm, RET_W), _row_spec(tm, ATT_W), _acc_spec((8, d))],
        out_shape=[SDS((t, d), F32), SDS((t, d), BF16), SDS((t, RET_W), BF16), SDS((t, ATT_W), BF16),
                   SDS((8, d), F32)],
        compiler_params=_params(("arbitrary",)),
    )(*after, dz2, dg, dup, dsb, z1, wt_gate, wt_up, w_pg, w_out, g1)


def _in_proj_bwd(dz1, parts, wt_in, after=()):
    t, d = dz1.shape
    tm = _row_tile(t, MATMUL_ROWS)
    widths = [p.shape[1] for p in parts]

    def body(*refs):
        dz_ref, part_refs, w_ref, dx_ref = refs[0], refs[1:1 + len(parts)], refs[-2], refs[-1]
        acc = ALPHA * dz_ref[...]
        lo = 0
        for p_ref, w in zip(part_refs, widths):
            acc = acc + _dot_nn(p_ref[...], w_ref[lo:lo + w, :])
            lo += w
        dx_ref[...] = acc

    body, lead = _after(after, body)
    return pl.pallas_call(
        body, name="in_proj_bwd", grid=(t // tm,),
        in_specs=lead + [_row_spec(tm, d)] + [_row_spec(tm, w) for w in widths] + [_full_spec(wt_in.shape)],
        out_specs=_row_spec(tm, d), out_shape=SDS((t, d), F32),
        compiler_params=_params(("parallel",)),
    )(*after, dz1, *parts, wt_in)


def _weight_grad(name, me, parts, rhs, after=()):
    t, n = rhs.shape
    widths = [p.shape[1] for p in parts]
    rows = sum(widths)
    own_rows = rows // N_DEV
    tk = _row_tile(t, MATMUL_ROWS)
    step = 256

    def body(*refs):
        me_ref, part_refs, rhs_ref = refs[0], refs[1:1 + len(parts)], refs[1 + len(parts)]
        full_ref, own_ref, acc = refs[-3], refs[-2], refs[-1]
        i = pl.program_id(0)

        @pl.when(i == 0)
        def _():
            acc[...] = jnp.zeros_like(acc)

        b = rhs_ref[...].astype(BF16)
        lo = 0
        for p_ref, w in zip(part_refs, widths):
            for c0 in range(0, w, step):
                c1 = min(c0 + step, w)
                acc[lo + c0:lo + c1, :] += _dot_tn(p_ref[:, c0:c1].astype(BF16), b)
            lo += w

        @pl.when(i == pl.num_programs(0) - 1)
        def _():
            full_ref[...] = acc[...].astype(BF16)
            own_ref[...] = acc[pl.ds(pl.multiple_of(me_ref[0] * own_rows, 8), own_rows), :]

    body, lead = _after(after, body)
    return pl.pallas_call(
        body, name=name, grid=(t // tk,),
        in_specs=lead + [_smem_spec()] + [_row_spec(tk, w) for w in widths] + [_row_spec(tk, n)],
        out_specs=[_full_spec((rows, n)), _full_spec((own_rows, n))],
        out_shape=[SDS((rows, n), BF16), SDS((own_rows, n), F32)],
        scratch_shapes=[pltpu.VMEM((rows, n), F32)],
        compiler_params=_params(("arbitrary",)),
    )(*after, me, *parts, rhs)


def _log_decay(decay_f, decay_b):
    def body(f_ref, b_ref, lf_ref, lb_ref):
        lf_ref[...] = jnp.log1p(-jnp.exp2(f_ref[...]))
        lb_ref[...] = jnp.log1p(-jnp.exp2(b_ref[...]))

    return pl.pallas_call(body, name="log_decay", out_shape=[SDS(decay_f.shape, F32)] * 2)(decay_f, decay_b)


def _chunk(ref, n):
    return ref[pl.ds(pl.multiple_of(n * CHUNK, CHUNK), CHUNK), :]


def _group_sum(is_a, v):
    sa = jnp.sum(jnp.where(is_a, v, 0.0), axis=1, keepdims=True)
    sb = jnp.sum(jnp.where(is_a, 0.0, v), axis=1, keepdims=True)
    return jnp.where(is_a, sa, sb)


def _seq_spec(s, col_block):
    return pl.BlockSpec((s, LANES), lambda b, h: (b, col_block + h))


def _smem_spec():
    return pl.BlockSpec(memory_space=pltpu.SMEM)


RET_UNROLL = 4


def _chunk_loop(n_chunks, body, init):
    u = RET_UNROLL if n_chunks % RET_UNROLL == 0 else 1

    def trip(i, carry):
        for j in range(u):
            carry = body(i * u + j, carry)
        return carry

    return lax.fori_loop(0, n_chunks // u, trip, init)


def _stacked_tables(lgf_ref, lgb_ref, pair):
    lane = lax.broadcasted_iota(jnp.int32, (1, LANES), 1)
    is_a = lane < HEAD_DIM
    lgf = jnp.where(is_a, lgf_ref[2 * pair], lgf_ref[2 * pair + 1])
    lgb = jnp.where(is_a, lgb_ref[2 * pair], lgb_ref[2 * pair + 1])
    row = lax.broadcasted_iota(jnp.int32, (CHUNK, 1), 0).astype(F32)
    kdec_f, qdec_f = jnp.exp(lgf * (CHUNK - 1.0 - row)), jnp.exp(lgf * (row + 1.0))
    kdec_b, qdec_b = jnp.exp(lgb * row), jnp.exp(lgb * (CHUNK - row))
    tab = dict(
        is_a=is_a, row=row, lam_f=jnp.exp(lgf * CHUNK), lam_b=jnp.exp(lgb * CHUNK),
        kdec=jnp.concatenate([kdec_f, kdec_b], axis=1), qdec=jnp.concatenate([qdec_f, qdec_b], axis=1),
        qexp=jnp.concatenate([jnp.broadcast_to(row + 1.0, (CHUNK, LANES)),
                              jnp.broadcast_to(CHUNK - row, (CHUNK, LANES))], axis=1),
        kexp=jnp.concatenate([jnp.broadcast_to(CHUNK - 1.0 - row, (CHUNK, LANES)),
                              jnp.broadcast_to(row, (CHUNK, LANES))], axis=1),
    )
    r = lax.broadcasted_iota(jnp.int32, (2 * LANES, LANES), 0)
    c = lax.broadcasted_iota(jnp.int32, (2 * LANES, LANES), 1)
    tab["diag2"] = ((r & (LANES - 1)) < HEAD_DIM) == (c < HEAD_DIM)
    i2 = lax.broadcasted_iota(jnp.int32, (2 * CHUNK, CHUNK), 0)
    j = lax.broadcasted_iota(jnp.int32, (2 * CHUNK, CHUNK), 1)
    head_b = i2 >= CHUNK
    diff = ((i2 & (CHUNK - 1)) - j).astype(F32)
    up, dn = jnp.maximum(diff, 0.0), jnp.maximum(-diff, 0.0)
    lgf2 = jnp.where(head_b, lgf_ref[2 * pair + 1], lgf_ref[2 * pair])
    lgb2 = jnp.where(head_b, lgb_ref[2 * pair + 1], lgb_ref[2 * pair])
    ef = jnp.where(diff >= 0, jnp.exp(lgf2 * up), 0.0)
    eb = jnp.where(diff <= 0, jnp.exp(lgb2 * dn), 0.0)
    tab["d2"] = ef + eb
    tab["df2"] = ef * up
    tab["db2"] = eb * dn
    return tab


def _stack_pair(is_a, x):
    zero = jnp.zeros_like(x)
    return jnp.concatenate([jnp.where(is_a, x, zero), jnp.where(is_a, zero, x)], axis=0)


def _unstack_pair(is_a, x2):
    return jnp.where(is_a, x2[0:CHUNK, :], x2[CHUNK:2 * CHUNK, :])


def _both_ways(x, dec):
    return (jnp.concatenate([x, x], axis=1) * dec).astype(BF16)


def _scan_states(n_chunks, st, up_rows, up_lam, down_rows, down_lam):
    zero = jnp.zeros((LANES, LANES), F32)

    def up(n, r):
        new = st[n, up_rows, :]
        st[n, up_rows, :] = r
        return r * up_lam + new

    def down(s, r):
        n = n_chunks - 1 - s
        new = st[n, down_rows, :]
        st[n, down_rows, :] = r
        return r * down_lam + new

    lax.fori_loop(0, n_chunks, up, zero)
    lax.fori_loop(0, n_chunks, down, zero)


FWD_ROWS, BWD_ROWS = pl.ds(0, LANES), pl.ds(LANES, LANES)


ST_GAIN, ST_XF, ST_XB, ST_IFA, ST_IFB, ST_IBA, ST_IBB, ST_LF, ST_LB = 0, 1, 2, 3, 4, 5, 6, 8, 9
ST_ROWS = 16


GW = GROUP * HEAD_DIM
KEYS = 3 * BLOCK


def _attn_tables(g, bias_ref):
    r = lax.broadcasted_iota(jnp.int32, (GROUP * BLOCK, KEYS), 0)
    kj = lax.broadcasted_iota(jnp.int32, (GROUP * BLOCK, KEYS), 1)
    qi = r & (BLOCK - 1)
    hh = lax.shift_right_logical(r, 7)
    dist = jnp.abs(kj - BLOCK - qi)
    slope = jnp.exp2(-(GROUP * g + hh + 1).astype(F32) * (8.0 / ATTN_HEADS))
    bias_ref[...] = jnp.where(dist <= BLOCK, -slope * dist.astype(F32), NEG_INF)


def _own_lanes(g):
    return lax.shift_right_logical(lax.broadcasted_iota(jnp.int32, (1, LANES), 1), 6) == g


def _mask_keys(x_ref, g, scale, pad_ref, s):
    pad_ref[0:BLOCK, :] = jnp.zeros((BLOCK, LANES), BF16)
    pad_ref[BLOCK + s:2 * BLOCK + s, :] = jnp.zeros((BLOCK, LANES), BF16)
    pad_ref[BLOCK:BLOCK + s, :] = jnp.where(_own_lanes(g), x_ref[...].astype(F32) * scale, 0.0).astype(BF16)


def _lane_block(x, j):
    return x[:, j * LANES:(j + 1) * LANES]


def _stack_heads(x, g):
    assert GROUP == 4 and GW == 2 * LANES
    x1 = pltpu.roll(x, HEAD_DIM, 1)
    keep = _own_lanes(g)
    zero = jnp.zeros((BLOCK, LANES), x.dtype)
    rows = []
    for h in range(GROUP):
        for_g0 = _lane_block(x, h // 2) if h % 2 == 0 else _lane_block(x1, ((h + 1) // 2) % 2)
        for_g1 = _lane_block(x, h // 2) if h % 2 == 1 else _lane_block(x1, h // 2)
        rows.append(jnp.where(keep, jnp.where(g == 0, for_g0, for_g1), zero))
    return jnp.concatenate(rows, axis=0)


def _unstack_heads(x4, g):
    p = [x4[h * BLOCK:(h + 1) * BLOCK, :] for h in range(GROUP)]
    cat = lambda a, b: jnp.concatenate([a, b], axis=1)
    in_place = jnp.where(g == 0, cat(p[0], p[2]), cat(p[1], p[3]))
    one_left = jnp.where(g == 0, cat(p[1], p[3]), cat(p[2], p[0]))
    return in_place + pltpu.roll(one_left, HEAD_DIM, 1)


def _sink_column(sink_ref, g):
    rh = lax.shift_right_logical(lax.broadcasted_iota(jnp.int32, (GROUP * BLOCK, 1), 0), 7)
    col = jnp.zeros((GROUP * BLOCK, 1), F32)
    for h in range(GROUP):
        col = jnp.where(rh == h, sink_ref[GROUP * g + h], col)
    return col


def _attn_probs(qm, k3, bias_ref, sink_col, n, s):
    logits = _dot_nt(qm, k3) + bias_ref[...]
    kpos = n * BLOCK - BLOCK + lax.broadcasted_iota(jnp.int32, (1, KEYS), 1)
    logits = jnp.where((kpos >= 0) & (kpos < s), logits, NEG_INF)
    m = jnp.maximum(jnp.max(logits, axis=1, keepdims=True), sink_col)
    e = jnp.exp(logits - m)
    e_sink = jnp.exp(sink_col - m)
    inv = 1.0 / (jnp.sum(e, axis=1, keepdims=True) + e_sink)
    return e * inv, e_sink * inv


PAIRS_PER_KV = (RET_HEADS // 2) // KV_HEADS


def _mixers_fwd(u, lgf, lgb, gn_gain, sink, b_loc, after=()):
    t = u.shape[0]
    s = t // b_loc
    n_chunks = s // CHUNK
    pairs = RET_HEADS // 2
    trips = n_chunks // RET_UNROLL
    blocks_half = (s // BLOCK) // PAIRS_PER_KV
    per_trip = blocks_half // trips
    assert n_chunks % RET_UNROLL == 0 and blocks_half % trips == 0 and PAIRS_PER_KV == 2

    def body(lgf_ref, lgb_ref, sink_ref, q_ref, k_ref, v_ref, g_ref, gain_ref, aq_ref, ak_ref, av_ref,
             r_ref, xhat_ref, rstd_ref, a_ref, st, kpad, vpad, bias):
        pair = pl.program_id(1)
        g, half = lax.shift_right_logical(pair, 1), pair & 1
        tab = _stacked_tables(lgf_ref, lgb_ref, pair)
        is_a = tab["is_a"]

        @pl.when(half == 0)
        def _():
            _attn_tables(g, bias)
            _mask_keys(ak_ref, g, Q_SCALE, kpad, s)
            _mask_keys(av_ref, g, 1.0, vpad, s)

        def kv_body(n, _):
            k8 = _chunk(k_ref, n).astype(F32) * Q_SCALE
            st[n] = jnp.where(tab["diag2"], _dot_tn(_both_ways(k8, tab["kdec"]), _chunk(v_ref, n)), 0.0)
            return 0

        _chunk_loop(n_chunks, kv_body, 0)
        _scan_states(n_chunks, st, FWD_ROWS, tab["lam_f"], BWD_ROWS, tab["lam_b"])
        sink_col = _sink_column(sink_ref, g)

        def retention_chunk(n):
            q = _chunk(q_ref, n)
            k8 = (_chunk(k_ref, n).astype(F32) * Q_SCALE).astype(BF16)
            v = _chunk(v_ref, n)
            p2 = (_dot_nt(_stack_pair(is_a, q), k8) * tab["d2"]).astype(BF16)
            y = _unstack_pair(is_a, _dot_nn(p2, v))
            y = y + _dot_nn(_both_ways(q.astype(F32), tab["qdec"]), st[n].astype(BF16))
            rows = pl.ds(pl.multiple_of(n * CHUNK, CHUNK), CHUNK)
            mu = _group_sum(is_a, y) * (1.0 / HEAD_DIM)
            dlt = y - mu
            var = _group_sum(is_a, dlt * dlt) * (1.0 / HEAD_DIM)
            rstd = lax.rsqrt(var + GN_EPS)
            xhat = dlt * rstd
            xhat_ref[rows, :] = xhat
            rstd_ref[rows, :] = rstd
            gate = _chunk(g_ref, n).astype(F32)
            r_ref[rows, :] = (xhat * gain_ref[...] * gate * _sigmoid(gate)).astype(BF16)

        def attention_block(blk):
            n = half * blocks_half + blk
            rows = pl.ds(pl.multiple_of(blk * BLOCK, BLOCK), BLOCK)
            keys = pl.ds(pl.multiple_of(n * BLOCK, BLOCK), KEYS)
            p, _ = _attn_probs(_stack_heads(aq_ref[rows, :], g), kpad[keys, :], bias, sink_col, n, s)
            a_ref[rows, :] = _unstack_heads(_dot_nn(p.astype(BF16), vpad[keys, :]), g).astype(BF16)

        def trip(i, _):
            for j in range(max(RET_UNROLL, per_trip)):
                if j < RET_UNROLL:
                    retention_chunk(i * RET_UNROLL + j)
                if j < per_trip:
                    attention_block(i * per_trip + j)
            return 0

        lax.fori_loop(0, trips, trip, 0)

    lane_blk = lambda c0: _seq_spec(s, c0 // LANES)
    half_rows = blocks_half * BLOCK
    aq_spec = pl.BlockSpec((half_rows, GW), lambda b, h: (b * PAIRS_PER_KV + (h & 1), C_AQ // GW + h // 2))
    a_spec = pl.BlockSpec((half_rows, GW), lambda b, h: (b * PAIRS_PER_KV + (h & 1), h // 2))
    kv_spec = lambda c0: pl.BlockSpec((s, LANES), lambda b, h: (b, c0 // LANES))
    pad = pltpu.VMEM((s + 2 * BLOCK, LANES), BF16)
    body, lead = _after(after, body)
    return pl.pallas_call(
        body, name="mixers_fwd", grid=(b_loc, pairs),
        in_specs=lead + [_smem_spec(), _smem_spec(), _smem_spec(), lane_blk(C_RQ), lane_blk(C_RK), lane_blk(C_RV),
                         lane_blk(C_RG), pl.BlockSpec((1, LANES), lambda b, h: (0, h)), aq_spec, kv_spec(C_AK),
                         kv_spec(C_AV)],
        out_specs=[_seq_spec(s, 0), _seq_spec(s, 0), _seq_spec(s, 0), a_spec],
        out_shape=[SDS((t, RET_W), BF16), SDS((t, RET_W), F32), SDS((t, RET_W), F32), SDS((t, ATT_W), BF16)],
        scratch_shapes=[pltpu.VMEM((n_chunks, 2 * LANES, LANES), F32), pad, pad,
                        pltpu.VMEM((GROUP * BLOCK, KEYS), F32)],
        compiler_params=_params(("arbitrary", "arbitrary")),
    )(*after, lgf, lgb, sink, u, u, u, u, gn_gain, u, u, u)


def _mixers_bwd(u, xhat, rstd, dr, da, lgf, lgb, gn_gain, sink, b_loc, after=()):
    t = u.shape[0]
    s = t // b_loc
    n_chunks = s // CHUNK
    pairs = RET_HEADS // 2
    trips = n_chunks // RET_UNROLL
    blocks_half = (s // BLOCK) // PAIRS_PER_KV
    per_trip = blocks_half // trips
    assert n_chunks % RET_UNROLL == 0 and blocks_half % trips == 0 and PAIRS_PER_KV == 2

    def body(lgf_ref, lgb_ref, sink_ref, q_ref, k_ref, v_ref, g_ref, xhat_ref, rstd_ref, dr_ref, gain_ref,
             aq_ref, ak_ref, av_ref, do_ref,
             dq_ref, dk_ref, dv_ref, dg_ref, st_ref, daq_ref, dak_ref, dav_ref, dsink_ref,
             st, gr, dy_s, kpad, vpad, bias, dk_acc, dv_acc):
        pair = pl.program_id(1)
        g, half = lax.shift_right_logical(pair, 1), pair & 1
        tab = _stacked_tables(lgf_ref, lgb_ref, pair)
        is_a = tab["is_a"]
        gain = gain_ref[...]

        @pl.when(half == 0)
        def _():
            _attn_tables(g, bias)
            _mask_keys(ak_ref, g, Q_SCALE, kpad, s)
            _mask_keys(av_ref, g, 1.0, vpad, s)
            dsink_ref[...] = jnp.zeros_like(dsink_ref)

        @pl.when(pair == 0)
        def _():
            dk_acc[...] = jnp.zeros_like(dk_acc)
            dv_acc[...] = jnp.zeros_like(dv_acc)

        def norm_body(n, dgain):
            rows = pl.ds(pl.multiple_of(n * CHUNK, CHUNK), CHUNK)
            xhat, rstd = xhat_ref[rows, :], rstd_ref[rows, :]
            gate = g_ref[rows, :].astype(F32)
            sg = _sigmoid(gate)
            silu = gate * sg
            d_out = dr_ref[rows, :].astype(F32)
            dg_ref[rows, :] = (d_out * xhat * gain * (sg * (1.0 + gate * (1.0 - sg)))).astype(BF16)
            dxh = d_out * gain * silu
            m1 = _group_sum(is_a, dxh) * (1.0 / HEAD_DIM)
            m2 = _group_sum(is_a, dxh * xhat) * (1.0 / HEAD_DIM)
            dy = (rstd * (dxh - m1 - xhat * m2)).astype(BF16)
            dy_s[rows, :] = dy
            k8 = k_ref[rows, :].astype(F32) * Q_SCALE
            st[n] = jnp.where(tab["diag2"], _dot_tn(_both_ways(k8, tab["kdec"]), v_ref[rows, :]), 0.0)
            qf = q_ref[rows, :].astype(F32)
            gr[n] = jnp.where(tab["diag2"], _dot_tn(_both_ways(qf, tab["qdec"]), dy), 0.0)
            return dgain + jnp.sum(d_out * xhat * silu, axis=0, keepdims=True)

        colsum = lambda x: jnp.sum(x, axis=0, keepdims=True)

        def grad_body(n, carry):
            xfb, ifa, ifb, iba, ibb, lf, lb = carry
            rows = pl.ds(pl.multiple_of(n * CHUNK, CHUNK), CHUNK)
            q = q_ref[rows, :]
            qf = q.astype(F32)
            k8f = k_ref[rows, :].astype(F32) * Q_SCALE
            k8 = k8f.astype(BF16)
            v = v_ref[rows, :]
            dy = dy_s[rows, :]
            q2, dy2 = _stack_pair(is_a, q), _stack_pair(is_a, dy)
            sc = _dot_nt(q2, k8)
            dp = _dot_nt(dy2, v)
            a2 = (sc * tab["d2"]).astype(BF16)
            ds2 = (dp * tab["d2"]).astype(BF16)
            dq = _unstack_pair(is_a, _dot_nn(ds2, k8))
            dk = _dot_tn(ds2, q2)
            dv = _dot_tn(a2, dy2)
            prod = sc * dp
            pf, pb = prod * tab["df2"], prod * tab["db2"]
            ifa, ifb = ifa + colsum(pf[0:CHUNK, :]), ifb + colsum(pf[CHUNK:2 * CHUNK, :])
            iba, ibb = iba + colsum(pb[0:CHUNK, :]), ibb + colsum(pb[CHUNK:2 * CHUNK, :])
            states, sgrads = st[n], gr[n]
            sb, gb = states.astype(BF16), sgrads.astype(BF16)
            dqc = _dot_nt(dy, sb) * tab["qdec"]
            dkc = _dot_nt(v, gb) * tab["kdec"]
            dv = dv + _dot_nn(_both_ways(k8f, tab["kdec"]), gb)
            dq_ref[rows, :] = (dq + dqc[:, 0:LANES] + dqc[:, LANES:2 * LANES]).astype(BF16)
            dk_ref[rows, :] = ((dk + dkc[:, 0:LANES] + dkc[:, LANES:2 * LANES]) * Q_SCALE).astype(BF16)
            dv_ref[rows, :] = dv.astype(BF16)
            q2w, k2w = jnp.concatenate([qf, qf], axis=1), jnp.concatenate([k8f, k8f], axis=1)
            xfb = xfb + colsum(tab["qexp"] * q2w * dqc + tab["kexp"] * k2w * dkc)
            prod_s = sgrads * states
            lf, lb = lf + colsum(prod_s[0:LANES, :]), lb + colsum(prod_s[LANES:2 * LANES, :])
            return xfb, ifa, ifb, iba, ibb, lf, lb

        sink_col = _sink_column(sink_ref, g)
        head_row = lax.broadcasted_iota(jnp.int32, dsink_ref.shape, 0)

        def attention_block(blk):
            n = half * blocks_half + blk
            rows = pl.ds(pl.multiple_of(blk * BLOCK, BLOCK), BLOCK)
            keys = pl.ds(pl.multiple_of(n * BLOCK, BLOCK), KEYS)
            qm = _stack_heads(aq_ref[rows, :], g)
            k3, v3 = kpad[keys, :], vpad[keys, :]
            p, p_sink = _attn_probs(qm, k3, bias, sink_col, n, s)
            dom = _stack_heads(do_ref[rows, :], g)
            dp = _dot_nt(dom, v3)
            delta = jnp.sum(p * dp, axis=1, keepdims=True)
            ds_mat = (p * (dp - delta)).astype(BF16)
            daq_ref[rows, :] = _unstack_heads(_dot_nn(ds_mat, k3), g).astype(BF16)
            dk_acc[keys, :] += _dot_tn(ds_mat, qm) * Q_SCALE
            dv_acc[keys, :] += _dot_tn(p.astype(BF16), dom)
            w = p_sink * delta
            upd = jnp.zeros(dsink_ref.shape, F32)
            for h in range(GROUP):
                upd = upd + jnp.where(head_row == h, -jnp.sum(w[h * BLOCK:(h + 1) * BLOCK, :]), 0.0)
            dsink_ref[...] += upd

        dgain = _chunk_loop(n_chunks, norm_body, jnp.zeros((1, LANES), F32))
        _scan_states(n_chunks, st, FWD_ROWS, tab["lam_f"], BWD_ROWS, tab["lam_b"])
        _scan_states(n_chunks, gr, BWD_ROWS, tab["lam_b"], FWD_ROWS, tab["lam_f"])

        def trip(i, carry):
            for j in range(max(RET_UNROLL, per_trip)):
                if j < RET_UNROLL:
                    carry = grad_body(i * RET_UNROLL + j, carry)
                if j < per_trip:
                    attention_block(i * per_trip + j)
            return carry

        z = jnp.zeros((1, LANES), F32)
        init = (jnp.zeros((1, 2 * LANES), F32), z, z, z, z, z, z)
        xfb, ifa, ifb, iba, ibb, lf, lb = lax.fori_loop(0, trips, trip, init)
        st_ref[...] = jnp.zeros_like(st_ref)
        st_ref[ST_GAIN:ST_GAIN + 1, :] = dgain
        st_ref[ST_XF:ST_XF + 1, :] = xfb[:, 0:LANES]
        st_ref[ST_XB:ST_XB + 1, :] = xfb[:, LANES:2 * LANES]
        st_ref[ST_IFA:ST_IFA + 1, :] = ifa
        st_ref[ST_IFB:ST_IFB + 1, :] = ifb
        st_ref[ST_IBA:ST_IBA + 1, :] = iba
        st_ref[ST_IBB:ST_IBB + 1, :] = ibb
        st_ref[ST_LF:ST_LF + 1, :] = lf * (CHUNK * tab["lam_f"])
        st_ref[ST_LB:ST_LB + 1, :] = lb * (CHUNK * tab["lam_b"])

        @pl.when(pair == pairs - 1)
        def _():
            dak_ref[...] = dk_acc[BLOCK:BLOCK + s, :].astype(BF16)
            dav_ref[...] = dv_acc[BLOCK:BLOCK + s, :].astype(BF16)

    lane_blk = lambda c0: _seq_spec(s, c0 // LANES)
    seq0 = _seq_spec(s, 0)
    half_rows = blocks_half * BLOCK
    aq_spec = pl.BlockSpec((half_rows, GW), lambda b, h: (b * PAIRS_PER_KV + (h & 1), C_AQ // GW + h // 2))
    a_spec = pl.BlockSpec((half_rows, GW), lambda b, h: (b * PAIRS_PER_KV + (h & 1), h // 2))
    kv_spec = lambda c0: pl.BlockSpec((s, LANES), lambda b, h: (b, c0 // LANES))
    kv_out = pl.BlockSpec((s, LANES), lambda b, h: (b, 0))
    state = pltpu.VMEM((n_chunks, 2 * LANES, LANES), F32)
    pad = pltpu.VMEM((s + 2 * BLOCK, LANES), BF16)
    acc = pltpu.VMEM((s + 2 * BLOCK, LANES), F32)
    body, lead = _after(after, body)
    return pl.pallas_call(
        body, name="mixers_bwd", grid=(b_loc, pairs),
        in_specs=lead + [_smem_spec(), _smem_spec(), _smem_spec(), lane_blk(C_RQ), lane_blk(C_RK), lane_blk(C_RV),
                         lane_blk(C_RG), seq0, seq0, seq0, pl.BlockSpec((1, LANES), lambda b, h: (0, h)),
                         aq_spec, kv_spec(C_AK), kv_spec(C_AV), a_spec],
        out_specs=[seq0] * 4 + [pl.BlockSpec((ST_ROWS, LANES), lambda b, h: (b, h)), a_spec, kv_out, kv_out,
                                pl.BlockSpec((8, LANES), lambda b, h: (b * KV_HEADS + h // 2, 0))],
        out_shape=[SDS((t, RET_W), BF16)] * 4 + [SDS((b_loc * ST_ROWS, RET_W), F32), SDS((t, ATT_W), BF16),
                                                   SDS((t, KV_W), BF16), SDS((t, KV_W), BF16),
                                                   SDS((b_loc * KV_HEADS * 8, LANES), F32)],
        scratch_shapes=[state, state, pltpu.VMEM((s, LANES), BF16), pad, pad,
                        pltpu.VMEM((GROUP * BLOCK, KEYS), F32), acc, acc],
        compiler_params=_params(("arbitrary", "arbitrary")),
    )(*after, lgf, lgb, sink, u, u, u, u, xhat, rstd, dr, gn_gain, u, u, u, da)


def _pack_small(acc2, acc1, ret_stats, dsink, b_loc, d):
    pairs = RET_HEADS // 2

    def body(acc2_ref, acc1_ref, st_ref, dsink_ref, out_ref):
        out_ref[...] = jnp.zeros_like(out_ref)
        out_ref[ROW_LN1G:ROW_LN1G + 1, :] = acc1_ref[0:1, :]
        out_ref[ROW_LN1B:ROW_LN1B + 1, :] = acc1_ref[1:2, :]
        out_ref[ROW_LN2G:ROW_LN2G + 1, :] = acc2_ref[1:2, :]
        out_ref[ROW_LN2B:ROW_LN2B + 1, :] = acc2_ref[2:3, :]
        out_ref[ROW_LOSS:ROW_LOSS + 1, :] = acc2_ref[0:1, :]
        st = st_ref[0:ST_ROWS, :]
        for b in range(1, b_loc):
            st = st + st_ref[b * ST_ROWS:(b + 1) * ST_ROWS, :]
        out_ref[ROW_GN:ROW_GN + 1, 0:RET_W] = st[ST_GAIN:ST_GAIN + 1, :]
        lane = lax.broadcasted_iota(jnp.int32, (1, d), 1)
        misc = jnp.zeros((1, d), F32)
        for pr in range(pairs):
            blk = st[:, pr * LANES:(pr + 1) * LANES]
            half = lax.broadcasted_iota(jnp.int32, (1, LANES), 1) < HEAD_DIM
            for h in range(2):
                sel = half if h == 0 else jnp.logical_not(half)
                cross_f = jnp.sum(jnp.where(sel, blk[ST_XF:ST_XF + 1, :] + blk[ST_LF:ST_LF + 1, :], 0.0))
                cross_b = jnp.sum(jnp.where(sel, blk[ST_XB:ST_XB + 1, :] + blk[ST_LB:ST_LB + 1, :], 0.0))
                intra_f = jnp.sum(blk[ST_IFA + h:ST_IFA + h + 1, :])
                intra_b = jnp.sum(blk[ST_IBA + h:ST_IBA + h + 1, :])
                head = 2 * pr + h
                misc = jnp.where(lane == MISC_DF + head, cross_f + intra_f, misc)
                misc = jnp.where(lane == MISC_DB + head, cross_b + intra_b, misc)
        for g in range(KV_HEADS):
            tot = dsink_ref[g * 8:(g + 1) * 8, :]
            for b in range(1, b_loc):
                tot = tot + dsink_ref[(b * KV_HEADS + g) * 8:(b * KV_HEADS + g + 1) * 8, :]
            for h in range(GROUP):
                misc = jnp.where(lane == MISC_SINK + GROUP * g + h, jnp.sum(tot[h:h + 1, 0:1]), misc)
        out_ref[ROW_MISC:ROW_MISC + 1, :] = misc

    return pl.pallas_call(body, name="pack_small", out_shape=SDS((SMALL_ROWS, d), F32))(acc2, acc1, ret_stats, dsink)


BIG = ("w_in", "w_out", "w_ffn_gate", "w_ffn_up", "w_ffn_down", "w_ple_proj", "w_ple_gate")
TRANSPOSED_OUTSIDE = ("w_in", "w_ffn_gate", "w_ffn_up")
TRANSPOSED_HERE = ("w_ple_proj",)
SMALL = ("ret_decay_fwd", "ret_decay_bwd", "ret_gn_gain", "attn_sink", "ln1_gain", "ln1_bias", "ln2_gain", "ln2_bias")
ORDER = ("w_in", "ret_decay_fwd", "ret_decay_bwd", "ret_gn_gain", "attn_sink", "w_out", "ln1_gain", "ln1_bias",
         "w_ffn_gate", "w_ffn_up", "w_ffn_down", "w_ple_proj", "w_ple_gate", "ln2_gain", "ln2_bias")


GATHER_ORDER = ("w_in", "w_ffn_up", "w_out", "w_ffn_gate", "w_ple_gate", "w_ple_proj", "w_ffn_down")
GATHER_TWO_LEVEL = ("w_in", "w_ffn_up")


def _local_step(x2, p2, target2, fetch, publish, small, b_loc, me):
    d = x2.shape[1]
    lgf, lgb = _log_decay(small["ret_decay_fwd"], small["ret_decay_bwd"])
    lgf1, lgb1, sink1 = lgf.reshape(-1), lgb.reshape(-1), small["attn_sink"].reshape(-1)
    (w_in,) = fetch(("w_in",), ())
    u, xb = _in_proj(x2, w_in)
    passed = fetch.pass_on("w_ffn_up", (xb,))
    r, ret_xhat, ret_rstd, a = _mixers_fwd(u, lgf1, lgb1, small["ret_gn_gain"], sink1, b_loc, passed)
    w_out, w_gate, w_up = fetch(("w_out", "w_ffn_gate", "w_ffn_up"), (r, a))
    z1, h1b, dact_dg, dact_du, act = _mix_ln1_ffn_up(
        r, a, x2, w_out, w_gate, w_up, small["ln1_gain"], small["ln1_bias"])
    w_pg, w_pe, w_down = fetch(("w_ple_gate", "w_ple_proj", "w_ffn_down"), (act,))
    dz2, dz2b, dsb, dpleb, dg, dup, acc2 = _ffn_down_ln2_loss(
        act, dact_dg, dact_du, h1b, p2, z1, target2, w_down, w_pg, w_pe,
        small["ln1_gain"], small["ln1_bias"], small["ln2_gain"], small["ln2_bias"])
    own = {}

    def grad(name, parts, rhs, after=()):
        whole, own[name] = _weight_grad("grad_" + name, me, parts, rhs, after)
        return whole

    t2 = publish("ffn", dict(w_ffn_down=grad("w_ffn_down", [act], dz2b),
                             w_ple_proj=grad("w_ple_proj", [dpleb], p2),
                             w_ple_gate=grad("w_ple_gate", [h1b], dsb),
                             w_ffn_gate=grad("w_ffn_gate", [dg], h1b),
                             w_ffn_up=grad("w_ffn_up", [dup], h1b)))
    dz1, dz1b, dr, da, acc1 = _dh1_ln1_bwd(dz2, dg, dup, dsb, z1, w_gate, w_up, w_pg, w_out, small["ln1_gain"], t2)
    t3 = publish("out", dict(w_out=grad("w_out", [r, a], dz1b)))
    dq, dk, dv, dgate, ret_stats, daq, dak, dav, dsink = _mixers_bwd(
        u, ret_xhat, ret_rstd, dr, da, lgf1, lgb1, small["ret_gn_gain"], sink1, b_loc, t3)
    parts = [dq, dk, dv, dgate, daq, dak, dav]
    small_part = _pack_small(acc2, acc1, ret_stats, dsink, b_loc, d)
    t4 = publish("in", dict(w_in=grad("w_in", parts, xb)), small_part)
    grad_x = _in_proj_bwd(dz1, parts, w_in, t4)
    return grad_x, own, small_part


def kernel(x, p, w_in, ret_decay_fwd, ret_decay_bwd, ret_gn_gain, attn_sink, w_out, ln1_gain, ln1_bias, w_ffn_gate, w_ffn_up, w_ffn_down, w_ple_proj, w_ple_gate, ln2_gain, ln2_bias, loss_target, m_w_in, m_ret_decay_fwd, m_ret_decay_bwd, m_ret_gn_gain, m_attn_sink, m_w_out, m_ln1_gain, m_ln1_bias, m_w_ffn_gate, m_w_ffn_up, m_w_ffn_down, m_w_ple_proj, m_w_ple_gate, m_ln2_gain, m_ln2_bias, v_w_in, v_ret_decay_fwd, v_ret_decay_bwd, v_ret_gn_gain, v_attn_sink, v_w_out, v_ln1_gain, v_ln1_bias, v_w_ffn_gate, v_w_ffn_up, v_w_ffn_down, v_w_ple_proj, v_w_ple_gate, v_ln2_gain, v_ln2_bias):
    given = dict(locals())

    def strip(n, a):
        if n not in BIG:
            return a
        return a[0].T if n in TRANSPOSED_OUTSIDE else a[0]

    def restore(n, a):
        if n not in BIG:
            return a
        return (a.T if n in TRANSPOSED_OUTSIDE else a)[None]

    w = {n: strip(n, given[n]) for n in ORDER}
    m = {n: strip(n, given["m_" + n]) for n in ORDER}
    v = {n: strip(n, given["v_" + n]) for n in ORDER}
    b_loc, s, d = x.shape
    x2 = x.reshape(b_loc * s, d)
    p2 = p[0].reshape(b_loc * s, p.shape[-1])
    target2 = loss_target.reshape(b_loc * s, d)

    small = {n: w[n] for n in SMALL}
    me = (4 * lax.axis_index("x") + 2 * lax.axis_index("y") + lax.axis_index("c")).astype(jnp.int32).reshape(1)

    started = {}
    for tag, names in (("first", GATHER_ORDER[:1]), ("rest", GATHER_ORDER[1:])):
        gathered = _prep_shards("prep_shards_" + tag, me, {n: w[n] for n in names})
        copies = [_gather_copy_near if n in GATHER_TWO_LEVEL else _gather_copy for n in names]
        group = _split_copy_start("gather_start_" + tag, [(gathered[n],) for n in names], copies)
        started.update({n: (group, i) for i, n in enumerate(names)})

    passing = {}

    def pass_on(n, after):
        group, i = started[n]
        if n == GATHER_ORDER[0]:
            after = (*after, started[GATHER_ORDER[1]][0]["token"])
        near = _split_copy_wait("gather_wait_" + n + "_near", group, [i], list(after))
        passing[n] = _split_copy_start("gather_pass_" + n, near, [_gather_copy_pass])
        return (passing[n]["token"],)

    def fetch(names, after):
        out = {}
        for n in [n for n in names if n in GATHER_TWO_LEVEL]:
            if n not in passing:
                pass_on(n, after)
            out[n] = _split_copy_wait("gather_wait_" + n, passing[n], [0], list(after))[0][0]
        direct = [n for n in names if n not in GATHER_TWO_LEVEL]
        if direct:
            group = started[direct[0]][0]
            got = _split_copy_wait("gather_wait_" + direct[0], group, [started[n][1] for n in direct], list(after))
            out.update({n: item[0] for n, item in zip(direct, got)})
        return [out[n] for n in names]

    scatters = []

    def publish(tag, products, small_sums=None):
        items = [(products[n], lax.empty((N_DEV - 1, products[n].shape[0] // N_DEV, products[n].shape[1]), BF16))
                 for n in products]
        copies = [_scatter_copy] * len(items)
        if small_sums is not None:
            items.append((small_sums, lax.empty((N_DEV - 1,) + small_sums.shape, F32)))
            copies.append(_small_copy)
        started = _split_copy_start("scatter_start_" + tag, items, copies)
        scatters.append((list(products), small_sums is not None, started))
        return (started["token"],)

    fetch.pass_on = pass_on
    grad_x, own, small_part = _local_step(x2, p2, target2, fetch, publish, small, b_loc, me)

    out_g, out_d, out_m, out_v = {}, {}, {}, {}
    after = [grad_x]
    for names, with_small, started in scatters:
        landed = _split_copy_wait("scatter_wait_" + names[0], started, list(range(len(started["items"]))), after)
        if with_small:
            loss, sg, sd, sm, sv = _small_adamw(
                me, small_part, landed[-1][1], small, {n: m[n] for n in SMALL}, {n: v[n] for n in SMALL})
            for dst, src in ((out_g, sg), (out_d, sd), (out_m, sm), (out_v, sv)):
                dst.update(src)
        for n, (_, recv) in zip(names, landed):
            out_g[n], out_d[n], out_m[n], out_v[n] = _reduce_adamw(
                n, own[n], recv, w[n], m[n], v[n], n in TRANSPOSED_HERE)
        after = [out_v[names[-1]]]

    outs = [loss[0, 0], grad_x.reshape(x.shape)]
    for group in (out_g, out_d, out_m, out_v):
        outs += [restore(n, group[n]) for n in ORDER]
    return tuple(outs)
```

```python
import functools

import jax
import jax.numpy as jnp
from jax import lax
from jax.experimental import pallas as pl
from jax.experimental.pallas import tpu as pltpu

F32, BF16 = jnp.float32, jnp.bfloat16
SDS = jax.ShapeDtypeStruct
MESH = pl.DeviceIdType.MESH

N_DEV = 8
HEAD_DIM = 64
RET_HEADS = 8
ATTN_HEADS = 8
KV_HEADS = 2
GROUP = ATTN_HEADS // KV_HEADS
RET_W = RET_HEADS * HEAD_DIM
ATT_W = ATTN_HEADS * HEAD_DIM
KV_W = KV_HEADS * HEAD_DIM
LANES = 128
CHUNK = 128
BLOCK = 128
Q_SCALE = HEAD_DIM ** -0.5
ALPHA = 2.0 ** 0.25
LN_EPS = 1e-5
GN_EPS = 1e-5
NEG_INF = -1e30
C_RQ, C_RK, C_RV, C_RG = 0, RET_W, 2 * RET_W, 3 * RET_W
C_AQ = 4 * RET_W
C_AK = C_AQ + ATT_W
C_AV = C_AK + KV_W
IN_W = C_AV + KV_W

ADAM_LR = 0.001
ADAM_B1 = 0.9
ADAM_B2 = 0.999
ADAM_EPS = 1e-08
ADAM_WD = 0.01
ADAM_STEP = 10

VMEM_LIMIT = 56 * 1024 * 1024
MATMUL_ROWS = 512
EPILOGUE_ROWS = 256
SUB_ROWS = 256
SMALL_ROWS = 16
ROW_LN1G, ROW_LN1B, ROW_LN2G, ROW_LN2B, ROW_LOSS, ROW_GN, ROW_MISC = 0, 1, 2, 3, 4, 5, 6
MISC_DF, MISC_DB, MISC_SINK = 0, 8, 16


def _dot_nn(a, b):
    return lax.dot_general(a, b, (((1,), (0,)), ((), ())), preferred_element_type=F32)


def _dot_nt(a, b):
    return lax.dot_general(a, b, (((1,), (1,)), ((), ())), preferred_element_type=F32)


def _dot_tn(a, b):
    return lax.dot_general(a, b, (((0,), (0,)), ((), ())), preferred_element_type=F32)


def _params(sem=None, vmem=VMEM_LIMIT):
    kw = {"vmem_limit_bytes": vmem}
    if sem is not None:
        kw["dimension_semantics"] = sem
    return pltpu.CompilerParams(**kw)


def _row_tile(t, want=512):
    tm = want
    while t % tm:
        tm //= 2
    return tm


def _sigmoid(x):
    return jax.nn.sigmoid(x)


def _layer_norm_stats(z):
    mu = jnp.mean(z, axis=1, keepdims=True)
    d = z - mu
    var = jnp.mean(d * d, axis=1, keepdims=True)
    rstd = lax.rsqrt(var + LN_EPS)
    return d * rstd, rstd


def _layer_norm_bwd(dxh, xhat, rstd):
    m1 = jnp.mean(dxh, axis=1, keepdims=True)
    m2 = jnp.mean(dxh * xhat, axis=1, keepdims=True)
    return rstd * (dxh - m1 - xhat * m2)


def _prep_shards(me, shards):
    names = list(shards)

    def body(me_ref, *refs):
        for name, src, dst in zip(names, refs[:len(names)], refs[len(names):]):
            val = src[...]
            dst[...] = (val.T if name in TRANSPOSED_HERE else val).astype(BF16)

    shape = lambda n, a: a.shape[::-1] if n in TRANSPOSED_HERE else a.shape
    shapes = [shape(n, shards[n]) for n in names]
    out = pl.pallas_call(
        body, name="prep_shards",
        grid_spec=pltpu.PrefetchScalarGridSpec(
            num_scalar_prefetch=1, grid=(1,),
            in_specs=[pl.BlockSpec(shards[n].shape, lambda i, me_ref: (0, 0)) for n in names],
            out_specs=[pl.BlockSpec(s, lambda i, me_ref: (me_ref[0], 0)) for s in shapes]),
        out_shape=[SDS((N_DEV * s[0], s[1]), BF16) for s in shapes], compiler_params=_params(("arbitrary",)),
    )(me, *[shards[n] for n in names])
    return dict(zip(names, out))


def _mesh_pos():
    return lax.axis_index("x"), lax.axis_index("y"), lax.axis_index("c")


HBM_SPEC = pl.BlockSpec(memory_space=pltpu.HBM)
SEM_SPEC = pl.BlockSpec(memory_space=pltpu.SEMAPHORE)
ANY_SPEC = pl.BlockSpec(memory_space=pl.ANY)
SIDE_EFFECT = pltpu.SideEffectType.DATAFLOW_SIDE_EFFECTING
PEER_SEMS = pltpu.SemaphoreType.DMA((N_DEV - 1,))


def _in_hbm(a):
    return pltpu.with_memory_space_constraint(a, pltpu.HBM)


def _split_copy_start(name, items, copies):
    n = len(items)
    flat = [a for it in items for a in it]
    k = len(flat)

    def body(*refs):
        arr, sems = list(refs[:k]), refs[k:k + 2 * n]
        for i, it in enumerate(items):
            mine = [arr.pop(0) for _ in it]
            for m in range(1, N_DEV):
                cp = copies[i](m, mine, sems[i].at[m - 1], sems[n + i].at[m - 1])
                if cp is not None:
                    cp.start()
        token = refs[-1]
        token[...] = jnp.zeros_like(token)

    res = pl.pallas_call(
        body, name=name,
        out_shape=[PEER_SEMS] * (2 * n) + [pltpu.HBM(a.shape, a.dtype) for a in flat] + [SDS((8, LANES), F32)],
        in_specs=[HBM_SPEC] * k,
        out_specs=[SEM_SPEC] * (2 * n) + [HBM_SPEC] * k + [pl.BlockSpec(memory_space=pltpu.VMEM)],
        input_output_aliases={j: 2 * n + j for j in range(k)},
        compiler_params=pltpu.CompilerParams(has_side_effects=SIDE_EFFECT),
    )(*[_in_hbm(a) for a in flat])
    thru, out_items = list(res[2 * n:2 * n + k]), []
    for it in items:
        out_items.append(tuple(thru.pop(0) for _ in it))
    return dict(send=res[:n], recv=res[n:2 * n], items=out_items, token=res[-1], copies=copies)


def _split_copy_wait(name, started, which, after):
    items = [started["items"][i] for i in which]
    copies = [started["copies"][i] for i in which]
    n = len(items)
    flat = [a for it in items for a in it]
    k = len(flat)

    def body(*refs):
        arr, sems = list(refs[:k]), refs[k:k + 2 * n]
        for i, it in enumerate(items):
            mine = [arr.pop(0) for _ in it]
            for m in range(1, N_DEV):
                cp = copies[i](m, mine, sems[i].at[m - 1], sems[n + i].at[m - 1])
                if cp is not None:
                    cp.wait_send()
                    cp.wait_recv()

    res = pl.pallas_call(
        body, name=name,
        out_shape=[pltpu.HBM(a.shape, a.dtype) for a in flat],
        in_specs=[HBM_SPEC] * k + [SEM_SPEC] * (2 * n) + [ANY_SPEC] * len(after),
        out_specs=[HBM_SPEC] * k,
        input_output_aliases={j: j for j in range(k)},
        compiler_params=pltpu.CompilerParams(has_side_effects=SIDE_EFFECT),
    )(*flat, *[started["send"][i] for i in which], *[started["recv"][i] for i in which], *[_in_hbm(a) for a in after])
    thru, out_items = list(res), []
    for it in items:
        out_items.append(tuple(thru.pop(0) for _ in it))
    return out_items


def _gather_copy(m, refs, send_sem, recv_sem):
    (land_ref,) = refs
    r = land_ref.shape[0] // N_DEV
    mine = land_ref.at[pl.ds(pl.multiple_of(_peer_index(0) * r, 8), r), :]
    return pltpu.make_async_remote_copy(src_ref=mine, dst_ref=mine, send_sem=send_sem, recv_sem=recv_sem,
                                        device_id=_peer(m), device_id_type=MESH)


def _gather_copy_near(m, refs, send_sem, recv_sem):
    return _gather_copy(m, refs, send_sem, recv_sem) if m == 1 or m % 2 == 0 else None


def _gather_copy_pass(m, refs, send_sem, recv_sem):
    if m == 1 or m % 2 == 0:
        return None
    (land_ref,) = refs
    r = land_ref.shape[0] // N_DEV
    block = land_ref.at[pl.ds(pl.multiple_of(_peer_index(m ^ 1) * r, 8), r), :]
    return pltpu.make_async_remote_copy(src_ref=block, dst_ref=block, send_sem=send_sem, recv_sem=recv_sem,
                                        device_id=_peer(1), device_id_type=MESH)


def _small_copy(m, refs, send_sem, recv_sem):
    part_ref, land_ref = refs
    return pltpu.make_async_remote_copy(src_ref=part_ref, dst_ref=land_ref.at[m - 1], send_sem=send_sem,
                                        recv_sem=recv_sem, device_id=_peer(m), device_id_type=MESH)


def _scatter_copy(m, refs, send_sem, recv_sem):
    buf_ref, land_ref = refs
    r = buf_ref.shape[0] // N_DEV
    src = buf_ref.at[pl.ds(pl.multiple_of(_peer_index(m) * r, 8), r), :]
    return pltpu.make_async_remote_copy(src_ref=src, dst_ref=land_ref.at[m - 1], send_sem=send_sem,
                                        recv_sem=recv_sem, device_id=_peer(m), device_id_type=MESH)


def _peer(m):
    x, y, c = _mesh_pos()
    bx, by, bc = (m >> 2) & 1, (m >> 1) & 1, m & 1
    return (x ^ bx if bx else x, y ^ by if by else y, c ^ bc if bc else c)


def _peer_index(m):
    x, y, c = _mesh_pos()
    return (4 * x + 2 * y + c) ^ m


SMALL_PLACE = {
    "ln1_gain": (ROW_LN1G, 0), "ln1_bias": (ROW_LN1B, 0), "ln2_gain": (ROW_LN2G, 0), "ln2_bias": (ROW_LN2B, 0),
    "ret_gn_gain": (ROW_GN, 0), "ret_decay_fwd": (ROW_MISC, MISC_DF), "ret_decay_bwd": (ROW_MISC, MISC_DB),
    "attn_sink": (ROW_MISC, MISC_SINK)}


def _small_adamw(me, part, landed, w, m, v):
    d = part.shape[1]
    names = list(SMALL_PLACE)
    k = len(names)

    def body(*refs):
        me_ref, part_ref, land_ref = refs[:3]
        refs = refs[2:]
        w_refs, m_refs, v_refs = refs[1:1 + k], refs[1 + k:1 + 2 * k], refs[1 + 2 * k:1 + 3 * k]
        outs = refs[1 + 3 * k:1 + 7 * k + 1]
        tot_ref = refs[-1]
        loss_ref, g_refs, dl_refs = outs[0], outs[1:1 + k], outs[1 + k:1 + 2 * k]
        nm_refs, nv_refs = outs[1 + 2 * k:1 + 3 * k], outs[1 + 3 * k:1 + 4 * k]
        tot = jnp.zeros(part_ref.shape, F32)
        for dev in range(N_DEV):
            j = dev ^ me_ref[0]
            tot = tot + jnp.where(j == 0, part_ref[...], land_ref[jnp.maximum(j, 1) - 1])
        tot_ref[...] = tot
        loss_ref[...] = (0.5 / d) * jnp.sum(tot_ref[ROW_LOSS:ROW_LOSS + 1, :], axis=1, keepdims=True)
        for i, name in enumerate(names):
            row, lo = SMALL_PLACE[name]
            wv = w_refs[i][...]
            g = tot_ref[row:row + 1, lo:lo + wv.shape[1]]
            if name.startswith("ret_decay"):
                p2 = jnp.exp2(wv)
                g = g * (-p2 * jnp.log(2.0) / (1.0 - p2))
            g_refs[i][...] = g
            _adamw_store(g, wv, m_refs[i][...], v_refs[i][...], dl_refs[i], nm_refs[i], nv_refs[i])

    shapes = [SDS(w[n].shape, F32) for n in names]
    vm = pl.BlockSpec(memory_space=pltpu.VMEM)
    res = pl.pallas_call(
        body, name="small_adamw", out_shape=[SDS((1, 1), F32)] + shapes * 4,
        in_specs=[_smem_spec()] + [vm] * (2 + 3 * k), out_specs=[vm] * (1 + 4 * k),
        scratch_shapes=[pltpu.VMEM(part.shape, F32)],
    )(me, part, landed, *[w[n] for n in names], *[m[n] for n in names], *[v[n] for n in names])
    groups = [dict(zip(names, res[1 + j * k:1 + (j + 1) * k])) for j in range(4)]
    return (res[0], *groups)


def _adamw_store(g, w, m, v, dl_ref, nm_ref, nv_ref):
    m = ADAM_B1 * m + (1.0 - ADAM_B1) * g
    v = ADAM_B2 * v + (1.0 - ADAM_B2) * (g * g)
    m_hat = m / (1.0 - ADAM_B1 ** ADAM_STEP)
    v_hat = v / (1.0 - ADAM_B2 ** ADAM_STEP)
    dl_ref[...] = -ADAM_LR * (m_hat / (jnp.sqrt(v_hat) + ADAM_EPS) + ADAM_WD * w)
    nm_ref[...] = m
    nv_ref[...] = v


def _reduce_adamw(name, owns, recvs, ws, ms, vs, transposed):
    count = len(owns)
    rows, n = owns[0].shape
    steps = 1 if transposed or rows % 32 else 4
    rb = rows // steps

    def body(*refs):
        ins, outs = refs[:5 * count], refs[5 * count:]
        j = pl.program_id(0)
        for k in range(count):
            @pl.when(j == k)
            def _(k=k):
                own_ref, recv_ref, w_ref, m_ref, v_ref = ins[5 * k:5 * k + 5]
                g_ref, dl_ref, nm_ref, nv_ref = outs[4 * k:4 * k + 4]
                g = own_ref[...]
                for p in range(N_DEV - 1):
                    g = g + recv_ref[p].astype(F32)
                if transposed:
                    g = g.T
                g_ref[...] = g
                _adamw_store(g, w_ref[...], m_ref[...], v_ref[...], dl_ref, nm_ref, nv_ref)

    def turn(k):
        return lambda j, i: jnp.where(j == k, i, jnp.where(j < k, 0, steps - 1))

    in_specs, out_specs = [], []
    for k in range(count):
        at = turn(k)
        blk = pl.BlockSpec(ws[0].shape if transposed else (rb, n), lambda j, i, at=at: (at(j, i), 0))
        in_specs += [pl.BlockSpec((rb, n), lambda j, i, at=at: (at(j, i), 0)),
                     pl.BlockSpec((N_DEV - 1, rb, n), lambda j, i, at=at: (0, at(j, i), 0)), blk, blk, blk]
        out_specs += [blk] * 4
    res = pl.pallas_call(
        body, name="adamw_" + name, grid=(count, steps), in_specs=in_specs, out_specs=out_specs,
        out_shape=[SDS(ws[0].shape, F32)] * (4 * count), compiler_params=_params(("arbitrary", "arbitrary")),
    )(*[a for k in range(count) for a in (owns[k], recvs[k], ws[k], ms[k], vs[k])])
    return [list(res[j::4]) for j in range(4)]


def _row_spec(tm, width):
    return pl.BlockSpec((tm, width), lambda i: (i, 0))


def _full_spec(shape):
    return pl.BlockSpec(shape, lambda i: (0,) * len(shape))


_acc_spec = _full_spec


def _sub_rows(tm):
    step = min(SUB_ROWS, tm)
    return [(lo, lo + step) for lo in range(0, tm, step)]


def _in_proj(x2, wt_in):
    t, d = x2.shape
    u_w = wt_in.shape[0]
    tm = _row_tile(t, MATMUL_ROWS)

    def body(x_ref, w_ref, u_ref, xb_ref):
        xb = x_ref[...].astype(BF16)
        xb_ref[...] = xb
        u_ref[...] = _dot_nt(xb, w_ref[...]).astype(BF16)

    return pl.pallas_call(
        body, name="in_proj", grid=(t // tm,),
        in_specs=[_row_spec(tm, d), _full_spec(wt_in.shape)],
        out_specs=[_row_spec(tm, u_w), _row_spec(tm, d)],
        out_shape=[SDS((t, u_w), BF16), SDS((t, d), BF16)],
        compiler_params=_params(("parallel",)),
    )(x2, wt_in)


def _col_halves(f):
    n = f // LANES
    k = (n + 1) // 2 * LANES
    return [(0, k), (k, f)] if k < f else [(0, f)]


def _mix_ln1_ffn_up(r, a, x2, w_out, wt_gate, wt_up, g1, b1):
    t, d = x2.shape
    f = wt_gate.shape[0]
    tm = _row_tile(t, EPILOGUE_ROWS)

    def body(r_ref, a_ref, x_ref, wo_ref, wg_ref, wu_ref, g_ref, b_ref, z_ref, hb_ref, dg_ref, du_ref, act_ref):
        mix = _dot_nn(r_ref[...], wo_ref[0:RET_W, :]) + _dot_nn(a_ref[...], wo_ref[RET_W:RET_W + ATT_W, :])
        z = ALPHA * x_ref[...] + mix
        xhat, _ = _layer_norm_stats(z)
        z_ref[...] = z
        h = (xhat * g_ref[...] + b_ref[...]).astype(BF16)
        hb_ref[...] = h
        for lo, hi in _col_halves(f):
            g = _dot_nt(h, wg_ref[lo:hi, :])
            u = _dot_nt(h, wu_ref[lo:hi, :])
            sg = _sigmoid(g)
            silu = g * sg
            dg_ref[:, lo:hi] = (u * (sg * (1.0 + g * (1.0 - sg)))).astype(BF16)
            du_ref[:, lo:hi] = silu.astype(BF16)
            act_ref[:, lo:hi] = (silu * u).astype(BF16)

    wide, narrow = _row_spec(tm, f), _row_spec(tm, d)
    return pl.pallas_call(
        body, name="mix_ln1_ffn_up", grid=(t // tm,),
        in_specs=[_row_spec(tm, RET_W), _row_spec(tm, ATT_W), narrow, _resident_spec(w_out.shape),
                  _resident_spec(wt_gate.shape), _resident_spec(wt_up.shape), _full_spec(g1.shape),
                  _full_spec(b1.shape)],
        out_specs=[narrow, narrow, wide, wide, wide],
        out_shape=[SDS((t, d), F32), SDS((t, d), BF16)] + [SDS((t, f), BF16)] * 3,
        compiler_params=_params(("parallel",)),
    )(r, a, x2, w_out, wt_gate, wt_up, g1, b1)


def _ffn_down_ln2_loss(act, dact_dg, dact_du, h1b, p2, z1, target, w_down, w_pg, wt_pe, g1, b1, g2, b2):
    t, d = z1.shape
    f = act.shape[1]
    pdim = p2.shape[1]
    tm = _row_tile(t, EPILOGUE_ROWS)

    def body(act_ref, fg_ref, fu_ref, hb_ref, p_ref, z1_ref, tgt_ref, wd_ref, wpg_ref, wpe_ref, g1_ref, b1_ref,
             g2_ref, b2_ref, dz_ref, dzb_ref, ds_ref, dple_ref, dg_ref, du_ref, acc_ref):
        @pl.when(pl.program_id(0) == 0)
        def _():
            acc_ref[...] = jnp.zeros_like(acc_ref)

        for lo, hi in _sub_rows(tm):
            xhat1, _ = _layer_norm_stats(z1_ref[lo:hi, :])
            h1 = xhat1 * g1_ref[...] + b1_ref[...]
            ffn = _dot_nn(act_ref[lo:hi, :], wd_ref[...])
            pg = _sigmoid(_dot_nn(hb_ref[lo:hi, :], wpg_ref[...]))
            ple = _dot_nt(p_ref[lo:hi, :].astype(BF16), wpe_ref[...])
            z2 = ALPHA * h1 + ffn + pg * ple
            xhat2, rstd2 = _layer_norm_stats(z2)
            err = xhat2 * g2_ref[...] + b2_ref[...] - tgt_ref[lo:hi, :]
            dy = err * (1.0 / d)
            dz = _layer_norm_bwd(dy * g2_ref[...], xhat2, rstd2)
            dzb = dz.astype(BF16)
            dz_ref[lo:hi, :] = dz
            dzb_ref[lo:hi, :] = dzb
            ds_ref[lo:hi, :] = (dz * ple * pg * (1.0 - pg)).astype(BF16)
            dple_ref[lo:hi, :] = (dz * pg).astype(BF16)
            acc_ref[0:1, :] += jnp.sum(err * err, axis=0, keepdims=True)
            acc_ref[1:2, :] += jnp.sum(dy * xhat2, axis=0, keepdims=True)
            acc_ref[2:3, :] += jnp.sum(dy, axis=0, keepdims=True)
            for c0, c1 in _col_halves(f):
                da = _dot_nt(dzb, wd_ref[c0:c1, :])
                dg_ref[lo:hi, c0:c1] = (da * fg_ref[lo:hi, c0:c1].astype(F32)).astype(BF16)
                du_ref[lo:hi, c0:c1] = (da * fu_ref[lo:hi, c0:c1].astype(F32)).astype(BF16)

    vec = _full_spec(g1.shape)
    wide, narrow = _row_spec(tm, f), _row_spec(tm, d)
    return pl.pallas_call(
        body, name="ffn_down_ln2_loss", grid=(t // tm,),
        in_specs=[wide, wide, wide, narrow, _row_spec(tm, pdim), narrow, narrow,
                  _full_spec(w_down.shape), _full_spec(w_pg.shape), _full_spec(wt_pe.shape), vec, vec, vec, vec],
        out_specs=[narrow] * 4 + [wide, wide, _acc_spec((8, d))],
        out_shape=[SDS((t, d), F32), SDS((t, d), BF16), SDS((t, d), BF16), SDS((t, d), BF16),
                   SDS((t, f), BF16), SDS((t, f), BF16), SDS((8, d), F32)],
        compiler_params=_params(("arbitrary",)),
    )(act, dact_dg, dact_du, h1b, p2, z1, target, w_down, w_pg, wt_pe, g1, b1, g2, b2)


def _after(after, body):
    k = len(after)
    return (lambda *refs: body(*refs[k:])), [ANY_SPEC] * k


def _resident_spec(shape):
    return pl.BlockSpec(shape, lambda i: (0,) * len(shape), pipeline_mode=pl.Buffered(1))


def _dh1_ln1_bwd(dz2, dg, dup, dsb, z1, wt_gate, wt_up, w_pg, w_out, g1, after=()):
    t, d = dz2.shape
    f = dg.shape[1]
    tm = _row_tile(t, EPILOGUE_ROWS)

    def body(dz_ref, dg_ref, du_ref, ds_ref, z1_ref, wg_ref, wu_ref, wpg_ref, wo_ref, g1_ref,
             dz1_ref, dz1b_ref, dr_ref, da_ref, acc_ref):
        @pl.when(pl.program_id(0) == 0)
        def _():
            acc_ref[...] = jnp.zeros_like(acc_ref)

        for lo, hi in _sub_rows(tm):
            dh = (ALPHA * dz_ref[lo:hi, :] + _dot_nn(dg_ref[lo:hi, :], wg_ref[...])
                  + _dot_nn(du_ref[lo:hi, :], wu_ref[...]) + _dot_nt(ds_ref[lo:hi, :], wpg_ref[...]))
            xhat, rstd = _layer_norm_stats(z1_ref[lo:hi, :])
            dz1 = _layer_norm_bwd(dh * g1_ref[...], xhat, rstd)
            dz1b = dz1.astype(BF16)
            dz1_ref[lo:hi, :] = dz1
            dz1b_ref[lo:hi, :] = dz1b
            acc_ref[0:1, :] += jnp.sum(dh * xhat, axis=0, keepdims=True)
            acc_ref[1:2, :] += jnp.sum(dh, axis=0, keepdims=True)
            dr_ref[lo:hi, :] = _dot_nt(dz1b, wo_ref[0:RET_W, :]).astype(BF16)
            da_ref[lo:hi, :] = _dot_nt(dz1b, wo_ref[RET_W:RET_W + ATT_W, :]).astype(BF16)

    body, lead = _after(after, body)
    return pl.pallas_call(
        body, name="dh1_ln1_bwd", grid=(t // tm,),
        in_specs=lead + [_row_spec(tm, d), _row_spec(tm, f), _row_spec(tm, f), _row_spec(tm, d), _row_spec(tm, d),
                         _resident_spec(wt_gate.shape), _resident_spec(wt_up.shape), _resident_spec(w_pg.shape),
                         _resident_spec(w_out.shape), _full_spec(g1.shape)],
        out_specs=[_row_spec(tm, d), _row_spec(tm, d), _row_spec(tm, RET_W), _row_spec(tm, ATT_W), _acc_spec((8, d))],
        out_shape=[SDS((t, d), F32), SDS((t, d), BF16), SDS((t, RET_W), BF16), SDS((t, ATT_W), BF16),
                   SDS((8, d), F32)],
        compiler_params=_params(("arbitrary",)),
    )(*after, dz2, dg, dup, dsb, z1, wt_gate, wt_up, w_pg, w_out, g1)


def _in_proj_bwd(dz1, parts, wt_in, after=()):
    t, d = dz1.shape
    tm = _row_tile(t, MATMUL_ROWS)
    widths = [p.shape[1] for p in parts]

    def body(*refs):
        dz_ref, part_refs, w_ref, dx_ref = refs[0], refs[1:1 + len(parts)], refs[-2], refs[-1]
        acc = ALPHA * dz_ref[...]
        lo = 0
        for p_ref, w in zip(part_refs, widths):
            acc = acc + _dot_nn(p_ref[...], w_ref[lo:lo + w, :])
            lo += w
        dx_ref[...] = acc

    body, lead = _after(after, body)
    return pl.pallas_call(
        body, name="in_proj_bwd", grid=(t // tm,),
        in_specs=lead + [_row_spec(tm, d)] + [_row_spec(tm, w) for w in widths] + [_full_spec(wt_in.shape)],
        out_specs=_row_spec(tm, d), out_shape=SDS((t, d), F32),
        compiler_params=_params(("parallel",)),
    )(*after, dz1, *parts, wt_in)


def _weight_grad(name, me, parts, rhs, after=()):
    t, n = rhs.shape
    widths = [p.shape[1] for p in parts]
    rows = sum(widths)
    own_rows = rows // N_DEV
    tk = _row_tile(t, MATMUL_ROWS)
    n_steps = t // tk
    step = 256

    def body(*refs):
        me_ref, part_refs, rhs_ref = refs[0], refs[1:1 + len(parts)], refs[1 + len(parts)]
        full_ref, own_ref, acc = refs[-3], refs[-2], refs[-1]
        i = pl.program_id(0)

        def products(first):
            b = rhs_ref[...].astype(BF16)
            lo = 0
            for p_ref, w in zip(part_refs, widths):
                for c0 in range(0, w, step):
                    c1 = min(c0 + step, w)
                    val = _dot_tn(p_ref[:, c0:c1].astype(BF16), b)
                    if first:
                        acc[lo + c0:lo + c1, :] = val
                    else:
                        acc[lo + c0:lo + c1, :] += val
                lo += w

        pl.when(i == 0)(functools.partial(products, True))
        pl.when(i > 0)(functools.partial(products, False))

        @pl.when(i == n_steps - 1)
        def _():
            full_ref[...] = acc[...].astype(BF16)
            own_ref[...] = acc[pl.ds(pl.multiple_of(me_ref[0] * own_rows, 8), own_rows), :]

    body, lead = _after(after, body)
    return pl.pallas_call(
        body, name=name, grid=(n_steps,),
        in_specs=lead + [_smem_spec()] + [_row_spec(tk, w) for w in widths] + [_row_spec(tk, n)],
        out_specs=[_full_spec((rows, n)), _full_spec((own_rows, n))],
        out_shape=[SDS((rows, n), BF16), SDS((own_rows, n), F32)],
        scratch_shapes=[pltpu.VMEM((rows, n), F32)],
        compiler_params=_params(("arbitrary",)),
    )(*after, me, *parts, rhs)


def _log_decay(decay_f, decay_b):
    def body(f_ref, b_ref, lf_ref, lb_ref):
        lf_ref[...] = jnp.log1p(-jnp.exp2(f_ref[...]))
        lb_ref[...] = jnp.log1p(-jnp.exp2(b_ref[...]))

    return pl.pallas_call(body, name="log_decay", out_shape=[SDS(decay_f.shape, F32)] * 2)(decay_f, decay_b)


def _chunk(ref, n):
    return ref[pl.ds(pl.multiple_of(n * CHUNK, CHUNK), CHUNK), :]


def _group_sum(is_a, v):
    sa = jnp.sum(jnp.where(is_a, v, 0.0), axis=1, keepdims=True)
    sb = jnp.sum(jnp.where(is_a, 0.0, v), axis=1, keepdims=True)
    return jnp.where(is_a, sa, sb)


def _seq_spec(s, col_block):
    return pl.BlockSpec((s, LANES), lambda b, h: (b, col_block + h))


def _smem_spec():
    return pl.BlockSpec(memory_space=pltpu.SMEM)


RET_UNROLL = 4


def _chunk_loop(n_chunks, body, init):
    u = RET_UNROLL if n_chunks % RET_UNROLL == 0 else 1

    def trip(i, carry):
        for j in range(u):
            carry = body(i * u + j, carry)
        return carry

    return lax.fori_loop(0, n_chunks // u, trip, init)


def _stacked_tables(lgf_ref, lgb_ref, pair):
    lane = lax.broadcasted_iota(jnp.int32, (1, LANES), 1)
    is_a = lane < HEAD_DIM
    lgf = jnp.where(is_a, lgf_ref[2 * pair], lgf_ref[2 * pair + 1])
    lgb = jnp.where(is_a, lgb_ref[2 * pair], lgb_ref[2 * pair + 1])
    row = lax.broadcasted_iota(jnp.int32, (CHUNK, 1), 0).astype(F32)
    kdec_f, qdec_f = jnp.exp(lgf * (CHUNK - 1.0 - row)), jnp.exp(lgf * (row + 1.0))
    kdec_b, qdec_b = jnp.exp(lgb * row), jnp.exp(lgb * (CHUNK - row))
    tab = dict(
        is_a=is_a, row=row, lam_f=jnp.exp(lgf * CHUNK), lam_b=jnp.exp(lgb * CHUNK),
        kdec=jnp.concatenate([kdec_f, kdec_b], axis=1), qdec=jnp.concatenate([qdec_f, qdec_b], axis=1),
        qexp=jnp.concatenate([jnp.broadcast_to(row + 1.0, (CHUNK, LANES)),
                              jnp.broadcast_to(CHUNK - row, (CHUNK, LANES))], axis=1),
        kexp=jnp.concatenate([jnp.broadcast_to(CHUNK - 1.0 - row, (CHUNK, LANES)),
                              jnp.broadcast_to(row, (CHUNK, LANES))], axis=1),
    )
    r = lax.broadcasted_iota(jnp.int32, (2 * LANES, LANES), 0)
    c = lax.broadcasted_iota(jnp.int32, (2 * LANES, LANES), 1)
    tab["diag2"] = ((r & (LANES - 1)) < HEAD_DIM) == (c < HEAD_DIM)
    i2 = lax.broadcasted_iota(jnp.int32, (2 * CHUNK, CHUNK), 0)
    j = lax.broadcasted_iota(jnp.int32, (2 * CHUNK, CHUNK), 1)
    head_b = i2 >= CHUNK
    diff = ((i2 & (CHUNK - 1)) - j).astype(F32)
    up, dn = jnp.maximum(diff, 0.0), jnp.maximum(-diff, 0.0)
    lgf2 = jnp.where(head_b, lgf_ref[2 * pair + 1], lgf_ref[2 * pair])
    lgb2 = jnp.where(head_b, lgb_ref[2 * pair + 1], lgb_ref[2 * pair])
    ef = jnp.where(diff >= 0, jnp.exp(lgf2 * up), 0.0)
    eb = jnp.where(diff <= 0, jnp.exp(lgb2 * dn), 0.0)
    tab["d2"] = ef + eb
    tab["df2"] = ef * up
    tab["db2"] = eb * dn
    return tab


def _stack_pair(is_a, x):
    zero = jnp.zeros_like(x)
    return jnp.concatenate([jnp.where(is_a, x, zero), jnp.where(is_a, zero, x)], axis=0)


def _unstack_pair(is_a, x2):
    return jnp.where(is_a, x2[0:CHUNK, :], x2[CHUNK:2 * CHUNK, :])


def _both_ways(x, dec):
    return (jnp.concatenate([x, x], axis=1) * dec).astype(BF16)


def _scan_states(n_chunks, st, up_rows, up_lam, down_rows, down_lam):
    zero = jnp.zeros((LANES, LANES), F32)

    def up(n, r):
        new = st[n, up_rows, :]
        st[n, up_rows, :] = r
        return r * up_lam + new

    def down(s, r):
        n = n_chunks - 1 - s
        new = st[n, down_rows, :]
        st[n, down_rows, :] = r
        return r * down_lam + new

    lax.fori_loop(0, n_chunks, up, zero)
    lax.fori_loop(0, n_chunks, down, zero)


FWD_ROWS, BWD_ROWS = pl.ds(0, LANES), pl.ds(LANES, LANES)


ST_GAIN, ST_XF, ST_XB, ST_IFA, ST_IFB, ST_IBA, ST_IBB, ST_LF, ST_LB = 0, 1, 2, 3, 4, 5, 6, 8, 9
ST_ROWS = 16


GW = GROUP * HEAD_DIM
KEYS = 3 * BLOCK


def _attn_tables(g, bias_ref):
    r = lax.broadcasted_iota(jnp.int32, (GROUP * BLOCK, KEYS), 0)
    kj = lax.broadcasted_iota(jnp.int32, (GROUP * BLOCK, KEYS), 1)
    qi = r & (BLOCK - 1)
    hh = lax.shift_right_logical(r, 7)
    dist = jnp.abs(kj - BLOCK - qi)
    slope = jnp.exp2(-(GROUP * g + hh + 1).astype(F32) * (8.0 / ATTN_HEADS))
    bias_ref[...] = jnp.where(dist <= BLOCK, -slope * dist.astype(F32), NEG_INF)


def _own_lanes(g):
    return lax.shift_right_logical(lax.broadcasted_iota(jnp.int32, (1, LANES), 1), 6) == g


def _mask_keys(x_ref, g, scale, pad_ref, s):
    pad_ref[0:BLOCK, :] = jnp.zeros((BLOCK, LANES), BF16)
    pad_ref[BLOCK + s:2 * BLOCK + s, :] = jnp.zeros((BLOCK, LANES), BF16)
    pad_ref[BLOCK:BLOCK + s, :] = jnp.where(_own_lanes(g), x_ref[...].astype(F32) * scale, 0.0).astype(BF16)


def _lane_block(x, j):
    return x[:, j * LANES:(j + 1) * LANES]


def _stack_heads(x, g):
    assert GROUP == 4 and GW == 2 * LANES
    x1 = pltpu.roll(x, HEAD_DIM, 1)
    keep = _own_lanes(g)
    zero = jnp.zeros((BLOCK, LANES), x.dtype)
    rows = []
    for h in range(GROUP):
        for_g0 = _lane_block(x, h // 2) if h % 2 == 0 else _lane_block(x1, ((h + 1) // 2) % 2)
        for_g1 = _lane_block(x, h // 2) if h % 2 == 1 else _lane_block(x1, h // 2)
        rows.append(jnp.where(keep, jnp.where(g == 0, for_g0, for_g1), zero))
    return jnp.concatenate(rows, axis=0)


def _unstack_heads(x4, g):
    p = [x4[h * BLOCK:(h + 1) * BLOCK, :] for h in range(GROUP)]
    cat = lambda a, b: jnp.concatenate([a, b], axis=1)
    in_place = jnp.where(g == 0, cat(p[0], p[2]), cat(p[1], p[3]))
    one_left = jnp.where(g == 0, cat(p[1], p[3]), cat(p[2], p[0]))
    return in_place + pltpu.roll(one_left, HEAD_DIM, 1)


def _sink_column(sink_ref, g):
    rh = lax.shift_right_logical(lax.broadcasted_iota(jnp.int32, (GROUP * BLOCK, 1), 0), 7)
    col = jnp.zeros((GROUP * BLOCK, 1), F32)
    for h in range(GROUP):
        col = jnp.where(rh == h, sink_ref[GROUP * g + h], col)
    return col


def _attn_probs(qm, k3, bias_ref, sink_col, n, s):
    logits = _dot_nt(qm, k3) + bias_ref[...]
    kpos = n * BLOCK - BLOCK + lax.broadcasted_iota(jnp.int32, (1, KEYS), 1)
    logits = jnp.where((kpos >= 0) & (kpos < s), logits, NEG_INF)
    m = jnp.maximum(jnp.max(logits, axis=1, keepdims=True), sink_col)
    e = jnp.exp(logits - m)
    e_sink = jnp.exp(sink_col - m)
    inv = 1.0 / (jnp.sum(e, axis=1, keepdims=True) + e_sink)
    return e * inv, e_sink * inv


PAIRS_PER_KV = (RET_HEADS // 2) // KV_HEADS


def _mixers_fwd(u, lgf, lgb, gn_gain, sink, b_loc, after=()):
    t = u.shape[0]
    s = t // b_loc
    n_chunks = s // CHUNK
    pairs = RET_HEADS // 2
    trips = n_chunks // RET_UNROLL
    blocks_half = (s // BLOCK) // PAIRS_PER_KV
    per_trip = blocks_half // trips
    assert n_chunks % RET_UNROLL == 0 and blocks_half % trips == 0 and PAIRS_PER_KV == 2

    def body(lgf_ref, lgb_ref, sink_ref, q_ref, k_ref, v_ref, g_ref, gain_ref, aq_ref, ak_ref, av_ref,
             r_ref, xhat_ref, rstd_ref, a_ref, st, kpad, vpad, bias):
        pair = pl.program_id(1)
        g, half = lax.shift_right_logical(pair, 1), pair & 1
        tab = _stacked_tables(lgf_ref, lgb_ref, pair)
        is_a = tab["is_a"]

        @pl.when(half == 0)
        def _():
            _attn_tables(g, bias)
            _mask_keys(ak_ref, g, Q_SCALE, kpad, s)
            _mask_keys(av_ref, g, 1.0, vpad, s)

        def kv_body(n, _):
            k8 = _chunk(k_ref, n).astype(F32) * Q_SCALE
            st[n] = jnp.where(tab["diag2"], _dot_tn(_both_ways(k8, tab["kdec"]), _chunk(v_ref, n)), 0.0)
            return 0

        _chunk_loop(n_chunks, kv_body, 0)
        _scan_states(n_chunks, st, FWD_ROWS, tab["lam_f"], BWD_ROWS, tab["lam_b"])
        sink_col = _sink_column(sink_ref, g)

        def retention_chunk(n):
            q = _chunk(q_ref, n)
            k8 = (_chunk(k_ref, n).astype(F32) * Q_SCALE).astype(BF16)
            v = _chunk(v_ref, n)
            p2 = (_dot_nt(_stack_pair(is_a, q), k8) * tab["d2"]).astype(BF16)
            y = _unstack_pair(is_a, _dot_nn(p2, v))
            y = y + _dot_nn(_both_ways(q.astype(F32), tab["qdec"]), st[n].astype(BF16))
            rows = pl.ds(pl.multiple_of(n * CHUNK, CHUNK), CHUNK)
            mu = _group_sum(is_a, y) * (1.0 / HEAD_DIM)
            dlt = y - mu
            var = _group_sum(is_a, dlt * dlt) * (1.0 / HEAD_DIM)
            rstd = lax.rsqrt(var + GN_EPS)
            xhat = dlt * rstd
            xhat_ref[rows, :] = xhat
            rstd_ref[rows, :] = rstd
            gate = _chunk(g_ref, n).astype(F32)
            r_ref[rows, :] = (xhat * gain_ref[...] * gate * _sigmoid(gate)).astype(BF16)

        def attention_block(blk):
            n = half * blocks_half + blk
            rows = pl.ds(pl.multiple_of(blk * BLOCK, BLOCK), BLOCK)
            keys = pl.ds(pl.multiple_of(n * BLOCK, BLOCK), KEYS)
            p, _ = _attn_probs(_stack_heads(aq_ref[rows, :], g), kpad[keys, :], bias, sink_col, n, s)
            a_ref[rows, :] = _unstack_heads(_dot_nn(p.astype(BF16), vpad[keys, :]), g).astype(BF16)

        def trip(i, _):
            for j in range(max(RET_UNROLL, per_trip)):
                if j < RET_UNROLL:
                    retention_chunk(i * RET_UNROLL + j)
                if j < per_trip:
                    attention_block(i * per_trip + j)
            return 0

        lax.fori_loop(0, trips, trip, 0)

    lane_blk = lambda c0: _seq_spec(s, c0 // LANES)
    half_rows = blocks_half * BLOCK
    aq_spec = pl.BlockSpec((half_rows, GW), lambda b, h: (b * PAIRS_PER_KV + (h & 1), C_AQ // GW + h // 2))
    a_spec = pl.BlockSpec((half_rows, GW), lambda b, h: (b * PAIRS_PER_KV + (h & 1), h // 2))
    kv_spec = lambda c0: pl.BlockSpec((s, LANES), lambda b, h: (b, c0 // LANES))
    pad = pltpu.VMEM((s + 2 * BLOCK, LANES), BF16)
    body, lead = _after(after, body)
    return pl.pallas_call(
        body, name="mixers_fwd", grid=(b_loc, pairs),
        in_specs=lead + [_smem_spec(), _smem_spec(), _smem_spec(), lane_blk(C_RQ), lane_blk(C_RK), lane_blk(C_RV),
                         lane_blk(C_RG), pl.BlockSpec((1, LANES), lambda b, h: (0, h)), aq_spec, kv_spec(C_AK),
                         kv_spec(C_AV)],
        out_specs=[_seq_spec(s, 0), _seq_spec(s, 0), _seq_spec(s, 0), a_spec],
        out_shape=[SDS((t, RET_W), BF16), SDS((t, RET_W), F32), SDS((t, RET_W), F32), SDS((t, ATT_W), BF16)],
        scratch_shapes=[pltpu.VMEM((n_chunks, 2 * LANES, LANES), F32), pad, pad,
                        pltpu.VMEM((GROUP * BLOCK, KEYS), F32)],
        compiler_params=_params(("arbitrary", "arbitrary")),
    )(*after, lgf, lgb, sink, u, u, u, u, gn_gain, u, u, u)


def _mixers_bwd(u, xhat, rstd, dr, da, lgf, lgb, gn_gain, sink, b_loc, after=()):
    t = u.shape[0]
    s = t // b_loc
    n_chunks = s // CHUNK
    pairs = RET_HEADS // 2
    trips = n_chunks // RET_UNROLL
    blocks_half = (s // BLOCK) // PAIRS_PER_KV
    per_trip = blocks_half // trips
    assert n_chunks % RET_UNROLL == 0 and blocks_half % trips == 0 and PAIRS_PER_KV == 2

    def body(lgf_ref, lgb_ref, sink_ref, q_ref, k_ref, v_ref, g_ref, xhat_ref, rstd_ref, dr_ref, gain_ref,
             aq_ref, ak_ref, av_ref, do_ref,
             dq_ref, dk_ref, dv_ref, dg_ref, st_ref, daq_ref, dak_ref, dav_ref, dsink_ref,
             st, gr, dy_s, kpad, vpad, bias, dk_acc, dv_acc):
        pair = pl.program_id(1)
        g, half = lax.shift_right_logical(pair, 1), pair & 1
        tab = _stacked_tables(lgf_ref, lgb_ref, pair)
        is_a = tab["is_a"]
        gain = gain_ref[...]

        @pl.when(half == 0)
        def _():
            _attn_tables(g, bias)
            _mask_keys(ak_ref, g, Q_SCALE, kpad, s)
            _mask_keys(av_ref, g, 1.0, vpad, s)
            dsink_ref[...] = jnp.zeros_like(dsink_ref)

        @pl.when(pair == 0)
        def _():
            dk_acc[...] = jnp.zeros_like(dk_acc)
            dv_acc[...] = jnp.zeros_like(dv_acc)

        def norm_body(n, dgain):
            rows = pl.ds(pl.multiple_of(n * CHUNK, CHUNK), CHUNK)
            xhat, rstd = xhat_ref[rows, :], rstd_ref[rows, :]
            gate = g_ref[rows, :].astype(F32)
            sg = _sigmoid(gate)
            silu = gate * sg
            d_out = dr_ref[rows, :].astype(F32)
            dg_ref[rows, :] = (d_out * xhat * gain * (sg * (1.0 + gate * (1.0 - sg)))).astype(BF16)
            dxh = d_out * gain * silu
            m1 = _group_sum(is_a, dxh) * (1.0 / HEAD_DIM)
            m2 = _group_sum(is_a, dxh * xhat) * (1.0 / HEAD_DIM)
            dy = (rstd * (dxh - m1 - xhat * m2)).astype(BF16)
            dy_s[rows, :] = dy
            k8 = k_ref[rows, :].astype(F32) * Q_SCALE
            st[n] = jnp.where(tab["diag2"], _dot_tn(_both_ways(k8, tab["kdec"]), v_ref[rows, :]), 0.0)
            qf = q_ref[rows, :].astype(F32)
            gr[n] = jnp.where(tab["diag2"], _dot_tn(_both_ways(qf, tab["qdec"]), dy), 0.0)
            return dgain + jnp.sum(d_out * xhat * silu, axis=0, keepdims=True)

        colsum = lambda x: jnp.sum(x, axis=0, keepdims=True)

        def grad_body(n, carry):
            xfb, ifa, ifb, iba, ibb, lf, lb = carry
            rows = pl.ds(pl.multiple_of(n * CHUNK, CHUNK), CHUNK)
            q = q_ref[rows, :]
            qf = q.astype(F32)
            k8f = k_ref[rows, :].astype(F32) * Q_SCALE
            k8 = k8f.astype(BF16)
            v = v_ref[rows, :]
            dy = dy_s[rows, :]
            q2, dy2 = _stack_pair(is_a, q), _stack_pair(is_a, dy)
            sc = _dot_nt(q2, k8)
            dp = _dot_nt(dy2, v)
            a2 = (sc * tab["d2"]).astype(BF16)
            ds2 = (dp * tab["d2"]).astype(BF16)
            dq = _unstack_pair(is_a, _dot_nn(ds2, k8))
            dk = _dot_tn(ds2, q2)
            dv = _dot_tn(a2, dy2)
            prod = sc * dp
            pf, pb = prod * tab["df2"], prod * tab["db2"]
            ifa, ifb = ifa + colsum(pf[0:CHUNK, :]), ifb + colsum(pf[CHUNK:2 * CHUNK, :])
            iba, ibb = iba + colsum(pb[0:CHUNK, :]), ibb + colsum(pb[CHUNK:2 * CHUNK, :])
            states, sgrads = st[n], gr[n]
            sb, gb = states.astype(BF16), sgrads.astype(BF16)
            dqc = _dot_nt(dy, sb) * tab["qdec"]
            dkc = _dot_nt(v, gb) * tab["kdec"]
            dv = dv + _dot_nn(_both_ways(k8f, tab["kdec"]), gb)
            dq_ref[rows, :] = (dq + dqc[:, 0:LANES] + dqc[:, LANES:2 * LANES]).astype(BF16)
            dk_ref[rows, :] = ((dk + dkc[:, 0:LANES] + dkc[:, LANES:2 * LANES]) * Q_SCALE).astype(BF16)
            dv_ref[rows, :] = dv.astype(BF16)
            q2w, k2w = jnp.concatenate([qf, qf], axis=1), jnp.concatenate([k8f, k8f], axis=1)
            xfb = xfb + colsum(tab["qexp"] * q2w * dqc + tab["kexp"] * k2w * dkc)
            prod_s = sgrads * states
            lf, lb = lf + colsum(prod_s[0:LANES, :]), lb + colsum(prod_s[LANES:2 * LANES, :])
            return xfb, ifa, ifb, iba, ibb, lf, lb

        sink_col = _sink_column(sink_ref, g)
        head_row = lax.broadcasted_iota(jnp.int32, dsink_ref.shape, 0)

        def attention_block(blk):
            n = half * blocks_half + blk
            rows = pl.ds(pl.multiple_of(blk * BLOCK, BLOCK), BLOCK)
            keys = pl.ds(pl.multiple_of(n * BLOCK, BLOCK), KEYS)
            qm = _stack_heads(aq_ref[rows, :], g)
            k3, v3 = kpad[keys, :], vpad[keys, :]
            p, p_sink = _attn_probs(qm, k3, bias, sink_col, n, s)
            dom = _stack_heads(do_ref[rows, :], g)
            dp = _dot_nt(dom, v3)
            delta = jnp.sum(p * dp, axis=1, keepdims=True)
            ds_mat = (p * (dp - delta)).astype(BF16)
            daq_ref[rows, :] = _unstack_heads(_dot_nn(ds_mat, k3), g).astype(BF16)
            dk_acc[keys, :] += _dot_tn(ds_mat, qm) * Q_SCALE
            dv_acc[keys, :] += _dot_tn(p.astype(BF16), dom)
            w = p_sink * delta
            upd = jnp.zeros(dsink_ref.shape, F32)
            for h in range(GROUP):
                upd = upd + jnp.where(head_row == h, -jnp.sum(w[h * BLOCK:(h + 1) * BLOCK, :]), 0.0)
            dsink_ref[...] += upd

        dgain = _chunk_loop(n_chunks, norm_body, jnp.zeros((1, LANES), F32))
        _scan_states(n_chunks, st, FWD_ROWS, tab["lam_f"], BWD_ROWS, tab["lam_b"])
        _scan_states(n_chunks, gr, BWD_ROWS, tab["lam_b"], FWD_ROWS, tab["lam_f"])

        def trip(i, carry):
            for j in range(max(RET_UNROLL, per_trip)):
                if j < RET_UNROLL:
                    carry = grad_body(i * RET_UNROLL + j, carry)
                if j < per_trip:
                    attention_block(i * per_trip + j)
            return carry

        z = jnp.zeros((1, LANES), F32)
        init = (jnp.zeros((1, 2 * LANES), F32), z, z, z, z, z, z)
        xfb, ifa, ifb, iba, ibb, lf, lb = lax.fori_loop(0, trips, trip, init)
        st_ref[...] = jnp.zeros_like(st_ref)
        st_ref[ST_GAIN:ST_GAIN + 1, :] = dgain
        st_ref[ST_XF:ST_XF + 1, :] = xfb[:, 0:LANES]
        st_ref[ST_XB:ST_XB + 1, :] = xfb[:, LANES:2 * LANES]
        st_ref[ST_IFA:ST_IFA + 1, :] = ifa
        st_ref[ST_IFB:ST_IFB + 1, :] = ifb
        st_ref[ST_IBA:ST_IBA + 1, :] = iba
        st_ref[ST_IBB:ST_IBB + 1, :] = ibb
        st_ref[ST_LF:ST_LF + 1, :] = lf * (CHUNK * tab["lam_f"])
        st_ref[ST_LB:ST_LB + 1, :] = lb * (CHUNK * tab["lam_b"])

        @pl.when(pair == pairs - 1)
        def _():
            dak_ref[...] = dk_acc[BLOCK:BLOCK + s, :].astype(BF16)
            dav_ref[...] = dv_acc[BLOCK:BLOCK + s, :].astype(BF16)

    lane_blk = lambda c0: _seq_spec(s, c0 // LANES)
    seq0 = _seq_spec(s, 0)
    half_rows = blocks_half * BLOCK
    aq_spec = pl.BlockSpec((half_rows, GW), lambda b, h: (b * PAIRS_PER_KV + (h & 1), C_AQ // GW + h // 2))
    a_spec = pl.BlockSpec((half_rows, GW), lambda b, h: (b * PAIRS_PER_KV + (h & 1), h // 2))
    kv_spec = lambda c0: pl.BlockSpec((s, LANES), lambda b, h: (b, c0 // LANES))
    kv_out = pl.BlockSpec((s, LANES), lambda b, h: (b, 0))
    state = pltpu.VMEM((n_chunks, 2 * LANES, LANES), F32)
    pad = pltpu.VMEM((s + 2 * BLOCK, LANES), BF16)
    acc = pltpu.VMEM((s + 2 * BLOCK, LANES), F32)
    body, lead = _after(after, body)
    return pl.pallas_call(
        body, name="mixers_bwd", grid=(b_loc, pairs),
        in_specs=lead + [_smem_spec(), _smem_spec(), _smem_spec(), lane_blk(C_RQ), lane_blk(C_RK), lane_blk(C_RV),
                         lane_blk(C_RG), seq0, seq0, seq0, pl.BlockSpec((1, LANES), lambda b, h: (0, h)),
                         aq_spec, kv_spec(C_AK), kv_spec(C_AV), a_spec],
        out_specs=[seq0] * 4 + [pl.BlockSpec((ST_ROWS, LANES), lambda b, h: (b, h)), a_spec, kv_out, kv_out,
                                pl.BlockSpec((8, LANES), lambda b, h: (b * KV_HEADS + h // 2, 0))],
        out_shape=[SDS((t, RET_W), BF16)] * 4 + [SDS((b_loc * ST_ROWS, RET_W), F32), SDS((t, ATT_W), BF16),
                                                   SDS((t, KV_W), BF16), SDS((t, KV_W), BF16),
                                                   SDS((b_loc * KV_HEADS * 8, LANES), F32)],
        scratch_shapes=[state, state, pltpu.VMEM((s, LANES), BF16), pad, pad,
                        pltpu.VMEM((GROUP * BLOCK, KEYS), F32), acc, acc],
        compiler_params=_params(("arbitrary", "arbitrary")),
    )(*after, lgf, lgb, sink, u, u, u, u, xhat, rstd, dr, gn_gain, u, u, u, da)


def _pack_small(acc2, acc1, ret_stats, dsink, b_loc, d):
    pairs = RET_HEADS // 2

    def body(acc2_ref, acc1_ref, st_ref, dsink_ref, out_ref):
        out_ref[...] = jnp.zeros_like(out_ref)
        out_ref[ROW_LN1G:ROW_LN1G + 1, :] = acc1_ref[0:1, :]
        out_ref[ROW_LN1B:ROW_LN1B + 1, :] = acc1_ref[1:2, :]
        out_ref[ROW_LN2G:ROW_LN2G + 1, :] = acc2_ref[1:2, :]
        out_ref[ROW_LN2B:ROW_LN2B + 1, :] = acc2_ref[2:3, :]
        out_ref[ROW_LOSS:ROW_LOSS + 1, :] = acc2_ref[0:1, :]
        st = st_ref[0:ST_ROWS, :]
        for b in range(1, b_loc):
            st = st + st_ref[b * ST_ROWS:(b + 1) * ST_ROWS, :]
        out_ref[ROW_GN:ROW_GN + 1, 0:RET_W] = st[ST_GAIN:ST_GAIN + 1, :]
        lane = lax.broadcasted_iota(jnp.int32, (1, d), 1)
        misc = jnp.zeros((1, d), F32)
        for pr in range(pairs):
            blk = st[:, pr * LANES:(pr + 1) * LANES]
            half = lax.broadcasted_iota(jnp.int32, (1, LANES), 1) < HEAD_DIM
            for h in range(2):
                sel = half if h == 0 else jnp.logical_not(half)
                cross_f = jnp.sum(jnp.where(sel, blk[ST_XF:ST_XF + 1, :] + blk[ST_LF:ST_LF + 1, :], 0.0))
                cross_b = jnp.sum(jnp.where(sel, blk[ST_XB:ST_XB + 1, :] + blk[ST_LB:ST_LB + 1, :], 0.0))
                intra_f = jnp.sum(blk[ST_IFA + h:ST_IFA + h + 1, :])
                intra_b = jnp.sum(blk[ST_IBA + h:ST_IBA + h + 1, :])
                head = 2 * pr + h
                misc = jnp.where(lane == MISC_DF + head, cross_f + intra_f, misc)
                misc = jnp.where(lane == MISC_DB + head, cross_b + intra_b, misc)
        for g in range(KV_HEADS):
            tot = dsink_ref[g * 8:(g + 1) * 8, :]
            for b in range(1, b_loc):
                tot = tot + dsink_ref[(b * KV_HEADS + g) * 8:(b * KV_HEADS + g + 1) * 8, :]
            for h in range(GROUP):
                misc = jnp.where(lane == MISC_SINK + GROUP * g + h, jnp.sum(tot[h:h + 1, 0:1]), misc)
        out_ref[ROW_MISC:ROW_MISC + 1, :] = misc

    return pl.pallas_call(body, name="pack_small", out_shape=SDS((SMALL_ROWS, d), F32))(acc2, acc1, ret_stats, dsink)


BIG = ("w_in", "w_out", "w_ffn_gate", "w_ffn_up", "w_ffn_down", "w_ple_proj", "w_ple_gate")
TRANSPOSED_OUTSIDE = ("w_in", "w_ffn_gate", "w_ffn_up")
TRANSPOSED_HERE = ("w_ple_proj",)
SMALL = ("ret_decay_fwd", "ret_decay_bwd", "ret_gn_gain", "attn_sink", "ln1_gain", "ln1_bias", "ln2_gain", "ln2_bias")
ORDER = ("w_in", "ret_decay_fwd", "ret_decay_bwd", "ret_gn_gain", "attn_sink", "w_out", "ln1_gain", "ln1_bias",
         "w_ffn_gate", "w_ffn_up", "w_ffn_down", "w_ple_proj", "w_ple_gate", "ln2_gain", "ln2_bias")


GATHER_ORDER = ("w_in", "w_ffn_up", "w_out", "w_ffn_gate", "w_ple_gate", "w_ple_proj", "w_ffn_down")
GATHER_TWO_LEVEL = ("w_in", "w_ffn_up")


def _local_step(x2, p2, target2, fetch, publish, small, b_loc, me):
    d = x2.shape[1]
    lgf, lgb = _log_decay(small["ret_decay_fwd"], small["ret_decay_bwd"])
    lgf1, lgb1, sink1 = lgf.reshape(-1), lgb.reshape(-1), small["attn_sink"].reshape(-1)
    (w_in,) = fetch(("w_in",), ())
    u, xb = _in_proj(x2, w_in)
    passed = fetch.pass_on("w_ffn_up", (xb,))
    r, ret_xhat, ret_rstd, a = _mixers_fwd(u, lgf1, lgb1, small["ret_gn_gain"], sink1, b_loc, passed)
    w_out, w_gate, w_up = fetch(("w_out", "w_ffn_gate", "w_ffn_up"), (r, a))
    z1, h1b, dact_dg, dact_du, act = _mix_ln1_ffn_up(
        r, a, x2, w_out, w_gate, w_up, small["ln1_gain"], small["ln1_bias"])
    w_pg, w_pe, w_down = fetch(("w_ple_gate", "w_ple_proj", "w_ffn_down"), (act,))
    dz2, dz2b, dsb, dpleb, dg, dup, acc2 = _ffn_down_ln2_loss(
        act, dact_dg, dact_du, h1b, p2, z1, target2, w_down, w_pg, w_pe,
        small["ln1_gain"], small["ln1_bias"], small["ln2_gain"], small["ln2_bias"])
    own = {}

    def grad(name, parts, rhs, after=()):
        whole, own[name] = _weight_grad("grad_" + name, me, parts, rhs, after)
        return whole

    t2 = publish("ffn", dict(w_ffn_down=grad("w_ffn_down", [act], dz2b),
                             w_ple_proj=grad("w_ple_proj", [dpleb], p2),
                             w_ple_gate=grad("w_ple_gate", [h1b], dsb),
                             w_ffn_gate=grad("w_ffn_gate", [dg], h1b),
                             w_ffn_up=grad("w_ffn_up", [dup], h1b)))
    dz1, dz1b, dr, da, acc1 = _dh1_ln1_bwd(dz2, dg, dup, dsb, z1, w_gate, w_up, w_pg, w_out, small["ln1_gain"], t2)
    t3 = publish("out", dict(w_out=grad("w_out", [r, a], dz1b)))
    dq, dk, dv, dgate, ret_stats, daq, dak, dav, dsink = _mixers_bwd(
        u, ret_xhat, ret_rstd, dr, da, lgf1, lgb1, small["ret_gn_gain"], sink1, b_loc, t3)
    parts = [dq, dk, dv, dgate, daq, dak, dav]
    small_part = _pack_small(acc2, acc1, ret_stats, dsink, b_loc, d)
    t4 = publish("in", dict(w_in=grad("w_in", parts, xb)), small_part)
    grad_x = _in_proj_bwd(dz1, parts, w_in, t4)
    return grad_x, own, small_part


def kernel(x, p, w_in, ret_decay_fwd, ret_decay_bwd, ret_gn_gain, attn_sink, w_out, ln1_gain, ln1_bias, w_ffn_gate, w_ffn_up, w_ffn_down, w_ple_proj, w_ple_gate, ln2_gain, ln2_bias, loss_target, m_w_in, m_ret_decay_fwd, m_ret_decay_bwd, m_ret_gn_gain, m_attn_sink, m_w_out, m_ln1_gain, m_ln1_bias, m_w_ffn_gate, m_w_ffn_up, m_w_ffn_down, m_w_ple_proj, m_w_ple_gate, m_ln2_gain, m_ln2_bias, v_w_in, v_ret_decay_fwd, v_ret_decay_bwd, v_ret_gn_gain, v_attn_sink, v_w_out, v_ln1_gain, v_ln1_bias, v_w_ffn_gate, v_w_ffn_up, v_w_ffn_down, v_w_ple_proj, v_w_ple_gate, v_ln2_gain, v_ln2_bias):
    given = dict(locals())

    def strip(n, a):
        if n not in BIG:
            return a
        return a[0].T if n in TRANSPOSED_OUTSIDE else a[0]

    def restore(n, a):
        if n not in BIG:
            return a
        return (a.T if n in TRANSPOSED_OUTSIDE else a)[None]

    w = {n: strip(n, given[n]) for n in ORDER}
    m = {n: strip(n, given["m_" + n]) for n in ORDER}
    v = {n: strip(n, given["v_" + n]) for n in ORDER}
    b_loc, s, d = x.shape
    x2 = x.reshape(b_loc * s, d)
    p2 = p[0].reshape(b_loc * s, p.shape[-1])
    target2 = loss_target.reshape(b_loc * s, d)

    small = {n: w[n] for n in SMALL}
    me = (4 * lax.axis_index("x") + 2 * lax.axis_index("y") + lax.axis_index("c")).astype(jnp.int32).reshape(1)

    gathered = _prep_shards(me, {n: w[n] for n in BIG})
    gather = _split_copy_start(
        "gather_start", [(gathered[n],) for n in GATHER_ORDER],
        [_gather_copy_near if n in GATHER_TWO_LEVEL else _gather_copy for n in GATHER_ORDER])

    passing = {}

    def pass_on(n, after):
        near = _split_copy_wait("gather_wait_" + n + "_near", gather, [GATHER_ORDER.index(n)], list(after))
        passing[n] = _split_copy_start("gather_pass_" + n, near, [_gather_copy_pass])
        return (passing[n]["token"],)

    def fetch(names, after):
        out = {}
        for n in [n for n in names if n in GATHER_TWO_LEVEL]:
            if n not in passing:
                pass_on(n, after)
            out[n] = _split_copy_wait("gather_wait_" + n, passing[n], [0], list(after))[0][0]
        direct = [n for n in names if n not in GATHER_TWO_LEVEL]
        if direct:
            got = _split_copy_wait("gather_wait_" + direct[0], gather, [GATHER_ORDER.index(n) for n in direct],
                                   list(after))
            out.update({n: item[0] for n, item in zip(direct, got)})
        return [out[n] for n in names]

    scatters = []

    def publish(tag, products, small_sums=None):
        items = [(products[n], lax.empty((N_DEV - 1, products[n].shape[0] // N_DEV, products[n].shape[1]), BF16))
                 for n in products]
        copies = [_scatter_copy] * len(items)
        if small_sums is not None:
            items.append((small_sums, lax.empty((N_DEV - 1,) + small_sums.shape, F32)))
            copies.append(_small_copy)
        started = _split_copy_start("scatter_start_" + tag, items, copies)
        scatters.append((list(products), small_sums is not None, started))
        return (started["token"],)

    fetch.pass_on = pass_on
    grad_x, own, small_part = _local_step(x2, p2, target2, fetch, publish, small, b_loc, me)

    out_g, out_d, out_m, out_v = {}, {}, {}, {}
    after = [grad_x]
    for names, with_small, started in scatters:
        landed = _split_copy_wait("scatter_wait_" + names[0], started, list(range(len(started["items"]))), after)
        if with_small:
            loss, sg, sd, sm, sv = _small_adamw(
                me, small_part, landed[-1][1], small, {n: m[n] for n in SMALL}, {n: v[n] for n in SMALL})
            for dst, src in ((out_g, sg), (out_d, sd), (out_m, sm), (out_v, sv)):
                dst.update(src)
        recv = {n: item[1] for n, item in zip(names, landed)}
        alike = {}
        for n in names:
            alike.setdefault((own[n].shape, n in TRANSPOSED_HERE), []).append(n)
        for (_, transposed), ns in alike.items():
            res = _reduce_adamw(ns[0], [own[n] for n in ns], [recv[n] for n in ns], [w[n] for n in ns],
                                [m[n] for n in ns], [v[n] for n in ns], transposed)
            for dst, vals in zip((out_g, out_d, out_m, out_v), res):
                dst.update(zip(ns, vals))
        after = [out_v[names[-1]]]

    outs = [loss[0, 0], grad_x.reshape(x.shape)]
    for group in (out_g, out_d, out_m, out_v):
        outs += [restore(n, group[n]) for n in ORDER]
    return tuple(outs)
```

```python
import functools

import jax
import jax.numpy as jnp
from jax import lax
from jax.experimental import pallas as pl
from jax.experimental.pallas import tpu as pltpu

F32, BF16 = jnp.float32, jnp.bfloat16
SDS = jax.ShapeDtypeStruct
MESH = pl.DeviceIdType.MESH

N_DEV = 8
HEAD_DIM = 64
RET_HEADS = 8
ATTN_HEADS = 8
KV_HEADS = 2
GROUP = ATTN_HEADS // KV_HEADS
RET_W = RET_HEADS * HEAD_DIM
ATT_W = ATTN_HEADS * HEAD_DIM
KV_W = KV_HEADS * HEAD_DIM
LANES = 128
CHUNK = 128
BLOCK = 128
Q_SCALE = HEAD_DIM ** -0.5
ALPHA = 2.0 ** 0.25
LN_EPS = 1e-5
GN_EPS = 1e-5
NEG_INF = -1e30
C_RQ, C_RK, C_RV, C_RG = 0, RET_W, 2 * RET_W, 3 * RET_W
C_AQ = 4 * RET_W
C_AK = C_AQ + ATT_W
C_AV = C_AK + KV_W
IN_W = C_AV + KV_W

ADAM_LR = 0.001
ADAM_B1 = 0.9
ADAM_B2 = 0.999
ADAM_EPS = 1e-08
ADAM_WD = 0.01
ADAM_STEP = 10

VMEM_LIMIT = 56 * 1024 * 1024
MATMUL_ROWS = 512
EPILOGUE_ROWS = 256
SUB_ROWS = 256
SMALL_ROWS = 16
ROW_LN1G, ROW_LN1B, ROW_LN2G, ROW_LN2B, ROW_LOSS, ROW_GN, ROW_MISC = 0, 1, 2, 3, 4, 5, 6
MISC_DF, MISC_DB, MISC_SINK = 0, 8, 16


def _dot_nn(a, b):
    return lax.dot_general(a, b, (((1,), (0,)), ((), ())), preferred_element_type=F32)


def _dot_nt(a, b):
    return lax.dot_general(a, b, (((1,), (1,)), ((), ())), preferred_element_type=F32)


def _dot_tn(a, b):
    return lax.dot_general(a, b, (((0,), (0,)), ((), ())), preferred_element_type=F32)


def _params(sem=None, vmem=VMEM_LIMIT):
    kw = {"vmem_limit_bytes": vmem}
    if sem is not None:
        kw["dimension_semantics"] = sem
    return pltpu.CompilerParams(**kw)


def _row_tile(t, want=512):
    tm = want
    while t % tm:
        tm //= 2
    return tm


def _sigmoid(x):
    return jax.nn.sigmoid(x)


def _layer_norm_stats(z):
    mu = jnp.mean(z, axis=1, keepdims=True)
    d = z - mu
    var = jnp.mean(d * d, axis=1, keepdims=True)
    rstd = lax.rsqrt(var + LN_EPS)
    return d * rstd, rstd


def _layer_norm_bwd(dxh, xhat, rstd):
    m1 = jnp.mean(dxh, axis=1, keepdims=True)
    m2 = jnp.mean(dxh * xhat, axis=1, keepdims=True)
    return rstd * (dxh - m1 - xhat * m2)


def _prep_shards(me, shards):
    names = list(shards)

    def body(me_ref, *refs):
        for name, src, dst in zip(names, refs[:len(names)], refs[len(names):]):
            val = src[...]
            dst[...] = (val.T if name in TRANSPOSED_HERE else val).astype(BF16)

    shape = lambda n, a: a.shape[::-1] if n in TRANSPOSED_HERE else a.shape
    shapes = [shape(n, shards[n]) for n in names]
    out = pl.pallas_call(
        body, name="prep_shards",
        grid_spec=pltpu.PrefetchScalarGridSpec(
            num_scalar_prefetch=1, grid=(1,),
            in_specs=[pl.BlockSpec(shards[n].shape, lambda i, me_ref: (0, 0)) for n in names],
            out_specs=[pl.BlockSpec(s, lambda i, me_ref: (me_ref[0], 0)) for s in shapes]),
        out_shape=[SDS((N_DEV * s[0], s[1]), BF16) for s in shapes], compiler_params=_params(("arbitrary",)),
    )(me, *[shards[n] for n in names])
    return dict(zip(names, out))


def _mesh_pos():
    return lax.axis_index("x"), lax.axis_index("y"), lax.axis_index("c")


HBM_SPEC = pl.BlockSpec(memory_space=pltpu.HBM)
SEM_SPEC = pl.BlockSpec(memory_space=pltpu.SEMAPHORE)
ANY_SPEC = pl.BlockSpec(memory_space=pl.ANY)
SIDE_EFFECT = pltpu.SideEffectType.DATAFLOW_SIDE_EFFECTING
PEER_SEMS = pltpu.SemaphoreType.DMA((N_DEV - 1,))


def _in_hbm(a):
    return pltpu.with_memory_space_constraint(a, pltpu.HBM)


def _split_copy_start(name, items, copies):
    n = len(items)
    flat = [a for it in items for a in it]
    k = len(flat)

    def body(*refs):
        arr, sems = list(refs[:k]), refs[k:k + 2 * n]
        for i, it in enumerate(items):
            mine = [arr.pop(0) for _ in it]
            for m in range(1, N_DEV):
                cp = copies[i](m, mine, sems[i].at[m - 1], sems[n + i].at[m - 1])
                if cp is not None:
                    cp.start()
        token = refs[-1]
        token[...] = jnp.zeros_like(token)

    res = pl.pallas_call(
        body, name=name,
        out_shape=[PEER_SEMS] * (2 * n) + [pltpu.HBM(a.shape, a.dtype) for a in flat] + [SDS((8, LANES), F32)],
        in_specs=[HBM_SPEC] * k,
        out_specs=[SEM_SPEC] * (2 * n) + [HBM_SPEC] * k + [pl.BlockSpec(memory_space=pltpu.VMEM)],
        input_output_aliases={j: 2 * n + j for j in range(k)},
        compiler_params=pltpu.CompilerParams(has_side_effects=SIDE_EFFECT),
    )(*[_in_hbm(a) for a in flat])
    thru, out_items = list(res[2 * n:2 * n + k]), []
    for it in items:
        out_items.append(tuple(thru.pop(0) for _ in it))
    return dict(send=res[:n], recv=res[n:2 * n], items=out_items, token=res[-1], copies=copies)


def _split_copy_wait(name, started, which, after):
    items = [started["items"][i] for i in which]
    copies = [started["copies"][i] for i in which]
    n = len(items)
    flat = [a for it in items for a in it]
    k = len(flat)

    def body(*refs):
        arr, sems = list(refs[:k]), refs[k:k + 2 * n]
        for i, it in enumerate(items):
            mine = [arr.pop(0) for _ in it]
            for m in range(1, N_DEV):
                cp = copies[i](m, mine, sems[i].at[m - 1], sems[n + i].at[m - 1])
                if cp is not None:
                    cp.wait_send()
                    cp.wait_recv()

    res = pl.pallas_call(
        body, name=name,
        out_shape=[pltpu.HBM(a.shape, a.dtype) for a in flat],
        in_specs=[HBM_SPEC] * k + [SEM_SPEC] * (2 * n) + [ANY_SPEC] * len(after),
        out_specs=[HBM_SPEC] * k,
        input_output_aliases={j: j for j in range(k)},
        compiler_params=pltpu.CompilerParams(has_side_effects=SIDE_EFFECT),
    )(*flat, *[started["send"][i] for i in which], *[started["recv"][i] for i in which], *[_in_hbm(a) for a in after])
    thru, out_items = list(res), []
    for it in items:
        out_items.append(tuple(thru.pop(0) for _ in it))
    return out_items


def _gather_copy(m, refs, send_sem, recv_sem):
    (land_ref,) = refs
    r = land_ref.shape[0] // N_DEV
    mine = land_ref.at[pl.ds(pl.multiple_of(_peer_index(0) * r, 8), r), :]
    return pltpu.make_async_remote_copy(src_ref=mine, dst_ref=mine, send_sem=send_sem, recv_sem=recv_sem,
                                        device_id=_peer(m), device_id_type=MESH)


def _gather_copy_near(m, refs, send_sem, recv_sem):
    return _gather_copy(m, refs, send_sem, recv_sem) if m == 1 or m % 2 == 0 else None


def _gather_copy_pass(m, refs, send_sem, recv_sem):
    if m == 1 or m % 2 == 0:
        return None
    (land_ref,) = refs
    r = land_ref.shape[0] // N_DEV
    block = land_ref.at[pl.ds(pl.multiple_of(_peer_index(m ^ 1) * r, 8), r), :]
    return pltpu.make_async_remote_copy(src_ref=block, dst_ref=block, send_sem=send_sem, recv_sem=recv_sem,
                                        device_id=_peer(1), device_id_type=MESH)


def _small_copy(m, refs, send_sem, recv_sem):
    part_ref, land_ref = refs
    return pltpu.make_async_remote_copy(src_ref=part_ref, dst_ref=land_ref.at[m - 1], send_sem=send_sem,
                                        recv_sem=recv_sem, device_id=_peer(m), device_id_type=MESH)


def _scatter_copy(m, refs, send_sem, recv_sem):
    buf_ref, land_ref = refs
    r = buf_ref.shape[0] // N_DEV
    src = buf_ref.at[pl.ds(pl.multiple_of(_peer_index(m) * r, 8), r), :]
    return pltpu.make_async_remote_copy(src_ref=src, dst_ref=land_ref.at[m - 1], send_sem=send_sem,
                                        recv_sem=recv_sem, device_id=_peer(m), device_id_type=MESH)


def _peer(m):
    x, y, c = _mesh_pos()
    bx, by, bc = (m >> 2) & 1, (m >> 1) & 1, m & 1
    return (x ^ bx if bx else x, y ^ by if by else y, c ^ bc if bc else c)


def _peer_index(m):
    x, y, c = _mesh_pos()
    return (4 * x + 2 * y + c) ^ m


SMALL_PLACE = {
    "ln1_gain": (ROW_LN1G, 0), "ln1_bias": (ROW_LN1B, 0), "ln2_gain": (ROW_LN2G, 0), "ln2_bias": (ROW_LN2B, 0),
    "ret_gn_gain": (ROW_GN, 0), "ret_decay_fwd": (ROW_MISC, MISC_DF), "ret_decay_bwd": (ROW_MISC, MISC_DB),
    "attn_sink": (ROW_MISC, MISC_SINK)}


def _small_adamw(me, part, landed, w, m, v):
    d = part.shape[1]
    names = list(SMALL_PLACE)
    k = len(names)

    def body(*refs):
        me_ref, part_ref, land_ref = refs[:3]
        refs = refs[2:]
        w_refs, m_refs, v_refs = refs[1:1 + k], refs[1 + k:1 + 2 * k], refs[1 + 2 * k:1 + 3 * k]
        outs = refs[1 + 3 * k:1 + 7 * k + 1]
        tot_ref = refs[-1]
        loss_ref, g_refs, dl_refs = outs[0], outs[1:1 + k], outs[1 + k:1 + 2 * k]
        nm_refs, nv_refs = outs[1 + 2 * k:1 + 3 * k], outs[1 + 3 * k:1 + 4 * k]
        tot = jnp.zeros(part_ref.shape, F32)
        for dev in range(N_DEV):
            j = dev ^ me_ref[0]
            tot = tot + jnp.where(j == 0, part_ref[...], land_ref[jnp.maximum(j, 1) - 1])
        tot_ref[...] = tot
        loss_ref[...] = (0.5 / d) * jnp.sum(tot_ref[ROW_LOSS:ROW_LOSS + 1, :], axis=1, keepdims=True)
        for i, name in enumerate(names):
            row, lo = SMALL_PLACE[name]
            wv = w_refs[i][...]
            g = tot_ref[row:row + 1, lo:lo + wv.shape[1]]
            if name.startswith("ret_decay"):
                p2 = jnp.exp2(wv)
                g = g * (-p2 * jnp.log(2.0) / (1.0 - p2))
            g_refs[i][...] = g
            _adamw_store(g, wv, m_refs[i][...], v_refs[i][...], dl_refs[i], nm_refs[i], nv_refs[i])

    shapes = [SDS(w[n].shape, F32) for n in names]
    vm = pl.BlockSpec(memory_space=pltpu.VMEM)
    res = pl.pallas_call(
        body, name="small_adamw", out_shape=[SDS((1, 1), F32)] + shapes * 4,
        in_specs=[_smem_spec()] + [vm] * (2 + 3 * k), out_specs=[vm] * (1 + 4 * k),
        scratch_shapes=[pltpu.VMEM(part.shape, F32)],
    )(me, part, landed, *[w[n] for n in names], *[m[n] for n in names], *[v[n] for n in names])
    groups = [dict(zip(names, res[1 + j * k:1 + (j + 1) * k])) for j in range(4)]
    return (res[0], *groups)


def _adamw_store(g, w, m, v, dl_ref, nm_ref, nv_ref):
    m = ADAM_B1 * m + (1.0 - ADAM_B1) * g
    v = ADAM_B2 * v + (1.0 - ADAM_B2) * (g * g)
    m_hat = m / (1.0 - ADAM_B1 ** ADAM_STEP)
    v_hat = v / (1.0 - ADAM_B2 ** ADAM_STEP)
    dl_ref[...] = -ADAM_LR * (m_hat / (jnp.sqrt(v_hat) + ADAM_EPS) + ADAM_WD * w)
    nm_ref[...] = m
    nv_ref[...] = v


def _reduce_adamw(name, owns, recvs, ws, ms, vs, transposed):
    count = len(owns)
    rows, n = owns[0].shape
    steps = 1 if transposed or rows % 32 else 4
    rb = rows // steps

    def body(*refs):
        ins, outs = refs[:5 * count], refs[5 * count:]
        j = pl.program_id(0)
        for k in range(count):
            @pl.when(j == k)
            def _(k=k):
                own_ref, recv_ref, w_ref, m_ref, v_ref = ins[5 * k:5 * k + 5]
                g_ref, dl_ref, nm_ref, nv_ref = outs[4 * k:4 * k + 4]
                g = own_ref[...]
                for p in range(N_DEV - 1):
                    g = g + recv_ref[p].astype(F32)
                if transposed:
                    g = g.T
                g_ref[...] = g
                _adamw_store(g, w_ref[...], m_ref[...], v_ref[...], dl_ref, nm_ref, nv_ref)

    def turn(k):
        return lambda j, i: jnp.where(j == k, i, jnp.where(j < k, 0, steps - 1))

    in_specs, out_specs = [], []
    for k in range(count):
        at = turn(k)
        blk = pl.BlockSpec(ws[0].shape if transposed else (rb, n), lambda j, i, at=at: (at(j, i), 0))
        in_specs += [pl.BlockSpec((rb, n), lambda j, i, at=at: (at(j, i), 0)),
                     pl.BlockSpec((N_DEV - 1, rb, n), lambda j, i, at=at: (0, at(j, i), 0)), blk, blk, blk]
        out_specs += [blk] * 4
    res = pl.pallas_call(
        body, name="adamw_" + name, grid=(count, steps), in_specs=in_specs, out_specs=out_specs,
        out_shape=[SDS(ws[0].shape, F32)] * (4 * count), compiler_params=_params(("arbitrary", "arbitrary")),
    )(*[a for k in range(count) for a in (owns[k], recvs[k], ws[k], ms[k], vs[k])])
    return [list(res[j::4]) for j in range(4)]


def _row_spec(tm, width):
    return pl.BlockSpec((tm, width), lambda i: (i, 0))


def _full_spec(shape):
    return pl.BlockSpec(shape, lambda i: (0,) * len(shape))


_acc_spec = _full_spec


def _sub_rows(tm):
    step = min(SUB_ROWS, tm)
    return [(lo, lo + step) for lo in range(0, tm, step)]


def _in_proj(x2, wt_in):
    t, d = x2.shape
    u_w = wt_in.shape[0]
    tm = _row_tile(t, MATMUL_ROWS)

    def body(x_ref, w_ref, u_ref, xb_ref):
        xb = x_ref[...].astype(BF16)
        xb_ref[...] = xb
        u_ref[...] = _dot_nt(xb, w_ref[...]).astype(BF16)

    return pl.pallas_call(
        body, name="in_proj", grid=(t // tm,),
        in_specs=[_row_spec(tm, d), _full_spec(wt_in.shape)],
        out_specs=[_row_spec(tm, u_w), _row_spec(tm, d)],
        out_shape=[SDS((t, u_w), BF16), SDS((t, d), BF16)],
        compiler_params=_params(("parallel",)),
    )(x2, wt_in)


def _col_halves(f):
    n = f // LANES
    k = (n + 1) // 2 * LANES
    return [(0, k), (k, f)] if k < f else [(0, f)]


def _mix_ln1_ffn_up(r, a, x2, w_out, wt_gate, wt_up, g1, b1):
    t, d = x2.shape
    f = wt_gate.shape[0]
    tm = _row_tile(t, EPILOGUE_ROWS)

    def body(r_ref, a_ref, x_ref, wo_ref, wg_ref, wu_ref, g_ref, b_ref, z_ref, hb_ref, dg_ref, du_ref, act_ref):
        mix = _dot_nn(r_ref[...], wo_ref[0:RET_W, :]) + _dot_nn(a_ref[...], wo_ref[RET_W:RET_W + ATT_W, :])
        z = ALPHA * x_ref[...] + mix
        xhat, _ = _layer_norm_stats(z)
        z_ref[...] = z
        h = (xhat * g_ref[...] + b_ref[...]).astype(BF16)
        hb_ref[...] = h
        for lo, hi in _col_halves(f):
            g = _dot_nt(h, wg_ref[lo:hi, :])
            u = _dot_nt(h, wu_ref[lo:hi, :])
            sg = _sigmoid(g)
            silu = g * sg
            dg_ref[:, lo:hi] = (u * (sg * (1.0 + g * (1.0 - sg)))).astype(BF16)
            du_ref[:, lo:hi] = silu.astype(BF16)
            act_ref[:, lo:hi] = (silu * u).astype(BF16)

    wide, narrow = _row_spec(tm, f), _row_spec(tm, d)
    return pl.pallas_call(
        body, name="mix_ln1_ffn_up", grid=(t // tm,),
        in_specs=[_row_spec(tm, RET_W), _row_spec(tm, ATT_W), narrow, _resident_spec(w_out.shape),
                  _resident_spec(wt_gate.shape), _resident_spec(wt_up.shape), _full_spec(g1.shape),
                  _full_spec(b1.shape)],
        out_specs=[narrow, narrow, wide, wide, wide],
        out_shape=[SDS((t, d), F32), SDS((t, d), BF16)] + [SDS((t, f), BF16)] * 3,
        compiler_params=_params(("parallel",)),
    )(r, a, x2, w_out, wt_gate, wt_up, g1, b1)


def _ffn_down_ln2_loss(act, dact_dg, dact_du, h1b, p2, z1, target, w_down, w_pg, wt_pe, g1, b1, g2, b2):
    t, d = z1.shape
    f = act.shape[1]
    pdim = p2.shape[1]
    tm = _row_tile(t, EPILOGUE_ROWS)

    def body(act_ref, fg_ref, fu_ref, hb_ref, p_ref, z1_ref, tgt_ref, wd_ref, wpg_ref, wpe_ref, g1_ref, b1_ref,
             g2_ref, b2_ref, dz_ref, dzb_ref, ds_ref, dple_ref, dg_ref, du_ref, acc_ref):
        @pl.when(pl.program_id(0) == 0)
        def _():
            acc_ref[...] = jnp.zeros_like(acc_ref)

        for lo, hi in _sub_rows(tm):
            xhat1, _ = _layer_norm_stats(z1_ref[lo:hi, :])
            h1 = xhat1 * g1_ref[...] + b1_ref[...]
            ffn = _dot_nn(act_ref[lo:hi, :], wd_ref[...])
            pg = _sigmoid(_dot_nn(hb_ref[lo:hi, :], wpg_ref[...]))
            ple = _dot_nt(p_ref[lo:hi, :].astype(BF16), wpe_ref[...])
            z2 = ALPHA * h1 + ffn + pg * ple
            xhat2, rstd2 = _layer_norm_stats(z2)
            err = xhat2 * g2_ref[...] + b2_ref[...] - tgt_ref[lo:hi, :]
            dy = err * (1.0 / d)
            dz = _layer_norm_bwd(dy * g2_ref[...], xhat2, rstd2)
            dzb = dz.astype(BF16)
            dz_ref[lo:hi, :] = dz
            dzb_ref[lo:hi, :] = dzb
            ds_ref[lo:hi, :] = (dz * ple * pg * (1.0 - pg)).astype(BF16)
            dple_ref[lo:hi, :] = (dz * pg).astype(BF16)
            acc_ref[0:1, :] += jnp.sum(err * err, axis=0, keepdims=True)
            acc_ref[1:2, :] += jnp.sum(dy * xhat2, axis=0, keepdims=True)
            acc_ref[2:3, :] += jnp.sum(dy, axis=0, keepdims=True)
            for c0, c1 in _col_halves(f):
                da = _dot_nt(dzb, wd_ref[c0:c1, :])
                dg_ref[lo:hi, c0:c1] = (da * fg_ref[lo:hi, c0:c1].astype(F32)).astype(BF16)
                du_ref[lo:hi, c0:c1] = (da * fu_ref[lo:hi, c0:c1].astype(F32)).astype(BF16)

    vec = _full_spec(g1.shape)
    wide, narrow = _row_spec(tm, f), _row_spec(tm, d)
    return pl.pallas_call(
        body, name="ffn_down_ln2_loss", grid=(t // tm,),
        in_specs=[wide, wide, wide, narrow, _row_spec(tm, pdim), narrow, narrow,
                  _full_spec(w_down.shape), _full_spec(w_pg.shape), _full_spec(wt_pe.shape), vec, vec, vec, vec],
        out_specs=[narrow] * 4 + [wide, wide, _acc_spec((8, d))],
        out_shape=[SDS((t, d), F32), SDS((t, d), BF16), SDS((t, d), BF16), SDS((t, d), BF16),
                   SDS((t, f), BF16), SDS((t, f), BF16), SDS((8, d), F32)],
        compiler_params=_params(("arbitrary",)),
    )(act, dact_dg, dact_du, h1b, p2, z1, target, w_down, w_pg, wt_pe, g1, b1, g2, b2)


def _after(after, body):
    k = len(after)
    return (lambda *refs: body(*refs[k:])), [ANY_SPEC] * k


def _resident_spec(shape):
    return pl.BlockSpec(shape, lambda i: (0,) * len(shape), pipeline_mode=pl.Buffered(1))


def _dh1_ln1_bwd(dz2, dg, dup, dsb, z1, wt_gate, wt_up, w_pg, w_out, g1, after=()):
    t, d = dz2.shape
    f = dg.shape[1]
    tm = _row_tile(t, EPILOGUE_ROWS)

    def body(dz_ref, dg_ref, du_ref, ds_ref, z1_ref, wg_ref, wu_ref, wpg_ref, wo_ref, g1_ref,
             dz1_ref, dz1b_ref, dr_ref, da_ref, acc_ref):
        @pl.when(pl.program_id(0) == 0)
        def _():
            acc_ref[...] = jnp.zeros_like(acc_ref)

        for lo, hi in _sub_rows(tm):
            dh = (ALPHA * dz_ref[lo:hi, :] + _dot_nn(dg_ref[lo:hi, :], wg_ref[...])
                  + _dot_nn(du_ref[lo:hi, :], wu_ref[...]) + _dot_nt(ds_ref[lo:hi, :], wpg_ref[...]))
            xhat, rstd = _layer_norm_stats(z1_ref[lo:hi, :])
            dz1 = _layer_norm_bwd(dh * g1_ref[...], xhat, rstd)
            dz1b = dz1.astype(BF16)
            dz1_ref[lo:hi, :] = dz1
            dz1b_ref[lo:hi, :] = dz1b
            acc_ref[0:1, :] += jnp.sum(dh * xhat, axis=0, keepdims=True)
            acc_ref[1:2, :] += jnp.sum(dh, axis=0, keepdims=True)
            dr_ref[lo:hi, :] = _dot_nt(dz1b, wo_ref[0:RET_W, :]).astype(BF16)
            da_ref[lo:hi, :] = _dot_nt(dz1b, wo_ref[RET_W:RET_W + ATT_W, :]).astype(BF16)

    body, lead = _after(after, body)
    return pl.pallas_call(
        body, name="dh1_ln1_bwd", grid=(t // tm,),
        in_specs=lead + [_row_spec(tm, d), _row_spec(tm, f), _row_spec(tm, f), _row_spec(tm, d), _row_spec(tm, d),
                         _resident_spec(wt_gate.shape), _resident_spec(wt_up.shape), _resident_spec(w_pg.shape),
                         _resident_spec(w_out.shape), _full_spec(g1.shape)],
        out_specs=[_row_spec(tm, d), _row_spec(tm, d), _row_spec(tm, RET_W), _row_spec(tm, ATT_W), _acc_spec((8, d))],
        out_shape=[SDS((t, d), F32), SDS((t, d), BF16), SDS((t, RET_W), BF16), SDS((t, ATT_W), BF16),
                   SDS((8, d), F32)],
        compiler_params=_params(("arbitrary",)),
    )(*after, dz2, dg, dup, dsb, z1, wt_gate, wt_up, w_pg, w_out, g1)


def _in_proj_bwd(dz1, parts, wt_in, after=()):
    t, d = dz1.shape
    tm = _row_tile(t, MATMUL_ROWS)
    widths = [p.shape[1] for p in parts]

    def body(*refs):
        dz_ref, part_refs, w_ref, dx_ref = refs[0], refs[1:1 + len(parts)], refs[-2], refs[-1]
        acc = ALPHA * dz_ref[...]
        lo = 0
        for p_ref, w in zip(part_refs, widths):
            acc = acc + _dot_nn(p_ref[...], w_ref[lo:lo + w, :])
            lo += w
        dx_ref[...] = acc

    body, lead = _after(after, body)
    return pl.pallas_call(
        body, name="in_proj_bwd", grid=(t // tm,),
        in_specs=lead + [_row_spec(tm, d)] + [_row_spec(tm, w) for w in widths] + [_full_spec(wt_in.shape)],
        out_specs=_row_spec(tm, d), out_shape=SDS((t, d), F32),
        compiler_params=_params(("parallel",)),
    )(*after, dz1, *parts, wt_in)


def _weight_grad(name, me, parts, rhs, after=()):
    t, n = rhs.shape
    widths = [p.shape[1] for p in parts]
    rows = sum(widths)
    own_rows = rows // N_DEV
    tk = _row_tile(t, MATMUL_ROWS)
    n_steps = t // tk
    step = 256

    def body(*refs):
        me_ref, part_refs, rhs_ref = refs[0], refs[1:1 + len(parts)], refs[1 + len(parts)]
        full_ref, own_ref, acc = refs[-3], refs[-2], refs[-1]
        i = pl.program_id(0)

        def products(first):
            b = rhs_ref[...].astype(BF16)
            lo = 0
            for p_ref, w in zip(part_refs, widths):
                for c0 in range(0, w, step):
                    c1 = min(c0 + step, w)
                    val = _dot_tn(p_ref[:, c0:c1].astype(BF16), b)
                    if first:
                        acc[lo + c0:lo + c1, :] = val
                    else:
                        acc[lo + c0:lo + c1, :] += val
                lo += w

        pl.when(i == 0)(functools.partial(products, True))
        pl.when(i > 0)(functools.partial(products, False))

        @pl.when(i == n_steps - 1)
        def _():
            full_ref[...] = acc[...].astype(BF16)
            own_ref[...] = acc[pl.ds(pl.multiple_of(me_ref[0] * own_rows, 8), own_rows), :]

    body, lead = _after(after, body)
    return pl.pallas_call(
        body, name=name, grid=(n_steps,),
        in_specs=lead + [_smem_spec()] + [_row_spec(tk, w) for w in widths] + [_row_spec(tk, n)],
        out_specs=[_full_spec((rows, n)), _full_spec((own_rows, n))],
        out_shape=[SDS((rows, n), BF16), SDS((own_rows, n), F32)],
        scratch_shapes=[pltpu.VMEM((rows, n), F32)],
        compiler_params=_params(("arbitrary",)),
    )(*after, me, *parts, rhs)


def _log_decay(decay_f, decay_b):
    def body(f_ref, b_ref, lf_ref, lb_ref):
        lf_ref[...] = jnp.log1p(-jnp.exp2(f_ref[...]))
        lb_ref[...] = jnp.log1p(-jnp.exp2(b_ref[...]))

    return pl.pallas_call(body, name="log_decay", out_shape=[SDS(decay_f.shape, F32)] * 2)(decay_f, decay_b)


def _chunk(ref, n):
    return ref[pl.ds(pl.multiple_of(n * CHUNK, CHUNK), CHUNK), :]


def _group_sum(is_a, v):
    sa = jnp.sum(jnp.where(is_a, v, 0.0), axis=1, keepdims=True)
    sb = jnp.sum(jnp.where(is_a, 0.0, v), axis=1, keepdims=True)
    return jnp.where(is_a, sa, sb)


def _seq_spec(s, col_block):
    return pl.BlockSpec((s, LANES), lambda b, h: (b, col_block + h))


def _smem_spec():
    return pl.BlockSpec(memory_space=pltpu.SMEM)


RET_UNROLL = 4


def _chunk_loop(n_chunks, body, init):
    u = RET_UNROLL if n_chunks % RET_UNROLL == 0 else 1

    def trip(i, carry):
        for j in range(u):
            carry = body(i * u + j, carry)
        return carry

    return lax.fori_loop(0, n_chunks // u, trip, init)


def _stacked_tables(lgf_ref, lgb_ref, pair):
    lane = lax.broadcasted_iota(jnp.int32, (1, LANES), 1)
    is_a = lane < HEAD_DIM
    lgf = jnp.where(is_a, lgf_ref[2 * pair], lgf_ref[2 * pair + 1])
    lgb = jnp.where(is_a, lgb_ref[2 * pair], lgb_ref[2 * pair + 1])
    row = lax.broadcasted_iota(jnp.int32, (CHUNK, 1), 0).astype(F32)
    kdec_f, qdec_f = jnp.exp(lgf * (CHUNK - 1.0 - row)), jnp.exp(lgf * (row + 1.0))
    kdec_b, qdec_b = jnp.exp(lgb * row), jnp.exp(lgb * (CHUNK - row))
    tab = dict(
        is_a=is_a, row=row, lam_f=jnp.exp(lgf * CHUNK), lam_b=jnp.exp(lgb * CHUNK),
        kdec=jnp.concatenate([kdec_f, kdec_b], axis=1), qdec=jnp.concatenate([qdec_f, qdec_b], axis=1),
        qexp=jnp.concatenate([jnp.broadcast_to(row + 1.0, (CHUNK, LANES)),
                              jnp.broadcast_to(CHUNK - row, (CHUNK, LANES))], axis=1),
        kexp=jnp.concatenate([jnp.broadcast_to(CHUNK - 1.0 - row, (CHUNK, LANES)),
                              jnp.broadcast_to(row, (CHUNK, LANES))], axis=1),
    )
    r = lax.broadcasted_iota(jnp.int32, (2 * LANES, LANES), 0)
    c = lax.broadcasted_iota(jnp.int32, (2 * LANES, LANES), 1)
    tab["diag2"] = ((r & (LANES - 1)) < HEAD_DIM) == (c < HEAD_DIM)
    i2 = lax.broadcasted_iota(jnp.int32, (2 * CHUNK, CHUNK), 0)
    j = lax.broadcasted_iota(jnp.int32, (2 * CHUNK, CHUNK), 1)
    head_b = i2 >= CHUNK
    diff = ((i2 & (CHUNK - 1)) - j).astype(F32)
    up, dn = jnp.maximum(diff, 0.0), jnp.maximum(-diff, 0.0)
    lgf2 = jnp.where(head_b, lgf_ref[2 * pair + 1], lgf_ref[2 * pair])
    lgb2 = jnp.where(head_b, lgb_ref[2 * pair + 1], lgb_ref[2 * pair])
    ef = jnp.where(diff >= 0, jnp.exp(lgf2 * up), 0.0)
    eb = jnp.where(diff <= 0, jnp.exp(lgb2 * dn), 0.0)
    tab["d2"] = ef + eb
    tab["df2"] = ef * up
    tab["db2"] = eb * dn
    return tab


def _stack_pair(is_a, x):
    zero = jnp.zeros_like(x)
    return jnp.concatenate([jnp.where(is_a, x, zero), jnp.where(is_a, zero, x)], axis=0)


def _unstack_pair(is_a, x2):
    return jnp.where(is_a, x2[0:CHUNK, :], x2[CHUNK:2 * CHUNK, :])


def _both_ways(x, dec):
    return (jnp.concatenate([x, x], axis=1) * dec).astype(BF16)


def _scan_states(n_chunks, st, up_rows, up_lam, down_rows, down_lam):
    zero = jnp.zeros((LANES, LANES), F32)

    def up(n, r):
        new = st[n, up_rows, :]
        st[n, up_rows, :] = r
        return r * up_lam + new

    def down(s, r):
        n = n_chunks - 1 - s
        new = st[n, down_rows, :]
        st[n, down_rows, :] = r
        return r * down_lam + new

    lax.fori_loop(0, n_chunks, up, zero)
    lax.fori_loop(0, n_chunks, down, zero)


FWD_ROWS, BWD_ROWS = pl.ds(0, LANES), pl.ds(LANES, LANES)


ST_GAIN, ST_XF, ST_XB, ST_IFA, ST_IFB, ST_IBA, ST_IBB, ST_LF, ST_LB = 0, 1, 2, 3, 4, 5, 6, 8, 9
ST_ROWS = 16


GW = GROUP * HEAD_DIM
KEYS = 3 * BLOCK


def _attn_tables(g, bias_ref):
    r = lax.broadcasted_iota(jnp.int32, (GROUP * BLOCK, KEYS), 0)
    kj = lax.broadcasted_iota(jnp.int32, (GROUP * BLOCK, KEYS), 1)
    qi = r & (BLOCK - 1)
    hh = lax.shift_right_logical(r, 7)
    dist = jnp.abs(kj - BLOCK - qi)
    slope = jnp.exp2(-(GROUP * g + hh + 1).astype(F32) * (8.0 / ATTN_HEADS))
    bias_ref[...] = jnp.where(dist <= BLOCK, -slope * dist.astype(F32), NEG_INF)


def _own_lanes(g):
    return lax.shift_right_logical(lax.broadcasted_iota(jnp.int32, (1, LANES), 1), 6) == g


def _mask_keys(x_ref, g, scale, pad_ref, s):
    pad_ref[0:BLOCK, :] = jnp.zeros((BLOCK, LANES), BF16)
    pad_ref[BLOCK + s:2 * BLOCK + s, :] = jnp.zeros((BLOCK, LANES), BF16)
    pad_ref[BLOCK:BLOCK + s, :] = jnp.where(_own_lanes(g), x_ref[...].astype(F32) * scale, 0.0).astype(BF16)


def _lane_block(x, j):
    return x[:, j * LANES:(j + 1) * LANES]


def _stack_heads(x, g):
    assert GROUP == 4 and GW == 2 * LANES
    x1 = pltpu.roll(x, HEAD_DIM, 1)
    keep = _own_lanes(g)
    zero = jnp.zeros((BLOCK, LANES), x.dtype)
    rows = []
    for h in range(GROUP):
        for_g0 = _lane_block(x, h // 2) if h % 2 == 0 else _lane_block(x1, ((h + 1) // 2) % 2)
        for_g1 = _lane_block(x, h // 2) if h % 2 == 1 else _lane_block(x1, h // 2)
        rows.append(jnp.where(keep, jnp.where(g == 0, for_g0, for_g1), zero))
    return jnp.concatenate(rows, axis=0)


def _unstack_heads(x4, g):
    p = [x4[h * BLOCK:(h + 1) * BLOCK, :] for h in range(GROUP)]
    cat = lambda a, b: jnp.concatenate([a, b], axis=1)
    in_place = jnp.where(g == 0, cat(p[0], p[2]), cat(p[1], p[3]))
    one_left = jnp.where(g == 0, cat(p[1], p[3]), cat(p[2], p[0]))
    return in_place + pltpu.roll(one_left, HEAD_DIM, 1)


def _sink_column(sink_ref, g):
    rh = lax.shift_right_logical(lax.broadcasted_iota(jnp.int32, (GROUP * BLOCK, 1), 0), 7)
    col = jnp.zeros((GROUP * BLOCK, 1), F32)
    for h in range(GROUP):
        col = jnp.where(rh == h, sink_ref[GROUP * g + h], col)
    return col


def _attn_probs(qm, k3, bias_ref, sink_col, n, s):
    logits = _dot_nt(qm, k3) + bias_ref[...]
    kpos = n * BLOCK - BLOCK + lax.broadcasted_iota(jnp.int32, (1, KEYS), 1)
    logits = jnp.where((kpos >= 0) & (kpos < s), logits, NEG_INF)
    m = jnp.maximum(jnp.max(logits, axis=1, keepdims=True), sink_col)
    e = jnp.exp(logits - m)
    e_sink = jnp.exp(sink_col - m)
    inv = 1.0 / (jnp.sum(e, axis=1, keepdims=True) + e_sink)
    return e * inv, e_sink * inv


PAIRS_PER_KV = (RET_HEADS // 2) // KV_HEADS
FWD_ORDER = "rrarra"
BWD_ORDER = "rararr"


def _trip_order(order, chunks, blocks):
    if order.count("r") == chunks and order.count("a") == blocks:
        return order
    return "r" * chunks + "a" * blocks


def _mixers_fwd(u, lgf, lgb, gn_gain, sink, b_loc, after=()):
    t = u.shape[0]
    s = t // b_loc
    n_chunks = s // CHUNK
    pairs = RET_HEADS // 2
    trips = n_chunks // RET_UNROLL
    blocks_half = (s // BLOCK) // PAIRS_PER_KV
    per_trip = blocks_half // trips
    assert n_chunks % RET_UNROLL == 0 and blocks_half % trips == 0 and PAIRS_PER_KV == 2

    def body(lgf_ref, lgb_ref, sink_ref, q_ref, k_ref, v_ref, g_ref, gain_ref, aq_ref, ak_ref, av_ref,
             r_ref, xhat_ref, rstd_ref, a_ref, st, kpad, vpad, bias):
        pair = pl.program_id(1)
        g, half = lax.shift_right_logical(pair, 1), pair & 1
        tab = _stacked_tables(lgf_ref, lgb_ref, pair)
        is_a = tab["is_a"]

        @pl.when(half == 0)
        def _():
            _attn_tables(g, bias)
            _mask_keys(ak_ref, g, Q_SCALE, kpad, s)
            _mask_keys(av_ref, g, 1.0, vpad, s)

        def kv_body(n, _):
            k8 = _chunk(k_ref, n).astype(F32) * Q_SCALE
            st[n] = jnp.where(tab["diag2"], _dot_tn(_both_ways(k8, tab["kdec"]), _chunk(v_ref, n)), 0.0)
            return 0

        _chunk_loop(n_chunks, kv_body, 0)
        _scan_states(n_chunks, st, FWD_ROWS, tab["lam_f"], BWD_ROWS, tab["lam_b"])
        sink_col = _sink_column(sink_ref, g)

        def retention_chunk(n):
            q = _chunk(q_ref, n)
            k8 = (_chunk(k_ref, n).astype(F32) * Q_SCALE).astype(BF16)
            v = _chunk(v_ref, n)
            p2 = (_dot_nt(_stack_pair(is_a, q), k8) * tab["d2"]).astype(BF16)
            y = _unstack_pair(is_a, _dot_nn(p2, v))
            y = y + _dot_nn(_both_ways(q.astype(F32), tab["qdec"]), st[n].astype(BF16))
            rows = pl.ds(pl.multiple_of(n * CHUNK, CHUNK), CHUNK)
            mu = _group_sum(is_a, y) * (1.0 / HEAD_DIM)
            dlt = y - mu
            var = _group_sum(is_a, dlt * dlt) * (1.0 / HEAD_DIM)
            rstd = lax.rsqrt(var + GN_EPS)
            xhat = dlt * rstd
            xhat_ref[rows, :] = xhat
            rstd_ref[rows, :] = rstd
            gate = _chunk(g_ref, n).astype(F32)
            r_ref[rows, :] = (xhat * gain_ref[...] * gate * _sigmoid(gate)).astype(BF16)

        def attention_block(blk):
            n = half * blocks_half + blk
            rows = pl.ds(pl.multiple_of(blk * BLOCK, BLOCK), BLOCK)
            keys = pl.ds(pl.multiple_of(n * BLOCK, BLOCK), KEYS)
            p, _ = _attn_probs(_stack_heads(aq_ref[rows, :], g), kpad[keys, :], bias, sink_col, n, s)
            a_ref[rows, :] = _unstack_heads(_dot_nn(p.astype(BF16), vpad[keys, :]), g).astype(BF16)

        def trip(i, _):
            chunk, blk = 0, 0
            for kind in _trip_order(FWD_ORDER, RET_UNROLL, per_trip):
                if kind == "r":
                    retention_chunk(i * RET_UNROLL + chunk)
                    chunk += 1
                else:
                    attention_block(i * per_trip + blk)
                    blk += 1
            return 0

        lax.fori_loop(0, trips, trip, 0)

    lane_blk = lambda c0: _seq_spec(s, c0 // LANES)
    half_rows = blocks_half * BLOCK
    aq_spec = pl.BlockSpec((half_rows, GW), lambda b, h: (b * PAIRS_PER_KV + (h & 1), C_AQ // GW + h // 2))
    a_spec = pl.BlockSpec((half_rows, GW), lambda b, h: (b * PAIRS_PER_KV + (h & 1), h // 2))
    kv_spec = lambda c0: pl.BlockSpec((s, LANES), lambda b, h: (b, c0 // LANES))
    pad = pltpu.VMEM((s + 2 * BLOCK, LANES), BF16)
    body, lead = _after(after, body)
    return pl.pallas_call(
        body, name="mixers_fwd", grid=(b_loc, pairs),
        in_specs=lead + [_smem_spec(), _smem_spec(), _smem_spec(), lane_blk(C_RQ), lane_blk(C_RK), lane_blk(C_RV),
                         lane_blk(C_RG), pl.BlockSpec((1, LANES), lambda b, h: (0, h)), aq_spec, kv_spec(C_AK),
                         kv_spec(C_AV)],
        out_specs=[_seq_spec(s, 0), _seq_spec(s, 0), _seq_spec(s, 0), a_spec],
        out_shape=[SDS((t, RET_W), BF16), SDS((t, RET_W), F32), SDS((t, RET_W), F32), SDS((t, ATT_W), BF16)],
        scratch_shapes=[pltpu.VMEM((n_chunks, 2 * LANES, LANES), F32), pad, pad,
                        pltpu.VMEM((GROUP * BLOCK, KEYS), F32)],
        compiler_params=_params(("arbitrary", "arbitrary")),
    )(*after, lgf, lgb, sink, u, u, u, u, gn_gain, u, u, u)


def _mixers_bwd(u, xhat, rstd, dr, da, lgf, lgb, gn_gain, sink, b_loc, after=()):
    t = u.shape[0]
    s = t // b_loc
    n_chunks = s // CHUNK
    pairs = RET_HEADS // 2
    trips = n_chunks // RET_UNROLL
    blocks_half = (s // BLOCK) // PAIRS_PER_KV
    per_trip = blocks_half // trips
    assert n_chunks % RET_UNROLL == 0 and blocks_half % trips == 0 and PAIRS_PER_KV == 2

    def body(lgf_ref, lgb_ref, sink_ref, q_ref, k_ref, v_ref, g_ref, xhat_ref, rstd_ref, dr_ref, gain_ref,
             aq_ref, ak_ref, av_ref, do_ref,
             dq_ref, dk_ref, dv_ref, dg_ref, st_ref, daq_ref, dak_ref, dav_ref, dsink_ref,
             st, gr, dy_s, kpad, vpad, bias, dk_acc, dv_acc):
        pair = pl.program_id(1)
        g, half = lax.shift_right_logical(pair, 1), pair & 1
        tab = _stacked_tables(lgf_ref, lgb_ref, pair)
        is_a = tab["is_a"]
        gain = gain_ref[...]

        @pl.when(half == 0)
        def _():
            _attn_tables(g, bias)
            _mask_keys(ak_ref, g, Q_SCALE, kpad, s)
            _mask_keys(av_ref, g, 1.0, vpad, s)
            dsink_ref[...] = jnp.zeros_like(dsink_ref)

        @pl.when(pair == 0)
        def _():
            dk_acc[...] = jnp.zeros_like(dk_acc)
            dv_acc[...] = jnp.zeros_like(dv_acc)

        def norm_body(n, dgain):
            rows = pl.ds(pl.multiple_of(n * CHUNK, CHUNK), CHUNK)
            xhat, rstd = xhat_ref[rows, :], rstd_ref[rows, :]
            gate = g_ref[rows, :].astype(F32)
            sg = _sigmoid(gate)
            silu = gate * sg
            d_out = dr_ref[rows, :].astype(F32)
            dg_ref[rows, :] = (d_out * xhat * gain * (sg * (1.0 + gate * (1.0 - sg)))).astype(BF16)
            dxh = d_out * gain * silu
            m1 = _group_sum(is_a, dxh) * (1.0 / HEAD_DIM)
            m2 = _group_sum(is_a, dxh * xhat) * (1.0 / HEAD_DIM)
            dy = (rstd * (dxh - m1 - xhat * m2)).astype(BF16)
            dy_s[rows, :] = dy
            k8 = k_ref[rows, :].astype(F32) * Q_SCALE
            st[n] = jnp.where(tab["diag2"], _dot_tn(_both_ways(k8, tab["kdec"]), v_ref[rows, :]), 0.0)
            qf = q_ref[rows, :].astype(F32)
            gr[n] = jnp.where(tab["diag2"], _dot_tn(_both_ways(qf, tab["qdec"]), dy), 0.0)
            return dgain + jnp.sum(d_out * xhat * silu, axis=0, keepdims=True)

        colsum = lambda x: jnp.sum(x, axis=0, keepdims=True)

        def grad_body(n, carry):
            xfb, ifa, ifb, iba, ibb, lf, lb = carry
            rows = pl.ds(pl.multiple_of(n * CHUNK, CHUNK), CHUNK)
            q = q_ref[rows, :]
            qf = q.astype(F32)
            k8f = k_ref[rows, :].astype(F32) * Q_SCALE
            k8 = k8f.astype(BF16)
            v = v_ref[rows, :]
            dy = dy_s[rows, :]
            q2, dy2 = _stack_pair(is_a, q), _stack_pair(is_a, dy)
            sc = _dot_nt(q2, k8)
            dp = _dot_nt(dy2, v)
            a2 = (sc * tab["d2"]).astype(BF16)
            ds2 = (dp * tab["d2"]).astype(BF16)
            dq = _unstack_pair(is_a, _dot_nn(ds2, k8))
            dk = _dot_tn(ds2, q2)
            dv = _dot_tn(a2, dy2)
            prod = sc * dp
            pf, pb = prod * tab["df2"], prod * tab["db2"]
            ifa, ifb = ifa + colsum(pf[0:CHUNK, :]), ifb + colsum(pf[CHUNK:2 * CHUNK, :])
            iba, ibb = iba + colsum(pb[0:CHUNK, :]), ibb + colsum(pb[CHUNK:2 * CHUNK, :])
            states, sgrads = st[n], gr[n]
            sb, gb = states.astype(BF16), sgrads.astype(BF16)
            dqc = _dot_nt(dy, sb) * tab["qdec"]
            dkc = _dot_nt(v, gb) * tab["kdec"]
            dv = dv + _dot_nn(_both_ways(k8f, tab["kdec"]), gb)
            dq_ref[rows, :] = (dq + dqc[:, 0:LANES] + dqc[:, LANES:2 * LANES]).astype(BF16)
            dk_ref[rows, :] = ((dk + dkc[:, 0:LANES] + dkc[:, LANES:2 * LANES]) * Q_SCALE).astype(BF16)
            dv_ref[rows, :] = dv.astype(BF16)
            q2w, k2w = jnp.concatenate([qf, qf], axis=1), jnp.concatenate([k8f, k8f], axis=1)
            xfb = xfb + colsum(tab["qexp"] * q2w * dqc + tab["kexp"] * k2w * dkc)
            prod_s = sgrads * states
            lf, lb = lf + colsum(prod_s[0:LANES, :]), lb + colsum(prod_s[LANES:2 * LANES, :])
            return xfb, ifa, ifb, iba, ibb, lf, lb

        sink_col = _sink_column(sink_ref, g)
        head_row = lax.broadcasted_iota(jnp.int32, dsink_ref.shape, 0)

        def attention_block(blk):
            n = half * blocks_half + blk
            rows = pl.ds(pl.multiple_of(blk * BLOCK, BLOCK), BLOCK)
            keys = pl.ds(pl.multiple_of(n * BLOCK, BLOCK), KEYS)
            qm = _stack_heads(aq_ref[rows, :], g)
            k3, v3 = kpad[keys, :], vpad[keys, :]
            p, p_sink = _attn_probs(qm, k3, bias, sink_col, n, s)
            dom = _stack_heads(do_ref[rows, :], g)
            dp = _dot_nt(dom, v3)
            delta = jnp.sum(p * dp, axis=1, keepdims=True)
            ds_mat = (p * (dp - delta)).astype(BF16)
            daq_ref[rows, :] = _unstack_heads(_dot_nn(ds_mat, k3), g).astype(BF16)
            dk_acc[keys, :] += _dot_tn(ds_mat, qm) * Q_SCALE
            dv_acc[keys, :] += _dot_tn(p.astype(BF16), dom)
            w = p_sink * delta
            upd = jnp.zeros(dsink_ref.shape, F32)
            for h in range(GROUP):
                upd = upd + jnp.where(head_row == h, -jnp.sum(w[h * BLOCK:(h + 1) * BLOCK, :]), 0.0)
            dsink_ref[...] += upd

        dgain = _chunk_loop(n_chunks, norm_body, jnp.zeros((1, LANES), F32))
        _scan_states(n_chunks, st, FWD_ROWS, tab["lam_f"], BWD_ROWS, tab["lam_b"])
        _scan_states(n_chunks, gr, BWD_ROWS, tab["lam_b"], FWD_ROWS, tab["lam_f"])

        def trip(i, carry):
            chunk, blk = 0, 0
            for kind in _trip_order(BWD_ORDER, RET_UNROLL, per_trip):
                if kind == "r":
                    carry = grad_body(i * RET_UNROLL + chunk, carry)
                    chunk += 1
                else:
                    attention_block(i * per_trip + blk)
                    blk += 1
            return carry

        z = jnp.zeros((1, LANES), F32)
        init = (jnp.zeros((1, 2 * LANES), F32), z, z, z, z, z, z)
        xfb, ifa, ifb, iba, ibb, lf, lb = lax.fori_loop(0, trips, trip, init)
        st_ref[...] = jnp.zeros_like(st_ref)
        st_ref[ST_GAIN:ST_GAIN + 1, :] = dgain
        st_ref[ST_XF:ST_XF + 1, :] = xfb[:, 0:LANES]
        st_ref[ST_XB:ST_XB + 1, :] = xfb[:, LANES:2 * LANES]
        st_ref[ST_IFA:ST_IFA + 1, :] = ifa
        st_ref[ST_IFB:ST_IFB + 1, :] = ifb
        st_ref[ST_IBA:ST_IBA + 1, :] = iba
        st_ref[ST_IBB:ST_IBB + 1, :] = ibb
        st_ref[ST_LF:ST_LF + 1, :] = lf * (CHUNK * tab["lam_f"])
        st_ref[ST_LB:ST_LB + 1, :] = lb * (CHUNK * tab["lam_b"])

        @pl.when(pair == pairs - 1)
        def _():
            dak_ref[...] = dk_acc[BLOCK:BLOCK + s, :].astype(BF16)
            dav_ref[...] = dv_acc[BLOCK:BLOCK + s, :].astype(BF16)

    lane_blk = lambda c0: _seq_spec(s, c0 // LANES)
    seq0 = _seq_spec(s, 0)
    half_rows = blocks_half * BLOCK
    aq_spec = pl.BlockSpec((half_rows, GW), lambda b, h: (b * PAIRS_PER_KV + (h & 1), C_AQ // GW + h // 2))
    a_spec = pl.BlockSpec((half_rows, GW), lambda b, h: (b * PAIRS_PER_KV + (h & 1), h // 2))
    kv_spec = lambda c0: pl.BlockSpec((s, LANES), lambda b, h: (b, c0 // LANES))
    kv_out = pl.BlockSpec((s, LANES), lambda b, h: (b, 0))
    state = pltpu.VMEM((n_chunks, 2 * LANES, LANES), F32)
    pad = pltpu.VMEM((s + 2 * BLOCK, LANES), BF16)
    acc = pltpu.VMEM((s + 2 * BLOCK, LANES), F32)
    body, lead = _after(after, body)
    return pl.pallas_call(
        body, name="mixers_bwd", grid=(b_loc, pairs),
        in_specs=lead + [_smem_spec(), _smem_spec(), _smem_spec(), lane_blk(C_RQ), lane_blk(C_RK), lane_blk(C_RV),
                         lane_blk(C_RG), seq0, seq0, seq0, pl.BlockSpec((1, LANES), lambda b, h: (0, h)),
                         aq_spec, kv_spec(C_AK), kv_spec(C_AV), a_spec],
        out_specs=[seq0] * 4 + [pl.BlockSpec((ST_ROWS, LANES), lambda b, h: (b, h)), a_spec, kv_out, kv_out,
                                pl.BlockSpec((8, LANES), lambda b, h: (b * KV_HEADS + h // 2, 0))],
        out_shape=[SDS((t, RET_W), BF16)] * 4 + [SDS((b_loc * ST_ROWS, RET_W), F32), SDS((t, ATT_W), BF16),
                                                   SDS((t, KV_W), BF16), SDS((t, KV_W), BF16),
                                                   SDS((b_loc * KV_HEADS * 8, LANES), F32)],
        scratch_shapes=[state, state, pltpu.VMEM((s, LANES), BF16), pad, pad,
                        pltpu.VMEM((GROUP * BLOCK, KEYS), F32), acc, acc],
        compiler_params=_params(("arbitrary", "arbitrary")),
    )(*after, lgf, lgb, sink, u, u, u, u, xhat, rstd, dr, gn_gain, u, u, u, da)


def _pack_small(acc2, acc1, ret_stats, dsink, b_loc, d):
    pairs = RET_HEADS // 2

    def body(acc2_ref, acc1_ref, st_ref, dsink_ref, out_ref):
        out_ref[...] = jnp.zeros_like(out_ref)
        out_ref[ROW_LN1G:ROW_LN1G + 1, :] = acc1_ref[0:1, :]
        out_ref[ROW_LN1B:ROW_LN1B + 1, :] = acc1_ref[1:2, :]
        out_ref[ROW_LN2G:ROW_LN2G + 1, :] = acc2_ref[1:2, :]
        out_ref[ROW_LN2B:ROW_LN2B + 1, :] = acc2_ref[2:3, :]
        out_ref[ROW_LOSS:ROW_LOSS + 1, :] = acc2_ref[0:1, :]
        st = st_ref[0:ST_ROWS, :]
        for b in range(1, b_loc):
            st = st + st_ref[b * ST_ROWS:(b + 1) * ST_ROWS, :]
        out_ref[ROW_GN:ROW_GN + 1, 0:RET_W] = st[ST_GAIN:ST_GAIN + 1, :]
        lane = lax.broadcasted_iota(jnp.int32, (1, d), 1)
        misc = jnp.zeros((1, d), F32)
        for pr in range(pairs):
            blk = st[:, pr * LANES:(pr + 1) * LANES]
            half = lax.broadcasted_iota(jnp.int32, (1, LANES), 1) < HEAD_DIM
            for h in range(2):
                sel = half if h == 0 else jnp.logical_not(half)
                cross_f = jnp.sum(jnp.where(sel, blk[ST_XF:ST_XF + 1, :] + blk[ST_LF:ST_LF + 1, :], 0.0))
                cross_b = jnp.sum(jnp.where(sel, blk[ST_XB:ST_XB + 1, :] + blk[ST_LB:ST_LB + 1, :], 0.0))
                intra_f = jnp.sum(blk[ST_IFA + h:ST_IFA + h + 1, :])
                intra_b = jnp.sum(blk[ST_IBA + h:ST_IBA + h + 1, :])
                head = 2 * pr + h
                misc = jnp.where(lane == MISC_DF + head, cross_f + intra_f, misc)
                misc = jnp.where(lane == MISC_DB + head, cross_b + intra_b, misc)
        for g in range(KV_HEADS):
            tot = dsink_ref[g * 8:(g + 1) * 8, :]
            for b in range(1, b_loc):
                tot = tot + dsink_ref[(b * KV_HEADS + g) * 8:(b * KV_HEADS + g + 1) * 8, :]
            for h in range(GROUP):
                misc = jnp.where(lane == MISC_SINK + GROUP * g + h, jnp.sum(tot[h:h + 1, 0:1]), misc)
        out_ref[ROW_MISC:ROW_MISC + 1, :] = misc

    return pl.pallas_call(body, name="pack_small", out_shape=SDS((SMALL_ROWS, d), F32))(acc2, acc1, ret_stats, dsink)


BIG = ("w_in", "w_out", "w_ffn_gate", "w_ffn_up", "w_ffn_down", "w_ple_proj", "w_ple_gate")
TRANSPOSED_OUTSIDE = ("w_in", "w_ffn_gate", "w_ffn_up")
TRANSPOSED_HERE = ("w_ple_proj",)
SMALL = ("ret_decay_fwd", "ret_decay_bwd", "ret_gn_gain", "attn_sink", "ln1_gain", "ln1_bias", "ln2_gain", "ln2_bias")
ORDER = ("w_in", "ret_decay_fwd", "ret_decay_bwd", "ret_gn_gain", "attn_sink", "w_out", "ln1_gain", "ln1_bias",
         "w_ffn_gate", "w_ffn_up", "w_ffn_down", "w_ple_proj", "w_ple_gate", "ln2_gain", "ln2_bias")


GATHER_ORDER = ("w_in", "w_ffn_up", "w_out", "w_ffn_gate", "w_ple_gate", "w_ple_proj", "w_ffn_down")
GATHER_TWO_LEVEL = ("w_in", "w_ffn_up")


def _local_step(x2, p2, target2, fetch, publish, small, b_loc, me):
    d = x2.shape[1]
    lgf, lgb = _log_decay(small["ret_decay_fwd"], small["ret_decay_bwd"])
    lgf1, lgb1, sink1 = lgf.reshape(-1), lgb.reshape(-1), small["attn_sink"].reshape(-1)
    (w_in,) = fetch(("w_in",), ())
    u, xb = _in_proj(x2, w_in)
    passed = fetch.pass_on("w_ffn_up", (xb,))
    r, ret_xhat, ret_rstd, a = _mixers_fwd(u, lgf1, lgb1, small["ret_gn_gain"], sink1, b_loc, passed)
    w_out, w_gate, w_up = fetch(("w_out", "w_ffn_gate", "w_ffn_up"), (r, a))
    z1, h1b, dact_dg, dact_du, act = _mix_ln1_ffn_up(
        r, a, x2, w_out, w_gate, w_up, small["ln1_gain"], small["ln1_bias"])
    w_pg, w_pe, w_down = fetch(("w_ple_gate", "w_ple_proj", "w_ffn_down"), (act,))
    dz2, dz2b, dsb, dpleb, dg, dup, acc2 = _ffn_down_ln2_loss(
        act, dact_dg, dact_du, h1b, p2, z1, target2, w_down, w_pg, w_pe,
        small["ln1_gain"], small["ln1_bias"], small["ln2_gain"], small["ln2_bias"])
    own = {}

    def grad(name, parts, rhs, after=()):
        whole, own[name] = _weight_grad("grad_" + name, me, parts, rhs, after)
        return whole

    t2 = publish("ffn", dict(w_ffn_down=grad("w_ffn_down", [act], dz2b),
                             w_ple_proj=grad("w_ple_proj", [dpleb], p2),
                             w_ple_gate=grad("w_ple_gate", [h1b], dsb),
                             w_ffn_gate=grad("w_ffn_gate", [dg], h1b),
                             w_ffn_up=grad("w_ffn_up", [dup], h1b)))
    dz1, dz1b, dr, da, acc1 = _dh1_ln1_bwd(dz2, dg, dup, dsb, z1, w_gate, w_up, w_pg, w_out, small["ln1_gain"], t2)
    t3 = publish("out", dict(w_out=grad("w_out", [r, a], dz1b)))
    dq, dk, dv, dgate, ret_stats, daq, dak, dav, dsink = _mixers_bwd(
        u, ret_xhat, ret_rstd, dr, da, lgf1, lgb1, small["ret_gn_gain"], sink1, b_loc, t3)
    parts = [dq, dk, dv, dgate, daq, dak, dav]
    small_part = _pack_small(acc2, acc1, ret_stats, dsink, b_loc, d)
    t4 = publish("in", dict(w_in=grad("w_in", parts, xb)), small_part)
    grad_x = _in_proj_bwd(dz1, parts, w_in, t4)
    return grad_x, own, small_part


def kernel(x, p, w_in, ret_decay_fwd, ret_decay_bwd, ret_gn_gain, attn_sink, w_out, ln1_gain, ln1_bias, w_ffn_gate, w_ffn_up, w_ffn_down, w_ple_proj, w_ple_gate, ln2_gain, ln2_bias, loss_target, m_w_in, m_ret_decay_fwd, m_ret_decay_bwd, m_ret_gn_gain, m_attn_sink, m_w_out, m_ln1_gain, m_ln1_bias, m_w_ffn_gate, m_w_ffn_up, m_w_ffn_down, m_w_ple_proj, m_w_ple_gate, m_ln2_gain, m_ln2_bias, v_w_in, v_ret_decay_fwd, v_ret_decay_bwd, v_ret_gn_gain, v_attn_sink, v_w_out, v_ln1_gain, v_ln1_bias, v_w_ffn_gate, v_w_ffn_up, v_w_ffn_down, v_w_ple_proj, v_w_ple_gate, v_ln2_gain, v_ln2_bias):
    given = dict(locals())

    def strip(n, a):
        if n not in BIG:
            return a
        return a[0].T if n in TRANSPOSED_OUTSIDE else a[0]

    def restore(n, a):
        if n not in BIG:
            return a
        return (a.T if n in TRANSPOSED_OUTSIDE else a)[None]

    w = {n: strip(n, given[n]) for n in ORDER}
    m = {n: strip(n, given["m_" + n]) for n in ORDER}
    v = {n: strip(n, given["v_" + n]) for n in ORDER}
    b_loc, s, d = x.shape
    x2 = x.reshape(b_loc * s, d)
    p2 = p[0].reshape(b_loc * s, p.shape[-1])
    target2 = loss_target.reshape(b_loc * s, d)

    small = {n: w[n] for n in SMALL}
    me = (4 * lax.axis_index("x") + 2 * lax.axis_index("y") + lax.axis_index("c")).astype(jnp.int32).reshape(1)

    gathered = _prep_shards(me, {n: w[n] for n in BIG})
    gather = _split_copy_start(
        "gather_start", [(gathered[n],) for n in GATHER_ORDER],
        [_gather_copy_near if n in GATHER_TWO_LEVEL else _gather_copy for n in GATHER_ORDER])

    passing = {}

    def pass_on(n, after):
        near = _split_copy_wait("gather_wait_" + n + "_near", gather, [GATHER_ORDER.index(n)], list(after))
        passing[n] = _split_copy_start("gather_pass_" + n, near, [_gather_copy_pass])
        return (passing[n]["token"],)

    def fetch(names, after):
        out = {}
        for n in [n for n in names if n in GATHER_TWO_LEVEL]:
            if n not in passing:
                pass_on(n, after)
            out[n] = _split_copy_wait("gather_wait_" + n, passing[n], [0], list(after))[0][0]
        direct = [n for n in names if n not in GATHER_TWO_LEVEL]
        if direct:
            got = _split_copy_wait("gather_wait_" + direct[0], gather, [GATHER_ORDER.index(n) for n in direct],
                                   list(after))
            out.update({n: item[0] for n, item in zip(direct, got)})
        return [out[n] for n in names]

    scatters = []

    def publish(tag, products, small_sums=None):
        items = [(products[n], lax.empty((N_DEV - 1, products[n].shape[0] // N_DEV, products[n].shape[1]), BF16))
                 for n in products]
        copies = [_scatter_copy] * len(items)
        if small_sums is not None:
            items.append((small_sums, lax.empty((N_DEV - 1,) + small_sums.shape, F32)))
            copies.append(_small_copy)
        started = _split_copy_start("scatter_start_" + tag, items, copies)
        scatters.append((list(products), small_sums is not None, started))
        return (started["token"],)

    fetch.pass_on = pass_on
    grad_x, own, small_part = _local_step(x2, p2, target2, fetch, publish, small, b_loc, me)

    out_g, out_d, out_m, out_v = {}, {}, {}, {}
    after = [grad_x]
    for names, with_small, started in scatters:
        landed = _split_copy_wait("scatter_wait_" + names[0], started, list(range(len(started["items"]))), after)
        if with_small:
            loss, sg, sd, sm, sv = _small_adamw(
                me, small_part, landed[-1][1], small, {n: m[n] for n in SMALL}, {n: v[n] for n in SMALL})
            for dst, src in ((out_g, sg), (out_d, sd), (out_m, sm), (out_v, sv)):
                dst.update(src)
        recv = {n: item[1] for n, item in zip(names, landed)}
        alike = {}
        for n in names:
            alike.setdefault((own[n].shape, n in TRANSPOSED_HERE), []).append(n)
        for (_, transposed), ns in alike.items():
            res = _reduce_adamw(ns[0], [own[n] for n in ns], [recv[n] for n in ns], [w[n] for n in ns],
                                [m[n] for n in ns], [v[n] for n in ns], transposed)
            for dst, vals in zip((out_g, out_d, out_m, out_v), res):
                dst.update(zip(ns, vals))
        after = [out_v[names[-1]]]

    outs = [loss[0, 0], grad_x.reshape(x.shape)]
    for group in (out_g, out_d, out_m, out_v):
        outs += [restore(n, group[n]) for n in ORDER]
    return tuple(outs)
```

```python
import functools

import jax
import jax.numpy as jnp
from jax import lax
from jax.experimental import pallas as pl
from jax.experimental.pallas import tpu as pltpu

F32, BF16 = jnp.float32, jnp.bfloat16
SDS = jax.ShapeDtypeStruct
MESH = pl.DeviceIdType.MESH

N_DEV = 8
HEAD_DIM = 64
RET_HEADS = 8
ATTN_HEADS = 8
KV_HEADS = 2
GROUP = ATTN_HEADS // KV_HEADS
RET_W = RET_HEADS * HEAD_DIM
ATT_W = ATTN_HEADS * HEAD_DIM
KV_W = KV_HEADS * HEAD_DIM
LANES = 128
CHUNK = 128
BLOCK = 128
Q_SCALE = HEAD_DIM ** -0.5
ALPHA = 2.0 ** 0.25
LN_EPS = 1e-5
GN_EPS = 1e-5
NEG_INF = -1e30
C_RQ, C_RK, C_RV, C_RG = 0, RET_W, 2 * RET_W, 3 * RET_W
C_AQ = 4 * RET_W
C_AK = C_AQ + ATT_W
C_AV = C_AK + KV_W
IN_W = C_AV + KV_W

ADAM_LR = 0.001
ADAM_B1 = 0.9
ADAM_B2 = 0.999
ADAM_EPS = 1e-08
ADAM_WD = 0.01
ADAM_STEP = 10

VMEM_LIMIT = 56 * 1024 * 1024
MATMUL_ROWS = 512
EPILOGUE_ROWS = 256
SUB_ROWS = 256
SMALL_ROWS = 16
ROW_LN1G, ROW_LN1B, ROW_LN2G, ROW_LN2B, ROW_LOSS, ROW_GN, ROW_MISC = 0, 1, 2, 3, 4, 5, 6
MISC_DF, MISC_DB, MISC_SINK = 0, 8, 16


def _dot_nn(a, b):
    return lax.dot_general(a, b, (((1,), (0,)), ((), ())), preferred_element_type=F32)


def _dot_nt(a, b):
    return lax.dot_general(a, b, (((1,), (1,)), ((), ())), preferred_element_type=F32)


def _dot_tn(a, b):
    return lax.dot_general(a, b, (((0,), (0,)), ((), ())), preferred_element_type=F32)


def _params(sem=None, vmem=VMEM_LIMIT):
    kw = {"vmem_limit_bytes": vmem}
    if sem is not None:
        kw["dimension_semantics"] = sem
    return pltpu.CompilerParams(**kw)


def _row_tile(t, want=512):
    tm = want
    while t % tm:
        tm //= 2
    return tm


def _sigmoid(x):
    return jax.nn.sigmoid(x)


def _layer_norm_stats(z):
    mu = jnp.mean(z, axis=1, keepdims=True)
    d = z - mu
    var = jnp.mean(d * d, axis=1, keepdims=True)
    rstd = lax.rsqrt(var + LN_EPS)
    return d * rstd, rstd


def _layer_norm_bwd(dxh, xhat, rstd):
    m1 = jnp.mean(dxh, axis=1, keepdims=True)
    m2 = jnp.mean(dxh * xhat, axis=1, keepdims=True)
    return rstd * (dxh - m1 - xhat * m2)


def _prep_shards(me, shards):
    names = list(shards)

    def body(me_ref, *refs):
        for name, src, dst in zip(names, refs[:len(names)], refs[len(names):]):
            val = src[...]
            dst[...] = (val.T if name in TRANSPOSED_HERE else val).astype(BF16)

    shape = lambda n, a: a.shape[::-1] if n in TRANSPOSED_HERE else a.shape
    shapes = [shape(n, shards[n]) for n in names]
    out = pl.pallas_call(
        body, name="prep_shards",
        grid_spec=pltpu.PrefetchScalarGridSpec(
            num_scalar_prefetch=1, grid=(1,),
            in_specs=[pl.BlockSpec(shards[n].shape, lambda i, me_ref: (0, 0)) for n in names],
            out_specs=[pl.BlockSpec(s, lambda i, me_ref: (me_ref[0], 0)) for s in shapes]),
        out_shape=[SDS((N_DEV * s[0], s[1]), BF16) for s in shapes], compiler_params=_params(("arbitrary",)),
    )(me, *[shards[n] for n in names])
    return dict(zip(names, out))


def _mesh_pos():
    return lax.axis_index("x"), lax.axis_index("y"), lax.axis_index("c")


HBM_SPEC = pl.BlockSpec(memory_space=pltpu.HBM)
SEM_SPEC = pl.BlockSpec(memory_space=pltpu.SEMAPHORE)
ANY_SPEC = pl.BlockSpec(memory_space=pl.ANY)
SIDE_EFFECT = pltpu.SideEffectType.DATAFLOW_SIDE_EFFECTING
PEER_SEMS = pltpu.SemaphoreType.DMA((N_DEV - 1,))


def _in_hbm(a):
    return pltpu.with_memory_space_constraint(a, pltpu.HBM)


def _split_copy_start(name, items, copies):
    n = len(items)
    flat = [a for it in items for a in it]
    k = len(flat)

    def body(*refs):
        arr, sems = list(refs[:k]), refs[k:k + 2 * n]
        for i, it in enumerate(items):
            mine = [arr.pop(0) for _ in it]
            for m in range(1, N_DEV):
                cp = copies[i](m, mine, sems[i].at[m - 1], sems[n + i].at[m - 1])
                if cp is not None:
                    cp.start()
        token = refs[-1]
        token[...] = jnp.zeros_like(token)

    res = pl.pallas_call(
        body, name=name,
        out_shape=[PEER_SEMS] * (2 * n) + [pltpu.HBM(a.shape, a.dtype) for a in flat] + [SDS((8, LANES), F32)],
        in_specs=[HBM_SPEC] * k,
        out_specs=[SEM_SPEC] * (2 * n) + [HBM_SPEC] * k + [pl.BlockSpec(memory_space=pltpu.VMEM)],
        input_output_aliases={j: 2 * n + j for j in range(k)},
        compiler_params=pltpu.CompilerParams(has_side_effects=SIDE_EFFECT),
    )(*[_in_hbm(a) for a in flat])
    thru, out_items = list(res[2 * n:2 * n + k]), []
    for it in items:
        out_items.append(tuple(thru.pop(0) for _ in it))
    return dict(send=res[:n], recv=res[n:2 * n], items=out_items, token=res[-1], copies=copies)


def _split_copy_wait(name, started, which, after):
    items = [started["items"][i] for i in which]
    copies = [started["copies"][i] for i in which]
    n = len(items)
    flat = [a for it in items for a in it]
    k = len(flat)

    def body(*refs):
        arr, sems = list(refs[:k]), refs[k:k + 2 * n]
        for i, it in enumerate(items):
            mine = [arr.pop(0) for _ in it]
            for m in range(1, N_DEV):
                cp = copies[i](m, mine, sems[i].at[m - 1], sems[n + i].at[m - 1])
                if cp is not None:
                    cp.wait_send()
                    cp.wait_recv()

    res = pl.pallas_call(
        body, name=name,
        out_shape=[pltpu.HBM(a.shape, a.dtype) for a in flat],
        in_specs=[HBM_SPEC] * k + [SEM_SPEC] * (2 * n) + [ANY_SPEC] * len(after),
        out_specs=[HBM_SPEC] * k,
        input_output_aliases={j: j for j in range(k)},
        compiler_params=pltpu.CompilerParams(has_side_effects=SIDE_EFFECT),
    )(*flat, *[started["send"][i] for i in which], *[started["recv"][i] for i in which], *[_in_hbm(a) for a in after])
    thru, out_items = list(res), []
    for it in items:
        out_items.append(tuple(thru.pop(0) for _ in it))
    return out_items


def _gather_copy(m, refs, send_sem, recv_sem):
    (land_ref,) = refs
    r = land_ref.shape[0] // N_DEV
    mine = land_ref.at[pl.ds(pl.multiple_of(_peer_index(0) * r, 8), r), :]
    return pltpu.make_async_remote_copy(src_ref=mine, dst_ref=mine, send_sem=send_sem, recv_sem=recv_sem,
                                        device_id=_peer(m), device_id_type=MESH)


def _gather_copy_near(m, refs, send_sem, recv_sem):
    return _gather_copy(m, refs, send_sem, recv_sem) if m == 1 or m % 2 == 0 else None


def _gather_copy_pass(m, refs, send_sem, recv_sem):
    if m == 1 or m % 2 == 0:
        return None
    (land_ref,) = refs
    r = land_ref.shape[0] // N_DEV
    block = land_ref.at[pl.ds(pl.multiple_of(_peer_index(m ^ 1) * r, 8), r), :]
    return pltpu.make_async_remote_copy(src_ref=block, dst_ref=block, send_sem=send_sem, recv_sem=recv_sem,
                                        device_id=_peer(1), device_id_type=MESH)


def _small_copy(m, refs, send_sem, recv_sem):
    part_ref, land_ref = refs
    return pltpu.make_async_remote_copy(src_ref=part_ref, dst_ref=land_ref.at[m - 1], send_sem=send_sem,
                                        recv_sem=recv_sem, device_id=_peer(m), device_id_type=MESH)


def _scatter_copy(m, refs, send_sem, recv_sem):
    buf_ref, land_ref = refs
    r = buf_ref.shape[0] // N_DEV
    src = buf_ref.at[pl.ds(pl.multiple_of(_peer_index(m) * r, 8), r), :]
    return pltpu.make_async_remote_copy(src_ref=src, dst_ref=land_ref.at[m - 1], send_sem=send_sem,
                                        recv_sem=recv_sem, device_id=_peer(m), device_id_type=MESH)


def _peer(m):
    x, y, c = _mesh_pos()
    bx, by, bc = (m >> 2) & 1, (m >> 1) & 1, m & 1
    return (x ^ bx if bx else x, y ^ by if by else y, c ^ bc if bc else c)


def _peer_index(m):
    x, y, c = _mesh_pos()
    return (4 * x + 2 * y + c) ^ m


SMALL_PLACE = {
    "ln1_gain": (ROW_LN1G, 0), "ln1_bias": (ROW_LN1B, 0), "ln2_gain": (ROW_LN2G, 0), "ln2_bias": (ROW_LN2B, 0),
    "ret_gn_gain": (ROW_GN, 0), "ret_decay_fwd": (ROW_MISC, MISC_DF), "ret_decay_bwd": (ROW_MISC, MISC_DB),
    "attn_sink": (ROW_MISC, MISC_SINK)}


def _small_adamw(me, part, landed, w, m, v):
    d = part.shape[1]
    names = list(SMALL_PLACE)
    k = len(names)

    def body(*refs):
        me_ref, part_ref, land_ref = refs[:3]
        refs = refs[2:]
        w_refs, m_refs, v_refs = refs[1:1 + k], refs[1 + k:1 + 2 * k], refs[1 + 2 * k:1 + 3 * k]
        outs = refs[1 + 3 * k:1 + 7 * k + 1]
        tot_ref = refs[-1]
        loss_ref, g_refs, dl_refs = outs[0], outs[1:1 + k], outs[1 + k:1 + 2 * k]
        nm_refs, nv_refs = outs[1 + 2 * k:1 + 3 * k], outs[1 + 3 * k:1 + 4 * k]
        tot = jnp.zeros(part_ref.shape, F32)
        for dev in range(N_DEV):
            j = dev ^ me_ref[0]
            tot = tot + jnp.where(j == 0, part_ref[...], land_ref[jnp.maximum(j, 1) - 1])
        tot_ref[...] = tot
        loss_ref[...] = (0.5 / d) * jnp.sum(tot_ref[ROW_LOSS:ROW_LOSS + 1, :], axis=1, keepdims=True)
        for i, name in enumerate(names):
            row, lo = SMALL_PLACE[name]
            wv = w_refs[i][...]
            g = tot_ref[row:row + 1, lo:lo + wv.shape[1]]
            if name.startswith("ret_decay"):
                p2 = jnp.exp2(wv)
                g = g * (-p2 * jnp.log(2.0) / (1.0 - p2))
            g_refs[i][...] = g
            _adamw_store(g, wv, m_refs[i][...], v_refs[i][...], dl_refs[i], nm_refs[i], nv_refs[i])

    shapes = [SDS(w[n].shape, F32) for n in names]
    vm = pl.BlockSpec(memory_space=pltpu.VMEM)
    res = pl.pallas_call(
        body, name="small_adamw", out_shape=[SDS((1, 1), F32)] + shapes * 4,
        in_specs=[_smem_spec()] + [vm] * (2 + 3 * k), out_specs=[vm] * (1 + 4 * k),
        scratch_shapes=[pltpu.VMEM(part.shape, F32)],
    )(me, part, landed, *[w[n] for n in names], *[m[n] for n in names], *[v[n] for n in names])
    groups = [dict(zip(names, res[1 + j * k:1 + (j + 1) * k])) for j in range(4)]
    return (res[0], *groups)


def _adamw_store(g, w, m, v, dl_ref, nm_ref, nv_ref):
    m = ADAM_B1 * m + (1.0 - ADAM_B1) * g
    v = ADAM_B2 * v + (1.0 - ADAM_B2) * (g * g)
    m_hat = m / (1.0 - ADAM_B1 ** ADAM_STEP)
    v_hat = v / (1.0 - ADAM_B2 ** ADAM_STEP)
    dl_ref[...] = -ADAM_LR * (m_hat / (jnp.sqrt(v_hat) + ADAM_EPS) + ADAM_WD * w)
    nm_ref[...] = m
    nv_ref[...] = v


def _reduce_adamw(name, owns, recvs, ws, ms, vs, transposed):
    count = len(owns)
    rows, n = owns[0].shape
    steps = 1 if transposed or rows % 32 else 4
    rb = rows // steps

    def body(*refs):
        ins, outs = refs[:5 * count], refs[5 * count:]
        j = pl.program_id(0)
        for k in range(count):
            @pl.when(j == k)
            def _(k=k):
                own_ref, recv_ref, w_ref, m_ref, v_ref = ins[5 * k:5 * k + 5]
                g_ref, dl_ref, nm_ref, nv_ref = outs[4 * k:4 * k + 4]
                g = own_ref[...]
                for p in range(N_DEV - 1):
                    g = g + recv_ref[p].astype(F32)
                if transposed:
                    g = g.T
                g_ref[...] = g
                _adamw_store(g, w_ref[...], m_ref[...], v_ref[...], dl_ref, nm_ref, nv_ref)

    def turn(k):
        return lambda j, i: jnp.where(j == k, i, jnp.where(j < k, 0, steps - 1))

    in_specs, out_specs = [], []
    for k in range(count):
        at = turn(k)
        blk = pl.BlockSpec(ws[0].shape if transposed else (rb, n), lambda j, i, at=at: (at(j, i), 0))
        in_specs += [pl.BlockSpec((rb, n), lambda j, i, at=at: (at(j, i), 0)),
                     pl.BlockSpec((N_DEV - 1, rb, n), lambda j, i, at=at: (0, at(j, i), 0)), blk, blk, blk]
        out_specs += [blk] * 4
    res = pl.pallas_call(
        body, name="adamw_" + name, grid=(count, steps), in_specs=in_specs, out_specs=out_specs,
        out_shape=[SDS(ws[0].shape, F32)] * (4 * count), compiler_params=_params(("arbitrary", "arbitrary")),
    )(*[a for k in range(count) for a in (owns[k], recvs[k], ws[k], ms[k], vs[k])])
    return [list(res[j::4]) for j in range(4)]


def _row_spec(tm, width):
    return pl.BlockSpec((tm, width), lambda i: (i, 0))


def _full_spec(shape):
    return pl.BlockSpec(shape, lambda i: (0,) * len(shape))


_acc_spec = _full_spec


def _sub_rows(tm):
    step = min(SUB_ROWS, tm)
    return [(lo, lo + step) for lo in range(0, tm, step)]


def _in_proj(x2, wt_in):
    t, d = x2.shape
    u_w = wt_in.shape[0]
    tm = _row_tile(t, MATMUL_ROWS)

    def body(x_ref, w_ref, u_ref, xb_ref):
        xb = x_ref[...].astype(BF16)
        xb_ref[...] = xb
        u_ref[...] = _dot_nt(xb, w_ref[...]).astype(BF16)

    return pl.pallas_call(
        body, name="in_proj", grid=(t // tm,),
        in_specs=[_row_spec(tm, d), _full_spec(wt_in.shape)],
        out_specs=[_row_spec(tm, u_w), _row_spec(tm, d)],
        out_shape=[SDS((t, u_w), BF16), SDS((t, d), BF16)],
        compiler_params=_params(("parallel",)),
    )(x2, wt_in)


def _col_halves(f):
    n = f // LANES
    k = (n + 1) // 2 * LANES
    return [(0, k), (k, f)] if k < f else [(0, f)]


def _mix_ln1_ffn_up(r, a, x2, w_out, wt_gate, wt_up, g1, b1):
    t, d = x2.shape
    f = wt_gate.shape[0]
    tm = _row_tile(t, EPILOGUE_ROWS)

    def body(r_ref, a_ref, x_ref, wo_ref, wg_ref, wu_ref, g_ref, b_ref, z_ref, hb_ref, dg_ref, du_ref, act_ref):
        mix = _dot_nn(r_ref[...], wo_ref[0:RET_W, :]) + _dot_nn(a_ref[...], wo_ref[RET_W:RET_W + ATT_W, :])
        z = ALPHA * x_ref[...] + mix
        xhat, _ = _layer_norm_stats(z)
        z_ref[...] = z
        h = (xhat * g_ref[...] + b_ref[...]).astype(BF16)
        hb_ref[...] = h
        g = _dot_nt(h, wg_ref[...])
        u = _dot_nt(h, wu_ref[...])
        sg = _sigmoid(g)
        silu = g * sg
        dg_ref[...] = (u * (sg * (1.0 + g * (1.0 - sg)))).astype(BF16)
        du_ref[...] = silu.astype(BF16)
        act_ref[...] = (silu * u).astype(BF16)

    wide, narrow = _row_spec(tm, f), _row_spec(tm, d)
    return pl.pallas_call(
        body, name="mix_ln1_ffn_up", grid=(t // tm,),
        in_specs=[_row_spec(tm, RET_W), _row_spec(tm, ATT_W), narrow, _resident_spec(w_out.shape),
                  _resident_spec(wt_gate.shape), _resident_spec(wt_up.shape), _full_spec(g1.shape),
                  _full_spec(b1.shape)],
        out_specs=[narrow, narrow, wide, wide, wide],
        out_shape=[SDS((t, d), F32), SDS((t, d), BF16)] + [SDS((t, f), BF16)] * 3,
        compiler_params=_params(("parallel",)),
    )(r, a, x2, w_out, wt_gate, wt_up, g1, b1)


def _ffn_down_ln2_loss(act, dact_dg, dact_du, h1b, p2, z1, target, w_down, w_pg, wt_pe, g1, b1, g2, b2):
    t, d = z1.shape
    f = act.shape[1]
    pdim = p2.shape[1]
    tm = _row_tile(t, EPILOGUE_ROWS)

    def body(act_ref, fg_ref, fu_ref, hb_ref, p_ref, z1_ref, tgt_ref, wd_ref, wpg_ref, wpe_ref, g1_ref, b1_ref,
             g2_ref, b2_ref, dz_ref, dzb_ref, ds_ref, dple_ref, dg_ref, du_ref, acc_ref):
        @pl.when(pl.program_id(0) == 0)
        def _():
            acc_ref[...] = jnp.zeros_like(acc_ref)

        for lo, hi in _sub_rows(tm):
            xhat1, _ = _layer_norm_stats(z1_ref[lo:hi, :])
            h1 = xhat1 * g1_ref[...] + b1_ref[...]
            ffn = _dot_nn(act_ref[lo:hi, :], wd_ref[...])
            pg = _sigmoid(_dot_nn(hb_ref[lo:hi, :], wpg_ref[...]))
            ple = _dot_nt(p_ref[lo:hi, :].astype(BF16), wpe_ref[...])
            z2 = ALPHA * h1 + ffn + pg * ple
            xhat2, rstd2 = _layer_norm_stats(z2)
            err = xhat2 * g2_ref[...] + b2_ref[...] - tgt_ref[lo:hi, :]
            dy = err * (1.0 / d)
            dz = _layer_norm_bwd(dy * g2_ref[...], xhat2, rstd2)
            dzb = dz.astype(BF16)
            dz_ref[lo:hi, :] = dz
            dzb_ref[lo:hi, :] = dzb
            ds_ref[lo:hi, :] = (dz * ple * pg * (1.0 - pg)).astype(BF16)
            dple_ref[lo:hi, :] = (dz * pg).astype(BF16)
            acc_ref[0:1, :] += jnp.sum(err * err, axis=0, keepdims=True)
            acc_ref[1:2, :] += jnp.sum(dy * xhat2, axis=0, keepdims=True)
            acc_ref[2:3, :] += jnp.sum(dy, axis=0, keepdims=True)
            for c0, c1 in _col_halves(f):
                da = _dot_nt(dzb, wd_ref[c0:c1, :])
                dg_ref[lo:hi, c0:c1] = (da * fg_ref[lo:hi, c0:c1].astype(F32)).astype(BF16)
                du_ref[lo:hi, c0:c1] = (da * fu_ref[lo:hi, c0:c1].astype(F32)).astype(BF16)

    vec = _full_spec(g1.shape)
    wide, narrow = _row_spec(tm, f), _row_spec(tm, d)
    return pl.pallas_call(
        body, name="ffn_down_ln2_loss", grid=(t // tm,),
        in_specs=[wide, wide, wide, narrow, _row_spec(tm, pdim), narrow, narrow,
                  _full_spec(w_down.shape), _full_spec(w_pg.shape), _full_spec(wt_pe.shape), vec, vec, vec, vec],
        out_specs=[narrow] * 4 + [wide, wide, _acc_spec((8, d))],
        out_shape=[SDS((t, d), F32), SDS((t, d), BF16), SDS((t, d), BF16), SDS((t, d), BF16),
                   SDS((t, f), BF16), SDS((t, f), BF16), SDS((8, d), F32)],
        compiler_params=_params(("arbitrary",)),
    )(act, dact_dg, dact_du, h1b, p2, z1, target, w_down, w_pg, wt_pe, g1, b1, g2, b2)


def _after(after, body):
    k = len(after)
    return (lambda *refs: body(*refs[k:])), [ANY_SPEC] * k


def _resident_spec(shape):
    return pl.BlockSpec(shape, lambda i: (0,) * len(shape), pipeline_mode=pl.Buffered(1))


def _dh1_ln1_bwd(dz2, dg, dup, dsb, z1, wt_gate, wt_up, w_pg, w_out, g1, after=()):
    t, d = dz2.shape
    f = dg.shape[1]
    tm = _row_tile(t, EPILOGUE_ROWS)

    def body(dz_ref, dg_ref, du_ref, ds_ref, z1_ref, wg_ref, wu_ref, wpg_ref, wo_ref, g1_ref,
             dz1_ref, dz1b_ref, dr_ref, da_ref, acc_ref):
        @pl.when(pl.program_id(0) == 0)
        def _():
            acc_ref[...] = jnp.zeros_like(acc_ref)

        for lo, hi in _sub_rows(tm):
            dh = (ALPHA * dz_ref[lo:hi, :] + _dot_nn(dg_ref[lo:hi, :], wg_ref[...])
                  + _dot_nn(du_ref[lo:hi, :], wu_ref[...]) + _dot_nt(ds_ref[lo:hi, :], wpg_ref[...]))
            xhat, rstd = _layer_norm_stats(z1_ref[lo:hi, :])
            dz1 = _layer_norm_bwd(dh * g1_ref[...], xhat, rstd)
            dz1b = dz1.astype(BF16)
            dz1_ref[lo:hi, :] = dz1
            dz1b_ref[lo:hi, :] = dz1b
            acc_ref[0:1, :] += jnp.sum(dh * xhat, axis=0, keepdims=True)
            acc_ref[1:2, :] += jnp.sum(dh, axis=0, keepdims=True)
            dr_ref[lo:hi, :] = _dot_nt(dz1b, wo_ref[0:RET_W, :]).astype(BF16)
            da_ref[lo:hi, :] = _dot_nt(dz1b, wo_ref[RET_W:RET_W + ATT_W, :]).astype(BF16)

    body, lead = _after(after, body)
    return pl.pallas_call(
        body, name="dh1_ln1_bwd", grid=(t // tm,),
        in_specs=lead + [_row_spec(tm, d), _row_spec(tm, f), _row_spec(tm, f), _row_spec(tm, d), _row_spec(tm, d),
                         _resident_spec(wt_gate.shape), _resident_spec(wt_up.shape), _resident_spec(w_pg.shape),
                         _resident_spec(w_out.shape), _full_spec(g1.shape)],
        out_specs=[_row_spec(tm, d), _row_spec(tm, d), _row_spec(tm, RET_W), _row_spec(tm, ATT_W), _acc_spec((8, d))],
        out_shape=[SDS((t, d), F32), SDS((t, d), BF16), SDS((t, RET_W), BF16), SDS((t, ATT_W), BF16),
                   SDS((8, d), F32)],
        compiler_params=_params(("arbitrary",)),
    )(*after, dz2, dg, dup, dsb, z1, wt_gate, wt_up, w_pg, w_out, g1)


def _in_proj_bwd(dz1, parts, wt_in, after=()):
    t, d = dz1.shape
    tm = _row_tile(t, MATMUL_ROWS)
    widths = [p.shape[1] for p in parts]

    def body(*refs):
        dz_ref, part_refs, w_ref, dx_ref = refs[0], refs[1:1 + len(parts)], refs[-2], refs[-1]
        acc = ALPHA * dz_ref[...]
        lo = 0
        for p_ref, w in zip(part_refs, widths):
            acc = acc + _dot_nn(p_ref[...], w_ref[lo:lo + w, :])
            lo += w
        dx_ref[...] = acc

    body, lead = _after(after, body)
    return pl.pallas_call(
        body, name="in_proj_bwd", grid=(t // tm,),
        in_specs=lead + [_row_spec(tm, d)] + [_row_spec(tm, w) for w in widths] + [_full_spec(wt_in.shape)],
        out_specs=_row_spec(tm, d), out_shape=SDS((t, d), F32),
        compiler_params=_params(("parallel",)),
    )(*after, dz1, *parts, wt_in)


def _weight_grad(name, me, parts, rhs, after=()):
    t, n = rhs.shape
    widths = [p.shape[1] for p in parts]
    rows = sum(widths)
    own_rows = rows // N_DEV
    tk = _row_tile(t, MATMUL_ROWS)
    n_steps = t // tk
    step = 256

    def body(*refs):
        me_ref, part_refs, rhs_ref = refs[0], refs[1:1 + len(parts)], refs[1 + len(parts)]
        full_ref, own_ref, acc = refs[-3], refs[-2], refs[-1]
        i = pl.program_id(0)

        def products(first):
            b = rhs_ref[...].astype(BF16)
            lo = 0
            for p_ref, w in zip(part_refs, widths):
                for c0 in range(0, w, step):
                    c1 = min(c0 + step, w)
                    val = _dot_tn(p_ref[:, c0:c1].astype(BF16), b)
                    if first:
                        acc[lo + c0:lo + c1, :] = val
                    else:
                        acc[lo + c0:lo + c1, :] += val
                lo += w

        pl.when(i == 0)(functools.partial(products, True))
        pl.when(i > 0)(functools.partial(products, False))

        @pl.when(i == n_steps - 1)
        def _():
            full_ref[...] = acc[...].astype(BF16)
            own_ref[...] = acc[pl.ds(pl.multiple_of(me_ref[0] * own_rows, 8), own_rows), :]

    body, lead = _after(after, body)
    return pl.pallas_call(
        body, name=name, grid=(n_steps,),
        in_specs=lead + [_smem_spec()] + [_row_spec(tk, w) for w in widths] + [_row_spec(tk, n)],
        out_specs=[_full_spec((rows, n)), _full_spec((own_rows, n))],
        out_shape=[SDS((rows, n), BF16), SDS((own_rows, n), F32)],
        scratch_shapes=[pltpu.VMEM((rows, n), F32)],
        compiler_params=_params(("arbitrary",)),
    )(*after, me, *parts, rhs)


def _log_decay(decay_f, decay_b):
    def body(f_ref, b_ref, lf_ref, lb_ref):
        lf_ref[...] = jnp.log1p(-jnp.exp2(f_ref[...]))
        lb_ref[...] = jnp.log1p(-jnp.exp2(b_ref[...]))

    return pl.pallas_call(body, name="log_decay", out_shape=[SDS(decay_f.shape, F32)] * 2)(decay_f, decay_b)


def _chunk(ref, n):
    return ref[pl.ds(pl.multiple_of(n * CHUNK, CHUNK), CHUNK), :]


def _group_sum(is_a, v):
    sa = jnp.sum(jnp.where(is_a, v, 0.0), axis=1, keepdims=True)
    sb = jnp.sum(jnp.where(is_a, 0.0, v), axis=1, keepdims=True)
    return jnp.where(is_a, sa, sb)


def _seq_spec(s, col_block):
    return pl.BlockSpec((s, LANES), lambda b, h: (b, col_block + h))


def _smem_spec():
    return pl.BlockSpec(memory_space=pltpu.SMEM)


RET_UNROLL = 4


def _chunk_loop(n_chunks, body, init):
    u = RET_UNROLL if n_chunks % RET_UNROLL == 0 else 1

    def trip(i, carry):
        for j in range(u):
            carry = body(i * u + j, carry)
        return carry

    return lax.fori_loop(0, n_chunks // u, trip, init)


def _stacked_tables(lgf_ref, lgb_ref, pair):
    lane = lax.broadcasted_iota(jnp.int32, (1, LANES), 1)
    is_a = lane < HEAD_DIM
    lgf = jnp.where(is_a, lgf_ref[2 * pair], lgf_ref[2 * pair + 1])
    lgb = jnp.where(is_a, lgb_ref[2 * pair], lgb_ref[2 * pair + 1])
    row = lax.broadcasted_iota(jnp.int32, (CHUNK, 1), 0).astype(F32)
    kdec_f, qdec_f = jnp.exp(lgf * (CHUNK - 1.0 - row)), jnp.exp(lgf * (row + 1.0))
    kdec_b, qdec_b = jnp.exp(lgb * row), jnp.exp(lgb * (CHUNK - row))
    tab = dict(
        is_a=is_a, row=row, lam_f=jnp.exp(lgf * CHUNK), lam_b=jnp.exp(lgb * CHUNK),
        kdec=jnp.concatenate([kdec_f, kdec_b], axis=1), qdec=jnp.concatenate([qdec_f, qdec_b], axis=1),
        qexp=jnp.concatenate([jnp.broadcast_to(row + 1.0, (CHUNK, LANES)),
                              jnp.broadcast_to(CHUNK - row, (CHUNK, LANES))], axis=1),
        kexp=jnp.concatenate([jnp.broadcast_to(CHUNK - 1.0 - row, (CHUNK, LANES)),
                              jnp.broadcast_to(row, (CHUNK, LANES))], axis=1),
    )
    r = lax.broadcasted_iota(jnp.int32, (2 * LANES, LANES), 0)
    c = lax.broadcasted_iota(jnp.int32, (2 * LANES, LANES), 1)
    tab["diag2"] = ((r & (LANES - 1)) < HEAD_DIM) == (c < HEAD_DIM)
    i2 = lax.broadcasted_iota(jnp.int32, (2 * CHUNK, CHUNK), 0)
    j = lax.broadcasted_iota(jnp.int32, (2 * CHUNK, CHUNK), 1)
    head_b = i2 >= CHUNK
    diff = ((i2 & (CHUNK - 1)) - j).astype(F32)
    up, dn = jnp.maximum(diff, 0.0), jnp.maximum(-diff, 0.0)
    lgf2 = jnp.where(head_b, lgf_ref[2 * pair + 1], lgf_ref[2 * pair])
    lgb2 = jnp.where(head_b, lgb_ref[2 * pair + 1], lgb_ref[2 * pair])
    ef = jnp.where(diff >= 0, jnp.exp(lgf2 * up), 0.0)
    eb = jnp.where(diff <= 0, jnp.exp(lgb2 * dn), 0.0)
    tab["d2"] = ef + eb
    tab["df2"] = ef * up
    tab["db2"] = eb * dn
    return tab


def _stack_pair(is_a, x):
    zero = jnp.zeros_like(x)
    return jnp.concatenate([jnp.where(is_a, x, zero), jnp.where(is_a, zero, x)], axis=0)


def _unstack_pair(is_a, x2):
    return jnp.where(is_a, x2[0:CHUNK, :], x2[CHUNK:2 * CHUNK, :])


def _both_ways(x, dec):
    return (jnp.concatenate([x, x], axis=1) * dec).astype(BF16)


def _scan_states(n_chunks, st, up_rows, up_lam, down_rows, down_lam):
    zero = jnp.zeros((LANES, LANES), F32)

    def up(n, r):
        new = st[n, up_rows, :]
        st[n, up_rows, :] = r
        return r * up_lam + new

    def down(s, r):
        n = n_chunks - 1 - s
        new = st[n, down_rows, :]
        st[n, down_rows, :] = r
        return r * down_lam + new

    lax.fori_loop(0, n_chunks, up, zero)
    lax.fori_loop(0, n_chunks, down, zero)


FWD_ROWS, BWD_ROWS = pl.ds(0, LANES), pl.ds(LANES, LANES)


ST_GAIN, ST_XF, ST_XB, ST_IFA, ST_IFB, ST_IBA, ST_IBB, ST_LF, ST_LB = 0, 1, 2, 3, 4, 5, 6, 8, 9
ST_ROWS = 16


GW = GROUP * HEAD_DIM
KEYS = 3 * BLOCK


def _attn_tables(g, bias_ref):
    r = lax.broadcasted_iota(jnp.int32, (GROUP * BLOCK, KEYS), 0)
    kj = lax.broadcasted_iota(jnp.int32, (GROUP * BLOCK, KEYS), 1)
    qi = r & (BLOCK - 1)
    hh = lax.shift_right_logical(r, 7)
    dist = jnp.abs(kj - BLOCK - qi)
    slope = jnp.exp2(-(GROUP * g + hh + 1).astype(F32) * (8.0 / ATTN_HEADS))
    bias_ref[...] = jnp.where(dist <= BLOCK, -slope * dist.astype(F32), NEG_INF)


def _own_lanes(g):
    return lax.shift_right_logical(lax.broadcasted_iota(jnp.int32, (1, LANES), 1), 6) == g


def _mask_keys(x_ref, g, scale, pad_ref, s):
    pad_ref[0:BLOCK, :] = jnp.zeros((BLOCK, LANES), BF16)
    pad_ref[BLOCK + s:2 * BLOCK + s, :] = jnp.zeros((BLOCK, LANES), BF16)
    pad_ref[BLOCK:BLOCK + s, :] = jnp.where(_own_lanes(g), x_ref[...].astype(F32) * scale, 0.0).astype(BF16)


def _lane_block(x, j):
    return x[:, j * LANES:(j + 1) * LANES]


def _stack_heads(x, g):
    assert GROUP == 4 and GW == 2 * LANES
    x1 = pltpu.roll(x, HEAD_DIM, 1)
    keep = _own_lanes(g)
    zero = jnp.zeros((BLOCK, LANES), x.dtype)
    rows = []
    for h in range(GROUP):
        for_g0 = _lane_block(x, h // 2) if h % 2 == 0 else _lane_block(x1, ((h + 1) // 2) % 2)
        for_g1 = _lane_block(x, h // 2) if h % 2 == 1 else _lane_block(x1, h // 2)
        rows.append(jnp.where(keep, jnp.where(g == 0, for_g0, for_g1), zero))
    return jnp.concatenate(rows, axis=0)


def _unstack_heads(x4, g):
    p = [x4[h * BLOCK:(h + 1) * BLOCK, :] for h in range(GROUP)]
    cat = lambda a, b: jnp.concatenate([a, b], axis=1)
    in_place = jnp.where(g == 0, cat(p[0], p[2]), cat(p[1], p[3]))
    one_left = jnp.where(g == 0, cat(p[1], p[3]), cat(p[2], p[0]))
    return in_place + pltpu.roll(one_left, HEAD_DIM, 1)


def _sink_column(sink_ref, g):
    rh = lax.shift_right_logical(lax.broadcasted_iota(jnp.int32, (GROUP * BLOCK, 1), 0), 7)
    col = jnp.zeros((GROUP * BLOCK, 1), F32)
    for h in range(GROUP):
        col = jnp.where(rh == h, sink_ref[GROUP * g + h], col)
    return col


def _attn_probs(qm, k3, bias_ref, sink_col, n, s):
    logits = _dot_nt(qm, k3) + bias_ref[...]
    kpos = n * BLOCK - BLOCK + lax.broadcasted_iota(jnp.int32, (1, KEYS), 1)
    logits = jnp.where((kpos >= 0) & (kpos < s), logits, NEG_INF)
    m = jnp.maximum(jnp.max(logits, axis=1, keepdims=True), sink_col)
    e = jnp.exp(logits - m)
    e_sink = jnp.exp(sink_col - m)
    inv = 1.0 / (jnp.sum(e, axis=1, keepdims=True) + e_sink)
    return e * inv, e_sink * inv


PAIRS_PER_KV = (RET_HEADS // 2) // KV_HEADS
FWD_ORDER = "rrarra"
BWD_ORDER = "rararr"


def _trip_order(order, chunks, blocks):
    if order.count("r") == chunks and order.count("a") == blocks:
        return order
    return "r" * chunks + "a" * blocks


def _mixers_fwd(u, lgf, lgb, gn_gain, sink, b_loc, after=()):
    t = u.shape[0]
    s = t // b_loc
    n_chunks = s // CHUNK
    pairs = RET_HEADS // 2
    trips = n_chunks // RET_UNROLL
    blocks_half = (s // BLOCK) // PAIRS_PER_KV
    per_trip = blocks_half // trips
    assert n_chunks % RET_UNROLL == 0 and blocks_half % trips == 0 and PAIRS_PER_KV == 2

    def body(lgf_ref, lgb_ref, sink_ref, q_ref, k_ref, v_ref, g_ref, gain_ref, aq_ref, ak_ref, av_ref,
             r_ref, xhat_ref, rstd_ref, a_ref, st, kpad, vpad, bias):
        pair = pl.program_id(1)
        g, half = lax.shift_right_logical(pair, 1), pair & 1
        tab = _stacked_tables(lgf_ref, lgb_ref, pair)
        is_a = tab["is_a"]

        @pl.when(half == 0)
        def _():
            _attn_tables(g, bias)
            _mask_keys(ak_ref, g, Q_SCALE, kpad, s)
            _mask_keys(av_ref, g, 1.0, vpad, s)

        def kv_body(n, _):
            k8 = _chunk(k_ref, n).astype(F32) * Q_SCALE
            st[n] = jnp.where(tab["diag2"], _dot_tn(_both_ways(k8, tab["kdec"]), _chunk(v_ref, n)), 0.0)
            return 0

        _chunk_loop(n_chunks, kv_body, 0)
        _scan_states(n_chunks, st, FWD_ROWS, tab["lam_f"], BWD_ROWS, tab["lam_b"])
        sink_col = _sink_column(sink_ref, g)

        def retention_chunk(n):
            q = _chunk(q_ref, n)
            k8 = (_chunk(k_ref, n).astype(F32) * Q_SCALE).astype(BF16)
            v = _chunk(v_ref, n)
            p2 = (_dot_nt(_stack_pair(is_a, q), k8) * tab["d2"]).astype(BF16)
            y = _unstack_pair(is_a, _dot_nn(p2, v))
            y = y + _dot_nn(_both_ways(q.astype(F32), tab["qdec"]), st[n].astype(BF16))
            rows = pl.ds(pl.multiple_of(n * CHUNK, CHUNK), CHUNK)
            mu = _group_sum(is_a, y) * (1.0 / HEAD_DIM)
            dlt = y - mu
            var = _group_sum(is_a, dlt * dlt) * (1.0 / HEAD_DIM)
            rstd = lax.rsqrt(var + GN_EPS)
            xhat = dlt * rstd
            xhat_ref[rows, :] = xhat
            rstd_ref[rows, :] = rstd
            gate = _chunk(g_ref, n).astype(F32)
            r_ref[rows, :] = (xhat * gain_ref[...] * gate * _sigmoid(gate)).astype(BF16)

        def attention_block(blk):
            n = half * blocks_half + blk
            rows = pl.ds(pl.multiple_of(blk * BLOCK, BLOCK), BLOCK)
            keys = pl.ds(pl.multiple_of(n * BLOCK, BLOCK), KEYS)
            p, _ = _attn_probs(_stack_heads(aq_ref[rows, :], g), kpad[keys, :], bias, sink_col, n, s)
            a_ref[rows, :] = _unstack_heads(_dot_nn(p.astype(BF16), vpad[keys, :]), g).astype(BF16)

        def trip(i, _):
            chunk, blk = 0, 0
            for kind in _trip_order(FWD_ORDER, RET_UNROLL, per_trip):
                if kind == "r":
                    retention_chunk(i * RET_UNROLL + chunk)
                    chunk += 1
                else:
                    attention_block(i * per_trip + blk)
                    blk += 1
            return 0

        lax.fori_loop(0, trips, trip, 0)

    lane_blk = lambda c0: _seq_spec(s, c0 // LANES)
    half_rows = blocks_half * BLOCK
    aq_spec = pl.BlockSpec((half_rows, GW), lambda b, h: (b * PAIRS_PER_KV + (h & 1), C_AQ // GW + h // 2))
    a_spec = pl.BlockSpec((half_rows, GW), lambda b, h: (b * PAIRS_PER_KV + (h & 1), h // 2))
    kv_spec = lambda c0: pl.BlockSpec((s, LANES), lambda b, h: (b, c0 // LANES))
    pad = pltpu.VMEM((s + 2 * BLOCK, LANES), BF16)
    body, lead = _after(after, body)
    return pl.pallas_call(
        body, name="mixers_fwd", grid=(b_loc, pairs),
        in_specs=lead + [_smem_spec(), _smem_spec(), _smem_spec(), lane_blk(C_RQ), lane_blk(C_RK), lane_blk(C_RV),
                         lane_blk(C_RG), pl.BlockSpec((1, LANES), lambda b, h: (0, h)), aq_spec, kv_spec(C_AK),
                         kv_spec(C_AV)],
        out_specs=[_seq_spec(s, 0), _seq_spec(s, 0), _seq_spec(s, 0), a_spec],
        out_shape=[SDS((t, RET_W), BF16), SDS((t, RET_W), F32), SDS((t, RET_W), F32), SDS((t, ATT_W), BF16)],
        scratch_shapes=[pltpu.VMEM((n_chunks, 2 * LANES, LANES), F32), pad, pad,
                        pltpu.VMEM((GROUP * BLOCK, KEYS), F32)],
        compiler_params=_params(("arbitrary", "arbitrary")),
    )(*after, lgf, lgb, sink, u, u, u, u, gn_gain, u, u, u)


def _mixers_bwd(u, xhat, rstd, dr, da, lgf, lgb, gn_gain, sink, b_loc, after=()):
    t = u.shape[0]
    s = t // b_loc
    n_chunks = s // CHUNK
    pairs = RET_HEADS // 2
    trips = n_chunks // RET_UNROLL
    blocks_half = (s // BLOCK) // PAIRS_PER_KV
    per_trip = blocks_half // trips
    assert n_chunks % RET_UNROLL == 0 and blocks_half % trips == 0 and PAIRS_PER_KV == 2

    def body(lgf_ref, lgb_ref, sink_ref, q_ref, k_ref, v_ref, g_ref, xhat_ref, rstd_ref, dr_ref, gain_ref,
             aq_ref, ak_ref, av_ref, do_ref,
             dq_ref, dk_ref, dv_ref, dg_ref, st_ref, daq_ref, dak_ref, dav_ref, dsink_ref,
             st, gr, dy_s, kpad, vpad, bias, dk_acc, dv_acc):
        pair = pl.program_id(1)
        g, half = lax.shift_right_logical(pair, 1), pair & 1
        tab = _stacked_tables(lgf_ref, lgb_ref, pair)
        is_a = tab["is_a"]
        gain = gain_ref[...]

        @pl.when(half == 0)
        def _():
            _attn_tables(g, bias)
            _mask_keys(ak_ref, g, Q_SCALE, kpad, s)
            _mask_keys(av_ref, g, 1.0, vpad, s)
            dsink_ref[...] = jnp.zeros_like(dsink_ref)

        @pl.when(pair == 0)
        def _():
            dk_acc[...] = jnp.zeros_like(dk_acc)
            dv_acc[...] = jnp.zeros_like(dv_acc)

        def norm_body(n, dgain):
            rows = pl.ds(pl.multiple_of(n * CHUNK, CHUNK), CHUNK)
            xhat, rstd = xhat_ref[rows, :], rstd_ref[rows, :]
            gate = g_ref[rows, :].astype(F32)
            sg = _sigmoid(gate)
            silu = gate * sg
            d_out = dr_ref[rows, :].astype(F32)
            dg_ref[rows, :] = (d_out * xhat * gain * (sg * (1.0 + gate * (1.0 - sg)))).astype(BF16)
            dxh = d_out * gain * silu
            m1 = _group_sum(is_a, dxh) * (1.0 / HEAD_DIM)
            m2 = _group_sum(is_a, dxh * xhat) * (1.0 / HEAD_DIM)
            dy = (rstd * (dxh - m1 - xhat * m2)).astype(BF16)
            dy_s[rows, :] = dy
            k8 = k_ref[rows, :].astype(F32) * Q_SCALE
            st[n] = jnp.where(tab["diag2"], _dot_tn(_both_ways(k8, tab["kdec"]), v_ref[rows, :]), 0.0)
            qf = q_ref[rows, :].astype(F32)
            gr[n] = jnp.where(tab["diag2"], _dot_tn(_both_ways(qf, tab["qdec"]), dy), 0.0)
            return dgain + jnp.sum(d_out * xhat * silu, axis=0, keepdims=True)

        colsum = lambda x: jnp.sum(x, axis=0, keepdims=True)

        def grad_body(n, carry):
            xfb, ifa, ifb, iba, ibb, lf, lb = carry
            rows = pl.ds(pl.multiple_of(n * CHUNK, CHUNK), CHUNK)
            q = q_ref[rows, :]
            qf = q.astype(F32)
            k8f = k_ref[rows, :].astype(F32) * Q_SCALE
            k8 = k8f.astype(BF16)
            v = v_ref[rows, :]
            dy = dy_s[rows, :]
            q2, dy2 = _stack_pair(is_a, q), _stack_pair(is_a, dy)
            sc = _dot_nt(q2, k8)
            dp = _dot_nt(dy2, v)
            a2 = (sc * tab["d2"]).astype(BF16)
            ds2 = (dp * tab["d2"]).astype(BF16)
            dq = _unstack_pair(is_a, _dot_nn(ds2, k8))
            dk = _dot_tn(ds2, q2)
            dv = _dot_tn(a2, dy2)
            prod = sc * dp
            pf, pb = prod * tab["df2"], prod * tab["db2"]
            ifa, ifb = ifa + colsum(pf[0:CHUNK, :]), ifb + colsum(pf[CHUNK:2 * CHUNK, :])
            iba, ibb = iba + colsum(pb[0:CHUNK, :]), ibb + colsum(pb[CHUNK:2 * CHUNK, :])
            states, sgrads = st[n], gr[n]
            sb, gb = states.astype(BF16), sgrads.astype(BF16)
            dqc = _dot_nt(dy, sb) * tab["qdec"]
            dkc = _dot_nt(v, gb) * tab["kdec"]
            dv = dv + _dot_nn(_both_ways(k8f, tab["kdec"]), gb)
            dq_ref[rows, :] = (dq + dqc[:, 0:LANES] + dqc[:, LANES:2 * LANES]).astype(BF16)
            dk_ref[rows, :] = ((dk + dkc[:, 0:LANES] + dkc[:, LANES:2 * LANES]) * Q_SCALE).astype(BF16)
            dv_ref[rows, :] = dv.astype(BF16)
            q2w, k2w = jnp.concatenate([qf, qf], axis=1), jnp.concatenate([k8f, k8f], axis=1)
            xfb = xfb + colsum(tab["qexp"] * q2w * dqc + tab["kexp"] * k2w * dkc)
            prod_s = sgrads * states
            lf, lb = lf + colsum(prod_s[0:LANES, :]), lb + colsum(prod_s[LANES:2 * LANES, :])
            return xfb, ifa, ifb, iba, ibb, lf, lb

        sink_col = _sink_column(sink_ref, g)
        head_row = lax.broadcasted_iota(jnp.int32, dsink_ref.shape, 0)

        def attention_block(blk):
            n = half * blocks_half + blk
            rows = pl.ds(pl.multiple_of(blk * BLOCK, BLOCK), BLOCK)
            keys = pl.ds(pl.multiple_of(n * BLOCK, BLOCK), KEYS)
            qm = _stack_heads(aq_ref[rows, :], g)
            k3, v3 = kpad[keys, :], vpad[keys, :]
            p, p_sink = _attn_probs(qm, k3, bias, sink_col, n, s)
            dom = _stack_heads(do_ref[rows, :], g)
            dp = _dot_nt(dom, v3)
            delta = jnp.sum(p * dp, axis=1, keepdims=True)
            ds_mat = (p * (dp - delta)).astype(BF16)
            daq_ref[rows, :] = _unstack_heads(_dot_nn(ds_mat, k3), g).astype(BF16)
            dk_acc[keys, :] += _dot_tn(ds_mat, qm) * Q_SCALE
            dv_acc[keys, :] += _dot_tn(p.astype(BF16), dom)
            w = p_sink * delta
            upd = jnp.zeros(dsink_ref.shape, F32)
            for h in range(GROUP):
                upd = upd + jnp.where(head_row == h, -jnp.sum(w[h * BLOCK:(h + 1) * BLOCK, :]), 0.0)
            dsink_ref[...] += upd

        dgain = _chunk_loop(n_chunks, norm_body, jnp.zeros((1, LANES), F32))
        _scan_states(n_chunks, st, FWD_ROWS, tab["lam_f"], BWD_ROWS, tab["lam_b"])
        _scan_states(n_chunks, gr, BWD_ROWS, tab["lam_b"], FWD_ROWS, tab["lam_f"])

        def trip(i, carry):
            chunk, blk = 0, 0
            for kind in _trip_order(BWD_ORDER, RET_UNROLL, per_trip):
                if kind == "r":
                    carry = grad_body(i * RET_UNROLL + chunk, carry)
                    chunk += 1
                else:
                    attention_block(i * per_trip + blk)
                    blk += 1
            return carry

        z = jnp.zeros((1, LANES), F32)
        init = (jnp.zeros((1, 2 * LANES), F32), z, z, z, z, z, z)
        xfb, ifa, ifb, iba, ibb, lf, lb = lax.fori_loop(0, trips, trip, init)
        st_ref[...] = jnp.zeros_like(st_ref)
        st_ref[ST_GAIN:ST_GAIN + 1, :] = dgain
        st_ref[ST_XF:ST_XF + 1, :] = xfb[:, 0:LANES]
        st_ref[ST_XB:ST_XB + 1, :] = xfb[:, LANES:2 * LANES]
        st_ref[ST_IFA:ST_IFA + 1, :] = ifa
        st_ref[ST_IFB:ST_IFB + 1, :] = ifb
        st_ref[ST_IBA:ST_IBA + 1, :] = iba
        st_ref[ST_IBB:ST_IBB + 1, :] = ibb
        st_ref[ST_LF:ST_LF + 1, :] = lf * (CHUNK * tab["lam_f"])
        st_ref[ST_LB:ST_LB + 1, :] = lb * (CHUNK * tab["lam_b"])

        @pl.when(pair == pairs - 1)
        def _():
            dak_ref[...] = dk_acc[BLOCK:BLOCK + s, :].astype(BF16)
            dav_ref[...] = dv_acc[BLOCK:BLOCK + s, :].astype(BF16)

    lane_blk = lambda c0: _seq_spec(s, c0 // LANES)
    seq0 = _seq_spec(s, 0)
    half_rows = blocks_half * BLOCK
    aq_spec = pl.BlockSpec((half_rows, GW), lambda b, h: (b * PAIRS_PER_KV + (h & 1), C_AQ // GW + h // 2))
    a_spec = pl.BlockSpec((half_rows, GW), lambda b, h: (b * PAIRS_PER_KV + (h & 1), h // 2))
    kv_spec = lambda c0: pl.BlockSpec((s, LANES), lambda b, h: (b, c0 // LANES))
    kv_out = pl.BlockSpec((s, LANES), lambda b, h: (b, 0))
    state = pltpu.VMEM((n_chunks, 2 * LANES, LANES), F32)
    pad = pltpu.VMEM((s + 2 * BLOCK, LANES), BF16)
    acc = pltpu.VMEM((s + 2 * BLOCK, LANES), F32)
    body, lead = _after(after, body)
    return pl.pallas_call(
        body, name="mixers_bwd", grid=(b_loc, pairs),
        in_specs=lead + [_smem_spec(), _smem_spec(), _smem_spec(), lane_blk(C_RQ), lane_blk(C_RK), lane_blk(C_RV),
                         lane_blk(C_RG), seq0, seq0, seq0, pl.BlockSpec((1, LANES), lambda b, h: (0, h)),
                         aq_spec, kv_spec(C_AK), kv_spec(C_AV), a_spec],
        out_specs=[seq0] * 4 + [pl.BlockSpec((ST_ROWS, LANES), lambda b, h: (b, h)), a_spec, kv_out, kv_out,
                                pl.BlockSpec((8, LANES), lambda b, h: (b * KV_HEADS + h // 2, 0))],
        out_shape=[SDS((t, RET_W), BF16)] * 4 + [SDS((b_loc * ST_ROWS, RET_W), F32), SDS((t, ATT_W), BF16),
                                                   SDS((t, KV_W), BF16), SDS((t, KV_W), BF16),
                                                   SDS((b_loc * KV_HEADS * 8, LANES), F32)],
        scratch_shapes=[state, state, pltpu.VMEM((s, LANES), BF16), pad, pad,
                        pltpu.VMEM((GROUP * BLOCK, KEYS), F32), acc, acc],
        compiler_params=_params(("arbitrary", "arbitrary")),
    )(*after, lgf, lgb, sink, u, u, u, u, xhat, rstd, dr, gn_gain, u, u, u, da)


def _pack_small(acc2, acc1, ret_stats, dsink, b_loc, d):
    pairs = RET_HEADS // 2

    def body(acc2_ref, acc1_ref, st_ref, dsink_ref, out_ref):
        out_ref[...] = jnp.zeros_like(out_ref)
        out_ref[ROW_LN1G:ROW_LN1G + 1, :] = acc1_ref[0:1, :]
        out_ref[ROW_LN1B:ROW_LN1B + 1, :] = acc1_ref[1:2, :]
        out_ref[ROW_LN2G:ROW_LN2G + 1, :] = acc2_ref[1:2, :]
        out_ref[ROW_LN2B:ROW_LN2B + 1, :] = acc2_ref[2:3, :]
        out_ref[ROW_LOSS:ROW_LOSS + 1, :] = acc2_ref[0:1, :]
        st = st_ref[0:ST_ROWS, :]
        for b in range(1, b_loc):
            st = st + st_ref[b * ST_ROWS:(b + 1) * ST_ROWS, :]
        out_ref[ROW_GN:ROW_GN + 1, 0:RET_W] = st[ST_GAIN:ST_GAIN + 1, :]
        lane = lax.broadcasted_iota(jnp.int32, (1, d), 1)
        misc = jnp.zeros((1, d), F32)
        for pr in range(pairs):
            blk = st[:, pr * LANES:(pr + 1) * LANES]
            half = lax.broadcasted_iota(jnp.int32, (1, LANES), 1) < HEAD_DIM
            for h in range(2):
                sel = half if h == 0 else jnp.logical_not(half)
                cross_f = jnp.sum(jnp.where(sel, blk[ST_XF:ST_XF + 1, :] + blk[ST_LF:ST_LF + 1, :], 0.0))
                cross_b = jnp.sum(jnp.where(sel, blk[ST_XB:ST_XB + 1, :] + blk[ST_LB:ST_LB + 1, :], 0.0))
                intra_f = jnp.sum(blk[ST_IFA + h:ST_IFA + h + 1, :])
                intra_b = jnp.sum(blk[ST_IBA + h:ST_IBA + h + 1, :])
                head = 2 * pr + h
                misc = jnp.where(lane == MISC_DF + head, cross_f + intra_f, misc)
                misc = jnp.where(lane == MISC_DB + head, cross_b + intra_b, misc)
        for g in range(KV_HEADS):
            tot = dsink_ref[g * 8:(g + 1) * 8, :]
            for b in range(1, b_loc):
                tot = tot + dsink_ref[(b * KV_HEADS + g) * 8:(b * KV_HEADS + g + 1) * 8, :]
            for h in range(GROUP):
                misc = jnp.where(lane == MISC_SINK + GROUP * g + h, jnp.sum(tot[h:h + 1, 0:1]), misc)
        out_ref[ROW_MISC:ROW_MISC + 1, :] = misc

    return pl.pallas_call(body, name="pack_small", out_shape=SDS((SMALL_ROWS, d), F32))(acc2, acc1, ret_stats, dsink)


BIG = ("w_in", "w_out", "w_ffn_gate", "w_ffn_up", "w_ffn_down", "w_ple_proj", "w_ple_gate")
TRANSPOSED_OUTSIDE = ("w_in", "w_ffn_gate", "w_ffn_up")
TRANSPOSED_HERE = ("w_ple_proj",)
SMALL = ("ret_decay_fwd", "ret_decay_bwd", "ret_gn_gain", "attn_sink", "ln1_gain", "ln1_bias", "ln2_gain", "ln2_bias")
ORDER = ("w_in", "ret_decay_fwd", "ret_decay_bwd", "ret_gn_gain", "attn_sink", "w_out", "ln1_gain", "ln1_bias",
         "w_ffn_gate", "w_ffn_up", "w_ffn_down", "w_ple_proj", "w_ple_gate", "ln2_gain", "ln2_bias")


GATHER_ORDER = ("w_in", "w_ffn_up", "w_out", "w_ffn_gate", "w_ple_gate", "w_ple_proj", "w_ffn_down")
GATHER_TWO_LEVEL = ("w_in", "w_ffn_up")


def _local_step(x2, p2, target2, fetch, publish, small, b_loc, me):
    d = x2.shape[1]
    lgf, lgb = _log_decay(small["ret_decay_fwd"], small["ret_decay_bwd"])
    lgf1, lgb1, sink1 = lgf.reshape(-1), lgb.reshape(-1), small["attn_sink"].reshape(-1)
    (w_in,) = fetch(("w_in",), ())
    u, xb = _in_proj(x2, w_in)
    passed = fetch.pass_on("w_ffn_up", (xb,))
    r, ret_xhat, ret_rstd, a = _mixers_fwd(u, lgf1, lgb1, small["ret_gn_gain"], sink1, b_loc, passed)
    w_out, w_gate, w_up = fetch(("w_out", "w_ffn_gate", "w_ffn_up"), (r, a))
    z1, h1b, dact_dg, dact_du, act = _mix_ln1_ffn_up(
        r, a, x2, w_out, w_gate, w_up, small["ln1_gain"], small["ln1_bias"])
    w_pg, w_pe, w_down = fetch(("w_ple_gate", "w_ple_proj", "w_ffn_down"), (act,))
    dz2, dz2b, dsb, dpleb, dg, dup, acc2 = _ffn_down_ln2_loss(
        act, dact_dg, dact_du, h1b, p2, z1, target2, w_down, w_pg, w_pe,
        small["ln1_gain"], small["ln1_bias"], small["ln2_gain"], small["ln2_bias"])
    own = {}

    def grad(name, parts, rhs, after=()):
        whole, own[name] = _weight_grad("grad_" + name, me, parts, rhs, after)
        return whole

    t2 = publish("ffn", dict(w_ffn_down=grad("w_ffn_down", [act], dz2b),
                             w_ple_proj=grad("w_ple_proj", [dpleb], p2),
                             w_ple_gate=grad("w_ple_gate", [h1b], dsb),
                             w_ffn_gate=grad("w_ffn_gate", [dg], h1b),
                             w_ffn_up=grad("w_ffn_up", [dup], h1b)))
    dz1, dz1b, dr, da, acc1 = _dh1_ln1_bwd(dz2, dg, dup, dsb, z1, w_gate, w_up, w_pg, w_out, small["ln1_gain"], t2)
    t3 = publish("out", dict(w_out=grad("w_out", [r, a], dz1b)))
    dq, dk, dv, dgate, ret_stats, daq, dak, dav, dsink = _mixers_bwd(
        u, ret_xhat, ret_rstd, dr, da, lgf1, lgb1, small["ret_gn_gain"], sink1, b_loc, t3)
    parts = [dq, dk, dv, dgate, daq, dak, dav]
    small_part = _pack_small(acc2, acc1, ret_stats, dsink, b_loc, d)
    t4 = publish("in", dict(w_in=grad("w_in", parts, xb)), small_part)
    grad_x = _in_proj_bwd(dz1, parts, w_in, t4)
    return grad_x, own, small_part


def kernel(x, p, w_in, ret_decay_fwd, ret_decay_bwd, ret_gn_gain, attn_sink, w_out, ln1_gain, ln1_bias, w_ffn_gate, w_ffn_up, w_ffn_down, w_ple_proj, w_ple_gate, ln2_gain, ln2_bias, loss_target, m_w_in, m_ret_decay_fwd, m_ret_decay_bwd, m_ret_gn_gain, m_attn_sink, m_w_out, m_ln1_gain, m_ln1_bias, m_w_ffn_gate, m_w_ffn_up, m_w_ffn_down, m_w_ple_proj, m_w_ple_gate, m_ln2_gain, m_ln2_bias, v_w_in, v_ret_decay_fwd, v_ret_decay_bwd, v_ret_gn_gain, v_attn_sink, v_w_out, v_ln1_gain, v_ln1_bias, v_w_ffn_gate, v_w_ffn_up, v_w_ffn_down, v_w_ple_proj, v_w_ple_gate, v_ln2_gain, v_ln2_bias):
    given = dict(locals())

    def strip(n, a):
        if n not in BIG:
            return a
        return a[0].T if n in TRANSPOSED_OUTSIDE else a[0]

    def restore(n, a):
        if n not in BIG:
            return a
        return (a.T if n in TRANSPOSED_OUTSIDE else a)[None]

    w = {n: strip(n, given[n]) for n in ORDER}
    m = {n: strip(n, given["m_" + n]) for n in ORDER}
    v = {n: strip(n, given["v_" + n]) for n in ORDER}
    b_loc, s, d = x.shape
    x2 = x.reshape(b_loc * s, d)
    p2 = p[0].reshape(b_loc * s, p.shape[-1])
    target2 = loss_target.reshape(b_loc * s, d)

    small = {n: w[n] for n in SMALL}
    me = (4 * lax.axis_index("x") + 2 * lax.axis_index("y") + lax.axis_index("c")).astype(jnp.int32).reshape(1)

    gathered = _prep_shards(me, {n: w[n] for n in BIG})
    gather = _split_copy_start(
        "gather_start", [(gathered[n],) for n in GATHER_ORDER],
        [_gather_copy_near if n in GATHER_TWO_LEVEL else _gather_copy for n in GATHER_ORDER])

    passing = {}

    def pass_on(n, after):
        near = _split_copy_wait("gather_wait_" + n + "_near", gather, [GATHER_ORDER.index(n)], list(after))
        passing[n] = _split_copy_start("gather_pass_" + n, near, [_gather_copy_pass])
        return (passing[n]["token"],)

    def fetch(names, after):
        out = {}
        for n in [n for n in names if n in GATHER_TWO_LEVEL]:
            if n not in passing:
                pass_on(n, after)
            out[n] = _split_copy_wait("gather_wait_" + n, passing[n], [0], list(after))[0][0]
        direct = [n for n in names if n not in GATHER_TWO_LEVEL]
        if direct:
            got = _split_copy_wait("gather_wait_" + direct[0], gather, [GATHER_ORDER.index(n) for n in direct],
                                   list(after))
            out.update({n: item[0] for n, item in zip(direct, got)})
        return [out[n] for n in names]

    scatters = []

    def publish(tag, products, small_sums=None):
        items = [(products[n], lax.empty((N_DEV - 1, products[n].shape[0] // N_DEV, products[n].shape[1]), BF16))
                 for n in products]
        copies = [_scatter_copy] * len(items)
        if small_sums is not None:
            items.append((small_sums, lax.empty((N_DEV - 1,) + small_sums.shape, F32)))
            copies.append(_small_copy)
        started = _split_copy_start("scatter_start_" + tag, items, copies)
        scatters.append((list(products), small_sums is not None, started))
        return (started["token"],)

    fetch.pass_on = pass_on
    grad_x, own, small_part = _local_step(x2, p2, target2, fetch, publish, small, b_loc, me)

    out_g, out_d, out_m, out_v = {}, {}, {}, {}
    after = [grad_x]
    for names, with_small, started in scatters:
        landed = _split_copy_wait("scatter_wait_" + names[0], started, list(range(len(started["items"]))), after)
        if with_small:
            loss, sg, sd, sm, sv = _small_adamw(
                me, small_part, landed[-1][1], small, {n: m[n] for n in SMALL}, {n: v[n] for n in SMALL})
            for dst, src in ((out_g, sg), (out_d, sd), (out_m, sm), (out_v, sv)):
                dst.update(src)
        recv = {n: item[1] for n, item in zip(names, landed)}
        alike = {}
        for n in names:
            alike.setdefault((own[n].shape, n in TRANSPOSED_HERE), []).append(n)
        for (_, transposed), ns in alike.items():
            res = _reduce_adamw(ns[0], [own[n] for n in ns], [recv[n] for n in ns], [w[n] for n in ns],
                                [m[n] for n in ns], [v[n] for n in ns], transposed)
            for dst, vals in zip((out_g, out_d, out_m, out_v), res):
                dst.update(zip(ns, vals))
        after = [out_v[names[-1]]]

    outs = [loss[0, 0], grad_x.reshape(x.shape)]
    for group in (out_g, out_d, out_m, out_v):
        outs += [restore(n, group[n]) for n in ORDER]
    return tuple(outs)
```

```python
import functools

import jax
import jax.numpy as jnp
from jax import lax
from jax.experimental import pallas as pl
from jax.experimental.pallas import tpu as pltpu

F32, BF16 = jnp.float32, jnp.bfloat16
SDS = jax.ShapeDtypeStruct
MESH = pl.DeviceIdType.MESH

N_DEV = 8
HEAD_DIM = 64
RET_HEADS = 8
ATTN_HEADS = 8
KV_HEADS = 2
GROUP = ATTN_HEADS // KV_HEADS
RET_W = RET_HEADS * HEAD_DIM
ATT_W = ATTN_HEADS * HEAD_DIM
KV_W = KV_HEADS * HEAD_DIM
LANES = 128
CHUNK = 128
BLOCK = 128
Q_SCALE = HEAD_DIM ** -0.5
ALPHA = 2.0 ** 0.25
LN_EPS = 1e-5
GN_EPS = 1e-5
NEG_INF = -1e30
C_RQ, C_RK, C_RV, C_RG = 0, RET_W, 2 * RET_W, 3 * RET_W
C_AQ = 4 * RET_W
C_AK = C_AQ + ATT_W
C_AV = C_AK + KV_W
IN_W = C_AV + KV_W

ADAM_LR = 0.001
ADAM_B1 = 0.9
ADAM_B2 = 0.999
ADAM_EPS = 1e-08
ADAM_WD = 0.01
ADAM_STEP = 10

VMEM_LIMIT = 56 * 1024 * 1024
MATMUL_ROWS = 512
EPILOGUE_ROWS = 256
SUB_ROWS = 256
SMALL_ROWS = 16
ROW_LN1G, ROW_LN1B, ROW_LN2G, ROW_LN2B, ROW_LOSS, ROW_GN, ROW_MISC = 0, 1, 2, 3, 4, 5, 6
MISC_DF, MISC_DB, MISC_SINK = 0, 8, 16


def _dot_nn(a, b):
    return lax.dot_general(a, b, (((1,), (0,)), ((), ())), preferred_element_type=F32)


def _dot_nt(a, b):
    return lax.dot_general(a, b, (((1,), (1,)), ((), ())), preferred_element_type=F32)


def _dot_tn(a, b):
    return lax.dot_general(a, b, (((0,), (0,)), ((), ())), preferred_element_type=F32)


def _params(sem=None, vmem=VMEM_LIMIT):
    kw = {"vmem_limit_bytes": vmem}
    if sem is not None:
        kw["dimension_semantics"] = sem
    return pltpu.CompilerParams(**kw)


def _row_tile(t, want=512):
    tm = want
    while t % tm:
        tm //= 2
    return tm


def _sigmoid(x):
    return jax.nn.sigmoid(x)


def _layer_norm_stats(z):
    mu = jnp.mean(z, axis=1, keepdims=True)
    d = z - mu
    var = jnp.mean(d * d, axis=1, keepdims=True)
    rstd = lax.rsqrt(var + LN_EPS)
    return d * rstd, rstd


def _layer_norm_bwd(dxh, xhat, rstd):
    m1 = jnp.mean(dxh, axis=1, keepdims=True)
    m2 = jnp.mean(dxh * xhat, axis=1, keepdims=True)
    return rstd * (dxh - m1 - xhat * m2)


def _prep_shards(me, shards):
    names = list(shards)

    def body(me_ref, *refs):
        for name, src, dst in zip(names, refs[:len(names)], refs[len(names):]):
            val = src[...]
            dst[...] = (val.T if name in TRANSPOSED_HERE else val).astype(BF16)

    shape = lambda n, a: a.shape[::-1] if n in TRANSPOSED_HERE else a.shape
    shapes = [shape(n, shards[n]) for n in names]
    out = pl.pallas_call(
        body, name="prep_shards",
        grid_spec=pltpu.PrefetchScalarGridSpec(
            num_scalar_prefetch=1, grid=(1,),
            in_specs=[pl.BlockSpec(shards[n].shape, lambda i, me_ref: (0, 0)) for n in names],
            out_specs=[pl.BlockSpec(s, lambda i, me_ref: (me_ref[0], 0)) for s in shapes]),
        out_shape=[SDS((N_DEV * s[0], s[1]), BF16) for s in shapes], compiler_params=_params(("arbitrary",)),
    )(me, *[shards[n] for n in names])
    return dict(zip(names, out))


def _mesh_pos():
    return lax.axis_index("x"), lax.axis_index("y"), lax.axis_index("c")


HBM_SPEC = pl.BlockSpec(memory_space=pltpu.HBM)
SEM_SPEC = pl.BlockSpec(memory_space=pltpu.SEMAPHORE)
ANY_SPEC = pl.BlockSpec(memory_space=pl.ANY)
SIDE_EFFECT = pltpu.SideEffectType.DATAFLOW_SIDE_EFFECTING
PEER_SEMS = pltpu.SemaphoreType.DMA((N_DEV - 1,))


def _in_hbm(a):
    return pltpu.with_memory_space_constraint(a, pltpu.HBM)


def _split_copy_start(name, items, copies):
    n = len(items)
    flat = [a for it in items for a in it]
    k = len(flat)

    def body(*refs):
        arr, sems = list(refs[:k]), refs[k:k + 2 * n]
        for i, it in enumerate(items):
            mine = [arr.pop(0) for _ in it]
            for m in range(1, N_DEV):
                cp = copies[i](m, mine, sems[i].at[m - 1], sems[n + i].at[m - 1])
                if cp is not None:
                    cp.start()
        token = refs[-1]
        token[...] = jnp.zeros_like(token)

    res = pl.pallas_call(
        body, name=name,
        out_shape=[PEER_SEMS] * (2 * n) + [pltpu.HBM(a.shape, a.dtype) for a in flat] + [SDS((8, LANES), F32)],
        in_specs=[HBM_SPEC] * k,
        out_specs=[SEM_SPEC] * (2 * n) + [HBM_SPEC] * k + [pl.BlockSpec(memory_space=pltpu.VMEM)],
        input_output_aliases={j: 2 * n + j for j in range(k)},
        compiler_params=pltpu.CompilerParams(has_side_effects=SIDE_EFFECT),
    )(*[_in_hbm(a) for a in flat])
    thru, out_items = list(res[2 * n:2 * n + k]), []
    for it in items:
        out_items.append(tuple(thru.pop(0) for _ in it))
    return dict(send=res[:n], recv=res[n:2 * n], items=out_items, token=res[-1], copies=copies)


def _split_copy_wait(name, started, which, after):
    items = [started["items"][i] for i in which]
    copies = [started["copies"][i] for i in which]
    n = len(items)
    flat = [a for it in items for a in it]
    k = len(flat)

    def body(*refs):
        arr, sems = list(refs[:k]), refs[k:k + 2 * n]
        for i, it in enumerate(items):
            mine = [arr.pop(0) for _ in it]
            for m in range(1, N_DEV):
                cp = copies[i](m, mine, sems[i].at[m - 1], sems[n + i].at[m - 1])
                if cp is not None:
                    cp.wait_send()
                    cp.wait_recv()

    res = pl.pallas_call(
        body, name=name,
        out_shape=[pltpu.HBM(a.shape, a.dtype) for a in flat],
        in_specs=[HBM_SPEC] * k + [SEM_SPEC] * (2 * n) + [ANY_SPEC] * len(after),
        out_specs=[HBM_SPEC] * k,
        input_output_aliases={j: j for j in range(k)},
        compiler_params=pltpu.CompilerParams(has_side_effects=SIDE_EFFECT),
    )(*flat, *[started["send"][i] for i in which], *[started["recv"][i] for i in which], *[_in_hbm(a) for a in after])
    thru, out_items = list(res), []
    for it in items:
        out_items.append(tuple(thru.pop(0) for _ in it))
    return out_items


def _gather_copy(m, refs, send_sem, recv_sem):
    (land_ref,) = refs
    r = land_ref.shape[0] // N_DEV
    mine = land_ref.at[pl.ds(pl.multiple_of(_peer_index(0) * r, 8), r), :]
    return pltpu.make_async_remote_copy(src_ref=mine, dst_ref=mine, send_sem=send_sem, recv_sem=recv_sem,
                                        device_id=_peer(m), device_id_type=MESH)


def _gather_copy_near(m, refs, send_sem, recv_sem):
    return _gather_copy(m, refs, send_sem, recv_sem) if m == 1 or m % 2 == 0 else None


def _gather_copy_pass(m, refs, send_sem, recv_sem):
    if m == 1 or m % 2 == 0:
        return None
    (land_ref,) = refs
    r = land_ref.shape[0] // N_DEV
    block = land_ref.at[pl.ds(pl.multiple_of(_peer_index(m ^ 1) * r, 8), r), :]
    return pltpu.make_async_remote_copy(src_ref=block, dst_ref=block, send_sem=send_sem, recv_sem=recv_sem,
                                        device_id=_peer(1), device_id_type=MESH)


def _small_copy(m, refs, send_sem, recv_sem):
    part_ref, land_ref = refs
    return pltpu.make_async_remote_copy(src_ref=part_ref, dst_ref=land_ref.at[m - 1], send_sem=send_sem,
                                        recv_sem=recv_sem, device_id=_peer(m), device_id_type=MESH)


def _scatter_copy(m, refs, send_sem, recv_sem):
    buf_ref, land_ref = refs
    r = buf_ref.shape[0] // N_DEV
    src = buf_ref.at[pl.ds(pl.multiple_of(_peer_index(m) * r, 8), r), :]
    return pltpu.make_async_remote_copy(src_ref=src, dst_ref=land_ref.at[m - 1], send_sem=send_sem,
                                        recv_sem=recv_sem, device_id=_peer(m), device_id_type=MESH)


def _peer(m):
    x, y, c = _mesh_pos()
    bx, by, bc = (m >> 2) & 1, (m >> 1) & 1, m & 1
    return (x ^ bx if bx else x, y ^ by if by else y, c ^ bc if bc else c)


def _peer_index(m):
    x, y, c = _mesh_pos()
    return (4 * x + 2 * y + c) ^ m


SMALL_PLACE = {
    "ln1_gain": (ROW_LN1G, 0), "ln1_bias": (ROW_LN1B, 0), "ln2_gain": (ROW_LN2G, 0), "ln2_bias": (ROW_LN2B, 0),
    "ret_gn_gain": (ROW_GN, 0), "ret_decay_fwd": (ROW_MISC, MISC_DF), "ret_decay_bwd": (ROW_MISC, MISC_DB),
    "attn_sink": (ROW_MISC, MISC_SINK)}


def _small_adamw(me, part, landed, w, m, v):
    d = part.shape[1]
    names = list(SMALL_PLACE)
    k = len(names)

    def body(*refs):
        me_ref, part_ref, land_ref = refs[:3]
        refs = refs[2:]
        w_refs, m_refs, v_refs = refs[1:1 + k], refs[1 + k:1 + 2 * k], refs[1 + 2 * k:1 + 3 * k]
        outs = refs[1 + 3 * k:1 + 7 * k + 1]
        tot_ref = refs[-1]
        loss_ref, g_refs, dl_refs = outs[0], outs[1:1 + k], outs[1 + k:1 + 2 * k]
        nm_refs, nv_refs = outs[1 + 2 * k:1 + 3 * k], outs[1 + 3 * k:1 + 4 * k]
        tot = jnp.zeros(part_ref.shape, F32)
        for dev in range(N_DEV):
            j = dev ^ me_ref[0]
            tot = tot + jnp.where(j == 0, part_ref[...], land_ref[jnp.maximum(j, 1) - 1])
        tot_ref[...] = tot
        loss_ref[...] = (0.5 / d) * jnp.sum(tot_ref[ROW_LOSS:ROW_LOSS + 1, :], axis=1, keepdims=True)
        for i, name in enumerate(names):
            row, lo = SMALL_PLACE[name]
            wv = w_refs[i][...]
            g = tot_ref[row:row + 1, lo:lo + wv.shape[1]]
            if name.startswith("ret_decay"):
                p2 = jnp.exp2(wv)
                g = g * (-p2 * jnp.log(2.0) / (1.0 - p2))
            g_refs[i][...] = g
            _adamw_store(g, wv, m_refs[i][...], v_refs[i][...], dl_refs[i], nm_refs[i], nv_refs[i])

    shapes = [SDS(w[n].shape, F32) for n in names]
    vm = pl.BlockSpec(memory_space=pltpu.VMEM)
    res = pl.pallas_call(
        body, name="small_adamw", out_shape=[SDS((1, 1), F32)] + shapes * 4,
        in_specs=[_smem_spec()] + [vm] * (2 + 3 * k), out_specs=[vm] * (1 + 4 * k),
        scratch_shapes=[pltpu.VMEM(part.shape, F32)],
    )(me, part, landed, *[w[n] for n in names], *[m[n] for n in names], *[v[n] for n in names])
    groups = [dict(zip(names, res[1 + j * k:1 + (j + 1) * k])) for j in range(4)]
    return (res[0], *groups)


def _adamw_store(g, w, m, v, dl_ref, nm_ref, nv_ref):
    m = ADAM_B1 * m + (1.0 - ADAM_B1) * g
    v = ADAM_B2 * v + (1.0 - ADAM_B2) * (g * g)
    m_hat = m / (1.0 - ADAM_B1 ** ADAM_STEP)
    v_hat = v / (1.0 - ADAM_B2 ** ADAM_STEP)
    dl_ref[...] = -ADAM_LR * (m_hat / (jnp.sqrt(v_hat) + ADAM_EPS) + ADAM_WD * w)
    nm_ref[...] = m
    nv_ref[...] = v


def _reduce_adamw(name, owns, recvs, ws, ms, vs, transposed):
    count = len(owns)
    rows, n = owns[0].shape
    steps = 1 if transposed or rows % 32 else 4
    rb = rows // steps

    def body(*refs):
        ins, outs = refs[:5 * count], refs[5 * count:]
        j = pl.program_id(0)
        for k in range(count):
            @pl.when(j == k)
            def _(k=k):
                own_ref, recv_ref, w_ref, m_ref, v_ref = ins[5 * k:5 * k + 5]
                g_ref, dl_ref, nm_ref, nv_ref = outs[4 * k:4 * k + 4]
                g = own_ref[...]
                for p in range(N_DEV - 1):
                    g = g + recv_ref[p].astype(F32)
                if transposed:
                    g = g.T
                g_ref[...] = g
                _adamw_store(g, w_ref[...], m_ref[...], v_ref[...], dl_ref, nm_ref, nv_ref)

    def turn(k):
        return lambda j, i: jnp.where(j == k, i, jnp.where(j < k, 0, steps - 1))

    in_specs, out_specs = [], []
    for k in range(count):
        at = turn(k)
        blk = pl.BlockSpec(ws[0].shape if transposed else (rb, n), lambda j, i, at=at: (at(j, i), 0))
        in_specs += [pl.BlockSpec((rb, n), lambda j, i, at=at: (at(j, i), 0)),
                     pl.BlockSpec((N_DEV - 1, rb, n), lambda j, i, at=at: (0, at(j, i), 0)), blk, blk, blk]
        out_specs += [blk] * 4
    res = pl.pallas_call(
        body, name="adamw_" + name, grid=(count, steps), in_specs=in_specs, out_specs=out_specs,
        out_shape=[SDS(ws[0].shape, F32)] * (4 * count), compiler_params=_params(("arbitrary", "arbitrary")),
    )(*[a for k in range(count) for a in (owns[k], recvs[k], ws[k], ms[k], vs[k])])
    return [list(res[j::4]) for j in range(4)]


def _row_spec(tm, width):
    return pl.BlockSpec((tm, width), lambda i: (i, 0))


def _full_spec(shape):
    return pl.BlockSpec(shape, lambda i: (0,) * len(shape))


_acc_spec = _full_spec


def _sub_rows(tm):
    step = min(SUB_ROWS, tm)
    return [(lo, lo + step) for lo in range(0, tm, step)]


def _in_proj(x2, wt_in):
    t, d = x2.shape
    u_w = wt_in.shape[0]
    tm = _row_tile(t, MATMUL_ROWS)

    def body(x_ref, w_ref, u_ref, xb_ref):
        xb = x_ref[...].astype(BF16)
        xb_ref[...] = xb
        u_ref[...] = _dot_nt(xb, w_ref[...]).astype(BF16)

    return pl.pallas_call(
        body, name="in_proj", grid=(t // tm,),
        in_specs=[_row_spec(tm, d), _full_spec(wt_in.shape)],
        out_specs=[_row_spec(tm, u_w), _row_spec(tm, d)],
        out_shape=[SDS((t, u_w), BF16), SDS((t, d), BF16)],
        compiler_params=_params(("parallel",)),
    )(x2, wt_in)


def _col_halves(f):
    n = f // LANES
    k = (n + 1) // 2 * LANES
    return [(0, k), (k, f)] if k < f else [(0, f)]


def _mix_ln1_ffn_up(r, a, x2, w_out, wt_gate, wt_up, g1, b1):
    t, d = x2.shape
    f = wt_gate.shape[0]
    tm = _row_tile(t, EPILOGUE_ROWS)

    def body(r_ref, a_ref, x_ref, wo_ref, wg_ref, wu_ref, g_ref, b_ref, z_ref, hb_ref, dg_ref, du_ref, act_ref):
        mix = _dot_nn(r_ref[...], wo_ref[0:RET_W, :]) + _dot_nn(a_ref[...], wo_ref[RET_W:RET_W + ATT_W, :])
        z = ALPHA * x_ref[...] + mix
        xhat, _ = _layer_norm_stats(z)
        z_ref[...] = z
        h = (xhat * g_ref[...] + b_ref[...]).astype(BF16)
        hb_ref[...] = h
        g = _dot_nt(h, wg_ref[...])
        u = _dot_nt(h, wu_ref[...])
        sg = _sigmoid(g)
        silu = g * sg
        dg_ref[...] = (u * (sg * (1.0 + g * (1.0 - sg)))).astype(BF16)
        du_ref[...] = silu.astype(BF16)
        act_ref[...] = (silu * u).astype(BF16)

    wide, narrow = _row_spec(tm, f), _row_spec(tm, d)
    return pl.pallas_call(
        body, name="mix_ln1_ffn_up", grid=(t // tm,),
        in_specs=[_row_spec(tm, RET_W), _row_spec(tm, ATT_W), narrow, _resident_spec(w_out.shape),
                  _resident_spec(wt_gate.shape), _resident_spec(wt_up.shape), _full_spec(g1.shape),
                  _full_spec(b1.shape)],
        out_specs=[narrow, narrow, wide, wide, wide],
        out_shape=[SDS((t, d), F32), SDS((t, d), BF16)] + [SDS((t, f), BF16)] * 3,
        compiler_params=_params(("parallel",)),
    )(r, a, x2, w_out, wt_gate, wt_up, g1, b1)


def _ffn_down_ln2_loss(act, dact_dg, dact_du, h1b, p2, z1, target, w_down, w_pg, wt_pe, g1, b1, g2, b2):
    t, d = z1.shape
    f = act.shape[1]
    pdim = p2.shape[1]
    tm = _row_tile(t, EPILOGUE_ROWS)

    def body(act_ref, fg_ref, fu_ref, hb_ref, p_ref, z1_ref, tgt_ref, wd_ref, wpg_ref, wpe_ref, g1_ref, b1_ref,
             g2_ref, b2_ref, dz_ref, dzb_ref, ds_ref, dple_ref, dg_ref, du_ref, acc_ref):
        @pl.when(pl.program_id(0) == 0)
        def _():
            acc_ref[...] = jnp.zeros_like(acc_ref)

        for lo, hi in _sub_rows(tm):
            xhat1, _ = _layer_norm_stats(z1_ref[lo:hi, :])
            h1 = xhat1 * g1_ref[...] + b1_ref[...]
            pg = _sigmoid(_dot_nn(hb_ref[lo:hi, :], wpg_ref[...]))
            ple = _dot_nt(p_ref[lo:hi, :].astype(BF16), wpe_ref[...])
            gated = pg * ple
            dgate = gated * (1.0 - pg)
            ffn = _dot_nn(act_ref[lo:hi, :], wd_ref[...])
            z2 = ALPHA * h1 + gated + ffn
            xhat2, rstd2 = _layer_norm_stats(z2)
            err = xhat2 * g2_ref[...] + b2_ref[...] - tgt_ref[lo:hi, :]
            dy = err * (1.0 / d)
            dz = _layer_norm_bwd(dy * g2_ref[...], xhat2, rstd2)
            dzb = dz.astype(BF16)
            dz_ref[lo:hi, :] = dz
            dzb_ref[lo:hi, :] = dzb
            ds_ref[lo:hi, :] = (dz * dgate).astype(BF16)
            dple_ref[lo:hi, :] = (dz * pg).astype(BF16)
            acc_ref[0:1, :] += jnp.sum(err * err, axis=0, keepdims=True)
            acc_ref[1:2, :] += jnp.sum(dy * xhat2, axis=0, keepdims=True)
            acc_ref[2:3, :] += jnp.sum(dy, axis=0, keepdims=True)
            for c0, c1 in _col_halves(f):
                da = _dot_nt(dzb, wd_ref[c0:c1, :])
                dg_ref[lo:hi, c0:c1] = (da * fg_ref[lo:hi, c0:c1].astype(F32)).astype(BF16)
                du_ref[lo:hi, c0:c1] = (da * fu_ref[lo:hi, c0:c1].astype(F32)).astype(BF16)

    vec = _full_spec(g1.shape)
    wide, narrow = _row_spec(tm, f), _row_spec(tm, d)
    return pl.pallas_call(
        body, name="ffn_down_ln2_loss", grid=(t // tm,),
        in_specs=[wide, wide, wide, narrow, _row_spec(tm, pdim), narrow, narrow,
                  _full_spec(w_down.shape), _full_spec(w_pg.shape), _full_spec(wt_pe.shape), vec, vec, vec, vec],
        out_specs=[narrow] * 4 + [wide, wide, _acc_spec((8, d))],
        out_shape=[SDS((t, d), F32), SDS((t, d), BF16), SDS((t, d), BF16), SDS((t, d), BF16),
                   SDS((t, f), BF16), SDS((t, f), BF16), SDS((8, d), F32)],
        compiler_params=_params(("arbitrary",)),
    )(act, dact_dg, dact_du, h1b, p2, z1, target, w_down, w_pg, wt_pe, g1, b1, g2, b2)


def _after(after, body):
    k = len(after)
    return (lambda *refs: body(*refs[k:])), [ANY_SPEC] * k


def _resident_spec(shape):
    return pl.BlockSpec(shape, lambda i: (0,) * len(shape), pipeline_mode=pl.Buffered(1))


def _dh1_ln1_bwd(dz2, dg, dup, dsb, z1, wt_gate, wt_up, w_pg, w_out, g1, after=()):
    t, d = dz2.shape
    f = dg.shape[1]
    tm = _row_tile(t, EPILOGUE_ROWS)

    def body(dz_ref, dg_ref, du_ref, ds_ref, z1_ref, wg_ref, wu_ref, wpg_ref, wo_ref, g1_ref,
             dz1_ref, dz1b_ref, dr_ref, da_ref, acc_ref):
        @pl.when(pl.program_id(0) == 0)
        def _():
            acc_ref[...] = jnp.zeros_like(acc_ref)

        for lo, hi in _sub_rows(tm):
            dh = (ALPHA * dz_ref[lo:hi, :] + _dot_nn(dg_ref[lo:hi, :], wg_ref[...])
                  + _dot_nn(du_ref[lo:hi, :], wu_ref[...]) + _dot_nt(ds_ref[lo:hi, :], wpg_ref[...]))
            xhat, rstd = _layer_norm_stats(z1_ref[lo:hi, :])
            dz1 = _layer_norm_bwd(dh * g1_ref[...], xhat, rstd)
            dz1b = dz1.astype(BF16)
            dz1_ref[lo:hi, :] = dz1
            dz1b_ref[lo:hi, :] = dz1b
            acc_ref[0:1, :] += jnp.sum(dh * xhat, axis=0, keepdims=True)
            acc_ref[1:2, :] += jnp.sum(dh, axis=0, keepdims=True)
            dr_ref[lo:hi, :] = _dot_nt(dz1b, wo_ref[0:RET_W, :]).astype(BF16)
            da_ref[lo:hi, :] = _dot_nt(dz1b, wo_ref[RET_W:RET_W + ATT_W, :]).astype(BF16)

    body, lead = _after(after, body)
    return pl.pallas_call(
        body, name="dh1_ln1_bwd", grid=(t // tm,),
        in_specs=lead + [_row_spec(tm, d), _row_spec(tm, f), _row_spec(tm, f), _row_spec(tm, d), _row_spec(tm, d),
                         _resident_spec(wt_gate.shape), _resident_spec(wt_up.shape), _resident_spec(w_pg.shape),
                         _resident_spec(w_out.shape), _full_spec(g1.shape)],
        out_specs=[_row_spec(tm, d), _row_spec(tm, d), _row_spec(tm, RET_W), _row_spec(tm, ATT_W), _acc_spec((8, d))],
        out_shape=[SDS((t, d), F32), SDS((t, d), BF16), SDS((t, RET_W), BF16), SDS((t, ATT_W), BF16),
                   SDS((8, d), F32)],
        compiler_params=_params(("arbitrary",)),
    )(*after, dz2, dg, dup, dsb, z1, wt_gate, wt_up, w_pg, w_out, g1)


def _in_proj_bwd(dz1, parts, wt_in, after=()):
    t, d = dz1.shape
    tm = _row_tile(t, MATMUL_ROWS)
    widths = [p.shape[1] for p in parts]

    def body(*refs):
        dz_ref, part_refs, w_ref, dx_ref = refs[0], refs[1:1 + len(parts)], refs[-2], refs[-1]
        acc = ALPHA * dz_ref[...]
        lo = 0
        for p_ref, w in zip(part_refs, widths):
            acc = acc + _dot_nn(p_ref[...], w_ref[lo:lo + w, :])
            lo += w
        dx_ref[...] = acc

    body, lead = _after(after, body)
    return pl.pallas_call(
        body, name="in_proj_bwd", grid=(t // tm,),
        in_specs=lead + [_row_spec(tm, d)] + [_row_spec(tm, w) for w in widths] + [_full_spec(wt_in.shape)],
        out_specs=_row_spec(tm, d), out_shape=SDS((t, d), F32),
        compiler_params=_params(("parallel",)),
    )(*after, dz1, *parts, wt_in)


def _weight_grad(name, me, parts, rhs, after=()):
    t, n = rhs.shape
    widths = [p.shape[1] for p in parts]
    rows = sum(widths)
    own_rows = rows // N_DEV
    tk = _row_tile(t, MATMUL_ROWS)
    n_steps = t // tk
    step = 256

    def body(*refs):
        me_ref, part_refs, rhs_ref = refs[0], refs[1:1 + len(parts)], refs[1 + len(parts)]
        full_ref, own_ref, acc = refs[-3], refs[-2], refs[-1]
        i = pl.program_id(0)

        def products(first):
            b = rhs_ref[...].astype(BF16)
            lo = 0
            for p_ref, w in zip(part_refs, widths):
                for c0 in range(0, w, step):
                    c1 = min(c0 + step, w)
                    val = _dot_tn(p_ref[:, c0:c1].astype(BF16), b)
                    if first:
                        acc[lo + c0:lo + c1, :] = val
                    else:
                        acc[lo + c0:lo + c1, :] += val
                lo += w

        pl.when(i == 0)(functools.partial(products, True))
        pl.when(i > 0)(functools.partial(products, False))

        @pl.when(i == n_steps - 1)
        def _():
            full_ref[...] = acc[...].astype(BF16)
            own_ref[...] = acc[pl.ds(pl.multiple_of(me_ref[0] * own_rows, 8), own_rows), :]

    body, lead = _after(after, body)
    return pl.pallas_call(
        body, name=name, grid=(n_steps,),
        in_specs=lead + [_smem_spec()] + [_row_spec(tk, w) for w in widths] + [_row_spec(tk, n)],
        out_specs=[_full_spec((rows, n)), _full_spec((own_rows, n))],
        out_shape=[SDS((rows, n), BF16), SDS((own_rows, n), F32)],
        scratch_shapes=[pltpu.VMEM((rows, n), F32)],
        compiler_params=_params(("arbitrary",)),
    )(*after, me, *parts, rhs)


def _weight_grad_jobs(name, me, jobs, after=()):
    count = len(jobs)
    t = jobs[0][0].shape[0]
    tk = _row_tile(t, MATMUL_ROWS)
    n_steps = t // tk
    shapes = [(lhs.shape[1], rhs.shape[1]) for lhs, rhs in jobs]
    most_rows, most_cols = max(r for r, _ in shapes), max(n for _, n in shapes)
    step = 256

    def body(*refs):
        me_ref, lhs_refs, rhs_refs = refs[0], refs[1:1 + count], refs[1 + count:1 + 2 * count]
        full_refs, own_refs = refs[1 + 2 * count:1 + 3 * count], refs[1 + 3 * count:1 + 4 * count]
        acc, whole, mine, sems = refs[1 + 4 * count:]
        job, i = pl.program_id(0), pl.program_id(1)

        def leaving(j):
            rows, n = shapes[j]
            return (pltpu.make_async_copy(whole.at[0:rows, 0:n], full_refs[j], sems.at[0]),
                    pltpu.make_async_copy(mine.at[0:rows // N_DEV, 0:n], own_refs[j], sems.at[1]))

        def products(j, first):
            rows, n = shapes[j]
            b = rhs_refs[j][...].astype(BF16)
            for c0 in range(0, rows, step):
                c1 = min(c0 + step, rows)
                val = _dot_tn(lhs_refs[j][:, c0:c1].astype(BF16), b)
                if first:
                    acc[c0:c1, 0:n] = val
                else:
                    acc[c0:c1, 0:n] += val

        def finish(j):
            rows, n = shapes[j]
            own_rows = rows // N_DEV
            if j > 0:
                for cp in leaving(j - 1):
                    cp.wait()
            whole[0:rows, 0:n] = acc[0:rows, 0:n].astype(BF16)
            mine[0:own_rows, 0:n] = acc[pl.ds(pl.multiple_of(me_ref[0] * own_rows, 8), own_rows), 0:n]
            for cp in leaving(j):
                cp.start()
            if j == count - 1:
                for cp in leaving(j):
                    cp.wait()

        for j in range(count):
            pl.when((job == j) & (i == 0))(functools.partial(products, j, True))
            pl.when((job == j) & (i > 0))(functools.partial(products, j, False))
            pl.when((job == j) & (i == n_steps - 1))(functools.partial(finish, j))

    def turn(j):
        return lambda job, i: (jnp.where(job == j, i, jnp.where(job < j, 0, n_steps - 1)), 0)

    body, lead = _after(after, body)
    res = pl.pallas_call(
        body, name=name, grid=(count, n_steps),
        in_specs=lead + [_smem_spec()] + [pl.BlockSpec((tk, rows), turn(j)) for j, (rows, _) in enumerate(shapes)]
        + [pl.BlockSpec((tk, n), turn(j)) for j, (_, n) in enumerate(shapes)],
        out_specs=[ANY_SPEC] * (2 * count),
        out_shape=[SDS((rows, n), BF16) for rows, n in shapes] + [SDS((rows // N_DEV, n), F32) for rows, n in shapes],
        scratch_shapes=[pltpu.VMEM((most_rows, most_cols), F32), pltpu.VMEM((most_rows, most_cols), BF16),
                        pltpu.VMEM((most_rows // N_DEV, most_cols), F32), pltpu.SemaphoreType.DMA((2,))],
        compiler_params=_params(("arbitrary", "arbitrary")),
    )(*after, me, *[lhs for lhs, _ in jobs], *[rhs for _, rhs in jobs])
    return list(res[:count]), list(res[count:])


def _log_decay(decay_f, decay_b):
    def body(f_ref, b_ref, lf_ref, lb_ref):
        lf_ref[...] = jnp.log1p(-jnp.exp2(f_ref[...]))
        lb_ref[...] = jnp.log1p(-jnp.exp2(b_ref[...]))

    return pl.pallas_call(body, name="log_decay", out_shape=[SDS(decay_f.shape, F32)] * 2)(decay_f, decay_b)


def _chunk(ref, n):
    return ref[pl.ds(pl.multiple_of(n * CHUNK, CHUNK), CHUNK), :]


def _group_sum(is_a, v):
    sa = jnp.sum(jnp.where(is_a, v, 0.0), axis=1, keepdims=True)
    sb = jnp.sum(jnp.where(is_a, 0.0, v), axis=1, keepdims=True)
    return jnp.where(is_a, sa, sb)


def _seq_spec(s, col_block):
    return pl.BlockSpec((s, LANES), lambda b, h: (b, col_block + h))


def _smem_spec():
    return pl.BlockSpec(memory_space=pltpu.SMEM)


RET_UNROLL = 4


def _chunk_loop(n_chunks, body, init):
    u = RET_UNROLL if n_chunks % RET_UNROLL == 0 else 1

    def trip(i, carry):
        for j in range(u):
            carry = body(i * u + j, carry)
        return carry

    return lax.fori_loop(0, n_chunks // u, trip, init)


def _stacked_tables(lgf_ref, lgb_ref, pair):
    lane = lax.broadcasted_iota(jnp.int32, (1, LANES), 1)
    is_a = lane < HEAD_DIM
    lgf = jnp.where(is_a, lgf_ref[2 * pair], lgf_ref[2 * pair + 1])
    lgb = jnp.where(is_a, lgb_ref[2 * pair], lgb_ref[2 * pair + 1])
    row = lax.broadcasted_iota(jnp.int32, (CHUNK, 1), 0).astype(F32)
    kdec_f, qdec_f = jnp.exp(lgf * (CHUNK - 1.0 - row)), jnp.exp(lgf * (row + 1.0))
    kdec_b, qdec_b = jnp.exp(lgb * row), jnp.exp(lgb * (CHUNK - row))
    tab = dict(
        is_a=is_a, row=row, lam_f=jnp.exp(lgf * CHUNK), lam_b=jnp.exp(lgb * CHUNK),
        kdec=jnp.concatenate([kdec_f, kdec_b], axis=1), qdec=jnp.concatenate([qdec_f, qdec_b], axis=1),
        qexp=jnp.concatenate([jnp.broadcast_to(row + 1.0, (CHUNK, LANES)),
                              jnp.broadcast_to(CHUNK - row, (CHUNK, LANES))], axis=1),
        kexp=jnp.concatenate([jnp.broadcast_to(CHUNK - 1.0 - row, (CHUNK, LANES)),
                              jnp.broadcast_to(row, (CHUNK, LANES))], axis=1),
    )
    r = lax.broadcasted_iota(jnp.int32, (2 * LANES, LANES), 0)
    c = lax.broadcasted_iota(jnp.int32, (2 * LANES, LANES), 1)
    tab["diag2"] = ((r & (LANES - 1)) < HEAD_DIM) == (c < HEAD_DIM)
    i2 = lax.broadcasted_iota(jnp.int32, (2 * CHUNK, CHUNK), 0)
    j = lax.broadcasted_iota(jnp.int32, (2 * CHUNK, CHUNK), 1)
    head_b = i2 >= CHUNK
    diff = ((i2 & (CHUNK - 1)) - j).astype(F32)
    up, dn = jnp.maximum(diff, 0.0), jnp.maximum(-diff, 0.0)
    lgf2 = jnp.where(head_b, lgf_ref[2 * pair + 1], lgf_ref[2 * pair])
    lgb2 = jnp.where(head_b, lgb_ref[2 * pair + 1], lgb_ref[2 * pair])
    ef = jnp.where(diff >= 0, jnp.exp(lgf2 * up), 0.0)
    eb = jnp.where(diff <= 0, jnp.exp(lgb2 * dn), 0.0)
    tab["d2"] = ef + eb
    tab["df2"] = ef * up
    tab["db2"] = eb * dn
    return tab


def _stack_pair(is_a, x):
    zero = jnp.zeros_like(x)
    return jnp.concatenate([jnp.where(is_a, x, zero), jnp.where(is_a, zero, x)], axis=0)


def _unstack_pair(is_a, x2):
    return jnp.where(is_a, x2[0:CHUNK, :], x2[CHUNK:2 * CHUNK, :])


def _both_ways(x, dec):
    return (jnp.concatenate([x, x], axis=1) * dec).astype(BF16)


def _scan_states(n_chunks, st, up_rows, up_lam, down_rows, down_lam):
    zero = jnp.zeros((LANES, LANES), F32)

    def up(n, r):
        new = st[n, up_rows, :]
        st[n, up_rows, :] = r
        return r * up_lam + new

    def down(s, r):
        n = n_chunks - 1 - s
        new = st[n, down_rows, :]
        st[n, down_rows, :] = r
        return r * down_lam + new

    lax.fori_loop(0, n_chunks, up, zero)
    lax.fori_loop(0, n_chunks, down, zero)


FWD_ROWS, BWD_ROWS = pl.ds(0, LANES), pl.ds(LANES, LANES)


ST_GAIN, ST_XF, ST_XB, ST_IFA, ST_IFB, ST_IBA, ST_IBB, ST_LF, ST_LB = 0, 1, 2, 3, 4, 5, 6, 8, 9
ST_ROWS = 16


GW = GROUP * HEAD_DIM
KEYS = 3 * BLOCK


def _attn_tables(g, bias_ref):
    r = lax.broadcasted_iota(jnp.int32, (GROUP * BLOCK, KEYS), 0)
    kj = lax.broadcasted_iota(jnp.int32, (GROUP * BLOCK, KEYS), 1)
    qi = r & (BLOCK - 1)
    hh = lax.shift_right_logical(r, 7)
    dist = jnp.abs(kj - BLOCK - qi)
    slope = jnp.exp2(-(GROUP * g + hh + 1).astype(F32) * (8.0 / ATTN_HEADS))
    bias_ref[...] = jnp.where(dist <= BLOCK, -slope * dist.astype(F32), NEG_INF)


def _own_lanes(g):
    return lax.shift_right_logical(lax.broadcasted_iota(jnp.int32, (1, LANES), 1), 6) == g


def _mask_keys(x_ref, g, scale, pad_ref, s):
    pad_ref[0:BLOCK, :] = jnp.zeros((BLOCK, LANES), BF16)
    pad_ref[BLOCK + s:2 * BLOCK + s, :] = jnp.zeros((BLOCK, LANES), BF16)
    pad_ref[BLOCK:BLOCK + s, :] = jnp.where(_own_lanes(g), x_ref[...].astype(F32) * scale, 0.0).astype(BF16)


def _lane_block(x, j):
    return x[:, j * LANES:(j + 1) * LANES]


def _stack_heads(x, g):
    assert GROUP == 4 and GW == 2 * LANES
    x1 = pltpu.roll(x, HEAD_DIM, 1)
    keep = _own_lanes(g)
    zero = jnp.zeros((BLOCK, LANES), x.dtype)
    rows = []
    for h in range(GROUP):
        for_g0 = _lane_block(x, h // 2) if h % 2 == 0 else _lane_block(x1, ((h + 1) // 2) % 2)
        for_g1 = _lane_block(x, h // 2) if h % 2 == 1 else _lane_block(x1, h // 2)
        rows.append(jnp.where(keep, jnp.where(g == 0, for_g0, for_g1), zero))
    return jnp.concatenate(rows, axis=0)


def _unstack_heads(x4, g):
    p = [x4[h * BLOCK:(h + 1) * BLOCK, :] for h in range(GROUP)]
    cat = lambda a, b: jnp.concatenate([a, b], axis=1)
    in_place = jnp.where(g == 0, cat(p[0], p[2]), cat(p[1], p[3]))
    one_left = jnp.where(g == 0, cat(p[1], p[3]), cat(p[2], p[0]))
    return in_place + pltpu.roll(one_left, HEAD_DIM, 1)


def _sink_column(sink_ref, g):
    rh = lax.shift_right_logical(lax.broadcasted_iota(jnp.int32, (GROUP * BLOCK, 1), 0), 7)
    col = jnp.zeros((GROUP * BLOCK, 1), F32)
    for h in range(GROUP):
        col = jnp.where(rh == h, sink_ref[GROUP * g + h], col)
    return col


def _attn_probs(qm, k3, bias_ref, sink_col, n, s):
    logits = _dot_nt(qm, k3) + bias_ref[...]
    kpos = n * BLOCK - BLOCK + lax.broadcasted_iota(jnp.int32, (1, KEYS), 1)
    logits = jnp.where((kpos >= 0) & (kpos < s), logits, NEG_INF)
    m = jnp.maximum(jnp.max(logits, axis=1, keepdims=True), sink_col)
    e = jnp.exp(logits - m)
    e_sink = jnp.exp(sink_col - m)
    inv = 1.0 / (jnp.sum(e, axis=1, keepdims=True) + e_sink)
    return e * inv, e_sink * inv


PAIRS_PER_KV = (RET_HEADS // 2) // KV_HEADS
FWD_ORDER = "rrarra"
BWD_ORDER = "rararr"


def _trip_order(order, chunks, blocks):
    if order.count("r") == chunks and order.count("a") == blocks:
        return order
    return "r" * chunks + "a" * blocks


def _mixers_fwd(u, lgf, lgb, gn_gain, sink, b_loc, after=()):
    t = u.shape[0]
    s = t // b_loc
    n_chunks = s // CHUNK
    pairs = RET_HEADS // 2
    trips = n_chunks // RET_UNROLL
    blocks_half = (s // BLOCK) // PAIRS_PER_KV
    per_trip = blocks_half // trips
    assert n_chunks % RET_UNROLL == 0 and blocks_half % trips == 0 and PAIRS_PER_KV == 2

    def body(lgf_ref, lgb_ref, sink_ref, q_ref, k_ref, v_ref, g_ref, gain_ref, aq_ref, ak_ref, av_ref,
             r_ref, xhat_ref, rstd_ref, a_ref, st, kpad, vpad, bias):
        pair = pl.program_id(1)
        g, half = lax.shift_right_logical(pair, 1), pair & 1
        tab = _stacked_tables(lgf_ref, lgb_ref, pair)
        is_a = tab["is_a"]

        @pl.when(half == 0)
        def _():
            _attn_tables(g, bias)
            _mask_keys(ak_ref, g, Q_SCALE, kpad, s)
            _mask_keys(av_ref, g, 1.0, vpad, s)

        def kv_body(n, _):
            k8 = _chunk(k_ref, n).astype(F32) * Q_SCALE
            st[n] = jnp.where(tab["diag2"], _dot_tn(_both_ways(k8, tab["kdec"]), _chunk(v_ref, n)), 0.0)
            return 0

        _chunk_loop(n_chunks, kv_body, 0)
        _scan_states(n_chunks, st, FWD_ROWS, tab["lam_f"], BWD_ROWS, tab["lam_b"])
        sink_col = _sink_column(sink_ref, g)

        def retention_chunk(n):
            q = _chunk(q_ref, n)
            k8 = (_chunk(k_ref, n).astype(F32) * Q_SCALE).astype(BF16)
            v = _chunk(v_ref, n)
            p2 = (_dot_nt(_stack_pair(is_a, q), k8) * tab["d2"]).astype(BF16)
            y = _unstack_pair(is_a, _dot_nn(p2, v))
            y = y + _dot_nn(_both_ways(q.astype(F32), tab["qdec"]), st[n].astype(BF16))
            rows = pl.ds(pl.multiple_of(n * CHUNK, CHUNK), CHUNK)
            mu = _group_sum(is_a, y) * (1.0 / HEAD_DIM)
            dlt = y - mu
            var = _group_sum(is_a, dlt * dlt) * (1.0 / HEAD_DIM)
            rstd = lax.rsqrt(var + GN_EPS)
            xhat = dlt * rstd
            xhat_ref[rows, :] = xhat
            rstd_ref[rows, :] = rstd
            gate = _chunk(g_ref, n).astype(F32)
            r_ref[rows, :] = (xhat * gain_ref[...] * gate * _sigmoid(gate)).astype(BF16)

        def attention_block(blk):
            n = half * blocks_half + blk
            rows = pl.ds(pl.multiple_of(blk * BLOCK, BLOCK), BLOCK)
            keys = pl.ds(pl.multiple_of(n * BLOCK, BLOCK), KEYS)
            p, _ = _attn_probs(_stack_heads(aq_ref[rows, :], g), kpad[keys, :], bias, sink_col, n, s)
            a_ref[rows, :] = _unstack_heads(_dot_nn(p.astype(BF16), vpad[keys, :]), g).astype(BF16)

        def trip(i, _):
            chunk, blk = 0, 0
            for kind in _trip_order(FWD_ORDER, RET_UNROLL, per_trip):
                if kind == "r":
                    retention_chunk(i * RET_UNROLL + chunk)
                    chunk += 1
                else:
                    attention_block(i * per_trip + blk)
                    blk += 1
            return 0

        lax.fori_loop(0, trips, trip, 0)

    lane_blk = lambda c0: _seq_spec(s, c0 // LANES)
    half_rows = blocks_half * BLOCK
    aq_spec = pl.BlockSpec((half_rows, GW), lambda b, h: (b * PAIRS_PER_KV + (h & 1), C_AQ // GW + h // 2))
    a_spec = pl.BlockSpec((half_rows, GW), lambda b, h: (b * PAIRS_PER_KV + (h & 1), h // 2))
    kv_spec = lambda c0: pl.BlockSpec((s, LANES), lambda b, h: (b, c0 // LANES))
    pad = pltpu.VMEM((s + 2 * BLOCK, LANES), BF16)
    body, lead = _after(after, body)
    return pl.pallas_call(
        body, name="mixers_fwd", grid=(b_loc, pairs),
        in_specs=lead + [_smem_spec(), _smem_spec(), _smem_spec(), lane_blk(C_RQ), lane_blk(C_RK), lane_blk(C_RV),
                         lane_blk(C_RG), pl.BlockSpec((1, LANES), lambda b, h: (0, h)), aq_spec, kv_spec(C_AK),
                         kv_spec(C_AV)],
        out_specs=[_seq_spec(s, 0), _seq_spec(s, 0), _seq_spec(s, 0), a_spec],
        out_shape=[SDS((t, RET_W), BF16), SDS((t, RET_W), F32), SDS((t, RET_W), F32), SDS((t, ATT_W), BF16)],
        scratch_shapes=[pltpu.VMEM((n_chunks, 2 * LANES, LANES), F32), pad, pad,
                        pltpu.VMEM((GROUP * BLOCK, KEYS), F32)],
        compiler_params=_params(("arbitrary", "arbitrary")),
    )(*after, lgf, lgb, sink, u, u, u, u, gn_gain, u, u, u)


def _mixers_bwd(u, xhat, rstd, dr, da, lgf, lgb, gn_gain, sink, b_loc, after=()):
    t = u.shape[0]
    s = t // b_loc
    n_chunks = s // CHUNK
    pairs = RET_HEADS // 2
    trips = n_chunks // RET_UNROLL
    blocks_half = (s // BLOCK) // PAIRS_PER_KV
    per_trip = blocks_half // trips
    assert n_chunks % RET_UNROLL == 0 and blocks_half % trips == 0 and PAIRS_PER_KV == 2

    def body(lgf_ref, lgb_ref, sink_ref, q_ref, k_ref, v_ref, g_ref, xhat_ref, rstd_ref, dr_ref, gain_ref,
             aq_ref, ak_ref, av_ref, do_ref,
             dq_ref, dk_ref, dv_ref, dg_ref, st_ref, daq_ref, dak_ref, dav_ref, dsink_ref,
             st, gr, dy_s, kpad, vpad, bias, dk_acc, dv_acc):
        pair = pl.program_id(1)
        g, half = lax.shift_right_logical(pair, 1), pair & 1
        tab = _stacked_tables(lgf_ref, lgb_ref, pair)
        is_a = tab["is_a"]
        gain = gain_ref[...]

        @pl.when(half == 0)
        def _():
            _attn_tables(g, bias)
            _mask_keys(ak_ref, g, Q_SCALE, kpad, s)
            _mask_keys(av_ref, g, 1.0, vpad, s)
            dsink_ref[...] = jnp.zeros_like(dsink_ref)

        @pl.when(pair == 0)
        def _():
            dk_acc[...] = jnp.zeros_like(dk_acc)
            dv_acc[...] = jnp.zeros_like(dv_acc)

        def norm_body(n, dgain):
            rows = pl.ds(pl.multiple_of(n * CHUNK, CHUNK), CHUNK)
            xhat, rstd = xhat_ref[rows, :], rstd_ref[rows, :]
            gate = g_ref[rows, :].astype(F32)
            sg = _sigmoid(gate)
            silu = gate * sg
            d_out = dr_ref[rows, :].astype(F32)
            dg_ref[rows, :] = (d_out * xhat * gain * (sg * (1.0 + gate * (1.0 - sg)))).astype(BF16)
            dxh = d_out * gain * silu
            m1 = _group_sum(is_a, dxh) * (1.0 / HEAD_DIM)
            m2 = _group_sum(is_a, dxh * xhat) * (1.0 / HEAD_DIM)
            dy = (rstd * (dxh - m1 - xhat * m2)).astype(BF16)
            dy_s[rows, :] = dy
            k8 = k_ref[rows, :].astype(F32) * Q_SCALE
            st[n] = jnp.where(tab["diag2"], _dot_tn(_both_ways(k8, tab["kdec"]), v_ref[rows, :]), 0.0)
            qf = q_ref[rows, :].astype(F32)
            gr[n] = jnp.where(tab["diag2"], _dot_tn(_both_ways(qf, tab["qdec"]), dy), 0.0)
            return dgain + jnp.sum(d_out * xhat * silu, axis=0, keepdims=True)

        colsum = lambda x: jnp.sum(x, axis=0, keepdims=True)

        def grad_body(n, carry):
            xfb, ifa, ifb, iba, ibb, lf, lb = carry
            rows = pl.ds(pl.multiple_of(n * CHUNK, CHUNK), CHUNK)
            q = q_ref[rows, :]
            qf = q.astype(F32)
            k8f = k_ref[rows, :].astype(F32) * Q_SCALE
            k8 = k8f.astype(BF16)
            v = v_ref[rows, :]
            dy = dy_s[rows, :]
            q2, dy2 = _stack_pair(is_a, q), _stack_pair(is_a, dy)
            sc = _dot_nt(q2, k8)
            dp = _dot_nt(dy2, v)
            a2 = (sc * tab["d2"]).astype(BF16)
            ds2 = (dp * tab["d2"]).astype(BF16)
            dq = _unstack_pair(is_a, _dot_nn(ds2, k8))
            dk = _dot_tn(ds2, q2)
            dv = _dot_tn(a2, dy2)
            prod = sc * dp
            pf, pb = prod * tab["df2"], prod * tab["db2"]
            ifa, ifb = ifa + colsum(pf[0:CHUNK, :]), ifb + colsum(pf[CHUNK:2 * CHUNK, :])
            iba, ibb = iba + colsum(pb[0:CHUNK, :]), ibb + colsum(pb[CHUNK:2 * CHUNK, :])
            states, sgrads = st[n], gr[n]
            sb, gb = states.astype(BF16), sgrads.astype(BF16)
            dqc = _dot_nt(dy, sb) * tab["qdec"]
            dkc = _dot_nt(v, gb) * tab["kdec"]
            dv = dv + _dot_nn(_both_ways(k8f, tab["kdec"]), gb)
            dq_ref[rows, :] = (dq + dqc[:, 0:LANES] + dqc[:, LANES:2 * LANES]).astype(BF16)
            dk_ref[rows, :] = ((dk + dkc[:, 0:LANES] + dkc[:, LANES:2 * LANES]) * Q_SCALE).astype(BF16)
            dv_ref[rows, :] = dv.astype(BF16)
            q2w, k2w = jnp.concatenate([qf, qf], axis=1), jnp.concatenate([k8f, k8f], axis=1)
            xfb = xfb + colsum(tab["qexp"] * q2w * dqc + tab["kexp"] * k2w * dkc)
            prod_s = sgrads * states
            lf, lb = lf + colsum(prod_s[0:LANES, :]), lb + colsum(prod_s[LANES:2 * LANES, :])
            return xfb, ifa, ifb, iba, ibb, lf, lb

        sink_col = _sink_column(sink_ref, g)
        head_row = lax.broadcasted_iota(jnp.int32, dsink_ref.shape, 0)

        def attention_block(blk):
            n = half * blocks_half + blk
            rows = pl.ds(pl.multiple_of(blk * BLOCK, BLOCK), BLOCK)
            keys = pl.ds(pl.multiple_of(n * BLOCK, BLOCK), KEYS)
            qm = _stack_heads(aq_ref[rows, :], g)
            k3, v3 = kpad[keys, :], vpad[keys, :]
            p, p_sink = _attn_probs(qm, k3, bias, sink_col, n, s)
            dom = _stack_heads(do_ref[rows, :], g)
            dp = _dot_nt(dom, v3)
            delta = jnp.sum(p * dp, axis=1, keepdims=True)
            ds_mat = (p * (dp - delta)).astype(BF16)
            daq_ref[rows, :] = _unstack_heads(_dot_nn(ds_mat, k3), g).astype(BF16)
            dk_acc[keys, :] += _dot_tn(ds_mat, qm) * Q_SCALE
            dv_acc[keys, :] += _dot_tn(p.astype(BF16), dom)
            w = p_sink * delta
            upd = jnp.zeros(dsink_ref.shape, F32)
            for h in range(GROUP):
                upd = upd + jnp.where(head_row == h, -jnp.sum(w[h * BLOCK:(h + 1) * BLOCK, :]), 0.0)
            dsink_ref[...] += upd

        dgain = _chunk_loop(n_chunks, norm_body, jnp.zeros((1, LANES), F32))
        _scan_states(n_chunks, st, FWD_ROWS, tab["lam_f"], BWD_ROWS, tab["lam_b"])
        _scan_states(n_chunks, gr, BWD_ROWS, tab["lam_b"], FWD_ROWS, tab["lam_f"])

        def trip(i, carry):
            chunk, blk = 0, 0
            for kind in _trip_order(BWD_ORDER, RET_UNROLL, per_trip):
                if kind == "r":
                    carry = grad_body(i * RET_UNROLL + chunk, carry)
                    chunk += 1
                else:
                    attention_block(i * per_trip + blk)
                    blk += 1
            return carry

        z = jnp.zeros((1, LANES), F32)
        init = (jnp.zeros((1, 2 * LANES), F32), z, z, z, z, z, z)
        xfb, ifa, ifb, iba, ibb, lf, lb = lax.fori_loop(0, trips, trip, init)
        st_ref[...] = jnp.zeros_like(st_ref)
        st_ref[ST_GAIN:ST_GAIN + 1, :] = dgain
        st_ref[ST_XF:ST_XF + 1, :] = xfb[:, 0:LANES]
        st_ref[ST_XB:ST_XB + 1, :] = xfb[:, LANES:2 * LANES]
        st_ref[ST_IFA:ST_IFA + 1, :] = ifa
        st_ref[ST_IFB:ST_IFB + 1, :] = ifb
        st_ref[ST_IBA:ST_IBA + 1, :] = iba
        st_ref[ST_IBB:ST_IBB + 1, :] = ibb
        st_ref[ST_LF:ST_LF + 1, :] = lf * (CHUNK * tab["lam_f"])
        st_ref[ST_LB:ST_LB + 1, :] = lb * (CHUNK * tab["lam_b"])

        @pl.when(pair == pairs - 1)
        def _():
            dak_ref[...] = dk_acc[BLOCK:BLOCK + s, :].astype(BF16)
            dav_ref[...] = dv_acc[BLOCK:BLOCK + s, :].astype(BF16)

    lane_blk = lambda c0: _seq_spec(s, c0 // LANES)
    seq0 = _seq_spec(s, 0)
    half_rows = blocks_half * BLOCK
    aq_spec = pl.BlockSpec((half_rows, GW), lambda b, h: (b * PAIRS_PER_KV + (h & 1), C_AQ // GW + h // 2))
    a_spec = pl.BlockSpec((half_rows, GW), lambda b, h: (b * PAIRS_PER_KV + (h & 1), h // 2))
    kv_spec = lambda c0: pl.BlockSpec((s, LANES), lambda b, h: (b, c0 // LANES))
    kv_out = pl.BlockSpec((s, LANES), lambda b, h: (b, 0))
    state = pltpu.VMEM((n_chunks, 2 * LANES, LANES), F32)
    pad = pltpu.VMEM((s + 2 * BLOCK, LANES), BF16)
    acc = pltpu.VMEM((s + 2 * BLOCK, LANES), F32)
    body, lead = _after(after, body)
    return pl.pallas_call(
        body, name="mixers_bwd", grid=(b_loc, pairs),
        in_specs=lead + [_smem_spec(), _smem_spec(), _smem_spec(), lane_blk(C_RQ), lane_blk(C_RK), lane_blk(C_RV),
                         lane_blk(C_RG), seq0, seq0, seq0, pl.BlockSpec((1, LANES), lambda b, h: (0, h)),
                         aq_spec, kv_spec(C_AK), kv_spec(C_AV), a_spec],
        out_specs=[seq0] * 4 + [pl.BlockSpec((ST_ROWS, LANES), lambda b, h: (b, h)), a_spec, kv_out, kv_out,
                                pl.BlockSpec((8, LANES), lambda b, h: (b * KV_HEADS + h // 2, 0))],
        out_shape=[SDS((t, RET_W), BF16)] * 4 + [SDS((b_loc * ST_ROWS, RET_W), F32), SDS((t, ATT_W), BF16),
                                                   SDS((t, KV_W), BF16), SDS((t, KV_W), BF16),
                                                   SDS((b_loc * KV_HEADS * 8, LANES), F32)],
        scratch_shapes=[state, state, pltpu.VMEM((s, LANES), BF16), pad, pad,
                        pltpu.VMEM((GROUP * BLOCK, KEYS), F32), acc, acc],
        compiler_params=_params(("arbitrary", "arbitrary")),
    )(*after, lgf, lgb, sink, u, u, u, u, xhat, rstd, dr, gn_gain, u, u, u, da)


def _pack_small(acc2, acc1, ret_stats, dsink, b_loc, d):
    pairs = RET_HEADS // 2

    def body(acc2_ref, acc1_ref, st_ref, dsink_ref, out_ref):
        out_ref[...] = jnp.zeros_like(out_ref)
        out_ref[ROW_LN1G:ROW_LN1G + 1, :] = acc1_ref[0:1, :]
        out_ref[ROW_LN1B:ROW_LN1B + 1, :] = acc1_ref[1:2, :]
        out_ref[ROW_LN2G:ROW_LN2G + 1, :] = acc2_ref[1:2, :]
        out_ref[ROW_LN2B:ROW_LN2B + 1, :] = acc2_ref[2:3, :]
        out_ref[ROW_LOSS:ROW_LOSS + 1, :] = acc2_ref[0:1, :]
        st = st_ref[0:ST_ROWS, :]
        for b in range(1, b_loc):
            st = st + st_ref[b * ST_ROWS:(b + 1) * ST_ROWS, :]
        out_ref[ROW_GN:ROW_GN + 1, 0:RET_W] = st[ST_GAIN:ST_GAIN + 1, :]
        lane = lax.broadcasted_iota(jnp.int32, (1, d), 1)
        misc = jnp.zeros((1, d), F32)
        for pr in range(pairs):
            blk = st[:, pr * LANES:(pr + 1) * LANES]
            half = lax.broadcasted_iota(jnp.int32, (1, LANES), 1) < HEAD_DIM
            for h in range(2):
                sel = half if h == 0 else jnp.logical_not(half)
                cross_f = jnp.sum(jnp.where(sel, blk[ST_XF:ST_XF + 1, :] + blk[ST_LF:ST_LF + 1, :], 0.0))
                cross_b = jnp.sum(jnp.where(sel, blk[ST_XB:ST_XB + 1, :] + blk[ST_LB:ST_LB + 1, :], 0.0))
                intra_f = jnp.sum(blk[ST_IFA + h:ST_IFA + h + 1, :])
                intra_b = jnp.sum(blk[ST_IBA + h:ST_IBA + h + 1, :])
                head = 2 * pr + h
                misc = jnp.where(lane == MISC_DF + head, cross_f + intra_f, misc)
                misc = jnp.where(lane == MISC_DB + head, cross_b + intra_b, misc)
        for g in range(KV_HEADS):
            tot = dsink_ref[g * 8:(g + 1) * 8, :]
            for b in range(1, b_loc):
                tot = tot + dsink_ref[(b * KV_HEADS + g) * 8:(b * KV_HEADS + g + 1) * 8, :]
            for h in range(GROUP):
                misc = jnp.where(lane == MISC_SINK + GROUP * g + h, jnp.sum(tot[h:h + 1, 0:1]), misc)
        out_ref[ROW_MISC:ROW_MISC + 1, :] = misc

    return pl.pallas_call(body, name="pack_small", out_shape=SDS((SMALL_ROWS, d), F32))(acc2, acc1, ret_stats, dsink)


BIG = ("w_in", "w_out", "w_ffn_gate", "w_ffn_up", "w_ffn_down", "w_ple_proj", "w_ple_gate")
TRANSPOSED_OUTSIDE = ("w_in", "w_ffn_gate", "w_ffn_up")
TRANSPOSED_HERE = ("w_ple_proj",)
SMALL = ("ret_decay_fwd", "ret_decay_bwd", "ret_gn_gain", "attn_sink", "ln1_gain", "ln1_bias", "ln2_gain", "ln2_bias")
ORDER = ("w_in", "ret_decay_fwd", "ret_decay_bwd", "ret_gn_gain", "attn_sink", "w_out", "ln1_gain", "ln1_bias",
         "w_ffn_gate", "w_ffn_up", "w_ffn_down", "w_ple_proj", "w_ple_gate", "ln2_gain", "ln2_bias")


GATHER_ORDER = ("w_in", "w_ffn_up", "w_out", "w_ffn_gate", "w_ple_gate", "w_ple_proj", "w_ffn_down")
GATHER_TWO_LEVEL = ("w_in", "w_ffn_up")


def _local_step(x2, p2, target2, fetch, publish, small, b_loc, me):
    d = x2.shape[1]
    lgf, lgb = _log_decay(small["ret_decay_fwd"], small["ret_decay_bwd"])
    lgf1, lgb1, sink1 = lgf.reshape(-1), lgb.reshape(-1), small["attn_sink"].reshape(-1)
    (w_in,) = fetch(("w_in",), ())
    u, xb = _in_proj(x2, w_in)
    passed = fetch.pass_on("w_ffn_up", (xb,))
    r, ret_xhat, ret_rstd, a = _mixers_fwd(u, lgf1, lgb1, small["ret_gn_gain"], sink1, b_loc, passed)
    w_out, w_gate, w_up = fetch(("w_out", "w_ffn_gate", "w_ffn_up"), (r, a))
    z1, h1b, dact_dg, dact_du, act = _mix_ln1_ffn_up(
        r, a, x2, w_out, w_gate, w_up, small["ln1_gain"], small["ln1_bias"])
    w_pg, w_pe, w_down = fetch(("w_ple_gate", "w_ple_proj", "w_ffn_down"), (act,))
    dz2, dz2b, dsb, dpleb, dg, dup, acc2 = _ffn_down_ln2_loss(
        act, dact_dg, dact_du, h1b, p2, z1, target2, w_down, w_pg, w_pe,
        small["ln1_gain"], small["ln1_bias"], small["ln2_gain"], small["ln2_bias"])
    own = {}

    def grad(name, parts, rhs, after=()):
        whole, own[name] = _weight_grad("grad_" + name, me, parts, rhs, after)
        return whole

    ffn_jobs = dict(w_ffn_down=(act, dz2b), w_ple_proj=(dpleb, p2), w_ple_gate=(h1b, dsb),
                    w_ffn_gate=(dg, h1b), w_ffn_up=(dup, h1b))
    wholes, owns = _weight_grad_jobs("grad_w_ffn", me, list(ffn_jobs.values()))
    own.update(zip(ffn_jobs, owns))
    t2 = publish("ffn", dict(zip(ffn_jobs, wholes)))
    dz1, dz1b, dr, da, acc1 = _dh1_ln1_bwd(dz2, dg, dup, dsb, z1, w_gate, w_up, w_pg, w_out, small["ln1_gain"], t2)
    t3 = publish("out", dict(w_out=grad("w_out", [r, a], dz1b)))
    dq, dk, dv, dgate, ret_stats, daq, dak, dav, dsink = _mixers_bwd(
        u, ret_xhat, ret_rstd, dr, da, lgf1, lgb1, small["ret_gn_gain"], sink1, b_loc, t3)
    parts = [dq, dk, dv, dgate, daq, dak, dav]
    small_part = _pack_small(acc2, acc1, ret_stats, dsink, b_loc, d)
    t4 = publish("in", dict(w_in=grad("w_in", parts, xb)), small_part)
    grad_x = _in_proj_bwd(dz1, parts, w_in, t4)
    return grad_x, own, small_part


def kernel(x, p, w_in, ret_decay_fwd, ret_decay_bwd, ret_gn_gain, attn_sink, w_out, ln1_gain, ln1_bias, w_ffn_gate, w_ffn_up, w_ffn_down, w_ple_proj, w_ple_gate, ln2_gain, ln2_bias, loss_target, m_w_in, m_ret_decay_fwd, m_ret_decay_bwd, m_ret_gn_gain, m_attn_sink, m_w_out, m_ln1_gain, m_ln1_bias, m_w_ffn_gate, m_w_ffn_up, m_w_ffn_down, m_w_ple_proj, m_w_ple_gate, m_ln2_gain, m_ln2_bias, v_w_in, v_ret_decay_fwd, v_ret_decay_bwd, v_ret_gn_gain, v_attn_sink, v_w_out, v_ln1_gain, v_ln1_bias, v_w_ffn_gate, v_w_ffn_up, v_w_ffn_down, v_w_ple_proj, v_w_ple_gate, v_ln2_gain, v_ln2_bias):
    given = dict(locals())

    def strip(n, a):
        if n not in BIG:
            return a
        return a[0].T if n in TRANSPOSED_OUTSIDE else a[0]

    def restore(n, a):
        if n not in BIG:
            return a
        return (a.T if n in TRANSPOSED_OUTSIDE else a)[None]

    w = {n: strip(n, given[n]) for n in ORDER}
    m = {n: strip(n, given["m_" + n]) for n in ORDER}
    v = {n: strip(n, given["v_" + n]) for n in ORDER}
    b_loc, s, d = x.shape
    x2 = x.reshape(b_loc * s, d)
    p2 = p[0].reshape(b_loc * s, p.shape[-1])
    target2 = loss_target.reshape(b_loc * s, d)

    small = {n: w[n] for n in SMALL}
    me = (4 * lax.axis_index("x") + 2 * lax.axis_index("y") + lax.axis_index("c")).astype(jnp.int32).reshape(1)

    gathered = _prep_shards(me, {n: w[n] for n in BIG})
    gather = _split_copy_start(
        "gather_start", [(gathered[n],) for n in GATHER_ORDER],
        [_gather_copy_near if n in GATHER_TWO_LEVEL else _gather_copy for n in GATHER_ORDER])

    passing = {}

    def pass_on(n, after):
        near = _split_copy_wait("gather_wait_" + n + "_near", gather, [GATHER_ORDER.index(n)], list(after))
        passing[n] = _split_copy_start("gather_pass_" + n, near, [_gather_copy_pass])
        return (passing[n]["token"],)

    def fetch(names, after):
        out = {}
        for n in [n for n in names if n in GATHER_TWO_LEVEL]:
            if n not in passing:
                pass_on(n, after)
            out[n] = _split_copy_wait("gather_wait_" + n, passing[n], [0], list(after))[0][0]
        direct = [n for n in names if n not in GATHER_TWO_LEVEL]
        if direct:
            got = _split_copy_wait("gather_wait_" + direct[0], gather, [GATHER_ORDER.index(n) for n in direct],
                                   list(after))
            out.update({n: item[0] for n, item in zip(direct, got)})
        return [out[n] for n in names]

    scatters = []

    def publish(tag, products, small_sums=None):
        items = [(products[n], lax.empty((N_DEV - 1, products[n].shape[0] // N_DEV, products[n].shape[1]), BF16))
                 for n in products]
        copies = [_scatter_copy] * len(items)
        if small_sums is not None:
            items.append((small_sums, lax.empty((N_DEV - 1,) + small_sums.shape, F32)))
            copies.append(_small_copy)
        started = _split_copy_start("scatter_start_" + tag, items, copies)
        scatters.append((list(products), small_sums is not None, started))
        return (started["token"],)

    fetch.pass_on = pass_on
    grad_x, own, small_part = _local_step(x2, p2, target2, fetch, publish, small, b_loc, me)

    out_g, out_d, out_m, out_v = {}, {}, {}, {}
    after = [grad_x]
    for names, with_small, started in scatters:
        landed = _split_copy_wait("scatter_wait_" + names[0], started, list(range(len(started["items"]))), after)
        if with_small:
            loss, sg, sd, sm, sv = _small_adamw(
                me, small_part, landed[-1][1], small, {n: m[n] for n in SMALL}, {n: v[n] for n in SMALL})
            for dst, src in ((out_g, sg), (out_d, sd), (out_m, sm), (out_v, sv)):
                dst.update(src)
        recv = {n: item[1] for n, item in zip(names, landed)}
        alike = {}
        for n in names:
            alike.setdefault((own[n].shape, n in TRANSPOSED_HERE), []).append(n)
        for (_, transposed), ns in alike.items():
            res = _reduce_adamw(ns[0], [own[n] for n in ns], [recv[n] for n in ns], [w[n] for n in ns],
                                [m[n] for n in ns], [v[n] for n in ns], transposed)
            for dst, vals in zip((out_g, out_d, out_m, out_v), res):
                dst.update(zip(ns, vals))
        after = [out_v[names[-1]]]

    outs = [loss[0, 0], grad_x.reshape(x.shape)]
    for group in (out_g, out_d, out_m, out_v):
        outs += [restore(n, group[n]) for n in ORDER]
    return tuple(outs)
```

```python
import functools

import jax
import jax.numpy as jnp
from jax import lax
from jax.experimental import pallas as pl
from jax.experimental.pallas import tpu as pltpu

F32, BF16 = jnp.float32, jnp.bfloat16
SDS = jax.ShapeDtypeStruct
MESH = pl.DeviceIdType.MESH

N_DEV = 8
HEAD_DIM = 64
RET_HEADS = 8
ATTN_HEADS = 8
KV_HEADS = 2
GROUP = ATTN_HEADS // KV_HEADS
RET_W = RET_HEADS * HEAD_DIM
ATT_W = ATTN_HEADS * HEAD_DIM
KV_W = KV_HEADS * HEAD_DIM
LANES = 128
CHUNK = 128
BLOCK = 128
Q_SCALE = HEAD_DIM ** -0.5
ALPHA = 2.0 ** 0.25
LN_EPS = 1e-5
GN_EPS = 1e-5
NEG_INF = -1e30
C_RQ, C_RK, C_RV, C_RG = 0, RET_W, 2 * RET_W, 3 * RET_W
C_AQ = 4 * RET_W
C_AK = C_AQ + ATT_W
C_AV = C_AK + KV_W
IN_W = C_AV + KV_W

ADAM_LR = 0.001
ADAM_B1 = 0.9
ADAM_B2 = 0.999
ADAM_EPS = 1e-08
ADAM_WD = 0.01
ADAM_STEP = 10

VMEM_LIMIT = 56 * 1024 * 1024
MATMUL_ROWS = 512
EPILOGUE_ROWS = 256
SUB_ROWS = 256
SMALL_ROWS = 16
ROW_LN1G, ROW_LN1B, ROW_LN2G, ROW_LN2B, ROW_LOSS, ROW_GN, ROW_MISC = 0, 1, 2, 3, 4, 5, 6
MISC_DF, MISC_DB, MISC_SINK = 0, 8, 16


def _dot_nn(a, b):
    return lax.dot_general(a, b, (((1,), (0,)), ((), ())), preferred_element_type=F32)


def _dot_nt(a, b):
    return lax.dot_general(a, b, (((1,), (1,)), ((), ())), preferred_element_type=F32)


def _dot_tn(a, b):
    return lax.dot_general(a, b, (((0,), (0,)), ((), ())), preferred_element_type=F32)


def _params(sem=None, vmem=VMEM_LIMIT):
    kw = {"vmem_limit_bytes": vmem}
    if sem is not None:
        kw["dimension_semantics"] = sem
    return pltpu.CompilerParams(**kw)


def _row_tile(t, want=512):
    tm = want
    while t % tm:
        tm //= 2
    return tm


def _sigmoid(x):
    return jax.nn.sigmoid(x)


def _layer_norm_stats(z):
    mu = jnp.mean(z, axis=1, keepdims=True)
    d = z - mu
    var = jnp.mean(d * d, axis=1, keepdims=True)
    rstd = lax.rsqrt(var + LN_EPS)
    return d * rstd, rstd


def _layer_norm_bwd(dxh, xhat, rstd):
    m1 = jnp.mean(dxh, axis=1, keepdims=True)
    m2 = jnp.mean(dxh * xhat, axis=1, keepdims=True)
    return rstd * (dxh - m1 - xhat * m2)


def _prep_shards(me, shards):
    names = list(shards)

    def body(me_ref, *refs):
        for name, src, dst in zip(names, refs[:len(names)], refs[len(names):]):
            val = src[...]
            dst[...] = (val.T if name in TRANSPOSED_HERE else val).astype(BF16)

    shape = lambda n, a: a.shape[::-1] if n in TRANSPOSED_HERE else a.shape
    shapes = [shape(n, shards[n]) for n in names]
    out = pl.pallas_call(
        body, name="prep_shards",
        grid_spec=pltpu.PrefetchScalarGridSpec(
            num_scalar_prefetch=1, grid=(1,),
            in_specs=[pl.BlockSpec(shards[n].shape, lambda i, me_ref: (0, 0)) for n in names],
            out_specs=[pl.BlockSpec(s, lambda i, me_ref: (me_ref[0], 0)) for s in shapes]),
        out_shape=[SDS((N_DEV * s[0], s[1]), BF16) for s in shapes], compiler_params=_params(("arbitrary",)),
    )(me, *[shards[n] for n in names])
    return dict(zip(names, out))


def _mesh_pos():
    return lax.axis_index("x"), lax.axis_index("y"), lax.axis_index("c")


HBM_SPEC = pl.BlockSpec(memory_space=pltpu.HBM)
SEM_SPEC = pl.BlockSpec(memory_space=pltpu.SEMAPHORE)
ANY_SPEC = pl.BlockSpec(memory_space=pl.ANY)
SIDE_EFFECT = pltpu.SideEffectType.DATAFLOW_SIDE_EFFECTING
PEER_SEMS = pltpu.SemaphoreType.DMA((N_DEV - 1,))


def _in_hbm(a):
    return pltpu.with_memory_space_constraint(a, pltpu.HBM)


def _split_copy_start(name, items, copies):
    n = len(items)
    flat = [a for it in items for a in it]
    k = len(flat)

    def body(*refs):
        arr, sems = list(refs[:k]), refs[k:k + 2 * n]
        for i, it in enumerate(items):
            mine = [arr.pop(0) for _ in it]
            for m in range(1, N_DEV):
                cp = copies[i](m, mine, sems[i].at[m - 1], sems[n + i].at[m - 1])
                if cp is not None:
                    cp.start()
        token = refs[-1]
        token[...] = jnp.zeros_like(token)

    res = pl.pallas_call(
        body, name=name,
        out_shape=[PEER_SEMS] * (2 * n) + [pltpu.HBM(a.shape, a.dtype) for a in flat] + [SDS((8, LANES), F32)],
        in_specs=[HBM_SPEC] * k,
        out_specs=[SEM_SPEC] * (2 * n) + [HBM_SPEC] * k + [pl.BlockSpec(memory_space=pltpu.VMEM)],
        input_output_aliases={j: 2 * n + j for j in range(k)},
        compiler_params=pltpu.CompilerParams(has_side_effects=SIDE_EFFECT),
    )(*[_in_hbm(a) for a in flat])
    thru, out_items = list(res[2 * n:2 * n + k]), []
    for it in items:
        out_items.append(tuple(thru.pop(0) for _ in it))
    return dict(send=res[:n], recv=res[n:2 * n], items=out_items, token=res[-1], copies=copies)


def _split_copy_wait(name, started, which, after):
    items = [started["items"][i] for i in which]
    copies = [started["copies"][i] for i in which]
    n = len(items)
    flat = [a for it in items for a in it]
    k = len(flat)

    def body(*refs):
        arr, sems = list(refs[:k]), refs[k:k + 2 * n]
        for i, it in enumerate(items):
            mine = [arr.pop(0) for _ in it]
            for m in range(1, N_DEV):
                cp = copies[i](m, mine, sems[i].at[m - 1], sems[n + i].at[m - 1])
                if cp is not None:
                    cp.wait_send()
                    cp.wait_recv()

    res = pl.pallas_call(
        body, name=name,
        out_shape=[pltpu.HBM(a.shape, a.dtype) for a in flat],
        in_specs=[HBM_SPEC] * k + [SEM_SPEC] * (2 * n) + [ANY_SPEC] * len(after),
        out_specs=[HBM_SPEC] * k,
        input_output_aliases={j: j for j in range(k)},
        compiler_params=pltpu.CompilerParams(has_side_effects=SIDE_EFFECT),
    )(*flat, *[started["send"][i] for i in which], *[started["recv"][i] for i in which], *[_in_hbm(a) for a in after])
    thru, out_items = list(res), []
    for it in items:
        out_items.append(tuple(thru.pop(0) for _ in it))
    return out_items


def _gather_copy(m, refs, send_sem, recv_sem):
    (land_ref,) = refs
    r = land_ref.shape[0] // N_DEV
    mine = land_ref.at[pl.ds(pl.multiple_of(_peer_index(0) * r, 8), r), :]
    return pltpu.make_async_remote_copy(src_ref=mine, dst_ref=mine, send_sem=send_sem, recv_sem=recv_sem,
                                        device_id=_peer(m), device_id_type=MESH)


def _gather_copy_near(m, refs, send_sem, recv_sem):
    return _gather_copy(m, refs, send_sem, recv_sem) if m == 1 or m % 2 == 0 else None


def _gather_copy_pass(m, refs, send_sem, recv_sem):
    if m == 1 or m % 2 == 0:
        return None
    (land_ref,) = refs
    r = land_ref.shape[0] // N_DEV
    block = land_ref.at[pl.ds(pl.multiple_of(_peer_index(m ^ 1) * r, 8), r), :]
    return pltpu.make_async_remote_copy(src_ref=block, dst_ref=block, send_sem=send_sem, recv_sem=recv_sem,
                                        device_id=_peer(1), device_id_type=MESH)


def _small_copy(m, refs, send_sem, recv_sem):
    part_ref, land_ref = refs
    return pltpu.make_async_remote_copy(src_ref=part_ref, dst_ref=land_ref.at[m - 1], send_sem=send_sem,
                                        recv_sem=recv_sem, device_id=_peer(m), device_id_type=MESH)


def _scatter_copy(m, refs, send_sem, recv_sem):
    buf_ref, land_ref = refs
    r = buf_ref.shape[0] // N_DEV
    src = buf_ref.at[pl.ds(pl.multiple_of(_peer_index(m) * r, 8), r), :]
    return pltpu.make_async_remote_copy(src_ref=src, dst_ref=land_ref.at[m - 1], send_sem=send_sem,
                                        recv_sem=recv_sem, device_id=_peer(m), device_id_type=MESH)


def _peer(m):
    x, y, c = _mesh_pos()
    bx, by, bc = (m >> 2) & 1, (m >> 1) & 1, m & 1
    return (x ^ bx if bx else x, y ^ by if by else y, c ^ bc if bc else c)


def _peer_index(m):
    x, y, c = _mesh_pos()
    return (4 * x + 2 * y + c) ^ m


SMALL_PLACE = {
    "ln1_gain": (ROW_LN1G, 0), "ln1_bias": (ROW_LN1B, 0), "ln2_gain": (ROW_LN2G, 0), "ln2_bias": (ROW_LN2B, 0),
    "ret_gn_gain": (ROW_GN, 0), "ret_decay_fwd": (ROW_MISC, MISC_DF), "ret_decay_bwd": (ROW_MISC, MISC_DB),
    "attn_sink": (ROW_MISC, MISC_SINK)}


def _small_adamw(me, part, landed, w, m, v):
    d = part.shape[1]
    names = list(SMALL_PLACE)
    k = len(names)

    def body(*refs):
        me_ref, part_ref, land_ref = refs[:3]
        refs = refs[2:]
        w_refs, m_refs, v_refs = refs[1:1 + k], refs[1 + k:1 + 2 * k], refs[1 + 2 * k:1 + 3 * k]
        outs = refs[1 + 3 * k:1 + 7 * k + 1]
        tot_ref = refs[-1]
        loss_ref, g_refs, dl_refs = outs[0], outs[1:1 + k], outs[1 + k:1 + 2 * k]
        nm_refs, nv_refs = outs[1 + 2 * k:1 + 3 * k], outs[1 + 3 * k:1 + 4 * k]
        tot = jnp.zeros(part_ref.shape, F32)
        for dev in range(N_DEV):
            j = dev ^ me_ref[0]
            tot = tot + jnp.where(j == 0, part_ref[...], land_ref[jnp.maximum(j, 1) - 1])
        tot_ref[...] = tot
        loss_ref[...] = (0.5 / d) * jnp.sum(tot_ref[ROW_LOSS:ROW_LOSS + 1, :], axis=1, keepdims=True)
        for i, name in enumerate(names):
            row, lo = SMALL_PLACE[name]
            wv = w_refs[i][...]
            g = tot_ref[row:row + 1, lo:lo + wv.shape[1]]
            if name.startswith("ret_decay"):
                p2 = jnp.exp2(wv)
                g = g * (-p2 * jnp.log(2.0) / (1.0 - p2))
            g_refs[i][...] = g
            _adamw_store(g, wv, m_refs[i][...], v_refs[i][...], dl_refs[i], nm_refs[i], nv_refs[i])

    shapes = [SDS(w[n].shape, F32) for n in names]
    vm = pl.BlockSpec(memory_space=pltpu.VMEM)
    res = pl.pallas_call(
        body, name="small_adamw", out_shape=[SDS((1, 1), F32)] + shapes * 4,
        in_specs=[_smem_spec()] + [vm] * (2 + 3 * k), out_specs=[vm] * (1 + 4 * k),
        scratch_shapes=[pltpu.VMEM(part.shape, F32)],
    )(me, part, landed, *[w[n] for n in names], *[m[n] for n in names], *[v[n] for n in names])
    groups = [dict(zip(names, res[1 + j * k:1 + (j + 1) * k])) for j in range(4)]
    return (res[0], *groups)


def _adamw_store(g, w, m, v, dl_ref, nm_ref, nv_ref):
    m = ADAM_B1 * m + (1.0 - ADAM_B1) * g
    v = ADAM_B2 * v + (1.0 - ADAM_B2) * (g * g)
    m_hat = m / (1.0 - ADAM_B1 ** ADAM_STEP)
    v_hat = v / (1.0 - ADAM_B2 ** ADAM_STEP)
    dl_ref[...] = -ADAM_LR * (m_hat / (jnp.sqrt(v_hat) + ADAM_EPS) + ADAM_WD * w)
    nm_ref[...] = m
    nv_ref[...] = v


def _reduce_adamw(name, owns, recvs, ws, ms, vs, transposed):
    count = len(owns)
    rows, n = owns[0].shape
    steps = 1 if transposed or rows % 32 else 4
    rb = rows // steps

    def body(*refs):
        ins, outs = refs[:5 * count], refs[5 * count:]
        j = pl.program_id(0)
        for k in range(count):
            @pl.when(j == k)
            def _(k=k):
                own_ref, recv_ref, w_ref, m_ref, v_ref = ins[5 * k:5 * k + 5]
                g_ref, dl_ref, nm_ref, nv_ref = outs[4 * k:4 * k + 4]
                g = own_ref[...]
                for p in range(N_DEV - 1):
                    g = g + recv_ref[p].astype(F32)
                if transposed:
                    g = g.T
                g_ref[...] = g
                _adamw_store(g, w_ref[...], m_ref[...], v_ref[...], dl_ref, nm_ref, nv_ref)

    def turn(k):
        return lambda j, i: jnp.where(j == k, i, jnp.where(j < k, 0, steps - 1))

    in_specs, out_specs = [], []
    for k in range(count):
        at = turn(k)
        blk = pl.BlockSpec(ws[0].shape if transposed else (rb, n), lambda j, i, at=at: (at(j, i), 0))
        in_specs += [pl.BlockSpec((rb, n), lambda j, i, at=at: (at(j, i), 0)),
                     pl.BlockSpec((N_DEV - 1, rb, n), lambda j, i, at=at: (0, at(j, i), 0)), blk, blk, blk]
        out_specs += [blk] * 4
    res = pl.pallas_call(
        body, name="adamw_" + name, grid=(count, steps), in_specs=in_specs, out_specs=out_specs,
        out_shape=[SDS(ws[0].shape, F32)] * (4 * count), compiler_params=_params(("arbitrary", "arbitrary")),
    )(*[a for k in range(count) for a in (owns[k], recvs[k], ws[k], ms[k], vs[k])])
    return [list(res[j::4]) for j in range(4)]


def _row_spec(tm, width):
    return pl.BlockSpec((tm, width), lambda i: (i, 0))


def _full_spec(shape):
    return pl.BlockSpec(shape, lambda i: (0,) * len(shape))


_acc_spec = _full_spec


def _sub_rows(tm):
    step = min(SUB_ROWS, tm)
    return [(lo, lo + step) for lo in range(0, tm, step)]


def _in_proj(x2, wt_in):
    t, d = x2.shape
    u_w = wt_in.shape[0]
    tm = _row_tile(t, MATMUL_ROWS)

    def body(x_ref, w_ref, u_ref, xb_ref):
        xb = x_ref[...].astype(BF16)
        xb_ref[...] = xb
        u_ref[...] = _dot_nt(xb, w_ref[...]).astype(BF16)

    return pl.pallas_call(
        body, name="in_proj", grid=(t // tm,),
        in_specs=[_row_spec(tm, d), _full_spec(wt_in.shape)],
        out_specs=[_row_spec(tm, u_w), _row_spec(tm, d)],
        out_shape=[SDS((t, u_w), BF16), SDS((t, d), BF16)],
        compiler_params=_params(("parallel",)),
    )(x2, wt_in)


def _col_halves(f):
    n = f // LANES
    k = (n + 1) // 2 * LANES
    return [(0, k), (k, f)] if k < f else [(0, f)]


def _mix_ln1_ffn_up(r, a, x2, w_out, wt_gate, wt_up, g1, b1):
    t, d = x2.shape
    f = wt_gate.shape[0]
    tm = _row_tile(t, EPILOGUE_ROWS)

    def body(r_ref, a_ref, x_ref, wo_ref, wg_ref, wu_ref, g_ref, b_ref, z_ref, hb_ref, dg_ref, du_ref, act_ref):
        mix = _dot_nn(r_ref[...], wo_ref[0:RET_W, :]) + _dot_nn(a_ref[...], wo_ref[RET_W:RET_W + ATT_W, :])
        z = ALPHA * x_ref[...] + mix
        xhat, _ = _layer_norm_stats(z)
        z_ref[...] = z
        h = (xhat * g_ref[...] + b_ref[...]).astype(BF16)
        hb_ref[...] = h
        g = _dot_nt(h, wg_ref[...])
        u = _dot_nt(h, wu_ref[...])
        sg = _sigmoid(g)
        silu = g * sg
        dg_ref[...] = (u * (sg * (1.0 + g * (1.0 - sg)))).astype(BF16)
        du_ref[...] = silu.astype(BF16)
        act_ref[...] = (silu * u).astype(BF16)

    wide, narrow = _row_spec(tm, f), _row_spec(tm, d)
    return pl.pallas_call(
        body, name="mix_ln1_ffn_up", grid=(t // tm,),
        in_specs=[_row_spec(tm, RET_W), _row_spec(tm, ATT_W), narrow, _resident_spec(w_out.shape),
                  _resident_spec(wt_gate.shape), _resident_spec(wt_up.shape), _full_spec(g1.shape),
                  _full_spec(b1.shape)],
        out_specs=[narrow, narrow, wide, wide, wide],
        out_shape=[SDS((t, d), F32), SDS((t, d), BF16)] + [SDS((t, f), BF16)] * 3,
        compiler_params=_params(("parallel",)),
    )(r, a, x2, w_out, wt_gate, wt_up, g1, b1)


def _ffn_down_ln2_loss(act, dact_dg, dact_du, h1b, p2, z1, target, w_down, w_pg, wt_pe, g1, b1, g2, b2):
    t, d = z1.shape
    f = act.shape[1]
    pdim = p2.shape[1]
    tm = _row_tile(t, EPILOGUE_ROWS)

    def body(act_ref, fg_ref, fu_ref, hb_ref, p_ref, z1_ref, tgt_ref, wd_ref, wpg_ref, wpe_ref, g1_ref, b1_ref,
             g2_ref, b2_ref, dz_ref, dzb_ref, ds_ref, dple_ref, dg_ref, du_ref, acc_ref):
        @pl.when(pl.program_id(0) == 0)
        def _():
            acc_ref[...] = jnp.zeros_like(acc_ref)

        for lo, hi in _sub_rows(tm):
            xhat1, _ = _layer_norm_stats(z1_ref[lo:hi, :])
            h1 = xhat1 * g1_ref[...] + b1_ref[...]
            pg = _sigmoid(_dot_nn(hb_ref[lo:hi, :], wpg_ref[...]))
            ple = _dot_nt(p_ref[lo:hi, :].astype(BF16), wpe_ref[...])
            gated = pg * ple
            dgate = gated * (1.0 - pg)
            ffn = _dot_nn(act_ref[lo:hi, :], wd_ref[...])
            z2 = ALPHA * h1 + gated + ffn
            xhat2, rstd2 = _layer_norm_stats(z2)
            err = xhat2 * g2_ref[...] + b2_ref[...] - tgt_ref[lo:hi, :]
            dy = err * (1.0 / d)
            dz = _layer_norm_bwd(dy * g2_ref[...], xhat2, rstd2)
            dzb = dz.astype(BF16)
            dz_ref[lo:hi, :] = dz
            dzb_ref[lo:hi, :] = dzb
            ds_ref[lo:hi, :] = (dz * dgate).astype(BF16)
            dple_ref[lo:hi, :] = (dz * pg).astype(BF16)
            acc_ref[0:1, :] += jnp.sum(err * err, axis=0, keepdims=True)
            acc_ref[1:2, :] += jnp.sum(dy * xhat2, axis=0, keepdims=True)
            acc_ref[2:3, :] += jnp.sum(dy, axis=0, keepdims=True)
            for c0, c1 in _col_halves(f):
                da = _dot_nt(dzb, wd_ref[c0:c1, :])
                dg_ref[lo:hi, c0:c1] = (da * fg_ref[lo:hi, c0:c1].astype(F32)).astype(BF16)
                du_ref[lo:hi, c0:c1] = (da * fu_ref[lo:hi, c0:c1].astype(F32)).astype(BF16)

    vec = _full_spec(g1.shape)
    wide, narrow = _row_spec(tm, f), _row_spec(tm, d)
    return pl.pallas_call(
        body, name="ffn_down_ln2_loss", grid=(t // tm,),
        in_specs=[wide, wide, wide, narrow, _row_spec(tm, pdim), narrow, narrow,
                  _full_spec(w_down.shape), _full_spec(w_pg.shape), _full_spec(wt_pe.shape), vec, vec, vec, vec],
        out_specs=[narrow] * 4 + [wide, wide, _acc_spec((8, d))],
        out_shape=[SDS((t, d), F32), SDS((t, d), BF16), SDS((t, d), BF16), SDS((t, d), BF16),
                   SDS((t, f), BF16), SDS((t, f), BF16), SDS((8, d), F32)],
        compiler_params=_params(("arbitrary",)),
    )(act, dact_dg, dact_du, h1b, p2, z1, target, w_down, w_pg, wt_pe, g1, b1, g2, b2)


def _after(after, body):
    k = len(after)
    return (lambda *refs: body(*refs[k:])), [ANY_SPEC] * k


def _resident_spec(shape):
    return pl.BlockSpec(shape, lambda i: (0,) * len(shape), pipeline_mode=pl.Buffered(1))


def _dh1_ln1_bwd(dz2, dg, dup, dsb, z1, wt_gate, wt_up, w_pg, w_out, g1, after=()):
    t, d = dz2.shape
    f = dg.shape[1]
    tm = _row_tile(t, EPILOGUE_ROWS)

    def body(dz_ref, dg_ref, du_ref, ds_ref, z1_ref, wg_ref, wu_ref, wpg_ref, wo_ref, g1_ref,
             dz1_ref, dz1b_ref, dr_ref, da_ref, acc_ref):
        @pl.when(pl.program_id(0) == 0)
        def _():
            acc_ref[...] = jnp.zeros_like(acc_ref)

        for lo, hi in _sub_rows(tm):
            dh = (ALPHA * dz_ref[lo:hi, :] + _dot_nn(dg_ref[lo:hi, :], wg_ref[...])
                  + _dot_nn(du_ref[lo:hi, :], wu_ref[...]) + _dot_nt(ds_ref[lo:hi, :], wpg_ref[...]))
            xhat, rstd = _layer_norm_stats(z1_ref[lo:hi, :])
            dz1 = _layer_norm_bwd(dh * g1_ref[...], xhat, rstd)
            dz1b = dz1.astype(BF16)
            dz1_ref[lo:hi, :] = dz1
            dz1b_ref[lo:hi, :] = dz1b
            acc_ref[0:1, :] += jnp.sum(dh * xhat, axis=0, keepdims=True)
            acc_ref[1:2, :] += jnp.sum(dh, axis=0, keepdims=True)
            dr_ref[lo:hi, :] = _dot_nt(dz1b, wo_ref[0:RET_W, :]).astype(BF16)
            da_ref[lo:hi, :] = _dot_nt(dz1b, wo_ref[RET_W:RET_W + ATT_W, :]).astype(BF16)

    body, lead = _after(after, body)
    return pl.pallas_call(
        body, name="dh1_ln1_bwd", grid=(t // tm,),
        in_specs=lead + [_row_spec(tm, d), _row_spec(tm, f), _row_spec(tm, f), _row_spec(tm, d), _row_spec(tm, d),
                         _resident_spec(wt_gate.shape), _resident_spec(wt_up.shape), _resident_spec(w_pg.shape),
                         _resident_spec(w_out.shape), _full_spec(g1.shape)],
        out_specs=[_row_spec(tm, d), _row_spec(tm, d), _row_spec(tm, RET_W), _row_spec(tm, ATT_W), _acc_spec((8, d))],
        out_shape=[SDS((t, d), F32), SDS((t, d), BF16), SDS((t, RET_W), BF16), SDS((t, ATT_W), BF16),
                   SDS((8, d), F32)],
        compiler_params=_params(("arbitrary",)),
    )(*after, dz2, dg, dup, dsb, z1, wt_gate, wt_up, w_pg, w_out, g1)


def _in_proj_bwd(dz1, parts, wt_in, after=()):
    t, d = dz1.shape
    tm = _row_tile(t, MATMUL_ROWS)
    widths = [p.shape[1] for p in parts]

    def body(*refs):
        dz_ref, part_refs, w_ref, dx_ref = refs[0], refs[1:1 + len(parts)], refs[-2], refs[-1]
        acc = ALPHA * dz_ref[...]
        lo = 0
        for p_ref, w in zip(part_refs, widths):
            acc = acc + _dot_nn(p_ref[...], w_ref[lo:lo + w, :])
            lo += w
        dx_ref[...] = acc

    body, lead = _after(after, body)
    return pl.pallas_call(
        body, name="in_proj_bwd", grid=(t // tm,),
        in_specs=lead + [_row_spec(tm, d)] + [_row_spec(tm, w) for w in widths] + [_full_spec(wt_in.shape)],
        out_specs=_row_spec(tm, d), out_shape=SDS((t, d), F32),
        compiler_params=_params(("parallel",)),
    )(*after, dz1, *parts, wt_in)


def _weight_grad(name, me, parts, rhs, after=()):
    t, n = rhs.shape
    widths = [p.shape[1] for p in parts]
    rows = sum(widths)
    own_rows = rows // N_DEV
    tk = _row_tile(t, MATMUL_ROWS)
    n_steps = t // tk
    step = 256

    def body(*refs):
        me_ref, part_refs, rhs_ref = refs[0], refs[1:1 + len(parts)], refs[1 + len(parts)]
        full_ref, own_ref, acc = refs[-3], refs[-2], refs[-1]
        i = pl.program_id(0)

        def products(first):
            b = rhs_ref[...].astype(BF16)
            lo = 0
            for p_ref, w in zip(part_refs, widths):
                for c0 in range(0, w, step):
                    c1 = min(c0 + step, w)
                    val = _dot_tn(p_ref[:, c0:c1].astype(BF16), b)
                    if first:
                        acc[lo + c0:lo + c1, :] = val
                    else:
                        acc[lo + c0:lo + c1, :] += val
                lo += w

        pl.when(i == 0)(functools.partial(products, True))
        pl.when(i > 0)(functools.partial(products, False))

        @pl.when(i == n_steps - 1)
        def _():
            full_ref[...] = acc[...].astype(BF16)
            own_ref[...] = acc[pl.ds(pl.multiple_of(me_ref[0] * own_rows, 8), own_rows), :]

    body, lead = _after(after, body)
    return pl.pallas_call(
        body, name=name, grid=(n_steps,),
        in_specs=lead + [_smem_spec()] + [_row_spec(tk, w) for w in widths] + [_row_spec(tk, n)],
        out_specs=[_full_spec((rows, n)), _full_spec((own_rows, n))],
        out_shape=[SDS((rows, n), BF16), SDS((own_rows, n), F32)],
        scratch_shapes=[pltpu.VMEM((rows, n), F32)],
        compiler_params=_params(("arbitrary",)),
    )(*after, me, *parts, rhs)


def _weight_grad_jobs(name, me, jobs, after=()):
    count = len(jobs)
    t = jobs[0][0].shape[0]
    tk = _row_tile(t, MATMUL_ROWS)
    n_steps = t // tk
    shapes = [(lhs.shape[1], rhs.shape[1]) for lhs, rhs in jobs]
    most_rows, most_cols = max(r for r, _ in shapes), max(n for _, n in shapes)
    step = 256

    def body(*refs):
        me_ref, lhs_refs, rhs_refs = refs[0], refs[1:1 + count], refs[1 + count:1 + 2 * count]
        full_refs, own_refs = refs[1 + 2 * count:1 + 3 * count], refs[1 + 3 * count:1 + 4 * count]
        acc, whole, mine, sems = refs[1 + 4 * count:]
        job, i = pl.program_id(0), pl.program_id(1)

        def leaving(j):
            rows, n = shapes[j]
            return (pltpu.make_async_copy(whole.at[0:rows, 0:n], full_refs[j], sems.at[0]),
                    pltpu.make_async_copy(mine.at[0:rows // N_DEV, 0:n], own_refs[j], sems.at[1]))

        def products(j, first):
            rows, n = shapes[j]
            b = rhs_refs[j][...].astype(BF16)
            for c0 in range(0, rows, step):
                c1 = min(c0 + step, rows)
                val = _dot_tn(lhs_refs[j][:, c0:c1].astype(BF16), b)
                if first:
                    acc[c0:c1, 0:n] = val
                else:
                    acc[c0:c1, 0:n] += val

        def finish(j):
            rows, n = shapes[j]
            own_rows = rows // N_DEV
            if j > 0:
                for cp in leaving(j - 1):
                    cp.wait()
            whole[0:rows, 0:n] = acc[0:rows, 0:n].astype(BF16)
            mine[0:own_rows, 0:n] = acc[pl.ds(pl.multiple_of(me_ref[0] * own_rows, 8), own_rows), 0:n]
            for cp in leaving(j):
                cp.start()
            if j == count - 1:
                for cp in leaving(j):
                    cp.wait()

        for j in range(count):
            pl.when((job == j) & (i == 0))(functools.partial(products, j, True))
            pl.when((job == j) & (i > 0))(functools.partial(products, j, False))
            pl.when((job == j) & (i == n_steps - 1))(functools.partial(finish, j))

    def turn(j):
        return lambda job, i: (jnp.where(job == j, i, jnp.where(job < j, 0, n_steps - 1)), 0)

    body, lead = _after(after, body)
    res = pl.pallas_call(
        body, name=name, grid=(count, n_steps),
        in_specs=lead + [_smem_spec()] + [pl.BlockSpec((tk, rows), turn(j)) for j, (rows, _) in enumerate(shapes)]
        + [pl.BlockSpec((tk, n), turn(j)) for j, (_, n) in enumerate(shapes)],
        out_specs=[ANY_SPEC] * (2 * count),
        out_shape=[SDS((rows, n), BF16) for rows, n in shapes] + [SDS((rows // N_DEV, n), F32) for rows, n in shapes],
        scratch_shapes=[pltpu.VMEM((most_rows, most_cols), F32), pltpu.VMEM((most_rows, most_cols), BF16),
                        pltpu.VMEM((most_rows // N_DEV, most_cols), F32), pltpu.SemaphoreType.DMA((2,))],
        compiler_params=_params(("arbitrary", "arbitrary")),
    )(*after, me, *[lhs for lhs, _ in jobs], *[rhs for _, rhs in jobs])
    return list(res[:count]), list(res[count:])


def _log_decay(decay_f, decay_b):
    def body(f_ref, b_ref, lf_ref, lb_ref):
        lf_ref[...] = jnp.log1p(-jnp.exp2(f_ref[...]))
        lb_ref[...] = jnp.log1p(-jnp.exp2(b_ref[...]))

    return pl.pallas_call(body, name="log_decay", out_shape=[SDS(decay_f.shape, F32)] * 2)(decay_f, decay_b)


def _chunk(ref, n):
    return ref[pl.ds(pl.multiple_of(n * CHUNK, CHUNK), CHUNK), :]


def _group_sum(is_a, v):
    sa = jnp.sum(jnp.where(is_a, v, 0.0), axis=1, keepdims=True)
    sb = jnp.sum(jnp.where(is_a, 0.0, v), axis=1, keepdims=True)
    return jnp.where(is_a, sa, sb)


def _seq_spec(s, col_block):
    return pl.BlockSpec((s, LANES), lambda b, h: (b, col_block + h))


def _smem_spec():
    return pl.BlockSpec(memory_space=pltpu.SMEM)


RET_UNROLL = 4


def _chunk_loop(n_chunks, body, init):
    u = RET_UNROLL if n_chunks % RET_UNROLL == 0 else 1

    def trip(i, carry):
        for j in range(u):
            carry = body(i * u + j, carry)
        return carry

    return lax.fori_loop(0, n_chunks // u, trip, init)


def _stacked_tables(lgf_ref, lgb_ref, pair):
    lane = lax.broadcasted_iota(jnp.int32, (1, LANES), 1)
    is_a = lane < HEAD_DIM
    lgf = jnp.where(is_a, lgf_ref[2 * pair], lgf_ref[2 * pair + 1])
    lgb = jnp.where(is_a, lgb_ref[2 * pair], lgb_ref[2 * pair + 1])
    row = lax.broadcasted_iota(jnp.int32, (CHUNK, 1), 0).astype(F32)
    kdec_f, qdec_f = jnp.exp(lgf * (CHUNK - 1.0 - row)), jnp.exp(lgf * (row + 1.0))
    kdec_b, qdec_b = jnp.exp(lgb * row), jnp.exp(lgb * (CHUNK - row))
    tab = dict(
        is_a=is_a, row=row, lam_f=jnp.exp(lgf * CHUNK), lam_b=jnp.exp(lgb * CHUNK),
        kdec=jnp.concatenate([kdec_f, kdec_b], axis=1), qdec=jnp.concatenate([qdec_f, qdec_b], axis=1),
        qexp=jnp.concatenate([jnp.broadcast_to(row + 1.0, (CHUNK, LANES)),
                              jnp.broadcast_to(CHUNK - row, (CHUNK, LANES))], axis=1),
        kexp=jnp.concatenate([jnp.broadcast_to(CHUNK - 1.0 - row, (CHUNK, LANES)),
                              jnp.broadcast_to(row, (CHUNK, LANES))], axis=1),
    )
    r = lax.broadcasted_iota(jnp.int32, (2 * LANES, LANES), 0)
    c = lax.broadcasted_iota(jnp.int32, (2 * LANES, LANES), 1)
    tab["diag2"] = ((r & (LANES - 1)) < HEAD_DIM) == (c < HEAD_DIM)
    i2 = lax.broadcasted_iota(jnp.int32, (2 * CHUNK, CHUNK), 0)
    j = lax.broadcasted_iota(jnp.int32, (2 * CHUNK, CHUNK), 1)
    head_b = i2 >= CHUNK
    diff = ((i2 & (CHUNK - 1)) - j).astype(F32)
    up, dn = jnp.maximum(diff, 0.0), jnp.maximum(-diff, 0.0)
    lgf2 = jnp.where(head_b, lgf_ref[2 * pair + 1], lgf_ref[2 * pair])
    lgb2 = jnp.where(head_b, lgb_ref[2 * pair + 1], lgb_ref[2 * pair])
    ef = jnp.where(diff >= 0, jnp.exp(lgf2 * up), 0.0)
    eb = jnp.where(diff <= 0, jnp.exp(lgb2 * dn), 0.0)
    tab["d2"] = ef + eb
    tab["df2"] = ef * up
    tab["db2"] = eb * dn
    return tab


def _stack_pair(is_a, x):
    zero = jnp.zeros_like(x)
    return jnp.concatenate([jnp.where(is_a, x, zero), jnp.where(is_a, zero, x)], axis=0)


def _unstack_pair(is_a, x2):
    return jnp.where(is_a, x2[0:CHUNK, :], x2[CHUNK:2 * CHUNK, :])


def _both_ways(x, dec):
    return (jnp.concatenate([x, x], axis=1) * dec).astype(BF16)


def _scan_states(n_chunks, st, up_rows, up_lam, down_rows, down_lam):
    zero = jnp.zeros((LANES, LANES), F32)

    def up(n, r):
        new = st[n, up_rows, :]
        st[n, up_rows, :] = r
        return r * up_lam + new

    def down(s, r):
        n = n_chunks - 1 - s
        new = st[n, down_rows, :]
        st[n, down_rows, :] = r
        return r * down_lam + new

    lax.fori_loop(0, n_chunks, up, zero)
    lax.fori_loop(0, n_chunks, down, zero)


FWD_ROWS, BWD_ROWS = pl.ds(0, LANES), pl.ds(LANES, LANES)


def _state_spec(n_chunks, pairs):
    return pl.BlockSpec((n_chunks, 2 * LANES, LANES), lambda b, h: (b * pairs + h, 0, 0))


ST_GAIN, ST_XF, ST_XB, ST_IFA, ST_IFB, ST_IBA, ST_IBB, ST_LF, ST_LB = 0, 1, 2, 3, 4, 5, 6, 8, 9
ST_ROWS = 16


GW = GROUP * HEAD_DIM
KEYS = 3 * BLOCK


def _attn_tables(g, bias_ref):
    r = lax.broadcasted_iota(jnp.int32, (GROUP * BLOCK, KEYS), 0)
    kj = lax.broadcasted_iota(jnp.int32, (GROUP * BLOCK, KEYS), 1)
    qi = r & (BLOCK - 1)
    hh = lax.shift_right_logical(r, 7)
    dist = jnp.abs(kj - BLOCK - qi)
    slope = jnp.exp2(-(GROUP * g + hh + 1).astype(F32) * (8.0 / ATTN_HEADS))
    bias_ref[...] = jnp.where(dist <= BLOCK, -slope * dist.astype(F32), NEG_INF)


def _own_lanes(g):
    return lax.shift_right_logical(lax.broadcasted_iota(jnp.int32, (1, LANES), 1), 6) == g


def _mask_keys(x_ref, g, scale, pad_ref, s):
    pad_ref[0:BLOCK, :] = jnp.zeros((BLOCK, LANES), BF16)
    pad_ref[BLOCK + s:2 * BLOCK + s, :] = jnp.zeros((BLOCK, LANES), BF16)
    pad_ref[BLOCK:BLOCK + s, :] = jnp.where(_own_lanes(g), x_ref[...].astype(F32) * scale, 0.0).astype(BF16)


def _lane_block(x, j):
    return x[:, j * LANES:(j + 1) * LANES]


def _stack_heads(x, g):
    assert GROUP == 4 and GW == 2 * LANES
    x1 = pltpu.roll(x, HEAD_DIM, 1)
    keep = _own_lanes(g)
    zero = jnp.zeros((BLOCK, LANES), x.dtype)
    rows = []
    for h in range(GROUP):
        for_g0 = _lane_block(x, h // 2) if h % 2 == 0 else _lane_block(x1, ((h + 1) // 2) % 2)
        for_g1 = _lane_block(x, h // 2) if h % 2 == 1 else _lane_block(x1, h // 2)
        rows.append(jnp.where(keep, jnp.where(g == 0, for_g0, for_g1), zero))
    return jnp.concatenate(rows, axis=0)


def _unstack_heads(x4, g):
    p = [x4[h * BLOCK:(h + 1) * BLOCK, :] for h in range(GROUP)]
    cat = lambda a, b: jnp.concatenate([a, b], axis=1)
    in_place = jnp.where(g == 0, cat(p[0], p[2]), cat(p[1], p[3]))
    one_left = jnp.where(g == 0, cat(p[1], p[3]), cat(p[2], p[0]))
    return in_place + pltpu.roll(one_left, HEAD_DIM, 1)


def _sink_column(sink_ref, g):
    rh = lax.shift_right_logical(lax.broadcasted_iota(jnp.int32, (GROUP * BLOCK, 1), 0), 7)
    col = jnp.zeros((GROUP * BLOCK, 1), F32)
    for h in range(GROUP):
        col = jnp.where(rh == h, sink_ref[GROUP * g + h], col)
    return col


def _attn_probs(qm, k3, bias_ref, sink_col, n, s):
    logits = _dot_nt(qm, k3) + bias_ref[...]
    kpos = n * BLOCK - BLOCK + lax.broadcasted_iota(jnp.int32, (1, KEYS), 1)
    logits = jnp.where((kpos >= 0) & (kpos < s), logits, NEG_INF)
    m = jnp.maximum(jnp.max(logits, axis=1, keepdims=True), sink_col)
    e = jnp.exp(logits - m)
    e_sink = jnp.exp(sink_col - m)
    inv = 1.0 / (jnp.sum(e, axis=1, keepdims=True) + e_sink)
    return e * inv, e_sink * inv


PAIRS_PER_KV = (RET_HEADS // 2) // KV_HEADS
FWD_ORDER = "rrarra"
BWD_ORDER = "rararr"


def _trip_order(order, chunks, blocks):
    if order.count("r") == chunks and order.count("a") == blocks:
        return order
    return "r" * chunks + "a" * blocks


def _mixers_fwd(u, lgf, lgb, gn_gain, sink, b_loc, after=()):
    t = u.shape[0]
    s = t // b_loc
    n_chunks = s // CHUNK
    pairs = RET_HEADS // 2
    trips = n_chunks // RET_UNROLL
    blocks_half = (s // BLOCK) // PAIRS_PER_KV
    per_trip = blocks_half // trips
    assert n_chunks % RET_UNROLL == 0 and blocks_half % trips == 0 and PAIRS_PER_KV == 2

    def body(lgf_ref, lgb_ref, sink_ref, q_ref, k_ref, v_ref, g_ref, gain_ref, aq_ref, ak_ref, av_ref,
             r_ref, xhat_ref, rstd_ref, a_ref, st, kpad, vpad, bias):
        pair = pl.program_id(1)
        g, half = lax.shift_right_logical(pair, 1), pair & 1
        tab = _stacked_tables(lgf_ref, lgb_ref, pair)
        is_a = tab["is_a"]

        @pl.when(half == 0)
        def _():
            _attn_tables(g, bias)
            _mask_keys(ak_ref, g, Q_SCALE, kpad, s)
            _mask_keys(av_ref, g, 1.0, vpad, s)

        def kv_body(n, _):
            k8 = _chunk(k_ref, n).astype(F32) * Q_SCALE
            st[n] = jnp.where(tab["diag2"], _dot_tn(_both_ways(k8, tab["kdec"]), _chunk(v_ref, n)), 0.0)
            return 0

        _chunk_loop(n_chunks, kv_body, 0)
        _scan_states(n_chunks, st, FWD_ROWS, tab["lam_f"], BWD_ROWS, tab["lam_b"])
        sink_col = _sink_column(sink_ref, g)

        def retention_chunk(n):
            q = _chunk(q_ref, n)
            k8 = (_chunk(k_ref, n).astype(F32) * Q_SCALE).astype(BF16)
            v = _chunk(v_ref, n)
            p2 = (_dot_nt(_stack_pair(is_a, q), k8) * tab["d2"]).astype(BF16)
            y = _unstack_pair(is_a, _dot_nn(p2, v))
            y = y + _dot_nn(_both_ways(q.astype(F32), tab["qdec"]), st[n].astype(BF16))
            rows = pl.ds(pl.multiple_of(n * CHUNK, CHUNK), CHUNK)
            mu = _group_sum(is_a, y) * (1.0 / HEAD_DIM)
            dlt = y - mu
            var = _group_sum(is_a, dlt * dlt) * (1.0 / HEAD_DIM)
            rstd = lax.rsqrt(var + GN_EPS)
            xhat = dlt * rstd
            xhat_ref[rows, :] = xhat
            rstd_ref[rows, :] = rstd
            gate = _chunk(g_ref, n).astype(F32)
            r_ref[rows, :] = (xhat * gain_ref[...] * gate * _sigmoid(gate)).astype(BF16)

        def attention_block(blk):
            n = half * blocks_half + blk
            rows = pl.ds(pl.multiple_of(blk * BLOCK, BLOCK), BLOCK)
            keys = pl.ds(pl.multiple_of(n * BLOCK, BLOCK), KEYS)
            p, _ = _attn_probs(_stack_heads(aq_ref[rows, :], g), kpad[keys, :], bias, sink_col, n, s)
            a_ref[rows, :] = _unstack_heads(_dot_nn(p.astype(BF16), vpad[keys, :]), g).astype(BF16)

        def trip(i, _):
            chunk, blk = 0, 0
            for kind in _trip_order(FWD_ORDER, RET_UNROLL, per_trip):
                if kind == "r":
                    retention_chunk(i * RET_UNROLL + chunk)
                    chunk += 1
                else:
                    attention_block(i * per_trip + blk)
                    blk += 1
            return 0

        lax.fori_loop(0, trips, trip, 0)

    lane_blk = lambda c0: _seq_spec(s, c0 // LANES)
    half_rows = blocks_half * BLOCK
    aq_spec = pl.BlockSpec((half_rows, GW), lambda b, h: (b * PAIRS_PER_KV + (h & 1), C_AQ // GW + h // 2))
    a_spec = pl.BlockSpec((half_rows, GW), lambda b, h: (b * PAIRS_PER_KV + (h & 1), h // 2))
    kv_spec = lambda c0: pl.BlockSpec((s, LANES), lambda b, h: (b, c0 // LANES))
    pad = pltpu.VMEM((s + 2 * BLOCK, LANES), BF16)
    body, lead = _after(after, body)
    return pl.pallas_call(
        body, name="mixers_fwd", grid=(b_loc, pairs),
        in_specs=lead + [_smem_spec(), _smem_spec(), _smem_spec(), lane_blk(C_RQ), lane_blk(C_RK), lane_blk(C_RV),
                         lane_blk(C_RG), pl.BlockSpec((1, LANES), lambda b, h: (0, h)), aq_spec, kv_spec(C_AK),
                         kv_spec(C_AV)],
        out_specs=[_seq_spec(s, 0), _seq_spec(s, 0), _seq_spec(s, 0), a_spec, _state_spec(n_chunks, pairs)],
        out_shape=[SDS((t, RET_W), BF16), SDS((t, RET_W), F32), SDS((t, RET_W), F32), SDS((t, ATT_W), BF16),
                   SDS((b_loc * pairs * n_chunks, 2 * LANES, LANES), F32)],
        scratch_shapes=[pad, pad, pltpu.VMEM((GROUP * BLOCK, KEYS), F32)],
        compiler_params=_params(("arbitrary", "arbitrary")),
    )(*after, lgf, lgb, sink, u, u, u, u, gn_gain, u, u, u)


def _mixers_bwd(u, xhat, rstd, states, dr, da, lgf, lgb, gn_gain, sink, b_loc, after=()):
    t = u.shape[0]
    s = t // b_loc
    n_chunks = s // CHUNK
    pairs = RET_HEADS // 2
    trips = n_chunks // RET_UNROLL
    blocks_half = (s // BLOCK) // PAIRS_PER_KV
    per_trip = blocks_half // trips
    assert n_chunks % RET_UNROLL == 0 and blocks_half % trips == 0 and PAIRS_PER_KV == 2

    def body(lgf_ref, lgb_ref, sink_ref, q_ref, k_ref, v_ref, g_ref, xhat_ref, rstd_ref, dr_ref, gain_ref,
             aq_ref, ak_ref, av_ref, do_ref, st,
             dq_ref, dk_ref, dv_ref, dg_ref, st_ref, daq_ref, dak_ref, dav_ref, dsink_ref,
             gr, dy_s, kpad, vpad, bias, dk_acc, dv_acc):
        pair = pl.program_id(1)
        g, half = lax.shift_right_logical(pair, 1), pair & 1
        tab = _stacked_tables(lgf_ref, lgb_ref, pair)
        is_a = tab["is_a"]
        gain = gain_ref[...]

        @pl.when(half == 0)
        def _():
            _attn_tables(g, bias)
            _mask_keys(ak_ref, g, Q_SCALE, kpad, s)
            _mask_keys(av_ref, g, 1.0, vpad, s)
            dsink_ref[...] = jnp.zeros_like(dsink_ref)

        @pl.when(pair == 0)
        def _():
            dk_acc[...] = jnp.zeros_like(dk_acc)
            dv_acc[...] = jnp.zeros_like(dv_acc)

        def norm_body(n, dgain):
            rows = pl.ds(pl.multiple_of(n * CHUNK, CHUNK), CHUNK)
            xhat, rstd = xhat_ref[rows, :], rstd_ref[rows, :]
            gate = g_ref[rows, :].astype(F32)
            sg = _sigmoid(gate)
            silu = gate * sg
            d_out = dr_ref[rows, :].astype(F32)
            dg_ref[rows, :] = (d_out * xhat * gain * (sg * (1.0 + gate * (1.0 - sg)))).astype(BF16)
            dxh = d_out * gain * silu
            m1 = _group_sum(is_a, dxh) * (1.0 / HEAD_DIM)
            m2 = _group_sum(is_a, dxh * xhat) * (1.0 / HEAD_DIM)
            dy = (rstd * (dxh - m1 - xhat * m2)).astype(BF16)
            dy_s[rows, :] = dy
            qf = q_ref[rows, :].astype(F32)
            gr[n] = jnp.where(tab["diag2"], _dot_tn(_both_ways(qf, tab["qdec"]), dy), 0.0)
            return dgain + jnp.sum(d_out * xhat * silu, axis=0, keepdims=True)

        colsum = lambda x: jnp.sum(x, axis=0, keepdims=True)

        def grad_body(n, carry):
            xfb, ifa, ifb, iba, ibb, lf, lb = carry
            rows = pl.ds(pl.multiple_of(n * CHUNK, CHUNK), CHUNK)
            q = q_ref[rows, :]
            qf = q.astype(F32)
            k8f = k_ref[rows, :].astype(F32) * Q_SCALE
            k8 = k8f.astype(BF16)
            v = v_ref[rows, :]
            dy = dy_s[rows, :]
            q2, dy2 = _stack_pair(is_a, q), _stack_pair(is_a, dy)
            sc = _dot_nt(q2, k8)
            dp = _dot_nt(dy2, v)
            a2 = (sc * tab["d2"]).astype(BF16)
            ds2 = (dp * tab["d2"]).astype(BF16)
            dq = _unstack_pair(is_a, _dot_nn(ds2, k8))
            dk = _dot_tn(ds2, q2)
            dv = _dot_tn(a2, dy2)
            prod = sc * dp
            pf, pb = prod * tab["df2"], prod * tab["db2"]
            ifa, ifb = ifa + colsum(pf[0:CHUNK, :]), ifb + colsum(pf[CHUNK:2 * CHUNK, :])
            iba, ibb = iba + colsum(pb[0:CHUNK, :]), ibb + colsum(pb[CHUNK:2 * CHUNK, :])
            states, sgrads = st[n], gr[n]
            sb, gb = states.astype(BF16), sgrads.astype(BF16)
            dqc = _dot_nt(dy, sb) * tab["qdec"]
            dkc = _dot_nt(v, gb) * tab["kdec"]
            dv = dv + _dot_nn(_both_ways(k8f, tab["kdec"]), gb)
            dq_ref[rows, :] = (dq + dqc[:, 0:LANES] + dqc[:, LANES:2 * LANES]).astype(BF16)
            dk_ref[rows, :] = ((dk + dkc[:, 0:LANES] + dkc[:, LANES:2 * LANES]) * Q_SCALE).astype(BF16)
            dv_ref[rows, :] = dv.astype(BF16)
            q2w, k2w = jnp.concatenate([qf, qf], axis=1), jnp.concatenate([k8f, k8f], axis=1)
            xfb = xfb + colsum(tab["qexp"] * q2w * dqc + tab["kexp"] * k2w * dkc)
            prod_s = sgrads * states
            lf, lb = lf + colsum(prod_s[0:LANES, :]), lb + colsum(prod_s[LANES:2 * LANES, :])
            return xfb, ifa, ifb, iba, ibb, lf, lb

        sink_col = _sink_column(sink_ref, g)
        head_row = lax.broadcasted_iota(jnp.int32, dsink_ref.shape, 0)

        def attention_block(blk):
            n = half * blocks_half + blk
            rows = pl.ds(pl.multiple_of(blk * BLOCK, BLOCK), BLOCK)
            keys = pl.ds(pl.multiple_of(n * BLOCK, BLOCK), KEYS)
            qm = _stack_heads(aq_ref[rows, :], g)
            k3, v3 = kpad[keys, :], vpad[keys, :]
            p, p_sink = _attn_probs(qm, k3, bias, sink_col, n, s)
            dom = _stack_heads(do_ref[rows, :], g)
            dp = _dot_nt(dom, v3)
            delta = jnp.sum(p * dp, axis=1, keepdims=True)
            ds_mat = (p * (dp - delta)).astype(BF16)
            daq_ref[rows, :] = _unstack_heads(_dot_nn(ds_mat, k3), g).astype(BF16)
            dk_acc[keys, :] += _dot_tn(ds_mat, qm) * Q_SCALE
            dv_acc[keys, :] += _dot_tn(p.astype(BF16), dom)
            w = p_sink * delta
            upd = jnp.zeros(dsink_ref.shape, F32)
            for h in range(GROUP):
                upd = upd + jnp.where(head_row == h, -jnp.sum(w[h * BLOCK:(h + 1) * BLOCK, :]), 0.0)
            dsink_ref[...] += upd

        dgain = _chunk_loop(n_chunks, norm_body, jnp.zeros((1, LANES), F32))
        _scan_states(n_chunks, gr, BWD_ROWS, tab["lam_b"], FWD_ROWS, tab["lam_f"])

        def trip(i, carry):
            chunk, blk = 0, 0
            for kind in _trip_order(BWD_ORDER, RET_UNROLL, per_trip):
                if kind == "r":
                    carry = grad_body(i * RET_UNROLL + chunk, carry)
                    chunk += 1
                else:
                    attention_block(i * per_trip + blk)
                    blk += 1
            return carry

        z = jnp.zeros((1, LANES), F32)
        init = (jnp.zeros((1, 2 * LANES), F32), z, z, z, z, z, z)
        xfb, ifa, ifb, iba, ibb, lf, lb = lax.fori_loop(0, trips, trip, init)
        st_ref[...] = jnp.zeros_like(st_ref)
        st_ref[ST_GAIN:ST_GAIN + 1, :] = dgain
        st_ref[ST_XF:ST_XF + 1, :] = xfb[:, 0:LANES]
        st_ref[ST_XB:ST_XB + 1, :] = xfb[:, LANES:2 * LANES]
        st_ref[ST_IFA:ST_IFA + 1, :] = ifa
        st_ref[ST_IFB:ST_IFB + 1, :] = ifb
        st_ref[ST_IBA:ST_IBA + 1, :] = iba
        st_ref[ST_IBB:ST_IBB + 1, :] = ibb
        st_ref[ST_LF:ST_LF + 1, :] = lf * (CHUNK * tab["lam_f"])
        st_ref[ST_LB:ST_LB + 1, :] = lb * (CHUNK * tab["lam_b"])

        @pl.when(pair == pairs - 1)
        def _():
            dak_ref[...] = dk_acc[BLOCK:BLOCK + s, :].astype(BF16)
            dav_ref[...] = dv_acc[BLOCK:BLOCK + s, :].astype(BF16)

    lane_blk = lambda c0: _seq_spec(s, c0 // LANES)
    seq0 = _seq_spec(s, 0)
    half_rows = blocks_half * BLOCK
    aq_spec = pl.BlockSpec((half_rows, GW), lambda b, h: (b * PAIRS_PER_KV + (h & 1), C_AQ // GW + h // 2))
    a_spec = pl.BlockSpec((half_rows, GW), lambda b, h: (b * PAIRS_PER_KV + (h & 1), h // 2))
    kv_spec = lambda c0: pl.BlockSpec((s, LANES), lambda b, h: (b, c0 // LANES))
    kv_out = pl.BlockSpec((s, LANES), lambda b, h: (b, 0))
    state = pltpu.VMEM((n_chunks, 2 * LANES, LANES), F32)
    pad = pltpu.VMEM((s + 2 * BLOCK, LANES), BF16)
    acc = pltpu.VMEM((s + 2 * BLOCK, LANES), F32)
    body, lead = _after(after, body)
    return pl.pallas_call(
        body, name="mixers_bwd", grid=(b_loc, pairs),
        in_specs=lead + [_smem_spec(), _smem_spec(), _smem_spec(), lane_blk(C_RQ), lane_blk(C_RK), lane_blk(C_RV),
                         lane_blk(C_RG), seq0, seq0, seq0, pl.BlockSpec((1, LANES), lambda b, h: (0, h)),
                         aq_spec, kv_spec(C_AK), kv_spec(C_AV), a_spec, _state_spec(n_chunks, pairs)],
        out_specs=[seq0] * 4 + [pl.BlockSpec((ST_ROWS, LANES), lambda b, h: (b, h)), a_spec, kv_out, kv_out,
                                pl.BlockSpec((8, LANES), lambda b, h: (b * KV_HEADS + h // 2, 0))],
        out_shape=[SDS((t, RET_W), BF16)] * 4 + [SDS((b_loc * ST_ROWS, RET_W), F32), SDS((t, ATT_W), BF16),
                                                   SDS((t, KV_W), BF16), SDS((t, KV_W), BF16),
                                                   SDS((b_loc * KV_HEADS * 8, LANES), F32)],
        scratch_shapes=[state, pltpu.VMEM((s, LANES), BF16), pad, pad,
                        pltpu.VMEM((GROUP * BLOCK, KEYS), F32), acc, acc],
        compiler_params=_params(("arbitrary", "arbitrary")),
    )(*after, lgf, lgb, sink, u, u, u, u, xhat, rstd, dr, gn_gain, u, u, u, da, states)


def _pack_small(acc2, acc1, ret_stats, dsink, b_loc, d):
    pairs = RET_HEADS // 2

    def body(acc2_ref, acc1_ref, st_ref, dsink_ref, out_ref):
        out_ref[...] = jnp.zeros_like(out_ref)
        out_ref[ROW_LN1G:ROW_LN1G + 1, :] = acc1_ref[0:1, :]
        out_ref[ROW_LN1B:ROW_LN1B + 1, :] = acc1_ref[1:2, :]
        out_ref[ROW_LN2G:ROW_LN2G + 1, :] = acc2_ref[1:2, :]
        out_ref[ROW_LN2B:ROW_LN2B + 1, :] = acc2_ref[2:3, :]
        out_ref[ROW_LOSS:ROW_LOSS + 1, :] = acc2_ref[0:1, :]
        st = st_ref[0:ST_ROWS, :]
        for b in range(1, b_loc):
            st = st + st_ref[b * ST_ROWS:(b + 1) * ST_ROWS, :]
        out_ref[ROW_GN:ROW_GN + 1, 0:RET_W] = st[ST_GAIN:ST_GAIN + 1, :]
        lane = lax.broadcasted_iota(jnp.int32, (1, d), 1)
        misc = jnp.zeros((1, d), F32)
        for pr in range(pairs):
            blk = st[:, pr * LANES:(pr + 1) * LANES]
            half = lax.broadcasted_iota(jnp.int32, (1, LANES), 1) < HEAD_DIM
            for h in range(2):
                sel = half if h == 0 else jnp.logical_not(half)
                cross_f = jnp.sum(jnp.where(sel, blk[ST_XF:ST_XF + 1, :] + blk[ST_LF:ST_LF + 1, :], 0.0))
                cross_b = jnp.sum(jnp.where(sel, blk[ST_XB:ST_XB + 1, :] + blk[ST_LB:ST_LB + 1, :], 0.0))
                intra_f = jnp.sum(blk[ST_IFA + h:ST_IFA + h + 1, :])
                intra_b = jnp.sum(blk[ST_IBA + h:ST_IBA + h + 1, :])
                head = 2 * pr + h
                misc = jnp.where(lane == MISC_DF + head, cross_f + intra_f, misc)
                misc = jnp.where(lane == MISC_DB + head, cross_b + intra_b, misc)
        for g in range(KV_HEADS):
            tot = dsink_ref[g * 8:(g + 1) * 8, :]
            for b in range(1, b_loc):
                tot = tot + dsink_ref[(b * KV_HEADS + g) * 8:(b * KV_HEADS + g + 1) * 8, :]
            for h in range(GROUP):
                misc = jnp.where(lane == MISC_SINK + GROUP * g + h, jnp.sum(tot[h:h + 1, 0:1]), misc)
        out_ref[ROW_MISC:ROW_MISC + 1, :] = misc

    return pl.pallas_call(body, name="pack_small", out_shape=SDS((SMALL_ROWS, d), F32))(acc2, acc1, ret_stats, dsink)


BIG = ("w_in", "w_out", "w_ffn_gate", "w_ffn_up", "w_ffn_down", "w_ple_proj", "w_ple_gate")
TRANSPOSED_OUTSIDE = ("w_in", "w_ffn_gate", "w_ffn_up")
TRANSPOSED_HERE = ("w_ple_proj",)
SMALL = ("ret_decay_fwd", "ret_decay_bwd", "ret_gn_gain", "attn_sink", "ln1_gain", "ln1_bias", "ln2_gain", "ln2_bias")
ORDER = ("w_in", "ret_decay_fwd", "ret_decay_bwd", "ret_gn_gain", "attn_sink", "w_out", "ln1_gain", "ln1_bias",
         "w_ffn_gate", "w_ffn_up", "w_ffn_down", "w_ple_proj", "w_ple_gate", "ln2_gain", "ln2_bias")


GATHER_ORDER = ("w_in", "w_ffn_up", "w_out", "w_ffn_gate", "w_ple_gate", "w_ple_proj", "w_ffn_down")
GATHER_TWO_LEVEL = ("w_in", "w_ffn_up")


def _local_step(x2, p2, target2, fetch, publish, small, b_loc, me):
    d = x2.shape[1]
    lgf, lgb = _log_decay(small["ret_decay_fwd"], small["ret_decay_bwd"])
    lgf1, lgb1, sink1 = lgf.reshape(-1), lgb.reshape(-1), small["attn_sink"].reshape(-1)
    (w_in,) = fetch(("w_in",), ())
    u, xb = _in_proj(x2, w_in)
    passed = fetch.pass_on("w_ffn_up", (xb,))
    r, ret_xhat, ret_rstd, a, ret_states = _mixers_fwd(u, lgf1, lgb1, small["ret_gn_gain"], sink1, b_loc, passed)
    w_out, w_gate, w_up = fetch(("w_out", "w_ffn_gate", "w_ffn_up"), (r, a))
    z1, h1b, dact_dg, dact_du, act = _mix_ln1_ffn_up(
        r, a, x2, w_out, w_gate, w_up, small["ln1_gain"], small["ln1_bias"])
    w_pg, w_pe, w_down = fetch(("w_ple_gate", "w_ple_proj", "w_ffn_down"), (act,))
    dz2, dz2b, dsb, dpleb, dg, dup, acc2 = _ffn_down_ln2_loss(
        act, dact_dg, dact_du, h1b, p2, z1, target2, w_down, w_pg, w_pe,
        small["ln1_gain"], small["ln1_bias"], small["ln2_gain"], small["ln2_bias"])
    own = {}

    def grad(name, parts, rhs, after=()):
        whole, own[name] = _weight_grad("grad_" + name, me, parts, rhs, after)
        return whole

    ffn_jobs = dict(w_ffn_down=(act, dz2b), w_ple_proj=(dpleb, p2), w_ple_gate=(h1b, dsb),
                    w_ffn_gate=(dg, h1b), w_ffn_up=(dup, h1b))
    wholes, owns = _weight_grad_jobs("grad_w_ffn", me, list(ffn_jobs.values()))
    own.update(zip(ffn_jobs, owns))
    t2 = publish("ffn", dict(zip(ffn_jobs, wholes)))
    dz1, dz1b, dr, da, acc1 = _dh1_ln1_bwd(dz2, dg, dup, dsb, z1, w_gate, w_up, w_pg, w_out, small["ln1_gain"], t2)
    t3 = publish("out", dict(w_out=grad("w_out", [r, a], dz1b)))
    dq, dk, dv, dgate, ret_stats, daq, dak, dav, dsink = _mixers_bwd(
        u, ret_xhat, ret_rstd, ret_states, dr, da, lgf1, lgb1, small["ret_gn_gain"], sink1, b_loc, t3)
    parts = [dq, dk, dv, dgate, daq, dak, dav]
    small_part = _pack_small(acc2, acc1, ret_stats, dsink, b_loc, d)
    t4 = publish("in", dict(w_in=grad("w_in", parts, xb)), small_part)
    grad_x = _in_proj_bwd(dz1, parts, w_in, t4)
    return grad_x, own, small_part


def kernel(x, p, w_in, ret_decay_fwd, ret_decay_bwd, ret_gn_gain, attn_sink, w_out, ln1_gain, ln1_bias, w_ffn_gate, w_ffn_up, w_ffn_down, w_ple_proj, w_ple_gate, ln2_gain, ln2_bias, loss_target, m_w_in, m_ret_decay_fwd, m_ret_decay_bwd, m_ret_gn_gain, m_attn_sink, m_w_out, m_ln1_gain, m_ln1_bias, m_w_ffn_gate, m_w_ffn_up, m_w_ffn_down, m_w_ple_proj, m_w_ple_gate, m_ln2_gain, m_ln2_bias, v_w_in, v_ret_decay_fwd, v_ret_decay_bwd, v_ret_gn_gain, v_attn_sink, v_w_out, v_ln1_gain, v_ln1_bias, v_w_ffn_gate, v_w_ffn_up, v_w_ffn_down, v_w_ple_proj, v_w_ple_gate, v_ln2_gain, v_ln2_bias):
    given = dict(locals())

    def strip(n, a):
        if n not in BIG:
            return a
        return a[0].T if n in TRANSPOSED_OUTSIDE else a[0]

    def restore(n, a):
        if n not in BIG:
            return a
        return (a.T if n in TRANSPOSED_OUTSIDE else a)[None]

    w = {n: strip(n, given[n]) for n in ORDER}
    m = {n: strip(n, given["m_" + n]) for n in ORDER}
    v = {n: strip(n, given["v_" + n]) for n in ORDER}
    b_loc, s, d = x.shape
    x2 = x.reshape(b_loc * s, d)
    p2 = p[0].reshape(b_loc * s, p.shape[-1])
    target2 = loss_target.reshape(b_loc * s, d)

    small = {n: w[n] for n in SMALL}
    me = (4 * lax.axis_index("x") + 2 * lax.axis_index("y") + lax.axis_index("c")).astype(jnp.int32).reshape(1)

    gathered = _prep_shards(me, {n: w[n] for n in BIG})
    gather = _split_copy_start(
        "gather_start", [(gathered[n],) for n in GATHER_ORDER],
        [_gather_copy_near if n in GATHER_TWO_LEVEL else _gather_copy for n in GATHER_ORDER])

    passing = {}

    def pass_on(n, after):
        near = _split_copy_wait("gather_wait_" + n + "_near", gather, [GATHER_ORDER.index(n)], list(after))
        passing[n] = _split_copy_start("gather_pass_" + n, near, [_gather_copy_pass])
        return (passing[n]["token"],)

    def fetch(names, after):
        out = {}
        for n in [n for n in names if n in GATHER_TWO_LEVEL]:
            if n not in passing:
                pass_on(n, after)
            out[n] = _split_copy_wait("gather_wait_" + n, passing[n], [0], list(after))[0][0]
        direct = [n for n in names if n not in GATHER_TWO_LEVEL]
        if direct:
            got = _split_copy_wait("gather_wait_" + direct[0], gather, [GATHER_ORDER.index(n) for n in direct],
                                   list(after))
            out.update({n: item[0] for n, item in zip(direct, got)})
        return [out[n] for n in names]

    scatters = []

    def publish(tag, products, small_sums=None):
        items = [(products[n], lax.empty((N_DEV - 1, products[n].shape[0] // N_DEV, products[n].shape[1]), BF16))
                 for n in products]
        copies = [_scatter_copy] * len(items)
        if small_sums is not None:
            items.append((small_sums, lax.empty((N_DEV - 1,) + small_sums.shape, F32)))
            copies.append(_small_copy)
        started = _split_copy_start("scatter_start_" + tag, items, copies)
        scatters.append((list(products), small_sums is not None, started))
        return (started["token"],)

    fetch.pass_on = pass_on
    grad_x, own, small_part = _local_step(x2, p2, target2, fetch, publish, small, b_loc, me)

    out_g, out_d, out_m, out_v = {}, {}, {}, {}
    after = [grad_x]
    for names, with_small, started in scatters:
        landed = _split_copy_wait("scatter_wait_" + names[0], started, list(range(len(started["items"]))), after)
        if with_small:
            loss, sg, sd, sm, sv = _small_adamw(
                me, small_part, landed[-1][1], small, {n: m[n] for n in SMALL}, {n: v[n] for n in SMALL})
            for dst, src in ((out_g, sg), (out_d, sd), (out_m, sm), (out_v, sv)):
                dst.update(src)
        recv = {n: item[1] for n, item in zip(names, landed)}
        alike = {}
        for n in names:
            alike.setdefault((own[n].shape, n in TRANSPOSED_HERE), []).append(n)
        for (_, transposed), ns in alike.items():
            res = _reduce_adamw(ns[0], [own[n] for n in ns], [recv[n] for n in ns], [w[n] for n in ns],
                                [m[n] for n in ns], [v[n] for n in ns], transposed)
            for dst, vals in zip((out_g, out_d, out_m, out_v), res):
                dst.update(zip(ns, vals))
        after = [out_v[names[-1]]]

    outs = [loss[0, 0], grad_x.reshape(x.shape)]
    for group in (out_g, out_d, out_m, out_v):
        outs += [restore(n, group[n]) for n in ORDER]
    return tuple(outs)
```

```python
import functools

import jax
import jax.numpy as jnp
from jax import lax
from jax.experimental import pallas as pl
from jax.experimental.pallas import tpu as pltpu

F32, BF16 = jnp.float32, jnp.bfloat16
SDS = jax.ShapeDtypeStruct
MESH = pl.DeviceIdType.MESH

N_DEV = 8
HEAD_DIM = 64
RET_HEADS = 8
ATTN_HEADS = 8
KV_HEADS = 2
GROUP = ATTN_HEADS // KV_HEADS
RET_W = RET_HEADS * HEAD_DIM
ATT_W = ATTN_HEADS * HEAD_DIM
KV_W = KV_HEADS * HEAD_DIM
LANES = 128
CHUNK = 128
BLOCK = 128
Q_SCALE = HEAD_DIM ** -0.5
ALPHA = 2.0 ** 0.25
LN_EPS = 1e-5
GN_EPS = 1e-5
NEG_INF = -1e30
C_RQ, C_RK, C_RV, C_RG = 0, RET_W, 2 * RET_W, 3 * RET_W
C_AQ = 4 * RET_W
C_AK = C_AQ + ATT_W
C_AV = C_AK + KV_W
IN_W = C_AV + KV_W

ADAM_LR = 0.001
ADAM_B1 = 0.9
ADAM_B2 = 0.999
ADAM_EPS = 1e-08
ADAM_WD = 0.01
ADAM_STEP = 10

VMEM_LIMIT = 56 * 1024 * 1024
MATMUL_ROWS = 512
EPILOGUE_ROWS = 256
SUB_ROWS = 256
SMALL_ROWS = 16
ROW_LN1G, ROW_LN1B, ROW_LN2G, ROW_LN2B, ROW_LOSS, ROW_GN, ROW_MISC = 0, 1, 2, 3, 4, 5, 6
MISC_DF, MISC_DB, MISC_SINK = 0, 8, 16


def _dot_nn(a, b):
    return lax.dot_general(a, b, (((1,), (0,)), ((), ())), preferred_element_type=F32)


def _dot_nt(a, b):
    return lax.dot_general(a, b, (((1,), (1,)), ((), ())), preferred_element_type=F32)


def _dot_tn(a, b):
    return lax.dot_general(a, b, (((0,), (0,)), ((), ())), preferred_element_type=F32)


def _params(sem=None, vmem=VMEM_LIMIT):
    kw = {"vmem_limit_bytes": vmem}
    if sem is not None:
        kw["dimension_semantics"] = sem
    return pltpu.CompilerParams(**kw)


def _row_tile(t, want=512):
    tm = want
    while t % tm:
        tm //= 2
    return tm


def _sigmoid(x):
    return jax.nn.sigmoid(x)


def _layer_norm_stats(z):
    mu = jnp.mean(z, axis=1, keepdims=True)
    d = z - mu
    var = jnp.mean(d * d, axis=1, keepdims=True)
    rstd = lax.rsqrt(var + LN_EPS)
    return d * rstd, rstd


def _layer_norm_bwd(dxh, xhat, rstd):
    m1 = jnp.mean(dxh, axis=1, keepdims=True)
    m2 = jnp.mean(dxh * xhat, axis=1, keepdims=True)
    return rstd * (dxh - m1 - xhat * m2)


def _mesh_pos():
    return lax.axis_index("x"), lax.axis_index("y"), lax.axis_index("c")


HBM_SPEC = pl.BlockSpec(memory_space=pltpu.HBM)
SEM_SPEC = pl.BlockSpec(memory_space=pltpu.SEMAPHORE)
ANY_SPEC = pl.BlockSpec(memory_space=pl.ANY)
SIDE_EFFECT = pltpu.SideEffectType.DATAFLOW_SIDE_EFFECTING
PEER_SEMS = pltpu.SemaphoreType.DMA((N_DEV - 1,))


def _in_hbm(a):
    return pltpu.with_memory_space_constraint(a, pltpu.HBM)


def _split_copy_start(name, items, copies):
    n = len(items)
    flat = [a for it in items for a in it]
    k = len(flat)

    def body(*refs):
        arr, sems = list(refs[:k]), refs[k:k + 2 * n]
        for i, it in enumerate(items):
            mine = [arr.pop(0) for _ in it]
            for m in range(1, N_DEV):
                cp = copies[i](m, mine, sems[i].at[m - 1], sems[n + i].at[m - 1])
                if cp is not None:
                    cp.start()
        token = refs[-1]
        token[...] = jnp.zeros_like(token)

    res = pl.pallas_call(
        body, name=name,
        out_shape=[PEER_SEMS] * (2 * n) + [pltpu.HBM(a.shape, a.dtype) for a in flat] + [SDS((8, LANES), F32)],
        in_specs=[HBM_SPEC] * k,
        out_specs=[SEM_SPEC] * (2 * n) + [HBM_SPEC] * k + [pl.BlockSpec(memory_space=pltpu.VMEM)],
        input_output_aliases={j: 2 * n + j for j in range(k)},
        compiler_params=pltpu.CompilerParams(has_side_effects=SIDE_EFFECT),
    )(*[_in_hbm(a) for a in flat])
    thru, out_items = list(res[2 * n:2 * n + k]), []
    for it in items:
        out_items.append(tuple(thru.pop(0) for _ in it))
    return dict(send=res[:n], recv=res[n:2 * n], items=out_items, token=res[-1], copies=copies)


def _gather_start(shards, copies):
    names = list(shards)
    n = len(names)
    flip = [name in TRANSPOSED_HERE for name in names]
    shapes = [shards[name].shape[::-1] if f else shards[name].shape for name, f in zip(names, flip)]
    most = (max(s[0] for s in shapes), max(s[1] for s in shapes))

    def body(*refs):
        src, sems, land, token = refs[:n], refs[n:3 * n], refs[3 * n:4 * n], refs[4 * n]
        wide, narrow, sem = refs[4 * n + 1:]
        for i, (rows, cols) in enumerate(shapes):
            raw = wide.at[0:cols, 0:rows] if flip[i] else wide.at[0:rows, 0:cols]
            bring = pltpu.make_async_copy(src[i], raw, sem.at[0])
            bring.start()
            bring.wait()
            narrow[0:rows, 0:cols] = (raw[...].T if flip[i] else raw[...]).astype(BF16)
            mine = land[i].at[pl.ds(pl.multiple_of(_peer_index(0) * rows, 8), rows), :]
            place = pltpu.make_async_copy(narrow.at[0:rows, 0:cols], mine, sem.at[0])
            place.start()
            place.wait()
            for m in range(1, N_DEV):
                cp = copies[i](m, [land[i]], sems[i].at[m - 1], sems[n + i].at[m - 1])
                if cp is not None:
                    cp.start()
        token[...] = jnp.zeros_like(token)

    side = max(most)
    res = pl.pallas_call(
        body, name="gather_start",
        out_shape=[PEER_SEMS] * (2 * n) + [pltpu.HBM((N_DEV * r, c), BF16) for r, c in shapes] + [SDS((8, LANES), F32)],
        in_specs=[HBM_SPEC] * n,
        out_specs=[SEM_SPEC] * (2 * n) + [HBM_SPEC] * n + [pl.BlockSpec(memory_space=pltpu.VMEM)],
        scratch_shapes=[pltpu.VMEM((side, side), F32), pltpu.VMEM(most, BF16), pltpu.SemaphoreType.DMA((1,))],
        compiler_params=pltpu.CompilerParams(has_side_effects=SIDE_EFFECT),
    )(*[_in_hbm(shards[name]) for name in names])
    return dict(send=res[:n], recv=res[n:2 * n], items=[(a,) for a in res[2 * n:3 * n]], token=res[-1], copies=copies)


def _split_copy_wait(name, started, which, after):
    items = [started["items"][i] for i in which]
    copies = [started["copies"][i] for i in which]
    n = len(items)
    flat = [a for it in items for a in it]
    k = len(flat)

    def body(*refs):
        arr, sems = list(refs[:k]), refs[k:k + 2 * n]
        for i, it in enumerate(items):
            mine = [arr.pop(0) for _ in it]
            for m in range(1, N_DEV):
                cp = copies[i](m, mine, sems[i].at[m - 1], sems[n + i].at[m - 1])
                if cp is not None:
                    cp.wait_send()
                    cp.wait_recv()

    res = pl.pallas_call(
        body, name=name,
        out_shape=[pltpu.HBM(a.shape, a.dtype) for a in flat],
        in_specs=[HBM_SPEC] * k + [SEM_SPEC] * (2 * n) + [ANY_SPEC] * len(after),
        out_specs=[HBM_SPEC] * k,
        input_output_aliases={j: j for j in range(k)},
        compiler_params=pltpu.CompilerParams(has_side_effects=SIDE_EFFECT),
    )(*flat, *[started["send"][i] for i in which], *[started["recv"][i] for i in which], *[_in_hbm(a) for a in after])
    thru, out_items = list(res), []
    for it in items:
        out_items.append(tuple(thru.pop(0) for _ in it))
    return out_items


def _gather_copy(m, refs, send_sem, recv_sem):
    (land_ref,) = refs
    r = land_ref.shape[0] // N_DEV
    mine = land_ref.at[pl.ds(pl.multiple_of(_peer_index(0) * r, 8), r), :]
    return pltpu.make_async_remote_copy(src_ref=mine, dst_ref=mine, send_sem=send_sem, recv_sem=recv_sem,
                                        device_id=_peer(m), device_id_type=MESH)


def _gather_copy_near(m, refs, send_sem, recv_sem):
    return _gather_copy(m, refs, send_sem, recv_sem) if m == 1 or m % 2 == 0 else None


def _gather_copy_pass(m, refs, send_sem, recv_sem):
    if m == 1 or m % 2 == 0:
        return None
    (land_ref,) = refs
    r = land_ref.shape[0] // N_DEV
    block = land_ref.at[pl.ds(pl.multiple_of(_peer_index(m ^ 1) * r, 8), r), :]
    return pltpu.make_async_remote_copy(src_ref=block, dst_ref=block, send_sem=send_sem, recv_sem=recv_sem,
                                        device_id=_peer(1), device_id_type=MESH)


def _small_copy(m, refs, send_sem, recv_sem):
    part_ref, land_ref = refs
    return pltpu.make_async_remote_copy(src_ref=part_ref, dst_ref=land_ref.at[m - 1], send_sem=send_sem,
                                        recv_sem=recv_sem, device_id=_peer(m), device_id_type=MESH)


def _scatter_copy(m, refs, send_sem, recv_sem):
    buf_ref, land_ref = refs
    r = buf_ref.shape[0] // N_DEV
    src = buf_ref.at[pl.ds(pl.multiple_of(_peer_index(m) * r, 8), r), :]
    return pltpu.make_async_remote_copy(src_ref=src, dst_ref=land_ref.at[m - 1], send_sem=send_sem,
                                        recv_sem=recv_sem, device_id=_peer(m), device_id_type=MESH)


def _peer(m):
    x, y, c = _mesh_pos()
    bx, by, bc = (m >> 2) & 1, (m >> 1) & 1, m & 1
    return (x ^ bx if bx else x, y ^ by if by else y, c ^ bc if bc else c)


def _peer_index(m):
    x, y, c = _mesh_pos()
    return (4 * x + 2 * y + c) ^ m


SMALL_PLACE = {
    "ln1_gain": (ROW_LN1G, 0), "ln1_bias": (ROW_LN1B, 0), "ln2_gain": (ROW_LN2G, 0), "ln2_bias": (ROW_LN2B, 0),
    "ret_gn_gain": (ROW_GN, 0), "ret_decay_fwd": (ROW_MISC, MISC_DF), "ret_decay_bwd": (ROW_MISC, MISC_DB),
    "attn_sink": (ROW_MISC, MISC_SINK)}


def _small_adamw(me, part, landed, w, m, v):
    d = part.shape[1]
    names = list(SMALL_PLACE)
    k = len(names)

    def body(*refs):
        me_ref, part_ref, land_ref = refs[:3]
        refs = refs[2:]
        w_refs, m_refs, v_refs = refs[1:1 + k], refs[1 + k:1 + 2 * k], refs[1 + 2 * k:1 + 3 * k]
        outs = refs[1 + 3 * k:1 + 7 * k + 1]
        tot_ref = refs[-1]
        loss_ref, g_refs, dl_refs = outs[0], outs[1:1 + k], outs[1 + k:1 + 2 * k]
        nm_refs, nv_refs = outs[1 + 2 * k:1 + 3 * k], outs[1 + 3 * k:1 + 4 * k]
        tot = jnp.zeros(part_ref.shape, F32)
        for dev in range(N_DEV):
            j = dev ^ me_ref[0]
            tot = tot + jnp.where(j == 0, part_ref[...], land_ref[jnp.maximum(j, 1) - 1])
        tot_ref[...] = tot
        loss_ref[...] = (0.5 / d) * jnp.sum(tot_ref[ROW_LOSS:ROW_LOSS + 1, :], axis=1, keepdims=True)
        for i, name in enumerate(names):
            row, lo = SMALL_PLACE[name]
            wv = w_refs[i][...]
            g = tot_ref[row:row + 1, lo:lo + wv.shape[1]]
            if name.startswith("ret_decay"):
                p2 = jnp.exp2(wv)
                g = g * (-p2 * jnp.log(2.0) / (1.0 - p2))
            g_refs[i][...] = g
            _adamw_store(g, wv, m_refs[i][...], v_refs[i][...], dl_refs[i], nm_refs[i], nv_refs[i])

    shapes = [SDS(w[n].shape, F32) for n in names]
    vm = pl.BlockSpec(memory_space=pltpu.VMEM)
    res = pl.pallas_call(
        body, name="small_adamw", out_shape=[SDS((1, 1), F32)] + shapes * 4,
        in_specs=[_smem_spec()] + [vm] * (2 + 3 * k), out_specs=[vm] * (1 + 4 * k),
        scratch_shapes=[pltpu.VMEM(part.shape, F32)],
    )(me, part, landed, *[w[n] for n in names], *[m[n] for n in names], *[v[n] for n in names])
    groups = [dict(zip(names, res[1 + j * k:1 + (j + 1) * k])) for j in range(4)]
    return (res[0], *groups)


def _adamw_store(g, w, m, v, dl_ref, nm_ref, nv_ref):
    m = ADAM_B1 * m + (1.0 - ADAM_B1) * g
    v = ADAM_B2 * v + (1.0 - ADAM_B2) * (g * g)
    m_hat = m / (1.0 - ADAM_B1 ** ADAM_STEP)
    v_hat = v / (1.0 - ADAM_B2 ** ADAM_STEP)
    dl_ref[...] = -ADAM_LR * (m_hat / (jnp.sqrt(v_hat) + ADAM_EPS) + ADAM_WD * w)
    nm_ref[...] = m
    nv_ref[...] = v


def _reduce_adamw(name, owns, recvs, ws, ms, vs, transposed):
    count = len(owns)
    rows, n = owns[0].shape
    steps = 1 if transposed or rows % 32 else 4
    rb = rows // steps

    def body(*refs):
        ins, outs = refs[:5 * count], refs[5 * count:]
        j = pl.program_id(0)
        for k in range(count):
            @pl.when(j == k)
            def _(k=k):
                own_ref, recv_ref, w_ref, m_ref, v_ref = ins[5 * k:5 * k + 5]
                g_ref, dl_ref, nm_ref, nv_ref = outs[4 * k:4 * k + 4]
                g = own_ref[...]
                for p in range(N_DEV - 1):
                    g = g + recv_ref[p].astype(F32)
                if transposed:
                    g = g.T
                g_ref[...] = g
                _adamw_store(g, w_ref[...], m_ref[...], v_ref[...], dl_ref, nm_ref, nv_ref)

    def turn(k):
        return lambda j, i: jnp.where(j == k, i, jnp.where(j < k, 0, steps - 1))

    in_specs, out_specs = [], []
    for k in range(count):
        at = turn(k)
        blk = pl.BlockSpec(ws[0].shape if transposed else (rb, n), lambda j, i, at=at: (at(j, i), 0))
        in_specs += [pl.BlockSpec((rb, n), lambda j, i, at=at: (at(j, i), 0)),
                     pl.BlockSpec((N_DEV - 1, rb, n), lambda j, i, at=at: (0, at(j, i), 0)), blk, blk, blk]
        out_specs += [blk] * 4
    res = pl.pallas_call(
        body, name="adamw_" + name, grid=(count, steps), in_specs=in_specs, out_specs=out_specs,
        out_shape=[SDS(ws[0].shape, F32)] * (4 * count), compiler_params=_params(("arbitrary", "arbitrary")),
    )(*[a for k in range(count) for a in (owns[k], recvs[k], ws[k], ms[k], vs[k])])
    return [list(res[j::4]) for j in range(4)]


def _row_spec(tm, width):
    return pl.BlockSpec((tm, width), lambda i: (i, 0))


def _full_spec(shape):
    return pl.BlockSpec(shape, lambda i: (0,) * len(shape))


_acc_spec = _full_spec


def _sub_rows(tm):
    step = min(SUB_ROWS, tm)
    return [(lo, lo + step) for lo in range(0, tm, step)]


def _in_proj(x2, wt_in):
    t, d = x2.shape
    u_w = wt_in.shape[0]
    tm = _row_tile(t, MATMUL_ROWS)

    def body(x_ref, w_ref, u_ref, xb_ref):
        xb = x_ref[...].astype(BF16)
        xb_ref[...] = xb
        u_ref[...] = _dot_nt(xb, w_ref[...]).astype(BF16)

    return pl.pallas_call(
        body, name="in_proj", grid=(t // tm,),
        in_specs=[_row_spec(tm, d), _full_spec(wt_in.shape)],
        out_specs=[_row_spec(tm, u_w), _row_spec(tm, d)],
        out_shape=[SDS((t, u_w), BF16), SDS((t, d), BF16)],
        compiler_params=_params(("parallel",)),
    )(x2, wt_in)


def _col_halves(f):
    n = f // LANES
    k = (n + 1) // 2 * LANES
    return [(0, k), (k, f)] if k < f else [(0, f)]


def _mix_ln1_ffn_up(r, a, x2, w_out, wt_gate, wt_up, g1, b1):
    t, d = x2.shape
    f = wt_gate.shape[0]
    tm = _row_tile(t, EPILOGUE_ROWS)

    def body(r_ref, a_ref, x_ref, wo_ref, wg_ref, wu_ref, g_ref, b_ref, xh_ref, rs_ref, hb_ref, dg_ref, du_ref,
             act_ref):
        mix = _dot_nn(r_ref[...], wo_ref[0:RET_W, :]) + _dot_nn(a_ref[...], wo_ref[RET_W:RET_W + ATT_W, :])
        z = ALPHA * x_ref[...] + mix
        xhat, rstd = _layer_norm_stats(z)
        xh_ref[...] = xhat
        rs_ref[...] = jnp.broadcast_to(rstd, rs_ref.shape)
        h = (xhat * g_ref[...] + b_ref[...]).astype(BF16)
        hb_ref[...] = h
        g = _dot_nt(h, wg_ref[...])
        u = _dot_nt(h, wu_ref[...])
        sg = _sigmoid(g)
        silu = g * sg
        dg_ref[...] = (u * (sg * (1.0 + g * (1.0 - sg)))).astype(BF16)
        du_ref[...] = silu.astype(BF16)
        act_ref[...] = (silu * u).astype(BF16)

    wide, narrow = _row_spec(tm, f), _row_spec(tm, d)
    return pl.pallas_call(
        body, name="mix_ln1_ffn_up", grid=(t // tm,),
        in_specs=[_row_spec(tm, RET_W), _row_spec(tm, ATT_W), narrow, _resident_spec(w_out.shape),
                  _resident_spec(wt_gate.shape), _resident_spec(wt_up.shape), _full_spec(g1.shape),
                  _full_spec(b1.shape)],
        out_specs=[narrow, _row_spec(tm, LANES), narrow, wide, wide, wide],
        out_shape=[SDS((t, d), F32), SDS((t, LANES), F32), SDS((t, d), BF16)] + [SDS((t, f), BF16)] * 3,
        compiler_params=_params(("parallel",)),
    )(r, a, x2, w_out, wt_gate, wt_up, g1, b1)


def _ffn_down_ln2_loss(act, dact_dg, dact_du, h1b, p2, xhat1, target, w_down, w_pg, wt_pe, g1, b1, g2, b2):
    t, d = xhat1.shape
    f = act.shape[1]
    pdim = p2.shape[1]
    tm = _row_tile(t, EPILOGUE_ROWS)

    def body(act_ref, fg_ref, fu_ref, hb_ref, p_ref, xh1_ref, tgt_ref, wd_ref, wpg_ref, wpe_ref, g1_ref, b1_ref,
             g2_ref, b2_ref, dz_ref, dzb_ref, ds_ref, dple_ref, dg_ref, du_ref, acc_ref):
        @pl.when(pl.program_id(0) == 0)
        def _():
            acc_ref[...] = jnp.zeros_like(acc_ref)

        for lo, hi in _sub_rows(tm):
            h1 = xh1_ref[lo:hi, :] * g1_ref[...] + b1_ref[...]
            pg = _sigmoid(_dot_nn(hb_ref[lo:hi, :], wpg_ref[...]))
            ple = _dot_nt(p_ref[lo:hi, :].astype(BF16), wpe_ref[...])
            gated = pg * ple
            dgate = gated * (1.0 - pg)
            ffn = _dot_nn(act_ref[lo:hi, :], wd_ref[...])
            z2 = ALPHA * h1 + gated + ffn
            xhat2, rstd2 = _layer_norm_stats(z2)
            err = xhat2 * g2_ref[...] + b2_ref[...] - tgt_ref[lo:hi, :]
            dy = err * (1.0 / d)
            dz = _layer_norm_bwd(dy * g2_ref[...], xhat2, rstd2)
            dzb = dz.astype(BF16)
            dz_ref[lo:hi, :] = dz
            dzb_ref[lo:hi, :] = dzb
            ds_ref[lo:hi, :] = (dz * dgate).astype(BF16)
            dple_ref[lo:hi, :] = (dz * pg).astype(BF16)
            acc_ref[0:1, :] += jnp.sum(err * err, axis=0, keepdims=True)
            acc_ref[1:2, :] += jnp.sum(dy * xhat2, axis=0, keepdims=True)
            acc_ref[2:3, :] += jnp.sum(dy, axis=0, keepdims=True)
            for c0, c1 in _col_halves(f):
                da = _dot_nt(dzb, wd_ref[c0:c1, :])
                dg_ref[lo:hi, c0:c1] = (da * fg_ref[lo:hi, c0:c1].astype(F32)).astype(BF16)
                du_ref[lo:hi, c0:c1] = (da * fu_ref[lo:hi, c0:c1].astype(F32)).astype(BF16)

    vec = _full_spec(g1.shape)
    wide, narrow = _row_spec(tm, f), _row_spec(tm, d)
    return pl.pallas_call(
        body, name="ffn_down_ln2_loss", grid=(t // tm,),
        in_specs=[wide, wide, wide, narrow, _row_spec(tm, pdim), narrow, narrow,
                  _full_spec(w_down.shape), _full_spec(w_pg.shape), _full_spec(wt_pe.shape), vec, vec, vec, vec],
        out_specs=[narrow] * 4 + [wide, wide, _acc_spec((8, d))],
        out_shape=[SDS((t, d), F32), SDS((t, d), BF16), SDS((t, d), BF16), SDS((t, d), BF16),
                   SDS((t, f), BF16), SDS((t, f), BF16), SDS((8, d), F32)],
        compiler_params=_params(("arbitrary",)),
    )(act, dact_dg, dact_du, h1b, p2, xhat1, target, w_down, w_pg, wt_pe, g1, b1, g2, b2)


def _after(after, body):
    k = len(after)
    return (lambda *refs: body(*refs[k:])), [ANY_SPEC] * k


def _resident_spec(shape):
    return pl.BlockSpec(shape, lambda i: (0,) * len(shape), pipeline_mode=pl.Buffered(1))


def _dh1_ln1_bwd(dz2, dg, dup, dsb, xhat1, rstd1, wt_gate, wt_up, w_pg, w_out, g1, after=()):
    t, d = dz2.shape
    f = dg.shape[1]
    tm = _row_tile(t, EPILOGUE_ROWS)

    def body(dz_ref, dg_ref, du_ref, ds_ref, xh1_ref, rs1_ref, wg_ref, wu_ref, wpg_ref, wo_ref, g1_ref,
             dz1_ref, dz1b_ref, dr_ref, da_ref, acc_ref):
        @pl.when(pl.program_id(0) == 0)
        def _():
            acc_ref[...] = jnp.zeros_like(acc_ref)

        for lo, hi in _sub_rows(tm):
            dh = (ALPHA * dz_ref[lo:hi, :] + _dot_nn(dg_ref[lo:hi, :], wg_ref[...])
                  + _dot_nn(du_ref[lo:hi, :], wu_ref[...]) + _dot_nt(ds_ref[lo:hi, :], wpg_ref[...]))
            xhat, rstd = xh1_ref[lo:hi, :], rs1_ref[lo:hi, 0:1]
            dz1 = _layer_norm_bwd(dh * g1_ref[...], xhat, rstd)
            dz1b = dz1.astype(BF16)
            dz1_ref[lo:hi, :] = dz1
            dz1b_ref[lo:hi, :] = dz1b
            acc_ref[0:1, :] += jnp.sum(dh * xhat, axis=0, keepdims=True)
            acc_ref[1:2, :] += jnp.sum(dh, axis=0, keepdims=True)
            dr_ref[lo:hi, :] = _dot_nt(dz1b, wo_ref[0:RET_W, :]).astype(BF16)
            da_ref[lo:hi, :] = _dot_nt(dz1b, wo_ref[RET_W:RET_W + ATT_W, :]).astype(BF16)

    body, lead = _after(after, body)
    return pl.pallas_call(
        body, name="dh1_ln1_bwd", grid=(t // tm,),
        in_specs=lead + [_row_spec(tm, d), _row_spec(tm, f), _row_spec(tm, f), _row_spec(tm, d), _row_spec(tm, d),
                         _row_spec(tm, LANES), _resident_spec(wt_gate.shape), _resident_spec(wt_up.shape), _resident_spec(w_pg.shape),
                         _resident_spec(w_out.shape), _full_spec(g1.shape)],
        out_specs=[_row_spec(tm, d), _row_spec(tm, d), _row_spec(tm, RET_W), _row_spec(tm, ATT_W), _acc_spec((8, d))],
        out_shape=[SDS((t, d), F32), SDS((t, d), BF16), SDS((t, RET_W), BF16), SDS((t, ATT_W), BF16),
                   SDS((8, d), F32)],
        compiler_params=_params(("arbitrary",)),
    )(*after, dz2, dg, dup, dsb, xhat1, rstd1, wt_gate, wt_up, w_pg, w_out, g1)


def _in_proj_bwd(dz1, parts, wt_in, after=()):
    t, d = dz1.shape
    tm = _row_tile(t, MATMUL_ROWS)
    widths = [p.shape[1] for p in parts]

    def body(*refs):
        dz_ref, part_refs, w_ref, dx_ref = refs[0], refs[1:1 + len(parts)], refs[-2], refs[-1]
        acc = ALPHA * dz_ref[...]
        lo = 0
        for p_ref, w in zip(part_refs, widths):
            acc = acc + _dot_nn(p_ref[...], w_ref[lo:lo + w, :])
            lo += w
        dx_ref[...] = acc

    body, lead = _after(after, body)
    return pl.pallas_call(
        body, name="in_proj_bwd", grid=(t // tm,),
        in_specs=lead + [_row_spec(tm, d)] + [_row_spec(tm, w) for w in widths] + [_full_spec(wt_in.shape)],
        out_specs=_row_spec(tm, d), out_shape=SDS((t, d), F32),
        compiler_params=_params(("parallel",)),
    )(*after, dz1, *parts, wt_in)


def _weight_grad(name, me, parts, rhs, after=()):
    t, n = rhs.shape
    widths = [p.shape[1] for p in parts]
    rows = sum(widths)
    own_rows = rows // N_DEV
    tk = _row_tile(t, MATMUL_ROWS)
    n_steps = t // tk
    step = 256

    def body(*refs):
        me_ref, part_refs, rhs_ref = refs[0], refs[1:1 + len(parts)], refs[1 + len(parts)]
        full_ref, own_ref, acc = refs[-3], refs[-2], refs[-1]
        i = pl.program_id(0)

        def products(first):
            b = rhs_ref[...].astype(BF16)
            lo = 0
            for p_ref, w in zip(part_refs, widths):
                for c0 in range(0, w, step):
                    c1 = min(c0 + step, w)
                    val = _dot_tn(p_ref[:, c0:c1].astype(BF16), b)
                    if first:
                        acc[lo + c0:lo + c1, :] = val
                    else:
                        acc[lo + c0:lo + c1, :] += val
                lo += w

        pl.when(i == 0)(functools.partial(products, True))
        pl.when(i > 0)(functools.partial(products, False))

        @pl.when(i == n_steps - 1)
        def _():
            full_ref[...] = acc[...].astype(BF16)
            own_ref[...] = acc[pl.ds(pl.multiple_of(me_ref[0] * own_rows, 8), own_rows), :]

    body, lead = _after(after, body)
    return pl.pallas_call(
        body, name=name, grid=(n_steps,),
        in_specs=lead + [_smem_spec()] + [_row_spec(tk, w) for w in widths] + [_row_spec(tk, n)],
        out_specs=[_full_spec((rows, n)), _full_spec((own_rows, n))],
        out_shape=[SDS((rows, n), BF16), SDS((own_rows, n), F32)],
        scratch_shapes=[pltpu.VMEM((rows, n), F32)],
        compiler_params=_params(("arbitrary",)),
    )(*after, me, *parts, rhs)


def _weight_grad_jobs(name, me, jobs, after=()):
    count = len(jobs)
    t = jobs[0][0].shape[0]
    tk = _row_tile(t, MATMUL_ROWS)
    n_steps = t // tk
    shapes = [(lhs.shape[1], rhs.shape[1]) for lhs, rhs in jobs]
    most_rows, most_cols = max(r for r, _ in shapes), max(n for _, n in shapes)
    step = 256

    def body(*refs):
        me_ref, lhs_refs, rhs_refs = refs[0], refs[1:1 + count], refs[1 + count:1 + 2 * count]
        full_refs, own_refs = refs[1 + 2 * count:1 + 3 * count], refs[1 + 3 * count:1 + 4 * count]
        acc, whole, mine, sems = refs[1 + 4 * count:]
        job, i = pl.program_id(0), pl.program_id(1)

        def leaving(j):
            rows, n = shapes[j]
            return (pltpu.make_async_copy(whole.at[0:rows, 0:n], full_refs[j], sems.at[0]),
                    pltpu.make_async_copy(mine.at[0:rows // N_DEV, 0:n], own_refs[j], sems.at[1]))

        def products(j, first):
            rows, n = shapes[j]
            b = rhs_refs[j][...].astype(BF16)
            for c0 in range(0, rows, step):
                c1 = min(c0 + step, rows)
                val = _dot_tn(lhs_refs[j][:, c0:c1].astype(BF16), b)
                if first:
                    acc[c0:c1, 0:n] = val
                else:
                    acc[c0:c1, 0:n] += val

        def finish(j):
            rows, n = shapes[j]
            own_rows = rows // N_DEV
            if j > 0:
                for cp in leaving(j - 1):
                    cp.wait()
            whole[0:rows, 0:n] = acc[0:rows, 0:n].astype(BF16)
            mine[0:own_rows, 0:n] = acc[pl.ds(pl.multiple_of(me_ref[0] * own_rows, 8), own_rows), 0:n]
            for cp in leaving(j):
                cp.start()
            if j == count - 1:
                for cp in leaving(j):
                    cp.wait()

        for j in range(count):
            pl.when((job == j) & (i == 0))(functools.partial(products, j, True))
            pl.when((job == j) & (i > 0))(functools.partial(products, j, False))
            pl.when((job == j) & (i == n_steps - 1))(functools.partial(finish, j))

    def turn(j):
        return lambda job, i: (jnp.where(job == j, i, jnp.where(job < j, 0, n_steps - 1)), 0)

    body, lead = _after(after, body)
    res = pl.pallas_call(
        body, name=name, grid=(count, n_steps),
        in_specs=lead + [_smem_spec()] + [pl.BlockSpec((tk, rows), turn(j)) for j, (rows, _) in enumerate(shapes)]
        + [pl.BlockSpec((tk, n), turn(j)) for j, (_, n) in enumerate(shapes)],
        out_specs=[ANY_SPEC] * (2 * count),
        out_shape=[SDS((rows, n), BF16) for rows, n in shapes] + [SDS((rows // N_DEV, n), F32) for rows, n in shapes],
        scratch_shapes=[pltpu.VMEM((most_rows, most_cols), F32), pltpu.VMEM((most_rows, most_cols), BF16),
                        pltpu.VMEM((most_rows // N_DEV, most_cols), F32), pltpu.SemaphoreType.DMA((2,))],
        compiler_params=_params(("arbitrary", "arbitrary")),
    )(*after, me, *[lhs for lhs, _ in jobs], *[rhs for _, rhs in jobs])
    return list(res[:count]), list(res[count:])


def _log_decay(decay_f, decay_b):
    def body(f_ref, b_ref, lf_ref, lb_ref):
        lf_ref[...] = jnp.log1p(-jnp.exp2(f_ref[...]))
        lb_ref[...] = jnp.log1p(-jnp.exp2(b_ref[...]))

    return pl.pallas_call(body, name="log_decay", out_shape=[SDS(decay_f.shape, F32)] * 2)(decay_f, decay_b)


def _chunk(ref, n):
    return ref[pl.ds(pl.multiple_of(n * CHUNK, CHUNK), CHUNK), :]


def _group_sum(is_a, v):
    sa = jnp.sum(jnp.where(is_a, v, 0.0), axis=1, keepdims=True)
    sb = jnp.sum(jnp.where(is_a, 0.0, v), axis=1, keepdims=True)
    return jnp.where(is_a, sa, sb)


def _seq_spec(s, col_block):
    return pl.BlockSpec((s, LANES), lambda b, h: (b, col_block + h))


def _smem_spec():
    return pl.BlockSpec(memory_space=pltpu.SMEM)


RET_UNROLL = 4


def _chunk_loop(n_chunks, body, init):
    u = RET_UNROLL if n_chunks % RET_UNROLL == 0 else 1

    def trip(i, carry):
        for j in range(u):
            carry = body(i * u + j, carry)
        return carry

    return lax.fori_loop(0, n_chunks // u, trip, init)


def _stacked_tables(lgf_ref, lgb_ref, pair):
    lane = lax.broadcasted_iota(jnp.int32, (1, LANES), 1)
    is_a = lane < HEAD_DIM
    lgf = jnp.where(is_a, lgf_ref[2 * pair], lgf_ref[2 * pair + 1])
    lgb = jnp.where(is_a, lgb_ref[2 * pair], lgb_ref[2 * pair + 1])
    row = lax.broadcasted_iota(jnp.int32, (CHUNK, 1), 0).astype(F32)
    kdec_f, qdec_f = jnp.exp(lgf * (CHUNK - 1.0 - row)), jnp.exp(lgf * (row + 1.0))
    kdec_b, qdec_b = jnp.exp(lgb * row), jnp.exp(lgb * (CHUNK - row))
    tab = dict(
        is_a=is_a, row=row, lam_f=jnp.exp(lgf * CHUNK), lam_b=jnp.exp(lgb * CHUNK),
        kdec=jnp.concatenate([kdec_f, kdec_b], axis=1), qdec=jnp.concatenate([qdec_f, qdec_b], axis=1),
        qexp=jnp.concatenate([jnp.broadcast_to(row + 1.0, (CHUNK, LANES)),
                              jnp.broadcast_to(CHUNK - row, (CHUNK, LANES))], axis=1),
        kexp=jnp.concatenate([jnp.broadcast_to(CHUNK - 1.0 - row, (CHUNK, LANES)),
                              jnp.broadcast_to(row, (CHUNK, LANES))], axis=1),
    )
    r = lax.broadcasted_iota(jnp.int32, (2 * LANES, LANES), 0)
    c = lax.broadcasted_iota(jnp.int32, (2 * LANES, LANES), 1)
    tab["diag2"] = ((r & (LANES - 1)) < HEAD_DIM) == (c < HEAD_DIM)
    i2 = lax.broadcasted_iota(jnp.int32, (2 * CHUNK, CHUNK), 0)
    j = lax.broadcasted_iota(jnp.int32, (2 * CHUNK, CHUNK), 1)
    head_b = i2 >= CHUNK
    diff = ((i2 & (CHUNK - 1)) - j).astype(F32)
    up, dn = jnp.maximum(diff, 0.0), jnp.maximum(-diff, 0.0)
    lgf2 = jnp.where(head_b, lgf_ref[2 * pair + 1], lgf_ref[2 * pair])
    lgb2 = jnp.where(head_b, lgb_ref[2 * pair + 1], lgb_ref[2 * pair])
    ef = jnp.where(diff >= 0, jnp.exp(lgf2 * up), 0.0)
    eb = jnp.where(diff <= 0, jnp.exp(lgb2 * dn), 0.0)
    tab["d2"] = ef + eb
    tab["df2"] = ef * up
    tab["db2"] = eb * dn
    return tab


def _stack_pair(is_a, x):
    zero = jnp.zeros_like(x)
    return jnp.concatenate([jnp.where(is_a, x, zero), jnp.where(is_a, zero, x)], axis=0)


def _unstack_pair(is_a, x2):
    return jnp.where(is_a, x2[0:CHUNK, :], x2[CHUNK:2 * CHUNK, :])


def _both_ways(x, dec):
    return (jnp.concatenate([x, x], axis=1) * dec).astype(BF16)


def _scan_states(n_chunks, st, up_rows, up_lam, down_rows, down_lam):
    zero = jnp.zeros((LANES, LANES), F32)

    def up(n, r):
        new = st[n, up_rows, :]
        st[n, up_rows, :] = r
        return r * up_lam + new

    def down(s, r):
        n = n_chunks - 1 - s
        new = st[n, down_rows, :]
        st[n, down_rows, :] = r
        return r * down_lam + new

    lax.fori_loop(0, n_chunks, up, zero)
    lax.fori_loop(0, n_chunks, down, zero)


FWD_ROWS, BWD_ROWS = pl.ds(0, LANES), pl.ds(LANES, LANES)


def _state_spec(n_chunks, pairs):
    return pl.BlockSpec((n_chunks, 2 * LANES, LANES), lambda b, h: (b * pairs + h, 0, 0))


ST_GAIN, ST_XF, ST_XB, ST_IFA, ST_IFB, ST_IBA, ST_IBB, ST_LF, ST_LB = 0, 1, 2, 3, 4, 5, 6, 8, 9
ST_ROWS = 16


GW = GROUP * HEAD_DIM
KEYS = 3 * BLOCK


def _attn_tables(g, bias_ref):
    r = lax.broadcasted_iota(jnp.int32, (GROUP * BLOCK, KEYS), 0)
    kj = lax.broadcasted_iota(jnp.int32, (GROUP * BLOCK, KEYS), 1)
    qi = r & (BLOCK - 1)
    hh = lax.shift_right_logical(r, 7)
    dist = jnp.abs(kj - BLOCK - qi)
    slope = jnp.exp2(-(GROUP * g + hh + 1).astype(F32) * (8.0 / ATTN_HEADS))
    bias_ref[...] = jnp.where(dist <= BLOCK, -slope * dist.astype(F32), NEG_INF)


def _own_lanes(g):
    return lax.shift_right_logical(lax.broadcasted_iota(jnp.int32, (1, LANES), 1), 6) == g


def _mask_keys(x_ref, g, scale, pad_ref, s):
    pad_ref[0:BLOCK, :] = jnp.zeros((BLOCK, LANES), BF16)
    pad_ref[BLOCK + s:2 * BLOCK + s, :] = jnp.zeros((BLOCK, LANES), BF16)
    pad_ref[BLOCK:BLOCK + s, :] = jnp.where(_own_lanes(g), x_ref[...].astype(F32) * scale, 0.0).astype(BF16)


def _lane_block(x, j):
    return x[:, j * LANES:(j + 1) * LANES]


def _stack_heads(x, g):
    assert GROUP == 4 and GW == 2 * LANES
    x1 = pltpu.roll(x, HEAD_DIM, 1)
    keep = _own_lanes(g)
    zero = jnp.zeros((BLOCK, LANES), x.dtype)
    rows = []
    for h in range(GROUP):
        for_g0 = _lane_block(x, h // 2) if h % 2 == 0 else _lane_block(x1, ((h + 1) // 2) % 2)
        for_g1 = _lane_block(x, h // 2) if h % 2 == 1 else _lane_block(x1, h // 2)
        rows.append(jnp.where(keep, jnp.where(g == 0, for_g0, for_g1), zero))
    return jnp.concatenate(rows, axis=0)


def _unstack_heads(x4, g):
    p = [x4[h * BLOCK:(h + 1) * BLOCK, :] for h in range(GROUP)]
    cat = lambda a, b: jnp.concatenate([a, b], axis=1)
    in_place = jnp.where(g == 0, cat(p[0], p[2]), cat(p[1], p[3]))
    one_left = jnp.where(g == 0, cat(p[1], p[3]), cat(p[2], p[0]))
    return in_place + pltpu.roll(one_left, HEAD_DIM, 1)


def _sink_column(sink_ref, g):
    rh = lax.shift_right_logical(lax.broadcasted_iota(jnp.int32, (GROUP * BLOCK, 1), 0), 7)
    col = jnp.zeros((GROUP * BLOCK, 1), F32)
    for h in range(GROUP):
        col = jnp.where(rh == h, sink_ref[GROUP * g + h], col)
    return col


def _attn_probs(qm, k3, bias_ref, sink_col, n, s):
    logits = _dot_nt(qm, k3) + bias_ref[...]
    kpos = n * BLOCK - BLOCK + lax.broadcasted_iota(jnp.int32, (1, KEYS), 1)
    logits = jnp.where((kpos >= 0) & (kpos < s), logits, NEG_INF)
    m = jnp.maximum(jnp.max(logits, axis=1, keepdims=True), sink_col)
    e = jnp.exp(logits - m)
    e_sink = jnp.exp(sink_col - m)
    inv = 1.0 / (jnp.sum(e, axis=1, keepdims=True) + e_sink)
    return e * inv, e_sink * inv


PAIRS_PER_KV = (RET_HEADS // 2) // KV_HEADS
FWD_ORDER = "rrarra"
BWD_ORDER = "rararr"


def _trip_order(order, chunks, blocks):
    if order.count("r") == chunks and order.count("a") == blocks:
        return order
    return "r" * chunks + "a" * blocks


def _mixers_fwd(u, lgf, lgb, gn_gain, sink, b_loc, after=()):
    t = u.shape[0]
    s = t // b_loc
    n_chunks = s // CHUNK
    pairs = RET_HEADS // 2
    trips = n_chunks // RET_UNROLL
    blocks_half = (s // BLOCK) // PAIRS_PER_KV
    per_trip = blocks_half // trips
    assert n_chunks % RET_UNROLL == 0 and blocks_half % trips == 0 and PAIRS_PER_KV == 2

    def body(lgf_ref, lgb_ref, sink_ref, q_ref, k_ref, v_ref, g_ref, gain_ref, aq_ref, ak_ref, av_ref,
             r_ref, xhat_ref, rstd_ref, a_ref, st, kpad, vpad, bias):
        pair = pl.program_id(1)
        g, half = lax.shift_right_logical(pair, 1), pair & 1
        tab = _stacked_tables(lgf_ref, lgb_ref, pair)
        is_a = tab["is_a"]

        @pl.when(half == 0)
        def _():
            _attn_tables(g, bias)
            _mask_keys(ak_ref, g, Q_SCALE, kpad, s)
            _mask_keys(av_ref, g, 1.0, vpad, s)

        def kv_body(n, _):
            k8 = _chunk(k_ref, n).astype(F32) * Q_SCALE
            st[n] = jnp.where(tab["diag2"], _dot_tn(_both_ways(k8, tab["kdec"]), _chunk(v_ref, n)), 0.0)
            return 0

        _chunk_loop(n_chunks, kv_body, 0)
        _scan_states(n_chunks, st, FWD_ROWS, tab["lam_f"], BWD_ROWS, tab["lam_b"])
        sink_col = _sink_column(sink_ref, g)

        def retention_chunk(n):
            q = _chunk(q_ref, n)
            k8 = (_chunk(k_ref, n).astype(F32) * Q_SCALE).astype(BF16)
            v = _chunk(v_ref, n)
            p2 = (_dot_nt(_stack_pair(is_a, q), k8) * tab["d2"]).astype(BF16)
            y = _unstack_pair(is_a, _dot_nn(p2, v))
            y = y + _dot_nn(_both_ways(q.astype(F32), tab["qdec"]), st[n].astype(BF16))
            rows = pl.ds(pl.multiple_of(n * CHUNK, CHUNK), CHUNK)
            mu = _group_sum(is_a, y) * (1.0 / HEAD_DIM)
            dlt = y - mu
            var = _group_sum(is_a, dlt * dlt) * (1.0 / HEAD_DIM)
            rstd = lax.rsqrt(var + GN_EPS)
            xhat = dlt * rstd
            xhat_ref[rows, :] = xhat
            rstd_ref[rows, :] = rstd
            gate = _chunk(g_ref, n).astype(F32)
            r_ref[rows, :] = (xhat * gain_ref[...] * gate * _sigmoid(gate)).astype(BF16)

        def attention_block(blk):
            n = half * blocks_half + blk
            rows = pl.ds(pl.multiple_of(blk * BLOCK, BLOCK), BLOCK)
            keys = pl.ds(pl.multiple_of(n * BLOCK, BLOCK), KEYS)
            p, _ = _attn_probs(_stack_heads(aq_ref[rows, :], g), kpad[keys, :], bias, sink_col, n, s)
            a_ref[rows, :] = _unstack_heads(_dot_nn(p.astype(BF16), vpad[keys, :]), g).astype(BF16)

        def trip(i, _):
            chunk, blk = 0, 0
            for kind in _trip_order(FWD_ORDER, RET_UNROLL, per_trip):
                if kind == "r":
                    retention_chunk(i * RET_UNROLL + chunk)
                    chunk += 1
                else:
                    attention_block(i * per_trip + blk)
                    blk += 1
            return 0

        lax.fori_loop(0, trips, trip, 0)

    lane_blk = lambda c0: _seq_spec(s, c0 // LANES)
    half_rows = blocks_half * BLOCK
    aq_spec = pl.BlockSpec((half_rows, GW), lambda b, h: (b * PAIRS_PER_KV + (h & 1), C_AQ // GW + h // 2))
    a_spec = pl.BlockSpec((half_rows, GW), lambda b, h: (b * PAIRS_PER_KV + (h & 1), h // 2))
    kv_spec = lambda c0: pl.BlockSpec((s, LANES), lambda b, h: (b, c0 // LANES))
    pad = pltpu.VMEM((s + 2 * BLOCK, LANES), BF16)
    body, lead = _after(after, body)
    return pl.pallas_call(
        body, name="mixers_fwd", grid=(b_loc, pairs),
        in_specs=lead + [_smem_spec(), _smem_spec(), _smem_spec(), lane_blk(C_RQ), lane_blk(C_RK), lane_blk(C_RV),
                         lane_blk(C_RG), pl.BlockSpec((1, LANES), lambda b, h: (0, h)), aq_spec, kv_spec(C_AK),
                         kv_spec(C_AV)],
        out_specs=[_seq_spec(s, 0), _seq_spec(s, 0), _seq_spec(s, 0), a_spec, _state_spec(n_chunks, pairs)],
        out_shape=[SDS((t, RET_W), BF16), SDS((t, RET_W), F32), SDS((t, RET_W), F32), SDS((t, ATT_W), BF16),
                   SDS((b_loc * pairs * n_chunks, 2 * LANES, LANES), F32)],
        scratch_shapes=[pad, pad, pltpu.VMEM((GROUP * BLOCK, KEYS), F32)],
        compiler_params=_params(("arbitrary", "arbitrary")),
    )(*after, lgf, lgb, sink, u, u, u, u, gn_gain, u, u, u)


def _mixers_bwd(u, xhat, rstd, states, dr, da, lgf, lgb, gn_gain, sink, b_loc, after=()):
    t = u.shape[0]
    s = t // b_loc
    n_chunks = s // CHUNK
    pairs = RET_HEADS // 2
    trips = n_chunks // RET_UNROLL
    blocks_half = (s // BLOCK) // PAIRS_PER_KV
    per_trip = blocks_half // trips
    assert n_chunks % RET_UNROLL == 0 and blocks_half % trips == 0 and PAIRS_PER_KV == 2

    def body(lgf_ref, lgb_ref, sink_ref, q_ref, k_ref, v_ref, g_ref, xhat_ref, rstd_ref, dr_ref, gain_ref,
             aq_ref, ak_ref, av_ref, do_ref, st,
             dq_ref, dk_ref, dv_ref, dg_ref, st_ref, daq_ref, dak_ref, dav_ref, dsink_ref,
             gr, dy_s, kpad, vpad, bias, dk_acc, dv_acc):
        pair = pl.program_id(1)
        g, half = lax.shift_right_logical(pair, 1), pair & 1
        tab = _stacked_tables(lgf_ref, lgb_ref, pair)
        is_a = tab["is_a"]
        gain = gain_ref[...]

        @pl.when(half == 0)
        def _():
            _attn_tables(g, bias)
            _mask_keys(ak_ref, g, Q_SCALE, kpad, s)
            _mask_keys(av_ref, g, 1.0, vpad, s)
            dsink_ref[...] = jnp.zeros_like(dsink_ref)

        @pl.when(pair == 0)
        def _():
            dk_acc[...] = jnp.zeros_like(dk_acc)
            dv_acc[...] = jnp.zeros_like(dv_acc)

        def norm_body(n, dgain):
            rows = pl.ds(pl.multiple_of(n * CHUNK, CHUNK), CHUNK)
            xhat, rstd = xhat_ref[rows, :], rstd_ref[rows, :]
            gate = g_ref[rows, :].astype(F32)
            sg = _sigmoid(gate)
            silu = gate * sg
            d_out = dr_ref[rows, :].astype(F32)
            dg_ref[rows, :] = (d_out * xhat * gain * (sg * (1.0 + gate * (1.0 - sg)))).astype(BF16)
            dxh = d_out * gain * silu
            m1 = _group_sum(is_a, dxh) * (1.0 / HEAD_DIM)
            m2 = _group_sum(is_a, dxh * xhat) * (1.0 / HEAD_DIM)
            dy = (rstd * (dxh - m1 - xhat * m2)).astype(BF16)
            dy_s[rows, :] = dy
            qf = q_ref[rows, :].astype(F32)
            gr[n] = jnp.where(tab["diag2"], _dot_tn(_both_ways(qf, tab["qdec"]), dy), 0.0)
            return dgain + jnp.sum(d_out * xhat * silu, axis=0, keepdims=True)

        colsum = lambda x: jnp.sum(x, axis=0, keepdims=True)

        def grad_body(n, carry):
            xfb, ifa, ifb, iba, ibb, lf, lb = carry
            rows = pl.ds(pl.multiple_of(n * CHUNK, CHUNK), CHUNK)
            q = q_ref[rows, :]
            qf = q.astype(F32)
            k8f = k_ref[rows, :].astype(F32) * Q_SCALE
            k8 = k8f.astype(BF16)
            v = v_ref[rows, :]
            dy = dy_s[rows, :]
            q2, dy2 = _stack_pair(is_a, q), _stack_pair(is_a, dy)
            sc = _dot_nt(q2, k8)
            dp = _dot_nt(dy2, v)
            a2 = (sc * tab["d2"]).astype(BF16)
            ds2 = (dp * tab["d2"]).astype(BF16)
            dq = _unstack_pair(is_a, _dot_nn(ds2, k8))
            dk = _dot_tn(ds2, q2)
            dv = _dot_tn(a2, dy2)
            prod = sc * dp
            pf, pb = prod * tab["df2"], prod * tab["db2"]
            ifa, ifb = ifa + colsum(pf[0:CHUNK, :]), ifb + colsum(pf[CHUNK:2 * CHUNK, :])
            iba, ibb = iba + colsum(pb[0:CHUNK, :]), ibb + colsum(pb[CHUNK:2 * CHUNK, :])
            states, sgrads = st[n], gr[n]
            sb, gb = states.astype(BF16), sgrads.astype(BF16)
            dqc = _dot_nt(dy, sb) * tab["qdec"]
            dkc = _dot_nt(v, gb) * tab["kdec"]
            dv = dv + _dot_nn(_both_ways(k8f, tab["kdec"]), gb)
            dq_ref[rows, :] = (dq + dqc[:, 0:LANES] + dqc[:, LANES:2 * LANES]).astype(BF16)
            dk_ref[rows, :] = ((dk + dkc[:, 0:LANES] + dkc[:, LANES:2 * LANES]) * Q_SCALE).astype(BF16)
            dv_ref[rows, :] = dv.astype(BF16)
            q2w, k2w = jnp.concatenate([qf, qf], axis=1), jnp.concatenate([k8f, k8f], axis=1)
            xfb = xfb + colsum(tab["qexp"] * q2w * dqc + tab["kexp"] * k2w * dkc)
            prod_s = sgrads * states
            lf, lb = lf + colsum(prod_s[0:LANES, :]), lb + colsum(prod_s[LANES:2 * LANES, :])
            return xfb, ifa, ifb, iba, ibb, lf, lb

        sink_col = _sink_column(sink_ref, g)
        head_row = lax.broadcasted_iota(jnp.int32, dsink_ref.shape, 0)

        def attention_block(blk):
            n = half * blocks_half + blk
            rows = pl.ds(pl.multiple_of(blk * BLOCK, BLOCK), BLOCK)
            keys = pl.ds(pl.multiple_of(n * BLOCK, BLOCK), KEYS)
            qm = _stack_heads(aq_ref[rows, :], g)
            k3, v3 = kpad[keys, :], vpad[keys, :]
            p, p_sink = _attn_probs(qm, k3, bias, sink_col, n, s)
            dom = _stack_heads(do_ref[rows, :], g)
            dp = _dot_nt(dom, v3)
            delta = jnp.sum(p * dp, axis=1, keepdims=True)
            ds_mat = (p * (dp - delta)).astype(BF16)
            daq_ref[rows, :] = _unstack_heads(_dot_nn(ds_mat, k3), g).astype(BF16)
            dk_acc[keys, :] += _dot_tn(ds_mat, qm) * Q_SCALE
            dv_acc[keys, :] += _dot_tn(p.astype(BF16), dom)
            w = p_sink * delta
            upd = jnp.zeros(dsink_ref.shape, F32)
            for h in range(GROUP):
                upd = upd + jnp.where(head_row == h, -jnp.sum(w[h * BLOCK:(h + 1) * BLOCK, :]), 0.0)
            dsink_ref[...] += upd

        dgain = _chunk_loop(n_chunks, norm_body, jnp.zeros((1, LANES), F32))
        _scan_states(n_chunks, gr, BWD_ROWS, tab["lam_b"], FWD_ROWS, tab["lam_f"])

        def trip(i, carry):
            chunk, blk = 0, 0
            for kind in _trip_order(BWD_ORDER, RET_UNROLL, per_trip):
                if kind == "r":
                    carry = grad_body(i * RET_UNROLL + chunk, carry)
                    chunk += 1
                else:
                    attention_block(i * per_trip + blk)
                    blk += 1
            return carry

        z = jnp.zeros((1, LANES), F32)
        init = (jnp.zeros((1, 2 * LANES), F32), z, z, z, z, z, z)
        xfb, ifa, ifb, iba, ibb, lf, lb = lax.fori_loop(0, trips, trip, init)
        st_ref[...] = jnp.zeros_like(st_ref)
        st_ref[ST_GAIN:ST_GAIN + 1, :] = dgain
        st_ref[ST_XF:ST_XF + 1, :] = xfb[:, 0:LANES]
        st_ref[ST_XB:ST_XB + 1, :] = xfb[:, LANES:2 * LANES]
        st_ref[ST_IFA:ST_IFA + 1, :] = ifa
        st_ref[ST_IFB:ST_IFB + 1, :] = ifb
        st_ref[ST_IBA:ST_IBA + 1, :] = iba
        st_ref[ST_IBB:ST_IBB + 1, :] = ibb
        st_ref[ST_LF:ST_LF + 1, :] = lf * (CHUNK * tab["lam_f"])
        st_ref[ST_LB:ST_LB + 1, :] = lb * (CHUNK * tab["lam_b"])

        @pl.when(pair == pairs - 1)
        def _():
            dak_ref[...] = dk_acc[BLOCK:BLOCK + s, :].astype(BF16)
            dav_ref[...] = dv_acc[BLOCK:BLOCK + s, :].astype(BF16)

    lane_blk = lambda c0: _seq_spec(s, c0 // LANES)
    seq0 = _seq_spec(s, 0)
    half_rows = blocks_half * BLOCK
    aq_spec = pl.BlockSpec((half_rows, GW), lambda b, h: (b * PAIRS_PER_KV + (h & 1), C_AQ // GW + h // 2))
    a_spec = pl.BlockSpec((half_rows, GW), lambda b, h: (b * PAIRS_PER_KV + (h & 1), h // 2))
    kv_spec = lambda c0: pl.BlockSpec((s, LANES), lambda b, h: (b, c0 // LANES))
    kv_out = pl.BlockSpec((s, LANES), lambda b, h: (b, 0))
    state = pltpu.VMEM((n_chunks, 2 * LANES, LANES), F32)
    pad = pltpu.VMEM((s + 2 * BLOCK, LANES), BF16)
    acc = pltpu.VMEM((s + 2 * BLOCK, LANES), F32)
    body, lead = _after(after, body)
    return pl.pallas_call(
        body, name="mixers_bwd", grid=(b_loc, pairs),
        in_specs=lead + [_smem_spec(), _smem_spec(), _smem_spec(), lane_blk(C_RQ), lane_blk(C_RK), lane_blk(C_RV),
                         lane_blk(C_RG), seq0, seq0, seq0, pl.BlockSpec((1, LANES), lambda b, h: (0, h)),
                         aq_spec, kv_spec(C_AK), kv_spec(C_AV), a_spec, _state_spec(n_chunks, pairs)],
        out_specs=[seq0] * 4 + [pl.BlockSpec((ST_ROWS, LANES), lambda b, h: (b, h)), a_spec, kv_out, kv_out,
                                pl.BlockSpec((8, LANES), lambda b, h: (b * KV_HEADS + h // 2, 0))],
        out_shape=[SDS((t, RET_W), BF16)] * 4 + [SDS((b_loc * ST_ROWS, RET_W), F32), SDS((t, ATT_W), BF16),
                                                   SDS((t, KV_W), BF16), SDS((t, KV_W), BF16),
                                                   SDS((b_loc * KV_HEADS * 8, LANES), F32)],
        scratch_shapes=[state, pltpu.VMEM((s, LANES), BF16), pad, pad,
                        pltpu.VMEM((GROUP * BLOCK, KEYS), F32), acc, acc],
        compiler_params=_params(("arbitrary", "arbitrary")),
    )(*after, lgf, lgb, sink, u, u, u, u, xhat, rstd, dr, gn_gain, u, u, u, da, states)


def _pack_small(acc2, acc1, ret_stats, dsink, b_loc, d):
    pairs = RET_HEADS // 2

    def body(acc2_ref, acc1_ref, st_ref, dsink_ref, out_ref):
        out_ref[...] = jnp.zeros_like(out_ref)
        out_ref[ROW_LN1G:ROW_LN1G + 1, :] = acc1_ref[0:1, :]
        out_ref[ROW_LN1B:ROW_LN1B + 1, :] = acc1_ref[1:2, :]
        out_ref[ROW_LN2G:ROW_LN2G + 1, :] = acc2_ref[1:2, :]
        out_ref[ROW_LN2B:ROW_LN2B + 1, :] = acc2_ref[2:3, :]
        out_ref[ROW_LOSS:ROW_LOSS + 1, :] = acc2_ref[0:1, :]
        st = st_ref[0:ST_ROWS, :]
        for b in range(1, b_loc):
            st = st + st_ref[b * ST_ROWS:(b + 1) * ST_ROWS, :]
        out_ref[ROW_GN:ROW_GN + 1, 0:RET_W] = st[ST_GAIN:ST_GAIN + 1, :]
        lane = lax.broadcasted_iota(jnp.int32, (1, d), 1)
        misc = jnp.zeros((1, d), F32)
        for pr in range(pairs):
            blk = st[:, pr * LANES:(pr + 1) * LANES]
            half = lax.broadcasted_iota(jnp.int32, (1, LANES), 1) < HEAD_DIM
            for h in range(2):
                sel = half if h == 0 else jnp.logical_not(half)
                cross_f = jnp.sum(jnp.where(sel, blk[ST_XF:ST_XF + 1, :] + blk[ST_LF:ST_LF + 1, :], 0.0))
                cross_b = jnp.sum(jnp.where(sel, blk[ST_XB:ST_XB + 1, :] + blk[ST_LB:ST_LB + 1, :], 0.0))
                intra_f = jnp.sum(blk[ST_IFA + h:ST_IFA + h + 1, :])
                intra_b = jnp.sum(blk[ST_IBA + h:ST_IBA + h + 1, :])
                head = 2 * pr + h
                misc = jnp.where(lane == MISC_DF + head, cross_f + intra_f, misc)
                misc = jnp.where(lane == MISC_DB + head, cross_b + intra_b, misc)
        for g in range(KV_HEADS):
            tot = dsink_ref[g * 8:(g + 1) * 8, :]
            for b in range(1, b_loc):
                tot = tot + dsink_ref[(b * KV_HEADS + g) * 8:(b * KV_HEADS + g + 1) * 8, :]
            for h in range(GROUP):
                misc = jnp.where(lane == MISC_SINK + GROUP * g + h, jnp.sum(tot[h:h + 1, 0:1]), misc)
        out_ref[ROW_MISC:ROW_MISC + 1, :] = misc

    return pl.pallas_call(body, name="pack_small", out_shape=SDS((SMALL_ROWS, d), F32))(acc2, acc1, ret_stats, dsink)


BIG = ("w_in", "w_out", "w_ffn_gate", "w_ffn_up", "w_ffn_down", "w_ple_proj", "w_ple_gate")
TRANSPOSED_OUTSIDE = ("w_in", "w_ffn_gate", "w_ffn_up")
TRANSPOSED_HERE = ("w_ple_proj",)
SMALL = ("ret_decay_fwd", "ret_decay_bwd", "ret_gn_gain", "attn_sink", "ln1_gain", "ln1_bias", "ln2_gain", "ln2_bias")
ORDER = ("w_in", "ret_decay_fwd", "ret_decay_bwd", "ret_gn_gain", "attn_sink", "w_out", "ln1_gain", "ln1_bias",
         "w_ffn_gate", "w_ffn_up", "w_ffn_down", "w_ple_proj", "w_ple_gate", "ln2_gain", "ln2_bias")


GATHER_ORDER = ("w_in", "w_ffn_up", "w_out", "w_ffn_gate", "w_ple_gate", "w_ple_proj", "w_ffn_down")
GATHER_TWO_LEVEL = ("w_in", "w_ffn_up")


def _local_step(x2, p2, target2, fetch, publish, small, b_loc, me):
    d = x2.shape[1]
    lgf, lgb = _log_decay(small["ret_decay_fwd"], small["ret_decay_bwd"])
    lgf1, lgb1, sink1 = lgf.reshape(-1), lgb.reshape(-1), small["attn_sink"].reshape(-1)
    (w_in,) = fetch(("w_in",), ())
    u, xb = _in_proj(x2, w_in)
    passed = fetch.pass_on("w_ffn_up", (xb,))
    r, ret_xhat, ret_rstd, a, ret_states = _mixers_fwd(u, lgf1, lgb1, small["ret_gn_gain"], sink1, b_loc, passed)
    w_out, w_gate, w_up = fetch(("w_out", "w_ffn_gate", "w_ffn_up"), (r, a))
    xhat1, rstd1, h1b, dact_dg, dact_du, act = _mix_ln1_ffn_up(
        r, a, x2, w_out, w_gate, w_up, small["ln1_gain"], small["ln1_bias"])
    w_pg, w_pe, w_down = fetch(("w_ple_gate", "w_ple_proj", "w_ffn_down"), (act,))
    dz2, dz2b, dsb, dpleb, dg, dup, acc2 = _ffn_down_ln2_loss(
        act, dact_dg, dact_du, h1b, p2, xhat1, target2, w_down, w_pg, w_pe,
        small["ln1_gain"], small["ln1_bias"], small["ln2_gain"], small["ln2_bias"])
    own = {}

    def grad(name, parts, rhs, after=()):
        whole, own[name] = _weight_grad("grad_" + name, me, parts, rhs, after)
        return whole

    ffn_jobs = dict(w_ffn_down=(act, dz2b), w_ple_proj=(dpleb, p2), w_ple_gate=(h1b, dsb),
                    w_ffn_gate=(dg, h1b), w_ffn_up=(dup, h1b))
    wholes, owns = _weight_grad_jobs("grad_w_ffn", me, list(ffn_jobs.values()))
    own.update(zip(ffn_jobs, owns))
    t2 = publish("ffn", dict(zip(ffn_jobs, wholes)))
    dz1, dz1b, dr, da, acc1 = _dh1_ln1_bwd(
        dz2, dg, dup, dsb, xhat1, rstd1, w_gate, w_up, w_pg, w_out, small["ln1_gain"], t2)
    t3 = publish("out", dict(w_out=grad("w_out", [r, a], dz1b)))
    dq, dk, dv, dgate, ret_stats, daq, dak, dav, dsink = _mixers_bwd(
        u, ret_xhat, ret_rstd, ret_states, dr, da, lgf1, lgb1, small["ret_gn_gain"], sink1, b_loc, t3)
    parts = [dq, dk, dv, dgate, daq, dak, dav]
    small_part = _pack_small(acc2, acc1, ret_stats, dsink, b_loc, d)
    t4 = publish("in", dict(w_in=grad("w_in", parts, xb)), small_part)
    grad_x = _in_proj_bwd(dz1, parts, w_in, t4)
    return grad_x, own, small_part


def kernel(x, p, w_in, ret_decay_fwd, ret_decay_bwd, ret_gn_gain, attn_sink, w_out, ln1_gain, ln1_bias, w_ffn_gate, w_ffn_up, w_ffn_down, w_ple_proj, w_ple_gate, ln2_gain, ln2_bias, loss_target, m_w_in, m_ret_decay_fwd, m_ret_decay_bwd, m_ret_gn_gain, m_attn_sink, m_w_out, m_ln1_gain, m_ln1_bias, m_w_ffn_gate, m_w_ffn_up, m_w_ffn_down, m_w_ple_proj, m_w_ple_gate, m_ln2_gain, m_ln2_bias, v_w_in, v_ret_decay_fwd, v_ret_decay_bwd, v_ret_gn_gain, v_attn_sink, v_w_out, v_ln1_gain, v_ln1_bias, v_w_ffn_gate, v_w_ffn_up, v_w_ffn_down, v_w_ple_proj, v_w_ple_gate, v_ln2_gain, v_ln2_bias):
    given = dict(locals())

    def strip(n, a):
        if n not in BIG:
            return a
        return a[0].T if n in TRANSPOSED_OUTSIDE else a[0]

    def restore(n, a):
        if n not in BIG:
            return a
        return (a.T if n in TRANSPOSED_OUTSIDE else a)[None]

    w = {n: strip(n, given[n]) for n in ORDER}
    m = {n: strip(n, given["m_" + n]) for n in ORDER}
    v = {n: strip(n, given["v_" + n]) for n in ORDER}
    b_loc, s, d = x.shape
    x2 = x.reshape(b_loc * s, d)
    p2 = p[0].reshape(b_loc * s, p.shape[-1])
    target2 = loss_target.reshape(b_loc * s, d)

    small = {n: w[n] for n in SMALL}
    me = (4 * lax.axis_index("x") + 2 * lax.axis_index("y") + lax.axis_index("c")).astype(jnp.int32).reshape(1)

    gather = _gather_start(
        {n: w[n] for n in GATHER_ORDER},
        [_gather_copy_near if n in GATHER_TWO_LEVEL else _gather_copy for n in GATHER_ORDER])

    passing = {}

    def pass_on(n, after):
        near = _split_copy_wait("gather_wait_" + n + "_near", gather, [GATHER_ORDER.index(n)], list(after))
        passing[n] = _split_copy_start("gather_pass_" + n, near, [_gather_copy_pass])
        return (passing[n]["token"],)

    def fetch(names, after):
        out = {}
        for n in [n for n in names if n in GATHER_TWO_LEVEL]:
            if n not in passing:
                pass_on(n, after)
            out[n] = _split_copy_wait("gather_wait_" + n, passing[n], [0], list(after))[0][0]
        direct = [n for n in names if n not in GATHER_TWO_LEVEL]
        if direct:
            got = _split_copy_wait("gather_wait_" + direct[0], gather, [GATHER_ORDER.index(n) for n in direct],
                                   list(after))
            out.update({n: item[0] for n, item in zip(direct, got)})
        return [out[n] for n in names]

    scatters = []

    def publish(tag, products, small_sums=None):
        items = [(products[n], lax.empty((N_DEV - 1, products[n].shape[0] // N_DEV, products[n].shape[1]), BF16))
                 for n in products]
        copies = [_scatter_copy] * len(items)
        if small_sums is not None:
            items.append((small_sums, lax.empty((N_DEV - 1,) + small_sums.shape, F32)))
            copies.append(_small_copy)
        started = _split_copy_start("scatter_start_" + tag, items, copies)
        scatters.append((list(products), small_sums is not None, started))
        return (started["token"],)

    fetch.pass_on = pass_on
    grad_x, own, small_part = _local_step(x2, p2, target2, fetch, publish, small, b_loc, me)

    out_g, out_d, out_m, out_v = {}, {}, {}, {}
    after = [grad_x]
    for names, with_small, started in scatters:
        landed = _split_copy_wait("scatter_wait_" + names[0], started, list(range(len(started["items"]))), after)
        if with_small:
            mine, from_peers = landed[-1]
            loss, sg, sd, sm, sv = _small_adamw(
                me, mine, from_peers, small, {n: m[n] for n in SMALL}, {n: v[n] for n in SMALL})
            for dst, src in ((out_g, sg), (out_d, sd), (out_m, sm), (out_v, sv)):
                dst.update(src)
        recv = {n: item[1] for n, item in zip(names, landed)}
        alike = {}
        for n in names:
            alike.setdefault((own[n].shape, n in TRANSPOSED_HERE), []).append(n)
        for (_, transposed), ns in alike.items():
            res = _reduce_adamw(ns[0], [own[n] for n in ns], [recv[n] for n in ns], [w[n] for n in ns],
                                [m[n] for n in ns], [v[n] for n in ns], transposed)
            for dst, vals in zip((out_g, out_d, out_m, out_v), res):
                dst.update(zip(ns, vals))
        after = [out_v[names[-1]]]

    outs = [loss[0, 0], grad_x.reshape(x.shape)]
    for group in (out_g, out_d, out_m, out_v):
        outs += [restore(n, group[n]) for n in ORDER]
    return tuple(outs)
```

```python
import functools

import jax
import jax.numpy as jnp
from jax import lax
from jax.experimental import pallas as pl
from jax.experimental.pallas import tpu as pltpu

F32, BF16 = jnp.float32, jnp.bfloat16
SDS = jax.ShapeDtypeStruct
MESH = pl.DeviceIdType.MESH

N_DEV = 8
HEAD_DIM = 64
RET_HEADS = 8
ATTN_HEADS = 8
KV_HEADS = 2
GROUP = ATTN_HEADS // KV_HEADS
RET_W = RET_HEADS * HEAD_DIM
ATT_W = ATTN_HEADS * HEAD_DIM
KV_W = KV_HEADS * HEAD_DIM
LANES = 128
CHUNK = 128
BLOCK = 128
Q_SCALE = HEAD_DIM ** -0.5
ALPHA = 2.0 ** 0.25
LN_EPS = 1e-5
GN_EPS = 1e-5
NEG_INF = -1e30
C_RQ, C_RK, C_RV, C_RG = 0, RET_W, 2 * RET_W, 3 * RET_W
C_AQ = 4 * RET_W
C_AK = C_AQ + ATT_W
C_AV = C_AK + KV_W
IN_W = C_AV + KV_W

ADAM_LR = 0.001
ADAM_B1 = 0.9
ADAM_B2 = 0.999
ADAM_EPS = 1e-08
ADAM_WD = 0.01
ADAM_STEP = 10

VMEM_LIMIT = 56 * 1024 * 1024
MATMUL_ROWS = 512
EPILOGUE_ROWS = 256
SUB_ROWS = 256
SMALL_ROWS = 16
ROW_LN1G, ROW_LN1B, ROW_LN2G, ROW_LN2B, ROW_LOSS, ROW_GN, ROW_MISC = 0, 1, 2, 3, 4, 5, 6
MISC_DF, MISC_DB, MISC_SINK = 0, 8, 16


def _dot_nn(a, b):
    return lax.dot_general(a, b, (((1,), (0,)), ((), ())), preferred_element_type=F32)


def _dot_nt(a, b):
    return lax.dot_general(a, b, (((1,), (1,)), ((), ())), preferred_element_type=F32)


def _dot_tn(a, b):
    return lax.dot_general(a, b, (((0,), (0,)), ((), ())), preferred_element_type=F32)


def _params(sem=None, vmem=VMEM_LIMIT):
    kw = {"vmem_limit_bytes": vmem}
    if sem is not None:
        kw["dimension_semantics"] = sem
    return pltpu.CompilerParams(**kw)


def _row_tile(t, want=512):
    tm = want
    while t % tm:
        tm //= 2
    return tm


def _sigmoid(x):
    return jax.nn.sigmoid(x)


def _layer_norm_stats(z):
    mu = jnp.mean(z, axis=1, keepdims=True)
    d = z - mu
    var = jnp.mean(d * d, axis=1, keepdims=True)
    rstd = lax.rsqrt(var + LN_EPS)
    return d * rstd, rstd


def _layer_norm_bwd(dxh, xhat, rstd):
    m1 = jnp.mean(dxh, axis=1, keepdims=True)
    m2 = jnp.mean(dxh * xhat, axis=1, keepdims=True)
    return rstd * (dxh - m1 - xhat * m2)


def _mesh_pos():
    return lax.axis_index("x"), lax.axis_index("y"), lax.axis_index("c")


HBM_SPEC = pl.BlockSpec(memory_space=pltpu.HBM)
SEM_SPEC = pl.BlockSpec(memory_space=pltpu.SEMAPHORE)
ANY_SPEC = pl.BlockSpec(memory_space=pl.ANY)
SIDE_EFFECT = pltpu.SideEffectType.DATAFLOW_SIDE_EFFECTING
PEER_SEMS = pltpu.SemaphoreType.DMA((N_DEV - 1,))


def _in_hbm(a):
    return pltpu.with_memory_space_constraint(a, pltpu.HBM)


def _split_copy_start(name, items, copies):
    n = len(items)
    flat = [a for it in items for a in it]
    k = len(flat)

    def body(*refs):
        arr, sems = list(refs[:k]), refs[k:k + 2 * n]
        for i, it in enumerate(items):
            mine = [arr.pop(0) for _ in it]
            for m in range(1, N_DEV):
                cp = copies[i](m, mine, sems[i].at[m - 1], sems[n + i].at[m - 1])
                if cp is not None:
                    cp.start()
        token = refs[-1]
        token[...] = jnp.zeros_like(token)

    res = pl.pallas_call(
        body, name=name,
        out_shape=[PEER_SEMS] * (2 * n) + [pltpu.HBM(a.shape, a.dtype) for a in flat] + [SDS((8, LANES), F32)],
        in_specs=[HBM_SPEC] * k,
        out_specs=[SEM_SPEC] * (2 * n) + [HBM_SPEC] * k + [pl.BlockSpec(memory_space=pltpu.VMEM)],
        input_output_aliases={j: 2 * n + j for j in range(k)},
        compiler_params=pltpu.CompilerParams(has_side_effects=SIDE_EFFECT),
    )(*[_in_hbm(a) for a in flat])
    thru, out_items = list(res[2 * n:2 * n + k]), []
    for it in items:
        out_items.append(tuple(thru.pop(0) for _ in it))
    return dict(send=res[:n], recv=res[n:2 * n], items=out_items, token=res[-1], copies=copies)


def _gather_start(shards, copies):
    names = list(shards)
    n = len(names)
    flip = [name in TRANSPOSED_HERE for name in names]
    shapes = [shards[name].shape[::-1] if f else shards[name].shape for name, f in zip(names, flip)]
    most = (max(s[0] for s in shapes), max(s[1] for s in shapes))

    def body(*refs):
        src, sems, land, token = refs[:n], refs[n:3 * n], refs[3 * n:4 * n], refs[4 * n]
        wide, narrow, sem = refs[4 * n + 1:]
        for i, (rows, cols) in enumerate(shapes):
            raw = wide.at[0:cols, 0:rows] if flip[i] else wide.at[0:rows, 0:cols]
            bring = pltpu.make_async_copy(src[i], raw, sem.at[0])
            bring.start()
            bring.wait()
            narrow[0:rows, 0:cols] = (raw[...].T if flip[i] else raw[...]).astype(BF16)
            mine = land[i].at[pl.ds(pl.multiple_of(_peer_index(0) * rows, 8), rows), :]
            place = pltpu.make_async_copy(narrow.at[0:rows, 0:cols], mine, sem.at[0])
            place.start()
            place.wait()
            for m in range(1, N_DEV):
                cp = copies[i](m, [land[i]], sems[i].at[m - 1], sems[n + i].at[m - 1])
                if cp is not None:
                    cp.start()
        token[...] = jnp.zeros_like(token)

    side = max(most)
    res = pl.pallas_call(
        body, name="gather_start",
        out_shape=[PEER_SEMS] * (2 * n) + [pltpu.HBM((N_DEV * r, c), BF16) for r, c in shapes] + [SDS((8, LANES), F32)],
        in_specs=[HBM_SPEC] * n,
        out_specs=[SEM_SPEC] * (2 * n) + [HBM_SPEC] * n + [pl.BlockSpec(memory_space=pltpu.VMEM)],
        scratch_shapes=[pltpu.VMEM((side, side), F32), pltpu.VMEM(most, BF16), pltpu.SemaphoreType.DMA((1,))],
        compiler_params=pltpu.CompilerParams(has_side_effects=SIDE_EFFECT),
    )(*[_in_hbm(shards[name]) for name in names])
    return dict(send=res[:n], recv=res[n:2 * n], items=[(a,) for a in res[2 * n:3 * n]], token=res[-1], copies=copies)


def _gather_relay(name, started, which, after):
    (land,) = started["items"][which]

    def body(*refs):
        land_ref, old_send, old_recv = refs[0], refs[1], refs[2]
        send, recv, token = refs[3 + len(after)], refs[4 + len(after)], refs[-1]
        for m in range(1, N_DEV):
            cp = _gather_copy_near(m, [land_ref], old_send.at[m - 1], old_recv.at[m - 1])
            if cp is None:
                continue
            cp.wait_send()
            cp.wait_recv()
            if m > 1:
                _gather_copy_pass(m + 1, [land_ref], send.at[m], recv.at[m]).start()
        token[...] = jnp.zeros_like(token)

    res = pl.pallas_call(
        body, name=name,
        out_shape=[PEER_SEMS, PEER_SEMS, pltpu.HBM(land.shape, land.dtype), SDS((8, LANES), F32)],
        in_specs=[HBM_SPEC, SEM_SPEC, SEM_SPEC] + [ANY_SPEC] * len(after),
        out_specs=[SEM_SPEC, SEM_SPEC, HBM_SPEC, pl.BlockSpec(memory_space=pltpu.VMEM)],
        input_output_aliases={0: 2},
        compiler_params=pltpu.CompilerParams(has_side_effects=SIDE_EFFECT),
    )(land, started["send"][which], started["recv"][which], *[_in_hbm(a) for a in after])
    return dict(send=[res[0]], recv=[res[1]], items=[(res[2],)], token=res[3], copies=[_gather_copy_pass])


def _split_copy_wait(name, started, which, after):
    items = [started["items"][i] for i in which]
    copies = [started["copies"][i] for i in which]
    n = len(items)
    flat = [a for it in items for a in it]
    k = len(flat)

    def body(*refs):
        arr, sems = list(refs[:k]), refs[k:k + 2 * n]
        for i, it in enumerate(items):
            mine = [arr.pop(0) for _ in it]
            for m in range(1, N_DEV):
                cp = copies[i](m, mine, sems[i].at[m - 1], sems[n + i].at[m - 1])
                if cp is not None:
                    cp.wait_send()
                    cp.wait_recv()

    res = pl.pallas_call(
        body, name=name,
        out_shape=[pltpu.HBM(a.shape, a.dtype) for a in flat],
        in_specs=[HBM_SPEC] * k + [SEM_SPEC] * (2 * n) + [ANY_SPEC] * len(after),
        out_specs=[HBM_SPEC] * k,
        input_output_aliases={j: j for j in range(k)},
        compiler_params=pltpu.CompilerParams(has_side_effects=SIDE_EFFECT),
    )(*flat, *[started["send"][i] for i in which], *[started["recv"][i] for i in which], *[_in_hbm(a) for a in after])
    thru, out_items = list(res), []
    for it in items:
        out_items.append(tuple(thru.pop(0) for _ in it))
    return out_items


def _gather_copy(m, refs, send_sem, recv_sem):
    (land_ref,) = refs
    r = land_ref.shape[0] // N_DEV
    mine = land_ref.at[pl.ds(pl.multiple_of(_peer_index(0) * r, 8), r), :]
    return pltpu.make_async_remote_copy(src_ref=mine, dst_ref=mine, send_sem=send_sem, recv_sem=recv_sem,
                                        device_id=_peer(m), device_id_type=MESH)


def _gather_copy_near(m, refs, send_sem, recv_sem):
    return _gather_copy(m, refs, send_sem, recv_sem) if m == 1 or m % 2 == 0 else None


def _gather_copy_pass(m, refs, send_sem, recv_sem):
    if m == 1 or m % 2 == 0:
        return None
    (land_ref,) = refs
    r = land_ref.shape[0] // N_DEV
    block = land_ref.at[pl.ds(pl.multiple_of(_peer_index(m ^ 1) * r, 8), r), :]
    return pltpu.make_async_remote_copy(src_ref=block, dst_ref=block, send_sem=send_sem, recv_sem=recv_sem,
                                        device_id=_peer(1), device_id_type=MESH)


def _small_copy(m, refs, send_sem, recv_sem):
    part_ref, land_ref = refs
    return pltpu.make_async_remote_copy(src_ref=part_ref, dst_ref=land_ref.at[m - 1], send_sem=send_sem,
                                        recv_sem=recv_sem, device_id=_peer(m), device_id_type=MESH)


def _scatter_copy(m, refs, send_sem, recv_sem):
    buf_ref, land_ref = refs
    r = buf_ref.shape[0] // N_DEV
    src = buf_ref.at[pl.ds(pl.multiple_of(_peer_index(m) * r, 8), r), :]
    return pltpu.make_async_remote_copy(src_ref=src, dst_ref=land_ref.at[m - 1], send_sem=send_sem,
                                        recv_sem=recv_sem, device_id=_peer(m), device_id_type=MESH)


def _peer(m):
    x, y, c = _mesh_pos()
    bx, by, bc = (m >> 2) & 1, (m >> 1) & 1, m & 1
    return (x ^ bx if bx else x, y ^ by if by else y, c ^ bc if bc else c)


def _peer_index(m):
    x, y, c = _mesh_pos()
    return (4 * x + 2 * y + c) ^ m


SMALL_PLACE = {
    "ln1_gain": (ROW_LN1G, 0), "ln1_bias": (ROW_LN1B, 0), "ln2_gain": (ROW_LN2G, 0), "ln2_bias": (ROW_LN2B, 0),
    "ret_gn_gain": (ROW_GN, 0), "ret_decay_fwd": (ROW_MISC, MISC_DF), "ret_decay_bwd": (ROW_MISC, MISC_DB),
    "attn_sink": (ROW_MISC, MISC_SINK)}


def _small_adamw(me, part, landed, w, m, v):
    d = part.shape[1]
    names = list(SMALL_PLACE)
    k = len(names)

    def body(*refs):
        me_ref, part_ref, land_ref = refs[:3]
        refs = refs[2:]
        w_refs, m_refs, v_refs = refs[1:1 + k], refs[1 + k:1 + 2 * k], refs[1 + 2 * k:1 + 3 * k]
        outs = refs[1 + 3 * k:1 + 7 * k + 1]
        tot_ref = refs[-1]
        loss_ref, g_refs, dl_refs = outs[0], outs[1:1 + k], outs[1 + k:1 + 2 * k]
        nm_refs, nv_refs = outs[1 + 2 * k:1 + 3 * k], outs[1 + 3 * k:1 + 4 * k]
        tot = jnp.zeros(part_ref.shape, F32)
        for dev in range(N_DEV):
            j = dev ^ me_ref[0]
            tot = tot + jnp.where(j == 0, part_ref[...], land_ref[jnp.maximum(j, 1) - 1])
        tot_ref[...] = tot
        loss_ref[...] = (0.5 / d) * jnp.sum(tot_ref[ROW_LOSS:ROW_LOSS + 1, :], axis=1, keepdims=True)
        for i, name in enumerate(names):
            row, lo = SMALL_PLACE[name]
            wv = w_refs[i][...]
            g = tot_ref[row:row + 1, lo:lo + wv.shape[1]]
            if name.startswith("ret_decay"):
                p2 = jnp.exp2(wv)
                g = g * (-p2 * jnp.log(2.0) / (1.0 - p2))
            g_refs[i][...] = g
            _adamw_store(g, wv, m_refs[i][...], v_refs[i][...], dl_refs[i], nm_refs[i], nv_refs[i])

    shapes = [SDS(w[n].shape, F32) for n in names]
    vm = pl.BlockSpec(memory_space=pltpu.VMEM)
    res = pl.pallas_call(
        body, name="small_adamw", out_shape=[SDS((1, 1), F32)] + shapes * 4,
        in_specs=[_smem_spec()] + [vm] * (2 + 3 * k), out_specs=[vm] * (1 + 4 * k),
        scratch_shapes=[pltpu.VMEM(part.shape, F32)],
    )(me, part, landed, *[w[n] for n in names], *[m[n] for n in names], *[v[n] for n in names])
    groups = [dict(zip(names, res[1 + j * k:1 + (j + 1) * k])) for j in range(4)]
    return (res[0], *groups)


def _adamw_store(g, w, m, v, dl_ref, nm_ref, nv_ref):
    m = ADAM_B1 * m + (1.0 - ADAM_B1) * g
    v = ADAM_B2 * v + (1.0 - ADAM_B2) * (g * g)
    m_hat = m / (1.0 - ADAM_B1 ** ADAM_STEP)
    v_hat = v / (1.0 - ADAM_B2 ** ADAM_STEP)
    dl_ref[...] = -ADAM_LR * (m_hat / (jnp.sqrt(v_hat) + ADAM_EPS) + ADAM_WD * w)
    nm_ref[...] = m
    nv_ref[...] = v


def _reduce_adamw(name, owns, recvs, ws, ms, vs, transposed):
    count = len(owns)
    rows, n = owns[0].shape
    steps = 1 if transposed or rows % 32 else 4
    rb = rows // steps

    def body(*refs):
        ins, outs = refs[:5 * count], refs[5 * count:]
        j = pl.program_id(0)
        for k in range(count):
            @pl.when(j == k)
            def _(k=k):
                own_ref, recv_ref, w_ref, m_ref, v_ref = ins[5 * k:5 * k + 5]
                g_ref, dl_ref, nm_ref, nv_ref = outs[4 * k:4 * k + 4]
                g = own_ref[...]
                for p in range(N_DEV - 1):
                    g = g + recv_ref[p].astype(F32)
                if transposed:
                    g = g.T
                g_ref[...] = g
                _adamw_store(g, w_ref[...], m_ref[...], v_ref[...], dl_ref, nm_ref, nv_ref)

    def turn(k):
        return lambda j, i: jnp.where(j == k, i, jnp.where(j < k, 0, steps - 1))

    in_specs, out_specs = [], []
    for k in range(count):
        at = turn(k)
        blk = pl.BlockSpec(ws[0].shape if transposed else (rb, n), lambda j, i, at=at: (at(j, i), 0))
        in_specs += [pl.BlockSpec((rb, n), lambda j, i, at=at: (at(j, i), 0)),
                     pl.BlockSpec((N_DEV - 1, rb, n), lambda j, i, at=at: (0, at(j, i), 0)), blk, blk, blk]
        out_specs += [blk] * 4
    res = pl.pallas_call(
        body, name="adamw_" + name, grid=(count, steps), in_specs=in_specs, out_specs=out_specs,
        out_shape=[SDS(ws[0].shape, F32)] * (4 * count), compiler_params=_params(("arbitrary", "arbitrary")),
    )(*[a for k in range(count) for a in (owns[k], recvs[k], ws[k], ms[k], vs[k])])
    return [list(res[j::4]) for j in range(4)]


def _row_spec(tm, width):
    return pl.BlockSpec((tm, width), lambda i: (i, 0))


def _full_spec(shape):
    return pl.BlockSpec(shape, lambda i: (0,) * len(shape))


_acc_spec = _full_spec


def _sub_rows(tm):
    step = min(SUB_ROWS, tm)
    return [(lo, lo + step) for lo in range(0, tm, step)]


def _in_proj(x2, wt_in):
    t, d = x2.shape
    u_w = wt_in.shape[0]
    tm = _row_tile(t, MATMUL_ROWS)

    def body(x_ref, w_ref, u_ref, xb_ref):
        xb = x_ref[...].astype(BF16)
        xb_ref[...] = xb
        u_ref[...] = _dot_nt(xb, w_ref[...]).astype(BF16)

    return pl.pallas_call(
        body, name="in_proj", grid=(t // tm,),
        in_specs=[_row_spec(tm, d), _full_spec(wt_in.shape)],
        out_specs=[_row_spec(tm, u_w), _row_spec(tm, d)],
        out_shape=[SDS((t, u_w), BF16), SDS((t, d), BF16)],
        compiler_params=_params(("parallel",)),
    )(x2, wt_in)


def _col_halves(f):
    n = f // LANES
    k = (n + 1) // 2 * LANES
    return [(0, k), (k, f)] if k < f else [(0, f)]


def _mix_ln1_ffn_up(r, a, x2, w_out, wt_gate, wt_up, g1, b1):
    t, d = x2.shape
    f = wt_gate.shape[0]
    tm = _row_tile(t, EPILOGUE_ROWS)

    def body(r_ref, a_ref, x_ref, wo_ref, wg_ref, wu_ref, g_ref, b_ref, xh_ref, rs_ref, hb_ref, dg_ref, du_ref,
             act_ref):
        mix = _dot_nn(r_ref[...], wo_ref[0:RET_W, :]) + _dot_nn(a_ref[...], wo_ref[RET_W:RET_W + ATT_W, :])
        z = ALPHA * x_ref[...] + mix
        xhat, rstd = _layer_norm_stats(z)
        xh_ref[...] = xhat
        rs_ref[...] = jnp.broadcast_to(rstd, rs_ref.shape)
        h = (xhat * g_ref[...] + b_ref[...]).astype(BF16)
        hb_ref[...] = h
        g = _dot_nt(h, wg_ref[...])
        u = _dot_nt(h, wu_ref[...])
        sg = _sigmoid(g)
        silu = g * sg
        dg_ref[...] = (u * (sg * (1.0 + g * (1.0 - sg)))).astype(BF16)
        du_ref[...] = silu.astype(BF16)
        act_ref[...] = (silu * u).astype(BF16)

    wide, narrow = _row_spec(tm, f), _row_spec(tm, d)
    return pl.pallas_call(
        body, name="mix_ln1_ffn_up", grid=(t // tm,),
        in_specs=[_row_spec(tm, RET_W), _row_spec(tm, ATT_W), narrow, _resident_spec(w_out.shape),
                  _resident_spec(wt_gate.shape), _resident_spec(wt_up.shape), _full_spec(g1.shape),
                  _full_spec(b1.shape)],
        out_specs=[narrow, _row_spec(tm, LANES), narrow, wide, wide, wide],
        out_shape=[SDS((t, d), F32), SDS((t, LANES), F32), SDS((t, d), BF16)] + [SDS((t, f), BF16)] * 3,
        compiler_params=_params(("parallel",)),
    )(r, a, x2, w_out, wt_gate, wt_up, g1, b1)


def _ffn_down_ln2_loss(act, dact_dg, dact_du, h1b, p2, xhat1, target, w_down, w_pg, wt_pe, g1, b1, g2, b2):
    t, d = xhat1.shape
    f = act.shape[1]
    pdim = p2.shape[1]
    tm = _row_tile(t, EPILOGUE_ROWS)

    def body(act_ref, fg_ref, fu_ref, hb_ref, p_ref, xh1_ref, tgt_ref, wd_ref, wpg_ref, wpe_ref, g1_ref, b1_ref,
             g2_ref, b2_ref, dz_ref, dzb_ref, ds_ref, dple_ref, dg_ref, du_ref, acc_ref):
        @pl.when(pl.program_id(0) == 0)
        def _():
            acc_ref[...] = jnp.zeros_like(acc_ref)

        for lo, hi in _sub_rows(tm):
            h1 = xh1_ref[lo:hi, :] * g1_ref[...] + b1_ref[...]
            pg = _sigmoid(_dot_nn(hb_ref[lo:hi, :], wpg_ref[...]))
            ple = _dot_nt(p_ref[lo:hi, :].astype(BF16), wpe_ref[...])
            gated = pg * ple
            dgate = gated * (1.0 - pg)
            ffn = _dot_nn(act_ref[lo:hi, :], wd_ref[...])
            z2 = ALPHA * h1 + gated + ffn
            xhat2, rstd2 = _layer_norm_stats(z2)
            err = xhat2 * g2_ref[...] + b2_ref[...] - tgt_ref[lo:hi, :]
            dy = err * (1.0 / d)
            dz = _layer_norm_bwd(dy * g2_ref[...], xhat2, rstd2)
            dzb = dz.astype(BF16)
            dz_ref[lo:hi, :] = dz
            dzb_ref[lo:hi, :] = dzb
            ds_ref[lo:hi, :] = (dz * dgate).astype(BF16)
            dple_ref[lo:hi, :] = (dz * pg).astype(BF16)
            acc_ref[0:1, :] += jnp.sum(err * err, axis=0, keepdims=True)
            acc_ref[1:2, :] += jnp.sum(dy * xhat2, axis=0, keepdims=True)
            acc_ref[2:3, :] += jnp.sum(dy, axis=0, keepdims=True)
            for c0, c1 in _col_halves(f):
                da = _dot_nt(dzb, wd_ref[c0:c1, :])
                dg_ref[lo:hi, c0:c1] = (da * fg_ref[lo:hi, c0:c1].astype(F32)).astype(BF16)
                du_ref[lo:hi, c0:c1] = (da * fu_ref[lo:hi, c0:c1].astype(F32)).astype(BF16)

    vec = _full_spec(g1.shape)
    wide, narrow = _row_spec(tm, f), _row_spec(tm, d)
    return pl.pallas_call(
        body, name="ffn_down_ln2_loss", grid=(t // tm,),
        in_specs=[wide, wide, wide, narrow, _row_spec(tm, pdim), narrow, narrow,
                  _full_spec(w_down.shape), _full_spec(w_pg.shape), _full_spec(wt_pe.shape), vec, vec, vec, vec],
        out_specs=[narrow] * 4 + [wide, wide, _acc_spec((8, d))],
        out_shape=[SDS((t, d), F32), SDS((t, d), BF16), SDS((t, d), BF16), SDS((t, d), BF16),
                   SDS((t, f), BF16), SDS((t, f), BF16), SDS((8, d), F32)],
        compiler_params=_params(("arbitrary",)),
    )(act, dact_dg, dact_du, h1b, p2, xhat1, target, w_down, w_pg, wt_pe, g1, b1, g2, b2)


def _after(after, body):
    k = len(after)
    return (lambda *refs: body(*refs[k:])), [ANY_SPEC] * k


def _resident_spec(shape):
    return pl.BlockSpec(shape, lambda i: (0,) * len(shape), pipeline_mode=pl.Buffered(1))


def _dh1_ln1_bwd(dz2, dg, dup, dsb, xhat1, rstd1, wt_gate, wt_up, w_pg, w_out, g1, after=()):
    t, d = dz2.shape
    f = dg.shape[1]
    tm = _row_tile(t, EPILOGUE_ROWS)

    def body(dz_ref, dg_ref, du_ref, ds_ref, xh1_ref, rs1_ref, wg_ref, wu_ref, wpg_ref, wo_ref, g1_ref,
             dz1_ref, dz1b_ref, dr_ref, da_ref, acc_ref):
        @pl.when(pl.program_id(0) == 0)
        def _():
            acc_ref[...] = jnp.zeros_like(acc_ref)

        for lo, hi in _sub_rows(tm):
            dh = (ALPHA * dz_ref[lo:hi, :] + _dot_nn(dg_ref[lo:hi, :], wg_ref[...])
                  + _dot_nn(du_ref[lo:hi, :], wu_ref[...]) + _dot_nt(ds_ref[lo:hi, :], wpg_ref[...]))
            xhat, rstd = xh1_ref[lo:hi, :], rs1_ref[lo:hi, 0:1]
            dz1 = _layer_norm_bwd(dh * g1_ref[...], xhat, rstd)
            dz1b = dz1.astype(BF16)
            dz1_ref[lo:hi, :] = dz1
            dz1b_ref[lo:hi, :] = dz1b
            acc_ref[0:1, :] += jnp.sum(dh * xhat, axis=0, keepdims=True)
            acc_ref[1:2, :] += jnp.sum(dh, axis=0, keepdims=True)
            dr_ref[lo:hi, :] = _dot_nt(dz1b, wo_ref[0:RET_W, :]).astype(BF16)
            da_ref[lo:hi, :] = _dot_nt(dz1b, wo_ref[RET_W:RET_W + ATT_W, :]).astype(BF16)

    body, lead = _after(after, body)
    return pl.pallas_call(
        body, name="dh1_ln1_bwd", grid=(t // tm,),
        in_specs=lead + [_row_spec(tm, d), _row_spec(tm, f), _row_spec(tm, f), _row_spec(tm, d), _row_spec(tm, d),
                         _row_spec(tm, LANES), _resident_spec(wt_gate.shape), _resident_spec(wt_up.shape), _resident_spec(w_pg.shape),
                         _resident_spec(w_out.shape), _full_spec(g1.shape)],
        out_specs=[_row_spec(tm, d), _row_spec(tm, d), _row_spec(tm, RET_W), _row_spec(tm, ATT_W), _acc_spec((8, d))],
        out_shape=[SDS((t, d), F32), SDS((t, d), BF16), SDS((t, RET_W), BF16), SDS((t, ATT_W), BF16),
                   SDS((8, d), F32)],
        compiler_params=_params(("arbitrary",)),
    )(*after, dz2, dg, dup, dsb, xhat1, rstd1, wt_gate, wt_up, w_pg, w_out, g1)


def _in_proj_bwd(dz1, parts, wt_in, after=()):
    t, d = dz1.shape
    tm = _row_tile(t, MATMUL_ROWS)
    widths = [p.shape[1] for p in parts]

    def body(*refs):
        dz_ref, part_refs, w_ref, dx_ref = refs[0], refs[1:1 + len(parts)], refs[-2], refs[-1]
        acc = ALPHA * dz_ref[...]
        lo = 0
        for p_ref, w in zip(part_refs, widths):
            acc = acc + _dot_nn(p_ref[...], w_ref[lo:lo + w, :])
            lo += w
        dx_ref[...] = acc

    body, lead = _after(after, body)
    return pl.pallas_call(
        body, name="in_proj_bwd", grid=(t // tm,),
        in_specs=lead + [_row_spec(tm, d)] + [_row_spec(tm, w) for w in widths] + [_full_spec(wt_in.shape)],
        out_specs=_row_spec(tm, d), out_shape=SDS((t, d), F32),
        compiler_params=_params(("parallel",)),
    )(*after, dz1, *parts, wt_in)


def _weight_grad(name, me, parts, rhs, after=()):
    t, n = rhs.shape
    widths = [p.shape[1] for p in parts]
    rows = sum(widths)
    own_rows = rows // N_DEV
    tk = _row_tile(t, MATMUL_ROWS)
    n_steps = t // tk
    step = 256

    def body(*refs):
        me_ref, part_refs, rhs_ref = refs[0], refs[1:1 + len(parts)], refs[1 + len(parts)]
        full_ref, own_ref, acc = refs[-3], refs[-2], refs[-1]
        i = pl.program_id(0)

        def products(first):
            b = rhs_ref[...].astype(BF16)
            lo = 0
            for p_ref, w in zip(part_refs, widths):
                for c0 in range(0, w, step):
                    c1 = min(c0 + step, w)
                    val = _dot_tn(p_ref[:, c0:c1].astype(BF16), b)
                    if first:
                        acc[lo + c0:lo + c1, :] = val
                    else:
                        acc[lo + c0:lo + c1, :] += val
                lo += w

        pl.when(i == 0)(functools.partial(products, True))
        pl.when(i > 0)(functools.partial(products, False))

        @pl.when(i == n_steps - 1)
        def _():
            full_ref[...] = acc[...].astype(BF16)
            own_ref[...] = acc[pl.ds(pl.multiple_of(me_ref[0] * own_rows, 8), own_rows), :]

    body, lead = _after(after, body)
    return pl.pallas_call(
        body, name=name, grid=(n_steps,),
        in_specs=lead + [_smem_spec()] + [_row_spec(tk, w) for w in widths] + [_row_spec(tk, n)],
        out_specs=[_full_spec((rows, n)), _full_spec((own_rows, n))],
        out_shape=[SDS((rows, n), BF16), SDS((own_rows, n), F32)],
        scratch_shapes=[pltpu.VMEM((rows, n), F32)],
        compiler_params=_params(("arbitrary",)),
    )(*after, me, *parts, rhs)


def _weight_grad_jobs(name, me, jobs, after=()):
    count = len(jobs)
    t = jobs[0][0].shape[0]
    tk = _row_tile(t, MATMUL_ROWS)
    n_steps = t // tk
    shapes = [(lhs.shape[1], rhs.shape[1]) for lhs, rhs in jobs]
    most_rows, most_cols = max(r for r, _ in shapes), max(n for _, n in shapes)
    step = 256

    def body(*refs):
        me_ref, lhs_refs, rhs_refs = refs[0], refs[1:1 + count], refs[1 + count:1 + 2 * count]
        full_refs, own_refs = refs[1 + 2 * count:1 + 3 * count], refs[1 + 3 * count:1 + 4 * count]
        acc, whole, mine, sems = refs[1 + 4 * count:]
        job, i = pl.program_id(0), pl.program_id(1)

        def leaving(j):
            rows, n = shapes[j]
            return (pltpu.make_async_copy(whole.at[0:rows, 0:n], full_refs[j], sems.at[0]),
                    pltpu.make_async_copy(mine.at[0:rows // N_DEV, 0:n], own_refs[j], sems.at[1]))

        def products(j, first):
            rows, n = shapes[j]
            b = rhs_refs[j][...].astype(BF16)
            for c0 in range(0, rows, step):
                c1 = min(c0 + step, rows)
                val = _dot_tn(lhs_refs[j][:, c0:c1].astype(BF16), b)
                if first:
                    acc[c0:c1, 0:n] = val
                else:
                    acc[c0:c1, 0:n] += val

        def finish(j):
            rows, n = shapes[j]
            own_rows = rows // N_DEV
            if j > 0:
                for cp in leaving(j - 1):
                    cp.wait()
            whole[0:rows, 0:n] = acc[0:rows, 0:n].astype(BF16)
            mine[0:own_rows, 0:n] = acc[pl.ds(pl.multiple_of(me_ref[0] * own_rows, 8), own_rows), 0:n]
            for cp in leaving(j):
                cp.start()
            if j == count - 1:
                for cp in leaving(j):
                    cp.wait()

        for j in range(count):
            pl.when((job == j) & (i == 0))(functools.partial(products, j, True))
            pl.when((job == j) & (i > 0))(functools.partial(products, j, False))
            pl.when((job == j) & (i == n_steps - 1))(functools.partial(finish, j))

    def turn(j):
        return lambda job, i: (jnp.where(job == j, i, jnp.where(job < j, 0, n_steps - 1)), 0)

    body, lead = _after(after, body)
    res = pl.pallas_call(
        body, name=name, grid=(count, n_steps),
        in_specs=lead + [_smem_spec()] + [pl.BlockSpec((tk, rows), turn(j)) for j, (rows, _) in enumerate(shapes)]
        + [pl.BlockSpec((tk, n), turn(j)) for j, (_, n) in enumerate(shapes)],
        out_specs=[ANY_SPEC] * (2 * count),
        out_shape=[SDS((rows, n), BF16) for rows, n in shapes] + [SDS((rows // N_DEV, n), F32) for rows, n in shapes],
        scratch_shapes=[pltpu.VMEM((most_rows, most_cols), F32), pltpu.VMEM((most_rows, most_cols), BF16),
                        pltpu.VMEM((most_rows // N_DEV, most_cols), F32), pltpu.SemaphoreType.DMA((2,))],
        compiler_params=_params(("arbitrary", "arbitrary")),
    )(*after, me, *[lhs for lhs, _ in jobs], *[rhs for _, rhs in jobs])
    return list(res[:count]), list(res[count:])


def _log_decay(decay_f, decay_b):
    def body(f_ref, b_ref, lf_ref, lb_ref):
        lf_ref[...] = jnp.log1p(-jnp.exp2(f_ref[...]))
        lb_ref[...] = jnp.log1p(-jnp.exp2(b_ref[...]))

    return pl.pallas_call(body, name="log_decay", out_shape=[SDS(decay_f.shape, F32)] * 2)(decay_f, decay_b)


def _chunk(ref, n):
    return ref[pl.ds(pl.multiple_of(n * CHUNK, CHUNK), CHUNK), :]


def _group_sum(is_a, v):
    sa = jnp.sum(jnp.where(is_a, v, 0.0), axis=1, keepdims=True)
    sb = jnp.sum(jnp.where(is_a, 0.0, v), axis=1, keepdims=True)
    return jnp.where(is_a, sa, sb)


def _seq_spec(s, col_block):
    return pl.BlockSpec((s, LANES), lambda b, h: (b, col_block + h))


def _smem_spec():
    return pl.BlockSpec(memory_space=pltpu.SMEM)


RET_UNROLL = 4


def _chunk_loop(n_chunks, body, init):
    u = RET_UNROLL if n_chunks % RET_UNROLL == 0 else 1

    def trip(i, carry):
        for j in range(u):
            carry = body(i * u + j, carry)
        return carry

    return lax.fori_loop(0, n_chunks // u, trip, init)


def _stacked_tables(lgf_ref, lgb_ref, pair):
    lane = lax.broadcasted_iota(jnp.int32, (1, LANES), 1)
    is_a = lane < HEAD_DIM
    lgf = jnp.where(is_a, lgf_ref[2 * pair], lgf_ref[2 * pair + 1])
    lgb = jnp.where(is_a, lgb_ref[2 * pair], lgb_ref[2 * pair + 1])
    row = lax.broadcasted_iota(jnp.int32, (CHUNK, 1), 0).astype(F32)
    kdec_f, qdec_f = jnp.exp(lgf * (CHUNK - 1.0 - row)), jnp.exp(lgf * (row + 1.0))
    kdec_b, qdec_b = jnp.exp(lgb * row), jnp.exp(lgb * (CHUNK - row))
    tab = dict(
        is_a=is_a, row=row, lam_f=jnp.exp(lgf * CHUNK), lam_b=jnp.exp(lgb * CHUNK),
        kdec=jnp.concatenate([kdec_f, kdec_b], axis=1), qdec=jnp.concatenate([qdec_f, qdec_b], axis=1),
        qexp=jnp.concatenate([jnp.broadcast_to(row + 1.0, (CHUNK, LANES)),
                              jnp.broadcast_to(CHUNK - row, (CHUNK, LANES))], axis=1),
        kexp=jnp.concatenate([jnp.broadcast_to(CHUNK - 1.0 - row, (CHUNK, LANES)),
                              jnp.broadcast_to(row, (CHUNK, LANES))], axis=1),
    )
    r = lax.broadcasted_iota(jnp.int32, (2 * LANES, LANES), 0)
    c = lax.broadcasted_iota(jnp.int32, (2 * LANES, LANES), 1)
    tab["diag2"] = ((r & (LANES - 1)) < HEAD_DIM) == (c < HEAD_DIM)
    i2 = lax.broadcasted_iota(jnp.int32, (2 * CHUNK, CHUNK), 0)
    j = lax.broadcasted_iota(jnp.int32, (2 * CHUNK, CHUNK), 1)
    head_b = i2 >= CHUNK
    diff = ((i2 & (CHUNK - 1)) - j).astype(F32)
    up, dn = jnp.maximum(diff, 0.0), jnp.maximum(-diff, 0.0)
    lgf2 = jnp.where(head_b, lgf_ref[2 * pair + 1], lgf_ref[2 * pair])
    lgb2 = jnp.where(head_b, lgb_ref[2 * pair + 1], lgb_ref[2 * pair])
    ef = jnp.where(diff >= 0, jnp.exp(lgf2 * up), 0.0)
    eb = jnp.where(diff <= 0, jnp.exp(lgb2 * dn), 0.0)
    tab["d2"] = ef + eb
    tab["df2"] = ef * up
    tab["db2"] = eb * dn
    return tab


def _stack_pair(is_a, x):
    zero = jnp.zeros_like(x)
    return jnp.concatenate([jnp.where(is_a, x, zero), jnp.where(is_a, zero, x)], axis=0)


def _unstack_pair(is_a, x2):
    return jnp.where(is_a, x2[0:CHUNK, :], x2[CHUNK:2 * CHUNK, :])


def _both_ways(x, dec):
    return (jnp.concatenate([x, x], axis=1) * dec).astype(BF16)


def _scan_states(n_chunks, st, up_rows, up_lam, down_rows, down_lam):
    zero = jnp.zeros((LANES, LANES), F32)

    def up(n, r):
        new = st[n, up_rows, :]
        st[n, up_rows, :] = r
        return r * up_lam + new

    def down(s, r):
        n = n_chunks - 1 - s
        new = st[n, down_rows, :]
        st[n, down_rows, :] = r
        return r * down_lam + new

    lax.fori_loop(0, n_chunks, up, zero)
    lax.fori_loop(0, n_chunks, down, zero)


FWD_ROWS, BWD_ROWS = pl.ds(0, LANES), pl.ds(LANES, LANES)


def _state_spec(n_chunks, pairs):
    return pl.BlockSpec((n_chunks, 2 * LANES, LANES), lambda b, h: (b * pairs + h, 0, 0))


ST_GAIN, ST_XF, ST_XB, ST_IFA, ST_IFB, ST_IBA, ST_IBB, ST_LF, ST_LB = 0, 1, 2, 3, 4, 5, 6, 8, 9
ST_ROWS = 16


GW = GROUP * HEAD_DIM
KEYS = 3 * BLOCK


def _attn_tables(g, bias_ref):
    r = lax.broadcasted_iota(jnp.int32, (GROUP * BLOCK, KEYS), 0)
    kj = lax.broadcasted_iota(jnp.int32, (GROUP * BLOCK, KEYS), 1)
    qi = r & (BLOCK - 1)
    hh = lax.shift_right_logical(r, 7)
    dist = jnp.abs(kj - BLOCK - qi)
    slope = jnp.exp2(-(GROUP * g + hh + 1).astype(F32) * (8.0 / ATTN_HEADS))
    bias_ref[...] = jnp.where(dist <= BLOCK, -slope * dist.astype(F32), NEG_INF)


def _own_lanes(g):
    return lax.shift_right_logical(lax.broadcasted_iota(jnp.int32, (1, LANES), 1), 6) == g


def _mask_keys(x_ref, g, scale, pad_ref, s):
    pad_ref[0:BLOCK, :] = jnp.zeros((BLOCK, LANES), BF16)
    pad_ref[BLOCK + s:2 * BLOCK + s, :] = jnp.zeros((BLOCK, LANES), BF16)
    pad_ref[BLOCK:BLOCK + s, :] = jnp.where(_own_lanes(g), x_ref[...].astype(F32) * scale, 0.0).astype(BF16)


def _lane_block(x, j):
    return x[:, j * LANES:(j + 1) * LANES]


def _stack_heads(x, g):
    assert GROUP == 4 and GW == 2 * LANES
    x1 = pltpu.roll(x, HEAD_DIM, 1)
    keep = _own_lanes(g)
    zero = jnp.zeros((BLOCK, LANES), x.dtype)
    rows = []
    for h in range(GROUP):
        for_g0 = _lane_block(x, h // 2) if h % 2 == 0 else _lane_block(x1, ((h + 1) // 2) % 2)
        for_g1 = _lane_block(x, h // 2) if h % 2 == 1 else _lane_block(x1, h // 2)
        rows.append(jnp.where(keep, jnp.where(g == 0, for_g0, for_g1), zero))
    return jnp.concatenate(rows, axis=0)


def _unstack_heads(x4, g):
    p = [x4[h * BLOCK:(h + 1) * BLOCK, :] for h in range(GROUP)]
    cat = lambda a, b: jnp.concatenate([a, b], axis=1)
    in_place = jnp.where(g == 0, cat(p[0], p[2]), cat(p[1], p[3]))
    one_left = jnp.where(g == 0, cat(p[1], p[3]), cat(p[2], p[0]))
    return in_place + pltpu.roll(one_left, HEAD_DIM, 1)


def _sink_column(sink_ref, g):
    rh = lax.shift_right_logical(lax.broadcasted_iota(jnp.int32, (GROUP * BLOCK, 1), 0), 7)
    col = jnp.zeros((GROUP * BLOCK, 1), F32)
    for h in range(GROUP):
        col = jnp.where(rh == h, sink_ref[GROUP * g + h], col)
    return col


def _attn_probs(qm, k3, bias_ref, sink_col, n, s):
    logits = _dot_nt(qm, k3) + bias_ref[...]
    kpos = n * BLOCK - BLOCK + lax.broadcasted_iota(jnp.int32, (1, KEYS), 1)
    logits = jnp.where((kpos >= 0) & (kpos < s), logits, NEG_INF)
    m = jnp.maximum(jnp.max(logits, axis=1, keepdims=True), sink_col)
    e = jnp.exp(logits - m)
    e_sink = jnp.exp(sink_col - m)
    inv = 1.0 / (jnp.sum(e, axis=1, keepdims=True) + e_sink)
    return e * inv, e_sink * inv


PAIRS_PER_KV = (RET_HEADS // 2) // KV_HEADS
FWD_ORDER = "rrarra"
BWD_ORDER = "rararr"


def _trip_order(order, chunks, blocks):
    if order.count("r") == chunks and order.count("a") == blocks:
        return order
    return "r" * chunks + "a" * blocks


def _mixers_fwd(u, lgf, lgb, gn_gain, sink, b_loc, after=()):
    t = u.shape[0]
    s = t // b_loc
    n_chunks = s // CHUNK
    pairs = RET_HEADS // 2
    trips = n_chunks // RET_UNROLL
    blocks_half = (s // BLOCK) // PAIRS_PER_KV
    per_trip = blocks_half // trips
    assert n_chunks % RET_UNROLL == 0 and blocks_half % trips == 0 and PAIRS_PER_KV == 2

    def body(lgf_ref, lgb_ref, sink_ref, q_ref, k_ref, v_ref, g_ref, gain_ref, aq_ref, ak_ref, av_ref,
             r_ref, xhat_ref, rstd_ref, a_ref, st, kpad, vpad, bias):
        pair = pl.program_id(1)
        g, half = lax.shift_right_logical(pair, 1), pair & 1
        tab = _stacked_tables(lgf_ref, lgb_ref, pair)
        is_a = tab["is_a"]

        @pl.when(half == 0)
        def _():
            _attn_tables(g, bias)
            _mask_keys(ak_ref, g, Q_SCALE, kpad, s)
            _mask_keys(av_ref, g, 1.0, vpad, s)

        def kv_body(n, _):
            k8 = _chunk(k_ref, n).astype(F32) * Q_SCALE
            st[n] = jnp.where(tab["diag2"], _dot_tn(_both_ways(k8, tab["kdec"]), _chunk(v_ref, n)), 0.0)
            return 0

        _chunk_loop(n_chunks, kv_body, 0)
        _scan_states(n_chunks, st, FWD_ROWS, tab["lam_f"], BWD_ROWS, tab["lam_b"])
        sink_col = _sink_column(sink_ref, g)

        def retention_chunk(n):
            q = _chunk(q_ref, n)
            k8 = (_chunk(k_ref, n).astype(F32) * Q_SCALE).astype(BF16)
            v = _chunk(v_ref, n)
            p2 = (_dot_nt(_stack_pair(is_a, q), k8) * tab["d2"]).astype(BF16)
            y = _unstack_pair(is_a, _dot_nn(p2, v))
            y = y + _dot_nn(_both_ways(q.astype(F32), tab["qdec"]), st[n].astype(BF16))
            rows = pl.ds(pl.multiple_of(n * CHUNK, CHUNK), CHUNK)
            mu = _group_sum(is_a, y) * (1.0 / HEAD_DIM)
            dlt = y - mu
            var = _group_sum(is_a, dlt * dlt) * (1.0 / HEAD_DIM)
            rstd = lax.rsqrt(var + GN_EPS)
            xhat = dlt * rstd
            xhat_ref[rows, :] = xhat
            rstd_ref[rows, :] = rstd
            gate = _chunk(g_ref, n).astype(F32)
            r_ref[rows, :] = (xhat * gain_ref[...] * gate * _sigmoid(gate)).astype(BF16)

        def attention_block(blk):
            n = half * blocks_half + blk
            rows = pl.ds(pl.multiple_of(blk * BLOCK, BLOCK), BLOCK)
            keys = pl.ds(pl.multiple_of(n * BLOCK, BLOCK), KEYS)
            p, _ = _attn_probs(_stack_heads(aq_ref[rows, :], g), kpad[keys, :], bias, sink_col, n, s)
            a_ref[rows, :] = _unstack_heads(_dot_nn(p.astype(BF16), vpad[keys, :]), g).astype(BF16)

        def trip(i, _):
            chunk, blk = 0, 0
            for kind in _trip_order(FWD_ORDER, RET_UNROLL, per_trip):
                if kind == "r":
                    retention_chunk(i * RET_UNROLL + chunk)
                    chunk += 1
                else:
                    attention_block(i * per_trip + blk)
                    blk += 1
            return 0

        lax.fori_loop(0, trips, trip, 0)

    lane_blk = lambda c0: _seq_spec(s, c0 // LANES)
    half_rows = blocks_half * BLOCK
    aq_spec = pl.BlockSpec((half_rows, GW), lambda b, h: (b * PAIRS_PER_KV + (h & 1), C_AQ // GW + h // 2))
    a_spec = pl.BlockSpec((half_rows, GW), lambda b, h: (b * PAIRS_PER_KV + (h & 1), h // 2))
    kv_spec = lambda c0: pl.BlockSpec((s, LANES), lambda b, h: (b, c0 // LANES))
    pad = pltpu.VMEM((s + 2 * BLOCK, LANES), BF16)
    body, lead = _after(after, body)
    return pl.pallas_call(
        body, name="mixers_fwd", grid=(b_loc, pairs),
        in_specs=lead + [_smem_spec(), _smem_spec(), _smem_spec(), lane_blk(C_RQ), lane_blk(C_RK), lane_blk(C_RV),
                         lane_blk(C_RG), pl.BlockSpec((1, LANES), lambda b, h: (0, h)), aq_spec, kv_spec(C_AK),
                         kv_spec(C_AV)],
        out_specs=[_seq_spec(s, 0), _seq_spec(s, 0), _seq_spec(s, 0), a_spec, _state_spec(n_chunks, pairs)],
        out_shape=[SDS((t, RET_W), BF16), SDS((t, RET_W), F32), SDS((t, RET_W), F32), SDS((t, ATT_W), BF16),
                   SDS((b_loc * pairs * n_chunks, 2 * LANES, LANES), F32)],
        scratch_shapes=[pad, pad, pltpu.VMEM((GROUP * BLOCK, KEYS), F32)],
        compiler_params=_params(("arbitrary", "arbitrary")),
    )(*after, lgf, lgb, sink, u, u, u, u, gn_gain, u, u, u)


def _mixers_bwd(u, xhat, rstd, states, dr, da, lgf, lgb, gn_gain, sink, b_loc, after=()):
    t = u.shape[0]
    s = t // b_loc
    n_chunks = s // CHUNK
    pairs = RET_HEADS // 2
    trips = n_chunks // RET_UNROLL
    blocks_half = (s // BLOCK) // PAIRS_PER_KV
    per_trip = blocks_half // trips
    assert n_chunks % RET_UNROLL == 0 and blocks_half % trips == 0 and PAIRS_PER_KV == 2

    def body(lgf_ref, lgb_ref, sink_ref, q_ref, k_ref, v_ref, g_ref, xhat_ref, rstd_ref, dr_ref, gain_ref,
             aq_ref, ak_ref, av_ref, do_ref, st,
             dq_ref, dk_ref, dv_ref, dg_ref, st_ref, daq_ref, dak_ref, dav_ref, dsink_ref,
             gr, dy_s, kpad, vpad, bias, dk_acc, dv_acc):
        pair = pl.program_id(1)
        g, half = lax.shift_right_logical(pair, 1), pair & 1
        tab = _stacked_tables(lgf_ref, lgb_ref, pair)
        is_a = tab["is_a"]
        gain = gain_ref[...]

        @pl.when(half == 0)
        def _():
            _attn_tables(g, bias)
            _mask_keys(ak_ref, g, Q_SCALE, kpad, s)
            _mask_keys(av_ref, g, 1.0, vpad, s)
            dsink_ref[...] = jnp.zeros_like(dsink_ref)

        @pl.when(pair == 0)
        def _():
            dk_acc[...] = jnp.zeros_like(dk_acc)
            dv_acc[...] = jnp.zeros_like(dv_acc)

        def norm_body(n, dgain):
            rows = pl.ds(pl.multiple_of(n * CHUNK, CHUNK), CHUNK)
            xhat, rstd = xhat_ref[rows, :], rstd_ref[rows, :]
            gate = g_ref[rows, :].astype(F32)
            sg = _sigmoid(gate)
            silu = gate * sg
            d_out = dr_ref[rows, :].astype(F32)
            dg_ref[rows, :] = (d_out * xhat * gain * (sg * (1.0 + gate * (1.0 - sg)))).astype(BF16)
            dxh = d_out * gain * silu
            m1 = _group_sum(is_a, dxh) * (1.0 / HEAD_DIM)
            m2 = _group_sum(is_a, dxh * xhat) * (1.0 / HEAD_DIM)
            dy = (rstd * (dxh - m1 - xhat * m2)).astype(BF16)
            dy_s[rows, :] = dy
            qf = q_ref[rows, :].astype(F32)
            gr[n] = jnp.where(tab["diag2"], _dot_tn(_both_ways(qf, tab["qdec"]), dy), 0.0)
            return dgain + jnp.sum(d_out * xhat * silu, axis=0, keepdims=True)

        colsum = lambda x: jnp.sum(x, axis=0, keepdims=True)

        def grad_body(n, carry):
            xfb, ifa, ifb, iba, ibb, lf, lb = carry
            rows = pl.ds(pl.multiple_of(n * CHUNK, CHUNK), CHUNK)
            q = q_ref[rows, :]
            qf = q.astype(F32)
            k8f = k_ref[rows, :].astype(F32) * Q_SCALE
            k8 = k8f.astype(BF16)
            v = v_ref[rows, :]
            dy = dy_s[rows, :]
            q2, dy2 = _stack_pair(is_a, q), _stack_pair(is_a, dy)
            sc = _dot_nt(q2, k8)
            dp = _dot_nt(dy2, v)
            a2 = (sc * tab["d2"]).astype(BF16)
            ds2 = (dp * tab["d2"]).astype(BF16)
            dq = _unstack_pair(is_a, _dot_nn(ds2, k8))
            dk = _dot_tn(ds2, q2)
            dv = _dot_tn(a2, dy2)
            prod = sc * dp
            pf, pb = prod * tab["df2"], prod * tab["db2"]
            ifa, ifb = ifa + colsum(pf[0:CHUNK, :]), ifb + colsum(pf[CHUNK:2 * CHUNK, :])
            iba, ibb = iba + colsum(pb[0:CHUNK, :]), ibb + colsum(pb[CHUNK:2 * CHUNK, :])
            states, sgrads = st[n], gr[n]
            sb, gb = states.astype(BF16), sgrads.astype(BF16)
            dqc = _dot_nt(dy, sb) * tab["qdec"]
            dkc = _dot_nt(v, gb) * tab["kdec"]
            dv = dv + _dot_nn(_both_ways(k8f, tab["kdec"]), gb)
            dq_ref[rows, :] = (dq + dqc[:, 0:LANES] + dqc[:, LANES:2 * LANES]).astype(BF16)
            dk_ref[rows, :] = ((dk + dkc[:, 0:LANES] + dkc[:, LANES:2 * LANES]) * Q_SCALE).astype(BF16)
            dv_ref[rows, :] = dv.astype(BF16)
            q2w, k2w = jnp.concatenate([qf, qf], axis=1), jnp.concatenate([k8f, k8f], axis=1)
            xfb = xfb + colsum(tab["qexp"] * q2w * dqc + tab["kexp"] * k2w * dkc)
            prod_s = sgrads * states
            lf, lb = lf + colsum(prod_s[0:LANES, :]), lb + colsum(prod_s[LANES:2 * LANES, :])
            return xfb, ifa, ifb, iba, ibb, lf, lb

        sink_col = _sink_column(sink_ref, g)
        head_row = lax.broadcasted_iota(jnp.int32, dsink_ref.shape, 0)

        def attention_block(blk):
            n = half * blocks_half + blk
            rows = pl.ds(pl.multiple_of(blk * BLOCK, BLOCK), BLOCK)
            keys = pl.ds(pl.multiple_of(n * BLOCK, BLOCK), KEYS)
            qm = _stack_heads(aq_ref[rows, :], g)
            k3, v3 = kpad[keys, :], vpad[keys, :]
            p, p_sink = _attn_probs(qm, k3, bias, sink_col, n, s)
            dom = _stack_heads(do_ref[rows, :], g)
            dp = _dot_nt(dom, v3)
            delta = jnp.sum(p * dp, axis=1, keepdims=True)
            ds_mat = (p * (dp - delta)).astype(BF16)
            daq_ref[rows, :] = _unstack_heads(_dot_nn(ds_mat, k3), g).astype(BF16)
            dk_acc[keys, :] += _dot_tn(ds_mat, qm) * Q_SCALE
            dv_acc[keys, :] += _dot_tn(p.astype(BF16), dom)
            w = p_sink * delta
            upd = jnp.zeros(dsink_ref.shape, F32)
            for h in range(GROUP):
                upd = upd + jnp.where(head_row == h, -jnp.sum(w[h * BLOCK:(h + 1) * BLOCK, :]), 0.0)
            dsink_ref[...] += upd

        dgain = _chunk_loop(n_chunks, norm_body, jnp.zeros((1, LANES), F32))
        _scan_states(n_chunks, gr, BWD_ROWS, tab["lam_b"], FWD_ROWS, tab["lam_f"])

        def trip(i, carry):
            chunk, blk = 0, 0
            for kind in _trip_order(BWD_ORDER, RET_UNROLL, per_trip):
                if kind == "r":
                    carry = grad_body(i * RET_UNROLL + chunk, carry)
                    chunk += 1
                else:
                    attention_block(i * per_trip + blk)
                    blk += 1
            return carry

        z = jnp.zeros((1, LANES), F32)
        init = (jnp.zeros((1, 2 * LANES), F32), z, z, z, z, z, z)
        xfb, ifa, ifb, iba, ibb, lf, lb = lax.fori_loop(0, trips, trip, init)
        st_ref[...] = jnp.zeros_like(st_ref)
        st_ref[ST_GAIN:ST_GAIN + 1, :] = dgain
        st_ref[ST_XF:ST_XF + 1, :] = xfb[:, 0:LANES]
        st_ref[ST_XB:ST_XB + 1, :] = xfb[:, LANES:2 * LANES]
        st_ref[ST_IFA:ST_IFA + 1, :] = ifa
        st_ref[ST_IFB:ST_IFB + 1, :] = ifb
        st_ref[ST_IBA:ST_IBA + 1, :] = iba
        st_ref[ST_IBB:ST_IBB + 1, :] = ibb
        st_ref[ST_LF:ST_LF + 1, :] = lf * (CHUNK * tab["lam_f"])
        st_ref[ST_LB:ST_LB + 1, :] = lb * (CHUNK * tab["lam_b"])

        @pl.when(pair == pairs - 1)
        def _():
            dak_ref[...] = dk_acc[BLOCK:BLOCK + s, :].astype(BF16)
            dav_ref[...] = dv_acc[BLOCK:BLOCK + s, :].astype(BF16)

    lane_blk = lambda c0: _seq_spec(s, c0 // LANES)
    seq0 = _seq_spec(s, 0)
    half_rows = blocks_half * BLOCK
    aq_spec = pl.BlockSpec((half_rows, GW), lambda b, h: (b * PAIRS_PER_KV + (h & 1), C_AQ // GW + h // 2))
    a_spec = pl.BlockSpec((half_rows, GW), lambda b, h: (b * PAIRS_PER_KV + (h & 1), h // 2))
    kv_spec = lambda c0: pl.BlockSpec((s, LANES), lambda b, h: (b, c0 // LANES))
    kv_out = pl.BlockSpec((s, LANES), lambda b, h: (b, 0))
    state = pltpu.VMEM((n_chunks, 2 * LANES, LANES), F32)
    pad = pltpu.VMEM((s + 2 * BLOCK, LANES), BF16)
    acc = pltpu.VMEM((s + 2 * BLOCK, LANES), F32)
    body, lead = _after(after, body)
    return pl.pallas_call(
        body, name="mixers_bwd", grid=(b_loc, pairs),
        in_specs=lead + [_smem_spec(), _smem_spec(), _smem_spec(), lane_blk(C_RQ), lane_blk(C_RK), lane_blk(C_RV),
                         lane_blk(C_RG), seq0, seq0, seq0, pl.BlockSpec((1, LANES), lambda b, h: (0, h)),
                         aq_spec, kv_spec(C_AK), kv_spec(C_AV), a_spec, _state_spec(n_chunks, pairs)],
        out_specs=[seq0] * 4 + [pl.BlockSpec((ST_ROWS, LANES), lambda b, h: (b, h)), a_spec, kv_out, kv_out,
                                pl.BlockSpec((8, LANES), lambda b, h: (b * KV_HEADS + h // 2, 0))],
        out_shape=[SDS((t, RET_W), BF16)] * 4 + [SDS((b_loc * ST_ROWS, RET_W), F32), SDS((t, ATT_W), BF16),
                                                   SDS((t, KV_W), BF16), SDS((t, KV_W), BF16),
                                                   SDS((b_loc * KV_HEADS * 8, LANES), F32)],
        scratch_shapes=[state, pltpu.VMEM((s, LANES), BF16), pad, pad,
                        pltpu.VMEM((GROUP * BLOCK, KEYS), F32), acc, acc],
        compiler_params=_params(("arbitrary", "arbitrary")),
    )(*after, lgf, lgb, sink, u, u, u, u, xhat, rstd, dr, gn_gain, u, u, u, da, states)


def _pack_small(acc2, acc1, ret_stats, dsink, b_loc, d):
    pairs = RET_HEADS // 2

    def body(acc2_ref, acc1_ref, st_ref, dsink_ref, out_ref):
        out_ref[...] = jnp.zeros_like(out_ref)
        out_ref[ROW_LN1G:ROW_LN1G + 1, :] = acc1_ref[0:1, :]
        out_ref[ROW_LN1B:ROW_LN1B + 1, :] = acc1_ref[1:2, :]
        out_ref[ROW_LN2G:ROW_LN2G + 1, :] = acc2_ref[1:2, :]
        out_ref[ROW_LN2B:ROW_LN2B + 1, :] = acc2_ref[2:3, :]
        out_ref[ROW_LOSS:ROW_LOSS + 1, :] = acc2_ref[0:1, :]
        st = st_ref[0:ST_ROWS, :]
        for b in range(1, b_loc):
            st = st + st_ref[b * ST_ROWS:(b + 1) * ST_ROWS, :]
        out_ref[ROW_GN:ROW_GN + 1, 0:RET_W] = st[ST_GAIN:ST_GAIN + 1, :]
        lane = lax.broadcasted_iota(jnp.int32, (1, d), 1)
        misc = jnp.zeros((1, d), F32)
        for pr in range(pairs):
            blk = st[:, pr * LANES:(pr + 1) * LANES]
            half = lax.broadcasted_iota(jnp.int32, (1, LANES), 1) < HEAD_DIM
            for h in range(2):
                sel = half if h == 0 else jnp.logical_not(half)
                cross_f = jnp.sum(jnp.where(sel, blk[ST_XF:ST_XF + 1, :] + blk[ST_LF:ST_LF + 1, :], 0.0))
                cross_b = jnp.sum(jnp.where(sel, blk[ST_XB:ST_XB + 1, :] + blk[ST_LB:ST_LB + 1, :], 0.0))
                intra_f = jnp.sum(blk[ST_IFA + h:ST_IFA + h + 1, :])
                intra_b = jnp.sum(blk[ST_IBA + h:ST_IBA + h + 1, :])
                head = 2 * pr + h
                misc = jnp.where(lane == MISC_DF + head, cross_f + intra_f, misc)
                misc = jnp.where(lane == MISC_DB + head, cross_b + intra_b, misc)
        for g in range(KV_HEADS):
            tot = dsink_ref[g * 8:(g + 1) * 8, :]
            for b in range(1, b_loc):
                tot = tot + dsink_ref[(b * KV_HEADS + g) * 8:(b * KV_HEADS + g + 1) * 8, :]
            for h in range(GROUP):
                misc = jnp.where(lane == MISC_SINK + GROUP * g + h, jnp.sum(tot[h:h + 1, 0:1]), misc)
        out_ref[ROW_MISC:ROW_MISC + 1, :] = misc

    return pl.pallas_call(body, name="pack_small", out_shape=SDS((SMALL_ROWS, d), F32))(acc2, acc1, ret_stats, dsink)


BIG = ("w_in", "w_out", "w_ffn_gate", "w_ffn_up", "w_ffn_down", "w_ple_proj", "w_ple_gate")
TRANSPOSED_OUTSIDE = ("w_in", "w_ffn_gate", "w_ffn_up")
TRANSPOSED_HERE = ("w_ple_proj",)
SMALL = ("ret_decay_fwd", "ret_decay_bwd", "ret_gn_gain", "attn_sink", "ln1_gain", "ln1_bias", "ln2_gain", "ln2_bias")
ORDER = ("w_in", "ret_decay_fwd", "ret_decay_bwd", "ret_gn_gain", "attn_sink", "w_out", "ln1_gain", "ln1_bias",
         "w_ffn_gate", "w_ffn_up", "w_ffn_down", "w_ple_proj", "w_ple_gate", "ln2_gain", "ln2_bias")


GATHER_ORDER = ("w_in", "w_ffn_up", "w_out", "w_ffn_gate", "w_ple_gate", "w_ple_proj", "w_ffn_down")
GATHER_TWO_LEVEL = ("w_in", "w_ffn_up")


def _local_step(x2, p2, target2, fetch, publish, small, b_loc, me):
    d = x2.shape[1]
    lgf, lgb = _log_decay(small["ret_decay_fwd"], small["ret_decay_bwd"])
    lgf1, lgb1, sink1 = lgf.reshape(-1), lgb.reshape(-1), small["attn_sink"].reshape(-1)
    (w_in,) = fetch(("w_in",), ())
    u, xb = _in_proj(x2, w_in)
    passed = fetch.pass_on("w_ffn_up", (xb,))
    r, ret_xhat, ret_rstd, a, ret_states = _mixers_fwd(u, lgf1, lgb1, small["ret_gn_gain"], sink1, b_loc, passed)
    w_out, w_gate, w_up = fetch(("w_out", "w_ffn_gate", "w_ffn_up"), (r, a))
    xhat1, rstd1, h1b, dact_dg, dact_du, act = _mix_ln1_ffn_up(
        r, a, x2, w_out, w_gate, w_up, small["ln1_gain"], small["ln1_bias"])
    w_pg, w_pe, w_down = fetch(("w_ple_gate", "w_ple_proj", "w_ffn_down"), (act,))
    dz2, dz2b, dsb, dpleb, dg, dup, acc2 = _ffn_down_ln2_loss(
        act, dact_dg, dact_du, h1b, p2, xhat1, target2, w_down, w_pg, w_pe,
        small["ln1_gain"], small["ln1_bias"], small["ln2_gain"], small["ln2_bias"])
    own = {}

    def grad(name, parts, rhs, after=()):
        whole, own[name] = _weight_grad("grad_" + name, me, parts, rhs, after)
        return whole

    ffn_jobs = dict(w_ffn_down=(act, dz2b), w_ple_proj=(dpleb, p2), w_ple_gate=(h1b, dsb),
                    w_ffn_gate=(dg, h1b), w_ffn_up=(dup, h1b))
    wholes, owns = _weight_grad_jobs("grad_w_ffn", me, list(ffn_jobs.values()))
    own.update(zip(ffn_jobs, owns))
    t2 = publish("ffn", dict(zip(ffn_jobs, wholes)))
    dz1, dz1b, dr, da, acc1 = _dh1_ln1_bwd(
        dz2, dg, dup, dsb, xhat1, rstd1, w_gate, w_up, w_pg, w_out, small["ln1_gain"], t2)
    t3 = publish("out", dict(w_out=grad("w_out", [r, a], dz1b)))
    dq, dk, dv, dgate, ret_stats, daq, dak, dav, dsink = _mixers_bwd(
        u, ret_xhat, ret_rstd, ret_states, dr, da, lgf1, lgb1, small["ret_gn_gain"], sink1, b_loc, t3)
    parts = [dq, dk, dv, dgate, daq, dak, dav]
    small_part = _pack_small(acc2, acc1, ret_stats, dsink, b_loc, d)
    t4 = publish("in", dict(w_in=grad("w_in", parts, xb)), small_part)
    grad_x = _in_proj_bwd(dz1, parts, w_in, t4)
    return grad_x, own, small_part


def kernel(x, p, w_in, ret_decay_fwd, ret_decay_bwd, ret_gn_gain, attn_sink, w_out, ln1_gain, ln1_bias, w_ffn_gate, w_ffn_up, w_ffn_down, w_ple_proj, w_ple_gate, ln2_gain, ln2_bias, loss_target, m_w_in, m_ret_decay_fwd, m_ret_decay_bwd, m_ret_gn_gain, m_attn_sink, m_w_out, m_ln1_gain, m_ln1_bias, m_w_ffn_gate, m_w_ffn_up, m_w_ffn_down, m_w_ple_proj, m_w_ple_gate, m_ln2_gain, m_ln2_bias, v_w_in, v_ret_decay_fwd, v_ret_decay_bwd, v_ret_gn_gain, v_attn_sink, v_w_out, v_ln1_gain, v_ln1_bias, v_w_ffn_gate, v_w_ffn_up, v_w_ffn_down, v_w_ple_proj, v_w_ple_gate, v_ln2_gain, v_ln2_bias):
    given = dict(locals())

    def strip(n, a):
        if n not in BIG:
            return a
        return a[0].T if n in TRANSPOSED_OUTSIDE else a[0]

    def restore(n, a):
        if n not in BIG:
            return a
        return (a.T if n in TRANSPOSED_OUTSIDE else a)[None]

    w = {n: strip(n, given[n]) for n in ORDER}
    m = {n: strip(n, given["m_" + n]) for n in ORDER}
    v = {n: strip(n, given["v_" + n]) for n in ORDER}
    b_loc, s, d = x.shape
    x2 = x.reshape(b_loc * s, d)
    p2 = p[0].reshape(b_loc * s, p.shape[-1])
    target2 = loss_target.reshape(b_loc * s, d)

    small = {n: w[n] for n in SMALL}
    me = (4 * lax.axis_index("x") + 2 * lax.axis_index("y") + lax.axis_index("c")).astype(jnp.int32).reshape(1)

    gather = _gather_start(
        {n: w[n] for n in GATHER_ORDER},
        [_gather_copy_near if n in GATHER_TWO_LEVEL else _gather_copy for n in GATHER_ORDER])

    passing = {}

    def pass_on(n, after):
        passing[n] = _gather_relay("gather_relay_" + n, gather, GATHER_ORDER.index(n), list(after))
        return (passing[n]["token"],)

    def fetch(names, after):
        out = {}
        for n in [n for n in names if n in GATHER_TWO_LEVEL]:
            if n not in passing:
                pass_on(n, after)
            out[n] = _split_copy_wait("gather_wait_" + n, passing[n], [0], list(after))[0][0]
        direct = [n for n in names if n not in GATHER_TWO_LEVEL]
        if direct:
            got = _split_copy_wait("gather_wait_" + direct[0], gather, [GATHER_ORDER.index(n) for n in direct],
                                   list(after))
            out.update({n: item[0] for n, item in zip(direct, got)})
        return [out[n] for n in names]

    scatters = []

    def publish(tag, products, small_sums=None):
        items = [(products[n], lax.empty((N_DEV - 1, products[n].shape[0] // N_DEV, products[n].shape[1]), BF16))
                 for n in products]
        copies = [_scatter_copy] * len(items)
        if small_sums is not None:
            items.append((small_sums, lax.empty((N_DEV - 1,) + small_sums.shape, F32)))
            copies.append(_small_copy)
        started = _split_copy_start("scatter_start_" + tag, items, copies)
        scatters.append((list(products), small_sums is not None, started))
        return (started["token"],)

    fetch.pass_on = pass_on
    grad_x, own, small_part = _local_step(x2, p2, target2, fetch, publish, small, b_loc, me)

    out_g, out_d, out_m, out_v = {}, {}, {}, {}
    after = [grad_x]
    for names, with_small, started in scatters:
        landed = _split_copy_wait("scatter_wait_" + names[0], started, list(range(len(started["items"]))), after)
        if with_small:
            mine, from_peers = landed[-1]
            loss, sg, sd, sm, sv = _small_adamw(
                me, mine, from_peers, small, {n: m[n] for n in SMALL}, {n: v[n] for n in SMALL})
            for dst, src in ((out_g, sg), (out_d, sd), (out_m, sm), (out_v, sv)):
                dst.update(src)
        recv = {n: item[1] for n, item in zip(names, landed)}
        alike = {}
        for n in names:
            alike.setdefault((own[n].shape, n in TRANSPOSED_HERE), []).append(n)
        for (_, transposed), ns in alike.items():
            res = _reduce_adamw(ns[0], [own[n] for n in ns], [recv[n] for n in ns], [w[n] for n in ns],
                                [m[n] for n in ns], [v[n] for n in ns], transposed)
            for dst, vals in zip((out_g, out_d, out_m, out_v), res):
                dst.update(zip(ns, vals))
        after = [out_v[names[-1]]]

    outs = [loss[0, 0], grad_x.reshape(x.shape)]
    for group in (out_g, out_d, out_m, out_v):
        outs += [restore(n, group[n]) for n in ORDER]
    return tuple(outs)
```

```python
import functools

import jax
import jax.numpy as jnp
from jax import lax
from jax.experimental import pallas as pl
from jax.experimental.pallas import tpu as pltpu

F32, BF16 = jnp.float32, jnp.bfloat16
SDS = jax.ShapeDtypeStruct
MESH = pl.DeviceIdType.MESH

N_DEV = 8
HEAD_DIM = 64
RET_HEADS = 8
ATTN_HEADS = 8
KV_HEADS = 2
GROUP = ATTN_HEADS // KV_HEADS
RET_W = RET_HEADS * HEAD_DIM
ATT_W = ATTN_HEADS * HEAD_DIM
KV_W = KV_HEADS * HEAD_DIM
LANES = 128
CHUNK = 128
BLOCK = 128
Q_SCALE = HEAD_DIM ** -0.5
ALPHA = 2.0 ** 0.25
LN_EPS = 1e-5
GN_EPS = 1e-5
NEG_INF = -1e30
C_RQ, C_RK, C_RV, C_RG = 0, RET_W, 2 * RET_W, 3 * RET_W
C_AQ = 4 * RET_W
C_AK = C_AQ + ATT_W
C_AV = C_AK + KV_W
IN_W = C_AV + KV_W

ADAM_LR = 0.001
ADAM_B1 = 0.9
ADAM_B2 = 0.999
ADAM_EPS = 1e-08
ADAM_WD = 0.01
ADAM_STEP = 10

VMEM_LIMIT = 56 * 1024 * 1024
MATMUL_ROWS = 512
EPILOGUE_ROWS = 256
SUB_ROWS = 256
SMALL_ROWS = 16
ROW_LN1G, ROW_LN1B, ROW_LN2G, ROW_LN2B, ROW_LOSS, ROW_GN, ROW_MISC = 0, 1, 2, 3, 4, 5, 6
MISC_DF, MISC_DB, MISC_SINK = 0, 8, 16


def _dot_nn(a, b):
    return lax.dot_general(a, b, (((1,), (0,)), ((), ())), preferred_element_type=F32)


def _dot_nt(a, b):
    return lax.dot_general(a, b, (((1,), (1,)), ((), ())), preferred_element_type=F32)


def _dot_tn(a, b):
    return lax.dot_general(a, b, (((0,), (0,)), ((), ())), preferred_element_type=F32)


def _params(sem=None, vmem=VMEM_LIMIT):
    kw = {"vmem_limit_bytes": vmem}
    if sem is not None:
        kw["dimension_semantics"] = sem
    return pltpu.CompilerParams(**kw)


def _row_tile(t, want=512):
    tm = want
    while t % tm:
        tm //= 2
    return tm


def _sigmoid(x):
    return jax.nn.sigmoid(x)


def _layer_norm_stats(z):
    mu = jnp.mean(z, axis=1, keepdims=True)
    d = z - mu
    var = jnp.mean(d * d, axis=1, keepdims=True)
    rstd = lax.rsqrt(var + LN_EPS)
    return d * rstd, rstd


def _layer_norm_bwd(dxh, xhat, rstd):
    m1 = jnp.mean(dxh, axis=1, keepdims=True)
    m2 = jnp.mean(dxh * xhat, axis=1, keepdims=True)
    return rstd * (dxh - m1 - xhat * m2)


def _mesh_pos():
    return lax.axis_index("x"), lax.axis_index("y"), lax.axis_index("c")


HBM_SPEC = pl.BlockSpec(memory_space=pltpu.HBM)
SEM_SPEC = pl.BlockSpec(memory_space=pltpu.SEMAPHORE)
ANY_SPEC = pl.BlockSpec(memory_space=pl.ANY)
SIDE_EFFECT = pltpu.SideEffectType.DATAFLOW_SIDE_EFFECTING
PEER_SEMS = pltpu.SemaphoreType.DMA((N_DEV - 1,))


def _in_hbm(a):
    return pltpu.with_memory_space_constraint(a, pltpu.HBM)


def _split_copy_start(name, items, copies):
    n = len(items)
    flat = [a for it in items for a in it]
    k = len(flat)

    def body(*refs):
        arr, sems = list(refs[:k]), refs[k:k + 2 * n]
        for i, it in enumerate(items):
            mine = [arr.pop(0) for _ in it]
            for m in range(1, N_DEV):
                cp = copies[i](m, mine, sems[i].at[m - 1], sems[n + i].at[m - 1])
                if cp is not None:
                    cp.start()
        token = refs[-1]
        token[...] = jnp.zeros_like(token)

    res = pl.pallas_call(
        body, name=name,
        out_shape=[PEER_SEMS] * (2 * n) + [pltpu.HBM(a.shape, a.dtype) for a in flat] + [SDS((8, LANES), F32)],
        in_specs=[HBM_SPEC] * k,
        out_specs=[SEM_SPEC] * (2 * n) + [HBM_SPEC] * k + [pl.BlockSpec(memory_space=pltpu.VMEM)],
        input_output_aliases={j: 2 * n + j for j in range(k)},
        compiler_params=pltpu.CompilerParams(has_side_effects=SIDE_EFFECT),
    )(*[_in_hbm(a) for a in flat])
    thru, out_items = list(res[2 * n:2 * n + k]), []
    for it in items:
        out_items.append(tuple(thru.pop(0) for _ in it))
    return dict(send=res[:n], recv=res[n:2 * n], items=out_items, token=res[-1], copies=copies)


def _gather_start(shards, copies):
    names = list(shards)
    n = len(names)
    flip = [name in TRANSPOSED_HERE for name in names]
    shapes = [shards[name].shape[::-1] if f else shards[name].shape for name, f in zip(names, flip)]
    most = (max(s[0] for s in shapes), max(s[1] for s in shapes))

    def body(*refs):
        src, sems, land, token = refs[:n], refs[n:3 * n], refs[3 * n:4 * n], refs[4 * n]
        wide, narrow, sem = refs[4 * n + 1:]
        for i, (rows, cols) in enumerate(shapes):
            raw = wide.at[0:cols, 0:rows] if flip[i] else wide.at[0:rows, 0:cols]
            bring = pltpu.make_async_copy(src[i], raw, sem.at[0])
            bring.start()
            bring.wait()
            narrow[0:rows, 0:cols] = (raw[...].T if flip[i] else raw[...]).astype(BF16)
            mine = land[i].at[pl.ds(pl.multiple_of(_peer_index(0) * rows, 8), rows), :]
            place = pltpu.make_async_copy(narrow.at[0:rows, 0:cols], mine, sem.at[0])
            place.start()
            place.wait()
            for m in range(1, N_DEV):
                cp = copies[i](m, [land[i]], sems[i].at[m - 1], sems[n + i].at[m - 1])
                if cp is not None:
                    cp.start()
        token[...] = jnp.zeros_like(token)

    side = max(most)
    res = pl.pallas_call(
        body, name="gather_start",
        out_shape=[PEER_SEMS] * (2 * n) + [pltpu.HBM((N_DEV * r, c), BF16) for r, c in shapes] + [SDS((8, LANES), F32)],
        in_specs=[HBM_SPEC] * n,
        out_specs=[SEM_SPEC] * (2 * n) + [HBM_SPEC] * n + [pl.BlockSpec(memory_space=pltpu.VMEM)],
        scratch_shapes=[pltpu.VMEM((side, side), F32), pltpu.VMEM(most, BF16), pltpu.SemaphoreType.DMA((1,))],
        compiler_params=pltpu.CompilerParams(has_side_effects=SIDE_EFFECT),
    )(*[_in_hbm(shards[name]) for name in names])
    return dict(send=res[:n], recv=res[n:2 * n], items=[(a,) for a in res[2 * n:3 * n]], token=res[-1], copies=copies)


def _gather_relay(name, started, which, after):
    (land,) = started["items"][which]

    def body(*refs):
        land_ref, old_send, old_recv = refs[0], refs[1], refs[2]
        send, recv, token = refs[3 + len(after)], refs[4 + len(after)], refs[-1]
        for m in range(1, N_DEV):
            cp = _gather_copy_near(m, [land_ref], old_send.at[m - 1], old_recv.at[m - 1])
            if cp is None:
                continue
            cp.wait_send()
            cp.wait_recv()
            if m > 1:
                _gather_copy_pass(m + 1, [land_ref], send.at[m], recv.at[m]).start()
        token[...] = jnp.zeros_like(token)

    res = pl.pallas_call(
        body, name=name,
        out_shape=[PEER_SEMS, PEER_SEMS, pltpu.HBM(land.shape, land.dtype), SDS((8, LANES), F32)],
        in_specs=[HBM_SPEC, SEM_SPEC, SEM_SPEC] + [ANY_SPEC] * len(after),
        out_specs=[SEM_SPEC, SEM_SPEC, HBM_SPEC, pl.BlockSpec(memory_space=pltpu.VMEM)],
        input_output_aliases={0: 2},
        compiler_params=pltpu.CompilerParams(has_side_effects=SIDE_EFFECT),
    )(land, started["send"][which], started["recv"][which], *[_in_hbm(a) for a in after])
    return dict(send=[res[0]], recv=[res[1]], items=[(res[2],)], token=res[3], copies=[_gather_copy_pass])


def _split_copy_wait(name, started, which, after):
    items = [started["items"][i] for i in which]
    copies = [started["copies"][i] for i in which]
    n = len(items)
    flat = [a for it in items for a in it]
    k = len(flat)

    def body(*refs):
        arr, sems = list(refs[:k]), refs[k:k + 2 * n]
        for i, it in enumerate(items):
            mine = [arr.pop(0) for _ in it]
            for m in range(1, N_DEV):
                cp = copies[i](m, mine, sems[i].at[m - 1], sems[n + i].at[m - 1])
                if cp is not None:
                    cp.wait_send()
                    cp.wait_recv()

    res = pl.pallas_call(
        body, name=name,
        out_shape=[pltpu.HBM(a.shape, a.dtype) for a in flat],
        in_specs=[HBM_SPEC] * k + [SEM_SPEC] * (2 * n) + [ANY_SPEC] * len(after),
        out_specs=[HBM_SPEC] * k,
        input_output_aliases={j: j for j in range(k)},
        compiler_params=pltpu.CompilerParams(has_side_effects=SIDE_EFFECT),
    )(*flat, *[started["send"][i] for i in which], *[started["recv"][i] for i in which], *[_in_hbm(a) for a in after])
    thru, out_items = list(res), []
    for it in items:
        out_items.append(tuple(thru.pop(0) for _ in it))
    return out_items


def _gather_copy(m, refs, send_sem, recv_sem):
    (land_ref,) = refs
    r = land_ref.shape[0] // N_DEV
    mine = land_ref.at[pl.ds(pl.multiple_of(_peer_index(0) * r, 8), r), :]
    return pltpu.make_async_remote_copy(src_ref=mine, dst_ref=mine, send_sem=send_sem, recv_sem=recv_sem,
                                        device_id=_peer(m), device_id_type=MESH)


def _gather_copy_near(m, refs, send_sem, recv_sem):
    return _gather_copy(m, refs, send_sem, recv_sem) if m == 1 or m % 2 == 0 else None


def _gather_copy_pass(m, refs, send_sem, recv_sem):
    if m == 1 or m % 2 == 0:
        return None
    (land_ref,) = refs
    r = land_ref.shape[0] // N_DEV
    block = land_ref.at[pl.ds(pl.multiple_of(_peer_index(m ^ 1) * r, 8), r), :]
    return pltpu.make_async_remote_copy(src_ref=block, dst_ref=block, send_sem=send_sem, recv_sem=recv_sem,
                                        device_id=_peer(1), device_id_type=MESH)


def _small_copy(m, refs, send_sem, recv_sem):
    part_ref, land_ref = refs
    return pltpu.make_async_remote_copy(src_ref=part_ref, dst_ref=land_ref.at[m - 1], send_sem=send_sem,
                                        recv_sem=recv_sem, device_id=_peer(m), device_id_type=MESH)


def _scatter_copy(m, refs, send_sem, recv_sem):
    buf_ref, land_ref = refs
    r = buf_ref.shape[0] // N_DEV
    src = buf_ref.at[pl.ds(pl.multiple_of(_peer_index(m) * r, 8), r), :]
    return pltpu.make_async_remote_copy(src_ref=src, dst_ref=land_ref.at[m - 1], send_sem=send_sem,
                                        recv_sem=recv_sem, device_id=_peer(m), device_id_type=MESH)


def _scatter_copy_sibling(m, refs, send_sem, recv_sem):
    if m % 2 == 0:
        return None
    buf_ref, land_ref = refs
    r = buf_ref.shape[0] // N_DEV
    src = buf_ref.at[pl.ds(pl.multiple_of(_peer_index(m) * r, 8), r), :]
    return pltpu.make_async_remote_copy(src_ref=src, dst_ref=land_ref.at[m >> 1], send_sem=send_sem,
                                        recv_sem=recv_sem, device_id=_peer(1), device_id_type=MESH)


def _scatter_copy_pair(m, refs, send_sem, recv_sem):
    if m % 2:
        return None
    sums_ref, land_ref = refs
    return pltpu.make_async_remote_copy(src_ref=sums_ref.at[m // 2 - 1], dst_ref=land_ref.at[m // 2 - 1],
                                        send_sem=send_sem, recv_sem=recv_sem, device_id=_peer(m),
                                        device_id_type=MESH)


def _scatter_relay(name, started, which, after):
    whole, near = started["items"][which]
    rows = whole.shape[0] // N_DEV
    n = whole.shape[1]
    far_slots = N_DEV // 2 - 1

    def body(*refs):
        whole_ref, near_ref, old_send, old_recv = refs[:4]
        send, recv, sums_ref, land_ref, token = refs[4 + len(after):9 + len(after)]
        mine, theirs, both, sem = refs[9 + len(after):]
        for m in range(1, N_DEV, 2):
            cp = _scatter_copy_sibling(m, [whole_ref, near_ref], old_send.at[m - 1], old_recv.at[m - 1])
            cp.wait_send()
            cp.wait_recv()
        for m in range(2, N_DEV, 2):
            block = whole_ref.at[pl.ds(pl.multiple_of(_peer_index(m) * rows, 8), rows), :]
            a = pltpu.make_async_copy(block, mine, sem.at[0])
            b = pltpu.make_async_copy(near_ref.at[m // 2], theirs, sem.at[1])
            a.start()
            b.start()
            a.wait()
            b.wait()
            both[...] = (mine[...].astype(F32) + theirs[...].astype(F32)).astype(BF16)
            out = pltpu.make_async_copy(both, sums_ref.at[m // 2 - 1], sem.at[0])
            out.start()
            out.wait()
            _scatter_copy_pair(m, [sums_ref, land_ref], send.at[m - 1], recv.at[m - 1]).start()
        own = pltpu.make_async_copy(near_ref.at[0], land_ref.at[far_slots], sem.at[0])
        own.start()
        own.wait()
        token[...] = jnp.zeros_like(token)

    block = pltpu.VMEM((rows, n), BF16)
    res = pl.pallas_call(
        body, name=name,
        out_shape=[PEER_SEMS, PEER_SEMS, pltpu.HBM((far_slots, rows, n), BF16),
                   pltpu.HBM((far_slots + 1, rows, n), BF16), SDS((8, LANES), F32)],
        in_specs=[HBM_SPEC, HBM_SPEC, SEM_SPEC, SEM_SPEC] + [ANY_SPEC] * len(after),
        out_specs=[SEM_SPEC, SEM_SPEC, HBM_SPEC, HBM_SPEC, pl.BlockSpec(memory_space=pltpu.VMEM)],
        scratch_shapes=[block, block, block, pltpu.SemaphoreType.DMA((2,))],
        compiler_params=pltpu.CompilerParams(has_side_effects=SIDE_EFFECT),
    )(whole, near, started["send"][which], started["recv"][which], *[_in_hbm(a) for a in after])
    return dict(send=[res[0]], recv=[res[1]], items=[(res[2], res[3])], token=res[4], copies=[_scatter_copy_pair])


def _peer(m):
    x, y, c = _mesh_pos()
    bx, by, bc = (m >> 2) & 1, (m >> 1) & 1, m & 1
    return (x ^ bx if bx else x, y ^ by if by else y, c ^ bc if bc else c)


def _peer_index(m):
    x, y, c = _mesh_pos()
    return (4 * x + 2 * y + c) ^ m


SMALL_PLACE = {
    "ln1_gain": (ROW_LN1G, 0), "ln1_bias": (ROW_LN1B, 0), "ln2_gain": (ROW_LN2G, 0), "ln2_bias": (ROW_LN2B, 0),
    "ret_gn_gain": (ROW_GN, 0), "ret_decay_fwd": (ROW_MISC, MISC_DF), "ret_decay_bwd": (ROW_MISC, MISC_DB),
    "attn_sink": (ROW_MISC, MISC_SINK)}


def _small_adamw(me, part, landed, w, m, v):
    d = part.shape[1]
    names = list(SMALL_PLACE)
    k = len(names)

    def body(*refs):
        me_ref, part_ref, land_ref = refs[:3]
        refs = refs[2:]
        w_refs, m_refs, v_refs = refs[1:1 + k], refs[1 + k:1 + 2 * k], refs[1 + 2 * k:1 + 3 * k]
        outs = refs[1 + 3 * k:1 + 7 * k + 1]
        tot_ref = refs[-1]
        loss_ref, g_refs, dl_refs = outs[0], outs[1:1 + k], outs[1 + k:1 + 2 * k]
        nm_refs, nv_refs = outs[1 + 2 * k:1 + 3 * k], outs[1 + 3 * k:1 + 4 * k]
        tot = jnp.zeros(part_ref.shape, F32)
        for dev in range(N_DEV):
            j = dev ^ me_ref[0]
            tot = tot + jnp.where(j == 0, part_ref[...], land_ref[jnp.maximum(j, 1) - 1])
        tot_ref[...] = tot
        loss_ref[...] = (0.5 / d) * jnp.sum(tot_ref[ROW_LOSS:ROW_LOSS + 1, :], axis=1, keepdims=True)
        for i, name in enumerate(names):
            row, lo = SMALL_PLACE[name]
            wv = w_refs[i][...]
            g = tot_ref[row:row + 1, lo:lo + wv.shape[1]]
            if name.startswith("ret_decay"):
                p2 = jnp.exp2(wv)
                g = g * (-p2 * jnp.log(2.0) / (1.0 - p2))
            g_refs[i][...] = g
            _adamw_store(g, wv, m_refs[i][...], v_refs[i][...], dl_refs[i], nm_refs[i], nv_refs[i])

    shapes = [SDS(w[n].shape, F32) for n in names]
    vm = pl.BlockSpec(memory_space=pltpu.VMEM)
    res = pl.pallas_call(
        body, name="small_adamw", out_shape=[SDS((1, 1), F32)] + shapes * 4,
        in_specs=[_smem_spec()] + [vm] * (2 + 3 * k), out_specs=[vm] * (1 + 4 * k),
        scratch_shapes=[pltpu.VMEM(part.shape, F32)],
    )(me, part, landed, *[w[n] for n in names], *[m[n] for n in names], *[v[n] for n in names])
    groups = [dict(zip(names, res[1 + j * k:1 + (j + 1) * k])) for j in range(4)]
    return (res[0], *groups)


def _adamw_store(g, w, m, v, dl_ref, nm_ref, nv_ref):
    m = ADAM_B1 * m + (1.0 - ADAM_B1) * g
    v = ADAM_B2 * v + (1.0 - ADAM_B2) * (g * g)
    m_hat = m / (1.0 - ADAM_B1 ** ADAM_STEP)
    v_hat = v / (1.0 - ADAM_B2 ** ADAM_STEP)
    dl_ref[...] = -ADAM_LR * (m_hat / (jnp.sqrt(v_hat) + ADAM_EPS) + ADAM_WD * w)
    nm_ref[...] = m
    nv_ref[...] = v


def _reduce_adamw(name, owns, recvs, ws, ms, vs, transposed):
    count = len(owns)
    rows, n = owns[0].shape
    steps = 1 if transposed or rows % 32 else 4
    rb = rows // steps

    def body(*refs):
        ins, outs = refs[:5 * count], refs[5 * count:]
        j = pl.program_id(0)
        for k in range(count):
            @pl.when(j == k)
            def _(k=k):
                own_ref, recv_ref, w_ref, m_ref, v_ref = ins[5 * k:5 * k + 5]
                g_ref, dl_ref, nm_ref, nv_ref = outs[4 * k:4 * k + 4]
                g = own_ref[...]
                for p in range(recv_ref.shape[0]):
                    g = g + recv_ref[p].astype(F32)
                if transposed:
                    g = g.T
                g_ref[...] = g
                _adamw_store(g, w_ref[...], m_ref[...], v_ref[...], dl_ref, nm_ref, nv_ref)

    def turn(k):
        return lambda j, i: jnp.where(j == k, i, jnp.where(j < k, 0, steps - 1))

    in_specs, out_specs = [], []
    for k in range(count):
        at = turn(k)
        blk = pl.BlockSpec(ws[0].shape if transposed else (rb, n), lambda j, i, at=at: (at(j, i), 0))
        in_specs += [pl.BlockSpec((rb, n), lambda j, i, at=at: (at(j, i), 0)),
                     pl.BlockSpec((recvs[k].shape[0], rb, n), lambda j, i, at=at: (0, at(j, i), 0)), blk, blk, blk]
        out_specs += [blk] * 4
    res = pl.pallas_call(
        body, name="adamw_" + name, grid=(count, steps), in_specs=in_specs, out_specs=out_specs,
        out_shape=[SDS(ws[0].shape, F32)] * (4 * count), compiler_params=_params(("arbitrary", "arbitrary")),
    )(*[a for k in range(count) for a in (owns[k], recvs[k], ws[k], ms[k], vs[k])])
    return [list(res[j::4]) for j in range(4)]


def _row_spec(tm, width):
    return pl.BlockSpec((tm, width), lambda i: (i, 0))


def _full_spec(shape):
    return pl.BlockSpec(shape, lambda i: (0,) * len(shape))


_acc_spec = _full_spec


def _sub_rows(tm):
    step = min(SUB_ROWS, tm)
    return [(lo, lo + step) for lo in range(0, tm, step)]


def _in_proj(x2, wt_in):
    t, d = x2.shape
    u_w = wt_in.shape[0]
    tm = _row_tile(t, MATMUL_ROWS)

    def body(x_ref, w_ref, u_ref, xb_ref):
        xb = x_ref[...].astype(BF16)
        xb_ref[...] = xb
        u_ref[...] = _dot_nt(xb, w_ref[...]).astype(BF16)

    return pl.pallas_call(
        body, name="in_proj", grid=(t // tm,),
        in_specs=[_row_spec(tm, d), _full_spec(wt_in.shape)],
        out_specs=[_row_spec(tm, u_w), _row_spec(tm, d)],
        out_shape=[SDS((t, u_w), BF16), SDS((t, d), BF16)],
        compiler_params=_params(("parallel",)),
    )(x2, wt_in)


def _col_halves(f):
    n = f // LANES
    k = (n + 1) // 2 * LANES
    return [(0, k), (k, f)] if k < f else [(0, f)]


def _mix_ln1_ffn_up(r, a, x2, w_out, wt_gate, wt_up, g1, b1):
    t, d = x2.shape
    f = wt_gate.shape[0]
    tm = _row_tile(t, EPILOGUE_ROWS)

    def body(r_ref, a_ref, x_ref, wo_ref, wg_ref, wu_ref, g_ref, b_ref, xh_ref, rs_ref, hb_ref, dg_ref, du_ref,
             act_ref):
        mix = _dot_nn(r_ref[...], wo_ref[0:RET_W, :]) + _dot_nn(a_ref[...], wo_ref[RET_W:RET_W + ATT_W, :])
        z = ALPHA * x_ref[...] + mix
        xhat, rstd = _layer_norm_stats(z)
        xh_ref[...] = xhat
        rs_ref[...] = jnp.broadcast_to(rstd, rs_ref.shape)
        h = (xhat * g_ref[...] + b_ref[...]).astype(BF16)
        hb_ref[...] = h
        g = _dot_nt(h, wg_ref[...])
        u = _dot_nt(h, wu_ref[...])
        sg = _sigmoid(g)
        silu = g * sg
        dg_ref[...] = (u * (sg * (1.0 + g * (1.0 - sg)))).astype(BF16)
        du_ref[...] = silu.astype(BF16)
        act_ref[...] = (silu * u).astype(BF16)

    wide, narrow = _row_spec(tm, f), _row_spec(tm, d)
    return pl.pallas_call(
        body, name="mix_ln1_ffn_up", grid=(t // tm,),
        in_specs=[_row_spec(tm, RET_W), _row_spec(tm, ATT_W), narrow, _resident_spec(w_out.shape),
                  _resident_spec(wt_gate.shape), _resident_spec(wt_up.shape), _full_spec(g1.shape),
                  _full_spec(b1.shape)],
        out_specs=[narrow, _row_spec(tm, LANES), narrow, wide, wide, wide],
        out_shape=[SDS((t, d), F32), SDS((t, LANES), F32), SDS((t, d), BF16)] + [SDS((t, f), BF16)] * 3,
        compiler_params=_params(("parallel",)),
    )(r, a, x2, w_out, wt_gate, wt_up, g1, b1)


def _ffn_down_ln2_loss(act, dact_dg, dact_du, h1b, p2, xhat1, target, w_down, w_pg, wt_pe, g1, b1, g2, b2):
    t, d = xhat1.shape
    f = act.shape[1]
    pdim = p2.shape[1]
    tm = _row_tile(t, EPILOGUE_ROWS)

    def body(act_ref, fg_ref, fu_ref, hb_ref, p_ref, xh1_ref, tgt_ref, wd_ref, wpg_ref, wpe_ref, g1_ref, b1_ref,
             g2_ref, b2_ref, dz_ref, dzb_ref, ds_ref, dple_ref, dg_ref, du_ref, acc_ref):
        @pl.when(pl.program_id(0) == 0)
        def _():
            acc_ref[...] = jnp.zeros_like(acc_ref)

        for lo, hi in _sub_rows(tm):
            h1 = xh1_ref[lo:hi, :] * g1_ref[...] + b1_ref[...]
            pg = _sigmoid(_dot_nn(hb_ref[lo:hi, :], wpg_ref[...]))
            ple = _dot_nt(p_ref[lo:hi, :].astype(BF16), wpe_ref[...])
            gated = pg * ple
            dgate = gated * (1.0 - pg)
            ffn = _dot_nn(act_ref[lo:hi, :], wd_ref[...])
            z2 = ALPHA * h1 + gated + ffn
            xhat2, rstd2 = _layer_norm_stats(z2)
            err = xhat2 * g2_ref[...] + b2_ref[...] - tgt_ref[lo:hi, :]
            dy = err * (1.0 / d)
            dz = _layer_norm_bwd(dy * g2_ref[...], xhat2, rstd2)
            dzb = dz.astype(BF16)
            dz_ref[lo:hi, :] = dz
            dzb_ref[lo:hi, :] = dzb
            ds_ref[lo:hi, :] = (dz * dgate).astype(BF16)
            dple_ref[lo:hi, :] = (dz * pg).astype(BF16)
            acc_ref[0:1, :] += jnp.sum(err * err, axis=0, keepdims=True)
            acc_ref[1:2, :] += jnp.sum(dy * xhat2, axis=0, keepdims=True)
            acc_ref[2:3, :] += jnp.sum(dy, axis=0, keepdims=True)
            for c0, c1 in _col_halves(f):
                da = _dot_nt(dzb, wd_ref[c0:c1, :])
                dg_ref[lo:hi, c0:c1] = (da * fg_ref[lo:hi, c0:c1].astype(F32)).astype(BF16)
                du_ref[lo:hi, c0:c1] = (da * fu_ref[lo:hi, c0:c1].astype(F32)).astype(BF16)

    vec = _full_spec(g1.shape)
    wide, narrow = _row_spec(tm, f), _row_spec(tm, d)
    return pl.pallas_call(
        body, name="ffn_down_ln2_loss", grid=(t // tm,),
        in_specs=[wide, wide, wide, narrow, _row_spec(tm, pdim), narrow, narrow,
                  _full_spec(w_down.shape), _full_spec(w_pg.shape), _full_spec(wt_pe.shape), vec, vec, vec, vec],
        out_specs=[narrow] * 4 + [wide, wide, _acc_spec((8, d))],
        out_shape=[SDS((t, d), F32), SDS((t, d), BF16), SDS((t, d), BF16), SDS((t, d), BF16),
                   SDS((t, f), BF16), SDS((t, f), BF16), SDS((8, d), F32)],
        compiler_params=_params(("arbitrary",)),
    )(act, dact_dg, dact_du, h1b, p2, xhat1, target, w_down, w_pg, wt_pe, g1, b1, g2, b2)


def _after(after, body):
    k = len(after)
    return (lambda *refs: body(*refs[k:])), [ANY_SPEC] * k


def _resident_spec(shape):
    return pl.BlockSpec(shape, lambda i: (0,) * len(shape), pipeline_mode=pl.Buffered(1))


def _dh1_ln1_bwd(dz2, dg, dup, dsb, xhat1, rstd1, wt_gate, wt_up, w_pg, w_out, g1, after=()):
    t, d = dz2.shape
    f = dg.shape[1]
    tm = _row_tile(t, EPILOGUE_ROWS)

    def body(dz_ref, dg_ref, du_ref, ds_ref, xh1_ref, rs1_ref, wg_ref, wu_ref, wpg_ref, wo_ref, g1_ref,
             dz1_ref, dz1b_ref, dr_ref, da_ref, acc_ref):
        @pl.when(pl.program_id(0) == 0)
        def _():
            acc_ref[...] = jnp.zeros_like(acc_ref)

        for lo, hi in _sub_rows(tm):
            dh = (ALPHA * dz_ref[lo:hi, :] + _dot_nn(dg_ref[lo:hi, :], wg_ref[...])
                  + _dot_nn(du_ref[lo:hi, :], wu_ref[...]) + _dot_nt(ds_ref[lo:hi, :], wpg_ref[...]))
            xhat, rstd = xh1_ref[lo:hi, :], rs1_ref[lo:hi, 0:1]
            dz1 = _layer_norm_bwd(dh * g1_ref[...], xhat, rstd)
            dz1b = dz1.astype(BF16)
            dz1_ref[lo:hi, :] = dz1
            dz1b_ref[lo:hi, :] = dz1b
            acc_ref[0:1, :] += jnp.sum(dh * xhat, axis=0, keepdims=True)
            acc_ref[1:2, :] += jnp.sum(dh, axis=0, keepdims=True)
            dr_ref[lo:hi, :] = _dot_nt(dz1b, wo_ref[0:RET_W, :]).astype(BF16)
            da_ref[lo:hi, :] = _dot_nt(dz1b, wo_ref[RET_W:RET_W + ATT_W, :]).astype(BF16)

    body, lead = _after(after, body)
    return pl.pallas_call(
        body, name="dh1_ln1_bwd", grid=(t // tm,),
        in_specs=lead + [_row_spec(tm, d), _row_spec(tm, f), _row_spec(tm, f), _row_spec(tm, d), _row_spec(tm, d),
                         _row_spec(tm, LANES), _resident_spec(wt_gate.shape), _resident_spec(wt_up.shape), _resident_spec(w_pg.shape),
                         _resident_spec(w_out.shape), _full_spec(g1.shape)],
        out_specs=[_row_spec(tm, d), _row_spec(tm, d), _row_spec(tm, RET_W), _row_spec(tm, ATT_W), _acc_spec((8, d))],
        out_shape=[SDS((t, d), F32), SDS((t, d), BF16), SDS((t, RET_W), BF16), SDS((t, ATT_W), BF16),
                   SDS((8, d), F32)],
        compiler_params=_params(("arbitrary",)),
    )(*after, dz2, dg, dup, dsb, xhat1, rstd1, wt_gate, wt_up, w_pg, w_out, g1)


def _in_proj_bwd(dz1, parts, wt_in, after=()):
    t, d = dz1.shape
    tm = _row_tile(t, MATMUL_ROWS)
    widths = [p.shape[1] for p in parts]

    def body(*refs):
        dz_ref, part_refs, w_ref, dx_ref = refs[0], refs[1:1 + len(parts)], refs[-2], refs[-1]
        acc = ALPHA * dz_ref[...]
        lo = 0
        for p_ref, w in zip(part_refs, widths):
            acc = acc + _dot_nn(p_ref[...], w_ref[lo:lo + w, :])
            lo += w
        dx_ref[...] = acc

    body, lead = _after(after, body)
    return pl.pallas_call(
        body, name="in_proj_bwd", grid=(t // tm,),
        in_specs=lead + [_row_spec(tm, d)] + [_row_spec(tm, w) for w in widths] + [_full_spec(wt_in.shape)],
        out_specs=_row_spec(tm, d), out_shape=SDS((t, d), F32),
        compiler_params=_params(("parallel",)),
    )(*after, dz1, *parts, wt_in)


def _weight_grad(name, me, parts, rhs, after=()):
    t, n = rhs.shape
    widths = [p.shape[1] for p in parts]
    rows = sum(widths)
    own_rows = rows // N_DEV
    tk = _row_tile(t, MATMUL_ROWS)
    n_steps = t // tk
    step = 256

    def body(*refs):
        me_ref, part_refs, rhs_ref = refs[0], refs[1:1 + len(parts)], refs[1 + len(parts)]
        full_ref, own_ref, acc = refs[-3], refs[-2], refs[-1]
        i = pl.program_id(0)

        def products(first):
            b = rhs_ref[...].astype(BF16)
            lo = 0
            for p_ref, w in zip(part_refs, widths):
                for c0 in range(0, w, step):
                    c1 = min(c0 + step, w)
                    val = _dot_tn(p_ref[:, c0:c1].astype(BF16), b)
                    if first:
                        acc[lo + c0:lo + c1, :] = val
                    else:
                        acc[lo + c0:lo + c1, :] += val
                lo += w

        pl.when(i == 0)(functools.partial(products, True))
        pl.when(i > 0)(functools.partial(products, False))

        @pl.when(i == n_steps - 1)
        def _():
            full_ref[...] = acc[...].astype(BF16)
            own_ref[...] = acc[pl.ds(pl.multiple_of(me_ref[0] * own_rows, 8), own_rows), :]

    body, lead = _after(after, body)
    return pl.pallas_call(
        body, name=name, grid=(n_steps,),
        in_specs=lead + [_smem_spec()] + [_row_spec(tk, w) for w in widths] + [_row_spec(tk, n)],
        out_specs=[_full_spec((rows, n)), _full_spec((own_rows, n))],
        out_shape=[SDS((rows, n), BF16), SDS((own_rows, n), F32)],
        scratch_shapes=[pltpu.VMEM((rows, n), F32)],
        compiler_params=_params(("arbitrary",)),
    )(*after, me, *parts, rhs)


def _weight_grad_jobs(name, me, jobs, after=()):
    count = len(jobs)
    t = jobs[0][0].shape[0]
    tk = _row_tile(t, MATMUL_ROWS)
    n_steps = t // tk
    shapes = [(lhs.shape[1], rhs.shape[1]) for lhs, rhs in jobs]
    most_rows, most_cols = max(r for r, _ in shapes), max(n for _, n in shapes)
    step = 256

    def body(*refs):
        me_ref, lhs_refs, rhs_refs = refs[0], refs[1:1 + count], refs[1 + count:1 + 2 * count]
        full_refs, own_refs = refs[1 + 2 * count:1 + 3 * count], refs[1 + 3 * count:1 + 4 * count]
        acc, whole, mine, sems = refs[1 + 4 * count:]
        job, i = pl.program_id(0), pl.program_id(1)

        def leaving(j):
            rows, n = shapes[j]
            return (pltpu.make_async_copy(whole.at[0:rows, 0:n], full_refs[j], sems.at[0]),
                    pltpu.make_async_copy(mine.at[0:rows // N_DEV, 0:n], own_refs[j], sems.at[1]))

        def products(j, first):
            rows, n = shapes[j]
            b = rhs_refs[j][...].astype(BF16)
            for c0 in range(0, rows, step):
                c1 = min(c0 + step, rows)
                val = _dot_tn(lhs_refs[j][:, c0:c1].astype(BF16), b)
                if first:
                    acc[c0:c1, 0:n] = val
                else:
                    acc[c0:c1, 0:n] += val

        def finish(j):
            rows, n = shapes[j]
            own_rows = rows // N_DEV
            if j > 0:
                for cp in leaving(j - 1):
                    cp.wait()
            whole[0:rows, 0:n] = acc[0:rows, 0:n].astype(BF16)
            mine[0:own_rows, 0:n] = acc[pl.ds(pl.multiple_of(me_ref[0] * own_rows, 8), own_rows), 0:n]
            for cp in leaving(j):
                cp.start()
            if j == count - 1:
                for cp in leaving(j):
                    cp.wait()

        for j in range(count):
            pl.when((job == j) & (i == 0))(functools.partial(products, j, True))
            pl.when((job == j) & (i > 0))(functools.partial(products, j, False))
            pl.when((job == j) & (i == n_steps - 1))(functools.partial(finish, j))

    def turn(j):
        return lambda job, i: (jnp.where(job == j, i, jnp.where(job < j, 0, n_steps - 1)), 0)

    body, lead = _after(after, body)
    res = pl.pallas_call(
        body, name=name, grid=(count, n_steps),
        in_specs=lead + [_smem_spec()] + [pl.BlockSpec((tk, rows), turn(j)) for j, (rows, _) in enumerate(shapes)]
        + [pl.BlockSpec((tk, n), turn(j)) for j, (_, n) in enumerate(shapes)],
        out_specs=[ANY_SPEC] * (2 * count),
        out_shape=[SDS((rows, n), BF16) for rows, n in shapes] + [SDS((rows // N_DEV, n), F32) for rows, n in shapes],
        scratch_shapes=[pltpu.VMEM((most_rows, most_cols), F32), pltpu.VMEM((most_rows, most_cols), BF16),
                        pltpu.VMEM((most_rows // N_DEV, most_cols), F32), pltpu.SemaphoreType.DMA((2,))],
        compiler_params=_params(("arbitrary", "arbitrary")),
    )(*after, me, *[lhs for lhs, _ in jobs], *[rhs for _, rhs in jobs])
    return list(res[:count]), list(res[count:])


def _log_decay(decay_f, decay_b):
    def body(f_ref, b_ref, lf_ref, lb_ref):
        lf_ref[...] = jnp.log1p(-jnp.exp2(f_ref[...]))
        lb_ref[...] = jnp.log1p(-jnp.exp2(b_ref[...]))

    return pl.pallas_call(body, name="log_decay", out_shape=[SDS(decay_f.shape, F32)] * 2)(decay_f, decay_b)


def _chunk(ref, n):
    return ref[pl.ds(pl.multiple_of(n * CHUNK, CHUNK), CHUNK), :]


def _group_sum(is_a, v):
    sa = jnp.sum(jnp.where(is_a, v, 0.0), axis=1, keepdims=True)
    sb = jnp.sum(jnp.where(is_a, 0.0, v), axis=1, keepdims=True)
    return jnp.where(is_a, sa, sb)


def _seq_spec(s, col_block):
    return pl.BlockSpec((s, LANES), lambda b, h: (b, col_block + h))


def _smem_spec():
    return pl.BlockSpec(memory_space=pltpu.SMEM)


RET_UNROLL = 4


def _chunk_loop(n_chunks, body, init):
    u = RET_UNROLL if n_chunks % RET_UNROLL == 0 else 1

    def trip(i, carry):
        for j in range(u):
            carry = body(i * u + j, carry)
        return carry

    return lax.fori_loop(0, n_chunks // u, trip, init)


def _stacked_tables(lgf_ref, lgb_ref, pair):
    lane = lax.broadcasted_iota(jnp.int32, (1, LANES), 1)
    is_a = lane < HEAD_DIM
    lgf = jnp.where(is_a, lgf_ref[2 * pair], lgf_ref[2 * pair + 1])
    lgb = jnp.where(is_a, lgb_ref[2 * pair], lgb_ref[2 * pair + 1])
    row = lax.broadcasted_iota(jnp.int32, (CHUNK, 1), 0).astype(F32)
    kdec_f, qdec_f = jnp.exp(lgf * (CHUNK - 1.0 - row)), jnp.exp(lgf * (row + 1.0))
    kdec_b, qdec_b = jnp.exp(lgb * row), jnp.exp(lgb * (CHUNK - row))
    tab = dict(
        is_a=is_a, row=row, lam_f=jnp.exp(lgf * CHUNK), lam_b=jnp.exp(lgb * CHUNK),
        kdec=jnp.concatenate([kdec_f, kdec_b], axis=1), qdec=jnp.concatenate([qdec_f, qdec_b], axis=1),
        qexp=jnp.concatenate([jnp.broadcast_to(row + 1.0, (CHUNK, LANES)),
                              jnp.broadcast_to(CHUNK - row, (CHUNK, LANES))], axis=1),
        kexp=jnp.concatenate([jnp.broadcast_to(CHUNK - 1.0 - row, (CHUNK, LANES)),
                              jnp.broadcast_to(row, (CHUNK, LANES))], axis=1),
    )
    r = lax.broadcasted_iota(jnp.int32, (2 * LANES, LANES), 0)
    c = lax.broadcasted_iota(jnp.int32, (2 * LANES, LANES), 1)
    tab["diag2"] = ((r & (LANES - 1)) < HEAD_DIM) == (c < HEAD_DIM)
    i2 = lax.broadcasted_iota(jnp.int32, (2 * CHUNK, CHUNK), 0)
    j = lax.broadcasted_iota(jnp.int32, (2 * CHUNK, CHUNK), 1)
    head_b = i2 >= CHUNK
    diff = ((i2 & (CHUNK - 1)) - j).astype(F32)
    up, dn = jnp.maximum(diff, 0.0), jnp.maximum(-diff, 0.0)
    lgf2 = jnp.where(head_b, lgf_ref[2 * pair + 1], lgf_ref[2 * pair])
    lgb2 = jnp.where(head_b, lgb_ref[2 * pair + 1], lgb_ref[2 * pair])
    ef = jnp.where(diff >= 0, jnp.exp(lgf2 * up), 0.0)
    eb = jnp.where(diff <= 0, jnp.exp(lgb2 * dn), 0.0)
    tab["d2"] = ef + eb
    tab["df2"] = ef * up
    tab["db2"] = eb * dn
    return tab


def _stack_pair(is_a, x):
    zero = jnp.zeros_like(x)
    return jnp.concatenate([jnp.where(is_a, x, zero), jnp.where(is_a, zero, x)], axis=0)


def _unstack_pair(is_a, x2):
    return jnp.where(is_a, x2[0:CHUNK, :], x2[CHUNK:2 * CHUNK, :])


def _both_ways(x, dec):
    return (jnp.concatenate([x, x], axis=1) * dec).astype(BF16)


def _scan_states(n_chunks, st, up_rows, up_lam, down_rows, down_lam):
    zero = jnp.zeros((LANES, LANES), F32)

    def up(n, r):
        new = st[n, up_rows, :]
        st[n, up_rows, :] = r
        return r * up_lam + new

    def down(s, r):
        n = n_chunks - 1 - s
        new = st[n, down_rows, :]
        st[n, down_rows, :] = r
        return r * down_lam + new

    lax.fori_loop(0, n_chunks, up, zero)
    lax.fori_loop(0, n_chunks, down, zero)


FWD_ROWS, BWD_ROWS = pl.ds(0, LANES), pl.ds(LANES, LANES)


def _state_spec(n_chunks, pairs):
    return pl.BlockSpec((n_chunks, 2 * LANES, LANES), lambda b, h: (b * pairs + h, 0, 0))


ST_GAIN, ST_XF, ST_XB, ST_IFA, ST_IFB, ST_IBA, ST_IBB, ST_LF, ST_LB = 0, 1, 2, 3, 4, 5, 6, 8, 9
ST_ROWS = 16


GW = GROUP * HEAD_DIM
KEYS = 3 * BLOCK


def _attn_tables(g, bias_ref):
    r = lax.broadcasted_iota(jnp.int32, (GROUP * BLOCK, KEYS), 0)
    kj = lax.broadcasted_iota(jnp.int32, (GROUP * BLOCK, KEYS), 1)
    qi = r & (BLOCK - 1)
    hh = lax.shift_right_logical(r, 7)
    dist = jnp.abs(kj - BLOCK - qi)
    slope = jnp.exp2(-(GROUP * g + hh + 1).astype(F32) * (8.0 / ATTN_HEADS))
    inside = jnp.where(dist <= BLOCK, -slope * dist.astype(F32), NEG_INF)
    bias_ref[BIAS_INSIDE] = inside
    bias_ref[BIAS_FIRST] = jnp.where(kj >= BLOCK, inside, NEG_INF)
    bias_ref[BIAS_LAST] = jnp.where(kj < 2 * BLOCK, inside, NEG_INF)


BIAS_INSIDE, BIAS_FIRST, BIAS_LAST = 0, 1, 2


def _own_lanes(g):
    return lax.shift_right_logical(lax.broadcasted_iota(jnp.int32, (1, LANES), 1), 6) == g


def _mask_keys(x_ref, g, scale, pad_ref, s):
    pad_ref[0:BLOCK, :] = jnp.zeros((BLOCK, LANES), BF16)
    pad_ref[BLOCK + s:2 * BLOCK + s, :] = jnp.zeros((BLOCK, LANES), BF16)
    pad_ref[BLOCK:BLOCK + s, :] = jnp.where(_own_lanes(g), x_ref[...].astype(F32) * scale, 0.0).astype(BF16)


def _lane_block(x, j):
    return x[:, j * LANES:(j + 1) * LANES]


def _stack_heads(x, g):
    assert GROUP == 4 and GW == 2 * LANES
    x1 = pltpu.roll(x, HEAD_DIM, 1)
    keep = _own_lanes(g)
    zero = jnp.zeros((BLOCK, LANES), x.dtype)
    rows = []
    for h in range(GROUP):
        for_g0 = _lane_block(x, h // 2) if h % 2 == 0 else _lane_block(x1, ((h + 1) // 2) % 2)
        for_g1 = _lane_block(x, h // 2) if h % 2 == 1 else _lane_block(x1, h // 2)
        rows.append(jnp.where(keep, jnp.where(g == 0, for_g0, for_g1), zero))
    return jnp.concatenate(rows, axis=0)


def _unstack_heads(x4, g):
    p = [x4[h * BLOCK:(h + 1) * BLOCK, :] for h in range(GROUP)]
    cat = lambda a, b: jnp.concatenate([a, b], axis=1)
    in_place = jnp.where(g == 0, cat(p[0], p[2]), cat(p[1], p[3]))
    one_left = jnp.where(g == 0, cat(p[1], p[3]), cat(p[2], p[0]))
    return in_place + pltpu.roll(one_left, HEAD_DIM, 1)


def _sink_column(sink_ref, g):
    rh = lax.shift_right_logical(lax.broadcasted_iota(jnp.int32, (GROUP * BLOCK, 1), 0), 7)
    col = jnp.zeros((GROUP * BLOCK, 1), F32)
    for h in range(GROUP):
        col = jnp.where(rh == h, sink_ref[GROUP * g + h], col)
    return col


def _attn_probs(qm, k3, bias_ref, sink_col, n, s):
    which = jnp.where(n == 0, BIAS_FIRST, jnp.where(n == s // BLOCK - 1, BIAS_LAST, BIAS_INSIDE))
    logits = _dot_nt(qm, k3) + bias_ref[which]
    m = jnp.maximum(jnp.max(logits, axis=1, keepdims=True), sink_col)
    e = jnp.exp(logits - m)
    e_sink = jnp.exp(sink_col - m)
    inv = 1.0 / (jnp.sum(e, axis=1, keepdims=True) + e_sink)
    return e * inv, e_sink * inv


PAIRS_PER_KV = (RET_HEADS // 2) // KV_HEADS
FWD_ORDER = "rrarra"
BWD_ORDER = "rararr"


def _trip_order(order, chunks, blocks):
    if order.count("r") == chunks and order.count("a") == blocks:
        return order
    return "r" * chunks + "a" * blocks


def _mixers_fwd(u, lgf, lgb, gn_gain, sink, b_loc, after=()):
    t = u.shape[0]
    s = t // b_loc
    n_chunks = s // CHUNK
    pairs = RET_HEADS // 2
    trips = n_chunks // RET_UNROLL
    blocks_half = (s // BLOCK) // PAIRS_PER_KV
    per_trip = blocks_half // trips
    assert n_chunks % RET_UNROLL == 0 and blocks_half % trips == 0 and PAIRS_PER_KV == 2 and s >= 2 * BLOCK

    def body(lgf_ref, lgb_ref, sink_ref, q_ref, k_ref, v_ref, g_ref, gain_ref, aq_ref, ak_ref, av_ref,
             r_ref, xhat_ref, rstd_ref, a_ref, st, kpad, vpad, bias):
        pair = pl.program_id(1)
        g, half = lax.shift_right_logical(pair, 1), pair & 1
        tab = _stacked_tables(lgf_ref, lgb_ref, pair)
        is_a = tab["is_a"]

        @pl.when(half == 0)
        def _():
            _attn_tables(g, bias)
            _mask_keys(ak_ref, g, Q_SCALE, kpad, s)
            _mask_keys(av_ref, g, 1.0, vpad, s)

        def kv_body(n, _):
            k8 = _chunk(k_ref, n).astype(F32) * Q_SCALE
            st[n] = jnp.where(tab["diag2"], _dot_tn(_both_ways(k8, tab["kdec"]), _chunk(v_ref, n)), 0.0)
            return 0

        _chunk_loop(n_chunks, kv_body, 0)
        _scan_states(n_chunks, st, FWD_ROWS, tab["lam_f"], BWD_ROWS, tab["lam_b"])
        sink_col = _sink_column(sink_ref, g)

        def retention_chunk(n):
            q = _chunk(q_ref, n)
            k8 = (_chunk(k_ref, n).astype(F32) * Q_SCALE).astype(BF16)
            v = _chunk(v_ref, n)
            p2 = (_dot_nt(_stack_pair(is_a, q), k8) * tab["d2"]).astype(BF16)
            y = _unstack_pair(is_a, _dot_nn(p2, v))
            y = y + _dot_nn(_both_ways(q.astype(F32), tab["qdec"]), st[n].astype(BF16))
            rows = pl.ds(pl.multiple_of(n * CHUNK, CHUNK), CHUNK)
            mu = _group_sum(is_a, y) * (1.0 / HEAD_DIM)
            dlt = y - mu
            var = _group_sum(is_a, dlt * dlt) * (1.0 / HEAD_DIM)
            rstd = lax.rsqrt(var + GN_EPS)
            xhat = dlt * rstd
            xhat_ref[rows, :] = xhat
            rstd_ref[rows, :] = rstd
            gate = _chunk(g_ref, n).astype(F32)
            r_ref[rows, :] = (xhat * gain_ref[...] * gate * _sigmoid(gate)).astype(BF16)

        def attention_block(blk):
            n = half * blocks_half + blk
            rows = pl.ds(pl.multiple_of(blk * BLOCK, BLOCK), BLOCK)
            keys = pl.ds(pl.multiple_of(n * BLOCK, BLOCK), KEYS)
            p, _ = _attn_probs(_stack_heads(aq_ref[rows, :], g), kpad[keys, :], bias, sink_col, n, s)
            a_ref[rows, :] = _unstack_heads(_dot_nn(p.astype(BF16), vpad[keys, :]), g).astype(BF16)

        def trip(i, _):
            chunk, blk = 0, 0
            for kind in _trip_order(FWD_ORDER, RET_UNROLL, per_trip):
                if kind == "r":
                    retention_chunk(i * RET_UNROLL + chunk)
                    chunk += 1
                else:
                    attention_block(i * per_trip + blk)
                    blk += 1
            return 0

        lax.fori_loop(0, trips, trip, 0)

    lane_blk = lambda c0: _seq_spec(s, c0 // LANES)
    half_rows = blocks_half * BLOCK
    aq_spec = pl.BlockSpec((half_rows, GW), lambda b, h: (b * PAIRS_PER_KV + (h & 1), C_AQ // GW + h // 2))
    a_spec = pl.BlockSpec((half_rows, GW), lambda b, h: (b * PAIRS_PER_KV + (h & 1), h // 2))
    kv_spec = lambda c0: pl.BlockSpec((s, LANES), lambda b, h: (b, c0 // LANES))
    pad = pltpu.VMEM((s + 2 * BLOCK, LANES), BF16)
    body, lead = _after(after, body)
    return pl.pallas_call(
        body, name="mixers_fwd", grid=(b_loc, pairs),
        in_specs=lead + [_smem_spec(), _smem_spec(), _smem_spec(), lane_blk(C_RQ), lane_blk(C_RK), lane_blk(C_RV),
                         lane_blk(C_RG), pl.BlockSpec((1, LANES), lambda b, h: (0, h)), aq_spec, kv_spec(C_AK),
                         kv_spec(C_AV)],
        out_specs=[_seq_spec(s, 0), _seq_spec(s, 0), _seq_spec(s, 0), a_spec, _state_spec(n_chunks, pairs)],
        out_shape=[SDS((t, RET_W), BF16), SDS((t, RET_W), F32), SDS((t, RET_W), F32), SDS((t, ATT_W), BF16),
                   SDS((b_loc * pairs * n_chunks, 2 * LANES, LANES), F32)],
        scratch_shapes=[pad, pad, pltpu.VMEM((3, GROUP * BLOCK, KEYS), F32)],
        compiler_params=_params(("arbitrary", "arbitrary")),
    )(*after, lgf, lgb, sink, u, u, u, u, gn_gain, u, u, u)


def _mixers_bwd(u, xhat, rstd, states, dr, da, lgf, lgb, gn_gain, sink, b_loc, after=()):
    t = u.shape[0]
    s = t // b_loc
    n_chunks = s // CHUNK
    pairs = RET_HEADS // 2
    trips = n_chunks // RET_UNROLL
    blocks_half = (s // BLOCK) // PAIRS_PER_KV
    per_trip = blocks_half // trips
    assert n_chunks % RET_UNROLL == 0 and blocks_half % trips == 0 and PAIRS_PER_KV == 2 and s >= 2 * BLOCK

    def body(lgf_ref, lgb_ref, sink_ref, q_ref, k_ref, v_ref, g_ref, xhat_ref, rstd_ref, dr_ref, gain_ref,
             aq_ref, ak_ref, av_ref, do_ref, st,
             dq_ref, dk_ref, dv_ref, dg_ref, st_ref, daq_ref, dak_ref, dav_ref, dsink_ref,
             gr, dy_s, kpad, vpad, bias, dk_acc, dv_acc):
        pair = pl.program_id(1)
        g, half = lax.shift_right_logical(pair, 1), pair & 1
        tab = _stacked_tables(lgf_ref, lgb_ref, pair)
        is_a = tab["is_a"]
        gain = gain_ref[...]

        @pl.when(half == 0)
        def _():
            _attn_tables(g, bias)
            _mask_keys(ak_ref, g, Q_SCALE, kpad, s)
            _mask_keys(av_ref, g, 1.0, vpad, s)
            dsink_ref[...] = jnp.zeros_like(dsink_ref)

        @pl.when(pair == 0)
        def _():
            dk_acc[...] = jnp.zeros_like(dk_acc)
            dv_acc[...] = jnp.zeros_like(dv_acc)

        def norm_body(n, dgain):
            rows = pl.ds(pl.multiple_of(n * CHUNK, CHUNK), CHUNK)
            xhat, rstd = xhat_ref[rows, :], rstd_ref[rows, :]
            gate = g_ref[rows, :].astype(F32)
            sg = _sigmoid(gate)
            silu = gate * sg
            d_out = dr_ref[rows, :].astype(F32)
            dg_ref[rows, :] = (d_out * xhat * gain * (sg * (1.0 + gate * (1.0 - sg)))).astype(BF16)
            dxh = d_out * gain * silu
            m1 = _group_sum(is_a, dxh) * (1.0 / HEAD_DIM)
            m2 = _group_sum(is_a, dxh * xhat) * (1.0 / HEAD_DIM)
            dy = (rstd * (dxh - m1 - xhat * m2)).astype(BF16)
            dy_s[rows, :] = dy
            qf = q_ref[rows, :].astype(F32)
            gr[n] = jnp.where(tab["diag2"], _dot_tn(_both_ways(qf, tab["qdec"]), dy), 0.0)
            return dgain + jnp.sum(d_out * xhat * silu, axis=0, keepdims=True)

        colsum = lambda x: jnp.sum(x, axis=0, keepdims=True)

        def grad_body(n, carry):
            xfb, ifa, ifb, iba, ibb, lf, lb = carry
            rows = pl.ds(pl.multiple_of(n * CHUNK, CHUNK), CHUNK)
            q = q_ref[rows, :]
            qf = q.astype(F32)
            k8f = k_ref[rows, :].astype(F32) * Q_SCALE
            k8 = k8f.astype(BF16)
            v = v_ref[rows, :]
            dy = dy_s[rows, :]
            q2, dy2 = _stack_pair(is_a, q), _stack_pair(is_a, dy)
            sc = _dot_nt(q2, k8)
            dp = _dot_nt(dy2, v)
            a2 = (sc * tab["d2"]).astype(BF16)
            ds2 = (dp * tab["d2"]).astype(BF16)
            dq = _unstack_pair(is_a, _dot_nn(ds2, k8))
            dk = _dot_tn(ds2, q2)
            dv = _dot_tn(a2, dy2)
            prod = sc * dp
            pf, pb = prod * tab["df2"], prod * tab["db2"]
            ifa, ifb = ifa + colsum(pf[0:CHUNK, :]), ifb + colsum(pf[CHUNK:2 * CHUNK, :])
            iba, ibb = iba + colsum(pb[0:CHUNK, :]), ibb + colsum(pb[CHUNK:2 * CHUNK, :])
            states, sgrads = st[n], gr[n]
            sb, gb = states.astype(BF16), sgrads.astype(BF16)
            dqc = _dot_nt(dy, sb) * tab["qdec"]
            dkc = _dot_nt(v, gb) * tab["kdec"]
            dv = dv + _dot_nn(_both_ways(k8f, tab["kdec"]), gb)
            dq_ref[rows, :] = (dq + dqc[:, 0:LANES] + dqc[:, LANES:2 * LANES]).astype(BF16)
            dk_ref[rows, :] = ((dk + dkc[:, 0:LANES] + dkc[:, LANES:2 * LANES]) * Q_SCALE).astype(BF16)
            dv_ref[rows, :] = dv.astype(BF16)
            q2w, k2w = jnp.concatenate([qf, qf], axis=1), jnp.concatenate([k8f, k8f], axis=1)
            xfb = xfb + colsum(tab["qexp"] * q2w * dqc + tab["kexp"] * k2w * dkc)
            prod_s = sgrads * states
            lf, lb = lf + colsum(prod_s[0:LANES, :]), lb + colsum(prod_s[LANES:2 * LANES, :])
            return xfb, ifa, ifb, iba, ibb, lf, lb

        sink_col = _sink_column(sink_ref, g)
        head_row = lax.broadcasted_iota(jnp.int32, dsink_ref.shape, 0)

        def attention_block(blk):
            n = half * blocks_half + blk
            rows = pl.ds(pl.multiple_of(blk * BLOCK, BLOCK), BLOCK)
            keys = pl.ds(pl.multiple_of(n * BLOCK, BLOCK), KEYS)
            qm = _stack_heads(aq_ref[rows, :], g)
            k3, v3 = kpad[keys, :], vpad[keys, :]
            p, p_sink = _attn_probs(qm, k3, bias, sink_col, n, s)
            dom = _stack_heads(do_ref[rows, :], g)
            dp = _dot_nt(dom, v3)
            delta = jnp.sum(p * dp, axis=1, keepdims=True)
            ds_mat = (p * (dp - delta)).astype(BF16)
            daq_ref[rows, :] = _unstack_heads(_dot_nn(ds_mat, k3), g).astype(BF16)
            dk_acc[keys, :] += _dot_tn(ds_mat, qm) * Q_SCALE
            dv_acc[keys, :] += _dot_tn(p.astype(BF16), dom)
            w = p_sink * delta
            upd = jnp.zeros(dsink_ref.shape, F32)
            for h in range(GROUP):
                upd = upd + jnp.where(head_row == h, -jnp.sum(w[h * BLOCK:(h + 1) * BLOCK, :]), 0.0)
            dsink_ref[...] += upd

        dgain = _chunk_loop(n_chunks, norm_body, jnp.zeros((1, LANES), F32))
        _scan_states(n_chunks, gr, BWD_ROWS, tab["lam_b"], FWD_ROWS, tab["lam_f"])

        def trip(i, carry):
            chunk, blk = 0, 0
            for kind in _trip_order(BWD_ORDER, RET_UNROLL, per_trip):
                if kind == "r":
                    carry = grad_body(i * RET_UNROLL + chunk, carry)
                    chunk += 1
                else:
                    attention_block(i * per_trip + blk)
                    blk += 1
            return carry

        z = jnp.zeros((1, LANES), F32)
        init = (jnp.zeros((1, 2 * LANES), F32), z, z, z, z, z, z)
        xfb, ifa, ifb, iba, ibb, lf, lb = lax.fori_loop(0, trips, trip, init)
        st_ref[...] = jnp.zeros_like(st_ref)
        st_ref[ST_GAIN:ST_GAIN + 1, :] = dgain
        st_ref[ST_XF:ST_XF + 1, :] = xfb[:, 0:LANES]
        st_ref[ST_XB:ST_XB + 1, :] = xfb[:, LANES:2 * LANES]
        st_ref[ST_IFA:ST_IFA + 1, :] = ifa
        st_ref[ST_IFB:ST_IFB + 1, :] = ifb
        st_ref[ST_IBA:ST_IBA + 1, :] = iba
        st_ref[ST_IBB:ST_IBB + 1, :] = ibb
        st_ref[ST_LF:ST_LF + 1, :] = lf * (CHUNK * tab["lam_f"])
        st_ref[ST_LB:ST_LB + 1, :] = lb * (CHUNK * tab["lam_b"])

        @pl.when(pair == pairs - 1)
        def _():
            dak_ref[...] = dk_acc[BLOCK:BLOCK + s, :].astype(BF16)
            dav_ref[...] = dv_acc[BLOCK:BLOCK + s, :].astype(BF16)

    lane_blk = lambda c0: _seq_spec(s, c0 // LANES)
    seq0 = _seq_spec(s, 0)
    half_rows = blocks_half * BLOCK
    aq_spec = pl.BlockSpec((half_rows, GW), lambda b, h: (b * PAIRS_PER_KV + (h & 1), C_AQ // GW + h // 2))
    a_spec = pl.BlockSpec((half_rows, GW), lambda b, h: (b * PAIRS_PER_KV + (h & 1), h // 2))
    kv_spec = lambda c0: pl.BlockSpec((s, LANES), lambda b, h: (b, c0 // LANES))
    kv_out = pl.BlockSpec((s, LANES), lambda b, h: (b, 0))
    state = pltpu.VMEM((n_chunks, 2 * LANES, LANES), F32)
    pad = pltpu.VMEM((s + 2 * BLOCK, LANES), BF16)
    acc = pltpu.VMEM((s + 2 * BLOCK, LANES), F32)
    body, lead = _after(after, body)
    return pl.pallas_call(
        body, name="mixers_bwd", grid=(b_loc, pairs),
        in_specs=lead + [_smem_spec(), _smem_spec(), _smem_spec(), lane_blk(C_RQ), lane_blk(C_RK), lane_blk(C_RV),
                         lane_blk(C_RG), seq0, seq0, seq0, pl.BlockSpec((1, LANES), lambda b, h: (0, h)),
                         aq_spec, kv_spec(C_AK), kv_spec(C_AV), a_spec, _state_spec(n_chunks, pairs)],
        out_specs=[seq0] * 4 + [pl.BlockSpec((ST_ROWS, LANES), lambda b, h: (b, h)), a_spec, kv_out, kv_out,
                                pl.BlockSpec((8, LANES), lambda b, h: (b * KV_HEADS + h // 2, 0))],
        out_shape=[SDS((t, RET_W), BF16)] * 4 + [SDS((b_loc * ST_ROWS, RET_W), F32), SDS((t, ATT_W), BF16),
                                                   SDS((t, KV_W), BF16), SDS((t, KV_W), BF16),
                                                   SDS((b_loc * KV_HEADS * 8, LANES), F32)],
        scratch_shapes=[state, pltpu.VMEM((s, LANES), BF16), pad, pad,
                        pltpu.VMEM((3, GROUP * BLOCK, KEYS), F32), acc, acc],
        compiler_params=_params(("arbitrary", "arbitrary")),
    )(*after, lgf, lgb, sink, u, u, u, u, xhat, rstd, dr, gn_gain, u, u, u, da, states)


def _pack_small(acc2, acc1, ret_stats, dsink, b_loc, d):
    pairs = RET_HEADS // 2

    def body(acc2_ref, acc1_ref, st_ref, dsink_ref, out_ref):
        out_ref[...] = jnp.zeros_like(out_ref)
        out_ref[ROW_LN1G:ROW_LN1G + 1, :] = acc1_ref[0:1, :]
        out_ref[ROW_LN1B:ROW_LN1B + 1, :] = acc1_ref[1:2, :]
        out_ref[ROW_LN2G:ROW_LN2G + 1, :] = acc2_ref[1:2, :]
        out_ref[ROW_LN2B:ROW_LN2B + 1, :] = acc2_ref[2:3, :]
        out_ref[ROW_LOSS:ROW_LOSS + 1, :] = acc2_ref[0:1, :]
        st = st_ref[0:ST_ROWS, :]
        for b in range(1, b_loc):
            st = st + st_ref[b * ST_ROWS:(b + 1) * ST_ROWS, :]
        out_ref[ROW_GN:ROW_GN + 1, 0:RET_W] = st[ST_GAIN:ST_GAIN + 1, :]
        lane = lax.broadcasted_iota(jnp.int32, (1, d), 1)
        misc = jnp.zeros((1, d), F32)
        for pr in range(pairs):
            blk = st[:, pr * LANES:(pr + 1) * LANES]
            half = lax.broadcasted_iota(jnp.int32, (1, LANES), 1) < HEAD_DIM
            for h in range(2):
                sel = half if h == 0 else jnp.logical_not(half)
                cross_f = jnp.sum(jnp.where(sel, blk[ST_XF:ST_XF + 1, :] + blk[ST_LF:ST_LF + 1, :], 0.0))
                cross_b = jnp.sum(jnp.where(sel, blk[ST_XB:ST_XB + 1, :] + blk[ST_LB:ST_LB + 1, :], 0.0))
                intra_f = jnp.sum(blk[ST_IFA + h:ST_IFA + h + 1, :])
                intra_b = jnp.sum(blk[ST_IBA + h:ST_IBA + h + 1, :])
                head = 2 * pr + h
                misc = jnp.where(lane == MISC_DF + head, cross_f + intra_f, misc)
                misc = jnp.where(lane == MISC_DB + head, cross_b + intra_b, misc)
        for g in range(KV_HEADS):
            tot = dsink_ref[g * 8:(g + 1) * 8, :]
            for b in range(1, b_loc):
                tot = tot + dsink_ref[(b * KV_HEADS + g) * 8:(b * KV_HEADS + g + 1) * 8, :]
            for h in range(GROUP):
                misc = jnp.where(lane == MISC_SINK + GROUP * g + h, jnp.sum(tot[h:h + 1, 0:1]), misc)
        out_ref[ROW_MISC:ROW_MISC + 1, :] = misc

    return pl.pallas_call(body, name="pack_small", out_shape=SDS((SMALL_ROWS, d), F32))(acc2, acc1, ret_stats, dsink)


BIG = ("w_in", "w_out", "w_ffn_gate", "w_ffn_up", "w_ffn_down", "w_ple_proj", "w_ple_gate")
TRANSPOSED_OUTSIDE = ("w_in", "w_ffn_gate", "w_ffn_up")
TRANSPOSED_HERE = ("w_ple_proj",)
SMALL = ("ret_decay_fwd", "ret_decay_bwd", "ret_gn_gain", "attn_sink", "ln1_gain", "ln1_bias", "ln2_gain", "ln2_bias")
ORDER = ("w_in", "ret_decay_fwd", "ret_decay_bwd", "ret_gn_gain", "attn_sink", "w_out", "ln1_gain", "ln1_bias",
         "w_ffn_gate", "w_ffn_up", "w_ffn_down", "w_ple_proj", "w_ple_gate", "ln2_gain", "ln2_bias")


GATHER_ORDER = ("w_in", "w_ffn_up", "w_out", "w_ffn_gate", "w_ple_gate", "w_ple_proj", "w_ffn_down")
GATHER_TWO_LEVEL = ("w_in", "w_ffn_up")
SCATTER_TWO_LEVEL = "in"


def _local_step(x2, p2, target2, fetch, publish, small, b_loc, me):
    d = x2.shape[1]
    lgf, lgb = _log_decay(small["ret_decay_fwd"], small["ret_decay_bwd"])
    lgf1, lgb1, sink1 = lgf.reshape(-1), lgb.reshape(-1), small["attn_sink"].reshape(-1)
    (w_in,) = fetch(("w_in",), ())
    u, xb = _in_proj(x2, w_in)
    passed = fetch.pass_on("w_ffn_up", (xb,))
    r, ret_xhat, ret_rstd, a, ret_states = _mixers_fwd(u, lgf1, lgb1, small["ret_gn_gain"], sink1, b_loc, passed)
    w_out, w_gate, w_up = fetch(("w_out", "w_ffn_gate", "w_ffn_up"), (r, a))
    xhat1, rstd1, h1b, dact_dg, dact_du, act = _mix_ln1_ffn_up(
        r, a, x2, w_out, w_gate, w_up, small["ln1_gain"], small["ln1_bias"])
    w_pg, w_pe, w_down = fetch(("w_ple_gate", "w_ple_proj", "w_ffn_down"), (act,))
    dz2, dz2b, dsb, dpleb, dg, dup, acc2 = _ffn_down_ln2_loss(
        act, dact_dg, dact_du, h1b, p2, xhat1, target2, w_down, w_pg, w_pe,
        small["ln1_gain"], small["ln1_bias"], small["ln2_gain"], small["ln2_bias"])
    own = {}

    def grad(name, parts, rhs, after=()):
        whole, own[name] = _weight_grad("grad_" + name, me, parts, rhs, after)
        return whole

    ffn_jobs = dict(w_ffn_down=(act, dz2b), w_ple_proj=(dpleb, p2), w_ple_gate=(h1b, dsb),
                    w_ffn_gate=(dg, h1b), w_ffn_up=(dup, h1b))
    wholes, owns = _weight_grad_jobs("grad_w_ffn", me, list(ffn_jobs.values()))
    own.update(zip(ffn_jobs, owns))
    t2 = publish("ffn", dict(zip(ffn_jobs, wholes)))
    dz1, dz1b, dr, da, acc1 = _dh1_ln1_bwd(
        dz2, dg, dup, dsb, xhat1, rstd1, w_gate, w_up, w_pg, w_out, small["ln1_gain"], t2)
    t3 = publish("out", dict(w_out=grad("w_out", [r, a], dz1b)))
    dq, dk, dv, dgate, ret_stats, daq, dak, dav, dsink = _mixers_bwd(
        u, ret_xhat, ret_rstd, ret_states, dr, da, lgf1, lgb1, small["ret_gn_gain"], sink1, b_loc, t3)
    parts = [dq, dk, dv, dgate, daq, dak, dav]
    small_part = _pack_small(acc2, acc1, ret_stats, dsink, b_loc, d)
    s1 = publish.settle("ffn", {n: own[n] for n in ffn_jobs}, (dq,))
    t4 = publish("in", dict(w_in=grad("w_in", parts, xb, s1)), small_part)
    s2 = publish.settle("out", dict(w_out=own["w_out"]), t4)
    t5 = publish.relay("in", s2)
    grad_x = _in_proj_bwd(dz1, parts, w_in, t5)
    return grad_x, own, small_part


def kernel(x, p, w_in, ret_decay_fwd, ret_decay_bwd, ret_gn_gain, attn_sink, w_out, ln1_gain, ln1_bias, w_ffn_gate, w_ffn_up, w_ffn_down, w_ple_proj, w_ple_gate, ln2_gain, ln2_bias, loss_target, m_w_in, m_ret_decay_fwd, m_ret_decay_bwd, m_ret_gn_gain, m_attn_sink, m_w_out, m_ln1_gain, m_ln1_bias, m_w_ffn_gate, m_w_ffn_up, m_w_ffn_down, m_w_ple_proj, m_w_ple_gate, m_ln2_gain, m_ln2_bias, v_w_in, v_ret_decay_fwd, v_ret_decay_bwd, v_ret_gn_gain, v_attn_sink, v_w_out, v_ln1_gain, v_ln1_bias, v_w_ffn_gate, v_w_ffn_up, v_w_ffn_down, v_w_ple_proj, v_w_ple_gate, v_ln2_gain, v_ln2_bias):
    given = dict(locals())

    def strip(n, a):
        if n not in BIG:
            return a
        return a[0].T if n in TRANSPOSED_OUTSIDE else a[0]

    def restore(n, a):
        if n not in BIG:
            return a
        return (a.T if n in TRANSPOSED_OUTSIDE else a)[None]

    w = {n: strip(n, given[n]) for n in ORDER}
    m = {n: strip(n, given["m_" + n]) for n in ORDER}
    v = {n: strip(n, given["v_" + n]) for n in ORDER}
    b_loc, s, d = x.shape
    x2 = x.reshape(b_loc * s, d)
    p2 = p[0].reshape(b_loc * s, p.shape[-1])
    target2 = loss_target.reshape(b_loc * s, d)

    small = {n: w[n] for n in SMALL}
    me = (4 * lax.axis_index("x") + 2 * lax.axis_index("y") + lax.axis_index("c")).astype(jnp.int32).reshape(1)

    gather = _gather_start(
        {n: w[n] for n in GATHER_ORDER},
        [_gather_copy_near if n in GATHER_TWO_LEVEL else _gather_copy for n in GATHER_ORDER])

    passing = {}

    def pass_on(n, after):
        passing[n] = _gather_relay("gather_relay_" + n, gather, GATHER_ORDER.index(n), list(after))
        return (passing[n]["token"],)

    def fetch(names, after):
        out = {}
        for n in [n for n in names if n in GATHER_TWO_LEVEL]:
            if n not in passing:
                pass_on(n, after)
            out[n] = _split_copy_wait("gather_wait_" + n, passing[n], [0], list(after))[0][0]
        direct = [n for n in names if n not in GATHER_TWO_LEVEL]
        if direct:
            got = _split_copy_wait("gather_wait_" + direct[0], gather, [GATHER_ORDER.index(n) for n in direct],
                                   list(after))
            out.update({n: item[0] for n, item in zip(direct, got)})
        return [out[n] for n in names]

    scatters = {}
    out_g, out_d, out_m, out_v = {}, {}, {}, {}

    def publish(tag, products, small_sums=None):
        slots, copy = (N_DEV // 2, _scatter_copy_sibling) if tag == SCATTER_TWO_LEVEL else (N_DEV - 1, _scatter_copy)
        items = [(products[n], lax.empty((slots, products[n].shape[0] // N_DEV, products[n].shape[1]), BF16))
                 for n in products]
        copies = [copy] * len(items)
        if small_sums is not None:
            items.append((small_sums, lax.empty((N_DEV - 1,) + small_sums.shape, F32)))
            copies.append(_small_copy)
        started = _split_copy_start("scatter_start_" + tag, items, copies)
        scatters[tag] = dict(names=list(products), started=started)
        return (started["token"],)

    def relay(tag, after):
        scatters[tag]["relayed"] = _scatter_relay("scatter_relay_" + tag, scatters[tag]["started"], 0, list(after))
        return (scatters[tag]["relayed"]["token"],)

    def settle(tag, owns, after):
        names = scatters[tag]["names"]
        if "relayed" in scatters[tag]:
            landed = _split_copy_wait("scatter_wait_" + names[0], scatters[tag]["relayed"], [0], list(after))
        else:
            landed = _split_copy_wait("scatter_wait_" + names[0], scatters[tag]["started"],
                                      list(range(len(names))), list(after))
        recv = {n: item[1] for n, item in zip(names, landed)}
        alike = {}
        for n in names:
            alike.setdefault((owns[n].shape, n in TRANSPOSED_HERE), []).append(n)
        for (_, transposed), ns in alike.items():
            res = _reduce_adamw(ns[0], [owns[n] for n in ns], [recv[n] for n in ns], [w[n] for n in ns],
                                [m[n] for n in ns], [v[n] for n in ns], transposed)
            for dst, vals in zip((out_g, out_d, out_m, out_v), res):
                dst.update(zip(ns, vals))
        return (out_v[names[-1]],)

    fetch.pass_on = pass_on
    publish.relay, publish.settle = relay, settle
    grad_x, own, small_part = _local_step(x2, p2, target2, fetch, publish, small, b_loc, me)

    last = scatters[SCATTER_TWO_LEVEL]
    done = settle(SCATTER_TWO_LEVEL, {n: own[n] for n in last["names"]}, (grad_x,))
    ((mine, from_peers),) = _split_copy_wait("scatter_wait_small", last["started"], [len(last["names"])], list(done))
    loss, sg, sd, sm, sv = _small_adamw(
        me, mine, from_peers, small, {n: m[n] for n in SMALL}, {n: v[n] for n in SMALL})
    for dst, src in ((out_g, sg), (out_d, sd), (out_m, sm), (out_v, sv)):
        dst.update(src)

    outs = [loss[0, 0], grad_x.reshape(x.shape)]
    for group in (out_g, out_d, out_m, out_v):
        outs += [restore(n, group[n]) for n in ORDER]
    return tuple(outs)
```

```python
import functools

import jax
import jax.numpy as jnp
from jax import lax
from jax.experimental import pallas as pl
from jax.experimental.pallas import tpu as pltpu

F32, BF16 = jnp.float32, jnp.bfloat16
SDS = jax.ShapeDtypeStruct
MESH = pl.DeviceIdType.MESH

N_DEV = 8
HEAD_DIM = 64
RET_HEADS = 8
ATTN_HEADS = 8
KV_HEADS = 2
GROUP = ATTN_HEADS // KV_HEADS
RET_W = RET_HEADS * HEAD_DIM
ATT_W = ATTN_HEADS * HEAD_DIM
KV_W = KV_HEADS * HEAD_DIM
LANES = 128
CHUNK = 128
BLOCK = 128
Q_SCALE = HEAD_DIM ** -0.5
ALPHA = 2.0 ** 0.25
LN_EPS = 1e-5
GN_EPS = 1e-5
NEG_INF = -1e30
C_RQ, C_RK, C_RV, C_RG = 0, RET_W, 2 * RET_W, 3 * RET_W
C_AQ = 4 * RET_W
C_AK = C_AQ + ATT_W
C_AV = C_AK + KV_W
IN_W = C_AV + KV_W

ADAM_LR = 0.001
ADAM_B1 = 0.9
ADAM_B2 = 0.999
ADAM_EPS = 1e-08
ADAM_WD = 0.01
ADAM_STEP = 10

VMEM_LIMIT = 56 * 1024 * 1024
MATMUL_ROWS = 512
EPILOGUE_ROWS = 256
SUB_ROWS = 256
SMALL_ROWS = 16
ROW_LN1G, ROW_LN1B, ROW_LN2G, ROW_LN2B, ROW_LOSS, ROW_GN, ROW_MISC = 0, 1, 2, 3, 4, 5, 6
MISC_DF, MISC_DB, MISC_SINK = 0, 8, 16


def _dot_nn(a, b):
    return lax.dot_general(a, b, (((1,), (0,)), ((), ())), preferred_element_type=F32)


def _dot_nt(a, b):
    return lax.dot_general(a, b, (((1,), (1,)), ((), ())), preferred_element_type=F32)


def _dot_tn(a, b):
    return lax.dot_general(a, b, (((0,), (0,)), ((), ())), preferred_element_type=F32)


def _params(sem=None, vmem=VMEM_LIMIT):
    kw = {"vmem_limit_bytes": vmem}
    if sem is not None:
        kw["dimension_semantics"] = sem
    return pltpu.CompilerParams(**kw)


def _row_tile(t, want=512):
    tm = want
    while t % tm:
        tm //= 2
    return tm


def _sigmoid(x):
    return jax.nn.sigmoid(x)


def _layer_norm_stats(z):
    mu = jnp.mean(z, axis=1, keepdims=True)
    d = z - mu
    var = jnp.mean(d * d, axis=1, keepdims=True)
    rstd = lax.rsqrt(var + LN_EPS)
    return d * rstd, rstd


def _layer_norm_bwd(dxh, xhat, rstd):
    m1 = jnp.mean(dxh, axis=1, keepdims=True)
    m2 = jnp.mean(dxh * xhat, axis=1, keepdims=True)
    return rstd * (dxh - m1 - xhat * m2)


def _mesh_pos():
    return lax.axis_index("x"), lax.axis_index("y"), lax.axis_index("c")


HBM_SPEC = pl.BlockSpec(memory_space=pltpu.HBM)
SEM_SPEC = pl.BlockSpec(memory_space=pltpu.SEMAPHORE)
ANY_SPEC = pl.BlockSpec(memory_space=pl.ANY)
SIDE_EFFECT = pltpu.SideEffectType.DATAFLOW_SIDE_EFFECTING
PEER_SEMS = pltpu.SemaphoreType.DMA((N_DEV - 1,))


def _in_hbm(a):
    return pltpu.with_memory_space_constraint(a, pltpu.HBM)


def _split_copy_start(name, items, copies):
    n = len(items)
    flat = [a for it in items for a in it]
    k = len(flat)

    def body(*refs):
        arr, sems = list(refs[:k]), refs[k:k + 2 * n]
        for i, it in enumerate(items):
            mine = [arr.pop(0) for _ in it]
            for m in range(1, N_DEV):
                cp = copies[i](m, mine, sems[i].at[m - 1], sems[n + i].at[m - 1])
                if cp is not None:
                    cp.start()
        token = refs[-1]
        token[...] = jnp.zeros_like(token)

    res = pl.pallas_call(
        body, name=name,
        out_shape=[PEER_SEMS] * (2 * n) + [pltpu.HBM(a.shape, a.dtype) for a in flat] + [SDS((8, LANES), F32)],
        in_specs=[HBM_SPEC] * k,
        out_specs=[SEM_SPEC] * (2 * n) + [HBM_SPEC] * k + [pl.BlockSpec(memory_space=pltpu.VMEM)],
        input_output_aliases={j: 2 * n + j for j in range(k)},
        compiler_params=pltpu.CompilerParams(has_side_effects=SIDE_EFFECT),
    )(*[_in_hbm(a) for a in flat])
    thru, out_items = list(res[2 * n:2 * n + k]), []
    for it in items:
        out_items.append(tuple(thru.pop(0) for _ in it))
    return dict(send=res[:n], recv=res[n:2 * n], items=out_items, token=res[-1], copies=copies)


def _gather_start(shards, copies):
    names = list(shards)
    n = len(names)
    flip = [name in TRANSPOSED_HERE for name in names]
    shapes = [shards[name].shape[::-1] if f else shards[name].shape for name, f in zip(names, flip)]
    most = (max(s[0] for s in shapes), max(s[1] for s in shapes))

    def body(*refs):
        src, sems, land, token = refs[:n], refs[n:3 * n], refs[3 * n:4 * n], refs[4 * n]
        wide, narrow, sem = refs[4 * n + 1:]
        for i, (rows, cols) in enumerate(shapes):
            raw = wide.at[0:cols, 0:rows] if flip[i] else wide.at[0:rows, 0:cols]
            bring = pltpu.make_async_copy(src[i], raw, sem.at[0])
            bring.start()
            bring.wait()
            narrow[0:rows, 0:cols] = (raw[...].T if flip[i] else raw[...]).astype(BF16)
            mine = land[i].at[pl.ds(pl.multiple_of(_peer_index(0) * rows, 8), rows), :]
            place = pltpu.make_async_copy(narrow.at[0:rows, 0:cols], mine, sem.at[0])
            place.start()
            place.wait()
            for m in range(1, N_DEV):
                cp = copies[i](m, [land[i]], sems[i].at[m - 1], sems[n + i].at[m - 1])
                if cp is not None:
                    cp.start()
        token[...] = jnp.zeros_like(token)

    side = max(most)
    res = pl.pallas_call(
        body, name="gather_start",
        out_shape=[PEER_SEMS] * (2 * n) + [pltpu.HBM((N_DEV * r, c), BF16) for r, c in shapes] + [SDS((8, LANES), F32)],
        in_specs=[HBM_SPEC] * n,
        out_specs=[SEM_SPEC] * (2 * n) + [HBM_SPEC] * n + [pl.BlockSpec(memory_space=pltpu.VMEM)],
        scratch_shapes=[pltpu.VMEM((side, side), F32), pltpu.VMEM(most, BF16), pltpu.SemaphoreType.DMA((1,))],
        compiler_params=pltpu.CompilerParams(has_side_effects=SIDE_EFFECT),
    )(*[_in_hbm(shards[name]) for name in names])
    return dict(send=res[:n], recv=res[n:2 * n], items=[(a,) for a in res[2 * n:3 * n]], token=res[-1], copies=copies)


def _gather_relay(name, started, which, after):
    (land,) = started["items"][which]

    def body(*refs):
        land_ref, old_send, old_recv = refs[0], refs[1], refs[2]
        send, recv, token = refs[3 + len(after)], refs[4 + len(after)], refs[-1]
        for m in range(1, N_DEV):
            cp = _gather_copy_near(m, [land_ref], old_send.at[m - 1], old_recv.at[m - 1])
            if cp is None:
                continue
            cp.wait_send()
            cp.wait_recv()
            if m > 1:
                _gather_copy_pass(m + 1, [land_ref], send.at[m], recv.at[m]).start()
        token[...] = jnp.zeros_like(token)

    res = pl.pallas_call(
        body, name=name,
        out_shape=[PEER_SEMS, PEER_SEMS, pltpu.HBM(land.shape, land.dtype), SDS((8, LANES), F32)],
        in_specs=[HBM_SPEC, SEM_SPEC, SEM_SPEC] + [ANY_SPEC] * len(after),
        out_specs=[SEM_SPEC, SEM_SPEC, HBM_SPEC, pl.BlockSpec(memory_space=pltpu.VMEM)],
        input_output_aliases={0: 2},
        compiler_params=pltpu.CompilerParams(has_side_effects=SIDE_EFFECT),
    )(land, started["send"][which], started["recv"][which], *[_in_hbm(a) for a in after])
    return dict(send=[res[0]], recv=[res[1]], items=[(res[2],)], token=res[3], copies=[_gather_copy_pass])


def _split_copy_wait(name, started, which, after):
    items = [started["items"][i] for i in which]
    copies = [started["copies"][i] for i in which]
    n = len(items)
    flat = [a for it in items for a in it]
    k = len(flat)

    def body(*refs):
        arr, sems = list(refs[:k]), refs[k:k + 2 * n]
        for i, it in enumerate(items):
            mine = [arr.pop(0) for _ in it]
            for m in range(1, N_DEV):
                cp = copies[i](m, mine, sems[i].at[m - 1], sems[n + i].at[m - 1])
                if cp is not None:
                    cp.wait_send()
                    cp.wait_recv()

    res = pl.pallas_call(
        body, name=name,
        out_shape=[pltpu.HBM(a.shape, a.dtype) for a in flat],
        in_specs=[HBM_SPEC] * k + [SEM_SPEC] * (2 * n) + [ANY_SPEC] * len(after),
        out_specs=[HBM_SPEC] * k,
        input_output_aliases={j: j for j in range(k)},
        compiler_params=pltpu.CompilerParams(has_side_effects=SIDE_EFFECT),
    )(*flat, *[started["send"][i] for i in which], *[started["recv"][i] for i in which], *[_in_hbm(a) for a in after])
    thru, out_items = list(res), []
    for it in items:
        out_items.append(tuple(thru.pop(0) for _ in it))
    return out_items


def _gather_copy(m, refs, send_sem, recv_sem):
    (land_ref,) = refs
    r = land_ref.shape[0] // N_DEV
    mine = land_ref.at[pl.ds(pl.multiple_of(_peer_index(0) * r, 8), r), :]
    return pltpu.make_async_remote_copy(src_ref=mine, dst_ref=mine, send_sem=send_sem, recv_sem=recv_sem,
                                        device_id=_peer(m), device_id_type=MESH)


def _gather_copy_near(m, refs, send_sem, recv_sem):
    return _gather_copy(m, refs, send_sem, recv_sem) if m == 1 or m % 2 == 0 else None


def _gather_copy_pass(m, refs, send_sem, recv_sem):
    if m == 1 or m % 2 == 0:
        return None
    (land_ref,) = refs
    r = land_ref.shape[0] // N_DEV
    block = land_ref.at[pl.ds(pl.multiple_of(_peer_index(m ^ 1) * r, 8), r), :]
    return pltpu.make_async_remote_copy(src_ref=block, dst_ref=block, send_sem=send_sem, recv_sem=recv_sem,
                                        device_id=_peer(1), device_id_type=MESH)


def _small_copy(m, refs, send_sem, recv_sem):
    part_ref, land_ref = refs
    return pltpu.make_async_remote_copy(src_ref=part_ref, dst_ref=land_ref.at[m - 1], send_sem=send_sem,
                                        recv_sem=recv_sem, device_id=_peer(m), device_id_type=MESH)


def _scatter_copy(m, refs, send_sem, recv_sem):
    buf_ref, land_ref = refs
    r = buf_ref.shape[0] // N_DEV
    src = buf_ref.at[pl.ds(pl.multiple_of(_peer_index(m) * r, 8), r), :]
    return pltpu.make_async_remote_copy(src_ref=src, dst_ref=land_ref.at[m - 1], send_sem=send_sem,
                                        recv_sem=recv_sem, device_id=_peer(m), device_id_type=MESH)


def _peer(m):
    x, y, c = _mesh_pos()
    bx, by, bc = (m >> 2) & 1, (m >> 1) & 1, m & 1
    return (x ^ bx if bx else x, y ^ by if by else y, c ^ bc if bc else c)


def _peer_index(m):
    x, y, c = _mesh_pos()
    return (4 * x + 2 * y + c) ^ m


SMALL_PLACE = {
    "ln1_gain": (ROW_LN1G, 0), "ln1_bias": (ROW_LN1B, 0), "ln2_gain": (ROW_LN2G, 0), "ln2_bias": (ROW_LN2B, 0),
    "ret_gn_gain": (ROW_GN, 0), "ret_decay_fwd": (ROW_MISC, MISC_DF), "ret_decay_bwd": (ROW_MISC, MISC_DB),
    "attn_sink": (ROW_MISC, MISC_SINK)}


def _small_adamw(me, part, landed, w, m, v):
    d = part.shape[1]
    names = list(SMALL_PLACE)
    k = len(names)

    def body(*refs):
        me_ref, part_ref, land_ref = refs[:3]
        refs = refs[2:]
        w_refs, m_refs, v_refs = refs[1:1 + k], refs[1 + k:1 + 2 * k], refs[1 + 2 * k:1 + 3 * k]
        outs = refs[1 + 3 * k:1 + 7 * k + 1]
        tot_ref = refs[-1]
        loss_ref, g_refs, dl_refs = outs[0], outs[1:1 + k], outs[1 + k:1 + 2 * k]
        nm_refs, nv_refs = outs[1 + 2 * k:1 + 3 * k], outs[1 + 3 * k:1 + 4 * k]
        tot = jnp.zeros(part_ref.shape, F32)
        for dev in range(N_DEV):
            j = dev ^ me_ref[0]
            tot = tot + jnp.where(j == 0, part_ref[...], land_ref[jnp.maximum(j, 1) - 1])
        tot_ref[...] = tot
        loss_ref[...] = (0.5 / d) * jnp.sum(tot_ref[ROW_LOSS:ROW_LOSS + 1, :], axis=1, keepdims=True)
        for i, name in enumerate(names):
            row, lo = SMALL_PLACE[name]
            wv = w_refs[i][...]
            g = tot_ref[row:row + 1, lo:lo + wv.shape[1]]
            if name.startswith("ret_decay"):
                p2 = jnp.exp2(wv)
                g = g * (-p2 * jnp.log(2.0) / (1.0 - p2))
            g_refs[i][...] = g
            _adamw_store(g, wv, m_refs[i][...], v_refs[i][...], dl_refs[i], nm_refs[i], nv_refs[i])

    shapes = [SDS(w[n].shape, F32) for n in names]
    vm = pl.BlockSpec(memory_space=pltpu.VMEM)
    res = pl.pallas_call(
        body, name="small_adamw", out_shape=[SDS((1, 1), F32)] + shapes * 4,
        in_specs=[_smem_spec()] + [vm] * (2 + 3 * k), out_specs=[vm] * (1 + 4 * k),
        scratch_shapes=[pltpu.VMEM(part.shape, F32)],
    )(me, part, landed, *[w[n] for n in names], *[m[n] for n in names], *[v[n] for n in names])
    groups = [dict(zip(names, res[1 + j * k:1 + (j + 1) * k])) for j in range(4)]
    return (res[0], *groups)


def _adamw_store(g, w, m, v, dl_ref, nm_ref, nv_ref):
    m = ADAM_B1 * m + (1.0 - ADAM_B1) * g
    v = ADAM_B2 * v + (1.0 - ADAM_B2) * (g * g)
    m_hat = m / (1.0 - ADAM_B1 ** ADAM_STEP)
    v_hat = v / (1.0 - ADAM_B2 ** ADAM_STEP)
    dl_ref[...] = -ADAM_LR * (m_hat / (jnp.sqrt(v_hat) + ADAM_EPS) + ADAM_WD * w)
    nm_ref[...] = m
    nv_ref[...] = v


def _reduce_adamw(name, owns, recvs, ws, ms, vs, transposed):
    count = len(owns)
    rows, n = owns[0].shape
    steps = 1 if transposed or rows % 32 else 4
    rb = rows // steps

    def body(*refs):
        ins, outs = refs[:5 * count], refs[5 * count:]
        j = pl.program_id(0)
        for k in range(count):
            @pl.when(j == k)
            def _(k=k):
                own_ref, recv_ref, w_ref, m_ref, v_ref = ins[5 * k:5 * k + 5]
                g_ref, dl_ref, nm_ref, nv_ref = outs[4 * k:4 * k + 4]
                g = own_ref[...]
                for p in range(recv_ref.shape[0]):
                    g = g + recv_ref[p].astype(F32)
                if transposed:
                    g = g.T
                g_ref[...] = g
                _adamw_store(g, w_ref[...], m_ref[...], v_ref[...], dl_ref, nm_ref, nv_ref)

    def turn(k):
        return lambda j, i: jnp.where(j == k, i, jnp.where(j < k, 0, steps - 1))

    in_specs, out_specs = [], []
    for k in range(count):
        at = turn(k)
        blk = pl.BlockSpec(ws[0].shape if transposed else (rb, n), lambda j, i, at=at: (at(j, i), 0))
        in_specs += [pl.BlockSpec((rb, n), lambda j, i, at=at: (at(j, i), 0)),
                     pl.BlockSpec((recvs[k].shape[0], rb, n), lambda j, i, at=at: (0, at(j, i), 0)), blk, blk, blk]
        out_specs += [blk] * 4
    res = pl.pallas_call(
        body, name="adamw_" + name, grid=(count, steps), in_specs=in_specs, out_specs=out_specs,
        out_shape=[SDS(ws[0].shape, F32)] * (4 * count), compiler_params=_params(("arbitrary", "arbitrary")),
    )(*[a for k in range(count) for a in (owns[k], recvs[k], ws[k], ms[k], vs[k])])
    return [list(res[j::4]) for j in range(4)]


def _row_spec(tm, width):
    return pl.BlockSpec((tm, width), lambda i: (i, 0))


def _full_spec(shape):
    return pl.BlockSpec(shape, lambda i: (0,) * len(shape))


_acc_spec = _full_spec


def _sub_rows(tm):
    step = min(SUB_ROWS, tm)
    return [(lo, lo + step) for lo in range(0, tm, step)]


def _in_proj(x2, wt_in):
    t, d = x2.shape
    u_w = wt_in.shape[0]
    tm = _row_tile(t, MATMUL_ROWS)

    def body(x_ref, w_ref, u_ref, xb_ref):
        xb = x_ref[...].astype(BF16)
        xb_ref[...] = xb
        u_ref[...] = _dot_nt(xb, w_ref[...]).astype(BF16)

    return pl.pallas_call(
        body, name="in_proj", grid=(t // tm,),
        in_specs=[_row_spec(tm, d), _full_spec(wt_in.shape)],
        out_specs=[_row_spec(tm, u_w), _row_spec(tm, d)],
        out_shape=[SDS((t, u_w), BF16), SDS((t, d), BF16)],
        compiler_params=_params(("parallel",)),
    )(x2, wt_in)


def _col_halves(f):
    n = f // LANES
    k = (n + 1) // 2 * LANES
    return [(0, k), (k, f)] if k < f else [(0, f)]


def _mix_ln1_ffn_up(r, a, x2, w_out, wt_gate, wt_up, g1, b1):
    t, d = x2.shape
    f = wt_gate.shape[0]
    tm = _row_tile(t, EPILOGUE_ROWS)

    def body(r_ref, a_ref, x_ref, wo_ref, wg_ref, wu_ref, g_ref, b_ref, xh_ref, rs_ref, hb_ref, dg_ref, du_ref,
             act_ref):
        mix = _dot_nn(r_ref[...], wo_ref[0:RET_W, :]) + _dot_nn(a_ref[...], wo_ref[RET_W:RET_W + ATT_W, :])
        z = ALPHA * x_ref[...] + mix
        xhat, rstd = _layer_norm_stats(z)
        xh_ref[...] = xhat
        rs_ref[...] = jnp.broadcast_to(rstd, rs_ref.shape)
        h = (xhat * g_ref[...] + b_ref[...]).astype(BF16)
        hb_ref[...] = h
        g = _dot_nt(h, wg_ref[...])
        u = _dot_nt(h, wu_ref[...])
        sg = _sigmoid(g)
        silu = g * sg
        dg_ref[...] = (u * (sg * (1.0 + g * (1.0 - sg)))).astype(BF16)
        du_ref[...] = silu.astype(BF16)
        act_ref[...] = (silu * u).astype(BF16)

    wide, narrow = _row_spec(tm, f), _row_spec(tm, d)
    return pl.pallas_call(
        body, name="mix_ln1_ffn_up", grid=(t // tm,),
        in_specs=[_row_spec(tm, RET_W), _row_spec(tm, ATT_W), narrow, _resident_spec(w_out.shape),
                  _resident_spec(wt_gate.shape), _resident_spec(wt_up.shape), _full_spec(g1.shape),
                  _full_spec(b1.shape)],
        out_specs=[narrow, _row_spec(tm, LANES), narrow, wide, wide, wide],
        out_shape=[SDS((t, d), F32), SDS((t, LANES), F32), SDS((t, d), BF16)] + [SDS((t, f), BF16)] * 3,
        compiler_params=_params(("parallel",)),
    )(r, a, x2, w_out, wt_gate, wt_up, g1, b1)


def _ffn_down_ln2_loss(act, dact_dg, dact_du, h1b, p2, xhat1, target, w_down, w_pg, wt_pe, g1, b1, g2, b2):
    t, d = xhat1.shape
    f = act.shape[1]
    pdim = p2.shape[1]
    tm = _row_tile(t, EPILOGUE_ROWS)

    def body(act_ref, fg_ref, fu_ref, hb_ref, p_ref, xh1_ref, tgt_ref, wd_ref, wpg_ref, wpe_ref, g1_ref, b1_ref,
             g2_ref, b2_ref, dz_ref, dzb_ref, ds_ref, dple_ref, dg_ref, du_ref, acc_ref):
        @pl.when(pl.program_id(0) == 0)
        def _():
            acc_ref[...] = jnp.zeros_like(acc_ref)

        for lo, hi in _sub_rows(tm):
            h1 = xh1_ref[lo:hi, :] * g1_ref[...] + b1_ref[...]
            pg = _sigmoid(_dot_nn(hb_ref[lo:hi, :], wpg_ref[...]))
            ple = _dot_nt(p_ref[lo:hi, :].astype(BF16), wpe_ref[...])
            gated = pg * ple
            dgate = gated * (1.0 - pg)
            ffn = _dot_nn(act_ref[lo:hi, :], wd_ref[...])
            z2 = ALPHA * h1 + gated + ffn
            xhat2, rstd2 = _layer_norm_stats(z2)
            err = xhat2 * g2_ref[...] + b2_ref[...] - tgt_ref[lo:hi, :]
            dy = err * (1.0 / d)
            dz = _layer_norm_bwd(dy * g2_ref[...], xhat2, rstd2)
            dzb = dz.astype(BF16)
            dz_ref[lo:hi, :] = dz
            dzb_ref[lo:hi, :] = dzb
            ds_ref[lo:hi, :] = (dz * dgate).astype(BF16)
            dple_ref[lo:hi, :] = (dz * pg).astype(BF16)
            acc_ref[0:1, :] += jnp.sum(err * err, axis=0, keepdims=True)
            acc_ref[1:2, :] += jnp.sum(dy * xhat2, axis=0, keepdims=True)
            acc_ref[2:3, :] += jnp.sum(dy, axis=0, keepdims=True)
            for c0, c1 in _col_halves(f):
                da = _dot_nt(dzb, wd_ref[c0:c1, :])
                dg_ref[lo:hi, c0:c1] = (da * fg_ref[lo:hi, c0:c1].astype(F32)).astype(BF16)
                du_ref[lo:hi, c0:c1] = (da * fu_ref[lo:hi, c0:c1].astype(F32)).astype(BF16)

    vec = _full_spec(g1.shape)
    wide, narrow = _row_spec(tm, f), _row_spec(tm, d)
    return pl.pallas_call(
        body, name="ffn_down_ln2_loss", grid=(t // tm,),
        in_specs=[wide, wide, wide, narrow, _row_spec(tm, pdim), narrow, narrow,
                  _full_spec(w_down.shape), _full_spec(w_pg.shape), _full_spec(wt_pe.shape), vec, vec, vec, vec],
        out_specs=[narrow] * 4 + [wide, wide, _acc_spec((8, d))],
        out_shape=[SDS((t, d), F32), SDS((t, d), BF16), SDS((t, d), BF16), SDS((t, d), BF16),
                   SDS((t, f), BF16), SDS((t, f), BF16), SDS((8, d), F32)],
        compiler_params=_params(("arbitrary",)),
    )(act, dact_dg, dact_du, h1b, p2, xhat1, target, w_down, w_pg, wt_pe, g1, b1, g2, b2)


def _after(after, body):
    k = len(after)
    return (lambda *refs: body(*refs[k:])), [ANY_SPEC] * k


def _resident_spec(shape):
    return pl.BlockSpec(shape, lambda i: (0,) * len(shape), pipeline_mode=pl.Buffered(1))


def _dh1_ln1_bwd(dz2, dg, dup, dsb, xhat1, rstd1, wt_gate, wt_up, w_pg, w_out, g1, after=()):
    t, d = dz2.shape
    f = dg.shape[1]
    tm = _row_tile(t, EPILOGUE_ROWS)

    def body(dz_ref, dg_ref, du_ref, ds_ref, xh1_ref, rs1_ref, wg_ref, wu_ref, wpg_ref, wo_ref, g1_ref,
             dz1_ref, dz1b_ref, dr_ref, da_ref, acc_ref):
        @pl.when(pl.program_id(0) == 0)
        def _():
            acc_ref[...] = jnp.zeros_like(acc_ref)

        for lo, hi in _sub_rows(tm):
            dh = (ALPHA * dz_ref[lo:hi, :] + _dot_nn(dg_ref[lo:hi, :], wg_ref[...])
                  + _dot_nn(du_ref[lo:hi, :], wu_ref[...]) + _dot_nt(ds_ref[lo:hi, :], wpg_ref[...]))
            xhat, rstd = xh1_ref[lo:hi, :], rs1_ref[lo:hi, 0:1]
            dz1 = _layer_norm_bwd(dh * g1_ref[...], xhat, rstd)
            dz1b = dz1.astype(BF16)
            dz1_ref[lo:hi, :] = dz1
            dz1b_ref[lo:hi, :] = dz1b
            acc_ref[0:1, :] += jnp.sum(dh * xhat, axis=0, keepdims=True)
            acc_ref[1:2, :] += jnp.sum(dh, axis=0, keepdims=True)
            dr_ref[lo:hi, :] = _dot_nt(dz1b, wo_ref[0:RET_W, :]).astype(BF16)
            da_ref[lo:hi, :] = _dot_nt(dz1b, wo_ref[RET_W:RET_W + ATT_W, :]).astype(BF16)

    body, lead = _after(after, body)
    return pl.pallas_call(
        body, name="dh1_ln1_bwd", grid=(t // tm,),
        in_specs=lead + [_row_spec(tm, d), _row_spec(tm, f), _row_spec(tm, f), _row_spec(tm, d), _row_spec(tm, d),
                         _row_spec(tm, LANES), _resident_spec(wt_gate.shape), _resident_spec(wt_up.shape), _resident_spec(w_pg.shape),
                         _resident_spec(w_out.shape), _full_spec(g1.shape)],
        out_specs=[_row_spec(tm, d), _row_spec(tm, d), _row_spec(tm, RET_W), _row_spec(tm, ATT_W), _acc_spec((8, d))],
        out_shape=[SDS((t, d), F32), SDS((t, d), BF16), SDS((t, RET_W), BF16), SDS((t, ATT_W), BF16),
                   SDS((8, d), F32)],
        compiler_params=_params(("arbitrary",)),
    )(*after, dz2, dg, dup, dsb, xhat1, rstd1, wt_gate, wt_up, w_pg, w_out, g1)


def _in_proj_bwd(dz1, parts, wt_in, after=()):
    t, d = dz1.shape
    tm = _row_tile(t, MATMUL_ROWS)
    widths = [p.shape[1] for p in parts]

    def body(*refs):
        dz_ref, part_refs, w_ref, dx_ref = refs[0], refs[1:1 + len(parts)], refs[-2], refs[-1]
        acc = ALPHA * dz_ref[...]
        lo = 0
        for p_ref, w in zip(part_refs, widths):
            acc = acc + _dot_nn(p_ref[...], w_ref[lo:lo + w, :])
            lo += w
        dx_ref[...] = acc

    body, lead = _after(after, body)
    return pl.pallas_call(
        body, name="in_proj_bwd", grid=(t // tm,),
        in_specs=lead + [_row_spec(tm, d)] + [_row_spec(tm, w) for w in widths] + [_full_spec(wt_in.shape)],
        out_specs=_row_spec(tm, d), out_shape=SDS((t, d), F32),
        compiler_params=_params(("parallel",)),
    )(*after, dz1, *parts, wt_in)


def _weight_grad(name, me, parts, rhs, after=()):
    t, n = rhs.shape
    widths = [p.shape[1] for p in parts]
    rows = sum(widths)
    own_rows = rows // N_DEV
    tk = _row_tile(t, MATMUL_ROWS)
    n_steps = t // tk
    step = 256

    def body(*refs):
        me_ref, part_refs, rhs_ref = refs[0], refs[1:1 + len(parts)], refs[1 + len(parts)]
        full_ref, own_ref, acc = refs[-3], refs[-2], refs[-1]
        i = pl.program_id(0)

        def products(first):
            b = rhs_ref[...].astype(BF16)
            lo = 0
            for p_ref, w in zip(part_refs, widths):
                for c0 in range(0, w, step):
                    c1 = min(c0 + step, w)
                    val = _dot_tn(p_ref[:, c0:c1].astype(BF16), b)
                    if first:
                        acc[lo + c0:lo + c1, :] = val
                    else:
                        acc[lo + c0:lo + c1, :] += val
                lo += w

        pl.when(i == 0)(functools.partial(products, True))
        pl.when(i > 0)(functools.partial(products, False))

        @pl.when(i == n_steps - 1)
        def _():
            full_ref[...] = acc[...].astype(BF16)
            own_ref[...] = acc[pl.ds(pl.multiple_of(me_ref[0] * own_rows, 8), own_rows), :]

    body, lead = _after(after, body)
    return pl.pallas_call(
        body, name=name, grid=(n_steps,),
        in_specs=lead + [_smem_spec()] + [_row_spec(tk, w) for w in widths] + [_row_spec(tk, n)],
        out_specs=[_full_spec((rows, n)), _full_spec((own_rows, n))],
        out_shape=[SDS((rows, n), BF16), SDS((own_rows, n), F32)],
        scratch_shapes=[pltpu.VMEM((rows, n), F32)],
        compiler_params=_params(("arbitrary",)),
    )(*after, me, *parts, rhs)


def _weight_grad_jobs(name, me, jobs, after=()):
    count = len(jobs)
    t = jobs[0][0].shape[0]
    tk = _row_tile(t, MATMUL_ROWS)
    n_steps = t // tk
    shapes = [(lhs.shape[1], rhs.shape[1]) for lhs, rhs in jobs]
    most_rows, most_cols = max(r for r, _ in shapes), max(n for _, n in shapes)
    step = 256

    def body(*refs):
        me_ref, lhs_refs, rhs_refs = refs[0], refs[1:1 + count], refs[1 + count:1 + 2 * count]
        full_refs, own_refs = refs[1 + 2 * count:1 + 3 * count], refs[1 + 3 * count:1 + 4 * count]
        acc, whole, mine, sems = refs[1 + 4 * count:]
        job, i = pl.program_id(0), pl.program_id(1)

        def leaving(j):
            rows, n = shapes[j]
            return (pltpu.make_async_copy(whole.at[0:rows, 0:n], full_refs[j], sems.at[0]),
                    pltpu.make_async_copy(mine.at[0:rows // N_DEV, 0:n], own_refs[j], sems.at[1]))

        def products(j, first):
            rows, n = shapes[j]
            b = rhs_refs[j][...].astype(BF16)
            for c0 in range(0, rows, step):
                c1 = min(c0 + step, rows)
                val = _dot_tn(lhs_refs[j][:, c0:c1].astype(BF16), b)
                if first:
                    acc[c0:c1, 0:n] = val
                else:
                    acc[c0:c1, 0:n] += val

        def finish(j):
            rows, n = shapes[j]
            own_rows = rows // N_DEV
            if j > 0:
                for cp in leaving(j - 1):
                    cp.wait()
            whole[0:rows, 0:n] = acc[0:rows, 0:n].astype(BF16)
            mine[0:own_rows, 0:n] = acc[pl.ds(pl.multiple_of(me_ref[0] * own_rows, 8), own_rows), 0:n]
            for cp in leaving(j):
                cp.start()
            if j == count - 1:
                for cp in leaving(j):
                    cp.wait()

        for j in range(count):
            pl.when((job == j) & (i == 0))(functools.partial(products, j, True))
            pl.when((job == j) & (i > 0))(functools.partial(products, j, False))
            pl.when((job == j) & (i == n_steps - 1))(functools.partial(finish, j))

    def turn(j):
        return lambda job, i: (jnp.where(job == j, i, jnp.where(job < j, 0, n_steps - 1)), 0)

    body, lead = _after(after, body)
    res = pl.pallas_call(
        body, name=name, grid=(count, n_steps),
        in_specs=lead + [_smem_spec()] + [pl.BlockSpec((tk, rows), turn(j)) for j, (rows, _) in enumerate(shapes)]
        + [pl.BlockSpec((tk, n), turn(j)) for j, (_, n) in enumerate(shapes)],
        out_specs=[ANY_SPEC] * (2 * count),
        out_shape=[SDS((rows, n), BF16) for rows, n in shapes] + [SDS((rows // N_DEV, n), F32) for rows, n in shapes],
        scratch_shapes=[pltpu.VMEM((most_rows, most_cols), F32), pltpu.VMEM((most_rows, most_cols), BF16),
                        pltpu.VMEM((most_rows // N_DEV, most_cols), F32), pltpu.SemaphoreType.DMA((2,))],
        compiler_params=_params(("arbitrary", "arbitrary")),
    )(*after, me, *[lhs for lhs, _ in jobs], *[rhs for _, rhs in jobs])
    return list(res[:count]), list(res[count:])


def _log_decay(decay_f, decay_b):
    def body(f_ref, b_ref, lf_ref, lb_ref):
        lf_ref[...] = jnp.log1p(-jnp.exp2(f_ref[...]))
        lb_ref[...] = jnp.log1p(-jnp.exp2(b_ref[...]))

    return pl.pallas_call(body, name="log_decay", out_shape=[SDS(decay_f.shape, F32)] * 2)(decay_f, decay_b)


def _chunk(ref, n):
    return ref[pl.ds(pl.multiple_of(n * CHUNK, CHUNK), CHUNK), :]


def _group_sum(is_a, v):
    sa = jnp.sum(jnp.where(is_a, v, 0.0), axis=1, keepdims=True)
    sb = jnp.sum(jnp.where(is_a, 0.0, v), axis=1, keepdims=True)
    return jnp.where(is_a, sa, sb)


def _seq_spec(s, col_block):
    return pl.BlockSpec((s, LANES), lambda b, h: (b, col_block + h))


def _smem_spec():
    return pl.BlockSpec(memory_space=pltpu.SMEM)


RET_UNROLL = 4


def _chunk_loop(n_chunks, body, init):
    u = RET_UNROLL if n_chunks % RET_UNROLL == 0 else 1

    def trip(i, carry):
        for j in range(u):
            carry = body(i * u + j, carry)
        return carry

    return lax.fori_loop(0, n_chunks // u, trip, init)


def _stacked_tables(lgf_ref, lgb_ref, pair):
    lane = lax.broadcasted_iota(jnp.int32, (1, LANES), 1)
    is_a = lane < HEAD_DIM
    lgf = jnp.where(is_a, lgf_ref[2 * pair], lgf_ref[2 * pair + 1])
    lgb = jnp.where(is_a, lgb_ref[2 * pair], lgb_ref[2 * pair + 1])
    row = lax.broadcasted_iota(jnp.int32, (CHUNK, 1), 0).astype(F32)
    kdec_f, qdec_f = jnp.exp(lgf * (CHUNK - 1.0 - row)), jnp.exp(lgf * (row + 1.0))
    kdec_b, qdec_b = jnp.exp(lgb * row), jnp.exp(lgb * (CHUNK - row))
    tab = dict(
        is_a=is_a, row=row, lam_f=jnp.exp(lgf * CHUNK), lam_b=jnp.exp(lgb * CHUNK),
        kdec=jnp.concatenate([kdec_f, kdec_b], axis=1), qdec=jnp.concatenate([qdec_f, qdec_b], axis=1),
        qexp=jnp.concatenate([jnp.broadcast_to(row + 1.0, (CHUNK, LANES)),
                              jnp.broadcast_to(CHUNK - row, (CHUNK, LANES))], axis=1),
        kexp=jnp.concatenate([jnp.broadcast_to(CHUNK - 1.0 - row, (CHUNK, LANES)),
                              jnp.broadcast_to(row, (CHUNK, LANES))], axis=1),
    )
    r = lax.broadcasted_iota(jnp.int32, (2 * LANES, LANES), 0)
    c = lax.broadcasted_iota(jnp.int32, (2 * LANES, LANES), 1)
    tab["diag2"] = ((r & (LANES - 1)) < HEAD_DIM) == (c < HEAD_DIM)
    i2 = lax.broadcasted_iota(jnp.int32, (2 * CHUNK, CHUNK), 0)
    j = lax.broadcasted_iota(jnp.int32, (2 * CHUNK, CHUNK), 1)
    head_b = i2 >= CHUNK
    diff = ((i2 & (CHUNK - 1)) - j).astype(F32)
    up, dn = jnp.maximum(diff, 0.0), jnp.maximum(-diff, 0.0)
    lgf2 = jnp.where(head_b, lgf_ref[2 * pair + 1], lgf_ref[2 * pair])
    lgb2 = jnp.where(head_b, lgb_ref[2 * pair + 1], lgb_ref[2 * pair])
    ef = jnp.where(diff >= 0, jnp.exp(lgf2 * up), 0.0)
    eb = jnp.where(diff <= 0, jnp.exp(lgb2 * dn), 0.0)
    tab["d2"] = ef + eb
    tab["df2"] = ef * up
    tab["db2"] = eb * dn
    return tab


def _stack_pair(is_a, x):
    zero = jnp.zeros_like(x)
    return jnp.concatenate([jnp.where(is_a, x, zero), jnp.where(is_a, zero, x)], axis=0)


def _unstack_pair(is_a, x2):
    return jnp.where(is_a, x2[0:CHUNK, :], x2[CHUNK:2 * CHUNK, :])


def _both_ways(x, dec):
    return (jnp.concatenate([x, x], axis=1) * dec).astype(BF16)


def _scan_states(n_chunks, st, up_rows, up_lam, down_rows, down_lam):
    zero = jnp.zeros((LANES, LANES), F32)

    def up(n, r):
        new = st[n, up_rows, :]
        st[n, up_rows, :] = r
        return r * up_lam + new

    def down(s, r):
        n = n_chunks - 1 - s
        new = st[n, down_rows, :]
        st[n, down_rows, :] = r
        return r * down_lam + new

    lax.fori_loop(0, n_chunks, up, zero)
    lax.fori_loop(0, n_chunks, down, zero)


FWD_ROWS, BWD_ROWS = pl.ds(0, LANES), pl.ds(LANES, LANES)


def _state_spec(n_chunks, pairs):
    return pl.BlockSpec((n_chunks, 2 * LANES, LANES), lambda b, h: (b * pairs + h, 0, 0))


ST_GAIN, ST_XF, ST_XB, ST_IFA, ST_IFB, ST_IBA, ST_IBB, ST_LF, ST_LB = 0, 1, 2, 3, 4, 5, 6, 8, 9
ST_ROWS = 16


GW = GROUP * HEAD_DIM
KEYS = 3 * BLOCK


def _attn_tables(g, bias_ref):
    r = lax.broadcasted_iota(jnp.int32, (GROUP * BLOCK, KEYS), 0)
    kj = lax.broadcasted_iota(jnp.int32, (GROUP * BLOCK, KEYS), 1)
    qi = r & (BLOCK - 1)
    hh = lax.shift_right_logical(r, 7)
    dist = jnp.abs(kj - BLOCK - qi)
    slope = jnp.exp2(-(GROUP * g + hh + 1).astype(F32) * (8.0 / ATTN_HEADS))
    inside = jnp.where(dist <= BLOCK, -slope * dist.astype(F32), NEG_INF)
    bias_ref[BIAS_INSIDE] = inside
    bias_ref[BIAS_FIRST] = jnp.where(kj >= BLOCK, inside, NEG_INF)
    bias_ref[BIAS_LAST] = jnp.where(kj < 2 * BLOCK, inside, NEG_INF)


BIAS_INSIDE, BIAS_FIRST, BIAS_LAST = 0, 1, 2


def _own_lanes(g):
    return lax.shift_right_logical(lax.broadcasted_iota(jnp.int32, (1, LANES), 1), 6) == g


def _mask_keys(x_ref, g, scale, pad_ref, s):
    pad_ref[0:BLOCK, :] = jnp.zeros((BLOCK, LANES), BF16)
    pad_ref[BLOCK + s:2 * BLOCK + s, :] = jnp.zeros((BLOCK, LANES), BF16)
    pad_ref[BLOCK:BLOCK + s, :] = jnp.where(_own_lanes(g), x_ref[...].astype(F32) * scale, 0.0).astype(BF16)


def _lane_block(x, j):
    return x[:, j * LANES:(j + 1) * LANES]


def _stack_heads(x, g):
    assert GROUP == 4 and GW == 2 * LANES
    x1 = pltpu.roll(x, HEAD_DIM, 1)
    keep = _own_lanes(g)
    zero = jnp.zeros((BLOCK, LANES), x.dtype)
    rows = []
    for h in range(GROUP):
        for_g0 = _lane_block(x, h // 2) if h % 2 == 0 else _lane_block(x1, ((h + 1) // 2) % 2)
        for_g1 = _lane_block(x, h // 2) if h % 2 == 1 else _lane_block(x1, h // 2)
        rows.append(jnp.where(keep, jnp.where(g == 0, for_g0, for_g1), zero))
    return jnp.concatenate(rows, axis=0)


def _unstack_heads(x4, g):
    p = [x4[h * BLOCK:(h + 1) * BLOCK, :] for h in range(GROUP)]
    cat = lambda a, b: jnp.concatenate([a, b], axis=1)
    in_place = jnp.where(g == 0, cat(p[0], p[2]), cat(p[1], p[3]))
    one_left = jnp.where(g == 0, cat(p[1], p[3]), cat(p[2], p[0]))
    return in_place + pltpu.roll(one_left, HEAD_DIM, 1)


def _sink_column(sink_ref, g):
    rh = lax.shift_right_logical(lax.broadcasted_iota(jnp.int32, (GROUP * BLOCK, 1), 0), 7)
    col = jnp.zeros((GROUP * BLOCK, 1), F32)
    for h in range(GROUP):
        col = jnp.where(rh == h, sink_ref[GROUP * g + h], col)
    return col


def _attn_probs(qm, k3, bias_ref, sink_col, n, s):
    which = jnp.where(n == 0, BIAS_FIRST, jnp.where(n == s // BLOCK - 1, BIAS_LAST, BIAS_INSIDE))
    logits = _dot_nt(qm, k3) + bias_ref[which]
    m = jnp.maximum(jnp.max(logits, axis=1, keepdims=True), sink_col)
    e = jnp.exp(logits - m)
    e_sink = jnp.exp(sink_col - m)
    inv = 1.0 / (jnp.sum(e, axis=1, keepdims=True) + e_sink)
    return e * inv, e_sink * inv


PAIRS_PER_KV = (RET_HEADS // 2) // KV_HEADS
FWD_ORDER = "rrarra"
BWD_ORDER = "rararr"


def _trip_order(order, chunks, blocks):
    if order.count("r") == chunks and order.count("a") == blocks:
        return order
    return "r" * chunks + "a" * blocks


def _mixers_fwd(u, lgf, lgb, gn_gain, sink, b_loc, after=()):
    t = u.shape[0]
    s = t // b_loc
    n_chunks = s // CHUNK
    pairs = RET_HEADS // 2
    trips = n_chunks // RET_UNROLL
    blocks_half = (s // BLOCK) // PAIRS_PER_KV
    per_trip = blocks_half // trips
    assert n_chunks % RET_UNROLL == 0 and blocks_half % trips == 0 and PAIRS_PER_KV == 2 and s >= 2 * BLOCK

    def body(lgf_ref, lgb_ref, sink_ref, q_ref, k_ref, v_ref, g_ref, gain_ref, aq_ref, ak_ref, av_ref,
             r_ref, xhat_ref, rstd_ref, a_ref, st, kpad, vpad, bias):
        pair = pl.program_id(1)
        g, half = lax.shift_right_logical(pair, 1), pair & 1
        tab = _stacked_tables(lgf_ref, lgb_ref, pair)
        is_a = tab["is_a"]

        @pl.when(half == 0)
        def _():
            _attn_tables(g, bias)
            _mask_keys(ak_ref, g, Q_SCALE, kpad, s)
            _mask_keys(av_ref, g, 1.0, vpad, s)

        def kv_body(n, _):
            k8 = _chunk(k_ref, n).astype(F32) * Q_SCALE
            st[n] = jnp.where(tab["diag2"], _dot_tn(_both_ways(k8, tab["kdec"]), _chunk(v_ref, n)), 0.0)
            return 0

        _chunk_loop(n_chunks, kv_body, 0)
        _scan_states(n_chunks, st, FWD_ROWS, tab["lam_f"], BWD_ROWS, tab["lam_b"])
        sink_col = _sink_column(sink_ref, g)

        def retention_chunk(n):
            q = _chunk(q_ref, n)
            k8 = (_chunk(k_ref, n).astype(F32) * Q_SCALE).astype(BF16)
            v = _chunk(v_ref, n)
            p2 = (_dot_nt(_stack_pair(is_a, q), k8) * tab["d2"]).astype(BF16)
            y = _unstack_pair(is_a, _dot_nn(p2, v))
            y = y + _dot_nn(_both_ways(q.astype(F32), tab["qdec"]), st[n].astype(BF16))
            rows = pl.ds(pl.multiple_of(n * CHUNK, CHUNK), CHUNK)
            mu = _group_sum(is_a, y) * (1.0 / HEAD_DIM)
            dlt = y - mu
            var = _group_sum(is_a, dlt * dlt) * (1.0 / HEAD_DIM)
            rstd = lax.rsqrt(var + GN_EPS)
            xhat = dlt * rstd
            xhat_ref[rows, :] = xhat
            rstd_ref[rows, :] = rstd
            gate = _chunk(g_ref, n).astype(F32)
            r_ref[rows, :] = (xhat * gain_ref[...] * gate * _sigmoid(gate)).astype(BF16)

        def attention_block(blk):
            n = half * blocks_half + blk
            rows = pl.ds(pl.multiple_of(blk * BLOCK, BLOCK), BLOCK)
            keys = pl.ds(pl.multiple_of(n * BLOCK, BLOCK), KEYS)
            p, _ = _attn_probs(_stack_heads(aq_ref[rows, :], g), kpad[keys, :], bias, sink_col, n, s)
            a_ref[rows, :] = _unstack_heads(_dot_nn(p.astype(BF16), vpad[keys, :]), g).astype(BF16)

        def trip(i, _):
            chunk, blk = 0, 0
            for kind in _trip_order(FWD_ORDER, RET_UNROLL, per_trip):
                if kind == "r":
                    retention_chunk(i * RET_UNROLL + chunk)
                    chunk += 1
                else:
                    attention_block(i * per_trip + blk)
                    blk += 1
            return 0

        lax.fori_loop(0, trips, trip, 0)

    lane_blk = lambda c0: _seq_spec(s, c0 // LANES)
    half_rows = blocks_half * BLOCK
    aq_spec = pl.BlockSpec((half_rows, GW), lambda b, h: (b * PAIRS_PER_KV + (h & 1), C_AQ // GW + h // 2))
    a_spec = pl.BlockSpec((half_rows, GW), lambda b, h: (b * PAIRS_PER_KV + (h & 1), h // 2))
    kv_spec = lambda c0: pl.BlockSpec((s, LANES), lambda b, h: (b, c0 // LANES))
    pad = pltpu.VMEM((s + 2 * BLOCK, LANES), BF16)
    body, lead = _after(after, body)
    return pl.pallas_call(
        body, name="mixers_fwd", grid=(b_loc, pairs),
        in_specs=lead + [_smem_spec(), _smem_spec(), _smem_spec(), lane_blk(C_RQ), lane_blk(C_RK), lane_blk(C_RV),
                         lane_blk(C_RG), pl.BlockSpec((1, LANES), lambda b, h: (0, h)), aq_spec, kv_spec(C_AK),
                         kv_spec(C_AV)],
        out_specs=[_seq_spec(s, 0), _seq_spec(s, 0), _seq_spec(s, 0), a_spec, _state_spec(n_chunks, pairs)],
        out_shape=[SDS((t, RET_W), BF16), SDS((t, RET_W), F32), SDS((t, RET_W), F32), SDS((t, ATT_W), BF16),
                   SDS((b_loc * pairs * n_chunks, 2 * LANES, LANES), F32)],
        scratch_shapes=[pad, pad, pltpu.VMEM((3, GROUP * BLOCK, KEYS), F32)],
        compiler_params=_params(("arbitrary", "arbitrary")),
    )(*after, lgf, lgb, sink, u, u, u, u, gn_gain, u, u, u)


def _mixers_bwd(u, xhat, rstd, states, dr, da, lgf, lgb, gn_gain, sink, b_loc, after=()):
    t = u.shape[0]
    s = t // b_loc
    n_chunks = s // CHUNK
    pairs = RET_HEADS // 2
    trips = n_chunks // RET_UNROLL
    blocks_half = (s // BLOCK) // PAIRS_PER_KV
    per_trip = blocks_half // trips
    assert n_chunks % RET_UNROLL == 0 and blocks_half % trips == 0 and PAIRS_PER_KV == 2 and s >= 2 * BLOCK

    def body(lgf_ref, lgb_ref, sink_ref, q_ref, k_ref, v_ref, g_ref, xhat_ref, rstd_ref, dr_ref, gain_ref,
             aq_ref, ak_ref, av_ref, do_ref, st,
             dq_ref, dk_ref, dv_ref, dg_ref, st_ref, daq_ref, dak_ref, dav_ref, dsink_ref,
             gr, dy_s, kpad, vpad, bias, dk_acc, dv_acc):
        pair = pl.program_id(1)
        g, half = lax.shift_right_logical(pair, 1), pair & 1
        tab = _stacked_tables(lgf_ref, lgb_ref, pair)
        is_a = tab["is_a"]
        gain = gain_ref[...]

        @pl.when(half == 0)
        def _():
            _attn_tables(g, bias)
            _mask_keys(ak_ref, g, Q_SCALE, kpad, s)
            _mask_keys(av_ref, g, 1.0, vpad, s)
            dsink_ref[...] = jnp.zeros_like(dsink_ref)

        @pl.when(pair == 0)
        def _():
            dk_acc[...] = jnp.zeros_like(dk_acc)
            dv_acc[...] = jnp.zeros_like(dv_acc)

        def norm_body(n, dgain):
            rows = pl.ds(pl.multiple_of(n * CHUNK, CHUNK), CHUNK)
            xhat, rstd = xhat_ref[rows, :], rstd_ref[rows, :]
            gate = g_ref[rows, :].astype(F32)
            sg = _sigmoid(gate)
            silu = gate * sg
            d_out = dr_ref[rows, :].astype(F32)
            dg_ref[rows, :] = (d_out * xhat * gain * (sg * (1.0 + gate * (1.0 - sg)))).astype(BF16)
            dxh = d_out * gain * silu
            m1 = _group_sum(is_a, dxh) * (1.0 / HEAD_DIM)
            m2 = _group_sum(is_a, dxh * xhat) * (1.0 / HEAD_DIM)
            dy = (rstd * (dxh - m1 - xhat * m2)).astype(BF16)
            dy_s[rows, :] = dy
            qf = q_ref[rows, :].astype(F32)
            gr[n] = jnp.where(tab["diag2"], _dot_tn(_both_ways(qf, tab["qdec"]), dy), 0.0)
            return dgain + jnp.sum(d_out * xhat * silu, axis=0, keepdims=True)

        colsum = lambda x: jnp.sum(x, axis=0, keepdims=True)

        def grad_body(n, carry):
            xfb, ifa, ifb, iba, ibb, lf, lb = carry
            rows = pl.ds(pl.multiple_of(n * CHUNK, CHUNK), CHUNK)
            q = q_ref[rows, :]
            qf = q.astype(F32)
            k8f = k_ref[rows, :].astype(F32) * Q_SCALE
            k8 = k8f.astype(BF16)
            v = v_ref[rows, :]
            dy = dy_s[rows, :]
            q2, dy2 = _stack_pair(is_a, q), _stack_pair(is_a, dy)
            sc = _dot_nt(q2, k8)
            dp = _dot_nt(dy2, v)
            a2 = (sc * tab["d2"]).astype(BF16)
            ds2 = (dp * tab["d2"]).astype(BF16)
            dq = _unstack_pair(is_a, _dot_nn(ds2, k8))
            dk = _dot_tn(ds2, q2)
            dv = _dot_tn(a2, dy2)
            prod = sc * dp
            pf, pb = prod * tab["df2"], prod * tab["db2"]
            ifa, ifb = ifa + colsum(pf[0:CHUNK, :]), ifb + colsum(pf[CHUNK:2 * CHUNK, :])
            iba, ibb = iba + colsum(pb[0:CHUNK, :]), ibb + colsum(pb[CHUNK:2 * CHUNK, :])
            states, sgrads = st[n], gr[n]
            sb, gb = states.astype(BF16), sgrads.astype(BF16)
            dqc = _dot_nt(dy, sb) * tab["qdec"]
            dkc = _dot_nt(v, gb) * tab["kdec"]
            dv = dv + _dot_nn(_both_ways(k8f, tab["kdec"]), gb)
            dq_ref[rows, :] = (dq + dqc[:, 0:LANES] + dqc[:, LANES:2 * LANES]).astype(BF16)
            dk_ref[rows, :] = ((dk + dkc[:, 0:LANES] + dkc[:, LANES:2 * LANES]) * Q_SCALE).astype(BF16)
            dv_ref[rows, :] = dv.astype(BF16)
            q2w, k2w = jnp.concatenate([qf, qf], axis=1), jnp.concatenate([k8f, k8f], axis=1)
            xfb = xfb + colsum(tab["qexp"] * q2w * dqc + tab["kexp"] * k2w * dkc)
            prod_s = sgrads * states
            lf, lb = lf + colsum(prod_s[0:LANES, :]), lb + colsum(prod_s[LANES:2 * LANES, :])
            return xfb, ifa, ifb, iba, ibb, lf, lb

        sink_col = _sink_column(sink_ref, g)
        head_row = lax.broadcasted_iota(jnp.int32, dsink_ref.shape, 0)

        def attention_block(blk):
            n = half * blocks_half + blk
            rows = pl.ds(pl.multiple_of(blk * BLOCK, BLOCK), BLOCK)
            keys = pl.ds(pl.multiple_of(n * BLOCK, BLOCK), KEYS)
            qm = _stack_heads(aq_ref[rows, :], g)
            k3, v3 = kpad[keys, :], vpad[keys, :]
            p, p_sink = _attn_probs(qm, k3, bias, sink_col, n, s)
            dom = _stack_heads(do_ref[rows, :], g)
            dp = _dot_nt(dom, v3)
            delta = jnp.sum(p * dp, axis=1, keepdims=True)
            ds_mat = (p * (dp - delta)).astype(BF16)
            daq_ref[rows, :] = _unstack_heads(_dot_nn(ds_mat, k3), g).astype(BF16)
            dk_acc[keys, :] += _dot_tn(ds_mat, qm) * Q_SCALE
            dv_acc[keys, :] += _dot_tn(p.astype(BF16), dom)
            w = p_sink * delta
            upd = jnp.zeros(dsink_ref.shape, F32)
            for h in range(GROUP):
                upd = upd + jnp.where(head_row == h, -jnp.sum(w[h * BLOCK:(h + 1) * BLOCK, :]), 0.0)
            dsink_ref[...] += upd

        dgain = _chunk_loop(n_chunks, norm_body, jnp.zeros((1, LANES), F32))
        _scan_states(n_chunks, gr, BWD_ROWS, tab["lam_b"], FWD_ROWS, tab["lam_f"])

        def trip(i, carry):
            chunk, blk = 0, 0
            for kind in _trip_order(BWD_ORDER, RET_UNROLL, per_trip):
                if kind == "r":
                    carry = grad_body(i * RET_UNROLL + chunk, carry)
                    chunk += 1
                else:
                    attention_block(i * per_trip + blk)
                    blk += 1
            return carry

        z = jnp.zeros((1, LANES), F32)
        init = (jnp.zeros((1, 2 * LANES), F32), z, z, z, z, z, z)
        xfb, ifa, ifb, iba, ibb, lf, lb = lax.fori_loop(0, trips, trip, init)
        st_ref[...] = jnp.zeros_like(st_ref)
        st_ref[ST_GAIN:ST_GAIN + 1, :] = dgain
        st_ref[ST_XF:ST_XF + 1, :] = xfb[:, 0:LANES]
        st_ref[ST_XB:ST_XB + 1, :] = xfb[:, LANES:2 * LANES]
        st_ref[ST_IFA:ST_IFA + 1, :] = ifa
        st_ref[ST_IFB:ST_IFB + 1, :] = ifb
        st_ref[ST_IBA:ST_IBA + 1, :] = iba
        st_ref[ST_IBB:ST_IBB + 1, :] = ibb
        st_ref[ST_LF:ST_LF + 1, :] = lf * (CHUNK * tab["lam_f"])
        st_ref[ST_LB:ST_LB + 1, :] = lb * (CHUNK * tab["lam_b"])

        @pl.when(pair == pairs - 1)
        def _():
            dak_ref[...] = dk_acc[BLOCK:BLOCK + s, :].astype(BF16)
            dav_ref[...] = dv_acc[BLOCK:BLOCK + s, :].astype(BF16)

    lane_blk = lambda c0: _seq_spec(s, c0 // LANES)
    seq0 = _seq_spec(s, 0)
    half_rows = blocks_half * BLOCK
    aq_spec = pl.BlockSpec((half_rows, GW), lambda b, h: (b * PAIRS_PER_KV + (h & 1), C_AQ // GW + h // 2))
    a_spec = pl.BlockSpec((half_rows, GW), lambda b, h: (b * PAIRS_PER_KV + (h & 1), h // 2))
    kv_spec = lambda c0: pl.BlockSpec((s, LANES), lambda b, h: (b, c0 // LANES))
    kv_out = pl.BlockSpec((s, LANES), lambda b, h: (b, 0))
    state = pltpu.VMEM((n_chunks, 2 * LANES, LANES), F32)
    pad = pltpu.VMEM((s + 2 * BLOCK, LANES), BF16)
    acc = pltpu.VMEM((s + 2 * BLOCK, LANES), F32)
    body, lead = _after(after, body)
    return pl.pallas_call(
        body, name="mixers_bwd", grid=(b_loc, pairs),
        in_specs=lead + [_smem_spec(), _smem_spec(), _smem_spec(), lane_blk(C_RQ), lane_blk(C_RK), lane_blk(C_RV),
                         lane_blk(C_RG), seq0, seq0, seq0, pl.BlockSpec((1, LANES), lambda b, h: (0, h)),
                         aq_spec, kv_spec(C_AK), kv_spec(C_AV), a_spec, _state_spec(n_chunks, pairs)],
        out_specs=[seq0] * 4 + [pl.BlockSpec((ST_ROWS, LANES), lambda b, h: (b, h)), a_spec, kv_out, kv_out,
                                pl.BlockSpec((8, LANES), lambda b, h: (b * KV_HEADS + h // 2, 0))],
        out_shape=[SDS((t, RET_W), BF16)] * 4 + [SDS((b_loc * ST_ROWS, RET_W), F32), SDS((t, ATT_W), BF16),
                                                   SDS((t, KV_W), BF16), SDS((t, KV_W), BF16),
                                                   SDS((b_loc * KV_HEADS * 8, LANES), F32)],
        scratch_shapes=[state, pltpu.VMEM((s, LANES), BF16), pad, pad,
                        pltpu.VMEM((3, GROUP * BLOCK, KEYS), F32), acc, acc],
        compiler_params=_params(("arbitrary", "arbitrary")),
    )(*after, lgf, lgb, sink, u, u, u, u, xhat, rstd, dr, gn_gain, u, u, u, da, states)


def _pack_small(acc2, acc1, ret_stats, dsink, b_loc, d):
    pairs = RET_HEADS // 2

    def body(acc2_ref, acc1_ref, st_ref, dsink_ref, out_ref):
        out_ref[...] = jnp.zeros_like(out_ref)
        out_ref[ROW_LN1G:ROW_LN1G + 1, :] = acc1_ref[0:1, :]
        out_ref[ROW_LN1B:ROW_LN1B + 1, :] = acc1_ref[1:2, :]
        out_ref[ROW_LN2G:ROW_LN2G + 1, :] = acc2_ref[1:2, :]
        out_ref[ROW_LN2B:ROW_LN2B + 1, :] = acc2_ref[2:3, :]
        out_ref[ROW_LOSS:ROW_LOSS + 1, :] = acc2_ref[0:1, :]
        st = st_ref[0:ST_ROWS, :]
        for b in range(1, b_loc):
            st = st + st_ref[b * ST_ROWS:(b + 1) * ST_ROWS, :]
        out_ref[ROW_GN:ROW_GN + 1, 0:RET_W] = st[ST_GAIN:ST_GAIN + 1, :]
        lane = lax.broadcasted_iota(jnp.int32, (1, d), 1)
        misc = jnp.zeros((1, d), F32)
        for pr in range(pairs):
            blk = st[:, pr * LANES:(pr + 1) * LANES]
            half = lax.broadcasted_iota(jnp.int32, (1, LANES), 1) < HEAD_DIM
            for h in range(2):
                sel = half if h == 0 else jnp.logical_not(half)
                cross_f = jnp.sum(jnp.where(sel, blk[ST_XF:ST_XF + 1, :] + blk[ST_LF:ST_LF + 1, :], 0.0))
                cross_b = jnp.sum(jnp.where(sel, blk[ST_XB:ST_XB + 1, :] + blk[ST_LB:ST_LB + 1, :], 0.0))
                intra_f = jnp.sum(blk[ST_IFA + h:ST_IFA + h + 1, :])
                intra_b = jnp.sum(blk[ST_IBA + h:ST_IBA + h + 1, :])
                head = 2 * pr + h
                misc = jnp.where(lane == MISC_DF + head, cross_f + intra_f, misc)
                misc = jnp.where(lane == MISC_DB + head, cross_b + intra_b, misc)
        for g in range(KV_HEADS):
            tot = dsink_ref[g * 8:(g + 1) * 8, :]
            for b in range(1, b_loc):
                tot = tot + dsink_ref[(b * KV_HEADS + g) * 8:(b * KV_HEADS + g + 1) * 8, :]
            for h in range(GROUP):
                misc = jnp.where(lane == MISC_SINK + GROUP * g + h, jnp.sum(tot[h:h + 1, 0:1]), misc)
        out_ref[ROW_MISC:ROW_MISC + 1, :] = misc

    return pl.pallas_call(body, name="pack_small", out_shape=SDS((SMALL_ROWS, d), F32))(acc2, acc1, ret_stats, dsink)


BIG = ("w_in", "w_out", "w_ffn_gate", "w_ffn_up", "w_ffn_down", "w_ple_proj", "w_ple_gate")
TRANSPOSED_OUTSIDE = ("w_in", "w_ffn_gate", "w_ffn_up")
TRANSPOSED_HERE = ("w_ple_proj",)
SMALL = ("ret_decay_fwd", "ret_decay_bwd", "ret_gn_gain", "attn_sink", "ln1_gain", "ln1_bias", "ln2_gain", "ln2_bias")
ORDER = ("w_in", "ret_decay_fwd", "ret_decay_bwd", "ret_gn_gain", "attn_sink", "w_out", "ln1_gain", "ln1_bias",
         "w_ffn_gate", "w_ffn_up", "w_ffn_down", "w_ple_proj", "w_ple_gate", "ln2_gain", "ln2_bias")


GATHER_ORDER = ("w_in", "w_ffn_up", "w_out", "w_ffn_gate", "w_ple_gate", "w_ple_proj", "w_ffn_down")
GATHER_TWO_LEVEL = ("w_in", "w_ffn_up")

def _local_step(x2, p2, target2, fetch, publish, small, b_loc, me):
    d = x2.shape[1]
    lgf, lgb = _log_decay(small["ret_decay_fwd"], small["ret_decay_bwd"])
    lgf1, lgb1, sink1 = lgf.reshape(-1), lgb.reshape(-1), small["attn_sink"].reshape(-1)
    (w_in,) = fetch(("w_in",), ())
    u, xb = _in_proj(x2, w_in)
    passed = fetch.pass_on("w_ffn_up", (xb,))
    r, ret_xhat, ret_rstd, a, ret_states = _mixers_fwd(u, lgf1, lgb1, small["ret_gn_gain"], sink1, b_loc, passed)
    w_out, w_gate, w_up = fetch(("w_out", "w_ffn_gate", "w_ffn_up"), (r, a))
    xhat1, rstd1, h1b, dact_dg, dact_du, act = _mix_ln1_ffn_up(
        r, a, x2, w_out, w_gate, w_up, small["ln1_gain"], small["ln1_bias"])
    w_pg, w_pe, w_down = fetch(("w_ple_gate", "w_ple_proj", "w_ffn_down"), (act,))
    dz2, dz2b, dsb, dpleb, dg, dup, acc2 = _ffn_down_ln2_loss(
        act, dact_dg, dact_du, h1b, p2, xhat1, target2, w_down, w_pg, w_pe,
        small["ln1_gain"], small["ln1_bias"], small["ln2_gain"], small["ln2_bias"])
    own = {}

    def grad(name, parts, rhs, after=()):
        whole, own[name] = _weight_grad("grad_" + name, me, parts, rhs, after)
        return whole

    ffn_jobs = dict(w_ffn_down=(act, dz2b), w_ple_proj=(dpleb, p2), w_ple_gate=(h1b, dsb),
                    w_ffn_gate=(dg, h1b), w_ffn_up=(dup, h1b))
    wholes, owns = _weight_grad_jobs("grad_w_ffn", me, list(ffn_jobs.values()))
    own.update(zip(ffn_jobs, owns))
    t2 = publish("ffn", dict(zip(ffn_jobs, wholes)))
    dz1, dz1b, dr, da, acc1 = _dh1_ln1_bwd(
        dz2, dg, dup, dsb, xhat1, rstd1, w_gate, w_up, w_pg, w_out, small["ln1_gain"], t2)
    t3 = publish("out", dict(w_out=grad("w_out", [r, a], dz1b)))
    dq, dk, dv, dgate, ret_stats, daq, dak, dav, dsink = _mixers_bwd(
        u, ret_xhat, ret_rstd, ret_states, dr, da, lgf1, lgb1, small["ret_gn_gain"], sink1, b_loc, t3)
    parts = [dq, dk, dv, dgate, daq, dak, dav]
    small_part = _pack_small(acc2, acc1, ret_stats, dsink, b_loc, d)
    t4 = publish("in", dict(w_in=grad("w_in", parts, xb)), small_part)
    grad_x = _in_proj_bwd(dz1, parts, w_in, t4)
    return grad_x, own, small_part


def kernel(x, p, w_in, ret_decay_fwd, ret_decay_bwd, ret_gn_gain, attn_sink, w_out, ln1_gain, ln1_bias, w_ffn_gate, w_ffn_up, w_ffn_down, w_ple_proj, w_ple_gate, ln2_gain, ln2_bias, loss_target, m_w_in, m_ret_decay_fwd, m_ret_decay_bwd, m_ret_gn_gain, m_attn_sink, m_w_out, m_ln1_gain, m_ln1_bias, m_w_ffn_gate, m_w_ffn_up, m_w_ffn_down, m_w_ple_proj, m_w_ple_gate, m_ln2_gain, m_ln2_bias, v_w_in, v_ret_decay_fwd, v_ret_decay_bwd, v_ret_gn_gain, v_attn_sink, v_w_out, v_ln1_gain, v_ln1_bias, v_w_ffn_gate, v_w_ffn_up, v_w_ffn_down, v_w_ple_proj, v_w_ple_gate, v_ln2_gain, v_ln2_bias):
    given = dict(locals())

    def strip(n, a):
        if n not in BIG:
            return a
        return a[0].T if n in TRANSPOSED_OUTSIDE else a[0]

    def restore(n, a):
        if n not in BIG:
            return a
        return (a.T if n in TRANSPOSED_OUTSIDE else a)[None]

    w = {n: strip(n, given[n]) for n in ORDER}
    m = {n: strip(n, given["m_" + n]) for n in ORDER}
    v = {n: strip(n, given["v_" + n]) for n in ORDER}
    b_loc, s, d = x.shape
    x2 = x.reshape(b_loc * s, d)
    p2 = p[0].reshape(b_loc * s, p.shape[-1])
    target2 = loss_target.reshape(b_loc * s, d)

    small = {n: w[n] for n in SMALL}
    me = (4 * lax.axis_index("x") + 2 * lax.axis_index("y") + lax.axis_index("c")).astype(jnp.int32).reshape(1)

    gather = _gather_start(
        {n: w[n] for n in GATHER_ORDER},
        [_gather_copy_near if n in GATHER_TWO_LEVEL else _gather_copy for n in GATHER_ORDER])

    passing = {}

    def pass_on(n, after):
        passing[n] = _gather_relay("gather_relay_" + n, gather, GATHER_ORDER.index(n), list(after))
        return (passing[n]["token"],)

    def fetch(names, after):
        out = {}
        for n in [n for n in names if n in GATHER_TWO_LEVEL]:
            if n not in passing:
                pass_on(n, after)
            out[n] = _split_copy_wait("gather_wait_" + n, passing[n], [0], list(after))[0][0]
        direct = [n for n in names if n not in GATHER_TWO_LEVEL]
        if direct:
            got = _split_copy_wait("gather_wait_" + direct[0], gather, [GATHER_ORDER.index(n) for n in direct],
                                   list(after))
            out.update({n: item[0] for n, item in zip(direct, got)})
        return [out[n] for n in names]

    scatters = []

    def publish(tag, products, small_sums=None):
        items = [(products[n], lax.empty((N_DEV - 1, products[n].shape[0] // N_DEV, products[n].shape[1]), BF16))
                 for n in products]
        copies = [_scatter_copy] * len(items)
        if small_sums is not None:
            items.append((small_sums, lax.empty((N_DEV - 1,) + small_sums.shape, F32)))
            copies.append(_small_copy)
        started = _split_copy_start("scatter_start_" + tag, items, copies)
        scatters.append((list(products), small_sums is not None, started))
        return (started["token"],)

    fetch.pass_on = pass_on
    grad_x, own, small_part = _local_step(x2, p2, target2, fetch, publish, small, b_loc, me)

    out_g, out_d, out_m, out_v = {}, {}, {}, {}
    after = [grad_x]
    for names, with_small, started in scatters:
        landed = _split_copy_wait("scatter_wait_" + names[0], started, list(range(len(started["items"]))), after)
        if with_small:
            mine, from_peers = landed[-1]
            loss, sg, sd, sm, sv = _small_adamw(
                me, mine, from_peers, small, {n: m[n] for n in SMALL}, {n: v[n] for n in SMALL})
            for dst, src in ((out_g, sg), (out_d, sd), (out_m, sm), (out_v, sv)):
                dst.update(src)
        recv = {n: item[1] for n, item in zip(names, landed)}
        alike = {}
        for n in names:
            alike.setdefault((own[n].shape, n in TRANSPOSED_HERE), []).append(n)
        for (_, transposed), ns in alike.items():
            res = _reduce_adamw(ns[0], [own[n] for n in ns], [recv[n] for n in ns], [w[n] for n in ns],
                                [m[n] for n in ns], [v[n] for n in ns], transposed)
            for dst, vals in zip((out_g, out_d, out_m, out_v), res):
                dst.update(zip(ns, vals))
        after = [out_v[names[-1]]]

    outs = [loss[0, 0], grad_x.reshape(x.shape)]
    for group in (out_g, out_d, out_m, out_v):
        outs += [restore(n, group[n]) for n in ORDER]
    return tuple(outs)
```

```python
import functools

import jax
import jax.numpy as jnp
from jax import lax
from jax.experimental import pallas as pl
from jax.experimental.pallas import tpu as pltpu

F32, BF16 = jnp.float32, jnp.bfloat16
SDS = jax.ShapeDtypeStruct
MESH = pl.DeviceIdType.MESH

N_DEV = 8
HEAD_DIM = 64
RET_HEADS = 8
ATTN_HEADS = 8
KV_HEADS = 2
GROUP = ATTN_HEADS // KV_HEADS
RET_W = RET_HEADS * HEAD_DIM
ATT_W = ATTN_HEADS * HEAD_DIM
KV_W = KV_HEADS * HEAD_DIM
LANES = 128
CHUNK = 128
BLOCK = 128
Q_SCALE = HEAD_DIM ** -0.5
ALPHA = 2.0 ** 0.25
LN_EPS = 1e-5
GN_EPS = 1e-5
NEG_INF = -1e30
C_RQ, C_RK, C_RV, C_RG = 0, RET_W, 2 * RET_W, 3 * RET_W
C_AQ = 4 * RET_W
C_AK = C_AQ + ATT_W
C_AV = C_AK + KV_W
IN_W = C_AV + KV_W

ADAM_LR = 0.001
ADAM_B1 = 0.9
ADAM_B2 = 0.999
ADAM_EPS = 1e-08
ADAM_WD = 0.01
ADAM_STEP = 10

VMEM_LIMIT = 56 * 1024 * 1024
MATMUL_ROWS = 512
EPILOGUE_ROWS = 256
SUB_ROWS = 256
SMALL_ROWS = 16
ROW_LN1G, ROW_LN1B, ROW_LN2G, ROW_LN2B, ROW_LOSS, ROW_GN, ROW_MISC = 0, 1, 2, 3, 4, 5, 6
MISC_DF, MISC_DB, MISC_SINK = 0, 8, 16


def _dot_nn(a, b):
    return lax.dot_general(a, b, (((1,), (0,)), ((), ())), preferred_element_type=F32)


def _dot_nt(a, b):
    return lax.dot_general(a, b, (((1,), (1,)), ((), ())), preferred_element_type=F32)


def _dot_tn(a, b):
    return lax.dot_general(a, b, (((0,), (0,)), ((), ())), preferred_element_type=F32)


def _params(sem=None, vmem=VMEM_LIMIT):
    kw = {"vmem_limit_bytes": vmem}
    if sem is not None:
        kw["dimension_semantics"] = sem
    return pltpu.CompilerParams(**kw)


def _row_tile(t, want=512):
    tm = want
    while t % tm:
        tm //= 2
    return tm


def _sigmoid(x):
    return jax.nn.sigmoid(x)


def _layer_norm_stats(z):
    mu = jnp.mean(z, axis=1, keepdims=True)
    d = z - mu
    var = jnp.mean(d * d, axis=1, keepdims=True)
    rstd = lax.rsqrt(var + LN_EPS)
    return d * rstd, rstd


def _layer_norm_bwd(dxh, xhat, rstd):
    m1 = jnp.mean(dxh, axis=1, keepdims=True)
    m2 = jnp.mean(dxh * xhat, axis=1, keepdims=True)
    return rstd * (dxh - m1 - xhat * m2)


def _mesh_pos():
    return lax.axis_index("x"), lax.axis_index("y"), lax.axis_index("c")


HBM_SPEC = pl.BlockSpec(memory_space=pltpu.HBM)
SEM_SPEC = pl.BlockSpec(memory_space=pltpu.SEMAPHORE)
ANY_SPEC = pl.BlockSpec(memory_space=pl.ANY)
SIDE_EFFECT = pltpu.SideEffectType.DATAFLOW_SIDE_EFFECTING
PEER_SEMS = pltpu.SemaphoreType.DMA((N_DEV - 1,))


def _in_hbm(a):
    return pltpu.with_memory_space_constraint(a, pltpu.HBM)


def _split_copy_start(name, items, copies):
    n = len(items)
    flat = [a for it in items for a in it]
    k = len(flat)

    def body(*refs):
        arr, sems = list(refs[:k]), refs[k:k + 2 * n]
        for i, it in enumerate(items):
            mine = [arr.pop(0) for _ in it]
            for m in range(1, N_DEV):
                cp = copies[i](m, mine, sems[i].at[m - 1], sems[n + i].at[m - 1])
                if cp is not None:
                    cp.start()
        token = refs[-1]
        token[...] = jnp.zeros_like(token)

    res = pl.pallas_call(
        body, name=name,
        out_shape=[PEER_SEMS] * (2 * n) + [pltpu.HBM(a.shape, a.dtype) for a in flat] + [SDS((8, LANES), F32)],
        in_specs=[HBM_SPEC] * k,
        out_specs=[SEM_SPEC] * (2 * n) + [HBM_SPEC] * k + [pl.BlockSpec(memory_space=pltpu.VMEM)],
        input_output_aliases={j: 2 * n + j for j in range(k)},
        compiler_params=pltpu.CompilerParams(has_side_effects=SIDE_EFFECT),
    )(*[_in_hbm(a) for a in flat])
    thru, out_items = list(res[2 * n:2 * n + k]), []
    for it in items:
        out_items.append(tuple(thru.pop(0) for _ in it))
    return dict(send=res[:n], recv=res[n:2 * n], items=out_items, token=res[-1], copies=copies)


def _gather_start(shards, copies):
    names = list(shards)
    n = len(names)
    flip = [name in TRANSPOSED_HERE for name in names]
    shapes = [shards[name].shape[::-1] if f else shards[name].shape for name, f in zip(names, flip)]
    most = (max(s[0] for s in shapes), max(s[1] for s in shapes))

    def body(*refs):
        src, sems, land, token = refs[:n], refs[n:3 * n], refs[3 * n:4 * n], refs[4 * n]
        wide, narrow, sem = refs[4 * n + 1:]
        for i, (rows, cols) in enumerate(shapes):
            raw = wide.at[0:cols, 0:rows] if flip[i] else wide.at[0:rows, 0:cols]
            bring = pltpu.make_async_copy(src[i], raw, sem.at[0])
            bring.start()
            bring.wait()
            narrow[0:rows, 0:cols] = (raw[...].T if flip[i] else raw[...]).astype(BF16)
            mine = land[i].at[pl.ds(pl.multiple_of(_peer_index(0) * rows, 8), rows), :]
            place = pltpu.make_async_copy(narrow.at[0:rows, 0:cols], mine, sem.at[0])
            place.start()
            place.wait()
            for m in range(1, N_DEV):
                cp = copies[i](m, [land[i]], sems[i].at[m - 1], sems[n + i].at[m - 1])
                if cp is not None:
                    cp.start()
        token[...] = jnp.zeros_like(token)

    side = max(most)
    res = pl.pallas_call(
        body, name="gather_start",
        out_shape=[PEER_SEMS] * (2 * n) + [pltpu.HBM((N_DEV * r, c), BF16) for r, c in shapes] + [SDS((8, LANES), F32)],
        in_specs=[HBM_SPEC] * n,
        out_specs=[SEM_SPEC] * (2 * n) + [HBM_SPEC] * n + [pl.BlockSpec(memory_space=pltpu.VMEM)],
        scratch_shapes=[pltpu.VMEM((side, side), F32), pltpu.VMEM(most, BF16), pltpu.SemaphoreType.DMA((1,))],
        compiler_params=pltpu.CompilerParams(has_side_effects=SIDE_EFFECT),
    )(*[_in_hbm(shards[name]) for name in names])
    return dict(send=res[:n], recv=res[n:2 * n], items=[(a,) for a in res[2 * n:3 * n]], token=res[-1], copies=copies)


def _gather_relay(name, started, which, after):
    (land,) = started["items"][which]

    def body(*refs):
        land_ref, old_send, old_recv = refs[0], refs[1], refs[2]
        send, recv, token = refs[3 + len(after)], refs[4 + len(after)], refs[-1]
        for m in range(1, N_DEV):
            cp = _gather_copy_near(m, [land_ref], old_send.at[m - 1], old_recv.at[m - 1])
            if cp is None:
                continue
            cp.wait_send()
            cp.wait_recv()
            if m > 1:
                _gather_copy_pass(m + 1, [land_ref], send.at[m], recv.at[m]).start()
        token[...] = jnp.zeros_like(token)

    res = pl.pallas_call(
        body, name=name,
        out_shape=[PEER_SEMS, PEER_SEMS, pltpu.HBM(land.shape, land.dtype), SDS((8, LANES), F32)],
        in_specs=[HBM_SPEC, SEM_SPEC, SEM_SPEC] + [ANY_SPEC] * len(after),
        out_specs=[SEM_SPEC, SEM_SPEC, HBM_SPEC, pl.BlockSpec(memory_space=pltpu.VMEM)],
        input_output_aliases={0: 2},
        compiler_params=pltpu.CompilerParams(has_side_effects=SIDE_EFFECT),
    )(land, started["send"][which], started["recv"][which], *[_in_hbm(a) for a in after])
    return dict(send=[res[0]], recv=[res[1]], items=[(res[2],)], token=res[3], copies=[_gather_copy_pass])


def _split_copy_wait(name, started, which, after):
    items = [started["items"][i] for i in which]
    copies = [started["copies"][i] for i in which]
    n = len(items)
    flat = [a for it in items for a in it]
    k = len(flat)

    def body(*refs):
        arr, sems = list(refs[:k]), refs[k:k + 2 * n]
        for i, it in enumerate(items):
            mine = [arr.pop(0) for _ in it]
            for m in range(1, N_DEV):
                cp = copies[i](m, mine, sems[i].at[m - 1], sems[n + i].at[m - 1])
                if cp is not None:
                    cp.wait_send()
                    cp.wait_recv()

    res = pl.pallas_call(
        body, name=name,
        out_shape=[pltpu.HBM(a.shape, a.dtype) for a in flat],
        in_specs=[HBM_SPEC] * k + [SEM_SPEC] * (2 * n) + [ANY_SPEC] * len(after),
        out_specs=[HBM_SPEC] * k,
        input_output_aliases={j: j for j in range(k)},
        compiler_params=pltpu.CompilerParams(has_side_effects=SIDE_EFFECT),
    )(*flat, *[started["send"][i] for i in which], *[started["recv"][i] for i in which], *[_in_hbm(a) for a in after])
    thru, out_items = list(res), []
    for it in items:
        out_items.append(tuple(thru.pop(0) for _ in it))
    return out_items


def _gather_copy(m, refs, send_sem, recv_sem):
    (land_ref,) = refs
    r = land_ref.shape[0] // N_DEV
    mine = land_ref.at[pl.ds(pl.multiple_of(_peer_index(0) * r, 8), r), :]
    return pltpu.make_async_remote_copy(src_ref=mine, dst_ref=mine, send_sem=send_sem, recv_sem=recv_sem,
                                        device_id=_peer(m), device_id_type=MESH)


def _gather_copy_near(m, refs, send_sem, recv_sem):
    return _gather_copy(m, refs, send_sem, recv_sem) if m == 1 or m % 2 == 0 else None


def _gather_copy_pass(m, refs, send_sem, recv_sem):
    if m == 1 or m % 2 == 0:
        return None
    (land_ref,) = refs
    r = land_ref.shape[0] // N_DEV
    block = land_ref.at[pl.ds(pl.multiple_of(_peer_index(m ^ 1) * r, 8), r), :]
    return pltpu.make_async_remote_copy(src_ref=block, dst_ref=block, send_sem=send_sem, recv_sem=recv_sem,
                                        device_id=_peer(1), device_id_type=MESH)


def _small_copy(m, refs, send_sem, recv_sem):
    part_ref, land_ref = refs
    return pltpu.make_async_remote_copy(src_ref=part_ref, dst_ref=land_ref.at[m - 1], send_sem=send_sem,
                                        recv_sem=recv_sem, device_id=_peer(m), device_id_type=MESH)


def _scatter_copy(m, refs, send_sem, recv_sem):
    buf_ref, land_ref = refs
    r = buf_ref.shape[0] // N_DEV
    src = buf_ref.at[pl.ds(pl.multiple_of(_peer_index(m) * r, 8), r), :]
    return pltpu.make_async_remote_copy(src_ref=src, dst_ref=land_ref.at[m - 1], send_sem=send_sem,
                                        recv_sem=recv_sem, device_id=_peer(m), device_id_type=MESH)


def _peer(m):
    x, y, c = _mesh_pos()
    bx, by, bc = (m >> 2) & 1, (m >> 1) & 1, m & 1
    return (x ^ bx if bx else x, y ^ by if by else y, c ^ bc if bc else c)


def _peer_index(m):
    x, y, c = _mesh_pos()
    return (4 * x + 2 * y + c) ^ m


SMALL_PLACE = {
    "ln1_gain": (ROW_LN1G, 0), "ln1_bias": (ROW_LN1B, 0), "ln2_gain": (ROW_LN2G, 0), "ln2_bias": (ROW_LN2B, 0),
    "ret_gn_gain": (ROW_GN, 0), "ret_decay_fwd": (ROW_MISC, MISC_DF), "ret_decay_bwd": (ROW_MISC, MISC_DB),
    "attn_sink": (ROW_MISC, MISC_SINK)}


def _small_adamw(me, part, landed, w, m, v):
    d = part.shape[1]
    names = list(SMALL_PLACE)
    k = len(names)

    def body(*refs):
        me_ref, part_ref, land_ref = refs[:3]
        refs = refs[2:]
        w_refs, m_refs, v_refs = refs[1:1 + k], refs[1 + k:1 + 2 * k], refs[1 + 2 * k:1 + 3 * k]
        outs = refs[1 + 3 * k:1 + 7 * k + 1]
        tot_ref = refs[-1]
        loss_ref, g_refs, dl_refs = outs[0], outs[1:1 + k], outs[1 + k:1 + 2 * k]
        nm_refs, nv_refs = outs[1 + 2 * k:1 + 3 * k], outs[1 + 3 * k:1 + 4 * k]
        tot = jnp.zeros(part_ref.shape, F32)
        for dev in range(N_DEV):
            j = dev ^ me_ref[0]
            tot = tot + jnp.where(j == 0, part_ref[...], land_ref[jnp.maximum(j, 1) - 1])
        tot_ref[...] = tot
        loss_ref[...] = (0.5 / d) * jnp.sum(tot_ref[ROW_LOSS:ROW_LOSS + 1, :], axis=1, keepdims=True)
        for i, name in enumerate(names):
            row, lo = SMALL_PLACE[name]
            wv = w_refs[i][...]
            g = tot_ref[row:row + 1, lo:lo + wv.shape[1]]
            if name.startswith("ret_decay"):
                p2 = jnp.exp2(wv)
                g = g * (-p2 * jnp.log(2.0) / (1.0 - p2))
            g_refs[i][...] = g
            _adamw_store(g, wv, m_refs[i][...], v_refs[i][...], dl_refs[i], nm_refs[i], nv_refs[i])

    shapes = [SDS(w[n].shape, F32) for n in names]
    vm = pl.BlockSpec(memory_space=pltpu.VMEM)
    res = pl.pallas_call(
        body, name="small_adamw", out_shape=[SDS((1, 1), F32)] + shapes * 4,
        in_specs=[_smem_spec()] + [vm] * (2 + 3 * k), out_specs=[vm] * (1 + 4 * k),
        scratch_shapes=[pltpu.VMEM(part.shape, F32)],
    )(me, part, landed, *[w[n] for n in names], *[m[n] for n in names], *[v[n] for n in names])
    groups = [dict(zip(names, res[1 + j * k:1 + (j + 1) * k])) for j in range(4)]
    return (res[0], *groups)


def _adamw_store(g, w, m, v, dl_ref, nm_ref, nv_ref):
    m = ADAM_B1 * m + (1.0 - ADAM_B1) * g
    v = ADAM_B2 * v + (1.0 - ADAM_B2) * (g * g)
    m_hat = m / (1.0 - ADAM_B1 ** ADAM_STEP)
    v_hat = v / (1.0 - ADAM_B2 ** ADAM_STEP)
    dl_ref[...] = -ADAM_LR * (m_hat / (jnp.sqrt(v_hat) + ADAM_EPS) + ADAM_WD * w)
    nm_ref[...] = m
    nv_ref[...] = v


def _reduce_adamw(name, owns, recvs, ws, ms, vs, transposed):
    count = len(owns)
    rows, n = owns[0].shape
    steps = 1 if transposed or rows % 16 else 2
    rb = rows // steps

    def body(*refs):
        ins, outs = refs[:5 * count], refs[5 * count:]
        j = pl.program_id(0)
        for k in range(count):
            @pl.when(j == k)
            def _(k=k):
                own_ref, recv_ref, w_ref, m_ref, v_ref = ins[5 * k:5 * k + 5]
                g_ref, dl_ref, nm_ref, nv_ref = outs[4 * k:4 * k + 4]
                g = own_ref[...]
                for p in range(recv_ref.shape[0]):
                    g = g + recv_ref[p].astype(F32)
                if transposed:
                    g = g.T
                g_ref[...] = g
                _adamw_store(g, w_ref[...], m_ref[...], v_ref[...], dl_ref, nm_ref, nv_ref)

    def turn(k):
        return lambda j, i: jnp.where(j == k, i, jnp.where(j < k, 0, steps - 1))

    in_specs, out_specs = [], []
    for k in range(count):
        at = turn(k)
        blk = pl.BlockSpec(ws[0].shape if transposed else (rb, n), lambda j, i, at=at: (at(j, i), 0))
        in_specs += [pl.BlockSpec((rb, n), lambda j, i, at=at: (at(j, i), 0)),
                     pl.BlockSpec((recvs[k].shape[0], rb, n), lambda j, i, at=at: (0, at(j, i), 0)), blk, blk, blk]
        out_specs += [blk] * 4
    res = pl.pallas_call(
        body, name="adamw_" + name, grid=(count, steps), in_specs=in_specs, out_specs=out_specs,
        out_shape=[SDS(ws[0].shape, F32)] * (4 * count), compiler_params=_params(("arbitrary", "arbitrary")),
    )(*[a for k in range(count) for a in (owns[k], recvs[k], ws[k], ms[k], vs[k])])
    return [list(res[j::4]) for j in range(4)]


def _row_spec(tm, width):
    return pl.BlockSpec((tm, width), lambda i: (i, 0))


def _full_spec(shape):
    return pl.BlockSpec(shape, lambda i: (0,) * len(shape))


_acc_spec = _full_spec


def _sub_rows(tm):
    step = min(SUB_ROWS, tm)
    return [(lo, lo + step) for lo in range(0, tm, step)]


def _in_proj(x2, wt_in):
    t, d = x2.shape
    u_w = wt_in.shape[0]
    tm = _row_tile(t, MATMUL_ROWS)

    def body(x_ref, w_ref, u_ref, xb_ref):
        xb = x_ref[...].astype(BF16)
        xb_ref[...] = xb
        u_ref[...] = _dot_nt(xb, w_ref[...]).astype(BF16)

    return pl.pallas_call(
        body, name="in_proj", grid=(t // tm,),
        in_specs=[_row_spec(tm, d), _full_spec(wt_in.shape)],
        out_specs=[_row_spec(tm, u_w), _row_spec(tm, d)],
        out_shape=[SDS((t, u_w), BF16), SDS((t, d), BF16)],
        compiler_params=_params(("parallel",)),
    )(x2, wt_in)


def _col_halves(f):
    n = f // LANES
    k = (n + 1) // 2 * LANES
    return [(0, k), (k, f)] if k < f else [(0, f)]


def _mix_ln1_ffn_up(r, a, x2, w_out, wt_gate, wt_up, g1, b1):
    t, d = x2.shape
    f = wt_gate.shape[0]
    tm = _row_tile(t, EPILOGUE_ROWS)

    def body(r_ref, a_ref, x_ref, wo_ref, wg_ref, wu_ref, g_ref, b_ref, xh_ref, rs_ref, hb_ref, dg_ref, du_ref,
             act_ref):
        mix = _dot_nn(r_ref[...], wo_ref[0:RET_W, :]) + _dot_nn(a_ref[...], wo_ref[RET_W:RET_W + ATT_W, :])
        z = ALPHA * x_ref[...] + mix
        xhat, rstd = _layer_norm_stats(z)
        xh_ref[...] = xhat
        rs_ref[...] = jnp.broadcast_to(rstd, rs_ref.shape)
        h = (xhat * g_ref[...] + b_ref[...]).astype(BF16)
        hb_ref[...] = h
        g = _dot_nt(h, wg_ref[...])
        u = _dot_nt(h, wu_ref[...])
        sg = _sigmoid(g)
        silu = g * sg
        dg_ref[...] = (u * (sg * (1.0 + g * (1.0 - sg)))).astype(BF16)
        du_ref[...] = silu.astype(BF16)
        act_ref[...] = (silu * u).astype(BF16)

    wide, narrow = _row_spec(tm, f), _row_spec(tm, d)
    return pl.pallas_call(
        body, name="mix_ln1_ffn_up", grid=(t // tm,),
        in_specs=[_row_spec(tm, RET_W), _row_spec(tm, ATT_W), narrow, _resident_spec(w_out.shape),
                  _resident_spec(wt_gate.shape), _resident_spec(wt_up.shape), _full_spec(g1.shape),
                  _full_spec(b1.shape)],
        out_specs=[narrow, _row_spec(tm, LANES), narrow, wide, wide, wide],
        out_shape=[SDS((t, d), F32), SDS((t, LANES), F32), SDS((t, d), BF16)] + [SDS((t, f), BF16)] * 3,
        compiler_params=_params(("parallel",)),
    )(r, a, x2, w_out, wt_gate, wt_up, g1, b1)


def _ffn_down_ln2_loss(act, dact_dg, dact_du, h1b, p2, xhat1, target, w_down, w_pg, wt_pe, g1, b1, g2, b2):
    t, d = xhat1.shape
    f = act.shape[1]
    pdim = p2.shape[1]
    tm = _row_tile(t, EPILOGUE_ROWS)

    def body(act_ref, fg_ref, fu_ref, hb_ref, p_ref, xh1_ref, tgt_ref, wd_ref, wpg_ref, wpe_ref, g1_ref, b1_ref,
             g2_ref, b2_ref, dz_ref, dzb_ref, ds_ref, dple_ref, dg_ref, du_ref, acc_ref):
        @pl.when(pl.program_id(0) == 0)
        def _():
            acc_ref[...] = jnp.zeros_like(acc_ref)

        for lo, hi in _sub_rows(tm):
            h1 = xh1_ref[lo:hi, :] * g1_ref[...] + b1_ref[...]
            pg = _sigmoid(_dot_nn(hb_ref[lo:hi, :], wpg_ref[...]))
            ple = _dot_nt(p_ref[lo:hi, :].astype(BF16), wpe_ref[...])
            gated = pg * ple
            dgate = gated * (1.0 - pg)
            ffn = _dot_nn(act_ref[lo:hi, :], wd_ref[...])
            z2 = ALPHA * h1 + gated + ffn
            xhat2, rstd2 = _layer_norm_stats(z2)
            err = xhat2 * g2_ref[...] + b2_ref[...] - tgt_ref[lo:hi, :]
            dy = err * (1.0 / d)
            dz = _layer_norm_bwd(dy * g2_ref[...], xhat2, rstd2)
            dzb = dz.astype(BF16)
            dz_ref[lo:hi, :] = dz
            dzb_ref[lo:hi, :] = dzb
            ds_ref[lo:hi, :] = (dz * dgate).astype(BF16)
            dple_ref[lo:hi, :] = (dz * pg).astype(BF16)
            acc_ref[0:1, :] += jnp.sum(err * err, axis=0, keepdims=True)
            acc_ref[1:2, :] += jnp.sum(dy * xhat2, axis=0, keepdims=True)
            acc_ref[2:3, :] += jnp.sum(dy, axis=0, keepdims=True)
            for c0, c1 in _col_halves(f):
                da = _dot_nt(dzb, wd_ref[c0:c1, :])
                dg_ref[lo:hi, c0:c1] = (da * fg_ref[lo:hi, c0:c1].astype(F32)).astype(BF16)
                du_ref[lo:hi, c0:c1] = (da * fu_ref[lo:hi, c0:c1].astype(F32)).astype(BF16)

    vec = _full_spec(g1.shape)
    wide, narrow = _row_spec(tm, f), _row_spec(tm, d)
    return pl.pallas_call(
        body, name="ffn_down_ln2_loss", grid=(t // tm,),
        in_specs=[wide, wide, wide, narrow, _row_spec(tm, pdim), narrow, narrow,
                  _full_spec(w_down.shape), _full_spec(w_pg.shape), _full_spec(wt_pe.shape), vec, vec, vec, vec],
        out_specs=[narrow] * 4 + [wide, wide, _acc_spec((8, d))],
        out_shape=[SDS((t, d), F32), SDS((t, d), BF16), SDS((t, d), BF16), SDS((t, d), BF16),
                   SDS((t, f), BF16), SDS((t, f), BF16), SDS((8, d), F32)],
        compiler_params=_params(("arbitrary",)),
    )(act, dact_dg, dact_du, h1b, p2, xhat1, target, w_down, w_pg, wt_pe, g1, b1, g2, b2)


def _after(after, body):
    k = len(after)
    return (lambda *refs: body(*refs[k:])), [ANY_SPEC] * k


def _resident_spec(shape):
    return pl.BlockSpec(shape, lambda i: (0,) * len(shape), pipeline_mode=pl.Buffered(1))


def _dh1_ln1_bwd(dz2, dg, dup, dsb, xhat1, rstd1, wt_gate, wt_up, w_pg, w_out, g1, after=()):
    t, d = dz2.shape
    f = dg.shape[1]
    tm = _row_tile(t, EPILOGUE_ROWS)

    def body(dz_ref, dg_ref, du_ref, ds_ref, xh1_ref, rs1_ref, wg_ref, wu_ref, wpg_ref, wo_ref, g1_ref,
             dz1_ref, dz1b_ref, dr_ref, da_ref, acc_ref):
        @pl.when(pl.program_id(0) == 0)
        def _():
            acc_ref[...] = jnp.zeros_like(acc_ref)

        for lo, hi in _sub_rows(tm):
            dh = (ALPHA * dz_ref[lo:hi, :] + _dot_nn(dg_ref[lo:hi, :], wg_ref[...])
                  + _dot_nn(du_ref[lo:hi, :], wu_ref[...]) + _dot_nt(ds_ref[lo:hi, :], wpg_ref[...]))
            xhat, rstd = xh1_ref[lo:hi, :], rs1_ref[lo:hi, 0:1]
            dz1 = _layer_norm_bwd(dh * g1_ref[...], xhat, rstd)
            dz1b = dz1.astype(BF16)
            dz1_ref[lo:hi, :] = dz1
            dz1b_ref[lo:hi, :] = dz1b
            acc_ref[0:1, :] += jnp.sum(dh * xhat, axis=0, keepdims=True)
            acc_ref[1:2, :] += jnp.sum(dh, axis=0, keepdims=True)
            dr_ref[lo:hi, :] = _dot_nt(dz1b, wo_ref[0:RET_W, :]).astype(BF16)
            da_ref[lo:hi, :] = _dot_nt(dz1b, wo_ref[RET_W:RET_W + ATT_W, :]).astype(BF16)

    body, lead = _after(after, body)
    return pl.pallas_call(
        body, name="dh1_ln1_bwd", grid=(t // tm,),
        in_specs=lead + [_row_spec(tm, d), _row_spec(tm, f), _row_spec(tm, f), _row_spec(tm, d), _row_spec(tm, d),
                         _row_spec(tm, LANES), _resident_spec(wt_gate.shape), _resident_spec(wt_up.shape), _resident_spec(w_pg.shape),
                         _resident_spec(w_out.shape), _full_spec(g1.shape)],
        out_specs=[_row_spec(tm, d), _row_spec(tm, d), _row_spec(tm, RET_W), _row_spec(tm, ATT_W), _acc_spec((8, d))],
        out_shape=[SDS((t, d), F32), SDS((t, d), BF16), SDS((t, RET_W), BF16), SDS((t, ATT_W), BF16),
                   SDS((8, d), F32)],
        compiler_params=_params(("arbitrary",)),
    )(*after, dz2, dg, dup, dsb, xhat1, rstd1, wt_gate, wt_up, w_pg, w_out, g1)


def _in_proj_bwd(dz1, parts, wt_in, after=()):
    t, d = dz1.shape
    tm = _row_tile(t, MATMUL_ROWS)
    widths = [p.shape[1] for p in parts]

    def body(*refs):
        dz_ref, part_refs, w_ref, dx_ref = refs[0], refs[1:1 + len(parts)], refs[-2], refs[-1]
        acc = ALPHA * dz_ref[...]
        lo = 0
        for p_ref, w in zip(part_refs, widths):
            acc = acc + _dot_nn(p_ref[...], w_ref[lo:lo + w, :])
            lo += w
        dx_ref[...] = acc

    body, lead = _after(after, body)
    return pl.pallas_call(
        body, name="in_proj_bwd", grid=(t // tm,),
        in_specs=lead + [_row_spec(tm, d)] + [_row_spec(tm, w) for w in widths] + [_full_spec(wt_in.shape)],
        out_specs=_row_spec(tm, d), out_shape=SDS((t, d), F32),
        compiler_params=_params(("parallel",)),
    )(*after, dz1, *parts, wt_in)


def _weight_grad(name, me, parts, rhs, after=()):
    t, n = rhs.shape
    widths = [p.shape[1] for p in parts]
    rows = sum(widths)
    own_rows = rows // N_DEV
    tk = _row_tile(t, MATMUL_ROWS)
    n_steps = t // tk
    step = 256

    def body(*refs):
        me_ref, part_refs, rhs_ref = refs[0], refs[1:1 + len(parts)], refs[1 + len(parts)]
        full_ref, own_ref, acc = refs[-3], refs[-2], refs[-1]
        i = pl.program_id(0)

        def products(first):
            b = rhs_ref[...].astype(BF16)
            lo = 0
            for p_ref, w in zip(part_refs, widths):
                for c0 in range(0, w, step):
                    c1 = min(c0 + step, w)
                    val = _dot_tn(p_ref[:, c0:c1].astype(BF16), b)
                    if first:
                        acc[lo + c0:lo + c1, :] = val
                    else:
                        acc[lo + c0:lo + c1, :] += val
                lo += w

        pl.when(i == 0)(functools.partial(products, True))
        pl.when(i > 0)(functools.partial(products, False))

        @pl.when(i == n_steps - 1)
        def _():
            full_ref[...] = acc[...].astype(BF16)
            own_ref[...] = acc[pl.ds(pl.multiple_of(me_ref[0] * own_rows, 8), own_rows), :]

    body, lead = _after(after, body)
    return pl.pallas_call(
        body, name=name, grid=(n_steps,),
        in_specs=lead + [_smem_spec()] + [_row_spec(tk, w) for w in widths] + [_row_spec(tk, n)],
        out_specs=[_full_spec((rows, n)), _full_spec((own_rows, n))],
        out_shape=[SDS((rows, n), BF16), SDS((own_rows, n), F32)],
        scratch_shapes=[pltpu.VMEM((rows, n), F32)],
        compiler_params=_params(("arbitrary",)),
    )(*after, me, *parts, rhs)


def _weight_grad_jobs(name, me, jobs, after=()):
    count = len(jobs)
    t = jobs[0][0].shape[0]
    tk = _row_tile(t, MATMUL_ROWS)
    n_steps = t // tk
    shapes = [(lhs.shape[1], rhs.shape[1]) for lhs, rhs in jobs]
    most_rows, most_cols = max(r for r, _ in shapes), max(n for _, n in shapes)
    step = 256

    def body(*refs):
        me_ref, lhs_refs, rhs_refs = refs[0], refs[1:1 + count], refs[1 + count:1 + 2 * count]
        full_refs, own_refs = refs[1 + 2 * count:1 + 3 * count], refs[1 + 3 * count:1 + 4 * count]
        acc, whole, mine, sems = refs[1 + 4 * count:]
        job, i = pl.program_id(0), pl.program_id(1)

        def leaving(j):
            rows, n = shapes[j]
            return (pltpu.make_async_copy(whole.at[0:rows, 0:n], full_refs[j], sems.at[0]),
                    pltpu.make_async_copy(mine.at[0:rows // N_DEV, 0:n], own_refs[j], sems.at[1]))

        def products(j, first):
            rows, n = shapes[j]
            b = rhs_refs[j][...].astype(BF16)
            for c0 in range(0, rows, step):
                c1 = min(c0 + step, rows)
                val = _dot_tn(lhs_refs[j][:, c0:c1].astype(BF16), b)
                if first:
                    acc[c0:c1, 0:n] = val
                else:
                    acc[c0:c1, 0:n] += val

        def finish(j):
            rows, n = shapes[j]
            own_rows = rows // N_DEV
            if j > 0:
                for cp in leaving(j - 1):
                    cp.wait()
            whole[0:rows, 0:n] = acc[0:rows, 0:n].astype(BF16)
            mine[0:own_rows, 0:n] = acc[pl.ds(pl.multiple_of(me_ref[0] * own_rows, 8), own_rows), 0:n]
            for cp in leaving(j):
                cp.start()
            if j == count - 1:
                for cp in leaving(j):
                    cp.wait()

        for j in range(count):
            pl.when((job == j) & (i == 0))(functools.partial(products, j, True))
            pl.when((job == j) & (i > 0))(functools.partial(products, j, False))
            pl.when((job == j) & (i == n_steps - 1))(functools.partial(finish, j))

    def turn(j):
        return lambda job, i: (jnp.where(job == j, i, jnp.where(job < j, 0, n_steps - 1)), 0)

    body, lead = _after(after, body)
    res = pl.pallas_call(
        body, name=name, grid=(count, n_steps),
        in_specs=lead + [_smem_spec()] + [pl.BlockSpec((tk, rows), turn(j)) for j, (rows, _) in enumerate(shapes)]
        + [pl.BlockSpec((tk, n), turn(j)) for j, (_, n) in enumerate(shapes)],
        out_specs=[ANY_SPEC] * (2 * count),
        out_shape=[SDS((rows, n), BF16) for rows, n in shapes] + [SDS((rows // N_DEV, n), F32) for rows, n in shapes],
        scratch_shapes=[pltpu.VMEM((most_rows, most_cols), F32), pltpu.VMEM((most_rows, most_cols), BF16),
                        pltpu.VMEM((most_rows // N_DEV, most_cols), F32), pltpu.SemaphoreType.DMA((2,))],
        compiler_params=_params(("arbitrary", "arbitrary")),
    )(*after, me, *[lhs for lhs, _ in jobs], *[rhs for _, rhs in jobs])
    return list(res[:count]), list(res[count:])


def _log_decay(decay_f, decay_b):
    def body(f_ref, b_ref, lf_ref, lb_ref):
        lf_ref[...] = jnp.log1p(-jnp.exp2(f_ref[...]))
        lb_ref[...] = jnp.log1p(-jnp.exp2(b_ref[...]))

    return pl.pallas_call(body, name="log_decay", out_shape=[SDS(decay_f.shape, F32)] * 2)(decay_f, decay_b)


def _chunk(ref, n):
    return ref[pl.ds(pl.multiple_of(n * CHUNK, CHUNK), CHUNK), :]


def _group_sum(is_a, v):
    sa = jnp.sum(jnp.where(is_a, v, 0.0), axis=1, keepdims=True)
    sb = jnp.sum(jnp.where(is_a, 0.0, v), axis=1, keepdims=True)
    return jnp.where(is_a, sa, sb)


def _seq_spec(s, col_block):
    return pl.BlockSpec((s, LANES), lambda b, h: (b, col_block + h))


def _smem_spec():
    return pl.BlockSpec(memory_space=pltpu.SMEM)


RET_UNROLL = 4


def _chunk_loop(n_chunks, body, init):
    u = RET_UNROLL if n_chunks % RET_UNROLL == 0 else 1

    def trip(i, carry):
        for j in range(u):
            carry = body(i * u + j, carry)
        return carry

    return lax.fori_loop(0, n_chunks // u, trip, init)


def _stacked_tables(lgf_ref, lgb_ref, pair):
    lane = lax.broadcasted_iota(jnp.int32, (1, LANES), 1)
    is_a = lane < HEAD_DIM
    lgf = jnp.where(is_a, lgf_ref[2 * pair], lgf_ref[2 * pair + 1])
    lgb = jnp.where(is_a, lgb_ref[2 * pair], lgb_ref[2 * pair + 1])
    row = lax.broadcasted_iota(jnp.int32, (CHUNK, 1), 0).astype(F32)
    kdec_f, qdec_f = jnp.exp(lgf * (CHUNK - 1.0 - row)), jnp.exp(lgf * (row + 1.0))
    kdec_b, qdec_b = jnp.exp(lgb * row), jnp.exp(lgb * (CHUNK - row))
    tab = dict(
        is_a=is_a, row=row, lam_f=jnp.exp(lgf * CHUNK), lam_b=jnp.exp(lgb * CHUNK),
        kdec=jnp.concatenate([kdec_f, kdec_b], axis=1), qdec=jnp.concatenate([qdec_f, qdec_b], axis=1),
        qexp=jnp.concatenate([jnp.broadcast_to(row + 1.0, (CHUNK, LANES)),
                              jnp.broadcast_to(CHUNK - row, (CHUNK, LANES))], axis=1),
        kexp=jnp.concatenate([jnp.broadcast_to(CHUNK - 1.0 - row, (CHUNK, LANES)),
                              jnp.broadcast_to(row, (CHUNK, LANES))], axis=1),
    )
    r = lax.broadcasted_iota(jnp.int32, (2 * LANES, LANES), 0)
    c = lax.broadcasted_iota(jnp.int32, (2 * LANES, LANES), 1)
    tab["diag2"] = ((r & (LANES - 1)) < HEAD_DIM) == (c < HEAD_DIM)
    i2 = lax.broadcasted_iota(jnp.int32, (2 * CHUNK, CHUNK), 0)
    j = lax.broadcasted_iota(jnp.int32, (2 * CHUNK, CHUNK), 1)
    head_b = i2 >= CHUNK
    diff = ((i2 & (CHUNK - 1)) - j).astype(F32)
    up, dn = jnp.maximum(diff, 0.0), jnp.maximum(-diff, 0.0)
    lgf2 = jnp.where(head_b, lgf_ref[2 * pair + 1], lgf_ref[2 * pair])
    lgb2 = jnp.where(head_b, lgb_ref[2 * pair + 1], lgb_ref[2 * pair])
    ef = jnp.where(diff >= 0, jnp.exp(lgf2 * up), 0.0)
    eb = jnp.where(diff <= 0, jnp.exp(lgb2 * dn), 0.0)
    tab["d2"] = ef + eb
    tab["df2"] = ef * up
    tab["db2"] = eb * dn
    return tab


def _stack_pair(is_a, x):
    zero = jnp.zeros_like(x)
    return jnp.concatenate([jnp.where(is_a, x, zero), jnp.where(is_a, zero, x)], axis=0)


def _unstack_pair(is_a, x2):
    return jnp.where(is_a, x2[0:CHUNK, :], x2[CHUNK:2 * CHUNK, :])


def _both_ways(x, dec):
    return (jnp.concatenate([x, x], axis=1) * dec).astype(BF16)


def _scan_states(n_chunks, st, up_rows, up_lam, down_rows, down_lam):
    zero = jnp.zeros((LANES, LANES), F32)

    def up(n, r):
        new = st[n, up_rows, :]
        st[n, up_rows, :] = r
        return r * up_lam + new

    def down(s, r):
        n = n_chunks - 1 - s
        new = st[n, down_rows, :]
        st[n, down_rows, :] = r
        return r * down_lam + new

    lax.fori_loop(0, n_chunks, up, zero)
    lax.fori_loop(0, n_chunks, down, zero)


FWD_ROWS, BWD_ROWS = pl.ds(0, LANES), pl.ds(LANES, LANES)


def _state_spec(n_chunks, pairs):
    return pl.BlockSpec((n_chunks, 2 * LANES, LANES), lambda b, h: (b * pairs + h, 0, 0))


ST_GAIN, ST_XF, ST_XB, ST_IFA, ST_IFB, ST_IBA, ST_IBB, ST_LF, ST_LB = 0, 1, 2, 3, 4, 5, 6, 8, 9
ST_ROWS = 16


GW = GROUP * HEAD_DIM
KEYS = 3 * BLOCK


def _attn_tables(g, bias_ref):
    r = lax.broadcasted_iota(jnp.int32, (GROUP * BLOCK, KEYS), 0)
    kj = lax.broadcasted_iota(jnp.int32, (GROUP * BLOCK, KEYS), 1)
    qi = r & (BLOCK - 1)
    hh = lax.shift_right_logical(r, 7)
    dist = jnp.abs(kj - BLOCK - qi)
    slope = jnp.exp2(-(GROUP * g + hh + 1).astype(F32) * (8.0 / ATTN_HEADS))
    inside = jnp.where(dist <= BLOCK, -slope * dist.astype(F32), NEG_INF)
    bias_ref[BIAS_INSIDE] = inside
    bias_ref[BIAS_FIRST] = jnp.where(kj >= BLOCK, inside, NEG_INF)
    bias_ref[BIAS_LAST] = jnp.where(kj < 2 * BLOCK, inside, NEG_INF)


BIAS_INSIDE, BIAS_FIRST, BIAS_LAST = 0, 1, 2


def _own_lanes(g):
    return lax.shift_right_logical(lax.broadcasted_iota(jnp.int32, (1, LANES), 1), 6) == g


def _mask_keys(x_ref, g, scale, pad_ref, s):
    pad_ref[0:BLOCK, :] = jnp.zeros((BLOCK, LANES), BF16)
    pad_ref[BLOCK + s:2 * BLOCK + s, :] = jnp.zeros((BLOCK, LANES), BF16)
    pad_ref[BLOCK:BLOCK + s, :] = jnp.where(_own_lanes(g), x_ref[...].astype(F32) * scale, 0.0).astype(BF16)


def _lane_block(x, j):
    return x[:, j * LANES:(j + 1) * LANES]


def _stack_heads(x, g):
    assert GROUP == 4 and GW == 2 * LANES
    x1 = pltpu.roll(x, HEAD_DIM, 1)
    keep = _own_lanes(g)
    zero = jnp.zeros((BLOCK, LANES), x.dtype)
    rows = []
    for h in range(GROUP):
        for_g0 = _lane_block(x, h // 2) if h % 2 == 0 else _lane_block(x1, ((h + 1) // 2) % 2)
        for_g1 = _lane_block(x, h // 2) if h % 2 == 1 else _lane_block(x1, h // 2)
        rows.append(jnp.where(keep, jnp.where(g == 0, for_g0, for_g1), zero))
    return jnp.concatenate(rows, axis=0)


def _unstack_heads(x4, g):
    p = [x4[h * BLOCK:(h + 1) * BLOCK, :] for h in range(GROUP)]
    cat = lambda a, b: jnp.concatenate([a, b], axis=1)
    in_place = jnp.where(g == 0, cat(p[0], p[2]), cat(p[1], p[3]))
    one_left = jnp.where(g == 0, cat(p[1], p[3]), cat(p[2], p[0]))
    return in_place + pltpu.roll(one_left, HEAD_DIM, 1)


def _sink_column(sink_ref, g):
    rh = lax.shift_right_logical(lax.broadcasted_iota(jnp.int32, (GROUP * BLOCK, 1), 0), 7)
    col = jnp.zeros((GROUP * BLOCK, 1), F32)
    for h in range(GROUP):
        col = jnp.where(rh == h, sink_ref[GROUP * g + h], col)
    return col


def _attn_probs(qm, k3, bias_ref, sink_col, n, s):
    which = jnp.where(n == 0, BIAS_FIRST, jnp.where(n == s // BLOCK - 1, BIAS_LAST, BIAS_INSIDE))
    logits = _dot_nt(qm, k3) + bias_ref[which]
    m = jnp.maximum(jnp.max(logits, axis=1, keepdims=True), sink_col)
    e = jnp.exp(logits - m)
    e_sink = jnp.exp(sink_col - m)
    inv = 1.0 / (jnp.sum(e, axis=1, keepdims=True) + e_sink)
    return e * inv, e_sink * inv


PAIRS_PER_KV = (RET_HEADS // 2) // KV_HEADS
FWD_ORDER = "rrarra"
BWD_ORDER = "rararr"


def _trip_order(order, chunks, blocks):
    if order.count("r") == chunks and order.count("a") == blocks:
        return order
    return "r" * chunks + "a" * blocks


def _mixers_fwd(u, lgf, lgb, gn_gain, sink, b_loc, after=()):
    t = u.shape[0]
    s = t // b_loc
    n_chunks = s // CHUNK
    pairs = RET_HEADS // 2
    trips = n_chunks // RET_UNROLL
    blocks_half = (s // BLOCK) // PAIRS_PER_KV
    per_trip = blocks_half // trips
    assert n_chunks % RET_UNROLL == 0 and blocks_half % trips == 0 and PAIRS_PER_KV == 2 and s >= 2 * BLOCK

    def body(lgf_ref, lgb_ref, sink_ref, q_ref, k_ref, v_ref, g_ref, gain_ref, aq_ref, ak_ref, av_ref,
             r_ref, xhat_ref, rstd_ref, a_ref, st, kpad, vpad, bias):
        pair = pl.program_id(1)
        g, half = lax.shift_right_logical(pair, 1), pair & 1
        tab = _stacked_tables(lgf_ref, lgb_ref, pair)
        is_a = tab["is_a"]

        @pl.when(half == 0)
        def _():
            _attn_tables(g, bias)
            _mask_keys(ak_ref, g, Q_SCALE, kpad, s)
            _mask_keys(av_ref, g, 1.0, vpad, s)

        def kv_body(n, _):
            k8 = _chunk(k_ref, n).astype(F32) * Q_SCALE
            st[n] = jnp.where(tab["diag2"], _dot_tn(_both_ways(k8, tab["kdec"]), _chunk(v_ref, n)), 0.0)
            return 0

        _chunk_loop(n_chunks, kv_body, 0)
        _scan_states(n_chunks, st, FWD_ROWS, tab["lam_f"], BWD_ROWS, tab["lam_b"])
        sink_col = _sink_column(sink_ref, g)

        def retention_chunk(n):
            q = _chunk(q_ref, n)
            k8 = (_chunk(k_ref, n).astype(F32) * Q_SCALE).astype(BF16)
            v = _chunk(v_ref, n)
            p2 = (_dot_nt(_stack_pair(is_a, q), k8) * tab["d2"]).astype(BF16)
            y = _unstack_pair(is_a, _dot_nn(p2, v))
            y = y + _dot_nn(_both_ways(q.astype(F32), tab["qdec"]), st[n].astype(BF16))
            rows = pl.ds(pl.multiple_of(n * CHUNK, CHUNK), CHUNK)
            mu = _group_sum(is_a, y) * (1.0 / HEAD_DIM)
            dlt = y - mu
            var = _group_sum(is_a, dlt * dlt) * (1.0 / HEAD_DIM)
            rstd = lax.rsqrt(var + GN_EPS)
            xhat = dlt * rstd
            xhat_ref[rows, :] = xhat
            rstd_ref[rows, :] = rstd
            gate = _chunk(g_ref, n).astype(F32)
            r_ref[rows, :] = (xhat * gain_ref[...] * gate * _sigmoid(gate)).astype(BF16)

        def attention_block(blk):
            n = half * blocks_half + blk
            rows = pl.ds(pl.multiple_of(blk * BLOCK, BLOCK), BLOCK)
            keys = pl.ds(pl.multiple_of(n * BLOCK, BLOCK), KEYS)
            p, _ = _attn_probs(_stack_heads(aq_ref[rows, :], g), kpad[keys, :], bias, sink_col, n, s)
            a_ref[rows, :] = _unstack_heads(_dot_nn(p.astype(BF16), vpad[keys, :]), g).astype(BF16)

        def trip(i, _):
            chunk, blk = 0, 0
            for kind in _trip_order(FWD_ORDER, RET_UNROLL, per_trip):
                if kind == "r":
                    retention_chunk(i * RET_UNROLL + chunk)
                    chunk += 1
                else:
                    attention_block(i * per_trip + blk)
                    blk += 1
            return 0

        lax.fori_loop(0, trips, trip, 0)

    lane_blk = lambda c0: _seq_spec(s, c0 // LANES)
    half_rows = blocks_half * BLOCK
    aq_spec = pl.BlockSpec((half_rows, GW), lambda b, h: (b * PAIRS_PER_KV + (h & 1), C_AQ // GW + h // 2))
    a_spec = pl.BlockSpec((half_rows, GW), lambda b, h: (b * PAIRS_PER_KV + (h & 1), h // 2))
    kv_spec = lambda c0: pl.BlockSpec((s, LANES), lambda b, h: (b, c0 // LANES))
    pad = pltpu.VMEM((s + 2 * BLOCK, LANES), BF16)
    body, lead = _after(after, body)
    return pl.pallas_call(
        body, name="mixers_fwd", grid=(b_loc, pairs),
        in_specs=lead + [_smem_spec(), _smem_spec(), _smem_spec(), lane_blk(C_RQ), lane_blk(C_RK), lane_blk(C_RV),
                         lane_blk(C_RG), pl.BlockSpec((1, LANES), lambda b, h: (0, h)), aq_spec, kv_spec(C_AK),
                         kv_spec(C_AV)],
        out_specs=[_seq_spec(s, 0), _seq_spec(s, 0), _seq_spec(s, 0), a_spec, _state_spec(n_chunks, pairs)],
        out_shape=[SDS((t, RET_W), BF16), SDS((t, RET_W), F32), SDS((t, RET_W), F32), SDS((t, ATT_W), BF16),
                   SDS((b_loc * pairs * n_chunks, 2 * LANES, LANES), F32)],
        scratch_shapes=[pad, pad, pltpu.VMEM((3, GROUP * BLOCK, KEYS), F32)],
        compiler_params=_params(("arbitrary", "arbitrary")),
    )(*after, lgf, lgb, sink, u, u, u, u, gn_gain, u, u, u)


def _mixers_bwd(u, xhat, rstd, states, dr, da, lgf, lgb, gn_gain, sink, b_loc, after=()):
    t = u.shape[0]
    s = t // b_loc
    n_chunks = s // CHUNK
    pairs = RET_HEADS // 2
    trips = n_chunks // RET_UNROLL
    blocks_half = (s // BLOCK) // PAIRS_PER_KV
    per_trip = blocks_half // trips
    assert n_chunks % RET_UNROLL == 0 and blocks_half % trips == 0 and PAIRS_PER_KV == 2 and s >= 2 * BLOCK

    def body(lgf_ref, lgb_ref, sink_ref, q_ref, k_ref, v_ref, g_ref, xhat_ref, rstd_ref, dr_ref, gain_ref,
             aq_ref, ak_ref, av_ref, do_ref, st,
             dq_ref, dk_ref, dv_ref, dg_ref, st_ref, daq_ref, dak_ref, dav_ref, dsink_ref,
             gr, dy_s, kpad, vpad, bias, dk_acc, dv_acc):
        pair = pl.program_id(1)
        g, half = lax.shift_right_logical(pair, 1), pair & 1
        tab = _stacked_tables(lgf_ref, lgb_ref, pair)
        is_a = tab["is_a"]
        gain = gain_ref[...]

        @pl.when(half == 0)
        def _():
            _attn_tables(g, bias)
            _mask_keys(ak_ref, g, Q_SCALE, kpad, s)
            _mask_keys(av_ref, g, 1.0, vpad, s)
            dsink_ref[...] = jnp.zeros_like(dsink_ref)

        @pl.when(pair == 0)
        def _():
            dk_acc[...] = jnp.zeros_like(dk_acc)
            dv_acc[...] = jnp.zeros_like(dv_acc)

        def norm_body(n, dgain):
            rows = pl.ds(pl.multiple_of(n * CHUNK, CHUNK), CHUNK)
            xhat, rstd = xhat_ref[rows, :], rstd_ref[rows, :]
            gate = g_ref[rows, :].astype(F32)
            sg = _sigmoid(gate)
            silu = gate * sg
            d_out = dr_ref[rows, :].astype(F32)
            dg_ref[rows, :] = (d_out * xhat * gain * (sg * (1.0 + gate * (1.0 - sg)))).astype(BF16)
            dxh = d_out * gain * silu
            m1 = _group_sum(is_a, dxh) * (1.0 / HEAD_DIM)
            m2 = _group_sum(is_a, dxh * xhat) * (1.0 / HEAD_DIM)
            dy = (rstd * (dxh - m1 - xhat * m2)).astype(BF16)
            dy_s[rows, :] = dy
            qf = q_ref[rows, :].astype(F32)
            gr[n] = jnp.where(tab["diag2"], _dot_tn(_both_ways(qf, tab["qdec"]), dy), 0.0)
            return dgain + jnp.sum(d_out * xhat * silu, axis=0, keepdims=True)

        colsum = lambda x: jnp.sum(x, axis=0, keepdims=True)

        def grad_body(n, carry):
            xfb, ifa, ifb, iba, ibb, lf, lb = carry
            rows = pl.ds(pl.multiple_of(n * CHUNK, CHUNK), CHUNK)
            q = q_ref[rows, :]
            qf = q.astype(F32)
            k8f = k_ref[rows, :].astype(F32) * Q_SCALE
            k8 = k8f.astype(BF16)
            v = v_ref[rows, :]
            dy = dy_s[rows, :]
            q2, dy2 = _stack_pair(is_a, q), _stack_pair(is_a, dy)
            sc = _dot_nt(q2, k8)
            dp = _dot_nt(dy2, v)
            a2 = (sc * tab["d2"]).astype(BF16)
            ds2 = (dp * tab["d2"]).astype(BF16)
            dq = _unstack_pair(is_a, _dot_nn(ds2, k8))
            dk = _dot_tn(ds2, q2)
            dv = _dot_tn(a2, dy2)
            prod = sc * dp
            pf, pb = prod * tab["df2"], prod * tab["db2"]
            ifa, ifb = ifa + colsum(pf[0:CHUNK, :]), ifb + colsum(pf[CHUNK:2 * CHUNK, :])
            iba, ibb = iba + colsum(pb[0:CHUNK, :]), ibb + colsum(pb[CHUNK:2 * CHUNK, :])
            states, sgrads = st[n], gr[n]
            sb, gb = states.astype(BF16), sgrads.astype(BF16)
            dqc = _dot_nt(dy, sb) * tab["qdec"]
            dkc = _dot_nt(v, gb) * tab["kdec"]
            dv = dv + _dot_nn(_both_ways(k8f, tab["kdec"]), gb)
            dq_ref[rows, :] = (dq + dqc[:, 0:LANES] + dqc[:, LANES:2 * LANES]).astype(BF16)
            dk_ref[rows, :] = ((dk + dkc[:, 0:LANES] + dkc[:, LANES:2 * LANES]) * Q_SCALE).astype(BF16)
            dv_ref[rows, :] = dv.astype(BF16)
            q2w, k2w = jnp.concatenate([qf, qf], axis=1), jnp.concatenate([k8f, k8f], axis=1)
            xfb = xfb + colsum(tab["qexp"] * q2w * dqc + tab["kexp"] * k2w * dkc)
            prod_s = sgrads * states
            lf, lb = lf + colsum(prod_s[0:LANES, :]), lb + colsum(prod_s[LANES:2 * LANES, :])
            return xfb, ifa, ifb, iba, ibb, lf, lb

        sink_col = _sink_column(sink_ref, g)
        head_row = lax.broadcasted_iota(jnp.int32, dsink_ref.shape, 0)

        def attention_block(blk):
            n = half * blocks_half + blk
            rows = pl.ds(pl.multiple_of(blk * BLOCK, BLOCK), BLOCK)
            keys = pl.ds(pl.multiple_of(n * BLOCK, BLOCK), KEYS)
            qm = _stack_heads(aq_ref[rows, :], g)
            k3, v3 = kpad[keys, :], vpad[keys, :]
            p, p_sink = _attn_probs(qm, k3, bias, sink_col, n, s)
            dom = _stack_heads(do_ref[rows, :], g)
            dp = _dot_nt(dom, v3)
            delta = jnp.sum(p * dp, axis=1, keepdims=True)
            ds_mat = (p * (dp - delta)).astype(BF16)
            daq_ref[rows, :] = _unstack_heads(_dot_nn(ds_mat, k3), g).astype(BF16)
            dk_acc[keys, :] += _dot_tn(ds_mat, qm) * Q_SCALE
            dv_acc[keys, :] += _dot_tn(p.astype(BF16), dom)
            w = p_sink * delta
            upd = jnp.zeros(dsink_ref.shape, F32)
            for h in range(GROUP):
                upd = upd + jnp.where(head_row == h, -jnp.sum(w[h * BLOCK:(h + 1) * BLOCK, :]), 0.0)
            dsink_ref[...] += upd

        dgain = _chunk_loop(n_chunks, norm_body, jnp.zeros((1, LANES), F32))
        _scan_states(n_chunks, gr, BWD_ROWS, tab["lam_b"], FWD_ROWS, tab["lam_f"])

        def trip(i, carry):
            chunk, blk = 0, 0
            for kind in _trip_order(BWD_ORDER, RET_UNROLL, per_trip):
                if kind == "r":
                    carry = grad_body(i * RET_UNROLL + chunk, carry)
                    chunk += 1
                else:
                    attention_block(i * per_trip + blk)
                    blk += 1
            return carry

        z = jnp.zeros((1, LANES), F32)
        init = (jnp.zeros((1, 2 * LANES), F32), z, z, z, z, z, z)
        xfb, ifa, ifb, iba, ibb, lf, lb = lax.fori_loop(0, trips, trip, init)
        st_ref[...] = jnp.zeros_like(st_ref)
        st_ref[ST_GAIN:ST_GAIN + 1, :] = dgain
        st_ref[ST_XF:ST_XF + 1, :] = xfb[:, 0:LANES]
        st_ref[ST_XB:ST_XB + 1, :] = xfb[:, LANES:2 * LANES]
        st_ref[ST_IFA:ST_IFA + 1, :] = ifa
        st_ref[ST_IFB:ST_IFB + 1, :] = ifb
        st_ref[ST_IBA:ST_IBA + 1, :] = iba
        st_ref[ST_IBB:ST_IBB + 1, :] = ibb
        st_ref[ST_LF:ST_LF + 1, :] = lf * (CHUNK * tab["lam_f"])
        st_ref[ST_LB:ST_LB + 1, :] = lb * (CHUNK * tab["lam_b"])

        @pl.when(pair == pairs - 1)
        def _():
            dak_ref[...] = dk_acc[BLOCK:BLOCK + s, :].astype(BF16)
            dav_ref[...] = dv_acc[BLOCK:BLOCK + s, :].astype(BF16)

    lane_blk = lambda c0: _seq_spec(s, c0 // LANES)
    seq0 = _seq_spec(s, 0)
    half_rows = blocks_half * BLOCK
    aq_spec = pl.BlockSpec((half_rows, GW), lambda b, h: (b * PAIRS_PER_KV + (h & 1), C_AQ // GW + h // 2))
    a_spec = pl.BlockSpec((half_rows, GW), lambda b, h: (b * PAIRS_PER_KV + (h & 1), h // 2))
    kv_spec = lambda c0: pl.BlockSpec((s, LANES), lambda b, h: (b, c0 // LANES))
    kv_out = pl.BlockSpec((s, LANES), lambda b, h: (b, 0))
    state = pltpu.VMEM((n_chunks, 2 * LANES, LANES), F32)
    pad = pltpu.VMEM((s + 2 * BLOCK, LANES), BF16)
    acc = pltpu.VMEM((s + 2 * BLOCK, LANES), F32)
    body, lead = _after(after, body)
    return pl.pallas_call(
        body, name="mixers_bwd", grid=(b_loc, pairs),
        in_specs=lead + [_smem_spec(), _smem_spec(), _smem_spec(), lane_blk(C_RQ), lane_blk(C_RK), lane_blk(C_RV),
                         lane_blk(C_RG), seq0, seq0, seq0, pl.BlockSpec((1, LANES), lambda b, h: (0, h)),
                         aq_spec, kv_spec(C_AK), kv_spec(C_AV), a_spec, _state_spec(n_chunks, pairs)],
        out_specs=[seq0] * 4 + [pl.BlockSpec((ST_ROWS, LANES), lambda b, h: (b, h)), a_spec, kv_out, kv_out,
                                pl.BlockSpec((8, LANES), lambda b, h: (b * KV_HEADS + h // 2, 0))],
        out_shape=[SDS((t, RET_W), BF16)] * 4 + [SDS((b_loc * ST_ROWS, RET_W), F32), SDS((t, ATT_W), BF16),
                                                   SDS((t, KV_W), BF16), SDS((t, KV_W), BF16),
                                                   SDS((b_loc * KV_HEADS * 8, LANES), F32)],
        scratch_shapes=[state, pltpu.VMEM((s, LANES), BF16), pad, pad,
                        pltpu.VMEM((3, GROUP * BLOCK, KEYS), F32), acc, acc],
        compiler_params=_params(("arbitrary", "arbitrary")),
    )(*after, lgf, lgb, sink, u, u, u, u, xhat, rstd, dr, gn_gain, u, u, u, da, states)


def _pack_small(acc2, acc1, ret_stats, dsink, b_loc, d):
    pairs = RET_HEADS // 2

    def body(acc2_ref, acc1_ref, st_ref, dsink_ref, out_ref):
        out_ref[...] = jnp.zeros_like(out_ref)
        out_ref[ROW_LN1G:ROW_LN1G + 1, :] = acc1_ref[0:1, :]
        out_ref[ROW_LN1B:ROW_LN1B + 1, :] = acc1_ref[1:2, :]
        out_ref[ROW_LN2G:ROW_LN2G + 1, :] = acc2_ref[1:2, :]
        out_ref[ROW_LN2B:ROW_LN2B + 1, :] = acc2_ref[2:3, :]
        out_ref[ROW_LOSS:ROW_LOSS + 1, :] = acc2_ref[0:1, :]
        st = st_ref[0:ST_ROWS, :]
        for b in range(1, b_loc):
            st = st + st_ref[b * ST_ROWS:(b + 1) * ST_ROWS, :]
        out_ref[ROW_GN:ROW_GN + 1, 0:RET_W] = st[ST_GAIN:ST_GAIN + 1, :]
        lane = lax.broadcasted_iota(jnp.int32, (1, d), 1)
        misc = jnp.zeros((1, d), F32)
        for pr in range(pairs):
            blk = st[:, pr * LANES:(pr + 1) * LANES]
            half = lax.broadcasted_iota(jnp.int32, (1, LANES), 1) < HEAD_DIM
            for h in range(2):
                sel = half if h == 0 else jnp.logical_not(half)
                cross_f = jnp.sum(jnp.where(sel, blk[ST_XF:ST_XF + 1, :] + blk[ST_LF:ST_LF + 1, :], 0.0))
                cross_b = jnp.sum(jnp.where(sel, blk[ST_XB:ST_XB + 1, :] + blk[ST_LB:ST_LB + 1, :], 0.0))
                intra_f = jnp.sum(blk[ST_IFA + h:ST_IFA + h + 1, :])
                intra_b = jnp.sum(blk[ST_IBA + h:ST_IBA + h + 1, :])
                head = 2 * pr + h
                misc = jnp.where(lane == MISC_DF + head, cross_f + intra_f, misc)
                misc = jnp.where(lane == MISC_DB + head, cross_b + intra_b, misc)
        for g in range(KV_HEADS):
            tot = dsink_ref[g * 8:(g + 1) * 8, :]
            for b in range(1, b_loc):
                tot = tot + dsink_ref[(b * KV_HEADS + g) * 8:(b * KV_HEADS + g + 1) * 8, :]
            for h in range(GROUP):
                misc = jnp.where(lane == MISC_SINK + GROUP * g + h, jnp.sum(tot[h:h + 1, 0:1]), misc)
        out_ref[ROW_MISC:ROW_MISC + 1, :] = misc

    return pl.pallas_call(body, name="pack_small", out_shape=SDS((SMALL_ROWS, d), F32))(acc2, acc1, ret_stats, dsink)


BIG = ("w_in", "w_out", "w_ffn_gate", "w_ffn_up", "w_ffn_down", "w_ple_proj", "w_ple_gate")
TRANSPOSED_OUTSIDE = ("w_in", "w_ffn_gate", "w_ffn_up")
TRANSPOSED_HERE = ("w_ple_proj",)
SMALL = ("ret_decay_fwd", "ret_decay_bwd", "ret_gn_gain", "attn_sink", "ln1_gain", "ln1_bias", "ln2_gain", "ln2_bias")
ORDER = ("w_in", "ret_decay_fwd", "ret_decay_bwd", "ret_gn_gain", "attn_sink", "w_out", "ln1_gain", "ln1_bias",
         "w_ffn_gate", "w_ffn_up", "w_ffn_down", "w_ple_proj", "w_ple_gate", "ln2_gain", "ln2_bias")


GATHER_ORDER = ("w_in", "w_ffn_up", "w_out", "w_ffn_gate", "w_ple_gate", "w_ple_proj", "w_ffn_down")
GATHER_TWO_LEVEL = ("w_in", "w_ffn_up")

def _local_step(x2, p2, target2, fetch, publish, small, b_loc, me):
    d = x2.shape[1]
    lgf, lgb = _log_decay(small["ret_decay_fwd"], small["ret_decay_bwd"])
    lgf1, lgb1, sink1 = lgf.reshape(-1), lgb.reshape(-1), small["attn_sink"].reshape(-1)
    (w_in,) = fetch(("w_in",), ())
    u, xb = _in_proj(x2, w_in)
    passed = fetch.pass_on("w_ffn_up", (xb,))
    r, ret_xhat, ret_rstd, a, ret_states = _mixers_fwd(u, lgf1, lgb1, small["ret_gn_gain"], sink1, b_loc, passed)
    w_out, w_gate, w_up = fetch(("w_out", "w_ffn_gate", "w_ffn_up"), (r, a))
    xhat1, rstd1, h1b, dact_dg, dact_du, act = _mix_ln1_ffn_up(
        r, a, x2, w_out, w_gate, w_up, small["ln1_gain"], small["ln1_bias"])
    w_pg, w_pe, w_down = fetch(("w_ple_gate", "w_ple_proj", "w_ffn_down"), (act,))
    dz2, dz2b, dsb, dpleb, dg, dup, acc2 = _ffn_down_ln2_loss(
        act, dact_dg, dact_du, h1b, p2, xhat1, target2, w_down, w_pg, w_pe,
        small["ln1_gain"], small["ln1_bias"], small["ln2_gain"], small["ln2_bias"])
    own = {}

    def grad(name, parts, rhs, after=()):
        whole, own[name] = _weight_grad("grad_" + name, me, parts, rhs, after)
        return whole

    ffn_jobs = dict(w_ffn_down=(act, dz2b), w_ple_proj=(dpleb, p2), w_ple_gate=(h1b, dsb),
                    w_ffn_gate=(dg, h1b), w_ffn_up=(dup, h1b))
    wholes, owns = _weight_grad_jobs("grad_w_ffn", me, list(ffn_jobs.values()))
    own.update(zip(ffn_jobs, owns))
    t2 = publish("ffn", dict(zip(ffn_jobs, wholes)))
    dz1, dz1b, dr, da, acc1 = _dh1_ln1_bwd(
        dz2, dg, dup, dsb, xhat1, rstd1, w_gate, w_up, w_pg, w_out, small["ln1_gain"], t2)
    t3 = publish("out", dict(w_out=grad("w_out", [r, a], dz1b)))
    dq, dk, dv, dgate, ret_stats, daq, dak, dav, dsink = _mixers_bwd(
        u, ret_xhat, ret_rstd, ret_states, dr, da, lgf1, lgb1, small["ret_gn_gain"], sink1, b_loc, t3)
    parts = [dq, dk, dv, dgate, daq, dak, dav]
    small_part = _pack_small(acc2, acc1, ret_stats, dsink, b_loc, d)
    t4 = publish("in", dict(w_in=grad("w_in", parts, xb)), small_part)
    grad_x = _in_proj_bwd(dz1, parts, w_in, t4)
    return grad_x, own, small_part


def kernel(x, p, w_in, ret_decay_fwd, ret_decay_bwd, ret_gn_gain, attn_sink, w_out, ln1_gain, ln1_bias, w_ffn_gate, w_ffn_up, w_ffn_down, w_ple_proj, w_ple_gate, ln2_gain, ln2_bias, loss_target, m_w_in, m_ret_decay_fwd, m_ret_decay_bwd, m_ret_gn_gain, m_attn_sink, m_w_out, m_ln1_gain, m_ln1_bias, m_w_ffn_gate, m_w_ffn_up, m_w_ffn_down, m_w_ple_proj, m_w_ple_gate, m_ln2_gain, m_ln2_bias, v_w_in, v_ret_decay_fwd, v_ret_decay_bwd, v_ret_gn_gain, v_attn_sink, v_w_out, v_ln1_gain, v_ln1_bias, v_w_ffn_gate, v_w_ffn_up, v_w_ffn_down, v_w_ple_proj, v_w_ple_gate, v_ln2_gain, v_ln2_bias):
    given = dict(locals())

    def strip(n, a):
        if n not in BIG:
            return a
        return a[0].T if n in TRANSPOSED_OUTSIDE else a[0]

    def restore(n, a):
        if n not in BIG:
            return a
        return (a.T if n in TRANSPOSED_OUTSIDE else a)[None]

    w = {n: strip(n, given[n]) for n in ORDER}
    m = {n: strip(n, given["m_" + n]) for n in ORDER}
    v = {n: strip(n, given["v_" + n]) for n in ORDER}
    b_loc, s, d = x.shape
    x2 = x.reshape(b_loc * s, d)
    p2 = p[0].reshape(b_loc * s, p.shape[-1])
    target2 = loss_target.reshape(b_loc * s, d)

    small = {n: w[n] for n in SMALL}
    me = (4 * lax.axis_index("x") + 2 * lax.axis_index("y") + lax.axis_index("c")).astype(jnp.int32).reshape(1)

    gather = _gather_start(
        {n: w[n] for n in GATHER_ORDER},
        [_gather_copy_near if n in GATHER_TWO_LEVEL else _gather_copy for n in GATHER_ORDER])

    passing = {}

    def pass_on(n, after):
        passing[n] = _gather_relay("gather_relay_" + n, gather, GATHER_ORDER.index(n), list(after))
        return (passing[n]["token"],)

    def fetch(names, after):
        out = {}
        for n in [n for n in names if n in GATHER_TWO_LEVEL]:
            if n not in passing:
                pass_on(n, after)
            out[n] = _split_copy_wait("gather_wait_" + n, passing[n], [0], list(after))[0][0]
        direct = [n for n in names if n not in GATHER_TWO_LEVEL]
        if direct:
            got = _split_copy_wait("gather_wait_" + direct[0], gather, [GATHER_ORDER.index(n) for n in direct],
                                   list(after))
            out.update({n: item[0] for n, item in zip(direct, got)})
        return [out[n] for n in names]

    scatters = []

    def publish(tag, products, small_sums=None):
        items = [(products[n], lax.empty((N_DEV - 1, products[n].shape[0] // N_DEV, products[n].shape[1]), BF16))
                 for n in products]
        copies = [_scatter_copy] * len(items)
        if small_sums is not None:
            items.append((small_sums, lax.empty((N_DEV - 1,) + small_sums.shape, F32)))
            copies.append(_small_copy)
        started = _split_copy_start("scatter_start_" + tag, items, copies)
        scatters.append((list(products), small_sums is not None, started))
        return (started["token"],)

    fetch.pass_on = pass_on
    grad_x, own, small_part = _local_step(x2, p2, target2, fetch, publish, small, b_loc, me)

    out_g, out_d, out_m, out_v = {}, {}, {}, {}
    after = [grad_x]
    for names, with_small, started in scatters:
        landed = _split_copy_wait("scatter_wait_" + names[0], started, list(range(len(started["items"]))), after)
        if with_small:
            mine, from_peers = landed[-1]
            loss, sg, sd, sm, sv = _small_adamw(
                me, mine, from_peers, small, {n: m[n] for n in SMALL}, {n: v[n] for n in SMALL})
            for dst, src in ((out_g, sg), (out_d, sd), (out_m, sm), (out_v, sv)):
                dst.update(src)
        recv = {n: item[1] for n, item in zip(names, landed)}
        alike = {}
        for n in names:
            alike.setdefault((own[n].shape, n in TRANSPOSED_HERE), []).append(n)
        for (_, transposed), ns in alike.items():
            res = _reduce_adamw(ns[0], [own[n] for n in ns], [recv[n] for n in ns], [w[n] for n in ns],
                                [m[n] for n in ns], [v[n] for n in ns], transposed)
            for dst, vals in zip((out_g, out_d, out_m, out_v), res):
                dst.update(zip(ns, vals))
        after = [out_v[names[-1]]]

    outs = [loss[0, 0], grad_x.reshape(x.shape)]
    for group in (out_g, out_d, out_m, out_v):
        outs += [restore(n, group[n]) for n in ORDER]
    return tuple(outs)
```

```python
import functools

import jax
import jax.numpy as jnp
from jax import lax
from jax.experimental import pallas as pl
from jax.experimental.pallas import tpu as pltpu

F32, BF16 = jnp.float32, jnp.bfloat16
SDS = jax.ShapeDtypeStruct
MESH = pl.DeviceIdType.MESH

N_DEV = 8
HEAD_DIM = 64
RET_HEADS = 8
ATTN_HEADS = 8
KV_HEADS = 2
GROUP = ATTN_HEADS // KV_HEADS
RET_W = RET_HEADS * HEAD_DIM
ATT_W = ATTN_HEADS * HEAD_DIM
KV_W = KV_HEADS * HEAD_DIM
LANES = 128
CHUNK = 128
BLOCK = 128
Q_SCALE = HEAD_DIM ** -0.5
ALPHA = 2.0 ** 0.25
LN_EPS = 1e-5
GN_EPS = 1e-5
NEG_INF = -1e30
C_RQ, C_RK, C_RV, C_RG = 0, RET_W, 2 * RET_W, 3 * RET_W
C_AQ = 4 * RET_W
C_AK = C_AQ + ATT_W
C_AV = C_AK + KV_W
IN_W = C_AV + KV_W

ADAM_LR = 0.001
ADAM_B1 = 0.9
ADAM_B2 = 0.999
ADAM_EPS = 1e-08
ADAM_WD = 0.01
ADAM_STEP = 10

VMEM_LIMIT = 56 * 1024 * 1024
MATMUL_ROWS = 512
EPILOGUE_ROWS = 256
SUB_ROWS = 512
SMALL_ROWS = 16
ROW_LN1G, ROW_LN1B, ROW_LN2G, ROW_LN2B, ROW_LOSS, ROW_GN, ROW_MISC = 0, 1, 2, 3, 4, 5, 6
MISC_DF, MISC_DB, MISC_SINK = 0, 8, 16


def _dot_nn(a, b):
    return lax.dot_general(a, b, (((1,), (0,)), ((), ())), preferred_element_type=F32)


def _dot_nt(a, b):
    return lax.dot_general(a, b, (((1,), (1,)), ((), ())), preferred_element_type=F32)


def _dot_tn(a, b):
    return lax.dot_general(a, b, (((0,), (0,)), ((), ())), preferred_element_type=F32)


def _params(sem=None, vmem=VMEM_LIMIT):
    kw = {"vmem_limit_bytes": vmem}
    if sem is not None:
        kw["dimension_semantics"] = sem
    return pltpu.CompilerParams(**kw)


def _row_tile(t, want=512):
    tm = want
    while t % tm:
        tm //= 2
    return tm


def _sigmoid(x):
    return jax.nn.sigmoid(x)


def _layer_norm_stats(z):
    mu = jnp.mean(z, axis=1, keepdims=True)
    d = z - mu
    var = jnp.mean(d * d, axis=1, keepdims=True)
    rstd = lax.rsqrt(var + LN_EPS)
    return d * rstd, rstd


def _layer_norm_bwd(dxh, xhat, rstd):
    m1 = jnp.mean(dxh, axis=1, keepdims=True)
    m2 = jnp.mean(dxh * xhat, axis=1, keepdims=True)
    return rstd * (dxh - m1 - xhat * m2)


def _mesh_pos():
    return lax.axis_index("x"), lax.axis_index("y"), lax.axis_index("c")


HBM_SPEC = pl.BlockSpec(memory_space=pltpu.HBM)
SEM_SPEC = pl.BlockSpec(memory_space=pltpu.SEMAPHORE)
ANY_SPEC = pl.BlockSpec(memory_space=pl.ANY)
SIDE_EFFECT = pltpu.SideEffectType.DATAFLOW_SIDE_EFFECTING
PEER_SEMS = pltpu.SemaphoreType.DMA((N_DEV - 1,))


def _in_hbm(a):
    return pltpu.with_memory_space_constraint(a, pltpu.HBM)


def _split_copy_start(name, items, copies):
    n = len(items)
    flat = [a for it in items for a in it]
    k = len(flat)

    def body(*refs):
        arr, sems = list(refs[:k]), refs[k:k + 2 * n]
        for i, it in enumerate(items):
            mine = [arr.pop(0) for _ in it]
            for m in range(1, N_DEV):
                cp = copies[i](m, mine, sems[i].at[m - 1], sems[n + i].at[m - 1])
                if cp is not None:
                    cp.start()
        token = refs[-1]
        token[...] = jnp.zeros_like(token)

    res = pl.pallas_call(
        body, name=name,
        out_shape=[PEER_SEMS] * (2 * n) + [pltpu.HBM(a.shape, a.dtype) for a in flat] + [SDS((8, LANES), F32)],
        in_specs=[HBM_SPEC] * k,
        out_specs=[SEM_SPEC] * (2 * n) + [HBM_SPEC] * k + [pl.BlockSpec(memory_space=pltpu.VMEM)],
        input_output_aliases={j: 2 * n + j for j in range(k)},
        compiler_params=pltpu.CompilerParams(has_side_effects=SIDE_EFFECT),
    )(*[_in_hbm(a) for a in flat])
    thru, out_items = list(res[2 * n:2 * n + k]), []
    for it in items:
        out_items.append(tuple(thru.pop(0) for _ in it))
    return dict(send=res[:n], recv=res[n:2 * n], items=out_items, token=res[-1], copies=copies)


def _gather_start(shards, copies):
    names = list(shards)
    n = len(names)
    flip = [name in TRANSPOSED_HERE for name in names]
    shapes = [shards[name].shape[::-1] if f else shards[name].shape for name, f in zip(names, flip)]
    most = (max(s[0] for s in shapes), max(s[1] for s in shapes))

    def body(*refs):
        src, sems, land, token = refs[:n], refs[n:3 * n], refs[3 * n:4 * n], refs[4 * n]
        wide, narrow, sem = refs[4 * n + 1:]
        for i, (rows, cols) in enumerate(shapes):
            raw = wide.at[0:cols, 0:rows] if flip[i] else wide.at[0:rows, 0:cols]
            bring = pltpu.make_async_copy(src[i], raw, sem.at[0])
            bring.start()
            bring.wait()
            narrow[0:rows, 0:cols] = (raw[...].T if flip[i] else raw[...]).astype(BF16)
            mine = land[i].at[pl.ds(pl.multiple_of(_peer_index(0) * rows, 8), rows), :]
            place = pltpu.make_async_copy(narrow.at[0:rows, 0:cols], mine, sem.at[0])
            place.start()
            place.wait()
            for m in range(1, N_DEV):
                cp = copies[i](m, [land[i]], sems[i].at[m - 1], sems[n + i].at[m - 1])
                if cp is not None:
                    cp.start()
        token[...] = jnp.zeros_like(token)

    side = max(most)
    res = pl.pallas_call(
        body, name="gather_start",
        out_shape=[PEER_SEMS] * (2 * n) + [pltpu.HBM((N_DEV * r, c), BF16) for r, c in shapes] + [SDS((8, LANES), F32)],
        in_specs=[HBM_SPEC] * n,
        out_specs=[SEM_SPEC] * (2 * n) + [HBM_SPEC] * n + [pl.BlockSpec(memory_space=pltpu.VMEM)],
        scratch_shapes=[pltpu.VMEM((side, side), F32), pltpu.VMEM(most, BF16), pltpu.SemaphoreType.DMA((1,))],
        compiler_params=pltpu.CompilerParams(has_side_effects=SIDE_EFFECT),
    )(*[_in_hbm(shards[name]) for name in names])
    return dict(send=res[:n], recv=res[n:2 * n], items=[(a,) for a in res[2 * n:3 * n]], token=res[-1], copies=copies)


def _gather_relay(name, started, which, after):
    (land,) = started["items"][which]

    def body(*refs):
        land_ref, old_send, old_recv = refs[0], refs[1], refs[2]
        send, recv, token = refs[3 + len(after)], refs[4 + len(after)], refs[-1]
        for m in range(1, N_DEV):
            cp = _gather_copy_near(m, [land_ref], old_send.at[m - 1], old_recv.at[m - 1])
            if cp is None:
                continue
            cp.wait_send()
            cp.wait_recv()
            if m > 1:
                _gather_copy_pass(m + 1, [land_ref], send.at[m], recv.at[m]).start()
        token[...] = jnp.zeros_like(token)

    res = pl.pallas_call(
        body, name=name,
        out_shape=[PEER_SEMS, PEER_SEMS, pltpu.HBM(land.shape, land.dtype), SDS((8, LANES), F32)],
        in_specs=[HBM_SPEC, SEM_SPEC, SEM_SPEC] + [ANY_SPEC] * len(after),
        out_specs=[SEM_SPEC, SEM_SPEC, HBM_SPEC, pl.BlockSpec(memory_space=pltpu.VMEM)],
        input_output_aliases={0: 2},
        compiler_params=pltpu.CompilerParams(has_side_effects=SIDE_EFFECT),
    )(land, started["send"][which], started["recv"][which], *[_in_hbm(a) for a in after])
    return dict(send=[res[0]], recv=[res[1]], items=[(res[2],)], token=res[3], copies=[_gather_copy_pass])


def _split_copy_wait(name, started, which, after):
    items = [started["items"][i] for i in which]
    copies = [started["copies"][i] for i in which]
    n = len(items)
    flat = [a for it in items for a in it]
    k = len(flat)

    def body(*refs):
        arr, sems = list(refs[:k]), refs[k:k + 2 * n]
        for i, it in enumerate(items):
            mine = [arr.pop(0) for _ in it]
            for m in range(1, N_DEV):
                cp = copies[i](m, mine, sems[i].at[m - 1], sems[n + i].at[m - 1])
                if cp is not None:
                    cp.wait_send()
                    cp.wait_recv()

    res = pl.pallas_call(
        body, name=name,
        out_shape=[pltpu.HBM(a.shape, a.dtype) for a in flat],
        in_specs=[HBM_SPEC] * k + [SEM_SPEC] * (2 * n) + [ANY_SPEC] * len(after),
        out_specs=[HBM_SPEC] * k,
        input_output_aliases={j: j for j in range(k)},
        compiler_params=pltpu.CompilerParams(has_side_effects=SIDE_EFFECT),
    )(*flat, *[started["send"][i] for i in which], *[started["recv"][i] for i in which], *[_in_hbm(a) for a in after])
    thru, out_items = list(res), []
    for it in items:
        out_items.append(tuple(thru.pop(0) for _ in it))
    return out_items


def _gather_copy(m, refs, send_sem, recv_sem):
    (land_ref,) = refs
    r = land_ref.shape[0] // N_DEV
    mine = land_ref.at[pl.ds(pl.multiple_of(_peer_index(0) * r, 8), r), :]
    return pltpu.make_async_remote_copy(src_ref=mine, dst_ref=mine, send_sem=send_sem, recv_sem=recv_sem,
                                        device_id=_peer(m), device_id_type=MESH)


def _gather_copy_near(m, refs, send_sem, recv_sem):
    return _gather_copy(m, refs, send_sem, recv_sem) if m == 1 or m % 2 == 0 else None


def _gather_copy_pass(m, refs, send_sem, recv_sem):
    if m == 1 or m % 2 == 0:
        return None
    (land_ref,) = refs
    r = land_ref.shape[0] // N_DEV
    block = land_ref.at[pl.ds(pl.multiple_of(_peer_index(m ^ 1) * r, 8), r), :]
    return pltpu.make_async_remote_copy(src_ref=block, dst_ref=block, send_sem=send_sem, recv_sem=recv_sem,
                                        device_id=_peer(1), device_id_type=MESH)


def _small_copy(m, refs, send_sem, recv_sem):
    part_ref, land_ref = refs
    return pltpu.make_async_remote_copy(src_ref=part_ref, dst_ref=land_ref.at[m - 1], send_sem=send_sem,
                                        recv_sem=recv_sem, device_id=_peer(m), device_id_type=MESH)


def _scatter_copy(m, refs, send_sem, recv_sem):
    buf_ref, land_ref = refs
    r = buf_ref.shape[0] // N_DEV
    src = buf_ref.at[pl.ds(pl.multiple_of(_peer_index(m) * r, 8), r), :]
    return pltpu.make_async_remote_copy(src_ref=src, dst_ref=land_ref.at[m - 1], send_sem=send_sem,
                                        recv_sem=recv_sem, device_id=_peer(m), device_id_type=MESH)


def _peer(m):
    x, y, c = _mesh_pos()
    bx, by, bc = (m >> 2) & 1, (m >> 1) & 1, m & 1
    return (x ^ bx if bx else x, y ^ by if by else y, c ^ bc if bc else c)


def _peer_index(m):
    x, y, c = _mesh_pos()
    return (4 * x + 2 * y + c) ^ m


SMALL_PLACE = {
    "ln1_gain": (ROW_LN1G, 0), "ln1_bias": (ROW_LN1B, 0), "ln2_gain": (ROW_LN2G, 0), "ln2_bias": (ROW_LN2B, 0),
    "ret_gn_gain": (ROW_GN, 0), "ret_decay_fwd": (ROW_MISC, MISC_DF), "ret_decay_bwd": (ROW_MISC, MISC_DB),
    "attn_sink": (ROW_MISC, MISC_SINK)}


def _small_adamw(me, part, landed, w, m, v):
    d = part.shape[1]
    names = list(SMALL_PLACE)
    k = len(names)

    def body(*refs):
        me_ref, part_ref, land_ref = refs[:3]
        refs = refs[2:]
        w_refs, m_refs, v_refs = refs[1:1 + k], refs[1 + k:1 + 2 * k], refs[1 + 2 * k:1 + 3 * k]
        outs = refs[1 + 3 * k:1 + 7 * k + 1]
        tot_ref = refs[-1]
        loss_ref, g_refs, dl_refs = outs[0], outs[1:1 + k], outs[1 + k:1 + 2 * k]
        nm_refs, nv_refs = outs[1 + 2 * k:1 + 3 * k], outs[1 + 3 * k:1 + 4 * k]
        tot = jnp.zeros(part_ref.shape, F32)
        for dev in range(N_DEV):
            j = dev ^ me_ref[0]
            tot = tot + jnp.where(j == 0, part_ref[...], land_ref[jnp.maximum(j, 1) - 1])
        tot_ref[...] = tot
        loss_ref[...] = (0.5 / d) * jnp.sum(tot_ref[ROW_LOSS:ROW_LOSS + 1, :], axis=1, keepdims=True)
        for i, name in enumerate(names):
            row, lo = SMALL_PLACE[name]
            wv = w_refs[i][...]
            g = tot_ref[row:row + 1, lo:lo + wv.shape[1]]
            if name.startswith("ret_decay"):
                p2 = jnp.exp2(wv)
                g = g * (-p2 * jnp.log(2.0) / (1.0 - p2))
            g_refs[i][...] = g
            _adamw_store(g, wv, m_refs[i][...], v_refs[i][...], dl_refs[i], nm_refs[i], nv_refs[i])

    shapes = [SDS(w[n].shape, F32) for n in names]
    vm = pl.BlockSpec(memory_space=pltpu.VMEM)
    res = pl.pallas_call(
        body, name="small_adamw", out_shape=[SDS((1, 1), F32)] + shapes * 4,
        in_specs=[_smem_spec()] + [vm] * (2 + 3 * k), out_specs=[vm] * (1 + 4 * k),
        scratch_shapes=[pltpu.VMEM(part.shape, F32)],
    )(me, part, landed, *[w[n] for n in names], *[m[n] for n in names], *[v[n] for n in names])
    groups = [dict(zip(names, res[1 + j * k:1 + (j + 1) * k])) for j in range(4)]
    return (res[0], *groups)


def _adamw_store(g, w, m, v, dl_ref, nm_ref, nv_ref):
    m = ADAM_B1 * m + (1.0 - ADAM_B1) * g
    v = ADAM_B2 * v + (1.0 - ADAM_B2) * (g * g)
    m_hat = m / (1.0 - ADAM_B1 ** ADAM_STEP)
    v_hat = v / (1.0 - ADAM_B2 ** ADAM_STEP)
    dl_ref[...] = -ADAM_LR * (m_hat / (jnp.sqrt(v_hat) + ADAM_EPS) + ADAM_WD * w)
    nm_ref[...] = m
    nv_ref[...] = v


def _reduce_adamw(name, owns, recvs, ws, ms, vs, transposed):
    count = len(owns)
    rows, n = owns[0].shape
    steps = 1 if transposed or rows % 16 else 2
    rb = rows // steps

    def body(*refs):
        ins, outs = refs[:5 * count], refs[5 * count:]
        j = pl.program_id(0)
        for k in range(count):
            @pl.when(j == k)
            def _(k=k):
                own_ref, recv_ref, w_ref, m_ref, v_ref = ins[5 * k:5 * k + 5]
                g_ref, dl_ref, nm_ref, nv_ref = outs[4 * k:4 * k + 4]
                g = own_ref[...]
                for p in range(recv_ref.shape[0]):
                    g = g + recv_ref[p].astype(F32)
                if transposed:
                    g = g.T
                g_ref[...] = g
                _adamw_store(g, w_ref[...], m_ref[...], v_ref[...], dl_ref, nm_ref, nv_ref)

    def turn(k):
        return lambda j, i: jnp.where(j == k, i, jnp.where(j < k, 0, steps - 1))

    in_specs, out_specs = [], []
    for k in range(count):
        at = turn(k)
        blk = pl.BlockSpec(ws[0].shape if transposed else (rb, n), lambda j, i, at=at: (at(j, i), 0))
        in_specs += [pl.BlockSpec((rb, n), lambda j, i, at=at: (at(j, i), 0)),
                     pl.BlockSpec((recvs[k].shape[0], rb, n), lambda j, i, at=at: (0, at(j, i), 0)), blk, blk, blk]
        out_specs += [blk] * 4
    res = pl.pallas_call(
        body, name="adamw_" + name, grid=(count, steps), in_specs=in_specs, out_specs=out_specs,
        out_shape=[SDS(ws[0].shape, F32)] * (4 * count), compiler_params=_params(("arbitrary", "arbitrary")),
    )(*[a for k in range(count) for a in (owns[k], recvs[k], ws[k], ms[k], vs[k])])
    return [list(res[j::4]) for j in range(4)]


def _row_spec(tm, width):
    return pl.BlockSpec((tm, width), lambda i: (i, 0))


def _full_spec(shape):
    return pl.BlockSpec(shape, lambda i: (0,) * len(shape))


_acc_spec = _full_spec


def _sub_rows(tm):
    step = min(SUB_ROWS, tm)
    return [(lo, lo + step) for lo in range(0, tm, step)]


def _in_proj(x2, wt_in):
    t, d = x2.shape
    u_w = wt_in.shape[0]
    tm = _row_tile(t, MATMUL_ROWS)

    def body(x_ref, w_ref, u_ref, xb_ref):
        xb = x_ref[...].astype(BF16)
        xb_ref[...] = xb
        u_ref[...] = _dot_nt(xb, w_ref[...]).astype(BF16)

    return pl.pallas_call(
        body, name="in_proj", grid=(t // tm,),
        in_specs=[_row_spec(tm, d), _full_spec(wt_in.shape)],
        out_specs=[_row_spec(tm, u_w), _row_spec(tm, d)],
        out_shape=[SDS((t, u_w), BF16), SDS((t, d), BF16)],
        compiler_params=_params(("parallel",)),
    )(x2, wt_in)


def _col_halves(f):
    n = f // LANES
    k = (n + 1) // 2 * LANES
    return [(0, k), (k, f)] if k < f else [(0, f)]


def _mix_ln1_ffn_up(r, a, x2, w_out, wt_gate, wt_up, g1, b1):
    t, d = x2.shape
    f = wt_gate.shape[0]
    tm = _row_tile(t, EPILOGUE_ROWS)

    def body(r_ref, a_ref, x_ref, wo_ref, wg_ref, wu_ref, g_ref, b_ref, xh_ref, rs_ref, hb_ref, dg_ref, du_ref,
             act_ref):
        mix = _dot_nn(r_ref[...], wo_ref[0:RET_W, :]) + _dot_nn(a_ref[...], wo_ref[RET_W:RET_W + ATT_W, :])
        z = ALPHA * x_ref[...] + mix
        xhat, rstd = _layer_norm_stats(z)
        xh_ref[...] = xhat
        rs_ref[...] = jnp.broadcast_to(rstd, rs_ref.shape)
        h = (xhat * g_ref[...] + b_ref[...]).astype(BF16)
        hb_ref[...] = h
        g = _dot_nt(h, wg_ref[...])
        u = _dot_nt(h, wu_ref[...])
        sg = _sigmoid(g)
        silu = g * sg
        dg_ref[...] = (u * (sg * (1.0 + g * (1.0 - sg)))).astype(BF16)
        du_ref[...] = silu.astype(BF16)
        act_ref[...] = (silu * u).astype(BF16)

    wide, narrow = _row_spec(tm, f), _row_spec(tm, d)
    return pl.pallas_call(
        body, name="mix_ln1_ffn_up", grid=(t // tm,),
        in_specs=[_row_spec(tm, RET_W), _row_spec(tm, ATT_W), narrow, _resident_spec(w_out.shape),
                  _resident_spec(wt_gate.shape), _resident_spec(wt_up.shape), _full_spec(g1.shape),
                  _full_spec(b1.shape)],
        out_specs=[narrow, _row_spec(tm, LANES), narrow, wide, wide, wide],
        out_shape=[SDS((t, d), F32), SDS((t, LANES), F32), SDS((t, d), BF16)] + [SDS((t, f), BF16)] * 3,
        compiler_params=_params(("parallel",)),
    )(r, a, x2, w_out, wt_gate, wt_up, g1, b1)


def _ffn_down_ln2_loss(act, dact_dg, dact_du, h1b, p2, xhat1, target, w_down, w_pg, wt_pe, g1, b1, g2, b2):
    t, d = xhat1.shape
    f = act.shape[1]
    pdim = p2.shape[1]
    tm = _row_tile(t, EPILOGUE_ROWS)

    def body(act_ref, fg_ref, fu_ref, hb_ref, p_ref, xh1_ref, tgt_ref, wd_ref, wpg_ref, wpe_ref, g1_ref, b1_ref,
             g2_ref, b2_ref, dz_ref, dzb_ref, ds_ref, dple_ref, dg_ref, du_ref, acc_ref):
        @pl.when(pl.program_id(0) == 0)
        def _():
            acc_ref[...] = jnp.zeros_like(acc_ref)

        for lo, hi in _sub_rows(tm):
            h1 = xh1_ref[lo:hi, :] * g1_ref[...] + b1_ref[...]
            pg = _sigmoid(_dot_nn(hb_ref[lo:hi, :], wpg_ref[...]))
            ple = _dot_nt(p_ref[lo:hi, :].astype(BF16), wpe_ref[...])
            gated = pg * ple
            dgate = gated * (1.0 - pg)
            ffn = _dot_nn(act_ref[lo:hi, :], wd_ref[...])
            z2 = ALPHA * h1 + gated + ffn
            xhat2, rstd2 = _layer_norm_stats(z2)
            err = xhat2 * g2_ref[...] + b2_ref[...] - tgt_ref[lo:hi, :]
            dy = err * (1.0 / d)
            dz = _layer_norm_bwd(dy * g2_ref[...], xhat2, rstd2)
            dzb = dz.astype(BF16)
            dz_ref[lo:hi, :] = dz
            dzb_ref[lo:hi, :] = dzb
            ds_ref[lo:hi, :] = (dz * dgate).astype(BF16)
            dple_ref[lo:hi, :] = (dz * pg).astype(BF16)
            acc_ref[0:1, :] += jnp.sum(err * err, axis=0, keepdims=True)
            acc_ref[1:2, :] += jnp.sum(dy * xhat2, axis=0, keepdims=True)
            acc_ref[2:3, :] += jnp.sum(dy, axis=0, keepdims=True)
            for c0, c1 in _col_halves(f):
                da = _dot_nt(dzb, wd_ref[c0:c1, :])
                dg_ref[lo:hi, c0:c1] = (da * fg_ref[lo:hi, c0:c1].astype(F32)).astype(BF16)
                du_ref[lo:hi, c0:c1] = (da * fu_ref[lo:hi, c0:c1].astype(F32)).astype(BF16)

    vec = _full_spec(g1.shape)
    wide, narrow = _row_spec(tm, f), _row_spec(tm, d)
    return pl.pallas_call(
        body, name="ffn_down_ln2_loss", grid=(t // tm,),
        in_specs=[wide, wide, wide, narrow, _row_spec(tm, pdim), narrow, narrow,
                  _full_spec(w_down.shape), _full_spec(w_pg.shape), _full_spec(wt_pe.shape), vec, vec, vec, vec],
        out_specs=[narrow] * 4 + [wide, wide, _acc_spec((8, d))],
        out_shape=[SDS((t, d), F32), SDS((t, d), BF16), SDS((t, d), BF16), SDS((t, d), BF16),
                   SDS((t, f), BF16), SDS((t, f), BF16), SDS((8, d), F32)],
        compiler_params=_params(("arbitrary",)),
    )(act, dact_dg, dact_du, h1b, p2, xhat1, target, w_down, w_pg, wt_pe, g1, b1, g2, b2)


def _after(after, body):
    k = len(after)
    return (lambda *refs: body(*refs[k:])), [ANY_SPEC] * k


def _resident_spec(shape):
    return pl.BlockSpec(shape, lambda i: (0,) * len(shape), pipeline_mode=pl.Buffered(1))


def _dh1_ln1_bwd(dz2, dg, dup, dsb, xhat1, rstd1, wt_gate, wt_up, w_pg, w_out, g1, after=()):
    t, d = dz2.shape
    f = dg.shape[1]
    tm = _row_tile(t, MATMUL_ROWS)

    def body(dz_ref, dg_ref, du_ref, ds_ref, xh1_ref, rs1_ref, wg_ref, wu_ref, wpg_ref, wo_ref, g1_ref,
             dz1_ref, dz1b_ref, dr_ref, da_ref, acc_ref):
        @pl.when(pl.program_id(0) == 0)
        def _():
            acc_ref[...] = jnp.zeros_like(acc_ref)

        for lo, hi in _sub_rows(tm):
            dh = (ALPHA * dz_ref[lo:hi, :] + _dot_nn(dg_ref[lo:hi, :], wg_ref[...])
                  + _dot_nn(du_ref[lo:hi, :], wu_ref[...]) + _dot_nt(ds_ref[lo:hi, :], wpg_ref[...]))
            xhat, rstd = xh1_ref[lo:hi, :], rs1_ref[lo:hi, 0:1]
            dz1 = _layer_norm_bwd(dh * g1_ref[...], xhat, rstd)
            dz1b = dz1.astype(BF16)
            dz1_ref[lo:hi, :] = dz1
            dz1b_ref[lo:hi, :] = dz1b
            acc_ref[0:1, :] += jnp.sum(dh * xhat, axis=0, keepdims=True)
            acc_ref[1:2, :] += jnp.sum(dh, axis=0, keepdims=True)
            dr_ref[lo:hi, :] = _dot_nt(dz1b, wo_ref[0:RET_W, :]).astype(BF16)
            da_ref[lo:hi, :] = _dot_nt(dz1b, wo_ref[RET_W:RET_W + ATT_W, :]).astype(BF16)

    body, lead = _after(after, body)
    return pl.pallas_call(
        body, name="dh1_ln1_bwd", grid=(t // tm,),
        in_specs=lead + [_row_spec(tm, d), _row_spec(tm, f), _row_spec(tm, f), _row_spec(tm, d), _row_spec(tm, d),
                         _row_spec(tm, LANES), _resident_spec(wt_gate.shape), _resident_spec(wt_up.shape), _resident_spec(w_pg.shape),
                         _resident_spec(w_out.shape), _full_spec(g1.shape)],
        out_specs=[_row_spec(tm, d), _row_spec(tm, d), _row_spec(tm, RET_W), _row_spec(tm, ATT_W), _acc_spec((8, d))],
        out_shape=[SDS((t, d), F32), SDS((t, d), BF16), SDS((t, RET_W), BF16), SDS((t, ATT_W), BF16),
                   SDS((8, d), F32)],
        compiler_params=_params(("arbitrary",)),
    )(*after, dz2, dg, dup, dsb, xhat1, rstd1, wt_gate, wt_up, w_pg, w_out, g1)


def _in_proj_bwd(dz1, parts, wt_in, after=()):
    t, d = dz1.shape
    tm = _row_tile(t, MATMUL_ROWS)
    widths = [p.shape[1] for p in parts]

    def body(*refs):
        dz_ref, part_refs, w_ref, dx_ref = refs[0], refs[1:1 + len(parts)], refs[-2], refs[-1]
        acc = ALPHA * dz_ref[...]
        lo = 0
        for p_ref, w in zip(part_refs, widths):
            acc = acc + _dot_nn(p_ref[...], w_ref[lo:lo + w, :])
            lo += w
        dx_ref[...] = acc

    body, lead = _after(after, body)
    return pl.pallas_call(
        body, name="in_proj_bwd", grid=(t // tm,),
        in_specs=lead + [_row_spec(tm, d)] + [_row_spec(tm, w) for w in widths] + [_full_spec(wt_in.shape)],
        out_specs=_row_spec(tm, d), out_shape=SDS((t, d), F32),
        compiler_params=_params(("parallel",)),
    )(*after, dz1, *parts, wt_in)


def _weight_grad(name, me, parts, rhs, after=()):
    t, n = rhs.shape
    widths = [p.shape[1] for p in parts]
    rows = sum(widths)
    own_rows = rows // N_DEV
    tk = _row_tile(t, MATMUL_ROWS)
    n_steps = t // tk
    step = 256

    def body(*refs):
        me_ref, part_refs, rhs_ref = refs[0], refs[1:1 + len(parts)], refs[1 + len(parts)]
        full_ref, own_ref, acc = refs[-3], refs[-2], refs[-1]
        i = pl.program_id(0)

        def products(first):
            b = rhs_ref[...].astype(BF16)
            lo = 0
            for p_ref, w in zip(part_refs, widths):
                for c0 in range(0, w, step):
                    c1 = min(c0 + step, w)
                    val = _dot_tn(p_ref[:, c0:c1].astype(BF16), b)
                    if first:
                        acc[lo + c0:lo + c1, :] = val
                    else:
                        acc[lo + c0:lo + c1, :] += val
                lo += w

        pl.when(i == 0)(functools.partial(products, True))
        pl.when(i > 0)(functools.partial(products, False))

        @pl.when(i == n_steps - 1)
        def _():
            full_ref[...] = acc[...].astype(BF16)
            own_ref[...] = acc[pl.ds(pl.multiple_of(me_ref[0] * own_rows, 8), own_rows), :]

    body, lead = _after(after, body)
    return pl.pallas_call(
        body, name=name, grid=(n_steps,),
        in_specs=lead + [_smem_spec()] + [_row_spec(tk, w) for w in widths] + [_row_spec(tk, n)],
        out_specs=[_full_spec((rows, n)), _full_spec((own_rows, n))],
        out_shape=[SDS((rows, n), BF16), SDS((own_rows, n), F32)],
        scratch_shapes=[pltpu.VMEM((rows, n), F32)],
        compiler_params=_params(("arbitrary",)),
    )(*after, me, *parts, rhs)


def _weight_grad_jobs(name, me, jobs, after=()):
    count = len(jobs)
    t = jobs[0][0].shape[0]
    tk = _row_tile(t, MATMUL_ROWS)
    n_steps = t // tk
    shapes = [(lhs.shape[1], rhs.shape[1]) for lhs, rhs in jobs]
    most_rows, most_cols = max(r for r, _ in shapes), max(n for _, n in shapes)
    step = 256

    def body(*refs):
        me_ref, lhs_refs, rhs_refs = refs[0], refs[1:1 + count], refs[1 + count:1 + 2 * count]
        full_refs, own_refs = refs[1 + 2 * count:1 + 3 * count], refs[1 + 3 * count:1 + 4 * count]
        acc, whole, mine, sems = refs[1 + 4 * count:]
        job, i = pl.program_id(0), pl.program_id(1)

        def leaving(j):
            rows, n = shapes[j]
            return (pltpu.make_async_copy(whole.at[0:rows, 0:n], full_refs[j], sems.at[0]),
                    pltpu.make_async_copy(mine.at[0:rows // N_DEV, 0:n], own_refs[j], sems.at[1]))

        def products(j, first):
            rows, n = shapes[j]
            b = rhs_refs[j][...].astype(BF16)
            for c0 in range(0, rows, step):
                c1 = min(c0 + step, rows)
                val = _dot_tn(lhs_refs[j][:, c0:c1].astype(BF16), b)
                if first:
                    acc[c0:c1, 0:n] = val
                else:
                    acc[c0:c1, 0:n] += val

        def finish(j):
            rows, n = shapes[j]
            own_rows = rows // N_DEV
            if j > 0:
                for cp in leaving(j - 1):
                    cp.wait()
            whole[0:rows, 0:n] = acc[0:rows, 0:n].astype(BF16)
            mine[0:own_rows, 0:n] = acc[pl.ds(pl.multiple_of(me_ref[0] * own_rows, 8), own_rows), 0:n]
            for cp in leaving(j):
                cp.start()
            if j == count - 1:
                for cp in leaving(j):
                    cp.wait()

        for j in range(count):
            pl.when((job == j) & (i == 0))(functools.partial(products, j, True))
            pl.when((job == j) & (i > 0))(functools.partial(products, j, False))
            pl.when((job == j) & (i == n_steps - 1))(functools.partial(finish, j))

    def turn(j):
        return lambda job, i: (jnp.where(job == j, i, jnp.where(job < j, 0, n_steps - 1)), 0)

    body, lead = _after(after, body)
    res = pl.pallas_call(
        body, name=name, grid=(count, n_steps),
        in_specs=lead + [_smem_spec()] + [pl.BlockSpec((tk, rows), turn(j)) for j, (rows, _) in enumerate(shapes)]
        + [pl.BlockSpec((tk, n), turn(j)) for j, (_, n) in enumerate(shapes)],
        out_specs=[ANY_SPEC] * (2 * count),
        out_shape=[SDS((rows, n), BF16) for rows, n in shapes] + [SDS((rows // N_DEV, n), F32) for rows, n in shapes],
        scratch_shapes=[pltpu.VMEM((most_rows, most_cols), F32), pltpu.VMEM((most_rows, most_cols), BF16),
                        pltpu.VMEM((most_rows // N_DEV, most_cols), F32), pltpu.SemaphoreType.DMA((2,))],
        compiler_params=_params(("arbitrary", "arbitrary")),
    )(*after, me, *[lhs for lhs, _ in jobs], *[rhs for _, rhs in jobs])
    return list(res[:count]), list(res[count:])


def _log_decay(decay_f, decay_b):
    def body(f_ref, b_ref, lf_ref, lb_ref):
        lf_ref[...] = jnp.log1p(-jnp.exp2(f_ref[...]))
        lb_ref[...] = jnp.log1p(-jnp.exp2(b_ref[...]))

    return pl.pallas_call(body, name="log_decay", out_shape=[SDS(decay_f.shape, F32)] * 2)(decay_f, decay_b)


def _chunk(ref, n):
    return ref[pl.ds(pl.multiple_of(n * CHUNK, CHUNK), CHUNK), :]


def _group_sum(is_a, v):
    sa = jnp.sum(jnp.where(is_a, v, 0.0), axis=1, keepdims=True)
    sb = jnp.sum(jnp.where(is_a, 0.0, v), axis=1, keepdims=True)
    return jnp.where(is_a, sa, sb)


def _seq_spec(s, col_block):
    return pl.BlockSpec((s, LANES), lambda b, h: (b, col_block + h))


def _smem_spec():
    return pl.BlockSpec(memory_space=pltpu.SMEM)


RET_UNROLL = 4


def _chunk_loop(n_chunks, body, init):
    u = RET_UNROLL if n_chunks % RET_UNROLL == 0 else 1

    def trip(i, carry):
        for j in range(u):
            carry = body(i * u + j, carry)
        return carry

    return lax.fori_loop(0, n_chunks // u, trip, init)


def _stacked_tables(lgf_ref, lgb_ref, pair):
    lane = lax.broadcasted_iota(jnp.int32, (1, LANES), 1)
    is_a = lane < HEAD_DIM
    lgf = jnp.where(is_a, lgf_ref[2 * pair], lgf_ref[2 * pair + 1])
    lgb = jnp.where(is_a, lgb_ref[2 * pair], lgb_ref[2 * pair + 1])
    row = lax.broadcasted_iota(jnp.int32, (CHUNK, 1), 0).astype(F32)
    kdec_f, qdec_f = jnp.exp(lgf * (CHUNK - 1.0 - row)), jnp.exp(lgf * (row + 1.0))
    kdec_b, qdec_b = jnp.exp(lgb * row), jnp.exp(lgb * (CHUNK - row))
    tab = dict(
        is_a=is_a, row=row, lam_f=jnp.exp(lgf * CHUNK), lam_b=jnp.exp(lgb * CHUNK),
        kdec=jnp.concatenate([kdec_f, kdec_b], axis=1), qdec=jnp.concatenate([qdec_f, qdec_b], axis=1),
        qexp=jnp.concatenate([jnp.broadcast_to(row + 1.0, (CHUNK, LANES)),
                              jnp.broadcast_to(CHUNK - row, (CHUNK, LANES))], axis=1),
        kexp=jnp.concatenate([jnp.broadcast_to(CHUNK - 1.0 - row, (CHUNK, LANES)),
                              jnp.broadcast_to(row, (CHUNK, LANES))], axis=1),
    )
    r = lax.broadcasted_iota(jnp.int32, (2 * LANES, LANES), 0)
    c = lax.broadcasted_iota(jnp.int32, (2 * LANES, LANES), 1)
    tab["diag2"] = ((r & (LANES - 1)) < HEAD_DIM) == (c < HEAD_DIM)
    i2 = lax.broadcasted_iota(jnp.int32, (2 * CHUNK, CHUNK), 0)
    j = lax.broadcasted_iota(jnp.int32, (2 * CHUNK, CHUNK), 1)
    head_b = i2 >= CHUNK
    diff = ((i2 & (CHUNK - 1)) - j).astype(F32)
    up, dn = jnp.maximum(diff, 0.0), jnp.maximum(-diff, 0.0)
    lgf2 = jnp.where(head_b, lgf_ref[2 * pair + 1], lgf_ref[2 * pair])
    lgb2 = jnp.where(head_b, lgb_ref[2 * pair + 1], lgb_ref[2 * pair])
    ef = jnp.where(diff >= 0, jnp.exp(lgf2 * up), 0.0)
    eb = jnp.where(diff <= 0, jnp.exp(lgb2 * dn), 0.0)
    tab["d2"] = ef + eb
    tab["df2"] = ef * up
    tab["db2"] = eb * dn
    return tab


def _stack_pair(is_a, x):
    zero = jnp.zeros_like(x)
    return jnp.concatenate([jnp.where(is_a, x, zero), jnp.where(is_a, zero, x)], axis=0)


def _unstack_pair(is_a, x2):
    return jnp.where(is_a, x2[0:CHUNK, :], x2[CHUNK:2 * CHUNK, :])


def _both_ways(x, dec):
    return (jnp.concatenate([x, x], axis=1) * dec).astype(BF16)


def _scan_states(n_chunks, st, up_rows, up_lam, down_rows, down_lam):
    zero = jnp.zeros((LANES, LANES), F32)

    def up(n, r):
        new = st[n, up_rows, :]
        st[n, up_rows, :] = r
        return r * up_lam + new

    def down(s, r):
        n = n_chunks - 1 - s
        new = st[n, down_rows, :]
        st[n, down_rows, :] = r
        return r * down_lam + new

    lax.fori_loop(0, n_chunks, up, zero)
    lax.fori_loop(0, n_chunks, down, zero)


FWD_ROWS, BWD_ROWS = pl.ds(0, LANES), pl.ds(LANES, LANES)


def _state_spec(n_chunks, pairs):
    return pl.BlockSpec((n_chunks, 2 * LANES, LANES), lambda b, h: (b * pairs + h, 0, 0))


ST_GAIN, ST_XF, ST_XB, ST_IFA, ST_IFB, ST_IBA, ST_IBB, ST_LF, ST_LB = 0, 1, 2, 3, 4, 5, 6, 8, 9
ST_ROWS = 16


GW = GROUP * HEAD_DIM
KEYS = 3 * BLOCK


def _attn_tables(g, bias_ref):
    r = lax.broadcasted_iota(jnp.int32, (GROUP * BLOCK, KEYS), 0)
    kj = lax.broadcasted_iota(jnp.int32, (GROUP * BLOCK, KEYS), 1)
    qi = r & (BLOCK - 1)
    hh = lax.shift_right_logical(r, 7)
    dist = jnp.abs(kj - BLOCK - qi)
    slope = jnp.exp2(-(GROUP * g + hh + 1).astype(F32) * (8.0 / ATTN_HEADS))
    inside = jnp.where(dist <= BLOCK, -slope * dist.astype(F32), NEG_INF)
    bias_ref[BIAS_INSIDE] = inside
    bias_ref[BIAS_FIRST] = jnp.where(kj >= BLOCK, inside, NEG_INF)
    bias_ref[BIAS_LAST] = jnp.where(kj < 2 * BLOCK, inside, NEG_INF)


BIAS_INSIDE, BIAS_FIRST, BIAS_LAST = 0, 1, 2


def _own_lanes(g):
    return lax.shift_right_logical(lax.broadcasted_iota(jnp.int32, (1, LANES), 1), 6) == g


def _mask_keys(x_ref, g, scale, pad_ref, s):
    pad_ref[0:BLOCK, :] = jnp.zeros((BLOCK, LANES), BF16)
    pad_ref[BLOCK + s:2 * BLOCK + s, :] = jnp.zeros((BLOCK, LANES), BF16)
    pad_ref[BLOCK:BLOCK + s, :] = jnp.where(_own_lanes(g), x_ref[...].astype(F32) * scale, 0.0).astype(BF16)


def _lane_block(x, j):
    return x[:, j * LANES:(j + 1) * LANES]


def _stack_heads(x, g):
    assert GROUP == 4 and GW == 2 * LANES
    x1 = pltpu.roll(x, HEAD_DIM, 1)
    keep = _own_lanes(g)
    zero = jnp.zeros((BLOCK, LANES), x.dtype)
    rows = []
    for h in range(GROUP):
        for_g0 = _lane_block(x, h // 2) if h % 2 == 0 else _lane_block(x1, ((h + 1) // 2) % 2)
        for_g1 = _lane_block(x, h // 2) if h % 2 == 1 else _lane_block(x1, h // 2)
        rows.append(jnp.where(keep, jnp.where(g == 0, for_g0, for_g1), zero))
    return jnp.concatenate(rows, axis=0)


def _unstack_heads(x4, g):
    p = [x4[h * BLOCK:(h + 1) * BLOCK, :] for h in range(GROUP)]
    cat = lambda a, b: jnp.concatenate([a, b], axis=1)
    in_place = jnp.where(g == 0, cat(p[0], p[2]), cat(p[1], p[3]))
    one_left = jnp.where(g == 0, cat(p[1], p[3]), cat(p[2], p[0]))
    return in_place + pltpu.roll(one_left, HEAD_DIM, 1)


def _sink_column(sink_ref, g):
    rh = lax.shift_right_logical(lax.broadcasted_iota(jnp.int32, (GROUP * BLOCK, 1), 0), 7)
    col = jnp.zeros((GROUP * BLOCK, 1), F32)
    for h in range(GROUP):
        col = jnp.where(rh == h, sink_ref[GROUP * g + h], col)
    return col


def _attn_probs(qm, k3, bias_ref, sink_col, n, s):
    which = jnp.where(n == 0, BIAS_FIRST, jnp.where(n == s // BLOCK - 1, BIAS_LAST, BIAS_INSIDE))
    logits = _dot_nt(qm, k3) + bias_ref[which]
    m = jnp.maximum(jnp.max(logits, axis=1, keepdims=True), sink_col)
    e = jnp.exp(logits - m)
    e_sink = jnp.exp(sink_col - m)
    inv = 1.0 / (jnp.sum(e, axis=1, keepdims=True) + e_sink)
    return e * inv, e_sink * inv


PAIRS_PER_KV = (RET_HEADS // 2) // KV_HEADS
FWD_ORDER = "rrarra"
BWD_ORDER = "rararr"


def _trip_order(order, chunks, blocks):
    if order.count("r") == chunks and order.count("a") == blocks:
        return order
    return "r" * chunks + "a" * blocks


def _mixers_fwd(u, lgf, lgb, gn_gain, sink, b_loc, after=()):
    t = u.shape[0]
    s = t // b_loc
    n_chunks = s // CHUNK
    pairs = RET_HEADS // 2
    trips = n_chunks // RET_UNROLL
    blocks_half = (s // BLOCK) // PAIRS_PER_KV
    per_trip = blocks_half // trips
    assert n_chunks % RET_UNROLL == 0 and blocks_half % trips == 0 and PAIRS_PER_KV == 2 and s >= 2 * BLOCK

    def body(lgf_ref, lgb_ref, sink_ref, q_ref, k_ref, v_ref, g_ref, gain_ref, aq_ref, ak_ref, av_ref,
             r_ref, xhat_ref, rstd_ref, a_ref, st, kpad, vpad, bias):
        pair = pl.program_id(1)
        g, half = lax.shift_right_logical(pair, 1), pair & 1
        tab = _stacked_tables(lgf_ref, lgb_ref, pair)
        is_a = tab["is_a"]

        @pl.when(half == 0)
        def _():
            _attn_tables(g, bias)
            _mask_keys(ak_ref, g, Q_SCALE, kpad, s)
            _mask_keys(av_ref, g, 1.0, vpad, s)

        def kv_body(n, _):
            k8 = _chunk(k_ref, n).astype(F32) * Q_SCALE
            st[n] = jnp.where(tab["diag2"], _dot_tn(_both_ways(k8, tab["kdec"]), _chunk(v_ref, n)), 0.0)
            return 0

        _chunk_loop(n_chunks, kv_body, 0)
        _scan_states(n_chunks, st, FWD_ROWS, tab["lam_f"], BWD_ROWS, tab["lam_b"])
        sink_col = _sink_column(sink_ref, g)

        def retention_chunk(n):
            q = _chunk(q_ref, n)
            k8 = (_chunk(k_ref, n).astype(F32) * Q_SCALE).astype(BF16)
            v = _chunk(v_ref, n)
            p2 = (_dot_nt(_stack_pair(is_a, q), k8) * tab["d2"]).astype(BF16)
            y = _unstack_pair(is_a, _dot_nn(p2, v))
            y = y + _dot_nn(_both_ways(q.astype(F32), tab["qdec"]), st[n].astype(BF16))
            rows = pl.ds(pl.multiple_of(n * CHUNK, CHUNK), CHUNK)
            mu = _group_sum(is_a, y) * (1.0 / HEAD_DIM)
            dlt = y - mu
            var = _group_sum(is_a, dlt * dlt) * (1.0 / HEAD_DIM)
            rstd = lax.rsqrt(var + GN_EPS)
            xhat = dlt * rstd
            xhat_ref[rows, :] = xhat
            rstd_ref[rows, :] = rstd
            gate = _chunk(g_ref, n).astype(F32)
            r_ref[rows, :] = (xhat * gain_ref[...] * gate * _sigmoid(gate)).astype(BF16)

        def attention_block(blk):
            n = half * blocks_half + blk
            rows = pl.ds(pl.multiple_of(blk * BLOCK, BLOCK), BLOCK)
            keys = pl.ds(pl.multiple_of(n * BLOCK, BLOCK), KEYS)
            p, _ = _attn_probs(_stack_heads(aq_ref[rows, :], g), kpad[keys, :], bias, sink_col, n, s)
            a_ref[rows, :] = _unstack_heads(_dot_nn(p.astype(BF16), vpad[keys, :]), g).astype(BF16)

        def trip(i, _):
            chunk, blk = 0, 0
            for kind in _trip_order(FWD_ORDER, RET_UNROLL, per_trip):
                if kind == "r":
                    retention_chunk(i * RET_UNROLL + chunk)
                    chunk += 1
                else:
                    attention_block(i * per_trip + blk)
                    blk += 1
            return 0

        lax.fori_loop(0, trips, trip, 0)

    lane_blk = lambda c0: _seq_spec(s, c0 // LANES)
    half_rows = blocks_half * BLOCK
    aq_spec = pl.BlockSpec((half_rows, GW), lambda b, h: (b * PAIRS_PER_KV + (h & 1), C_AQ // GW + h // 2))
    a_spec = pl.BlockSpec((half_rows, GW), lambda b, h: (b * PAIRS_PER_KV + (h & 1), h // 2))
    kv_spec = lambda c0: pl.BlockSpec((s, LANES), lambda b, h: (b, c0 // LANES))
    pad = pltpu.VMEM((s + 2 * BLOCK, LANES), BF16)
    body, lead = _after(after, body)
    return pl.pallas_call(
        body, name="mixers_fwd", grid=(b_loc, pairs),
        in_specs=lead + [_smem_spec(), _smem_spec(), _smem_spec(), lane_blk(C_RQ), lane_blk(C_RK), lane_blk(C_RV),
                         lane_blk(C_RG), pl.BlockSpec((1, LANES), lambda b, h: (0, h)), aq_spec, kv_spec(C_AK),
                         kv_spec(C_AV)],
        out_specs=[_seq_spec(s, 0), _seq_spec(s, 0), _seq_spec(s, 0), a_spec, _state_spec(n_chunks, pairs)],
        out_shape=[SDS((t, RET_W), BF16), SDS((t, RET_W), F32), SDS((t, RET_W), F32), SDS((t, ATT_W), BF16),
                   SDS((b_loc * pairs * n_chunks, 2 * LANES, LANES), F32)],
        scratch_shapes=[pad, pad, pltpu.VMEM((3, GROUP * BLOCK, KEYS), F32)],
        compiler_params=_params(("arbitrary", "arbitrary")),
    )(*after, lgf, lgb, sink, u, u, u, u, gn_gain, u, u, u)


def _mixers_bwd(u, xhat, rstd, states, dr, da, lgf, lgb, gn_gain, sink, b_loc, after=()):
    t = u.shape[0]
    s = t // b_loc
    n_chunks = s // CHUNK
    pairs = RET_HEADS // 2
    trips = n_chunks // RET_UNROLL
    blocks_half = (s // BLOCK) // PAIRS_PER_KV
    per_trip = blocks_half // trips
    assert n_chunks % RET_UNROLL == 0 and blocks_half % trips == 0 and PAIRS_PER_KV == 2 and s >= 2 * BLOCK

    def body(lgf_ref, lgb_ref, sink_ref, q_ref, k_ref, v_ref, g_ref, xhat_ref, rstd_ref, dr_ref, gain_ref,
             aq_ref, ak_ref, av_ref, do_ref, st,
             dq_ref, dk_ref, dv_ref, dg_ref, st_ref, daq_ref, dak_ref, dav_ref, dsink_ref,
             gr, dy_s, kpad, vpad, bias, dk_acc, dv_acc):
        pair = pl.program_id(1)
        g, half = lax.shift_right_logical(pair, 1), pair & 1
        tab = _stacked_tables(lgf_ref, lgb_ref, pair)
        is_a = tab["is_a"]
        gain = gain_ref[...]

        @pl.when(half == 0)
        def _():
            _attn_tables(g, bias)
            _mask_keys(ak_ref, g, Q_SCALE, kpad, s)
            _mask_keys(av_ref, g, 1.0, vpad, s)
            dsink_ref[...] = jnp.zeros_like(dsink_ref)

        @pl.when(pair == 0)
        def _():
            dk_acc[...] = jnp.zeros_like(dk_acc)
            dv_acc[...] = jnp.zeros_like(dv_acc)

        def norm_body(n, dgain):
            rows = pl.ds(pl.multiple_of(n * CHUNK, CHUNK), CHUNK)
            xhat, rstd = xhat_ref[rows, :], rstd_ref[rows, :]
            gate = g_ref[rows, :].astype(F32)
            sg = _sigmoid(gate)
            silu = gate * sg
            d_out = dr_ref[rows, :].astype(F32)
            dg_ref[rows, :] = (d_out * xhat * gain * (sg * (1.0 + gate * (1.0 - sg)))).astype(BF16)
            dxh = d_out * gain * silu
            m1 = _group_sum(is_a, dxh) * (1.0 / HEAD_DIM)
            m2 = _group_sum(is_a, dxh * xhat) * (1.0 / HEAD_DIM)
            dy = (rstd * (dxh - m1 - xhat * m2)).astype(BF16)
            dy_s[rows, :] = dy
            qf = q_ref[rows, :].astype(F32)
            gr[n] = jnp.where(tab["diag2"], _dot_tn(_both_ways(qf, tab["qdec"]), dy), 0.0)
            return dgain + jnp.sum(d_out * xhat * silu, axis=0, keepdims=True)

        colsum = lambda x: jnp.sum(x, axis=0, keepdims=True)

        def grad_body(n, carry):
            xfb, ifa, ifb, iba, ibb, lf, lb = carry
            rows = pl.ds(pl.multiple_of(n * CHUNK, CHUNK), CHUNK)
            q = q_ref[rows, :]
            qf = q.astype(F32)
            k8f = k_ref[rows, :].astype(F32) * Q_SCALE
            k8 = k8f.astype(BF16)
            v = v_ref[rows, :]
            dy = dy_s[rows, :]
            q2, dy2 = _stack_pair(is_a, q), _stack_pair(is_a, dy)
            sc = _dot_nt(q2, k8)
            dp = _dot_nt(dy2, v)
            a2 = (sc * tab["d2"]).astype(BF16)
            ds2 = (dp * tab["d2"]).astype(BF16)
            dq = _unstack_pair(is_a, _dot_nn(ds2, k8))
            dk = _dot_tn(ds2, q2)
            dv = _dot_tn(a2, dy2)
            prod = sc * dp
            pf, pb = prod * tab["df2"], prod * tab["db2"]
            ifa, ifb = ifa + colsum(pf[0:CHUNK, :]), ifb + colsum(pf[CHUNK:2 * CHUNK, :])
            iba, ibb = iba + colsum(pb[0:CHUNK, :]), ibb + colsum(pb[CHUNK:2 * CHUNK, :])
            states, sgrads = st[n], gr[n]
            sb, gb = states.astype(BF16), sgrads.astype(BF16)
            dqc = _dot_nt(dy, sb) * tab["qdec"]
            dkc = _dot_nt(v, gb) * tab["kdec"]
            dv = dv + _dot_nn(_both_ways(k8f, tab["kdec"]), gb)
            dq_ref[rows, :] = (dq + dqc[:, 0:LANES] + dqc[:, LANES:2 * LANES]).astype(BF16)
            dk_ref[rows, :] = ((dk + dkc[:, 0:LANES] + dkc[:, LANES:2 * LANES]) * Q_SCALE).astype(BF16)
            dv_ref[rows, :] = dv.astype(BF16)
            q2w, k2w = jnp.concatenate([qf, qf], axis=1), jnp.concatenate([k8f, k8f], axis=1)
            xfb = xfb + colsum(tab["qexp"] * q2w * dqc + tab["kexp"] * k2w * dkc)
            prod_s = sgrads * states
            lf, lb = lf + colsum(prod_s[0:LANES, :]), lb + colsum(prod_s[LANES:2 * LANES, :])
            return xfb, ifa, ifb, iba, ibb, lf, lb

        sink_col = _sink_column(sink_ref, g)
        head_row = lax.broadcasted_iota(jnp.int32, dsink_ref.shape, 0)

        def attention_block(blk):
            n = half * blocks_half + blk
            rows = pl.ds(pl.multiple_of(blk * BLOCK, BLOCK), BLOCK)
            keys = pl.ds(pl.multiple_of(n * BLOCK, BLOCK), KEYS)
            qm = _stack_heads(aq_ref[rows, :], g)
            k3, v3 = kpad[keys, :], vpad[keys, :]
            p, p_sink = _attn_probs(qm, k3, bias, sink_col, n, s)
            dom = _stack_heads(do_ref[rows, :], g)
            dp = _dot_nt(dom, v3)
            delta = jnp.sum(p * dp, axis=1, keepdims=True)
            ds_mat = (p * (dp - delta)).astype(BF16)
            daq_ref[rows, :] = _unstack_heads(_dot_nn(ds_mat, k3), g).astype(BF16)
            dk_acc[keys, :] += _dot_tn(ds_mat, qm) * Q_SCALE
            dv_acc[keys, :] += _dot_tn(p.astype(BF16), dom)
            w = p_sink * delta
            upd = jnp.zeros(dsink_ref.shape, F32)
            for h in range(GROUP):
                upd = upd + jnp.where(head_row == h, -jnp.sum(w[h * BLOCK:(h + 1) * BLOCK, :]), 0.0)
            dsink_ref[...] += upd

        dgain = _chunk_loop(n_chunks, norm_body, jnp.zeros((1, LANES), F32))
        _scan_states(n_chunks, gr, BWD_ROWS, tab["lam_b"], FWD_ROWS, tab["lam_f"])

        def trip(i, carry):
            chunk, blk = 0, 0
            for kind in _trip_order(BWD_ORDER, RET_UNROLL, per_trip):
                if kind == "r":
                    carry = grad_body(i * RET_UNROLL + chunk, carry)
                    chunk += 1
                else:
                    attention_block(i * per_trip + blk)
                    blk += 1
            return carry

        z = jnp.zeros((1, LANES), F32)
        init = (jnp.zeros((1, 2 * LANES), F32), z, z, z, z, z, z)
        xfb, ifa, ifb, iba, ibb, lf, lb = lax.fori_loop(0, trips, trip, init)
        st_ref[...] = jnp.zeros_like(st_ref)
        st_ref[ST_GAIN:ST_GAIN + 1, :] = dgain
        st_ref[ST_XF:ST_XF + 1, :] = xfb[:, 0:LANES]
        st_ref[ST_XB:ST_XB + 1, :] = xfb[:, LANES:2 * LANES]
        st_ref[ST_IFA:ST_IFA + 1, :] = ifa
        st_ref[ST_IFB:ST_IFB + 1, :] = ifb
        st_ref[ST_IBA:ST_IBA + 1, :] = iba
        st_ref[ST_IBB:ST_IBB + 1, :] = ibb
        st_ref[ST_LF:ST_LF + 1, :] = lf * (CHUNK * tab["lam_f"])
        st_ref[ST_LB:ST_LB + 1, :] = lb * (CHUNK * tab["lam_b"])

        @pl.when(pair == pairs - 1)
        def _():
            dak_ref[...] = dk_acc[BLOCK:BLOCK + s, :].astype(BF16)
            dav_ref[...] = dv_acc[BLOCK:BLOCK + s, :].astype(BF16)

    lane_blk = lambda c0: _seq_spec(s, c0 // LANES)
    seq0 = _seq_spec(s, 0)
    half_rows = blocks_half * BLOCK
    aq_spec = pl.BlockSpec((half_rows, GW), lambda b, h: (b * PAIRS_PER_KV + (h & 1), C_AQ // GW + h // 2))
    a_spec = pl.BlockSpec((half_rows, GW), lambda b, h: (b * PAIRS_PER_KV + (h & 1), h // 2))
    kv_spec = lambda c0: pl.BlockSpec((s, LANES), lambda b, h: (b, c0 // LANES))
    kv_out = pl.BlockSpec((s, LANES), lambda b, h: (b, 0))
    state = pltpu.VMEM((n_chunks, 2 * LANES, LANES), F32)
    pad = pltpu.VMEM((s + 2 * BLOCK, LANES), BF16)
    acc = pltpu.VMEM((s + 2 * BLOCK, LANES), F32)
    body, lead = _after(after, body)
    return pl.pallas_call(
        body, name="mixers_bwd", grid=(b_loc, pairs),
        in_specs=lead + [_smem_spec(), _smem_spec(), _smem_spec(), lane_blk(C_RQ), lane_blk(C_RK), lane_blk(C_RV),
                         lane_blk(C_RG), seq0, seq0, seq0, pl.BlockSpec((1, LANES), lambda b, h: (0, h)),
                         aq_spec, kv_spec(C_AK), kv_spec(C_AV), a_spec, _state_spec(n_chunks, pairs)],
        out_specs=[seq0] * 4 + [pl.BlockSpec((ST_ROWS, LANES), lambda b, h: (b, h)), a_spec, kv_out, kv_out,
                                pl.BlockSpec((8, LANES), lambda b, h: (b * KV_HEADS + h // 2, 0))],
        out_shape=[SDS((t, RET_W), BF16)] * 4 + [SDS((b_loc * ST_ROWS, RET_W), F32), SDS((t, ATT_W), BF16),
                                                   SDS((t, KV_W), BF16), SDS((t, KV_W), BF16),
                                                   SDS((b_loc * KV_HEADS * 8, LANES), F32)],
        scratch_shapes=[state, pltpu.VMEM((s, LANES), BF16), pad, pad,
                        pltpu.VMEM((3, GROUP * BLOCK, KEYS), F32), acc, acc],
        compiler_params=_params(("arbitrary", "arbitrary")),
    )(*after, lgf, lgb, sink, u, u, u, u, xhat, rstd, dr, gn_gain, u, u, u, da, states)


def _pack_small(acc2, acc1, ret_stats, dsink, b_loc, d):
    pairs = RET_HEADS // 2

    def body(acc2_ref, acc1_ref, st_ref, dsink_ref, out_ref):
        out_ref[...] = jnp.zeros_like(out_ref)
        out_ref[ROW_LN1G:ROW_LN1G + 1, :] = acc1_ref[0:1, :]
        out_ref[ROW_LN1B:ROW_LN1B + 1, :] = acc1_ref[1:2, :]
        out_ref[ROW_LN2G:ROW_LN2G + 1, :] = acc2_ref[1:2, :]
        out_ref[ROW_LN2B:ROW_LN2B + 1, :] = acc2_ref[2:3, :]
        out_ref[ROW_LOSS:ROW_LOSS + 1, :] = acc2_ref[0:1, :]
        st = st_ref[0:ST_ROWS, :]
        for b in range(1, b_loc):
            st = st + st_ref[b * ST_ROWS:(b + 1) * ST_ROWS, :]
        out_ref[ROW_GN:ROW_GN + 1, 0:RET_W] = st[ST_GAIN:ST_GAIN + 1, :]
        lane = lax.broadcasted_iota(jnp.int32, (1, d), 1)
        misc = jnp.zeros((1, d), F32)
        for pr in range(pairs):
            blk = st[:, pr * LANES:(pr + 1) * LANES]
            half = lax.broadcasted_iota(jnp.int32, (1, LANES), 1) < HEAD_DIM
            for h in range(2):
                sel = half if h == 0 else jnp.logical_not(half)
                cross_f = jnp.sum(jnp.where(sel, blk[ST_XF:ST_XF + 1, :] + blk[ST_LF:ST_LF + 1, :], 0.0))
                cross_b = jnp.sum(jnp.where(sel, blk[ST_XB:ST_XB + 1, :] + blk[ST_LB:ST_LB + 1, :], 0.0))
                intra_f = jnp.sum(blk[ST_IFA + h:ST_IFA + h + 1, :])
                intra_b = jnp.sum(blk[ST_IBA + h:ST_IBA + h + 1, :])
                head = 2 * pr + h
                misc = jnp.where(lane == MISC_DF + head, cross_f + intra_f, misc)
                misc = jnp.where(lane == MISC_DB + head, cross_b + intra_b, misc)
        for g in range(KV_HEADS):
            tot = dsink_ref[g * 8:(g + 1) * 8, :]
            for b in range(1, b_loc):
                tot = tot + dsink_ref[(b * KV_HEADS + g) * 8:(b * KV_HEADS + g + 1) * 8, :]
            for h in range(GROUP):
                misc = jnp.where(lane == MISC_SINK + GROUP * g + h, jnp.sum(tot[h:h + 1, 0:1]), misc)
        out_ref[ROW_MISC:ROW_MISC + 1, :] = misc

    return pl.pallas_call(body, name="pack_small", out_shape=SDS((SMALL_ROWS, d), F32))(acc2, acc1, ret_stats, dsink)


BIG = ("w_in", "w_out", "w_ffn_gate", "w_ffn_up", "w_ffn_down", "w_ple_proj", "w_ple_gate")
TRANSPOSED_OUTSIDE = ("w_in", "w_ffn_gate", "w_ffn_up")
TRANSPOSED_HERE = ("w_ple_proj",)
SMALL = ("ret_decay_fwd", "ret_decay_bwd", "ret_gn_gain", "attn_sink", "ln1_gain", "ln1_bias", "ln2_gain", "ln2_bias")
ORDER = ("w_in", "ret_decay_fwd", "ret_decay_bwd", "ret_gn_gain", "attn_sink", "w_out", "ln1_gain", "ln1_bias",
         "w_ffn_gate", "w_ffn_up", "w_ffn_down", "w_ple_proj", "w_ple_gate", "ln2_gain", "ln2_bias")


GATHER_ORDER = ("w_in", "w_ffn_up", "w_out", "w_ffn_gate", "w_ple_gate", "w_ple_proj", "w_ffn_down")
GATHER_TWO_LEVEL = ("w_in", "w_ffn_up")

def _local_step(x2, p2, target2, fetch, publish, small, b_loc, me):
    d = x2.shape[1]
    lgf, lgb = _log_decay(small["ret_decay_fwd"], small["ret_decay_bwd"])
    lgf1, lgb1, sink1 = lgf.reshape(-1), lgb.reshape(-1), small["attn_sink"].reshape(-1)
    (w_in,) = fetch(("w_in",), ())
    u, xb = _in_proj(x2, w_in)
    passed = fetch.pass_on("w_ffn_up", (xb,))
    r, ret_xhat, ret_rstd, a, ret_states = _mixers_fwd(u, lgf1, lgb1, small["ret_gn_gain"], sink1, b_loc, passed)
    w_out, w_gate, w_up = fetch(("w_out", "w_ffn_gate", "w_ffn_up"), (r, a))
    xhat1, rstd1, h1b, dact_dg, dact_du, act = _mix_ln1_ffn_up(
        r, a, x2, w_out, w_gate, w_up, small["ln1_gain"], small["ln1_bias"])
    w_pg, w_pe, w_down = fetch(("w_ple_gate", "w_ple_proj", "w_ffn_down"), (act,))
    dz2, dz2b, dsb, dpleb, dg, dup, acc2 = _ffn_down_ln2_loss(
        act, dact_dg, dact_du, h1b, p2, xhat1, target2, w_down, w_pg, w_pe,
        small["ln1_gain"], small["ln1_bias"], small["ln2_gain"], small["ln2_bias"])
    own = {}

    def grad(name, parts, rhs, after=()):
        whole, own[name] = _weight_grad("grad_" + name, me, parts, rhs, after)
        return whole

    ffn_jobs = dict(w_ffn_down=(act, dz2b), w_ple_proj=(dpleb, p2), w_ple_gate=(h1b, dsb),
                    w_ffn_gate=(dg, h1b), w_ffn_up=(dup, h1b))
    wholes, owns = _weight_grad_jobs("grad_w_ffn", me, list(ffn_jobs.values()))
    own.update(zip(ffn_jobs, owns))
    t2 = publish("ffn", dict(zip(ffn_jobs, wholes)))
    dz1, dz1b, dr, da, acc1 = _dh1_ln1_bwd(
        dz2, dg, dup, dsb, xhat1, rstd1, w_gate, w_up, w_pg, w_out, small["ln1_gain"], t2)
    t3 = publish("out", dict(w_out=grad("w_out", [r, a], dz1b)))
    dq, dk, dv, dgate, ret_stats, daq, dak, dav, dsink = _mixers_bwd(
        u, ret_xhat, ret_rstd, ret_states, dr, da, lgf1, lgb1, small["ret_gn_gain"], sink1, b_loc, t3)
    parts = [dq, dk, dv, dgate, daq, dak, dav]
    small_part = _pack_small(acc2, acc1, ret_stats, dsink, b_loc, d)
    t4 = publish("in", dict(w_in=grad("w_in", parts, xb)), small_part)
    grad_x = _in_proj_bwd(dz1, parts, w_in, t4)
    return grad_x, own, small_part


def kernel(x, p, w_in, ret_decay_fwd, ret_decay_bwd, ret_gn_gain, attn_sink, w_out, ln1_gain, ln1_bias, w_ffn_gate, w_ffn_up, w_ffn_down, w_ple_proj, w_ple_gate, ln2_gain, ln2_bias, loss_target, m_w_in, m_ret_decay_fwd, m_ret_decay_bwd, m_ret_gn_gain, m_attn_sink, m_w_out, m_ln1_gain, m_ln1_bias, m_w_ffn_gate, m_w_ffn_up, m_w_ffn_down, m_w_ple_proj, m_w_ple_gate, m_ln2_gain, m_ln2_bias, v_w_in, v_ret_decay_fwd, v_ret_decay_bwd, v_ret_gn_gain, v_attn_sink, v_w_out, v_ln1_gain, v_ln1_bias, v_w_ffn_gate, v_w_ffn_up, v_w_ffn_down, v_w_ple_proj, v_w_ple_gate, v_ln2_gain, v_ln2_bias):
    given = dict(locals())

    def strip(n, a):
        if n not in BIG:
            return a
        return a[0].T if n in TRANSPOSED_OUTSIDE else a[0]

    def restore(n, a):
        if n not in BIG:
            return a
        return (a.T if n in TRANSPOSED_OUTSIDE else a)[None]

    w = {n: strip(n, given[n]) for n in ORDER}
    m = {n: strip(n, given["m_" + n]) for n in ORDER}
    v = {n: strip(n, given["v_" + n]) for n in ORDER}
    b_loc, s, d = x.shape
    x2 = x.reshape(b_loc * s, d)
    p2 = p[0].reshape(b_loc * s, p.shape[-1])
    target2 = loss_target.reshape(b_loc * s, d)

    small = {n: w[n] for n in SMALL}
    me = (4 * lax.axis_index("x") + 2 * lax.axis_index("y") + lax.axis_index("c")).astype(jnp.int32).reshape(1)

    gather = _gather_start(
        {n: w[n] for n in GATHER_ORDER},
        [_gather_copy_near if n in GATHER_TWO_LEVEL else _gather_copy for n in GATHER_ORDER])

    passing = {}

    def pass_on(n, after):
        passing[n] = _gather_relay("gather_relay_" + n, gather, GATHER_ORDER.index(n), list(after))
        return (passing[n]["token"],)

    def fetch(names, after):
        out = {}
        for n in [n for n in names if n in GATHER_TWO_LEVEL]:
            if n not in passing:
                pass_on(n, after)
            out[n] = _split_copy_wait("gather_wait_" + n, passing[n], [0], list(after))[0][0]
        direct = [n for n in names if n not in GATHER_TWO_LEVEL]
        if direct:
            got = _split_copy_wait("gather_wait_" + direct[0], gather, [GATHER_ORDER.index(n) for n in direct],
                                   list(after))
            out.update({n: item[0] for n, item in zip(direct, got)})
        return [out[n] for n in names]

    scatters = []

    def publish(tag, products, small_sums=None):
        items = [(products[n], lax.empty((N_DEV - 1, products[n].shape[0] // N_DEV, products[n].shape[1]), BF16))
                 for n in products]
        copies = [_scatter_copy] * len(items)
        if small_sums is not None:
            items.append((small_sums, lax.empty((N_DEV - 1,) + small_sums.shape, F32)))
            copies.append(_small_copy)
        started = _split_copy_start("scatter_start_" + tag, items, copies)
        scatters.append((list(products), small_sums is not None, started))
        return (started["token"],)

    fetch.pass_on = pass_on
    grad_x, own, small_part = _local_step(x2, p2, target2, fetch, publish, small, b_loc, me)

    out_g, out_d, out_m, out_v = {}, {}, {}, {}
    after = [grad_x]
    for names, with_small, started in scatters:
        landed = _split_copy_wait("scatter_wait_" + names[0], started, list(range(len(started["items"]))), after)
        if with_small:
            mine, from_peers = landed[-1]
            loss, sg, sd, sm, sv = _small_adamw(
                me, mine, from_peers, small, {n: m[n] for n in SMALL}, {n: v[n] for n in SMALL})
            for dst, src in ((out_g, sg), (out_d, sd), (out_m, sm), (out_v, sv)):
                dst.update(src)
        recv = {n: item[1] for n, item in zip(names, landed)}
        alike = {}
        for n in names:
            alike.setdefault((own[n].shape, n in TRANSPOSED_HERE), []).append(n)
        for (_, transposed), ns in alike.items():
            res = _reduce_adamw(ns[0], [own[n] for n in ns], [recv[n] for n in ns], [w[n] for n in ns],
                                [m[n] for n in ns], [v[n] for n in ns], transposed)
            for dst, vals in zip((out_g, out_d, out_m, out_v), res):
                dst.update(zip(ns, vals))
        after = [out_v[names[-1]]]

    outs = [loss[0, 0], grad_x.reshape(x.shape)]
    for group in (out_g, out_d, out_m, out_v):
        outs += [restore(n, group[n]) for n in ORDER]
    return tuple(outs)
```

```python
import functools

import jax
import jax.numpy as jnp
from jax import lax
from jax.experimental import pallas as pl
from jax.experimental.pallas import tpu as pltpu

F32, BF16 = jnp.float32, jnp.bfloat16
SDS = jax.ShapeDtypeStruct
MESH = pl.DeviceIdType.MESH

N_DEV = 8
HEAD_DIM = 64
RET_HEADS = 8
ATTN_HEADS = 8
KV_HEADS = 2
GROUP = ATTN_HEADS // KV_HEADS
RET_W = RET_HEADS * HEAD_DIM
ATT_W = ATTN_HEADS * HEAD_DIM
KV_W = KV_HEADS * HEAD_DIM
LANES = 128
CHUNK = 128
BLOCK = 128
Q_SCALE = HEAD_DIM ** -0.5
ALPHA = 2.0 ** 0.25
LN_EPS = 1e-5
GN_EPS = 1e-5
NEG_INF = -1e30
C_RQ, C_RK, C_RV, C_RG = 0, RET_W, 2 * RET_W, 3 * RET_W
C_AQ = 4 * RET_W
C_AK = C_AQ + ATT_W
C_AV = C_AK + KV_W
IN_W = C_AV + KV_W

ADAM_LR = 0.001
ADAM_B1 = 0.9
ADAM_B2 = 0.999
ADAM_EPS = 1e-08
ADAM_WD = 0.01
ADAM_STEP = 10

VMEM_LIMIT = 56 * 1024 * 1024
MATMUL_ROWS = 512
EPILOGUE_ROWS = 256
SUB_ROWS = 512
SMALL_ROWS = 16
ROW_LN1G, ROW_LN1B, ROW_LN2G, ROW_LN2B, ROW_LOSS, ROW_GN, ROW_MISC = 0, 1, 2, 3, 4, 5, 6
MISC_DF, MISC_DB, MISC_SINK = 0, 8, 16


def _dot_nn(a, b):
    return lax.dot_general(a, b, (((1,), (0,)), ((), ())), preferred_element_type=F32)


def _dot_nt(a, b):
    return lax.dot_general(a, b, (((1,), (1,)), ((), ())), preferred_element_type=F32)


def _dot_tn(a, b):
    return lax.dot_general(a, b, (((0,), (0,)), ((), ())), preferred_element_type=F32)


def _params(sem=None, vmem=VMEM_LIMIT):
    kw = {"vmem_limit_bytes": vmem}
    if sem is not None:
        kw["dimension_semantics"] = sem
    return pltpu.CompilerParams(**kw)


def _row_tile(t, want=512):
    tm = want
    while t % tm:
        tm //= 2
    return tm


def _sigmoid(x):
    return jax.nn.sigmoid(x)


def _layer_norm_stats(z):
    mu = jnp.mean(z, axis=1, keepdims=True)
    d = z - mu
    var = jnp.mean(d * d, axis=1, keepdims=True)
    rstd = lax.rsqrt(var + LN_EPS)
    return d * rstd, rstd


def _layer_norm_bwd(dxh, xhat, rstd):
    m1 = jnp.mean(dxh, axis=1, keepdims=True)
    m2 = jnp.mean(dxh * xhat, axis=1, keepdims=True)
    return rstd * (dxh - m1 - xhat * m2)


def _mesh_pos():
    return lax.axis_index("x"), lax.axis_index("y"), lax.axis_index("c")


HBM_SPEC = pl.BlockSpec(memory_space=pltpu.HBM)
SEM_SPEC = pl.BlockSpec(memory_space=pltpu.SEMAPHORE)
ANY_SPEC = pl.BlockSpec(memory_space=pl.ANY)
SIDE_EFFECT = pltpu.SideEffectType.DATAFLOW_SIDE_EFFECTING
PEER_SEMS = pltpu.SemaphoreType.DMA((N_DEV - 1,))
ISSUE_ORDER = (4, 2, 6, 1, 5, 3, 7)


def _in_hbm(a):
    return pltpu.with_memory_space_constraint(a, pltpu.HBM)


def _split_copy_start(name, items, copies):
    n = len(items)
    flat = [a for it in items for a in it]
    k = len(flat)

    def body(*refs):
        arr, sems = list(refs[:k]), refs[k:k + 2 * n]
        for i, it in enumerate(items):
            mine = [arr.pop(0) for _ in it]
            for m in ISSUE_ORDER:
                cp = copies[i](m, mine, sems[i].at[m - 1], sems[n + i].at[m - 1])
                if cp is not None:
                    cp.start()
        token = refs[-1]
        token[...] = jnp.zeros_like(token)

    res = pl.pallas_call(
        body, name=name,
        out_shape=[PEER_SEMS] * (2 * n) + [pltpu.HBM(a.shape, a.dtype) for a in flat] + [SDS((8, LANES), F32)],
        in_specs=[HBM_SPEC] * k,
        out_specs=[SEM_SPEC] * (2 * n) + [HBM_SPEC] * k + [pl.BlockSpec(memory_space=pltpu.VMEM)],
        input_output_aliases={j: 2 * n + j for j in range(k)},
        compiler_params=pltpu.CompilerParams(has_side_effects=SIDE_EFFECT),
    )(*[_in_hbm(a) for a in flat])
    thru, out_items = list(res[2 * n:2 * n + k]), []
    for it in items:
        out_items.append(tuple(thru.pop(0) for _ in it))
    return dict(send=res[:n], recv=res[n:2 * n], items=out_items, token=res[-1], copies=copies)


def _gather_start(shards, copies):
    names = list(shards)
    n = len(names)
    flip = [name in TRANSPOSED_HERE for name in names]
    shapes = [shards[name].shape[::-1] if f else shards[name].shape for name, f in zip(names, flip)]
    most = (max(s[0] for s in shapes), max(s[1] for s in shapes))

    def body(*refs):
        src, sems, land, token = refs[:n], refs[n:3 * n], refs[3 * n:4 * n], refs[4 * n]
        wide, narrow, sem = refs[4 * n + 1:]
        for i, (rows, cols) in enumerate(shapes):
            raw = wide.at[0:cols, 0:rows] if flip[i] else wide.at[0:rows, 0:cols]
            bring = pltpu.make_async_copy(src[i], raw, sem.at[0])
            bring.start()
            bring.wait()
            narrow[0:rows, 0:cols] = (raw[...].T if flip[i] else raw[...]).astype(BF16)
            mine = land[i].at[pl.ds(pl.multiple_of(_peer_index(0) * rows, 8), rows), :]
            place = pltpu.make_async_copy(narrow.at[0:rows, 0:cols], mine, sem.at[0])
            place.start()
            place.wait()
            for m in ISSUE_ORDER:
                cp = copies[i](m, [land[i]], sems[i].at[m - 1], sems[n + i].at[m - 1])
                if cp is not None:
                    cp.start()
        token[...] = jnp.zeros_like(token)

    side = max(most)
    res = pl.pallas_call(
        body, name="gather_start",
        out_shape=[PEER_SEMS] * (2 * n) + [pltpu.HBM((N_DEV * r, c), BF16) for r, c in shapes] + [SDS((8, LANES), F32)],
        in_specs=[HBM_SPEC] * n,
        out_specs=[SEM_SPEC] * (2 * n) + [HBM_SPEC] * n + [pl.BlockSpec(memory_space=pltpu.VMEM)],
        scratch_shapes=[pltpu.VMEM((side, side), F32), pltpu.VMEM(most, BF16), pltpu.SemaphoreType.DMA((1,))],
        compiler_params=pltpu.CompilerParams(has_side_effects=SIDE_EFFECT),
    )(*[_in_hbm(shards[name]) for name in names])
    return dict(send=res[:n], recv=res[n:2 * n], items=[(a,) for a in res[2 * n:3 * n]], token=res[-1], copies=copies)


def _gather_relay(name, started, which, after):
    (land,) = started["items"][which]

    def body(*refs):
        land_ref, old_send, old_recv = refs[0], refs[1], refs[2]
        send, recv, token = refs[3 + len(after)], refs[4 + len(after)], refs[-1]
        for m in ISSUE_ORDER:
            cp = _gather_copy_near(m, [land_ref], old_send.at[m - 1], old_recv.at[m - 1])
            if cp is None:
                continue
            cp.wait_send()
            cp.wait_recv()
            if m > 1:
                _gather_copy_pass(m + 1, [land_ref], send.at[m], recv.at[m]).start()
        token[...] = jnp.zeros_like(token)

    res = pl.pallas_call(
        body, name=name,
        out_shape=[PEER_SEMS, PEER_SEMS, pltpu.HBM(land.shape, land.dtype), SDS((8, LANES), F32)],
        in_specs=[HBM_SPEC, SEM_SPEC, SEM_SPEC] + [ANY_SPEC] * len(after),
        out_specs=[SEM_SPEC, SEM_SPEC, HBM_SPEC, pl.BlockSpec(memory_space=pltpu.VMEM)],
        input_output_aliases={0: 2},
        compiler_params=pltpu.CompilerParams(has_side_effects=SIDE_EFFECT),
    )(land, started["send"][which], started["recv"][which], *[_in_hbm(a) for a in after])
    return dict(send=[res[0]], recv=[res[1]], items=[(res[2],)], token=res[3], copies=[_gather_copy_pass])


def _split_copy_wait(name, started, which, after):
    items = [started["items"][i] for i in which]
    copies = [started["copies"][i] for i in which]
    n = len(items)
    flat = [a for it in items for a in it]
    k = len(flat)

    def body(*refs):
        arr, sems = list(refs[:k]), refs[k:k + 2 * n]
        for i, it in enumerate(items):
            mine = [arr.pop(0) for _ in it]
            for m in range(1, N_DEV):
                cp = copies[i](m, mine, sems[i].at[m - 1], sems[n + i].at[m - 1])
                if cp is not None:
                    cp.wait_send()
                    cp.wait_recv()

    res = pl.pallas_call(
        body, name=name,
        out_shape=[pltpu.HBM(a.shape, a.dtype) for a in flat],
        in_specs=[HBM_SPEC] * k + [SEM_SPEC] * (2 * n) + [ANY_SPEC] * len(after),
        out_specs=[HBM_SPEC] * k,
        input_output_aliases={j: j for j in range(k)},
        compiler_params=pltpu.CompilerParams(has_side_effects=SIDE_EFFECT),
    )(*flat, *[started["send"][i] for i in which], *[started["recv"][i] for i in which], *[_in_hbm(a) for a in after])
    thru, out_items = list(res), []
    for it in items:
        out_items.append(tuple(thru.pop(0) for _ in it))
    return out_items


def _gather_copy(m, refs, send_sem, recv_sem):
    (land_ref,) = refs
    r = land_ref.shape[0] // N_DEV
    mine = land_ref.at[pl.ds(pl.multiple_of(_peer_index(0) * r, 8), r), :]
    return pltpu.make_async_remote_copy(src_ref=mine, dst_ref=mine, send_sem=send_sem, recv_sem=recv_sem,
                                        device_id=_peer(m), device_id_type=MESH)


def _gather_copy_near(m, refs, send_sem, recv_sem):
    return _gather_copy(m, refs, send_sem, recv_sem) if m == 1 or m % 2 == 0 else None


def _gather_copy_pass(m, refs, send_sem, recv_sem):
    if m == 1 or m % 2 == 0:
        return None
    (land_ref,) = refs
    r = land_ref.shape[0] // N_DEV
    block = land_ref.at[pl.ds(pl.multiple_of(_peer_index(m ^ 1) * r, 8), r), :]
    return pltpu.make_async_remote_copy(src_ref=block, dst_ref=block, send_sem=send_sem, recv_sem=recv_sem,
                                        device_id=_peer(1), device_id_type=MESH)


def _small_copy(m, refs, send_sem, recv_sem):
    part_ref, land_ref = refs
    return pltpu.make_async_remote_copy(src_ref=part_ref, dst_ref=land_ref.at[m - 1], send_sem=send_sem,
                                        recv_sem=recv_sem, device_id=_peer(m), device_id_type=MESH)


def _scatter_copy(m, refs, send_sem, recv_sem):
    buf_ref, land_ref = refs
    r = buf_ref.shape[0] // N_DEV
    src = buf_ref.at[pl.ds(pl.multiple_of(_peer_index(m) * r, 8), r), :]
    return pltpu.make_async_remote_copy(src_ref=src, dst_ref=land_ref.at[m - 1], send_sem=send_sem,
                                        recv_sem=recv_sem, device_id=_peer(m), device_id_type=MESH)


def _peer(m):
    x, y, c = _mesh_pos()
    bx, by, bc = (m >> 2) & 1, (m >> 1) & 1, m & 1
    return (x ^ bx if bx else x, y ^ by if by else y, c ^ bc if bc else c)


def _peer_index(m):
    x, y, c = _mesh_pos()
    return (4 * x + 2 * y + c) ^ m


SMALL_PLACE = {
    "ln1_gain": (ROW_LN1G, 0), "ln1_bias": (ROW_LN1B, 0), "ln2_gain": (ROW_LN2G, 0), "ln2_bias": (ROW_LN2B, 0),
    "ret_gn_gain": (ROW_GN, 0), "ret_decay_fwd": (ROW_MISC, MISC_DF), "ret_decay_bwd": (ROW_MISC, MISC_DB),
    "attn_sink": (ROW_MISC, MISC_SINK)}


def _small_adamw(me, part, landed, w, m, v):
    d = part.shape[1]
    names = list(SMALL_PLACE)
    k = len(names)

    def body(*refs):
        me_ref, part_ref, land_ref = refs[:3]
        refs = refs[2:]
        w_refs, m_refs, v_refs = refs[1:1 + k], refs[1 + k:1 + 2 * k], refs[1 + 2 * k:1 + 3 * k]
        outs = refs[1 + 3 * k:1 + 7 * k + 1]
        tot_ref = refs[-1]
        loss_ref, g_refs, dl_refs = outs[0], outs[1:1 + k], outs[1 + k:1 + 2 * k]
        nm_refs, nv_refs = outs[1 + 2 * k:1 + 3 * k], outs[1 + 3 * k:1 + 4 * k]
        tot = jnp.zeros(part_ref.shape, F32)
        for dev in range(N_DEV):
            j = dev ^ me_ref[0]
            tot = tot + jnp.where(j == 0, part_ref[...], land_ref[jnp.maximum(j, 1) - 1])
        tot_ref[...] = tot
        loss_ref[...] = (0.5 / d) * jnp.sum(tot_ref[ROW_LOSS:ROW_LOSS + 1, :], axis=1, keepdims=True)
        for i, name in enumerate(names):
            row, lo = SMALL_PLACE[name]
            wv = w_refs[i][...]
            g = tot_ref[row:row + 1, lo:lo + wv.shape[1]]
            if name.startswith("ret_decay"):
                p2 = jnp.exp2(wv)
                g = g * (-p2 * jnp.log(2.0) / (1.0 - p2))
            g_refs[i][...] = g
            _adamw_store(g, wv, m_refs[i][...], v_refs[i][...], dl_refs[i], nm_refs[i], nv_refs[i])

    shapes = [SDS(w[n].shape, F32) for n in names]
    vm = pl.BlockSpec(memory_space=pltpu.VMEM)
    res = pl.pallas_call(
        body, name="small_adamw", out_shape=[SDS((1, 1), F32)] + shapes * 4,
        in_specs=[_smem_spec()] + [vm] * (2 + 3 * k), out_specs=[vm] * (1 + 4 * k),
        scratch_shapes=[pltpu.VMEM(part.shape, F32)],
    )(me, part, landed, *[w[n] for n in names], *[m[n] for n in names], *[v[n] for n in names])
    groups = [dict(zip(names, res[1 + j * k:1 + (j + 1) * k])) for j in range(4)]
    return (res[0], *groups)


def _adamw_store(g, w, m, v, dl_ref, nm_ref, nv_ref):
    m = ADAM_B1 * m + (1.0 - ADAM_B1) * g
    v = ADAM_B2 * v + (1.0 - ADAM_B2) * (g * g)
    m_hat = m / (1.0 - ADAM_B1 ** ADAM_STEP)
    v_hat = v / (1.0 - ADAM_B2 ** ADAM_STEP)
    dl_ref[...] = -ADAM_LR * (m_hat / (jnp.sqrt(v_hat) + ADAM_EPS) + ADAM_WD * w)
    nm_ref[...] = m
    nv_ref[...] = v


def _reduce_adamw(name, owns, recvs, ws, ms, vs, transposed):
    count = len(owns)
    rows, n = owns[0].shape
    steps = 1 if transposed or rows % 16 else 2
    rb = rows // steps

    def body(*refs):
        ins, outs = refs[:5 * count], refs[5 * count:]
        j = pl.program_id(0)
        for k in range(count):
            @pl.when(j == k)
            def _(k=k):
                own_ref, recv_ref, w_ref, m_ref, v_ref = ins[5 * k:5 * k + 5]
                g_ref, dl_ref, nm_ref, nv_ref = outs[4 * k:4 * k + 4]
                g = own_ref[...]
                for p in range(recv_ref.shape[0]):
                    g = g + recv_ref[p].astype(F32)
                if transposed:
                    g = g.T
                g_ref[...] = g
                _adamw_store(g, w_ref[...], m_ref[...], v_ref[...], dl_ref, nm_ref, nv_ref)

    def turn(k):
        return lambda j, i: jnp.where(j == k, i, jnp.where(j < k, 0, steps - 1))

    in_specs, out_specs = [], []
    for k in range(count):
        at = turn(k)
        blk = pl.BlockSpec(ws[0].shape if transposed else (rb, n), lambda j, i, at=at: (at(j, i), 0))
        in_specs += [pl.BlockSpec((rb, n), lambda j, i, at=at: (at(j, i), 0)),
                     pl.BlockSpec((recvs[k].shape[0], rb, n), lambda j, i, at=at: (0, at(j, i), 0)), blk, blk, blk]
        out_specs += [blk] * 4
    res = pl.pallas_call(
        body, name="adamw_" + name, grid=(count, steps), in_specs=in_specs, out_specs=out_specs,
        out_shape=[SDS(ws[0].shape, F32)] * (4 * count), compiler_params=_params(("arbitrary", "arbitrary")),
    )(*[a for k in range(count) for a in (owns[k], recvs[k], ws[k], ms[k], vs[k])])
    return [list(res[j::4]) for j in range(4)]


def _row_spec(tm, width):
    return pl.BlockSpec((tm, width), lambda i: (i, 0))


def _full_spec(shape):
    return pl.BlockSpec(shape, lambda i: (0,) * len(shape))


_acc_spec = _full_spec


def _sub_rows(tm):
    step = min(SUB_ROWS, tm)
    return [(lo, lo + step) for lo in range(0, tm, step)]


def _in_proj(x2, wt_in):
    t, d = x2.shape
    u_w = wt_in.shape[0]
    tm = _row_tile(t, MATMUL_ROWS)

    def body(x_ref, w_ref, u_ref, xb_ref):
        xb = x_ref[...].astype(BF16)
        xb_ref[...] = xb
        u_ref[...] = _dot_nt(xb, w_ref[...]).astype(BF16)

    return pl.pallas_call(
        body, name="in_proj", grid=(t // tm,),
        in_specs=[_row_spec(tm, d), _full_spec(wt_in.shape)],
        out_specs=[_row_spec(tm, u_w), _row_spec(tm, d)],
        out_shape=[SDS((t, u_w), BF16), SDS((t, d), BF16)],
        compiler_params=_params(("parallel",)),
    )(x2, wt_in)


def _col_halves(f):
    n = f // LANES
    k = (n + 1) // 2 * LANES
    return [(0, k), (k, f)] if k < f else [(0, f)]


def _mix_ln1_ffn_up(r, a, x2, w_out, wt_gate, wt_up, g1, b1):
    t, d = x2.shape
    f = wt_gate.shape[0]
    tm = _row_tile(t, EPILOGUE_ROWS)

    def body(r_ref, a_ref, x_ref, wo_ref, wg_ref, wu_ref, g_ref, b_ref, xh_ref, rs_ref, hb_ref, dg_ref, du_ref,
             act_ref):
        mix = _dot_nn(r_ref[...], wo_ref[0:RET_W, :]) + _dot_nn(a_ref[...], wo_ref[RET_W:RET_W + ATT_W, :])
        z = ALPHA * x_ref[...] + mix
        xhat, rstd = _layer_norm_stats(z)
        xh_ref[...] = xhat
        rs_ref[...] = jnp.broadcast_to(rstd, rs_ref.shape)
        h = (xhat * g_ref[...] + b_ref[...]).astype(BF16)
        hb_ref[...] = h
        g = _dot_nt(h, wg_ref[...])
        u = _dot_nt(h, wu_ref[...])
        sg = _sigmoid(g)
        silu = g * sg
        dg_ref[...] = (u * (sg * (1.0 + g * (1.0 - sg)))).astype(BF16)
        du_ref[...] = silu.astype(BF16)
        act_ref[...] = (silu * u).astype(BF16)

    wide, narrow = _row_spec(tm, f), _row_spec(tm, d)
    return pl.pallas_call(
        body, name="mix_ln1_ffn_up", grid=(t // tm,),
        in_specs=[_row_spec(tm, RET_W), _row_spec(tm, ATT_W), narrow, _resident_spec(w_out.shape),
                  _resident_spec(wt_gate.shape), _resident_spec(wt_up.shape), _full_spec(g1.shape),
                  _full_spec(b1.shape)],
        out_specs=[narrow, _row_spec(tm, LANES), narrow, wide, wide, wide],
        out_shape=[SDS((t, d), F32), SDS((t, LANES), F32), SDS((t, d), BF16)] + [SDS((t, f), BF16)] * 3,
        compiler_params=_params(("parallel",)),
    )(r, a, x2, w_out, wt_gate, wt_up, g1, b1)


def _ffn_down_ln2_loss(act, dact_dg, dact_du, h1b, p2, xhat1, target, w_down, w_pg, wt_pe, g1, b1, g2, b2):
    t, d = xhat1.shape
    f = act.shape[1]
    pdim = p2.shape[1]
    tm = _row_tile(t, EPILOGUE_ROWS)

    def body(act_ref, fg_ref, fu_ref, hb_ref, p_ref, xh1_ref, tgt_ref, wd_ref, wpg_ref, wpe_ref, g1_ref, b1_ref,
             g2_ref, b2_ref, dz_ref, dzb_ref, ds_ref, dple_ref, dg_ref, du_ref, acc_ref):
        @pl.when(pl.program_id(0) == 0)
        def _():
            acc_ref[...] = jnp.zeros_like(acc_ref)

        for lo, hi in _sub_rows(tm):
            h1 = xh1_ref[lo:hi, :] * g1_ref[...] + b1_ref[...]
            pg = _sigmoid(_dot_nn(hb_ref[lo:hi, :], wpg_ref[...]))
            ple = _dot_nt(p_ref[lo:hi, :].astype(BF16), wpe_ref[...])
            gated = pg * ple
            dgate = gated * (1.0 - pg)
            ffn = _dot_nn(act_ref[lo:hi, :], wd_ref[...])
            z2 = ALPHA * h1 + gated + ffn
            xhat2, rstd2 = _layer_norm_stats(z2)
            err = xhat2 * g2_ref[...] + b2_ref[...] - tgt_ref[lo:hi, :]
            dy = err * (1.0 / d)
            dz = _layer_norm_bwd(dy * g2_ref[...], xhat2, rstd2)
            dzb = dz.astype(BF16)
            dz_ref[lo:hi, :] = dz
            dzb_ref[lo:hi, :] = dzb
            ds_ref[lo:hi, :] = (dz * dgate).astype(BF16)
            dple_ref[lo:hi, :] = (dz * pg).astype(BF16)
            acc_ref[0:1, :] += jnp.sum(err * err, axis=0, keepdims=True)
            acc_ref[1:2, :] += jnp.sum(dy * xhat2, axis=0, keepdims=True)
            acc_ref[2:3, :] += jnp.sum(dy, axis=0, keepdims=True)
            for c0, c1 in _col_halves(f):
                da = _dot_nt(dzb, wd_ref[c0:c1, :])
                dg_ref[lo:hi, c0:c1] = (da * fg_ref[lo:hi, c0:c1].astype(F32)).astype(BF16)
                du_ref[lo:hi, c0:c1] = (da * fu_ref[lo:hi, c0:c1].astype(F32)).astype(BF16)

    vec = _full_spec(g1.shape)
    wide, narrow = _row_spec(tm, f), _row_spec(tm, d)
    return pl.pallas_call(
        body, name="ffn_down_ln2_loss", grid=(t // tm,),
        in_specs=[wide, wide, wide, narrow, _row_spec(tm, pdim), narrow, narrow,
                  _full_spec(w_down.shape), _full_spec(w_pg.shape), _full_spec(wt_pe.shape), vec, vec, vec, vec],
        out_specs=[narrow] * 4 + [wide, wide, _acc_spec((8, d))],
        out_shape=[SDS((t, d), F32), SDS((t, d), BF16), SDS((t, d), BF16), SDS((t, d), BF16),
                   SDS((t, f), BF16), SDS((t, f), BF16), SDS((8, d), F32)],
        compiler_params=_params(("arbitrary",)),
    )(act, dact_dg, dact_du, h1b, p2, xhat1, target, w_down, w_pg, wt_pe, g1, b1, g2, b2)


def _after(after, body):
    k = len(after)
    return (lambda *refs: body(*refs[k:])), [ANY_SPEC] * k


def _resident_spec(shape):
    return pl.BlockSpec(shape, lambda i: (0,) * len(shape), pipeline_mode=pl.Buffered(1))


def _dh1_ln1_bwd(dz2, dg, dup, dsb, xhat1, rstd1, wt_gate, wt_up, w_pg, w_out, g1, after=()):
    t, d = dz2.shape
    f = dg.shape[1]
    tm = _row_tile(t, MATMUL_ROWS)

    def body(dz_ref, dg_ref, du_ref, ds_ref, xh1_ref, rs1_ref, wg_ref, wu_ref, wpg_ref, wo_ref, g1_ref,
             dz1_ref, dz1b_ref, dr_ref, da_ref, acc_ref):
        @pl.when(pl.program_id(0) == 0)
        def _():
            acc_ref[...] = jnp.zeros_like(acc_ref)

        for lo, hi in _sub_rows(tm):
            dh = (ALPHA * dz_ref[lo:hi, :] + _dot_nn(dg_ref[lo:hi, :], wg_ref[...])
                  + _dot_nn(du_ref[lo:hi, :], wu_ref[...]) + _dot_nt(ds_ref[lo:hi, :], wpg_ref[...]))
            xhat, rstd = xh1_ref[lo:hi, :], rs1_ref[lo:hi, 0:1]
            dz1 = _layer_norm_bwd(dh * g1_ref[...], xhat, rstd)
            dz1b = dz1.astype(BF16)
            dz1_ref[lo:hi, :] = dz1
            dz1b_ref[lo:hi, :] = dz1b
            acc_ref[0:1, :] += jnp.sum(dh * xhat, axis=0, keepdims=True)
            acc_ref[1:2, :] += jnp.sum(dh, axis=0, keepdims=True)
            dr_ref[lo:hi, :] = _dot_nt(dz1b, wo_ref[0:RET_W, :]).astype(BF16)
            da_ref[lo:hi, :] = _dot_nt(dz1b, wo_ref[RET_W:RET_W + ATT_W, :]).astype(BF16)

    body, lead = _after(after, body)
    return pl.pallas_call(
        body, name="dh1_ln1_bwd", grid=(t // tm,),
        in_specs=lead + [_row_spec(tm, d), _row_spec(tm, f), _row_spec(tm, f), _row_spec(tm, d), _row_spec(tm, d),
                         _row_spec(tm, LANES), _resident_spec(wt_gate.shape), _resident_spec(wt_up.shape), _resident_spec(w_pg.shape),
                         _resident_spec(w_out.shape), _full_spec(g1.shape)],
        out_specs=[_row_spec(tm, d), _row_spec(tm, d), _row_spec(tm, RET_W), _row_spec(tm, ATT_W), _acc_spec((8, d))],
        out_shape=[SDS((t, d), F32), SDS((t, d), BF16), SDS((t, RET_W), BF16), SDS((t, ATT_W), BF16),
                   SDS((8, d), F32)],
        compiler_params=_params(("arbitrary",)),
    )(*after, dz2, dg, dup, dsb, xhat1, rstd1, wt_gate, wt_up, w_pg, w_out, g1)


def _in_proj_bwd(dz1, parts, wt_in, after=()):
    t, d = dz1.shape
    tm = _row_tile(t, MATMUL_ROWS)
    widths = [p.shape[1] for p in parts]

    def body(*refs):
        dz_ref, part_refs, w_ref, dx_ref = refs[0], refs[1:1 + len(parts)], refs[-2], refs[-1]
        acc = ALPHA * dz_ref[...]
        lo = 0
        for p_ref, w in zip(part_refs, widths):
            acc = acc + _dot_nn(p_ref[...], w_ref[lo:lo + w, :])
            lo += w
        dx_ref[...] = acc

    body, lead = _after(after, body)
    return pl.pallas_call(
        body, name="in_proj_bwd", grid=(t // tm,),
        in_specs=lead + [_row_spec(tm, d)] + [_row_spec(tm, w) for w in widths] + [_full_spec(wt_in.shape)],
        out_specs=_row_spec(tm, d), out_shape=SDS((t, d), F32),
        compiler_params=_params(("parallel",)),
    )(*after, dz1, *parts, wt_in)


def _weight_grad(name, me, parts, rhs, after=()):
    t, n = rhs.shape
    widths = [p.shape[1] for p in parts]
    rows = sum(widths)
    own_rows = rows // N_DEV
    tk = _row_tile(t, MATMUL_ROWS)
    n_steps = t // tk
    step = 256

    def body(*refs):
        me_ref, part_refs, rhs_ref = refs[0], refs[1:1 + len(parts)], refs[1 + len(parts)]
        full_ref, own_ref, acc = refs[-3], refs[-2], refs[-1]
        i = pl.program_id(0)

        def products(first):
            b = rhs_ref[...].astype(BF16)
            lo = 0
            for p_ref, w in zip(part_refs, widths):
                for c0 in range(0, w, step):
                    c1 = min(c0 + step, w)
                    val = _dot_tn(p_ref[:, c0:c1].astype(BF16), b)
                    if first:
                        acc[lo + c0:lo + c1, :] = val
                    else:
                        acc[lo + c0:lo + c1, :] += val
                lo += w

        pl.when(i == 0)(functools.partial(products, True))
        pl.when(i > 0)(functools.partial(products, False))

        @pl.when(i == n_steps - 1)
        def _():
            full_ref[...] = acc[...].astype(BF16)
            own_ref[...] = acc[pl.ds(pl.multiple_of(me_ref[0] * own_rows, 8), own_rows), :]

    body, lead = _after(after, body)
    return pl.pallas_call(
        body, name=name, grid=(n_steps,),
        in_specs=lead + [_smem_spec()] + [_row_spec(tk, w) for w in widths] + [_row_spec(tk, n)],
        out_specs=[_full_spec((rows, n)), _full_spec((own_rows, n))],
        out_shape=[SDS((rows, n), BF16), SDS((own_rows, n), F32)],
        scratch_shapes=[pltpu.VMEM((rows, n), F32)],
        compiler_params=_params(("arbitrary",)),
    )(*after, me, *parts, rhs)


def _weight_grad_jobs(name, me, jobs, after=()):
    count = len(jobs)
    t = jobs[0][0].shape[0]
    tk = _row_tile(t, MATMUL_ROWS)
    n_steps = t // tk
    shapes = [(lhs.shape[1], rhs.shape[1]) for lhs, rhs in jobs]
    most_rows, most_cols = max(r for r, _ in shapes), max(n for _, n in shapes)
    step = 256

    def body(*refs):
        me_ref, lhs_refs, rhs_refs = refs[0], refs[1:1 + count], refs[1 + count:1 + 2 * count]
        full_refs, own_refs = refs[1 + 2 * count:1 + 3 * count], refs[1 + 3 * count:1 + 4 * count]
        acc, whole, mine, sems = refs[1 + 4 * count:]
        job, i = pl.program_id(0), pl.program_id(1)

        def leaving(j):
            rows, n = shapes[j]
            return (pltpu.make_async_copy(whole.at[0:rows, 0:n], full_refs[j], sems.at[0]),
                    pltpu.make_async_copy(mine.at[0:rows // N_DEV, 0:n], own_refs[j], sems.at[1]))

        def products(j, first):
            rows, n = shapes[j]
            b = rhs_refs[j][...].astype(BF16)
            for c0 in range(0, rows, step):
                c1 = min(c0 + step, rows)
                val = _dot_tn(lhs_refs[j][:, c0:c1].astype(BF16), b)
                if first:
                    acc[c0:c1, 0:n] = val
                else:
                    acc[c0:c1, 0:n] += val

        def finish(j):
            rows, n = shapes[j]
            own_rows = rows // N_DEV
            if j > 0:
                for cp in leaving(j - 1):
                    cp.wait()
            whole[0:rows, 0:n] = acc[0:rows, 0:n].astype(BF16)
            mine[0:own_rows, 0:n] = acc[pl.ds(pl.multiple_of(me_ref[0] * own_rows, 8), own_rows), 0:n]
            for cp in leaving(j):
                cp.start()
            if j == count - 1:
                for cp in leaving(j):
                    cp.wait()

        for j in range(count):
            pl.when((job == j) & (i == 0))(functools.partial(products, j, True))
            pl.when((job == j) & (i > 0))(functools.partial(products, j, False))
            pl.when((job == j) & (i == n_steps - 1))(functools.partial(finish, j))

    def turn(j):
        return lambda job, i: (jnp.where(job == j, i, jnp.where(job < j, 0, n_steps - 1)), 0)

    body, lead = _after(after, body)
    res = pl.pallas_call(
        body, name=name, grid=(count, n_steps),
        in_specs=lead + [_smem_spec()] + [pl.BlockSpec((tk, rows), turn(j)) for j, (rows, _) in enumerate(shapes)]
        + [pl.BlockSpec((tk, n), turn(j)) for j, (_, n) in enumerate(shapes)],
        out_specs=[ANY_SPEC] * (2 * count),
        out_shape=[SDS((rows, n), BF16) for rows, n in shapes] + [SDS((rows // N_DEV, n), F32) for rows, n in shapes],
        scratch_shapes=[pltpu.VMEM((most_rows, most_cols), F32), pltpu.VMEM((most_rows, most_cols), BF16),
                        pltpu.VMEM((most_rows // N_DEV, most_cols), F32), pltpu.SemaphoreType.DMA((2,))],
        compiler_params=_params(("arbitrary", "arbitrary")),
    )(*after, me, *[lhs for lhs, _ in jobs], *[rhs for _, rhs in jobs])
    return list(res[:count]), list(res[count:])


def _log_decay(decay_f, decay_b):
    def body(f_ref, b_ref, lf_ref, lb_ref):
        lf_ref[...] = jnp.log1p(-jnp.exp2(f_ref[...]))
        lb_ref[...] = jnp.log1p(-jnp.exp2(b_ref[...]))

    return pl.pallas_call(body, name="log_decay", out_shape=[SDS(decay_f.shape, F32)] * 2)(decay_f, decay_b)


def _chunk(ref, n):
    return ref[pl.ds(pl.multiple_of(n * CHUNK, CHUNK), CHUNK), :]


def _group_sum(is_a, v):
    sa = jnp.sum(jnp.where(is_a, v, 0.0), axis=1, keepdims=True)
    sb = jnp.sum(jnp.where(is_a, 0.0, v), axis=1, keepdims=True)
    return jnp.where(is_a, sa, sb)


def _seq_spec(s, col_block):
    return pl.BlockSpec((s, LANES), lambda b, h: (b, col_block + h))


def _smem_spec():
    return pl.BlockSpec(memory_space=pltpu.SMEM)


RET_UNROLL = 4


def _chunk_loop(n_chunks, body, init):
    u = RET_UNROLL if n_chunks % RET_UNROLL == 0 else 1

    def trip(i, carry):
        for j in range(u):
            carry = body(i * u + j, carry)
        return carry

    return lax.fori_loop(0, n_chunks // u, trip, init)


def _stacked_tables(lgf_ref, lgb_ref, pair):
    lane = lax.broadcasted_iota(jnp.int32, (1, LANES), 1)
    is_a = lane < HEAD_DIM
    lgf = jnp.where(is_a, lgf_ref[2 * pair], lgf_ref[2 * pair + 1])
    lgb = jnp.where(is_a, lgb_ref[2 * pair], lgb_ref[2 * pair + 1])
    row = lax.broadcasted_iota(jnp.int32, (CHUNK, 1), 0).astype(F32)
    kdec_f, qdec_f = jnp.exp(lgf * (CHUNK - 1.0 - row)), jnp.exp(lgf * (row + 1.0))
    kdec_b, qdec_b = jnp.exp(lgb * row), jnp.exp(lgb * (CHUNK - row))
    tab = dict(
        is_a=is_a, row=row, lam_f=jnp.exp(lgf * CHUNK), lam_b=jnp.exp(lgb * CHUNK),
        kdec=jnp.concatenate([kdec_f, kdec_b], axis=1), qdec=jnp.concatenate([qdec_f, qdec_b], axis=1),
        qexp=jnp.concatenate([jnp.broadcast_to(row + 1.0, (CHUNK, LANES)),
                              jnp.broadcast_to(CHUNK - row, (CHUNK, LANES))], axis=1),
        kexp=jnp.concatenate([jnp.broadcast_to(CHUNK - 1.0 - row, (CHUNK, LANES)),
                              jnp.broadcast_to(row, (CHUNK, LANES))], axis=1),
    )
    r = lax.broadcasted_iota(jnp.int32, (2 * LANES, LANES), 0)
    c = lax.broadcasted_iota(jnp.int32, (2 * LANES, LANES), 1)
    tab["diag2"] = ((r & (LANES - 1)) < HEAD_DIM) == (c < HEAD_DIM)
    i2 = lax.broadcasted_iota(jnp.int32, (2 * CHUNK, CHUNK), 0)
    j = lax.broadcasted_iota(jnp.int32, (2 * CHUNK, CHUNK), 1)
    head_b = i2 >= CHUNK
    diff = ((i2 & (CHUNK - 1)) - j).astype(F32)
    up, dn = jnp.maximum(diff, 0.0), jnp.maximum(-diff, 0.0)
    lgf2 = jnp.where(head_b, lgf_ref[2 * pair + 1], lgf_ref[2 * pair])
    lgb2 = jnp.where(head_b, lgb_ref[2 * pair + 1], lgb_ref[2 * pair])
    ef = jnp.where(diff >= 0, jnp.exp(lgf2 * up), 0.0)
    eb = jnp.where(diff <= 0, jnp.exp(lgb2 * dn), 0.0)
    tab["d2"] = ef + eb
    tab["df2"] = ef * up
    tab["db2"] = eb * dn
    return tab


def _stack_pair(is_a, x):
    zero = jnp.zeros_like(x)
    return jnp.concatenate([jnp.where(is_a, x, zero), jnp.where(is_a, zero, x)], axis=0)


def _unstack_pair(is_a, x2):
    return jnp.where(is_a, x2[0:CHUNK, :], x2[CHUNK:2 * CHUNK, :])


def _both_ways(x, dec):
    return (jnp.concatenate([x, x], axis=1) * dec).astype(BF16)


def _scan_states(n_chunks, st, up_rows, up_lam, down_rows, down_lam):
    zero = jnp.zeros((LANES, LANES), F32)

    def up(n, r):
        new = st[n, up_rows, :]
        st[n, up_rows, :] = r
        return r * up_lam + new

    def down(s, r):
        n = n_chunks - 1 - s
        new = st[n, down_rows, :]
        st[n, down_rows, :] = r
        return r * down_lam + new

    lax.fori_loop(0, n_chunks, up, zero)
    lax.fori_loop(0, n_chunks, down, zero)


FWD_ROWS, BWD_ROWS = pl.ds(0, LANES), pl.ds(LANES, LANES)


def _state_spec(n_chunks, pairs):
    return pl.BlockSpec((n_chunks, 2 * LANES, LANES), lambda b, h: (b * pairs + h, 0, 0))


ST_GAIN, ST_XF, ST_XB, ST_IFA, ST_IFB, ST_IBA, ST_IBB, ST_LF, ST_LB = 0, 1, 2, 3, 4, 5, 6, 8, 9
ST_ROWS = 16


GW = GROUP * HEAD_DIM
KEYS = 3 * BLOCK


def _attn_tables(g, bias_ref):
    r = lax.broadcasted_iota(jnp.int32, (GROUP * BLOCK, KEYS), 0)
    kj = lax.broadcasted_iota(jnp.int32, (GROUP * BLOCK, KEYS), 1)
    qi = r & (BLOCK - 1)
    hh = lax.shift_right_logical(r, 7)
    dist = jnp.abs(kj - BLOCK - qi)
    slope = jnp.exp2(-(GROUP * g + hh + 1).astype(F32) * (8.0 / ATTN_HEADS))
    inside = jnp.where(dist <= BLOCK, -slope * dist.astype(F32), NEG_INF)
    bias_ref[BIAS_INSIDE] = inside
    bias_ref[BIAS_FIRST] = jnp.where(kj >= BLOCK, inside, NEG_INF)
    bias_ref[BIAS_LAST] = jnp.where(kj < 2 * BLOCK, inside, NEG_INF)


BIAS_INSIDE, BIAS_FIRST, BIAS_LAST = 0, 1, 2


def _own_lanes(g):
    return lax.shift_right_logical(lax.broadcasted_iota(jnp.int32, (1, LANES), 1), 6) == g


def _mask_keys(x_ref, g, scale, pad_ref, s):
    pad_ref[0:BLOCK, :] = jnp.zeros((BLOCK, LANES), BF16)
    pad_ref[BLOCK + s:2 * BLOCK + s, :] = jnp.zeros((BLOCK, LANES), BF16)
    pad_ref[BLOCK:BLOCK + s, :] = jnp.where(_own_lanes(g), x_ref[...].astype(F32) * scale, 0.0).astype(BF16)


def _lane_block(x, j):
    return x[:, j * LANES:(j + 1) * LANES]


def _stack_heads(x, g):
    assert GROUP == 4 and GW == 2 * LANES
    x1 = pltpu.roll(x, HEAD_DIM, 1)
    keep = _own_lanes(g)
    zero = jnp.zeros((BLOCK, LANES), x.dtype)
    rows = []
    for h in range(GROUP):
        for_g0 = _lane_block(x, h // 2) if h % 2 == 0 else _lane_block(x1, ((h + 1) // 2) % 2)
        for_g1 = _lane_block(x, h // 2) if h % 2 == 1 else _lane_block(x1, h // 2)
        rows.append(jnp.where(keep, jnp.where(g == 0, for_g0, for_g1), zero))
    return jnp.concatenate(rows, axis=0)


def _unstack_heads(x4, g):
    p = [x4[h * BLOCK:(h + 1) * BLOCK, :] for h in range(GROUP)]
    cat = lambda a, b: jnp.concatenate([a, b], axis=1)
    in_place = jnp.where(g == 0, cat(p[0], p[2]), cat(p[1], p[3]))
    one_left = jnp.where(g == 0, cat(p[1], p[3]), cat(p[2], p[0]))
    return in_place + pltpu.roll(one_left, HEAD_DIM, 1)


def _sink_column(sink_ref, g):
    rh = lax.shift_right_logical(lax.broadcasted_iota(jnp.int32, (GROUP * BLOCK, 1), 0), 7)
    col = jnp.zeros((GROUP * BLOCK, 1), F32)
    for h in range(GROUP):
        col = jnp.where(rh == h, sink_ref[GROUP * g + h], col)
    return col


def _attn_probs(qm, k3, bias_ref, sink_col, n, s):
    which = jnp.where(n == 0, BIAS_FIRST, jnp.where(n == s // BLOCK - 1, BIAS_LAST, BIAS_INSIDE))
    logits = _dot_nt(qm, k3) + bias_ref[which]
    m = jnp.maximum(jnp.max(logits, axis=1, keepdims=True), sink_col)
    e = jnp.exp(logits - m)
    e_sink = jnp.exp(sink_col - m)
    inv = 1.0 / (jnp.sum(e, axis=1, keepdims=True) + e_sink)
    return e * inv, e_sink * inv


PAIRS_PER_KV = (RET_HEADS // 2) // KV_HEADS
FWD_ORDER = "rrarra"
BWD_ORDER = "rararr"


def _trip_order(order, chunks, blocks):
    if order.count("r") == chunks and order.count("a") == blocks:
        return order
    return "r" * chunks + "a" * blocks


def _mixers_fwd(u, lgf, lgb, gn_gain, sink, b_loc, after=()):
    t = u.shape[0]
    s = t // b_loc
    n_chunks = s // CHUNK
    pairs = RET_HEADS // 2
    trips = n_chunks // RET_UNROLL
    blocks_half = (s // BLOCK) // PAIRS_PER_KV
    per_trip = blocks_half // trips
    assert n_chunks % RET_UNROLL == 0 and blocks_half % trips == 0 and PAIRS_PER_KV == 2 and s >= 2 * BLOCK

    def body(lgf_ref, lgb_ref, sink_ref, q_ref, k_ref, v_ref, g_ref, gain_ref, aq_ref, ak_ref, av_ref,
             r_ref, xhat_ref, rstd_ref, a_ref, st, kpad, vpad, bias):
        pair = pl.program_id(1)
        g, half = lax.shift_right_logical(pair, 1), pair & 1
        tab = _stacked_tables(lgf_ref, lgb_ref, pair)
        is_a = tab["is_a"]

        @pl.when(half == 0)
        def _():
            _attn_tables(g, bias)
            _mask_keys(ak_ref, g, Q_SCALE, kpad, s)
            _mask_keys(av_ref, g, 1.0, vpad, s)

        def kv_body(n, _):
            k8 = _chunk(k_ref, n).astype(F32) * Q_SCALE
            st[n] = jnp.where(tab["diag2"], _dot_tn(_both_ways(k8, tab["kdec"]), _chunk(v_ref, n)), 0.0)
            return 0

        _chunk_loop(n_chunks, kv_body, 0)
        _scan_states(n_chunks, st, FWD_ROWS, tab["lam_f"], BWD_ROWS, tab["lam_b"])
        sink_col = _sink_column(sink_ref, g)

        def retention_chunk(n):
            q = _chunk(q_ref, n)
            k8 = (_chunk(k_ref, n).astype(F32) * Q_SCALE).astype(BF16)
            v = _chunk(v_ref, n)
            p2 = (_dot_nt(_stack_pair(is_a, q), k8) * tab["d2"]).astype(BF16)
            y = _unstack_pair(is_a, _dot_nn(p2, v))
            y = y + _dot_nn(_both_ways(q.astype(F32), tab["qdec"]), st[n].astype(BF16))
            rows = pl.ds(pl.multiple_of(n * CHUNK, CHUNK), CHUNK)
            mu = _group_sum(is_a, y) * (1.0 / HEAD_DIM)
            dlt = y - mu
            var = _group_sum(is_a, dlt * dlt) * (1.0 / HEAD_DIM)
            rstd = lax.rsqrt(var + GN_EPS)
            xhat = dlt * rstd
            xhat_ref[rows, :] = xhat
            rstd_ref[rows, :] = rstd
            gate = _chunk(g_ref, n).astype(F32)
            r_ref[rows, :] = (xhat * gain_ref[...] * gate * _sigmoid(gate)).astype(BF16)

        def attention_block(blk):
            n = half * blocks_half + blk
            rows = pl.ds(pl.multiple_of(blk * BLOCK, BLOCK), BLOCK)
            keys = pl.ds(pl.multiple_of(n * BLOCK, BLOCK), KEYS)
            p, _ = _attn_probs(_stack_heads(aq_ref[rows, :], g), kpad[keys, :], bias, sink_col, n, s)
            a_ref[rows, :] = _unstack_heads(_dot_nn(p.astype(BF16), vpad[keys, :]), g).astype(BF16)

        def trip(i, _):
            chunk, blk = 0, 0
            for kind in _trip_order(FWD_ORDER, RET_UNROLL, per_trip):
                if kind == "r":
                    retention_chunk(i * RET_UNROLL + chunk)
                    chunk += 1
                else:
                    attention_block(i * per_trip + blk)
                    blk += 1
            return 0

        lax.fori_loop(0, trips, trip, 0)

    lane_blk = lambda c0: _seq_spec(s, c0 // LANES)
    half_rows = blocks_half * BLOCK
    aq_spec = pl.BlockSpec((half_rows, GW), lambda b, h: (b * PAIRS_PER_KV + (h & 1), C_AQ // GW + h // 2))
    a_spec = pl.BlockSpec((half_rows, GW), lambda b, h: (b * PAIRS_PER_KV + (h & 1), h // 2))
    kv_spec = lambda c0: pl.BlockSpec((s, LANES), lambda b, h: (b, c0 // LANES))
    pad = pltpu.VMEM((s + 2 * BLOCK, LANES), BF16)
    body, lead = _after(after, body)
    return pl.pallas_call(
        body, name="mixers_fwd", grid=(b_loc, pairs),
        in_specs=lead + [_smem_spec(), _smem_spec(), _smem_spec(), lane_blk(C_RQ), lane_blk(C_RK), lane_blk(C_RV),
                         lane_blk(C_RG), pl.BlockSpec((1, LANES), lambda b, h: (0, h)), aq_spec, kv_spec(C_AK),
                         kv_spec(C_AV)],
        out_specs=[_seq_spec(s, 0), _seq_spec(s, 0), _seq_spec(s, 0), a_spec, _state_spec(n_chunks, pairs)],
        out_shape=[SDS((t, RET_W), BF16), SDS((t, RET_W), F32), SDS((t, RET_W), F32), SDS((t, ATT_W), BF16),
                   SDS((b_loc * pairs * n_chunks, 2 * LANES, LANES), F32)],
        scratch_shapes=[pad, pad, pltpu.VMEM((3, GROUP * BLOCK, KEYS), F32)],
        compiler_params=_params(("arbitrary", "arbitrary")),
    )(*after, lgf, lgb, sink, u, u, u, u, gn_gain, u, u, u)


def _mixers_bwd(u, xhat, rstd, states, dr, da, lgf, lgb, gn_gain, sink, b_loc, after=()):
    t = u.shape[0]
    s = t // b_loc
    n_chunks = s // CHUNK
    pairs = RET_HEADS // 2
    trips = n_chunks // RET_UNROLL
    blocks_half = (s // BLOCK) // PAIRS_PER_KV
    per_trip = blocks_half // trips
    assert n_chunks % RET_UNROLL == 0 and blocks_half % trips == 0 and PAIRS_PER_KV == 2 and s >= 2 * BLOCK

    def body(lgf_ref, lgb_ref, sink_ref, q_ref, k_ref, v_ref, g_ref, xhat_ref, rstd_ref, dr_ref, gain_ref,
             aq_ref, ak_ref, av_ref, do_ref, st,
             dq_ref, dk_ref, dv_ref, dg_ref, st_ref, daq_ref, dak_ref, dav_ref, dsink_ref,
             gr, dy_s, kpad, vpad, bias, dk_acc, dv_acc):
        pair = pl.program_id(1)
        g, half = lax.shift_right_logical(pair, 1), pair & 1
        tab = _stacked_tables(lgf_ref, lgb_ref, pair)
        is_a = tab["is_a"]
        gain = gain_ref[...]

        @pl.when(half == 0)
        def _():
            _attn_tables(g, bias)
            _mask_keys(ak_ref, g, Q_SCALE, kpad, s)
            _mask_keys(av_ref, g, 1.0, vpad, s)
            dsink_ref[...] = jnp.zeros_like(dsink_ref)

        @pl.when(pair == 0)
        def _():
            dk_acc[...] = jnp.zeros_like(dk_acc)
            dv_acc[...] = jnp.zeros_like(dv_acc)

        def norm_body(n, dgain):
            rows = pl.ds(pl.multiple_of(n * CHUNK, CHUNK), CHUNK)
            xhat, rstd = xhat_ref[rows, :], rstd_ref[rows, :]
            gate = g_ref[rows, :].astype(F32)
            sg = _sigmoid(gate)
            silu = gate * sg
            d_out = dr_ref[rows, :].astype(F32)
            dg_ref[rows, :] = (d_out * xhat * gain * (sg * (1.0 + gate * (1.0 - sg)))).astype(BF16)
            dxh = d_out * gain * silu
            m1 = _group_sum(is_a, dxh) * (1.0 / HEAD_DIM)
            m2 = _group_sum(is_a, dxh * xhat) * (1.0 / HEAD_DIM)
            dy = (rstd * (dxh - m1 - xhat * m2)).astype(BF16)
            dy_s[rows, :] = dy
            qf = q_ref[rows, :].astype(F32)
            gr[n] = jnp.where(tab["diag2"], _dot_tn(_both_ways(qf, tab["qdec"]), dy), 0.0)
            return dgain + jnp.sum(d_out * xhat * silu, axis=0, keepdims=True)

        colsum = lambda x: jnp.sum(x, axis=0, keepdims=True)

        def grad_body(n, carry):
            xfb, ifa, ifb, iba, ibb, lf, lb = carry
            rows = pl.ds(pl.multiple_of(n * CHUNK, CHUNK), CHUNK)
            q = q_ref[rows, :]
            qf = q.astype(F32)
            k8f = k_ref[rows, :].astype(F32) * Q_SCALE
            k8 = k8f.astype(BF16)
            v = v_ref[rows, :]
            dy = dy_s[rows, :]
            q2, dy2 = _stack_pair(is_a, q), _stack_pair(is_a, dy)
            sc = _dot_nt(q2, k8)
            dp = _dot_nt(dy2, v)
            a2 = (sc * tab["d2"]).astype(BF16)
            ds2 = (dp * tab["d2"]).astype(BF16)
            dq = _unstack_pair(is_a, _dot_nn(ds2, k8))
            dk = _dot_tn(ds2, q2)
            dv = _dot_tn(a2, dy2)
            prod = sc * dp
            pf, pb = prod * tab["df2"], prod * tab["db2"]
            ifa, ifb = ifa + colsum(pf[0:CHUNK, :]), ifb + colsum(pf[CHUNK:2 * CHUNK, :])
            iba, ibb = iba + colsum(pb[0:CHUNK, :]), ibb + colsum(pb[CHUNK:2 * CHUNK, :])
            states, sgrads = st[n], gr[n]
            sb, gb = states.astype(BF16), sgrads.astype(BF16)
            dqc = _dot_nt(dy, sb) * tab["qdec"]
            dkc = _dot_nt(v, gb) * tab["kdec"]
            dv = dv + _dot_nn(_both_ways(k8f, tab["kdec"]), gb)
            dq_ref[rows, :] = (dq + dqc[:, 0:LANES] + dqc[:, LANES:2 * LANES]).astype(BF16)
            dk_ref[rows, :] = ((dk + dkc[:, 0:LANES] + dkc[:, LANES:2 * LANES]) * Q_SCALE).astype(BF16)
            dv_ref[rows, :] = dv.astype(BF16)
            q2w, k2w = jnp.concatenate([qf, qf], axis=1), jnp.concatenate([k8f, k8f], axis=1)
            xfb = xfb + colsum(tab["qexp"] * q2w * dqc + tab["kexp"] * k2w * dkc)
            prod_s = sgrads * states
            lf, lb = lf + colsum(prod_s[0:LANES, :]), lb + colsum(prod_s[LANES:2 * LANES, :])
            return xfb, ifa, ifb, iba, ibb, lf, lb

        sink_col = _sink_column(sink_ref, g)
        head_row = lax.broadcasted_iota(jnp.int32, dsink_ref.shape, 0)

        def attention_block(blk):
            n = half * blocks_half + blk
            rows = pl.ds(pl.multiple_of(blk * BLOCK, BLOCK), BLOCK)
            keys = pl.ds(pl.multiple_of(n * BLOCK, BLOCK), KEYS)
            qm = _stack_heads(aq_ref[rows, :], g)
            k3, v3 = kpad[keys, :], vpad[keys, :]
            p, p_sink = _attn_probs(qm, k3, bias, sink_col, n, s)
            dom = _stack_heads(do_ref[rows, :], g)
            dp = _dot_nt(dom, v3)
            delta = jnp.sum(p * dp, axis=1, keepdims=True)
            ds_mat = (p * (dp - delta)).astype(BF16)
            daq_ref[rows, :] = _unstack_heads(_dot_nn(ds_mat, k3), g).astype(BF16)
            dk_acc[keys, :] += _dot_tn(ds_mat, qm) * Q_SCALE
            dv_acc[keys, :] += _dot_tn(p.astype(BF16), dom)
            w = p_sink * delta
            upd = jnp.zeros(dsink_ref.shape, F32)
            for h in range(GROUP):
                upd = upd + jnp.where(head_row == h, -jnp.sum(w[h * BLOCK:(h + 1) * BLOCK, :]), 0.0)
            dsink_ref[...] += upd

        dgain = _chunk_loop(n_chunks, norm_body, jnp.zeros((1, LANES), F32))
        _scan_states(n_chunks, gr, BWD_ROWS, tab["lam_b"], FWD_ROWS, tab["lam_f"])

        def trip(i, carry):
            chunk, blk = 0, 0
            for kind in _trip_order(BWD_ORDER, RET_UNROLL, per_trip):
                if kind == "r":
                    carry = grad_body(i * RET_UNROLL + chunk, carry)
                    chunk += 1
                else:
                    attention_block(i * per_trip + blk)
                    blk += 1
            return carry

        z = jnp.zeros((1, LANES), F32)
        init = (jnp.zeros((1, 2 * LANES), F32), z, z, z, z, z, z)
        xfb, ifa, ifb, iba, ibb, lf, lb = lax.fori_loop(0, trips, trip, init)
        st_ref[...] = jnp.zeros_like(st_ref)
        st_ref[ST_GAIN:ST_GAIN + 1, :] = dgain
        st_ref[ST_XF:ST_XF + 1, :] = xfb[:, 0:LANES]
        st_ref[ST_XB:ST_XB + 1, :] = xfb[:, LANES:2 * LANES]
        st_ref[ST_IFA:ST_IFA + 1, :] = ifa
        st_ref[ST_IFB:ST_IFB + 1, :] = ifb
        st_ref[ST_IBA:ST_IBA + 1, :] = iba
        st_ref[ST_IBB:ST_IBB + 1, :] = ibb
        st_ref[ST_LF:ST_LF + 1, :] = lf * (CHUNK * tab["lam_f"])
        st_ref[ST_LB:ST_LB + 1, :] = lb * (CHUNK * tab["lam_b"])

        @pl.when(pair == pairs - 1)
        def _():
            dak_ref[...] = dk_acc[BLOCK:BLOCK + s, :].astype(BF16)
            dav_ref[...] = dv_acc[BLOCK:BLOCK + s, :].astype(BF16)

    lane_blk = lambda c0: _seq_spec(s, c0 // LANES)
    seq0 = _seq_spec(s, 0)
    half_rows = blocks_half * BLOCK
    aq_spec = pl.BlockSpec((half_rows, GW), lambda b, h: (b * PAIRS_PER_KV + (h & 1), C_AQ // GW + h // 2))
    a_spec = pl.BlockSpec((half_rows, GW), lambda b, h: (b * PAIRS_PER_KV + (h & 1), h // 2))
    kv_spec = lambda c0: pl.BlockSpec((s, LANES), lambda b, h: (b, c0 // LANES))
    kv_out = pl.BlockSpec((s, LANES), lambda b, h: (b, 0))
    state = pltpu.VMEM((n_chunks, 2 * LANES, LANES), F32)
    pad = pltpu.VMEM((s + 2 * BLOCK, LANES), BF16)
    acc = pltpu.VMEM((s + 2 * BLOCK, LANES), F32)
    body, lead = _after(after, body)
    return pl.pallas_call(
        body, name="mixers_bwd", grid=(b_loc, pairs),
        in_specs=lead + [_smem_spec(), _smem_spec(), _smem_spec(), lane_blk(C_RQ), lane_blk(C_RK), lane_blk(C_RV),
                         lane_blk(C_RG), seq0, seq0, seq0, pl.BlockSpec((1, LANES), lambda b, h: (0, h)),
                         aq_spec, kv_spec(C_AK), kv_spec(C_AV), a_spec, _state_spec(n_chunks, pairs)],
        out_specs=[seq0] * 4 + [pl.BlockSpec((ST_ROWS, LANES), lambda b, h: (b, h)), a_spec, kv_out, kv_out,
                                pl.BlockSpec((8, LANES), lambda b, h: (b * KV_HEADS + h // 2, 0))],
        out_shape=[SDS((t, RET_W), BF16)] * 4 + [SDS((b_loc * ST_ROWS, RET_W), F32), SDS((t, ATT_W), BF16),
                                                   SDS((t, KV_W), BF16), SDS((t, KV_W), BF16),
                                                   SDS((b_loc * KV_HEADS * 8, LANES), F32)],
        scratch_shapes=[state, pltpu.VMEM((s, LANES), BF16), pad, pad,
                        pltpu.VMEM((3, GROUP * BLOCK, KEYS), F32), acc, acc],
        compiler_params=_params(("arbitrary", "arbitrary")),
    )(*after, lgf, lgb, sink, u, u, u, u, xhat, rstd, dr, gn_gain, u, u, u, da, states)


def _pack_small(acc2, acc1, ret_stats, dsink, b_loc, d):
    pairs = RET_HEADS // 2

    def body(acc2_ref, acc1_ref, st_ref, dsink_ref, out_ref):
        out_ref[...] = jnp.zeros_like(out_ref)
        out_ref[ROW_LN1G:ROW_LN1G + 1, :] = acc1_ref[0:1, :]
        out_ref[ROW_LN1B:ROW_LN1B + 1, :] = acc1_ref[1:2, :]
        out_ref[ROW_LN2G:ROW_LN2G + 1, :] = acc2_ref[1:2, :]
        out_ref[ROW_LN2B:ROW_LN2B + 1, :] = acc2_ref[2:3, :]
        out_ref[ROW_LOSS:ROW_LOSS + 1, :] = acc2_ref[0:1, :]
        st = st_ref[0:ST_ROWS, :]
        for b in range(1, b_loc):
            st = st + st_ref[b * ST_ROWS:(b + 1) * ST_ROWS, :]
        out_ref[ROW_GN:ROW_GN + 1, 0:RET_W] = st[ST_GAIN:ST_GAIN + 1, :]
        lane = lax.broadcasted_iota(jnp.int32, (1, d), 1)
        misc = jnp.zeros((1, d), F32)
        for pr in range(pairs):
            blk = st[:, pr * LANES:(pr + 1) * LANES]
            half = lax.broadcasted_iota(jnp.int32, (1, LANES), 1) < HEAD_DIM
            for h in range(2):
                sel = half if h == 0 else jnp.logical_not(half)
                cross_f = jnp.sum(jnp.where(sel, blk[ST_XF:ST_XF + 1, :] + blk[ST_LF:ST_LF + 1, :], 0.0))
                cross_b = jnp.sum(jnp.where(sel, blk[ST_XB:ST_XB + 1, :] + blk[ST_LB:ST_LB + 1, :], 0.0))
                intra_f = jnp.sum(blk[ST_IFA + h:ST_IFA + h + 1, :])
                intra_b = jnp.sum(blk[ST_IBA + h:ST_IBA + h + 1, :])
                head = 2 * pr + h
                misc = jnp.where(lane == MISC_DF + head, cross_f + intra_f, misc)
                misc = jnp.where(lane == MISC_DB + head, cross_b + intra_b, misc)
        for g in range(KV_HEADS):
            tot = dsink_ref[g * 8:(g + 1) * 8, :]
            for b in range(1, b_loc):
                tot = tot + dsink_ref[(b * KV_HEADS + g) * 8:(b * KV_HEADS + g + 1) * 8, :]
            for h in range(GROUP):
                misc = jnp.where(lane == MISC_SINK + GROUP * g + h, jnp.sum(tot[h:h + 1, 0:1]), misc)
        out_ref[ROW_MISC:ROW_MISC + 1, :] = misc

    return pl.pallas_call(body, name="pack_small", out_shape=SDS((SMALL_ROWS, d), F32))(acc2, acc1, ret_stats, dsink)


BIG = ("w_in", "w_out", "w_ffn_gate", "w_ffn_up", "w_ffn_down", "w_ple_proj", "w_ple_gate")
TRANSPOSED_OUTSIDE = ("w_in", "w_ffn_gate", "w_ffn_up")
TRANSPOSED_HERE = ("w_ple_proj",)
SMALL = ("ret_decay_fwd", "ret_decay_bwd", "ret_gn_gain", "attn_sink", "ln1_gain", "ln1_bias", "ln2_gain", "ln2_bias")
ORDER = ("w_in", "ret_decay_fwd", "ret_decay_bwd", "ret_gn_gain", "attn_sink", "w_out", "ln1_gain", "ln1_bias",
         "w_ffn_gate", "w_ffn_up", "w_ffn_down", "w_ple_proj", "w_ple_gate", "ln2_gain", "ln2_bias")


GATHER_ORDER = ("w_in", "w_ffn_up", "w_out", "w_ffn_gate", "w_ple_gate", "w_ple_proj", "w_ffn_down")
GATHER_TWO_LEVEL = ("w_in", "w_ffn_up")

def _local_step(x2, p2, target2, fetch, publish, small, b_loc, me):
    d = x2.shape[1]
    lgf, lgb = _log_decay(small["ret_decay_fwd"], small["ret_decay_bwd"])
    lgf1, lgb1, sink1 = lgf.reshape(-1), lgb.reshape(-1), small["attn_sink"].reshape(-1)
    (w_in,) = fetch(("w_in",), ())
    u, xb = _in_proj(x2, w_in)
    passed = fetch.pass_on("w_ffn_up", (xb,))
    r, ret_xhat, ret_rstd, a, ret_states = _mixers_fwd(u, lgf1, lgb1, small["ret_gn_gain"], sink1, b_loc, passed)
    w_out, w_gate, w_up = fetch(("w_out", "w_ffn_gate", "w_ffn_up"), (r, a))
    xhat1, rstd1, h1b, dact_dg, dact_du, act = _mix_ln1_ffn_up(
        r, a, x2, w_out, w_gate, w_up, small["ln1_gain"], small["ln1_bias"])
    w_pg, w_pe, w_down = fetch(("w_ple_gate", "w_ple_proj", "w_ffn_down"), (act,))
    dz2, dz2b, dsb, dpleb, dg, dup, acc2 = _ffn_down_ln2_loss(
        act, dact_dg, dact_du, h1b, p2, xhat1, target2, w_down, w_pg, w_pe,
        small["ln1_gain"], small["ln1_bias"], small["ln2_gain"], small["ln2_bias"])
    own = {}

    def grad(name, parts, rhs, after=()):
        whole, own[name] = _weight_grad("grad_" + name, me, parts, rhs, after)
        return whole

    ffn_jobs = dict(w_ffn_down=(act, dz2b), w_ple_proj=(dpleb, p2), w_ple_gate=(h1b, dsb),
                    w_ffn_gate=(dg, h1b), w_ffn_up=(dup, h1b))
    wholes, owns = _weight_grad_jobs("grad_w_ffn", me, list(ffn_jobs.values()))
    own.update(zip(ffn_jobs, owns))
    t2 = publish("ffn", dict(zip(ffn_jobs, wholes)))
    dz1, dz1b, dr, da, acc1 = _dh1_ln1_bwd(
        dz2, dg, dup, dsb, xhat1, rstd1, w_gate, w_up, w_pg, w_out, small["ln1_gain"], t2)
    t3 = publish("out", dict(w_out=grad("w_out", [r, a], dz1b)))
    dq, dk, dv, dgate, ret_stats, daq, dak, dav, dsink = _mixers_bwd(
        u, ret_xhat, ret_rstd, ret_states, dr, da, lgf1, lgb1, small["ret_gn_gain"], sink1, b_loc, t3)
    parts = [dq, dk, dv, dgate, daq, dak, dav]
    small_part = _pack_small(acc2, acc1, ret_stats, dsink, b_loc, d)
    t4 = publish("in", dict(w_in=grad("w_in", parts, xb)), small_part)
    grad_x = _in_proj_bwd(dz1, parts, w_in, t4)
    return grad_x, own, small_part


def kernel(x, p, w_in, ret_decay_fwd, ret_decay_bwd, ret_gn_gain, attn_sink, w_out, ln1_gain, ln1_bias, w_ffn_gate, w_ffn_up, w_ffn_down, w_ple_proj, w_ple_gate, ln2_gain, ln2_bias, loss_target, m_w_in, m_ret_decay_fwd, m_ret_decay_bwd, m_ret_gn_gain, m_attn_sink, m_w_out, m_ln1_gain, m_ln1_bias, m_w_ffn_gate, m_w_ffn_up, m_w_ffn_down, m_w_ple_proj, m_w_ple_gate, m_ln2_gain, m_ln2_bias, v_w_in, v_ret_decay_fwd, v_ret_decay_bwd, v_ret_gn_gain, v_attn_sink, v_w_out, v_ln1_gain, v_ln1_bias, v_w_ffn_gate, v_w_ffn_up, v_w_ffn_down, v_w_ple_proj, v_w_ple_gate, v_ln2_gain, v_ln2_bias):
    given = dict(locals())

    def strip(n, a):
        if n not in BIG:
            return a
        return a[0].T if n in TRANSPOSED_OUTSIDE else a[0]

    def restore(n, a):
        if n not in BIG:
            return a
        return (a.T if n in TRANSPOSED_OUTSIDE else a)[None]

    w = {n: strip(n, given[n]) for n in ORDER}
    m = {n: strip(n, given["m_" + n]) for n in ORDER}
    v = {n: strip(n, given["v_" + n]) for n in ORDER}
    b_loc, s, d = x.shape
    x2 = x.reshape(b_loc * s, d)
    p2 = p[0].reshape(b_loc * s, p.shape[-1])
    target2 = loss_target.reshape(b_loc * s, d)

    small = {n: w[n] for n in SMALL}
    me = (4 * lax.axis_index("x") + 2 * lax.axis_index("y") + lax.axis_index("c")).astype(jnp.int32).reshape(1)

    gather = _gather_start(
        {n: w[n] for n in GATHER_ORDER},
        [_gather_copy_near if n in GATHER_TWO_LEVEL else _gather_copy for n in GATHER_ORDER])

    passing = {}

    def pass_on(n, after):
        passing[n] = _gather_relay("gather_relay_" + n, gather, GATHER_ORDER.index(n), list(after))
        return (passing[n]["token"],)

    def fetch(names, after):
        out = {}
        for n in [n for n in names if n in GATHER_TWO_LEVEL]:
            if n not in passing:
                pass_on(n, after)
            out[n] = _split_copy_wait("gather_wait_" + n, passing[n], [0], list(after))[0][0]
        direct = [n for n in names if n not in GATHER_TWO_LEVEL]
        if direct:
            got = _split_copy_wait("gather_wait_" + direct[0], gather, [GATHER_ORDER.index(n) for n in direct],
                                   list(after))
            out.update({n: item[0] for n, item in zip(direct, got)})
        return [out[n] for n in names]

    scatters = []

    def publish(tag, products, small_sums=None):
        items = [(products[n], lax.empty((N_DEV - 1, products[n].shape[0] // N_DEV, products[n].shape[1]), BF16))
                 for n in products]
        copies = [_scatter_copy] * len(items)
        if small_sums is not None:
            items.append((small_sums, lax.empty((N_DEV - 1,) + small_sums.shape, F32)))
            copies.append(_small_copy)
        started = _split_copy_start("scatter_start_" + tag, items, copies)
        scatters.append((list(products), small_sums is not None, started))
        return (started["token"],)

    fetch.pass_on = pass_on
    grad_x, own, small_part = _local_step(x2, p2, target2, fetch, publish, small, b_loc, me)

    out_g, out_d, out_m, out_v = {}, {}, {}, {}
    after = [grad_x]
    for names, with_small, started in scatters:
        landed = _split_copy_wait("scatter_wait_" + names[0], started, list(range(len(started["items"]))), after)
        if with_small:
            mine, from_peers = landed[-1]
            loss, sg, sd, sm, sv = _small_adamw(
                me, mine, from_peers, small, {n: m[n] for n in SMALL}, {n: v[n] for n in SMALL})
            for dst, src in ((out_g, sg), (out_d, sd), (out_m, sm), (out_v, sv)):
                dst.update(src)
        recv = {n: item[1] for n, item in zip(names, landed)}
        alike = {}
        for n in names:
            alike.setdefault((own[n].shape, n in TRANSPOSED_HERE), []).append(n)
        for (_, transposed), ns in alike.items():
            res = _reduce_adamw(ns[0], [own[n] for n in ns], [recv[n] for n in ns], [w[n] for n in ns],
                                [m[n] for n in ns], [v[n] for n in ns], transposed)
            for dst, vals in zip((out_g, out_d, out_m, out_v), res):
                dst.update(zip(ns, vals))
        after = [out_v[names[-1]]]

    outs = [loss[0, 0], grad_x.reshape(x.shape)]
    for group in (out_g, out_d, out_m, out_v):
        outs += [restore(n, group[n]) for n in ORDER]
    return tuple(outs)
```

```python
import functools

import jax
import jax.numpy as jnp
from jax import lax
from jax.experimental import pallas as pl
from jax.experimental.pallas import tpu as pltpu

F32, BF16 = jnp.float32, jnp.bfloat16
SDS = jax.ShapeDtypeStruct
MESH = pl.DeviceIdType.MESH

N_DEV = 8
HEAD_DIM = 64
RET_HEADS = 8
ATTN_HEADS = 8
KV_HEADS = 2
GROUP = ATTN_HEADS // KV_HEADS
RET_W = RET_HEADS * HEAD_DIM
ATT_W = ATTN_HEADS * HEAD_DIM
KV_W = KV_HEADS * HEAD_DIM
LANES = 128
CHUNK = 128
BLOCK = 128
Q_SCALE = HEAD_DIM ** -0.5
ALPHA = 2.0 ** 0.25
LN_EPS = 1e-5
GN_EPS = 1e-5
NEG_INF = -1e30
C_RQ, C_RK, C_RV, C_RG = 0, RET_W, 2 * RET_W, 3 * RET_W
C_AQ = 4 * RET_W
C_AK = C_AQ + ATT_W
C_AV = C_AK + KV_W
IN_W = C_AV + KV_W

ADAM_LR = 0.001
ADAM_B1 = 0.9
ADAM_B2 = 0.999
ADAM_EPS = 1e-08
ADAM_WD = 0.01
ADAM_STEP = 10

VMEM_LIMIT = 56 * 1024 * 1024
MATMUL_ROWS = 512
EPILOGUE_ROWS = 256
SUB_ROWS = 512
SMALL_ROWS = 16
ROW_LN1G, ROW_LN1B, ROW_LN2G, ROW_LN2B, ROW_LOSS, ROW_GN, ROW_MISC = 0, 1, 2, 3, 4, 5, 6
MISC_DF, MISC_DB, MISC_SINK = 0, 8, 16


def _dot_nn(a, b):
    return lax.dot_general(a, b, (((1,), (0,)), ((), ())), preferred_element_type=F32)


def _dot_nt(a, b):
    return lax.dot_general(a, b, (((1,), (1,)), ((), ())), preferred_element_type=F32)


def _dot_tn(a, b):
    return lax.dot_general(a, b, (((0,), (0,)), ((), ())), preferred_element_type=F32)


def _params(sem=None, vmem=VMEM_LIMIT):
    kw = {"vmem_limit_bytes": vmem}
    if sem is not None:
        kw["dimension_semantics"] = sem
    return pltpu.CompilerParams(**kw)


def _row_tile(t, want=512):
    tm = want
    while t % tm:
        tm //= 2
    return tm


def _sigmoid(x):
    return jax.nn.sigmoid(x)


def _layer_norm_stats(z):
    mu = jnp.mean(z, axis=1, keepdims=True)
    d = z - mu
    var = jnp.mean(d * d, axis=1, keepdims=True)
    rstd = lax.rsqrt(var + LN_EPS)
    return d * rstd, rstd


def _layer_norm_bwd(dxh, xhat, rstd):
    m1 = jnp.mean(dxh, axis=1, keepdims=True)
    m2 = jnp.mean(dxh * xhat, axis=1, keepdims=True)
    return rstd * (dxh - m1 - xhat * m2)


def _mesh_pos():
    return lax.axis_index("x"), lax.axis_index("y"), lax.axis_index("c")


HBM_SPEC = pl.BlockSpec(memory_space=pltpu.HBM)
SEM_SPEC = pl.BlockSpec(memory_space=pltpu.SEMAPHORE)
ANY_SPEC = pl.BlockSpec(memory_space=pl.ANY)
SIDE_EFFECT = pltpu.SideEffectType.DATAFLOW_SIDE_EFFECTING
PEER_SEMS = pltpu.SemaphoreType.DMA((N_DEV - 1,))


def _in_hbm(a):
    return pltpu.with_memory_space_constraint(a, pltpu.HBM)


def _split_copy_start(name, items, copies):
    n = len(items)
    flat = [a for it in items for a in it]
    k = len(flat)

    def body(*refs):
        arr, sems = list(refs[:k]), refs[k:k + 2 * n]
        for i, it in enumerate(items):
            mine = [arr.pop(0) for _ in it]
            for m in range(1, N_DEV):
                cp = copies[i](m, mine, sems[i].at[m - 1], sems[n + i].at[m - 1])
                if cp is not None:
                    cp.start()
        token = refs[-1]
        token[...] = jnp.zeros_like(token)

    res = pl.pallas_call(
        body, name=name,
        out_shape=[PEER_SEMS] * (2 * n) + [pltpu.HBM(a.shape, a.dtype) for a in flat] + [SDS((8, LANES), F32)],
        in_specs=[HBM_SPEC] * k,
        out_specs=[SEM_SPEC] * (2 * n) + [HBM_SPEC] * k + [pl.BlockSpec(memory_space=pltpu.VMEM)],
        input_output_aliases={j: 2 * n + j for j in range(k)},
        compiler_params=pltpu.CompilerParams(has_side_effects=SIDE_EFFECT),
    )(*[_in_hbm(a) for a in flat])
    thru, out_items = list(res[2 * n:2 * n + k]), []
    for it in items:
        out_items.append(tuple(thru.pop(0) for _ in it))
    return dict(send=res[:n], recv=res[n:2 * n], items=out_items, token=res[-1], copies=copies)


def _gather_start(shards, copies):
    names = list(shards)
    n = len(names)
    flip = [name in TRANSPOSED_HERE for name in names]
    shapes = [shards[name].shape[::-1] if f else shards[name].shape for name, f in zip(names, flip)]
    most = (max(s[0] for s in shapes), max(s[1] for s in shapes))

    def body(*refs):
        src, sems, land, token = refs[:n], refs[n:3 * n], refs[3 * n:4 * n], refs[4 * n]
        wide, narrow, sem = refs[4 * n + 1:]
        for i, (rows, cols) in enumerate(shapes):
            raw = wide.at[0:cols, 0:rows] if flip[i] else wide.at[0:rows, 0:cols]
            bring = pltpu.make_async_copy(src[i], raw, sem.at[0])
            bring.start()
            bring.wait()
            narrow[0:rows, 0:cols] = (raw[...].T if flip[i] else raw[...]).astype(BF16)
            mine = land[i].at[pl.ds(pl.multiple_of(_peer_index(0) * rows, 8), rows), :]
            place = pltpu.make_async_copy(narrow.at[0:rows, 0:cols], mine, sem.at[0])
            place.start()
            place.wait()
            for m in range(1, N_DEV):
                cp = copies[i](m, [land[i]], sems[i].at[m - 1], sems[n + i].at[m - 1])
                if cp is not None:
                    cp.start()
        token[...] = jnp.zeros_like(token)

    side = max(most)
    res = pl.pallas_call(
        body, name="gather_start",
        out_shape=[PEER_SEMS] * (2 * n) + [pltpu.HBM((N_DEV * r, c), BF16) for r, c in shapes] + [SDS((8, LANES), F32)],
        in_specs=[HBM_SPEC] * n,
        out_specs=[SEM_SPEC] * (2 * n) + [HBM_SPEC] * n + [pl.BlockSpec(memory_space=pltpu.VMEM)],
        scratch_shapes=[pltpu.VMEM((side, side), F32), pltpu.VMEM(most, BF16), pltpu.SemaphoreType.DMA((1,))],
        compiler_params=pltpu.CompilerParams(has_side_effects=SIDE_EFFECT),
    )(*[_in_hbm(shards[name]) for name in names])
    return dict(send=res[:n], recv=res[n:2 * n], items=[(a,) for a in res[2 * n:3 * n]], token=res[-1], copies=copies)


def _gather_relay(name, started, which, after):
    (land,) = started["items"][which]

    def body(*refs):
        land_ref, old_send, old_recv = refs[0], refs[1], refs[2]
        send, recv, token = refs[3 + len(after)], refs[4 + len(after)], refs[-1]
        for m in range(1, N_DEV):
            cp = _gather_copy_near(m, [land_ref], old_send.at[m - 1], old_recv.at[m - 1])
            if cp is None:
                continue
            cp.wait_send()
            cp.wait_recv()
            if m > 1:
                _gather_copy_pass(m + 1, [land_ref], send.at[m], recv.at[m]).start()
        token[...] = jnp.zeros_like(token)

    res = pl.pallas_call(
        body, name=name,
        out_shape=[PEER_SEMS, PEER_SEMS, pltpu.HBM(land.shape, land.dtype), SDS((8, LANES), F32)],
        in_specs=[HBM_SPEC, SEM_SPEC, SEM_SPEC] + [ANY_SPEC] * len(after),
        out_specs=[SEM_SPEC, SEM_SPEC, HBM_SPEC, pl.BlockSpec(memory_space=pltpu.VMEM)],
        input_output_aliases={0: 2},
        compiler_params=pltpu.CompilerParams(has_side_effects=SIDE_EFFECT),
    )(land, started["send"][which], started["recv"][which], *[_in_hbm(a) for a in after])
    return dict(send=[res[0]], recv=[res[1]], items=[(res[2],)], token=res[3], copies=[_gather_copy_pass])


def _split_copy_wait(name, started, which, after):
    items = [started["items"][i] for i in which]
    copies = [started["copies"][i] for i in which]
    n = len(items)
    flat = [a for it in items for a in it]
    k = len(flat)

    def body(*refs):
        arr, sems = list(refs[:k]), refs[k:k + 2 * n]
        for i, it in enumerate(items):
            mine = [arr.pop(0) for _ in it]
            for m in range(1, N_DEV):
                cp = copies[i](m, mine, sems[i].at[m - 1], sems[n + i].at[m - 1])
                if cp is not None:
                    cp.wait_send()
                    cp.wait_recv()

    res = pl.pallas_call(
        body, name=name,
        out_shape=[pltpu.HBM(a.shape, a.dtype) for a in flat],
        in_specs=[HBM_SPEC] * k + [SEM_SPEC] * (2 * n) + [ANY_SPEC] * len(after),
        out_specs=[HBM_SPEC] * k,
        input_output_aliases={j: j for j in range(k)},
        compiler_params=pltpu.CompilerParams(has_side_effects=SIDE_EFFECT),
    )(*flat, *[started["send"][i] for i in which], *[started["recv"][i] for i in which], *[_in_hbm(a) for a in after])
    thru, out_items = list(res), []
    for it in items:
        out_items.append(tuple(thru.pop(0) for _ in it))
    return out_items


def _gather_copy(m, refs, send_sem, recv_sem):
    (land_ref,) = refs
    r = land_ref.shape[0] // N_DEV
    mine = land_ref.at[pl.ds(pl.multiple_of(_peer_index(0) * r, 8), r), :]
    return pltpu.make_async_remote_copy(src_ref=mine, dst_ref=mine, send_sem=send_sem, recv_sem=recv_sem,
                                        device_id=_peer(m), device_id_type=MESH)


def _gather_copy_near(m, refs, send_sem, recv_sem):
    return _gather_copy(m, refs, send_sem, recv_sem) if m == 1 or m % 2 == 0 else None


def _gather_copy_pass(m, refs, send_sem, recv_sem):
    if m == 1 or m % 2 == 0:
        return None
    (land_ref,) = refs
    r = land_ref.shape[0] // N_DEV
    block = land_ref.at[pl.ds(pl.multiple_of(_peer_index(m ^ 1) * r, 8), r), :]
    return pltpu.make_async_remote_copy(src_ref=block, dst_ref=block, send_sem=send_sem, recv_sem=recv_sem,
                                        device_id=_peer(1), device_id_type=MESH)


def _small_copy(m, refs, send_sem, recv_sem):
    part_ref, land_ref = refs
    return pltpu.make_async_remote_copy(src_ref=part_ref, dst_ref=land_ref.at[m - 1], send_sem=send_sem,
                                        recv_sem=recv_sem, device_id=_peer(m), device_id_type=MESH)


def _scatter_copy(m, refs, send_sem, recv_sem):
    buf_ref, land_ref = refs
    r = buf_ref.shape[0] // N_DEV
    src = buf_ref.at[pl.ds(pl.multiple_of(_peer_index(m) * r, 8), r), :]
    return pltpu.make_async_remote_copy(src_ref=src, dst_ref=land_ref.at[m - 1], send_sem=send_sem,
                                        recv_sem=recv_sem, device_id=_peer(m), device_id_type=MESH)


def _peer(m):
    x, y, c = _mesh_pos()
    bx, by, bc = (m >> 2) & 1, (m >> 1) & 1, m & 1
    return (x ^ bx if bx else x, y ^ by if by else y, c ^ bc if bc else c)


def _peer_index(m):
    x, y, c = _mesh_pos()
    return (4 * x + 2 * y + c) ^ m


SMALL_PLACE = {
    "ln1_gain": (ROW_LN1G, 0), "ln1_bias": (ROW_LN1B, 0), "ln2_gain": (ROW_LN2G, 0), "ln2_bias": (ROW_LN2B, 0),
    "ret_gn_gain": (ROW_GN, 0), "ret_decay_fwd": (ROW_MISC, MISC_DF), "ret_decay_bwd": (ROW_MISC, MISC_DB),
    "attn_sink": (ROW_MISC, MISC_SINK)}


def _small_adamw(me, part, landed, w, m, v):
    d = part.shape[1]
    names = list(SMALL_PLACE)
    k = len(names)

    def body(*refs):
        me_ref, part_ref, land_ref = refs[:3]
        refs = refs[2:]
        w_refs, m_refs, v_refs = refs[1:1 + k], refs[1 + k:1 + 2 * k], refs[1 + 2 * k:1 + 3 * k]
        outs = refs[1 + 3 * k:1 + 7 * k + 1]
        tot_ref = refs[-1]
        loss_ref, g_refs, dl_refs = outs[0], outs[1:1 + k], outs[1 + k:1 + 2 * k]
        nm_refs, nv_refs = outs[1 + 2 * k:1 + 3 * k], outs[1 + 3 * k:1 + 4 * k]
        tot = jnp.zeros(part_ref.shape, F32)
        for dev in range(N_DEV):
            j = dev ^ me_ref[0]
            tot = tot + jnp.where(j == 0, part_ref[...], land_ref[jnp.maximum(j, 1) - 1])
        tot_ref[...] = tot
        loss_ref[...] = (0.5 / d) * jnp.sum(tot_ref[ROW_LOSS:ROW_LOSS + 1, :], axis=1, keepdims=True)
        for i, name in enumerate(names):
            row, lo = SMALL_PLACE[name]
            wv = w_refs[i][...]
            g = tot_ref[row:row + 1, lo:lo + wv.shape[1]]
            if name.startswith("ret_decay"):
                p2 = jnp.exp2(wv)
                g = g * (-p2 * jnp.log(2.0) / (1.0 - p2))
            g_refs[i][...] = g
            _adamw_store(g, wv, m_refs[i][...], v_refs[i][...], dl_refs[i], nm_refs[i], nv_refs[i])

    shapes = [SDS(w[n].shape, F32) for n in names]
    vm = pl.BlockSpec(memory_space=pltpu.VMEM)
    res = pl.pallas_call(
        body, name="small_adamw", out_shape=[SDS((1, 1), F32)] + shapes * 4,
        in_specs=[_smem_spec()] + [vm] * (2 + 3 * k), out_specs=[vm] * (1 + 4 * k),
        scratch_shapes=[pltpu.VMEM(part.shape, F32)],
    )(me, part, landed, *[w[n] for n in names], *[m[n] for n in names], *[v[n] for n in names])
    groups = [dict(zip(names, res[1 + j * k:1 + (j + 1) * k])) for j in range(4)]
    return (res[0], *groups)


def _adamw_store(g, w, m, v, dl_ref, nm_ref, nv_ref):
    m = ADAM_B1 * m + (1.0 - ADAM_B1) * g
    v = ADAM_B2 * v + (1.0 - ADAM_B2) * (g * g)
    m_hat = m / (1.0 - ADAM_B1 ** ADAM_STEP)
    v_hat = v / (1.0 - ADAM_B2 ** ADAM_STEP)
    dl_ref[...] = -ADAM_LR * (m_hat / (jnp.sqrt(v_hat) + ADAM_EPS) + ADAM_WD * w)
    nm_ref[...] = m
    nv_ref[...] = v


def _reduce_adamw(name, owns, recvs, ws, ms, vs, transposed):
    count = len(owns)
    rows, n = owns[0].shape
    steps = 1 if transposed or rows % 16 else 2
    rb = rows // steps

    def body(*refs):
        ins, outs = refs[:5 * count], refs[5 * count:]
        j = pl.program_id(0)
        for k in range(count):
            @pl.when(j == k)
            def _(k=k):
                own_ref, recv_ref, w_ref, m_ref, v_ref = ins[5 * k:5 * k + 5]
                g_ref, dl_ref, nm_ref, nv_ref = outs[4 * k:4 * k + 4]
                g = own_ref[...]
                for p in range(recv_ref.shape[0]):
                    g = g + recv_ref[p].astype(F32)
                if transposed:
                    g = g.T
                g_ref[...] = g
                _adamw_store(g, w_ref[...], m_ref[...], v_ref[...], dl_ref, nm_ref, nv_ref)

    def turn(k):
        return lambda j, i: jnp.where(j == k, i, jnp.where(j < k, 0, steps - 1))

    in_specs, out_specs = [], []
    for k in range(count):
        at = turn(k)
        blk = pl.BlockSpec(ws[0].shape if transposed else (rb, n), lambda j, i, at=at: (at(j, i), 0))
        in_specs += [pl.BlockSpec((rb, n), lambda j, i, at=at: (at(j, i), 0)),
                     pl.BlockSpec((recvs[k].shape[0], rb, n), lambda j, i, at=at: (0, at(j, i), 0)), blk, blk, blk]
        out_specs += [blk] * 4
    res = pl.pallas_call(
        body, name="adamw_" + name, grid=(count, steps), in_specs=in_specs, out_specs=out_specs,
        out_shape=[SDS(ws[0].shape, F32)] * (4 * count), compiler_params=_params(("arbitrary", "arbitrary")),
    )(*[a for k in range(count) for a in (owns[k], recvs[k], ws[k], ms[k], vs[k])])
    return [list(res[j::4]) for j in range(4)]


def _row_spec(tm, width):
    return pl.BlockSpec((tm, width), lambda i: (i, 0))


def _full_spec(shape):
    return pl.BlockSpec(shape, lambda i: (0,) * len(shape))


_acc_spec = _full_spec


def _sub_rows(tm):
    step = min(SUB_ROWS, tm)
    return [(lo, lo + step) for lo in range(0, tm, step)]


def _in_proj(x2, wt_in):
    t, d = x2.shape
    u_w = wt_in.shape[0]
    tm = _row_tile(t, MATMUL_ROWS)

    def body(x_ref, w_ref, u_ref, xb_ref):
        xb = x_ref[...].astype(BF16)
        xb_ref[...] = xb
        u_ref[...] = _dot_nt(xb, w_ref[...]).astype(BF16)

    return pl.pallas_call(
        body, name="in_proj", grid=(t // tm,),
        in_specs=[_row_spec(tm, d), _full_spec(wt_in.shape)],
        out_specs=[_row_spec(tm, u_w), _row_spec(tm, d)],
        out_shape=[SDS((t, u_w), BF16), SDS((t, d), BF16)],
        compiler_params=_params(("parallel",)),
    )(x2, wt_in)


def _col_halves(f):
    n = f // LANES
    k = (n + 1) // 2 * LANES
    return [(0, k), (k, f)] if k < f else [(0, f)]


def _mix_ln1_ffn_up(r, a, x2, w_out, wt_gate, wt_up, g1, b1):
    t, d = x2.shape
    f = wt_gate.shape[0]
    tm = _row_tile(t, EPILOGUE_ROWS)

    def body(r_ref, a_ref, x_ref, wo_ref, wg_ref, wu_ref, g_ref, b_ref, xh_ref, rs_ref, hb_ref, dg_ref, du_ref,
             act_ref):
        mix = _dot_nn(r_ref[...], wo_ref[0:RET_W, :]) + _dot_nn(a_ref[...], wo_ref[RET_W:RET_W + ATT_W, :])
        z = ALPHA * x_ref[...] + mix
        xhat, rstd = _layer_norm_stats(z)
        xh_ref[...] = xhat
        rs_ref[...] = jnp.broadcast_to(rstd, rs_ref.shape)
        h = (xhat * g_ref[...] + b_ref[...]).astype(BF16)
        hb_ref[...] = h
        g = _dot_nt(h, wg_ref[...])
        u = _dot_nt(h, wu_ref[...])
        sg = _sigmoid(g)
        silu = g * sg
        dg_ref[...] = (u * (sg * (1.0 + g * (1.0 - sg)))).astype(BF16)
        du_ref[...] = silu.astype(BF16)
        act_ref[...] = (silu * u).astype(BF16)

    wide, narrow = _row_spec(tm, f), _row_spec(tm, d)
    return pl.pallas_call(
        body, name="mix_ln1_ffn_up", grid=(t // tm,),
        in_specs=[_row_spec(tm, RET_W), _row_spec(tm, ATT_W), narrow, _resident_spec(w_out.shape),
                  _resident_spec(wt_gate.shape), _resident_spec(wt_up.shape), _full_spec(g1.shape),
                  _full_spec(b1.shape)],
        out_specs=[narrow, _row_spec(tm, LANES), narrow, wide, wide, wide],
        out_shape=[SDS((t, d), F32), SDS((t, LANES), F32), SDS((t, d), BF16)] + [SDS((t, f), BF16)] * 3,
        compiler_params=_params(("parallel",)),
    )(r, a, x2, w_out, wt_gate, wt_up, g1, b1)


def _ffn_down_ln2_loss(act, dact_dg, dact_du, h1b, p2, xhat1, target, w_down, w_pg, wt_pe, g1, b1, g2, b2):
    t, d = xhat1.shape
    f = act.shape[1]
    pdim = p2.shape[1]
    tm = _row_tile(t, EPILOGUE_ROWS)

    def body(act_ref, fg_ref, fu_ref, hb_ref, p_ref, xh1_ref, tgt_ref, wd_ref, wpg_ref, wpe_ref, g1_ref, b1_ref,
             g2_ref, b2_ref, dz_ref, dzb_ref, ds_ref, dple_ref, dg_ref, du_ref, acc_ref):
        @pl.when(pl.program_id(0) == 0)
        def _():
            acc_ref[...] = jnp.zeros_like(acc_ref)

        for lo, hi in _sub_rows(tm):
            h1 = xh1_ref[lo:hi, :] * g1_ref[...] + b1_ref[...]
            pg = _sigmoid(_dot_nn(hb_ref[lo:hi, :], wpg_ref[...]))
            ple = _dot_nt(p_ref[lo:hi, :].astype(BF16), wpe_ref[...])
            gated = pg * ple
            dgate = gated * (1.0 - pg)
            ffn = _dot_nn(act_ref[lo:hi, :], wd_ref[...])
            z2 = ALPHA * h1 + gated + ffn
            xhat2, rstd2 = _layer_norm_stats(z2)
            err = xhat2 * g2_ref[...] + b2_ref[...] - tgt_ref[lo:hi, :]
            dy = err * (1.0 / d)
            dz = _layer_norm_bwd(dy * g2_ref[...], xhat2, rstd2)
            dzb = dz.astype(BF16)
            dz_ref[lo:hi, :] = dz
            dzb_ref[lo:hi, :] = dzb
            ds_ref[lo:hi, :] = (dz * dgate).astype(BF16)
            dple_ref[lo:hi, :] = (dz * pg).astype(BF16)
            acc_ref[0:1, :] += jnp.sum(err * err, axis=0, keepdims=True)
            acc_ref[1:2, :] += jnp.sum(dy * xhat2, axis=0, keepdims=True)
            acc_ref[2:3, :] += jnp.sum(dy, axis=0, keepdims=True)
            for c0, c1 in _col_halves(f):
                da = _dot_nt(dzb, wd_ref[c0:c1, :])
                dg_ref[lo:hi, c0:c1] = (da * fg_ref[lo:hi, c0:c1].astype(F32)).astype(BF16)
                du_ref[lo:hi, c0:c1] = (da * fu_ref[lo:hi, c0:c1].astype(F32)).astype(BF16)

    vec = _full_spec(g1.shape)
    wide, narrow = _row_spec(tm, f), _row_spec(tm, d)
    return pl.pallas_call(
        body, name="ffn_down_ln2_loss", grid=(t // tm,),
        in_specs=[wide, wide, wide, narrow, _row_spec(tm, pdim), narrow, narrow,
                  _full_spec(w_down.shape), _full_spec(w_pg.shape), _full_spec(wt_pe.shape), vec, vec, vec, vec],
        out_specs=[narrow] * 4 + [wide, wide, _acc_spec((8, d))],
        out_shape=[SDS((t, d), F32), SDS((t, d), BF16), SDS((t, d), BF16), SDS((t, d), BF16),
                   SDS((t, f), BF16), SDS((t, f), BF16), SDS((8, d), F32)],
        compiler_params=_params(("arbitrary",)),
    )(act, dact_dg, dact_du, h1b, p2, xhat1, target, w_down, w_pg, wt_pe, g1, b1, g2, b2)


def _after(after, body):
    k = len(after)
    return (lambda *refs: body(*refs[k:])), [ANY_SPEC] * k


def _resident_spec(shape):
    return pl.BlockSpec(shape, lambda i: (0,) * len(shape), pipeline_mode=pl.Buffered(1))


def _dh1_ln1_bwd(dz2, dg, dup, dsb, xhat1, rstd1, wt_gate, wt_up, w_pg, w_out, g1, after=()):
    t, d = dz2.shape
    f = dg.shape[1]
    tm = _row_tile(t, MATMUL_ROWS)

    def body(dz_ref, dg_ref, du_ref, ds_ref, xh1_ref, rs1_ref, wg_ref, wu_ref, wpg_ref, wo_ref, g1_ref,
             dz1_ref, dz1b_ref, dr_ref, da_ref, acc_ref):
        @pl.when(pl.program_id(0) == 0)
        def _():
            acc_ref[...] = jnp.zeros_like(acc_ref)

        for lo, hi in _sub_rows(tm):
            dh = (ALPHA * dz_ref[lo:hi, :] + _dot_nn(dg_ref[lo:hi, :], wg_ref[...])
                  + _dot_nn(du_ref[lo:hi, :], wu_ref[...]) + _dot_nt(ds_ref[lo:hi, :], wpg_ref[...]))
            xhat, rstd = xh1_ref[lo:hi, :], rs1_ref[lo:hi, 0:1]
            dz1 = _layer_norm_bwd(dh * g1_ref[...], xhat, rstd)
            dz1b = dz1.astype(BF16)
            dz1_ref[lo:hi, :] = dz1
            dz1b_ref[lo:hi, :] = dz1b
            acc_ref[0:1, :] += jnp.sum(dh * xhat, axis=0, keepdims=True)
            acc_ref[1:2, :] += jnp.sum(dh, axis=0, keepdims=True)
            dr_ref[lo:hi, :] = _dot_nt(dz1b, wo_ref[0:RET_W, :]).astype(BF16)
            da_ref[lo:hi, :] = _dot_nt(dz1b, wo_ref[RET_W:RET_W + ATT_W, :]).astype(BF16)

    body, lead = _after(after, body)
    return pl.pallas_call(
        body, name="dh1_ln1_bwd", grid=(t // tm,),
        in_specs=lead + [_row_spec(tm, d), _row_spec(tm, f), _row_spec(tm, f), _row_spec(tm, d), _row_spec(tm, d),
                         _row_spec(tm, LANES), _resident_spec(wt_gate.shape), _resident_spec(wt_up.shape), _resident_spec(w_pg.shape),
                         _resident_spec(w_out.shape), _full_spec(g1.shape)],
        out_specs=[_row_spec(tm, d), _row_spec(tm, d), _row_spec(tm, RET_W), _row_spec(tm, ATT_W), _acc_spec((8, d))],
        out_shape=[SDS((t, d), F32), SDS((t, d), BF16), SDS((t, RET_W), BF16), SDS((t, ATT_W), BF16),
                   SDS((8, d), F32)],
        compiler_params=_params(("arbitrary",)),
    )(*after, dz2, dg, dup, dsb, xhat1, rstd1, wt_gate, wt_up, w_pg, w_out, g1)


def _in_proj_bwd(dz1, parts, wt_in, after=()):
    t, d = dz1.shape
    tm = _row_tile(t, MATMUL_ROWS)
    widths = [p.shape[1] for p in parts]

    def body(*refs):
        dz_ref, part_refs, w_ref, dx_ref = refs[0], refs[1:1 + len(parts)], refs[-2], refs[-1]
        acc = ALPHA * dz_ref[...]
        lo = 0
        for p_ref, w in zip(part_refs, widths):
            acc = acc + _dot_nn(p_ref[...], w_ref[lo:lo + w, :])
            lo += w
        dx_ref[...] = acc

    body, lead = _after(after, body)
    return pl.pallas_call(
        body, name="in_proj_bwd", grid=(t // tm,),
        in_specs=lead + [_row_spec(tm, d)] + [_row_spec(tm, w) for w in widths] + [_full_spec(wt_in.shape)],
        out_specs=_row_spec(tm, d), out_shape=SDS((t, d), F32),
        compiler_params=_params(("parallel",)),
    )(*after, dz1, *parts, wt_in)


def _weight_grad(name, me, parts, rhs, after=()):
    t, n = rhs.shape
    widths = [p.shape[1] for p in parts]
    rows = sum(widths)
    own_rows = rows // N_DEV
    tk = _row_tile(t, MATMUL_ROWS)
    n_steps = t // tk
    step = 256

    def body(*refs):
        me_ref, part_refs, rhs_ref = refs[0], refs[1:1 + len(parts)], refs[1 + len(parts)]
        full_ref, own_ref, acc = refs[-3], refs[-2], refs[-1]
        i = pl.program_id(0)

        def products(first):
            b = rhs_ref[...].astype(BF16)
            lo = 0
            for p_ref, w in zip(part_refs, widths):
                for c0 in range(0, w, step):
                    c1 = min(c0 + step, w)
                    val = _dot_tn(p_ref[:, c0:c1].astype(BF16), b)
                    if first:
                        acc[lo + c0:lo + c1, :] = val
                    else:
                        acc[lo + c0:lo + c1, :] += val
                lo += w

        pl.when(i == 0)(functools.partial(products, True))
        pl.when(i > 0)(functools.partial(products, False))

        @pl.when(i == n_steps - 1)
        def _():
            full_ref[...] = acc[...].astype(BF16)
            own_ref[...] = acc[pl.ds(pl.multiple_of(me_ref[0] * own_rows, 8), own_rows), :]

    body, lead = _after(after, body)
    return pl.pallas_call(
        body, name=name, grid=(n_steps,),
        in_specs=lead + [_smem_spec()] + [_row_spec(tk, w) for w in widths] + [_row_spec(tk, n)],
        out_specs=[_full_spec((rows, n)), _full_spec((own_rows, n))],
        out_shape=[SDS((rows, n), BF16), SDS((own_rows, n), F32)],
        scratch_shapes=[pltpu.VMEM((rows, n), F32)],
        compiler_params=_params(("arbitrary",)),
    )(*after, me, *parts, rhs)


def _weight_grad_jobs(name, me, jobs, after=()):
    count = len(jobs)
    t = jobs[0][0].shape[0]
    tk = _row_tile(t, MATMUL_ROWS)
    n_steps = t // tk
    shapes = [(lhs.shape[1], rhs.shape[1]) for lhs, rhs in jobs]
    most_rows, most_cols = max(r for r, _ in shapes), max(n for _, n in shapes)
    step = 256

    def body(*refs):
        me_ref, lhs_refs, rhs_refs = refs[0], refs[1:1 + count], refs[1 + count:1 + 2 * count]
        full_refs, own_refs = refs[1 + 2 * count:1 + 3 * count], refs[1 + 3 * count:1 + 4 * count]
        acc, whole, mine, sems = refs[1 + 4 * count:]
        job, i = pl.program_id(0), pl.program_id(1)

        def leaving(j):
            rows, n = shapes[j]
            return (pltpu.make_async_copy(whole.at[0:rows, 0:n], full_refs[j], sems.at[0]),
                    pltpu.make_async_copy(mine.at[0:rows // N_DEV, 0:n], own_refs[j], sems.at[1]))

        def products(j, first):
            rows, n = shapes[j]
            b = rhs_refs[j][...].astype(BF16)
            for c0 in range(0, rows, step):
                c1 = min(c0 + step, rows)
                val = _dot_tn(lhs_refs[j][:, c0:c1].astype(BF16), b)
                if first:
                    acc[c0:c1, 0:n] = val
                else:
                    acc[c0:c1, 0:n] += val

        def finish(j):
            rows, n = shapes[j]
            own_rows = rows // N_DEV
            if j > 0:
                for cp in leaving(j - 1):
                    cp.wait()
            whole[0:rows, 0:n] = acc[0:rows, 0:n].astype(BF16)
            mine[0:own_rows, 0:n] = acc[pl.ds(pl.multiple_of(me_ref[0] * own_rows, 8), own_rows), 0:n]
            for cp in leaving(j):
                cp.start()
            if j == count - 1:
                for cp in leaving(j):
                    cp.wait()

        for j in range(count):
            pl.when((job == j) & (i == 0))(functools.partial(products, j, True))
            pl.when((job == j) & (i > 0))(functools.partial(products, j, False))
            pl.when((job == j) & (i == n_steps - 1))(functools.partial(finish, j))

    def turn(j):
        return lambda job, i: (jnp.where(job == j, i, jnp.where(job < j, 0, n_steps - 1)), 0)

    body, lead = _after(after, body)
    res = pl.pallas_call(
        body, name=name, grid=(count, n_steps),
        in_specs=lead + [_smem_spec()] + [pl.BlockSpec((tk, rows), turn(j)) for j, (rows, _) in enumerate(shapes)]
        + [pl.BlockSpec((tk, n), turn(j)) for j, (_, n) in enumerate(shapes)],
        out_specs=[ANY_SPEC] * (2 * count),
        out_shape=[SDS((rows, n), BF16) for rows, n in shapes] + [SDS((rows // N_DEV, n), F32) for rows, n in shapes],
        scratch_shapes=[pltpu.VMEM((most_rows, most_cols), F32), pltpu.VMEM((most_rows, most_cols), BF16),
                        pltpu.VMEM((most_rows // N_DEV, most_cols), F32), pltpu.SemaphoreType.DMA((2,))],
        compiler_params=_params(("arbitrary", "arbitrary")),
    )(*after, me, *[lhs for lhs, _ in jobs], *[rhs for _, rhs in jobs])
    return list(res[:count]), list(res[count:])


def _log_decay(decay_f, decay_b):
    def body(f_ref, b_ref, lf_ref, lb_ref):
        lf_ref[...] = jnp.log1p(-jnp.exp2(f_ref[...]))
        lb_ref[...] = jnp.log1p(-jnp.exp2(b_ref[...]))

    return pl.pallas_call(body, name="log_decay", out_shape=[SDS(decay_f.shape, F32)] * 2)(decay_f, decay_b)


def _chunk(ref, n):
    return ref[pl.ds(pl.multiple_of(n * CHUNK, CHUNK), CHUNK), :]


def _group_sum(is_a, v):
    sa = jnp.sum(jnp.where(is_a, v, 0.0), axis=1, keepdims=True)
    sb = jnp.sum(jnp.where(is_a, 0.0, v), axis=1, keepdims=True)
    return jnp.where(is_a, sa, sb)


def _seq_spec(s, col_block):
    return pl.BlockSpec((s, LANES), lambda b, h: (b, col_block + h))


def _smem_spec():
    return pl.BlockSpec(memory_space=pltpu.SMEM)


RET_UNROLL = 4
PREP_UNROLL = 8


def _chunk_loop(n_chunks, body, init):
    u = PREP_UNROLL if n_chunks % PREP_UNROLL == 0 else 1

    def trip(i, carry):
        for j in range(u):
            carry = body(i * u + j, carry)
        return carry

    return lax.fori_loop(0, n_chunks // u, trip, init)


def _stacked_tables(lgf_ref, lgb_ref, pair):
    lane = lax.broadcasted_iota(jnp.int32, (1, LANES), 1)
    is_a = lane < HEAD_DIM
    lgf = jnp.where(is_a, lgf_ref[2 * pair], lgf_ref[2 * pair + 1])
    lgb = jnp.where(is_a, lgb_ref[2 * pair], lgb_ref[2 * pair + 1])
    row = lax.broadcasted_iota(jnp.int32, (CHUNK, 1), 0).astype(F32)
    kdec_f, qdec_f = jnp.exp(lgf * (CHUNK - 1.0 - row)), jnp.exp(lgf * (row + 1.0))
    kdec_b, qdec_b = jnp.exp(lgb * row), jnp.exp(lgb * (CHUNK - row))
    tab = dict(
        is_a=is_a, row=row, lam_f=jnp.exp(lgf * CHUNK), lam_b=jnp.exp(lgb * CHUNK),
        kdec=jnp.concatenate([kdec_f, kdec_b], axis=1), qdec=jnp.concatenate([qdec_f, qdec_b], axis=1),
        qexp=jnp.concatenate([jnp.broadcast_to(row + 1.0, (CHUNK, LANES)),
                              jnp.broadcast_to(CHUNK - row, (CHUNK, LANES))], axis=1),
        kexp=jnp.concatenate([jnp.broadcast_to(CHUNK - 1.0 - row, (CHUNK, LANES)),
                              jnp.broadcast_to(row, (CHUNK, LANES))], axis=1),
    )
    r = lax.broadcasted_iota(jnp.int32, (2 * LANES, LANES), 0)
    c = lax.broadcasted_iota(jnp.int32, (2 * LANES, LANES), 1)
    tab["diag2"] = ((r & (LANES - 1)) < HEAD_DIM) == (c < HEAD_DIM)
    i2 = lax.broadcasted_iota(jnp.int32, (2 * CHUNK, CHUNK), 0)
    j = lax.broadcasted_iota(jnp.int32, (2 * CHUNK, CHUNK), 1)
    head_b = i2 >= CHUNK
    diff = ((i2 & (CHUNK - 1)) - j).astype(F32)
    up, dn = jnp.maximum(diff, 0.0), jnp.maximum(-diff, 0.0)
    lgf2 = jnp.where(head_b, lgf_ref[2 * pair + 1], lgf_ref[2 * pair])
    lgb2 = jnp.where(head_b, lgb_ref[2 * pair + 1], lgb_ref[2 * pair])
    ef = jnp.where(diff >= 0, jnp.exp(lgf2 * up), 0.0)
    eb = jnp.where(diff <= 0, jnp.exp(lgb2 * dn), 0.0)
    tab["d2"] = ef + eb
    tab["df2"] = ef * up
    tab["db2"] = eb * dn
    return tab


def _stack_pair(is_a, x):
    zero = jnp.zeros_like(x)
    return jnp.concatenate([jnp.where(is_a, x, zero), jnp.where(is_a, zero, x)], axis=0)


def _unstack_pair(is_a, x2):
    return jnp.where(is_a, x2[0:CHUNK, :], x2[CHUNK:2 * CHUNK, :])


def _both_ways(x, dec):
    return (jnp.concatenate([x, x], axis=1) * dec).astype(BF16)


def _scan_states(n_chunks, st, up_rows, up_lam, down_rows, down_lam):
    zero = jnp.zeros((LANES, LANES), F32)

    def up(n, r):
        new = st[n, up_rows, :]
        st[n, up_rows, :] = r
        return r * up_lam + new

    def down(s, r):
        n = n_chunks - 1 - s
        new = st[n, down_rows, :]
        st[n, down_rows, :] = r
        return r * down_lam + new

    lax.fori_loop(0, n_chunks, up, zero)
    lax.fori_loop(0, n_chunks, down, zero)


FWD_ROWS, BWD_ROWS = pl.ds(0, LANES), pl.ds(LANES, LANES)


def _state_spec(n_chunks, pairs):
    return pl.BlockSpec((n_chunks, 2 * LANES, LANES), lambda b, h: (b * pairs + h, 0, 0))


ST_GAIN, ST_XF, ST_XB, ST_IFA, ST_IFB, ST_IBA, ST_IBB, ST_LF, ST_LB = 0, 1, 2, 3, 4, 5, 6, 8, 9
ST_ROWS = 16


GW = GROUP * HEAD_DIM
KEYS = 3 * BLOCK


def _attn_tables(g, bias_ref):
    r = lax.broadcasted_iota(jnp.int32, (GROUP * BLOCK, KEYS), 0)
    kj = lax.broadcasted_iota(jnp.int32, (GROUP * BLOCK, KEYS), 1)
    qi = r & (BLOCK - 1)
    hh = lax.shift_right_logical(r, 7)
    dist = jnp.abs(kj - BLOCK - qi)
    slope = jnp.exp2(-(GROUP * g + hh + 1).astype(F32) * (8.0 / ATTN_HEADS))
    inside = jnp.where(dist <= BLOCK, -slope * dist.astype(F32), NEG_INF)
    bias_ref[BIAS_INSIDE] = inside
    bias_ref[BIAS_FIRST] = jnp.where(kj >= BLOCK, inside, NEG_INF)
    bias_ref[BIAS_LAST] = jnp.where(kj < 2 * BLOCK, inside, NEG_INF)


BIAS_INSIDE, BIAS_FIRST, BIAS_LAST = 0, 1, 2


def _own_lanes(g):
    return lax.shift_right_logical(lax.broadcasted_iota(jnp.int32, (1, LANES), 1), 6) == g


def _mask_keys(x_ref, g, scale, pad_ref, s):
    pad_ref[0:BLOCK, :] = jnp.zeros((BLOCK, LANES), BF16)
    pad_ref[BLOCK + s:2 * BLOCK + s, :] = jnp.zeros((BLOCK, LANES), BF16)
    pad_ref[BLOCK:BLOCK + s, :] = jnp.where(_own_lanes(g), x_ref[...].astype(F32) * scale, 0.0).astype(BF16)


def _lane_block(x, j):
    return x[:, j * LANES:(j + 1) * LANES]


def _stack_heads(x, g):
    assert GROUP == 4 and GW == 2 * LANES
    x1 = pltpu.roll(x, HEAD_DIM, 1)
    keep = _own_lanes(g)
    zero = jnp.zeros((BLOCK, LANES), x.dtype)
    rows = []
    for h in range(GROUP):
        for_g0 = _lane_block(x, h // 2) if h % 2 == 0 else _lane_block(x1, ((h + 1) // 2) % 2)
        for_g1 = _lane_block(x, h // 2) if h % 2 == 1 else _lane_block(x1, h // 2)
        rows.append(jnp.where(keep, jnp.where(g == 0, for_g0, for_g1), zero))
    return jnp.concatenate(rows, axis=0)


def _unstack_heads(x4, g):
    p = [x4[h * BLOCK:(h + 1) * BLOCK, :] for h in range(GROUP)]
    cat = lambda a, b: jnp.concatenate([a, b], axis=1)
    in_place = jnp.where(g == 0, cat(p[0], p[2]), cat(p[1], p[3]))
    one_left = jnp.where(g == 0, cat(p[1], p[3]), cat(p[2], p[0]))
    return in_place + pltpu.roll(one_left, HEAD_DIM, 1)


def _sink_column(sink_ref, g):
    rh = lax.shift_right_logical(lax.broadcasted_iota(jnp.int32, (GROUP * BLOCK, 1), 0), 7)
    col = jnp.zeros((GROUP * BLOCK, 1), F32)
    for h in range(GROUP):
        col = jnp.where(rh == h, sink_ref[GROUP * g + h], col)
    return col


def _attn_probs(qm, k3, bias_ref, sink_col, n, s):
    which = jnp.where(n == 0, BIAS_FIRST, jnp.where(n == s // BLOCK - 1, BIAS_LAST, BIAS_INSIDE))
    logits = _dot_nt(qm, k3) + bias_ref[which]
    m = jnp.maximum(jnp.max(logits, axis=1, keepdims=True), sink_col)
    e = jnp.exp(logits - m)
    e_sink = jnp.exp(sink_col - m)
    inv = 1.0 / (jnp.sum(e, axis=1, keepdims=True) + e_sink)
    return e * inv, e_sink * inv


PAIRS_PER_KV = (RET_HEADS // 2) // KV_HEADS
FWD_ORDER = "rrarra"
BWD_ORDER = "rararr"


def _trip_order(order, chunks, blocks):
    if order.count("r") == chunks and order.count("a") == blocks:
        return order
    return "r" * chunks + "a" * blocks


def _mixers_fwd(u, lgf, lgb, gn_gain, sink, b_loc, after=()):
    t = u.shape[0]
    s = t // b_loc
    n_chunks = s // CHUNK
    pairs = RET_HEADS // 2
    trips = n_chunks // RET_UNROLL
    blocks_half = (s // BLOCK) // PAIRS_PER_KV
    per_trip = blocks_half // trips
    assert n_chunks % RET_UNROLL == 0 and blocks_half % trips == 0 and PAIRS_PER_KV == 2 and s >= 2 * BLOCK

    def body(lgf_ref, lgb_ref, sink_ref, q_ref, k_ref, v_ref, g_ref, gain_ref, aq_ref, ak_ref, av_ref,
             r_ref, xhat_ref, rstd_ref, a_ref, st, kpad, vpad, bias):
        pair = pl.program_id(1)
        g, half = lax.shift_right_logical(pair, 1), pair & 1
        tab = _stacked_tables(lgf_ref, lgb_ref, pair)
        is_a = tab["is_a"]

        @pl.when(half == 0)
        def _():
            _attn_tables(g, bias)
            _mask_keys(ak_ref, g, Q_SCALE, kpad, s)
            _mask_keys(av_ref, g, 1.0, vpad, s)

        def kv_body(n, _):
            k8 = _chunk(k_ref, n).astype(F32) * Q_SCALE
            st[n] = jnp.where(tab["diag2"], _dot_tn(_both_ways(k8, tab["kdec"]), _chunk(v_ref, n)), 0.0)
            return 0

        _chunk_loop(n_chunks, kv_body, 0)
        _scan_states(n_chunks, st, FWD_ROWS, tab["lam_f"], BWD_ROWS, tab["lam_b"])
        sink_col = _sink_column(sink_ref, g)

        def retention_chunk(n):
            q = _chunk(q_ref, n)
            k8 = (_chunk(k_ref, n).astype(F32) * Q_SCALE).astype(BF16)
            v = _chunk(v_ref, n)
            p2 = (_dot_nt(_stack_pair(is_a, q), k8) * tab["d2"]).astype(BF16)
            y = _unstack_pair(is_a, _dot_nn(p2, v))
            y = y + _dot_nn(_both_ways(q.astype(F32), tab["qdec"]), st[n].astype(BF16))
            rows = pl.ds(pl.multiple_of(n * CHUNK, CHUNK), CHUNK)
            mu = _group_sum(is_a, y) * (1.0 / HEAD_DIM)
            dlt = y - mu
            var = _group_sum(is_a, dlt * dlt) * (1.0 / HEAD_DIM)
            rstd = lax.rsqrt(var + GN_EPS)
            xhat = dlt * rstd
            xhat_ref[rows, :] = xhat
            rstd_ref[rows, :] = rstd
            gate = _chunk(g_ref, n).astype(F32)
            r_ref[rows, :] = (xhat * gain_ref[...] * gate * _sigmoid(gate)).astype(BF16)

        def attention_block(blk):
            n = half * blocks_half + blk
            rows = pl.ds(pl.multiple_of(blk * BLOCK, BLOCK), BLOCK)
            keys = pl.ds(pl.multiple_of(n * BLOCK, BLOCK), KEYS)
            p, _ = _attn_probs(_stack_heads(aq_ref[rows, :], g), kpad[keys, :], bias, sink_col, n, s)
            a_ref[rows, :] = _unstack_heads(_dot_nn(p.astype(BF16), vpad[keys, :]), g).astype(BF16)

        def trip(i, _):
            chunk, blk = 0, 0
            for kind in _trip_order(FWD_ORDER, RET_UNROLL, per_trip):
                if kind == "r":
                    retention_chunk(i * RET_UNROLL + chunk)
                    chunk += 1
                else:
                    attention_block(i * per_trip + blk)
                    blk += 1
            return 0

        lax.fori_loop(0, trips, trip, 0)

    lane_blk = lambda c0: _seq_spec(s, c0 // LANES)
    half_rows = blocks_half * BLOCK
    aq_spec = pl.BlockSpec((half_rows, GW), lambda b, h: (b * PAIRS_PER_KV + (h & 1), C_AQ // GW + h // 2))
    a_spec = pl.BlockSpec((half_rows, GW), lambda b, h: (b * PAIRS_PER_KV + (h & 1), h // 2))
    kv_spec = lambda c0: pl.BlockSpec((s, LANES), lambda b, h: (b, c0 // LANES))
    pad = pltpu.VMEM((s + 2 * BLOCK, LANES), BF16)
    body, lead = _after(after, body)
    return pl.pallas_call(
        body, name="mixers_fwd", grid=(b_loc, pairs),
        in_specs=lead + [_smem_spec(), _smem_spec(), _smem_spec(), lane_blk(C_RQ), lane_blk(C_RK), lane_blk(C_RV),
                         lane_blk(C_RG), pl.BlockSpec((1, LANES), lambda b, h: (0, h)), aq_spec, kv_spec(C_AK),
                         kv_spec(C_AV)],
        out_specs=[_seq_spec(s, 0), _seq_spec(s, 0), _seq_spec(s, 0), a_spec, _state_spec(n_chunks, pairs)],
        out_shape=[SDS((t, RET_W), BF16), SDS((t, RET_W), F32), SDS((t, RET_W), F32), SDS((t, ATT_W), BF16),
                   SDS((b_loc * pairs * n_chunks, 2 * LANES, LANES), F32)],
        scratch_shapes=[pad, pad, pltpu.VMEM((3, GROUP * BLOCK, KEYS), F32)],
        compiler_params=_params(("arbitrary", "arbitrary")),
    )(*after, lgf, lgb, sink, u, u, u, u, gn_gain, u, u, u)


def _mixers_bwd(u, xhat, rstd, states, dr, da, lgf, lgb, gn_gain, sink, b_loc, after=()):
    t = u.shape[0]
    s = t // b_loc
    n_chunks = s // CHUNK
    pairs = RET_HEADS // 2
    trips = n_chunks // RET_UNROLL
    blocks_half = (s // BLOCK) // PAIRS_PER_KV
    per_trip = blocks_half // trips
    assert n_chunks % RET_UNROLL == 0 and blocks_half % trips == 0 and PAIRS_PER_KV == 2 and s >= 2 * BLOCK

    def body(lgf_ref, lgb_ref, sink_ref, q_ref, k_ref, v_ref, g_ref, xhat_ref, rstd_ref, dr_ref, gain_ref,
             aq_ref, ak_ref, av_ref, do_ref, st,
             dq_ref, dk_ref, dv_ref, dg_ref, st_ref, daq_ref, dak_ref, dav_ref, dsink_ref,
             gr, dy_s, kpad, vpad, bias, dk_acc, dv_acc):
        pair = pl.program_id(1)
        g, half = lax.shift_right_logical(pair, 1), pair & 1
        tab = _stacked_tables(lgf_ref, lgb_ref, pair)
        is_a = tab["is_a"]
        gain = gain_ref[...]

        @pl.when(half == 0)
        def _():
            _attn_tables(g, bias)
            _mask_keys(ak_ref, g, Q_SCALE, kpad, s)
            _mask_keys(av_ref, g, 1.0, vpad, s)
            dsink_ref[...] = jnp.zeros_like(dsink_ref)

        @pl.when(pair == 0)
        def _():
            dk_acc[...] = jnp.zeros_like(dk_acc)
            dv_acc[...] = jnp.zeros_like(dv_acc)

        def norm_body(n, dgain):
            rows = pl.ds(pl.multiple_of(n * CHUNK, CHUNK), CHUNK)
            xhat, rstd = xhat_ref[rows, :], rstd_ref[rows, :]
            gate = g_ref[rows, :].astype(F32)
            sg = _sigmoid(gate)
            silu = gate * sg
            d_out = dr_ref[rows, :].astype(F32)
            dg_ref[rows, :] = (d_out * xhat * gain * (sg * (1.0 + gate * (1.0 - sg)))).astype(BF16)
            dxh = d_out * gain * silu
            m1 = _group_sum(is_a, dxh) * (1.0 / HEAD_DIM)
            m2 = _group_sum(is_a, dxh * xhat) * (1.0 / HEAD_DIM)
            dy = (rstd * (dxh - m1 - xhat * m2)).astype(BF16)
            dy_s[rows, :] = dy
            qf = q_ref[rows, :].astype(F32)
            gr[n] = jnp.where(tab["diag2"], _dot_tn(_both_ways(qf, tab["qdec"]), dy), 0.0)
            return dgain + jnp.sum(d_out * xhat * silu, axis=0, keepdims=True)

        colsum = lambda x: jnp.sum(x, axis=0, keepdims=True)

        def grad_body(n, carry):
            xfb, ifa, ifb, iba, ibb, lf, lb = carry
            rows = pl.ds(pl.multiple_of(n * CHUNK, CHUNK), CHUNK)
            q = q_ref[rows, :]
            qf = q.astype(F32)
            k8f = k_ref[rows, :].astype(F32) * Q_SCALE
            k8 = k8f.astype(BF16)
            v = v_ref[rows, :]
            dy = dy_s[rows, :]
            q2, dy2 = _stack_pair(is_a, q), _stack_pair(is_a, dy)
            sc = _dot_nt(q2, k8)
            dp = _dot_nt(dy2, v)
            a2 = (sc * tab["d2"]).astype(BF16)
            ds2 = (dp * tab["d2"]).astype(BF16)
            dq = _unstack_pair(is_a, _dot_nn(ds2, k8))
            dk = _dot_tn(ds2, q2)
            dv = _dot_tn(a2, dy2)
            prod = sc * dp
            pf, pb = prod * tab["df2"], prod * tab["db2"]
            ifa, ifb = ifa + colsum(pf[0:CHUNK, :]), ifb + colsum(pf[CHUNK:2 * CHUNK, :])
            iba, ibb = iba + colsum(pb[0:CHUNK, :]), ibb + colsum(pb[CHUNK:2 * CHUNK, :])
            states, sgrads = st[n], gr[n]
            sb, gb = states.astype(BF16), sgrads.astype(BF16)
            dqc = _dot_nt(dy, sb) * tab["qdec"]
            dkc = _dot_nt(v, gb) * tab["kdec"]
            dv = dv + _dot_nn(_both_ways(k8f, tab["kdec"]), gb)
            dq_ref[rows, :] = (dq + dqc[:, 0:LANES] + dqc[:, LANES:2 * LANES]).astype(BF16)
            dk_ref[rows, :] = ((dk + dkc[:, 0:LANES] + dkc[:, LANES:2 * LANES]) * Q_SCALE).astype(BF16)
            dv_ref[rows, :] = dv.astype(BF16)
            q2w, k2w = jnp.concatenate([qf, qf], axis=1), jnp.concatenate([k8f, k8f], axis=1)
            xfb = xfb + colsum(tab["qexp"] * q2w * dqc + tab["kexp"] * k2w * dkc)
            prod_s = sgrads * states
            lf, lb = lf + colsum(prod_s[0:LANES, :]), lb + colsum(prod_s[LANES:2 * LANES, :])
            return xfb, ifa, ifb, iba, ibb, lf, lb

        sink_col = _sink_column(sink_ref, g)
        head_row = lax.broadcasted_iota(jnp.int32, dsink_ref.shape, 0)

        def attention_block(blk):
            n = half * blocks_half + blk
            rows = pl.ds(pl.multiple_of(blk * BLOCK, BLOCK), BLOCK)
            keys = pl.ds(pl.multiple_of(n * BLOCK, BLOCK), KEYS)
            qm = _stack_heads(aq_ref[rows, :], g)
            k3, v3 = kpad[keys, :], vpad[keys, :]
            p, p_sink = _attn_probs(qm, k3, bias, sink_col, n, s)
            dom = _stack_heads(do_ref[rows, :], g)
            dp = _dot_nt(dom, v3)
            delta = jnp.sum(p * dp, axis=1, keepdims=True)
            ds_mat = (p * (dp - delta)).astype(BF16)
            daq_ref[rows, :] = _unstack_heads(_dot_nn(ds_mat, k3), g).astype(BF16)
            dk_acc[keys, :] += _dot_tn(ds_mat, qm) * Q_SCALE
            dv_acc[keys, :] += _dot_tn(p.astype(BF16), dom)
            w = p_sink * delta
            upd = jnp.zeros(dsink_ref.shape, F32)
            for h in range(GROUP):
                upd = upd + jnp.where(head_row == h, -jnp.sum(w[h * BLOCK:(h + 1) * BLOCK, :]), 0.0)
            dsink_ref[...] += upd

        dgain = _chunk_loop(n_chunks, norm_body, jnp.zeros((1, LANES), F32))
        _scan_states(n_chunks, gr, BWD_ROWS, tab["lam_b"], FWD_ROWS, tab["lam_f"])

        def trip(i, carry):
            chunk, blk = 0, 0
            for kind in _trip_order(BWD_ORDER, RET_UNROLL, per_trip):
                if kind == "r":
                    carry = grad_body(i * RET_UNROLL + chunk, carry)
                    chunk += 1
                else:
                    attention_block(i * per_trip + blk)
                    blk += 1
            return carry

        z = jnp.zeros((1, LANES), F32)
        init = (jnp.zeros((1, 2 * LANES), F32), z, z, z, z, z, z)
        xfb, ifa, ifb, iba, ibb, lf, lb = lax.fori_loop(0, trips, trip, init)
        st_ref[...] = jnp.zeros_like(st_ref)
        st_ref[ST_GAIN:ST_GAIN + 1, :] = dgain
        st_ref[ST_XF:ST_XF + 1, :] = xfb[:, 0:LANES]
        st_ref[ST_XB:ST_XB + 1, :] = xfb[:, LANES:2 * LANES]
        st_ref[ST_IFA:ST_IFA + 1, :] = ifa
        st_ref[ST_IFB:ST_IFB + 1, :] = ifb
        st_ref[ST_IBA:ST_IBA + 1, :] = iba
        st_ref[ST_IBB:ST_IBB + 1, :] = ibb
        st_ref[ST_LF:ST_LF + 1, :] = lf * (CHUNK * tab["lam_f"])
        st_ref[ST_LB:ST_LB + 1, :] = lb * (CHUNK * tab["lam_b"])

        @pl.when(pair == pairs - 1)
        def _():
            dak_ref[...] = dk_acc[BLOCK:BLOCK + s, :].astype(BF16)
            dav_ref[...] = dv_acc[BLOCK:BLOCK + s, :].astype(BF16)

    lane_blk = lambda c0: _seq_spec(s, c0 // LANES)
    seq0 = _seq_spec(s, 0)
    half_rows = blocks_half * BLOCK
    aq_spec = pl.BlockSpec((half_rows, GW), lambda b, h: (b * PAIRS_PER_KV + (h & 1), C_AQ // GW + h // 2))
    a_spec = pl.BlockSpec((half_rows, GW), lambda b, h: (b * PAIRS_PER_KV + (h & 1), h // 2))
    kv_spec = lambda c0: pl.BlockSpec((s, LANES), lambda b, h: (b, c0 // LANES))
    kv_out = pl.BlockSpec((s, LANES), lambda b, h: (b, 0))
    state = pltpu.VMEM((n_chunks, 2 * LANES, LANES), F32)
    pad = pltpu.VMEM((s + 2 * BLOCK, LANES), BF16)
    acc = pltpu.VMEM((s + 2 * BLOCK, LANES), F32)
    body, lead = _after(after, body)
    return pl.pallas_call(
        body, name="mixers_bwd", grid=(b_loc, pairs),
        in_specs=lead + [_smem_spec(), _smem_spec(), _smem_spec(), lane_blk(C_RQ), lane_blk(C_RK), lane_blk(C_RV),
                         lane_blk(C_RG), seq0, seq0, seq0, pl.BlockSpec((1, LANES), lambda b, h: (0, h)),
                         aq_spec, kv_spec(C_AK), kv_spec(C_AV), a_spec, _state_spec(n_chunks, pairs)],
        out_specs=[seq0] * 4 + [pl.BlockSpec((ST_ROWS, LANES), lambda b, h: (b, h)), a_spec, kv_out, kv_out,
                                pl.BlockSpec((8, LANES), lambda b, h: (b * KV_HEADS + h // 2, 0))],
        out_shape=[SDS((t, RET_W), BF16)] * 4 + [SDS((b_loc * ST_ROWS, RET_W), F32), SDS((t, ATT_W), BF16),
                                                   SDS((t, KV_W), BF16), SDS((t, KV_W), BF16),
                                                   SDS((b_loc * KV_HEADS * 8, LANES), F32)],
        scratch_shapes=[state, pltpu.VMEM((s, LANES), BF16), pad, pad,
                        pltpu.VMEM((3, GROUP * BLOCK, KEYS), F32), acc, acc],
        compiler_params=_params(("arbitrary", "arbitrary")),
    )(*after, lgf, lgb, sink, u, u, u, u, xhat, rstd, dr, gn_gain, u, u, u, da, states)


def _pack_small(acc2, acc1, ret_stats, dsink, b_loc, d):
    pairs = RET_HEADS // 2

    def body(acc2_ref, acc1_ref, st_ref, dsink_ref, out_ref):
        out_ref[...] = jnp.zeros_like(out_ref)
        out_ref[ROW_LN1G:ROW_LN1G + 1, :] = acc1_ref[0:1, :]
        out_ref[ROW_LN1B:ROW_LN1B + 1, :] = acc1_ref[1:2, :]
        out_ref[ROW_LN2G:ROW_LN2G + 1, :] = acc2_ref[1:2, :]
        out_ref[ROW_LN2B:ROW_LN2B + 1, :] = acc2_ref[2:3, :]
        out_ref[ROW_LOSS:ROW_LOSS + 1, :] = acc2_ref[0:1, :]
        st = st_ref[0:ST_ROWS, :]
        for b in range(1, b_loc):
            st = st + st_ref[b * ST_ROWS:(b + 1) * ST_ROWS, :]
        out_ref[ROW_GN:ROW_GN + 1, 0:RET_W] = st[ST_GAIN:ST_GAIN + 1, :]
        lane = lax.broadcasted_iota(jnp.int32, (1, d), 1)
        misc = jnp.zeros((1, d), F32)
        for pr in range(pairs):
            blk = st[:, pr * LANES:(pr + 1) * LANES]
            half = lax.broadcasted_iota(jnp.int32, (1, LANES), 1) < HEAD_DIM
            for h in range(2):
                sel = half if h == 0 else jnp.logical_not(half)
                cross_f = jnp.sum(jnp.where(sel, blk[ST_XF:ST_XF + 1, :] + blk[ST_LF:ST_LF + 1, :], 0.0))
                cross_b = jnp.sum(jnp.where(sel, blk[ST_XB:ST_XB + 1, :] + blk[ST_LB:ST_LB + 1, :], 0.0))
                intra_f = jnp.sum(blk[ST_IFA + h:ST_IFA + h + 1, :])
                intra_b = jnp.sum(blk[ST_IBA + h:ST_IBA + h + 1, :])
                head = 2 * pr + h
                misc = jnp.where(lane == MISC_DF + head, cross_f + intra_f, misc)
                misc = jnp.where(lane == MISC_DB + head, cross_b + intra_b, misc)
        for g in range(KV_HEADS):
            tot = dsink_ref[g * 8:(g + 1) * 8, :]
            for b in range(1, b_loc):
                tot = tot + dsink_ref[(b * KV_HEADS + g) * 8:(b * KV_HEADS + g + 1) * 8, :]
            for h in range(GROUP):
                misc = jnp.where(lane == MISC_SINK + GROUP * g + h, jnp.sum(tot[h:h + 1, 0:1]), misc)
        out_ref[ROW_MISC:ROW_MISC + 1, :] = misc

    return pl.pallas_call(body, name="pack_small", out_shape=SDS((SMALL_ROWS, d), F32))(acc2, acc1, ret_stats, dsink)


BIG = ("w_in", "w_out", "w_ffn_gate", "w_ffn_up", "w_ffn_down", "w_ple_proj", "w_ple_gate")
TRANSPOSED_OUTSIDE = ("w_in", "w_ffn_gate", "w_ffn_up")
TRANSPOSED_HERE = ("w_ple_proj",)
SMALL = ("ret_decay_fwd", "ret_decay_bwd", "ret_gn_gain", "attn_sink", "ln1_gain", "ln1_bias", "ln2_gain", "ln2_bias")
ORDER = ("w_in", "ret_decay_fwd", "ret_decay_bwd", "ret_gn_gain", "attn_sink", "w_out", "ln1_gain", "ln1_bias",
         "w_ffn_gate", "w_ffn_up", "w_ffn_down", "w_ple_proj", "w_ple_gate", "ln2_gain", "ln2_bias")


GATHER_ORDER = ("w_in", "w_ffn_up", "w_out", "w_ffn_gate", "w_ple_gate", "w_ple_proj", "w_ffn_down")
GATHER_TWO_LEVEL = ("w_in", "w_ffn_up")

def _local_step(x2, p2, target2, fetch, publish, small, b_loc, me):
    d = x2.shape[1]
    lgf, lgb = _log_decay(small["ret_decay_fwd"], small["ret_decay_bwd"])
    lgf1, lgb1, sink1 = lgf.reshape(-1), lgb.reshape(-1), small["attn_sink"].reshape(-1)
    (w_in,) = fetch(("w_in",), ())
    u, xb = _in_proj(x2, w_in)
    passed = fetch.pass_on("w_ffn_up", (xb,))
    r, ret_xhat, ret_rstd, a, ret_states = _mixers_fwd(u, lgf1, lgb1, small["ret_gn_gain"], sink1, b_loc, passed)
    w_out, w_gate, w_up = fetch(("w_out", "w_ffn_gate", "w_ffn_up"), (r, a))
    xhat1, rstd1, h1b, dact_dg, dact_du, act = _mix_ln1_ffn_up(
        r, a, x2, w_out, w_gate, w_up, small["ln1_gain"], small["ln1_bias"])
    w_pg, w_pe, w_down = fetch(("w_ple_gate", "w_ple_proj", "w_ffn_down"), (act,))
    dz2, dz2b, dsb, dpleb, dg, dup, acc2 = _ffn_down_ln2_loss(
        act, dact_dg, dact_du, h1b, p2, xhat1, target2, w_down, w_pg, w_pe,
        small["ln1_gain"], small["ln1_bias"], small["ln2_gain"], small["ln2_bias"])
    own = {}

    def grad(name, parts, rhs, after=()):
        whole, own[name] = _weight_grad("grad_" + name, me, parts, rhs, after)
        return whole

    ffn_jobs = dict(w_ffn_down=(act, dz2b), w_ple_proj=(dpleb, p2), w_ple_gate=(h1b, dsb),
                    w_ffn_gate=(dg, h1b), w_ffn_up=(dup, h1b))
    wholes, owns = _weight_grad_jobs("grad_w_ffn", me, list(ffn_jobs.values()))
    own.update(zip(ffn_jobs, owns))
    t2 = publish("ffn", dict(zip(ffn_jobs, wholes)))
    dz1, dz1b, dr, da, acc1 = _dh1_ln1_bwd(
        dz2, dg, dup, dsb, xhat1, rstd1, w_gate, w_up, w_pg, w_out, small["ln1_gain"], t2)
    t3 = publish("out", dict(w_out=grad("w_out", [r, a], dz1b)))
    dq, dk, dv, dgate, ret_stats, daq, dak, dav, dsink = _mixers_bwd(
        u, ret_xhat, ret_rstd, ret_states, dr, da, lgf1, lgb1, small["ret_gn_gain"], sink1, b_loc, t3)
    parts = [dq, dk, dv, dgate, daq, dak, dav]
    small_part = _pack_small(acc2, acc1, ret_stats, dsink, b_loc, d)
    t4 = publish("in", dict(w_in=grad("w_in", parts, xb)), small_part)
    grad_x = _in_proj_bwd(dz1, parts, w_in, t4)
    return grad_x, own, small_part


def kernel(x, p, w_in, ret_decay_fwd, ret_decay_bwd, ret_gn_gain, attn_sink, w_out, ln1_gain, ln1_bias, w_ffn_gate, w_ffn_up, w_ffn_down, w_ple_proj, w_ple_gate, ln2_gain, ln2_bias, loss_target, m_w_in, m_ret_decay_fwd, m_ret_decay_bwd, m_ret_gn_gain, m_attn_sink, m_w_out, m_ln1_gain, m_ln1_bias, m_w_ffn_gate, m_w_ffn_up, m_w_ffn_down, m_w_ple_proj, m_w_ple_gate, m_ln2_gain, m_ln2_bias, v_w_in, v_ret_decay_fwd, v_ret_decay_bwd, v_ret_gn_gain, v_attn_sink, v_w_out, v_ln1_gain, v_ln1_bias, v_w_ffn_gate, v_w_ffn_up, v_w_ffn_down, v_w_ple_proj, v_w_ple_gate, v_ln2_gain, v_ln2_bias):
    given = dict(locals())

    def strip(n, a):
        if n not in BIG:
            return a
        return a[0].T if n in TRANSPOSED_OUTSIDE else a[0]

    def restore(n, a):
        if n not in BIG:
            return a
        return (a.T if n in TRANSPOSED_OUTSIDE else a)[None]

    w = {n: strip(n, given[n]) for n in ORDER}
    m = {n: strip(n, given["m_" + n]) for n in ORDER}
    v = {n: strip(n, given["v_" + n]) for n in ORDER}
    b_loc, s, d = x.shape
    x2 = x.reshape(b_loc * s, d)
    p2 = p[0].reshape(b_loc * s, p.shape[-1])
    target2 = loss_target.reshape(b_loc * s, d)

    small = {n: w[n] for n in SMALL}
    me = (4 * lax.axis_index("x") + 2 * lax.axis_index("y") + lax.axis_index("c")).astype(jnp.int32).reshape(1)

    gather = _gather_start(
        {n: w[n] for n in GATHER_ORDER},
        [_gather_copy_near if n in GATHER_TWO_LEVEL else _gather_copy for n in GATHER_ORDER])

    passing = {}

    def pass_on(n, after):
        passing[n] = _gather_relay("gather_relay_" + n, gather, GATHER_ORDER.index(n), list(after))
        return (passing[n]["token"],)

    def fetch(names, after):
        out = {}
        for n in [n for n in names if n in GATHER_TWO_LEVEL]:
            if n not in passing:
                pass_on(n, after)
            out[n] = _split_copy_wait("gather_wait_" + n, passing[n], [0], list(after))[0][0]
        direct = [n for n in names if n not in GATHER_TWO_LEVEL]
        if direct:
            got = _split_copy_wait("gather_wait_" + direct[0], gather, [GATHER_ORDER.index(n) for n in direct],
                                   list(after))
            out.update({n: item[0] for n, item in zip(direct, got)})
        return [out[n] for n in names]

    scatters = []

    def publish(tag, products, small_sums=None):
        items = [(products[n], lax.empty((N_DEV - 1, products[n].shape[0] // N_DEV, products[n].shape[1]), BF16))
                 for n in products]
        copies = [_scatter_copy] * len(items)
        if small_sums is not None:
            items.append((small_sums, lax.empty((N_DEV - 1,) + small_sums.shape, F32)))
            copies.append(_small_copy)
        started = _split_copy_start("scatter_start_" + tag, items, copies)
        scatters.append((list(products), small_sums is not None, started))
        return (started["token"],)

    fetch.pass_on = pass_on
    grad_x, own, small_part = _local_step(x2, p2, target2, fetch, publish, small, b_loc, me)

    out_g, out_d, out_m, out_v = {}, {}, {}, {}
    after = [grad_x]
    for names, with_small, started in scatters:
        landed = _split_copy_wait("scatter_wait_" + names[0], started, list(range(len(started["items"]))), after)
        if with_small:
            mine, from_peers = landed[-1]
            loss, sg, sd, sm, sv = _small_adamw(
                me, mine, from_peers, small, {n: m[n] for n in SMALL}, {n: v[n] for n in SMALL})
            for dst, src in ((out_g, sg), (out_d, sd), (out_m, sm), (out_v, sv)):
                dst.update(src)
        recv = {n: item[1] for n, item in zip(names, landed)}
        alike = {}
        for n in names:
            alike.setdefault((own[n].shape, n in TRANSPOSED_HERE), []).append(n)
        for (_, transposed), ns in alike.items():
            res = _reduce_adamw(ns[0], [own[n] for n in ns], [recv[n] for n in ns], [w[n] for n in ns],
                                [m[n] for n in ns], [v[n] for n in ns], transposed)
            for dst, vals in zip((out_g, out_d, out_m, out_v), res):
                dst.update(zip(ns, vals))
        after = [out_v[names[-1]]]

    outs = [loss[0, 0], grad_x.reshape(x.shape)]
    for group in (out_g, out_d, out_m, out_v):
        outs += [restore(n, group[n]) for n in ORDER]
    return tuple(outs)
```

```python
import functools

import jax
import jax.numpy as jnp
from jax import lax
from jax.experimental import pallas as pl
from jax.experimental.pallas import tpu as pltpu

F32, BF16 = jnp.float32, jnp.bfloat16
SDS = jax.ShapeDtypeStruct
MESH = pl.DeviceIdType.MESH

N_DEV = 8
HEAD_DIM = 64
RET_HEADS = 8
ATTN_HEADS = 8
KV_HEADS = 2
GROUP = ATTN_HEADS // KV_HEADS
RET_W = RET_HEADS * HEAD_DIM
ATT_W = ATTN_HEADS * HEAD_DIM
KV_W = KV_HEADS * HEAD_DIM
LANES = 128
CHUNK = 128
BLOCK = 128
Q_SCALE = HEAD_DIM ** -0.5
ALPHA = 2.0 ** 0.25
LN_EPS = 1e-5
GN_EPS = 1e-5
NEG_INF = -1e30
C_RQ, C_RK, C_RV, C_RG = 0, RET_W, 2 * RET_W, 3 * RET_W
C_AQ = 4 * RET_W
C_AK = C_AQ + ATT_W
C_AV = C_AK + KV_W
IN_W = C_AV + KV_W

ADAM_LR = 0.001
ADAM_B1 = 0.9
ADAM_B2 = 0.999
ADAM_EPS = 1e-08
ADAM_WD = 0.01
ADAM_STEP = 10

VMEM_LIMIT = 56 * 1024 * 1024
MATMUL_ROWS = 512
EPILOGUE_ROWS = 256
SUB_ROWS = 512
SMALL_ROWS = 16
ROW_LN1G, ROW_LN1B, ROW_LN2G, ROW_LN2B, ROW_LOSS, ROW_GN, ROW_MISC = 0, 1, 2, 3, 4, 5, 6
MISC_DF, MISC_DB, MISC_SINK = 0, 8, 16


def _dot_nn(a, b):
    return lax.dot_general(a, b, (((1,), (0,)), ((), ())), preferred_element_type=F32)


def _dot_nt(a, b):
    return lax.dot_general(a, b, (((1,), (1,)), ((), ())), preferred_element_type=F32)


def _dot_tn(a, b):
    return lax.dot_general(a, b, (((0,), (0,)), ((), ())), preferred_element_type=F32)


def _params(sem=None, vmem=VMEM_LIMIT):
    kw = {"vmem_limit_bytes": vmem}
    if sem is not None:
        kw["dimension_semantics"] = sem
    return pltpu.CompilerParams(**kw)


def _row_tile(t, want=512):
    tm = want
    while t % tm:
        tm //= 2
    return tm


def _sigmoid(x):
    return jax.nn.sigmoid(x)


def _layer_norm_stats(z):
    mu = jnp.mean(z, axis=1, keepdims=True)
    d = z - mu
    var = jnp.mean(d * d, axis=1, keepdims=True)
    rstd = lax.rsqrt(var + LN_EPS)
    return d * rstd, rstd


def _layer_norm_bwd(dxh, xhat, rstd):
    m1 = jnp.mean(dxh, axis=1, keepdims=True)
    m2 = jnp.mean(dxh * xhat, axis=1, keepdims=True)
    return rstd * (dxh - m1 - xhat * m2)


def _mesh_pos():
    return lax.axis_index("x"), lax.axis_index("y"), lax.axis_index("c")


HBM_SPEC = pl.BlockSpec(memory_space=pltpu.HBM)
SEM_SPEC = pl.BlockSpec(memory_space=pltpu.SEMAPHORE)
ANY_SPEC = pl.BlockSpec(memory_space=pl.ANY)
SIDE_EFFECT = pltpu.SideEffectType.DATAFLOW_SIDE_EFFECTING
PEER_SEMS = pltpu.SemaphoreType.DMA((N_DEV - 1,))


def _in_hbm(a):
    return pltpu.with_memory_space_constraint(a, pltpu.HBM)


def _split_copy_start(name, items, copies):
    n = len(items)
    flat = [a for it in items for a in it]
    k = len(flat)

    def body(*refs):
        arr, sems = list(refs[:k]), refs[k:k + 2 * n]
        for i, it in enumerate(items):
            mine = [arr.pop(0) for _ in it]
            for m in range(1, N_DEV):
                cp = copies[i](m, mine, sems[i].at[m - 1], sems[n + i].at[m - 1])
                if cp is not None:
                    cp.start()
        token = refs[-1]
        token[...] = jnp.zeros_like(token)

    res = pl.pallas_call(
        body, name=name,
        out_shape=[PEER_SEMS] * (2 * n) + [pltpu.HBM(a.shape, a.dtype) for a in flat] + [SDS((8, LANES), F32)],
        in_specs=[HBM_SPEC] * k,
        out_specs=[SEM_SPEC] * (2 * n) + [HBM_SPEC] * k + [pl.BlockSpec(memory_space=pltpu.VMEM)],
        input_output_aliases={j: 2 * n + j for j in range(k)},
        compiler_params=pltpu.CompilerParams(has_side_effects=SIDE_EFFECT),
    )(*[_in_hbm(a) for a in flat])
    thru, out_items = list(res[2 * n:2 * n + k]), []
    for it in items:
        out_items.append(tuple(thru.pop(0) for _ in it))
    return dict(send=res[:n], recv=res[n:2 * n], items=out_items, token=res[-1], copies=copies)


def _gather_start(shards, copies):
    names = list(shards)
    n = len(names)
    flip = [name in TRANSPOSED_HERE for name in names]
    shapes = [shards[name].shape[::-1] if f else shards[name].shape for name, f in zip(names, flip)]
    most = (max(s[0] for s in shapes), max(s[1] for s in shapes))

    def body(*refs):
        src, sems, land, token = refs[:n], refs[n:3 * n], refs[3 * n:4 * n], refs[4 * n]
        wide, narrow, sem = refs[4 * n + 1:]
        for i, (rows, cols) in enumerate(shapes):
            raw = wide.at[0:cols, 0:rows] if flip[i] else wide.at[0:rows, 0:cols]
            bring = pltpu.make_async_copy(src[i], raw, sem.at[0])
            bring.start()
            bring.wait()
            narrow[0:rows, 0:cols] = (raw[...].T if flip[i] else raw[...]).astype(BF16)
            mine = land[i].at[pl.ds(pl.multiple_of(_peer_index(0) * rows, 8), rows), :]
            place = pltpu.make_async_copy(narrow.at[0:rows, 0:cols], mine, sem.at[0])
            place.start()
            place.wait()
            for m in range(1, N_DEV):
                cp = copies[i](m, [land[i]], sems[i].at[m - 1], sems[n + i].at[m - 1])
                if cp is not None:
                    cp.start()
        token[...] = jnp.zeros_like(token)

    side = max(most)
    res = pl.pallas_call(
        body, name="gather_start",
        out_shape=[PEER_SEMS] * (2 * n) + [pltpu.HBM((N_DEV * r, c), BF16) for r, c in shapes] + [SDS((8, LANES), F32)],
        in_specs=[HBM_SPEC] * n,
        out_specs=[SEM_SPEC] * (2 * n) + [HBM_SPEC] * n + [pl.BlockSpec(memory_space=pltpu.VMEM)],
        scratch_shapes=[pltpu.VMEM((side, side), F32), pltpu.VMEM(most, BF16), pltpu.SemaphoreType.DMA((1,))],
        compiler_params=pltpu.CompilerParams(has_side_effects=SIDE_EFFECT),
    )(*[_in_hbm(shards[name]) for name in names])
    return dict(send=res[:n], recv=res[n:2 * n], items=[(a,) for a in res[2 * n:3 * n]], token=res[-1], copies=copies)


def _gather_relay(name, started, which, after):
    (land,) = started["items"][which]

    def body(*refs):
        land_ref, old_send, old_recv = refs[0], refs[1], refs[2]
        send, recv, token = refs[3 + len(after)], refs[4 + len(after)], refs[-1]
        for m in range(1, N_DEV):
            cp = _gather_copy_near(m, [land_ref], old_send.at[m - 1], old_recv.at[m - 1])
            if cp is None:
                continue
            cp.wait_send()
            cp.wait_recv()
            if m > 1:
                _gather_copy_pass(m + 1, [land_ref], send.at[m], recv.at[m]).start()
        token[...] = jnp.zeros_like(token)

    res = pl.pallas_call(
        body, name=name,
        out_shape=[PEER_SEMS, PEER_SEMS, pltpu.HBM(land.shape, land.dtype), SDS((8, LANES), F32)],
        in_specs=[HBM_SPEC, SEM_SPEC, SEM_SPEC] + [ANY_SPEC] * len(after),
        out_specs=[SEM_SPEC, SEM_SPEC, HBM_SPEC, pl.BlockSpec(memory_space=pltpu.VMEM)],
        input_output_aliases={0: 2},
        compiler_params=pltpu.CompilerParams(has_side_effects=SIDE_EFFECT),
    )(land, started["send"][which], started["recv"][which], *[_in_hbm(a) for a in after])
    return dict(send=[res[0]], recv=[res[1]], items=[(res[2],)], token=res[3], copies=[_gather_copy_pass])


def _split_copy_wait(name, started, which, after):
    items = [started["items"][i] for i in which]
    copies = [started["copies"][i] for i in which]
    n = len(items)
    flat = [a for it in items for a in it]
    k = len(flat)

    def body(*refs):
        arr, sems = list(refs[:k]), refs[k:k + 2 * n]
        for i, it in enumerate(items):
            mine = [arr.pop(0) for _ in it]
            for m in range(1, N_DEV):
                cp = copies[i](m, mine, sems[i].at[m - 1], sems[n + i].at[m - 1])
                if cp is not None:
                    cp.wait_send()
                    cp.wait_recv()

    res = pl.pallas_call(
        body, name=name,
        out_shape=[pltpu.HBM(a.shape, a.dtype) for a in flat],
        in_specs=[HBM_SPEC] * k + [SEM_SPEC] * (2 * n) + [ANY_SPEC] * len(after),
        out_specs=[HBM_SPEC] * k,
        input_output_aliases={j: j for j in range(k)},
        compiler_params=pltpu.CompilerParams(has_side_effects=SIDE_EFFECT),
    )(*flat, *[started["send"][i] for i in which], *[started["recv"][i] for i in which], *[_in_hbm(a) for a in after])
    thru, out_items = list(res), []
    for it in items:
        out_items.append(tuple(thru.pop(0) for _ in it))
    return out_items


def _gather_copy(m, refs, send_sem, recv_sem):
    (land_ref,) = refs
    r = land_ref.shape[0] // N_DEV
    mine = land_ref.at[pl.ds(pl.multiple_of(_peer_index(0) * r, 8), r), :]
    return pltpu.make_async_remote_copy(src_ref=mine, dst_ref=mine, send_sem=send_sem, recv_sem=recv_sem,
                                        device_id=_peer(m), device_id_type=MESH)


def _gather_copy_near(m, refs, send_sem, recv_sem):
    return _gather_copy(m, refs, send_sem, recv_sem) if m == 1 or m % 2 == 0 else None


def _gather_copy_pass(m, refs, send_sem, recv_sem):
    if m == 1 or m % 2 == 0:
        return None
    (land_ref,) = refs
    r = land_ref.shape[0] // N_DEV
    block = land_ref.at[pl.ds(pl.multiple_of(_peer_index(m ^ 1) * r, 8), r), :]
    return pltpu.make_async_remote_copy(src_ref=block, dst_ref=block, send_sem=send_sem, recv_sem=recv_sem,
                                        device_id=_peer(1), device_id_type=MESH)


def _small_copy(m, refs, send_sem, recv_sem):
    part_ref, land_ref = refs
    return pltpu.make_async_remote_copy(src_ref=part_ref, dst_ref=land_ref.at[m - 1], send_sem=send_sem,
                                        recv_sem=recv_sem, device_id=_peer(m), device_id_type=MESH)


def _scatter_copy(m, refs, send_sem, recv_sem):
    buf_ref, land_ref = refs
    r = buf_ref.shape[0] // N_DEV
    src = buf_ref.at[pl.ds(pl.multiple_of(_peer_index(m) * r, 8), r), :]
    return pltpu.make_async_remote_copy(src_ref=src, dst_ref=land_ref.at[m - 1], send_sem=send_sem,
                                        recv_sem=recv_sem, device_id=_peer(m), device_id_type=MESH)


def _peer(m):
    x, y, c = _mesh_pos()
    bx, by, bc = (m >> 2) & 1, (m >> 1) & 1, m & 1
    return (x ^ bx if bx else x, y ^ by if by else y, c ^ bc if bc else c)


def _peer_index(m):
    x, y, c = _mesh_pos()
    return (4 * x + 2 * y + c) ^ m


SMALL_PLACE = {
    "ln1_gain": (ROW_LN1G, 0), "ln1_bias": (ROW_LN1B, 0), "ln2_gain": (ROW_LN2G, 0), "ln2_bias": (ROW_LN2B, 0),
    "ret_gn_gain": (ROW_GN, 0), "ret_decay_fwd": (ROW_MISC, MISC_DF), "ret_decay_bwd": (ROW_MISC, MISC_DB),
    "attn_sink": (ROW_MISC, MISC_SINK)}


def _small_adamw(me, part, landed, w, m, v):
    d = part.shape[1]
    names = list(SMALL_PLACE)
    k = len(names)

    def body(*refs):
        me_ref, part_ref, land_ref = refs[:3]
        refs = refs[2:]
        w_refs, m_refs, v_refs = refs[1:1 + k], refs[1 + k:1 + 2 * k], refs[1 + 2 * k:1 + 3 * k]
        outs = refs[1 + 3 * k:1 + 7 * k + 1]
        tot_ref = refs[-1]
        loss_ref, g_refs, dl_refs = outs[0], outs[1:1 + k], outs[1 + k:1 + 2 * k]
        nm_refs, nv_refs = outs[1 + 2 * k:1 + 3 * k], outs[1 + 3 * k:1 + 4 * k]
        tot = jnp.zeros(part_ref.shape, F32)
        for dev in range(N_DEV):
            j = dev ^ me_ref[0]
            tot = tot + jnp.where(j == 0, part_ref[...], land_ref[jnp.maximum(j, 1) - 1])
        tot_ref[...] = tot
        loss_ref[...] = (0.5 / d) * jnp.sum(tot_ref[ROW_LOSS:ROW_LOSS + 1, :], axis=1, keepdims=True)
        for i, name in enumerate(names):
            row, lo = SMALL_PLACE[name]
            wv = w_refs[i][...]
            g = tot_ref[row:row + 1, lo:lo + wv.shape[1]]
            if name.startswith("ret_decay"):
                p2 = jnp.exp2(wv)
                g = g * (-p2 * jnp.log(2.0) / (1.0 - p2))
            g_refs[i][...] = g
            _adamw_store(g, wv, m_refs[i][...], v_refs[i][...], dl_refs[i], nm_refs[i], nv_refs[i])

    shapes = [SDS(w[n].shape, F32) for n in names]
    vm = pl.BlockSpec(memory_space=pltpu.VMEM)
    res = pl.pallas_call(
        body, name="small_adamw", out_shape=[SDS((1, 1), F32)] + shapes * 4,
        in_specs=[_smem_spec()] + [vm] * (2 + 3 * k), out_specs=[vm] * (1 + 4 * k),
        scratch_shapes=[pltpu.VMEM(part.shape, F32)],
    )(me, part, landed, *[w[n] for n in names], *[m[n] for n in names], *[v[n] for n in names])
    groups = [dict(zip(names, res[1 + j * k:1 + (j + 1) * k])) for j in range(4)]
    return (res[0], *groups)


def _adamw_store(g, w, m, v, dl_ref, nm_ref, nv_ref):
    m = ADAM_B1 * m + (1.0 - ADAM_B1) * g
    v = ADAM_B2 * v + (1.0 - ADAM_B2) * (g * g)
    m_hat = m / (1.0 - ADAM_B1 ** ADAM_STEP)
    v_hat = v / (1.0 - ADAM_B2 ** ADAM_STEP)
    dl_ref[...] = -ADAM_LR * (m_hat / (jnp.sqrt(v_hat) + ADAM_EPS) + ADAM_WD * w)
    nm_ref[...] = m
    nv_ref[...] = v


def _reduce_adamw(name, owns, recvs, ws, ms, vs, transposed):
    count = len(owns)
    rows, n = owns[0].shape
    steps = 1 if transposed or rows % 16 else 2
    rb = rows // steps

    def body(*refs):
        ins, outs = refs[:5 * count], refs[5 * count:]
        j = pl.program_id(0)
        for k in range(count):
            @pl.when(j == k)
            def _(k=k):
                own_ref, recv_ref, w_ref, m_ref, v_ref = ins[5 * k:5 * k + 5]
                g_ref, dl_ref, nm_ref, nv_ref = outs[4 * k:4 * k + 4]
                g = own_ref[...]
                for p in range(recv_ref.shape[0]):
                    g = g + recv_ref[p].astype(F32)
                if transposed:
                    g = g.T
                g_ref[...] = g
                _adamw_store(g, w_ref[...], m_ref[...], v_ref[...], dl_ref, nm_ref, nv_ref)

    def turn(k):
        return lambda j, i: jnp.where(j == k, i, jnp.where(j < k, 0, steps - 1))

    in_specs, out_specs = [], []
    for k in range(count):
        at = turn(k)
        blk = pl.BlockSpec(ws[0].shape if transposed else (rb, n), lambda j, i, at=at: (at(j, i), 0))
        in_specs += [pl.BlockSpec((rb, n), lambda j, i, at=at: (at(j, i), 0)),
                     pl.BlockSpec((recvs[k].shape[0], rb, n), lambda j, i, at=at: (0, at(j, i), 0)), blk, blk, blk]
        out_specs += [blk] * 4
    res = pl.pallas_call(
        body, name="adamw_" + name, grid=(count, steps), in_specs=in_specs, out_specs=out_specs,
        out_shape=[SDS(ws[0].shape, F32)] * (4 * count), compiler_params=_params(("arbitrary", "arbitrary")),
    )(*[a for k in range(count) for a in (owns[k], recvs[k], ws[k], ms[k], vs[k])])
    return [list(res[j::4]) for j in range(4)]


def _row_spec(tm, width):
    return pl.BlockSpec((tm, width), lambda i: (i, 0))


def _full_spec(shape):
    return pl.BlockSpec(shape, lambda i: (0,) * len(shape))


_acc_spec = _full_spec


def _sub_rows(tm):
    step = min(SUB_ROWS, tm)
    return [(lo, lo + step) for lo in range(0, tm, step)]


def _in_proj(x2, wt_in):
    t, d = x2.shape
    u_w = wt_in.shape[0]
    tm = _row_tile(t, MATMUL_ROWS)

    def body(x_ref, w_ref, u_ref, xb_ref):
        xb = x_ref[...].astype(BF16)
        xb_ref[...] = xb
        u_ref[...] = _dot_nt(xb, w_ref[...]).astype(BF16)

    return pl.pallas_call(
        body, name="in_proj", grid=(t // tm,),
        in_specs=[_row_spec(tm, d), _full_spec(wt_in.shape)],
        out_specs=[_row_spec(tm, u_w), _row_spec(tm, d)],
        out_shape=[SDS((t, u_w), BF16), SDS((t, d), BF16)],
        compiler_params=_params(("parallel",)),
    )(x2, wt_in)


def _col_halves(f):
    n = f // LANES
    k = (n + 1) // 2 * LANES
    return [(0, k), (k, f)] if k < f else [(0, f)]


def _mix_ln1_ffn_up(r, a, x2, w_out, wt_gate, wt_up, g1, b1):
    t, d = x2.shape
    f = wt_gate.shape[0]
    tm = _row_tile(t, EPILOGUE_ROWS)

    def body(r_ref, a_ref, x_ref, wo_ref, wg_ref, wu_ref, g_ref, b_ref, xh_ref, rs_ref, hb_ref, dg_ref, du_ref,
             act_ref):
        mix = _dot_nn(r_ref[...], wo_ref[0:RET_W, :]) + _dot_nn(a_ref[...], wo_ref[RET_W:RET_W + ATT_W, :])
        z = ALPHA * x_ref[...] + mix
        xhat, rstd = _layer_norm_stats(z)
        xh_ref[...] = xhat
        rs_ref[...] = jnp.broadcast_to(rstd, rs_ref.shape)
        h = (xhat * g_ref[...] + b_ref[...]).astype(BF16)
        hb_ref[...] = h
        g = _dot_nt(h, wg_ref[...])
        u = _dot_nt(h, wu_ref[...])
        sg = _sigmoid(g)
        silu = g * sg
        dg_ref[...] = (u * (sg * (1.0 + g * (1.0 - sg)))).astype(BF16)
        du_ref[...] = silu.astype(BF16)
        act_ref[...] = (silu * u).astype(BF16)

    wide, narrow = _row_spec(tm, f), _row_spec(tm, d)
    return pl.pallas_call(
        body, name="mix_ln1_ffn_up", grid=(t // tm,),
        in_specs=[_row_spec(tm, RET_W), _row_spec(tm, ATT_W), narrow, _resident_spec(w_out.shape),
                  _resident_spec(wt_gate.shape), _resident_spec(wt_up.shape), _full_spec(g1.shape),
                  _full_spec(b1.shape)],
        out_specs=[narrow, _row_spec(tm, LANES), narrow, wide, wide, wide],
        out_shape=[SDS((t, d), F32), SDS((t, LANES), F32), SDS((t, d), BF16)] + [SDS((t, f), BF16)] * 3,
        compiler_params=_params(("parallel",)),
    )(r, a, x2, w_out, wt_gate, wt_up, g1, b1)


def _ffn_down_ln2_loss(act, dact_dg, dact_du, h1b, p2, xhat1, target, w_down, w_pg, wt_pe, g1, b1, g2, b2):
    t, d = xhat1.shape
    f = act.shape[1]
    pdim = p2.shape[1]
    tm = _row_tile(t, EPILOGUE_ROWS)

    def body(act_ref, fg_ref, fu_ref, hb_ref, p_ref, xh1_ref, tgt_ref, wd_ref, wpg_ref, wpe_ref, g1_ref, b1_ref,
             g2_ref, b2_ref, dz_ref, dzb_ref, ds_ref, dple_ref, dg_ref, du_ref, acc_ref):
        @pl.when(pl.program_id(0) == 0)
        def _():
            acc_ref[...] = jnp.zeros_like(acc_ref)

        for lo, hi in _sub_rows(tm):
            h1 = xh1_ref[lo:hi, :] * g1_ref[...] + b1_ref[...]
            pg = _sigmoid(_dot_nn(hb_ref[lo:hi, :], wpg_ref[...]))
            ple = _dot_nt(p_ref[lo:hi, :].astype(BF16), wpe_ref[...])
            gated = pg * ple
            dgate = gated * (1.0 - pg)
            ffn = _dot_nn(act_ref[lo:hi, :], wd_ref[...])
            z2 = ALPHA * h1 + gated + ffn
            xhat2, rstd2 = _layer_norm_stats(z2)
            err = xhat2 * g2_ref[...] + b2_ref[...] - tgt_ref[lo:hi, :]
            dy = err * (1.0 / d)
            dz = _layer_norm_bwd(dy * g2_ref[...], xhat2, rstd2)
            dzb = dz.astype(BF16)
            dz_ref[lo:hi, :] = dz
            dzb_ref[lo:hi, :] = dzb
            ds_ref[lo:hi, :] = (dz * dgate).astype(BF16)
            dple_ref[lo:hi, :] = (dz * pg).astype(BF16)
            acc_ref[0:1, :] += jnp.sum(err * err, axis=0, keepdims=True)
            acc_ref[1:2, :] += jnp.sum(dy * xhat2, axis=0, keepdims=True)
            acc_ref[2:3, :] += jnp.sum(dy, axis=0, keepdims=True)
            for c0, c1 in _col_halves(f):
                da = _dot_nt(dzb, wd_ref[c0:c1, :])
                dg_ref[lo:hi, c0:c1] = (da * fg_ref[lo:hi, c0:c1].astype(F32)).astype(BF16)
                du_ref[lo:hi, c0:c1] = (da * fu_ref[lo:hi, c0:c1].astype(F32)).astype(BF16)

    vec = _full_spec(g1.shape)
    wide, narrow = _row_spec(tm, f), _row_spec(tm, d)
    return pl.pallas_call(
        body, name="ffn_down_ln2_loss", grid=(t // tm,),
        in_specs=[wide, wide, wide, narrow, _row_spec(tm, pdim), narrow, narrow,
                  _full_spec(w_down.shape), _full_spec(w_pg.shape), _full_spec(wt_pe.shape), vec, vec, vec, vec],
        out_specs=[narrow] * 4 + [wide, wide, _acc_spec((8, d))],
        out_shape=[SDS((t, d), F32), SDS((t, d), BF16), SDS((t, d), BF16), SDS((t, d), BF16),
                   SDS((t, f), BF16), SDS((t, f), BF16), SDS((8, d), F32)],
        compiler_params=_params(("arbitrary",)),
    )(act, dact_dg, dact_du, h1b, p2, xhat1, target, w_down, w_pg, wt_pe, g1, b1, g2, b2)


def _after(after, body):
    k = len(after)
    return (lambda *refs: body(*refs[k:])), [ANY_SPEC] * k


def _resident_spec(shape):
    return pl.BlockSpec(shape, lambda i: (0,) * len(shape), pipeline_mode=pl.Buffered(1))


def _dh1_ln1_bwd(dz2, dg, dup, dsb, xhat1, rstd1, wt_gate, wt_up, w_pg, w_out, g1, after=()):
    t, d = dz2.shape
    f = dg.shape[1]
    tm = _row_tile(t, MATMUL_ROWS)

    def body(dz_ref, dg_ref, du_ref, ds_ref, xh1_ref, rs1_ref, wg_ref, wu_ref, wpg_ref, wo_ref, g1_ref,
             dz1_ref, dz1b_ref, dr_ref, da_ref, acc_ref):
        @pl.when(pl.program_id(0) == 0)
        def _():
            acc_ref[...] = jnp.zeros_like(acc_ref)

        for lo, hi in _sub_rows(tm):
            dh = (ALPHA * dz_ref[lo:hi, :] + _dot_nn(dg_ref[lo:hi, :], wg_ref[...])
                  + _dot_nn(du_ref[lo:hi, :], wu_ref[...]) + _dot_nt(ds_ref[lo:hi, :], wpg_ref[...]))
            xhat, rstd = xh1_ref[lo:hi, :], rs1_ref[lo:hi, 0:1]
            dz1 = _layer_norm_bwd(dh * g1_ref[...], xhat, rstd)
            dz1b = dz1.astype(BF16)
            dz1_ref[lo:hi, :] = dz1
            dz1b_ref[lo:hi, :] = dz1b
            acc_ref[0:1, :] += jnp.sum(dh * xhat, axis=0, keepdims=True)
            acc_ref[1:2, :] += jnp.sum(dh, axis=0, keepdims=True)
            dr_ref[lo:hi, :] = _dot_nt(dz1b, wo_ref[0:RET_W, :]).astype(BF16)
            da_ref[lo:hi, :] = _dot_nt(dz1b, wo_ref[RET_W:RET_W + ATT_W, :]).astype(BF16)

    body, lead = _after(after, body)
    return pl.pallas_call(
        body, name="dh1_ln1_bwd", grid=(t // tm,),
        in_specs=lead + [_row_spec(tm, d), _row_spec(tm, f), _row_spec(tm, f), _row_spec(tm, d), _row_spec(tm, d),
                         _row_spec(tm, LANES), _resident_spec(wt_gate.shape), _resident_spec(wt_up.shape), _resident_spec(w_pg.shape),
                         _resident_spec(w_out.shape), _full_spec(g1.shape)],
        out_specs=[_row_spec(tm, d), _row_spec(tm, d), _row_spec(tm, RET_W), _row_spec(tm, ATT_W), _acc_spec((8, d))],
        out_shape=[SDS((t, d), F32), SDS((t, d), BF16), SDS((t, RET_W), BF16), SDS((t, ATT_W), BF16),
                   SDS((8, d), F32)],
        compiler_params=_params(("arbitrary",)),
    )(*after, dz2, dg, dup, dsb, xhat1, rstd1, wt_gate, wt_up, w_pg, w_out, g1)


def _in_proj_bwd(dz1, parts, wt_in, after=()):
    t, d = dz1.shape
    tm = _row_tile(t, MATMUL_ROWS)
    widths = [p.shape[1] for p in parts]

    def body(*refs):
        dz_ref, part_refs, w_ref, dx_ref = refs[0], refs[1:1 + len(parts)], refs[-2], refs[-1]
        acc = ALPHA * dz_ref[...]
        lo = 0
        for p_ref, w in zip(part_refs, widths):
            acc = acc + _dot_nn(p_ref[...], w_ref[lo:lo + w, :])
            lo += w
        dx_ref[...] = acc

    body, lead = _after(after, body)
    return pl.pallas_call(
        body, name="in_proj_bwd", grid=(t // tm,),
        in_specs=lead + [_row_spec(tm, d)] + [_row_spec(tm, w) for w in widths] + [_full_spec(wt_in.shape)],
        out_specs=_row_spec(tm, d), out_shape=SDS((t, d), F32),
        compiler_params=_params(("parallel",)),
    )(*after, dz1, *parts, wt_in)


def _weight_grad(name, me, parts, rhs, after=()):
    t, n = rhs.shape
    widths = [p.shape[1] for p in parts]
    rows = sum(widths)
    own_rows = rows // N_DEV
    tk = _row_tile(t, MATMUL_ROWS)
    n_steps = t // tk
    step = 256

    def body(*refs):
        me_ref, part_refs, rhs_ref = refs[0], refs[1:1 + len(parts)], refs[1 + len(parts)]
        full_ref, own_ref, acc = refs[-3], refs[-2], refs[-1]
        i = pl.program_id(0)

        def products(first):
            b = rhs_ref[...].astype(BF16)
            lo = 0
            for p_ref, w in zip(part_refs, widths):
                for c0 in range(0, w, step):
                    c1 = min(c0 + step, w)
                    val = _dot_tn(p_ref[:, c0:c1].astype(BF16), b)
                    if first:
                        acc[lo + c0:lo + c1, :] = val
                    else:
                        acc[lo + c0:lo + c1, :] += val
                lo += w

        pl.when(i == 0)(functools.partial(products, True))
        pl.when(i > 0)(functools.partial(products, False))

        @pl.when(i == n_steps - 1)
        def _():
            full_ref[...] = acc[...].astype(BF16)
            own_ref[...] = acc[pl.ds(pl.multiple_of(me_ref[0] * own_rows, 8), own_rows), :]

    body, lead = _after(after, body)
    return pl.pallas_call(
        body, name=name, grid=(n_steps,),
        in_specs=lead + [_smem_spec()] + [_row_spec(tk, w) for w in widths] + [_row_spec(tk, n)],
        out_specs=[_full_spec((rows, n)), _full_spec((own_rows, n))],
        out_shape=[SDS((rows, n), BF16), SDS((own_rows, n), F32)],
        scratch_shapes=[pltpu.VMEM((rows, n), F32)],
        compiler_params=_params(("arbitrary",)),
    )(*after, me, *parts, rhs)


def _weight_grad_jobs(name, me, jobs, after=()):
    count = len(jobs)
    t = jobs[0][0].shape[0]
    tk = _row_tile(t, MATMUL_ROWS)
    n_steps = t // tk
    shapes = [(lhs.shape[1], rhs.shape[1]) for lhs, rhs in jobs]
    most_rows, most_cols = max(r for r, _ in shapes), max(n for _, n in shapes)
    step = 256

    def body(*refs):
        me_ref, lhs_refs, rhs_refs = refs[0], refs[1:1 + count], refs[1 + count:1 + 2 * count]
        full_refs, own_refs = refs[1 + 2 * count:1 + 3 * count], refs[1 + 3 * count:1 + 4 * count]
        acc, whole, mine, sems = refs[1 + 4 * count:]
        job, i = pl.program_id(0), pl.program_id(1)

        def leaving(j):
            rows, n = shapes[j]
            return (pltpu.make_async_copy(whole.at[0:rows, 0:n], full_refs[j], sems.at[0]),
                    pltpu.make_async_copy(mine.at[0:rows // N_DEV, 0:n], own_refs[j], sems.at[1]))

        def products(j, first):
            rows, n = shapes[j]
            b = rhs_refs[j][...].astype(BF16)
            for c0 in range(0, rows, step):
                c1 = min(c0 + step, rows)
                val = _dot_tn(lhs_refs[j][:, c0:c1].astype(BF16), b)
                if first:
                    acc[c0:c1, 0:n] = val
                else:
                    acc[c0:c1, 0:n] += val

        def finish(j):
            rows, n = shapes[j]
            own_rows = rows // N_DEV
            if j > 0:
                for cp in leaving(j - 1):
                    cp.wait()
            whole[0:rows, 0:n] = acc[0:rows, 0:n].astype(BF16)
            mine[0:own_rows, 0:n] = acc[pl.ds(pl.multiple_of(me_ref[0] * own_rows, 8), own_rows), 0:n]
            for cp in leaving(j):
                cp.start()
            if j == count - 1:
                for cp in leaving(j):
                    cp.wait()

        for j in range(count):
            pl.when((job == j) & (i == 0))(functools.partial(products, j, True))
            pl.when((job == j) & (i > 0))(functools.partial(products, j, False))
            pl.when((job == j) & (i == n_steps - 1))(functools.partial(finish, j))

    def turn(j):
        return lambda job, i: (jnp.where(job == j, i, jnp.where(job < j, 0, n_steps - 1)), 0)

    body, lead = _after(after, body)
    res = pl.pallas_call(
        body, name=name, grid=(count, n_steps),
        in_specs=lead + [_smem_spec()] + [pl.BlockSpec((tk, rows), turn(j)) for j, (rows, _) in enumerate(shapes)]
        + [pl.BlockSpec((tk, n), turn(j)) for j, (_, n) in enumerate(shapes)],
        out_specs=[ANY_SPEC] * (2 * count),
        out_shape=[SDS((rows, n), BF16) for rows, n in shapes] + [SDS((rows // N_DEV, n), F32) for rows, n in shapes],
        scratch_shapes=[pltpu.VMEM((most_rows, most_cols), F32), pltpu.VMEM((most_rows, most_cols), BF16),
                        pltpu.VMEM((most_rows // N_DEV, most_cols), F32), pltpu.SemaphoreType.DMA((2,))],
        compiler_params=_params(("arbitrary", "arbitrary")),
    )(*after, me, *[lhs for lhs, _ in jobs], *[rhs for _, rhs in jobs])
    return list(res[:count]), list(res[count:])


def _log_decay(decay_f, decay_b):
    def body(f_ref, b_ref, lf_ref, lb_ref):
        lf_ref[...] = jnp.log1p(-jnp.exp2(f_ref[...]))
        lb_ref[...] = jnp.log1p(-jnp.exp2(b_ref[...]))

    return pl.pallas_call(body, name="log_decay", out_shape=[SDS(decay_f.shape, F32)] * 2)(decay_f, decay_b)


def _chunk(ref, n):
    return ref[pl.ds(pl.multiple_of(n * CHUNK, CHUNK), CHUNK), :]


def _group_sum(is_a, v):
    sa = jnp.sum(jnp.where(is_a, v, 0.0), axis=1, keepdims=True)
    sb = jnp.sum(jnp.where(is_a, 0.0, v), axis=1, keepdims=True)
    return jnp.where(is_a, sa, sb)


def _seq_spec(s, col_block):
    return pl.BlockSpec((s, LANES), lambda b, h: (b, col_block + h))


def _smem_spec():
    return pl.BlockSpec(memory_space=pltpu.SMEM)


RET_UNROLL = 4
FWD_PREP_UNROLL = 16
BWD_PREP_UNROLL = 8


def _chunk_loop(n_chunks, body, init, unroll):
    u = unroll if n_chunks % unroll == 0 else 1

    def trip(i, carry):
        for j in range(u):
            carry = body(i * u + j, carry)
        return carry

    return lax.fori_loop(0, n_chunks // u, trip, init)


def _stacked_tables(lgf_ref, lgb_ref, pair):
    lane = lax.broadcasted_iota(jnp.int32, (1, LANES), 1)
    is_a = lane < HEAD_DIM
    lgf = jnp.where(is_a, lgf_ref[2 * pair], lgf_ref[2 * pair + 1])
    lgb = jnp.where(is_a, lgb_ref[2 * pair], lgb_ref[2 * pair + 1])
    row = lax.broadcasted_iota(jnp.int32, (CHUNK, 1), 0).astype(F32)
    kdec_f, qdec_f = jnp.exp(lgf * (CHUNK - 1.0 - row)), jnp.exp(lgf * (row + 1.0))
    kdec_b, qdec_b = jnp.exp(lgb * row), jnp.exp(lgb * (CHUNK - row))
    tab = dict(
        is_a=is_a, row=row, lam_f=jnp.exp(lgf * CHUNK), lam_b=jnp.exp(lgb * CHUNK),
        kdec=jnp.concatenate([kdec_f, kdec_b], axis=1), qdec=jnp.concatenate([qdec_f, qdec_b], axis=1),
        qexp=jnp.concatenate([jnp.broadcast_to(row + 1.0, (CHUNK, LANES)),
                              jnp.broadcast_to(CHUNK - row, (CHUNK, LANES))], axis=1),
        kexp=jnp.concatenate([jnp.broadcast_to(CHUNK - 1.0 - row, (CHUNK, LANES)),
                              jnp.broadcast_to(row, (CHUNK, LANES))], axis=1),
    )
    r = lax.broadcasted_iota(jnp.int32, (2 * LANES, LANES), 0)
    c = lax.broadcasted_iota(jnp.int32, (2 * LANES, LANES), 1)
    tab["diag2"] = ((r & (LANES - 1)) < HEAD_DIM) == (c < HEAD_DIM)
    i2 = lax.broadcasted_iota(jnp.int32, (2 * CHUNK, CHUNK), 0)
    j = lax.broadcasted_iota(jnp.int32, (2 * CHUNK, CHUNK), 1)
    head_b = i2 >= CHUNK
    diff = ((i2 & (CHUNK - 1)) - j).astype(F32)
    up, dn = jnp.maximum(diff, 0.0), jnp.maximum(-diff, 0.0)
    lgf2 = jnp.where(head_b, lgf_ref[2 * pair + 1], lgf_ref[2 * pair])
    lgb2 = jnp.where(head_b, lgb_ref[2 * pair + 1], lgb_ref[2 * pair])
    ef = jnp.where(diff >= 0, jnp.exp(lgf2 * up), 0.0)
    eb = jnp.where(diff <= 0, jnp.exp(lgb2 * dn), 0.0)
    tab["d2"] = ef + eb
    tab["df2"] = ef * up
    tab["db2"] = eb * dn
    return tab


def _stack_pair(is_a, x):
    zero = jnp.zeros_like(x)
    return jnp.concatenate([jnp.where(is_a, x, zero), jnp.where(is_a, zero, x)], axis=0)


def _unstack_pair(is_a, x2):
    return jnp.where(is_a, x2[0:CHUNK, :], x2[CHUNK:2 * CHUNK, :])


def _both_ways(x, dec):
    return (jnp.concatenate([x, x], axis=1) * dec).astype(BF16)


def _scan_states(n_chunks, st, up_rows, up_lam, down_rows, down_lam):
    zero = jnp.zeros((LANES, LANES), F32)

    def up(n, r):
        new = st[n, up_rows, :]
        st[n, up_rows, :] = r
        return r * up_lam + new

    def down(s, r):
        n = n_chunks - 1 - s
        new = st[n, down_rows, :]
        st[n, down_rows, :] = r
        return r * down_lam + new

    lax.fori_loop(0, n_chunks, up, zero)
    lax.fori_loop(0, n_chunks, down, zero)


FWD_ROWS, BWD_ROWS = pl.ds(0, LANES), pl.ds(LANES, LANES)


def _state_spec(n_chunks, pairs):
    return pl.BlockSpec((n_chunks, 2 * LANES, LANES), lambda b, h: (b * pairs + h, 0, 0))


ST_GAIN, ST_XF, ST_XB, ST_IFA, ST_IFB, ST_IBA, ST_IBB, ST_LF, ST_LB = 0, 1, 2, 3, 4, 5, 6, 8, 9
ST_ROWS = 16


GW = GROUP * HEAD_DIM
KEYS = 3 * BLOCK


def _attn_tables(g, bias_ref):
    r = lax.broadcasted_iota(jnp.int32, (GROUP * BLOCK, KEYS), 0)
    kj = lax.broadcasted_iota(jnp.int32, (GROUP * BLOCK, KEYS), 1)
    qi = r & (BLOCK - 1)
    hh = lax.shift_right_logical(r, 7)
    dist = jnp.abs(kj - BLOCK - qi)
    slope = jnp.exp2(-(GROUP * g + hh + 1).astype(F32) * (8.0 / ATTN_HEADS))
    inside = jnp.where(dist <= BLOCK, -slope * dist.astype(F32), NEG_INF)
    bias_ref[BIAS_INSIDE] = inside
    bias_ref[BIAS_FIRST] = jnp.where(kj >= BLOCK, inside, NEG_INF)
    bias_ref[BIAS_LAST] = jnp.where(kj < 2 * BLOCK, inside, NEG_INF)


BIAS_INSIDE, BIAS_FIRST, BIAS_LAST = 0, 1, 2


def _own_lanes(g):
    return lax.shift_right_logical(lax.broadcasted_iota(jnp.int32, (1, LANES), 1), 6) == g


def _mask_keys(x_ref, g, scale, pad_ref, s):
    pad_ref[0:BLOCK, :] = jnp.zeros((BLOCK, LANES), BF16)
    pad_ref[BLOCK + s:2 * BLOCK + s, :] = jnp.zeros((BLOCK, LANES), BF16)
    pad_ref[BLOCK:BLOCK + s, :] = jnp.where(_own_lanes(g), x_ref[...].astype(F32) * scale, 0.0).astype(BF16)


def _lane_block(x, j):
    return x[:, j * LANES:(j + 1) * LANES]


def _stack_heads(x, g):
    assert GROUP == 4 and GW == 2 * LANES
    x1 = pltpu.roll(x, HEAD_DIM, 1)
    keep = _own_lanes(g)
    zero = jnp.zeros((BLOCK, LANES), x.dtype)
    rows = []
    for h in range(GROUP):
        for_g0 = _lane_block(x, h // 2) if h % 2 == 0 else _lane_block(x1, ((h + 1) // 2) % 2)
        for_g1 = _lane_block(x, h // 2) if h % 2 == 1 else _lane_block(x1, h // 2)
        rows.append(jnp.where(keep, jnp.where(g == 0, for_g0, for_g1), zero))
    return jnp.concatenate(rows, axis=0)


def _unstack_heads(x4, g):
    p = [x4[h * BLOCK:(h + 1) * BLOCK, :] for h in range(GROUP)]
    cat = lambda a, b: jnp.concatenate([a, b], axis=1)
    in_place = jnp.where(g == 0, cat(p[0], p[2]), cat(p[1], p[3]))
    one_left = jnp.where(g == 0, cat(p[1], p[3]), cat(p[2], p[0]))
    return in_place + pltpu.roll(one_left, HEAD_DIM, 1)


def _sink_column(sink_ref, g):
    rh = lax.shift_right_logical(lax.broadcasted_iota(jnp.int32, (GROUP * BLOCK, 1), 0), 7)
    col = jnp.zeros((GROUP * BLOCK, 1), F32)
    for h in range(GROUP):
        col = jnp.where(rh == h, sink_ref[GROUP * g + h], col)
    return col


def _attn_probs(qm, k3, bias_ref, sink_col, n, s):
    which = jnp.where(n == 0, BIAS_FIRST, jnp.where(n == s // BLOCK - 1, BIAS_LAST, BIAS_INSIDE))
    logits = _dot_nt(qm, k3) + bias_ref[which]
    m = jnp.maximum(jnp.max(logits, axis=1, keepdims=True), sink_col)
    e = jnp.exp(logits - m)
    e_sink = jnp.exp(sink_col - m)
    inv = 1.0 / (jnp.sum(e, axis=1, keepdims=True) + e_sink)
    return e * inv, e_sink * inv


PAIRS_PER_KV = (RET_HEADS // 2) // KV_HEADS
FWD_ORDER = "rrarra"
BWD_ORDER = "rararr"


def _trip_order(order, chunks, blocks):
    if order.count("r") == chunks and order.count("a") == blocks:
        return order
    return "r" * chunks + "a" * blocks


def _mixers_fwd(u, lgf, lgb, gn_gain, sink, b_loc, after=()):
    t = u.shape[0]
    s = t // b_loc
    n_chunks = s // CHUNK
    pairs = RET_HEADS // 2
    trips = n_chunks // RET_UNROLL
    blocks_half = (s // BLOCK) // PAIRS_PER_KV
    per_trip = blocks_half // trips
    assert n_chunks % RET_UNROLL == 0 and blocks_half % trips == 0 and PAIRS_PER_KV == 2 and s >= 2 * BLOCK

    def body(lgf_ref, lgb_ref, sink_ref, q_ref, k_ref, v_ref, g_ref, gain_ref, aq_ref, ak_ref, av_ref,
             r_ref, xhat_ref, rstd_ref, a_ref, st, kpad, vpad, bias):
        pair = pl.program_id(1)
        g, half = lax.shift_right_logical(pair, 1), pair & 1
        tab = _stacked_tables(lgf_ref, lgb_ref, pair)
        is_a = tab["is_a"]

        @pl.when(half == 0)
        def _():
            _attn_tables(g, bias)
            _mask_keys(ak_ref, g, Q_SCALE, kpad, s)
            _mask_keys(av_ref, g, 1.0, vpad, s)

        def kv_body(n, _):
            k8 = _chunk(k_ref, n).astype(F32) * Q_SCALE
            st[n] = jnp.where(tab["diag2"], _dot_tn(_both_ways(k8, tab["kdec"]), _chunk(v_ref, n)), 0.0)
            return 0

        _chunk_loop(n_chunks, kv_body, 0, FWD_PREP_UNROLL)
        _scan_states(n_chunks, st, FWD_ROWS, tab["lam_f"], BWD_ROWS, tab["lam_b"])
        sink_col = _sink_column(sink_ref, g)

        def retention_chunk(n):
            q = _chunk(q_ref, n)
            k8 = (_chunk(k_ref, n).astype(F32) * Q_SCALE).astype(BF16)
            v = _chunk(v_ref, n)
            p2 = (_dot_nt(_stack_pair(is_a, q), k8) * tab["d2"]).astype(BF16)
            y = _unstack_pair(is_a, _dot_nn(p2, v))
            y = y + _dot_nn(_both_ways(q.astype(F32), tab["qdec"]), st[n].astype(BF16))
            rows = pl.ds(pl.multiple_of(n * CHUNK, CHUNK), CHUNK)
            mu = _group_sum(is_a, y) * (1.0 / HEAD_DIM)
            dlt = y - mu
            var = _group_sum(is_a, dlt * dlt) * (1.0 / HEAD_DIM)
            rstd = lax.rsqrt(var + GN_EPS)
            xhat = dlt * rstd
            xhat_ref[rows, :] = xhat
            rstd_ref[rows, :] = rstd
            gate = _chunk(g_ref, n).astype(F32)
            r_ref[rows, :] = (xhat * gain_ref[...] * gate * _sigmoid(gate)).astype(BF16)

        def attention_block(blk):
            n = half * blocks_half + blk
            rows = pl.ds(pl.multiple_of(blk * BLOCK, BLOCK), BLOCK)
            keys = pl.ds(pl.multiple_of(n * BLOCK, BLOCK), KEYS)
            p, _ = _attn_probs(_stack_heads(aq_ref[rows, :], g), kpad[keys, :], bias, sink_col, n, s)
            a_ref[rows, :] = _unstack_heads(_dot_nn(p.astype(BF16), vpad[keys, :]), g).astype(BF16)

        def trip(i, _):
            chunk, blk = 0, 0
            for kind in _trip_order(FWD_ORDER, RET_UNROLL, per_trip):
                if kind == "r":
                    retention_chunk(i * RET_UNROLL + chunk)
                    chunk += 1
                else:
                    attention_block(i * per_trip + blk)
                    blk += 1
            return 0

        lax.fori_loop(0, trips, trip, 0)

    lane_blk = lambda c0: _seq_spec(s, c0 // LANES)
    half_rows = blocks_half * BLOCK
    aq_spec = pl.BlockSpec((half_rows, GW), lambda b, h: (b * PAIRS_PER_KV + (h & 1), C_AQ // GW + h // 2))
    a_spec = pl.BlockSpec((half_rows, GW), lambda b, h: (b * PAIRS_PER_KV + (h & 1), h // 2))
    kv_spec = lambda c0: pl.BlockSpec((s, LANES), lambda b, h: (b, c0 // LANES))
    pad = pltpu.VMEM((s + 2 * BLOCK, LANES), BF16)
    body, lead = _after(after, body)
    return pl.pallas_call(
        body, name="mixers_fwd", grid=(b_loc, pairs),
        in_specs=lead + [_smem_spec(), _smem_spec(), _smem_spec(), lane_blk(C_RQ), lane_blk(C_RK), lane_blk(C_RV),
                         lane_blk(C_RG), pl.BlockSpec((1, LANES), lambda b, h: (0, h)), aq_spec, kv_spec(C_AK),
                         kv_spec(C_AV)],
        out_specs=[_seq_spec(s, 0), _seq_spec(s, 0), _seq_spec(s, 0), a_spec, _state_spec(n_chunks, pairs)],
        out_shape=[SDS((t, RET_W), BF16), SDS((t, RET_W), F32), SDS((t, RET_W), F32), SDS((t, ATT_W), BF16),
                   SDS((b_loc * pairs * n_chunks, 2 * LANES, LANES), F32)],
        scratch_shapes=[pad, pad, pltpu.VMEM((3, GROUP * BLOCK, KEYS), F32)],
        compiler_params=_params(("arbitrary", "arbitrary")),
    )(*after, lgf, lgb, sink, u, u, u, u, gn_gain, u, u, u)


def _mixers_bwd(u, xhat, rstd, states, dr, da, lgf, lgb, gn_gain, sink, b_loc, after=()):
    t = u.shape[0]
    s = t // b_loc
    n_chunks = s // CHUNK
    pairs = RET_HEADS // 2
    trips = n_chunks // RET_UNROLL
    blocks_half = (s // BLOCK) // PAIRS_PER_KV
    per_trip = blocks_half // trips
    assert n_chunks % RET_UNROLL == 0 and blocks_half % trips == 0 and PAIRS_PER_KV == 2 and s >= 2 * BLOCK

    def body(lgf_ref, lgb_ref, sink_ref, q_ref, k_ref, v_ref, g_ref, xhat_ref, rstd_ref, dr_ref, gain_ref,
             aq_ref, ak_ref, av_ref, do_ref, st,
             dq_ref, dk_ref, dv_ref, dg_ref, st_ref, daq_ref, dak_ref, dav_ref, dsink_ref,
             gr, dy_s, kpad, vpad, bias, dk_acc, dv_acc):
        pair = pl.program_id(1)
        g, half = lax.shift_right_logical(pair, 1), pair & 1
        tab = _stacked_tables(lgf_ref, lgb_ref, pair)
        is_a = tab["is_a"]
        gain = gain_ref[...]

        @pl.when(half == 0)
        def _():
            _attn_tables(g, bias)
            _mask_keys(ak_ref, g, Q_SCALE, kpad, s)
            _mask_keys(av_ref, g, 1.0, vpad, s)
            dsink_ref[...] = jnp.zeros_like(dsink_ref)

        @pl.when(pair == 0)
        def _():
            dk_acc[...] = jnp.zeros_like(dk_acc)
            dv_acc[...] = jnp.zeros_like(dv_acc)

        def norm_body(n, dgain):
            rows = pl.ds(pl.multiple_of(n * CHUNK, CHUNK), CHUNK)
            xhat, rstd = xhat_ref[rows, :], rstd_ref[rows, :]
            gate = g_ref[rows, :].astype(F32)
            sg = _sigmoid(gate)
            silu = gate * sg
            d_out = dr_ref[rows, :].astype(F32)
            dg_ref[rows, :] = (d_out * xhat * gain * (sg * (1.0 + gate * (1.0 - sg)))).astype(BF16)
            dxh = d_out * gain * silu
            m1 = _group_sum(is_a, dxh) * (1.0 / HEAD_DIM)
            m2 = _group_sum(is_a, dxh * xhat) * (1.0 / HEAD_DIM)
            dy = (rstd * (dxh - m1 - xhat * m2)).astype(BF16)
            dy_s[rows, :] = dy
            qf = q_ref[rows, :].astype(F32)
            gr[n] = jnp.where(tab["diag2"], _dot_tn(_both_ways(qf, tab["qdec"]), dy), 0.0)
            return dgain + jnp.sum(d_out * xhat * silu, axis=0, keepdims=True)

        colsum = lambda x: jnp.sum(x, axis=0, keepdims=True)

        def grad_body(n, carry):
            xfb, ifa, ifb, iba, ibb, lf, lb = carry
            rows = pl.ds(pl.multiple_of(n * CHUNK, CHUNK), CHUNK)
            q = q_ref[rows, :]
            qf = q.astype(F32)
            k8f = k_ref[rows, :].astype(F32) * Q_SCALE
            k8 = k8f.astype(BF16)
            v = v_ref[rows, :]
            dy = dy_s[rows, :]
            q2, dy2 = _stack_pair(is_a, q), _stack_pair(is_a, dy)
            sc = _dot_nt(q2, k8)
            dp = _dot_nt(dy2, v)
            a2 = (sc * tab["d2"]).astype(BF16)
            ds2 = (dp * tab["d2"]).astype(BF16)
            dq = _unstack_pair(is_a, _dot_nn(ds2, k8))
            dk = _dot_tn(ds2, q2)
            dv = _dot_tn(a2, dy2)
            prod = sc * dp
            pf, pb = prod * tab["df2"], prod * tab["db2"]
            ifa, ifb = ifa + colsum(pf[0:CHUNK, :]), ifb + colsum(pf[CHUNK:2 * CHUNK, :])
            iba, ibb = iba + colsum(pb[0:CHUNK, :]), ibb + colsum(pb[CHUNK:2 * CHUNK, :])
            states, sgrads = st[n], gr[n]
            sb, gb = states.astype(BF16), sgrads.astype(BF16)
            dqc = _dot_nt(dy, sb) * tab["qdec"]
            dkc = _dot_nt(v, gb) * tab["kdec"]
            dv = dv + _dot_nn(_both_ways(k8f, tab["kdec"]), gb)
            dq_ref[rows, :] = (dq + dqc[:, 0:LANES] + dqc[:, LANES:2 * LANES]).astype(BF16)
            dk_ref[rows, :] = ((dk + dkc[:, 0:LANES] + dkc[:, LANES:2 * LANES]) * Q_SCALE).astype(BF16)
            dv_ref[rows, :] = dv.astype(BF16)
            q2w, k2w = jnp.concatenate([qf, qf], axis=1), jnp.concatenate([k8f, k8f], axis=1)
            xfb = xfb + colsum(tab["qexp"] * q2w * dqc + tab["kexp"] * k2w * dkc)
            prod_s = sgrads * states
            lf, lb = lf + colsum(prod_s[0:LANES, :]), lb + colsum(prod_s[LANES:2 * LANES, :])
            return xfb, ifa, ifb, iba, ibb, lf, lb

        sink_col = _sink_column(sink_ref, g)
        head_row = lax.broadcasted_iota(jnp.int32, dsink_ref.shape, 0)

        def attention_block(blk):
            n = half * blocks_half + blk
            rows = pl.ds(pl.multiple_of(blk * BLOCK, BLOCK), BLOCK)
            keys = pl.ds(pl.multiple_of(n * BLOCK, BLOCK), KEYS)
            qm = _stack_heads(aq_ref[rows, :], g)
            k3, v3 = kpad[keys, :], vpad[keys, :]
            p, p_sink = _attn_probs(qm, k3, bias, sink_col, n, s)
            dom = _stack_heads(do_ref[rows, :], g)
            dp = _dot_nt(dom, v3)
            delta = jnp.sum(p * dp, axis=1, keepdims=True)
            ds_mat = (p * (dp - delta)).astype(BF16)
            daq_ref[rows, :] = _unstack_heads(_dot_nn(ds_mat, k3), g).astype(BF16)
            dk_acc[keys, :] += _dot_tn(ds_mat, qm) * Q_SCALE
            dv_acc[keys, :] += _dot_tn(p.astype(BF16), dom)
            w = p_sink * delta
            upd = jnp.zeros(dsink_ref.shape, F32)
            for h in range(GROUP):
                upd = upd + jnp.where(head_row == h, -jnp.sum(w[h * BLOCK:(h + 1) * BLOCK, :]), 0.0)
            dsink_ref[...] += upd

        dgain = _chunk_loop(n_chunks, norm_body, jnp.zeros((1, LANES), F32), BWD_PREP_UNROLL)
        _scan_states(n_chunks, gr, BWD_ROWS, tab["lam_b"], FWD_ROWS, tab["lam_f"])

        def trip(i, carry):
            chunk, blk = 0, 0
            for kind in _trip_order(BWD_ORDER, RET_UNROLL, per_trip):
                if kind == "r":
                    carry = grad_body(i * RET_UNROLL + chunk, carry)
                    chunk += 1
                else:
                    attention_block(i * per_trip + blk)
                    blk += 1
            return carry

        z = jnp.zeros((1, LANES), F32)
        init = (jnp.zeros((1, 2 * LANES), F32), z, z, z, z, z, z)
        xfb, ifa, ifb, iba, ibb, lf, lb = lax.fori_loop(0, trips, trip, init)
        st_ref[...] = jnp.zeros_like(st_ref)
        st_ref[ST_GAIN:ST_GAIN + 1, :] = dgain
        st_ref[ST_XF:ST_XF + 1, :] = xfb[:, 0:LANES]
        st_ref[ST_XB:ST_XB + 1, :] = xfb[:, LANES:2 * LANES]
        st_ref[ST_IFA:ST_IFA + 1, :] = ifa
        st_ref[ST_IFB:ST_IFB + 1, :] = ifb
        st_ref[ST_IBA:ST_IBA + 1, :] = iba
        st_ref[ST_IBB:ST_IBB + 1, :] = ibb
        st_ref[ST_LF:ST_LF + 1, :] = lf * (CHUNK * tab["lam_f"])
        st_ref[ST_LB:ST_LB + 1, :] = lb * (CHUNK * tab["lam_b"])

        @pl.when(pair == pairs - 1)
        def _():
            dak_ref[...] = dk_acc[BLOCK:BLOCK + s, :].astype(BF16)
            dav_ref[...] = dv_acc[BLOCK:BLOCK + s, :].astype(BF16)

    lane_blk = lambda c0: _seq_spec(s, c0 // LANES)
    seq0 = _seq_spec(s, 0)
    half_rows = blocks_half * BLOCK
    aq_spec = pl.BlockSpec((half_rows, GW), lambda b, h: (b * PAIRS_PER_KV + (h & 1), C_AQ // GW + h // 2))
    a_spec = pl.BlockSpec((half_rows, GW), lambda b, h: (b * PAIRS_PER_KV + (h & 1), h // 2))
    kv_spec = lambda c0: pl.BlockSpec((s, LANES), lambda b, h: (b, c0 // LANES))
    kv_out = pl.BlockSpec((s, LANES), lambda b, h: (b, 0))
    state = pltpu.VMEM((n_chunks, 2 * LANES, LANES), F32)
    pad = pltpu.VMEM((s + 2 * BLOCK, LANES), BF16)
    acc = pltpu.VMEM((s + 2 * BLOCK, LANES), F32)
    body, lead = _after(after, body)
    return pl.pallas_call(
        body, name="mixers_bwd", grid=(b_loc, pairs),
        in_specs=lead + [_smem_spec(), _smem_spec(), _smem_spec(), lane_blk(C_RQ), lane_blk(C_RK), lane_blk(C_RV),
                         lane_blk(C_RG), seq0, seq0, seq0, pl.BlockSpec((1, LANES), lambda b, h: (0, h)),
                         aq_spec, kv_spec(C_AK), kv_spec(C_AV), a_spec, _state_spec(n_chunks, pairs)],
        out_specs=[seq0] * 4 + [pl.BlockSpec((ST_ROWS, LANES), lambda b, h: (b, h)), a_spec, kv_out, kv_out,
                                pl.BlockSpec((8, LANES), lambda b, h: (b * KV_HEADS + h // 2, 0))],
        out_shape=[SDS((t, RET_W), BF16)] * 4 + [SDS((b_loc * ST_ROWS, RET_W), F32), SDS((t, ATT_W), BF16),
                                                   SDS((t, KV_W), BF16), SDS((t, KV_W), BF16),
                                                   SDS((b_loc * KV_HEADS * 8, LANES), F32)],
        scratch_shapes=[state, pltpu.VMEM((s, LANES), BF16), pad, pad,
                        pltpu.VMEM((3, GROUP * BLOCK, KEYS), F32), acc, acc],
        compiler_params=_params(("arbitrary", "arbitrary")),
    )(*after, lgf, lgb, sink, u, u, u, u, xhat, rstd, dr, gn_gain, u, u, u, da, states)


def _pack_small(acc2, acc1, ret_stats, dsink, b_loc, d):
    pairs = RET_HEADS // 2

    def body(acc2_ref, acc1_ref, st_ref, dsink_ref, out_ref):
        out_ref[...] = jnp.zeros_like(out_ref)
        out_ref[ROW_LN1G:ROW_LN1G + 1, :] = acc1_ref[0:1, :]
        out_ref[ROW_LN1B:ROW_LN1B + 1, :] = acc1_ref[1:2, :]
        out_ref[ROW_LN2G:ROW_LN2G + 1, :] = acc2_ref[1:2, :]
        out_ref[ROW_LN2B:ROW_LN2B + 1, :] = acc2_ref[2:3, :]
        out_ref[ROW_LOSS:ROW_LOSS + 1, :] = acc2_ref[0:1, :]
        st = st_ref[0:ST_ROWS, :]
        for b in range(1, b_loc):
            st = st + st_ref[b * ST_ROWS:(b + 1) * ST_ROWS, :]
        out_ref[ROW_GN:ROW_GN + 1, 0:RET_W] = st[ST_GAIN:ST_GAIN + 1, :]
        lane = lax.broadcasted_iota(jnp.int32, (1, d), 1)
        misc = jnp.zeros((1, d), F32)
        for pr in range(pairs):
            blk = st[:, pr * LANES:(pr + 1) * LANES]
            half = lax.broadcasted_iota(jnp.int32, (1, LANES), 1) < HEAD_DIM
            for h in range(2):
                sel = half if h == 0 else jnp.logical_not(half)
                cross_f = jnp.sum(jnp.where(sel, blk[ST_XF:ST_XF + 1, :] + blk[ST_LF:ST_LF + 1, :], 0.0))
                cross_b = jnp.sum(jnp.where(sel, blk[ST_XB:ST_XB + 1, :] + blk[ST_LB:ST_LB + 1, :], 0.0))
                intra_f = jnp.sum(blk[ST_IFA + h:ST_IFA + h + 1, :])
                intra_b = jnp.sum(blk[ST_IBA + h:ST_IBA + h + 1, :])
                head = 2 * pr + h
                misc = jnp.where(lane == MISC_DF + head, cross_f + intra_f, misc)
                misc = jnp.where(lane == MISC_DB + head, cross_b + intra_b, misc)
        for g in range(KV_HEADS):
            tot = dsink_ref[g * 8:(g + 1) * 8, :]
            for b in range(1, b_loc):
                tot = tot + dsink_ref[(b * KV_HEADS + g) * 8:(b * KV_HEADS + g + 1) * 8, :]
            for h in range(GROUP):
                misc = jnp.where(lane == MISC_SINK + GROUP * g + h, jnp.sum(tot[h:h + 1, 0:1]), misc)
        out_ref[ROW_MISC:ROW_MISC + 1, :] = misc

    return pl.pallas_call(body, name="pack_small", out_shape=SDS((SMALL_ROWS, d), F32))(acc2, acc1, ret_stats, dsink)


BIG = ("w_in", "w_out", "w_ffn_gate", "w_ffn_up", "w_ffn_down", "w_ple_proj", "w_ple_gate")
TRANSPOSED_OUTSIDE = ("w_in", "w_ffn_gate", "w_ffn_up")
TRANSPOSED_HERE = ("w_ple_proj",)
SMALL = ("ret_decay_fwd", "ret_decay_bwd", "ret_gn_gain", "attn_sink", "ln1_gain", "ln1_bias", "ln2_gain", "ln2_bias")
ORDER = ("w_in", "ret_decay_fwd", "ret_decay_bwd", "ret_gn_gain", "attn_sink", "w_out", "ln1_gain", "ln1_bias",
         "w_ffn_gate", "w_ffn_up", "w_ffn_down", "w_ple_proj", "w_ple_gate", "ln2_gain", "ln2_bias")


GATHER_ORDER = ("w_in", "w_ffn_up", "w_out", "w_ffn_gate", "w_ple_gate", "w_ple_proj", "w_ffn_down")
GATHER_TWO_LEVEL = ("w_in", "w_ffn_up")

def _local_step(x2, p2, target2, fetch, publish, small, b_loc, me):
    d = x2.shape[1]
    lgf, lgb = _log_decay(small["ret_decay_fwd"], small["ret_decay_bwd"])
    lgf1, lgb1, sink1 = lgf.reshape(-1), lgb.reshape(-1), small["attn_sink"].reshape(-1)
    (w_in,) = fetch(("w_in",), ())
    u, xb = _in_proj(x2, w_in)
    passed = fetch.pass_on("w_ffn_up", (xb,))
    r, ret_xhat, ret_rstd, a, ret_states = _mixers_fwd(u, lgf1, lgb1, small["ret_gn_gain"], sink1, b_loc, passed)
    w_out, w_gate, w_up = fetch(("w_out", "w_ffn_gate", "w_ffn_up"), (r, a))
    xhat1, rstd1, h1b, dact_dg, dact_du, act = _mix_ln1_ffn_up(
        r, a, x2, w_out, w_gate, w_up, small["ln1_gain"], small["ln1_bias"])
    w_pg, w_pe, w_down = fetch(("w_ple_gate", "w_ple_proj", "w_ffn_down"), (act,))
    dz2, dz2b, dsb, dpleb, dg, dup, acc2 = _ffn_down_ln2_loss(
        act, dact_dg, dact_du, h1b, p2, xhat1, target2, w_down, w_pg, w_pe,
        small["ln1_gain"], small["ln1_bias"], small["ln2_gain"], small["ln2_bias"])
    own = {}

    def grad(name, parts, rhs, after=()):
        whole, own[name] = _weight_grad("grad_" + name, me, parts, rhs, after)
        return whole

    ffn_jobs = dict(w_ffn_down=(act, dz2b), w_ple_proj=(dpleb, p2), w_ple_gate=(h1b, dsb),
                    w_ffn_gate=(dg, h1b), w_ffn_up=(dup, h1b))
    wholes, owns = _weight_grad_jobs("grad_w_ffn", me, list(ffn_jobs.values()))
    own.update(zip(ffn_jobs, owns))
    t2 = publish("ffn", dict(zip(ffn_jobs, wholes)))
    dz1, dz1b, dr, da, acc1 = _dh1_ln1_bwd(
        dz2, dg, dup, dsb, xhat1, rstd1, w_gate, w_up, w_pg, w_out, small["ln1_gain"], t2)
    t3 = publish("out", dict(w_out=grad("w_out", [r, a], dz1b)))
    dq, dk, dv, dgate, ret_stats, daq, dak, dav, dsink = _mixers_bwd(
        u, ret_xhat, ret_rstd, ret_states, dr, da, lgf1, lgb1, small["ret_gn_gain"], sink1, b_loc, t3)
    parts = [dq, dk, dv, dgate, daq, dak, dav]
    small_part = _pack_small(acc2, acc1, ret_stats, dsink, b_loc, d)
    t4 = publish("in", dict(w_in=grad("w_in", parts, xb)), small_part)
    grad_x = _in_proj_bwd(dz1, parts, w_in, t4)
    return grad_x, own, small_part


def kernel(x, p, w_in, ret_decay_fwd, ret_decay_bwd, ret_gn_gain, attn_sink, w_out, ln1_gain, ln1_bias, w_ffn_gate, w_ffn_up, w_ffn_down, w_ple_proj, w_ple_gate, ln2_gain, ln2_bias, loss_target, m_w_in, m_ret_decay_fwd, m_ret_decay_bwd, m_ret_gn_gain, m_attn_sink, m_w_out, m_ln1_gain, m_ln1_bias, m_w_ffn_gate, m_w_ffn_up, m_w_ffn_down, m_w_ple_proj, m_w_ple_gate, m_ln2_gain, m_ln2_bias, v_w_in, v_ret_decay_fwd, v_ret_decay_bwd, v_ret_gn_gain, v_attn_sink, v_w_out, v_ln1_gain, v_ln1_bias, v_w_ffn_gate, v_w_ffn_up, v_w_ffn_down, v_w_ple_proj, v_w_ple_gate, v_ln2_gain, v_ln2_bias):
    given = dict(locals())

    def strip(n, a):
        if n not in BIG:
            return a
        return a[0].T if n in TRANSPOSED_OUTSIDE else a[0]

    def restore(n, a):
        if n not in BIG:
            return a
        return (a.T if n in TRANSPOSED_OUTSIDE else a)[None]

    w = {n: strip(n, given[n]) for n in ORDER}
    m = {n: strip(n, given["m_" + n]) for n in ORDER}
    v = {n: strip(n, given["v_" + n]) for n in ORDER}
    b_loc, s, d = x.shape
    x2 = x.reshape(b_loc * s, d)
    p2 = p[0].reshape(b_loc * s, p.shape[-1])
    target2 = loss_target.reshape(b_loc * s, d)

    small = {n: w[n] for n in SMALL}
    me = (4 * lax.axis_index("x") + 2 * lax.axis_index("y") + lax.axis_index("c")).astype(jnp.int32).reshape(1)

    gather = _gather_start(
        {n: w[n] for n in GATHER_ORDER},
        [_gather_copy_near if n in GATHER_TWO_LEVEL else _gather_copy for n in GATHER_ORDER])

    passing = {}

    def pass_on(n, after):
        passing[n] = _gather_relay("gather_relay_" + n, gather, GATHER_ORDER.index(n), list(after))
        return (passing[n]["token"],)

    def fetch(names, after):
        out = {}
        for n in [n for n in names if n in GATHER_TWO_LEVEL]:
            if n not in passing:
                pass_on(n, after)
            out[n] = _split_copy_wait("gather_wait_" + n, passing[n], [0], list(after))[0][0]
        direct = [n for n in names if n not in GATHER_TWO_LEVEL]
        if direct:
            got = _split_copy_wait("gather_wait_" + direct[0], gather, [GATHER_ORDER.index(n) for n in direct],
                                   list(after))
            out.update({n: item[0] for n, item in zip(direct, got)})
        return [out[n] for n in names]

    scatters = []

    def publish(tag, products, small_sums=None):
        items = [(products[n], lax.empty((N_DEV - 1, products[n].shape[0] // N_DEV, products[n].shape[1]), BF16))
                 for n in products]
        copies = [_scatter_copy] * len(items)
        if small_sums is not None:
            items.append((small_sums, lax.empty((N_DEV - 1,) + small_sums.shape, F32)))
            copies.append(_small_copy)
        started = _split_copy_start("scatter_start_" + tag, items, copies)
        scatters.append((list(products), small_sums is not None, started))
        return (started["token"],)

    fetch.pass_on = pass_on
    grad_x, own, small_part = _local_step(x2, p2, target2, fetch, publish, small, b_loc, me)

    out_g, out_d, out_m, out_v = {}, {}, {}, {}
    after = [grad_x]
    for names, with_small, started in scatters:
        landed = _split_copy_wait("scatter_wait_" + names[0], started, list(range(len(started["items"]))), after)
        if with_small:
            mine, from_peers = landed[-1]
            loss, sg, sd, sm, sv = _small_adamw(
                me, mine, from_peers, small, {n: m[n] for n in SMALL}, {n: v[n] for n in SMALL})
            for dst, src in ((out_g, sg), (out_d, sd), (out_m, sm), (out_v, sv)):
                dst.update(src)
        recv = {n: item[1] for n, item in zip(names, landed)}
        alike = {}
        for n in names:
            alike.setdefault((own[n].shape, n in TRANSPOSED_HERE), []).append(n)
        for (_, transposed), ns in alike.items():
            res = _reduce_adamw(ns[0], [own[n] for n in ns], [recv[n] for n in ns], [w[n] for n in ns],
                                [m[n] for n in ns], [v[n] for n in ns], transposed)
            for dst, vals in zip((out_g, out_d, out_m, out_v), res):
                dst.update(zip(ns, vals))
        after = [out_v[names[-1]]]

    outs = [loss[0, 0], grad_x.reshape(x.shape)]
    for group in (out_g, out_d, out_m, out_v):
        outs += [restore(n, group[n]) for n in ORDER]
    return tuple(outs)
```

```python
import functools

import jax
import jax.numpy as jnp
from jax import lax
from jax.experimental import pallas as pl
from jax.experimental.pallas import tpu as pltpu

F32, BF16 = jnp.float32, jnp.bfloat16
SDS = jax.ShapeDtypeStruct
MESH = pl.DeviceIdType.MESH

N_DEV = 8
HEAD_DIM = 64
RET_HEADS = 8
ATTN_HEADS = 8
KV_HEADS = 2
GROUP = ATTN_HEADS // KV_HEADS
RET_W = RET_HEADS * HEAD_DIM
ATT_W = ATTN_HEADS * HEAD_DIM
KV_W = KV_HEADS * HEAD_DIM
LANES = 128
CHUNK = 128
BLOCK = 128
Q_SCALE = HEAD_DIM ** -0.5
ALPHA = 2.0 ** 0.25
LN_EPS = 1e-5
GN_EPS = 1e-5
NEG_INF = -1e30
C_RQ, C_RK, C_RV, C_RG = 0, RET_W, 2 * RET_W, 3 * RET_W
C_AQ = 4 * RET_W
C_AK = C_AQ + ATT_W
C_AV = C_AK + KV_W
IN_W = C_AV + KV_W

ADAM_LR = 0.001
ADAM_B1 = 0.9
ADAM_B2 = 0.999
ADAM_EPS = 1e-08
ADAM_WD = 0.01
ADAM_STEP = 10

VMEM_LIMIT = 56 * 1024 * 1024
MATMUL_ROWS = 512
EPILOGUE_ROWS = 256
SUB_ROWS = 512
SMALL_ROWS = 16
ROW_LN1G, ROW_LN1B, ROW_LN2G, ROW_LN2B, ROW_LOSS, ROW_GN, ROW_MISC = 0, 1, 2, 3, 4, 5, 6
MISC_DF, MISC_DB, MISC_SINK = 0, 8, 16


def _dot_nn(a, b):
    return lax.dot_general(a, b, (((1,), (0,)), ((), ())), preferred_element_type=F32)


def _dot_nt(a, b):
    return lax.dot_general(a, b, (((1,), (1,)), ((), ())), preferred_element_type=F32)


def _dot_tn(a, b):
    return lax.dot_general(a, b, (((0,), (0,)), ((), ())), preferred_element_type=F32)


def _params(sem=None, vmem=VMEM_LIMIT):
    kw = {"vmem_limit_bytes": vmem}
    if sem is not None:
        kw["dimension_semantics"] = sem
    return pltpu.CompilerParams(**kw)


def _row_tile(t, want=512):
    tm = want
    while t % tm:
        tm //= 2
    return tm


def _sigmoid(x):
    return jax.nn.sigmoid(x)


def _layer_norm_stats(z):
    mu = jnp.mean(z, axis=1, keepdims=True)
    d = z - mu
    var = jnp.mean(d * d, axis=1, keepdims=True)
    rstd = lax.rsqrt(var + LN_EPS)
    return d * rstd, rstd


def _layer_norm_bwd(dxh, xhat, rstd):
    m1 = jnp.mean(dxh, axis=1, keepdims=True)
    m2 = jnp.mean(dxh * xhat, axis=1, keepdims=True)
    return rstd * (dxh - m1 - xhat * m2)


def _mesh_pos():
    return lax.axis_index("x"), lax.axis_index("y"), lax.axis_index("c")


HBM_SPEC = pl.BlockSpec(memory_space=pltpu.HBM)
SEM_SPEC = pl.BlockSpec(memory_space=pltpu.SEMAPHORE)
ANY_SPEC = pl.BlockSpec(memory_space=pl.ANY)
SIDE_EFFECT = pltpu.SideEffectType.DATAFLOW_SIDE_EFFECTING
PEER_SEMS = pltpu.SemaphoreType.DMA((N_DEV - 1,))


def _in_hbm(a):
    return pltpu.with_memory_space_constraint(a, pltpu.HBM)


def _split_copy_start(name, items, copies):
    n = len(items)
    flat = [a for it in items for a in it]
    k = len(flat)

    def body(*refs):
        arr, sems = list(refs[:k]), refs[k:k + 2 * n]
        for i, it in enumerate(items):
            mine = [arr.pop(0) for _ in it]
            for m in range(1, N_DEV):
                cp = copies[i](m, mine, sems[i].at[m - 1], sems[n + i].at[m - 1])
                if cp is not None:
                    cp.start()
        token = refs[-1]
        token[...] = jnp.zeros_like(token)

    res = pl.pallas_call(
        body, name=name,
        out_shape=[PEER_SEMS] * (2 * n) + [pltpu.HBM(a.shape, a.dtype) for a in flat] + [SDS((8, LANES), F32)],
        in_specs=[HBM_SPEC] * k,
        out_specs=[SEM_SPEC] * (2 * n) + [HBM_SPEC] * k + [pl.BlockSpec(memory_space=pltpu.VMEM)],
        input_output_aliases={j: 2 * n + j for j in range(k)},
        compiler_params=pltpu.CompilerParams(has_side_effects=SIDE_EFFECT),
    )(*[_in_hbm(a) for a in flat])
    thru, out_items = list(res[2 * n:2 * n + k]), []
    for it in items:
        out_items.append(tuple(thru.pop(0) for _ in it))
    return dict(send=res[:n], recv=res[n:2 * n], items=out_items, token=res[-1], copies=copies)


def _gather_start(shards, copies):
    names = list(shards)
    n = len(names)
    flip = [name in TRANSPOSED_HERE for name in names]
    shapes = [shards[name].shape[::-1] if f else shards[name].shape for name, f in zip(names, flip)]
    most = (max(s[0] for s in shapes), max(s[1] for s in shapes))

    def body(*refs):
        src, sems, land, token = refs[:n], refs[n:3 * n], refs[3 * n:4 * n], refs[4 * n]
        wide, narrow, sem = refs[4 * n + 1:]
        for i, (rows, cols) in enumerate(shapes):
            raw = wide.at[0:cols, 0:rows] if flip[i] else wide.at[0:rows, 0:cols]
            bring = pltpu.make_async_copy(src[i], raw, sem.at[0])
            bring.start()
            bring.wait()
            narrow[0:rows, 0:cols] = (raw[...].T if flip[i] else raw[...]).astype(BF16)
            mine = land[i].at[pl.ds(pl.multiple_of(_peer_index(0) * rows, 8), rows), :]
            place = pltpu.make_async_copy(narrow.at[0:rows, 0:cols], mine, sem.at[0])
            place.start()
            place.wait()
            for m in range(1, N_DEV):
                cp = copies[i](m, [land[i]], sems[i].at[m - 1], sems[n + i].at[m - 1])
                if cp is not None:
                    cp.start()
        token[...] = jnp.zeros_like(token)

    side = max(most)
    res = pl.pallas_call(
        body, name="gather_start",
        out_shape=[PEER_SEMS] * (2 * n) + [pltpu.HBM((N_DEV * r, c), BF16) for r, c in shapes] + [SDS((8, LANES), F32)],
        in_specs=[HBM_SPEC] * n,
        out_specs=[SEM_SPEC] * (2 * n) + [HBM_SPEC] * n + [pl.BlockSpec(memory_space=pltpu.VMEM)],
        scratch_shapes=[pltpu.VMEM((side, side), F32), pltpu.VMEM(most, BF16), pltpu.SemaphoreType.DMA((1,))],
        compiler_params=pltpu.CompilerParams(has_side_effects=SIDE_EFFECT),
    )(*[_in_hbm(shards[name]) for name in names])
    return dict(send=res[:n], recv=res[n:2 * n], items=[(a,) for a in res[2 * n:3 * n]], token=res[-1], copies=copies)


def _gather_relay(name, started, which, after):
    lands = [started["items"][w][0] for w in which]
    k = len(lands)

    def body(*refs):
        land_refs, old_send, old_recv = refs[:k], refs[k:2 * k], refs[2 * k:3 * k]
        outs = refs[3 * k + len(after):]
        send, recv, token = outs[:k], outs[k:2 * k], outs[-1]
        for j in range(k):
            for m in range(1, N_DEV):
                cp = _gather_copy_near(m, [land_refs[j]], old_send[j].at[m - 1], old_recv[j].at[m - 1])
                if cp is None:
                    continue
                cp.wait_send()
                cp.wait_recv()
                if m > 1:
                    _gather_copy_pass(m + 1, [land_refs[j]], send[j].at[m], recv[j].at[m]).start()
        token[...] = jnp.zeros_like(token)

    res = pl.pallas_call(
        body, name=name,
        out_shape=[PEER_SEMS] * (2 * k) + [pltpu.HBM(a.shape, a.dtype) for a in lands] + [SDS((8, LANES), F32)],
        in_specs=[HBM_SPEC] * k + [SEM_SPEC] * (2 * k) + [ANY_SPEC] * len(after),
        out_specs=[SEM_SPEC] * (2 * k) + [HBM_SPEC] * k + [pl.BlockSpec(memory_space=pltpu.VMEM)],
        input_output_aliases={j: 2 * k + j for j in range(k)},
        compiler_params=pltpu.CompilerParams(has_side_effects=SIDE_EFFECT),
    )(*lands, *[started["send"][w] for w in which], *[started["recv"][w] for w in which],
      *[_in_hbm(a) for a in after])
    return dict(send=res[:k], recv=res[k:2 * k], items=[(a,) for a in res[2 * k:3 * k]], token=res[-1],
                copies=[_gather_copy_pass] * k)


def _split_copy_wait(name, started, which, after):
    items = [started["items"][i] for i in which]
    copies = [started["copies"][i] for i in which]
    n = len(items)
    flat = [a for it in items for a in it]
    k = len(flat)

    def body(*refs):
        arr, sems = list(refs[:k]), refs[k:k + 2 * n]
        for i, it in enumerate(items):
            mine = [arr.pop(0) for _ in it]
            for m in range(1, N_DEV):
                cp = copies[i](m, mine, sems[i].at[m - 1], sems[n + i].at[m - 1])
                if cp is not None:
                    cp.wait_send()
                    cp.wait_recv()

    res = pl.pallas_call(
        body, name=name,
        out_shape=[pltpu.HBM(a.shape, a.dtype) for a in flat],
        in_specs=[HBM_SPEC] * k + [SEM_SPEC] * (2 * n) + [ANY_SPEC] * len(after),
        out_specs=[HBM_SPEC] * k,
        input_output_aliases={j: j for j in range(k)},
        compiler_params=pltpu.CompilerParams(has_side_effects=SIDE_EFFECT),
    )(*flat, *[started["send"][i] for i in which], *[started["recv"][i] for i in which], *[_in_hbm(a) for a in after])
    thru, out_items = list(res), []
    for it in items:
        out_items.append(tuple(thru.pop(0) for _ in it))
    return out_items


def _gather_copy(m, refs, send_sem, recv_sem):
    (land_ref,) = refs
    r = land_ref.shape[0] // N_DEV
    mine = land_ref.at[pl.ds(pl.multiple_of(_peer_index(0) * r, 8), r), :]
    return pltpu.make_async_remote_copy(src_ref=mine, dst_ref=mine, send_sem=send_sem, recv_sem=recv_sem,
                                        device_id=_peer(m), device_id_type=MESH)


def _gather_copy_near(m, refs, send_sem, recv_sem):
    return _gather_copy(m, refs, send_sem, recv_sem) if m == 1 or m % 2 == 0 else None


def _gather_copy_pass(m, refs, send_sem, recv_sem):
    if m == 1 or m % 2 == 0:
        return None
    (land_ref,) = refs
    r = land_ref.shape[0] // N_DEV
    block = land_ref.at[pl.ds(pl.multiple_of(_peer_index(m ^ 1) * r, 8), r), :]
    return pltpu.make_async_remote_copy(src_ref=block, dst_ref=block, send_sem=send_sem, recv_sem=recv_sem,
                                        device_id=_peer(1), device_id_type=MESH)


def _small_copy(m, refs, send_sem, recv_sem):
    part_ref, land_ref = refs
    return pltpu.make_async_remote_copy(src_ref=part_ref, dst_ref=land_ref.at[m - 1], send_sem=send_sem,
                                        recv_sem=recv_sem, device_id=_peer(m), device_id_type=MESH)


def _scatter_copy(m, refs, send_sem, recv_sem):
    buf_ref, land_ref = refs
    r = buf_ref.shape[0] // N_DEV
    src = buf_ref.at[pl.ds(pl.multiple_of(_peer_index(m) * r, 8), r), :]
    return pltpu.make_async_remote_copy(src_ref=src, dst_ref=land_ref.at[m - 1], send_sem=send_sem,
                                        recv_sem=recv_sem, device_id=_peer(m), device_id_type=MESH)


def _peer(m):
    x, y, c = _mesh_pos()
    bx, by, bc = (m >> 2) & 1, (m >> 1) & 1, m & 1
    return (x ^ bx if bx else x, y ^ by if by else y, c ^ bc if bc else c)


def _peer_index(m):
    x, y, c = _mesh_pos()
    return (4 * x + 2 * y + c) ^ m


SMALL_PLACE = {
    "ln1_gain": (ROW_LN1G, 0), "ln1_bias": (ROW_LN1B, 0), "ln2_gain": (ROW_LN2G, 0), "ln2_bias": (ROW_LN2B, 0),
    "ret_gn_gain": (ROW_GN, 0), "ret_decay_fwd": (ROW_MISC, MISC_DF), "ret_decay_bwd": (ROW_MISC, MISC_DB),
    "attn_sink": (ROW_MISC, MISC_SINK)}


def _small_adamw(me, part, landed, w, m, v):
    d = part.shape[1]
    names = list(SMALL_PLACE)
    k = len(names)

    def body(*refs):
        me_ref, part_ref, land_ref = refs[:3]
        refs = refs[2:]
        w_refs, m_refs, v_refs = refs[1:1 + k], refs[1 + k:1 + 2 * k], refs[1 + 2 * k:1 + 3 * k]
        outs = refs[1 + 3 * k:1 + 7 * k + 1]
        tot_ref = refs[-1]
        loss_ref, g_refs, dl_refs = outs[0], outs[1:1 + k], outs[1 + k:1 + 2 * k]
        nm_refs, nv_refs = outs[1 + 2 * k:1 + 3 * k], outs[1 + 3 * k:1 + 4 * k]
        tot = jnp.zeros(part_ref.shape, F32)
        for dev in range(N_DEV):
            j = dev ^ me_ref[0]
            tot = tot + jnp.where(j == 0, part_ref[...], land_ref[jnp.maximum(j, 1) - 1])
        tot_ref[...] = tot
        loss_ref[...] = (0.5 / d) * jnp.sum(tot_ref[ROW_LOSS:ROW_LOSS + 1, :], axis=1, keepdims=True)
        for i, name in enumerate(names):
            row, lo = SMALL_PLACE[name]
            wv = w_refs[i][...]
            g = tot_ref[row:row + 1, lo:lo + wv.shape[1]]
            if name.startswith("ret_decay"):
                p2 = jnp.exp2(wv)
                g = g * (-p2 * jnp.log(2.0) / (1.0 - p2))
            g_refs[i][...] = g
            _adamw_store(g, wv, m_refs[i][...], v_refs[i][...], dl_refs[i], nm_refs[i], nv_refs[i])

    shapes = [SDS(w[n].shape, F32) for n in names]
    vm = pl.BlockSpec(memory_space=pltpu.VMEM)
    res = pl.pallas_call(
        body, name="small_adamw", out_shape=[SDS((1, 1), F32)] + shapes * 4,
        in_specs=[_smem_spec()] + [vm] * (2 + 3 * k), out_specs=[vm] * (1 + 4 * k),
        scratch_shapes=[pltpu.VMEM(part.shape, F32)],
    )(me, part, landed, *[w[n] for n in names], *[m[n] for n in names], *[v[n] for n in names])
    groups = [dict(zip(names, res[1 + j * k:1 + (j + 1) * k])) for j in range(4)]
    return (res[0], *groups)


def _adamw_store(g, w, m, v, dl_ref, nm_ref, nv_ref):
    m = ADAM_B1 * m + (1.0 - ADAM_B1) * g
    v = ADAM_B2 * v + (1.0 - ADAM_B2) * (g * g)
    m_hat = m / (1.0 - ADAM_B1 ** ADAM_STEP)
    v_hat = v / (1.0 - ADAM_B2 ** ADAM_STEP)
    dl_ref[...] = -ADAM_LR * (m_hat / (jnp.sqrt(v_hat) + ADAM_EPS) + ADAM_WD * w)
    nm_ref[...] = m
    nv_ref[...] = v


def _reduce_adamw(name, owns, recvs, ws, ms, vs, transposed):
    count = len(owns)
    rows, n = owns[0].shape
    steps = 1 if transposed or rows % 16 else 2
    rb = rows // steps

    def body(*refs):
        ins, outs = refs[:5 * count], refs[5 * count:]
        j = pl.program_id(0)
        for k in range(count):
            @pl.when(j == k)
            def _(k=k):
                own_ref, recv_ref, w_ref, m_ref, v_ref = ins[5 * k:5 * k + 5]
                g_ref, dl_ref, nm_ref, nv_ref = outs[4 * k:4 * k + 4]
                g = own_ref[...]
                for p in range(recv_ref.shape[0]):
                    g = g + recv_ref[p].astype(F32)
                if transposed:
                    g = g.T
                g_ref[...] = g
                _adamw_store(g, w_ref[...], m_ref[...], v_ref[...], dl_ref, nm_ref, nv_ref)

    def turn(k):
        return lambda j, i: jnp.where(j == k, i, jnp.where(j < k, 0, steps - 1))

    in_specs, out_specs = [], []
    for k in range(count):
        at = turn(k)
        blk = pl.BlockSpec(ws[0].shape if transposed else (rb, n), lambda j, i, at=at: (at(j, i), 0))
        in_specs += [pl.BlockSpec((rb, n), lambda j, i, at=at: (at(j, i), 0)),
                     pl.BlockSpec((recvs[k].shape[0], rb, n), lambda j, i, at=at: (0, at(j, i), 0)), blk, blk, blk]
        out_specs += [blk] * 4
    res = pl.pallas_call(
        body, name="adamw_" + name, grid=(count, steps), in_specs=in_specs, out_specs=out_specs,
        out_shape=[SDS(ws[0].shape, F32)] * (4 * count), compiler_params=_params(("arbitrary", "arbitrary")),
    )(*[a for k in range(count) for a in (owns[k], recvs[k], ws[k], ms[k], vs[k])])
    return [list(res[j::4]) for j in range(4)]


def _row_spec(tm, width):
    return pl.BlockSpec((tm, width), lambda i: (i, 0))


def _full_spec(shape):
    return pl.BlockSpec(shape, lambda i: (0,) * len(shape))


_acc_spec = _full_spec


def _sub_rows(tm):
    step = min(SUB_ROWS, tm)
    return [(lo, lo + step) for lo in range(0, tm, step)]


def _in_proj(x2, wt_in):
    t, d = x2.shape
    u_w = wt_in.shape[0]
    tm = _row_tile(t, MATMUL_ROWS)

    def body(x_ref, w_ref, u_ref, xb_ref):
        xb = x_ref[...].astype(BF16)
        xb_ref[...] = xb
        u_ref[...] = _dot_nt(xb, w_ref[...]).astype(BF16)

    return pl.pallas_call(
        body, name="in_proj", grid=(t // tm,),
        in_specs=[_row_spec(tm, d), _full_spec(wt_in.shape)],
        out_specs=[_row_spec(tm, u_w), _row_spec(tm, d)],
        out_shape=[SDS((t, u_w), BF16), SDS((t, d), BF16)],
        compiler_params=_params(("parallel",)),
    )(x2, wt_in)


def _col_halves(f):
    n = f // LANES
    k = (n + 1) // 2 * LANES
    return [(0, k), (k, f)] if k < f else [(0, f)]


def _mix_ln1_ffn_up(r, a, x2, w_out, wt_gate, wt_up, g1, b1):
    t, d = x2.shape
    f = wt_gate.shape[0]
    tm = _row_tile(t, EPILOGUE_ROWS)

    def body(r_ref, a_ref, x_ref, wo_ref, wg_ref, wu_ref, g_ref, b_ref, xh_ref, rs_ref, hb_ref, dg_ref, du_ref,
             act_ref):
        mix = _dot_nn(r_ref[...], wo_ref[0:RET_W, :]) + _dot_nn(a_ref[...], wo_ref[RET_W:RET_W + ATT_W, :])
        z = ALPHA * x_ref[...] + mix
        xhat, rstd = _layer_norm_stats(z)
        xh_ref[...] = xhat
        rs_ref[...] = jnp.broadcast_to(rstd, rs_ref.shape)
        h = (xhat * g_ref[...] + b_ref[...]).astype(BF16)
        hb_ref[...] = h
        g = _dot_nt(h, wg_ref[...])
        u = _dot_nt(h, wu_ref[...])
        sg = _sigmoid(g)
        silu = g * sg
        dg_ref[...] = (u * (sg * (1.0 + g * (1.0 - sg)))).astype(BF16)
        du_ref[...] = silu.astype(BF16)
        act_ref[...] = (silu * u).astype(BF16)

    wide, narrow = _row_spec(tm, f), _row_spec(tm, d)
    return pl.pallas_call(
        body, name="mix_ln1_ffn_up", grid=(t // tm,),
        in_specs=[_row_spec(tm, RET_W), _row_spec(tm, ATT_W), narrow, _resident_spec(w_out.shape),
                  _resident_spec(wt_gate.shape), _resident_spec(wt_up.shape), _full_spec(g1.shape),
                  _full_spec(b1.shape)],
        out_specs=[narrow, _row_spec(tm, LANES), narrow, wide, wide, wide],
        out_shape=[SDS((t, d), F32), SDS((t, LANES), F32), SDS((t, d), BF16)] + [SDS((t, f), BF16)] * 3,
        compiler_params=_params(("parallel",)),
    )(r, a, x2, w_out, wt_gate, wt_up, g1, b1)


def _ffn_down_ln2_loss(act, dact_dg, dact_du, h1b, p2, xhat1, target, w_down, w_pg, wt_pe, g1, b1, g2, b2):
    t, d = xhat1.shape
    f = act.shape[1]
    pdim = p2.shape[1]
    tm = _row_tile(t, EPILOGUE_ROWS)

    def body(act_ref, fg_ref, fu_ref, hb_ref, p_ref, xh1_ref, tgt_ref, wd_ref, wpg_ref, wpe_ref, g1_ref, b1_ref,
             g2_ref, b2_ref, dz_ref, dzb_ref, ds_ref, dple_ref, dg_ref, du_ref, acc_ref):
        @pl.when(pl.program_id(0) == 0)
        def _():
            acc_ref[...] = jnp.zeros_like(acc_ref)

        for lo, hi in _sub_rows(tm):
            h1 = xh1_ref[lo:hi, :] * g1_ref[...] + b1_ref[...]
            pg = _sigmoid(_dot_nn(hb_ref[lo:hi, :], wpg_ref[...]))
            ple = _dot_nt(p_ref[lo:hi, :].astype(BF16), wpe_ref[...])
            gated = pg * ple
            dgate = gated * (1.0 - pg)
            ffn = _dot_nn(act_ref[lo:hi, :], wd_ref[...])
            z2 = ALPHA * h1 + gated + ffn
            xhat2, rstd2 = _layer_norm_stats(z2)
            err = xhat2 * g2_ref[...] + b2_ref[...] - tgt_ref[lo:hi, :]
            dy = err * (1.0 / d)
            dz = _layer_norm_bwd(dy * g2_ref[...], xhat2, rstd2)
            dzb = dz.astype(BF16)
            dz_ref[lo:hi, :] = dz
            dzb_ref[lo:hi, :] = dzb
            ds_ref[lo:hi, :] = (dz * dgate).astype(BF16)
            dple_ref[lo:hi, :] = (dz * pg).astype(BF16)
            acc_ref[0:1, :] += jnp.sum(err * err, axis=0, keepdims=True)
            acc_ref[1:2, :] += jnp.sum(dy * xhat2, axis=0, keepdims=True)
            acc_ref[2:3, :] += jnp.sum(dy, axis=0, keepdims=True)
            for c0, c1 in _col_halves(f):
                da = _dot_nt(dzb, wd_ref[c0:c1, :])
                dg_ref[lo:hi, c0:c1] = (da * fg_ref[lo:hi, c0:c1].astype(F32)).astype(BF16)
                du_ref[lo:hi, c0:c1] = (da * fu_ref[lo:hi, c0:c1].astype(F32)).astype(BF16)

    vec = _full_spec(g1.shape)
    wide, narrow = _row_spec(tm, f), _row_spec(tm, d)
    return pl.pallas_call(
        body, name="ffn_down_ln2_loss", grid=(t // tm,),
        in_specs=[wide, wide, wide, narrow, _row_spec(tm, pdim), narrow, narrow,
                  _full_spec(w_down.shape), _full_spec(w_pg.shape), _full_spec(wt_pe.shape), vec, vec, vec, vec],
        out_specs=[narrow] * 4 + [wide, wide, _acc_spec((8, d))],
        out_shape=[SDS((t, d), F32), SDS((t, d), BF16), SDS((t, d), BF16), SDS((t, d), BF16),
                   SDS((t, f), BF16), SDS((t, f), BF16), SDS((8, d), F32)],
        compiler_params=_params(("arbitrary",)),
    )(act, dact_dg, dact_du, h1b, p2, xhat1, target, w_down, w_pg, wt_pe, g1, b1, g2, b2)


def _after(after, body):
    k = len(after)
    return (lambda *refs: body(*refs[k:])), [ANY_SPEC] * k


def _resident_spec(shape):
    return pl.BlockSpec(shape, lambda i: (0,) * len(shape), pipeline_mode=pl.Buffered(1))


def _dh1_ln1_bwd(dz2, dg, dup, dsb, xhat1, rstd1, wt_gate, wt_up, w_pg, w_out, g1, after=()):
    t, d = dz2.shape
    f = dg.shape[1]
    tm = _row_tile(t, MATMUL_ROWS)

    def body(dz_ref, dg_ref, du_ref, ds_ref, xh1_ref, rs1_ref, wg_ref, wu_ref, wpg_ref, wo_ref, g1_ref,
             dz1_ref, dz1b_ref, dr_ref, da_ref, acc_ref):
        @pl.when(pl.program_id(0) == 0)
        def _():
            acc_ref[...] = jnp.zeros_like(acc_ref)

        for lo, hi in _sub_rows(tm):
            dh = (ALPHA * dz_ref[lo:hi, :] + _dot_nn(dg_ref[lo:hi, :], wg_ref[...])
                  + _dot_nn(du_ref[lo:hi, :], wu_ref[...]) + _dot_nt(ds_ref[lo:hi, :], wpg_ref[...]))
            xhat, rstd = xh1_ref[lo:hi, :], rs1_ref[lo:hi, 0:1]
            dz1 = _layer_norm_bwd(dh * g1_ref[...], xhat, rstd)
            dz1b = dz1.astype(BF16)
            dz1_ref[lo:hi, :] = dz1
            dz1b_ref[lo:hi, :] = dz1b
            acc_ref[0:1, :] += jnp.sum(dh * xhat, axis=0, keepdims=True)
            acc_ref[1:2, :] += jnp.sum(dh, axis=0, keepdims=True)
            dr_ref[lo:hi, :] = _dot_nt(dz1b, wo_ref[0:RET_W, :]).astype(BF16)
            da_ref[lo:hi, :] = _dot_nt(dz1b, wo_ref[RET_W:RET_W + ATT_W, :]).astype(BF16)

    body, lead = _after(after, body)
    return pl.pallas_call(
        body, name="dh1_ln1_bwd", grid=(t // tm,),
        in_specs=lead + [_row_spec(tm, d), _row_spec(tm, f), _row_spec(tm, f), _row_spec(tm, d), _row_spec(tm, d),
                         _row_spec(tm, LANES), _resident_spec(wt_gate.shape), _resident_spec(wt_up.shape), _resident_spec(w_pg.shape),
                         _resident_spec(w_out.shape), _full_spec(g1.shape)],
        out_specs=[_row_spec(tm, d), _row_spec(tm, d), _row_spec(tm, RET_W), _row_spec(tm, ATT_W), _acc_spec((8, d))],
        out_shape=[SDS((t, d), F32), SDS((t, d), BF16), SDS((t, RET_W), BF16), SDS((t, ATT_W), BF16),
                   SDS((8, d), F32)],
        compiler_params=_params(("arbitrary",)),
    )(*after, dz2, dg, dup, dsb, xhat1, rstd1, wt_gate, wt_up, w_pg, w_out, g1)


def _in_proj_bwd(dz1, parts, wt_in, after=()):
    t, d = dz1.shape
    tm = _row_tile(t, MATMUL_ROWS)
    widths = [p.shape[1] for p in parts]

    def body(*refs):
        dz_ref, part_refs, w_ref, dx_ref = refs[0], refs[1:1 + len(parts)], refs[-2], refs[-1]
        acc = ALPHA * dz_ref[...]
        lo = 0
        for p_ref, w in zip(part_refs, widths):
            acc = acc + _dot_nn(p_ref[...], w_ref[lo:lo + w, :])
            lo += w
        dx_ref[...] = acc

    body, lead = _after(after, body)
    return pl.pallas_call(
        body, name="in_proj_bwd", grid=(t // tm,),
        in_specs=lead + [_row_spec(tm, d)] + [_row_spec(tm, w) for w in widths] + [_full_spec(wt_in.shape)],
        out_specs=_row_spec(tm, d), out_shape=SDS((t, d), F32),
        compiler_params=_params(("parallel",)),
    )(*after, dz1, *parts, wt_in)


def _weight_grad(name, me, parts, rhs, after=()):
    t, n = rhs.shape
    widths = [p.shape[1] for p in parts]
    rows = sum(widths)
    own_rows = rows // N_DEV
    tk = _row_tile(t, MATMUL_ROWS)
    n_steps = t // tk
    step = 256

    def body(*refs):
        me_ref, part_refs, rhs_ref = refs[0], refs[1:1 + len(parts)], refs[1 + len(parts)]
        full_ref, own_ref, acc = refs[-3], refs[-2], refs[-1]
        i = pl.program_id(0)

        def products(first):
            b = rhs_ref[...].astype(BF16)
            lo = 0
            for p_ref, w in zip(part_refs, widths):
                for c0 in range(0, w, step):
                    c1 = min(c0 + step, w)
                    val = _dot_tn(p_ref[:, c0:c1].astype(BF16), b)
                    if first:
                        acc[lo + c0:lo + c1, :] = val
                    else:
                        acc[lo + c0:lo + c1, :] += val
                lo += w

        pl.when(i == 0)(functools.partial(products, True))
        pl.when(i > 0)(functools.partial(products, False))

        @pl.when(i == n_steps - 1)
        def _():
            full_ref[...] = acc[...].astype(BF16)
            own_ref[...] = acc[pl.ds(pl.multiple_of(me_ref[0] * own_rows, 8), own_rows), :]

    body, lead = _after(after, body)
    return pl.pallas_call(
        body, name=name, grid=(n_steps,),
        in_specs=lead + [_smem_spec()] + [_row_spec(tk, w) for w in widths] + [_row_spec(tk, n)],
        out_specs=[_full_spec((rows, n)), _full_spec((own_rows, n))],
        out_shape=[SDS((rows, n), BF16), SDS((own_rows, n), F32)],
        scratch_shapes=[pltpu.VMEM((rows, n), F32)],
        compiler_params=_params(("arbitrary",)),
    )(*after, me, *parts, rhs)


def _weight_grad_jobs(name, me, jobs, after=()):
    count = len(jobs)
    t = jobs[0][0].shape[0]
    tk = _row_tile(t, MATMUL_ROWS)
    n_steps = t // tk
    shapes = [(lhs.shape[1], rhs.shape[1]) for lhs, rhs in jobs]
    most_rows, most_cols = max(r for r, _ in shapes), max(n for _, n in shapes)
    step = 256

    def body(*refs):
        me_ref, lhs_refs, rhs_refs = refs[0], refs[1:1 + count], refs[1 + count:1 + 2 * count]
        full_refs, own_refs = refs[1 + 2 * count:1 + 3 * count], refs[1 + 3 * count:1 + 4 * count]
        acc, whole, mine, sems = refs[1 + 4 * count:]
        job, i = pl.program_id(0), pl.program_id(1)

        def leaving(j):
            rows, n = shapes[j]
            return (pltpu.make_async_copy(whole.at[0:rows, 0:n], full_refs[j], sems.at[0]),
                    pltpu.make_async_copy(mine.at[0:rows // N_DEV, 0:n], own_refs[j], sems.at[1]))

        def products(j, first):
            rows, n = shapes[j]
            b = rhs_refs[j][...].astype(BF16)
            for c0 in range(0, rows, step):
                c1 = min(c0 + step, rows)
                val = _dot_tn(lhs_refs[j][:, c0:c1].astype(BF16), b)
                if first:
                    acc[c0:c1, 0:n] = val
                else:
                    acc[c0:c1, 0:n] += val

        def finish(j):
            rows, n = shapes[j]
            own_rows = rows // N_DEV
            if j > 0:
                for cp in leaving(j - 1):
                    cp.wait()
            whole[0:rows, 0:n] = acc[0:rows, 0:n].astype(BF16)
            mine[0:own_rows, 0:n] = acc[pl.ds(pl.multiple_of(me_ref[0] * own_rows, 8), own_rows), 0:n]
            for cp in leaving(j):
                cp.start()
            if j == count - 1:
                for cp in leaving(j):
                    cp.wait()

        for j in range(count):
            pl.when((job == j) & (i == 0))(functools.partial(products, j, True))
            pl.when((job == j) & (i > 0))(functools.partial(products, j, False))
            pl.when((job == j) & (i == n_steps - 1))(functools.partial(finish, j))

    def turn(j):
        return lambda job, i: (jnp.where(job == j, i, jnp.where(job < j, 0, n_steps - 1)), 0)

    body, lead = _after(after, body)
    res = pl.pallas_call(
        body, name=name, grid=(count, n_steps),
        in_specs=lead + [_smem_spec()] + [pl.BlockSpec((tk, rows), turn(j)) for j, (rows, _) in enumerate(shapes)]
        + [pl.BlockSpec((tk, n), turn(j)) for j, (_, n) in enumerate(shapes)],
        out_specs=[ANY_SPEC] * (2 * count),
        out_shape=[SDS((rows, n), BF16) for rows, n in shapes] + [SDS((rows // N_DEV, n), F32) for rows, n in shapes],
        scratch_shapes=[pltpu.VMEM((most_rows, most_cols), F32), pltpu.VMEM((most_rows, most_cols), BF16),
                        pltpu.VMEM((most_rows // N_DEV, most_cols), F32), pltpu.SemaphoreType.DMA((2,))],
        compiler_params=_params(("arbitrary", "arbitrary")),
    )(*after, me, *[lhs for lhs, _ in jobs], *[rhs for _, rhs in jobs])
    return list(res[:count]), list(res[count:])


def _log_decay(decay_f, decay_b):
    def body(f_ref, b_ref, lf_ref, lb_ref):
        lf_ref[...] = jnp.log1p(-jnp.exp2(f_ref[...]))
        lb_ref[...] = jnp.log1p(-jnp.exp2(b_ref[...]))

    return pl.pallas_call(body, name="log_decay", out_shape=[SDS(decay_f.shape, F32)] * 2)(decay_f, decay_b)


def _chunk(ref, n):
    return ref[pl.ds(pl.multiple_of(n * CHUNK, CHUNK), CHUNK), :]


def _group_sum(is_a, v):
    sa = jnp.sum(jnp.where(is_a, v, 0.0), axis=1, keepdims=True)
    sb = jnp.sum(jnp.where(is_a, 0.0, v), axis=1, keepdims=True)
    return jnp.where(is_a, sa, sb)


def _seq_spec(s, col_block):
    return pl.BlockSpec((s, LANES), lambda b, h: (b, col_block + h))


def _smem_spec():
    return pl.BlockSpec(memory_space=pltpu.SMEM)


RET_UNROLL = 4
FWD_PREP_UNROLL = 16
BWD_PREP_UNROLL = 8


def _chunk_loop(n_chunks, body, init, unroll):
    u = unroll if n_chunks % unroll == 0 else 1

    def trip(i, carry):
        for j in range(u):
            carry = body(i * u + j, carry)
        return carry

    return lax.fori_loop(0, n_chunks // u, trip, init)


def _stacked_tables(lgf_ref, lgb_ref, pair):
    lane = lax.broadcasted_iota(jnp.int32, (1, LANES), 1)
    is_a = lane < HEAD_DIM
    lgf = jnp.where(is_a, lgf_ref[2 * pair], lgf_ref[2 * pair + 1])
    lgb = jnp.where(is_a, lgb_ref[2 * pair], lgb_ref[2 * pair + 1])
    row = lax.broadcasted_iota(jnp.int32, (CHUNK, 1), 0).astype(F32)
    kdec_f, qdec_f = jnp.exp(lgf * (CHUNK - 1.0 - row)), jnp.exp(lgf * (row + 1.0))
    kdec_b, qdec_b = jnp.exp(lgb * row), jnp.exp(lgb * (CHUNK - row))
    tab = dict(
        is_a=is_a, row=row, lam_f=jnp.exp(lgf * CHUNK), lam_b=jnp.exp(lgb * CHUNK),
        kdec=jnp.concatenate([kdec_f, kdec_b], axis=1), qdec=jnp.concatenate([qdec_f, qdec_b], axis=1),
        qexp=jnp.concatenate([jnp.broadcast_to(row + 1.0, (CHUNK, LANES)),
                              jnp.broadcast_to(CHUNK - row, (CHUNK, LANES))], axis=1),
        kexp=jnp.concatenate([jnp.broadcast_to(CHUNK - 1.0 - row, (CHUNK, LANES)),
                              jnp.broadcast_to(row, (CHUNK, LANES))], axis=1),
    )
    r = lax.broadcasted_iota(jnp.int32, (2 * LANES, LANES), 0)
    c = lax.broadcasted_iota(jnp.int32, (2 * LANES, LANES), 1)
    tab["diag2"] = ((r & (LANES - 1)) < HEAD_DIM) == (c < HEAD_DIM)
    i2 = lax.broadcasted_iota(jnp.int32, (2 * CHUNK, CHUNK), 0)
    j = lax.broadcasted_iota(jnp.int32, (2 * CHUNK, CHUNK), 1)
    head_b = i2 >= CHUNK
    diff = ((i2 & (CHUNK - 1)) - j).astype(F32)
    up, dn = jnp.maximum(diff, 0.0), jnp.maximum(-diff, 0.0)
    lgf2 = jnp.where(head_b, lgf_ref[2 * pair + 1], lgf_ref[2 * pair])
    lgb2 = jnp.where(head_b, lgb_ref[2 * pair + 1], lgb_ref[2 * pair])
    ef = jnp.where(diff >= 0, jnp.exp(lgf2 * up), 0.0)
    eb = jnp.where(diff <= 0, jnp.exp(lgb2 * dn), 0.0)
    tab["d2"] = ef + eb
    tab["df2"] = ef * up
    tab["db2"] = eb * dn
    return tab


def _stack_pair(is_a, x):
    zero = jnp.zeros_like(x)
    return jnp.concatenate([jnp.where(is_a, x, zero), jnp.where(is_a, zero, x)], axis=0)


def _unstack_pair(is_a, x2):
    return jnp.where(is_a, x2[0:CHUNK, :], x2[CHUNK:2 * CHUNK, :])


def _both_ways(x, dec):
    return (jnp.concatenate([x, x], axis=1) * dec).astype(BF16)


def _scan_states(n_chunks, st, up_rows, up_lam, down_rows, down_lam):
    zero = jnp.zeros((LANES, LANES), F32)

    def up(n, r):
        new = st[n, up_rows, :]
        st[n, up_rows, :] = r
        return r * up_lam + new

    def down(s, r):
        n = n_chunks - 1 - s
        new = st[n, down_rows, :]
        st[n, down_rows, :] = r
        return r * down_lam + new

    lax.fori_loop(0, n_chunks, up, zero)
    lax.fori_loop(0, n_chunks, down, zero)


FWD_ROWS, BWD_ROWS = pl.ds(0, LANES), pl.ds(LANES, LANES)


def _state_spec(n_chunks, pairs):
    return pl.BlockSpec((n_chunks, 2 * LANES, LANES), lambda b, h: (b * pairs + h, 0, 0))


ST_GAIN, ST_XF, ST_XB, ST_IFA, ST_IFB, ST_IBA, ST_IBB, ST_LF, ST_LB = 0, 1, 2, 3, 4, 5, 6, 8, 9
ST_ROWS = 16


GW = GROUP * HEAD_DIM
KEYS = 3 * BLOCK


def _attn_tables(g, bias_ref):
    r = lax.broadcasted_iota(jnp.int32, (GROUP * BLOCK, KEYS), 0)
    kj = lax.broadcasted_iota(jnp.int32, (GROUP * BLOCK, KEYS), 1)
    qi = r & (BLOCK - 1)
    hh = lax.shift_right_logical(r, 7)
    dist = jnp.abs(kj - BLOCK - qi)
    slope = jnp.exp2(-(GROUP * g + hh + 1).astype(F32) * (8.0 / ATTN_HEADS))
    inside = jnp.where(dist <= BLOCK, -slope * dist.astype(F32), NEG_INF)
    bias_ref[BIAS_INSIDE] = inside
    bias_ref[BIAS_FIRST] = jnp.where(kj >= BLOCK, inside, NEG_INF)
    bias_ref[BIAS_LAST] = jnp.where(kj < 2 * BLOCK, inside, NEG_INF)


BIAS_INSIDE, BIAS_FIRST, BIAS_LAST = 0, 1, 2


def _own_lanes(g):
    return lax.shift_right_logical(lax.broadcasted_iota(jnp.int32, (1, LANES), 1), 6) == g


def _mask_keys(x_ref, g, scale, pad_ref, s):
    pad_ref[0:BLOCK, :] = jnp.zeros((BLOCK, LANES), BF16)
    pad_ref[BLOCK + s:2 * BLOCK + s, :] = jnp.zeros((BLOCK, LANES), BF16)
    pad_ref[BLOCK:BLOCK + s, :] = jnp.where(_own_lanes(g), x_ref[...].astype(F32) * scale, 0.0).astype(BF16)


def _lane_block(x, j):
    return x[:, j * LANES:(j + 1) * LANES]


def _stack_heads(x, g):
    assert GROUP == 4 and GW == 2 * LANES
    x1 = pltpu.roll(x, HEAD_DIM, 1)
    keep = _own_lanes(g)
    zero = jnp.zeros((BLOCK, LANES), x.dtype)
    rows = []
    for h in range(GROUP):
        for_g0 = _lane_block(x, h // 2) if h % 2 == 0 else _lane_block(x1, ((h + 1) // 2) % 2)
        for_g1 = _lane_block(x, h // 2) if h % 2 == 1 else _lane_block(x1, h // 2)
        rows.append(jnp.where(keep, jnp.where(g == 0, for_g0, for_g1), zero))
    return jnp.concatenate(rows, axis=0)


def _unstack_heads(x4, g):
    p = [x4[h * BLOCK:(h + 1) * BLOCK, :] for h in range(GROUP)]
    cat = lambda a, b: jnp.concatenate([a, b], axis=1)
    in_place = jnp.where(g == 0, cat(p[0], p[2]), cat(p[1], p[3]))
    one_left = jnp.where(g == 0, cat(p[1], p[3]), cat(p[2], p[0]))
    return in_place + pltpu.roll(one_left, HEAD_DIM, 1)


def _sink_column(sink_ref, g):
    rh = lax.shift_right_logical(lax.broadcasted_iota(jnp.int32, (GROUP * BLOCK, 1), 0), 7)
    col = jnp.zeros((GROUP * BLOCK, 1), F32)
    for h in range(GROUP):
        col = jnp.where(rh == h, sink_ref[GROUP * g + h], col)
    return col


def _attn_probs(qm, k3, bias_ref, sink_col, n, s):
    which = jnp.where(n == 0, BIAS_FIRST, jnp.where(n == s // BLOCK - 1, BIAS_LAST, BIAS_INSIDE))
    logits = _dot_nt(qm, k3) + bias_ref[which]
    m = jnp.maximum(jnp.max(logits, axis=1, keepdims=True), sink_col)
    e = jnp.exp(logits - m)
    e_sink = jnp.exp(sink_col - m)
    inv = 1.0 / (jnp.sum(e, axis=1, keepdims=True) + e_sink)
    return e * inv, e_sink * inv


PAIRS_PER_KV = (RET_HEADS // 2) // KV_HEADS
FWD_ORDER = "rrarra"
BWD_ORDER = "rararr"


def _trip_order(order, chunks, blocks):
    if order.count("r") == chunks and order.count("a") == blocks:
        return order
    return "r" * chunks + "a" * blocks


def _mixers_fwd(u, lgf, lgb, gn_gain, sink, b_loc, after=()):
    t = u.shape[0]
    s = t // b_loc
    n_chunks = s // CHUNK
    pairs = RET_HEADS // 2
    trips = n_chunks // RET_UNROLL
    blocks_half = (s // BLOCK) // PAIRS_PER_KV
    per_trip = blocks_half // trips
    assert n_chunks % RET_UNROLL == 0 and blocks_half % trips == 0 and PAIRS_PER_KV == 2 and s >= 2 * BLOCK

    def body(lgf_ref, lgb_ref, sink_ref, q_ref, k_ref, v_ref, g_ref, gain_ref, aq_ref, ak_ref, av_ref,
             r_ref, xhat_ref, rstd_ref, a_ref, st, kpad, vpad, bias):
        pair = pl.program_id(1)
        g, half = lax.shift_right_logical(pair, 1), pair & 1
        tab = _stacked_tables(lgf_ref, lgb_ref, pair)
        is_a = tab["is_a"]

        @pl.when(half == 0)
        def _():
            _attn_tables(g, bias)
            _mask_keys(ak_ref, g, Q_SCALE, kpad, s)
            _mask_keys(av_ref, g, 1.0, vpad, s)

        def kv_body(n, _):
            k8 = _chunk(k_ref, n).astype(F32) * Q_SCALE
            st[n] = jnp.where(tab["diag2"], _dot_tn(_both_ways(k8, tab["kdec"]), _chunk(v_ref, n)), 0.0)
            return 0

        _chunk_loop(n_chunks, kv_body, 0, FWD_PREP_UNROLL)
        _scan_states(n_chunks, st, FWD_ROWS, tab["lam_f"], BWD_ROWS, tab["lam_b"])
        sink_col = _sink_column(sink_ref, g)

        def retention_chunk(n):
            q = _chunk(q_ref, n)
            k8 = (_chunk(k_ref, n).astype(F32) * Q_SCALE).astype(BF16)
            v = _chunk(v_ref, n)
            p2 = (_dot_nt(_stack_pair(is_a, q), k8) * tab["d2"]).astype(BF16)
            y = _unstack_pair(is_a, _dot_nn(p2, v))
            y = y + _dot_nn(_both_ways(q.astype(F32), tab["qdec"]), st[n].astype(BF16))
            rows = pl.ds(pl.multiple_of(n * CHUNK, CHUNK), CHUNK)
            mu = _group_sum(is_a, y) * (1.0 / HEAD_DIM)
            dlt = y - mu
            var = _group_sum(is_a, dlt * dlt) * (1.0 / HEAD_DIM)
            rstd = lax.rsqrt(var + GN_EPS)
            xhat = dlt * rstd
            xhat_ref[rows, :] = xhat
            rstd_ref[rows, :] = rstd
            gate = _chunk(g_ref, n).astype(F32)
            r_ref[rows, :] = (xhat * gain_ref[...] * gate * _sigmoid(gate)).astype(BF16)

        def attention_block(blk):
            n = half * blocks_half + blk
            rows = pl.ds(pl.multiple_of(blk * BLOCK, BLOCK), BLOCK)
            keys = pl.ds(pl.multiple_of(n * BLOCK, BLOCK), KEYS)
            p, _ = _attn_probs(_stack_heads(aq_ref[rows, :], g), kpad[keys, :], bias, sink_col, n, s)
            a_ref[rows, :] = _unstack_heads(_dot_nn(p.astype(BF16), vpad[keys, :]), g).astype(BF16)

        def trip(i, _):
            chunk, blk = 0, 0
            for kind in _trip_order(FWD_ORDER, RET_UNROLL, per_trip):
                if kind == "r":
                    retention_chunk(i * RET_UNROLL + chunk)
                    chunk += 1
                else:
                    attention_block(i * per_trip + blk)
                    blk += 1
            return 0

        lax.fori_loop(0, trips, trip, 0)

    lane_blk = lambda c0: _seq_spec(s, c0 // LANES)
    half_rows = blocks_half * BLOCK
    aq_spec = pl.BlockSpec((half_rows, GW), lambda b, h: (b * PAIRS_PER_KV + (h & 1), C_AQ // GW + h // 2))
    a_spec = pl.BlockSpec((half_rows, GW), lambda b, h: (b * PAIRS_PER_KV + (h & 1), h // 2))
    kv_spec = lambda c0: pl.BlockSpec((s, LANES), lambda b, h: (b, c0 // LANES))
    pad = pltpu.VMEM((s + 2 * BLOCK, LANES), BF16)
    body, lead = _after(after, body)
    return pl.pallas_call(
        body, name="mixers_fwd", grid=(b_loc, pairs),
        in_specs=lead + [_smem_spec(), _smem_spec(), _smem_spec(), lane_blk(C_RQ), lane_blk(C_RK), lane_blk(C_RV),
                         lane_blk(C_RG), pl.BlockSpec((1, LANES), lambda b, h: (0, h)), aq_spec, kv_spec(C_AK),
                         kv_spec(C_AV)],
        out_specs=[_seq_spec(s, 0), _seq_spec(s, 0), _seq_spec(s, 0), a_spec, _state_spec(n_chunks, pairs)],
        out_shape=[SDS((t, RET_W), BF16), SDS((t, RET_W), F32), SDS((t, RET_W), F32), SDS((t, ATT_W), BF16),
                   SDS((b_loc * pairs * n_chunks, 2 * LANES, LANES), F32)],
        scratch_shapes=[pad, pad, pltpu.VMEM((3, GROUP * BLOCK, KEYS), F32)],
        compiler_params=_params(("arbitrary", "arbitrary")),
    )(*after, lgf, lgb, sink, u, u, u, u, gn_gain, u, u, u)


def _mixers_bwd(u, xhat, rstd, states, dr, da, lgf, lgb, gn_gain, sink, b_loc, after=()):
    t = u.shape[0]
    s = t // b_loc
    n_chunks = s // CHUNK
    pairs = RET_HEADS // 2
    trips = n_chunks // RET_UNROLL
    blocks_half = (s // BLOCK) // PAIRS_PER_KV
    per_trip = blocks_half // trips
    assert n_chunks % RET_UNROLL == 0 and blocks_half % trips == 0 and PAIRS_PER_KV == 2 and s >= 2 * BLOCK

    def body(lgf_ref, lgb_ref, sink_ref, q_ref, k_ref, v_ref, g_ref, xhat_ref, rstd_ref, dr_ref, gain_ref,
             aq_ref, ak_ref, av_ref, do_ref, st,
             dq_ref, dk_ref, dv_ref, dg_ref, st_ref, daq_ref, dak_ref, dav_ref, dsink_ref,
             gr, dy_s, kpad, vpad, bias, dk_acc, dv_acc):
        pair = pl.program_id(1)
        g, half = lax.shift_right_logical(pair, 1), pair & 1
        tab = _stacked_tables(lgf_ref, lgb_ref, pair)
        is_a = tab["is_a"]
        gain = gain_ref[...]

        @pl.when(half == 0)
        def _():
            _attn_tables(g, bias)
            _mask_keys(ak_ref, g, Q_SCALE, kpad, s)
            _mask_keys(av_ref, g, 1.0, vpad, s)
            dsink_ref[...] = jnp.zeros_like(dsink_ref)

        @pl.when(pair == 0)
        def _():
            dk_acc[...] = jnp.zeros_like(dk_acc)
            dv_acc[...] = jnp.zeros_like(dv_acc)

        def norm_body(n, dgain):
            rows = pl.ds(pl.multiple_of(n * CHUNK, CHUNK), CHUNK)
            xhat, rstd = xhat_ref[rows, :], rstd_ref[rows, :]
            gate = g_ref[rows, :].astype(F32)
            sg = _sigmoid(gate)
            silu = gate * sg
            d_out = dr_ref[rows, :].astype(F32)
            dg_ref[rows, :] = (d_out * xhat * gain * (sg * (1.0 + gate * (1.0 - sg)))).astype(BF16)
            dxh = d_out * gain * silu
            m1 = _group_sum(is_a, dxh) * (1.0 / HEAD_DIM)
            m2 = _group_sum(is_a, dxh * xhat) * (1.0 / HEAD_DIM)
            dy = (rstd * (dxh - m1 - xhat * m2)).astype(BF16)
            dy_s[rows, :] = dy
            qf = q_ref[rows, :].astype(F32)
            gr[n] = jnp.where(tab["diag2"], _dot_tn(_both_ways(qf, tab["qdec"]), dy), 0.0)
            return dgain + jnp.sum(d_out * xhat * silu, axis=0, keepdims=True)

        colsum = lambda x: jnp.sum(x, axis=0, keepdims=True)

        def grad_body(n, carry):
            xfb, ifa, ifb, iba, ibb, lf, lb = carry
            rows = pl.ds(pl.multiple_of(n * CHUNK, CHUNK), CHUNK)
            q = q_ref[rows, :]
            qf = q.astype(F32)
            k8f = k_ref[rows, :].astype(F32) * Q_SCALE
            k8 = k8f.astype(BF16)
            v = v_ref[rows, :]
            dy = dy_s[rows, :]
            q2, dy2 = _stack_pair(is_a, q), _stack_pair(is_a, dy)
            sc = _dot_nt(q2, k8)
            dp = _dot_nt(dy2, v)
            a2 = (sc * tab["d2"]).astype(BF16)
            ds2 = (dp * tab["d2"]).astype(BF16)
            dq = _unstack_pair(is_a, _dot_nn(ds2, k8))
            dk = _dot_tn(ds2, q2)
            dv = _dot_tn(a2, dy2)
            prod = sc * dp
            pf, pb = prod * tab["df2"], prod * tab["db2"]
            ifa, ifb = ifa + colsum(pf[0:CHUNK, :]), ifb + colsum(pf[CHUNK:2 * CHUNK, :])
            iba, ibb = iba + colsum(pb[0:CHUNK, :]), ibb + colsum(pb[CHUNK:2 * CHUNK, :])
            states, sgrads = st[n], gr[n]
            sb, gb = states.astype(BF16), sgrads.astype(BF16)
            dqc = _dot_nt(dy, sb) * tab["qdec"]
            dkc = _dot_nt(v, gb) * tab["kdec"]
            dv = dv + _dot_nn(_both_ways(k8f, tab["kdec"]), gb)
            dq_ref[rows, :] = (dq + dqc[:, 0:LANES] + dqc[:, LANES:2 * LANES]).astype(BF16)
            dk_ref[rows, :] = ((dk + dkc[:, 0:LANES] + dkc[:, LANES:2 * LANES]) * Q_SCALE).astype(BF16)
            dv_ref[rows, :] = dv.astype(BF16)
            q2w, k2w = jnp.concatenate([qf, qf], axis=1), jnp.concatenate([k8f, k8f], axis=1)
            xfb = xfb + colsum(tab["qexp"] * q2w * dqc + tab["kexp"] * k2w * dkc)
            prod_s = sgrads * states
            lf, lb = lf + colsum(prod_s[0:LANES, :]), lb + colsum(prod_s[LANES:2 * LANES, :])
            return xfb, ifa, ifb, iba, ibb, lf, lb

        sink_col = _sink_column(sink_ref, g)
        head_row = lax.broadcasted_iota(jnp.int32, dsink_ref.shape, 0)

        def attention_block(blk):
            n = half * blocks_half + blk
            rows = pl.ds(pl.multiple_of(blk * BLOCK, BLOCK), BLOCK)
            keys = pl.ds(pl.multiple_of(n * BLOCK, BLOCK), KEYS)
            qm = _stack_heads(aq_ref[rows, :], g)
            k3, v3 = kpad[keys, :], vpad[keys, :]
            p, p_sink = _attn_probs(qm, k3, bias, sink_col, n, s)
            dom = _stack_heads(do_ref[rows, :], g)
            dp = _dot_nt(dom, v3)
            delta = jnp.sum(p * dp, axis=1, keepdims=True)
            ds_mat = (p * (dp - delta)).astype(BF16)
            daq_ref[rows, :] = _unstack_heads(_dot_nn(ds_mat, k3), g).astype(BF16)
            dk_acc[keys, :] += _dot_tn(ds_mat, qm) * Q_SCALE
            dv_acc[keys, :] += _dot_tn(p.astype(BF16), dom)
            w = p_sink * delta
            upd = jnp.zeros(dsink_ref.shape, F32)
            for h in range(GROUP):
                upd = upd + jnp.where(head_row == h, -jnp.sum(w[h * BLOCK:(h + 1) * BLOCK, :]), 0.0)
            dsink_ref[...] += upd

        dgain = _chunk_loop(n_chunks, norm_body, jnp.zeros((1, LANES), F32), BWD_PREP_UNROLL)
        _scan_states(n_chunks, gr, BWD_ROWS, tab["lam_b"], FWD_ROWS, tab["lam_f"])

        def trip(i, carry):
            chunk, blk = 0, 0
            for kind in _trip_order(BWD_ORDER, RET_UNROLL, per_trip):
                if kind == "r":
                    carry = grad_body(i * RET_UNROLL + chunk, carry)
                    chunk += 1
                else:
                    attention_block(i * per_trip + blk)
                    blk += 1
            return carry

        z = jnp.zeros((1, LANES), F32)
        init = (jnp.zeros((1, 2 * LANES), F32), z, z, z, z, z, z)
        xfb, ifa, ifb, iba, ibb, lf, lb = lax.fori_loop(0, trips, trip, init)
        st_ref[...] = jnp.zeros_like(st_ref)
        st_ref[ST_GAIN:ST_GAIN + 1, :] = dgain
        st_ref[ST_XF:ST_XF + 1, :] = xfb[:, 0:LANES]
        st_ref[ST_XB:ST_XB + 1, :] = xfb[:, LANES:2 * LANES]
        st_ref[ST_IFA:ST_IFA + 1, :] = ifa
        st_ref[ST_IFB:ST_IFB + 1, :] = ifb
        st_ref[ST_IBA:ST_IBA + 1, :] = iba
        st_ref[ST_IBB:ST_IBB + 1, :] = ibb
        st_ref[ST_LF:ST_LF + 1, :] = lf * (CHUNK * tab["lam_f"])
        st_ref[ST_LB:ST_LB + 1, :] = lb * (CHUNK * tab["lam_b"])

        @pl.when(pair == pairs - 1)
        def _():
            dak_ref[...] = dk_acc[BLOCK:BLOCK + s, :].astype(BF16)
            dav_ref[...] = dv_acc[BLOCK:BLOCK + s, :].astype(BF16)

    lane_blk = lambda c0: _seq_spec(s, c0 // LANES)
    seq0 = _seq_spec(s, 0)
    half_rows = blocks_half * BLOCK
    aq_spec = pl.BlockSpec((half_rows, GW), lambda b, h: (b * PAIRS_PER_KV + (h & 1), C_AQ // GW + h // 2))
    a_spec = pl.BlockSpec((half_rows, GW), lambda b, h: (b * PAIRS_PER_KV + (h & 1), h // 2))
    kv_spec = lambda c0: pl.BlockSpec((s, LANES), lambda b, h: (b, c0 // LANES))
    kv_out = pl.BlockSpec((s, LANES), lambda b, h: (b, 0))
    state = pltpu.VMEM((n_chunks, 2 * LANES, LANES), F32)
    pad = pltpu.VMEM((s + 2 * BLOCK, LANES), BF16)
    acc = pltpu.VMEM((s + 2 * BLOCK, LANES), F32)
    body, lead = _after(after, body)
    return pl.pallas_call(
        body, name="mixers_bwd", grid=(b_loc, pairs),
        in_specs=lead + [_smem_spec(), _smem_spec(), _smem_spec(), lane_blk(C_RQ), lane_blk(C_RK), lane_blk(C_RV),
                         lane_blk(C_RG), seq0, seq0, seq0, pl.BlockSpec((1, LANES), lambda b, h: (0, h)),
                         aq_spec, kv_spec(C_AK), kv_spec(C_AV), a_spec, _state_spec(n_chunks, pairs)],
        out_specs=[seq0] * 4 + [pl.BlockSpec((ST_ROWS, LANES), lambda b, h: (b, h)), a_spec, kv_out, kv_out,
                                pl.BlockSpec((8, LANES), lambda b, h: (b * KV_HEADS + h // 2, 0))],
        out_shape=[SDS((t, RET_W), BF16)] * 4 + [SDS((b_loc * ST_ROWS, RET_W), F32), SDS((t, ATT_W), BF16),
                                                   SDS((t, KV_W), BF16), SDS((t, KV_W), BF16),
                                                   SDS((b_loc * KV_HEADS * 8, LANES), F32)],
        scratch_shapes=[state, pltpu.VMEM((s, LANES), BF16), pad, pad,
                        pltpu.VMEM((3, GROUP * BLOCK, KEYS), F32), acc, acc],
        compiler_params=_params(("arbitrary", "arbitrary")),
    )(*after, lgf, lgb, sink, u, u, u, u, xhat, rstd, dr, gn_gain, u, u, u, da, states)


def _pack_small(acc2, acc1, ret_stats, dsink, b_loc, d):
    pairs = RET_HEADS // 2

    def body(acc2_ref, acc1_ref, st_ref, dsink_ref, out_ref):
        out_ref[...] = jnp.zeros_like(out_ref)
        out_ref[ROW_LN1G:ROW_LN1G + 1, :] = acc1_ref[0:1, :]
        out_ref[ROW_LN1B:ROW_LN1B + 1, :] = acc1_ref[1:2, :]
        out_ref[ROW_LN2G:ROW_LN2G + 1, :] = acc2_ref[1:2, :]
        out_ref[ROW_LN2B:ROW_LN2B + 1, :] = acc2_ref[2:3, :]
        out_ref[ROW_LOSS:ROW_LOSS + 1, :] = acc2_ref[0:1, :]
        st = st_ref[0:ST_ROWS, :]
        for b in range(1, b_loc):
            st = st + st_ref[b * ST_ROWS:(b + 1) * ST_ROWS, :]
        out_ref[ROW_GN:ROW_GN + 1, 0:RET_W] = st[ST_GAIN:ST_GAIN + 1, :]
        lane = lax.broadcasted_iota(jnp.int32, (1, d), 1)
        misc = jnp.zeros((1, d), F32)
        for pr in range(pairs):
            blk = st[:, pr * LANES:(pr + 1) * LANES]
            half = lax.broadcasted_iota(jnp.int32, (1, LANES), 1) < HEAD_DIM
            for h in range(2):
                sel = half if h == 0 else jnp.logical_not(half)
                cross_f = jnp.sum(jnp.where(sel, blk[ST_XF:ST_XF + 1, :] + blk[ST_LF:ST_LF + 1, :], 0.0))
                cross_b = jnp.sum(jnp.where(sel, blk[ST_XB:ST_XB + 1, :] + blk[ST_LB:ST_LB + 1, :], 0.0))
                intra_f = jnp.sum(blk[ST_IFA + h:ST_IFA + h + 1, :])
                intra_b = jnp.sum(blk[ST_IBA + h:ST_IBA + h + 1, :])
                head = 2 * pr + h
                misc = jnp.where(lane == MISC_DF + head, cross_f + intra_f, misc)
                misc = jnp.where(lane == MISC_DB + head, cross_b + intra_b, misc)
        for g in range(KV_HEADS):
            tot = dsink_ref[g * 8:(g + 1) * 8, :]
            for b in range(1, b_loc):
                tot = tot + dsink_ref[(b * KV_HEADS + g) * 8:(b * KV_HEADS + g + 1) * 8, :]
            for h in range(GROUP):
                misc = jnp.where(lane == MISC_SINK + GROUP * g + h, jnp.sum(tot[h:h + 1, 0:1]), misc)
        out_ref[ROW_MISC:ROW_MISC + 1, :] = misc

    return pl.pallas_call(body, name="pack_small", out_shape=SDS((SMALL_ROWS, d), F32))(acc2, acc1, ret_stats, dsink)


BIG = ("w_in", "w_out", "w_ffn_gate", "w_ffn_up", "w_ffn_down", "w_ple_proj", "w_ple_gate")
TRANSPOSED_OUTSIDE = ("w_in", "w_ffn_gate", "w_ffn_up")
TRANSPOSED_HERE = ("w_ple_proj",)
SMALL = ("ret_decay_fwd", "ret_decay_bwd", "ret_gn_gain", "attn_sink", "ln1_gain", "ln1_bias", "ln2_gain", "ln2_bias")
ORDER = ("w_in", "ret_decay_fwd", "ret_decay_bwd", "ret_gn_gain", "attn_sink", "w_out", "ln1_gain", "ln1_bias",
         "w_ffn_gate", "w_ffn_up", "w_ffn_down", "w_ple_proj", "w_ple_gate", "ln2_gain", "ln2_bias")


GATHER_ORDER = ("w_in", "w_ffn_up", "w_out", "w_ffn_gate", "w_ple_gate", "w_ple_proj", "w_ffn_down")
GATHER_TWO_LEVEL = ("w_in", "w_ffn_up", "w_out")

def _local_step(x2, p2, target2, fetch, publish, small, b_loc, me):
    d = x2.shape[1]
    lgf, lgb = _log_decay(small["ret_decay_fwd"], small["ret_decay_bwd"])
    lgf1, lgb1, sink1 = lgf.reshape(-1), lgb.reshape(-1), small["attn_sink"].reshape(-1)
    (w_in,) = fetch(("w_in",), ())
    u, xb = _in_proj(x2, w_in)
    passed = fetch.pass_on(("w_ffn_up", "w_out"), (xb,))
    r, ret_xhat, ret_rstd, a, ret_states = _mixers_fwd(u, lgf1, lgb1, small["ret_gn_gain"], sink1, b_loc, passed)
    w_out, w_gate, w_up = fetch(("w_out", "w_ffn_gate", "w_ffn_up"), (r, a))
    xhat1, rstd1, h1b, dact_dg, dact_du, act = _mix_ln1_ffn_up(
        r, a, x2, w_out, w_gate, w_up, small["ln1_gain"], small["ln1_bias"])
    w_pg, w_pe, w_down = fetch(("w_ple_gate", "w_ple_proj", "w_ffn_down"), (act,))
    dz2, dz2b, dsb, dpleb, dg, dup, acc2 = _ffn_down_ln2_loss(
        act, dact_dg, dact_du, h1b, p2, xhat1, target2, w_down, w_pg, w_pe,
        small["ln1_gain"], small["ln1_bias"], small["ln2_gain"], small["ln2_bias"])
    own = {}

    def grad(name, parts, rhs, after=()):
        whole, own[name] = _weight_grad("grad_" + name, me, parts, rhs, after)
        return whole

    ffn_jobs = dict(w_ffn_down=(act, dz2b), w_ple_proj=(dpleb, p2), w_ple_gate=(h1b, dsb),
                    w_ffn_gate=(dg, h1b), w_ffn_up=(dup, h1b))
    wholes, owns = _weight_grad_jobs("grad_w_ffn", me, list(ffn_jobs.values()))
    own.update(zip(ffn_jobs, owns))
    t2 = publish("ffn", dict(zip(ffn_jobs, wholes)))
    dz1, dz1b, dr, da, acc1 = _dh1_ln1_bwd(
        dz2, dg, dup, dsb, xhat1, rstd1, w_gate, w_up, w_pg, w_out, small["ln1_gain"], t2)
    t3 = publish("out", dict(w_out=grad("w_out", [r, a], dz1b)))
    dq, dk, dv, dgate, ret_stats, daq, dak, dav, dsink = _mixers_bwd(
        u, ret_xhat, ret_rstd, ret_states, dr, da, lgf1, lgb1, small["ret_gn_gain"], sink1, b_loc, t3)
    parts = [dq, dk, dv, dgate, daq, dak, dav]
    small_part = _pack_small(acc2, acc1, ret_stats, dsink, b_loc, d)
    t4 = publish("in", dict(w_in=grad("w_in", parts, xb)), small_part)
    grad_x = _in_proj_bwd(dz1, parts, w_in, t4)
    return grad_x, own, small_part


def kernel(x, p, w_in, ret_decay_fwd, ret_decay_bwd, ret_gn_gain, attn_sink, w_out, ln1_gain, ln1_bias, w_ffn_gate, w_ffn_up, w_ffn_down, w_ple_proj, w_ple_gate, ln2_gain, ln2_bias, loss_target, m_w_in, m_ret_decay_fwd, m_ret_decay_bwd, m_ret_gn_gain, m_attn_sink, m_w_out, m_ln1_gain, m_ln1_bias, m_w_ffn_gate, m_w_ffn_up, m_w_ffn_down, m_w_ple_proj, m_w_ple_gate, m_ln2_gain, m_ln2_bias, v_w_in, v_ret_decay_fwd, v_ret_decay_bwd, v_ret_gn_gain, v_attn_sink, v_w_out, v_ln1_gain, v_ln1_bias, v_w_ffn_gate, v_w_ffn_up, v_w_ffn_down, v_w_ple_proj, v_w_ple_gate, v_ln2_gain, v_ln2_bias):
    given = dict(locals())

    def strip(n, a):
        if n not in BIG:
            return a
        return a[0].T if n in TRANSPOSED_OUTSIDE else a[0]

    def restore(n, a):
        if n not in BIG:
            return a
        return (a.T if n in TRANSPOSED_OUTSIDE else a)[None]

    w = {n: strip(n, given[n]) for n in ORDER}
    m = {n: strip(n, given["m_" + n]) for n in ORDER}
    v = {n: strip(n, given["v_" + n]) for n in ORDER}
    b_loc, s, d = x.shape
    x2 = x.reshape(b_loc * s, d)
    p2 = p[0].reshape(b_loc * s, p.shape[-1])
    target2 = loss_target.reshape(b_loc * s, d)

    small = {n: w[n] for n in SMALL}
    me = (4 * lax.axis_index("x") + 2 * lax.axis_index("y") + lax.axis_index("c")).astype(jnp.int32).reshape(1)

    gather = _gather_start(
        {n: w[n] for n in GATHER_ORDER},
        [_gather_copy_near if n in GATHER_TWO_LEVEL else _gather_copy for n in GATHER_ORDER])

    passing = {}

    def pass_on(names, after):
        relayed = _gather_relay("gather_relay_" + names[0], gather, [GATHER_ORDER.index(n) for n in names], list(after))
        passing.update({n: (relayed, j) for j, n in enumerate(names)})
        return (relayed["token"],)

    def fetch(names, after):
        out = {}
        for n in [n for n in names if n in GATHER_TWO_LEVEL]:
            if n not in passing:
                pass_on((n,), after)
            relayed, j = passing[n]
            out[n] = _split_copy_wait("gather_wait_" + n, relayed, [j], list(after))[0][0]
        direct = [n for n in names if n not in GATHER_TWO_LEVEL]
        if direct:
            got = _split_copy_wait("gather_wait_" + direct[0], gather, [GATHER_ORDER.index(n) for n in direct],
                                   list(after))
            out.update({n: item[0] for n, item in zip(direct, got)})
        return [out[n] for n in names]

    scatters = []

    def publish(tag, products, small_sums=None):
        items = [(products[n], lax.empty((N_DEV - 1, products[n].shape[0] // N_DEV, products[n].shape[1]), BF16))
                 for n in products]
        copies = [_scatter_copy] * len(items)
        if small_sums is not None:
            items.append((small_sums, lax.empty((N_DEV - 1,) + small_sums.shape, F32)))
            copies.append(_small_copy)
        started = _split_copy_start("scatter_start_" + tag, items, copies)
        scatters.append((list(products), small_sums is not None, started))
        return (started["token"],)

    fetch.pass_on = pass_on
    grad_x, own, small_part = _local_step(x2, p2, target2, fetch, publish, small, b_loc, me)

    out_g, out_d, out_m, out_v = {}, {}, {}, {}
    after = [grad_x]
    for names, with_small, started in scatters:
        landed = _split_copy_wait("scatter_wait_" + names[0], started, list(range(len(started["items"]))), after)
        if with_small:
            mine, from_peers = landed[-1]
            loss, sg, sd, sm, sv = _small_adamw(
                me, mine, from_peers, small, {n: m[n] for n in SMALL}, {n: v[n] for n in SMALL})
            for dst, src in ((out_g, sg), (out_d, sd), (out_m, sm), (out_v, sv)):
                dst.update(src)
        recv = {n: item[1] for n, item in zip(names, landed)}
        alike = {}
        for n in names:
            alike.setdefault((own[n].shape, n in TRANSPOSED_HERE), []).append(n)
        for (_, transposed), ns in alike.items():
            res = _reduce_adamw(ns[0], [own[n] for n in ns], [recv[n] for n in ns], [w[n] for n in ns],
                                [m[n] for n in ns], [v[n] for n in ns], transposed)
            for dst, vals in zip((out_g, out_d, out_m, out_v), res):
                dst.update(zip(ns, vals))
        after = [out_v[names[-1]]]

    outs = [loss[0, 0], grad_x.reshape(x.shape)]
    for group in (out_g, out_d, out_m, out_v):
        outs += [restore(n, group[n]) for n in ORDER]
    return tuple(outs)
```

```python
import functools

import jax
import jax.numpy as jnp
from jax import lax
from jax.experimental import pallas as pl
from jax.experimental.pallas import tpu as pltpu

F32, BF16 = jnp.float32, jnp.bfloat16
SDS = jax.ShapeDtypeStruct
MESH = pl.DeviceIdType.MESH

N_DEV = 8
HEAD_DIM = 64
RET_HEADS = 8
ATTN_HEADS = 8
KV_HEADS = 2
GROUP = ATTN_HEADS // KV_HEADS
RET_W = RET_HEADS * HEAD_DIM
ATT_W = ATTN_HEADS * HEAD_DIM
KV_W = KV_HEADS * HEAD_DIM
LANES = 128
CHUNK = 128
BLOCK = 128
Q_SCALE = HEAD_DIM ** -0.5
ALPHA = 2.0 ** 0.25
LN_EPS = 1e-5
GN_EPS = 1e-5
NEG_INF = -1e30
C_RQ, C_RK, C_RV, C_RG = 0, RET_W, 2 * RET_W, 3 * RET_W
C_AQ = 4 * RET_W
C_AK = C_AQ + ATT_W
C_AV = C_AK + KV_W
IN_W = C_AV + KV_W

ADAM_LR = 0.001
ADAM_B1 = 0.9
ADAM_B2 = 0.999
ADAM_EPS = 1e-08
ADAM_WD = 0.01
ADAM_STEP = 10

VMEM_LIMIT = 56 * 1024 * 1024
MATMUL_ROWS = 512
EPILOGUE_ROWS = 256
SUB_ROWS = 512
SMALL_ROWS = 16
ROW_LN1G, ROW_LN1B, ROW_LN2G, ROW_LN2B, ROW_LOSS, ROW_GN, ROW_MISC = 0, 1, 2, 3, 4, 5, 6
MISC_DF, MISC_DB, MISC_SINK = 0, 8, 16


def _dot_nn(a, b):
    return lax.dot_general(a, b, (((1,), (0,)), ((), ())), preferred_element_type=F32)


def _dot_nt(a, b):
    return lax.dot_general(a, b, (((1,), (1,)), ((), ())), preferred_element_type=F32)


def _dot_tn(a, b):
    return lax.dot_general(a, b, (((0,), (0,)), ((), ())), preferred_element_type=F32)


def _params(sem=None, vmem=VMEM_LIMIT):
    kw = {"vmem_limit_bytes": vmem}
    if sem is not None:
        kw["dimension_semantics"] = sem
    return pltpu.CompilerParams(**kw)


def _row_tile(t, want=512):
    tm = want
    while t % tm:
        tm //= 2
    return tm


def _sigmoid(x):
    return jax.nn.sigmoid(x)


def _layer_norm_stats(z):
    mu = jnp.mean(z, axis=1, keepdims=True)
    d = z - mu
    var = jnp.mean(d * d, axis=1, keepdims=True)
    rstd = lax.rsqrt(var + LN_EPS)
    return d * rstd, rstd


def _layer_norm_bwd(dxh, xhat, rstd):
    m1 = jnp.mean(dxh, axis=1, keepdims=True)
    m2 = jnp.mean(dxh * xhat, axis=1, keepdims=True)
    return rstd * (dxh - m1 - xhat * m2)


def _mesh_pos():
    return lax.axis_index("x"), lax.axis_index("y"), lax.axis_index("c")


HBM_SPEC = pl.BlockSpec(memory_space=pltpu.HBM)
SEM_SPEC = pl.BlockSpec(memory_space=pltpu.SEMAPHORE)
ANY_SPEC = pl.BlockSpec(memory_space=pl.ANY)
SIDE_EFFECT = pltpu.SideEffectType.DATAFLOW_SIDE_EFFECTING
PEER_SEMS = pltpu.SemaphoreType.DMA((N_DEV - 1,))


def _in_hbm(a):
    return pltpu.with_memory_space_constraint(a, pltpu.HBM)


def _split_copy_start(name, items, copies):
    n = len(items)
    flat = [a for it in items for a in it]
    k = len(flat)

    def body(*refs):
        arr, sems = list(refs[:k]), refs[k:k + 2 * n]
        for i, it in enumerate(items):
            mine = [arr.pop(0) for _ in it]
            for m in range(1, N_DEV):
                cp = copies[i](m, mine, sems[i].at[m - 1], sems[n + i].at[m - 1])
                if cp is not None:
                    cp.start()
        token = refs[-1]
        token[...] = jnp.zeros_like(token)

    res = pl.pallas_call(
        body, name=name,
        out_shape=[PEER_SEMS] * (2 * n) + [pltpu.HBM(a.shape, a.dtype) for a in flat] + [SDS((8, LANES), F32)],
        in_specs=[HBM_SPEC] * k,
        out_specs=[SEM_SPEC] * (2 * n) + [HBM_SPEC] * k + [pl.BlockSpec(memory_space=pltpu.VMEM)],
        input_output_aliases={j: 2 * n + j for j in range(k)},
        compiler_params=pltpu.CompilerParams(has_side_effects=SIDE_EFFECT),
    )(*[_in_hbm(a) for a in flat])
    thru, out_items = list(res[2 * n:2 * n + k]), []
    for it in items:
        out_items.append(tuple(thru.pop(0) for _ in it))
    return dict(send=res[:n], recv=res[n:2 * n], items=out_items, token=res[-1], copies=copies)


def _gather_start(shards, copies):
    names = list(shards)
    n = len(names)
    flip = [name in TRANSPOSED_HERE for name in names]
    shapes = [shards[name].shape[::-1] if f else shards[name].shape for name, f in zip(names, flip)]
    most = (max(s[0] for s in shapes), max(s[1] for s in shapes))

    def body(*refs):
        src, sems, land, token = refs[:n], refs[n:3 * n], refs[3 * n:4 * n], refs[4 * n]
        wide, narrow, sem = refs[4 * n + 1:]
        for i, (rows, cols) in enumerate(shapes):
            raw = wide.at[0:cols, 0:rows] if flip[i] else wide.at[0:rows, 0:cols]
            bring = pltpu.make_async_copy(src[i], raw, sem.at[0])
            bring.start()
            bring.wait()
            narrow[0:rows, 0:cols] = (raw[...].T if flip[i] else raw[...]).astype(BF16)
            mine = land[i].at[pl.ds(pl.multiple_of(_peer_index(0) * rows, 8), rows), :]
            place = pltpu.make_async_copy(narrow.at[0:rows, 0:cols], mine, sem.at[0])
            place.start()
            place.wait()
            for m in range(1, N_DEV):
                cp = copies[i](m, [land[i]], sems[i].at[m - 1], sems[n + i].at[m - 1])
                if cp is not None:
                    cp.start()
        token[...] = jnp.zeros_like(token)

    side = max(most)
    res = pl.pallas_call(
        body, name="gather_start",
        out_shape=[PEER_SEMS] * (2 * n) + [pltpu.HBM((N_DEV * r, c), BF16) for r, c in shapes] + [SDS((8, LANES), F32)],
        in_specs=[HBM_SPEC] * n,
        out_specs=[SEM_SPEC] * (2 * n) + [HBM_SPEC] * n + [pl.BlockSpec(memory_space=pltpu.VMEM)],
        scratch_shapes=[pltpu.VMEM((side, side), F32), pltpu.VMEM(most, BF16), pltpu.SemaphoreType.DMA((1,))],
        compiler_params=pltpu.CompilerParams(has_side_effects=SIDE_EFFECT),
    )(*[_in_hbm(shards[name]) for name in names])
    return dict(send=res[:n], recv=res[n:2 * n], items=[(a,) for a in res[2 * n:3 * n]], token=res[-1], copies=copies)


def _gather_relay(name, started, which, after):
    lands = [started["items"][w][0] for w in which]
    k = len(lands)

    def body(*refs):
        land_refs, old_send, old_recv = refs[:k], refs[k:2 * k], refs[2 * k:3 * k]
        outs = refs[3 * k + len(after):]
        send, recv, token = outs[:k], outs[k:2 * k], outs[-1]
        for j in range(k):
            for m in range(1, N_DEV):
                cp = _gather_copy_near(m, [land_refs[j]], old_send[j].at[m - 1], old_recv[j].at[m - 1])
                if cp is None:
                    continue
                cp.wait_send()
                cp.wait_recv()
                if m > 1:
                    _gather_copy_pass(m + 1, [land_refs[j]], send[j].at[m], recv[j].at[m]).start()
        token[...] = jnp.zeros_like(token)

    res = pl.pallas_call(
        body, name=name,
        out_shape=[PEER_SEMS] * (2 * k) + [pltpu.HBM(a.shape, a.dtype) for a in lands] + [SDS((8, LANES), F32)],
        in_specs=[HBM_SPEC] * k + [SEM_SPEC] * (2 * k) + [ANY_SPEC] * len(after),
        out_specs=[SEM_SPEC] * (2 * k) + [HBM_SPEC] * k + [pl.BlockSpec(memory_space=pltpu.VMEM)],
        input_output_aliases={j: 2 * k + j for j in range(k)},
        compiler_params=pltpu.CompilerParams(has_side_effects=SIDE_EFFECT),
    )(*lands, *[started["send"][w] for w in which], *[started["recv"][w] for w in which],
      *[_in_hbm(a) for a in after])
    return dict(send=res[:k], recv=res[k:2 * k], items=[(a,) for a in res[2 * k:3 * k]], token=res[-1],
                copies=[_gather_copy_pass] * k)


def _split_copy_wait(name, started, which, after):
    items = [started["items"][i] for i in which]
    copies = [started["copies"][i] for i in which]
    n = len(items)
    flat = [a for it in items for a in it]
    k = len(flat)

    def body(*refs):
        arr, sems = list(refs[:k]), refs[k:k + 2 * n]
        for i, it in enumerate(items):
            mine = [arr.pop(0) for _ in it]
            for m in range(1, N_DEV):
                cp = copies[i](m, mine, sems[i].at[m - 1], sems[n + i].at[m - 1])
                if cp is not None:
                    cp.wait_send()
                    cp.wait_recv()

    res = pl.pallas_call(
        body, name=name,
        out_shape=[pltpu.HBM(a.shape, a.dtype) for a in flat],
        in_specs=[HBM_SPEC] * k + [SEM_SPEC] * (2 * n) + [ANY_SPEC] * len(after),
        out_specs=[HBM_SPEC] * k,
        input_output_aliases={j: j for j in range(k)},
        compiler_params=pltpu.CompilerParams(has_side_effects=SIDE_EFFECT),
    )(*flat, *[started["send"][i] for i in which], *[started["recv"][i] for i in which], *[_in_hbm(a) for a in after])
    thru, out_items = list(res), []
    for it in items:
        out_items.append(tuple(thru.pop(0) for _ in it))
    return out_items


def _gather_copy(m, refs, send_sem, recv_sem):
    (land_ref,) = refs
    r = land_ref.shape[0] // N_DEV
    mine = land_ref.at[pl.ds(pl.multiple_of(_peer_index(0) * r, 8), r), :]
    return pltpu.make_async_remote_copy(src_ref=mine, dst_ref=mine, send_sem=send_sem, recv_sem=recv_sem,
                                        device_id=_peer(m), device_id_type=MESH)


def _gather_copy_near(m, refs, send_sem, recv_sem):
    return _gather_copy(m, refs, send_sem, recv_sem) if m == 1 or m % 2 == 0 else None


def _gather_copy_pass(m, refs, send_sem, recv_sem):
    if m == 1 or m % 2 == 0:
        return None
    (land_ref,) = refs
    r = land_ref.shape[0] // N_DEV
    block = land_ref.at[pl.ds(pl.multiple_of(_peer_index(m ^ 1) * r, 8), r), :]
    return pltpu.make_async_remote_copy(src_ref=block, dst_ref=block, send_sem=send_sem, recv_sem=recv_sem,
                                        device_id=_peer(1), device_id_type=MESH)


def _small_copy(m, refs, send_sem, recv_sem):
    part_ref, land_ref = refs
    return pltpu.make_async_remote_copy(src_ref=part_ref, dst_ref=land_ref.at[m - 1], send_sem=send_sem,
                                        recv_sem=recv_sem, device_id=_peer(m), device_id_type=MESH)


def _scatter_copy(m, refs, send_sem, recv_sem):
    buf_ref, land_ref = refs
    r = buf_ref.shape[0] // N_DEV
    src = buf_ref.at[pl.ds(pl.multiple_of(_peer_index(m) * r, 8), r), :]
    return pltpu.make_async_remote_copy(src_ref=src, dst_ref=land_ref.at[m - 1], send_sem=send_sem,
                                        recv_sem=recv_sem, device_id=_peer(m), device_id_type=MESH)


def _peer(m):
    x, y, c = _mesh_pos()
    bx, by, bc = (m >> 2) & 1, (m >> 1) & 1, m & 1
    return (x ^ bx if bx else x, y ^ by if by else y, c ^ bc if bc else c)


def _peer_index(m):
    x, y, c = _mesh_pos()
    return (4 * x + 2 * y + c) ^ m


SMALL_PLACE = {
    "ln1_gain": (ROW_LN1G, 0), "ln1_bias": (ROW_LN1B, 0), "ln2_gain": (ROW_LN2G, 0), "ln2_bias": (ROW_LN2B, 0),
    "ret_gn_gain": (ROW_GN, 0), "ret_decay_fwd": (ROW_MISC, MISC_DF), "ret_decay_bwd": (ROW_MISC, MISC_DB),
    "attn_sink": (ROW_MISC, MISC_SINK)}


def _small_adamw(me, part, landed, w, m, v):
    d = part.shape[1]
    names = list(SMALL_PLACE)
    k = len(names)

    def body(*refs):
        me_ref, part_ref, land_ref = refs[:3]
        refs = refs[2:]
        w_refs, m_refs, v_refs = refs[1:1 + k], refs[1 + k:1 + 2 * k], refs[1 + 2 * k:1 + 3 * k]
        outs = refs[1 + 3 * k:1 + 7 * k + 1]
        tot_ref = refs[-1]
        loss_ref, g_refs, dl_refs = outs[0], outs[1:1 + k], outs[1 + k:1 + 2 * k]
        nm_refs, nv_refs = outs[1 + 2 * k:1 + 3 * k], outs[1 + 3 * k:1 + 4 * k]
        tot = jnp.zeros(part_ref.shape, F32)
        for dev in range(N_DEV):
            j = dev ^ me_ref[0]
            tot = tot + jnp.where(j == 0, part_ref[...], land_ref[jnp.maximum(j, 1) - 1])
        tot_ref[...] = tot
        loss_ref[...] = (0.5 / d) * jnp.sum(tot_ref[ROW_LOSS:ROW_LOSS + 1, :], axis=1, keepdims=True)
        for i, name in enumerate(names):
            row, lo = SMALL_PLACE[name]
            wv = w_refs[i][...]
            g = tot_ref[row:row + 1, lo:lo + wv.shape[1]]
            if name.startswith("ret_decay"):
                p2 = jnp.exp2(wv)
                g = g * (-p2 * jnp.log(2.0) / (1.0 - p2))
            g_refs[i][...] = g
            _adamw_store(g, wv, m_refs[i][...], v_refs[i][...], dl_refs[i], nm_refs[i], nv_refs[i])

    shapes = [SDS(w[n].shape, F32) for n in names]
    vm = pl.BlockSpec(memory_space=pltpu.VMEM)
    res = pl.pallas_call(
        body, name="small_adamw", out_shape=[SDS((1, 1), F32)] + shapes * 4,
        in_specs=[_smem_spec()] + [vm] * (2 + 3 * k), out_specs=[vm] * (1 + 4 * k),
        scratch_shapes=[pltpu.VMEM(part.shape, F32)],
    )(me, part, landed, *[w[n] for n in names], *[m[n] for n in names], *[v[n] for n in names])
    groups = [dict(zip(names, res[1 + j * k:1 + (j + 1) * k])) for j in range(4)]
    return (res[0], *groups)


def _adamw_store(g, w, m, v, dl_ref, nm_ref, nv_ref):
    m = ADAM_B1 * m + (1.0 - ADAM_B1) * g
    v = ADAM_B2 * v + (1.0 - ADAM_B2) * (g * g)
    m_hat = m / (1.0 - ADAM_B1 ** ADAM_STEP)
    v_hat = v / (1.0 - ADAM_B2 ** ADAM_STEP)
    dl_ref[...] = -ADAM_LR * (m_hat / (jnp.sqrt(v_hat) + ADAM_EPS) + ADAM_WD * w)
    nm_ref[...] = m
    nv_ref[...] = v


def _reduce_adamw(name, owns, recvs, ws, ms, vs, transposed):
    count = len(owns)
    rows, n = owns[0].shape
    steps = 1 if transposed or rows % 16 else 2
    rb = rows // steps

    def body(*refs):
        ins, outs = refs[:5 * count], refs[5 * count:]
        j = pl.program_id(0)
        for k in range(count):
            @pl.when(j == k)
            def _(k=k):
                own_ref, recv_ref, w_ref, m_ref, v_ref = ins[5 * k:5 * k + 5]
                g_ref, dl_ref, nm_ref, nv_ref = outs[4 * k:4 * k + 4]
                g = own_ref[...]
                for p in range(recv_ref.shape[0]):
                    g = g + recv_ref[p].astype(F32)
                if transposed:
                    g = g.T
                g_ref[...] = g
                _adamw_store(g, w_ref[...], m_ref[...], v_ref[...], dl_ref, nm_ref, nv_ref)

    def turn(k):
        return lambda j, i: jnp.where(j == k, i, jnp.where(j < k, 0, steps - 1))

    in_specs, out_specs = [], []
    for k in range(count):
        at = turn(k)
        blk = pl.BlockSpec(ws[0].shape if transposed else (rb, n), lambda j, i, at=at: (at(j, i), 0))
        in_specs += [pl.BlockSpec((rb, n), lambda j, i, at=at: (at(j, i), 0)),
                     pl.BlockSpec((recvs[k].shape[0], rb, n), lambda j, i, at=at: (0, at(j, i), 0)), blk, blk, blk]
        out_specs += [blk] * 4
    res = pl.pallas_call(
        body, name="adamw_" + name, grid=(count, steps), in_specs=in_specs, out_specs=out_specs,
        out_shape=[SDS(ws[0].shape, F32)] * (4 * count), compiler_params=_params(("arbitrary", "arbitrary")),
    )(*[a for k in range(count) for a in (owns[k], recvs[k], ws[k], ms[k], vs[k])])
    return [list(res[j::4]) for j in range(4)]


def _row_spec(tm, width):
    return pl.BlockSpec((tm, width), lambda i: (i, 0))


def _full_spec(shape):
    return pl.BlockSpec(shape, lambda i: (0,) * len(shape))


_acc_spec = _full_spec


def _sub_rows(tm):
    step = min(SUB_ROWS, tm)
    return [(lo, lo + step) for lo in range(0, tm, step)]


def _in_proj(x2, wt_in):
    t, d = x2.shape
    u_w = wt_in.shape[0]
    tm = _row_tile(t, MATMUL_ROWS)

    def body(x_ref, w_ref, u_ref, xb_ref):
        xb = x_ref[...].astype(BF16)
        xb_ref[...] = xb
        u_ref[...] = _dot_nt(xb, w_ref[...]).astype(BF16)

    return pl.pallas_call(
        body, name="in_proj", grid=(t // tm,),
        in_specs=[_row_spec(tm, d), _full_spec(wt_in.shape)],
        out_specs=[_row_spec(tm, u_w), _row_spec(tm, d)],
        out_shape=[SDS((t, u_w), BF16), SDS((t, d), BF16)],
        compiler_params=_params(("parallel",)),
    )(x2, wt_in)


def _col_halves(f):
    n = f // LANES
    k = (n + 1) // 2 * LANES
    return [(0, k), (k, f)] if k < f else [(0, f)]


def _mix_ln1_ffn_up(r, a, x2, w_out, wt_gate, wt_up, g1, b1):
    t, d = x2.shape
    f = wt_gate.shape[0]
    tm = _row_tile(t, EPILOGUE_ROWS)

    def body(r_ref, a_ref, x_ref, wo_ref, wg_ref, wu_ref, g_ref, b_ref, xh_ref, rs_ref, hb_ref, dg_ref, du_ref,
             act_ref):
        mix = _dot_nn(r_ref[...], wo_ref[0:RET_W, :]) + _dot_nn(a_ref[...], wo_ref[RET_W:RET_W + ATT_W, :])
        z = ALPHA * x_ref[...] + mix
        xhat, rstd = _layer_norm_stats(z)
        xh_ref[...] = xhat
        rs_ref[...] = jnp.broadcast_to(rstd, rs_ref.shape)
        h = (xhat * g_ref[...] + b_ref[...]).astype(BF16)
        hb_ref[...] = h
        g = _dot_nt(h, wg_ref[...])
        u = _dot_nt(h, wu_ref[...])
        sg = _sigmoid(g)
        silu = g * sg
        dg_ref[...] = (u * (sg * (1.0 + g * (1.0 - sg)))).astype(BF16)
        du_ref[...] = silu.astype(BF16)
        act_ref[...] = (silu * u).astype(BF16)

    wide, narrow = _row_spec(tm, f), _row_spec(tm, d)
    return pl.pallas_call(
        body, name="mix_ln1_ffn_up", grid=(t // tm,),
        in_specs=[_row_spec(tm, RET_W), _row_spec(tm, ATT_W), narrow, _resident_spec(w_out.shape),
                  _resident_spec(wt_gate.shape), _resident_spec(wt_up.shape), _full_spec(g1.shape),
                  _full_spec(b1.shape)],
        out_specs=[narrow, _row_spec(tm, LANES), narrow, wide, wide, wide],
        out_shape=[SDS((t, d), F32), SDS((t, LANES), F32), SDS((t, d), BF16)] + [SDS((t, f), BF16)] * 3,
        compiler_params=_params(("parallel",)),
    )(r, a, x2, w_out, wt_gate, wt_up, g1, b1)


def _ffn_down_ln2_loss(act, dact_dg, dact_du, h1b, p2, xhat1, target, w_down, w_pg, wt_pe, g1, b1, g2, b2):
    t, d = xhat1.shape
    f = act.shape[1]
    pdim = p2.shape[1]
    tm = _row_tile(t, EPILOGUE_ROWS)

    def body(act_ref, fg_ref, fu_ref, hb_ref, p_ref, xh1_ref, tgt_ref, wd_ref, wpg_ref, wpe_ref, g1_ref, b1_ref,
             g2_ref, b2_ref, dz_ref, dzb_ref, ds_ref, dple_ref, dg_ref, du_ref, acc_ref):
        @pl.when(pl.program_id(0) == 0)
        def _():
            acc_ref[...] = jnp.zeros_like(acc_ref)

        for lo, hi in _sub_rows(tm):
            h1 = xh1_ref[lo:hi, :] * g1_ref[...] + b1_ref[...]
            pg = _sigmoid(_dot_nn(hb_ref[lo:hi, :], wpg_ref[...]))
            ple = _dot_nt(p_ref[lo:hi, :].astype(BF16), wpe_ref[...])
            gated = pg * ple
            dgate = gated * (1.0 - pg)
            ffn = _dot_nn(act_ref[lo:hi, :], wd_ref[...])
            z2 = ALPHA * h1 + gated + ffn
            xhat2, rstd2 = _layer_norm_stats(z2)
            err = xhat2 * g2_ref[...] + b2_ref[...] - tgt_ref[lo:hi, :]
            dy = err * (1.0 / d)
            dz = _layer_norm_bwd(dy * g2_ref[...], xhat2, rstd2)
            dzb = dz.astype(BF16)
            dz_ref[lo:hi, :] = dz
            dzb_ref[lo:hi, :] = dzb
            ds_ref[lo:hi, :] = (dz * dgate).astype(BF16)
            dple_ref[lo:hi, :] = (dz * pg).astype(BF16)
            acc_ref[0:1, :] += jnp.sum(err * err, axis=0, keepdims=True)
            acc_ref[1:2, :] += jnp.sum(dy * xhat2, axis=0, keepdims=True)
            acc_ref[2:3, :] += jnp.sum(dy, axis=0, keepdims=True)
            for c0, c1 in _col_halves(f):
                da = _dot_nt(dzb, wd_ref[c0:c1, :])
                dg_ref[lo:hi, c0:c1] = (da * fg_ref[lo:hi, c0:c1].astype(F32)).astype(BF16)
                du_ref[lo:hi, c0:c1] = (da * fu_ref[lo:hi, c0:c1].astype(F32)).astype(BF16)

    vec = _full_spec(g1.shape)
    wide, narrow = _row_spec(tm, f), _row_spec(tm, d)
    return pl.pallas_call(
        body, name="ffn_down_ln2_loss", grid=(t // tm,),
        in_specs=[wide, wide, wide, narrow, _row_spec(tm, pdim), narrow, narrow,
                  _full_spec(w_down.shape), _full_spec(w_pg.shape), _full_spec(wt_pe.shape), vec, vec, vec, vec],
        out_specs=[narrow] * 4 + [wide, wide, _acc_spec((8, d))],
        out_shape=[SDS((t, d), F32), SDS((t, d), BF16), SDS((t, d), BF16), SDS((t, d), BF16),
                   SDS((t, f), BF16), SDS((t, f), BF16), SDS((8, d), F32)],
        compiler_params=_params(("arbitrary",)),
    )(act, dact_dg, dact_du, h1b, p2, xhat1, target, w_down, w_pg, wt_pe, g1, b1, g2, b2)


def _after(after, body):
    k = len(after)
    return (lambda *refs: body(*refs[k:])), [ANY_SPEC] * k


def _resident_spec(shape):
    return pl.BlockSpec(shape, lambda i: (0,) * len(shape), pipeline_mode=pl.Buffered(1))


def _dh1_ln1_bwd(dz2, dg, dup, dsb, xhat1, rstd1, wt_gate, wt_up, w_pg, w_out, g1, after=()):
    t, d = dz2.shape
    f = dg.shape[1]
    tm = _row_tile(t, MATMUL_ROWS)

    def body(dz_ref, dg_ref, du_ref, ds_ref, xh1_ref, rs1_ref, wg_ref, wu_ref, wpg_ref, wo_ref, g1_ref,
             dz1_ref, dz1b_ref, dr_ref, da_ref, acc_ref):
        @pl.when(pl.program_id(0) == 0)
        def _():
            acc_ref[...] = jnp.zeros_like(acc_ref)

        for lo, hi in _sub_rows(tm):
            dh = (ALPHA * dz_ref[lo:hi, :] + _dot_nn(dg_ref[lo:hi, :], wg_ref[...])
                  + _dot_nn(du_ref[lo:hi, :], wu_ref[...]) + _dot_nt(ds_ref[lo:hi, :], wpg_ref[...]))
            xhat, rstd = xh1_ref[lo:hi, :], rs1_ref[lo:hi, 0:1]
            dz1 = _layer_norm_bwd(dh * g1_ref[...], xhat, rstd)
            dz1b = dz1.astype(BF16)
            dz1_ref[lo:hi, :] = dz1
            dz1b_ref[lo:hi, :] = dz1b
            acc_ref[0:1, :] += jnp.sum(dh * xhat, axis=0, keepdims=True)
            acc_ref[1:2, :] += jnp.sum(dh, axis=0, keepdims=True)
            dr_ref[lo:hi, :] = _dot_nt(dz1b, wo_ref[0:RET_W, :]).astype(BF16)
            da_ref[lo:hi, :] = _dot_nt(dz1b, wo_ref[RET_W:RET_W + ATT_W, :]).astype(BF16)

    body, lead = _after(after, body)
    return pl.pallas_call(
        body, name="dh1_ln1_bwd", grid=(t // tm,),
        in_specs=lead + [_row_spec(tm, d), _row_spec(tm, f), _row_spec(tm, f), _row_spec(tm, d), _row_spec(tm, d),
                         _row_spec(tm, LANES), _resident_spec(wt_gate.shape), _resident_spec(wt_up.shape), _resident_spec(w_pg.shape),
                         _resident_spec(w_out.shape), _full_spec(g1.shape)],
        out_specs=[_row_spec(tm, d), _row_spec(tm, d), _row_spec(tm, RET_W), _row_spec(tm, ATT_W), _acc_spec((8, d))],
        out_shape=[SDS((t, d), F32), SDS((t, d), BF16), SDS((t, RET_W), BF16), SDS((t, ATT_W), BF16),
                   SDS((8, d), F32)],
        compiler_params=_params(("arbitrary",)),
    )(*after, dz2, dg, dup, dsb, xhat1, rstd1, wt_gate, wt_up, w_pg, w_out, g1)


def _in_proj_bwd(dz1, parts, wt_in, after=()):
    t, d = dz1.shape
    tm = _row_tile(t, MATMUL_ROWS)
    widths = [p.shape[1] for p in parts]

    def body(*refs):
        dz_ref, part_refs, w_ref, dx_ref = refs[0], refs[1:1 + len(parts)], refs[-2], refs[-1]
        acc = ALPHA * dz_ref[...]
        lo = 0
        for p_ref, w in zip(part_refs, widths):
            acc = acc + _dot_nn(p_ref[...], w_ref[lo:lo + w, :])
            lo += w
        dx_ref[...] = acc

    body, lead = _after(after, body)
    return pl.pallas_call(
        body, name="in_proj_bwd", grid=(t // tm,),
        in_specs=lead + [_row_spec(tm, d)] + [_row_spec(tm, w) for w in widths] + [_full_spec(wt_in.shape)],
        out_specs=_row_spec(tm, d), out_shape=SDS((t, d), F32),
        compiler_params=_params(("parallel",)),
    )(*after, dz1, *parts, wt_in)


def _weight_grad(name, me, parts, rhs, after=()):
    t, n = rhs.shape
    widths = [p.shape[1] for p in parts]
    rows = sum(widths)
    own_rows = rows // N_DEV
    tk = _row_tile(t, MATMUL_ROWS)
    n_steps = t // tk
    step = 256

    def body(*refs):
        me_ref, part_refs, rhs_ref = refs[0], refs[1:1 + len(parts)], refs[1 + len(parts)]
        full_ref, own_ref, acc = refs[-3], refs[-2], refs[-1]
        i = pl.program_id(0)

        def products(first):
            b = rhs_ref[...].astype(BF16)
            lo = 0
            for p_ref, w in zip(part_refs, widths):
                for c0 in range(0, w, step):
                    c1 = min(c0 + step, w)
                    val = _dot_tn(p_ref[:, c0:c1].astype(BF16), b)
                    if first:
                        acc[lo + c0:lo + c1, :] = val
                    else:
                        acc[lo + c0:lo + c1, :] += val
                lo += w

        pl.when(i == 0)(functools.partial(products, True))
        pl.when(i > 0)(functools.partial(products, False))

        @pl.when(i == n_steps - 1)
        def _():
            full_ref[...] = acc[...].astype(BF16)
            own_ref[...] = acc[pl.ds(pl.multiple_of(me_ref[0] * own_rows, 8), own_rows), :]

    body, lead = _after(after, body)
    return pl.pallas_call(
        body, name=name, grid=(n_steps,),
        in_specs=lead + [_smem_spec()] + [_row_spec(tk, w) for w in widths] + [_row_spec(tk, n)],
        out_specs=[_full_spec((rows, n)), _full_spec((own_rows, n))],
        out_shape=[SDS((rows, n), BF16), SDS((own_rows, n), F32)],
        scratch_shapes=[pltpu.VMEM((rows, n), F32)],
        compiler_params=_params(("arbitrary",)),
    )(*after, me, *parts, rhs)


def _weight_grad_jobs(name, me, jobs, after=()):
    count = len(jobs)
    t = jobs[0][0].shape[0]
    tk = _row_tile(t, MATMUL_ROWS)
    n_steps = t // tk
    shapes = [(lhs.shape[1], rhs.shape[1]) for lhs, rhs in jobs]
    most_rows, most_cols = max(r for r, _ in shapes), max(n for _, n in shapes)
    step = 256

    def body(*refs):
        me_ref, lhs_refs, rhs_refs = refs[0], refs[1:1 + count], refs[1 + count:1 + 2 * count]
        full_refs, own_refs = refs[1 + 2 * count:1 + 3 * count], refs[1 + 3 * count:1 + 4 * count]
        acc, whole, mine, sems = refs[1 + 4 * count:]
        job, i = pl.program_id(0), pl.program_id(1)

        def leaving(j):
            rows, n = shapes[j]
            return (pltpu.make_async_copy(whole.at[0:rows, 0:n], full_refs[j], sems.at[0]),
                    pltpu.make_async_copy(mine.at[0:rows // N_DEV, 0:n], own_refs[j], sems.at[1]))

        def products(j, first):
            rows, n = shapes[j]
            b = rhs_refs[j][...].astype(BF16)
            for c0 in range(0, rows, step):
                c1 = min(c0 + step, rows)
                val = _dot_tn(lhs_refs[j][:, c0:c1].astype(BF16), b)
                if first:
                    acc[c0:c1, 0:n] = val
                else:
                    acc[c0:c1, 0:n] += val

        def finish(j):
            rows, n = shapes[j]
            own_rows = rows // N_DEV
            if j > 0:
                for cp in leaving(j - 1):
                    cp.wait()
            whole[0:rows, 0:n] = acc[0:rows, 0:n].astype(BF16)
            mine[0:own_rows, 0:n] = acc[pl.ds(pl.multiple_of(me_ref[0] * own_rows, 8), own_rows), 0:n]
            for cp in leaving(j):
                cp.start()
            if j == count - 1:
                for cp in leaving(j):
                    cp.wait()

        for j in range(count):
            pl.when((job == j) & (i == 0))(functools.partial(products, j, True))
            pl.when((job == j) & (i > 0))(functools.partial(products, j, False))
            pl.when((job == j) & (i == n_steps - 1))(functools.partial(finish, j))

    def turn(j):
        return lambda job, i: (jnp.where(job == j, i, jnp.where(job < j, 0, n_steps - 1)), 0)

    body, lead = _after(after, body)
    res = pl.pallas_call(
        body, name=name, grid=(count, n_steps),
        in_specs=lead + [_smem_spec()] + [pl.BlockSpec((tk, rows), turn(j)) for j, (rows, _) in enumerate(shapes)]
        + [pl.BlockSpec((tk, n), turn(j)) for j, (_, n) in enumerate(shapes)],
        out_specs=[ANY_SPEC] * (2 * count),
        out_shape=[SDS((rows, n), BF16) for rows, n in shapes] + [SDS((rows // N_DEV, n), F32) for rows, n in shapes],
        scratch_shapes=[pltpu.VMEM((most_rows, most_cols), F32), pltpu.VMEM((most_rows, most_cols), BF16),
                        pltpu.VMEM((most_rows // N_DEV, most_cols), F32), pltpu.SemaphoreType.DMA((2,))],
        compiler_params=_params(("arbitrary", "arbitrary")),
    )(*after, me, *[lhs for lhs, _ in jobs], *[rhs for _, rhs in jobs])
    return list(res[:count]), list(res[count:])


def _log_decay(decay_f, decay_b):
    def body(f_ref, b_ref, lf_ref, lb_ref):
        lf_ref[...] = jnp.log1p(-jnp.exp2(f_ref[...]))
        lb_ref[...] = jnp.log1p(-jnp.exp2(b_ref[...]))

    return pl.pallas_call(body, name="log_decay", out_shape=[SDS(decay_f.shape, F32)] * 2)(decay_f, decay_b)


def _chunk(ref, n):
    return ref[pl.ds(pl.multiple_of(n * CHUNK, CHUNK), CHUNK), :]


def _group_sum(is_a, v):
    sa = jnp.sum(jnp.where(is_a, v, 0.0), axis=1, keepdims=True)
    sb = jnp.sum(jnp.where(is_a, 0.0, v), axis=1, keepdims=True)
    return jnp.where(is_a, sa, sb)


def _seq_spec(s, col_block):
    return pl.BlockSpec((s, LANES), lambda b, h: (b, col_block + h))


def _smem_spec():
    return pl.BlockSpec(memory_space=pltpu.SMEM)


RET_UNROLL = 4
FWD_PREP_UNROLL = 16
BWD_PREP_UNROLL = 8


def _chunk_loop(n_chunks, body, init, unroll):
    u = unroll if n_chunks % unroll == 0 else 1

    def trip(i, carry):
        for j in range(u):
            carry = body(i * u + j, carry)
        return carry

    return lax.fori_loop(0, n_chunks // u, trip, init)


def _stacked_tables(lgf_ref, lgb_ref, pair):
    lane = lax.broadcasted_iota(jnp.int32, (1, LANES), 1)
    is_a = lane < HEAD_DIM
    lgf = jnp.where(is_a, lgf_ref[2 * pair], lgf_ref[2 * pair + 1])
    lgb = jnp.where(is_a, lgb_ref[2 * pair], lgb_ref[2 * pair + 1])
    row = lax.broadcasted_iota(jnp.int32, (CHUNK, 1), 0).astype(F32)
    kdec_f, qdec_f = jnp.exp(lgf * (CHUNK - 1.0 - row)), jnp.exp(lgf * (row + 1.0))
    kdec_b, qdec_b = jnp.exp(lgb * row), jnp.exp(lgb * (CHUNK - row))
    tab = dict(
        is_a=is_a, row=row, lam_f=jnp.exp(lgf * CHUNK), lam_b=jnp.exp(lgb * CHUNK),
        kdec=jnp.concatenate([kdec_f, kdec_b], axis=1), qdec=jnp.concatenate([qdec_f, qdec_b], axis=1),
        qexp=jnp.concatenate([jnp.broadcast_to(row + 1.0, (CHUNK, LANES)),
                              jnp.broadcast_to(CHUNK - row, (CHUNK, LANES))], axis=1),
        kexp=jnp.concatenate([jnp.broadcast_to(CHUNK - 1.0 - row, (CHUNK, LANES)),
                              jnp.broadcast_to(row, (CHUNK, LANES))], axis=1),
    )
    r = lax.broadcasted_iota(jnp.int32, (2 * LANES, LANES), 0)
    c = lax.broadcasted_iota(jnp.int32, (2 * LANES, LANES), 1)
    tab["diag2"] = ((r & (LANES - 1)) < HEAD_DIM) == (c < HEAD_DIM)
    i2 = lax.broadcasted_iota(jnp.int32, (2 * CHUNK, CHUNK), 0)
    j = lax.broadcasted_iota(jnp.int32, (2 * CHUNK, CHUNK), 1)
    head_b = i2 >= CHUNK
    diff = ((i2 & (CHUNK - 1)) - j).astype(F32)
    up, dn = jnp.maximum(diff, 0.0), jnp.maximum(-diff, 0.0)
    lgf2 = jnp.where(head_b, lgf_ref[2 * pair + 1], lgf_ref[2 * pair])
    lgb2 = jnp.where(head_b, lgb_ref[2 * pair + 1], lgb_ref[2 * pair])
    ef = jnp.where(diff >= 0, jnp.exp(lgf2 * up), 0.0)
    eb = jnp.where(diff <= 0, jnp.exp(lgb2 * dn), 0.0)
    tab["d2"] = ef + eb
    tab["df2"] = ef * up
    tab["db2"] = eb * dn
    return tab


def _stack_pair(is_a, x):
    zero = jnp.zeros_like(x)
    return jnp.concatenate([jnp.where(is_a, x, zero), jnp.where(is_a, zero, x)], axis=0)


def _unstack_pair(is_a, x2):
    return jnp.where(is_a, x2[0:CHUNK, :], x2[CHUNK:2 * CHUNK, :])


def _both_ways(x, dec):
    return (jnp.concatenate([x, x], axis=1) * dec).astype(BF16)


def _scan_states(n_chunks, st, up_rows, up_lam, down_rows, down_lam):
    zero = jnp.zeros((LANES, LANES), F32)

    def up(n, r):
        new = st[n, up_rows, :]
        st[n, up_rows, :] = r
        return r * up_lam + new

    def down(s, r):
        n = n_chunks - 1 - s
        new = st[n, down_rows, :]
        st[n, down_rows, :] = r
        return r * down_lam + new

    lax.fori_loop(0, n_chunks, up, zero)
    lax.fori_loop(0, n_chunks, down, zero)


FWD_ROWS, BWD_ROWS = pl.ds(0, LANES), pl.ds(LANES, LANES)


def _state_spec(n_chunks, pairs):
    return pl.BlockSpec((n_chunks, 2 * LANES, LANES), lambda b, h: (b * pairs + h, 0, 0))


ST_GAIN, ST_XF, ST_XB, ST_IFA, ST_IFB, ST_IBA, ST_IBB, ST_LF, ST_LB = 0, 1, 2, 3, 4, 5, 6, 8, 9
ST_ROWS = 16


GW = GROUP * HEAD_DIM
KEYS = 3 * BLOCK


def _attn_tables(g, bias_ref):
    r = lax.broadcasted_iota(jnp.int32, (GROUP * BLOCK, KEYS), 0)
    kj = lax.broadcasted_iota(jnp.int32, (GROUP * BLOCK, KEYS), 1)
    qi = r & (BLOCK - 1)
    hh = lax.shift_right_logical(r, 7)
    dist = jnp.abs(kj - BLOCK - qi)
    slope = jnp.exp2(-(GROUP * g + hh + 1).astype(F32) * (8.0 / ATTN_HEADS))
    inside = jnp.where(dist <= BLOCK, -slope * dist.astype(F32), NEG_INF)
    bias_ref[BIAS_INSIDE] = inside
    bias_ref[BIAS_FIRST] = jnp.where(kj >= BLOCK, inside, NEG_INF)
    bias_ref[BIAS_LAST] = jnp.where(kj < 2 * BLOCK, inside, NEG_INF)


BIAS_INSIDE, BIAS_FIRST, BIAS_LAST = 0, 1, 2


def _own_lanes(g):
    return lax.shift_right_logical(lax.broadcasted_iota(jnp.int32, (1, LANES), 1), 6) == g


def _mask_keys(x_ref, g, scale, pad_ref, s):
    pad_ref[0:BLOCK, :] = jnp.zeros((BLOCK, LANES), BF16)
    pad_ref[BLOCK + s:2 * BLOCK + s, :] = jnp.zeros((BLOCK, LANES), BF16)
    pad_ref[BLOCK:BLOCK + s, :] = jnp.where(_own_lanes(g), x_ref[...].astype(F32) * scale, 0.0).astype(BF16)


def _lane_block(x, j):
    return x[:, j * LANES:(j + 1) * LANES]


def _stack_heads(x, g):
    assert GROUP == 4 and GW == 2 * LANES
    x1 = pltpu.roll(x, HEAD_DIM, 1)
    keep = _own_lanes(g)
    zero = jnp.zeros((BLOCK, LANES), x.dtype)
    rows = []
    for h in range(GROUP):
        for_g0 = _lane_block(x, h // 2) if h % 2 == 0 else _lane_block(x1, ((h + 1) // 2) % 2)
        for_g1 = _lane_block(x, h // 2) if h % 2 == 1 else _lane_block(x1, h // 2)
        rows.append(jnp.where(keep, jnp.where(g == 0, for_g0, for_g1), zero))
    return jnp.concatenate(rows, axis=0)


def _unstack_heads(x4, g):
    p = [x4[h * BLOCK:(h + 1) * BLOCK, :] for h in range(GROUP)]
    cat = lambda a, b: jnp.concatenate([a, b], axis=1)
    in_place = jnp.where(g == 0, cat(p[0], p[2]), cat(p[1], p[3]))
    one_left = jnp.where(g == 0, cat(p[1], p[3]), cat(p[2], p[0]))
    return in_place + pltpu.roll(one_left, HEAD_DIM, 1)


def _sink_column(sink_ref, g):
    rh = lax.shift_right_logical(lax.broadcasted_iota(jnp.int32, (GROUP * BLOCK, 1), 0), 7)
    col = jnp.zeros((GROUP * BLOCK, 1), F32)
    for h in range(GROUP):
        col = jnp.where(rh == h, sink_ref[GROUP * g + h], col)
    return col


def _attn_probs(qm, k3, bias_ref, sink_col, n, s):
    which = jnp.where(n == 0, BIAS_FIRST, jnp.where(n == s // BLOCK - 1, BIAS_LAST, BIAS_INSIDE))
    logits = _dot_nt(qm, k3) + bias_ref[which]
    m = jnp.maximum(jnp.max(logits, axis=1, keepdims=True), sink_col)
    e = jnp.exp(logits - m)
    e_sink = jnp.exp(sink_col - m)
    inv = 1.0 / (jnp.sum(e, axis=1, keepdims=True) + e_sink)
    return e * inv, e_sink * inv


PAIRS_PER_KV = (RET_HEADS // 2) // KV_HEADS
FWD_ORDER = "rrarra"
BWD_ORDER = "rararr"


def _trip_order(order, chunks, blocks):
    if order.count("r") == chunks and order.count("a") == blocks:
        return order
    return "r" * chunks + "a" * blocks


def _mixers_fwd(u, lgf, lgb, gn_gain, sink, b_loc, after=()):
    t = u.shape[0]
    s = t // b_loc
    n_chunks = s // CHUNK
    pairs = RET_HEADS // 2
    trips = n_chunks // RET_UNROLL
    blocks_half = (s // BLOCK) // PAIRS_PER_KV
    per_trip = blocks_half // trips
    assert n_chunks % RET_UNROLL == 0 and blocks_half % trips == 0 and PAIRS_PER_KV == 2 and s >= 2 * BLOCK

    def body(lgf_ref, lgb_ref, sink_ref, q_ref, k_ref, v_ref, g_ref, gain_ref, aq_ref, ak_ref, av_ref,
             r_ref, xhat_ref, rstd_ref, a_ref, st, kpad, vpad, bias):
        pair = pl.program_id(1)
        g, half = lax.shift_right_logical(pair, 1), pair & 1
        tab = _stacked_tables(lgf_ref, lgb_ref, pair)
        is_a = tab["is_a"]

        @pl.when(half == 0)
        def _():
            _attn_tables(g, bias)
            _mask_keys(ak_ref, g, Q_SCALE, kpad, s)
            _mask_keys(av_ref, g, 1.0, vpad, s)

        def kv_body(n, _):
            k8 = _chunk(k_ref, n).astype(F32) * Q_SCALE
            st[n] = jnp.where(tab["diag2"], _dot_tn(_both_ways(k8, tab["kdec"]), _chunk(v_ref, n)), 0.0)
            return 0

        _chunk_loop(n_chunks, kv_body, 0, FWD_PREP_UNROLL)
        _scan_states(n_chunks, st, FWD_ROWS, tab["lam_f"], BWD_ROWS, tab["lam_b"])
        sink_col = _sink_column(sink_ref, g)

        def retention_chunk(n):
            q = _chunk(q_ref, n)
            k8 = (_chunk(k_ref, n).astype(F32) * Q_SCALE).astype(BF16)
            v = _chunk(v_ref, n)
            p2 = (_dot_nt(_stack_pair(is_a, q), k8) * tab["d2"]).astype(BF16)
            y = _unstack_pair(is_a, _dot_nn(p2, v))
            y = y + _dot_nn(_both_ways(q.astype(F32), tab["qdec"]), st[n].astype(BF16))
            rows = pl.ds(pl.multiple_of(n * CHUNK, CHUNK), CHUNK)
            mu = _group_sum(is_a, y) * (1.0 / HEAD_DIM)
            dlt = y - mu
            var = _group_sum(is_a, dlt * dlt) * (1.0 / HEAD_DIM)
            rstd = lax.rsqrt(var + GN_EPS)
            xhat = dlt * rstd
            xhat_ref[rows, :] = xhat
            rstd_ref[rows, :] = rstd
            gate = _chunk(g_ref, n).astype(F32)
            r_ref[rows, :] = (xhat * gain_ref[...] * gate * _sigmoid(gate)).astype(BF16)

        def attention_block(blk):
            n = half * blocks_half + blk
            rows = pl.ds(pl.multiple_of(blk * BLOCK, BLOCK), BLOCK)
            keys = pl.ds(pl.multiple_of(n * BLOCK, BLOCK), KEYS)
            p, _ = _attn_probs(_stack_heads(aq_ref[rows, :], g), kpad[keys, :], bias, sink_col, n, s)
            a_ref[rows, :] = _unstack_heads(_dot_nn(p.astype(BF16), vpad[keys, :]), g).astype(BF16)

        def trip(i, _):
            chunk, blk = 0, 0
            for kind in _trip_order(FWD_ORDER, RET_UNROLL, per_trip):
                if kind == "r":
                    retention_chunk(i * RET_UNROLL + chunk)
                    chunk += 1
                else:
                    attention_block(i * per_trip + blk)
                    blk += 1
            return 0

        lax.fori_loop(0, trips, trip, 0)

    lane_blk = lambda c0: _seq_spec(s, c0 // LANES)
    half_rows = blocks_half * BLOCK
    aq_spec = pl.BlockSpec((half_rows, GW), lambda b, h: (b * PAIRS_PER_KV + (h & 1), C_AQ // GW + h // 2))
    a_spec = pl.BlockSpec((half_rows, GW), lambda b, h: (b * PAIRS_PER_KV + (h & 1), h // 2))
    kv_spec = lambda c0: pl.BlockSpec((s, LANES), lambda b, h: (b, c0 // LANES))
    pad = pltpu.VMEM((s + 2 * BLOCK, LANES), BF16)
    body, lead = _after(after, body)
    return pl.pallas_call(
        body, name="mixers_fwd", grid=(b_loc, pairs),
        in_specs=lead + [_smem_spec(), _smem_spec(), _smem_spec(), lane_blk(C_RQ), lane_blk(C_RK), lane_blk(C_RV),
                         lane_blk(C_RG), pl.BlockSpec((1, LANES), lambda b, h: (0, h)), aq_spec, kv_spec(C_AK),
                         kv_spec(C_AV)],
        out_specs=[_seq_spec(s, 0), _seq_spec(s, 0), _seq_spec(s, 0), a_spec, _state_spec(n_chunks, pairs)],
        out_shape=[SDS((t, RET_W), BF16), SDS((t, RET_W), F32), SDS((t, RET_W), F32), SDS((t, ATT_W), BF16),
                   SDS((b_loc * pairs * n_chunks, 2 * LANES, LANES), F32)],
        scratch_shapes=[pad, pad, pltpu.VMEM((3, GROUP * BLOCK, KEYS), F32)],
        compiler_params=_params(("arbitrary", "arbitrary")),
    )(*after, lgf, lgb, sink, u, u, u, u, gn_gain, u, u, u)


def _mixers_bwd(u, xhat, rstd, states, dr, da, lgf, lgb, gn_gain, sink, b_loc, after=()):
    t = u.shape[0]
    s = t // b_loc
    n_chunks = s // CHUNK
    pairs = RET_HEADS // 2
    trips = n_chunks // RET_UNROLL
    blocks_half = (s // BLOCK) // PAIRS_PER_KV
    per_trip = blocks_half // trips
    assert n_chunks % RET_UNROLL == 0 and blocks_half % trips == 0 and PAIRS_PER_KV == 2 and s >= 2 * BLOCK

    def body(lgf_ref, lgb_ref, sink_ref, q_ref, k_ref, v_ref, g_ref, xhat_ref, rstd_ref, dr_ref, gain_ref,
             aq_ref, ak_ref, av_ref, do_ref, st,
             dq_ref, dk_ref, dv_ref, dg_ref, st_ref, daq_ref, dak_ref, dav_ref, dsink_ref,
             gr, dy_s, kpad, vpad, bias, dk_acc, dv_acc):
        pair = pl.program_id(1)
        g, half = lax.shift_right_logical(pair, 1), pair & 1
        tab = _stacked_tables(lgf_ref, lgb_ref, pair)
        is_a = tab["is_a"]
        gain = gain_ref[...]

        @pl.when(half == 0)
        def _():
            _attn_tables(g, bias)
            _mask_keys(ak_ref, g, Q_SCALE, kpad, s)
            _mask_keys(av_ref, g, 1.0, vpad, s)
            dsink_ref[...] = jnp.zeros_like(dsink_ref)

        @pl.when(pair == 0)
        def _():
            dk_acc[...] = jnp.zeros_like(dk_acc)
            dv_acc[...] = jnp.zeros_like(dv_acc)

        def norm_body(n, dgain):
            rows = pl.ds(pl.multiple_of(n * CHUNK, CHUNK), CHUNK)
            xhat, rstd = xhat_ref[rows, :], rstd_ref[rows, :]
            gate = g_ref[rows, :].astype(F32)
            sg = _sigmoid(gate)
            silu = gate * sg
            d_out = dr_ref[rows, :].astype(F32)
            dg_ref[rows, :] = (d_out * xhat * gain * (sg * (1.0 + gate * (1.0 - sg)))).astype(BF16)
            dxh = d_out * gain * silu
            m1 = _group_sum(is_a, dxh) * (1.0 / HEAD_DIM)
            m2 = _group_sum(is_a, dxh * xhat) * (1.0 / HEAD_DIM)
            dy = (rstd * (dxh - m1 - xhat * m2)).astype(BF16)
            dy_s[rows, :] = dy
            qf = q_ref[rows, :].astype(F32)
            gr[n] = jnp.where(tab["diag2"], _dot_tn(_both_ways(qf, tab["qdec"]), dy), 0.0)
            return dgain + jnp.sum(d_out * xhat * silu, axis=0, keepdims=True)

        colsum = lambda x: jnp.sum(x, axis=0, keepdims=True)

        def grad_body(n, carry):
            xfb, ifa, ifb, iba, ibb, lf, lb = carry
            rows = pl.ds(pl.multiple_of(n * CHUNK, CHUNK), CHUNK)
            q = q_ref[rows, :]
            qf = q.astype(F32)
            k8f = k_ref[rows, :].astype(F32) * Q_SCALE
            k8 = k8f.astype(BF16)
            v = v_ref[rows, :]
            dy = dy_s[rows, :]
            q2, dy2 = _stack_pair(is_a, q), _stack_pair(is_a, dy)
            sc = _dot_nt(q2, k8)
            dp = _dot_nt(dy2, v)
            a2 = (sc * tab["d2"]).astype(BF16)
            ds2 = (dp * tab["d2"]).astype(BF16)
            dq = _unstack_pair(is_a, _dot_nn(ds2, k8))
            dk = _dot_tn(ds2, q2)
            dv = _dot_tn(a2, dy2)
            prod = sc * dp
            pf, pb = prod * tab["df2"], prod * tab["db2"]
            ifa, ifb = ifa + colsum(pf[0:CHUNK, :]), ifb + colsum(pf[CHUNK:2 * CHUNK, :])
            iba, ibb = iba + colsum(pb[0:CHUNK, :]), ibb + colsum(pb[CHUNK:2 * CHUNK, :])
            states, sgrads = st[n], gr[n]
            sb, gb = states.astype(BF16), sgrads.astype(BF16)
            dqc = _dot_nt(dy, sb) * tab["qdec"]
            dkc = _dot_nt(v, gb) * tab["kdec"]
            dv = dv + _dot_nn(_both_ways(k8f, tab["kdec"]), gb)
            dq_ref[rows, :] = (dq + dqc[:, 0:LANES] + dqc[:, LANES:2 * LANES]).astype(BF16)
            dk_ref[rows, :] = ((dk + dkc[:, 0:LANES] + dkc[:, LANES:2 * LANES]) * Q_SCALE).astype(BF16)
            dv_ref[rows, :] = dv.astype(BF16)
            q2w, k2w = jnp.concatenate([qf, qf], axis=1), jnp.concatenate([k8f, k8f], axis=1)
            xfb = xfb + colsum(tab["qexp"] * q2w * dqc + tab["kexp"] * k2w * dkc)
            prod_s = sgrads * states
            lf, lb = lf + colsum(prod_s[0:LANES, :]), lb + colsum(prod_s[LANES:2 * LANES, :])
            return xfb, ifa, ifb, iba, ibb, lf, lb

        sink_col = _sink_column(sink_ref, g)
        head_row = lax.broadcasted_iota(jnp.int32, dsink_ref.shape, 0)

        def attention_block(blk):
            n = half * blocks_half + blk
            rows = pl.ds(pl.multiple_of(blk * BLOCK, BLOCK), BLOCK)
            keys = pl.ds(pl.multiple_of(n * BLOCK, BLOCK), KEYS)
            qm = _stack_heads(aq_ref[rows, :], g)
            k3, v3 = kpad[keys, :], vpad[keys, :]
            p, p_sink = _attn_probs(qm, k3, bias, sink_col, n, s)
            dom = _stack_heads(do_ref[rows, :], g)
            dp = _dot_nt(dom, v3)
            delta = jnp.sum(p * dp, axis=1, keepdims=True)
            ds_mat = (p * (dp - delta)).astype(BF16)
            daq_ref[rows, :] = _unstack_heads(_dot_nn(ds_mat, k3), g).astype(BF16)
            dk_acc[keys, :] += _dot_tn(ds_mat, qm) * Q_SCALE
            dv_acc[keys, :] += _dot_tn(p.astype(BF16), dom)
            w = p_sink * delta
            upd = jnp.zeros(dsink_ref.shape, F32)
            for h in range(GROUP):
                upd = upd + jnp.where(head_row == h, -jnp.sum(w[h * BLOCK:(h + 1) * BLOCK, :]), 0.0)
            dsink_ref[...] += upd

        dgain = _chunk_loop(n_chunks, norm_body, jnp.zeros((1, LANES), F32), BWD_PREP_UNROLL)
        _scan_states(n_chunks, gr, BWD_ROWS, tab["lam_b"], FWD_ROWS, tab["lam_f"])

        def trip(i, carry):
            chunk, blk = 0, 0
            for kind in _trip_order(BWD_ORDER, RET_UNROLL, per_trip):
                if kind == "r":
                    carry = grad_body(i * RET_UNROLL + chunk, carry)
                    chunk += 1
                else:
                    attention_block(i * per_trip + blk)
                    blk += 1
            return carry

        z = jnp.zeros((1, LANES), F32)
        init = (jnp.zeros((1, 2 * LANES), F32), z, z, z, z, z, z)
        xfb, ifa, ifb, iba, ibb, lf, lb = lax.fori_loop(0, trips, trip, init)
        st_ref[...] = jnp.zeros_like(st_ref)
        st_ref[ST_GAIN:ST_GAIN + 1, :] = dgain
        st_ref[ST_XF:ST_XF + 1, :] = xfb[:, 0:LANES]
        st_ref[ST_XB:ST_XB + 1, :] = xfb[:, LANES:2 * LANES]
        st_ref[ST_IFA:ST_IFA + 1, :] = ifa
        st_ref[ST_IFB:ST_IFB + 1, :] = ifb
        st_ref[ST_IBA:ST_IBA + 1, :] = iba
        st_ref[ST_IBB:ST_IBB + 1, :] = ibb
        st_ref[ST_LF:ST_LF + 1, :] = lf * (CHUNK * tab["lam_f"])
        st_ref[ST_LB:ST_LB + 1, :] = lb * (CHUNK * tab["lam_b"])

        @pl.when(pair == pairs - 1)
        def _():
            dak_ref[...] = dk_acc[BLOCK:BLOCK + s, :].astype(BF16)
            dav_ref[...] = dv_acc[BLOCK:BLOCK + s, :].astype(BF16)

    lane_blk = lambda c0: _seq_spec(s, c0 // LANES)
    seq0 = _seq_spec(s, 0)
    half_rows = blocks_half * BLOCK
    aq_spec = pl.BlockSpec((half_rows, GW), lambda b, h: (b * PAIRS_PER_KV + (h & 1), C_AQ // GW + h // 2))
    a_spec = pl.BlockSpec((half_rows, GW), lambda b, h: (b * PAIRS_PER_KV + (h & 1), h // 2))
    kv_spec = lambda c0: pl.BlockSpec((s, LANES), lambda b, h: (b, c0 // LANES))
    kv_out = pl.BlockSpec((s, LANES), lambda b, h: (b, 0))
    state = pltpu.VMEM((n_chunks, 2 * LANES, LANES), F32)
    pad = pltpu.VMEM((s + 2 * BLOCK, LANES), BF16)
    acc = pltpu.VMEM((s + 2 * BLOCK, LANES), F32)
    body, lead = _after(after, body)
    return pl.pallas_call(
        body, name="mixers_bwd", grid=(b_loc, pairs),
        in_specs=lead + [_smem_spec(), _smem_spec(), _smem_spec(), lane_blk(C_RQ), lane_blk(C_RK), lane_blk(C_RV),
                         lane_blk(C_RG), seq0, seq0, seq0, pl.BlockSpec((1, LANES), lambda b, h: (0, h)),
                         aq_spec, kv_spec(C_AK), kv_spec(C_AV), a_spec, _state_spec(n_chunks, pairs)],
        out_specs=[seq0] * 4 + [pl.BlockSpec((ST_ROWS, LANES), lambda b, h: (b, h)), a_spec, kv_out, kv_out,
                                pl.BlockSpec((8, LANES), lambda b, h: (b * KV_HEADS + h // 2, 0))],
        out_shape=[SDS((t, RET_W), BF16)] * 4 + [SDS((b_loc * ST_ROWS, RET_W), F32), SDS((t, ATT_W), BF16),
                                                   SDS((t, KV_W), BF16), SDS((t, KV_W), BF16),
                                                   SDS((b_loc * KV_HEADS * 8, LANES), F32)],
        scratch_shapes=[state, pltpu.VMEM((s, LANES), BF16), pad, pad,
                        pltpu.VMEM((3, GROUP * BLOCK, KEYS), F32), acc, acc],
        compiler_params=_params(("arbitrary", "arbitrary")),
    )(*after, lgf, lgb, sink, u, u, u, u, xhat, rstd, dr, gn_gain, u, u, u, da, states)


def _pack_small(acc2, acc1, ret_stats, dsink, b_loc, d):
    pairs = RET_HEADS // 2

    def body(acc2_ref, acc1_ref, st_ref, dsink_ref, out_ref):
        out_ref[...] = jnp.zeros_like(out_ref)
        out_ref[ROW_LN1G:ROW_LN1G + 1, :] = acc1_ref[0:1, :]
        out_ref[ROW_LN1B:ROW_LN1B + 1, :] = acc1_ref[1:2, :]
        out_ref[ROW_LN2G:ROW_LN2G + 1, :] = acc2_ref[1:2, :]
        out_ref[ROW_LN2B:ROW_LN2B + 1, :] = acc2_ref[2:3, :]
        out_ref[ROW_LOSS:ROW_LOSS + 1, :] = acc2_ref[0:1, :]
        st = st_ref[0:ST_ROWS, :]
        for b in range(1, b_loc):
            st = st + st_ref[b * ST_ROWS:(b + 1) * ST_ROWS, :]
        out_ref[ROW_GN:ROW_GN + 1, 0:RET_W] = st[ST_GAIN:ST_GAIN + 1, :]
        lane = lax.broadcasted_iota(jnp.int32, (1, d), 1)
        misc = jnp.zeros((1, d), F32)
        for pr in range(pairs):
            blk = st[:, pr * LANES:(pr + 1) * LANES]
            half = lax.broadcasted_iota(jnp.int32, (1, LANES), 1) < HEAD_DIM
            for h in range(2):
                sel = half if h == 0 else jnp.logical_not(half)
                cross_f = jnp.sum(jnp.where(sel, blk[ST_XF:ST_XF + 1, :] + blk[ST_LF:ST_LF + 1, :], 0.0))
                cross_b = jnp.sum(jnp.where(sel, blk[ST_XB:ST_XB + 1, :] + blk[ST_LB:ST_LB + 1, :], 0.0))
                intra_f = jnp.sum(blk[ST_IFA + h:ST_IFA + h + 1, :])
                intra_b = jnp.sum(blk[ST_IBA + h:ST_IBA + h + 1, :])
                head = 2 * pr + h
                misc = jnp.where(lane == MISC_DF + head, cross_f + intra_f, misc)
                misc = jnp.where(lane == MISC_DB + head, cross_b + intra_b, misc)
        for g in range(KV_HEADS):
            tot = dsink_ref[g * 8:(g + 1) * 8, :]
            for b in range(1, b_loc):
                tot = tot + dsink_ref[(b * KV_HEADS + g) * 8:(b * KV_HEADS + g + 1) * 8, :]
            for h in range(GROUP):
                misc = jnp.where(lane == MISC_SINK + GROUP * g + h, jnp.sum(tot[h:h + 1, 0:1]), misc)
        out_ref[ROW_MISC:ROW_MISC + 1, :] = misc

    return pl.pallas_call(body, name="pack_small", out_shape=SDS((SMALL_ROWS, d), F32))(acc2, acc1, ret_stats, dsink)


BIG = ("w_in", "w_out", "w_ffn_gate", "w_ffn_up", "w_ffn_down", "w_ple_proj", "w_ple_gate")
TRANSPOSED_OUTSIDE = ("w_in", "w_ffn_gate", "w_ffn_up")
TRANSPOSED_HERE = ("w_ple_proj",)
SMALL = ("ret_decay_fwd", "ret_decay_bwd", "ret_gn_gain", "attn_sink", "ln1_gain", "ln1_bias", "ln2_gain", "ln2_bias")
ORDER = ("w_in", "ret_decay_fwd", "ret_decay_bwd", "ret_gn_gain", "attn_sink", "w_out", "ln1_gain", "ln1_bias",
         "w_ffn_gate", "w_ffn_up", "w_ffn_down", "w_ple_proj", "w_ple_gate", "ln2_gain", "ln2_bias")


GATHER_ORDER = ("w_in", "w_ffn_up", "w_out", "w_ffn_gate", "w_ple_gate", "w_ple_proj", "w_ffn_down")
GATHER_TWO_LEVEL = ("w_in", "w_ffn_up", "w_out")

def _local_step(x2, p2, target2, fetch, publish, small, b_loc, me):
    d = x2.shape[1]
    lgf, lgb = _log_decay(small["ret_decay_fwd"], small["ret_decay_bwd"])
    lgf1, lgb1, sink1 = lgf.reshape(-1), lgb.reshape(-1), small["attn_sink"].reshape(-1)
    (w_in,) = fetch(("w_in",), ())
    u, xb = _in_proj(x2, w_in)
    passed = fetch.pass_on(("w_ffn_up", "w_out"), (xb, lgf))
    r, ret_xhat, ret_rstd, a, ret_states = _mixers_fwd(u, lgf1, lgb1, small["ret_gn_gain"], sink1, b_loc, passed)
    w_out, w_gate, w_up = fetch(("w_out", "w_ffn_gate", "w_ffn_up"), (r, a))
    xhat1, rstd1, h1b, dact_dg, dact_du, act = _mix_ln1_ffn_up(
        r, a, x2, w_out, w_gate, w_up, small["ln1_gain"], small["ln1_bias"])
    w_pg, w_pe, w_down = fetch(("w_ple_gate", "w_ple_proj", "w_ffn_down"), (act,))
    dz2, dz2b, dsb, dpleb, dg, dup, acc2 = _ffn_down_ln2_loss(
        act, dact_dg, dact_du, h1b, p2, xhat1, target2, w_down, w_pg, w_pe,
        small["ln1_gain"], small["ln1_bias"], small["ln2_gain"], small["ln2_bias"])
    own = {}

    def grad(name, parts, rhs, after=()):
        whole, own[name] = _weight_grad("grad_" + name, me, parts, rhs, after)
        return whole

    ffn_jobs = dict(w_ffn_down=(act, dz2b), w_ple_proj=(dpleb, p2), w_ple_gate=(h1b, dsb),
                    w_ffn_gate=(dg, h1b), w_ffn_up=(dup, h1b))
    wholes, owns = _weight_grad_jobs("grad_w_ffn", me, list(ffn_jobs.values()))
    own.update(zip(ffn_jobs, owns))
    t2 = publish("ffn", dict(zip(ffn_jobs, wholes)))
    dz1, dz1b, dr, da, acc1 = _dh1_ln1_bwd(
        dz2, dg, dup, dsb, xhat1, rstd1, w_gate, w_up, w_pg, w_out, small["ln1_gain"], t2)
    t3 = publish("out", dict(w_out=grad("w_out", [r, a], dz1b)))
    dq, dk, dv, dgate, ret_stats, daq, dak, dav, dsink = _mixers_bwd(
        u, ret_xhat, ret_rstd, ret_states, dr, da, lgf1, lgb1, small["ret_gn_gain"], sink1, b_loc, t3)
    parts = [dq, dk, dv, dgate, daq, dak, dav]
    small_part = _pack_small(acc2, acc1, ret_stats, dsink, b_loc, d)
    t4 = publish("in", dict(w_in=grad("w_in", parts, xb)), small_part)
    grad_x = _in_proj_bwd(dz1, parts, w_in, t4)
    return grad_x, own, small_part


def kernel(x, p, w_in, ret_decay_fwd, ret_decay_bwd, ret_gn_gain, attn_sink, w_out, ln1_gain, ln1_bias, w_ffn_gate, w_ffn_up, w_ffn_down, w_ple_proj, w_ple_gate, ln2_gain, ln2_bias, loss_target, m_w_in, m_ret_decay_fwd, m_ret_decay_bwd, m_ret_gn_gain, m_attn_sink, m_w_out, m_ln1_gain, m_ln1_bias, m_w_ffn_gate, m_w_ffn_up, m_w_ffn_down, m_w_ple_proj, m_w_ple_gate, m_ln2_gain, m_ln2_bias, v_w_in, v_ret_decay_fwd, v_ret_decay_bwd, v_ret_gn_gain, v_attn_sink, v_w_out, v_ln1_gain, v_ln1_bias, v_w_ffn_gate, v_w_ffn_up, v_w_ffn_down, v_w_ple_proj, v_w_ple_gate, v_ln2_gain, v_ln2_bias):
    given = dict(locals())

    def strip(n, a):
        if n not in BIG:
            return a
        return a[0].T if n in TRANSPOSED_OUTSIDE else a[0]

    def restore(n, a):
        if n not in BIG:
            return a
        return (a.T if n in TRANSPOSED_OUTSIDE else a)[None]

    w = {n: strip(n, given[n]) for n in ORDER}
    m = {n: strip(n, given["m_" + n]) for n in ORDER}
    v = {n: strip(n, given["v_" + n]) for n in ORDER}
    b_loc, s, d = x.shape
    x2 = x.reshape(b_loc * s, d)
    p2 = p[0].reshape(b_loc * s, p.shape[-1])
    target2 = loss_target.reshape(b_loc * s, d)

    small = {n: w[n] for n in SMALL}
    me = (4 * lax.axis_index("x") + 2 * lax.axis_index("y") + lax.axis_index("c")).astype(jnp.int32).reshape(1)

    gather = _gather_start(
        {n: w[n] for n in GATHER_ORDER},
        [_gather_copy_near if n in GATHER_TWO_LEVEL else _gather_copy for n in GATHER_ORDER])

    passing = {}

    def pass_on(names, after):
        relayed = _gather_relay("gather_relay_" + names[0], gather, [GATHER_ORDER.index(n) for n in names], list(after))
        passing.update({n: (relayed, j) for j, n in enumerate(names)})
        return (relayed["token"],)

    def fetch(names, after):
        out = {}
        for n in [n for n in names if n in GATHER_TWO_LEVEL]:
            if n not in passing:
                pass_on((n,), after)
            relayed, j = passing[n]
            out[n] = _split_copy_wait("gather_wait_" + n, relayed, [j], list(after))[0][0]
        direct = [n for n in names if n not in GATHER_TWO_LEVEL]
        if direct:
            got = _split_copy_wait("gather_wait_" + direct[0], gather, [GATHER_ORDER.index(n) for n in direct],
                                   list(after))
            out.update({n: item[0] for n, item in zip(direct, got)})
        return [out[n] for n in names]

    scatters = []

    def publish(tag, products, small_sums=None):
        items = [(products[n], lax.empty((N_DEV - 1, products[n].shape[0] // N_DEV, products[n].shape[1]), BF16))
                 for n in products]
        copies = [_scatter_copy] * len(items)
        if small_sums is not None:
            items.append((small_sums, lax.empty((N_DEV - 1,) + small_sums.shape, F32)))
            copies.append(_small_copy)
        started = _split_copy_start("scatter_start_" + tag, items, copies)
        scatters.append((list(products), small_sums is not None, started))
        return (started["token"],)

    fetch.pass_on = pass_on
    grad_x, own, small_part = _local_step(x2, p2, target2, fetch, publish, small, b_loc, me)

    out_g, out_d, out_m, out_v = {}, {}, {}, {}
    after = [grad_x]
    for names, with_small, started in scatters:
        landed = _split_copy_wait("scatter_wait_" + names[0], started, list(range(len(started["items"]))), after)
        if with_small:
            mine, from_peers = landed[-1]
            loss, sg, sd, sm, sv = _small_adamw(
                me, mine, from_peers, small, {n: m[n] for n in SMALL}, {n: v[n] for n in SMALL})
            for dst, src in ((out_g, sg), (out_d, sd), (out_m, sm), (out_v, sv)):
                dst.update(src)
        recv = {n: item[1] for n, item in zip(names, landed)}
        alike = {}
        for n in names:
            alike.setdefault((own[n].shape, n in TRANSPOSED_HERE), []).append(n)
        for (_, transposed), ns in alike.items():
            res = _reduce_adamw(ns[0], [own[n] for n in ns], [recv[n] for n in ns], [w[n] for n in ns],
                                [m[n] for n in ns], [v[n] for n in ns], transposed)
            for dst, vals in zip((out_g, out_d, out_m, out_v), res):
                dst.update(zip(ns, vals))
        after = [out_v[names[-1]]]

    outs = [loss[0, 0], grad_x.reshape(x.shape)]
    for group in (out_g, out_d, out_m, out_v):
        outs += [restore(n, group[n]) for n in ORDER]
    return tuple(outs)
```

```python
import functools

import jax
import jax.numpy as jnp
from jax import lax
from jax.experimental import pallas as pl
from jax.experimental.pallas import tpu as pltpu

F32, BF16 = jnp.float32, jnp.bfloat16
SDS = jax.ShapeDtypeStruct
MESH = pl.DeviceIdType.MESH

N_DEV = 8
HEAD_DIM = 64
RET_HEADS = 8
ATTN_HEADS = 8
KV_HEADS = 2
GROUP = ATTN_HEADS // KV_HEADS
RET_W = RET_HEADS * HEAD_DIM
ATT_W = ATTN_HEADS * HEAD_DIM
KV_W = KV_HEADS * HEAD_DIM
LANES = 128
CHUNK = 128
BLOCK = 128
Q_SCALE = HEAD_DIM ** -0.5
ALPHA = 2.0 ** 0.25
LN_EPS = 1e-5
GN_EPS = 1e-5
NEG_INF = -1e30
C_RQ, C_RK, C_RV, C_RG = 0, RET_W, 2 * RET_W, 3 * RET_W
C_AQ = 4 * RET_W
C_AK = C_AQ + ATT_W
C_AV = C_AK + KV_W
IN_W = C_AV + KV_W

ADAM_LR = 0.001
ADAM_B1 = 0.9
ADAM_B2 = 0.999
ADAM_EPS = 1e-08
ADAM_WD = 0.01
ADAM_STEP = 10

VMEM_LIMIT = 56 * 1024 * 1024
MATMUL_ROWS = 512
EPILOGUE_ROWS = 256
SUB_ROWS = 512
SMALL_ROWS = 16
ROW_LN1G, ROW_LN1B, ROW_LN2G, ROW_LN2B, ROW_LOSS, ROW_GN, ROW_MISC = 0, 1, 2, 3, 4, 5, 6
MISC_DF, MISC_DB, MISC_SINK = 0, 8, 16


def _dot_nn(a, b):
    return lax.dot_general(a, b, (((1,), (0,)), ((), ())), preferred_element_type=F32)


def _dot_nt(a, b):
    return lax.dot_general(a, b, (((1,), (1,)), ((), ())), preferred_element_type=F32)


def _dot_tn(a, b):
    return lax.dot_general(a, b, (((0,), (0,)), ((), ())), preferred_element_type=F32)


def _params(sem=None, vmem=VMEM_LIMIT):
    kw = {"vmem_limit_bytes": vmem}
    if sem is not None:
        kw["dimension_semantics"] = sem
    return pltpu.CompilerParams(**kw)


def _row_tile(t, want=512):
    tm = want
    while t % tm:
        tm //= 2
    return tm


def _sigmoid(x):
    return jax.nn.sigmoid(x)


def _layer_norm_stats(z):
    mu = jnp.mean(z, axis=1, keepdims=True)
    d = z - mu
    var = jnp.mean(d * d, axis=1, keepdims=True)
    rstd = lax.rsqrt(var + LN_EPS)
    return d * rstd, rstd


def _layer_norm_bwd(dxh, xhat, rstd):
    m1 = jnp.mean(dxh, axis=1, keepdims=True)
    m2 = jnp.mean(dxh * xhat, axis=1, keepdims=True)
    return rstd * (dxh - m1 - xhat * m2)


def _mesh_pos():
    return lax.axis_index("x"), lax.axis_index("y"), lax.axis_index("c")


HBM_SPEC = pl.BlockSpec(memory_space=pltpu.HBM)
SEM_SPEC = pl.BlockSpec(memory_space=pltpu.SEMAPHORE)
ANY_SPEC = pl.BlockSpec(memory_space=pl.ANY)
SIDE_EFFECT = pltpu.SideEffectType.DATAFLOW_SIDE_EFFECTING
PEER_SEMS = pltpu.SemaphoreType.DMA((N_DEV - 1,))


def _in_hbm(a):
    return pltpu.with_memory_space_constraint(a, pltpu.HBM)


def _split_copy_start(name, items, copies):
    n = len(items)
    flat = [a for it in items for a in it]
    k = len(flat)

    def body(*refs):
        arr, sems = list(refs[:k]), refs[k:k + 2 * n]
        for i, it in enumerate(items):
            mine = [arr.pop(0) for _ in it]
            for m in range(1, N_DEV):
                cp = copies[i](m, mine, sems[i].at[m - 1], sems[n + i].at[m - 1])
                if cp is not None:
                    cp.start()
        token = refs[-1]
        token[...] = jnp.zeros_like(token)

    res = pl.pallas_call(
        body, name=name,
        out_shape=[PEER_SEMS] * (2 * n) + [pltpu.HBM(a.shape, a.dtype) for a in flat] + [SDS((8, LANES), F32)],
        in_specs=[HBM_SPEC] * k,
        out_specs=[SEM_SPEC] * (2 * n) + [HBM_SPEC] * k + [pl.BlockSpec(memory_space=pltpu.VMEM)],
        input_output_aliases={j: 2 * n + j for j in range(k)},
        compiler_params=pltpu.CompilerParams(has_side_effects=SIDE_EFFECT),
    )(*[_in_hbm(a) for a in flat])
    thru, out_items = list(res[2 * n:2 * n + k]), []
    for it in items:
        out_items.append(tuple(thru.pop(0) for _ in it))
    return dict(send=res[:n], recv=res[n:2 * n], items=out_items, token=res[-1], copies=copies)


def _gather_start(shards, copies):
    names = list(shards)
    n = len(names)
    flip = [name in TRANSPOSED_HERE for name in names]
    shapes = [shards[name].shape[::-1] if f else shards[name].shape for name, f in zip(names, flip)]
    most = (max(s[0] for s in shapes), max(s[1] for s in shapes))

    def body(*refs):
        src, sems, land, token = refs[:n], refs[n:3 * n], refs[3 * n:4 * n], refs[4 * n]
        wide, narrow, sem = refs[4 * n + 1:]
        for i, (rows, cols) in enumerate(shapes):
            raw = wide.at[0:cols, 0:rows] if flip[i] else wide.at[0:rows, 0:cols]
            bring = pltpu.make_async_copy(src[i], raw, sem.at[0])
            bring.start()
            bring.wait()
            narrow[0:rows, 0:cols] = (raw[...].T if flip[i] else raw[...]).astype(BF16)
            mine = land[i].at[pl.ds(pl.multiple_of(_peer_index(0) * rows, 8), rows), :]
            place = pltpu.make_async_copy(narrow.at[0:rows, 0:cols], mine, sem.at[0])
            place.start()
            place.wait()
            for m in range(1, N_DEV):
                cp = copies[i](m, [land[i]], sems[i].at[m - 1], sems[n + i].at[m - 1])
                if cp is not None:
                    cp.start()
        token[...] = jnp.zeros_like(token)

    side = max(most)
    res = pl.pallas_call(
        body, name="gather_start",
        out_shape=[PEER_SEMS] * (2 * n) + [pltpu.HBM((N_DEV * r, c), BF16) for r, c in shapes] + [SDS((8, LANES), F32)],
        in_specs=[HBM_SPEC] * n,
        out_specs=[SEM_SPEC] * (2 * n) + [HBM_SPEC] * n + [pl.BlockSpec(memory_space=pltpu.VMEM)],
        scratch_shapes=[pltpu.VMEM((side, side), F32), pltpu.VMEM(most, BF16), pltpu.SemaphoreType.DMA((1,))],
        compiler_params=pltpu.CompilerParams(has_side_effects=SIDE_EFFECT),
    )(*[_in_hbm(shards[name]) for name in names])
    return dict(send=res[:n], recv=res[n:2 * n], items=[(a,) for a in res[2 * n:3 * n]], token=res[-1], copies=copies)


def _gather_relay(name, started, which, after):
    lands = [started["items"][w][0] for w in which]
    k = len(lands)

    def body(*refs):
        land_refs, old_send, old_recv = refs[:k], refs[k:2 * k], refs[2 * k:3 * k]
        outs = refs[3 * k + len(after):]
        send, recv, token = outs[:k], outs[k:2 * k], outs[-1]
        for j in range(k):
            for m in range(1, N_DEV):
                cp = _gather_copy_near(m, [land_refs[j]], old_send[j].at[m - 1], old_recv[j].at[m - 1])
                if cp is None:
                    continue
                cp.wait_send()
                cp.wait_recv()
                if m > 1:
                    _gather_copy_pass(m + 1, [land_refs[j]], send[j].at[m], recv[j].at[m]).start()
        token[...] = jnp.zeros_like(token)

    res = pl.pallas_call(
        body, name=name,
        out_shape=[PEER_SEMS] * (2 * k) + [pltpu.HBM(a.shape, a.dtype) for a in lands] + [SDS((8, LANES), F32)],
        in_specs=[HBM_SPEC] * k + [SEM_SPEC] * (2 * k) + [ANY_SPEC] * len(after),
        out_specs=[SEM_SPEC] * (2 * k) + [HBM_SPEC] * k + [pl.BlockSpec(memory_space=pltpu.VMEM)],
        input_output_aliases={j: 2 * k + j for j in range(k)},
        compiler_params=pltpu.CompilerParams(has_side_effects=SIDE_EFFECT),
    )(*lands, *[started["send"][w] for w in which], *[started["recv"][w] for w in which],
      *[_in_hbm(a) for a in after])
    return dict(send=res[:k], recv=res[k:2 * k], items=[(a,) for a in res[2 * k:3 * k]], token=res[-1],
                copies=[_gather_copy_pass] * k)


def _split_copy_wait(name, started, which, after):
    items = [started["items"][i] for i in which]
    copies = [started["copies"][i] for i in which]
    n = len(items)
    flat = [a for it in items for a in it]
    k = len(flat)

    def body(*refs):
        arr, sems = list(refs[:k]), refs[k:k + 2 * n]
        for i, it in enumerate(items):
            mine = [arr.pop(0) for _ in it]
            for m in range(1, N_DEV):
                cp = copies[i](m, mine, sems[i].at[m - 1], sems[n + i].at[m - 1])
                if cp is not None:
                    cp.wait_send()
                    cp.wait_recv()

    res = pl.pallas_call(
        body, name=name,
        out_shape=[pltpu.HBM(a.shape, a.dtype) for a in flat],
        in_specs=[HBM_SPEC] * k + [SEM_SPEC] * (2 * n) + [ANY_SPEC] * len(after),
        out_specs=[HBM_SPEC] * k,
        input_output_aliases={j: j for j in range(k)},
        compiler_params=pltpu.CompilerParams(has_side_effects=SIDE_EFFECT),
    )(*flat, *[started["send"][i] for i in which], *[started["recv"][i] for i in which], *[_in_hbm(a) for a in after])
    thru, out_items = list(res), []
    for it in items:
        out_items.append(tuple(thru.pop(0) for _ in it))
    return out_items


def _gather_copy(m, refs, send_sem, recv_sem):
    (land_ref,) = refs
    r = land_ref.shape[0] // N_DEV
    mine = land_ref.at[pl.ds(pl.multiple_of(_peer_index(0) * r, 8), r), :]
    return pltpu.make_async_remote_copy(src_ref=mine, dst_ref=mine, send_sem=send_sem, recv_sem=recv_sem,
                                        device_id=_peer(m), device_id_type=MESH)


def _gather_copy_near(m, refs, send_sem, recv_sem):
    return _gather_copy(m, refs, send_sem, recv_sem) if m == 1 or m % 2 == 0 else None


def _gather_copy_pass(m, refs, send_sem, recv_sem):
    if m == 1 or m % 2 == 0:
        return None
    (land_ref,) = refs
    r = land_ref.shape[0] // N_DEV
    block = land_ref.at[pl.ds(pl.multiple_of(_peer_index(m ^ 1) * r, 8), r), :]
    return pltpu.make_async_remote_copy(src_ref=block, dst_ref=block, send_sem=send_sem, recv_sem=recv_sem,
                                        device_id=_peer(1), device_id_type=MESH)


def _small_copy(m, refs, send_sem, recv_sem):
    part_ref, land_ref = refs
    return pltpu.make_async_remote_copy(src_ref=part_ref, dst_ref=land_ref.at[m - 1], send_sem=send_sem,
                                        recv_sem=recv_sem, device_id=_peer(m), device_id_type=MESH)


def _scatter_copy(m, refs, send_sem, recv_sem):
    buf_ref, land_ref = refs
    r = buf_ref.shape[0] // N_DEV
    src = buf_ref.at[pl.ds(pl.multiple_of(_peer_index(m) * r, 8), r), :]
    return pltpu.make_async_remote_copy(src_ref=src, dst_ref=land_ref.at[m - 1], send_sem=send_sem,
                                        recv_sem=recv_sem, device_id=_peer(m), device_id_type=MESH)


def _peer(m):
    x, y, c = _mesh_pos()
    bx, by, bc = (m >> 2) & 1, (m >> 1) & 1, m & 1
    return (x ^ bx if bx else x, y ^ by if by else y, c ^ bc if bc else c)


def _peer_index(m):
    x, y, c = _mesh_pos()
    return (4 * x + 2 * y + c) ^ m


SMALL_PLACE = {
    "ln1_gain": (ROW_LN1G, 0), "ln1_bias": (ROW_LN1B, 0), "ln2_gain": (ROW_LN2G, 0), "ln2_bias": (ROW_LN2B, 0),
    "ret_gn_gain": (ROW_GN, 0), "ret_decay_fwd": (ROW_MISC, MISC_DF), "ret_decay_bwd": (ROW_MISC, MISC_DB),
    "attn_sink": (ROW_MISC, MISC_SINK)}


def _small_adamw(me, part, landed, w, m, v):
    d = part.shape[1]
    names = list(SMALL_PLACE)
    k = len(names)

    def body(*refs):
        me_ref, part_ref, land_ref = refs[:3]
        refs = refs[2:]
        w_refs, m_refs, v_refs = refs[1:1 + k], refs[1 + k:1 + 2 * k], refs[1 + 2 * k:1 + 3 * k]
        outs = refs[1 + 3 * k:1 + 7 * k + 1]
        tot_ref = refs[-1]
        loss_ref, g_refs, dl_refs = outs[0], outs[1:1 + k], outs[1 + k:1 + 2 * k]
        nm_refs, nv_refs = outs[1 + 2 * k:1 + 3 * k], outs[1 + 3 * k:1 + 4 * k]
        tot = jnp.zeros(part_ref.shape, F32)
        for dev in range(N_DEV):
            j = dev ^ me_ref[0]
            tot = tot + jnp.where(j == 0, part_ref[...], land_ref[jnp.maximum(j, 1) - 1])
        tot_ref[...] = tot
        loss_ref[...] = (0.5 / d) * jnp.sum(tot_ref[ROW_LOSS:ROW_LOSS + 1, :], axis=1, keepdims=True)
        for i, name in enumerate(names):
            row, lo = SMALL_PLACE[name]
            wv = w_refs[i][...]
            g = tot_ref[row:row + 1, lo:lo + wv.shape[1]]
            if name.startswith("ret_decay"):
                p2 = jnp.exp2(wv)
                g = g * (-p2 * jnp.log(2.0) / (1.0 - p2))
            g_refs[i][...] = g
            _adamw_store(g, wv, m_refs[i][...], v_refs[i][...], dl_refs[i], nm_refs[i], nv_refs[i])

    shapes = [SDS(w[n].shape, F32) for n in names]
    vm = pl.BlockSpec(memory_space=pltpu.VMEM)
    res = pl.pallas_call(
        body, name="small_adamw", out_shape=[SDS((1, 1), F32)] + shapes * 4,
        in_specs=[_smem_spec()] + [vm] * (2 + 3 * k), out_specs=[vm] * (1 + 4 * k),
        scratch_shapes=[pltpu.VMEM(part.shape, F32)],
    )(me, part, landed, *[w[n] for n in names], *[m[n] for n in names], *[v[n] for n in names])
    groups = [dict(zip(names, res[1 + j * k:1 + (j + 1) * k])) for j in range(4)]
    return (res[0], *groups)


def _adamw_store(g, w, m, v, dl_ref, nm_ref, nv_ref):
    m = ADAM_B1 * m + (1.0 - ADAM_B1) * g
    v = ADAM_B2 * v + (1.0 - ADAM_B2) * (g * g)
    m_hat = m / (1.0 - ADAM_B1 ** ADAM_STEP)
    v_hat = v / (1.0 - ADAM_B2 ** ADAM_STEP)
    dl_ref[...] = -ADAM_LR * (m_hat / (jnp.sqrt(v_hat) + ADAM_EPS) + ADAM_WD * w)
    nm_ref[...] = m
    nv_ref[...] = v


def _reduce_adamw(name, owns, recvs, ws, ms, vs, transposed):
    count = len(owns)
    rows, n = owns[0].shape
    steps = 1 if transposed or rows % 16 else 2
    rb = rows // steps

    def body(*refs):
        ins, outs = refs[:5 * count], refs[5 * count:]
        j = pl.program_id(0)
        for k in range(count):
            @pl.when(j == k)
            def _(k=k):
                own_ref, recv_ref, w_ref, m_ref, v_ref = ins[5 * k:5 * k + 5]
                g_ref, dl_ref, nm_ref, nv_ref = outs[4 * k:4 * k + 4]
                g = own_ref[...]
                for p in range(recv_ref.shape[0]):
                    g = g + recv_ref[p].astype(F32)
                if transposed:
                    g = g.T
                g_ref[...] = g
                _adamw_store(g, w_ref[...], m_ref[...], v_ref[...], dl_ref, nm_ref, nv_ref)

    def turn(k):
        return lambda j, i: jnp.where(j == k, i, jnp.where(j < k, 0, steps - 1))

    in_specs, out_specs = [], []
    for k in range(count):
        at = turn(k)
        blk = pl.BlockSpec(ws[0].shape if transposed else (rb, n), lambda j, i, at=at: (at(j, i), 0))
        in_specs += [pl.BlockSpec((rb, n), lambda j, i, at=at: (at(j, i), 0)),
                     pl.BlockSpec((recvs[k].shape[0], rb, n), lambda j, i, at=at: (0, at(j, i), 0)), blk, blk, blk]
        out_specs += [blk] * 4
    res = pl.pallas_call(
        body, name="adamw_" + name, grid=(count, steps), in_specs=in_specs, out_specs=out_specs,
        out_shape=[SDS(ws[0].shape, F32)] * (4 * count), compiler_params=_params(("arbitrary", "arbitrary")),
    )(*[a for k in range(count) for a in (owns[k], recvs[k], ws[k], ms[k], vs[k])])
    return [list(res[j::4]) for j in range(4)]


def _row_spec(tm, width):
    return pl.BlockSpec((tm, width), lambda i: (i, 0))


def _full_spec(shape):
    return pl.BlockSpec(shape, lambda i: (0,) * len(shape))


_acc_spec = _full_spec


def _sub_rows(tm):
    step = min(SUB_ROWS, tm)
    return [(lo, lo + step) for lo in range(0, tm, step)]


def _in_proj(x2, wt_in):
    t, d = x2.shape
    u_w = wt_in.shape[0]
    tm = _row_tile(t, MATMUL_ROWS)

    def body(x_ref, w_ref, u_ref, xb_ref):
        xb = x_ref[...].astype(BF16)
        xb_ref[...] = xb
        u_ref[...] = _dot_nt(xb, w_ref[...]).astype(BF16)

    return pl.pallas_call(
        body, name="in_proj", grid=(t // tm,),
        in_specs=[_row_spec(tm, d), _full_spec(wt_in.shape)],
        out_specs=[_row_spec(tm, u_w), _row_spec(tm, d)],
        out_shape=[SDS((t, u_w), BF16), SDS((t, d), BF16)],
        compiler_params=_params(("parallel",)),
    )(x2, wt_in)


def _col_halves(f):
    n = f // LANES
    k = (n + 1) // 2 * LANES
    return [(0, k), (k, f)] if k < f else [(0, f)]


def _mix_ln1_ffn_up(r, a, x2, w_out, wt_gate, wt_up, g1, b1):
    t, d = x2.shape
    f = wt_gate.shape[0]
    tm = _row_tile(t, EPILOGUE_ROWS)

    def body(r_ref, a_ref, x_ref, wo_ref, wg_ref, wu_ref, g_ref, b_ref, xh_ref, rs_ref, hb_ref, dg_ref, du_ref,
             act_ref):
        mix = _dot_nn(r_ref[...], wo_ref[0:RET_W, :]) + _dot_nn(a_ref[...], wo_ref[RET_W:RET_W + ATT_W, :])
        z = ALPHA * x_ref[...] + mix
        xhat, rstd = _layer_norm_stats(z)
        xh_ref[...] = xhat
        rs_ref[...] = jnp.broadcast_to(rstd, rs_ref.shape)
        h = (xhat * g_ref[...] + b_ref[...]).astype(BF16)
        hb_ref[...] = h
        g = _dot_nt(h, wg_ref[...])
        u = _dot_nt(h, wu_ref[...])
        sg = _sigmoid(g)
        silu = g * sg
        dg_ref[...] = (u * (sg * (1.0 + g * (1.0 - sg)))).astype(BF16)
        du_ref[...] = silu.astype(BF16)
        act_ref[...] = (silu * u).astype(BF16)

    wide, narrow = _row_spec(tm, f), _row_spec(tm, d)
    return pl.pallas_call(
        body, name="mix_ln1_ffn_up", grid=(t // tm,),
        in_specs=[_row_spec(tm, RET_W), _row_spec(tm, ATT_W), narrow, _resident_spec(w_out.shape),
                  _resident_spec(wt_gate.shape), _resident_spec(wt_up.shape), _full_spec(g1.shape),
                  _full_spec(b1.shape)],
        out_specs=[narrow, _row_spec(tm, LANES), narrow, wide, wide, wide],
        out_shape=[SDS((t, d), F32), SDS((t, LANES), F32), SDS((t, d), BF16)] + [SDS((t, f), BF16)] * 3,
        compiler_params=_params(("parallel",)),
    )(r, a, x2, w_out, wt_gate, wt_up, g1, b1)


def _ffn_down_ln2_loss(act, dact_dg, dact_du, h1b, p2, xhat1, target, w_down, w_pg, wt_pe, g1, b1, g2, b2):
    t, d = xhat1.shape
    f = act.shape[1]
    pdim = p2.shape[1]
    tm = _row_tile(t, EPILOGUE_ROWS)

    def body(act_ref, fg_ref, fu_ref, hb_ref, p_ref, xh1_ref, tgt_ref, wd_ref, wpg_ref, wpe_ref, g1_ref, b1_ref,
             g2_ref, b2_ref, dz_ref, dzb_ref, ds_ref, dple_ref, dg_ref, du_ref, acc_ref):
        @pl.when(pl.program_id(0) == 0)
        def _():
            acc_ref[...] = jnp.zeros_like(acc_ref)

        for lo, hi in _sub_rows(tm):
            h1 = xh1_ref[lo:hi, :] * g1_ref[...] + b1_ref[...]
            pg = _sigmoid(_dot_nn(hb_ref[lo:hi, :], wpg_ref[...]))
            ple = _dot_nt(p_ref[lo:hi, :].astype(BF16), wpe_ref[...])
            gated = pg * ple
            dgate = gated * (1.0 - pg)
            ffn = _dot_nn(act_ref[lo:hi, :], wd_ref[...])
            z2 = ALPHA * h1 + gated + ffn
            xhat2, rstd2 = _layer_norm_stats(z2)
            err = xhat2 * g2_ref[...] + b2_ref[...] - tgt_ref[lo:hi, :]
            dy = err * (1.0 / d)
            dz = _layer_norm_bwd(dy * g2_ref[...], xhat2, rstd2)
            dzb = dz.astype(BF16)
            dz_ref[lo:hi, :] = dz
            dzb_ref[lo:hi, :] = dzb
            ds_ref[lo:hi, :] = (dz * dgate).astype(BF16)
            dple_ref[lo:hi, :] = (dz * pg).astype(BF16)
            acc_ref[0:1, :] += jnp.sum(err * err, axis=0, keepdims=True)
            acc_ref[1:2, :] += jnp.sum(dy * xhat2, axis=0, keepdims=True)
            acc_ref[2:3, :] += jnp.sum(dy, axis=0, keepdims=True)
            for c0, c1 in _col_halves(f):
                da = _dot_nt(dzb, wd_ref[c0:c1, :])
                dg_ref[lo:hi, c0:c1] = (da * fg_ref[lo:hi, c0:c1].astype(F32)).astype(BF16)
                du_ref[lo:hi, c0:c1] = (da * fu_ref[lo:hi, c0:c1].astype(F32)).astype(BF16)

    vec = _full_spec(g1.shape)
    wide, narrow = _row_spec(tm, f), _row_spec(tm, d)
    return pl.pallas_call(
        body, name="ffn_down_ln2_loss", grid=(t // tm,),
        in_specs=[wide, wide, wide, narrow, _row_spec(tm, pdim), narrow, narrow,
                  _full_spec(w_down.shape), _full_spec(w_pg.shape), _full_spec(wt_pe.shape), vec, vec, vec, vec],
        out_specs=[narrow] * 4 + [wide, wide, _acc_spec((8, d))],
        out_shape=[SDS((t, d), F32), SDS((t, d), BF16), SDS((t, d), BF16), SDS((t, d), BF16),
                   SDS((t, f), BF16), SDS((t, f), BF16), SDS((8, d), F32)],
        compiler_params=_params(("arbitrary",)),
    )(act, dact_dg, dact_du, h1b, p2, xhat1, target, w_down, w_pg, wt_pe, g1, b1, g2, b2)


def _after(after, body):
    k = len(after)
    return (lambda *refs: body(*refs[k:])), [ANY_SPEC] * k


def _resident_spec(shape):
    return pl.BlockSpec(shape, lambda i: (0,) * len(shape), pipeline_mode=pl.Buffered(1))


def _dh1_ln1_bwd(dz2, dg, dup, dsb, xhat1, rstd1, wt_gate, wt_up, w_pg, w_out, g1, after=()):
    t, d = dz2.shape
    f = dg.shape[1]
    tm = _row_tile(t, MATMUL_ROWS)

    def body(dz_ref, dg_ref, du_ref, ds_ref, xh1_ref, rs1_ref, wg_ref, wu_ref, wpg_ref, wo_ref, g1_ref,
             dz1_ref, dz1b_ref, dr_ref, da_ref, acc_ref):
        @pl.when(pl.program_id(0) == 0)
        def _():
            acc_ref[...] = jnp.zeros_like(acc_ref)

        for lo, hi in _sub_rows(tm):
            dh = (ALPHA * dz_ref[lo:hi, :] + _dot_nn(dg_ref[lo:hi, :], wg_ref[...])
                  + _dot_nn(du_ref[lo:hi, :], wu_ref[...]) + _dot_nt(ds_ref[lo:hi, :], wpg_ref[...]))
            xhat, rstd = xh1_ref[lo:hi, :], rs1_ref[lo:hi, 0:1]
            dz1 = _layer_norm_bwd(dh * g1_ref[...], xhat, rstd)
            dz1b = dz1.astype(BF16)
            dz1_ref[lo:hi, :] = dz1
            dz1b_ref[lo:hi, :] = dz1b
            acc_ref[0:1, :] += jnp.sum(dh * xhat, axis=0, keepdims=True)
            acc_ref[1:2, :] += jnp.sum(dh, axis=0, keepdims=True)
            dr_ref[lo:hi, :] = _dot_nt(dz1b, wo_ref[0:RET_W, :]).astype(BF16)
            da_ref[lo:hi, :] = _dot_nt(dz1b, wo_ref[RET_W:RET_W + ATT_W, :]).astype(BF16)

    body, lead = _after(after, body)
    return pl.pallas_call(
        body, name="dh1_ln1_bwd", grid=(t // tm,),
        in_specs=lead + [_row_spec(tm, d), _row_spec(tm, f), _row_spec(tm, f), _row_spec(tm, d), _row_spec(tm, d),
                         _row_spec(tm, LANES), _resident_spec(wt_gate.shape), _resident_spec(wt_up.shape), _resident_spec(w_pg.shape),
                         _resident_spec(w_out.shape), _full_spec(g1.shape)],
        out_specs=[_row_spec(tm, d), _row_spec(tm, d), _row_spec(tm, RET_W), _row_spec(tm, ATT_W), _acc_spec((8, d))],
        out_shape=[SDS((t, d), F32), SDS((t, d), BF16), SDS((t, RET_W), BF16), SDS((t, ATT_W), BF16),
                   SDS((8, d), F32)],
        compiler_params=_params(("arbitrary",)),
    )(*after, dz2, dg, dup, dsb, xhat1, rstd1, wt_gate, wt_up, w_pg, w_out, g1)


def _in_proj_bwd(dz1, parts, wt_in, after=()):
    t, d = dz1.shape
    tm = _row_tile(t, MATMUL_ROWS)
    widths = [p.shape[1] for p in parts]

    def body(*refs):
        dz_ref, part_refs, w_ref, dx_ref = refs[0], refs[1:1 + len(parts)], refs[-2], refs[-1]
        acc = ALPHA * dz_ref[...]
        lo = 0
        for p_ref, w in zip(part_refs, widths):
            acc = acc + _dot_nn(p_ref[...], w_ref[lo:lo + w, :])
            lo += w
        dx_ref[...] = acc

    body, lead = _after(after, body)
    return pl.pallas_call(
        body, name="in_proj_bwd", grid=(t // tm,),
        in_specs=lead + [_row_spec(tm, d)] + [_row_spec(tm, w) for w in widths] + [_full_spec(wt_in.shape)],
        out_specs=_row_spec(tm, d), out_shape=SDS((t, d), F32),
        compiler_params=_params(("parallel",)),
    )(*after, dz1, *parts, wt_in)


def _weight_grad(name, me, parts, rhs, after=()):
    t, n = rhs.shape
    widths = [p.shape[1] for p in parts]
    rows = sum(widths)
    own_rows = rows // N_DEV
    tk = _row_tile(t, MATMUL_ROWS)
    n_steps = t // tk
    step = 256

    def body(*refs):
        me_ref, part_refs, rhs_ref = refs[0], refs[1:1 + len(parts)], refs[1 + len(parts)]
        full_ref, own_ref, acc = refs[-3], refs[-2], refs[-1]
        i = pl.program_id(0)

        def products(first):
            b = rhs_ref[...].astype(BF16)
            lo = 0
            for p_ref, w in zip(part_refs, widths):
                for c0 in range(0, w, step):
                    c1 = min(c0 + step, w)
                    val = _dot_tn(p_ref[:, c0:c1].astype(BF16), b)
                    if first:
                        acc[lo + c0:lo + c1, :] = val
                    else:
                        acc[lo + c0:lo + c1, :] += val
                lo += w

        pl.when(i == 0)(functools.partial(products, True))
        pl.when(i > 0)(functools.partial(products, False))

        @pl.when(i == n_steps - 1)
        def _():
            full_ref[...] = acc[...].astype(BF16)
            own_ref[...] = acc[pl.ds(pl.multiple_of(me_ref[0] * own_rows, 8), own_rows), :]

    body, lead = _after(after, body)
    return pl.pallas_call(
        body, name=name, grid=(n_steps,),
        in_specs=lead + [_smem_spec()] + [_row_spec(tk, w) for w in widths] + [_row_spec(tk, n)],
        out_specs=[_full_spec((rows, n)), _full_spec((own_rows, n))],
        out_shape=[SDS((rows, n), BF16), SDS((own_rows, n), F32)],
        scratch_shapes=[pltpu.VMEM((rows, n), F32)],
        compiler_params=_params(("arbitrary",)),
    )(*after, me, *parts, rhs)


def _weight_grad_jobs(name, me, jobs, after=()):
    count = len(jobs)
    t = jobs[0][0].shape[0]
    tk = _row_tile(t, MATMUL_ROWS)
    n_steps = t // tk
    shapes = [(lhs.shape[1], rhs.shape[1]) for lhs, rhs in jobs]
    most_rows, most_cols = max(r for r, _ in shapes), max(n for _, n in shapes)
    step = 256

    def body(*refs):
        me_ref, lhs_refs, rhs_refs = refs[0], refs[1:1 + count], refs[1 + count:1 + 2 * count]
        full_refs, own_refs = refs[1 + 2 * count:1 + 3 * count], refs[1 + 3 * count:1 + 4 * count]
        acc, whole, mine, sems = refs[1 + 4 * count:]
        job, i = pl.program_id(0), pl.program_id(1)

        def leaving(j):
            rows, n = shapes[j]
            return (pltpu.make_async_copy(whole.at[0:rows, 0:n], full_refs[j], sems.at[0]),
                    pltpu.make_async_copy(mine.at[0:rows // N_DEV, 0:n], own_refs[j], sems.at[1]))

        def products(j, first):
            rows, n = shapes[j]
            b = rhs_refs[j][...].astype(BF16)
            for c0 in range(0, rows, step):
                c1 = min(c0 + step, rows)
                val = _dot_tn(lhs_refs[j][:, c0:c1].astype(BF16), b)
                if first:
                    acc[c0:c1, 0:n] = val
                else:
                    acc[c0:c1, 0:n] += val

        def finish(j):
            rows, n = shapes[j]
            own_rows = rows // N_DEV
            if j > 0:
                for cp in leaving(j - 1):
                    cp.wait()
            whole[0:rows, 0:n] = acc[0:rows, 0:n].astype(BF16)
            mine[0:own_rows, 0:n] = acc[pl.ds(pl.multiple_of(me_ref[0] * own_rows, 8), own_rows), 0:n]
            for cp in leaving(j):
                cp.start()
            if j == count - 1:
                for cp in leaving(j):
                    cp.wait()

        for j in range(count):
            pl.when((job == j) & (i == 0))(functools.partial(products, j, True))
            pl.when((job == j) & (i > 0))(functools.partial(products, j, False))
            pl.when((job == j) & (i == n_steps - 1))(functools.partial(finish, j))

    def turn(j):
        return lambda job, i: (jnp.where(job == j, i, jnp.where(job < j, 0, n_steps - 1)), 0)

    body, lead = _after(after, body)
    res = pl.pallas_call(
        body, name=name, grid=(count, n_steps),
        in_specs=lead + [_smem_spec()] + [pl.BlockSpec((tk, rows), turn(j)) for j, (rows, _) in enumerate(shapes)]
        + [pl.BlockSpec((tk, n), turn(j)) for j, (_, n) in enumerate(shapes)],
        out_specs=[ANY_SPEC] * (2 * count),
        out_shape=[SDS((rows, n), BF16) for rows, n in shapes] + [SDS((rows // N_DEV, n), F32) for rows, n in shapes],
        scratch_shapes=[pltpu.VMEM((most_rows, most_cols), F32), pltpu.VMEM((most_rows, most_cols), BF16),
                        pltpu.VMEM((most_rows // N_DEV, most_cols), F32), pltpu.SemaphoreType.DMA((2,))],
        compiler_params=_params(("arbitrary", "arbitrary")),
    )(*after, me, *[lhs for lhs, _ in jobs], *[rhs for _, rhs in jobs])
    return list(res[:count]), list(res[count:])


def _log_decay(decay_f, decay_b):
    def body(f_ref, b_ref, lf_ref, lb_ref):
        lf_ref[...] = jnp.log1p(-jnp.exp2(f_ref[...]))
        lb_ref[...] = jnp.log1p(-jnp.exp2(b_ref[...]))

    return pl.pallas_call(body, name="log_decay", out_shape=[SDS(decay_f.shape, F32)] * 2)(decay_f, decay_b)


def _chunk(ref, n):
    return ref[pl.ds(pl.multiple_of(n * CHUNK, CHUNK), CHUNK), :]


def _group_sum(is_a, v):
    sa = jnp.sum(jnp.where(is_a, v, 0.0), axis=1, keepdims=True)
    sb = jnp.sum(jnp.where(is_a, 0.0, v), axis=1, keepdims=True)
    return jnp.where(is_a, sa, sb)


def _seq_spec(s, col_block):
    return pl.BlockSpec((s, LANES), lambda b, h: (b, col_block + h))


def _smem_spec():
    return pl.BlockSpec(memory_space=pltpu.SMEM)


RET_UNROLL = 4
BWD_MAIN_UNROLL = 8
FWD_PREP_UNROLL = 16
BWD_PREP_UNROLL = 8


def _chunk_loop(n_chunks, body, init, unroll):
    u = unroll if n_chunks % unroll == 0 else 1

    def trip(i, carry):
        for j in range(u):
            carry = body(i * u + j, carry)
        return carry

    return lax.fori_loop(0, n_chunks // u, trip, init)


def _stacked_tables(lgf_ref, lgb_ref, pair):
    lane = lax.broadcasted_iota(jnp.int32, (1, LANES), 1)
    is_a = lane < HEAD_DIM
    lgf = jnp.where(is_a, lgf_ref[2 * pair], lgf_ref[2 * pair + 1])
    lgb = jnp.where(is_a, lgb_ref[2 * pair], lgb_ref[2 * pair + 1])
    row = lax.broadcasted_iota(jnp.int32, (CHUNK, 1), 0).astype(F32)
    kdec_f, qdec_f = jnp.exp(lgf * (CHUNK - 1.0 - row)), jnp.exp(lgf * (row + 1.0))
    kdec_b, qdec_b = jnp.exp(lgb * row), jnp.exp(lgb * (CHUNK - row))
    tab = dict(
        is_a=is_a, row=row, lam_f=jnp.exp(lgf * CHUNK), lam_b=jnp.exp(lgb * CHUNK),
        kdec=jnp.concatenate([kdec_f, kdec_b], axis=1), qdec=jnp.concatenate([qdec_f, qdec_b], axis=1),
        qexp=jnp.concatenate([jnp.broadcast_to(row + 1.0, (CHUNK, LANES)),
                              jnp.broadcast_to(CHUNK - row, (CHUNK, LANES))], axis=1),
        kexp=jnp.concatenate([jnp.broadcast_to(CHUNK - 1.0 - row, (CHUNK, LANES)),
                              jnp.broadcast_to(row, (CHUNK, LANES))], axis=1),
    )
    r = lax.broadcasted_iota(jnp.int32, (2 * LANES, LANES), 0)
    c = lax.broadcasted_iota(jnp.int32, (2 * LANES, LANES), 1)
    tab["diag2"] = ((r & (LANES - 1)) < HEAD_DIM) == (c < HEAD_DIM)
    i2 = lax.broadcasted_iota(jnp.int32, (2 * CHUNK, CHUNK), 0)
    j = lax.broadcasted_iota(jnp.int32, (2 * CHUNK, CHUNK), 1)
    head_b = i2 >= CHUNK
    diff = ((i2 & (CHUNK - 1)) - j).astype(F32)
    up, dn = jnp.maximum(diff, 0.0), jnp.maximum(-diff, 0.0)
    lgf2 = jnp.where(head_b, lgf_ref[2 * pair + 1], lgf_ref[2 * pair])
    lgb2 = jnp.where(head_b, lgb_ref[2 * pair + 1], lgb_ref[2 * pair])
    ef = jnp.where(diff >= 0, jnp.exp(lgf2 * up), 0.0)
    eb = jnp.where(diff <= 0, jnp.exp(lgb2 * dn), 0.0)
    tab["d2"] = ef + eb
    tab["df2"] = ef * up
    tab["db2"] = eb * dn
    return tab


def _stack_pair(is_a, x):
    zero = jnp.zeros_like(x)
    return jnp.concatenate([jnp.where(is_a, x, zero), jnp.where(is_a, zero, x)], axis=0)


def _unstack_pair(is_a, x2):
    return jnp.where(is_a, x2[0:CHUNK, :], x2[CHUNK:2 * CHUNK, :])


def _both_ways(x, dec):
    return (jnp.concatenate([x, x], axis=1) * dec).astype(BF16)


def _scan_states(n_chunks, st, up_rows, up_lam, down_rows, down_lam):
    zero = jnp.zeros((LANES, LANES), F32)

    def up(n, r):
        new = st[n, up_rows, :]
        st[n, up_rows, :] = r
        return r * up_lam + new

    def down(s, r):
        n = n_chunks - 1 - s
        new = st[n, down_rows, :]
        st[n, down_rows, :] = r
        return r * down_lam + new

    lax.fori_loop(0, n_chunks, up, zero)
    lax.fori_loop(0, n_chunks, down, zero)


FWD_ROWS, BWD_ROWS = pl.ds(0, LANES), pl.ds(LANES, LANES)


def _state_spec(n_chunks, pairs):
    return pl.BlockSpec((n_chunks, 2 * LANES, LANES), lambda b, h: (b * pairs + h, 0, 0))


ST_GAIN, ST_XF, ST_XB, ST_IFA, ST_IFB, ST_IBA, ST_IBB, ST_LF, ST_LB = 0, 1, 2, 3, 4, 5, 6, 8, 9
ST_ROWS = 16


GW = GROUP * HEAD_DIM
KEYS = 3 * BLOCK


def _attn_tables(g, bias_ref):
    r = lax.broadcasted_iota(jnp.int32, (GROUP * BLOCK, KEYS), 0)
    kj = lax.broadcasted_iota(jnp.int32, (GROUP * BLOCK, KEYS), 1)
    qi = r & (BLOCK - 1)
    hh = lax.shift_right_logical(r, 7)
    dist = jnp.abs(kj - BLOCK - qi)
    slope = jnp.exp2(-(GROUP * g + hh + 1).astype(F32) * (8.0 / ATTN_HEADS))
    inside = jnp.where(dist <= BLOCK, -slope * dist.astype(F32), NEG_INF)
    bias_ref[BIAS_INSIDE] = inside
    bias_ref[BIAS_FIRST] = jnp.where(kj >= BLOCK, inside, NEG_INF)
    bias_ref[BIAS_LAST] = jnp.where(kj < 2 * BLOCK, inside, NEG_INF)


BIAS_INSIDE, BIAS_FIRST, BIAS_LAST = 0, 1, 2


def _own_lanes(g):
    return lax.shift_right_logical(lax.broadcasted_iota(jnp.int32, (1, LANES), 1), 6) == g


def _mask_keys(x_ref, g, scale, pad_ref, s):
    pad_ref[0:BLOCK, :] = jnp.zeros((BLOCK, LANES), BF16)
    pad_ref[BLOCK + s:2 * BLOCK + s, :] = jnp.zeros((BLOCK, LANES), BF16)
    pad_ref[BLOCK:BLOCK + s, :] = jnp.where(_own_lanes(g), x_ref[...].astype(F32) * scale, 0.0).astype(BF16)


def _lane_block(x, j):
    return x[:, j * LANES:(j + 1) * LANES]


def _stack_heads(x, g):
    assert GROUP == 4 and GW == 2 * LANES
    x1 = pltpu.roll(x, HEAD_DIM, 1)
    keep = _own_lanes(g)
    zero = jnp.zeros((BLOCK, LANES), x.dtype)
    rows = []
    for h in range(GROUP):
        for_g0 = _lane_block(x, h // 2) if h % 2 == 0 else _lane_block(x1, ((h + 1) // 2) % 2)
        for_g1 = _lane_block(x, h // 2) if h % 2 == 1 else _lane_block(x1, h // 2)
        rows.append(jnp.where(keep, jnp.where(g == 0, for_g0, for_g1), zero))
    return jnp.concatenate(rows, axis=0)


def _unstack_heads(x4, g):
    p = [x4[h * BLOCK:(h + 1) * BLOCK, :] for h in range(GROUP)]
    cat = lambda a, b: jnp.concatenate([a, b], axis=1)
    in_place = jnp.where(g == 0, cat(p[0], p[2]), cat(p[1], p[3]))
    one_left = jnp.where(g == 0, cat(p[1], p[3]), cat(p[2], p[0]))
    return in_place + pltpu.roll(one_left, HEAD_DIM, 1)


def _sink_column(sink_ref, g):
    rh = lax.shift_right_logical(lax.broadcasted_iota(jnp.int32, (GROUP * BLOCK, 1), 0), 7)
    col = jnp.zeros((GROUP * BLOCK, 1), F32)
    for h in range(GROUP):
        col = jnp.where(rh == h, sink_ref[GROUP * g + h], col)
    return col


def _attn_probs(qm, k3, bias_ref, sink_col, n, s):
    which = jnp.where(n == 0, BIAS_FIRST, jnp.where(n == s // BLOCK - 1, BIAS_LAST, BIAS_INSIDE))
    logits = _dot_nt(qm, k3) + bias_ref[which]
    m = jnp.maximum(jnp.max(logits, axis=1, keepdims=True), sink_col)
    e = jnp.exp(logits - m)
    e_sink = jnp.exp(sink_col - m)
    inv = 1.0 / (jnp.sum(e, axis=1, keepdims=True) + e_sink)
    return e * inv, e_sink * inv


PAIRS_PER_KV = (RET_HEADS // 2) // KV_HEADS
FWD_ORDER = "rrarra"
BWD_ORDER = "rararr"


def _trip_order(order, chunks, blocks):
    if order.count("r") == chunks and order.count("a") == blocks:
        return order
    return "r" * chunks + "a" * blocks


def _mixers_fwd(u, lgf, lgb, gn_gain, sink, b_loc, after=()):
    t = u.shape[0]
    s = t // b_loc
    n_chunks = s // CHUNK
    pairs = RET_HEADS // 2
    trips = n_chunks // RET_UNROLL
    blocks_half = (s // BLOCK) // PAIRS_PER_KV
    per_trip = blocks_half // trips
    assert n_chunks % RET_UNROLL == 0 and blocks_half % trips == 0 and PAIRS_PER_KV == 2 and s >= 2 * BLOCK

    def body(lgf_ref, lgb_ref, sink_ref, q_ref, k_ref, v_ref, g_ref, gain_ref, aq_ref, ak_ref, av_ref,
             r_ref, xhat_ref, rstd_ref, a_ref, st, kpad, vpad, bias):
        pair = pl.program_id(1)
        g, half = lax.shift_right_logical(pair, 1), pair & 1
        tab = _stacked_tables(lgf_ref, lgb_ref, pair)
        is_a = tab["is_a"]

        @pl.when(half == 0)
        def _():
            _attn_tables(g, bias)
            _mask_keys(ak_ref, g, Q_SCALE, kpad, s)
            _mask_keys(av_ref, g, 1.0, vpad, s)

        def kv_body(n, _):
            k8 = _chunk(k_ref, n).astype(F32) * Q_SCALE
            st[n] = jnp.where(tab["diag2"], _dot_tn(_both_ways(k8, tab["kdec"]), _chunk(v_ref, n)), 0.0)
            return 0

        _chunk_loop(n_chunks, kv_body, 0, FWD_PREP_UNROLL)
        _scan_states(n_chunks, st, FWD_ROWS, tab["lam_f"], BWD_ROWS, tab["lam_b"])
        sink_col = _sink_column(sink_ref, g)

        def retention_chunk(n):
            q = _chunk(q_ref, n)
            k8 = (_chunk(k_ref, n).astype(F32) * Q_SCALE).astype(BF16)
            v = _chunk(v_ref, n)
            p2 = (_dot_nt(_stack_pair(is_a, q), k8) * tab["d2"]).astype(BF16)
            y = _unstack_pair(is_a, _dot_nn(p2, v))
            y = y + _dot_nn(_both_ways(q.astype(F32), tab["qdec"]), st[n].astype(BF16))
            rows = pl.ds(pl.multiple_of(n * CHUNK, CHUNK), CHUNK)
            mu = _group_sum(is_a, y) * (1.0 / HEAD_DIM)
            dlt = y - mu
            var = _group_sum(is_a, dlt * dlt) * (1.0 / HEAD_DIM)
            rstd = lax.rsqrt(var + GN_EPS)
            xhat = dlt * rstd
            xhat_ref[rows, :] = xhat
            rstd_ref[rows, :] = rstd
            gate = _chunk(g_ref, n).astype(F32)
            r_ref[rows, :] = (xhat * gain_ref[...] * gate * _sigmoid(gate)).astype(BF16)

        def attention_block(blk):
            n = half * blocks_half + blk
            rows = pl.ds(pl.multiple_of(blk * BLOCK, BLOCK), BLOCK)
            keys = pl.ds(pl.multiple_of(n * BLOCK, BLOCK), KEYS)
            p, _ = _attn_probs(_stack_heads(aq_ref[rows, :], g), kpad[keys, :], bias, sink_col, n, s)
            a_ref[rows, :] = _unstack_heads(_dot_nn(p.astype(BF16), vpad[keys, :]), g).astype(BF16)

        def trip(i, _):
            chunk, blk = 0, 0
            for kind in _trip_order(FWD_ORDER, RET_UNROLL, per_trip):
                if kind == "r":
                    retention_chunk(i * RET_UNROLL + chunk)
                    chunk += 1
                else:
                    attention_block(i * per_trip + blk)
                    blk += 1
            return 0

        lax.fori_loop(0, trips, trip, 0)

    lane_blk = lambda c0: _seq_spec(s, c0 // LANES)
    half_rows = blocks_half * BLOCK
    aq_spec = pl.BlockSpec((half_rows, GW), lambda b, h: (b * PAIRS_PER_KV + (h & 1), C_AQ // GW + h // 2))
    a_spec = pl.BlockSpec((half_rows, GW), lambda b, h: (b * PAIRS_PER_KV + (h & 1), h // 2))
    kv_spec = lambda c0: pl.BlockSpec((s, LANES), lambda b, h: (b, c0 // LANES))
    pad = pltpu.VMEM((s + 2 * BLOCK, LANES), BF16)
    body, lead = _after(after, body)
    return pl.pallas_call(
        body, name="mixers_fwd", grid=(b_loc, pairs),
        in_specs=lead + [_smem_spec(), _smem_spec(), _smem_spec(), lane_blk(C_RQ), lane_blk(C_RK), lane_blk(C_RV),
                         lane_blk(C_RG), pl.BlockSpec((1, LANES), lambda b, h: (0, h)), aq_spec, kv_spec(C_AK),
                         kv_spec(C_AV)],
        out_specs=[_seq_spec(s, 0), _seq_spec(s, 0), _seq_spec(s, 0), a_spec, _state_spec(n_chunks, pairs)],
        out_shape=[SDS((t, RET_W), BF16), SDS((t, RET_W), F32), SDS((t, RET_W), F32), SDS((t, ATT_W), BF16),
                   SDS((b_loc * pairs * n_chunks, 2 * LANES, LANES), F32)],
        scratch_shapes=[pad, pad, pltpu.VMEM((3, GROUP * BLOCK, KEYS), F32)],
        compiler_params=_params(("arbitrary", "arbitrary")),
    )(*after, lgf, lgb, sink, u, u, u, u, gn_gain, u, u, u)


def _mixers_bwd(u, xhat, rstd, states, dr, da, lgf, lgb, gn_gain, sink, b_loc, after=()):
    t = u.shape[0]
    s = t // b_loc
    n_chunks = s // CHUNK
    pairs = RET_HEADS // 2
    unroll = BWD_MAIN_UNROLL if n_chunks % BWD_MAIN_UNROLL == 0 else RET_UNROLL
    trips = n_chunks // unroll
    blocks_half = (s // BLOCK) // PAIRS_PER_KV
    per_trip = blocks_half // trips
    assert n_chunks % unroll == 0 and blocks_half % trips == 0 and PAIRS_PER_KV == 2 and s >= 2 * BLOCK

    def body(lgf_ref, lgb_ref, sink_ref, q_ref, k_ref, v_ref, g_ref, xhat_ref, rstd_ref, dr_ref, gain_ref,
             aq_ref, ak_ref, av_ref, do_ref, st,
             dq_ref, dk_ref, dv_ref, dg_ref, st_ref, daq_ref, dak_ref, dav_ref, dsink_ref,
             gr, dy_s, kpad, vpad, bias, dk_acc, dv_acc):
        pair = pl.program_id(1)
        g, half = lax.shift_right_logical(pair, 1), pair & 1
        tab = _stacked_tables(lgf_ref, lgb_ref, pair)
        is_a = tab["is_a"]
        gain = gain_ref[...]

        @pl.when(half == 0)
        def _():
            _attn_tables(g, bias)
            _mask_keys(ak_ref, g, Q_SCALE, kpad, s)
            _mask_keys(av_ref, g, 1.0, vpad, s)
            dsink_ref[...] = jnp.zeros_like(dsink_ref)

        @pl.when(pair == 0)
        def _():
            dk_acc[...] = jnp.zeros_like(dk_acc)
            dv_acc[...] = jnp.zeros_like(dv_acc)

        def norm_body(n, dgain):
            rows = pl.ds(pl.multiple_of(n * CHUNK, CHUNK), CHUNK)
            xhat, rstd = xhat_ref[rows, :], rstd_ref[rows, :]
            gate = g_ref[rows, :].astype(F32)
            sg = _sigmoid(gate)
            silu = gate * sg
            d_out = dr_ref[rows, :].astype(F32)
            dg_ref[rows, :] = (d_out * xhat * gain * (sg * (1.0 + gate * (1.0 - sg)))).astype(BF16)
            dxh = d_out * gain * silu
            m1 = _group_sum(is_a, dxh) * (1.0 / HEAD_DIM)
            m2 = _group_sum(is_a, dxh * xhat) * (1.0 / HEAD_DIM)
            dy = (rstd * (dxh - m1 - xhat * m2)).astype(BF16)
            dy_s[rows, :] = dy
            qf = q_ref[rows, :].astype(F32)
            gr[n] = jnp.where(tab["diag2"], _dot_tn(_both_ways(qf, tab["qdec"]), dy), 0.0)
            return dgain + jnp.sum(d_out * xhat * silu, axis=0, keepdims=True)

        colsum = lambda x: jnp.sum(x, axis=0, keepdims=True)

        def grad_body(n, carry):
            xfb, ifa, ifb, iba, ibb, lf, lb = carry
            rows = pl.ds(pl.multiple_of(n * CHUNK, CHUNK), CHUNK)
            q = q_ref[rows, :]
            qf = q.astype(F32)
            k8f = k_ref[rows, :].astype(F32) * Q_SCALE
            k8 = k8f.astype(BF16)
            v = v_ref[rows, :]
            dy = dy_s[rows, :]
            q2, dy2 = _stack_pair(is_a, q), _stack_pair(is_a, dy)
            sc = _dot_nt(q2, k8)
            dp = _dot_nt(dy2, v)
            a2 = (sc * tab["d2"]).astype(BF16)
            ds2 = (dp * tab["d2"]).astype(BF16)
            dq = _unstack_pair(is_a, _dot_nn(ds2, k8))
            dk = _dot_tn(ds2, q2)
            dv = _dot_tn(a2, dy2)
            prod = sc * dp
            pf, pb = prod * tab["df2"], prod * tab["db2"]
            ifa, ifb = ifa + colsum(pf[0:CHUNK, :]), ifb + colsum(pf[CHUNK:2 * CHUNK, :])
            iba, ibb = iba + colsum(pb[0:CHUNK, :]), ibb + colsum(pb[CHUNK:2 * CHUNK, :])
            states, sgrads = st[n], gr[n]
            sb, gb = states.astype(BF16), sgrads.astype(BF16)
            dqc = _dot_nt(dy, sb) * tab["qdec"]
            dkc = _dot_nt(v, gb) * tab["kdec"]
            dv = dv + _dot_nn(_both_ways(k8f, tab["kdec"]), gb)
            dq_ref[rows, :] = (dq + dqc[:, 0:LANES] + dqc[:, LANES:2 * LANES]).astype(BF16)
            dk_ref[rows, :] = ((dk + dkc[:, 0:LANES] + dkc[:, LANES:2 * LANES]) * Q_SCALE).astype(BF16)
            dv_ref[rows, :] = dv.astype(BF16)
            q2w, k2w = jnp.concatenate([qf, qf], axis=1), jnp.concatenate([k8f, k8f], axis=1)
            xfb = xfb + colsum(tab["qexp"] * q2w * dqc + tab["kexp"] * k2w * dkc)
            prod_s = sgrads * states
            lf, lb = lf + colsum(prod_s[0:LANES, :]), lb + colsum(prod_s[LANES:2 * LANES, :])
            return xfb, ifa, ifb, iba, ibb, lf, lb

        sink_col = _sink_column(sink_ref, g)
        head_row = lax.broadcasted_iota(jnp.int32, dsink_ref.shape, 0)

        def attention_block(blk):
            n = half * blocks_half + blk
            rows = pl.ds(pl.multiple_of(blk * BLOCK, BLOCK), BLOCK)
            keys = pl.ds(pl.multiple_of(n * BLOCK, BLOCK), KEYS)
            qm = _stack_heads(aq_ref[rows, :], g)
            k3, v3 = kpad[keys, :], vpad[keys, :]
            p, p_sink = _attn_probs(qm, k3, bias, sink_col, n, s)
            dom = _stack_heads(do_ref[rows, :], g)
            dp = _dot_nt(dom, v3)
            delta = jnp.sum(p * dp, axis=1, keepdims=True)
            ds_mat = (p * (dp - delta)).astype(BF16)
            daq_ref[rows, :] = _unstack_heads(_dot_nn(ds_mat, k3), g).astype(BF16)
            dk_acc[keys, :] += _dot_tn(ds_mat, qm) * Q_SCALE
            dv_acc[keys, :] += _dot_tn(p.astype(BF16), dom)
            w = p_sink * delta
            upd = jnp.zeros(dsink_ref.shape, F32)
            for h in range(GROUP):
                upd = upd + jnp.where(head_row == h, -jnp.sum(w[h * BLOCK:(h + 1) * BLOCK, :]), 0.0)
            dsink_ref[...] += upd

        dgain = _chunk_loop(n_chunks, norm_body, jnp.zeros((1, LANES), F32), BWD_PREP_UNROLL)
        _scan_states(n_chunks, gr, BWD_ROWS, tab["lam_b"], FWD_ROWS, tab["lam_f"])

        def trip(i, carry):
            chunk, blk = 0, 0
            for kind in _trip_order(BWD_ORDER * (unroll // RET_UNROLL), unroll, per_trip):
                if kind == "r":
                    carry = grad_body(i * unroll + chunk, carry)
                    chunk += 1
                else:
                    attention_block(i * per_trip + blk)
                    blk += 1
            return carry

        z = jnp.zeros((1, LANES), F32)
        init = (jnp.zeros((1, 2 * LANES), F32), z, z, z, z, z, z)
        xfb, ifa, ifb, iba, ibb, lf, lb = lax.fori_loop(0, trips, trip, init)
        st_ref[...] = jnp.zeros_like(st_ref)
        st_ref[ST_GAIN:ST_GAIN + 1, :] = dgain
        st_ref[ST_XF:ST_XF + 1, :] = xfb[:, 0:LANES]
        st_ref[ST_XB:ST_XB + 1, :] = xfb[:, LANES:2 * LANES]
        st_ref[ST_IFA:ST_IFA + 1, :] = ifa
        st_ref[ST_IFB:ST_IFB + 1, :] = ifb
        st_ref[ST_IBA:ST_IBA + 1, :] = iba
        st_ref[ST_IBB:ST_IBB + 1, :] = ibb
        st_ref[ST_LF:ST_LF + 1, :] = lf * (CHUNK * tab["lam_f"])
        st_ref[ST_LB:ST_LB + 1, :] = lb * (CHUNK * tab["lam_b"])

        @pl.when(pair == pairs - 1)
        def _():
            dak_ref[...] = dk_acc[BLOCK:BLOCK + s, :].astype(BF16)
            dav_ref[...] = dv_acc[BLOCK:BLOCK + s, :].astype(BF16)

    lane_blk = lambda c0: _seq_spec(s, c0 // LANES)
    seq0 = _seq_spec(s, 0)
    half_rows = blocks_half * BLOCK
    aq_spec = pl.BlockSpec((half_rows, GW), lambda b, h: (b * PAIRS_PER_KV + (h & 1), C_AQ // GW + h // 2))
    a_spec = pl.BlockSpec((half_rows, GW), lambda b, h: (b * PAIRS_PER_KV + (h & 1), h // 2))
    kv_spec = lambda c0: pl.BlockSpec((s, LANES), lambda b, h: (b, c0 // LANES))
    kv_out = pl.BlockSpec((s, LANES), lambda b, h: (b, 0))
    state = pltpu.VMEM((n_chunks, 2 * LANES, LANES), F32)
    pad = pltpu.VMEM((s + 2 * BLOCK, LANES), BF16)
    acc = pltpu.VMEM((s + 2 * BLOCK, LANES), F32)
    body, lead = _after(after, body)
    return pl.pallas_call(
        body, name="mixers_bwd", grid=(b_loc, pairs),
        in_specs=lead + [_smem_spec(), _smem_spec(), _smem_spec(), lane_blk(C_RQ), lane_blk(C_RK), lane_blk(C_RV),
                         lane_blk(C_RG), seq0, seq0, seq0, pl.BlockSpec((1, LANES), lambda b, h: (0, h)),
                         aq_spec, kv_spec(C_AK), kv_spec(C_AV), a_spec, _state_spec(n_chunks, pairs)],
        out_specs=[seq0] * 4 + [pl.BlockSpec((ST_ROWS, LANES), lambda b, h: (b, h)), a_spec, kv_out, kv_out,
                                pl.BlockSpec((8, LANES), lambda b, h: (b * KV_HEADS + h // 2, 0))],
        out_shape=[SDS((t, RET_W), BF16)] * 4 + [SDS((b_loc * ST_ROWS, RET_W), F32), SDS((t, ATT_W), BF16),
                                                   SDS((t, KV_W), BF16), SDS((t, KV_W), BF16),
                                                   SDS((b_loc * KV_HEADS * 8, LANES), F32)],
        scratch_shapes=[state, pltpu.VMEM((s, LANES), BF16), pad, pad,
                        pltpu.VMEM((3, GROUP * BLOCK, KEYS), F32), acc, acc],
        compiler_params=_params(("arbitrary", "arbitrary")),
    )(*after, lgf, lgb, sink, u, u, u, u, xhat, rstd, dr, gn_gain, u, u, u, da, states)


def _pack_small(acc2, acc1, ret_stats, dsink, b_loc, d):
    pairs = RET_HEADS // 2

    def body(acc2_ref, acc1_ref, st_ref, dsink_ref, out_ref):
        out_ref[...] = jnp.zeros_like(out_ref)
        out_ref[ROW_LN1G:ROW_LN1G + 1, :] = acc1_ref[0:1, :]
        out_ref[ROW_LN1B:ROW_LN1B + 1, :] = acc1_ref[1:2, :]
        out_ref[ROW_LN2G:ROW_LN2G + 1, :] = acc2_ref[1:2, :]
        out_ref[ROW_LN2B:ROW_LN2B + 1, :] = acc2_ref[2:3, :]
        out_ref[ROW_LOSS:ROW_LOSS + 1, :] = acc2_ref[0:1, :]
        st = st_ref[0:ST_ROWS, :]
        for b in range(1, b_loc):
            st = st + st_ref[b * ST_ROWS:(b + 1) * ST_ROWS, :]
        out_ref[ROW_GN:ROW_GN + 1, 0:RET_W] = st[ST_GAIN:ST_GAIN + 1, :]
        lane = lax.broadcasted_iota(jnp.int32, (1, d), 1)
        misc = jnp.zeros((1, d), F32)
        for pr in range(pairs):
            blk = st[:, pr * LANES:(pr + 1) * LANES]
            half = lax.broadcasted_iota(jnp.int32, (1, LANES), 1) < HEAD_DIM
            for h in range(2):
                sel = half if h == 0 else jnp.logical_not(half)
                cross_f = jnp.sum(jnp.where(sel, blk[ST_XF:ST_XF + 1, :] + blk[ST_LF:ST_LF + 1, :], 0.0))
                cross_b = jnp.sum(jnp.where(sel, blk[ST_XB:ST_XB + 1, :] + blk[ST_LB:ST_LB + 1, :], 0.0))
                intra_f = jnp.sum(blk[ST_IFA + h:ST_IFA + h + 1, :])
                intra_b = jnp.sum(blk[ST_IBA + h:ST_IBA + h + 1, :])
                head = 2 * pr + h
                misc = jnp.where(lane == MISC_DF + head, cross_f + intra_f, misc)
                misc = jnp.where(lane == MISC_DB + head, cross_b + intra_b, misc)
        for g in range(KV_HEADS):
            tot = dsink_ref[g * 8:(g + 1) * 8, :]
            for b in range(1, b_loc):
                tot = tot + dsink_ref[(b * KV_HEADS + g) * 8:(b * KV_HEADS + g + 1) * 8, :]
            for h in range(GROUP):
                misc = jnp.where(lane == MISC_SINK + GROUP * g + h, jnp.sum(tot[h:h + 1, 0:1]), misc)
        out_ref[ROW_MISC:ROW_MISC + 1, :] = misc

    return pl.pallas_call(body, name="pack_small", out_shape=SDS((SMALL_ROWS, d), F32))(acc2, acc1, ret_stats, dsink)


BIG = ("w_in", "w_out", "w_ffn_gate", "w_ffn_up", "w_ffn_down", "w_ple_proj", "w_ple_gate")
TRANSPOSED_OUTSIDE = ("w_in", "w_ffn_gate", "w_ffn_up")
TRANSPOSED_HERE = ("w_ple_proj",)
SMALL = ("ret_decay_fwd", "ret_decay_bwd", "ret_gn_gain", "attn_sink", "ln1_gain", "ln1_bias", "ln2_gain", "ln2_bias")
ORDER = ("w_in", "ret_decay_fwd", "ret_decay_bwd", "ret_gn_gain", "attn_sink", "w_out", "ln1_gain", "ln1_bias",
         "w_ffn_gate", "w_ffn_up", "w_ffn_down", "w_ple_proj", "w_ple_gate", "ln2_gain", "ln2_bias")


GATHER_ORDER = ("w_in", "w_ffn_up", "w_out", "w_ffn_gate", "w_ple_gate", "w_ple_proj", "w_ffn_down")
GATHER_TWO_LEVEL = ("w_in", "w_ffn_up", "w_out")

def _local_step(x2, p2, target2, fetch, publish, small, b_loc, me):
    d = x2.shape[1]
    lgf, lgb = _log_decay(small["ret_decay_fwd"], small["ret_decay_bwd"])
    lgf1, lgb1, sink1 = lgf.reshape(-1), lgb.reshape(-1), small["attn_sink"].reshape(-1)
    (w_in,) = fetch(("w_in",), ())
    u, xb = _in_proj(x2, w_in)
    passed = fetch.pass_on(("w_ffn_up", "w_out"), (xb, lgf))
    r, ret_xhat, ret_rstd, a, ret_states = _mixers_fwd(u, lgf1, lgb1, small["ret_gn_gain"], sink1, b_loc, passed)
    w_out, w_gate, w_up = fetch(("w_out", "w_ffn_gate", "w_ffn_up"), (r, a))
    xhat1, rstd1, h1b, dact_dg, dact_du, act = _mix_ln1_ffn_up(
        r, a, x2, w_out, w_gate, w_up, small["ln1_gain"], small["ln1_bias"])
    w_pg, w_pe, w_down = fetch(("w_ple_gate", "w_ple_proj", "w_ffn_down"), (act,))
    dz2, dz2b, dsb, dpleb, dg, dup, acc2 = _ffn_down_ln2_loss(
        act, dact_dg, dact_du, h1b, p2, xhat1, target2, w_down, w_pg, w_pe,
        small["ln1_gain"], small["ln1_bias"], small["ln2_gain"], small["ln2_bias"])
    own = {}

    def grad(name, parts, rhs, after=()):
        whole, own[name] = _weight_grad("grad_" + name, me, parts, rhs, after)
        return whole

    ffn_jobs = dict(w_ffn_down=(act, dz2b), w_ple_proj=(dpleb, p2), w_ple_gate=(h1b, dsb),
                    w_ffn_gate=(dg, h1b), w_ffn_up=(dup, h1b))
    wholes, owns = _weight_grad_jobs("grad_w_ffn", me, list(ffn_jobs.values()))
    own.update(zip(ffn_jobs, owns))
    t2 = publish("ffn", dict(zip(ffn_jobs, wholes)))
    dz1, dz1b, dr, da, acc1 = _dh1_ln1_bwd(
        dz2, dg, dup, dsb, xhat1, rstd1, w_gate, w_up, w_pg, w_out, small["ln1_gain"], t2)
    t3 = publish("out", dict(w_out=grad("w_out", [r, a], dz1b)))
    dq, dk, dv, dgate, ret_stats, daq, dak, dav, dsink = _mixers_bwd(
        u, ret_xhat, ret_rstd, ret_states, dr, da, lgf1, lgb1, small["ret_gn_gain"], sink1, b_loc, t3)
    parts = [dq, dk, dv, dgate, daq, dak, dav]
    small_part = _pack_small(acc2, acc1, ret_stats, dsink, b_loc, d)
    t4 = publish("in", dict(w_in=grad("w_in", parts, xb)), small_part)
    grad_x = _in_proj_bwd(dz1, parts, w_in, t4)
    return grad_x, own, small_part


def kernel(x, p, w_in, ret_decay_fwd, ret_decay_bwd, ret_gn_gain, attn_sink, w_out, ln1_gain, ln1_bias, w_ffn_gate, w_ffn_up, w_ffn_down, w_ple_proj, w_ple_gate, ln2_gain, ln2_bias, loss_target, m_w_in, m_ret_decay_fwd, m_ret_decay_bwd, m_ret_gn_gain, m_attn_sink, m_w_out, m_ln1_gain, m_ln1_bias, m_w_ffn_gate, m_w_ffn_up, m_w_ffn_down, m_w_ple_proj, m_w_ple_gate, m_ln2_gain, m_ln2_bias, v_w_in, v_ret_decay_fwd, v_ret_decay_bwd, v_ret_gn_gain, v_attn_sink, v_w_out, v_ln1_gain, v_ln1_bias, v_w_ffn_gate, v_w_ffn_up, v_w_ffn_down, v_w_ple_proj, v_w_ple_gate, v_ln2_gain, v_ln2_bias):
    given = dict(locals())

    def strip(n, a):
        if n not in BIG:
            return a
        return a[0].T if n in TRANSPOSED_OUTSIDE else a[0]

    def restore(n, a):
        if n not in BIG:
            return a
        return (a.T if n in TRANSPOSED_OUTSIDE else a)[None]

    w = {n: strip(n, given[n]) for n in ORDER}
    m = {n: strip(n, given["m_" + n]) for n in ORDER}
    v = {n: strip(n, given["v_" + n]) for n in ORDER}
    b_loc, s, d = x.shape
    x2 = x.reshape(b_loc * s, d)
    p2 = p[0].reshape(b_loc * s, p.shape[-1])
    target2 = loss_target.reshape(b_loc * s, d)

    small = {n: w[n] for n in SMALL}
    me = (4 * lax.axis_index("x") + 2 * lax.axis_index("y") + lax.axis_index("c")).astype(jnp.int32).reshape(1)

    gather = _gather_start(
        {n: w[n] for n in GATHER_ORDER},
        [_gather_copy_near if n in GATHER_TWO_LEVEL else _gather_copy for n in GATHER_ORDER])

    passing = {}

    def pass_on(names, after):
        relayed = _gather_relay("gather_relay_" + names[0], gather, [GATHER_ORDER.index(n) for n in names], list(after))
        passing.update({n: (relayed, j) for j, n in enumerate(names)})
        return (relayed["token"],)

    def fetch(names, after):
        out = {}
        for n in [n for n in names if n in GATHER_TWO_LEVEL]:
            if n not in passing:
                pass_on((n,), after)
            relayed, j = passing[n]
            out[n] = _split_copy_wait("gather_wait_" + n, relayed, [j], list(after))[0][0]
        direct = [n for n in names if n not in GATHER_TWO_LEVEL]
        if direct:
            got = _split_copy_wait("gather_wait_" + direct[0], gather, [GATHER_ORDER.index(n) for n in direct],
                                   list(after))
            out.update({n: item[0] for n, item in zip(direct, got)})
        return [out[n] for n in names]

    scatters = []

    def publish(tag, products, small_sums=None):
        items = [(products[n], lax.empty((N_DEV - 1, products[n].shape[0] // N_DEV, products[n].shape[1]), BF16))
                 for n in products]
        copies = [_scatter_copy] * len(items)
        if small_sums is not None:
            items.append((small_sums, lax.empty((N_DEV - 1,) + small_sums.shape, F32)))
            copies.append(_small_copy)
        started = _split_copy_start("scatter_start_" + tag, items, copies)
        scatters.append((list(products), small_sums is not None, started))
        return (started["token"],)

    fetch.pass_on = pass_on
    grad_x, own, small_part = _local_step(x2, p2, target2, fetch, publish, small, b_loc, me)

    out_g, out_d, out_m, out_v = {}, {}, {}, {}
    after = [grad_x]
    for names, with_small, started in scatters:
        landed = _split_copy_wait("scatter_wait_" + names[0], started, list(range(len(started["items"]))), after)
        if with_small:
            mine, from_peers = landed[-1]
            loss, sg, sd, sm, sv = _small_adamw(
                me, mine, from_peers, small, {n: m[n] for n in SMALL}, {n: v[n] for n in SMALL})
            for dst, src in ((out_g, sg), (out_d, sd), (out_m, sm), (out_v, sv)):
                dst.update(src)
        recv = {n: item[1] for n, item in zip(names, landed)}
        alike = {}
        for n in names:
            alike.setdefault((own[n].shape, n in TRANSPOSED_HERE), []).append(n)
        for (_, transposed), ns in alike.items():
            res = _reduce_adamw(ns[0], [own[n] for n in ns], [recv[n] for n in ns], [w[n] for n in ns],
                                [m[n] for n in ns], [v[n] for n in ns], transposed)
            for dst, vals in zip((out_g, out_d, out_m, out_v), res):
                dst.update(zip(ns, vals))
        after = [out_v[names[-1]]]

    outs = [loss[0, 0], grad_x.reshape(x.shape)]
    for group in (out_g, out_d, out_m, out_v):
        outs += [restore(n, group[n]) for n in ORDER]
    return tuple(outs)
```

```python
import functools

import jax
import jax.numpy as jnp
from jax import lax
from jax.experimental import pallas as pl
from jax.experimental.pallas import tpu as pltpu

F32, BF16 = jnp.float32, jnp.bfloat16
SDS = jax.ShapeDtypeStruct
MESH = pl.DeviceIdType.MESH

N_DEV = 8
HEAD_DIM = 64
RET_HEADS = 8
ATTN_HEADS = 8
KV_HEADS = 2
GROUP = ATTN_HEADS // KV_HEADS
RET_W = RET_HEADS * HEAD_DIM
ATT_W = ATTN_HEADS * HEAD_DIM
KV_W = KV_HEADS * HEAD_DIM
LANES = 128
CHUNK = 128
BLOCK = 128
Q_SCALE = HEAD_DIM ** -0.5
ALPHA = 2.0 ** 0.25
LN_EPS = 1e-5
GN_EPS = 1e-5
NEG_INF = -1e30
C_RQ, C_RK, C_RV, C_RG = 0, RET_W, 2 * RET_W, 3 * RET_W
C_AQ = 4 * RET_W
C_AK = C_AQ + ATT_W
C_AV = C_AK + KV_W
IN_W = C_AV + KV_W

ADAM_LR = 0.001
ADAM_B1 = 0.9
ADAM_B2 = 0.999
ADAM_EPS = 1e-08
ADAM_WD = 0.01
ADAM_STEP = 10

VMEM_LIMIT = 56 * 1024 * 1024
MATMUL_ROWS = 512
EPILOGUE_ROWS = 256
SUB_ROWS = 512
SMALL_ROWS = 16
ROW_LN1G, ROW_LN1B, ROW_LN2G, ROW_LN2B, ROW_LOSS, ROW_GN, ROW_MISC = 0, 1, 2, 3, 4, 5, 6
MISC_DF, MISC_DB, MISC_SINK = 0, 8, 16


def _dot_nn(a, b):
    return lax.dot_general(a, b, (((1,), (0,)), ((), ())), preferred_element_type=F32)


def _dot_nt(a, b):
    return lax.dot_general(a, b, (((1,), (1,)), ((), ())), preferred_element_type=F32)


def _dot_tn(a, b):
    return lax.dot_general(a, b, (((0,), (0,)), ((), ())), preferred_element_type=F32)


def _params(sem=None, vmem=VMEM_LIMIT):
    kw = {"vmem_limit_bytes": vmem}
    if sem is not None:
        kw["dimension_semantics"] = sem
    return pltpu.CompilerParams(**kw)


def _row_tile(t, want=512):
    tm = want
    while t % tm:
        tm //= 2
    return tm


def _sigmoid(x):
    return jax.nn.sigmoid(x)


def _layer_norm_stats(z):
    mu = jnp.mean(z, axis=1, keepdims=True)
    d = z - mu
    var = jnp.mean(d * d, axis=1, keepdims=True)
    rstd = lax.rsqrt(var + LN_EPS)
    return d * rstd, rstd


def _layer_norm_bwd(dxh, xhat, rstd):
    m1 = jnp.mean(dxh, axis=1, keepdims=True)
    m2 = jnp.mean(dxh * xhat, axis=1, keepdims=True)
    return rstd * (dxh - m1 - xhat * m2)


def _mesh_pos():
    return lax.axis_index("x"), lax.axis_index("y"), lax.axis_index("c")


HBM_SPEC = pl.BlockSpec(memory_space=pltpu.HBM)
SEM_SPEC = pl.BlockSpec(memory_space=pltpu.SEMAPHORE)
ANY_SPEC = pl.BlockSpec(memory_space=pl.ANY)
SIDE_EFFECT = pltpu.SideEffectType.DATAFLOW_SIDE_EFFECTING
PEER_SEMS = pltpu.SemaphoreType.DMA((N_DEV - 1,))


def _in_hbm(a):
    return pltpu.with_memory_space_constraint(a, pltpu.HBM)


def _split_copy_start(name, items, copies):
    n = len(items)
    flat = [a for it in items for a in it]
    k = len(flat)

    def body(*refs):
        arr, sems = list(refs[:k]), refs[k:k + 2 * n]
        for i, it in enumerate(items):
            mine = [arr.pop(0) for _ in it]
            for m in range(1, N_DEV):
                cp = copies[i](m, mine, sems[i].at[m - 1], sems[n + i].at[m - 1])
                if cp is not None:
                    cp.start()
        token = refs[-1]
        token[...] = jnp.zeros_like(token)

    res = pl.pallas_call(
        body, name=name,
        out_shape=[PEER_SEMS] * (2 * n) + [pltpu.HBM(a.shape, a.dtype) for a in flat] + [SDS((8, LANES), F32)],
        in_specs=[HBM_SPEC] * k,
        out_specs=[SEM_SPEC] * (2 * n) + [HBM_SPEC] * k + [pl.BlockSpec(memory_space=pltpu.VMEM)],
        input_output_aliases={j: 2 * n + j for j in range(k)},
        compiler_params=pltpu.CompilerParams(has_side_effects=SIDE_EFFECT),
    )(*[_in_hbm(a) for a in flat])
    thru, out_items = list(res[2 * n:2 * n + k]), []
    for it in items:
        out_items.append(tuple(thru.pop(0) for _ in it))
    return dict(send=res[:n], recv=res[n:2 * n], items=out_items, token=res[-1], copies=copies)


def _gather_start(shards, copies):
    names = list(shards)
    n = len(names)
    flip = [name in TRANSPOSED_HERE for name in names]
    shapes = [shards[name].shape[::-1] if f else shards[name].shape for name, f in zip(names, flip)]
    most = (max(s[0] for s in shapes), max(s[1] for s in shapes))

    def body(*refs):
        src, sems, land, token = refs[:n], refs[n:3 * n], refs[3 * n:4 * n], refs[4 * n]
        wide, narrow, sem = refs[4 * n + 1:]
        for i, (rows, cols) in enumerate(shapes):
            raw = wide.at[0:cols, 0:rows] if flip[i] else wide.at[0:rows, 0:cols]
            bring = pltpu.make_async_copy(src[i], raw, sem.at[0])
            bring.start()
            bring.wait()
            narrow[0:rows, 0:cols] = (raw[...].T if flip[i] else raw[...]).astype(BF16)
            mine = land[i].at[pl.ds(pl.multiple_of(_peer_index(0) * rows, 8), rows), :]
            place = pltpu.make_async_copy(narrow.at[0:rows, 0:cols], mine, sem.at[0])
            place.start()
            place.wait()
            for m in range(1, N_DEV):
                cp = copies[i](m, [land[i]], sems[i].at[m - 1], sems[n + i].at[m - 1])
                if cp is not None:
                    cp.start()
        token[...] = jnp.zeros_like(token)

    side = max(most)
    res = pl.pallas_call(
        body, name="gather_start",
        out_shape=[PEER_SEMS] * (2 * n) + [pltpu.HBM((N_DEV * r, c), BF16) for r, c in shapes] + [SDS((8, LANES), F32)],
        in_specs=[HBM_SPEC] * n,
        out_specs=[SEM_SPEC] * (2 * n) + [HBM_SPEC] * n + [pl.BlockSpec(memory_space=pltpu.VMEM)],
        scratch_shapes=[pltpu.VMEM((side, side), F32), pltpu.VMEM(most, BF16), pltpu.SemaphoreType.DMA((1,))],
        compiler_params=pltpu.CompilerParams(has_side_effects=SIDE_EFFECT),
    )(*[_in_hbm(shards[name]) for name in names])
    return dict(send=res[:n], recv=res[n:2 * n], items=[(a,) for a in res[2 * n:3 * n]], token=res[-1], copies=copies)


def _gather_relay(name, started, which, after):
    lands = [started["items"][w][0] for w in which]
    k = len(lands)

    def body(*refs):
        land_refs, old_send, old_recv = refs[:k], refs[k:2 * k], refs[2 * k:3 * k]
        outs = refs[3 * k + len(after):]
        send, recv, token = outs[:k], outs[k:2 * k], outs[-1]
        for j in range(k):
            for m in range(1, N_DEV):
                cp = _gather_copy_near(m, [land_refs[j]], old_send[j].at[m - 1], old_recv[j].at[m - 1])
                if cp is None:
                    continue
                cp.wait_send()
                cp.wait_recv()
                if m > 1:
                    _gather_copy_pass(m + 1, [land_refs[j]], send[j].at[m], recv[j].at[m]).start()
        token[...] = jnp.zeros_like(token)

    res = pl.pallas_call(
        body, name=name,
        out_shape=[PEER_SEMS] * (2 * k) + [pltpu.HBM(a.shape, a.dtype) for a in lands] + [SDS((8, LANES), F32)],
        in_specs=[HBM_SPEC] * k + [SEM_SPEC] * (2 * k) + [ANY_SPEC] * len(after),
        out_specs=[SEM_SPEC] * (2 * k) + [HBM_SPEC] * k + [pl.BlockSpec(memory_space=pltpu.VMEM)],
        input_output_aliases={j: 2 * k + j for j in range(k)},
        compiler_params=pltpu.CompilerParams(has_side_effects=SIDE_EFFECT),
    )(*lands, *[started["send"][w] for w in which], *[started["recv"][w] for w in which],
      *[_in_hbm(a) for a in after])
    return dict(send=res[:k], recv=res[k:2 * k], items=[(a,) for a in res[2 * k:3 * k]], token=res[-1],
                copies=[_gather_copy_pass] * k)


def _split_copy_wait(name, started, which, after):
    items = [started["items"][i] for i in which]
    copies = [started["copies"][i] for i in which]
    n = len(items)
    flat = [a for it in items for a in it]
    k = len(flat)

    def body(*refs):
        arr, sems = list(refs[:k]), refs[k:k + 2 * n]
        for i, it in enumerate(items):
            mine = [arr.pop(0) for _ in it]
            for m in range(1, N_DEV):
                cp = copies[i](m, mine, sems[i].at[m - 1], sems[n + i].at[m - 1])
                if cp is not None:
                    cp.wait_send()
                    cp.wait_recv()

    res = pl.pallas_call(
        body, name=name,
        out_shape=[pltpu.HBM(a.shape, a.dtype) for a in flat],
        in_specs=[HBM_SPEC] * k + [SEM_SPEC] * (2 * n) + [ANY_SPEC] * len(after),
        out_specs=[HBM_SPEC] * k,
        input_output_aliases={j: j for j in range(k)},
        compiler_params=pltpu.CompilerParams(has_side_effects=SIDE_EFFECT),
    )(*flat, *[started["send"][i] for i in which], *[started["recv"][i] for i in which], *[_in_hbm(a) for a in after])
    thru, out_items = list(res), []
    for it in items:
        out_items.append(tuple(thru.pop(0) for _ in it))
    return out_items


def _gather_copy(m, refs, send_sem, recv_sem):
    (land_ref,) = refs
    r = land_ref.shape[0] // N_DEV
    mine = land_ref.at[pl.ds(pl.multiple_of(_peer_index(0) * r, 8), r), :]
    return pltpu.make_async_remote_copy(src_ref=mine, dst_ref=mine, send_sem=send_sem, recv_sem=recv_sem,
                                        device_id=_peer(m), device_id_type=MESH)


def _gather_copy_near(m, refs, send_sem, recv_sem):
    return _gather_copy(m, refs, send_sem, recv_sem) if m == 1 or m % 2 == 0 else None


def _gather_copy_pass(m, refs, send_sem, recv_sem):
    if m == 1 or m % 2 == 0:
        return None
    (land_ref,) = refs
    r = land_ref.shape[0] // N_DEV
    block = land_ref.at[pl.ds(pl.multiple_of(_peer_index(m ^ 1) * r, 8), r), :]
    return pltpu.make_async_remote_copy(src_ref=block, dst_ref=block, send_sem=send_sem, recv_sem=recv_sem,
                                        device_id=_peer(1), device_id_type=MESH)


def _small_copy(m, refs, send_sem, recv_sem):
    part_ref, land_ref = refs
    return pltpu.make_async_remote_copy(src_ref=part_ref, dst_ref=land_ref.at[m - 1], send_sem=send_sem,
                                        recv_sem=recv_sem, device_id=_peer(m), device_id_type=MESH)


def _scatter_copy(m, refs, send_sem, recv_sem):
    buf_ref, land_ref = refs
    r = buf_ref.shape[0] // N_DEV
    src = buf_ref.at[pl.ds(pl.multiple_of(_peer_index(m) * r, 8), r), :]
    return pltpu.make_async_remote_copy(src_ref=src, dst_ref=land_ref.at[m - 1], send_sem=send_sem,
                                        recv_sem=recv_sem, device_id=_peer(m), device_id_type=MESH)


def _peer(m):
    x, y, c = _mesh_pos()
    bx, by, bc = (m >> 2) & 1, (m >> 1) & 1, m & 1
    return (x ^ bx if bx else x, y ^ by if by else y, c ^ bc if bc else c)


def _peer_index(m):
    x, y, c = _mesh_pos()
    return (4 * x + 2 * y + c) ^ m


SMALL_PLACE = {
    "ln1_gain": (ROW_LN1G, 0), "ln1_bias": (ROW_LN1B, 0), "ln2_gain": (ROW_LN2G, 0), "ln2_bias": (ROW_LN2B, 0),
    "ret_gn_gain": (ROW_GN, 0), "ret_decay_fwd": (ROW_MISC, MISC_DF), "ret_decay_bwd": (ROW_MISC, MISC_DB),
    "attn_sink": (ROW_MISC, MISC_SINK)}


def _small_adamw(me, part, landed, w, m, v):
    d = part.shape[1]
    names = list(SMALL_PLACE)
    k = len(names)

    def body(*refs):
        me_ref, part_ref, land_ref = refs[:3]
        refs = refs[2:]
        w_refs, m_refs, v_refs = refs[1:1 + k], refs[1 + k:1 + 2 * k], refs[1 + 2 * k:1 + 3 * k]
        outs = refs[1 + 3 * k:1 + 7 * k + 1]
        tot_ref = refs[-1]
        loss_ref, g_refs, dl_refs = outs[0], outs[1:1 + k], outs[1 + k:1 + 2 * k]
        nm_refs, nv_refs = outs[1 + 2 * k:1 + 3 * k], outs[1 + 3 * k:1 + 4 * k]
        tot = jnp.zeros(part_ref.shape, F32)
        for dev in range(N_DEV):
            j = dev ^ me_ref[0]
            tot = tot + jnp.where(j == 0, part_ref[...], land_ref[jnp.maximum(j, 1) - 1])
        tot_ref[...] = tot
        loss_ref[...] = (0.5 / d) * jnp.sum(tot_ref[ROW_LOSS:ROW_LOSS + 1, :], axis=1, keepdims=True)
        for i, name in enumerate(names):
            row, lo = SMALL_PLACE[name]
            wv = w_refs[i][...]
            g = tot_ref[row:row + 1, lo:lo + wv.shape[1]]
            if name.startswith("ret_decay"):
                p2 = jnp.exp2(wv)
                g = g * (-p2 * jnp.log(2.0) / (1.0 - p2))
            g_refs[i][...] = g
            _adamw_store(g, wv, m_refs[i][...], v_refs[i][...], dl_refs[i], nm_refs[i], nv_refs[i])

    shapes = [SDS(w[n].shape, F32) for n in names]
    vm = pl.BlockSpec(memory_space=pltpu.VMEM)
    res = pl.pallas_call(
        body, name="small_adamw", out_shape=[SDS((1, 1), F32)] + shapes * 4,
        in_specs=[_smem_spec()] + [vm] * (2 + 3 * k), out_specs=[vm] * (1 + 4 * k),
        scratch_shapes=[pltpu.VMEM(part.shape, F32)],
    )(me, part, landed, *[w[n] for n in names], *[m[n] for n in names], *[v[n] for n in names])
    groups = [dict(zip(names, res[1 + j * k:1 + (j + 1) * k])) for j in range(4)]
    return (res[0], *groups)


def _adamw_store(g, w, m, v, dl_ref, nm_ref, nv_ref):
    m = ADAM_B1 * m + (1.0 - ADAM_B1) * g
    v = ADAM_B2 * v + (1.0 - ADAM_B2) * (g * g)
    m_hat = m / (1.0 - ADAM_B1 ** ADAM_STEP)
    v_hat = v / (1.0 - ADAM_B2 ** ADAM_STEP)
    dl_ref[...] = -ADAM_LR * (m_hat / (jnp.sqrt(v_hat) + ADAM_EPS) + ADAM_WD * w)
    nm_ref[...] = m
    nv_ref[...] = v


def _reduce_adamw(name, owns, recvs, ws, ms, vs, transposed):
    count = len(owns)
    rows, n = owns[0].shape
    steps = 1 if transposed or rows % 16 else 2
    rb = rows // steps

    def body(*refs):
        ins, outs = refs[:5 * count], refs[5 * count:]
        j = pl.program_id(0)
        for k in range(count):
            @pl.when(j == k)
            def _(k=k):
                own_ref, recv_ref, w_ref, m_ref, v_ref = ins[5 * k:5 * k + 5]
                g_ref, dl_ref, nm_ref, nv_ref = outs[4 * k:4 * k + 4]
                g = own_ref[...]
                for p in range(recv_ref.shape[0]):
                    g = g + recv_ref[p].astype(F32)
                if transposed:
                    g = g.T
                g_ref[...] = g
                _adamw_store(g, w_ref[...], m_ref[...], v_ref[...], dl_ref, nm_ref, nv_ref)

    def turn(k):
        return lambda j, i: jnp.where(j == k, i, jnp.where(j < k, 0, steps - 1))

    in_specs, out_specs = [], []
    for k in range(count):
        at = turn(k)
        blk = pl.BlockSpec(ws[0].shape if transposed else (rb, n), lambda j, i, at=at: (at(j, i), 0))
        in_specs += [pl.BlockSpec((rb, n), lambda j, i, at=at: (at(j, i), 0)),
                     pl.BlockSpec((recvs[k].shape[0], rb, n), lambda j, i, at=at: (0, at(j, i), 0)), blk, blk, blk]
        out_specs += [blk] * 4
    res = pl.pallas_call(
        body, name="adamw_" + name, grid=(count, steps), in_specs=in_specs, out_specs=out_specs,
        out_shape=[SDS(ws[0].shape, F32)] * (4 * count), compiler_params=_params(("arbitrary", "arbitrary")),
    )(*[a for k in range(count) for a in (owns[k], recvs[k], ws[k], ms[k], vs[k])])
    return [list(res[j::4]) for j in range(4)]


def _row_spec(tm, width):
    return pl.BlockSpec((tm, width), lambda i: (i, 0))


def _full_spec(shape):
    return pl.BlockSpec(shape, lambda i: (0,) * len(shape))


_acc_spec = _full_spec


def _sub_rows(tm):
    step = min(SUB_ROWS, tm)
    return [(lo, lo + step) for lo in range(0, tm, step)]


def _in_proj(x2, wt_in):
    t, d = x2.shape
    u_w = wt_in.shape[0]
    tm = _row_tile(t, MATMUL_ROWS)

    def body(x_ref, w_ref, u_ref, xb_ref):
        xb = x_ref[...].astype(BF16)
        xb_ref[...] = xb
        u_ref[...] = _dot_nt(xb, w_ref[...]).astype(BF16)

    return pl.pallas_call(
        body, name="in_proj", grid=(t // tm,),
        in_specs=[_row_spec(tm, d), _full_spec(wt_in.shape)],
        out_specs=[_row_spec(tm, u_w), _row_spec(tm, d)],
        out_shape=[SDS((t, u_w), BF16), SDS((t, d), BF16)],
        compiler_params=_params(("parallel",)),
    )(x2, wt_in)


def _col_halves(f):
    n = f // LANES
    k = (n + 1) // 2 * LANES
    return [(0, k), (k, f)] if k < f else [(0, f)]


def _mix_ln1_ffn_up(r, a, x2, w_out, wt_gate, wt_up, g1, b1):
    t, d = x2.shape
    f = wt_gate.shape[0]
    tm = _row_tile(t, EPILOGUE_ROWS)

    def body(r_ref, a_ref, x_ref, wo_ref, wg_ref, wu_ref, g_ref, b_ref, xh_ref, rs_ref, hb_ref, dg_ref, du_ref,
             act_ref):
        mix = _dot_nn(r_ref[...], wo_ref[0:RET_W, :]) + _dot_nn(a_ref[...], wo_ref[RET_W:RET_W + ATT_W, :])
        z = ALPHA * x_ref[...] + mix
        xhat, rstd = _layer_norm_stats(z)
        xh_ref[...] = xhat
        rs_ref[...] = jnp.broadcast_to(rstd, rs_ref.shape)
        h = (xhat * g_ref[...] + b_ref[...]).astype(BF16)
        hb_ref[...] = h
        g = _dot_nt(h, wg_ref[...])
        u = _dot_nt(h, wu_ref[...])
        sg = _sigmoid(g)
        silu = g * sg
        dg_ref[...] = (u * (sg * (1.0 + g * (1.0 - sg)))).astype(BF16)
        du_ref[...] = silu.astype(BF16)
        act_ref[...] = (silu * u).astype(BF16)

    wide, narrow = _row_spec(tm, f), _row_spec(tm, d)
    return pl.pallas_call(
        body, name="mix_ln1_ffn_up", grid=(t // tm,),
        in_specs=[_row_spec(tm, RET_W), _row_spec(tm, ATT_W), narrow, _resident_spec(w_out.shape),
                  _resident_spec(wt_gate.shape), _resident_spec(wt_up.shape), _full_spec(g1.shape),
                  _full_spec(b1.shape)],
        out_specs=[narrow, _row_spec(tm, LANES), narrow, wide, wide, wide],
        out_shape=[SDS((t, d), F32), SDS((t, LANES), F32), SDS((t, d), BF16)] + [SDS((t, f), BF16)] * 3,
        compiler_params=_params(("parallel",)),
    )(r, a, x2, w_out, wt_gate, wt_up, g1, b1)


def _ffn_down_ln2_loss(act, dact_dg, dact_du, h1b, p2, xhat1, target, w_down, w_pg, wt_pe, g1, b1, g2, b2):
    t, d = xhat1.shape
    f = act.shape[1]
    pdim = p2.shape[1]
    tm = _row_tile(t, EPILOGUE_ROWS)

    def body(act_ref, fg_ref, fu_ref, hb_ref, p_ref, xh1_ref, tgt_ref, wd_ref, wpg_ref, wpe_ref, g1_ref, b1_ref,
             g2_ref, b2_ref, dz_ref, dzb_ref, ds_ref, dple_ref, dg_ref, du_ref, acc_ref):
        @pl.when(pl.program_id(0) == 0)
        def _():
            acc_ref[...] = jnp.zeros_like(acc_ref)

        for lo, hi in _sub_rows(tm):
            h1 = xh1_ref[lo:hi, :] * g1_ref[...] + b1_ref[...]
            pg = _sigmoid(_dot_nn(hb_ref[lo:hi, :], wpg_ref[...]))
            ple = _dot_nt(p_ref[lo:hi, :].astype(BF16), wpe_ref[...])
            gated = pg * ple
            dgate = gated * (1.0 - pg)
            ffn = _dot_nn(act_ref[lo:hi, :], wd_ref[...])
            z2 = ALPHA * h1 + gated + ffn
            xhat2, rstd2 = _layer_norm_stats(z2)
            err = xhat2 * g2_ref[...] + b2_ref[...] - tgt_ref[lo:hi, :]
            dy = err * (1.0 / d)
            dz = _layer_norm_bwd(dy * g2_ref[...], xhat2, rstd2)
            dzb = dz.astype(BF16)
            dz_ref[lo:hi, :] = dz
            dzb_ref[lo:hi, :] = dzb
            ds_ref[lo:hi, :] = (dz * dgate).astype(BF16)
            dple_ref[lo:hi, :] = (dz * pg).astype(BF16)
            acc_ref[0:1, :] += jnp.sum(err * err, axis=0, keepdims=True)
            acc_ref[1:2, :] += jnp.sum(dy * xhat2, axis=0, keepdims=True)
            acc_ref[2:3, :] += jnp.sum(dy, axis=0, keepdims=True)
            for c0, c1 in _col_halves(f):
                da = _dot_nt(dzb, wd_ref[c0:c1, :])
                dg_ref[lo:hi, c0:c1] = (da * fg_ref[lo:hi, c0:c1].astype(F32)).astype(BF16)
                du_ref[lo:hi, c0:c1] = (da * fu_ref[lo:hi, c0:c1].astype(F32)).astype(BF16)

    vec = _full_spec(g1.shape)
    wide, narrow = _row_spec(tm, f), _row_spec(tm, d)
    return pl.pallas_call(
        body, name="ffn_down_ln2_loss", grid=(t // tm,),
        in_specs=[wide, wide, wide, narrow, _row_spec(tm, pdim), narrow, narrow,
                  _full_spec(w_down.shape), _full_spec(w_pg.shape), _full_spec(wt_pe.shape), vec, vec, vec, vec],
        out_specs=[narrow] * 4 + [wide, wide, _acc_spec((8, d))],
        out_shape=[SDS((t, d), F32), SDS((t, d), BF16), SDS((t, d), BF16), SDS((t, d), BF16),
                   SDS((t, f), BF16), SDS((t, f), BF16), SDS((8, d), F32)],
        compiler_params=_params(("arbitrary",)),
    )(act, dact_dg, dact_du, h1b, p2, xhat1, target, w_down, w_pg, wt_pe, g1, b1, g2, b2)


def _after(after, body):
    k = len(after)
    return (lambda *refs: body(*refs[k:])), [ANY_SPEC] * k


def _resident_spec(shape):
    return pl.BlockSpec(shape, lambda i: (0,) * len(shape), pipeline_mode=pl.Buffered(1))


def _dh1_ln1_bwd(dz2, dg, dup, dsb, xhat1, rstd1, wt_gate, wt_up, w_pg, w_out, g1, after=()):
    t, d = dz2.shape
    f = dg.shape[1]
    tm = _row_tile(t, MATMUL_ROWS)

    def body(dz_ref, dg_ref, du_ref, ds_ref, xh1_ref, rs1_ref, wg_ref, wu_ref, wpg_ref, wo_ref, g1_ref,
             dz1_ref, dz1b_ref, dr_ref, da_ref, acc_ref):
        @pl.when(pl.program_id(0) == 0)
        def _():
            acc_ref[...] = jnp.zeros_like(acc_ref)

        for lo, hi in _sub_rows(tm):
            dh = (ALPHA * dz_ref[lo:hi, :] + _dot_nn(dg_ref[lo:hi, :], wg_ref[...])
                  + _dot_nn(du_ref[lo:hi, :], wu_ref[...]) + _dot_nt(ds_ref[lo:hi, :], wpg_ref[...]))
            xhat, rstd = xh1_ref[lo:hi, :], rs1_ref[lo:hi, 0:1]
            dz1 = _layer_norm_bwd(dh * g1_ref[...], xhat, rstd)
            dz1b = dz1.astype(BF16)
            dz1_ref[lo:hi, :] = dz1
            dz1b_ref[lo:hi, :] = dz1b
            acc_ref[0:1, :] += jnp.sum(dh * xhat, axis=0, keepdims=True)
            acc_ref[1:2, :] += jnp.sum(dh, axis=0, keepdims=True)
            dr_ref[lo:hi, :] = _dot_nt(dz1b, wo_ref[0:RET_W, :]).astype(BF16)
            da_ref[lo:hi, :] = _dot_nt(dz1b, wo_ref[RET_W:RET_W + ATT_W, :]).astype(BF16)

    body, lead = _after(after, body)
    return pl.pallas_call(
        body, name="dh1_ln1_bwd", grid=(t // tm,),
        in_specs=lead + [_row_spec(tm, d), _row_spec(tm, f), _row_spec(tm, f), _row_spec(tm, d), _row_spec(tm, d),
                         _row_spec(tm, LANES), _resident_spec(wt_gate.shape), _resident_spec(wt_up.shape), _resident_spec(w_pg.shape),
                         _resident_spec(w_out.shape), _full_spec(g1.shape)],
        out_specs=[_row_spec(tm, d), _row_spec(tm, d), _row_spec(tm, RET_W), _row_spec(tm, ATT_W), _acc_spec((8, d))],
        out_shape=[SDS((t, d), F32), SDS((t, d), BF16), SDS((t, RET_W), BF16), SDS((t, ATT_W), BF16),
                   SDS((8, d), F32)],
        compiler_params=_params(("arbitrary",)),
    )(*after, dz2, dg, dup, dsb, xhat1, rstd1, wt_gate, wt_up, w_pg, w_out, g1)


def _in_proj_bwd(dz1, parts, wt_in, after=()):
    t, d = dz1.shape
    tm = _row_tile(t, MATMUL_ROWS)
    widths = [p.shape[1] for p in parts]

    def body(*refs):
        dz_ref, part_refs, w_ref, dx_ref = refs[0], refs[1:1 + len(parts)], refs[-2], refs[-1]
        acc = ALPHA * dz_ref[...]
        lo = 0
        for p_ref, w in zip(part_refs, widths):
            acc = acc + _dot_nn(p_ref[...], w_ref[lo:lo + w, :])
            lo += w
        dx_ref[...] = acc

    body, lead = _after(after, body)
    return pl.pallas_call(
        body, name="in_proj_bwd", grid=(t // tm,),
        in_specs=lead + [_row_spec(tm, d)] + [_row_spec(tm, w) for w in widths] + [_full_spec(wt_in.shape)],
        out_specs=_row_spec(tm, d), out_shape=SDS((t, d), F32),
        compiler_params=_params(("parallel",)),
    )(*after, dz1, *parts, wt_in)


def _weight_grad(name, me, parts, rhs, after=()):
    t, n = rhs.shape
    widths = [p.shape[1] for p in parts]
    rows = sum(widths)
    own_rows = rows // N_DEV
    tk = _row_tile(t, MATMUL_ROWS)
    n_steps = t // tk
    step = 256

    def body(*refs):
        me_ref, part_refs, rhs_ref = refs[0], refs[1:1 + len(parts)], refs[1 + len(parts)]
        full_ref, own_ref, acc = refs[-3], refs[-2], refs[-1]
        i = pl.program_id(0)

        def products(first):
            b = rhs_ref[...].astype(BF16)
            lo = 0
            for p_ref, w in zip(part_refs, widths):
                for c0 in range(0, w, step):
                    c1 = min(c0 + step, w)
                    val = _dot_tn(p_ref[:, c0:c1].astype(BF16), b)
                    if first:
                        acc[lo + c0:lo + c1, :] = val
                    else:
                        acc[lo + c0:lo + c1, :] += val
                lo += w

        pl.when(i == 0)(functools.partial(products, True))
        pl.when(i > 0)(functools.partial(products, False))

        @pl.when(i == n_steps - 1)
        def _():
            full_ref[...] = acc[...].astype(BF16)
            own_ref[...] = acc[pl.ds(pl.multiple_of(me_ref[0] * own_rows, 8), own_rows), :]

    body, lead = _after(after, body)
    return pl.pallas_call(
        body, name=name, grid=(n_steps,),
        in_specs=lead + [_smem_spec()] + [_row_spec(tk, w) for w in widths] + [_row_spec(tk, n)],
        out_specs=[_full_spec((rows, n)), _full_spec((own_rows, n))],
        out_shape=[SDS((rows, n), BF16), SDS((own_rows, n), F32)],
        scratch_shapes=[pltpu.VMEM((rows, n), F32)],
        compiler_params=_params(("arbitrary",)),
    )(*after, me, *parts, rhs)


def _weight_grad_jobs(name, me, jobs, after=()):
    count = len(jobs)
    t = jobs[0][0].shape[0]
    tk = _row_tile(t, MATMUL_ROWS)
    n_steps = t // tk
    shapes = [(lhs.shape[1], rhs.shape[1]) for lhs, rhs in jobs]
    most_rows, most_cols = max(r for r, _ in shapes), max(n for _, n in shapes)
    step = 256

    def body(*refs):
        me_ref, lhs_refs, rhs_refs = refs[0], refs[1:1 + count], refs[1 + count:1 + 2 * count]
        full_refs, own_refs = refs[1 + 2 * count:1 + 3 * count], refs[1 + 3 * count:1 + 4 * count]
        acc, whole, mine, sems = refs[1 + 4 * count:]
        job, i = pl.program_id(0), pl.program_id(1)

        def leaving(j):
            rows, n = shapes[j]
            return (pltpu.make_async_copy(whole.at[0:rows, 0:n], full_refs[j], sems.at[0]),
                    pltpu.make_async_copy(mine.at[0:rows // N_DEV, 0:n], own_refs[j], sems.at[1]))

        def products(j, first):
            rows, n = shapes[j]
            b = rhs_refs[j][...].astype(BF16)
            for c0 in range(0, rows, step):
                c1 = min(c0 + step, rows)
                val = _dot_tn(lhs_refs[j][:, c0:c1].astype(BF16), b)
                if first:
                    acc[c0:c1, 0:n] = val
                else:
                    acc[c0:c1, 0:n] += val

        def finish(j):
            rows, n = shapes[j]
            own_rows = rows // N_DEV
            if j > 0:
                for cp in leaving(j - 1):
                    cp.wait()
            whole[0:rows, 0:n] = acc[0:rows, 0:n].astype(BF16)
            mine[0:own_rows, 0:n] = acc[pl.ds(pl.multiple_of(me_ref[0] * own_rows, 8), own_rows), 0:n]
            for cp in leaving(j):
                cp.start()
            if j == count - 1:
                for cp in leaving(j):
                    cp.wait()

        for j in range(count):
            pl.when((job == j) & (i == 0))(functools.partial(products, j, True))
            pl.when((job == j) & (i > 0))(functools.partial(products, j, False))
            pl.when((job == j) & (i == n_steps - 1))(functools.partial(finish, j))

    def turn(j):
        return lambda job, i: (jnp.where(job == j, i, jnp.where(job < j, 0, n_steps - 1)), 0)

    body, lead = _after(after, body)
    res = pl.pallas_call(
        body, name=name, grid=(count, n_steps),
        in_specs=lead + [_smem_spec()] + [pl.BlockSpec((tk, rows), turn(j)) for j, (rows, _) in enumerate(shapes)]
        + [pl.BlockSpec((tk, n), turn(j)) for j, (_, n) in enumerate(shapes)],
        out_specs=[ANY_SPEC] * (2 * count),
        out_shape=[SDS((rows, n), BF16) for rows, n in shapes] + [SDS((rows // N_DEV, n), F32) for rows, n in shapes],
        scratch_shapes=[pltpu.VMEM((most_rows, most_cols), F32), pltpu.VMEM((most_rows, most_cols), BF16),
                        pltpu.VMEM((most_rows // N_DEV, most_cols), F32), pltpu.SemaphoreType.DMA((2,))],
        compiler_params=_params(("arbitrary", "arbitrary")),
    )(*after, me, *[lhs for lhs, _ in jobs], *[rhs for _, rhs in jobs])
    return list(res[:count]), list(res[count:])


def _log_decay(decay_f, decay_b):
    def body(f_ref, b_ref, lf_ref, lb_ref):
        lf_ref[...] = jnp.log1p(-jnp.exp2(f_ref[...]))
        lb_ref[...] = jnp.log1p(-jnp.exp2(b_ref[...]))

    return pl.pallas_call(body, name="log_decay", out_shape=[SDS(decay_f.shape, F32)] * 2)(decay_f, decay_b)


def _chunk(ref, n):
    return ref[pl.ds(pl.multiple_of(n * CHUNK, CHUNK), CHUNK), :]


def _group_sum(is_a, v):
    sa = jnp.sum(jnp.where(is_a, v, 0.0), axis=1, keepdims=True)
    sb = jnp.sum(jnp.where(is_a, 0.0, v), axis=1, keepdims=True)
    return jnp.where(is_a, sa, sb)


def _seq_spec(s, col_block):
    return pl.BlockSpec((s, LANES), lambda b, h: (b, col_block + h))


def _smem_spec():
    return pl.BlockSpec(memory_space=pltpu.SMEM)


RET_UNROLL = 4
BWD_MAIN_UNROLL = 8
FWD_PREP_UNROLL = 16
BWD_PREP_UNROLL = 8


def _chunk_loop(n_chunks, body, init, unroll):
    u = unroll if n_chunks % unroll == 0 else 1

    def trip(i, carry):
        for j in range(u):
            carry = body(i * u + j, carry)
        return carry

    return lax.fori_loop(0, n_chunks // u, trip, init)


def _stacked_tables(lgf_ref, lgb_ref, pair):
    lane = lax.broadcasted_iota(jnp.int32, (1, LANES), 1)
    is_a = lane < HEAD_DIM
    lgf = jnp.where(is_a, lgf_ref[2 * pair], lgf_ref[2 * pair + 1])
    lgb = jnp.where(is_a, lgb_ref[2 * pair], lgb_ref[2 * pair + 1])
    row = lax.broadcasted_iota(jnp.int32, (CHUNK, 1), 0).astype(F32)
    kdec_f, qdec_f = jnp.exp(lgf * (CHUNK - 1.0 - row)), jnp.exp(lgf * (row + 1.0))
    kdec_b, qdec_b = jnp.exp(lgb * row), jnp.exp(lgb * (CHUNK - row))
    tab = dict(
        is_a=is_a, row=row, lam_f=jnp.exp(lgf * CHUNK), lam_b=jnp.exp(lgb * CHUNK),
        kdec=jnp.concatenate([kdec_f, kdec_b], axis=1), qdec=jnp.concatenate([qdec_f, qdec_b], axis=1),
        qexp=jnp.concatenate([jnp.broadcast_to(row + 1.0, (CHUNK, LANES)),
                              jnp.broadcast_to(CHUNK - row, (CHUNK, LANES))], axis=1),
        kexp=jnp.concatenate([jnp.broadcast_to(CHUNK - 1.0 - row, (CHUNK, LANES)),
                              jnp.broadcast_to(row, (CHUNK, LANES))], axis=1),
    )
    r = lax.broadcasted_iota(jnp.int32, (2 * LANES, LANES), 0)
    c = lax.broadcasted_iota(jnp.int32, (2 * LANES, LANES), 1)
    tab["diag2"] = ((r & (LANES - 1)) < HEAD_DIM) == (c < HEAD_DIM)
    i2 = lax.broadcasted_iota(jnp.int32, (2 * CHUNK, CHUNK), 0)
    j = lax.broadcasted_iota(jnp.int32, (2 * CHUNK, CHUNK), 1)
    head_b = i2 >= CHUNK
    diff = ((i2 & (CHUNK - 1)) - j).astype(F32)
    up, dn = jnp.maximum(diff, 0.0), jnp.maximum(-diff, 0.0)
    lgf2 = jnp.where(head_b, lgf_ref[2 * pair + 1], lgf_ref[2 * pair])
    lgb2 = jnp.where(head_b, lgb_ref[2 * pair + 1], lgb_ref[2 * pair])
    ef = jnp.where(diff >= 0, jnp.exp(lgf2 * up), 0.0)
    eb = jnp.where(diff <= 0, jnp.exp(lgb2 * dn), 0.0)
    tab["d2"] = ef + eb
    tab["df2"] = ef * up
    tab["db2"] = eb * dn
    return tab


def _stack_pair(is_a, x):
    zero = jnp.zeros_like(x)
    return jnp.concatenate([jnp.where(is_a, x, zero), jnp.where(is_a, zero, x)], axis=0)


def _unstack_pair(is_a, x2):
    return jnp.where(is_a, x2[0:CHUNK, :], x2[CHUNK:2 * CHUNK, :])


def _both_ways(x, dec):
    return (jnp.concatenate([x, x], axis=1) * dec).astype(BF16)


def _scan_states(n_chunks, st, up_rows, up_lam, down_rows, down_lam):
    zero = jnp.zeros((LANES, LANES), F32)

    def up(n, r):
        new = st[n, up_rows, :]
        st[n, up_rows, :] = r
        return r * up_lam + new

    def down(s, r):
        n = n_chunks - 1 - s
        new = st[n, down_rows, :]
        st[n, down_rows, :] = r
        return r * down_lam + new

    lax.fori_loop(0, n_chunks, up, zero)
    lax.fori_loop(0, n_chunks, down, zero)


FWD_ROWS, BWD_ROWS = pl.ds(0, LANES), pl.ds(LANES, LANES)


def _state_spec(n_chunks, pairs):
    return pl.BlockSpec((n_chunks, 2 * LANES, LANES), lambda b, h: (b * pairs + h, 0, 0))


ST_GAIN, ST_XF, ST_XB, ST_IFA, ST_IFB, ST_IBA, ST_IBB, ST_LF, ST_LB = 0, 1, 2, 3, 4, 5, 6, 8, 9
ST_ROWS = 16


GW = GROUP * HEAD_DIM
KEYS = 3 * BLOCK


def _attn_tables(g, bias_ref):
    r = lax.broadcasted_iota(jnp.int32, (GROUP * BLOCK, KEYS), 0)
    kj = lax.broadcasted_iota(jnp.int32, (GROUP * BLOCK, KEYS), 1)
    qi = r & (BLOCK - 1)
    hh = lax.shift_right_logical(r, 7)
    dist = jnp.abs(kj - BLOCK - qi)
    slope = jnp.exp2(-(GROUP * g + hh + 1).astype(F32) * (8.0 / ATTN_HEADS))
    inside = jnp.where(dist <= BLOCK, -slope * dist.astype(F32), NEG_INF)
    bias_ref[BIAS_INSIDE] = inside
    bias_ref[BIAS_FIRST] = jnp.where(kj >= BLOCK, inside, NEG_INF)
    bias_ref[BIAS_LAST] = jnp.where(kj < 2 * BLOCK, inside, NEG_INF)


BIAS_INSIDE, BIAS_FIRST, BIAS_LAST = 0, 1, 2


def _own_lanes(g):
    return lax.shift_right_logical(lax.broadcasted_iota(jnp.int32, (1, LANES), 1), 6) == g


def _mask_keys(x_ref, g, scale, pad_ref, s):
    pad_ref[0:BLOCK, :] = jnp.zeros((BLOCK, LANES), BF16)
    pad_ref[BLOCK + s:2 * BLOCK + s, :] = jnp.zeros((BLOCK, LANES), BF16)
    pad_ref[BLOCK:BLOCK + s, :] = jnp.where(_own_lanes(g), x_ref[...].astype(F32) * scale, 0.0).astype(BF16)


def _lane_block(x, j):
    return x[:, j * LANES:(j + 1) * LANES]


def _stack_heads(x, g):
    assert GROUP == 4 and GW == 2 * LANES
    x1 = pltpu.roll(x, HEAD_DIM, 1)
    keep = _own_lanes(g)
    zero = jnp.zeros((BLOCK, LANES), x.dtype)
    rows = []
    for h in range(GROUP):
        for_g0 = _lane_block(x, h // 2) if h % 2 == 0 else _lane_block(x1, ((h + 1) // 2) % 2)
        for_g1 = _lane_block(x, h // 2) if h % 2 == 1 else _lane_block(x1, h // 2)
        rows.append(jnp.where(keep, jnp.where(g == 0, for_g0, for_g1), zero))
    return jnp.concatenate(rows, axis=0)


def _unstack_heads(x4, g):
    p = [x4[h * BLOCK:(h + 1) * BLOCK, :] for h in range(GROUP)]
    cat = lambda a, b: jnp.concatenate([a, b], axis=1)
    in_place = jnp.where(g == 0, cat(p[0], p[2]), cat(p[1], p[3]))
    one_left = jnp.where(g == 0, cat(p[1], p[3]), cat(p[2], p[0]))
    return in_place + pltpu.roll(one_left, HEAD_DIM, 1)


def _sink_column(sink_ref, g):
    rh = lax.shift_right_logical(lax.broadcasted_iota(jnp.int32, (GROUP * BLOCK, 1), 0), 7)
    col = jnp.zeros((GROUP * BLOCK, 1), F32)
    for h in range(GROUP):
        col = jnp.where(rh == h, sink_ref[GROUP * g + h], col)
    return col


def _attn_probs(qm, k3, bias_ref, sink_col, n, s):
    which = jnp.where(n == 0, BIAS_FIRST, jnp.where(n == s // BLOCK - 1, BIAS_LAST, BIAS_INSIDE))
    logits = _dot_nt(qm, k3) + bias_ref[which]
    m = jnp.maximum(jnp.max(logits, axis=1, keepdims=True), sink_col)
    e = jnp.exp(logits - m)
    e_sink = jnp.exp(sink_col - m)
    inv = 1.0 / (jnp.sum(e, axis=1, keepdims=True) + e_sink)
    return e * inv, e_sink * inv


PAIRS_PER_KV = (RET_HEADS // 2) // KV_HEADS
FWD_ORDER = "rrarra"
BWD_ORDER = "rrarar"


def _trip_order(order, chunks, blocks):
    if order.count("r") == chunks and order.count("a") == blocks:
        return order
    return "r" * chunks + "a" * blocks


def _mixers_fwd(u, lgf, lgb, gn_gain, sink, b_loc, after=()):
    t = u.shape[0]
    s = t // b_loc
    n_chunks = s // CHUNK
    pairs = RET_HEADS // 2
    trips = n_chunks // RET_UNROLL
    blocks_half = (s // BLOCK) // PAIRS_PER_KV
    per_trip = blocks_half // trips
    assert n_chunks % RET_UNROLL == 0 and blocks_half % trips == 0 and PAIRS_PER_KV == 2 and s >= 2 * BLOCK

    def body(lgf_ref, lgb_ref, sink_ref, q_ref, k_ref, v_ref, g_ref, gain_ref, aq_ref, ak_ref, av_ref,
             r_ref, xhat_ref, rstd_ref, a_ref, st, kpad, vpad, bias):
        pair = pl.program_id(1)
        g, half = lax.shift_right_logical(pair, 1), pair & 1
        tab = _stacked_tables(lgf_ref, lgb_ref, pair)
        is_a = tab["is_a"]

        @pl.when(half == 0)
        def _():
            _attn_tables(g, bias)
            _mask_keys(ak_ref, g, Q_SCALE, kpad, s)
            _mask_keys(av_ref, g, 1.0, vpad, s)

        def kv_body(n, _):
            k8 = _chunk(k_ref, n).astype(F32) * Q_SCALE
            st[n] = jnp.where(tab["diag2"], _dot_tn(_both_ways(k8, tab["kdec"]), _chunk(v_ref, n)), 0.0)
            return 0

        _chunk_loop(n_chunks, kv_body, 0, FWD_PREP_UNROLL)
        _scan_states(n_chunks, st, FWD_ROWS, tab["lam_f"], BWD_ROWS, tab["lam_b"])
        sink_col = _sink_column(sink_ref, g)

        def retention_chunk(n):
            q = _chunk(q_ref, n)
            k8 = (_chunk(k_ref, n).astype(F32) * Q_SCALE).astype(BF16)
            v = _chunk(v_ref, n)
            p2 = (_dot_nt(_stack_pair(is_a, q), k8) * tab["d2"]).astype(BF16)
            y = _unstack_pair(is_a, _dot_nn(p2, v))
            y = y + _dot_nn(_both_ways(q.astype(F32), tab["qdec"]), st[n].astype(BF16))
            rows = pl.ds(pl.multiple_of(n * CHUNK, CHUNK), CHUNK)
            mu = _group_sum(is_a, y) * (1.0 / HEAD_DIM)
            dlt = y - mu
            var = _group_sum(is_a, dlt * dlt) * (1.0 / HEAD_DIM)
            rstd = lax.rsqrt(var + GN_EPS)
            xhat = dlt * rstd
            xhat_ref[rows, :] = xhat
            rstd_ref[rows, :] = rstd
            gate = _chunk(g_ref, n).astype(F32)
            r_ref[rows, :] = (xhat * gain_ref[...] * gate * _sigmoid(gate)).astype(BF16)

        def attention_block(blk):
            n = half * blocks_half + blk
            rows = pl.ds(pl.multiple_of(blk * BLOCK, BLOCK), BLOCK)
            keys = pl.ds(pl.multiple_of(n * BLOCK, BLOCK), KEYS)
            p, _ = _attn_probs(_stack_heads(aq_ref[rows, :], g), kpad[keys, :], bias, sink_col, n, s)
            a_ref[rows, :] = _unstack_heads(_dot_nn(p.astype(BF16), vpad[keys, :]), g).astype(BF16)

        def trip(i, _):
            chunk, blk = 0, 0
            for kind in _trip_order(FWD_ORDER, RET_UNROLL, per_trip):
                if kind == "r":
                    retention_chunk(i * RET_UNROLL + chunk)
                    chunk += 1
                else:
                    attention_block(i * per_trip + blk)
                    blk += 1
            return 0

        lax.fori_loop(0, trips, trip, 0)

    lane_blk = lambda c0: _seq_spec(s, c0 // LANES)
    half_rows = blocks_half * BLOCK
    aq_spec = pl.BlockSpec((half_rows, GW), lambda b, h: (b * PAIRS_PER_KV + (h & 1), C_AQ // GW + h // 2))
    a_spec = pl.BlockSpec((half_rows, GW), lambda b, h: (b * PAIRS_PER_KV + (h & 1), h // 2))
    kv_spec = lambda c0: pl.BlockSpec((s, LANES), lambda b, h: (b, c0 // LANES))
    pad = pltpu.VMEM((s + 2 * BLOCK, LANES), BF16)
    body, lead = _after(after, body)
    return pl.pallas_call(
        body, name="mixers_fwd", grid=(b_loc, pairs),
        in_specs=lead + [_smem_spec(), _smem_spec(), _smem_spec(), lane_blk(C_RQ), lane_blk(C_RK), lane_blk(C_RV),
                         lane_blk(C_RG), pl.BlockSpec((1, LANES), lambda b, h: (0, h)), aq_spec, kv_spec(C_AK),
                         kv_spec(C_AV)],
        out_specs=[_seq_spec(s, 0), _seq_spec(s, 0), _seq_spec(s, 0), a_spec, _state_spec(n_chunks, pairs)],
        out_shape=[SDS((t, RET_W), BF16), SDS((t, RET_W), F32), SDS((t, RET_W), F32), SDS((t, ATT_W), BF16),
                   SDS((b_loc * pairs * n_chunks, 2 * LANES, LANES), F32)],
        scratch_shapes=[pad, pad, pltpu.VMEM((3, GROUP * BLOCK, KEYS), F32)],
        compiler_params=_params(("arbitrary", "arbitrary")),
    )(*after, lgf, lgb, sink, u, u, u, u, gn_gain, u, u, u)


def _mixers_bwd(u, xhat, rstd, states, dr, da, lgf, lgb, gn_gain, sink, b_loc, after=()):
    t = u.shape[0]
    s = t // b_loc
    n_chunks = s // CHUNK
    pairs = RET_HEADS // 2
    unroll = BWD_MAIN_UNROLL if n_chunks % BWD_MAIN_UNROLL == 0 else RET_UNROLL
    trips = n_chunks // unroll
    blocks_half = (s // BLOCK) // PAIRS_PER_KV
    per_trip = blocks_half // trips
    assert n_chunks % unroll == 0 and blocks_half % trips == 0 and PAIRS_PER_KV == 2 and s >= 2 * BLOCK

    def body(lgf_ref, lgb_ref, sink_ref, q_ref, k_ref, v_ref, g_ref, xhat_ref, rstd_ref, dr_ref, gain_ref,
             aq_ref, ak_ref, av_ref, do_ref, st,
             dq_ref, dk_ref, dv_ref, dg_ref, st_ref, daq_ref, dak_ref, dav_ref, dsink_ref,
             gr, dy_s, kpad, vpad, bias, dk_acc, dv_acc):
        pair = pl.program_id(1)
        g, half = lax.shift_right_logical(pair, 1), pair & 1
        tab = _stacked_tables(lgf_ref, lgb_ref, pair)
        is_a = tab["is_a"]
        gain = gain_ref[...]

        @pl.when(half == 0)
        def _():
            _attn_tables(g, bias)
            _mask_keys(ak_ref, g, Q_SCALE, kpad, s)
            _mask_keys(av_ref, g, 1.0, vpad, s)
            dsink_ref[...] = jnp.zeros_like(dsink_ref)

        @pl.when(pair == 0)
        def _():
            dk_acc[...] = jnp.zeros_like(dk_acc)
            dv_acc[...] = jnp.zeros_like(dv_acc)

        def norm_body(n, dgain):
            rows = pl.ds(pl.multiple_of(n * CHUNK, CHUNK), CHUNK)
            xhat, rstd = xhat_ref[rows, :], rstd_ref[rows, :]
            gate = g_ref[rows, :].astype(F32)
            sg = _sigmoid(gate)
            silu = gate * sg
            d_out = dr_ref[rows, :].astype(F32)
            dg_ref[rows, :] = (d_out * xhat * gain * (sg * (1.0 + gate * (1.0 - sg)))).astype(BF16)
            dxh = d_out * gain * silu
            m1 = _group_sum(is_a, dxh) * (1.0 / HEAD_DIM)
            m2 = _group_sum(is_a, dxh * xhat) * (1.0 / HEAD_DIM)
            dy = (rstd * (dxh - m1 - xhat * m2)).astype(BF16)
            dy_s[rows, :] = dy
            qf = q_ref[rows, :].astype(F32)
            gr[n] = jnp.where(tab["diag2"], _dot_tn(_both_ways(qf, tab["qdec"]), dy), 0.0)
            return dgain + jnp.sum(d_out * xhat * silu, axis=0, keepdims=True)

        colsum = lambda x: jnp.sum(x, axis=0, keepdims=True)

        def grad_body(n, carry):
            xfb, ifa, ifb, iba, ibb, lf, lb = carry
            rows = pl.ds(pl.multiple_of(n * CHUNK, CHUNK), CHUNK)
            q = q_ref[rows, :]
            qf = q.astype(F32)
            k8f = k_ref[rows, :].astype(F32) * Q_SCALE
            k8 = k8f.astype(BF16)
            v = v_ref[rows, :]
            dy = dy_s[rows, :]
            q2, dy2 = _stack_pair(is_a, q), _stack_pair(is_a, dy)
            sc = _dot_nt(q2, k8)
            dp = _dot_nt(dy2, v)
            a2 = (sc * tab["d2"]).astype(BF16)
            ds2 = (dp * tab["d2"]).astype(BF16)
            dq = _unstack_pair(is_a, _dot_nn(ds2, k8))
            dk = _dot_tn(ds2, q2)
            dv = _dot_tn(a2, dy2)
            prod = sc * dp
            pf, pb = prod * tab["df2"], prod * tab["db2"]
            ifa, ifb = ifa + colsum(pf[0:CHUNK, :]), ifb + colsum(pf[CHUNK:2 * CHUNK, :])
            iba, ibb = iba + colsum(pb[0:CHUNK, :]), ibb + colsum(pb[CHUNK:2 * CHUNK, :])
            states, sgrads = st[n], gr[n]
            sb, gb = states.astype(BF16), sgrads.astype(BF16)
            dqc = _dot_nt(dy, sb) * tab["qdec"]
            dkc = _dot_nt(v, gb) * tab["kdec"]
            dv = dv + _dot_nn(_both_ways(k8f, tab["kdec"]), gb)
            dq_ref[rows, :] = (dq + dqc[:, 0:LANES] + dqc[:, LANES:2 * LANES]).astype(BF16)
            dk_ref[rows, :] = ((dk + dkc[:, 0:LANES] + dkc[:, LANES:2 * LANES]) * Q_SCALE).astype(BF16)
            dv_ref[rows, :] = dv.astype(BF16)
            q2w, k2w = jnp.concatenate([qf, qf], axis=1), jnp.concatenate([k8f, k8f], axis=1)
            xfb = xfb + colsum(tab["qexp"] * q2w * dqc + tab["kexp"] * k2w * dkc)
            prod_s = sgrads * states
            lf, lb = lf + colsum(prod_s[0:LANES, :]), lb + colsum(prod_s[LANES:2 * LANES, :])
            return xfb, ifa, ifb, iba, ibb, lf, lb

        sink_col = _sink_column(sink_ref, g)
        head_row = lax.broadcasted_iota(jnp.int32, dsink_ref.shape, 0)

        def attention_block(blk):
            n = half * blocks_half + blk
            rows = pl.ds(pl.multiple_of(blk * BLOCK, BLOCK), BLOCK)
            keys = pl.ds(pl.multiple_of(n * BLOCK, BLOCK), KEYS)
            qm = _stack_heads(aq_ref[rows, :], g)
            k3, v3 = kpad[keys, :], vpad[keys, :]
            p, p_sink = _attn_probs(qm, k3, bias, sink_col, n, s)
            dom = _stack_heads(do_ref[rows, :], g)
            dp = _dot_nt(dom, v3)
            delta = jnp.sum(p * dp, axis=1, keepdims=True)
            ds_mat = (p * (dp - delta)).astype(BF16)
            daq_ref[rows, :] = _unstack_heads(_dot_nn(ds_mat, k3), g).astype(BF16)
            dk_acc[keys, :] += _dot_tn(ds_mat, qm) * Q_SCALE
            dv_acc[keys, :] += _dot_tn(p.astype(BF16), dom)
            w = p_sink * delta
            upd = jnp.zeros(dsink_ref.shape, F32)
            for h in range(GROUP):
                upd = upd + jnp.where(head_row == h, -jnp.sum(w[h * BLOCK:(h + 1) * BLOCK, :]), 0.0)
            dsink_ref[...] += upd

        dgain = _chunk_loop(n_chunks, norm_body, jnp.zeros((1, LANES), F32), BWD_PREP_UNROLL)
        _scan_states(n_chunks, gr, BWD_ROWS, tab["lam_b"], FWD_ROWS, tab["lam_f"])

        def trip(i, carry):
            chunk, blk = 0, 0
            for kind in _trip_order(BWD_ORDER * (unroll // RET_UNROLL), unroll, per_trip):
                if kind == "r":
                    carry = grad_body(i * unroll + chunk, carry)
                    chunk += 1
                else:
                    attention_block(i * per_trip + blk)
                    blk += 1
            return carry

        z = jnp.zeros((1, LANES), F32)
        init = (jnp.zeros((1, 2 * LANES), F32), z, z, z, z, z, z)
        xfb, ifa, ifb, iba, ibb, lf, lb = lax.fori_loop(0, trips, trip, init)
        st_ref[...] = jnp.zeros_like(st_ref)
        st_ref[ST_GAIN:ST_GAIN + 1, :] = dgain
        st_ref[ST_XF:ST_XF + 1, :] = xfb[:, 0:LANES]
        st_ref[ST_XB:ST_XB + 1, :] = xfb[:, LANES:2 * LANES]
        st_ref[ST_IFA:ST_IFA + 1, :] = ifa
        st_ref[ST_IFB:ST_IFB + 1, :] = ifb
        st_ref[ST_IBA:ST_IBA + 1, :] = iba
        st_ref[ST_IBB:ST_IBB + 1, :] = ibb
        st_ref[ST_LF:ST_LF + 1, :] = lf * (CHUNK * tab["lam_f"])
        st_ref[ST_LB:ST_LB + 1, :] = lb * (CHUNK * tab["lam_b"])

        @pl.when(pair == pairs - 1)
        def _():
            dak_ref[...] = dk_acc[BLOCK:BLOCK + s, :].astype(BF16)
            dav_ref[...] = dv_acc[BLOCK:BLOCK + s, :].astype(BF16)

    lane_blk = lambda c0: _seq_spec(s, c0 // LANES)
    seq0 = _seq_spec(s, 0)
    half_rows = blocks_half * BLOCK
    aq_spec = pl.BlockSpec((half_rows, GW), lambda b, h: (b * PAIRS_PER_KV + (h & 1), C_AQ // GW + h // 2))
    a_spec = pl.BlockSpec((half_rows, GW), lambda b, h: (b * PAIRS_PER_KV + (h & 1), h // 2))
    kv_spec = lambda c0: pl.BlockSpec((s, LANES), lambda b, h: (b, c0 // LANES))
    kv_out = pl.BlockSpec((s, LANES), lambda b, h: (b, 0))
    state = pltpu.VMEM((n_chunks, 2 * LANES, LANES), F32)
    pad = pltpu.VMEM((s + 2 * BLOCK, LANES), BF16)
    acc = pltpu.VMEM((s + 2 * BLOCK, LANES), F32)
    body, lead = _after(after, body)
    return pl.pallas_call(
        body, name="mixers_bwd", grid=(b_loc, pairs),
        in_specs=lead + [_smem_spec(), _smem_spec(), _smem_spec(), lane_blk(C_RQ), lane_blk(C_RK), lane_blk(C_RV),
                         lane_blk(C_RG), seq0, seq0, seq0, pl.BlockSpec((1, LANES), lambda b, h: (0, h)),
                         aq_spec, kv_spec(C_AK), kv_spec(C_AV), a_spec, _state_spec(n_chunks, pairs)],
        out_specs=[seq0] * 4 + [pl.BlockSpec((ST_ROWS, LANES), lambda b, h: (b, h)), a_spec, kv_out, kv_out,
                                pl.BlockSpec((8, LANES), lambda b, h: (b * KV_HEADS + h // 2, 0))],
        out_shape=[SDS((t, RET_W), BF16)] * 4 + [SDS((b_loc * ST_ROWS, RET_W), F32), SDS((t, ATT_W), BF16),
                                                   SDS((t, KV_W), BF16), SDS((t, KV_W), BF16),
                                                   SDS((b_loc * KV_HEADS * 8, LANES), F32)],
        scratch_shapes=[state, pltpu.VMEM((s, LANES), BF16), pad, pad,
                        pltpu.VMEM((3, GROUP * BLOCK, KEYS), F32), acc, acc],
        compiler_params=_params(("arbitrary", "arbitrary")),
    )(*after, lgf, lgb, sink, u, u, u, u, xhat, rstd, dr, gn_gain, u, u, u, da, states)


def _pack_small(acc2, acc1, ret_stats, dsink, b_loc, d):
    pairs = RET_HEADS // 2

    def body(acc2_ref, acc1_ref, st_ref, dsink_ref, out_ref):
        out_ref[...] = jnp.zeros_like(out_ref)
        out_ref[ROW_LN1G:ROW_LN1G + 1, :] = acc1_ref[0:1, :]
        out_ref[ROW_LN1B:ROW_LN1B + 1, :] = acc1_ref[1:2, :]
        out_ref[ROW_LN2G:ROW_LN2G + 1, :] = acc2_ref[1:2, :]
        out_ref[ROW_LN2B:ROW_LN2B + 1, :] = acc2_ref[2:3, :]
        out_ref[ROW_LOSS:ROW_LOSS + 1, :] = acc2_ref[0:1, :]
        st = st_ref[0:ST_ROWS, :]
        for b in range(1, b_loc):
            st = st + st_ref[b * ST_ROWS:(b + 1) * ST_ROWS, :]
        out_ref[ROW_GN:ROW_GN + 1, 0:RET_W] = st[ST_GAIN:ST_GAIN + 1, :]
        lane = lax.broadcasted_iota(jnp.int32, (1, d), 1)
        misc = jnp.zeros((1, d), F32)
        for pr in range(pairs):
            blk = st[:, pr * LANES:(pr + 1) * LANES]
            half = lax.broadcasted_iota(jnp.int32, (1, LANES), 1) < HEAD_DIM
            for h in range(2):
                sel = half if h == 0 else jnp.logical_not(half)
                cross_f = jnp.sum(jnp.where(sel, blk[ST_XF:ST_XF + 1, :] + blk[ST_LF:ST_LF + 1, :], 0.0))
                cross_b = jnp.sum(jnp.where(sel, blk[ST_XB:ST_XB + 1, :] + blk[ST_LB:ST_LB + 1, :], 0.0))
                intra_f = jnp.sum(blk[ST_IFA + h:ST_IFA + h + 1, :])
                intra_b = jnp.sum(blk[ST_IBA + h:ST_IBA + h + 1, :])
                head = 2 * pr + h
                misc = jnp.where(lane == MISC_DF + head, cross_f + intra_f, misc)
                misc = jnp.where(lane == MISC_DB + head, cross_b + intra_b, misc)
        for g in range(KV_HEADS):
            tot = dsink_ref[g * 8:(g + 1) * 8, :]
            for b in range(1, b_loc):
                tot = tot + dsink_ref[(b * KV_HEADS + g) * 8:(b * KV_HEADS + g + 1) * 8, :]
            for h in range(GROUP):
                misc = jnp.where(lane == MISC_SINK + GROUP * g + h, jnp.sum(tot[h:h + 1, 0:1]), misc)
        out_ref[ROW_MISC:ROW_MISC + 1, :] = misc

    return pl.pallas_call(body, name="pack_small", out_shape=SDS((SMALL_ROWS, d), F32))(acc2, acc1, ret_stats, dsink)


BIG = ("w_in", "w_out", "w_ffn_gate", "w_ffn_up", "w_ffn_down", "w_ple_proj", "w_ple_gate")
TRANSPOSED_OUTSIDE = ("w_in", "w_ffn_gate", "w_ffn_up")
TRANSPOSED_HERE = ("w_ple_proj",)
SMALL = ("ret_decay_fwd", "ret_decay_bwd", "ret_gn_gain", "attn_sink", "ln1_gain", "ln1_bias", "ln2_gain", "ln2_bias")
ORDER = ("w_in", "ret_decay_fwd", "ret_decay_bwd", "ret_gn_gain", "attn_sink", "w_out", "ln1_gain", "ln1_bias",
         "w_ffn_gate", "w_ffn_up", "w_ffn_down", "w_ple_proj", "w_ple_gate", "ln2_gain", "ln2_bias")


GATHER_ORDER = ("w_in", "w_ffn_up", "w_out", "w_ffn_gate", "w_ple_gate", "w_ple_proj", "w_ffn_down")
GATHER_TWO_LEVEL = ("w_in", "w_ffn_up", "w_out")

def _local_step(x2, p2, target2, fetch, publish, small, b_loc, me):
    d = x2.shape[1]
    lgf, lgb = _log_decay(small["ret_decay_fwd"], small["ret_decay_bwd"])
    lgf1, lgb1, sink1 = lgf.reshape(-1), lgb.reshape(-1), small["attn_sink"].reshape(-1)
    (w_in,) = fetch(("w_in",), ())
    u, xb = _in_proj(x2, w_in)
    passed = fetch.pass_on(("w_ffn_up", "w_out"), (xb, lgf))
    r, ret_xhat, ret_rstd, a, ret_states = _mixers_fwd(u, lgf1, lgb1, small["ret_gn_gain"], sink1, b_loc, passed)
    w_out, w_gate, w_up = fetch(("w_out", "w_ffn_gate", "w_ffn_up"), (r, a))
    xhat1, rstd1, h1b, dact_dg, dact_du, act = _mix_ln1_ffn_up(
        r, a, x2, w_out, w_gate, w_up, small["ln1_gain"], small["ln1_bias"])
    w_pg, w_pe, w_down = fetch(("w_ple_gate", "w_ple_proj", "w_ffn_down"), (act,))
    dz2, dz2b, dsb, dpleb, dg, dup, acc2 = _ffn_down_ln2_loss(
        act, dact_dg, dact_du, h1b, p2, xhat1, target2, w_down, w_pg, w_pe,
        small["ln1_gain"], small["ln1_bias"], small["ln2_gain"], small["ln2_bias"])
    own = {}

    def grad(name, parts, rhs, after=()):
        whole, own[name] = _weight_grad("grad_" + name, me, parts, rhs, after)
        return whole

    ffn_jobs = dict(w_ffn_down=(act, dz2b), w_ple_proj=(dpleb, p2), w_ple_gate=(h1b, dsb),
                    w_ffn_gate=(dg, h1b), w_ffn_up=(dup, h1b))
    wholes, owns = _weight_grad_jobs("grad_w_ffn", me, list(ffn_jobs.values()))
    own.update(zip(ffn_jobs, owns))
    t2 = publish("ffn", dict(zip(ffn_jobs, wholes)))
    dz1, dz1b, dr, da, acc1 = _dh1_ln1_bwd(
        dz2, dg, dup, dsb, xhat1, rstd1, w_gate, w_up, w_pg, w_out, small["ln1_gain"], t2)
    t3 = publish("out", dict(w_out=grad("w_out", [r, a], dz1b)))
    dq, dk, dv, dgate, ret_stats, daq, dak, dav, dsink = _mixers_bwd(
        u, ret_xhat, ret_rstd, ret_states, dr, da, lgf1, lgb1, small["ret_gn_gain"], sink1, b_loc, t3)
    parts = [dq, dk, dv, dgate, daq, dak, dav]
    small_part = _pack_small(acc2, acc1, ret_stats, dsink, b_loc, d)
    t4 = publish("in", dict(w_in=grad("w_in", parts, xb)), small_part)
    grad_x = _in_proj_bwd(dz1, parts, w_in, t4)
    return grad_x, own, small_part


def kernel(x, p, w_in, ret_decay_fwd, ret_decay_bwd, ret_gn_gain, attn_sink, w_out, ln1_gain, ln1_bias, w_ffn_gate, w_ffn_up, w_ffn_down, w_ple_proj, w_ple_gate, ln2_gain, ln2_bias, loss_target, m_w_in, m_ret_decay_fwd, m_ret_decay_bwd, m_ret_gn_gain, m_attn_sink, m_w_out, m_ln1_gain, m_ln1_bias, m_w_ffn_gate, m_w_ffn_up, m_w_ffn_down, m_w_ple_proj, m_w_ple_gate, m_ln2_gain, m_ln2_bias, v_w_in, v_ret_decay_fwd, v_ret_decay_bwd, v_ret_gn_gain, v_attn_sink, v_w_out, v_ln1_gain, v_ln1_bias, v_w_ffn_gate, v_w_ffn_up, v_w_ffn_down, v_w_ple_proj, v_w_ple_gate, v_ln2_gain, v_ln2_bias):
    given = dict(locals())

    def strip(n, a):
        if n not in BIG:
            return a
        return a[0].T if n in TRANSPOSED_OUTSIDE else a[0]

    def restore(n, a):
        if n not in BIG:
            return a
        return (a.T if n in TRANSPOSED_OUTSIDE else a)[None]

    w = {n: strip(n, given[n]) for n in ORDER}
    m = {n: strip(n, given["m_" + n]) for n in ORDER}
    v = {n: strip(n, given["v_" + n]) for n in ORDER}
    b_loc, s, d = x.shape
    x2 = x.reshape(b_loc * s, d)
    p2 = p[0].reshape(b_loc * s, p.shape[-1])
    target2 = loss_target.reshape(b_loc * s, d)

    small = {n: w[n] for n in SMALL}
    me = (4 * lax.axis_index("x") + 2 * lax.axis_index("y") + lax.axis_index("c")).astype(jnp.int32).reshape(1)

    gather = _gather_start(
        {n: w[n] for n in GATHER_ORDER},
        [_gather_copy_near if n in GATHER_TWO_LEVEL else _gather_copy for n in GATHER_ORDER])

    passing = {}

    def pass_on(names, after):
        relayed = _gather_relay("gather_relay_" + names[0], gather, [GATHER_ORDER.index(n) for n in names], list(after))
        passing.update({n: (relayed, j) for j, n in enumerate(names)})
        return (relayed["token"],)

    def fetch(names, after):
        out = {}
        for n in [n for n in names if n in GATHER_TWO_LEVEL]:
            if n not in passing:
                pass_on((n,), after)
            relayed, j = passing[n]
            out[n] = _split_copy_wait("gather_wait_" + n, relayed, [j], list(after))[0][0]
        direct = [n for n in names if n not in GATHER_TWO_LEVEL]
        if direct:
            got = _split_copy_wait("gather_wait_" + direct[0], gather, [GATHER_ORDER.index(n) for n in direct],
                                   list(after))
            out.update({n: item[0] for n, item in zip(direct, got)})
        return [out[n] for n in names]

    scatters = []

    def publish(tag, products, small_sums=None):
        items = [(products[n], lax.empty((N_DEV - 1, products[n].shape[0] // N_DEV, products[n].shape[1]), BF16))
                 for n in products]
        copies = [_scatter_copy] * len(items)
        if small_sums is not None:
            items.append((small_sums, lax.empty((N_DEV - 1,) + small_sums.shape, F32)))
            copies.append(_small_copy)
        started = _split_copy_start("scatter_start_" + tag, items, copies)
        scatters.append((list(products), small_sums is not None, started))
        return (started["token"],)

    fetch.pass_on = pass_on
    grad_x, own, small_part = _local_step(x2, p2, target2, fetch, publish, small, b_loc, me)

    out_g, out_d, out_m, out_v = {}, {}, {}, {}
    after = [grad_x]
    for names, with_small, started in scatters:
        landed = _split_copy_wait("scatter_wait_" + names[0], started, list(range(len(started["items"]))), after)
        if with_small:
            mine, from_peers = landed[-1]
            loss, sg, sd, sm, sv = _small_adamw(
                me, mine, from_peers, small, {n: m[n] for n in SMALL}, {n: v[n] for n in SMALL})
            for dst, src in ((out_g, sg), (out_d, sd), (out_m, sm), (out_v, sv)):
                dst.update(src)
        recv = {n: item[1] for n, item in zip(names, landed)}
        alike = {}
        for n in names:
            alike.setdefault((own[n].shape, n in TRANSPOSED_HERE), []).append(n)
        for (_, transposed), ns in alike.items():
            res = _reduce_adamw(ns[0], [own[n] for n in ns], [recv[n] for n in ns], [w[n] for n in ns],
                                [m[n] for n in ns], [v[n] for n in ns], transposed)
            for dst, vals in zip((out_g, out_d, out_m, out_v), res):
                dst.update(zip(ns, vals))
        after = [out_v[names[-1]]]

    outs = [loss[0, 0], grad_x.reshape(x.shape)]
    for group in (out_g, out_d, out_m, out_v):
        outs += [restore(n, group[n]) for n in ORDER]
    return tuple(outs)
```

```python
import functools

import jax
import jax.numpy as jnp
from jax import lax
from jax.experimental import pallas as pl
from jax.experimental.pallas import tpu as pltpu

F32, BF16 = jnp.float32, jnp.bfloat16
SDS = jax.ShapeDtypeStruct
MESH = pl.DeviceIdType.MESH

N_DEV = 8
HEAD_DIM = 64
RET_HEADS = 8
ATTN_HEADS = 8
KV_HEADS = 2
GROUP = ATTN_HEADS // KV_HEADS
RET_W = RET_HEADS * HEAD_DIM
ATT_W = ATTN_HEADS * HEAD_DIM
KV_W = KV_HEADS * HEAD_DIM
LANES = 128
CHUNK = 128
BLOCK = 128
Q_SCALE = HEAD_DIM ** -0.5
ALPHA = 2.0 ** 0.25
LN_EPS = 1e-5
GN_EPS = 1e-5
NEG_INF = -1e30
C_RQ, C_RK, C_RV, C_RG = 0, RET_W, 2 * RET_W, 3 * RET_W
C_AQ = 4 * RET_W
C_AK = C_AQ + ATT_W
C_AV = C_AK + KV_W
IN_W = C_AV + KV_W

ADAM_LR = 0.001
ADAM_B1 = 0.9
ADAM_B2 = 0.999
ADAM_EPS = 1e-08
ADAM_WD = 0.01
ADAM_STEP = 10

VMEM_LIMIT = 56 * 1024 * 1024
MATMUL_ROWS = 512
EPILOGUE_ROWS = 256
SUB_ROWS = 512
SMALL_ROWS = 16
ROW_LN1G, ROW_LN1B, ROW_LN2G, ROW_LN2B, ROW_LOSS, ROW_GN, ROW_MISC = 0, 1, 2, 3, 4, 5, 6
MISC_DF, MISC_DB, MISC_SINK = 0, 8, 16


def _dot_nn(a, b):
    return lax.dot_general(a, b, (((1,), (0,)), ((), ())), preferred_element_type=F32)


def _dot_nt(a, b):
    return lax.dot_general(a, b, (((1,), (1,)), ((), ())), preferred_element_type=F32)


def _dot_tn(a, b):
    return lax.dot_general(a, b, (((0,), (0,)), ((), ())), preferred_element_type=F32)


def _params(sem=None, vmem=VMEM_LIMIT):
    kw = {"vmem_limit_bytes": vmem}
    if sem is not None:
        kw["dimension_semantics"] = sem
    return pltpu.CompilerParams(**kw)


def _row_tile(t, want=512):
    tm = want
    while t % tm:
        tm //= 2
    return tm


def _sigmoid(x):
    return jax.nn.sigmoid(x)


def _layer_norm_stats(z):
    mu = jnp.mean(z, axis=1, keepdims=True)
    d = z - mu
    var = jnp.mean(d * d, axis=1, keepdims=True)
    rstd = lax.rsqrt(var + LN_EPS)
    return d * rstd, rstd


def _layer_norm_bwd(dxh, xhat, rstd):
    m1 = jnp.mean(dxh, axis=1, keepdims=True)
    m2 = jnp.mean(dxh * xhat, axis=1, keepdims=True)
    return rstd * (dxh - m1 - xhat * m2)


def _mesh_pos():
    return lax.axis_index("x"), lax.axis_index("y"), lax.axis_index("c")


HBM_SPEC = pl.BlockSpec(memory_space=pltpu.HBM)
SEM_SPEC = pl.BlockSpec(memory_space=pltpu.SEMAPHORE)
ANY_SPEC = pl.BlockSpec(memory_space=pl.ANY)
SIDE_EFFECT = pltpu.SideEffectType.DATAFLOW_SIDE_EFFECTING
PEER_SEMS = pltpu.SemaphoreType.DMA((N_DEV - 1,))


def _in_hbm(a):
    return pltpu.with_memory_space_constraint(a, pltpu.HBM)


def _split_copy_start(name, items, copies):
    n = len(items)
    flat = [a for it in items for a in it]
    k = len(flat)

    def body(*refs):
        arr, sems = list(refs[:k]), refs[k:k + 2 * n]
        for i, it in enumerate(items):
            mine = [arr.pop(0) for _ in it]
            for m in range(1, N_DEV):
                cp = copies[i](m, mine, sems[i].at[m - 1], sems[n + i].at[m - 1])
                if cp is not None:
                    cp.start()
        token = refs[-1]
        token[...] = jnp.zeros_like(token)

    res = pl.pallas_call(
        body, name=name,
        out_shape=[PEER_SEMS] * (2 * n) + [pltpu.HBM(a.shape, a.dtype) for a in flat] + [SDS((8, LANES), F32)],
        in_specs=[HBM_SPEC] * k,
        out_specs=[SEM_SPEC] * (2 * n) + [HBM_SPEC] * k + [pl.BlockSpec(memory_space=pltpu.VMEM)],
        input_output_aliases={j: 2 * n + j for j in range(k)},
        compiler_params=pltpu.CompilerParams(has_side_effects=SIDE_EFFECT),
    )(*[_in_hbm(a) for a in flat])
    thru, out_items = list(res[2 * n:2 * n + k]), []
    for it in items:
        out_items.append(tuple(thru.pop(0) for _ in it))
    return dict(send=res[:n], recv=res[n:2 * n], items=out_items, token=res[-1], copies=copies)


def _gather_start(shards, copies):
    names = list(shards)
    n = len(names)
    flip = [name in TRANSPOSED_HERE for name in names]
    shapes = [shards[name].shape[::-1] if f else shards[name].shape for name, f in zip(names, flip)]
    most = (max(s[0] for s in shapes), max(s[1] for s in shapes))

    def body(*refs):
        src, sems, land, token = refs[:n], refs[n:3 * n], refs[3 * n:4 * n], refs[4 * n]
        wide, narrow, sem = refs[4 * n + 1:]
        for i, (rows, cols) in enumerate(shapes):
            raw = wide.at[0:cols, 0:rows] if flip[i] else wide.at[0:rows, 0:cols]
            bring = pltpu.make_async_copy(src[i], raw, sem.at[0])
            bring.start()
            bring.wait()
            narrow[0:rows, 0:cols] = (raw[...].T if flip[i] else raw[...]).astype(BF16)
            mine = land[i].at[pl.ds(pl.multiple_of(_peer_index(0) * rows, 8), rows), :]
            place = pltpu.make_async_copy(narrow.at[0:rows, 0:cols], mine, sem.at[0])
            place.start()
            place.wait()
            for m in range(1, N_DEV):
                cp = copies[i](m, [land[i]], sems[i].at[m - 1], sems[n + i].at[m - 1])
                if cp is not None:
                    cp.start()
        token[...] = jnp.zeros_like(token)

    side = max(most)
    res = pl.pallas_call(
        body, name="gather_start",
        out_shape=[PEER_SEMS] * (2 * n) + [pltpu.HBM((N_DEV * r, c), BF16) for r, c in shapes] + [SDS((8, LANES), F32)],
        in_specs=[HBM_SPEC] * n,
        out_specs=[SEM_SPEC] * (2 * n) + [HBM_SPEC] * n + [pl.BlockSpec(memory_space=pltpu.VMEM)],
        scratch_shapes=[pltpu.VMEM((side, side), F32), pltpu.VMEM(most, BF16), pltpu.SemaphoreType.DMA((1,))],
        compiler_params=pltpu.CompilerParams(has_side_effects=SIDE_EFFECT),
    )(*[_in_hbm(shards[name]) for name in names])
    return dict(send=res[:n], recv=res[n:2 * n], items=[(a,) for a in res[2 * n:3 * n]], token=res[-1], copies=copies)


def _gather_relay(name, started, which, after):
    lands = [started["items"][w][0] for w in which]
    k = len(lands)

    def body(*refs):
        land_refs, old_send, old_recv = refs[:k], refs[k:2 * k], refs[2 * k:3 * k]
        outs = refs[3 * k + len(after):]
        send, recv, token = outs[:k], outs[k:2 * k], outs[-1]
        for j in range(k):
            for m in range(1, N_DEV):
                cp = _gather_copy_near(m, [land_refs[j]], old_send[j].at[m - 1], old_recv[j].at[m - 1])
                if cp is None:
                    continue
                cp.wait_send()
                cp.wait_recv()
                if m > 1:
                    _gather_copy_pass(m + 1, [land_refs[j]], send[j].at[m], recv[j].at[m]).start()
        token[...] = jnp.zeros_like(token)

    res = pl.pallas_call(
        body, name=name,
        out_shape=[PEER_SEMS] * (2 * k) + [pltpu.HBM(a.shape, a.dtype) for a in lands] + [SDS((8, LANES), F32)],
        in_specs=[HBM_SPEC] * k + [SEM_SPEC] * (2 * k) + [ANY_SPEC] * len(after),
        out_specs=[SEM_SPEC] * (2 * k) + [HBM_SPEC] * k + [pl.BlockSpec(memory_space=pltpu.VMEM)],
        input_output_aliases={j: 2 * k + j for j in range(k)},
        compiler_params=pltpu.CompilerParams(has_side_effects=SIDE_EFFECT),
    )(*lands, *[started["send"][w] for w in which], *[started["recv"][w] for w in which],
      *[_in_hbm(a) for a in after])
    return dict(send=res[:k], recv=res[k:2 * k], items=[(a,) for a in res[2 * k:3 * k]], token=res[-1],
                copies=[_gather_copy_pass] * k)


def _split_copy_wait(name, started, which, after):
    items = [started["items"][i] for i in which]
    copies = [started["copies"][i] for i in which]
    n = len(items)
    flat = [a for it in items for a in it]
    k = len(flat)

    def body(*refs):
        arr, sems = list(refs[:k]), refs[k:k + 2 * n]
        for i, it in enumerate(items):
            mine = [arr.pop(0) for _ in it]
            for m in range(1, N_DEV):
                cp = copies[i](m, mine, sems[i].at[m - 1], sems[n + i].at[m - 1])
                if cp is not None:
                    cp.wait_send()
                    cp.wait_recv()

    res = pl.pallas_call(
        body, name=name,
        out_shape=[pltpu.HBM(a.shape, a.dtype) for a in flat],
        in_specs=[HBM_SPEC] * k + [SEM_SPEC] * (2 * n) + [ANY_SPEC] * len(after),
        out_specs=[HBM_SPEC] * k,
        input_output_aliases={j: j for j in range(k)},
        compiler_params=pltpu.CompilerParams(has_side_effects=SIDE_EFFECT),
    )(*flat, *[started["send"][i] for i in which], *[started["recv"][i] for i in which], *[_in_hbm(a) for a in after])
    thru, out_items = list(res), []
    for it in items:
        out_items.append(tuple(thru.pop(0) for _ in it))
    return out_items


def _gather_copy(m, refs, send_sem, recv_sem):
    (land_ref,) = refs
    r = land_ref.shape[0] // N_DEV
    mine = land_ref.at[pl.ds(pl.multiple_of(_peer_index(0) * r, 8), r), :]
    return pltpu.make_async_remote_copy(src_ref=mine, dst_ref=mine, send_sem=send_sem, recv_sem=recv_sem,
                                        device_id=_peer(m), device_id_type=MESH)


def _gather_copy_near(m, refs, send_sem, recv_sem):
    return _gather_copy(m, refs, send_sem, recv_sem) if m == 1 or m % 2 == 0 else None


def _gather_copy_pass(m, refs, send_sem, recv_sem):
    if m == 1 or m % 2 == 0:
        return None
    (land_ref,) = refs
    r = land_ref.shape[0] // N_DEV
    block = land_ref.at[pl.ds(pl.multiple_of(_peer_index(m ^ 1) * r, 8), r), :]
    return pltpu.make_async_remote_copy(src_ref=block, dst_ref=block, send_sem=send_sem, recv_sem=recv_sem,
                                        device_id=_peer(1), device_id_type=MESH)


def _small_copy(m, refs, send_sem, recv_sem):
    part_ref, land_ref = refs
    return pltpu.make_async_remote_copy(src_ref=part_ref, dst_ref=land_ref.at[m - 1], send_sem=send_sem,
                                        recv_sem=recv_sem, device_id=_peer(m), device_id_type=MESH)


def _scatter_copy(m, refs, send_sem, recv_sem):
    buf_ref, land_ref = refs
    r = buf_ref.shape[0] // N_DEV
    src = buf_ref.at[pl.ds(pl.multiple_of(_peer_index(m) * r, 8), r), :]
    return pltpu.make_async_remote_copy(src_ref=src, dst_ref=land_ref.at[m - 1], send_sem=send_sem,
                                        recv_sem=recv_sem, device_id=_peer(m), device_id_type=MESH)


def _peer(m):
    x, y, c = _mesh_pos()
    bx, by, bc = (m >> 2) & 1, (m >> 1) & 1, m & 1
    return (x ^ bx if bx else x, y ^ by if by else y, c ^ bc if bc else c)


def _peer_index(m):
    x, y, c = _mesh_pos()
    return (4 * x + 2 * y + c) ^ m


SMALL_PLACE = {
    "ln1_gain": (ROW_LN1G, 0), "ln1_bias": (ROW_LN1B, 0), "ln2_gain": (ROW_LN2G, 0), "ln2_bias": (ROW_LN2B, 0),
    "ret_gn_gain": (ROW_GN, 0), "ret_decay_fwd": (ROW_MISC, MISC_DF), "ret_decay_bwd": (ROW_MISC, MISC_DB),
    "attn_sink": (ROW_MISC, MISC_SINK)}


def _small_adamw(me, part, landed, w, m, v):
    d = part.shape[1]
    names = list(SMALL_PLACE)
    k = len(names)

    def body(*refs):
        me_ref, part_ref, land_ref = refs[:3]
        refs = refs[2:]
        w_refs, m_refs, v_refs = refs[1:1 + k], refs[1 + k:1 + 2 * k], refs[1 + 2 * k:1 + 3 * k]
        outs = refs[1 + 3 * k:1 + 7 * k + 1]
        tot_ref = refs[-1]
        loss_ref, g_refs, dl_refs = outs[0], outs[1:1 + k], outs[1 + k:1 + 2 * k]
        nm_refs, nv_refs = outs[1 + 2 * k:1 + 3 * k], outs[1 + 3 * k:1 + 4 * k]
        tot = jnp.zeros(part_ref.shape, F32)
        for dev in range(N_DEV):
            j = dev ^ me_ref[0]
            tot = tot + jnp.where(j == 0, part_ref[...], land_ref[jnp.maximum(j, 1) - 1])
        tot_ref[...] = tot
        loss_ref[...] = (0.5 / d) * jnp.sum(tot_ref[ROW_LOSS:ROW_LOSS + 1, :], axis=1, keepdims=True)
        for i, name in enumerate(names):
            row, lo = SMALL_PLACE[name]
            wv = w_refs[i][...]
            g = tot_ref[row:row + 1, lo:lo + wv.shape[1]]
            if name.startswith("ret_decay"):
                p2 = jnp.exp2(wv)
                g = g * (-p2 * jnp.log(2.0) / (1.0 - p2))
            g_refs[i][...] = g
            _adamw_store(g, wv, m_refs[i][...], v_refs[i][...], dl_refs[i], nm_refs[i], nv_refs[i])

    shapes = [SDS(w[n].shape, F32) for n in names]
    vm = pl.BlockSpec(memory_space=pltpu.VMEM)
    res = pl.pallas_call(
        body, name="small_adamw", out_shape=[SDS((1, 1), F32)] + shapes * 4,
        in_specs=[_smem_spec()] + [vm] * (2 + 3 * k), out_specs=[vm] * (1 + 4 * k),
        scratch_shapes=[pltpu.VMEM(part.shape, F32)],
    )(me, part, landed, *[w[n] for n in names], *[m[n] for n in names], *[v[n] for n in names])
    groups = [dict(zip(names, res[1 + j * k:1 + (j + 1) * k])) for j in range(4)]
    return (res[0], *groups)


def _adamw_store(g, w, m, v, dl_ref, nm_ref, nv_ref):
    m = ADAM_B1 * m + (1.0 - ADAM_B1) * g
    v = ADAM_B2 * v + (1.0 - ADAM_B2) * (g * g)
    m_hat = m / (1.0 - ADAM_B1 ** ADAM_STEP)
    v_hat = v / (1.0 - ADAM_B2 ** ADAM_STEP)
    dl_ref[...] = -ADAM_LR * (m_hat / (jnp.sqrt(v_hat) + ADAM_EPS) + ADAM_WD * w)
    nm_ref[...] = m
    nv_ref[...] = v


def _reduce_adamw(name, owns, recvs, ws, ms, vs, transposed):
    count = len(owns)
    rows, n = owns[0].shape
    steps = 1 if transposed or rows % 16 else 2
    rb = rows // steps

    def body(*refs):
        ins, outs = refs[:5 * count], refs[5 * count:]
        j = pl.program_id(0)
        for k in range(count):
            @pl.when(j == k)
            def _(k=k):
                own_ref, recv_ref, w_ref, m_ref, v_ref = ins[5 * k:5 * k + 5]
                g_ref, dl_ref, nm_ref, nv_ref = outs[4 * k:4 * k + 4]
                g = own_ref[...]
                for p in range(recv_ref.shape[0]):
                    g = g + recv_ref[p].astype(F32)
                if transposed:
                    g = g.T
                g_ref[...] = g
                _adamw_store(g, w_ref[...], m_ref[...], v_ref[...], dl_ref, nm_ref, nv_ref)

    def turn(k):
        return lambda j, i: jnp.where(j == k, i, jnp.where(j < k, 0, steps - 1))

    in_specs, out_specs = [], []
    for k in range(count):
        at = turn(k)
        blk = pl.BlockSpec(ws[0].shape if transposed else (rb, n), lambda j, i, at=at: (at(j, i), 0))
        in_specs += [pl.BlockSpec((rb, n), lambda j, i, at=at: (at(j, i), 0)),
                     pl.BlockSpec((recvs[k].shape[0], rb, n), lambda j, i, at=at: (0, at(j, i), 0)), blk, blk, blk]
        out_specs += [blk] * 4
    res = pl.pallas_call(
        body, name="adamw_" + name, grid=(count, steps), in_specs=in_specs, out_specs=out_specs,
        out_shape=[SDS(ws[0].shape, F32)] * (4 * count), compiler_params=_params(("arbitrary", "arbitrary")),
    )(*[a for k in range(count) for a in (owns[k], recvs[k], ws[k], ms[k], vs[k])])
    return [list(res[j::4]) for j in range(4)]


def _row_spec(tm, width):
    return pl.BlockSpec((tm, width), lambda i: (i, 0))


def _full_spec(shape):
    return pl.BlockSpec(shape, lambda i: (0,) * len(shape))


_acc_spec = _full_spec


def _sub_rows(tm):
    step = min(SUB_ROWS, tm)
    return [(lo, lo + step) for lo in range(0, tm, step)]


def _in_proj(x2, wt_in):
    t, d = x2.shape
    u_w = wt_in.shape[0]
    tm = _row_tile(t, MATMUL_ROWS)

    def body(x_ref, w_ref, u_ref, xb_ref):
        xb = x_ref[...].astype(BF16)
        xb_ref[...] = xb
        u_ref[...] = _dot_nt(xb, w_ref[...]).astype(BF16)

    return pl.pallas_call(
        body, name="in_proj", grid=(t // tm,),
        in_specs=[_row_spec(tm, d), _full_spec(wt_in.shape)],
        out_specs=[_row_spec(tm, u_w), _row_spec(tm, d)],
        out_shape=[SDS((t, u_w), BF16), SDS((t, d), BF16)],
        compiler_params=_params(("parallel",)),
    )(x2, wt_in)


def _col_halves(f):
    n = f // LANES
    k = (n + 1) // 2 * LANES
    return [(0, k), (k, f)] if k < f else [(0, f)]


def _mix_ln1_ffn_up(r, a, x2, w_out, wt_gate, wt_up, g1, b1):
    t, d = x2.shape
    f = wt_gate.shape[0]
    tm = _row_tile(t, EPILOGUE_ROWS)

    def body(r_ref, a_ref, x_ref, wo_ref, wg_ref, wu_ref, g_ref, b_ref, xh_ref, rs_ref, hb_ref, dg_ref, du_ref,
             act_ref):
        mix = _dot_nn(r_ref[...], wo_ref[0:RET_W, :]) + _dot_nn(a_ref[...], wo_ref[RET_W:RET_W + ATT_W, :])
        z = ALPHA * x_ref[...] + mix
        xhat, rstd = _layer_norm_stats(z)
        xh_ref[...] = xhat
        rs_ref[...] = jnp.broadcast_to(rstd, rs_ref.shape)
        h = (xhat * g_ref[...] + b_ref[...]).astype(BF16)
        hb_ref[...] = h
        g = _dot_nt(h, wg_ref[...])
        u = _dot_nt(h, wu_ref[...])
        sg = _sigmoid(g)
        silu = g * sg
        dg_ref[...] = (u * (sg * (1.0 + g * (1.0 - sg)))).astype(BF16)
        du_ref[...] = silu.astype(BF16)
        act_ref[...] = (silu * u).astype(BF16)

    wide, narrow = _row_spec(tm, f), _row_spec(tm, d)
    return pl.pallas_call(
        body, name="mix_ln1_ffn_up", grid=(t // tm,),
        in_specs=[_row_spec(tm, RET_W), _row_spec(tm, ATT_W), narrow, _resident_spec(w_out.shape),
                  _resident_spec(wt_gate.shape), _resident_spec(wt_up.shape), _full_spec(g1.shape),
                  _full_spec(b1.shape)],
        out_specs=[narrow, _row_spec(tm, LANES), narrow, wide, wide, wide],
        out_shape=[SDS((t, d), F32), SDS((t, LANES), F32), SDS((t, d), BF16)] + [SDS((t, f), BF16)] * 3,
        compiler_params=_params(("parallel",)),
    )(r, a, x2, w_out, wt_gate, wt_up, g1, b1)


def _ffn_down_ln2_loss(me, act, dact_dg, dact_du, h1b, p2, xhat1, target, w_down, w_pg, wt_pe, g1, b1, g2, b2):
    t, d = xhat1.shape
    f = act.shape[1]
    pdim = p2.shape[1]
    tm = _row_tile(t, EPILOGUE_ROWS)
    n_steps = t // tm
    own_rows = d // N_DEV

    def body(me_ref, act_ref, fg_ref, fu_ref, hb_ref, p_ref, xh1_ref, tgt_ref, wd_ref, wpg_ref, wpe_ref, g1_ref,
             b1_ref, g2_ref, b2_ref, dz_ref, dzb_ref, ds_ref, dg_ref, du_ref, acc_ref,
             pe_whole, pe_own, pg_whole, pg_own, pe_acc, pg_acc):
        @pl.when(pl.program_id(0) == 0)
        def _():
            acc_ref[...] = jnp.zeros_like(acc_ref)
            pe_acc[...] = jnp.zeros_like(pe_acc)
            pg_acc[...] = jnp.zeros_like(pg_acc)

        for lo, hi in _sub_rows(tm):
            h1 = xh1_ref[lo:hi, :] * g1_ref[...] + b1_ref[...]
            pb = p_ref[lo:hi, :].astype(BF16)
            pg = _sigmoid(_dot_nn(hb_ref[lo:hi, :], wpg_ref[...]))
            ple = _dot_nt(pb, wpe_ref[...])
            gated = pg * ple
            dgate = gated * (1.0 - pg)
            ffn = _dot_nn(act_ref[lo:hi, :], wd_ref[...])
            z2 = ALPHA * h1 + gated + ffn
            xhat2, rstd2 = _layer_norm_stats(z2)
            err = xhat2 * g2_ref[...] + b2_ref[...] - tgt_ref[lo:hi, :]
            dy = err * (1.0 / d)
            dz = _layer_norm_bwd(dy * g2_ref[...], xhat2, rstd2)
            dzb = dz.astype(BF16)
            dz_ref[lo:hi, :] = dz
            dzb_ref[lo:hi, :] = dzb
            dsb, dpleb = (dz * dgate).astype(BF16), (dz * pg).astype(BF16)
            ds_ref[lo:hi, :] = dsb
            pe_acc[...] += _dot_tn(dpleb, pb)
            pg_acc[...] += _dot_tn(hb_ref[lo:hi, :], dsb)
            acc_ref[0:1, :] += jnp.sum(err * err, axis=0, keepdims=True)
            acc_ref[1:2, :] += jnp.sum(dy * xhat2, axis=0, keepdims=True)
            acc_ref[2:3, :] += jnp.sum(dy, axis=0, keepdims=True)
            for c0, c1 in _col_halves(f):
                da = _dot_nt(dzb, wd_ref[c0:c1, :])
                dg_ref[lo:hi, c0:c1] = (da * fg_ref[lo:hi, c0:c1].astype(F32)).astype(BF16)
                du_ref[lo:hi, c0:c1] = (da * fu_ref[lo:hi, c0:c1].astype(F32)).astype(BF16)

        @pl.when(pl.program_id(0) == n_steps - 1)
        def _():
            mine = pl.ds(pl.multiple_of(me_ref[0] * own_rows, 8), own_rows)
            for whole, own, acc in ((pe_whole, pe_own, pe_acc), (pg_whole, pg_own, pg_acc)):
                whole[...] = acc[...].astype(BF16)
                own[...] = acc[mine, :]

    vec = _full_spec(g1.shape)
    wide, narrow = _row_spec(tm, f), _row_spec(tm, d)
    products = [(d, pdim), (own_rows, pdim), (d, d), (own_rows, d)]
    return pl.pallas_call(
        body, name="ffn_down_ln2_loss", grid=(n_steps,),
        in_specs=[_smem_spec(), wide, wide, wide, narrow, _row_spec(tm, pdim), narrow, narrow,
                  _full_spec(w_down.shape), _full_spec(w_pg.shape), _full_spec(wt_pe.shape), vec, vec, vec, vec],
        out_specs=[narrow] * 3 + [wide, wide, _acc_spec((8, d))] + [_full_spec(s) for s in products],
        out_shape=[SDS((t, d), F32), SDS((t, d), BF16), SDS((t, d), BF16),
                   SDS((t, f), BF16), SDS((t, f), BF16), SDS((8, d), F32)]
        + [SDS(s, BF16 if k % 2 == 0 else F32) for k, s in enumerate(products)],
        scratch_shapes=[pltpu.VMEM((d, pdim), F32), pltpu.VMEM((d, d), F32)],
        compiler_params=_params(("arbitrary",)),
    )(me, act, dact_dg, dact_du, h1b, p2, xhat1, target, w_down, w_pg, wt_pe, g1, b1, g2, b2)


def _after(after, body):
    k = len(after)
    return (lambda *refs: body(*refs[k:])), [ANY_SPEC] * k


def _resident_spec(shape):
    return pl.BlockSpec(shape, lambda i: (0,) * len(shape), pipeline_mode=pl.Buffered(1))


def _dh1_ln1_bwd(dz2, dg, dup, dsb, xhat1, rstd1, wt_gate, wt_up, w_pg, w_out, g1, after=()):
    t, d = dz2.shape
    f = dg.shape[1]
    tm = _row_tile(t, MATMUL_ROWS)

    def body(dz_ref, dg_ref, du_ref, ds_ref, xh1_ref, rs1_ref, wg_ref, wu_ref, wpg_ref, wo_ref, g1_ref,
             dz1_ref, dz1b_ref, dr_ref, da_ref, acc_ref):
        @pl.when(pl.program_id(0) == 0)
        def _():
            acc_ref[...] = jnp.zeros_like(acc_ref)

        for lo, hi in _sub_rows(tm):
            dh = (ALPHA * dz_ref[lo:hi, :] + _dot_nn(dg_ref[lo:hi, :], wg_ref[...])
                  + _dot_nn(du_ref[lo:hi, :], wu_ref[...]) + _dot_nt(ds_ref[lo:hi, :], wpg_ref[...]))
            xhat, rstd = xh1_ref[lo:hi, :], rs1_ref[lo:hi, 0:1]
            dz1 = _layer_norm_bwd(dh * g1_ref[...], xhat, rstd)
            dz1b = dz1.astype(BF16)
            dz1_ref[lo:hi, :] = dz1
            dz1b_ref[lo:hi, :] = dz1b
            acc_ref[0:1, :] += jnp.sum(dh * xhat, axis=0, keepdims=True)
            acc_ref[1:2, :] += jnp.sum(dh, axis=0, keepdims=True)
            dr_ref[lo:hi, :] = _dot_nt(dz1b, wo_ref[0:RET_W, :]).astype(BF16)
            da_ref[lo:hi, :] = _dot_nt(dz1b, wo_ref[RET_W:RET_W + ATT_W, :]).astype(BF16)

    body, lead = _after(after, body)
    return pl.pallas_call(
        body, name="dh1_ln1_bwd", grid=(t // tm,),
        in_specs=lead + [_row_spec(tm, d), _row_spec(tm, f), _row_spec(tm, f), _row_spec(tm, d), _row_spec(tm, d),
                         _row_spec(tm, LANES), _resident_spec(wt_gate.shape), _resident_spec(wt_up.shape), _resident_spec(w_pg.shape),
                         _resident_spec(w_out.shape), _full_spec(g1.shape)],
        out_specs=[_row_spec(tm, d), _row_spec(tm, d), _row_spec(tm, RET_W), _row_spec(tm, ATT_W), _acc_spec((8, d))],
        out_shape=[SDS((t, d), F32), SDS((t, d), BF16), SDS((t, RET_W), BF16), SDS((t, ATT_W), BF16),
                   SDS((8, d), F32)],
        compiler_params=_params(("arbitrary",)),
    )(*after, dz2, dg, dup, dsb, xhat1, rstd1, wt_gate, wt_up, w_pg, w_out, g1)


def _in_proj_bwd(dz1, parts, wt_in, after=()):
    t, d = dz1.shape
    tm = _row_tile(t, MATMUL_ROWS)
    widths = [p.shape[1] for p in parts]

    def body(*refs):
        dz_ref, part_refs, w_ref, dx_ref = refs[0], refs[1:1 + len(parts)], refs[-2], refs[-1]
        acc = ALPHA * dz_ref[...]
        lo = 0
        for p_ref, w in zip(part_refs, widths):
            acc = acc + _dot_nn(p_ref[...], w_ref[lo:lo + w, :])
            lo += w
        dx_ref[...] = acc

    body, lead = _after(after, body)
    return pl.pallas_call(
        body, name="in_proj_bwd", grid=(t // tm,),
        in_specs=lead + [_row_spec(tm, d)] + [_row_spec(tm, w) for w in widths] + [_full_spec(wt_in.shape)],
        out_specs=_row_spec(tm, d), out_shape=SDS((t, d), F32),
        compiler_params=_params(("parallel",)),
    )(*after, dz1, *parts, wt_in)


def _weight_grad(name, me, parts, rhs, after=()):
    t, n = rhs.shape
    widths = [p.shape[1] for p in parts]
    rows = sum(widths)
    own_rows = rows // N_DEV
    tk = _row_tile(t, MATMUL_ROWS)
    n_steps = t // tk
    step = 256

    def body(*refs):
        me_ref, part_refs, rhs_ref = refs[0], refs[1:1 + len(parts)], refs[1 + len(parts)]
        full_ref, own_ref, acc = refs[-3], refs[-2], refs[-1]
        i = pl.program_id(0)

        def products(first):
            b = rhs_ref[...].astype(BF16)
            lo = 0
            for p_ref, w in zip(part_refs, widths):
                for c0 in range(0, w, step):
                    c1 = min(c0 + step, w)
                    val = _dot_tn(p_ref[:, c0:c1].astype(BF16), b)
                    if first:
                        acc[lo + c0:lo + c1, :] = val
                    else:
                        acc[lo + c0:lo + c1, :] += val
                lo += w

        pl.when(i == 0)(functools.partial(products, True))
        pl.when(i > 0)(functools.partial(products, False))

        @pl.when(i == n_steps - 1)
        def _():
            full_ref[...] = acc[...].astype(BF16)
            own_ref[...] = acc[pl.ds(pl.multiple_of(me_ref[0] * own_rows, 8), own_rows), :]

    body, lead = _after(after, body)
    return pl.pallas_call(
        body, name=name, grid=(n_steps,),
        in_specs=lead + [_smem_spec()] + [_row_spec(tk, w) for w in widths] + [_row_spec(tk, n)],
        out_specs=[_full_spec((rows, n)), _full_spec((own_rows, n))],
        out_shape=[SDS((rows, n), BF16), SDS((own_rows, n), F32)],
        scratch_shapes=[pltpu.VMEM((rows, n), F32)],
        compiler_params=_params(("arbitrary",)),
    )(*after, me, *parts, rhs)


def _weight_grad_jobs(name, me, jobs, after=()):
    count = len(jobs)
    t = jobs[0][0].shape[0]
    tk = _row_tile(t, MATMUL_ROWS)
    n_steps = t // tk
    shapes = [(lhs.shape[1], rhs.shape[1]) for lhs, rhs in jobs]
    most_rows, most_cols = max(r for r, _ in shapes), max(n for _, n in shapes)
    step = 256

    def body(*refs):
        me_ref, lhs_refs, rhs_refs = refs[0], refs[1:1 + count], refs[1 + count:1 + 2 * count]
        full_refs, own_refs = refs[1 + 2 * count:1 + 3 * count], refs[1 + 3 * count:1 + 4 * count]
        acc, whole, mine, sems = refs[1 + 4 * count:]
        job, i = pl.program_id(0), pl.program_id(1)

        def leaving(j):
            rows, n = shapes[j]
            return (pltpu.make_async_copy(whole.at[0:rows, 0:n], full_refs[j], sems.at[0]),
                    pltpu.make_async_copy(mine.at[0:rows // N_DEV, 0:n], own_refs[j], sems.at[1]))

        def products(j, first):
            rows, n = shapes[j]
            b = rhs_refs[j][...].astype(BF16)
            for c0 in range(0, rows, step):
                c1 = min(c0 + step, rows)
                val = _dot_tn(lhs_refs[j][:, c0:c1].astype(BF16), b)
                if first:
                    acc[c0:c1, 0:n] = val
                else:
                    acc[c0:c1, 0:n] += val

        def finish(j):
            rows, n = shapes[j]
            own_rows = rows // N_DEV
            if j > 0:
                for cp in leaving(j - 1):
                    cp.wait()
            whole[0:rows, 0:n] = acc[0:rows, 0:n].astype(BF16)
            mine[0:own_rows, 0:n] = acc[pl.ds(pl.multiple_of(me_ref[0] * own_rows, 8), own_rows), 0:n]
            for cp in leaving(j):
                cp.start()
            if j == count - 1:
                for cp in leaving(j):
                    cp.wait()

        for j in range(count):
            pl.when((job == j) & (i == 0))(functools.partial(products, j, True))
            pl.when((job == j) & (i > 0))(functools.partial(products, j, False))
            pl.when((job == j) & (i == n_steps - 1))(functools.partial(finish, j))

    def turn(j):
        return lambda job, i: (jnp.where(job == j, i, jnp.where(job < j, 0, n_steps - 1)), 0)

    body, lead = _after(after, body)
    res = pl.pallas_call(
        body, name=name, grid=(count, n_steps),
        in_specs=lead + [_smem_spec()] + [pl.BlockSpec((tk, rows), turn(j)) for j, (rows, _) in enumerate(shapes)]
        + [pl.BlockSpec((tk, n), turn(j)) for j, (_, n) in enumerate(shapes)],
        out_specs=[ANY_SPEC] * (2 * count),
        out_shape=[SDS((rows, n), BF16) for rows, n in shapes] + [SDS((rows // N_DEV, n), F32) for rows, n in shapes],
        scratch_shapes=[pltpu.VMEM((most_rows, most_cols), F32), pltpu.VMEM((most_rows, most_cols), BF16),
                        pltpu.VMEM((most_rows // N_DEV, most_cols), F32), pltpu.SemaphoreType.DMA((2,))],
        compiler_params=_params(("arbitrary", "arbitrary")),
    )(*after, me, *[lhs for lhs, _ in jobs], *[rhs for _, rhs in jobs])
    return list(res[:count]), list(res[count:])


def _log_decay(decay_f, decay_b):
    def body(f_ref, b_ref, lf_ref, lb_ref):
        lf_ref[...] = jnp.log1p(-jnp.exp2(f_ref[...]))
        lb_ref[...] = jnp.log1p(-jnp.exp2(b_ref[...]))

    return pl.pallas_call(body, name="log_decay", out_shape=[SDS(decay_f.shape, F32)] * 2)(decay_f, decay_b)


def _chunk(ref, n):
    return ref[pl.ds(pl.multiple_of(n * CHUNK, CHUNK), CHUNK), :]


def _group_sum(is_a, v):
    sa = jnp.sum(jnp.where(is_a, v, 0.0), axis=1, keepdims=True)
    sb = jnp.sum(jnp.where(is_a, 0.0, v), axis=1, keepdims=True)
    return jnp.where(is_a, sa, sb)


def _seq_spec(s, col_block):
    return pl.BlockSpec((s, LANES), lambda b, h: (b, col_block + h))


def _smem_spec():
    return pl.BlockSpec(memory_space=pltpu.SMEM)


RET_UNROLL = 4
BWD_MAIN_UNROLL = 8
FWD_PREP_UNROLL = 16
BWD_PREP_UNROLL = 8


def _chunk_loop(n_chunks, body, init, unroll):
    u = unroll if n_chunks % unroll == 0 else 1

    def trip(i, carry):
        for j in range(u):
            carry = body(i * u + j, carry)
        return carry

    return lax.fori_loop(0, n_chunks // u, trip, init)


def _stacked_tables(lgf_ref, lgb_ref, pair):
    lane = lax.broadcasted_iota(jnp.int32, (1, LANES), 1)
    is_a = lane < HEAD_DIM
    lgf = jnp.where(is_a, lgf_ref[2 * pair], lgf_ref[2 * pair + 1])
    lgb = jnp.where(is_a, lgb_ref[2 * pair], lgb_ref[2 * pair + 1])
    row = lax.broadcasted_iota(jnp.int32, (CHUNK, 1), 0).astype(F32)
    kdec_f, qdec_f = jnp.exp(lgf * (CHUNK - 1.0 - row)), jnp.exp(lgf * (row + 1.0))
    kdec_b, qdec_b = jnp.exp(lgb * row), jnp.exp(lgb * (CHUNK - row))
    tab = dict(
        is_a=is_a, row=row, lam_f=jnp.exp(lgf * CHUNK), lam_b=jnp.exp(lgb * CHUNK),
        kdec=jnp.concatenate([kdec_f, kdec_b], axis=1), qdec=jnp.concatenate([qdec_f, qdec_b], axis=1),
        qexp=jnp.concatenate([jnp.broadcast_to(row + 1.0, (CHUNK, LANES)),
                              jnp.broadcast_to(CHUNK - row, (CHUNK, LANES))], axis=1),
        kexp=jnp.concatenate([jnp.broadcast_to(CHUNK - 1.0 - row, (CHUNK, LANES)),
                              jnp.broadcast_to(row, (CHUNK, LANES))], axis=1),
    )
    r = lax.broadcasted_iota(jnp.int32, (2 * LANES, LANES), 0)
    c = lax.broadcasted_iota(jnp.int32, (2 * LANES, LANES), 1)
    tab["diag2"] = ((r & (LANES - 1)) < HEAD_DIM) == (c < HEAD_DIM)
    i2 = lax.broadcasted_iota(jnp.int32, (2 * CHUNK, CHUNK), 0)
    j = lax.broadcasted_iota(jnp.int32, (2 * CHUNK, CHUNK), 1)
    head_b = i2 >= CHUNK
    diff = ((i2 & (CHUNK - 1)) - j).astype(F32)
    up, dn = jnp.maximum(diff, 0.0), jnp.maximum(-diff, 0.0)
    lgf2 = jnp.where(head_b, lgf_ref[2 * pair + 1], lgf_ref[2 * pair])
    lgb2 = jnp.where(head_b, lgb_ref[2 * pair + 1], lgb_ref[2 * pair])
    ef = jnp.where(diff >= 0, jnp.exp(lgf2 * up), 0.0)
    eb = jnp.where(diff <= 0, jnp.exp(lgb2 * dn), 0.0)
    tab["d2"] = ef + eb
    tab["df2"] = ef * up
    tab["db2"] = eb * dn
    return tab


def _stack_pair(is_a, x):
    zero = jnp.zeros_like(x)
    return jnp.concatenate([jnp.where(is_a, x, zero), jnp.where(is_a, zero, x)], axis=0)


def _unstack_pair(is_a, x2):
    return jnp.where(is_a, x2[0:CHUNK, :], x2[CHUNK:2 * CHUNK, :])


def _both_ways(x, dec):
    return (jnp.concatenate([x, x], axis=1) * dec).astype(BF16)


def _scan_states(n_chunks, st, up_rows, up_lam, down_rows, down_lam):
    zero = jnp.zeros((LANES, LANES), F32)

    def up(n, r):
        new = st[n, up_rows, :]
        st[n, up_rows, :] = r
        return r * up_lam + new

    def down(s, r):
        n = n_chunks - 1 - s
        new = st[n, down_rows, :]
        st[n, down_rows, :] = r
        return r * down_lam + new

    lax.fori_loop(0, n_chunks, up, zero)
    lax.fori_loop(0, n_chunks, down, zero)


FWD_ROWS, BWD_ROWS = pl.ds(0, LANES), pl.ds(LANES, LANES)


def _state_spec(n_chunks, pairs):
    return pl.BlockSpec((n_chunks, 2 * LANES, LANES), lambda b, h: (b * pairs + h, 0, 0))


ST_GAIN, ST_XF, ST_XB, ST_IFA, ST_IFB, ST_IBA, ST_IBB, ST_LF, ST_LB = 0, 1, 2, 3, 4, 5, 6, 8, 9
ST_ROWS = 16


GW = GROUP * HEAD_DIM
KEYS = 3 * BLOCK


def _attn_tables(g, bias_ref):
    r = lax.broadcasted_iota(jnp.int32, (GROUP * BLOCK, KEYS), 0)
    kj = lax.broadcasted_iota(jnp.int32, (GROUP * BLOCK, KEYS), 1)
    qi = r & (BLOCK - 1)
    hh = lax.shift_right_logical(r, 7)
    dist = jnp.abs(kj - BLOCK - qi)
    slope = jnp.exp2(-(GROUP * g + hh + 1).astype(F32) * (8.0 / ATTN_HEADS))
    inside = jnp.where(dist <= BLOCK, -slope * dist.astype(F32), NEG_INF)
    bias_ref[BIAS_INSIDE] = inside
    bias_ref[BIAS_FIRST] = jnp.where(kj >= BLOCK, inside, NEG_INF)
    bias_ref[BIAS_LAST] = jnp.where(kj < 2 * BLOCK, inside, NEG_INF)


BIAS_INSIDE, BIAS_FIRST, BIAS_LAST = 0, 1, 2


def _own_lanes(g):
    return lax.shift_right_logical(lax.broadcasted_iota(jnp.int32, (1, LANES), 1), 6) == g


def _mask_keys(x_ref, g, scale, pad_ref, s):
    pad_ref[0:BLOCK, :] = jnp.zeros((BLOCK, LANES), BF16)
    pad_ref[BLOCK + s:2 * BLOCK + s, :] = jnp.zeros((BLOCK, LANES), BF16)
    pad_ref[BLOCK:BLOCK + s, :] = jnp.where(_own_lanes(g), x_ref[...].astype(F32) * scale, 0.0).astype(BF16)


def _lane_block(x, j):
    return x[:, j * LANES:(j + 1) * LANES]


def _stack_heads(x, g):
    assert GROUP == 4 and GW == 2 * LANES
    x1 = pltpu.roll(x, HEAD_DIM, 1)
    keep = _own_lanes(g)
    zero = jnp.zeros((BLOCK, LANES), x.dtype)
    rows = []
    for h in range(GROUP):
        for_g0 = _lane_block(x, h // 2) if h % 2 == 0 else _lane_block(x1, ((h + 1) // 2) % 2)
        for_g1 = _lane_block(x, h // 2) if h % 2 == 1 else _lane_block(x1, h // 2)
        rows.append(jnp.where(keep, jnp.where(g == 0, for_g0, for_g1), zero))
    return jnp.concatenate(rows, axis=0)


def _unstack_heads(x4, g):
    p = [x4[h * BLOCK:(h + 1) * BLOCK, :] for h in range(GROUP)]
    cat = lambda a, b: jnp.concatenate([a, b], axis=1)
    in_place = jnp.where(g == 0, cat(p[0], p[2]), cat(p[1], p[3]))
    one_left = jnp.where(g == 0, cat(p[1], p[3]), cat(p[2], p[0]))
    return in_place + pltpu.roll(one_left, HEAD_DIM, 1)


def _sink_column(sink_ref, g):
    rh = lax.shift_right_logical(lax.broadcasted_iota(jnp.int32, (GROUP * BLOCK, 1), 0), 7)
    col = jnp.zeros((GROUP * BLOCK, 1), F32)
    for h in range(GROUP):
        col = jnp.where(rh == h, sink_ref[GROUP * g + h], col)
    return col


def _attn_probs(qm, k3, bias_ref, sink_col, n, s):
    which = jnp.where(n == 0, BIAS_FIRST, jnp.where(n == s // BLOCK - 1, BIAS_LAST, BIAS_INSIDE))
    logits = _dot_nt(qm, k3) + bias_ref[which]
    m = jnp.maximum(jnp.max(logits, axis=1, keepdims=True), sink_col)
    e = jnp.exp(logits - m)
    e_sink = jnp.exp(sink_col - m)
    inv = 1.0 / (jnp.sum(e, axis=1, keepdims=True) + e_sink)
    return e * inv, e_sink * inv


PAIRS_PER_KV = (RET_HEADS // 2) // KV_HEADS
FWD_ORDER = "rrarra"
BWD_ORDER = "rrarar"


def _trip_order(order, chunks, blocks):
    if order.count("r") == chunks and order.count("a") == blocks:
        return order
    return "r" * chunks + "a" * blocks


def _mixers_fwd(u, lgf, lgb, gn_gain, sink, b_loc, after=()):
    t = u.shape[0]
    s = t // b_loc
    n_chunks = s // CHUNK
    pairs = RET_HEADS // 2
    trips = n_chunks // RET_UNROLL
    blocks_half = (s // BLOCK) // PAIRS_PER_KV
    per_trip = blocks_half // trips
    assert n_chunks % RET_UNROLL == 0 and blocks_half % trips == 0 and PAIRS_PER_KV == 2 and s >= 2 * BLOCK

    def body(lgf_ref, lgb_ref, sink_ref, q_ref, k_ref, v_ref, g_ref, gain_ref, aq_ref, ak_ref, av_ref,
             r_ref, xhat_ref, rstd_ref, a_ref, st, kpad, vpad, bias):
        pair = pl.program_id(1)
        g, half = lax.shift_right_logical(pair, 1), pair & 1
        tab = _stacked_tables(lgf_ref, lgb_ref, pair)
        is_a = tab["is_a"]

        @pl.when(half == 0)
        def _():
            _attn_tables(g, bias)
            _mask_keys(ak_ref, g, Q_SCALE, kpad, s)
            _mask_keys(av_ref, g, 1.0, vpad, s)

        def kv_body(n, _):
            k8 = _chunk(k_ref, n).astype(F32) * Q_SCALE
            st[n] = jnp.where(tab["diag2"], _dot_tn(_both_ways(k8, tab["kdec"]), _chunk(v_ref, n)), 0.0)
            return 0

        _chunk_loop(n_chunks, kv_body, 0, FWD_PREP_UNROLL)
        _scan_states(n_chunks, st, FWD_ROWS, tab["lam_f"], BWD_ROWS, tab["lam_b"])
        sink_col = _sink_column(sink_ref, g)

        def retention_chunk(n):
            q = _chunk(q_ref, n)
            k8 = (_chunk(k_ref, n).astype(F32) * Q_SCALE).astype(BF16)
            v = _chunk(v_ref, n)
            p2 = (_dot_nt(_stack_pair(is_a, q), k8) * tab["d2"]).astype(BF16)
            y = _unstack_pair(is_a, _dot_nn(p2, v))
            y = y + _dot_nn(_both_ways(q.astype(F32), tab["qdec"]), st[n].astype(BF16))
            rows = pl.ds(pl.multiple_of(n * CHUNK, CHUNK), CHUNK)
            mu = _group_sum(is_a, y) * (1.0 / HEAD_DIM)
            dlt = y - mu
            var = _group_sum(is_a, dlt * dlt) * (1.0 / HEAD_DIM)
            rstd = lax.rsqrt(var + GN_EPS)
            xhat = dlt * rstd
            xhat_ref[rows, :] = xhat
            rstd_ref[rows, :] = rstd
            gate = _chunk(g_ref, n).astype(F32)
            r_ref[rows, :] = (xhat * gain_ref[...] * gate * _sigmoid(gate)).astype(BF16)

        def attention_block(blk):
            n = half * blocks_half + blk
            rows = pl.ds(pl.multiple_of(blk * BLOCK, BLOCK), BLOCK)
            keys = pl.ds(pl.multiple_of(n * BLOCK, BLOCK), KEYS)
            p, _ = _attn_probs(_stack_heads(aq_ref[rows, :], g), kpad[keys, :], bias, sink_col, n, s)
            a_ref[rows, :] = _unstack_heads(_dot_nn(p.astype(BF16), vpad[keys, :]), g).astype(BF16)

        def trip(i, _):
            chunk, blk = 0, 0
            for kind in _trip_order(FWD_ORDER, RET_UNROLL, per_trip):
                if kind == "r":
                    retention_chunk(i * RET_UNROLL + chunk)
                    chunk += 1
                else:
                    attention_block(i * per_trip + blk)
                    blk += 1
            return 0

        lax.fori_loop(0, trips, trip, 0)

    lane_blk = lambda c0: _seq_spec(s, c0 // LANES)
    half_rows = blocks_half * BLOCK
    aq_spec = pl.BlockSpec((half_rows, GW), lambda b, h: (b * PAIRS_PER_KV + (h & 1), C_AQ // GW + h // 2))
    a_spec = pl.BlockSpec((half_rows, GW), lambda b, h: (b * PAIRS_PER_KV + (h & 1), h // 2))
    kv_spec = lambda c0: pl.BlockSpec((s, LANES), lambda b, h: (b, c0 // LANES))
    pad = pltpu.VMEM((s + 2 * BLOCK, LANES), BF16)
    body, lead = _after(after, body)
    return pl.pallas_call(
        body, name="mixers_fwd", grid=(b_loc, pairs),
        in_specs=lead + [_smem_spec(), _smem_spec(), _smem_spec(), lane_blk(C_RQ), lane_blk(C_RK), lane_blk(C_RV),
                         lane_blk(C_RG), pl.BlockSpec((1, LANES), lambda b, h: (0, h)), aq_spec, kv_spec(C_AK),
                         kv_spec(C_AV)],
        out_specs=[_seq_spec(s, 0), _seq_spec(s, 0), _seq_spec(s, 0), a_spec, _state_spec(n_chunks, pairs)],
        out_shape=[SDS((t, RET_W), BF16), SDS((t, RET_W), F32), SDS((t, RET_W), F32), SDS((t, ATT_W), BF16),
                   SDS((b_loc * pairs * n_chunks, 2 * LANES, LANES), F32)],
        scratch_shapes=[pad, pad, pltpu.VMEM((3, GROUP * BLOCK, KEYS), F32)],
        compiler_params=_params(("arbitrary", "arbitrary")),
    )(*after, lgf, lgb, sink, u, u, u, u, gn_gain, u, u, u)


def _mixers_bwd(u, xhat, rstd, states, dr, da, lgf, lgb, gn_gain, sink, b_loc, after=()):
    t = u.shape[0]
    s = t // b_loc
    n_chunks = s // CHUNK
    pairs = RET_HEADS // 2
    unroll = BWD_MAIN_UNROLL if n_chunks % BWD_MAIN_UNROLL == 0 else RET_UNROLL
    trips = n_chunks // unroll
    blocks_half = (s // BLOCK) // PAIRS_PER_KV
    per_trip = blocks_half // trips
    assert n_chunks % unroll == 0 and blocks_half % trips == 0 and PAIRS_PER_KV == 2 and s >= 2 * BLOCK

    def body(lgf_ref, lgb_ref, sink_ref, q_ref, k_ref, v_ref, g_ref, xhat_ref, rstd_ref, dr_ref, gain_ref,
             aq_ref, ak_ref, av_ref, do_ref, st,
             dq_ref, dk_ref, dv_ref, dg_ref, st_ref, daq_ref, dak_ref, dav_ref, dsink_ref,
             gr, dy_s, kpad, vpad, bias, dk_acc, dv_acc):
        pair = pl.program_id(1)
        g, half = lax.shift_right_logical(pair, 1), pair & 1
        tab = _stacked_tables(lgf_ref, lgb_ref, pair)
        is_a = tab["is_a"]
        gain = gain_ref[...]

        @pl.when(half == 0)
        def _():
            _attn_tables(g, bias)
            _mask_keys(ak_ref, g, Q_SCALE, kpad, s)
            _mask_keys(av_ref, g, 1.0, vpad, s)
            dsink_ref[...] = jnp.zeros_like(dsink_ref)

        @pl.when(pair == 0)
        def _():
            dk_acc[...] = jnp.zeros_like(dk_acc)
            dv_acc[...] = jnp.zeros_like(dv_acc)

        def norm_body(n, dgain):
            rows = pl.ds(pl.multiple_of(n * CHUNK, CHUNK), CHUNK)
            xhat, rstd = xhat_ref[rows, :], rstd_ref[rows, :]
            gate = g_ref[rows, :].astype(F32)
            sg = _sigmoid(gate)
            silu = gate * sg
            d_out = dr_ref[rows, :].astype(F32)
            dg_ref[rows, :] = (d_out * xhat * gain * (sg * (1.0 + gate * (1.0 - sg)))).astype(BF16)
            dxh = d_out * gain * silu
            m1 = _group_sum(is_a, dxh) * (1.0 / HEAD_DIM)
            m2 = _group_sum(is_a, dxh * xhat) * (1.0 / HEAD_DIM)
            dy = (rstd * (dxh - m1 - xhat * m2)).astype(BF16)
            dy_s[rows, :] = dy
            qf = q_ref[rows, :].astype(F32)
            gr[n] = jnp.where(tab["diag2"], _dot_tn(_both_ways(qf, tab["qdec"]), dy), 0.0)
            return dgain + jnp.sum(d_out * xhat * silu, axis=0, keepdims=True)

        colsum = lambda x: jnp.sum(x, axis=0, keepdims=True)

        def grad_body(n, carry):
            xfb, ifa, ifb, iba, ibb, lf, lb = carry
            rows = pl.ds(pl.multiple_of(n * CHUNK, CHUNK), CHUNK)
            q = q_ref[rows, :]
            qf = q.astype(F32)
            k8f = k_ref[rows, :].astype(F32) * Q_SCALE
            k8 = k8f.astype(BF16)
            v = v_ref[rows, :]
            dy = dy_s[rows, :]
            q2, dy2 = _stack_pair(is_a, q), _stack_pair(is_a, dy)
            sc = _dot_nt(q2, k8)
            dp = _dot_nt(dy2, v)
            a2 = (sc * tab["d2"]).astype(BF16)
            ds2 = (dp * tab["d2"]).astype(BF16)
            dq = _unstack_pair(is_a, _dot_nn(ds2, k8))
            dk = _dot_tn(ds2, q2)
            dv = _dot_tn(a2, dy2)
            prod = sc * dp
            pf, pb = prod * tab["df2"], prod * tab["db2"]
            ifa, ifb = ifa + colsum(pf[0:CHUNK, :]), ifb + colsum(pf[CHUNK:2 * CHUNK, :])
            iba, ibb = iba + colsum(pb[0:CHUNK, :]), ibb + colsum(pb[CHUNK:2 * CHUNK, :])
            states, sgrads = st[n], gr[n]
            sb, gb = states.astype(BF16), sgrads.astype(BF16)
            dqc = _dot_nt(dy, sb) * tab["qdec"]
            dkc = _dot_nt(v, gb) * tab["kdec"]
            dv = dv + _dot_nn(_both_ways(k8f, tab["kdec"]), gb)
            dq_ref[rows, :] = (dq + dqc[:, 0:LANES] + dqc[:, LANES:2 * LANES]).astype(BF16)
            dk_ref[rows, :] = ((dk + dkc[:, 0:LANES] + dkc[:, LANES:2 * LANES]) * Q_SCALE).astype(BF16)
            dv_ref[rows, :] = dv.astype(BF16)
            q2w, k2w = jnp.concatenate([qf, qf], axis=1), jnp.concatenate([k8f, k8f], axis=1)
            xfb = xfb + colsum(tab["qexp"] * q2w * dqc + tab["kexp"] * k2w * dkc)
            prod_s = sgrads * states
            lf, lb = lf + colsum(prod_s[0:LANES, :]), lb + colsum(prod_s[LANES:2 * LANES, :])
            return xfb, ifa, ifb, iba, ibb, lf, lb

        sink_col = _sink_column(sink_ref, g)
        head_row = lax.broadcasted_iota(jnp.int32, dsink_ref.shape, 0)

        def attention_block(blk):
            n = half * blocks_half + blk
            rows = pl.ds(pl.multiple_of(blk * BLOCK, BLOCK), BLOCK)
            keys = pl.ds(pl.multiple_of(n * BLOCK, BLOCK), KEYS)
            qm = _stack_heads(aq_ref[rows, :], g)
            k3, v3 = kpad[keys, :], vpad[keys, :]
            p, p_sink = _attn_probs(qm, k3, bias, sink_col, n, s)
            dom = _stack_heads(do_ref[rows, :], g)
            dp = _dot_nt(dom, v3)
            delta = jnp.sum(p * dp, axis=1, keepdims=True)
            ds_mat = (p * (dp - delta)).astype(BF16)
            daq_ref[rows, :] = _unstack_heads(_dot_nn(ds_mat, k3), g).astype(BF16)
            dk_acc[keys, :] += _dot_tn(ds_mat, qm) * Q_SCALE
            dv_acc[keys, :] += _dot_tn(p.astype(BF16), dom)
            w = p_sink * delta
            upd = jnp.zeros(dsink_ref.shape, F32)
            for h in range(GROUP):
                upd = upd + jnp.where(head_row == h, -jnp.sum(w[h * BLOCK:(h + 1) * BLOCK, :]), 0.0)
            dsink_ref[...] += upd

        dgain = _chunk_loop(n_chunks, norm_body, jnp.zeros((1, LANES), F32), BWD_PREP_UNROLL)
        _scan_states(n_chunks, gr, BWD_ROWS, tab["lam_b"], FWD_ROWS, tab["lam_f"])

        def trip(i, carry):
            chunk, blk = 0, 0
            for kind in _trip_order(BWD_ORDER * (unroll // RET_UNROLL), unroll, per_trip):
                if kind == "r":
                    carry = grad_body(i * unroll + chunk, carry)
                    chunk += 1
                else:
                    attention_block(i * per_trip + blk)
                    blk += 1
            return carry

        z = jnp.zeros((1, LANES), F32)
        init = (jnp.zeros((1, 2 * LANES), F32), z, z, z, z, z, z)
        xfb, ifa, ifb, iba, ibb, lf, lb = lax.fori_loop(0, trips, trip, init)
        st_ref[...] = jnp.zeros_like(st_ref)
        st_ref[ST_GAIN:ST_GAIN + 1, :] = dgain
        st_ref[ST_XF:ST_XF + 1, :] = xfb[:, 0:LANES]
        st_ref[ST_XB:ST_XB + 1, :] = xfb[:, LANES:2 * LANES]
        st_ref[ST_IFA:ST_IFA + 1, :] = ifa
        st_ref[ST_IFB:ST_IFB + 1, :] = ifb
        st_ref[ST_IBA:ST_IBA + 1, :] = iba
        st_ref[ST_IBB:ST_IBB + 1, :] = ibb
        st_ref[ST_LF:ST_LF + 1, :] = lf * (CHUNK * tab["lam_f"])
        st_ref[ST_LB:ST_LB + 1, :] = lb * (CHUNK * tab["lam_b"])

        @pl.when(pair == pairs - 1)
        def _():
            dak_ref[...] = dk_acc[BLOCK:BLOCK + s, :].astype(BF16)
            dav_ref[...] = dv_acc[BLOCK:BLOCK + s, :].astype(BF16)

    lane_blk = lambda c0: _seq_spec(s, c0 // LANES)
    seq0 = _seq_spec(s, 0)
    half_rows = blocks_half * BLOCK
    aq_spec = pl.BlockSpec((half_rows, GW), lambda b, h: (b * PAIRS_PER_KV + (h & 1), C_AQ // GW + h // 2))
    a_spec = pl.BlockSpec((half_rows, GW), lambda b, h: (b * PAIRS_PER_KV + (h & 1), h // 2))
    kv_spec = lambda c0: pl.BlockSpec((s, LANES), lambda b, h: (b, c0 // LANES))
    kv_out = pl.BlockSpec((s, LANES), lambda b, h: (b, 0))
    state = pltpu.VMEM((n_chunks, 2 * LANES, LANES), F32)
    pad = pltpu.VMEM((s + 2 * BLOCK, LANES), BF16)
    acc = pltpu.VMEM((s + 2 * BLOCK, LANES), F32)
    body, lead = _after(after, body)
    return pl.pallas_call(
        body, name="mixers_bwd", grid=(b_loc, pairs),
        in_specs=lead + [_smem_spec(), _smem_spec(), _smem_spec(), lane_blk(C_RQ), lane_blk(C_RK), lane_blk(C_RV),
                         lane_blk(C_RG), seq0, seq0, seq0, pl.BlockSpec((1, LANES), lambda b, h: (0, h)),
                         aq_spec, kv_spec(C_AK), kv_spec(C_AV), a_spec, _state_spec(n_chunks, pairs)],
        out_specs=[seq0] * 4 + [pl.BlockSpec((ST_ROWS, LANES), lambda b, h: (b, h)), a_spec, kv_out, kv_out,
                                pl.BlockSpec((8, LANES), lambda b, h: (b * KV_HEADS + h // 2, 0))],
        out_shape=[SDS((t, RET_W), BF16)] * 4 + [SDS((b_loc * ST_ROWS, RET_W), F32), SDS((t, ATT_W), BF16),
                                                   SDS((t, KV_W), BF16), SDS((t, KV_W), BF16),
                                                   SDS((b_loc * KV_HEADS * 8, LANES), F32)],
        scratch_shapes=[state, pltpu.VMEM((s, LANES), BF16), pad, pad,
                        pltpu.VMEM((3, GROUP * BLOCK, KEYS), F32), acc, acc],
        compiler_params=_params(("arbitrary", "arbitrary")),
    )(*after, lgf, lgb, sink, u, u, u, u, xhat, rstd, dr, gn_gain, u, u, u, da, states)


def _pack_small(acc2, acc1, ret_stats, dsink, b_loc, d):
    pairs = RET_HEADS // 2

    def body(acc2_ref, acc1_ref, st_ref, dsink_ref, out_ref):
        out_ref[...] = jnp.zeros_like(out_ref)
        out_ref[ROW_LN1G:ROW_LN1G + 1, :] = acc1_ref[0:1, :]
        out_ref[ROW_LN1B:ROW_LN1B + 1, :] = acc1_ref[1:2, :]
        out_ref[ROW_LN2G:ROW_LN2G + 1, :] = acc2_ref[1:2, :]
        out_ref[ROW_LN2B:ROW_LN2B + 1, :] = acc2_ref[2:3, :]
        out_ref[ROW_LOSS:ROW_LOSS + 1, :] = acc2_ref[0:1, :]
        st = st_ref[0:ST_ROWS, :]
        for b in range(1, b_loc):
            st = st + st_ref[b * ST_ROWS:(b + 1) * ST_ROWS, :]
        out_ref[ROW_GN:ROW_GN + 1, 0:RET_W] = st[ST_GAIN:ST_GAIN + 1, :]
        lane = lax.broadcasted_iota(jnp.int32, (1, d), 1)
        misc = jnp.zeros((1, d), F32)
        for pr in range(pairs):
            blk = st[:, pr * LANES:(pr + 1) * LANES]
            half = lax.broadcasted_iota(jnp.int32, (1, LANES), 1) < HEAD_DIM
            for h in range(2):
                sel = half if h == 0 else jnp.logical_not(half)
                cross_f = jnp.sum(jnp.where(sel, blk[ST_XF:ST_XF + 1, :] + blk[ST_LF:ST_LF + 1, :], 0.0))
                cross_b = jnp.sum(jnp.where(sel, blk[ST_XB:ST_XB + 1, :] + blk[ST_LB:ST_LB + 1, :], 0.0))
                intra_f = jnp.sum(blk[ST_IFA + h:ST_IFA + h + 1, :])
                intra_b = jnp.sum(blk[ST_IBA + h:ST_IBA + h + 1, :])
                head = 2 * pr + h
                misc = jnp.where(lane == MISC_DF + head, cross_f + intra_f, misc)
                misc = jnp.where(lane == MISC_DB + head, cross_b + intra_b, misc)
        for g in range(KV_HEADS):
            tot = dsink_ref[g * 8:(g + 1) * 8, :]
            for b in range(1, b_loc):
                tot = tot + dsink_ref[(b * KV_HEADS + g) * 8:(b * KV_HEADS + g + 1) * 8, :]
            for h in range(GROUP):
                misc = jnp.where(lane == MISC_SINK + GROUP * g + h, jnp.sum(tot[h:h + 1, 0:1]), misc)
        out_ref[ROW_MISC:ROW_MISC + 1, :] = misc

    return pl.pallas_call(body, name="pack_small", out_shape=SDS((SMALL_ROWS, d), F32))(acc2, acc1, ret_stats, dsink)


BIG = ("w_in", "w_out", "w_ffn_gate", "w_ffn_up", "w_ffn_down", "w_ple_proj", "w_ple_gate")
TRANSPOSED_OUTSIDE = ("w_in", "w_ffn_gate", "w_ffn_up")
TRANSPOSED_HERE = ("w_ple_proj",)
SMALL = ("ret_decay_fwd", "ret_decay_bwd", "ret_gn_gain", "attn_sink", "ln1_gain", "ln1_bias", "ln2_gain", "ln2_bias")
ORDER = ("w_in", "ret_decay_fwd", "ret_decay_bwd", "ret_gn_gain", "attn_sink", "w_out", "ln1_gain", "ln1_bias",
         "w_ffn_gate", "w_ffn_up", "w_ffn_down", "w_ple_proj", "w_ple_gate", "ln2_gain", "ln2_bias")


GATHER_ORDER = ("w_in", "w_ffn_up", "w_out", "w_ffn_gate", "w_ple_gate", "w_ple_proj", "w_ffn_down")
GATHER_TWO_LEVEL = ("w_in", "w_ffn_up", "w_out")

def _local_step(x2, p2, target2, fetch, publish, small, b_loc, me):
    d = x2.shape[1]
    lgf, lgb = _log_decay(small["ret_decay_fwd"], small["ret_decay_bwd"])
    lgf1, lgb1, sink1 = lgf.reshape(-1), lgb.reshape(-1), small["attn_sink"].reshape(-1)
    (w_in,) = fetch(("w_in",), ())
    u, xb = _in_proj(x2, w_in)
    passed = fetch.pass_on(("w_ffn_up", "w_out"), (xb, lgf))
    r, ret_xhat, ret_rstd, a, ret_states = _mixers_fwd(u, lgf1, lgb1, small["ret_gn_gain"], sink1, b_loc, passed)
    w_out, w_gate, w_up = fetch(("w_out", "w_ffn_gate", "w_ffn_up"), (r, a))
    xhat1, rstd1, h1b, dact_dg, dact_du, act = _mix_ln1_ffn_up(
        r, a, x2, w_out, w_gate, w_up, small["ln1_gain"], small["ln1_bias"])
    w_pg, w_pe, w_down = fetch(("w_ple_gate", "w_ple_proj", "w_ffn_down"), (act,))
    own = {}
    dz2, dz2b, dsb, dg, dup, acc2, pe_whole, own["w_ple_proj"], pg_whole, own["w_ple_gate"] = (
        _ffn_down_ln2_loss(me, act, dact_dg, dact_du, h1b, p2, xhat1, target2, w_down, w_pg, w_pe,
                           small["ln1_gain"], small["ln1_bias"], small["ln2_gain"], small["ln2_bias"]))

    def grad(name, parts, rhs, after=()):
        whole, own[name] = _weight_grad("grad_" + name, me, parts, rhs, after)
        return whole

    ffn_jobs = dict(w_ffn_down=(act, dz2b), w_ffn_gate=(dg, h1b), w_ffn_up=(dup, h1b))
    wholes, owns = _weight_grad_jobs("grad_w_ffn", me, list(ffn_jobs.values()))
    own.update(zip(ffn_jobs, owns))
    t2 = publish("ffn", dict(zip(ffn_jobs, wholes), w_ple_proj=pe_whole, w_ple_gate=pg_whole))
    dz1, dz1b, dr, da, acc1 = _dh1_ln1_bwd(
        dz2, dg, dup, dsb, xhat1, rstd1, w_gate, w_up, w_pg, w_out, small["ln1_gain"], t2)
    t3 = publish("out", dict(w_out=grad("w_out", [r, a], dz1b)))
    dq, dk, dv, dgate, ret_stats, daq, dak, dav, dsink = _mixers_bwd(
        u, ret_xhat, ret_rstd, ret_states, dr, da, lgf1, lgb1, small["ret_gn_gain"], sink1, b_loc, t3)
    parts = [dq, dk, dv, dgate, daq, dak, dav]
    small_part = _pack_small(acc2, acc1, ret_stats, dsink, b_loc, d)
    t4 = publish("in", dict(w_in=grad("w_in", parts, xb)), small_part)
    grad_x = _in_proj_bwd(dz1, parts, w_in, t4)
    return grad_x, own, small_part


def kernel(x, p, w_in, ret_decay_fwd, ret_decay_bwd, ret_gn_gain, attn_sink, w_out, ln1_gain, ln1_bias, w_ffn_gate, w_ffn_up, w_ffn_down, w_ple_proj, w_ple_gate, ln2_gain, ln2_bias, loss_target, m_w_in, m_ret_decay_fwd, m_ret_decay_bwd, m_ret_gn_gain, m_attn_sink, m_w_out, m_ln1_gain, m_ln1_bias, m_w_ffn_gate, m_w_ffn_up, m_w_ffn_down, m_w_ple_proj, m_w_ple_gate, m_ln2_gain, m_ln2_bias, v_w_in, v_ret_decay_fwd, v_ret_decay_bwd, v_ret_gn_gain, v_attn_sink, v_w_out, v_ln1_gain, v_ln1_bias, v_w_ffn_gate, v_w_ffn_up, v_w_ffn_down, v_w_ple_proj, v_w_ple_gate, v_ln2_gain, v_ln2_bias):
    given = dict(locals())

    def strip(n, a):
        if n not in BIG:
            return a
        return a[0].T if n in TRANSPOSED_OUTSIDE else a[0]

    def restore(n, a):
        if n not in BIG:
            return a
        return (a.T if n in TRANSPOSED_OUTSIDE else a)[None]

    w = {n: strip(n, given[n]) for n in ORDER}
    m = {n: strip(n, given["m_" + n]) for n in ORDER}
    v = {n: strip(n, given["v_" + n]) for n in ORDER}
    b_loc, s, d = x.shape
    x2 = x.reshape(b_loc * s, d)
    p2 = p[0].reshape(b_loc * s, p.shape[-1])
    target2 = loss_target.reshape(b_loc * s, d)

    small = {n: w[n] for n in SMALL}
    me = (4 * lax.axis_index("x") + 2 * lax.axis_index("y") + lax.axis_index("c")).astype(jnp.int32).reshape(1)

    gather = _gather_start(
        {n: w[n] for n in GATHER_ORDER},
        [_gather_copy_near if n in GATHER_TWO_LEVEL else _gather_copy for n in GATHER_ORDER])

    passing = {}

    def pass_on(names, after):
        relayed = _gather_relay("gather_relay_" + names[0], gather, [GATHER_ORDER.index(n) for n in names], list(after))
        passing.update({n: (relayed, j) for j, n in enumerate(names)})
        return (relayed["token"],)

    def fetch(names, after):
        out = {}
        for n in [n for n in names if n in GATHER_TWO_LEVEL]:
            if n not in passing:
                pass_on((n,), after)
            relayed, j = passing[n]
            out[n] = _split_copy_wait("gather_wait_" + n, relayed, [j], list(after))[0][0]
        direct = [n for n in names if n not in GATHER_TWO_LEVEL]
        if direct:
            got = _split_copy_wait("gather_wait_" + direct[0], gather, [GATHER_ORDER.index(n) for n in direct],
                                   list(after))
            out.update({n: item[0] for n, item in zip(direct, got)})
        return [out[n] for n in names]

    scatters = []

    def publish(tag, products, small_sums=None):
        items = [(products[n], lax.empty((N_DEV - 1, products[n].shape[0] // N_DEV, products[n].shape[1]), BF16))
                 for n in products]
        copies = [_scatter_copy] * len(items)
        if small_sums is not None:
            items.append((small_sums, lax.empty((N_DEV - 1,) + small_sums.shape, F32)))
            copies.append(_small_copy)
        started = _split_copy_start("scatter_start_" + tag, items, copies)
        scatters.append((list(products), small_sums is not None, started))
        return (started["token"],)

    fetch.pass_on = pass_on
    grad_x, own, small_part = _local_step(x2, p2, target2, fetch, publish, small, b_loc, me)

    out_g, out_d, out_m, out_v = {}, {}, {}, {}
    after = [grad_x]
    for names, with_small, started in scatters:
        landed = _split_copy_wait("scatter_wait_" + names[0], started, list(range(len(started["items"]))), after)
        if with_small:
            mine, from_peers = landed[-1]
            loss, sg, sd, sm, sv = _small_adamw(
                me, mine, from_peers, small, {n: m[n] for n in SMALL}, {n: v[n] for n in SMALL})
            for dst, src in ((out_g, sg), (out_d, sd), (out_m, sm), (out_v, sv)):
                dst.update(src)
        recv = {n: item[1] for n, item in zip(names, landed)}
        alike = {}
        for n in names:
            alike.setdefault((own[n].shape, n in TRANSPOSED_HERE), []).append(n)
        for (_, transposed), ns in alike.items():
            res = _reduce_adamw(ns[0], [own[n] for n in ns], [recv[n] for n in ns], [w[n] for n in ns],
                                [m[n] for n in ns], [v[n] for n in ns], transposed)
            for dst, vals in zip((out_g, out_d, out_m, out_v), res):
                dst.update(zip(ns, vals))
        after = [out_v[names[-1]]]

    outs = [loss[0, 0], grad_x.reshape(x.shape)]
    for group in (out_g, out_d, out_m, out_v):
        outs += [restore(n, group[n]) for n in ORDER]
    return tuple(outs)
```

```python
import functools

import jax
import jax.numpy as jnp
from jax import lax
from jax.experimental import pallas as pl
from jax.experimental.pallas import tpu as pltpu

F32, BF16 = jnp.float32, jnp.bfloat16
SDS = jax.ShapeDtypeStruct
MESH = pl.DeviceIdType.MESH

N_DEV = 8
HEAD_DIM = 64
RET_HEADS = 8
ATTN_HEADS = 8
KV_HEADS = 2
GROUP = ATTN_HEADS // KV_HEADS
RET_W = RET_HEADS * HEAD_DIM
ATT_W = ATTN_HEADS * HEAD_DIM
KV_W = KV_HEADS * HEAD_DIM
LANES = 128
CHUNK = 128
BLOCK = 128
Q_SCALE = HEAD_DIM ** -0.5
ALPHA = 2.0 ** 0.25
LN_EPS = 1e-5
GN_EPS = 1e-5
NEG_INF = -1e30
C_RQ, C_RK, C_RV, C_RG = 0, RET_W, 2 * RET_W, 3 * RET_W
C_AQ = 4 * RET_W
C_AK = C_AQ + ATT_W
C_AV = C_AK + KV_W
IN_W = C_AV + KV_W

ADAM_LR = 0.001
ADAM_B1 = 0.9
ADAM_B2 = 0.999
ADAM_EPS = 1e-08
ADAM_WD = 0.01
ADAM_STEP = 10

VMEM_LIMIT = 56 * 1024 * 1024
MATMUL_ROWS = 512
EPILOGUE_ROWS = 256
SUB_ROWS = 512
SMALL_ROWS = 16
ROW_LN1G, ROW_LN1B, ROW_LN2G, ROW_LN2B, ROW_LOSS, ROW_GN, ROW_MISC = 0, 1, 2, 3, 4, 5, 6
MISC_DF, MISC_DB, MISC_SINK = 0, 8, 16


def _dot_nn(a, b):
    return lax.dot_general(a, b, (((1,), (0,)), ((), ())), preferred_element_type=F32)


def _dot_nt(a, b):
    return lax.dot_general(a, b, (((1,), (1,)), ((), ())), preferred_element_type=F32)


def _dot_tn(a, b):
    return lax.dot_general(a, b, (((0,), (0,)), ((), ())), preferred_element_type=F32)


def _params(sem=None, vmem=VMEM_LIMIT):
    kw = {"vmem_limit_bytes": vmem}
    if sem is not None:
        kw["dimension_semantics"] = sem
    return pltpu.CompilerParams(**kw)


def _row_tile(t, want=512):
    tm = want
    while t % tm:
        tm //= 2
    return tm


def _sigmoid(x):
    return jax.nn.sigmoid(x)


def _layer_norm_stats(z):
    mu = jnp.mean(z, axis=1, keepdims=True)
    d = z - mu
    var = jnp.mean(d * d, axis=1, keepdims=True)
    rstd = lax.rsqrt(var + LN_EPS)
    return d * rstd, rstd


def _layer_norm_bwd(dxh, xhat, rstd):
    m1 = jnp.mean(dxh, axis=1, keepdims=True)
    m2 = jnp.mean(dxh * xhat, axis=1, keepdims=True)
    return rstd * (dxh - m1 - xhat * m2)


def _mesh_pos():
    return lax.axis_index("x"), lax.axis_index("y"), lax.axis_index("c")


HBM_SPEC = pl.BlockSpec(memory_space=pltpu.HBM)
SEM_SPEC = pl.BlockSpec(memory_space=pltpu.SEMAPHORE)
ANY_SPEC = pl.BlockSpec(memory_space=pl.ANY)
SIDE_EFFECT = pltpu.SideEffectType.DATAFLOW_SIDE_EFFECTING
PEER_SEMS = pltpu.SemaphoreType.DMA((N_DEV - 1,))


def _in_hbm(a):
    return pltpu.with_memory_space_constraint(a, pltpu.HBM)


def _split_copy_start(name, items, copies):
    n = len(items)
    flat = [a for it in items for a in it]
    k = len(flat)

    def body(*refs):
        arr, sems = list(refs[:k]), refs[k:k + 2 * n]
        for i, it in enumerate(items):
            mine = [arr.pop(0) for _ in it]
            for m in range(1, N_DEV):
                cp = copies[i](m, mine, sems[i].at[m - 1], sems[n + i].at[m - 1])
                if cp is not None:
                    cp.start()
        token = refs[-1]
        token[...] = jnp.zeros_like(token)

    res = pl.pallas_call(
        body, name=name,
        out_shape=[PEER_SEMS] * (2 * n) + [pltpu.HBM(a.shape, a.dtype) for a in flat] + [SDS((8, LANES), F32)],
        in_specs=[HBM_SPEC] * k,
        out_specs=[SEM_SPEC] * (2 * n) + [HBM_SPEC] * k + [pl.BlockSpec(memory_space=pltpu.VMEM)],
        input_output_aliases={j: 2 * n + j for j in range(k)},
        compiler_params=pltpu.CompilerParams(has_side_effects=SIDE_EFFECT),
    )(*[_in_hbm(a) for a in flat])
    thru, out_items = list(res[2 * n:2 * n + k]), []
    for it in items:
        out_items.append(tuple(thru.pop(0) for _ in it))
    return dict(send=res[:n], recv=res[n:2 * n], items=out_items, token=res[-1], copies=copies)


def _gather_start(shards, copies):
    names = list(shards)
    n = len(names)
    flip = [name in TRANSPOSED_HERE for name in names]
    shapes = [shards[name].shape[::-1] if f else shards[name].shape for name, f in zip(names, flip)]
    most = (max(s[0] for s in shapes), max(s[1] for s in shapes))

    def body(*refs):
        src, sems, land, token = refs[:n], refs[n:3 * n], refs[3 * n:4 * n], refs[4 * n]
        wide, narrow, sem = refs[4 * n + 1:]
        for i, (rows, cols) in enumerate(shapes):
            raw = wide.at[0:cols, 0:rows] if flip[i] else wide.at[0:rows, 0:cols]
            bring = pltpu.make_async_copy(src[i], raw, sem.at[0])
            bring.start()
            bring.wait()
            narrow[0:rows, 0:cols] = (raw[...].T if flip[i] else raw[...]).astype(BF16)
            mine = land[i].at[pl.ds(pl.multiple_of(_peer_index(0) * rows, 8), rows), :]
            place = pltpu.make_async_copy(narrow.at[0:rows, 0:cols], mine, sem.at[0])
            place.start()
            place.wait()
            for m in range(1, N_DEV):
                cp = copies[i](m, [land[i]], sems[i].at[m - 1], sems[n + i].at[m - 1])
                if cp is not None:
                    cp.start()
        token[...] = jnp.zeros_like(token)

    side = max(most)
    res = pl.pallas_call(
        body, name="gather_start",
        out_shape=[PEER_SEMS] * (2 * n) + [pltpu.HBM((N_DEV * r, c), BF16) for r, c in shapes] + [SDS((8, LANES), F32)],
        in_specs=[HBM_SPEC] * n,
        out_specs=[SEM_SPEC] * (2 * n) + [HBM_SPEC] * n + [pl.BlockSpec(memory_space=pltpu.VMEM)],
        scratch_shapes=[pltpu.VMEM((side, side), F32), pltpu.VMEM(most, BF16), pltpu.SemaphoreType.DMA((1,))],
        compiler_params=pltpu.CompilerParams(has_side_effects=SIDE_EFFECT),
    )(*[_in_hbm(shards[name]) for name in names])
    return dict(send=res[:n], recv=res[n:2 * n], items=[(a,) for a in res[2 * n:3 * n]], token=res[-1], copies=copies)


def _gather_relay(name, started, which, after):
    lands = [started["items"][w][0] for w in which]
    k = len(lands)

    def body(*refs):
        land_refs, old_send, old_recv = refs[:k], refs[k:2 * k], refs[2 * k:3 * k]
        outs = refs[3 * k + len(after):]
        send, recv, token = outs[:k], outs[k:2 * k], outs[-1]
        for j in range(k):
            for m in range(1, N_DEV):
                cp = _gather_copy_near(m, [land_refs[j]], old_send[j].at[m - 1], old_recv[j].at[m - 1])
                if cp is None:
                    continue
                cp.wait_send()
                cp.wait_recv()
                if m > 1:
                    _gather_copy_pass(m + 1, [land_refs[j]], send[j].at[m], recv[j].at[m]).start()
        token[...] = jnp.zeros_like(token)

    res = pl.pallas_call(
        body, name=name,
        out_shape=[PEER_SEMS] * (2 * k) + [pltpu.HBM(a.shape, a.dtype) for a in lands] + [SDS((8, LANES), F32)],
        in_specs=[HBM_SPEC] * k + [SEM_SPEC] * (2 * k) + [ANY_SPEC] * len(after),
        out_specs=[SEM_SPEC] * (2 * k) + [HBM_SPEC] * k + [pl.BlockSpec(memory_space=pltpu.VMEM)],
        input_output_aliases={j: 2 * k + j for j in range(k)},
        compiler_params=pltpu.CompilerParams(has_side_effects=SIDE_EFFECT),
    )(*lands, *[started["send"][w] for w in which], *[started["recv"][w] for w in which],
      *[_in_hbm(a) for a in after])
    return dict(send=res[:k], recv=res[k:2 * k], items=[(a,) for a in res[2 * k:3 * k]], token=res[-1],
                copies=[_gather_copy_pass] * k)


def _split_copy_wait(name, started, which, after):
    items = [started["items"][i] for i in which]
    copies = [started["copies"][i] for i in which]
    n = len(items)
    flat = [a for it in items for a in it]
    k = len(flat)

    def body(*refs):
        arr, sems = list(refs[:k]), refs[k:k + 2 * n]
        for i, it in enumerate(items):
            mine = [arr.pop(0) for _ in it]
            for m in range(1, N_DEV):
                cp = copies[i](m, mine, sems[i].at[m - 1], sems[n + i].at[m - 1])
                if cp is not None:
                    cp.wait_send()
                    cp.wait_recv()

    res = pl.pallas_call(
        body, name=name,
        out_shape=[pltpu.HBM(a.shape, a.dtype) for a in flat],
        in_specs=[HBM_SPEC] * k + [SEM_SPEC] * (2 * n) + [ANY_SPEC] * len(after),
        out_specs=[HBM_SPEC] * k,
        input_output_aliases={j: j for j in range(k)},
        compiler_params=pltpu.CompilerParams(has_side_effects=SIDE_EFFECT),
    )(*flat, *[started["send"][i] for i in which], *[started["recv"][i] for i in which], *[_in_hbm(a) for a in after])
    thru, out_items = list(res), []
    for it in items:
        out_items.append(tuple(thru.pop(0) for _ in it))
    return out_items


def _gather_copy(m, refs, send_sem, recv_sem):
    (land_ref,) = refs
    r = land_ref.shape[0] // N_DEV
    mine = land_ref.at[pl.ds(pl.multiple_of(_peer_index(0) * r, 8), r), :]
    return pltpu.make_async_remote_copy(src_ref=mine, dst_ref=mine, send_sem=send_sem, recv_sem=recv_sem,
                                        device_id=_peer(m), device_id_type=MESH)


def _gather_copy_near(m, refs, send_sem, recv_sem):
    return _gather_copy(m, refs, send_sem, recv_sem) if m == 1 or m % 2 == 0 else None


def _gather_copy_pass(m, refs, send_sem, recv_sem):
    if m == 1 or m % 2 == 0:
        return None
    (land_ref,) = refs
    r = land_ref.shape[0] // N_DEV
    block = land_ref.at[pl.ds(pl.multiple_of(_peer_index(m ^ 1) * r, 8), r), :]
    return pltpu.make_async_remote_copy(src_ref=block, dst_ref=block, send_sem=send_sem, recv_sem=recv_sem,
                                        device_id=_peer(1), device_id_type=MESH)


def _small_copy(m, refs, send_sem, recv_sem):
    part_ref, land_ref = refs
    return pltpu.make_async_remote_copy(src_ref=part_ref, dst_ref=land_ref.at[m - 1], send_sem=send_sem,
                                        recv_sem=recv_sem, device_id=_peer(m), device_id_type=MESH)


def _scatter_copy(m, refs, send_sem, recv_sem):
    buf_ref, land_ref = refs
    r = buf_ref.shape[0] // N_DEV
    src = buf_ref.at[pl.ds(pl.multiple_of(_peer_index(m) * r, 8), r), :]
    return pltpu.make_async_remote_copy(src_ref=src, dst_ref=land_ref.at[m - 1], send_sem=send_sem,
                                        recv_sem=recv_sem, device_id=_peer(m), device_id_type=MESH)


def _peer(m):
    x, y, c = _mesh_pos()
    bx, by, bc = (m >> 2) & 1, (m >> 1) & 1, m & 1
    return (x ^ bx if bx else x, y ^ by if by else y, c ^ bc if bc else c)


def _peer_index(m):
    x, y, c = _mesh_pos()
    return (4 * x + 2 * y + c) ^ m


SMALL_PLACE = {
    "ln1_gain": (ROW_LN1G, 0), "ln1_bias": (ROW_LN1B, 0), "ln2_gain": (ROW_LN2G, 0), "ln2_bias": (ROW_LN2B, 0),
    "ret_gn_gain": (ROW_GN, 0), "ret_decay_fwd": (ROW_MISC, MISC_DF), "ret_decay_bwd": (ROW_MISC, MISC_DB),
    "attn_sink": (ROW_MISC, MISC_SINK)}


def _small_adamw(me, part, landed, w, m, v):
    d = part.shape[1]
    names = list(SMALL_PLACE)
    k = len(names)

    def body(*refs):
        me_ref, part_ref, land_ref = refs[:3]
        refs = refs[2:]
        w_refs, m_refs, v_refs = refs[1:1 + k], refs[1 + k:1 + 2 * k], refs[1 + 2 * k:1 + 3 * k]
        outs = refs[1 + 3 * k:1 + 7 * k + 1]
        tot_ref = refs[-1]
        loss_ref, g_refs, dl_refs = outs[0], outs[1:1 + k], outs[1 + k:1 + 2 * k]
        nm_refs, nv_refs = outs[1 + 2 * k:1 + 3 * k], outs[1 + 3 * k:1 + 4 * k]
        tot = jnp.zeros(part_ref.shape, F32)
        for dev in range(N_DEV):
            j = dev ^ me_ref[0]
            tot = tot + jnp.where(j == 0, part_ref[...], land_ref[jnp.maximum(j, 1) - 1])
        tot_ref[...] = tot
        loss_ref[...] = (0.5 / d) * jnp.sum(tot_ref[ROW_LOSS:ROW_LOSS + 1, :], axis=1, keepdims=True)
        for i, name in enumerate(names):
            row, lo = SMALL_PLACE[name]
            wv = w_refs[i][...]
            g = tot_ref[row:row + 1, lo:lo + wv.shape[1]]
            if name.startswith("ret_decay"):
                p2 = jnp.exp2(wv)
                g = g * (-p2 * jnp.log(2.0) / (1.0 - p2))
            g_refs[i][...] = g
            _adamw_store(g, wv, m_refs[i][...], v_refs[i][...], dl_refs[i], nm_refs[i], nv_refs[i])

    shapes = [SDS(w[n].shape, F32) for n in names]
    vm = pl.BlockSpec(memory_space=pltpu.VMEM)
    res = pl.pallas_call(
        body, name="small_adamw", out_shape=[SDS((1, 1), F32)] + shapes * 4,
        in_specs=[_smem_spec()] + [vm] * (2 + 3 * k), out_specs=[vm] * (1 + 4 * k),
        scratch_shapes=[pltpu.VMEM(part.shape, F32)],
    )(me, part, landed, *[w[n] for n in names], *[m[n] for n in names], *[v[n] for n in names])
    groups = [dict(zip(names, res[1 + j * k:1 + (j + 1) * k])) for j in range(4)]
    return (res[0], *groups)


def _adamw_store(g, w, m, v, dl_ref, nm_ref, nv_ref):
    m = ADAM_B1 * m + (1.0 - ADAM_B1) * g
    v = ADAM_B2 * v + (1.0 - ADAM_B2) * (g * g)
    m_hat = m / (1.0 - ADAM_B1 ** ADAM_STEP)
    v_hat = v / (1.0 - ADAM_B2 ** ADAM_STEP)
    dl_ref[...] = -ADAM_LR * (m_hat / (jnp.sqrt(v_hat) + ADAM_EPS) + ADAM_WD * w)
    nm_ref[...] = m
    nv_ref[...] = v


def _reduce_adamw(name, owns, recvs, ws, ms, vs, transposed):
    count = len(owns)
    rows, n = owns[0].shape
    steps = 1 if transposed or rows % 16 else 2
    rb = rows // steps

    def body(*refs):
        ins, outs = refs[:5 * count], refs[5 * count:]
        j = pl.program_id(0)
        for k in range(count):
            @pl.when(j == k)
            def _(k=k):
                own_ref, recv_ref, w_ref, m_ref, v_ref = ins[5 * k:5 * k + 5]
                g_ref, dl_ref, nm_ref, nv_ref = outs[4 * k:4 * k + 4]
                g = own_ref[...]
                for p in range(recv_ref.shape[0]):
                    g = g + recv_ref[p].astype(F32)
                if transposed:
                    g = g.T
                g_ref[...] = g
                _adamw_store(g, w_ref[...], m_ref[...], v_ref[...], dl_ref, nm_ref, nv_ref)

    def turn(k):
        return lambda j, i: jnp.where(j == k, i, jnp.where(j < k, 0, steps - 1))

    in_specs, out_specs = [], []
    for k in range(count):
        at = turn(k)
        blk = pl.BlockSpec(ws[0].shape if transposed else (rb, n), lambda j, i, at=at: (at(j, i), 0))
        in_specs += [pl.BlockSpec((rb, n), lambda j, i, at=at: (at(j, i), 0)),
                     pl.BlockSpec((recvs[k].shape[0], rb, n), lambda j, i, at=at: (0, at(j, i), 0)), blk, blk, blk]
        out_specs += [blk] * 4
    res = pl.pallas_call(
        body, name="adamw_" + name, grid=(count, steps), in_specs=in_specs, out_specs=out_specs,
        out_shape=[SDS(ws[0].shape, F32)] * (4 * count), compiler_params=_params(("arbitrary", "arbitrary")),
    )(*[a for k in range(count) for a in (owns[k], recvs[k], ws[k], ms[k], vs[k])])
    return [list(res[j::4]) for j in range(4)]


def _row_spec(tm, width):
    return pl.BlockSpec((tm, width), lambda i: (i, 0))


def _full_spec(shape):
    return pl.BlockSpec(shape, lambda i: (0,) * len(shape))


_acc_spec = _full_spec


def _sub_rows(tm):
    step = min(SUB_ROWS, tm)
    return [(lo, lo + step) for lo in range(0, tm, step)]


def _in_proj(x2, wt_in):
    t, d = x2.shape
    u_w = wt_in.shape[0]
    tm = _row_tile(t, MATMUL_ROWS)

    def body(x_ref, w_ref, u_ref, xb_ref):
        xb = x_ref[...].astype(BF16)
        xb_ref[...] = xb
        u_ref[...] = _dot_nt(xb, w_ref[...]).astype(BF16)

    return pl.pallas_call(
        body, name="in_proj", grid=(t // tm,),
        in_specs=[_row_spec(tm, d), _full_spec(wt_in.shape)],
        out_specs=[_row_spec(tm, u_w), _row_spec(tm, d)],
        out_shape=[SDS((t, u_w), BF16), SDS((t, d), BF16)],
        compiler_params=_params(("parallel",)),
    )(x2, wt_in)


def _col_halves(f):
    n = f // LANES
    k = (n + 1) // 2 * LANES
    return [(0, k), (k, f)] if k < f else [(0, f)]


def _mix_ln1_ffn_up(r, a, x2, w_out, wt_gate, wt_up, g1, b1):
    t, d = x2.shape
    f = wt_gate.shape[0]
    tm = _row_tile(t, EPILOGUE_ROWS)

    def body(r_ref, a_ref, x_ref, wo_ref, wg_ref, wu_ref, g_ref, b_ref, xh_ref, rs_ref, hb_ref, dg_ref, du_ref,
             act_ref):
        mix = _dot_nn(r_ref[...], wo_ref[0:RET_W, :]) + _dot_nn(a_ref[...], wo_ref[RET_W:RET_W + ATT_W, :])
        z = ALPHA * x_ref[...] + mix
        xhat, rstd = _layer_norm_stats(z)
        xh_ref[...] = xhat
        rs_ref[...] = jnp.broadcast_to(rstd, rs_ref.shape)
        h = (xhat * g_ref[...] + b_ref[...]).astype(BF16)
        hb_ref[...] = h
        g = _dot_nt(h, wg_ref[...])
        u = _dot_nt(h, wu_ref[...])
        sg = _sigmoid(g)
        silu = g * sg
        dg_ref[...] = (u * (sg * (1.0 + g * (1.0 - sg)))).astype(BF16)
        du_ref[...] = silu.astype(BF16)
        act_ref[...] = (silu * u).astype(BF16)

    wide, narrow = _row_spec(tm, f), _row_spec(tm, d)
    return pl.pallas_call(
        body, name="mix_ln1_ffn_up", grid=(t // tm,),
        in_specs=[_row_spec(tm, RET_W), _row_spec(tm, ATT_W), narrow, _resident_spec(w_out.shape),
                  _resident_spec(wt_gate.shape), _resident_spec(wt_up.shape), _full_spec(g1.shape),
                  _full_spec(b1.shape)],
        out_specs=[narrow, _row_spec(tm, LANES), narrow, wide, wide, wide],
        out_shape=[SDS((t, d), F32), SDS((t, LANES), F32), SDS((t, d), BF16)] + [SDS((t, f), BF16)] * 3,
        compiler_params=_params(("parallel",)),
    )(r, a, x2, w_out, wt_gate, wt_up, g1, b1)


def _ffn_down_ln2_loss(me, act, dact_dg, dact_du, h1b, p2, xhat1, target, w_down, w_pg, wt_pe, g1, b1, g2, b2):
    t, d = xhat1.shape
    f = act.shape[1]
    pdim = p2.shape[1]
    tm = _row_tile(t, EPILOGUE_ROWS)
    n_steps = t // tm
    own_rows = d // N_DEV

    def body(me_ref, act_ref, fg_ref, fu_ref, hb_ref, p_ref, xh1_ref, tgt_ref, wd_ref, wpg_ref, wpe_ref, g1_ref,
             b1_ref, g2_ref, b2_ref, dz_ref, dzb_ref, ds_ref, dg_ref, du_ref, acc_ref,
             pe_whole, pe_own, pg_whole, pg_own, pe_acc, pg_acc):
        @pl.when(pl.program_id(0) == 0)
        def _():
            acc_ref[...] = jnp.zeros_like(acc_ref)
            pe_acc[...] = jnp.zeros_like(pe_acc)
            pg_acc[...] = jnp.zeros_like(pg_acc)

        for lo, hi in _sub_rows(tm):
            h1 = xh1_ref[lo:hi, :] * g1_ref[...] + b1_ref[...]
            pb = p_ref[lo:hi, :].astype(BF16)
            pg = _sigmoid(_dot_nn(hb_ref[lo:hi, :], wpg_ref[...]))
            ple = _dot_nt(pb, wpe_ref[...])
            gated = pg * ple
            dgate = gated * (1.0 - pg)
            ffn = _dot_nn(act_ref[lo:hi, :], wd_ref[...])
            z2 = ALPHA * h1 + gated + ffn
            xhat2, rstd2 = _layer_norm_stats(z2)
            err = xhat2 * g2_ref[...] + b2_ref[...] - tgt_ref[lo:hi, :]
            dy = err * (1.0 / d)
            dz = _layer_norm_bwd(dy * g2_ref[...], xhat2, rstd2)
            dzb = dz.astype(BF16)
            dz_ref[lo:hi, :] = dz
            dzb_ref[lo:hi, :] = dzb
            dsb, dpleb = (dz * dgate).astype(BF16), (dz * pg).astype(BF16)
            ds_ref[lo:hi, :] = dsb
            pe_acc[...] += _dot_tn(dpleb, pb)
            pg_acc[...] += _dot_tn(hb_ref[lo:hi, :], dsb)
            acc_ref[0:1, :] += jnp.sum(err * err, axis=0, keepdims=True)
            acc_ref[1:2, :] += jnp.sum(dy * xhat2, axis=0, keepdims=True)
            acc_ref[2:3, :] += jnp.sum(dy, axis=0, keepdims=True)
            for c0, c1 in _col_halves(f):
                da = _dot_nt(dzb, wd_ref[c0:c1, :])
                dg_ref[lo:hi, c0:c1] = (da * fg_ref[lo:hi, c0:c1].astype(F32)).astype(BF16)
                du_ref[lo:hi, c0:c1] = (da * fu_ref[lo:hi, c0:c1].astype(F32)).astype(BF16)

        @pl.when(pl.program_id(0) == n_steps - 1)
        def _():
            mine = pl.ds(pl.multiple_of(me_ref[0] * own_rows, 8), own_rows)
            for whole, own, acc in ((pe_whole, pe_own, pe_acc), (pg_whole, pg_own, pg_acc)):
                whole[...] = acc[...].astype(BF16)
                own[...] = acc[mine, :]

    vec = _full_spec(g1.shape)
    wide, narrow = _row_spec(tm, f), _row_spec(tm, d)
    products = [(d, pdim), (own_rows, pdim), (d, d), (own_rows, d)]
    return pl.pallas_call(
        body, name="ffn_down_ln2_loss", grid=(n_steps,),
        in_specs=[_smem_spec(), wide, wide, wide, narrow, _row_spec(tm, pdim), narrow, narrow,
                  _full_spec(w_down.shape), _full_spec(w_pg.shape), _full_spec(wt_pe.shape), vec, vec, vec, vec],
        out_specs=[narrow] * 3 + [wide, wide, _acc_spec((8, d))] + [_full_spec(s) for s in products],
        out_shape=[SDS((t, d), F32), SDS((t, d), BF16), SDS((t, d), BF16),
                   SDS((t, f), BF16), SDS((t, f), BF16), SDS((8, d), F32)]
        + [SDS(s, BF16 if k % 2 == 0 else F32) for k, s in enumerate(products)],
        scratch_shapes=[pltpu.VMEM((d, pdim), F32), pltpu.VMEM((d, d), F32)],
        compiler_params=_params(("arbitrary",)),
    )(me, act, dact_dg, dact_du, h1b, p2, xhat1, target, w_down, w_pg, wt_pe, g1, b1, g2, b2)


def _after(after, body):
    k = len(after)
    return (lambda *refs: body(*refs[k:])), [ANY_SPEC] * k


def _resident_spec(shape):
    return pl.BlockSpec(shape, lambda i: (0,) * len(shape), pipeline_mode=pl.Buffered(1))


def _dh1_ln1_bwd(me, dz2, dg, dup, dsb, xhat1, rstd1, r, a, wt_gate, wt_up, w_pg, w_out, g1, after=()):
    t, d = dz2.shape
    f = dg.shape[1]
    tm = _row_tile(t, EPILOGUE_ROWS)
    n_steps = t // tm
    wo_rows = RET_W + ATT_W
    own_rows = wo_rows // N_DEV

    def body(me_ref, dz_ref, dg_ref, du_ref, ds_ref, xh1_ref, rs1_ref, r_ref, a_ref, wg_ref, wu_ref, wpg_ref,
             wo_ref, g1_ref, dz1_ref, dz1b_ref, dr_ref, da_ref, acc_ref, wo_whole, wo_own, wo_acc):
        @pl.when(pl.program_id(0) == 0)
        def _():
            acc_ref[...] = jnp.zeros_like(acc_ref)
            wo_acc[...] = jnp.zeros_like(wo_acc)

        for lo, hi in _sub_rows(tm):
            dh = (ALPHA * dz_ref[lo:hi, :] + _dot_nn(dg_ref[lo:hi, :], wg_ref[...])
                  + _dot_nn(du_ref[lo:hi, :], wu_ref[...]) + _dot_nt(ds_ref[lo:hi, :], wpg_ref[...]))
            xhat, rstd = xh1_ref[lo:hi, :], rs1_ref[lo:hi, 0:1]
            dz1 = _layer_norm_bwd(dh * g1_ref[...], xhat, rstd)
            dz1b = dz1.astype(BF16)
            dz1_ref[lo:hi, :] = dz1
            dz1b_ref[lo:hi, :] = dz1b
            acc_ref[0:1, :] += jnp.sum(dh * xhat, axis=0, keepdims=True)
            acc_ref[1:2, :] += jnp.sum(dh, axis=0, keepdims=True)
            dr_ref[lo:hi, :] = _dot_nt(dz1b, wo_ref[0:RET_W, :]).astype(BF16)
            da_ref[lo:hi, :] = _dot_nt(dz1b, wo_ref[RET_W:RET_W + ATT_W, :]).astype(BF16)
            wo_acc[0:RET_W, :] += _dot_tn(r_ref[lo:hi, :], dz1b)
            wo_acc[RET_W:wo_rows, :] += _dot_tn(a_ref[lo:hi, :], dz1b)

        @pl.when(pl.program_id(0) == n_steps - 1)
        def _():
            wo_whole[...] = wo_acc[...].astype(BF16)
            wo_own[...] = wo_acc[pl.ds(pl.multiple_of(me_ref[0] * own_rows, 8), own_rows), :]

    body, lead = _after(after, body)
    return pl.pallas_call(
        body, name="dh1_ln1_bwd", grid=(n_steps,),
        in_specs=lead + [_smem_spec(), _row_spec(tm, d), _row_spec(tm, f), _row_spec(tm, f), _row_spec(tm, d),
                         _row_spec(tm, d), _row_spec(tm, LANES), _row_spec(tm, RET_W), _row_spec(tm, ATT_W),
                         _resident_spec(wt_gate.shape), _resident_spec(wt_up.shape), _resident_spec(w_pg.shape),
                         _resident_spec(w_out.shape), _full_spec(g1.shape)],
        out_specs=[_row_spec(tm, d), _row_spec(tm, d), _row_spec(tm, RET_W), _row_spec(tm, ATT_W), _acc_spec((8, d)),
                   _resident_spec((wo_rows, d)), _resident_spec((own_rows, d))],
        out_shape=[SDS((t, d), F32), SDS((t, d), BF16), SDS((t, RET_W), BF16), SDS((t, ATT_W), BF16),
                   SDS((8, d), F32), SDS((wo_rows, d), BF16), SDS((own_rows, d), F32)],
        scratch_shapes=[pltpu.VMEM((wo_rows, d), F32)],
        compiler_params=_params(("arbitrary",)),
    )(*after, me, dz2, dg, dup, dsb, xhat1, rstd1, r, a, wt_gate, wt_up, w_pg, w_out, g1)


def _in_proj_bwd(dz1, parts, wt_in, after=()):
    t, d = dz1.shape
    tm = _row_tile(t, MATMUL_ROWS)
    widths = [p.shape[1] for p in parts]

    def body(*refs):
        dz_ref, part_refs, w_ref, dx_ref = refs[0], refs[1:1 + len(parts)], refs[-2], refs[-1]
        acc = ALPHA * dz_ref[...]
        lo = 0
        for p_ref, w in zip(part_refs, widths):
            acc = acc + _dot_nn(p_ref[...], w_ref[lo:lo + w, :])
            lo += w
        dx_ref[...] = acc

    body, lead = _after(after, body)
    return pl.pallas_call(
        body, name="in_proj_bwd", grid=(t // tm,),
        in_specs=lead + [_row_spec(tm, d)] + [_row_spec(tm, w) for w in widths] + [_full_spec(wt_in.shape)],
        out_specs=_row_spec(tm, d), out_shape=SDS((t, d), F32),
        compiler_params=_params(("parallel",)),
    )(*after, dz1, *parts, wt_in)


def _weight_grad(name, me, parts, rhs, after=()):
    t, n = rhs.shape
    widths = [p.shape[1] for p in parts]
    rows = sum(widths)
    own_rows = rows // N_DEV
    tk = _row_tile(t, MATMUL_ROWS)
    n_steps = t // tk
    step = 256

    def body(*refs):
        me_ref, part_refs, rhs_ref = refs[0], refs[1:1 + len(parts)], refs[1 + len(parts)]
        full_ref, own_ref, acc = refs[-3], refs[-2], refs[-1]
        i = pl.program_id(0)

        def products(first):
            b = rhs_ref[...].astype(BF16)
            lo = 0
            for p_ref, w in zip(part_refs, widths):
                for c0 in range(0, w, step):
                    c1 = min(c0 + step, w)
                    val = _dot_tn(p_ref[:, c0:c1].astype(BF16), b)
                    if first:
                        acc[lo + c0:lo + c1, :] = val
                    else:
                        acc[lo + c0:lo + c1, :] += val
                lo += w

        pl.when(i == 0)(functools.partial(products, True))
        pl.when(i > 0)(functools.partial(products, False))

        @pl.when(i == n_steps - 1)
        def _():
            full_ref[...] = acc[...].astype(BF16)
            own_ref[...] = acc[pl.ds(pl.multiple_of(me_ref[0] * own_rows, 8), own_rows), :]

    body, lead = _after(after, body)
    return pl.pallas_call(
        body, name=name, grid=(n_steps,),
        in_specs=lead + [_smem_spec()] + [_row_spec(tk, w) for w in widths] + [_row_spec(tk, n)],
        out_specs=[_full_spec((rows, n)), _full_spec((own_rows, n))],
        out_shape=[SDS((rows, n), BF16), SDS((own_rows, n), F32)],
        scratch_shapes=[pltpu.VMEM((rows, n), F32)],
        compiler_params=_params(("arbitrary",)),
    )(*after, me, *parts, rhs)


def _weight_grad_jobs(name, me, jobs, after=()):
    count = len(jobs)
    t = jobs[0][0].shape[0]
    tk = _row_tile(t, MATMUL_ROWS)
    n_steps = t // tk
    shapes = [(lhs.shape[1], rhs.shape[1]) for lhs, rhs in jobs]
    most_rows, most_cols = max(r for r, _ in shapes), max(n for _, n in shapes)
    step = 256

    def body(*refs):
        me_ref, lhs_refs, rhs_refs = refs[0], refs[1:1 + count], refs[1 + count:1 + 2 * count]
        full_refs, own_refs = refs[1 + 2 * count:1 + 3 * count], refs[1 + 3 * count:1 + 4 * count]
        acc, whole, mine, sems = refs[1 + 4 * count:]
        job, i = pl.program_id(0), pl.program_id(1)

        def leaving(j):
            rows, n = shapes[j]
            return (pltpu.make_async_copy(whole.at[0:rows, 0:n], full_refs[j], sems.at[0]),
                    pltpu.make_async_copy(mine.at[0:rows // N_DEV, 0:n], own_refs[j], sems.at[1]))

        def products(j, first):
            rows, n = shapes[j]
            b = rhs_refs[j][...].astype(BF16)
            for c0 in range(0, rows, step):
                c1 = min(c0 + step, rows)
                val = _dot_tn(lhs_refs[j][:, c0:c1].astype(BF16), b)
                if first:
                    acc[c0:c1, 0:n] = val
                else:
                    acc[c0:c1, 0:n] += val

        def finish(j):
            rows, n = shapes[j]
            own_rows = rows // N_DEV
            if j > 0:
                for cp in leaving(j - 1):
                    cp.wait()
            whole[0:rows, 0:n] = acc[0:rows, 0:n].astype(BF16)
            mine[0:own_rows, 0:n] = acc[pl.ds(pl.multiple_of(me_ref[0] * own_rows, 8), own_rows), 0:n]
            for cp in leaving(j):
                cp.start()
            if j == count - 1:
                for cp in leaving(j):
                    cp.wait()

        for j in range(count):
            pl.when((job == j) & (i == 0))(functools.partial(products, j, True))
            pl.when((job == j) & (i > 0))(functools.partial(products, j, False))
            pl.when((job == j) & (i == n_steps - 1))(functools.partial(finish, j))

    def turn(j):
        return lambda job, i: (jnp.where(job == j, i, jnp.where(job < j, 0, n_steps - 1)), 0)

    body, lead = _after(after, body)
    res = pl.pallas_call(
        body, name=name, grid=(count, n_steps),
        in_specs=lead + [_smem_spec()] + [pl.BlockSpec((tk, rows), turn(j)) for j, (rows, _) in enumerate(shapes)]
        + [pl.BlockSpec((tk, n), turn(j)) for j, (_, n) in enumerate(shapes)],
        out_specs=[ANY_SPEC] * (2 * count),
        out_shape=[SDS((rows, n), BF16) for rows, n in shapes] + [SDS((rows // N_DEV, n), F32) for rows, n in shapes],
        scratch_shapes=[pltpu.VMEM((most_rows, most_cols), F32), pltpu.VMEM((most_rows, most_cols), BF16),
                        pltpu.VMEM((most_rows // N_DEV, most_cols), F32), pltpu.SemaphoreType.DMA((2,))],
        compiler_params=_params(("arbitrary", "arbitrary")),
    )(*after, me, *[lhs for lhs, _ in jobs], *[rhs for _, rhs in jobs])
    return list(res[:count]), list(res[count:])


def _log_decay(decay_f, decay_b):
    def body(f_ref, b_ref, lf_ref, lb_ref):
        lf_ref[...] = jnp.log1p(-jnp.exp2(f_ref[...]))
        lb_ref[...] = jnp.log1p(-jnp.exp2(b_ref[...]))

    return pl.pallas_call(body, name="log_decay", out_shape=[SDS(decay_f.shape, F32)] * 2)(decay_f, decay_b)


def _chunk(ref, n):
    return ref[pl.ds(pl.multiple_of(n * CHUNK, CHUNK), CHUNK), :]


def _group_sum(is_a, v):
    sa = jnp.sum(jnp.where(is_a, v, 0.0), axis=1, keepdims=True)
    sb = jnp.sum(jnp.where(is_a, 0.0, v), axis=1, keepdims=True)
    return jnp.where(is_a, sa, sb)


def _seq_spec(s, col_block):
    return pl.BlockSpec((s, LANES), lambda b, h: (b, col_block + h))


def _smem_spec():
    return pl.BlockSpec(memory_space=pltpu.SMEM)


RET_UNROLL = 4
BWD_MAIN_UNROLL = 8
FWD_PREP_UNROLL = 16
BWD_PREP_UNROLL = 8


def _chunk_loop(n_chunks, body, init, unroll):
    u = unroll if n_chunks % unroll == 0 else 1

    def trip(i, carry):
        for j in range(u):
            carry = body(i * u + j, carry)
        return carry

    return lax.fori_loop(0, n_chunks // u, trip, init)


def _stacked_tables(lgf_ref, lgb_ref, pair):
    lane = lax.broadcasted_iota(jnp.int32, (1, LANES), 1)
    is_a = lane < HEAD_DIM
    lgf = jnp.where(is_a, lgf_ref[2 * pair], lgf_ref[2 * pair + 1])
    lgb = jnp.where(is_a, lgb_ref[2 * pair], lgb_ref[2 * pair + 1])
    row = lax.broadcasted_iota(jnp.int32, (CHUNK, 1), 0).astype(F32)
    kdec_f, qdec_f = jnp.exp(lgf * (CHUNK - 1.0 - row)), jnp.exp(lgf * (row + 1.0))
    kdec_b, qdec_b = jnp.exp(lgb * row), jnp.exp(lgb * (CHUNK - row))
    tab = dict(
        is_a=is_a, row=row, lam_f=jnp.exp(lgf * CHUNK), lam_b=jnp.exp(lgb * CHUNK),
        kdec=jnp.concatenate([kdec_f, kdec_b], axis=1), qdec=jnp.concatenate([qdec_f, qdec_b], axis=1),
        qexp=jnp.concatenate([jnp.broadcast_to(row + 1.0, (CHUNK, LANES)),
                              jnp.broadcast_to(CHUNK - row, (CHUNK, LANES))], axis=1),
        kexp=jnp.concatenate([jnp.broadcast_to(CHUNK - 1.0 - row, (CHUNK, LANES)),
                              jnp.broadcast_to(row, (CHUNK, LANES))], axis=1),
    )
    r = lax.broadcasted_iota(jnp.int32, (2 * LANES, LANES), 0)
    c = lax.broadcasted_iota(jnp.int32, (2 * LANES, LANES), 1)
    tab["diag2"] = ((r & (LANES - 1)) < HEAD_DIM) == (c < HEAD_DIM)
    i2 = lax.broadcasted_iota(jnp.int32, (2 * CHUNK, CHUNK), 0)
    j = lax.broadcasted_iota(jnp.int32, (2 * CHUNK, CHUNK), 1)
    head_b = i2 >= CHUNK
    diff = ((i2 & (CHUNK - 1)) - j).astype(F32)
    up, dn = jnp.maximum(diff, 0.0), jnp.maximum(-diff, 0.0)
    lgf2 = jnp.where(head_b, lgf_ref[2 * pair + 1], lgf_ref[2 * pair])
    lgb2 = jnp.where(head_b, lgb_ref[2 * pair + 1], lgb_ref[2 * pair])
    ef = jnp.where(diff >= 0, jnp.exp(lgf2 * up), 0.0)
    eb = jnp.where(diff <= 0, jnp.exp(lgb2 * dn), 0.0)
    tab["d2"] = ef + eb
    tab["df2"] = ef * up
    tab["db2"] = eb * dn
    return tab


def _stack_pair(is_a, x):
    zero = jnp.zeros_like(x)
    return jnp.concatenate([jnp.where(is_a, x, zero), jnp.where(is_a, zero, x)], axis=0)


def _unstack_pair(is_a, x2):
    return jnp.where(is_a, x2[0:CHUNK, :], x2[CHUNK:2 * CHUNK, :])


def _both_ways(x, dec):
    return (jnp.concatenate([x, x], axis=1) * dec).astype(BF16)


def _scan_states(n_chunks, st, up_rows, up_lam, down_rows, down_lam):
    zero = jnp.zeros((LANES, LANES), F32)

    def up(n, r):
        new = st[n, up_rows, :]
        st[n, up_rows, :] = r
        return r * up_lam + new

    def down(s, r):
        n = n_chunks - 1 - s
        new = st[n, down_rows, :]
        st[n, down_rows, :] = r
        return r * down_lam + new

    lax.fori_loop(0, n_chunks, up, zero)
    lax.fori_loop(0, n_chunks, down, zero)


FWD_ROWS, BWD_ROWS = pl.ds(0, LANES), pl.ds(LANES, LANES)


def _state_spec(n_chunks, pairs):
    return pl.BlockSpec((n_chunks, 2 * LANES, LANES), lambda b, h: (b * pairs + h, 0, 0))


ST_GAIN, ST_XF, ST_XB, ST_IFA, ST_IFB, ST_IBA, ST_IBB, ST_LF, ST_LB = 0, 1, 2, 3, 4, 5, 6, 8, 9
ST_ROWS = 16


GW = GROUP * HEAD_DIM
KEYS = 3 * BLOCK


def _attn_tables(g, bias_ref):
    r = lax.broadcasted_iota(jnp.int32, (GROUP * BLOCK, KEYS), 0)
    kj = lax.broadcasted_iota(jnp.int32, (GROUP * BLOCK, KEYS), 1)
    qi = r & (BLOCK - 1)
    hh = lax.shift_right_logical(r, 7)
    dist = jnp.abs(kj - BLOCK - qi)
    slope = jnp.exp2(-(GROUP * g + hh + 1).astype(F32) * (8.0 / ATTN_HEADS))
    inside = jnp.where(dist <= BLOCK, -slope * dist.astype(F32), NEG_INF)
    bias_ref[BIAS_INSIDE] = inside
    bias_ref[BIAS_FIRST] = jnp.where(kj >= BLOCK, inside, NEG_INF)
    bias_ref[BIAS_LAST] = jnp.where(kj < 2 * BLOCK, inside, NEG_INF)


BIAS_INSIDE, BIAS_FIRST, BIAS_LAST = 0, 1, 2


def _own_lanes(g):
    return lax.shift_right_logical(lax.broadcasted_iota(jnp.int32, (1, LANES), 1), 6) == g


def _mask_keys(x_ref, g, scale, pad_ref, s):
    pad_ref[0:BLOCK, :] = jnp.zeros((BLOCK, LANES), BF16)
    pad_ref[BLOCK + s:2 * BLOCK + s, :] = jnp.zeros((BLOCK, LANES), BF16)
    pad_ref[BLOCK:BLOCK + s, :] = jnp.where(_own_lanes(g), x_ref[...].astype(F32) * scale, 0.0).astype(BF16)


def _lane_block(x, j):
    return x[:, j * LANES:(j + 1) * LANES]


def _stack_heads(x, g):
    assert GROUP == 4 and GW == 2 * LANES
    x1 = pltpu.roll(x, HEAD_DIM, 1)
    keep = _own_lanes(g)
    zero = jnp.zeros((BLOCK, LANES), x.dtype)
    rows = []
    for h in range(GROUP):
        for_g0 = _lane_block(x, h // 2) if h % 2 == 0 else _lane_block(x1, ((h + 1) // 2) % 2)
        for_g1 = _lane_block(x, h // 2) if h % 2 == 1 else _lane_block(x1, h // 2)
        rows.append(jnp.where(keep, jnp.where(g == 0, for_g0, for_g1), zero))
    return jnp.concatenate(rows, axis=0)


def _unstack_heads(x4, g):
    p = [x4[h * BLOCK:(h + 1) * BLOCK, :] for h in range(GROUP)]
    cat = lambda a, b: jnp.concatenate([a, b], axis=1)
    in_place = jnp.where(g == 0, cat(p[0], p[2]), cat(p[1], p[3]))
    one_left = jnp.where(g == 0, cat(p[1], p[3]), cat(p[2], p[0]))
    return in_place + pltpu.roll(one_left, HEAD_DIM, 1)


def _sink_column(sink_ref, g):
    rh = lax.shift_right_logical(lax.broadcasted_iota(jnp.int32, (GROUP * BLOCK, 1), 0), 7)
    col = jnp.zeros((GROUP * BLOCK, 1), F32)
    for h in range(GROUP):
        col = jnp.where(rh == h, sink_ref[GROUP * g + h], col)
    return col


def _attn_probs(qm, k3, bias_ref, sink_col, n, s):
    which = jnp.where(n == 0, BIAS_FIRST, jnp.where(n == s // BLOCK - 1, BIAS_LAST, BIAS_INSIDE))
    logits = _dot_nt(qm, k3) + bias_ref[which]
    m = jnp.maximum(jnp.max(logits, axis=1, keepdims=True), sink_col)
    e = jnp.exp(logits - m)
    e_sink = jnp.exp(sink_col - m)
    inv = 1.0 / (jnp.sum(e, axis=1, keepdims=True) + e_sink)
    return e * inv, e_sink * inv


PAIRS_PER_KV = (RET_HEADS // 2) // KV_HEADS
FWD_ORDER = "rrarra"
BWD_ORDER = "rrarar"


def _trip_order(order, chunks, blocks):
    if order.count("r") == chunks and order.count("a") == blocks:
        return order
    return "r" * chunks + "a" * blocks


def _mixers_fwd(u, lgf, lgb, gn_gain, sink, b_loc, after=()):
    t = u.shape[0]
    s = t // b_loc
    n_chunks = s // CHUNK
    pairs = RET_HEADS // 2
    trips = n_chunks // RET_UNROLL
    blocks_half = (s // BLOCK) // PAIRS_PER_KV
    per_trip = blocks_half // trips
    assert n_chunks % RET_UNROLL == 0 and blocks_half % trips == 0 and PAIRS_PER_KV == 2 and s >= 2 * BLOCK

    def body(lgf_ref, lgb_ref, sink_ref, q_ref, k_ref, v_ref, g_ref, gain_ref, aq_ref, ak_ref, av_ref,
             r_ref, xhat_ref, rstd_ref, a_ref, st, kpad, vpad, bias):
        pair = pl.program_id(1)
        g, half = lax.shift_right_logical(pair, 1), pair & 1
        tab = _stacked_tables(lgf_ref, lgb_ref, pair)
        is_a = tab["is_a"]

        @pl.when(half == 0)
        def _():
            _attn_tables(g, bias)
            _mask_keys(ak_ref, g, Q_SCALE, kpad, s)
            _mask_keys(av_ref, g, 1.0, vpad, s)

        def kv_body(n, _):
            k8 = _chunk(k_ref, n).astype(F32) * Q_SCALE
            st[n] = jnp.where(tab["diag2"], _dot_tn(_both_ways(k8, tab["kdec"]), _chunk(v_ref, n)), 0.0)
            return 0

        _chunk_loop(n_chunks, kv_body, 0, FWD_PREP_UNROLL)
        _scan_states(n_chunks, st, FWD_ROWS, tab["lam_f"], BWD_ROWS, tab["lam_b"])
        sink_col = _sink_column(sink_ref, g)

        def retention_chunk(n):
            q = _chunk(q_ref, n)
            k8 = (_chunk(k_ref, n).astype(F32) * Q_SCALE).astype(BF16)
            v = _chunk(v_ref, n)
            p2 = (_dot_nt(_stack_pair(is_a, q), k8) * tab["d2"]).astype(BF16)
            y = _unstack_pair(is_a, _dot_nn(p2, v))
            y = y + _dot_nn(_both_ways(q.astype(F32), tab["qdec"]), st[n].astype(BF16))
            rows = pl.ds(pl.multiple_of(n * CHUNK, CHUNK), CHUNK)
            mu = _group_sum(is_a, y) * (1.0 / HEAD_DIM)
            dlt = y - mu
            var = _group_sum(is_a, dlt * dlt) * (1.0 / HEAD_DIM)
            rstd = lax.rsqrt(var + GN_EPS)
            xhat = dlt * rstd
            xhat_ref[rows, :] = xhat
            rstd_ref[rows, :] = rstd
            gate = _chunk(g_ref, n).astype(F32)
            r_ref[rows, :] = (xhat * gain_ref[...] * gate * _sigmoid(gate)).astype(BF16)

        def attention_block(blk):
            n = half * blocks_half + blk
            rows = pl.ds(pl.multiple_of(blk * BLOCK, BLOCK), BLOCK)
            keys = pl.ds(pl.multiple_of(n * BLOCK, BLOCK), KEYS)
            p, _ = _attn_probs(_stack_heads(aq_ref[rows, :], g), kpad[keys, :], bias, sink_col, n, s)
            a_ref[rows, :] = _unstack_heads(_dot_nn(p.astype(BF16), vpad[keys, :]), g).astype(BF16)

        def trip(i, _):
            chunk, blk = 0, 0
            for kind in _trip_order(FWD_ORDER, RET_UNROLL, per_trip):
                if kind == "r":
                    retention_chunk(i * RET_UNROLL + chunk)
                    chunk += 1
                else:
                    attention_block(i * per_trip + blk)
                    blk += 1
            return 0

        lax.fori_loop(0, trips, trip, 0)

    lane_blk = lambda c0: _seq_spec(s, c0 // LANES)
    half_rows = blocks_half * BLOCK
    aq_spec = pl.BlockSpec((half_rows, GW), lambda b, h: (b * PAIRS_PER_KV + (h & 1), C_AQ // GW + h // 2))
    a_spec = pl.BlockSpec((half_rows, GW), lambda b, h: (b * PAIRS_PER_KV + (h & 1), h // 2))
    kv_spec = lambda c0: pl.BlockSpec((s, LANES), lambda b, h: (b, c0 // LANES))
    pad = pltpu.VMEM((s + 2 * BLOCK, LANES), BF16)
    body, lead = _after(after, body)
    return pl.pallas_call(
        body, name="mixers_fwd", grid=(b_loc, pairs),
        in_specs=lead + [_smem_spec(), _smem_spec(), _smem_spec(), lane_blk(C_RQ), lane_blk(C_RK), lane_blk(C_RV),
                         lane_blk(C_RG), pl.BlockSpec((1, LANES), lambda b, h: (0, h)), aq_spec, kv_spec(C_AK),
                         kv_spec(C_AV)],
        out_specs=[_seq_spec(s, 0), _seq_spec(s, 0), _seq_spec(s, 0), a_spec, _state_spec(n_chunks, pairs)],
        out_shape=[SDS((t, RET_W), BF16), SDS((t, RET_W), F32), SDS((t, RET_W), F32), SDS((t, ATT_W), BF16),
                   SDS((b_loc * pairs * n_chunks, 2 * LANES, LANES), F32)],
        scratch_shapes=[pad, pad, pltpu.VMEM((3, GROUP * BLOCK, KEYS), F32)],
        compiler_params=_params(("arbitrary", "arbitrary")),
    )(*after, lgf, lgb, sink, u, u, u, u, gn_gain, u, u, u)


def _mixers_bwd(u, xhat, rstd, states, dr, da, lgf, lgb, gn_gain, sink, b_loc, after=()):
    t = u.shape[0]
    s = t // b_loc
    n_chunks = s // CHUNK
    pairs = RET_HEADS // 2
    unroll = BWD_MAIN_UNROLL if n_chunks % BWD_MAIN_UNROLL == 0 else RET_UNROLL
    trips = n_chunks // unroll
    blocks_half = (s // BLOCK) // PAIRS_PER_KV
    per_trip = blocks_half // trips
    assert n_chunks % unroll == 0 and blocks_half % trips == 0 and PAIRS_PER_KV == 2 and s >= 2 * BLOCK

    def body(lgf_ref, lgb_ref, sink_ref, q_ref, k_ref, v_ref, g_ref, xhat_ref, rstd_ref, dr_ref, gain_ref,
             aq_ref, ak_ref, av_ref, do_ref, st,
             dq_ref, dk_ref, dv_ref, dg_ref, st_ref, daq_ref, dak_ref, dav_ref, dsink_ref,
             gr, dy_s, kpad, vpad, bias, dk_acc, dv_acc):
        pair = pl.program_id(1)
        g, half = lax.shift_right_logical(pair, 1), pair & 1
        tab = _stacked_tables(lgf_ref, lgb_ref, pair)
        is_a = tab["is_a"]
        gain = gain_ref[...]

        @pl.when(half == 0)
        def _():
            _attn_tables(g, bias)
            _mask_keys(ak_ref, g, Q_SCALE, kpad, s)
            _mask_keys(av_ref, g, 1.0, vpad, s)
            dsink_ref[...] = jnp.zeros_like(dsink_ref)

        @pl.when(pair == 0)
        def _():
            dk_acc[...] = jnp.zeros_like(dk_acc)
            dv_acc[...] = jnp.zeros_like(dv_acc)

        def norm_body(n, dgain):
            rows = pl.ds(pl.multiple_of(n * CHUNK, CHUNK), CHUNK)
            xhat, rstd = xhat_ref[rows, :], rstd_ref[rows, :]
            gate = g_ref[rows, :].astype(F32)
            sg = _sigmoid(gate)
            silu = gate * sg
            d_out = dr_ref[rows, :].astype(F32)
            dg_ref[rows, :] = (d_out * xhat * gain * (sg * (1.0 + gate * (1.0 - sg)))).astype(BF16)
            dxh = d_out * gain * silu
            m1 = _group_sum(is_a, dxh) * (1.0 / HEAD_DIM)
            m2 = _group_sum(is_a, dxh * xhat) * (1.0 / HEAD_DIM)
            dy = (rstd * (dxh - m1 - xhat * m2)).astype(BF16)
            dy_s[rows, :] = dy
            qf = q_ref[rows, :].astype(F32)
            gr[n] = jnp.where(tab["diag2"], _dot_tn(_both_ways(qf, tab["qdec"]), dy), 0.0)
            return dgain + jnp.sum(d_out * xhat * silu, axis=0, keepdims=True)

        colsum = lambda x: jnp.sum(x, axis=0, keepdims=True)

        def grad_body(n, carry):
            xfb, ifa, ifb, iba, ibb, lf, lb = carry
            rows = pl.ds(pl.multiple_of(n * CHUNK, CHUNK), CHUNK)
            q = q_ref[rows, :]
            qf = q.astype(F32)
            k8f = k_ref[rows, :].astype(F32) * Q_SCALE
            k8 = k8f.astype(BF16)
            v = v_ref[rows, :]
            dy = dy_s[rows, :]
            q2, dy2 = _stack_pair(is_a, q), _stack_pair(is_a, dy)
            sc = _dot_nt(q2, k8)
            dp = _dot_nt(dy2, v)
            a2 = (sc * tab["d2"]).astype(BF16)
            ds2 = (dp * tab["d2"]).astype(BF16)
            dq = _unstack_pair(is_a, _dot_nn(ds2, k8))
            dk = _dot_tn(ds2, q2)
            dv = _dot_tn(a2, dy2)
            prod = sc * dp
            pf, pb = prod * tab["df2"], prod * tab["db2"]
            ifa, ifb = ifa + colsum(pf[0:CHUNK, :]), ifb + colsum(pf[CHUNK:2 * CHUNK, :])
            iba, ibb = iba + colsum(pb[0:CHUNK, :]), ibb + colsum(pb[CHUNK:2 * CHUNK, :])
            states, sgrads = st[n], gr[n]
            sb, gb = states.astype(BF16), sgrads.astype(BF16)
            dqc = _dot_nt(dy, sb) * tab["qdec"]
            dkc = _dot_nt(v, gb) * tab["kdec"]
            dv = dv + _dot_nn(_both_ways(k8f, tab["kdec"]), gb)
            dq_ref[rows, :] = (dq + dqc[:, 0:LANES] + dqc[:, LANES:2 * LANES]).astype(BF16)
            dk_ref[rows, :] = ((dk + dkc[:, 0:LANES] + dkc[:, LANES:2 * LANES]) * Q_SCALE).astype(BF16)
            dv_ref[rows, :] = dv.astype(BF16)
            q2w, k2w = jnp.concatenate([qf, qf], axis=1), jnp.concatenate([k8f, k8f], axis=1)
            xfb = xfb + colsum(tab["qexp"] * q2w * dqc + tab["kexp"] * k2w * dkc)
            prod_s = sgrads * states
            lf, lb = lf + colsum(prod_s[0:LANES, :]), lb + colsum(prod_s[LANES:2 * LANES, :])
            return xfb, ifa, ifb, iba, ibb, lf, lb

        sink_col = _sink_column(sink_ref, g)
        head_row = lax.broadcasted_iota(jnp.int32, dsink_ref.shape, 0)

        def attention_block(blk):
            n = half * blocks_half + blk
            rows = pl.ds(pl.multiple_of(blk * BLOCK, BLOCK), BLOCK)
            keys = pl.ds(pl.multiple_of(n * BLOCK, BLOCK), KEYS)
            qm = _stack_heads(aq_ref[rows, :], g)
            k3, v3 = kpad[keys, :], vpad[keys, :]
            p, p_sink = _attn_probs(qm, k3, bias, sink_col, n, s)
            dom = _stack_heads(do_ref[rows, :], g)
            dp = _dot_nt(dom, v3)
            delta = jnp.sum(p * dp, axis=1, keepdims=True)
            ds_mat = (p * (dp - delta)).astype(BF16)
            daq_ref[rows, :] = _unstack_heads(_dot_nn(ds_mat, k3), g).astype(BF16)
            dk_acc[keys, :] += _dot_tn(ds_mat, qm) * Q_SCALE
            dv_acc[keys, :] += _dot_tn(p.astype(BF16), dom)
            w = p_sink * delta
            upd = jnp.zeros(dsink_ref.shape, F32)
            for h in range(GROUP):
                upd = upd + jnp.where(head_row == h, -jnp.sum(w[h * BLOCK:(h + 1) * BLOCK, :]), 0.0)
            dsink_ref[...] += upd

        dgain = _chunk_loop(n_chunks, norm_body, jnp.zeros((1, LANES), F32), BWD_PREP_UNROLL)
        _scan_states(n_chunks, gr, BWD_ROWS, tab["lam_b"], FWD_ROWS, tab["lam_f"])

        def trip(i, carry):
            chunk, blk = 0, 0
            for kind in _trip_order(BWD_ORDER * (unroll // RET_UNROLL), unroll, per_trip):
                if kind == "r":
                    carry = grad_body(i * unroll + chunk, carry)
                    chunk += 1
                else:
                    attention_block(i * per_trip + blk)
                    blk += 1
            return carry

        z = jnp.zeros((1, LANES), F32)
        init = (jnp.zeros((1, 2 * LANES), F32), z, z, z, z, z, z)
        xfb, ifa, ifb, iba, ibb, lf, lb = lax.fori_loop(0, trips, trip, init)
        st_ref[...] = jnp.zeros_like(st_ref)
        st_ref[ST_GAIN:ST_GAIN + 1, :] = dgain
        st_ref[ST_XF:ST_XF + 1, :] = xfb[:, 0:LANES]
        st_ref[ST_XB:ST_XB + 1, :] = xfb[:, LANES:2 * LANES]
        st_ref[ST_IFA:ST_IFA + 1, :] = ifa
        st_ref[ST_IFB:ST_IFB + 1, :] = ifb
        st_ref[ST_IBA:ST_IBA + 1, :] = iba
        st_ref[ST_IBB:ST_IBB + 1, :] = ibb
        st_ref[ST_LF:ST_LF + 1, :] = lf * (CHUNK * tab["lam_f"])
        st_ref[ST_LB:ST_LB + 1, :] = lb * (CHUNK * tab["lam_b"])

        @pl.when(pair == pairs - 1)
        def _():
            dak_ref[...] = dk_acc[BLOCK:BLOCK + s, :].astype(BF16)
            dav_ref[...] = dv_acc[BLOCK:BLOCK + s, :].astype(BF16)

    lane_blk = lambda c0: _seq_spec(s, c0 // LANES)
    seq0 = _seq_spec(s, 0)
    half_rows = blocks_half * BLOCK
    aq_spec = pl.BlockSpec((half_rows, GW), lambda b, h: (b * PAIRS_PER_KV + (h & 1), C_AQ // GW + h // 2))
    a_spec = pl.BlockSpec((half_rows, GW), lambda b, h: (b * PAIRS_PER_KV + (h & 1), h // 2))
    kv_spec = lambda c0: pl.BlockSpec((s, LANES), lambda b, h: (b, c0 // LANES))
    kv_out = pl.BlockSpec((s, LANES), lambda b, h: (b, 0))
    state = pltpu.VMEM((n_chunks, 2 * LANES, LANES), F32)
    pad = pltpu.VMEM((s + 2 * BLOCK, LANES), BF16)
    acc = pltpu.VMEM((s + 2 * BLOCK, LANES), F32)
    body, lead = _after(after, body)
    return pl.pallas_call(
        body, name="mixers_bwd", grid=(b_loc, pairs),
        in_specs=lead + [_smem_spec(), _smem_spec(), _smem_spec(), lane_blk(C_RQ), lane_blk(C_RK), lane_blk(C_RV),
                         lane_blk(C_RG), seq0, seq0, seq0, pl.BlockSpec((1, LANES), lambda b, h: (0, h)),
                         aq_spec, kv_spec(C_AK), kv_spec(C_AV), a_spec, _state_spec(n_chunks, pairs)],
        out_specs=[seq0] * 4 + [pl.BlockSpec((ST_ROWS, LANES), lambda b, h: (b, h)), a_spec, kv_out, kv_out,
                                pl.BlockSpec((8, LANES), lambda b, h: (b * KV_HEADS + h // 2, 0))],
        out_shape=[SDS((t, RET_W), BF16)] * 4 + [SDS((b_loc * ST_ROWS, RET_W), F32), SDS((t, ATT_W), BF16),
                                                   SDS((t, KV_W), BF16), SDS((t, KV_W), BF16),
                                                   SDS((b_loc * KV_HEADS * 8, LANES), F32)],
        scratch_shapes=[state, pltpu.VMEM((s, LANES), BF16), pad, pad,
                        pltpu.VMEM((3, GROUP * BLOCK, KEYS), F32), acc, acc],
        compiler_params=_params(("arbitrary", "arbitrary")),
    )(*after, lgf, lgb, sink, u, u, u, u, xhat, rstd, dr, gn_gain, u, u, u, da, states)


def _pack_small(acc2, acc1, ret_stats, dsink, b_loc, d):
    pairs = RET_HEADS // 2

    def body(acc2_ref, acc1_ref, st_ref, dsink_ref, out_ref):
        out_ref[...] = jnp.zeros_like(out_ref)
        out_ref[ROW_LN1G:ROW_LN1G + 1, :] = acc1_ref[0:1, :]
        out_ref[ROW_LN1B:ROW_LN1B + 1, :] = acc1_ref[1:2, :]
        out_ref[ROW_LN2G:ROW_LN2G + 1, :] = acc2_ref[1:2, :]
        out_ref[ROW_LN2B:ROW_LN2B + 1, :] = acc2_ref[2:3, :]
        out_ref[ROW_LOSS:ROW_LOSS + 1, :] = acc2_ref[0:1, :]
        st = st_ref[0:ST_ROWS, :]
        for b in range(1, b_loc):
            st = st + st_ref[b * ST_ROWS:(b + 1) * ST_ROWS, :]
        out_ref[ROW_GN:ROW_GN + 1, 0:RET_W] = st[ST_GAIN:ST_GAIN + 1, :]
        lane = lax.broadcasted_iota(jnp.int32, (1, d), 1)
        misc = jnp.zeros((1, d), F32)
        for pr in range(pairs):
            blk = st[:, pr * LANES:(pr + 1) * LANES]
            half = lax.broadcasted_iota(jnp.int32, (1, LANES), 1) < HEAD_DIM
            for h in range(2):
                sel = half if h == 0 else jnp.logical_not(half)
                cross_f = jnp.sum(jnp.where(sel, blk[ST_XF:ST_XF + 1, :] + blk[ST_LF:ST_LF + 1, :], 0.0))
                cross_b = jnp.sum(jnp.where(sel, blk[ST_XB:ST_XB + 1, :] + blk[ST_LB:ST_LB + 1, :], 0.0))
                intra_f = jnp.sum(blk[ST_IFA + h:ST_IFA + h + 1, :])
                intra_b = jnp.sum(blk[ST_IBA + h:ST_IBA + h + 1, :])
                head = 2 * pr + h
                misc = jnp.where(lane == MISC_DF + head, cross_f + intra_f, misc)
                misc = jnp.where(lane == MISC_DB + head, cross_b + intra_b, misc)
        for g in range(KV_HEADS):
            tot = dsink_ref[g * 8:(g + 1) * 8, :]
            for b in range(1, b_loc):
                tot = tot + dsink_ref[(b * KV_HEADS + g) * 8:(b * KV_HEADS + g + 1) * 8, :]
            for h in range(GROUP):
                misc = jnp.where(lane == MISC_SINK + GROUP * g + h, jnp.sum(tot[h:h + 1, 0:1]), misc)
        out_ref[ROW_MISC:ROW_MISC + 1, :] = misc

    return pl.pallas_call(body, name="pack_small", out_shape=SDS((SMALL_ROWS, d), F32))(acc2, acc1, ret_stats, dsink)


BIG = ("w_in", "w_out", "w_ffn_gate", "w_ffn_up", "w_ffn_down", "w_ple_proj", "w_ple_gate")
TRANSPOSED_OUTSIDE = ("w_in", "w_ffn_gate", "w_ffn_up")
TRANSPOSED_HERE = ("w_ple_proj",)
SMALL = ("ret_decay_fwd", "ret_decay_bwd", "ret_gn_gain", "attn_sink", "ln1_gain", "ln1_bias", "ln2_gain", "ln2_bias")
ORDER = ("w_in", "ret_decay_fwd", "ret_decay_bwd", "ret_gn_gain", "attn_sink", "w_out", "ln1_gain", "ln1_bias",
         "w_ffn_gate", "w_ffn_up", "w_ffn_down", "w_ple_proj", "w_ple_gate", "ln2_gain", "ln2_bias")


GATHER_ORDER = ("w_in", "w_ffn_up", "w_out", "w_ffn_gate", "w_ple_gate", "w_ple_proj", "w_ffn_down")
GATHER_TWO_LEVEL = ("w_in", "w_ffn_up", "w_out")

def _local_step(x2, p2, target2, fetch, publish, small, b_loc, me):
    d = x2.shape[1]
    lgf, lgb = _log_decay(small["ret_decay_fwd"], small["ret_decay_bwd"])
    lgf1, lgb1, sink1 = lgf.reshape(-1), lgb.reshape(-1), small["attn_sink"].reshape(-1)
    (w_in,) = fetch(("w_in",), ())
    u, xb = _in_proj(x2, w_in)
    passed = fetch.pass_on(("w_ffn_up", "w_out"), (xb, lgf))
    r, ret_xhat, ret_rstd, a, ret_states = _mixers_fwd(u, lgf1, lgb1, small["ret_gn_gain"], sink1, b_loc, passed)
    w_out, w_gate, w_up = fetch(("w_out", "w_ffn_gate", "w_ffn_up"), (r, a))
    xhat1, rstd1, h1b, dact_dg, dact_du, act = _mix_ln1_ffn_up(
        r, a, x2, w_out, w_gate, w_up, small["ln1_gain"], small["ln1_bias"])
    w_pg, w_pe, w_down = fetch(("w_ple_gate", "w_ple_proj", "w_ffn_down"), (act,))
    own = {}
    dz2, dz2b, dsb, dg, dup, acc2, pe_whole, own["w_ple_proj"], pg_whole, own["w_ple_gate"] = (
        _ffn_down_ln2_loss(me, act, dact_dg, dact_du, h1b, p2, xhat1, target2, w_down, w_pg, w_pe,
                           small["ln1_gain"], small["ln1_bias"], small["ln2_gain"], small["ln2_bias"]))

    def grad(name, parts, rhs, after=()):
        whole, own[name] = _weight_grad("grad_" + name, me, parts, rhs, after)
        return whole

    ffn_jobs = dict(w_ffn_down=(act, dz2b), w_ffn_gate=(dg, h1b), w_ffn_up=(dup, h1b))
    wholes, owns = _weight_grad_jobs("grad_w_ffn", me, list(ffn_jobs.values()))
    own.update(zip(ffn_jobs, owns))
    t2 = publish("ffn", dict(zip(ffn_jobs, wholes), w_ple_proj=pe_whole, w_ple_gate=pg_whole))
    dz1, dz1b, dr, da, acc1, wo_whole, own["w_out"] = _dh1_ln1_bwd(
        me, dz2, dg, dup, dsb, xhat1, rstd1, r, a, w_gate, w_up, w_pg, w_out, small["ln1_gain"], t2)
    t3 = publish("out", dict(w_out=wo_whole))
    dq, dk, dv, dgate, ret_stats, daq, dak, dav, dsink = _mixers_bwd(
        u, ret_xhat, ret_rstd, ret_states, dr, da, lgf1, lgb1, small["ret_gn_gain"], sink1, b_loc, t3)
    parts = [dq, dk, dv, dgate, daq, dak, dav]
    small_part = _pack_small(acc2, acc1, ret_stats, dsink, b_loc, d)
    t4 = publish("in", dict(w_in=grad("w_in", parts, xb)), small_part)
    grad_x = _in_proj_bwd(dz1, parts, w_in, t4)
    return grad_x, own, small_part


def kernel(x, p, w_in, ret_decay_fwd, ret_decay_bwd, ret_gn_gain, attn_sink, w_out, ln1_gain, ln1_bias, w_ffn_gate, w_ffn_up, w_ffn_down, w_ple_proj, w_ple_gate, ln2_gain, ln2_bias, loss_target, m_w_in, m_ret_decay_fwd, m_ret_decay_bwd, m_ret_gn_gain, m_attn_sink, m_w_out, m_ln1_gain, m_ln1_bias, m_w_ffn_gate, m_w_ffn_up, m_w_ffn_down, m_w_ple_proj, m_w_ple_gate, m_ln2_gain, m_ln2_bias, v_w_in, v_ret_decay_fwd, v_ret_decay_bwd, v_ret_gn_gain, v_attn_sink, v_w_out, v_ln1_gain, v_ln1_bias, v_w_ffn_gate, v_w_ffn_up, v_w_ffn_down, v_w_ple_proj, v_w_ple_gate, v_ln2_gain, v_ln2_bias):
    given = dict(locals())

    def strip(n, a):
        if n not in BIG:
            return a
        return a[0].T if n in TRANSPOSED_OUTSIDE else a[0]

    def restore(n, a):
        if n not in BIG:
            return a
        return (a.T if n in TRANSPOSED_OUTSIDE else a)[None]

    w = {n: strip(n, given[n]) for n in ORDER}
    m = {n: strip(n, given["m_" + n]) for n in ORDER}
    v = {n: strip(n, given["v_" + n]) for n in ORDER}
    b_loc, s, d = x.shape
    x2 = x.reshape(b_loc * s, d)
    p2 = p[0].reshape(b_loc * s, p.shape[-1])
    target2 = loss_target.reshape(b_loc * s, d)

    small = {n: w[n] for n in SMALL}
    me = (4 * lax.axis_index("x") + 2 * lax.axis_index("y") + lax.axis_index("c")).astype(jnp.int32).reshape(1)

    gather = _gather_start(
        {n: w[n] for n in GATHER_ORDER},
        [_gather_copy_near if n in GATHER_TWO_LEVEL else _gather_copy for n in GATHER_ORDER])

    passing = {}

    def pass_on(names, after):
        relayed = _gather_relay("gather_relay_" + names[0], gather, [GATHER_ORDER.index(n) for n in names], list(after))
        passing.update({n: (relayed, j) for j, n in enumerate(names)})
        return (relayed["token"],)

    def fetch(names, after):
        out = {}
        for n in [n for n in names if n in GATHER_TWO_LEVEL]:
            if n not in passing:
                pass_on((n,), after)
            relayed, j = passing[n]
            out[n] = _split_copy_wait("gather_wait_" + n, relayed, [j], list(after))[0][0]
        direct = [n for n in names if n not in GATHER_TWO_LEVEL]
        if direct:
            got = _split_copy_wait("gather_wait_" + direct[0], gather, [GATHER_ORDER.index(n) for n in direct],
                                   list(after))
            out.update({n: item[0] for n, item in zip(direct, got)})
        return [out[n] for n in names]

    scatters = []

    def publish(tag, products, small_sums=None):
        items = [(products[n], lax.empty((N_DEV - 1, products[n].shape[0] // N_DEV, products[n].shape[1]), BF16))
                 for n in products]
        copies = [_scatter_copy] * len(items)
        if small_sums is not None:
            items.append((small_sums, lax.empty((N_DEV - 1,) + small_sums.shape, F32)))
            copies.append(_small_copy)
        started = _split_copy_start("scatter_start_" + tag, items, copies)
        scatters.append((list(products), small_sums is not None, started))
        return (started["token"],)

    fetch.pass_on = pass_on
    grad_x, own, small_part = _local_step(x2, p2, target2, fetch, publish, small, b_loc, me)

    out_g, out_d, out_m, out_v = {}, {}, {}, {}
    after = [grad_x]
    for names, with_small, started in scatters:
        landed = _split_copy_wait("scatter_wait_" + names[0], started, list(range(len(started["items"]))), after)
        if with_small:
            mine, from_peers = landed[-1]
            loss, sg, sd, sm, sv = _small_adamw(
                me, mine, from_peers, small, {n: m[n] for n in SMALL}, {n: v[n] for n in SMALL})
            for dst, src in ((out_g, sg), (out_d, sd), (out_m, sm), (out_v, sv)):
                dst.update(src)
        recv = {n: item[1] for n, item in zip(names, landed)}
        alike = {}
        for n in names:
            alike.setdefault((own[n].shape, n in TRANSPOSED_HERE), []).append(n)
        for (_, transposed), ns in alike.items():
            res = _reduce_adamw(ns[0], [own[n] for n in ns], [recv[n] for n in ns], [w[n] for n in ns],
                                [m[n] for n in ns], [v[n] for n in ns], transposed)
            for dst, vals in zip((out_g, out_d, out_m, out_v), res):
                dst.update(zip(ns, vals))
        after = [out_v[names[-1]]]

    outs = [loss[0, 0], grad_x.reshape(x.shape)]
    for group in (out_g, out_d, out_m, out_v):
        outs += [restore(n, group[n]) for n in ORDER]
    return tuple(outs)
```

```python
import functools

import jax
import jax.numpy as jnp
from jax import lax
from jax.experimental import pallas as pl
from jax.experimental.pallas import tpu as pltpu

F32, BF16 = jnp.float32, jnp.bfloat16
SDS = jax.ShapeDtypeStruct
MESH = pl.DeviceIdType.MESH

N_DEV = 8
HEAD_DIM = 64
RET_HEADS = 8
ATTN_HEADS = 8
KV_HEADS = 2
GROUP = ATTN_HEADS // KV_HEADS
RET_W = RET_HEADS * HEAD_DIM
ATT_W = ATTN_HEADS * HEAD_DIM
KV_W = KV_HEADS * HEAD_DIM
LANES = 128
CHUNK = 128
BLOCK = 128
Q_SCALE = HEAD_DIM ** -0.5
ALPHA = 2.0 ** 0.25
LN_EPS = 1e-5
GN_EPS = 1e-5
NEG_INF = -1e30
C_RQ, C_RK, C_RV, C_RG = 0, RET_W, 2 * RET_W, 3 * RET_W
C_AQ = 4 * RET_W
C_AK = C_AQ + ATT_W
C_AV = C_AK + KV_W
IN_W = C_AV + KV_W

ADAM_LR = 0.001
ADAM_B1 = 0.9
ADAM_B2 = 0.999
ADAM_EPS = 1e-08
ADAM_WD = 0.01
ADAM_STEP = 10

VMEM_LIMIT = 56 * 1024 * 1024
DH1_VMEM_LIMIT = 61 * 1024 * 1024
MATMUL_ROWS = 512
EPILOGUE_ROWS = 256
SUB_ROWS = 512
SMALL_ROWS = 16
ROW_LN1G, ROW_LN1B, ROW_LN2G, ROW_LN2B, ROW_LOSS, ROW_GN, ROW_MISC = 0, 1, 2, 3, 4, 5, 6
MISC_DF, MISC_DB, MISC_SINK = 0, 8, 16


def _dot_nn(a, b):
    return lax.dot_general(a, b, (((1,), (0,)), ((), ())), preferred_element_type=F32)


def _dot_nt(a, b):
    return lax.dot_general(a, b, (((1,), (1,)), ((), ())), preferred_element_type=F32)


def _dot_tn(a, b):
    return lax.dot_general(a, b, (((0,), (0,)), ((), ())), preferred_element_type=F32)


def _params(sem=None, vmem=VMEM_LIMIT):
    kw = {"vmem_limit_bytes": vmem}
    if sem is not None:
        kw["dimension_semantics"] = sem
    return pltpu.CompilerParams(**kw)


def _row_tile(t, want=512):
    tm = want
    while t % tm:
        tm //= 2
    return tm


def _sigmoid(x):
    return jax.nn.sigmoid(x)


def _layer_norm_stats(z):
    mu = jnp.mean(z, axis=1, keepdims=True)
    d = z - mu
    var = jnp.mean(d * d, axis=1, keepdims=True)
    rstd = lax.rsqrt(var + LN_EPS)
    return d * rstd, rstd


def _layer_norm_bwd(dxh, xhat, rstd):
    m1 = jnp.mean(dxh, axis=1, keepdims=True)
    m2 = jnp.mean(dxh * xhat, axis=1, keepdims=True)
    return rstd * (dxh - m1 - xhat * m2)


def _mesh_pos():
    return lax.axis_index("x"), lax.axis_index("y"), lax.axis_index("c")


HBM_SPEC = pl.BlockSpec(memory_space=pltpu.HBM)
SEM_SPEC = pl.BlockSpec(memory_space=pltpu.SEMAPHORE)
ANY_SPEC = pl.BlockSpec(memory_space=pl.ANY)
SIDE_EFFECT = pltpu.SideEffectType.DATAFLOW_SIDE_EFFECTING
PEER_SEMS = pltpu.SemaphoreType.DMA((N_DEV - 1,))


def _in_hbm(a):
    return pltpu.with_memory_space_constraint(a, pltpu.HBM)


def _split_copy_start(name, items, copies):
    n = len(items)
    flat = [a for it in items for a in it]
    k = len(flat)

    def body(*refs):
        arr, sems = list(refs[:k]), refs[k:k + 2 * n]
        for i, it in enumerate(items):
            mine = [arr.pop(0) for _ in it]
            for m in range(1, N_DEV):
                cp = copies[i](m, mine, sems[i].at[m - 1], sems[n + i].at[m - 1])
                if cp is not None:
                    cp.start()
        token = refs[-1]
        token[...] = jnp.zeros_like(token)

    res = pl.pallas_call(
        body, name=name,
        out_shape=[PEER_SEMS] * (2 * n) + [pltpu.HBM(a.shape, a.dtype) for a in flat] + [SDS((8, LANES), F32)],
        in_specs=[HBM_SPEC] * k,
        out_specs=[SEM_SPEC] * (2 * n) + [HBM_SPEC] * k + [pl.BlockSpec(memory_space=pltpu.VMEM)],
        input_output_aliases={j: 2 * n + j for j in range(k)},
        compiler_params=pltpu.CompilerParams(has_side_effects=SIDE_EFFECT),
    )(*[_in_hbm(a) for a in flat])
    thru, out_items = list(res[2 * n:2 * n + k]), []
    for it in items:
        out_items.append(tuple(thru.pop(0) for _ in it))
    return dict(send=res[:n], recv=res[n:2 * n], items=out_items, token=res[-1], copies=copies)


def _gather_start(shards, copies):
    names = list(shards)
    n = len(names)
    flip = [name in TRANSPOSED_HERE for name in names]
    shapes = [shards[name].shape[::-1] if f else shards[name].shape for name, f in zip(names, flip)]
    most = (max(s[0] for s in shapes), max(s[1] for s in shapes))

    def body(*refs):
        src, sems, land, token = refs[:n], refs[n:3 * n], refs[3 * n:4 * n], refs[4 * n]
        wide, narrow, sem = refs[4 * n + 1:]
        for i, (rows, cols) in enumerate(shapes):
            raw = wide.at[0:cols, 0:rows] if flip[i] else wide.at[0:rows, 0:cols]
            bring = pltpu.make_async_copy(src[i], raw, sem.at[0])
            bring.start()
            bring.wait()
            narrow[0:rows, 0:cols] = (raw[...].T if flip[i] else raw[...]).astype(BF16)
            mine = land[i].at[pl.ds(pl.multiple_of(_peer_index(0) * rows, 8), rows), :]
            place = pltpu.make_async_copy(narrow.at[0:rows, 0:cols], mine, sem.at[0])
            place.start()
            place.wait()
            for m in range(1, N_DEV):
                cp = copies[i](m, [land[i]], sems[i].at[m - 1], sems[n + i].at[m - 1])
                if cp is not None:
                    cp.start()
        token[...] = jnp.zeros_like(token)

    side = max(most)
    res = pl.pallas_call(
        body, name="gather_start",
        out_shape=[PEER_SEMS] * (2 * n) + [pltpu.HBM((N_DEV * r, c), BF16) for r, c in shapes] + [SDS((8, LANES), F32)],
        in_specs=[HBM_SPEC] * n,
        out_specs=[SEM_SPEC] * (2 * n) + [HBM_SPEC] * n + [pl.BlockSpec(memory_space=pltpu.VMEM)],
        scratch_shapes=[pltpu.VMEM((side, side), F32), pltpu.VMEM(most, BF16), pltpu.SemaphoreType.DMA((1,))],
        compiler_params=pltpu.CompilerParams(has_side_effects=SIDE_EFFECT),
    )(*[_in_hbm(shards[name]) for name in names])
    return dict(send=res[:n], recv=res[n:2 * n], items=[(a,) for a in res[2 * n:3 * n]], token=res[-1], copies=copies)


def _gather_relay(name, started, which, after):
    lands = [started["items"][w][0] for w in which]
    k = len(lands)

    def body(*refs):
        land_refs, old_send, old_recv = refs[:k], refs[k:2 * k], refs[2 * k:3 * k]
        outs = refs[3 * k + len(after):]
        send, recv, token = outs[:k], outs[k:2 * k], outs[-1]
        for j in range(k):
            for m in range(1, N_DEV):
                cp = _gather_copy_near(m, [land_refs[j]], old_send[j].at[m - 1], old_recv[j].at[m - 1])
                if cp is None:
                    continue
                cp.wait_send()
                cp.wait_recv()
                if m > 1:
                    _gather_copy_pass(m + 1, [land_refs[j]], send[j].at[m], recv[j].at[m]).start()
        token[...] = jnp.zeros_like(token)

    res = pl.pallas_call(
        body, name=name,
        out_shape=[PEER_SEMS] * (2 * k) + [pltpu.HBM(a.shape, a.dtype) for a in lands] + [SDS((8, LANES), F32)],
        in_specs=[HBM_SPEC] * k + [SEM_SPEC] * (2 * k) + [ANY_SPEC] * len(after),
        out_specs=[SEM_SPEC] * (2 * k) + [HBM_SPEC] * k + [pl.BlockSpec(memory_space=pltpu.VMEM)],
        input_output_aliases={j: 2 * k + j for j in range(k)},
        compiler_params=pltpu.CompilerParams(has_side_effects=SIDE_EFFECT),
    )(*lands, *[started["send"][w] for w in which], *[started["recv"][w] for w in which],
      *[_in_hbm(a) for a in after])
    return dict(send=res[:k], recv=res[k:2 * k], items=[(a,) for a in res[2 * k:3 * k]], token=res[-1],
                copies=[_gather_copy_pass] * k)


def _split_copy_wait(name, started, which, after):
    items = [started["items"][i] for i in which]
    copies = [started["copies"][i] for i in which]
    n = len(items)
    flat = [a for it in items for a in it]
    k = len(flat)

    def body(*refs):
        arr, sems = list(refs[:k]), refs[k:k + 2 * n]
        for i, it in enumerate(items):
            mine = [arr.pop(0) for _ in it]
            for m in range(1, N_DEV):
                cp = copies[i](m, mine, sems[i].at[m - 1], sems[n + i].at[m - 1])
                if cp is not None:
                    cp.wait_send()
                    cp.wait_recv()

    res = pl.pallas_call(
        body, name=name,
        out_shape=[pltpu.HBM(a.shape, a.dtype) for a in flat],
        in_specs=[HBM_SPEC] * k + [SEM_SPEC] * (2 * n) + [ANY_SPEC] * len(after),
        out_specs=[HBM_SPEC] * k,
        input_output_aliases={j: j for j in range(k)},
        compiler_params=pltpu.CompilerParams(has_side_effects=SIDE_EFFECT),
    )(*flat, *[started["send"][i] for i in which], *[started["recv"][i] for i in which], *[_in_hbm(a) for a in after])
    thru, out_items = list(res), []
    for it in items:
        out_items.append(tuple(thru.pop(0) for _ in it))
    return out_items


def _gather_copy(m, refs, send_sem, recv_sem):
    (land_ref,) = refs
    r = land_ref.shape[0] // N_DEV
    mine = land_ref.at[pl.ds(pl.multiple_of(_peer_index(0) * r, 8), r), :]
    return pltpu.make_async_remote_copy(src_ref=mine, dst_ref=mine, send_sem=send_sem, recv_sem=recv_sem,
                                        device_id=_peer(m), device_id_type=MESH)


def _gather_copy_near(m, refs, send_sem, recv_sem):
    return _gather_copy(m, refs, send_sem, recv_sem) if m == 1 or m % 2 == 0 else None


def _gather_copy_pass(m, refs, send_sem, recv_sem):
    if m == 1 or m % 2 == 0:
        return None
    (land_ref,) = refs
    r = land_ref.shape[0] // N_DEV
    block = land_ref.at[pl.ds(pl.multiple_of(_peer_index(m ^ 1) * r, 8), r), :]
    return pltpu.make_async_remote_copy(src_ref=block, dst_ref=block, send_sem=send_sem, recv_sem=recv_sem,
                                        device_id=_peer(1), device_id_type=MESH)


def _small_copy(m, refs, send_sem, recv_sem):
    part_ref, land_ref = refs
    return pltpu.make_async_remote_copy(src_ref=part_ref, dst_ref=land_ref.at[m - 1], send_sem=send_sem,
                                        recv_sem=recv_sem, device_id=_peer(m), device_id_type=MESH)


def _scatter_copy(m, refs, send_sem, recv_sem):
    buf_ref, land_ref = refs
    r = buf_ref.shape[0] // N_DEV
    src = buf_ref.at[pl.ds(pl.multiple_of(_peer_index(m) * r, 8), r), :]
    return pltpu.make_async_remote_copy(src_ref=src, dst_ref=land_ref.at[m - 1], send_sem=send_sem,
                                        recv_sem=recv_sem, device_id=_peer(m), device_id_type=MESH)


def _peer(m):
    x, y, c = _mesh_pos()
    bx, by, bc = (m >> 2) & 1, (m >> 1) & 1, m & 1
    return (x ^ bx if bx else x, y ^ by if by else y, c ^ bc if bc else c)


def _peer_index(m):
    x, y, c = _mesh_pos()
    return (4 * x + 2 * y + c) ^ m


SMALL_PLACE = {
    "ln1_gain": (ROW_LN1G, 0), "ln1_bias": (ROW_LN1B, 0), "ln2_gain": (ROW_LN2G, 0), "ln2_bias": (ROW_LN2B, 0),
    "ret_gn_gain": (ROW_GN, 0), "ret_decay_fwd": (ROW_MISC, MISC_DF), "ret_decay_bwd": (ROW_MISC, MISC_DB),
    "attn_sink": (ROW_MISC, MISC_SINK)}


def _small_adamw(me, part, landed, w, m, v):
    d = part.shape[1]
    names = list(SMALL_PLACE)
    k = len(names)

    def body(*refs):
        me_ref, part_ref, land_ref = refs[:3]
        refs = refs[2:]
        w_refs, m_refs, v_refs = refs[1:1 + k], refs[1 + k:1 + 2 * k], refs[1 + 2 * k:1 + 3 * k]
        outs = refs[1 + 3 * k:1 + 7 * k + 1]
        tot_ref = refs[-1]
        loss_ref, g_refs, dl_refs = outs[0], outs[1:1 + k], outs[1 + k:1 + 2 * k]
        nm_refs, nv_refs = outs[1 + 2 * k:1 + 3 * k], outs[1 + 3 * k:1 + 4 * k]
        tot = jnp.zeros(part_ref.shape, F32)
        for dev in range(N_DEV):
            j = dev ^ me_ref[0]
            tot = tot + jnp.where(j == 0, part_ref[...], land_ref[jnp.maximum(j, 1) - 1])
        tot_ref[...] = tot
        loss_ref[...] = (0.5 / d) * jnp.sum(tot_ref[ROW_LOSS:ROW_LOSS + 1, :], axis=1, keepdims=True)
        for i, name in enumerate(names):
            row, lo = SMALL_PLACE[name]
            wv = w_refs[i][...]
            g = tot_ref[row:row + 1, lo:lo + wv.shape[1]]
            if name.startswith("ret_decay"):
                p2 = jnp.exp2(wv)
                g = g * (-p2 * jnp.log(2.0) / (1.0 - p2))
            g_refs[i][...] = g
            _adamw_store(g, wv, m_refs[i][...], v_refs[i][...], dl_refs[i], nm_refs[i], nv_refs[i])

    shapes = [SDS(w[n].shape, F32) for n in names]
    vm = pl.BlockSpec(memory_space=pltpu.VMEM)
    res = pl.pallas_call(
        body, name="small_adamw", out_shape=[SDS((1, 1), F32)] + shapes * 4,
        in_specs=[_smem_spec()] + [vm] * (2 + 3 * k), out_specs=[vm] * (1 + 4 * k),
        scratch_shapes=[pltpu.VMEM(part.shape, F32)],
    )(me, part, landed, *[w[n] for n in names], *[m[n] for n in names], *[v[n] for n in names])
    groups = [dict(zip(names, res[1 + j * k:1 + (j + 1) * k])) for j in range(4)]
    return (res[0], *groups)


def _adamw_store(g, w, m, v, dl_ref, nm_ref, nv_ref):
    m = ADAM_B1 * m + (1.0 - ADAM_B1) * g
    v = ADAM_B2 * v + (1.0 - ADAM_B2) * (g * g)
    m_hat = m / (1.0 - ADAM_B1 ** ADAM_STEP)
    v_hat = v / (1.0 - ADAM_B2 ** ADAM_STEP)
    dl_ref[...] = -ADAM_LR * (m_hat / (jnp.sqrt(v_hat) + ADAM_EPS) + ADAM_WD * w)
    nm_ref[...] = m
    nv_ref[...] = v


def _reduce_adamw(name, owns, recvs, ws, ms, vs, transposed):
    count = len(owns)
    rows, n = owns[0].shape
    steps = 1 if transposed or rows % 16 else 2
    rb = rows // steps

    def body(*refs):
        ins, outs = refs[:5 * count], refs[5 * count:]
        j = pl.program_id(0)
        for k in range(count):
            @pl.when(j == k)
            def _(k=k):
                own_ref, recv_ref, w_ref, m_ref, v_ref = ins[5 * k:5 * k + 5]
                g_ref, dl_ref, nm_ref, nv_ref = outs[4 * k:4 * k + 4]
                g = own_ref[...]
                for p in range(recv_ref.shape[0]):
                    g = g + recv_ref[p].astype(F32)
                if transposed:
                    g = g.T
                g_ref[...] = g
                _adamw_store(g, w_ref[...], m_ref[...], v_ref[...], dl_ref, nm_ref, nv_ref)

    def turn(k):
        return lambda j, i: jnp.where(j == k, i, jnp.where(j < k, 0, steps - 1))

    in_specs, out_specs = [], []
    for k in range(count):
        at = turn(k)
        blk = pl.BlockSpec(ws[0].shape if transposed else (rb, n), lambda j, i, at=at: (at(j, i), 0))
        in_specs += [pl.BlockSpec((rb, n), lambda j, i, at=at: (at(j, i), 0)),
                     pl.BlockSpec((recvs[k].shape[0], rb, n), lambda j, i, at=at: (0, at(j, i), 0)), blk, blk, blk]
        out_specs += [blk] * 4
    res = pl.pallas_call(
        body, name="adamw_" + name, grid=(count, steps), in_specs=in_specs, out_specs=out_specs,
        out_shape=[SDS(ws[0].shape, F32)] * (4 * count), compiler_params=_params(("arbitrary", "arbitrary")),
    )(*[a for k in range(count) for a in (owns[k], recvs[k], ws[k], ms[k], vs[k])])
    return [list(res[j::4]) for j in range(4)]


def _row_spec(tm, width):
    return pl.BlockSpec((tm, width), lambda i: (i, 0))


def _full_spec(shape):
    return pl.BlockSpec(shape, lambda i: (0,) * len(shape))


_acc_spec = _full_spec


def _sub_rows(tm):
    step = min(SUB_ROWS, tm)
    return [(lo, lo + step) for lo in range(0, tm, step)]


def _in_proj(x2, wt_in):
    t, d = x2.shape
    u_w = wt_in.shape[0]
    tm = _row_tile(t, MATMUL_ROWS)

    def body(x_ref, w_ref, u_ref, xb_ref):
        xb = x_ref[...].astype(BF16)
        xb_ref[...] = xb
        u_ref[...] = _dot_nt(xb, w_ref[...]).astype(BF16)

    return pl.pallas_call(
        body, name="in_proj", grid=(t // tm,),
        in_specs=[_row_spec(tm, d), _full_spec(wt_in.shape)],
        out_specs=[_row_spec(tm, u_w), _row_spec(tm, d)],
        out_shape=[SDS((t, u_w), BF16), SDS((t, d), BF16)],
        compiler_params=_params(("parallel",)),
    )(x2, wt_in)


def _col_halves(f):
    n = f // LANES
    k = (n + 1) // 2 * LANES
    return [(0, k), (k, f)] if k < f else [(0, f)]


def _mix_ln1_ffn_up(r, a, x2, w_out, wt_gate, wt_up, g1, b1):
    t, d = x2.shape
    f = wt_gate.shape[0]
    tm = _row_tile(t, EPILOGUE_ROWS)

    def body(r_ref, a_ref, x_ref, wo_ref, wg_ref, wu_ref, g_ref, b_ref, xh_ref, rs_ref, hb_ref, dg_ref, du_ref,
             act_ref):
        mix = _dot_nn(r_ref[...], wo_ref[0:RET_W, :]) + _dot_nn(a_ref[...], wo_ref[RET_W:RET_W + ATT_W, :])
        z = ALPHA * x_ref[...] + mix
        xhat, rstd = _layer_norm_stats(z)
        xh_ref[...] = xhat
        rs_ref[...] = jnp.broadcast_to(rstd, rs_ref.shape)
        h = (xhat * g_ref[...] + b_ref[...]).astype(BF16)
        hb_ref[...] = h
        g = _dot_nt(h, wg_ref[...])
        u = _dot_nt(h, wu_ref[...])
        sg = _sigmoid(g)
        silu = g * sg
        dg_ref[...] = (u * (sg * (1.0 + g * (1.0 - sg)))).astype(BF16)
        du_ref[...] = silu.astype(BF16)
        act_ref[...] = (silu * u).astype(BF16)

    wide, narrow = _row_spec(tm, f), _row_spec(tm, d)
    return pl.pallas_call(
        body, name="mix_ln1_ffn_up", grid=(t // tm,),
        in_specs=[_row_spec(tm, RET_W), _row_spec(tm, ATT_W), narrow, _resident_spec(w_out.shape),
                  _resident_spec(wt_gate.shape), _resident_spec(wt_up.shape), _full_spec(g1.shape),
                  _full_spec(b1.shape)],
        out_specs=[narrow, _row_spec(tm, LANES), narrow, wide, wide, wide],
        out_shape=[SDS((t, d), F32), SDS((t, LANES), F32), SDS((t, d), BF16)] + [SDS((t, f), BF16)] * 3,
        compiler_params=_params(("parallel",)),
    )(r, a, x2, w_out, wt_gate, wt_up, g1, b1)


def _ffn_down_ln2_loss(me, act, dact_dg, dact_du, h1b, p2, xhat1, target, w_down, w_pg, wt_pe, g1, b1, g2, b2):
    t, d = xhat1.shape
    f = act.shape[1]
    pdim = p2.shape[1]
    tm = _row_tile(t, EPILOGUE_ROWS)
    n_steps = t // tm
    own_rows = d // N_DEV

    def body(me_ref, act_ref, fg_ref, fu_ref, hb_ref, p_ref, xh1_ref, tgt_ref, wd_ref, wpg_ref, wpe_ref, g1_ref,
             b1_ref, g2_ref, b2_ref, dz_ref, dzb_ref, ds_ref, dg_ref, du_ref, acc_ref,
             pe_whole, pe_own, pg_whole, pg_own, pe_acc, pg_acc):
        @pl.when(pl.program_id(0) == 0)
        def _():
            acc_ref[...] = jnp.zeros_like(acc_ref)
            pe_acc[...] = jnp.zeros_like(pe_acc)
            pg_acc[...] = jnp.zeros_like(pg_acc)

        for lo, hi in _sub_rows(tm):
            h1 = xh1_ref[lo:hi, :] * g1_ref[...] + b1_ref[...]
            pb = p_ref[lo:hi, :].astype(BF16)
            pg = _sigmoid(_dot_nn(hb_ref[lo:hi, :], wpg_ref[...]))
            ple = _dot_nt(pb, wpe_ref[...])
            gated = pg * ple
            dgate = gated * (1.0 - pg)
            ffn = _dot_nn(act_ref[lo:hi, :], wd_ref[...])
            z2 = ALPHA * h1 + gated + ffn
            xhat2, rstd2 = _layer_norm_stats(z2)
            err = xhat2 * g2_ref[...] + b2_ref[...] - tgt_ref[lo:hi, :]
            dy = err * (1.0 / d)
            dz = _layer_norm_bwd(dy * g2_ref[...], xhat2, rstd2)
            dzb = dz.astype(BF16)
            dz_ref[lo:hi, :] = dz
            dzb_ref[lo:hi, :] = dzb
            dsb, dpleb = (dz * dgate).astype(BF16), (dz * pg).astype(BF16)
            ds_ref[lo:hi, :] = dsb
            pe_acc[...] += _dot_tn(dpleb, pb)
            pg_acc[...] += _dot_tn(hb_ref[lo:hi, :], dsb)
            acc_ref[0:1, :] += jnp.sum(err * err, axis=0, keepdims=True)
            acc_ref[1:2, :] += jnp.sum(dy * xhat2, axis=0, keepdims=True)
            acc_ref[2:3, :] += jnp.sum(dy, axis=0, keepdims=True)
            for c0, c1 in _col_halves(f):
                da = _dot_nt(dzb, wd_ref[c0:c1, :])
                dg_ref[lo:hi, c0:c1] = (da * fg_ref[lo:hi, c0:c1].astype(F32)).astype(BF16)
                du_ref[lo:hi, c0:c1] = (da * fu_ref[lo:hi, c0:c1].astype(F32)).astype(BF16)

        @pl.when(pl.program_id(0) == n_steps - 1)
        def _():
            mine = pl.ds(pl.multiple_of(me_ref[0] * own_rows, 8), own_rows)
            for whole, own, acc in ((pe_whole, pe_own, pe_acc), (pg_whole, pg_own, pg_acc)):
                whole[...] = acc[...].astype(BF16)
                own[...] = acc[mine, :]

    vec = _full_spec(g1.shape)
    wide, narrow = _row_spec(tm, f), _row_spec(tm, d)
    products = [(d, pdim), (own_rows, pdim), (d, d), (own_rows, d)]
    return pl.pallas_call(
        body, name="ffn_down_ln2_loss", grid=(n_steps,),
        in_specs=[_smem_spec(), wide, wide, wide, narrow, _row_spec(tm, pdim), narrow, narrow,
                  _full_spec(w_down.shape), _full_spec(w_pg.shape), _full_spec(wt_pe.shape), vec, vec, vec, vec],
        out_specs=[narrow] * 3 + [wide, wide, _acc_spec((8, d))] + [_full_spec(s) for s in products],
        out_shape=[SDS((t, d), F32), SDS((t, d), BF16), SDS((t, d), BF16),
                   SDS((t, f), BF16), SDS((t, f), BF16), SDS((8, d), F32)]
        + [SDS(s, BF16 if k % 2 == 0 else F32) for k, s in enumerate(products)],
        scratch_shapes=[pltpu.VMEM((d, pdim), F32), pltpu.VMEM((d, d), F32)],
        compiler_params=_params(("arbitrary",)),
    )(me, act, dact_dg, dact_du, h1b, p2, xhat1, target, w_down, w_pg, wt_pe, g1, b1, g2, b2)


def _after(after, body):
    k = len(after)
    return (lambda *refs: body(*refs[k:])), [ANY_SPEC] * k


def _resident_spec(shape):
    return pl.BlockSpec(shape, lambda i: (0,) * len(shape), pipeline_mode=pl.Buffered(1))


def _dh1_ln1_bwd(me, dz2, dg, dup, dsb, xhat1, rstd1, r, a, wt_gate, wt_up, w_pg, w_out, g1, after=()):
    t, d = dz2.shape
    f = dg.shape[1]
    tm = _row_tile(t, MATMUL_ROWS)
    n_steps = t // tm
    wo_rows = RET_W + ATT_W
    own_rows = wo_rows // N_DEV

    def body(me_ref, dz_ref, dg_ref, du_ref, ds_ref, xh1_ref, rs1_ref, r_ref, a_ref, wg_ref, wu_ref, wpg_ref,
             wo_ref, g1_ref, dz1_ref, dz1b_ref, dr_ref, da_ref, acc_ref, wo_whole, wo_own, wo_acc):
        @pl.when(pl.program_id(0) == 0)
        def _():
            acc_ref[...] = jnp.zeros_like(acc_ref)
            wo_acc[...] = jnp.zeros_like(wo_acc)

        for lo, hi in _sub_rows(tm):
            dh = (ALPHA * dz_ref[lo:hi, :] + _dot_nn(dg_ref[lo:hi, :], wg_ref[...])
                  + _dot_nn(du_ref[lo:hi, :], wu_ref[...]) + _dot_nt(ds_ref[lo:hi, :], wpg_ref[...]))
            xhat, rstd = xh1_ref[lo:hi, :], rs1_ref[lo:hi, 0:1]
            dz1 = _layer_norm_bwd(dh * g1_ref[...], xhat, rstd)
            dz1b = dz1.astype(BF16)
            dz1_ref[lo:hi, :] = dz1
            dz1b_ref[lo:hi, :] = dz1b
            acc_ref[0:1, :] += jnp.sum(dh * xhat, axis=0, keepdims=True)
            acc_ref[1:2, :] += jnp.sum(dh, axis=0, keepdims=True)
            dr_ref[lo:hi, :] = _dot_nt(dz1b, wo_ref[0:RET_W, :]).astype(BF16)
            da_ref[lo:hi, :] = _dot_nt(dz1b, wo_ref[RET_W:RET_W + ATT_W, :]).astype(BF16)
            wo_acc[0:RET_W, :] += _dot_tn(r_ref[lo:hi, :], dz1b)
            wo_acc[RET_W:wo_rows, :] += _dot_tn(a_ref[lo:hi, :], dz1b)

        @pl.when(pl.program_id(0) == n_steps - 1)
        def _():
            wo_whole[...] = wo_acc[...].astype(BF16)
            wo_own[...] = wo_acc[pl.ds(pl.multiple_of(me_ref[0] * own_rows, 8), own_rows), :]

    body, lead = _after(after, body)
    return pl.pallas_call(
        body, name="dh1_ln1_bwd", grid=(n_steps,),
        in_specs=lead + [_smem_spec(), _row_spec(tm, d), _row_spec(tm, f), _row_spec(tm, f), _row_spec(tm, d),
                         _row_spec(tm, d), _row_spec(tm, LANES), _row_spec(tm, RET_W), _row_spec(tm, ATT_W),
                         _resident_spec(wt_gate.shape), _resident_spec(wt_up.shape), _resident_spec(w_pg.shape),
                         _resident_spec(w_out.shape), _full_spec(g1.shape)],
        out_specs=[_row_spec(tm, d), _row_spec(tm, d), _row_spec(tm, RET_W), _row_spec(tm, ATT_W), _acc_spec((8, d)),
                   _resident_spec((wo_rows, d)), _resident_spec((own_rows, d))],
        out_shape=[SDS((t, d), F32), SDS((t, d), BF16), SDS((t, RET_W), BF16), SDS((t, ATT_W), BF16),
                   SDS((8, d), F32), SDS((wo_rows, d), BF16), SDS((own_rows, d), F32)],
        scratch_shapes=[pltpu.VMEM((wo_rows, d), F32)],
        compiler_params=_params(("arbitrary",), DH1_VMEM_LIMIT),
    )(*after, me, dz2, dg, dup, dsb, xhat1, rstd1, r, a, wt_gate, wt_up, w_pg, w_out, g1)


def _in_proj_bwd(dz1, parts, wt_in, after=()):
    t, d = dz1.shape
    tm = _row_tile(t, MATMUL_ROWS)
    widths = [p.shape[1] for p in parts]

    def body(*refs):
        dz_ref, part_refs, w_ref, dx_ref = refs[0], refs[1:1 + len(parts)], refs[-2], refs[-1]
        acc = ALPHA * dz_ref[...]
        lo = 0
        for p_ref, w in zip(part_refs, widths):
            acc = acc + _dot_nn(p_ref[...], w_ref[lo:lo + w, :])
            lo += w
        dx_ref[...] = acc

    body, lead = _after(after, body)
    return pl.pallas_call(
        body, name="in_proj_bwd", grid=(t // tm,),
        in_specs=lead + [_row_spec(tm, d)] + [_row_spec(tm, w) for w in widths] + [_full_spec(wt_in.shape)],
        out_specs=_row_spec(tm, d), out_shape=SDS((t, d), F32),
        compiler_params=_params(("parallel",)),
    )(*after, dz1, *parts, wt_in)


def _weight_grad(name, me, parts, rhs, after=()):
    t, n = rhs.shape
    widths = [p.shape[1] for p in parts]
    rows = sum(widths)
    own_rows = rows // N_DEV
    tk = _row_tile(t, MATMUL_ROWS)
    n_steps = t // tk
    step = 256

    def body(*refs):
        me_ref, part_refs, rhs_ref = refs[0], refs[1:1 + len(parts)], refs[1 + len(parts)]
        full_ref, own_ref, acc = refs[-3], refs[-2], refs[-1]
        i = pl.program_id(0)

        def products(first):
            b = rhs_ref[...].astype(BF16)
            lo = 0
            for p_ref, w in zip(part_refs, widths):
                for c0 in range(0, w, step):
                    c1 = min(c0 + step, w)
                    val = _dot_tn(p_ref[:, c0:c1].astype(BF16), b)
                    if first:
                        acc[lo + c0:lo + c1, :] = val
                    else:
                        acc[lo + c0:lo + c1, :] += val
                lo += w

        pl.when(i == 0)(functools.partial(products, True))
        pl.when(i > 0)(functools.partial(products, False))

        @pl.when(i == n_steps - 1)
        def _():
            full_ref[...] = acc[...].astype(BF16)
            own_ref[...] = acc[pl.ds(pl.multiple_of(me_ref[0] * own_rows, 8), own_rows), :]

    body, lead = _after(after, body)
    return pl.pallas_call(
        body, name=name, grid=(n_steps,),
        in_specs=lead + [_smem_spec()] + [_row_spec(tk, w) for w in widths] + [_row_spec(tk, n)],
        out_specs=[_full_spec((rows, n)), _full_spec((own_rows, n))],
        out_shape=[SDS((rows, n), BF16), SDS((own_rows, n), F32)],
        scratch_shapes=[pltpu.VMEM((rows, n), F32)],
        compiler_params=_params(("arbitrary",)),
    )(*after, me, *parts, rhs)


def _weight_grad_jobs(name, me, jobs, after=()):
    count = len(jobs)
    t = jobs[0][0].shape[0]
    tk = _row_tile(t, MATMUL_ROWS)
    n_steps = t // tk
    shapes = [(lhs.shape[1], rhs.shape[1]) for lhs, rhs in jobs]
    most_rows, most_cols = max(r for r, _ in shapes), max(n for _, n in shapes)
    step = 256

    def body(*refs):
        me_ref, lhs_refs, rhs_refs = refs[0], refs[1:1 + count], refs[1 + count:1 + 2 * count]
        full_refs, own_refs = refs[1 + 2 * count:1 + 3 * count], refs[1 + 3 * count:1 + 4 * count]
        acc, whole, mine, sems = refs[1 + 4 * count:]
        job, i = pl.program_id(0), pl.program_id(1)

        def leaving(j):
            rows, n = shapes[j]
            return (pltpu.make_async_copy(whole.at[0:rows, 0:n], full_refs[j], sems.at[0]),
                    pltpu.make_async_copy(mine.at[0:rows // N_DEV, 0:n], own_refs[j], sems.at[1]))

        def products(j, first):
            rows, n = shapes[j]
            b = rhs_refs[j][...].astype(BF16)
            for c0 in range(0, rows, step):
                c1 = min(c0 + step, rows)
                val = _dot_tn(lhs_refs[j][:, c0:c1].astype(BF16), b)
                if first:
                    acc[c0:c1, 0:n] = val
                else:
                    acc[c0:c1, 0:n] += val

        def finish(j):
            rows, n = shapes[j]
            own_rows = rows // N_DEV
            if j > 0:
                for cp in leaving(j - 1):
                    cp.wait()
            whole[0:rows, 0:n] = acc[0:rows, 0:n].astype(BF16)
            mine[0:own_rows, 0:n] = acc[pl.ds(pl.multiple_of(me_ref[0] * own_rows, 8), own_rows), 0:n]
            for cp in leaving(j):
                cp.start()
            if j == count - 1:
                for cp in leaving(j):
                    cp.wait()

        for j in range(count):
            pl.when((job == j) & (i == 0))(functools.partial(products, j, True))
            pl.when((job == j) & (i > 0))(functools.partial(products, j, False))
            pl.when((job == j) & (i == n_steps - 1))(functools.partial(finish, j))

    def turn(j):
        return lambda job, i: (jnp.where(job == j, i, jnp.where(job < j, 0, n_steps - 1)), 0)

    body, lead = _after(after, body)
    res = pl.pallas_call(
        body, name=name, grid=(count, n_steps),
        in_specs=lead + [_smem_spec()] + [pl.BlockSpec((tk, rows), turn(j)) for j, (rows, _) in enumerate(shapes)]
        + [pl.BlockSpec((tk, n), turn(j)) for j, (_, n) in enumerate(shapes)],
        out_specs=[ANY_SPEC] * (2 * count),
        out_shape=[SDS((rows, n), BF16) for rows, n in shapes] + [SDS((rows // N_DEV, n), F32) for rows, n in shapes],
        scratch_shapes=[pltpu.VMEM((most_rows, most_cols), F32), pltpu.VMEM((most_rows, most_cols), BF16),
                        pltpu.VMEM((most_rows // N_DEV, most_cols), F32), pltpu.SemaphoreType.DMA((2,))],
        compiler_params=_params(("arbitrary", "arbitrary")),
    )(*after, me, *[lhs for lhs, _ in jobs], *[rhs for _, rhs in jobs])
    return list(res[:count]), list(res[count:])


def _log_decay(decay_f, decay_b):
    def body(f_ref, b_ref, lf_ref, lb_ref):
        lf_ref[...] = jnp.log1p(-jnp.exp2(f_ref[...]))
        lb_ref[...] = jnp.log1p(-jnp.exp2(b_ref[...]))

    return pl.pallas_call(body, name="log_decay", out_shape=[SDS(decay_f.shape, F32)] * 2)(decay_f, decay_b)


def _chunk(ref, n):
    return ref[pl.ds(pl.multiple_of(n * CHUNK, CHUNK), CHUNK), :]


def _group_sum(is_a, v):
    sa = jnp.sum(jnp.where(is_a, v, 0.0), axis=1, keepdims=True)
    sb = jnp.sum(jnp.where(is_a, 0.0, v), axis=1, keepdims=True)
    return jnp.where(is_a, sa, sb)


def _seq_spec(s, col_block):
    return pl.BlockSpec((s, LANES), lambda b, h: (b, col_block + h))


def _smem_spec():
    return pl.BlockSpec(memory_space=pltpu.SMEM)


RET_UNROLL = 4
BWD_MAIN_UNROLL = 8
FWD_PREP_UNROLL = 16
BWD_PREP_UNROLL = 8


def _chunk_loop(n_chunks, body, init, unroll):
    u = unroll if n_chunks % unroll == 0 else 1

    def trip(i, carry):
        for j in range(u):
            carry = body(i * u + j, carry)
        return carry

    return lax.fori_loop(0, n_chunks // u, trip, init)


def _stacked_tables(lgf_ref, lgb_ref, pair):
    lane = lax.broadcasted_iota(jnp.int32, (1, LANES), 1)
    is_a = lane < HEAD_DIM
    lgf = jnp.where(is_a, lgf_ref[2 * pair], lgf_ref[2 * pair + 1])
    lgb = jnp.where(is_a, lgb_ref[2 * pair], lgb_ref[2 * pair + 1])
    row = lax.broadcasted_iota(jnp.int32, (CHUNK, 1), 0).astype(F32)
    kdec_f, qdec_f = jnp.exp(lgf * (CHUNK - 1.0 - row)), jnp.exp(lgf * (row + 1.0))
    kdec_b, qdec_b = jnp.exp(lgb * row), jnp.exp(lgb * (CHUNK - row))
    tab = dict(
        is_a=is_a, row=row, lam_f=jnp.exp(lgf * CHUNK), lam_b=jnp.exp(lgb * CHUNK),
        kdec=jnp.concatenate([kdec_f, kdec_b], axis=1), qdec=jnp.concatenate([qdec_f, qdec_b], axis=1),
        qexp=jnp.concatenate([jnp.broadcast_to(row + 1.0, (CHUNK, LANES)),
                              jnp.broadcast_to(CHUNK - row, (CHUNK, LANES))], axis=1),
        kexp=jnp.concatenate([jnp.broadcast_to(CHUNK - 1.0 - row, (CHUNK, LANES)),
                              jnp.broadcast_to(row, (CHUNK, LANES))], axis=1),
    )
    r = lax.broadcasted_iota(jnp.int32, (2 * LANES, LANES), 0)
    c = lax.broadcasted_iota(jnp.int32, (2 * LANES, LANES), 1)
    tab["diag2"] = ((r & (LANES - 1)) < HEAD_DIM) == (c < HEAD_DIM)
    i2 = lax.broadcasted_iota(jnp.int32, (2 * CHUNK, CHUNK), 0)
    j = lax.broadcasted_iota(jnp.int32, (2 * CHUNK, CHUNK), 1)
    head_b = i2 >= CHUNK
    diff = ((i2 & (CHUNK - 1)) - j).astype(F32)
    up, dn = jnp.maximum(diff, 0.0), jnp.maximum(-diff, 0.0)
    lgf2 = jnp.where(head_b, lgf_ref[2 * pair + 1], lgf_ref[2 * pair])
    lgb2 = jnp.where(head_b, lgb_ref[2 * pair + 1], lgb_ref[2 * pair])
    ef = jnp.where(diff >= 0, jnp.exp(lgf2 * up), 0.0)
    eb = jnp.where(diff <= 0, jnp.exp(lgb2 * dn), 0.0)
    tab["d2"] = ef + eb
    tab["df2"] = ef * up
    tab["db2"] = eb * dn
    return tab


def _stack_pair(is_a, x):
    zero = jnp.zeros_like(x)
    return jnp.concatenate([jnp.where(is_a, x, zero), jnp.where(is_a, zero, x)], axis=0)


def _unstack_pair(is_a, x2):
    return jnp.where(is_a, x2[0:CHUNK, :], x2[CHUNK:2 * CHUNK, :])


def _both_ways(x, dec):
    return (jnp.concatenate([x, x], axis=1) * dec).astype(BF16)


def _scan_states(n_chunks, st, up_rows, up_lam, down_rows, down_lam):
    zero = jnp.zeros((LANES, LANES), F32)

    def up(n, r):
        new = st[n, up_rows, :]
        st[n, up_rows, :] = r
        return r * up_lam + new

    def down(s, r):
        n = n_chunks - 1 - s
        new = st[n, down_rows, :]
        st[n, down_rows, :] = r
        return r * down_lam + new

    lax.fori_loop(0, n_chunks, up, zero)
    lax.fori_loop(0, n_chunks, down, zero)


FWD_ROWS, BWD_ROWS = pl.ds(0, LANES), pl.ds(LANES, LANES)


def _state_spec(n_chunks, pairs):
    return pl.BlockSpec((n_chunks, 2 * LANES, LANES), lambda b, h: (b * pairs + h, 0, 0))


ST_GAIN, ST_XF, ST_XB, ST_IFA, ST_IFB, ST_IBA, ST_IBB, ST_LF, ST_LB = 0, 1, 2, 3, 4, 5, 6, 8, 9
ST_ROWS = 16


GW = GROUP * HEAD_DIM
KEYS = 3 * BLOCK


def _attn_tables(g, bias_ref):
    r = lax.broadcasted_iota(jnp.int32, (GROUP * BLOCK, KEYS), 0)
    kj = lax.broadcasted_iota(jnp.int32, (GROUP * BLOCK, KEYS), 1)
    qi = r & (BLOCK - 1)
    hh = lax.shift_right_logical(r, 7)
    dist = jnp.abs(kj - BLOCK - qi)
    slope = jnp.exp2(-(GROUP * g + hh + 1).astype(F32) * (8.0 / ATTN_HEADS))
    inside = jnp.where(dist <= BLOCK, -slope * dist.astype(F32), NEG_INF)
    bias_ref[BIAS_INSIDE] = inside
    bias_ref[BIAS_FIRST] = jnp.where(kj >= BLOCK, inside, NEG_INF)
    bias_ref[BIAS_LAST] = jnp.where(kj < 2 * BLOCK, inside, NEG_INF)


BIAS_INSIDE, BIAS_FIRST, BIAS_LAST = 0, 1, 2


def _own_lanes(g):
    return lax.shift_right_logical(lax.broadcasted_iota(jnp.int32, (1, LANES), 1), 6) == g


def _mask_keys(x_ref, g, scale, pad_ref, s):
    pad_ref[0:BLOCK, :] = jnp.zeros((BLOCK, LANES), BF16)
    pad_ref[BLOCK + s:2 * BLOCK + s, :] = jnp.zeros((BLOCK, LANES), BF16)
    pad_ref[BLOCK:BLOCK + s, :] = jnp.where(_own_lanes(g), x_ref[...].astype(F32) * scale, 0.0).astype(BF16)


def _lane_block(x, j):
    return x[:, j * LANES:(j + 1) * LANES]


def _stack_heads(x, g):
    assert GROUP == 4 and GW == 2 * LANES
    x1 = pltpu.roll(x, HEAD_DIM, 1)
    keep = _own_lanes(g)
    zero = jnp.zeros((BLOCK, LANES), x.dtype)
    rows = []
    for h in range(GROUP):
        for_g0 = _lane_block(x, h // 2) if h % 2 == 0 else _lane_block(x1, ((h + 1) // 2) % 2)
        for_g1 = _lane_block(x, h // 2) if h % 2 == 1 else _lane_block(x1, h // 2)
        rows.append(jnp.where(keep, jnp.where(g == 0, for_g0, for_g1), zero))
    return jnp.concatenate(rows, axis=0)


def _unstack_heads(x4, g):
    p = [x4[h * BLOCK:(h + 1) * BLOCK, :] for h in range(GROUP)]
    cat = lambda a, b: jnp.concatenate([a, b], axis=1)
    in_place = jnp.where(g == 0, cat(p[0], p[2]), cat(p[1], p[3]))
    one_left = jnp.where(g == 0, cat(p[1], p[3]), cat(p[2], p[0]))
    return in_place + pltpu.roll(one_left, HEAD_DIM, 1)


def _sink_column(sink_ref, g):
    rh = lax.shift_right_logical(lax.broadcasted_iota(jnp.int32, (GROUP * BLOCK, 1), 0), 7)
    col = jnp.zeros((GROUP * BLOCK, 1), F32)
    for h in range(GROUP):
        col = jnp.where(rh == h, sink_ref[GROUP * g + h], col)
    return col


def _attn_probs(qm, k3, bias_ref, sink_col, n, s):
    which = jnp.where(n == 0, BIAS_FIRST, jnp.where(n == s // BLOCK - 1, BIAS_LAST, BIAS_INSIDE))
    logits = _dot_nt(qm, k3) + bias_ref[which]
    m = jnp.maximum(jnp.max(logits, axis=1, keepdims=True), sink_col)
    e = jnp.exp(logits - m)
    e_sink = jnp.exp(sink_col - m)
    inv = 1.0 / (jnp.sum(e, axis=1, keepdims=True) + e_sink)
    return e * inv, e_sink * inv


PAIRS_PER_KV = (RET_HEADS // 2) // KV_HEADS
FWD_ORDER = "rrarra"
BWD_ORDER = "rrarar"


def _trip_order(order, chunks, blocks):
    if order.count("r") == chunks and order.count("a") == blocks:
        return order
    return "r" * chunks + "a" * blocks


def _mixers_fwd(u, lgf, lgb, gn_gain, sink, b_loc, after=()):
    t = u.shape[0]
    s = t // b_loc
    n_chunks = s // CHUNK
    pairs = RET_HEADS // 2
    trips = n_chunks // RET_UNROLL
    blocks_half = (s // BLOCK) // PAIRS_PER_KV
    per_trip = blocks_half // trips
    assert n_chunks % RET_UNROLL == 0 and blocks_half % trips == 0 and PAIRS_PER_KV == 2 and s >= 2 * BLOCK

    def body(lgf_ref, lgb_ref, sink_ref, q_ref, k_ref, v_ref, g_ref, gain_ref, aq_ref, ak_ref, av_ref,
             r_ref, xhat_ref, rstd_ref, a_ref, st, kpad, vpad, bias):
        pair = pl.program_id(1)
        g, half = lax.shift_right_logical(pair, 1), pair & 1
        tab = _stacked_tables(lgf_ref, lgb_ref, pair)
        is_a = tab["is_a"]

        @pl.when(half == 0)
        def _():
            _attn_tables(g, bias)
            _mask_keys(ak_ref, g, Q_SCALE, kpad, s)
            _mask_keys(av_ref, g, 1.0, vpad, s)

        def kv_body(n, _):
            k8 = _chunk(k_ref, n).astype(F32) * Q_SCALE
            st[n] = jnp.where(tab["diag2"], _dot_tn(_both_ways(k8, tab["kdec"]), _chunk(v_ref, n)), 0.0)
            return 0

        _chunk_loop(n_chunks, kv_body, 0, FWD_PREP_UNROLL)
        _scan_states(n_chunks, st, FWD_ROWS, tab["lam_f"], BWD_ROWS, tab["lam_b"])
        sink_col = _sink_column(sink_ref, g)

        def retention_chunk(n):
            q = _chunk(q_ref, n)
            k8 = (_chunk(k_ref, n).astype(F32) * Q_SCALE).astype(BF16)
            v = _chunk(v_ref, n)
            p2 = (_dot_nt(_stack_pair(is_a, q), k8) * tab["d2"]).astype(BF16)
            y = _unstack_pair(is_a, _dot_nn(p2, v))
            y = y + _dot_nn(_both_ways(q.astype(F32), tab["qdec"]), st[n].astype(BF16))
            rows = pl.ds(pl.multiple_of(n * CHUNK, CHUNK), CHUNK)
            mu = _group_sum(is_a, y) * (1.0 / HEAD_DIM)
            dlt = y - mu
            var = _group_sum(is_a, dlt * dlt) * (1.0 / HEAD_DIM)
            rstd = lax.rsqrt(var + GN_EPS)
            xhat = dlt * rstd
            xhat_ref[rows, :] = xhat
            rstd_ref[rows, :] = rstd
            gate = _chunk(g_ref, n).astype(F32)
            r_ref[rows, :] = (xhat * gain_ref[...] * gate * _sigmoid(gate)).astype(BF16)

        def attention_block(blk):
            n = half * blocks_half + blk
            rows = pl.ds(pl.multiple_of(blk * BLOCK, BLOCK), BLOCK)
            keys = pl.ds(pl.multiple_of(n * BLOCK, BLOCK), KEYS)
            p, _ = _attn_probs(_stack_heads(aq_ref[rows, :], g), kpad[keys, :], bias, sink_col, n, s)
            a_ref[rows, :] = _unstack_heads(_dot_nn(p.astype(BF16), vpad[keys, :]), g).astype(BF16)

        def trip(i, _):
            chunk, blk = 0, 0
            for kind in _trip_order(FWD_ORDER, RET_UNROLL, per_trip):
                if kind == "r":
                    retention_chunk(i * RET_UNROLL + chunk)
                    chunk += 1
                else:
                    attention_block(i * per_trip + blk)
                    blk += 1
            return 0

        lax.fori_loop(0, trips, trip, 0)

    lane_blk = lambda c0: _seq_spec(s, c0 // LANES)
    half_rows = blocks_half * BLOCK
    aq_spec = pl.BlockSpec((half_rows, GW), lambda b, h: (b * PAIRS_PER_KV + (h & 1), C_AQ // GW + h // 2))
    a_spec = pl.BlockSpec((half_rows, GW), lambda b, h: (b * PAIRS_PER_KV + (h & 1), h // 2))
    kv_spec = lambda c0: pl.BlockSpec((s, LANES), lambda b, h: (b, c0 // LANES))
    pad = pltpu.VMEM((s + 2 * BLOCK, LANES), BF16)
    body, lead = _after(after, body)
    return pl.pallas_call(
        body, name="mixers_fwd", grid=(b_loc, pairs),
        in_specs=lead + [_smem_spec(), _smem_spec(), _smem_spec(), lane_blk(C_RQ), lane_blk(C_RK), lane_blk(C_RV),
                         lane_blk(C_RG), pl.BlockSpec((1, LANES), lambda b, h: (0, h)), aq_spec, kv_spec(C_AK),
                         kv_spec(C_AV)],
        out_specs=[_seq_spec(s, 0), _seq_spec(s, 0), _seq_spec(s, 0), a_spec, _state_spec(n_chunks, pairs)],
        out_shape=[SDS((t, RET_W), BF16), SDS((t, RET_W), F32), SDS((t, RET_W), F32), SDS((t, ATT_W), BF16),
                   SDS((b_loc * pairs * n_chunks, 2 * LANES, LANES), F32)],
        scratch_shapes=[pad, pad, pltpu.VMEM((3, GROUP * BLOCK, KEYS), F32)],
        compiler_params=_params(("arbitrary", "arbitrary")),
    )(*after, lgf, lgb, sink, u, u, u, u, gn_gain, u, u, u)


def _mixers_bwd(u, xhat, rstd, states, dr, da, lgf, lgb, gn_gain, sink, b_loc, after=()):
    t = u.shape[0]
    s = t // b_loc
    n_chunks = s // CHUNK
    pairs = RET_HEADS // 2
    unroll = BWD_MAIN_UNROLL if n_chunks % BWD_MAIN_UNROLL == 0 else RET_UNROLL
    trips = n_chunks // unroll
    blocks_half = (s // BLOCK) // PAIRS_PER_KV
    per_trip = blocks_half // trips
    assert n_chunks % unroll == 0 and blocks_half % trips == 0 and PAIRS_PER_KV == 2 and s >= 2 * BLOCK

    def body(lgf_ref, lgb_ref, sink_ref, q_ref, k_ref, v_ref, g_ref, xhat_ref, rstd_ref, dr_ref, gain_ref,
             aq_ref, ak_ref, av_ref, do_ref, st,
             dq_ref, dk_ref, dv_ref, dg_ref, st_ref, daq_ref, dak_ref, dav_ref, dsink_ref,
             gr, dy_s, kpad, vpad, bias, dk_acc, dv_acc):
        pair = pl.program_id(1)
        g, half = lax.shift_right_logical(pair, 1), pair & 1
        tab = _stacked_tables(lgf_ref, lgb_ref, pair)
        is_a = tab["is_a"]
        gain = gain_ref[...]

        @pl.when(half == 0)
        def _():
            _attn_tables(g, bias)
            _mask_keys(ak_ref, g, Q_SCALE, kpad, s)
            _mask_keys(av_ref, g, 1.0, vpad, s)
            dsink_ref[...] = jnp.zeros_like(dsink_ref)

        @pl.when(pair == 0)
        def _():
            dk_acc[...] = jnp.zeros_like(dk_acc)
            dv_acc[...] = jnp.zeros_like(dv_acc)

        def norm_body(n, dgain):
            rows = pl.ds(pl.multiple_of(n * CHUNK, CHUNK), CHUNK)
            xhat, rstd = xhat_ref[rows, :], rstd_ref[rows, :]
            gate = g_ref[rows, :].astype(F32)
            sg = _sigmoid(gate)
            silu = gate * sg
            d_out = dr_ref[rows, :].astype(F32)
            dg_ref[rows, :] = (d_out * xhat * gain * (sg * (1.0 + gate * (1.0 - sg)))).astype(BF16)
            dxh = d_out * gain * silu
            m1 = _group_sum(is_a, dxh) * (1.0 / HEAD_DIM)
            m2 = _group_sum(is_a, dxh * xhat) * (1.0 / HEAD_DIM)
            dy = (rstd * (dxh - m1 - xhat * m2)).astype(BF16)
            dy_s[rows, :] = dy
            qf = q_ref[rows, :].astype(F32)
            gr[n] = jnp.where(tab["diag2"], _dot_tn(_both_ways(qf, tab["qdec"]), dy), 0.0)
            return dgain + jnp.sum(d_out * xhat * silu, axis=0, keepdims=True)

        colsum = lambda x: jnp.sum(x, axis=0, keepdims=True)

        def grad_body(n, carry):
            xfb, ifa, ifb, iba, ibb, lf, lb = carry
            rows = pl.ds(pl.multiple_of(n * CHUNK, CHUNK), CHUNK)
            q = q_ref[rows, :]
            qf = q.astype(F32)
            k8f = k_ref[rows, :].astype(F32) * Q_SCALE
            k8 = k8f.astype(BF16)
            v = v_ref[rows, :]
            dy = dy_s[rows, :]
            q2, dy2 = _stack_pair(is_a, q), _stack_pair(is_a, dy)
            sc = _dot_nt(q2, k8)
            dp = _dot_nt(dy2, v)
            a2 = (sc * tab["d2"]).astype(BF16)
            ds2 = (dp * tab["d2"]).astype(BF16)
            dq = _unstack_pair(is_a, _dot_nn(ds2, k8))
            dk = _dot_tn(ds2, q2)
            dv = _dot_tn(a2, dy2)
            prod = sc * dp
            pf, pb = prod * tab["df2"], prod * tab["db2"]
            ifa, ifb = ifa + colsum(pf[0:CHUNK, :]), ifb + colsum(pf[CHUNK:2 * CHUNK, :])
            iba, ibb = iba + colsum(pb[0:CHUNK, :]), ibb + colsum(pb[CHUNK:2 * CHUNK, :])
            states, sgrads = st[n], gr[n]
            sb, gb = states.astype(BF16), sgrads.astype(BF16)
            dqc = _dot_nt(dy, sb) * tab["qdec"]
            dkc = _dot_nt(v, gb) * tab["kdec"]
            dv = dv + _dot_nn(_both_ways(k8f, tab["kdec"]), gb)
            dq_ref[rows, :] = (dq + dqc[:, 0:LANES] + dqc[:, LANES:2 * LANES]).astype(BF16)
            dk_ref[rows, :] = ((dk + dkc[:, 0:LANES] + dkc[:, LANES:2 * LANES]) * Q_SCALE).astype(BF16)
            dv_ref[rows, :] = dv.astype(BF16)
            q2w, k2w = jnp.concatenate([qf, qf], axis=1), jnp.concatenate([k8f, k8f], axis=1)
            xfb = xfb + colsum(tab["qexp"] * q2w * dqc + tab["kexp"] * k2w * dkc)
            prod_s = sgrads * states
            lf, lb = lf + colsum(prod_s[0:LANES, :]), lb + colsum(prod_s[LANES:2 * LANES, :])
            return xfb, ifa, ifb, iba, ibb, lf, lb

        sink_col = _sink_column(sink_ref, g)
        head_row = lax.broadcasted_iota(jnp.int32, dsink_ref.shape, 0)

        def attention_block(blk):
            n = half * blocks_half + blk
            rows = pl.ds(pl.multiple_of(blk * BLOCK, BLOCK), BLOCK)
            keys = pl.ds(pl.multiple_of(n * BLOCK, BLOCK), KEYS)
            qm = _stack_heads(aq_ref[rows, :], g)
            k3, v3 = kpad[keys, :], vpad[keys, :]
            p, p_sink = _attn_probs(qm, k3, bias, sink_col, n, s)
            dom = _stack_heads(do_ref[rows, :], g)
            dp = _dot_nt(dom, v3)
            delta = jnp.sum(p * dp, axis=1, keepdims=True)
            ds_mat = (p * (dp - delta)).astype(BF16)
            daq_ref[rows, :] = _unstack_heads(_dot_nn(ds_mat, k3), g).astype(BF16)
            dk_acc[keys, :] += _dot_tn(ds_mat, qm) * Q_SCALE
            dv_acc[keys, :] += _dot_tn(p.astype(BF16), dom)
            w = p_sink * delta
            upd = jnp.zeros(dsink_ref.shape, F32)
            for h in range(GROUP):
                upd = upd + jnp.where(head_row == h, -jnp.sum(w[h * BLOCK:(h + 1) * BLOCK, :]), 0.0)
            dsink_ref[...] += upd

        dgain = _chunk_loop(n_chunks, norm_body, jnp.zeros((1, LANES), F32), BWD_PREP_UNROLL)
        _scan_states(n_chunks, gr, BWD_ROWS, tab["lam_b"], FWD_ROWS, tab["lam_f"])

        def trip(i, carry):
            chunk, blk = 0, 0
            for kind in _trip_order(BWD_ORDER * (unroll // RET_UNROLL), unroll, per_trip):
                if kind == "r":
                    carry = grad_body(i * unroll + chunk, carry)
                    chunk += 1
                else:
                    attention_block(i * per_trip + blk)
                    blk += 1
            return carry

        z = jnp.zeros((1, LANES), F32)
        init = (jnp.zeros((1, 2 * LANES), F32), z, z, z, z, z, z)
        xfb, ifa, ifb, iba, ibb, lf, lb = lax.fori_loop(0, trips, trip, init)
        st_ref[...] = jnp.zeros_like(st_ref)
        st_ref[ST_GAIN:ST_GAIN + 1, :] = dgain
        st_ref[ST_XF:ST_XF + 1, :] = xfb[:, 0:LANES]
        st_ref[ST_XB:ST_XB + 1, :] = xfb[:, LANES:2 * LANES]
        st_ref[ST_IFA:ST_IFA + 1, :] = ifa
        st_ref[ST_IFB:ST_IFB + 1, :] = ifb
        st_ref[ST_IBA:ST_IBA + 1, :] = iba
        st_ref[ST_IBB:ST_IBB + 1, :] = ibb
        st_ref[ST_LF:ST_LF + 1, :] = lf * (CHUNK * tab["lam_f"])
        st_ref[ST_LB:ST_LB + 1, :] = lb * (CHUNK * tab["lam_b"])

        @pl.when(pair == pairs - 1)
        def _():
            dak_ref[...] = dk_acc[BLOCK:BLOCK + s, :].astype(BF16)
            dav_ref[...] = dv_acc[BLOCK:BLOCK + s, :].astype(BF16)

    lane_blk = lambda c0: _seq_spec(s, c0 // LANES)
    seq0 = _seq_spec(s, 0)
    half_rows = blocks_half * BLOCK
    aq_spec = pl.BlockSpec((half_rows, GW), lambda b, h: (b * PAIRS_PER_KV + (h & 1), C_AQ // GW + h // 2))
    a_spec = pl.BlockSpec((half_rows, GW), lambda b, h: (b * PAIRS_PER_KV + (h & 1), h // 2))
    kv_spec = lambda c0: pl.BlockSpec((s, LANES), lambda b, h: (b, c0 // LANES))
    kv_out = pl.BlockSpec((s, LANES), lambda b, h: (b, 0))
    state = pltpu.VMEM((n_chunks, 2 * LANES, LANES), F32)
    pad = pltpu.VMEM((s + 2 * BLOCK, LANES), BF16)
    acc = pltpu.VMEM((s + 2 * BLOCK, LANES), F32)
    body, lead = _after(after, body)
    return pl.pallas_call(
        body, name="mixers_bwd", grid=(b_loc, pairs),
        in_specs=lead + [_smem_spec(), _smem_spec(), _smem_spec(), lane_blk(C_RQ), lane_blk(C_RK), lane_blk(C_RV),
                         lane_blk(C_RG), seq0, seq0, seq0, pl.BlockSpec((1, LANES), lambda b, h: (0, h)),
                         aq_spec, kv_spec(C_AK), kv_spec(C_AV), a_spec, _state_spec(n_chunks, pairs)],
        out_specs=[seq0] * 4 + [pl.BlockSpec((ST_ROWS, LANES), lambda b, h: (b, h)), a_spec, kv_out, kv_out,
                                pl.BlockSpec((8, LANES), lambda b, h: (b * KV_HEADS + h // 2, 0))],
        out_shape=[SDS((t, RET_W), BF16)] * 4 + [SDS((b_loc * ST_ROWS, RET_W), F32), SDS((t, ATT_W), BF16),
                                                   SDS((t, KV_W), BF16), SDS((t, KV_W), BF16),
                                                   SDS((b_loc * KV_HEADS * 8, LANES), F32)],
        scratch_shapes=[state, pltpu.VMEM((s, LANES), BF16), pad, pad,
                        pltpu.VMEM((3, GROUP * BLOCK, KEYS), F32), acc, acc],
        compiler_params=_params(("arbitrary", "arbitrary")),
    )(*after, lgf, lgb, sink, u, u, u, u, xhat, rstd, dr, gn_gain, u, u, u, da, states)


def _pack_small(acc2, acc1, ret_stats, dsink, b_loc, d):
    pairs = RET_HEADS // 2

    def body(acc2_ref, acc1_ref, st_ref, dsink_ref, out_ref):
        out_ref[...] = jnp.zeros_like(out_ref)
        out_ref[ROW_LN1G:ROW_LN1G + 1, :] = acc1_ref[0:1, :]
        out_ref[ROW_LN1B:ROW_LN1B + 1, :] = acc1_ref[1:2, :]
        out_ref[ROW_LN2G:ROW_LN2G + 1, :] = acc2_ref[1:2, :]
        out_ref[ROW_LN2B:ROW_LN2B + 1, :] = acc2_ref[2:3, :]
        out_ref[ROW_LOSS:ROW_LOSS + 1, :] = acc2_ref[0:1, :]
        st = st_ref[0:ST_ROWS, :]
        for b in range(1, b_loc):
            st = st + st_ref[b * ST_ROWS:(b + 1) * ST_ROWS, :]
        out_ref[ROW_GN:ROW_GN + 1, 0:RET_W] = st[ST_GAIN:ST_GAIN + 1, :]
        lane = lax.broadcasted_iota(jnp.int32, (1, d), 1)
        misc = jnp.zeros((1, d), F32)
        for pr in range(pairs):
            blk = st[:, pr * LANES:(pr + 1) * LANES]
            half = lax.broadcasted_iota(jnp.int32, (1, LANES), 1) < HEAD_DIM
            for h in range(2):
                sel = half if h == 0 else jnp.logical_not(half)
                cross_f = jnp.sum(jnp.where(sel, blk[ST_XF:ST_XF + 1, :] + blk[ST_LF:ST_LF + 1, :], 0.0))
                cross_b = jnp.sum(jnp.where(sel, blk[ST_XB:ST_XB + 1, :] + blk[ST_LB:ST_LB + 1, :], 0.0))
                intra_f = jnp.sum(blk[ST_IFA + h:ST_IFA + h + 1, :])
                intra_b = jnp.sum(blk[ST_IBA + h:ST_IBA + h + 1, :])
                head = 2 * pr + h
                misc = jnp.where(lane == MISC_DF + head, cross_f + intra_f, misc)
                misc = jnp.where(lane == MISC_DB + head, cross_b + intra_b, misc)
        for g in range(KV_HEADS):
            tot = dsink_ref[g * 8:(g + 1) * 8, :]
            for b in range(1, b_loc):
                tot = tot + dsink_ref[(b * KV_HEADS + g) * 8:(b * KV_HEADS + g + 1) * 8, :]
            for h in range(GROUP):
                misc = jnp.where(lane == MISC_SINK + GROUP * g + h, jnp.sum(tot[h:h + 1, 0:1]), misc)
        out_ref[ROW_MISC:ROW_MISC + 1, :] = misc

    return pl.pallas_call(body, name="pack_small", out_shape=SDS((SMALL_ROWS, d), F32))(acc2, acc1, ret_stats, dsink)


BIG = ("w_in", "w_out", "w_ffn_gate", "w_ffn_up", "w_ffn_down", "w_ple_proj", "w_ple_gate")
TRANSPOSED_OUTSIDE = ("w_in", "w_ffn_gate", "w_ffn_up")
TRANSPOSED_HERE = ("w_ple_proj",)
SMALL = ("ret_decay_fwd", "ret_decay_bwd", "ret_gn_gain", "attn_sink", "ln1_gain", "ln1_bias", "ln2_gain", "ln2_bias")
ORDER = ("w_in", "ret_decay_fwd", "ret_decay_bwd", "ret_gn_gain", "attn_sink", "w_out", "ln1_gain", "ln1_bias",
         "w_ffn_gate", "w_ffn_up", "w_ffn_down", "w_ple_proj", "w_ple_gate", "ln2_gain", "ln2_bias")


GATHER_ORDER = ("w_in", "w_ffn_up", "w_out", "w_ffn_gate", "w_ple_gate", "w_ple_proj", "w_ffn_down")
GATHER_TWO_LEVEL = ("w_in", "w_ffn_up", "w_out")

def _local_step(x2, p2, target2, fetch, publish, small, b_loc, me):
    d = x2.shape[1]
    lgf, lgb = _log_decay(small["ret_decay_fwd"], small["ret_decay_bwd"])
    lgf1, lgb1, sink1 = lgf.reshape(-1), lgb.reshape(-1), small["attn_sink"].reshape(-1)
    (w_in,) = fetch(("w_in",), ())
    u, xb = _in_proj(x2, w_in)
    passed = fetch.pass_on(("w_ffn_up", "w_out"), (xb, lgf))
    r, ret_xhat, ret_rstd, a, ret_states = _mixers_fwd(u, lgf1, lgb1, small["ret_gn_gain"], sink1, b_loc, passed)
    w_out, w_gate, w_up = fetch(("w_out", "w_ffn_gate", "w_ffn_up"), (r, a))
    xhat1, rstd1, h1b, dact_dg, dact_du, act = _mix_ln1_ffn_up(
        r, a, x2, w_out, w_gate, w_up, small["ln1_gain"], small["ln1_bias"])
    w_pg, w_pe, w_down = fetch(("w_ple_gate", "w_ple_proj", "w_ffn_down"), (act,))
    own = {}
    dz2, dz2b, dsb, dg, dup, acc2, pe_whole, own["w_ple_proj"], pg_whole, own["w_ple_gate"] = (
        _ffn_down_ln2_loss(me, act, dact_dg, dact_du, h1b, p2, xhat1, target2, w_down, w_pg, w_pe,
                           small["ln1_gain"], small["ln1_bias"], small["ln2_gain"], small["ln2_bias"]))

    def grad(name, parts, rhs, after=()):
        whole, own[name] = _weight_grad("grad_" + name, me, parts, rhs, after)
        return whole

    ffn_jobs = dict(w_ffn_down=(act, dz2b), w_ffn_gate=(dg, h1b), w_ffn_up=(dup, h1b))
    wholes, owns = _weight_grad_jobs("grad_w_ffn", me, list(ffn_jobs.values()))
    own.update(zip(ffn_jobs, owns))
    t2 = publish("ffn", dict(zip(ffn_jobs, wholes), w_ple_proj=pe_whole, w_ple_gate=pg_whole))
    dz1, dz1b, dr, da, acc1, wo_whole, own["w_out"] = _dh1_ln1_bwd(
        me, dz2, dg, dup, dsb, xhat1, rstd1, r, a, w_gate, w_up, w_pg, w_out, small["ln1_gain"], t2)
    t3 = publish("out", dict(w_out=wo_whole))
    dq, dk, dv, dgate, ret_stats, daq, dak, dav, dsink = _mixers_bwd(
        u, ret_xhat, ret_rstd, ret_states, dr, da, lgf1, lgb1, small["ret_gn_gain"], sink1, b_loc, t3)
    parts = [dq, dk, dv, dgate, daq, dak, dav]
    small_part = _pack_small(acc2, acc1, ret_stats, dsink, b_loc, d)
    t4 = publish("in", dict(w_in=grad("w_in", parts, xb)), small_part)
    grad_x = _in_proj_bwd(dz1, parts, w_in, t4)
    return grad_x, own, small_part


def kernel(x, p, w_in, ret_decay_fwd, ret_decay_bwd, ret_gn_gain, attn_sink, w_out, ln1_gain, ln1_bias, w_ffn_gate, w_ffn_up, w_ffn_down, w_ple_proj, w_ple_gate, ln2_gain, ln2_bias, loss_target, m_w_in, m_ret_decay_fwd, m_ret_decay_bwd, m_ret_gn_gain, m_attn_sink, m_w_out, m_ln1_gain, m_ln1_bias, m_w_ffn_gate, m_w_ffn_up, m_w_ffn_down, m_w_ple_proj, m_w_ple_gate, m_ln2_gain, m_ln2_bias, v_w_in, v_ret_decay_fwd, v_ret_decay_bwd, v_ret_gn_gain, v_attn_sink, v_w_out, v_ln1_gain, v_ln1_bias, v_w_ffn_gate, v_w_ffn_up, v_w_ffn_down, v_w_ple_proj, v_w_ple_gate, v_ln2_gain, v_ln2_bias):
    given = dict(locals())

    def strip(n, a):
        if n not in BIG:
            return a
        return a[0].T if n in TRANSPOSED_OUTSIDE else a[0]

    def restore(n, a):
        if n not in BIG:
            return a
        return (a.T if n in TRANSPOSED_OUTSIDE else a)[None]

    w = {n: strip(n, given[n]) for n in ORDER}
    m = {n: strip(n, given["m_" + n]) for n in ORDER}
    v = {n: strip(n, given["v_" + n]) for n in ORDER}
    b_loc, s, d = x.shape
    x2 = x.reshape(b_loc * s, d)
    p2 = p[0].reshape(b_loc * s, p.shape[-1])
    target2 = loss_target.reshape(b_loc * s, d)

    small = {n: w[n] for n in SMALL}
    me = (4 * lax.axis_index("x") + 2 * lax.axis_index("y") + lax.axis_index("c")).astype(jnp.int32).reshape(1)

    gather = _gather_start(
        {n: w[n] for n in GATHER_ORDER},
        [_gather_copy_near if n in GATHER_TWO_LEVEL else _gather_copy for n in GATHER_ORDER])

    passing = {}

    def pass_on(names, after):
        relayed = _gather_relay("gather_relay_" + names[0], gather, [GATHER_ORDER.index(n) for n in names], list(after))
        passing.update({n: (relayed, j) for j, n in enumerate(names)})
        return (relayed["token"],)

    def fetch(names, after):
        out = {}
        for n in [n for n in names if n in GATHER_TWO_LEVEL]:
            if n not in passing:
                pass_on((n,), after)
            relayed, j = passing[n]
            out[n] = _split_copy_wait("gather_wait_" + n, relayed, [j], list(after))[0][0]
        direct = [n for n in names if n not in GATHER_TWO_LEVEL]
        if direct:
            got = _split_copy_wait("gather_wait_" + direct[0], gather, [GATHER_ORDER.index(n) for n in direct],
                                   list(after))
            out.update({n: item[0] for n, item in zip(direct, got)})
        return [out[n] for n in names]

    scatters = []

    def publish(tag, products, small_sums=None):
        items = [(products[n], lax.empty((N_DEV - 1, products[n].shape[0] // N_DEV, products[n].shape[1]), BF16))
                 for n in products]
        copies = [_scatter_copy] * len(items)
        if small_sums is not None:
            items.append((small_sums, lax.empty((N_DEV - 1,) + small_sums.shape, F32)))
            copies.append(_small_copy)
        started = _split_copy_start("scatter_start_" + tag, items, copies)
        scatters.append((list(products), small_sums is not None, started))
        return (started["token"],)

    fetch.pass_on = pass_on
    grad_x, own, small_part = _local_step(x2, p2, target2, fetch, publish, small, b_loc, me)

    out_g, out_d, out_m, out_v = {}, {}, {}, {}
    after = [grad_x]
    for names, with_small, started in scatters:
        landed = _split_copy_wait("scatter_wait_" + names[0], started, list(range(len(started["items"]))), after)
        if with_small:
            mine, from_peers = landed[-1]
            loss, sg, sd, sm, sv = _small_adamw(
                me, mine, from_peers, small, {n: m[n] for n in SMALL}, {n: v[n] for n in SMALL})
            for dst, src in ((out_g, sg), (out_d, sd), (out_m, sm), (out_v, sv)):
                dst.update(src)
        recv = {n: item[1] for n, item in zip(names, landed)}
        alike = {}
        for n in names:
            alike.setdefault((own[n].shape, n in TRANSPOSED_HERE), []).append(n)
        for (_, transposed), ns in alike.items():
            res = _reduce_adamw(ns[0], [own[n] for n in ns], [recv[n] for n in ns], [w[n] for n in ns],
                                [m[n] for n in ns], [v[n] for n in ns], transposed)
            for dst, vals in zip((out_g, out_d, out_m, out_v), res):
                dst.update(zip(ns, vals))
        after = [out_v[names[-1]]]

    outs = [loss[0, 0], grad_x.reshape(x.shape)]
    for group in (out_g, out_d, out_m, out_v):
        outs += [restore(n, group[n]) for n in ORDER]
    return tuple(outs)
```

```python
import functools

import jax
import jax.numpy as jnp
from jax import lax
from jax.experimental import pallas as pl
from jax.experimental.pallas import tpu as pltpu

F32, BF16 = jnp.float32, jnp.bfloat16
SDS = jax.ShapeDtypeStruct
MESH = pl.DeviceIdType.MESH

N_DEV = 8
HEAD_DIM = 64
RET_HEADS = 8
ATTN_HEADS = 8
KV_HEADS = 2
GROUP = ATTN_HEADS // KV_HEADS
RET_W = RET_HEADS * HEAD_DIM
ATT_W = ATTN_HEADS * HEAD_DIM
KV_W = KV_HEADS * HEAD_DIM
LANES = 128
CHUNK = 128
BLOCK = 128
Q_SCALE = HEAD_DIM ** -0.5
ALPHA = 2.0 ** 0.25
LN_EPS = 1e-5
GN_EPS = 1e-5
NEG_INF = -1e30
C_RQ, C_RK, C_RV, C_RG = 0, RET_W, 2 * RET_W, 3 * RET_W
C_AQ = 4 * RET_W
C_AK = C_AQ + ATT_W
C_AV = C_AK + KV_W
IN_W = C_AV + KV_W

ADAM_LR = 0.001
ADAM_B1 = 0.9
ADAM_B2 = 0.999
ADAM_EPS = 1e-08
ADAM_WD = 0.01
ADAM_STEP = 10

VMEM_LIMIT = 56 * 1024 * 1024
DH1_VMEM_LIMIT = 61 * 1024 * 1024
MATMUL_ROWS = 512
EPILOGUE_ROWS = 256
SUB_ROWS = 512
SMALL_ROWS = 16
ROW_LN1G, ROW_LN1B, ROW_LN2G, ROW_LN2B, ROW_LOSS, ROW_GN, ROW_MISC = 0, 1, 2, 3, 4, 5, 6
MISC_DF, MISC_DB, MISC_SINK = 0, 8, 16


def _dot_nn(a, b):
    return lax.dot_general(a, b, (((1,), (0,)), ((), ())), preferred_element_type=F32)


def _dot_nt(a, b):
    return lax.dot_general(a, b, (((1,), (1,)), ((), ())), preferred_element_type=F32)


def _dot_tn(a, b):
    return lax.dot_general(a, b, (((0,), (0,)), ((), ())), preferred_element_type=F32)


def _params(sem=None, vmem=VMEM_LIMIT):
    kw = {"vmem_limit_bytes": vmem}
    if sem is not None:
        kw["dimension_semantics"] = sem
    return pltpu.CompilerParams(**kw)


def _row_tile(t, want=512):
    tm = want
    while t % tm:
        tm //= 2
    return tm


def _sigmoid(x):
    return jax.nn.sigmoid(x)


def _layer_norm_stats(z):
    mu = jnp.mean(z, axis=1, keepdims=True)
    d = z - mu
    var = jnp.mean(d * d, axis=1, keepdims=True)
    rstd = lax.rsqrt(var + LN_EPS)
    return d * rstd, rstd


def _layer_norm_bwd(dxh, xhat, rstd):
    m1 = jnp.mean(dxh, axis=1, keepdims=True)
    m2 = jnp.mean(dxh * xhat, axis=1, keepdims=True)
    return rstd * (dxh - m1 - xhat * m2)


def _mesh_pos():
    return lax.axis_index("x"), lax.axis_index("y"), lax.axis_index("c")


HBM_SPEC = pl.BlockSpec(memory_space=pltpu.HBM)
SEM_SPEC = pl.BlockSpec(memory_space=pltpu.SEMAPHORE)
ANY_SPEC = pl.BlockSpec(memory_space=pl.ANY)
SIDE_EFFECT = pltpu.SideEffectType.DATAFLOW_SIDE_EFFECTING
PEER_SEMS = pltpu.SemaphoreType.DMA((N_DEV - 1,))


def _in_hbm(a):
    return pltpu.with_memory_space_constraint(a, pltpu.HBM)


def _split_copy_start(name, items, copies):
    n = len(items)
    flat = [a for it in items for a in it]
    k = len(flat)

    def body(*refs):
        arr, sems = list(refs[:k]), refs[k:k + 2 * n]
        for i, it in enumerate(items):
            mine = [arr.pop(0) for _ in it]
            for m in range(1, N_DEV):
                cp = copies[i](m, mine, sems[i].at[m - 1], sems[n + i].at[m - 1])
                if cp is not None:
                    cp.start()
        token = refs[-1]
        token[...] = jnp.zeros_like(token)

    res = pl.pallas_call(
        body, name=name,
        out_shape=[PEER_SEMS] * (2 * n) + [pltpu.HBM(a.shape, a.dtype) for a in flat] + [SDS((8, LANES), F32)],
        in_specs=[HBM_SPEC] * k,
        out_specs=[SEM_SPEC] * (2 * n) + [HBM_SPEC] * k + [pl.BlockSpec(memory_space=pltpu.VMEM)],
        input_output_aliases={j: 2 * n + j for j in range(k)},
        compiler_params=pltpu.CompilerParams(has_side_effects=SIDE_EFFECT),
    )(*[_in_hbm(a) for a in flat])
    thru, out_items = list(res[2 * n:2 * n + k]), []
    for it in items:
        out_items.append(tuple(thru.pop(0) for _ in it))
    return dict(send=res[:n], recv=res[n:2 * n], items=out_items, token=res[-1], copies=copies)


def _gather_start(shards, copies):
    names = list(shards)
    n = len(names)
    flip = [name in TRANSPOSED_HERE for name in names]
    shapes = [shards[name].shape[::-1] if f else shards[name].shape for name, f in zip(names, flip)]
    most = (max(s[0] for s in shapes), max(s[1] for s in shapes))

    def body(*refs):
        src, sems, land, token = refs[:n], refs[n:3 * n], refs[3 * n:4 * n], refs[4 * n]
        wide, narrow, sem = refs[4 * n + 1:]
        for i, (rows, cols) in enumerate(shapes):
            raw = wide.at[0:cols, 0:rows] if flip[i] else wide.at[0:rows, 0:cols]
            bring = pltpu.make_async_copy(src[i], raw, sem.at[0])
            bring.start()
            bring.wait()
            narrow[0:rows, 0:cols] = (raw[...].T if flip[i] else raw[...]).astype(BF16)
            mine = land[i].at[pl.ds(pl.multiple_of(_peer_index(0) * rows, 8), rows), :]
            place = pltpu.make_async_copy(narrow.at[0:rows, 0:cols], mine, sem.at[0])
            place.start()
            place.wait()
            for m in range(1, N_DEV):
                cp = copies[i](m, [land[i]], sems[i].at[m - 1], sems[n + i].at[m - 1])
                if cp is not None:
                    cp.start()
        token[...] = jnp.zeros_like(token)

    side = max(most)
    res = pl.pallas_call(
        body, name="gather_start",
        out_shape=[PEER_SEMS] * (2 * n) + [pltpu.HBM((N_DEV * r, c), BF16) for r, c in shapes] + [SDS((8, LANES), F32)],
        in_specs=[HBM_SPEC] * n,
        out_specs=[SEM_SPEC] * (2 * n) + [HBM_SPEC] * n + [pl.BlockSpec(memory_space=pltpu.VMEM)],
        scratch_shapes=[pltpu.VMEM((side, side), F32), pltpu.VMEM(most, BF16), pltpu.SemaphoreType.DMA((1,))],
        compiler_params=pltpu.CompilerParams(has_side_effects=SIDE_EFFECT),
    )(*[_in_hbm(shards[name]) for name in names])
    return dict(send=res[:n], recv=res[n:2 * n], items=[(a,) for a in res[2 * n:3 * n]], token=res[-1], copies=copies)


def _gather_relay(name, started, which, after):
    lands = [started["items"][w][0] for w in which]
    k = len(lands)

    def body(*refs):
        land_refs, old_send, old_recv = refs[:k], refs[k:2 * k], refs[2 * k:3 * k]
        outs = refs[3 * k + len(after):]
        send, recv, token = outs[:k], outs[k:2 * k], outs[-1]
        for j in range(k):
            for m in range(1, N_DEV):
                cp = _gather_copy_near(m, [land_refs[j]], old_send[j].at[m - 1], old_recv[j].at[m - 1])
                if cp is None:
                    continue
                cp.wait_send()
                cp.wait_recv()
                if m > 1:
                    _gather_copy_pass(m + 1, [land_refs[j]], send[j].at[m], recv[j].at[m]).start()
        token[...] = jnp.zeros_like(token)

    res = pl.pallas_call(
        body, name=name,
        out_shape=[PEER_SEMS] * (2 * k) + [pltpu.HBM(a.shape, a.dtype) for a in lands] + [SDS((8, LANES), F32)],
        in_specs=[HBM_SPEC] * k + [SEM_SPEC] * (2 * k) + [ANY_SPEC] * len(after),
        out_specs=[SEM_SPEC] * (2 * k) + [HBM_SPEC] * k + [pl.BlockSpec(memory_space=pltpu.VMEM)],
        input_output_aliases={j: 2 * k + j for j in range(k)},
        compiler_params=pltpu.CompilerParams(has_side_effects=SIDE_EFFECT),
    )(*lands, *[started["send"][w] for w in which], *[started["recv"][w] for w in which],
      *[_in_hbm(a) for a in after])
    return dict(send=res[:k], recv=res[k:2 * k], items=[(a,) for a in res[2 * k:3 * k]], token=res[-1],
                copies=[_gather_copy_pass] * k)


def _split_copy_wait(name, started, which, after):
    items = [started["items"][i] for i in which]
    copies = [started["copies"][i] for i in which]
    n = len(items)
    flat = [a for it in items for a in it]
    k = len(flat)

    def body(*refs):
        arr, sems = list(refs[:k]), refs[k:k + 2 * n]
        for i, it in enumerate(items):
            mine = [arr.pop(0) for _ in it]
            for m in range(1, N_DEV):
                cp = copies[i](m, mine, sems[i].at[m - 1], sems[n + i].at[m - 1])
                if cp is not None:
                    cp.wait_send()
                    cp.wait_recv()

    res = pl.pallas_call(
        body, name=name,
        out_shape=[pltpu.HBM(a.shape, a.dtype) for a in flat],
        in_specs=[HBM_SPEC] * k + [SEM_SPEC] * (2 * n) + [ANY_SPEC] * len(after),
        out_specs=[HBM_SPEC] * k,
        input_output_aliases={j: j for j in range(k)},
        compiler_params=pltpu.CompilerParams(has_side_effects=SIDE_EFFECT),
    )(*flat, *[started["send"][i] for i in which], *[started["recv"][i] for i in which], *[_in_hbm(a) for a in after])
    thru, out_items = list(res), []
    for it in items:
        out_items.append(tuple(thru.pop(0) for _ in it))
    return out_items


def _gather_copy(m, refs, send_sem, recv_sem):
    (land_ref,) = refs
    r = land_ref.shape[0] // N_DEV
    mine = land_ref.at[pl.ds(pl.multiple_of(_peer_index(0) * r, 8), r), :]
    return pltpu.make_async_remote_copy(src_ref=mine, dst_ref=mine, send_sem=send_sem, recv_sem=recv_sem,
                                        device_id=_peer(m), device_id_type=MESH)


def _gather_copy_near(m, refs, send_sem, recv_sem):
    return _gather_copy(m, refs, send_sem, recv_sem) if m == 1 or m % 2 == 0 else None


def _gather_copy_pass(m, refs, send_sem, recv_sem):
    if m == 1 or m % 2 == 0:
        return None
    (land_ref,) = refs
    r = land_ref.shape[0] // N_DEV
    block = land_ref.at[pl.ds(pl.multiple_of(_peer_index(m ^ 1) * r, 8), r), :]
    return pltpu.make_async_remote_copy(src_ref=block, dst_ref=block, send_sem=send_sem, recv_sem=recv_sem,
                                        device_id=_peer(1), device_id_type=MESH)


def _small_copy(m, refs, send_sem, recv_sem):
    part_ref, land_ref = refs
    return pltpu.make_async_remote_copy(src_ref=part_ref, dst_ref=land_ref.at[m - 1], send_sem=send_sem,
                                        recv_sem=recv_sem, device_id=_peer(m), device_id_type=MESH)


def _scatter_copy(m, refs, send_sem, recv_sem):
    buf_ref, land_ref = refs
    r = buf_ref.shape[0] // N_DEV
    src = buf_ref.at[pl.ds(pl.multiple_of(_peer_index(m) * r, 8), r), :]
    return pltpu.make_async_remote_copy(src_ref=src, dst_ref=land_ref.at[m - 1], send_sem=send_sem,
                                        recv_sem=recv_sem, device_id=_peer(m), device_id_type=MESH)


def _peer(m):
    x, y, c = _mesh_pos()
    bx, by, bc = (m >> 2) & 1, (m >> 1) & 1, m & 1
    return (x ^ bx if bx else x, y ^ by if by else y, c ^ bc if bc else c)


def _peer_index(m):
    x, y, c = _mesh_pos()
    return (4 * x + 2 * y + c) ^ m


SMALL_PLACE = {
    "ln1_gain": (ROW_LN1G, 0), "ln1_bias": (ROW_LN1B, 0), "ln2_gain": (ROW_LN2G, 0), "ln2_bias": (ROW_LN2B, 0),
    "ret_gn_gain": (ROW_GN, 0), "ret_decay_fwd": (ROW_MISC, MISC_DF), "ret_decay_bwd": (ROW_MISC, MISC_DB),
    "attn_sink": (ROW_MISC, MISC_SINK)}


def _small_adamw(me, part, landed, w, m, v):
    d = part.shape[1]
    names = list(SMALL_PLACE)
    k = len(names)

    def body(*refs):
        me_ref, part_ref, land_ref = refs[:3]
        refs = refs[2:]
        w_refs, m_refs, v_refs = refs[1:1 + k], refs[1 + k:1 + 2 * k], refs[1 + 2 * k:1 + 3 * k]
        outs = refs[1 + 3 * k:1 + 7 * k + 1]
        tot_ref = refs[-1]
        loss_ref, g_refs, dl_refs = outs[0], outs[1:1 + k], outs[1 + k:1 + 2 * k]
        nm_refs, nv_refs = outs[1 + 2 * k:1 + 3 * k], outs[1 + 3 * k:1 + 4 * k]
        tot = jnp.zeros(part_ref.shape, F32)
        for dev in range(N_DEV):
            j = dev ^ me_ref[0]
            tot = tot + jnp.where(j == 0, part_ref[...], land_ref[jnp.maximum(j, 1) - 1])
        tot_ref[...] = tot
        loss_ref[...] = (0.5 / d) * jnp.sum(tot_ref[ROW_LOSS:ROW_LOSS + 1, :], axis=1, keepdims=True)
        for i, name in enumerate(names):
            row, lo = SMALL_PLACE[name]
            wv = w_refs[i][...]
            g = tot_ref[row:row + 1, lo:lo + wv.shape[1]]
            if name.startswith("ret_decay"):
                p2 = jnp.exp2(wv)
                g = g * (-p2 * jnp.log(2.0) / (1.0 - p2))
            g_refs[i][...] = g
            _adamw_store(g, wv, m_refs[i][...], v_refs[i][...], dl_refs[i], nm_refs[i], nv_refs[i])

    shapes = [SDS(w[n].shape, F32) for n in names]
    vm = pl.BlockSpec(memory_space=pltpu.VMEM)
    res = pl.pallas_call(
        body, name="small_adamw", out_shape=[SDS((1, 1), F32)] + shapes * 4,
        in_specs=[_smem_spec()] + [vm] * (2 + 3 * k), out_specs=[vm] * (1 + 4 * k),
        scratch_shapes=[pltpu.VMEM(part.shape, F32)],
    )(me, part, landed, *[w[n] for n in names], *[m[n] for n in names], *[v[n] for n in names])
    groups = [dict(zip(names, res[1 + j * k:1 + (j + 1) * k])) for j in range(4)]
    return (res[0], *groups)


def _adamw_store(g, w, m, v, dl_ref, nm_ref, nv_ref):
    m = ADAM_B1 * m + (1.0 - ADAM_B1) * g
    v = ADAM_B2 * v + (1.0 - ADAM_B2) * (g * g)
    m_hat = m / (1.0 - ADAM_B1 ** ADAM_STEP)
    v_hat = v / (1.0 - ADAM_B2 ** ADAM_STEP)
    dl_ref[...] = -ADAM_LR * (m_hat / (jnp.sqrt(v_hat) + ADAM_EPS) + ADAM_WD * w)
    nm_ref[...] = m
    nv_ref[...] = v


def _reduce_adamw(name, owns, recvs, ws, ms, vs, transposed):
    count = len(owns)
    rows, n = owns[0].shape
    steps = 1 if transposed or rows % 16 else 2
    rb = rows // steps

    def body(*refs):
        ins, outs = refs[:5 * count], refs[5 * count:]
        j = pl.program_id(0)
        for k in range(count):
            @pl.when(j == k)
            def _(k=k):
                own_ref, recv_ref, w_ref, m_ref, v_ref = ins[5 * k:5 * k + 5]
                g_ref, dl_ref, nm_ref, nv_ref = outs[4 * k:4 * k + 4]
                g = own_ref[...]
                for p in range(recv_ref.shape[0]):
                    g = g + recv_ref[p].astype(F32)
                if transposed:
                    g = g.T
                g_ref[...] = g
                _adamw_store(g, w_ref[...], m_ref[...], v_ref[...], dl_ref, nm_ref, nv_ref)

    def turn(k):
        return lambda j, i: jnp.where(j == k, i, jnp.where(j < k, 0, steps - 1))

    in_specs, out_specs = [], []
    for k in range(count):
        at = turn(k)
        blk = pl.BlockSpec(ws[0].shape if transposed else (rb, n), lambda j, i, at=at: (at(j, i), 0))
        in_specs += [pl.BlockSpec((rb, n), lambda j, i, at=at: (at(j, i), 0)),
                     pl.BlockSpec((recvs[k].shape[0], rb, n), lambda j, i, at=at: (0, at(j, i), 0)), blk, blk, blk]
        out_specs += [blk] * 4
    res = pl.pallas_call(
        body, name="adamw_" + name, grid=(count, steps), in_specs=in_specs, out_specs=out_specs,
        out_shape=[SDS(ws[0].shape, F32)] * (4 * count), compiler_params=_params(("arbitrary", "arbitrary")),
    )(*[a for k in range(count) for a in (owns[k], recvs[k], ws[k], ms[k], vs[k])])
    return [list(res[j::4]) for j in range(4)]


def _row_spec(tm, width):
    return pl.BlockSpec((tm, width), lambda i: (i, 0))


def _full_spec(shape):
    return pl.BlockSpec(shape, lambda i: (0,) * len(shape))


_acc_spec = _full_spec


def _sub_rows(tm):
    step = min(SUB_ROWS, tm)
    return [(lo, lo + step) for lo in range(0, tm, step)]


def _in_proj(x2, wt_in):
    t, d = x2.shape
    u_w = wt_in.shape[0]
    tm = _row_tile(t, MATMUL_ROWS)

    def body(x_ref, w_ref, u_ref, xb_ref):
        xb = x_ref[...].astype(BF16)
        xb_ref[...] = xb
        u_ref[...] = _dot_nt(xb, w_ref[...]).astype(BF16)

    return pl.pallas_call(
        body, name="in_proj", grid=(t // tm,),
        in_specs=[_row_spec(tm, d), _full_spec(wt_in.shape)],
        out_specs=[_row_spec(tm, u_w), _row_spec(tm, d)],
        out_shape=[SDS((t, u_w), BF16), SDS((t, d), BF16)],
        compiler_params=_params(("parallel",)),
    )(x2, wt_in)


def _col_halves(f):
    n = f // LANES
    k = (n + 1) // 2 * LANES
    return [(0, k), (k, f)] if k < f else [(0, f)]


def _mix_ln1_ffn_up(r, a, x2, w_out, wt_gate, wt_up, g1, b1):
    t, d = x2.shape
    f = wt_gate.shape[0]
    tm = _row_tile(t, EPILOGUE_ROWS)

    def body(r_ref, a_ref, x_ref, wo_ref, wg_ref, wu_ref, g_ref, b_ref, xh_ref, rs_ref, hb_ref, dg_ref, du_ref,
             act_ref):
        mix = _dot_nn(r_ref[...], wo_ref[0:RET_W, :]) + _dot_nn(a_ref[...], wo_ref[RET_W:RET_W + ATT_W, :])
        z = ALPHA * x_ref[...] + mix
        xhat, rstd = _layer_norm_stats(z)
        xh_ref[...] = xhat
        rs_ref[...] = jnp.broadcast_to(rstd, rs_ref.shape)
        h = (xhat * g_ref[...] + b_ref[...]).astype(BF16)
        hb_ref[...] = h
        g = _dot_nt(h, wg_ref[...])
        u = _dot_nt(h, wu_ref[...])
        sg = _sigmoid(g)
        silu = g * sg
        dg_ref[...] = (u * (sg * (1.0 + g * (1.0 - sg)))).astype(BF16)
        du_ref[...] = silu.astype(BF16)
        act_ref[...] = (silu * u).astype(BF16)

    wide, narrow = _row_spec(tm, f), _row_spec(tm, d)
    return pl.pallas_call(
        body, name="mix_ln1_ffn_up", grid=(t // tm,),
        in_specs=[_row_spec(tm, RET_W), _row_spec(tm, ATT_W), narrow, _resident_spec(w_out.shape),
                  _resident_spec(wt_gate.shape), _resident_spec(wt_up.shape), _full_spec(g1.shape),
                  _full_spec(b1.shape)],
        out_specs=[narrow, _row_spec(tm, LANES), narrow, wide, wide, wide],
        out_shape=[SDS((t, d), F32), SDS((t, LANES), F32), SDS((t, d), BF16)] + [SDS((t, f), BF16)] * 3,
        compiler_params=_params(("parallel",)),
    )(r, a, x2, w_out, wt_gate, wt_up, g1, b1)


def _ffn_down_ln2_loss(me, act, dact_dg, dact_du, h1b, p2, xhat1, target, w_down, w_pg, wt_pe, g1, b1, g2, b2):
    t, d = xhat1.shape
    f = act.shape[1]
    pdim = p2.shape[1]
    tm = _row_tile(t, EPILOGUE_ROWS)
    n_steps = t // tm
    own_rows = d // N_DEV

    def body(me_ref, act_ref, fg_ref, fu_ref, hb_ref, p_ref, xh1_ref, tgt_ref, wd_ref, wpg_ref, wpe_ref, g1_ref,
             b1_ref, g2_ref, b2_ref, dz_ref, dzb_ref, ds_ref, dg_ref, du_ref, acc_ref,
             pe_whole, pe_own, pg_whole, pg_own, pe_acc, pg_acc):
        @pl.when(pl.program_id(0) == 0)
        def _():
            acc_ref[...] = jnp.zeros_like(acc_ref)
            pe_acc[...] = jnp.zeros_like(pe_acc)
            pg_acc[...] = jnp.zeros_like(pg_acc)

        for lo, hi in _sub_rows(tm):
            h1 = xh1_ref[lo:hi, :] * g1_ref[...] + b1_ref[...]
            pb = p_ref[lo:hi, :].astype(BF16)
            pg = _sigmoid(_dot_nn(hb_ref[lo:hi, :], wpg_ref[...]))
            ple = _dot_nt(pb, wpe_ref[...])
            gated = pg * ple
            dgate = gated * (1.0 - pg)
            ffn = _dot_nn(act_ref[lo:hi, :], wd_ref[...])
            z2 = ALPHA * h1 + gated + ffn
            xhat2, rstd2 = _layer_norm_stats(z2)
            err = xhat2 * g2_ref[...] + b2_ref[...] - tgt_ref[lo:hi, :]
            dy = err * (1.0 / d)
            dz = _layer_norm_bwd(dy * g2_ref[...], xhat2, rstd2)
            dzb = dz.astype(BF16)
            dz_ref[lo:hi, :] = dz
            dzb_ref[lo:hi, :] = dzb
            dsb, dpleb = (dz * dgate).astype(BF16), (dz * pg).astype(BF16)
            ds_ref[lo:hi, :] = dsb
            pe_acc[...] += _dot_tn(dpleb, pb)
            pg_acc[...] += _dot_tn(hb_ref[lo:hi, :], dsb)
            acc_ref[0:1, :] += jnp.sum(err * err, axis=0, keepdims=True)
            acc_ref[1:2, :] += jnp.sum(dy * xhat2, axis=0, keepdims=True)
            acc_ref[2:3, :] += jnp.sum(dy, axis=0, keepdims=True)
            for c0, c1 in _col_halves(f):
                da = _dot_nt(dzb, wd_ref[c0:c1, :])
                dg_ref[lo:hi, c0:c1] = (da * fg_ref[lo:hi, c0:c1].astype(F32)).astype(BF16)
                du_ref[lo:hi, c0:c1] = (da * fu_ref[lo:hi, c0:c1].astype(F32)).astype(BF16)

        @pl.when(pl.program_id(0) == n_steps - 1)
        def _():
            mine = pl.ds(pl.multiple_of(me_ref[0] * own_rows, 8), own_rows)
            for whole, own, acc in ((pe_whole, pe_own, pe_acc), (pg_whole, pg_own, pg_acc)):
                whole[...] = acc[...].astype(BF16)
                own[...] = acc[mine, :]

    vec = _full_spec(g1.shape)
    wide, narrow = _row_spec(tm, f), _row_spec(tm, d)
    products = [(d, pdim), (own_rows, pdim), (d, d), (own_rows, d)]
    return pl.pallas_call(
        body, name="ffn_down_ln2_loss", grid=(n_steps,),
        in_specs=[_smem_spec(), wide, wide, wide, narrow, _row_spec(tm, pdim), narrow, narrow,
                  _full_spec(w_down.shape), _full_spec(w_pg.shape), _full_spec(wt_pe.shape), vec, vec, vec, vec],
        out_specs=[narrow] * 3 + [wide, wide, _acc_spec((8, d))] + [_full_spec(s) for s in products],
        out_shape=[SDS((t, d), F32), SDS((t, d), BF16), SDS((t, d), BF16),
                   SDS((t, f), BF16), SDS((t, f), BF16), SDS((8, d), F32)]
        + [SDS(s, BF16 if k % 2 == 0 else F32) for k, s in enumerate(products)],
        scratch_shapes=[pltpu.VMEM((d, pdim), F32), pltpu.VMEM((d, d), F32)],
        compiler_params=_params(("arbitrary",)),
    )(me, act, dact_dg, dact_du, h1b, p2, xhat1, target, w_down, w_pg, wt_pe, g1, b1, g2, b2)


def _after(after, body):
    k = len(after)
    return (lambda *refs: body(*refs[k:])), [ANY_SPEC] * k


def _resident_spec(shape):
    return pl.BlockSpec(shape, lambda i: (0,) * len(shape), pipeline_mode=pl.Buffered(1))


def _dh1_ln1_bwd(me, dz2, dg, dup, dsb, xhat1, rstd1, r, a, wt_gate, wt_up, w_pg, w_out, g1, after=()):
    t, d = dz2.shape
    f = dg.shape[1]
    tm = _row_tile(t, MATMUL_ROWS)
    n_steps = t // tm
    wo_rows = RET_W + ATT_W
    own_rows = wo_rows // N_DEV

    def body(me_ref, dz_ref, dg_ref, du_ref, ds_ref, xh1_ref, rs1_ref, r_ref, a_ref, wg_ref, wu_ref, wpg_ref,
             wo_ref, g1_ref, dz1_ref, dz1b_ref, dr_ref, da_ref, acc_ref, wo_whole, wo_own, wo_acc):
        @pl.when(pl.program_id(0) == 0)
        def _():
            acc_ref[...] = jnp.zeros_like(acc_ref)
            wo_acc[...] = jnp.zeros_like(wo_acc)

        for lo, hi in _sub_rows(tm):
            dh = (ALPHA * dz_ref[lo:hi, :] + _dot_nn(dg_ref[lo:hi, :], wg_ref[...])
                  + _dot_nn(du_ref[lo:hi, :], wu_ref[...]) + _dot_nt(ds_ref[lo:hi, :], wpg_ref[...]))
            xhat, rstd = xh1_ref[lo:hi, :], rs1_ref[lo:hi, 0:1]
            dz1 = _layer_norm_bwd(dh * g1_ref[...], xhat, rstd)
            dz1b = dz1.astype(BF16)
            dz1_ref[lo:hi, :] = dz1
            dz1b_ref[lo:hi, :] = dz1b
            acc_ref[0:1, :] += jnp.sum(dh * xhat, axis=0, keepdims=True)
            acc_ref[1:2, :] += jnp.sum(dh, axis=0, keepdims=True)
            dr_ref[lo:hi, :] = _dot_nt(dz1b, wo_ref[0:RET_W, :]).astype(BF16)
            da_ref[lo:hi, :] = _dot_nt(dz1b, wo_ref[RET_W:RET_W + ATT_W, :]).astype(BF16)
            wo_acc[0:RET_W, :] += _dot_tn(r_ref[lo:hi, :], dz1b)
            wo_acc[RET_W:wo_rows, :] += _dot_tn(a_ref[lo:hi, :], dz1b)

        @pl.when(pl.program_id(0) == n_steps - 1)
        def _():
            wo_whole[...] = wo_acc[...].astype(BF16)
            wo_own[...] = wo_acc[pl.ds(pl.multiple_of(me_ref[0] * own_rows, 8), own_rows), :]

    body, lead = _after(after, body)
    return pl.pallas_call(
        body, name="dh1_ln1_bwd", grid=(n_steps,),
        in_specs=lead + [_smem_spec(), _row_spec(tm, d), _row_spec(tm, f), _row_spec(tm, f), _row_spec(tm, d),
                         _row_spec(tm, d), _row_spec(tm, LANES), _row_spec(tm, RET_W), _row_spec(tm, ATT_W),
                         _resident_spec(wt_gate.shape), _resident_spec(wt_up.shape), _resident_spec(w_pg.shape),
                         _resident_spec(w_out.shape), _full_spec(g1.shape)],
        out_specs=[_row_spec(tm, d), _row_spec(tm, d), _row_spec(tm, RET_W), _row_spec(tm, ATT_W), _acc_spec((8, d)),
                   _resident_spec((wo_rows, d)), _resident_spec((own_rows, d))],
        out_shape=[SDS((t, d), F32), SDS((t, d), BF16), SDS((t, RET_W), BF16), SDS((t, ATT_W), BF16),
                   SDS((8, d), F32), SDS((wo_rows, d), BF16), SDS((own_rows, d), F32)],
        scratch_shapes=[pltpu.VMEM((wo_rows, d), F32)],
        compiler_params=_params(("arbitrary",), DH1_VMEM_LIMIT),
    )(*after, me, dz2, dg, dup, dsb, xhat1, rstd1, r, a, wt_gate, wt_up, w_pg, w_out, g1)


def _in_proj_bwd(dz1, parts, wt_in, after=()):
    t, d = dz1.shape
    tm = _row_tile(t, MATMUL_ROWS)
    widths = [p.shape[1] for p in parts]

    def body(*refs):
        dz_ref, part_refs, w_ref, dx_ref = refs[0], refs[1:1 + len(parts)], refs[-2], refs[-1]
        acc = ALPHA * dz_ref[...]
        lo = 0
        for p_ref, w in zip(part_refs, widths):
            acc = acc + _dot_nn(p_ref[...], w_ref[lo:lo + w, :])
            lo += w
        dx_ref[...] = acc

    body, lead = _after(after, body)
    return pl.pallas_call(
        body, name="in_proj_bwd", grid=(t // tm,),
        in_specs=lead + [_row_spec(tm, d)] + [_row_spec(tm, w) for w in widths] + [_full_spec(wt_in.shape)],
        out_specs=_row_spec(tm, d), out_shape=SDS((t, d), F32),
        compiler_params=_params(("parallel",)),
    )(*after, dz1, *parts, wt_in)


def _weight_grad(name, me, parts, rhs, after=()):
    t, n = rhs.shape
    widths = [p.shape[1] for p in parts]
    rows = sum(widths)
    own_rows = rows // N_DEV
    tk = _row_tile(t, MATMUL_ROWS)
    n_steps = t // tk
    step = 256

    def body(*refs):
        me_ref, part_refs, rhs_ref = refs[0], refs[1:1 + len(parts)], refs[1 + len(parts)]
        full_ref, own_ref, acc = refs[-3], refs[-2], refs[-1]
        i = pl.program_id(0)

        def products(first):
            b = rhs_ref[...].astype(BF16)
            lo = 0
            for p_ref, w in zip(part_refs, widths):
                for c0 in range(0, w, step):
                    c1 = min(c0 + step, w)
                    val = _dot_tn(p_ref[:, c0:c1].astype(BF16), b)
                    if first:
                        acc[lo + c0:lo + c1, :] = val
                    else:
                        acc[lo + c0:lo + c1, :] += val
                lo += w

        pl.when(i == 0)(functools.partial(products, True))
        pl.when(i > 0)(functools.partial(products, False))

        @pl.when(i == n_steps - 1)
        def _():
            full_ref[...] = acc[...].astype(BF16)
            own_ref[...] = acc[pl.ds(pl.multiple_of(me_ref[0] * own_rows, 8), own_rows), :]

    body, lead = _after(after, body)
    return pl.pallas_call(
        body, name=name, grid=(n_steps,),
        in_specs=lead + [_smem_spec()] + [_row_spec(tk, w) for w in widths] + [_row_spec(tk, n)],
        out_specs=[_full_spec((rows, n)), _full_spec((own_rows, n))],
        out_shape=[SDS((rows, n), BF16), SDS((own_rows, n), F32)],
        scratch_shapes=[pltpu.VMEM((rows, n), F32)],
        compiler_params=_params(("arbitrary",)),
    )(*after, me, *parts, rhs)


def _weight_grad_jobs(name, me, jobs, after=()):
    count = len(jobs)
    t = jobs[0][0].shape[0]
    tk = _row_tile(t, MATMUL_ROWS)
    n_steps = t // tk
    shapes = [(lhs.shape[1], rhs.shape[1]) for lhs, rhs in jobs]
    most_rows, most_cols = max(r for r, _ in shapes), max(n for _, n in shapes)
    step = 256

    def body(*refs):
        me_ref, lhs_refs, rhs_refs = refs[0], refs[1:1 + count], refs[1 + count:1 + 2 * count]
        full_refs, own_refs = refs[1 + 2 * count:1 + 3 * count], refs[1 + 3 * count:1 + 4 * count]
        acc, whole, mine, sems = refs[1 + 4 * count:]
        job, i = pl.program_id(0), pl.program_id(1)

        def leaving(j):
            rows, n = shapes[j]
            return (pltpu.make_async_copy(whole.at[0:rows, 0:n], full_refs[j], sems.at[0]),
                    pltpu.make_async_copy(mine.at[0:rows // N_DEV, 0:n], own_refs[j], sems.at[1]))

        def products(j, first):
            rows, n = shapes[j]
            b = rhs_refs[j][...].astype(BF16)
            for c0 in range(0, rows, step):
                c1 = min(c0 + step, rows)
                val = _dot_tn(lhs_refs[j][:, c0:c1].astype(BF16), b)
                if first:
                    acc[c0:c1, 0:n] = val
                else:
                    acc[c0:c1, 0:n] += val

        def finish(j):
            rows, n = shapes[j]
            own_rows = rows // N_DEV
            if j > 0:
                for cp in leaving(j - 1):
                    cp.wait()
            whole[0:rows, 0:n] = acc[0:rows, 0:n].astype(BF16)
            mine[0:own_rows, 0:n] = acc[pl.ds(pl.multiple_of(me_ref[0] * own_rows, 8), own_rows), 0:n]
            for cp in leaving(j):
                cp.start()
            if j == count - 1:
                for cp in leaving(j):
                    cp.wait()

        for j in range(count):
            pl.when((job == j) & (i == 0))(functools.partial(products, j, True))
            pl.when((job == j) & (i > 0))(functools.partial(products, j, False))
            pl.when((job == j) & (i == n_steps - 1))(functools.partial(finish, j))

    def turn(j):
        return lambda job, i: (jnp.where(job == j, i, jnp.where(job < j, 0, n_steps - 1)), 0)

    body, lead = _after(after, body)
    res = pl.pallas_call(
        body, name=name, grid=(count, n_steps),
        in_specs=lead + [_smem_spec()] + [pl.BlockSpec((tk, rows), turn(j)) for j, (rows, _) in enumerate(shapes)]
        + [pl.BlockSpec((tk, n), turn(j)) for j, (_, n) in enumerate(shapes)],
        out_specs=[ANY_SPEC] * (2 * count),
        out_shape=[SDS((rows, n), BF16) for rows, n in shapes] + [SDS((rows // N_DEV, n), F32) for rows, n in shapes],
        scratch_shapes=[pltpu.VMEM((most_rows, most_cols), F32), pltpu.VMEM((most_rows, most_cols), BF16),
                        pltpu.VMEM((most_rows // N_DEV, most_cols), F32), pltpu.SemaphoreType.DMA((2,))],
        compiler_params=_params(("arbitrary", "arbitrary")),
    )(*after, me, *[lhs for lhs, _ in jobs], *[rhs for _, rhs in jobs])
    return list(res[:count]), list(res[count:])


def _log_decay(decay_f, decay_b):
    def body(f_ref, b_ref, lf_ref, lb_ref):
        lf_ref[...] = jnp.log1p(-jnp.exp2(f_ref[...]))
        lb_ref[...] = jnp.log1p(-jnp.exp2(b_ref[...]))

    return pl.pallas_call(body, name="log_decay", out_shape=[SDS(decay_f.shape, F32)] * 2)(decay_f, decay_b)


def _chunk(ref, n):
    return ref[pl.ds(pl.multiple_of(n * CHUNK, CHUNK), CHUNK), :]


def _group_sum(is_a, v):
    sa = jnp.sum(jnp.where(is_a, v, 0.0), axis=1, keepdims=True)
    sb = jnp.sum(jnp.where(is_a, 0.0, v), axis=1, keepdims=True)
    return jnp.where(is_a, sa, sb)


def _seq_spec(s, col_block):
    return pl.BlockSpec((s, LANES), lambda b, h: (b, col_block + h))


def _smem_spec():
    return pl.BlockSpec(memory_space=pltpu.SMEM)


RET_UNROLL = 4
BWD_MAIN_UNROLL = 8
FWD_PREP_UNROLL = 16
BWD_PREP_UNROLL = 8


def _chunk_loop(n_chunks, body, init, unroll):
    u = unroll if n_chunks % unroll == 0 else 1

    def trip(i, carry):
        for j in range(u):
            carry = body(i * u + j, carry)
        return carry

    return lax.fori_loop(0, n_chunks // u, trip, init)


def _stacked_tables(lgf_ref, lgb_ref, pair):
    lane = lax.broadcasted_iota(jnp.int32, (1, LANES), 1)
    is_a = lane < HEAD_DIM
    lgf = jnp.where(is_a, lgf_ref[2 * pair], lgf_ref[2 * pair + 1])
    lgb = jnp.where(is_a, lgb_ref[2 * pair], lgb_ref[2 * pair + 1])
    row = lax.broadcasted_iota(jnp.int32, (CHUNK, 1), 0).astype(F32)
    kdec_f, qdec_f = jnp.exp(lgf * (CHUNK - 1.0 - row)), jnp.exp(lgf * (row + 1.0))
    kdec_b, qdec_b = jnp.exp(lgb * row), jnp.exp(lgb * (CHUNK - row))
    tab = dict(
        is_a=is_a, row=row, lam_f=jnp.exp(lgf * CHUNK), lam_b=jnp.exp(lgb * CHUNK),
        kdec=jnp.concatenate([kdec_f, kdec_b], axis=1), qdec=jnp.concatenate([qdec_f, qdec_b], axis=1),
        qexp=jnp.concatenate([jnp.broadcast_to(row + 1.0, (CHUNK, LANES)),
                              jnp.broadcast_to(CHUNK - row, (CHUNK, LANES))], axis=1),
        kexp=jnp.concatenate([jnp.broadcast_to(CHUNK - 1.0 - row, (CHUNK, LANES)),
                              jnp.broadcast_to(row, (CHUNK, LANES))], axis=1),
    )
    r = lax.broadcasted_iota(jnp.int32, (2 * LANES, LANES), 0)
    c = lax.broadcasted_iota(jnp.int32, (2 * LANES, LANES), 1)
    tab["diag2"] = ((r & (LANES - 1)) < HEAD_DIM) == (c < HEAD_DIM)
    i2 = lax.broadcasted_iota(jnp.int32, (2 * CHUNK, CHUNK), 0)
    j = lax.broadcasted_iota(jnp.int32, (2 * CHUNK, CHUNK), 1)
    head_b = i2 >= CHUNK
    diff = ((i2 & (CHUNK - 1)) - j).astype(F32)
    up, dn = jnp.maximum(diff, 0.0), jnp.maximum(-diff, 0.0)
    lgf2 = jnp.where(head_b, lgf_ref[2 * pair + 1], lgf_ref[2 * pair])
    lgb2 = jnp.where(head_b, lgb_ref[2 * pair + 1], lgb_ref[2 * pair])
    ef = jnp.where(diff >= 0, jnp.exp(lgf2 * up), 0.0)
    eb = jnp.where(diff <= 0, jnp.exp(lgb2 * dn), 0.0)
    tab["d2"] = ef + eb
    tab["df2"] = ef * up
    tab["db2"] = eb * dn
    return tab


def _stack_pair(is_a, x):
    zero = jnp.zeros_like(x)
    return jnp.concatenate([jnp.where(is_a, x, zero), jnp.where(is_a, zero, x)], axis=0)


def _unstack_pair(is_a, x2):
    return jnp.where(is_a, x2[0:CHUNK, :], x2[CHUNK:2 * CHUNK, :])


def _both_ways(x, dec):
    return (jnp.concatenate([x, x], axis=1) * dec).astype(BF16)


def _scan_states(n_chunks, st, up_rows, up_lam, down_rows, down_lam):
    zero = jnp.zeros((LANES, LANES), F32)

    def up(n, r):
        new = st[n, up_rows, :]
        st[n, up_rows, :] = r
        return r * up_lam + new

    def down(s, r):
        n = n_chunks - 1 - s
        new = st[n, down_rows, :]
        st[n, down_rows, :] = r
        return r * down_lam + new

    lax.fori_loop(0, n_chunks, up, zero)
    lax.fori_loop(0, n_chunks, down, zero)


FWD_ROWS, BWD_ROWS = pl.ds(0, LANES), pl.ds(LANES, LANES)


def _state_spec(n_chunks, pairs):
    return pl.BlockSpec((n_chunks, 2 * LANES, LANES), lambda b, h: (b * pairs + h, 0, 0))


ST_GAIN, ST_XF, ST_XB, ST_IFA, ST_IFB, ST_IBA, ST_IBB, ST_LF, ST_LB = 0, 1, 2, 3, 4, 5, 6, 8, 9
ST_ROWS = 16


GW = GROUP * HEAD_DIM
KEYS = 3 * BLOCK


def _attn_tables(g, bias_ref):
    r = lax.broadcasted_iota(jnp.int32, (GROUP * BLOCK, KEYS), 0)
    kj = lax.broadcasted_iota(jnp.int32, (GROUP * BLOCK, KEYS), 1)
    qi = r & (BLOCK - 1)
    hh = lax.shift_right_logical(r, 7)
    dist = jnp.abs(kj - BLOCK - qi)
    slope = jnp.exp2(-(GROUP * g + hh + 1).astype(F32) * (8.0 / ATTN_HEADS))
    inside = jnp.where(dist <= BLOCK, -slope * dist.astype(F32), NEG_INF)
    bias_ref[BIAS_INSIDE] = inside
    bias_ref[BIAS_FIRST] = jnp.where(kj >= BLOCK, inside, NEG_INF)
    bias_ref[BIAS_LAST] = jnp.where(kj < 2 * BLOCK, inside, NEG_INF)


BIAS_INSIDE, BIAS_FIRST, BIAS_LAST = 0, 1, 2


def _own_lanes(g):
    return lax.shift_right_logical(lax.broadcasted_iota(jnp.int32, (1, LANES), 1), 6) == g


def _mask_keys(x_ref, g, scale, pad_ref, s):
    pad_ref[0:BLOCK, :] = jnp.zeros((BLOCK, LANES), BF16)
    pad_ref[BLOCK + s:2 * BLOCK + s, :] = jnp.zeros((BLOCK, LANES), BF16)
    pad_ref[BLOCK:BLOCK + s, :] = jnp.where(_own_lanes(g), x_ref[...].astype(F32) * scale, 0.0).astype(BF16)


def _lane_block(x, j):
    return x[:, j * LANES:(j + 1) * LANES]


def _stack_heads(x, g):
    assert GROUP == 4 and GW == 2 * LANES
    x1 = pltpu.roll(x, HEAD_DIM, 1)
    keep = _own_lanes(g)
    zero = jnp.zeros((BLOCK, LANES), x.dtype)
    rows = []
    for h in range(GROUP):
        for_g0 = _lane_block(x, h // 2) if h % 2 == 0 else _lane_block(x1, ((h + 1) // 2) % 2)
        for_g1 = _lane_block(x, h // 2) if h % 2 == 1 else _lane_block(x1, h // 2)
        rows.append(jnp.where(keep, jnp.where(g == 0, for_g0, for_g1), zero))
    return jnp.concatenate(rows, axis=0)


def _unstack_heads(x4, g):
    p = [x4[h * BLOCK:(h + 1) * BLOCK, :] for h in range(GROUP)]
    cat = lambda a, b: jnp.concatenate([a, b], axis=1)
    in_place = jnp.where(g == 0, cat(p[0], p[2]), cat(p[1], p[3]))
    one_left = jnp.where(g == 0, cat(p[1], p[3]), cat(p[2], p[0]))
    return in_place + pltpu.roll(one_left, HEAD_DIM, 1)


def _sink_column(sink_ref, g):
    rh = lax.shift_right_logical(lax.broadcasted_iota(jnp.int32, (GROUP * BLOCK, 1), 0), 7)
    col = jnp.zeros((GROUP * BLOCK, 1), F32)
    for h in range(GROUP):
        col = jnp.where(rh == h, sink_ref[GROUP * g + h], col)
    return col


def _attn_probs(qm, k3, bias_ref, sink_col, n, s):
    which = jnp.where(n == 0, BIAS_FIRST, jnp.where(n == s // BLOCK - 1, BIAS_LAST, BIAS_INSIDE))
    logits = _dot_nt(qm, k3) + bias_ref[which]
    m = jnp.maximum(jnp.max(logits, axis=1, keepdims=True), sink_col)
    e = jnp.exp(logits - m)
    e_sink = jnp.exp(sink_col - m)
    inv = 1.0 / (jnp.sum(e, axis=1, keepdims=True) + e_sink)
    return e * inv, e_sink * inv


PAIRS_PER_KV = (RET_HEADS // 2) // KV_HEADS
FWD_ORDER = "rrarra"
BWD_ORDER = "rrarar"


def _trip_order(order, chunks, blocks):
    if order.count("r") == chunks and order.count("a") == blocks:
        return order
    return "r" * chunks + "a" * blocks


def _mixers_fwd(u, lgf, lgb, gn_gain, sink, b_loc, after=()):
    t = u.shape[0]
    s = t // b_loc
    n_chunks = s // CHUNK
    pairs = RET_HEADS // 2
    trips = n_chunks // RET_UNROLL
    blocks_half = (s // BLOCK) // PAIRS_PER_KV
    per_trip = blocks_half // trips
    assert n_chunks % RET_UNROLL == 0 and blocks_half % trips == 0 and PAIRS_PER_KV == 2 and s >= 2 * BLOCK

    def body(lgf_ref, lgb_ref, sink_ref, q_ref, k_ref, v_ref, g_ref, gain_ref, aq_ref, ak_ref, av_ref,
             r_ref, xhat_ref, rstd_ref, a_ref, st, kpad, vpad, bias):
        pair = pl.program_id(1)
        g, half = lax.shift_right_logical(pair, 1), pair & 1
        tab = _stacked_tables(lgf_ref, lgb_ref, pair)
        is_a = tab["is_a"]

        @pl.when(half == 0)
        def _():
            _attn_tables(g, bias)
            _mask_keys(ak_ref, g, Q_SCALE, kpad, s)
            _mask_keys(av_ref, g, 1.0, vpad, s)

        def kv_body(n, _):
            k8 = _chunk(k_ref, n).astype(F32) * Q_SCALE
            st[n] = jnp.where(tab["diag2"], _dot_tn(_both_ways(k8, tab["kdec"]), _chunk(v_ref, n)), 0.0)
            return 0

        _chunk_loop(n_chunks, kv_body, 0, FWD_PREP_UNROLL)
        _scan_states(n_chunks, st, FWD_ROWS, tab["lam_f"], BWD_ROWS, tab["lam_b"])
        sink_col = _sink_column(sink_ref, g)

        def retention_chunk(n):
            q = _chunk(q_ref, n)
            k8 = (_chunk(k_ref, n).astype(F32) * Q_SCALE).astype(BF16)
            v = _chunk(v_ref, n)
            p2 = (_dot_nt(_stack_pair(is_a, q), k8) * tab["d2"]).astype(BF16)
            y = _unstack_pair(is_a, _dot_nn(p2, v))
            y = y + _dot_nn(_both_ways(q.astype(F32), tab["qdec"]), st[n].astype(BF16))
            rows = pl.ds(pl.multiple_of(n * CHUNK, CHUNK), CHUNK)
            mu = _group_sum(is_a, y) * (1.0 / HEAD_DIM)
            dlt = y - mu
            var = _group_sum(is_a, dlt * dlt) * (1.0 / HEAD_DIM)
            rstd = lax.rsqrt(var + GN_EPS)
            xhat = dlt * rstd
            xhat_ref[rows, :] = xhat
            rstd_ref[rows, :] = rstd
            gate = _chunk(g_ref, n).astype(F32)
            r_ref[rows, :] = (xhat * gain_ref[...] * gate * _sigmoid(gate)).astype(BF16)

        def attention_block(blk):
            n = half * blocks_half + blk
            rows = pl.ds(pl.multiple_of(blk * BLOCK, BLOCK), BLOCK)
            keys = pl.ds(pl.multiple_of(n * BLOCK, BLOCK), KEYS)
            p, _ = _attn_probs(_stack_heads(aq_ref[rows, :], g), kpad[keys, :], bias, sink_col, n, s)
            a_ref[rows, :] = _unstack_heads(_dot_nn(p.astype(BF16), vpad[keys, :]), g).astype(BF16)

        def trip(i, _):
            chunk, blk = 0, 0
            for kind in _trip_order(FWD_ORDER, RET_UNROLL, per_trip):
                if kind == "r":
                    retention_chunk(i * RET_UNROLL + chunk)
                    chunk += 1
                else:
                    attention_block(i * per_trip + blk)
                    blk += 1
            return 0

        lax.fori_loop(0, trips, trip, 0)

    lane_blk = lambda c0: _seq_spec(s, c0 // LANES)
    half_rows = blocks_half * BLOCK
    aq_spec = pl.BlockSpec((half_rows, GW), lambda b, h: (b * PAIRS_PER_KV + (h & 1), C_AQ // GW + h // 2))
    a_spec = pl.BlockSpec((half_rows, GW), lambda b, h: (b * PAIRS_PER_KV + (h & 1), h // 2))
    kv_spec = lambda c0: pl.BlockSpec((s, LANES), lambda b, h: (b, c0 // LANES))
    pad = pltpu.VMEM((s + 2 * BLOCK, LANES), BF16)
    body, lead = _after(after, body)
    return pl.pallas_call(
        body, name="mixers_fwd", grid=(b_loc, pairs),
        in_specs=lead + [_smem_spec(), _smem_spec(), _smem_spec(), lane_blk(C_RQ), lane_blk(C_RK), lane_blk(C_RV),
                         lane_blk(C_RG), pl.BlockSpec((1, LANES), lambda b, h: (0, h)), aq_spec, kv_spec(C_AK),
                         kv_spec(C_AV)],
        out_specs=[_seq_spec(s, 0), _seq_spec(s, 0), _seq_spec(s, 0), a_spec, _state_spec(n_chunks, pairs)],
        out_shape=[SDS((t, RET_W), BF16), SDS((t, RET_W), F32), SDS((t, RET_W), F32), SDS((t, ATT_W), BF16),
                   SDS((b_loc * pairs * n_chunks, 2 * LANES, LANES), F32)],
        scratch_shapes=[pad, pad, pltpu.VMEM((3, GROUP * BLOCK, KEYS), F32)],
        compiler_params=_params(("arbitrary", "arbitrary")),
    )(*after, lgf, lgb, sink, u, u, u, u, gn_gain, u, u, u)


def _mixers_bwd(u, xhat, rstd, states, dr, da, lgf, lgb, gn_gain, sink, b_loc, after=()):
    t = u.shape[0]
    s = t // b_loc
    n_chunks = s // CHUNK
    pairs = RET_HEADS // 2
    unroll = BWD_MAIN_UNROLL if n_chunks % BWD_MAIN_UNROLL == 0 else RET_UNROLL
    trips = n_chunks // unroll
    blocks_half = (s // BLOCK) // PAIRS_PER_KV
    per_trip = blocks_half // trips
    assert n_chunks % unroll == 0 and blocks_half % trips == 0 and PAIRS_PER_KV == 2 and s >= 2 * BLOCK

    def body(lgf_ref, lgb_ref, sink_ref, q_ref, k_ref, v_ref, g_ref, xhat_ref, rstd_ref, dr_ref, gain_ref,
             aq_ref, ak_ref, av_ref, do_ref, st,
             dq_ref, dk_ref, dv_ref, dg_ref, st_ref, daq_ref, dak_ref, dav_ref, dsink_ref,
             gr, dy_s, kpad, vpad, bias, dk_acc, dv_acc):
        pair = pl.program_id(1)
        g, half = lax.shift_right_logical(pair, 1), pair & 1
        tab = _stacked_tables(lgf_ref, lgb_ref, pair)
        is_a = tab["is_a"]
        gain = gain_ref[...]

        @pl.when(half == 0)
        def _():
            _attn_tables(g, bias)
            _mask_keys(ak_ref, g, Q_SCALE, kpad, s)
            _mask_keys(av_ref, g, 1.0, vpad, s)
            dsink_ref[...] = jnp.zeros_like(dsink_ref)

        @pl.when(pair == 0)
        def _():
            dk_acc[...] = jnp.zeros_like(dk_acc)
            dv_acc[...] = jnp.zeros_like(dv_acc)

        def norm_body(n, dgain):
            rows = pl.ds(pl.multiple_of(n * CHUNK, CHUNK), CHUNK)
            xhat, rstd = xhat_ref[rows, :], rstd_ref[rows, :]
            gate = g_ref[rows, :].astype(F32)
            sg = _sigmoid(gate)
            silu = gate * sg
            d_out = dr_ref[rows, :].astype(F32)
            dg_ref[rows, :] = (d_out * xhat * gain * (sg * (1.0 + gate * (1.0 - sg)))).astype(BF16)
            dxh = d_out * gain * silu
            m1 = _group_sum(is_a, dxh) * (1.0 / HEAD_DIM)
            m2 = _group_sum(is_a, dxh * xhat) * (1.0 / HEAD_DIM)
            dy = (rstd * (dxh - m1 - xhat * m2)).astype(BF16)
            dy_s[rows, :] = dy
            qf = q_ref[rows, :].astype(F32)
            gr[n] = jnp.where(tab["diag2"], _dot_tn(_both_ways(qf, tab["qdec"]), dy), 0.0)
            return dgain + jnp.sum(d_out * xhat * silu, axis=0, keepdims=True)

        colsum = lambda x: jnp.sum(x, axis=0, keepdims=True)

        def grad_body(n, carry):
            xfb, ifa, ifb, iba, ibb, lf, lb = carry
            rows = pl.ds(pl.multiple_of(n * CHUNK, CHUNK), CHUNK)
            q = q_ref[rows, :]
            qf = q.astype(F32)
            k8f = k_ref[rows, :].astype(F32) * Q_SCALE
            k8 = k8f.astype(BF16)
            v = v_ref[rows, :]
            dy = dy_s[rows, :]
            q2, dy2 = _stack_pair(is_a, q), _stack_pair(is_a, dy)
            sc = _dot_nt(q2, k8)
            dp = _dot_nt(dy2, v)
            a2 = (sc * tab["d2"]).astype(BF16)
            ds2 = (dp * tab["d2"]).astype(BF16)
            dq = _unstack_pair(is_a, _dot_nn(ds2, k8))
            dk = _dot_tn(ds2, q2)
            dv = _dot_tn(a2, dy2)
            prod = sc * dp
            pf, pb = prod * tab["df2"], prod * tab["db2"]
            ifa, ifb = ifa + colsum(pf[0:CHUNK, :]), ifb + colsum(pf[CHUNK:2 * CHUNK, :])
            iba, ibb = iba + colsum(pb[0:CHUNK, :]), ibb + colsum(pb[CHUNK:2 * CHUNK, :])
            states, sgrads = st[n], gr[n]
            sb, gb = states.astype(BF16), sgrads.astype(BF16)
            dqc = _dot_nt(dy, sb) * tab["qdec"]
            dkc = _dot_nt(v, gb) * tab["kdec"]
            dv = dv + _dot_nn(_both_ways(k8f, tab["kdec"]), gb)
            dq_ref[rows, :] = (dq + dqc[:, 0:LANES] + dqc[:, LANES:2 * LANES]).astype(BF16)
            dk_ref[rows, :] = ((dk + dkc[:, 0:LANES] + dkc[:, LANES:2 * LANES]) * Q_SCALE).astype(BF16)
            dv_ref[rows, :] = dv.astype(BF16)
            q2w, k2w = jnp.concatenate([qf, qf], axis=1), jnp.concatenate([k8f, k8f], axis=1)
            xfb = xfb + colsum(tab["qexp"] * q2w * dqc + tab["kexp"] * k2w * dkc)
            prod_s = sgrads * states
            lf, lb = lf + colsum(prod_s[0:LANES, :]), lb + colsum(prod_s[LANES:2 * LANES, :])
            return xfb, ifa, ifb, iba, ibb, lf, lb

        sink_col = _sink_column(sink_ref, g)
        head_row = lax.broadcasted_iota(jnp.int32, dsink_ref.shape, 0)

        def attention_block(blk):
            n = half * blocks_half + blk
            rows = pl.ds(pl.multiple_of(blk * BLOCK, BLOCK), BLOCK)
            keys = pl.ds(pl.multiple_of(n * BLOCK, BLOCK), KEYS)
            qm = _stack_heads(aq_ref[rows, :], g)
            k3, v3 = kpad[keys, :], vpad[keys, :]
            p, p_sink = _attn_probs(qm, k3, bias, sink_col, n, s)
            dom = _stack_heads(do_ref[rows, :], g)
            dp = _dot_nt(dom, v3)
            delta = jnp.sum(p * dp, axis=1, keepdims=True)
            ds_mat = (p * (dp - delta)).astype(BF16)
            daq_ref[rows, :] = _unstack_heads(_dot_nn(ds_mat, k3), g).astype(BF16)
            dk_acc[keys, :] += _dot_tn(ds_mat, qm) * Q_SCALE
            dv_acc[keys, :] += _dot_tn(p.astype(BF16), dom)
            w = p_sink * delta
            upd = jnp.zeros(dsink_ref.shape, F32)
            for h in range(GROUP):
                upd = upd + jnp.where(head_row == h, -jnp.sum(w[h * BLOCK:(h + 1) * BLOCK, :]), 0.0)
            dsink_ref[...] += upd

        dgain = _chunk_loop(n_chunks, norm_body, jnp.zeros((1, LANES), F32), BWD_PREP_UNROLL)
        _scan_states(n_chunks, gr, BWD_ROWS, tab["lam_b"], FWD_ROWS, tab["lam_f"])

        def trip(i, carry):
            chunk, blk = 0, 0
            for kind in _trip_order(BWD_ORDER * (unroll // RET_UNROLL), unroll, per_trip):
                if kind == "r":
                    carry = grad_body(i * unroll + chunk, carry)
                    chunk += 1
                else:
                    attention_block(i * per_trip + blk)
                    blk += 1
            return carry

        z = jnp.zeros((1, LANES), F32)
        init = (jnp.zeros((1, 2 * LANES), F32), z, z, z, z, z, z)
        xfb, ifa, ifb, iba, ibb, lf, lb = lax.fori_loop(0, trips, trip, init)
        st_ref[...] = jnp.zeros_like(st_ref)
        st_ref[ST_GAIN:ST_GAIN + 1, :] = dgain
        st_ref[ST_XF:ST_XF + 1, :] = xfb[:, 0:LANES]
        st_ref[ST_XB:ST_XB + 1, :] = xfb[:, LANES:2 * LANES]
        st_ref[ST_IFA:ST_IFA + 1, :] = ifa
        st_ref[ST_IFB:ST_IFB + 1, :] = ifb
        st_ref[ST_IBA:ST_IBA + 1, :] = iba
        st_ref[ST_IBB:ST_IBB + 1, :] = ibb
        st_ref[ST_LF:ST_LF + 1, :] = lf * (CHUNK * tab["lam_f"])
        st_ref[ST_LB:ST_LB + 1, :] = lb * (CHUNK * tab["lam_b"])

        @pl.when(pair == pairs - 1)
        def _():
            dak_ref[...] = dk_acc[BLOCK:BLOCK + s, :].astype(BF16)
            dav_ref[...] = dv_acc[BLOCK:BLOCK + s, :].astype(BF16)

    lane_blk = lambda c0: _seq_spec(s, c0 // LANES)
    seq0 = _seq_spec(s, 0)
    half_rows = blocks_half * BLOCK
    aq_spec = pl.BlockSpec((half_rows, GW), lambda b, h: (b * PAIRS_PER_KV + (h & 1), C_AQ // GW + h // 2))
    a_spec = pl.BlockSpec((half_rows, GW), lambda b, h: (b * PAIRS_PER_KV + (h & 1), h // 2))
    kv_spec = lambda c0: pl.BlockSpec((s, LANES), lambda b, h: (b, c0 // LANES))
    kv_out = pl.BlockSpec((s, LANES), lambda b, h: (b, 0))
    state = pltpu.VMEM((n_chunks, 2 * LANES, LANES), F32)
    pad = pltpu.VMEM((s + 2 * BLOCK, LANES), BF16)
    acc = pltpu.VMEM((s + 2 * BLOCK, LANES), F32)
    body, lead = _after(after, body)
    return pl.pallas_call(
        body, name="mixers_bwd", grid=(b_loc, pairs),
        in_specs=lead + [_smem_spec(), _smem_spec(), _smem_spec(), lane_blk(C_RQ), lane_blk(C_RK), lane_blk(C_RV),
                         lane_blk(C_RG), seq0, seq0, seq0, pl.BlockSpec((1, LANES), lambda b, h: (0, h)),
                         aq_spec, kv_spec(C_AK), kv_spec(C_AV), a_spec, _state_spec(n_chunks, pairs)],
        out_specs=[seq0] * 4 + [pl.BlockSpec((ST_ROWS, LANES), lambda b, h: (b, h)), a_spec, kv_out, kv_out,
                                pl.BlockSpec((8, LANES), lambda b, h: (b * KV_HEADS + h // 2, 0))],
        out_shape=[SDS((t, RET_W), BF16)] * 4 + [SDS((b_loc * ST_ROWS, RET_W), F32), SDS((t, ATT_W), BF16),
                                                   SDS((t, KV_W), BF16), SDS((t, KV_W), BF16),
                                                   SDS((b_loc * KV_HEADS * 8, LANES), F32)],
        scratch_shapes=[state, pltpu.VMEM((s, LANES), BF16), pad, pad,
                        pltpu.VMEM((3, GROUP * BLOCK, KEYS), F32), acc, acc],
        compiler_params=_params(("arbitrary", "arbitrary")),
    )(*after, lgf, lgb, sink, u, u, u, u, xhat, rstd, dr, gn_gain, u, u, u, da, states)


def _pack_small(acc2, acc1, ret_stats, dsink, b_loc, d):
    pairs = RET_HEADS // 2

    def body(acc2_ref, acc1_ref, st_ref, dsink_ref, out_ref):
        out_ref[...] = jnp.zeros_like(out_ref)
        out_ref[ROW_LN1G:ROW_LN1G + 1, :] = acc1_ref[0:1, :]
        out_ref[ROW_LN1B:ROW_LN1B + 1, :] = acc1_ref[1:2, :]
        out_ref[ROW_LN2G:ROW_LN2G + 1, :] = acc2_ref[1:2, :]
        out_ref[ROW_LN2B:ROW_LN2B + 1, :] = acc2_ref[2:3, :]
        out_ref[ROW_LOSS:ROW_LOSS + 1, :] = acc2_ref[0:1, :]
        st = st_ref[0:ST_ROWS, :]
        for b in range(1, b_loc):
            st = st + st_ref[b * ST_ROWS:(b + 1) * ST_ROWS, :]
        out_ref[ROW_GN:ROW_GN + 1, 0:RET_W] = st[ST_GAIN:ST_GAIN + 1, :]
        lane = lax.broadcasted_iota(jnp.int32, (1, d), 1)
        misc = jnp.zeros((1, d), F32)
        for pr in range(pairs):
            blk = st[:, pr * LANES:(pr + 1) * LANES]
            half = lax.broadcasted_iota(jnp.int32, (1, LANES), 1) < HEAD_DIM
            for h in range(2):
                sel = half if h == 0 else jnp.logical_not(half)
                cross_f = jnp.sum(jnp.where(sel, blk[ST_XF:ST_XF + 1, :] + blk[ST_LF:ST_LF + 1, :], 0.0))
                cross_b = jnp.sum(jnp.where(sel, blk[ST_XB:ST_XB + 1, :] + blk[ST_LB:ST_LB + 1, :], 0.0))
                intra_f = jnp.sum(blk[ST_IFA + h:ST_IFA + h + 1, :])
                intra_b = jnp.sum(blk[ST_IBA + h:ST_IBA + h + 1, :])
                head = 2 * pr + h
                misc = jnp.where(lane == MISC_DF + head, cross_f + intra_f, misc)
                misc = jnp.where(lane == MISC_DB + head, cross_b + intra_b, misc)
        for g in range(KV_HEADS):
            tot = dsink_ref[g * 8:(g + 1) * 8, :]
            for b in range(1, b_loc):
                tot = tot + dsink_ref[(b * KV_HEADS + g) * 8:(b * KV_HEADS + g + 1) * 8, :]
            for h in range(GROUP):
                misc = jnp.where(lane == MISC_SINK + GROUP * g + h, jnp.sum(tot[h:h + 1, 0:1]), misc)
        out_ref[ROW_MISC:ROW_MISC + 1, :] = misc

    return pl.pallas_call(body, name="pack_small", out_shape=SDS((SMALL_ROWS, d), F32))(acc2, acc1, ret_stats, dsink)


BIG = ("w_in", "w_out", "w_ffn_gate", "w_ffn_up", "w_ffn_down", "w_ple_proj", "w_ple_gate")
TRANSPOSED_OUTSIDE = ("w_in", "w_ffn_gate", "w_ffn_up")
TRANSPOSED_HERE = ("w_ple_proj",)
SMALL = ("ret_decay_fwd", "ret_decay_bwd", "ret_gn_gain", "attn_sink", "ln1_gain", "ln1_bias", "ln2_gain", "ln2_bias")
ORDER = ("w_in", "ret_decay_fwd", "ret_decay_bwd", "ret_gn_gain", "attn_sink", "w_out", "ln1_gain", "ln1_bias",
         "w_ffn_gate", "w_ffn_up", "w_ffn_down", "w_ple_proj", "w_ple_gate", "ln2_gain", "ln2_bias")


GATHER_ORDER = ("w_in", "w_ffn_up", "w_out", "w_ffn_gate", "w_ple_gate", "w_ple_proj", "w_ffn_down")
GATHER_TWO_LEVEL = ("w_in", "w_ffn_up", "w_out")

def _local_step(x2, p2, target2, fetch, publish, small, b_loc, me):
    d = x2.shape[1]
    lgf, lgb = _log_decay(small["ret_decay_fwd"], small["ret_decay_bwd"])
    lgf1, lgb1, sink1 = lgf.reshape(-1), lgb.reshape(-1), small["attn_sink"].reshape(-1)
    (w_in,) = fetch(("w_in",), ())
    u, xb = _in_proj(x2, w_in)
    passed = fetch.pass_on(("w_ffn_up", "w_out"), (xb, lgf))
    r, ret_xhat, ret_rstd, a, ret_states = _mixers_fwd(u, lgf1, lgb1, small["ret_gn_gain"], sink1, b_loc, passed)
    w_out, w_gate, w_up = fetch(("w_out", "w_ffn_gate", "w_ffn_up"), (r, a))
    xhat1, rstd1, h1b, dact_dg, dact_du, act = _mix_ln1_ffn_up(
        r, a, x2, w_out, w_gate, w_up, small["ln1_gain"], small["ln1_bias"])
    w_pg, w_pe, w_down = fetch(("w_ple_gate", "w_ple_proj", "w_ffn_down"), (act,))
    own = {}
    dz2, dz2b, dsb, dg, dup, acc2, pe_whole, own["w_ple_proj"], pg_whole, own["w_ple_gate"] = (
        _ffn_down_ln2_loss(me, act, dact_dg, dact_du, h1b, p2, xhat1, target2, w_down, w_pg, w_pe,
                           small["ln1_gain"], small["ln1_bias"], small["ln2_gain"], small["ln2_bias"]))

    def grad(name, parts, rhs, after=()):
        whole, own[name] = _weight_grad("grad_" + name, me, parts, rhs, after)
        return whole

    ffn_jobs = dict(w_ffn_down=(act, dz2b), w_ffn_gate=(dg, h1b), w_ffn_up=(dup, h1b))
    wholes, owns = _weight_grad_jobs("grad_w_ffn", me, list(ffn_jobs.values()))
    own.update(zip(ffn_jobs, owns))
    t2 = publish("ffn", dict(zip(ffn_jobs, wholes), w_ple_proj=pe_whole, w_ple_gate=pg_whole))
    dz1, dz1b, dr, da, acc1, wo_whole, own["w_out"] = _dh1_ln1_bwd(
        me, dz2, dg, dup, dsb, xhat1, rstd1, r, a, w_gate, w_up, w_pg, w_out, small["ln1_gain"], t2)
    t3 = publish("out", dict(w_out=wo_whole))
    dq, dk, dv, dgate, ret_stats, daq, dak, dav, dsink = _mixers_bwd(
        u, ret_xhat, ret_rstd, ret_states, dr, da, lgf1, lgb1, small["ret_gn_gain"], sink1, b_loc, t3)
    parts = [dq, dk, dv, dgate, daq, dak, dav]
    small_part = _pack_small(acc2, acc1, ret_stats, dsink, b_loc, d)
    t4 = publish("in", dict(w_in=grad("w_in", parts, xb, (small_part,))), small_part)
    grad_x = _in_proj_bwd(dz1, parts, w_in, t4)
    return grad_x, own, small_part


def kernel(x, p, w_in, ret_decay_fwd, ret_decay_bwd, ret_gn_gain, attn_sink, w_out, ln1_gain, ln1_bias, w_ffn_gate, w_ffn_up, w_ffn_down, w_ple_proj, w_ple_gate, ln2_gain, ln2_bias, loss_target, m_w_in, m_ret_decay_fwd, m_ret_decay_bwd, m_ret_gn_gain, m_attn_sink, m_w_out, m_ln1_gain, m_ln1_bias, m_w_ffn_gate, m_w_ffn_up, m_w_ffn_down, m_w_ple_proj, m_w_ple_gate, m_ln2_gain, m_ln2_bias, v_w_in, v_ret_decay_fwd, v_ret_decay_bwd, v_ret_gn_gain, v_attn_sink, v_w_out, v_ln1_gain, v_ln1_bias, v_w_ffn_gate, v_w_ffn_up, v_w_ffn_down, v_w_ple_proj, v_w_ple_gate, v_ln2_gain, v_ln2_bias):
    given = dict(locals())

    def strip(n, a):
        if n not in BIG:
            return a
        return a[0].T if n in TRANSPOSED_OUTSIDE else a[0]

    def restore(n, a):
        if n not in BIG:
            return a
        return (a.T if n in TRANSPOSED_OUTSIDE else a)[None]

    w = {n: strip(n, given[n]) for n in ORDER}
    m = {n: strip(n, given["m_" + n]) for n in ORDER}
    v = {n: strip(n, given["v_" + n]) for n in ORDER}
    b_loc, s, d = x.shape
    x2 = x.reshape(b_loc * s, d)
    p2 = p[0].reshape(b_loc * s, p.shape[-1])
    target2 = loss_target.reshape(b_loc * s, d)

    small = {n: w[n] for n in SMALL}
    me = (4 * lax.axis_index("x") + 2 * lax.axis_index("y") + lax.axis_index("c")).astype(jnp.int32).reshape(1)

    gather = _gather_start(
        {n: w[n] for n in GATHER_ORDER},
        [_gather_copy_near if n in GATHER_TWO_LEVEL else _gather_copy for n in GATHER_ORDER])

    passing = {}

    def pass_on(names, after):
        relayed = _gather_relay("gather_relay_" + names[0], gather, [GATHER_ORDER.index(n) for n in names], list(after))
        passing.update({n: (relayed, j) for j, n in enumerate(names)})
        return (relayed["token"],)

    def fetch(names, after):
        out = {}
        for n in [n for n in names if n in GATHER_TWO_LEVEL]:
            if n not in passing:
                pass_on((n,), after)
            relayed, j = passing[n]
            out[n] = _split_copy_wait("gather_wait_" + n, relayed, [j], list(after))[0][0]
        direct = [n for n in names if n not in GATHER_TWO_LEVEL]
        if direct:
            got = _split_copy_wait("gather_wait_" + direct[0], gather, [GATHER_ORDER.index(n) for n in direct],
                                   list(after))
            out.update({n: item[0] for n, item in zip(direct, got)})
        return [out[n] for n in names]

    scatters = []

    def publish(tag, products, small_sums=None):
        items = [(products[n], lax.empty((N_DEV - 1, products[n].shape[0] // N_DEV, products[n].shape[1]), BF16))
                 for n in products]
        copies = [_scatter_copy] * len(items)
        if small_sums is not None:
            items.append((small_sums, lax.empty((N_DEV - 1,) + small_sums.shape, F32)))
            copies.append(_small_copy)
        started = _split_copy_start("scatter_start_" + tag, items, copies)
        scatters.append((list(products), small_sums is not None, started))
        return (started["token"],)

    fetch.pass_on = pass_on
    grad_x, own, small_part = _local_step(x2, p2, target2, fetch, publish, small, b_loc, me)

    out_g, out_d, out_m, out_v = {}, {}, {}, {}
    after = [grad_x]
    for names, with_small, started in scatters:
        landed = _split_copy_wait("scatter_wait_" + names[0], started, list(range(len(started["items"]))), after)
        if with_small:
            mine, from_peers = landed[-1]
            loss, sg, sd, sm, sv = _small_adamw(
                me, mine, from_peers, small, {n: m[n] for n in SMALL}, {n: v[n] for n in SMALL})
            for dst, src in ((out_g, sg), (out_d, sd), (out_m, sm), (out_v, sv)):
                dst.update(src)
        recv = {n: item[1] for n, item in zip(names, landed)}
        alike = {}
        for n in names:
            alike.setdefault((own[n].shape, n in TRANSPOSED_HERE), []).append(n)
        for (_, transposed), ns in alike.items():
            res = _reduce_adamw(ns[0], [own[n] for n in ns], [recv[n] for n in ns], [w[n] for n in ns],
                                [m[n] for n in ns], [v[n] for n in ns], transposed)
            for dst, vals in zip((out_g, out_d, out_m, out_v), res):
                dst.update(zip(ns, vals))
        after = [out_v[names[-1]]]

    outs = [loss[0, 0], grad_x.reshape(x.shape)]
    for group in (out_g, out_d, out_m, out_v):
        outs += [restore(n, group[n]) for n in ORDER]
    return tuple(outs)
```
